```python
import jax, jax.numpy as jnp
from jax import lax
import numpy as np

D_MODEL = 1024
BATCH = 16
SEQ = 2048
DEPTH = 1

CHUNK = 64
PLE_DIM = 256
EPS = 1e-6

A_HEADS = 8
A_HEAD_DIM = 64
A_WIDTH = A_HEADS * A_HEAD_DIM
A_LOOKBACK = 8
A_BAND = A_LOOKBACK + 1
REL_CLIP = 128
N_REL = 2 * REL_CLIP + 1

B_HEADS = 4
B_KEY_DIM = 128
B_VAL_DIM = 128
B_QK_WIDTH = B_HEADS * B_KEY_DIM
B_V_WIDTH = B_HEADS * B_VAL_DIM
CONV_WIDTH = 4
B_CONV_CH = 2 * B_QK_WIDTH + B_V_WIDTH

D_FF = ((8 * D_MODEL // 3 + 255) // 256) * 256

N_BRANCH = 2
SPLIT_A = 3 * A_WIDTH
SPLIT_CONV = SPLIT_A + B_CONV_CH
SPLIT_Z = SPLIT_CONV + B_V_WIDTH
SPLIT_BETA = SPLIT_Z + B_HEADS
SPLIT_DECAY = SPLIT_BETA + B_HEADS
D_IN = SPLIT_DECAY + N_BRANCH * D_MODEL

kernel_name = "hybrid_chunked_attn_gated_deltanet_block"


def rmsnorm(x, g):
    xf = x.astype(jnp.float32)
    y = xf * lax.rsqrt(jnp.mean(xf * xf, axis=-1, keepdims=True) + EPS)
    return (y * g.astype(jnp.float32)).astype(x.dtype)


def l2norm(x):
    return x * lax.rsqrt(jnp.sum(x * x, axis=-1, keepdims=True) + EPS)


def causal_depthwise_conv(x, w):
    s = x.shape[1]
    k = w.shape[0]
    xp = jnp.pad(x, ((0, 0), (k - 1, 0), (0, 0)))
    out = xp[:, 0:s] * w[0]
    for i in range(1, k):
        out = out + xp[:, i:i + s] * w[i]
    return out


def chunked_band_attention(q, k, v, rel_bias):
    b, s, h, d = q.shape
    nc = s // CHUNK
    qc = q.reshape(b, nc, CHUNK, h, d) * (d ** -0.5)
    pad = ((0, 0), (A_LOOKBACK, 0), (0, 0), (0, 0), (0, 0))
    kc = jnp.pad(k.reshape(b, nc, CHUNK, h, d), pad)
    vc = jnp.pad(v.reshape(b, nc, CHUNK, h, d), pad)
    scores = jnp.stack(
        [jnp.einsum('bnqhd,bnkhd->bhnqk', qc, kc[:, j:j + nc]) for j in range(A_BAND)],
        axis=4)
    scores = scores.reshape(b, h, nc, CHUNK, A_BAND * CHUNK).astype(jnp.float32)
    qi = jnp.arange(CHUNK)
    kj = jnp.arange(A_BAND * CHUNK)
    rel = (A_LOOKBACK * CHUNK + qi[:, None]) - kj[None, :]
    idx = jnp.clip(rel, -REL_CLIP, REL_CLIP) + REL_CLIP
    bias = rel_bias[:, idx].astype(jnp.float32)
    valid = (jnp.arange(nc)[:, None] + jnp.arange(A_BAND)[None, :]) >= A_LOOKBACK
    valid = jnp.repeat(valid, CHUNK, axis=1)
    scores = jnp.where(valid[None, None, :, None, :], scores + bias[None, :, None], -1e30)
    probs = jax.nn.softmax(scores, axis=-1).astype(v.dtype)
    probs = probs.reshape(b, h, nc, CHUNK, A_BAND, CHUNK)
    out = jnp.einsum('bhnqk,bnkhd->bnqhd', probs[:, :, :, :, 0], vc[:, 0:nc])
    for j in range(1, A_BAND):
        out = out + jnp.einsum('bhnqk,bnkhd->bnqhd', probs[:, :, :, :, j], vc[:, j:j + nc])
    return out.reshape(b, s, h * d)


def gated_delta_rule(q, k, v, g, beta):
    out_dtype = v.dtype
    f32 = jnp.float32
    b, s, h, dk = q.shape
    dv = v.shape[-1]
    nc = s // CHUNK
    L = CHUNK

    def to_chunks(t):
        return jnp.moveaxis(t.reshape((b, nc, L, h) + t.shape[3:]), 3, 1)

    q = to_chunks(l2norm(q.astype(f32)) * (dk ** -0.5))
    k = to_chunks(l2norm(k.astype(f32)))
    v = to_chunks(v.astype(f32))
    g = to_chunks(g.astype(f32))
    beta = to_chunks(beta.astype(f32))

    gc = jnp.cumsum(g, axis=-1)
    ti = jnp.arange(L)
    tri_incl = ti[:, None] >= ti[None, :]
    tri_strict = ti[:, None] > ti[None, :]
    diff = gc[..., :, None] - gc[..., None, :]
    decay = jnp.exp(jnp.where(tri_incl, diff, -jnp.inf))
    kk = jnp.einsum('bhnid,bhnjd->bhnij', k, k)
    a_mat = jnp.where(tri_strict, beta[..., :, None] * kk * decay, 0.0)
    eye = jnp.eye(L, dtype=f32)
    rhs = jnp.concatenate([beta[..., None] * v,
                           (beta * jnp.exp(gc))[..., None] * k], axis=-1)
    sol = lax.linalg.triangular_solve(eye + a_mat, rhs, left_side=True,
                                      lower=True, unit_diagonal=True)
    u, wk = sol[..., :dv], sol[..., dv:]
    pqk = jnp.einsum('bhnid,bhnjd->bhnij', q, k) * decay
    gamma = jnp.exp(gc)
    g_last = gc[..., -1]
    kdec = k * jnp.exp(g_last[..., None] - gc)[..., None]

    def front(t):
        return jnp.moveaxis(t, 2, 0)

    xs = (front(u), front(wk), front(q), front(pqk), front(gamma), front(kdec), front(g_last))

    def step(state, inp):
        u_c, wk_c, q_c, p_c, gam_c, kd_c, gl_c = inp
        w = u_c - jnp.einsum('bhld,bhvd->bhlv', wk_c, state)
        o = gam_c[..., None] * jnp.einsum('bhld,bhvd->bhlv', q_c, state) \
            + jnp.einsum('bhts,bhsv->bhtv', p_c, w)
        state = jnp.exp(gl_c)[..., None, None] * state + jnp.einsum('bhlv,bhld->bhvd', w, kd_c)
        return state, o

    s0 = jnp.zeros((b, h, dv, dk), f32)
    _, o = lax.scan(step, s0, xs)
    o = jnp.transpose(o, (1, 0, 3, 2, 4)).reshape(b, s, h, dv)
    return o.astype(out_dtype)


def _fwd_setup_inputs(seed: int = 0) -> dict:
    key = jax.random.key(seed)
    ks = jax.random.split(key, 24)
    f32 = jnp.float32

    def nrm(k, shape, scale):
        return jax.random.normal(k, shape, f32) * scale

    def gain(k, shape):
        return 1.0 + 0.1 * jax.random.normal(k, shape, f32)

    x = jax.random.normal(ks[0], (BATCH, SEQ, D_MODEL), f32)
    p = jax.random.normal(ks[1], (DEPTH, BATCH, SEQ, PLE_DIM), f32)
    g_mix = gain(ks[2], (DEPTH, D_MODEL))
    w_in = nrm(ks[3], (DEPTH, D_MODEL, D_IN), D_MODEL ** -0.5)
    conv_w = nrm(ks[4], (DEPTH, CONV_WIDTH, B_CONV_CH), CONV_WIDTH ** -0.5)
    a_log = jnp.log(jax.random.uniform(ks[5], (DEPTH, B_HEADS), f32, 1.0, 16.0))
    dt = jnp.exp(jax.random.uniform(ks[6], (DEPTH, B_HEADS), f32,
                                    np.log(1e-3), np.log(1e-1)))
    dt_bias = dt + jnp.log(-jnp.expm1(-dt))
    rel_bias = nrm(ks[7], (DEPTH, A_HEADS, N_REL), 0.1)
    w_onorm = gain(ks[8], (DEPTH, B_VAL_DIM))
    w_branch_a = nrm(ks[9], (DEPTH, A_WIDTH, D_MODEL), A_WIDTH ** -0.5)
    w_branch_b = nrm(ks[10], (DEPTH, B_V_WIDTH, D_MODEL), B_V_WIDTH ** -0.5)
    w_out = nrm(ks[11], (DEPTH, D_MODEL, D_MODEL), D_MODEL ** -0.5)
    g_ffn = gain(ks[12], (DEPTH, D_MODEL))
    w_gate_up = nrm(ks[13], (DEPTH, D_MODEL, 2 * D_FF), D_MODEL ** -0.5)
    w_down = nrm(ks[14], (DEPTH, D_FF, D_MODEL), D_FF ** -0.5)
    g_ple = gain(ks[15], (DEPTH, D_MODEL))
    w_ple_gate = nrm(ks[16], (DEPTH, D_MODEL, D_MODEL), D_MODEL ** -0.5)
    w_ple_proj = nrm(ks[17], (DEPTH, PLE_DIM, D_MODEL), PLE_DIM ** -0.5)
    g_final = gain(ks[18], (D_MODEL,))
    return {"x": x, "p": p, "g_mix": g_mix, "w_in": w_in, "conv_w": conv_w,
            "a_log": a_log, "dt_bias": dt_bias, "rel_bias": rel_bias, "w_onorm": w_onorm,
            "w_branch_a": w_branch_a, "w_branch_b": w_branch_b, "w_out": w_out,
            "g_ffn": g_ffn, "w_gate_up": w_gate_up, "w_down": w_down,
            "g_ple": g_ple, "w_ple_gate": w_ple_gate, "w_ple_proj": w_ple_proj,
            "g_final": g_final}


def _fwd_reference(x, p, g_mix, w_in, conv_w, a_log, dt_bias, rel_bias, w_onorm,
              w_branch_a, w_branch_b, w_out, g_ffn, w_gate_up, w_down,
              g_ple, w_ple_gate, w_ple_proj, g_final):
    b, s, _ = x.shape
    for i in range(DEPTH):
        h = rmsnorm(x, g_mix[i])
        proj = h @ w_in[i]
        qkv_a = proj[..., :SPLIT_A]
        conv_in = proj[..., SPLIT_A:SPLIT_CONV]
        z = proj[..., SPLIT_CONV:SPLIT_Z]
        beta_raw = proj[..., SPLIT_Z:SPLIT_BETA]
        decay_raw = proj[..., SPLIT_BETA:SPLIT_DECAY]
        gates = proj[..., SPLIT_DECAY:]

        q_a = qkv_a[..., :A_WIDTH].reshape(b, s, A_HEADS, A_HEAD_DIM)
        k_a = qkv_a[..., A_WIDTH:2 * A_WIDTH].reshape(b, s, A_HEADS, A_HEAD_DIM)
        v_a = qkv_a[..., 2 * A_WIDTH:].reshape(b, s, A_HEADS, A_HEAD_DIM)
        y_a = chunked_band_attention(q_a, k_a, v_a, rel_bias[i])

        c = jax.nn.silu(causal_depthwise_conv(conv_in, conv_w[i]))
        q_b = c[..., :B_QK_WIDTH].reshape(b, s, B_HEADS, B_KEY_DIM)
        k_b = c[..., B_QK_WIDTH:2 * B_QK_WIDTH].reshape(b, s, B_HEADS, B_KEY_DIM)
        v_b = c[..., 2 * B_QK_WIDTH:].reshape(b, s, B_HEADS, B_VAL_DIM)
        beta = jax.nn.sigmoid(beta_raw)
        g = -jnp.exp(a_log[i]) * jax.nn.softplus(decay_raw + dt_bias[i])
        o_b = gated_delta_rule(q_b, k_b, v_b, g, beta)
        o_b = rmsnorm(o_b, w_onorm[i]) * jax.nn.silu(z.reshape(b, s, B_HEADS, B_VAL_DIM))
        y_b = o_b.reshape(b, s, B_V_WIDTH)

        gate_a = jax.nn.sigmoid(gates[..., :D_MODEL])
        gate_b = jax.nn.sigmoid(gates[..., D_MODEL:])
        merged = gate_a * (y_a @ w_branch_a[i]) + gate_b * (y_b @ w_branch_b[i])
        x = x + merged @ w_out[i]

        h = rmsnorm(x, g_ffn[i])
        gu = h @ w_gate_up[i]
        x = x + (jax.nn.silu(gu[..., :D_FF]) * gu[..., D_FF:]) @ w_down[i]

        ple_gate = jax.nn.sigmoid(rmsnorm(x, g_ple[i]) @ w_ple_gate[i])
        x = x + ple_gate * (p[i] @ w_ple_proj[i])
    return rmsnorm(x, g_final)


import jax as _jax
import jax.numpy as _jnp

TWIN_FORMAT = 'train_step'
FWD_PARAMS = ['x', 'p', 'g_mix', 'w_in', 'conv_w', 'a_log', 'dt_bias', 'rel_bias', 'w_onorm', 'w_branch_a', 'w_branch_b', 'w_out', 'g_ffn', 'w_gate_up', 'w_down', 'g_ple', 'w_ple_gate', 'w_ple_proj', 'g_final']
TWIN_WEIGHTS = ['g_mix', 'w_in', 'conv_w', 'a_log', 'dt_bias', 'rel_bias', 'w_onorm', 'w_branch_a', 'w_branch_b', 'w_out', 'g_ffn', 'w_gate_up', 'w_down', 'g_ple', 'w_ple_gate', 'w_ple_proj', 'g_final']
TWIN_DIFF_INPUT = 'x'
TWIN_INPUTS = ['x', 'p', 'g_mix', 'w_in', 'conv_w', 'a_log', 'dt_bias', 'rel_bias', 'w_onorm', 'w_branch_a', 'w_branch_b', 'w_out', 'g_ffn', 'w_gate_up', 'w_down', 'g_ple', 'w_ple_gate', 'w_ple_proj', 'g_final', 'loss_target', 'm_g_mix', 'm_w_in', 'm_conv_w', 'm_a_log', 'm_dt_bias', 'm_rel_bias', 'm_w_onorm', 'm_w_branch_a', 'm_w_branch_b', 'm_w_out', 'm_g_ffn', 'm_w_gate_up', 'm_w_down', 'm_g_ple', 'm_w_ple_gate', 'm_w_ple_proj', 'm_g_final', 'v_g_mix', 'v_w_in', 'v_conv_w', 'v_a_log', 'v_dt_bias', 'v_rel_bias', 'v_w_onorm', 'v_w_branch_a', 'v_w_branch_b', 'v_w_out', 'v_g_ffn', 'v_w_gate_up', 'v_w_down', 'v_g_ple', 'v_w_ple_gate', 'v_w_ple_proj', 'v_g_final']
TWIN_OUTPUTS = ['loss', 'grad_x', 'grad_g_mix', 'grad_w_in', 'grad_conv_w', 'grad_a_log', 'grad_dt_bias', 'grad_rel_bias', 'grad_w_onorm', 'grad_w_branch_a', 'grad_w_branch_b', 'grad_w_out', 'grad_g_ffn', 'grad_w_gate_up', 'grad_w_down', 'grad_g_ple', 'grad_w_ple_gate', 'grad_w_ple_proj', 'grad_g_final', 'delta_g_mix', 'delta_w_in', 'delta_conv_w', 'delta_a_log', 'delta_dt_bias', 'delta_rel_bias', 'delta_w_onorm', 'delta_w_branch_a', 'delta_w_branch_b', 'delta_w_out', 'delta_g_ffn', 'delta_w_gate_up', 'delta_w_down', 'delta_g_ple', 'delta_w_ple_gate', 'delta_w_ple_proj', 'delta_g_final', 'new_m_g_mix', 'new_m_w_in', 'new_m_conv_w', 'new_m_a_log', 'new_m_dt_bias', 'new_m_rel_bias', 'new_m_w_onorm', 'new_m_w_branch_a', 'new_m_w_branch_b', 'new_m_w_out', 'new_m_g_ffn', 'new_m_w_gate_up', 'new_m_w_down', 'new_m_g_ple', 'new_m_w_ple_gate', 'new_m_w_ple_proj', 'new_m_g_final', 'new_v_g_mix', 'new_v_w_in', 'new_v_conv_w', 'new_v_a_log', 'new_v_dt_bias', 'new_v_rel_bias', 'new_v_w_onorm', 'new_v_w_branch_a', 'new_v_w_branch_b', 'new_v_w_out', 'new_v_g_ffn', 'new_v_w_gate_up', 'new_v_w_down', 'new_v_g_ple', 'new_v_w_ple_gate', 'new_v_w_ple_proj', 'new_v_g_final']
TWIN_LEAF_KINDS = {'loss': 'loss', 'grad_x': 'grad_x', 'grad_g_mix': 'grad_w', 'grad_w_in': 'grad_w', 'grad_conv_w': 'grad_w', 'grad_a_log': 'grad_w', 'grad_dt_bias': 'grad_w', 'grad_rel_bias': 'grad_w', 'grad_w_onorm': 'grad_w', 'grad_w_branch_a': 'grad_w', 'grad_w_branch_b': 'grad_w', 'grad_w_out': 'grad_w', 'grad_g_ffn': 'grad_w', 'grad_w_gate_up': 'grad_w', 'grad_w_down': 'grad_w', 'grad_g_ple': 'grad_w', 'grad_w_ple_gate': 'grad_w', 'grad_w_ple_proj': 'grad_w', 'grad_g_final': 'grad_w', 'delta_g_mix': 'delta_w', 'delta_w_in': 'delta_w', 'delta_conv_w': 'delta_w', 'delta_a_log': 'delta_w', 'delta_dt_bias': 'delta_w', 'delta_rel_bias': 'delta_w', 'delta_w_onorm': 'delta_w', 'delta_w_branch_a': 'delta_w', 'delta_w_branch_b': 'delta_w', 'delta_w_out': 'delta_w', 'delta_g_ffn': 'delta_w', 'delta_w_gate_up': 'delta_w', 'delta_w_down': 'delta_w', 'delta_g_ple': 'delta_w', 'delta_w_ple_gate': 'delta_w', 'delta_w_ple_proj': 'delta_w', 'delta_g_final': 'delta_w', 'new_m_g_mix': 'new_m', 'new_m_w_in': 'new_m', 'new_m_conv_w': 'new_m', 'new_m_a_log': 'new_m', 'new_m_dt_bias': 'new_m', 'new_m_rel_bias': 'new_m', 'new_m_w_onorm': 'new_m', 'new_m_w_branch_a': 'new_m', 'new_m_w_branch_b': 'new_m', 'new_m_w_out': 'new_m', 'new_m_g_ffn': 'new_m', 'new_m_w_gate_up': 'new_m', 'new_m_w_down': 'new_m', 'new_m_g_ple': 'new_m', 'new_m_w_ple_gate': 'new_m', 'new_m_w_ple_proj': 'new_m', 'new_m_g_final': 'new_m', 'new_v_g_mix': 'new_v', 'new_v_w_in': 'new_v', 'new_v_conv_w': 'new_v', 'new_v_a_log': 'new_v', 'new_v_dt_bias': 'new_v', 'new_v_rel_bias': 'new_v', 'new_v_w_onorm': 'new_v', 'new_v_w_branch_a': 'new_v', 'new_v_w_branch_b': 'new_v', 'new_v_w_out': 'new_v', 'new_v_g_ffn': 'new_v', 'new_v_w_gate_up': 'new_v', 'new_v_w_down': 'new_v', 'new_v_g_ple': 'new_v', 'new_v_w_ple_gate': 'new_v', 'new_v_w_ple_proj': 'new_v', 'new_v_g_final': 'new_v'}


def _forward(args):
    return _fwd_reference(*[args[k] for k in FWD_PARAMS])


def _output_shape():
    out = _jax.eval_shape(lambda: _forward(_fwd_setup_inputs(0)))
    return out.shape, out.dtype

N_MICROBATCH = 1
ADAM_LR = 0.001
ADAM_B1 = 0.9
ADAM_B2 = 0.999
ADAM_EPS = 1e-08
ADAM_WD = 0.01
ADAM_STEP = 10
PER_EXAMPLE_BATCH_AXIS = {'x': 0, 'p': 1, 'loss_target': 0}
SHARED_INPUTS = []
_WEIGHT_DTYPES = {'g_mix': _jnp.float32, 'w_in': _jnp.float32, 'conv_w': _jnp.float32, 'a_log': _jnp.float32, 'dt_bias': _jnp.float32, 'rel_bias': _jnp.float32, 'w_onorm': _jnp.float32, 'w_branch_a': _jnp.float32, 'w_branch_b': _jnp.float32, 'w_out': _jnp.float32, 'g_ffn': _jnp.float32, 'w_gate_up': _jnp.float32, 'w_down': _jnp.float32, 'g_ple': _jnp.float32, 'w_ple_gate': _jnp.float32, 'w_ple_proj': _jnp.float32, 'g_final': _jnp.float32}
MOMENT_SCALE = {'g_mix': 1.050068e-01, 'w_in': 4.306850e-02, 'conv_w': 6.511430e-02, 'a_log': 6.822348e-01, 'dt_bias': 6.723481e-01, 'rel_bias': 9.660735e-03, 'w_onorm': 1.877409e-01, 'w_branch_a': 1.475774e-02, 'w_branch_b': 5.858486e-02, 'w_out': 5.830025e-02, 'g_ffn': 1.337538e-01, 'w_gate_up': 5.242156e-02, 'w_down': 8.716680e-02, 'g_ple': 3.101422e-02, 'w_ple_gate': 3.094709e-02, 'w_ple_proj': 8.133027e-02, 'g_final': 3.217795e+01}


def _to_microbatches(a, axis):
    t = _jnp.moveaxis(a, axis, 0)
    t = t.reshape((N_MICROBATCH, t.shape[0] // N_MICROBATCH) + t.shape[1:])
    return _jnp.moveaxis(t, 1, axis + 1)


def setup_inputs(seed: int = 0) -> dict:
    inp = _fwd_setup_inputs(seed)
    key = _jax.random.fold_in(_jax.random.key(seed), 7919)
    shape, _ = _output_shape()
    out = dict(inp)
    out["loss_target"] = _jax.random.normal(_jax.random.fold_in(key, 0), shape, _jnp.float32)
    for i, name in enumerate(TWIN_WEIGHTS):
        w = inp[name].astype(_jnp.float32)
        if MOMENT_SCALE is None:
            s = _jnp.sqrt(_jnp.mean(_jnp.square(w)) + 1e-30)
        else:
            s = MOMENT_SCALE[name]
        km, kv = _jax.random.split(_jax.random.fold_in(key, i + 1))
        out[name] = w
        out["m_" + name] = s * _jax.random.normal(km, w.shape, _jnp.float32)
        out["v_" + name] = (s * s) * _jax.random.uniform(kv, w.shape, _jnp.float32, 0.5, 1.5)
    if N_MICROBATCH > 1:
        for name, axis in PER_EXAMPLE_BATCH_AXIS.items():
            out[name] = _to_microbatches(out[name], axis)
    return {'x': out['x'], 'p': out['p'], 'g_mix': out['g_mix'], 'w_in': out['w_in'], 'conv_w': out['conv_w'], 'a_log': out['a_log'], 'dt_bias': out['dt_bias'], 'rel_bias': out['rel_bias'], 'w_onorm': out['w_onorm'], 'w_branch_a': out['w_branch_a'], 'w_branch_b': out['w_branch_b'], 'w_out': out['w_out'], 'g_ffn': out['g_ffn'], 'w_gate_up': out['w_gate_up'], 'w_down': out['w_down'], 'g_ple': out['g_ple'], 'w_ple_gate': out['w_ple_gate'], 'w_ple_proj': out['w_ple_proj'], 'g_final': out['g_final'], 'loss_target': out['loss_target'], 'm_g_mix': out['m_g_mix'], 'm_w_in': out['m_w_in'], 'm_conv_w': out['m_conv_w'], 'm_a_log': out['m_a_log'], 'm_dt_bias': out['m_dt_bias'], 'm_rel_bias': out['m_rel_bias'], 'm_w_onorm': out['m_w_onorm'], 'm_w_branch_a': out['m_w_branch_a'], 'm_w_branch_b': out['m_w_branch_b'], 'm_w_out': out['m_w_out'], 'm_g_ffn': out['m_g_ffn'], 'm_w_gate_up': out['m_w_gate_up'], 'm_w_down': out['m_w_down'], 'm_g_ple': out['m_g_ple'], 'm_w_ple_gate': out['m_w_ple_gate'], 'm_w_ple_proj': out['m_w_ple_proj'], 'm_g_final': out['m_g_final'], 'v_g_mix': out['v_g_mix'], 'v_w_in': out['v_w_in'], 'v_conv_w': out['v_conv_w'], 'v_a_log': out['v_a_log'], 'v_dt_bias': out['v_dt_bias'], 'v_rel_bias': out['v_rel_bias'], 'v_w_onorm': out['v_w_onorm'], 'v_w_branch_a': out['v_w_branch_a'], 'v_w_branch_b': out['v_w_branch_b'], 'v_w_out': out['v_w_out'], 'v_g_ffn': out['v_g_ffn'], 'v_w_gate_up': out['v_w_gate_up'], 'v_w_down': out['v_w_down'], 'v_g_ple': out['v_g_ple'], 'v_w_ple_gate': out['v_w_ple_gate'], 'v_w_ple_proj': out['v_w_ple_proj'], 'v_g_final': out['v_g_final']}


def _loss(weights, diff, rest, loss_target):
    with _jax.named_scope("forward"):
        args = {**rest, TWIN_DIFF_INPUT: diff, **{k: w.astype(_WEIGHT_DTYPES[k]) for k, w in weights.items()}}
        y = _forward(args)
    with _jax.named_scope("loss_head"):
        err = _jnp.square(y.astype(_jnp.float32) - loss_target)
        return 0.5 * _jnp.sum(_jnp.mean(err, axis=-1)) if err.ndim else 0.5 * err


def _adamw(w, g, m, v):
    m = ADAM_B1 * m + (1.0 - ADAM_B1) * g
    v = ADAM_B2 * v + (1.0 - ADAM_B2) * _jnp.square(g)
    m_hat = m / (1.0 - ADAM_B1 ** ADAM_STEP)
    v_hat = v / (1.0 - ADAM_B2 ** ADAM_STEP)
    delta = -ADAM_LR * (m_hat / (_jnp.sqrt(v_hat) + ADAM_EPS) + ADAM_WD * w)
    return delta, m, v


def reference(x, p, g_mix, w_in, conv_w, a_log, dt_bias, rel_bias, w_onorm, w_branch_a, w_branch_b, w_out, g_ffn, w_gate_up, w_down, g_ple, w_ple_gate, w_ple_proj, g_final, loss_target, m_g_mix, m_w_in, m_conv_w, m_a_log, m_dt_bias, m_rel_bias, m_w_onorm, m_w_branch_a, m_w_branch_b, m_w_out, m_g_ffn, m_w_gate_up, m_w_down, m_g_ple, m_w_ple_gate, m_w_ple_proj, m_g_final, v_g_mix, v_w_in, v_conv_w, v_a_log, v_dt_bias, v_rel_bias, v_w_onorm, v_w_branch_a, v_w_branch_b, v_w_out, v_g_ffn, v_w_gate_up, v_w_down, v_g_ple, v_w_ple_gate, v_w_ple_proj, v_g_final):
    given = dict(x=x, p=p, g_mix=g_mix, w_in=w_in, conv_w=conv_w, a_log=a_log, dt_bias=dt_bias, rel_bias=rel_bias, w_onorm=w_onorm, w_branch_a=w_branch_a, w_branch_b=w_branch_b, w_out=w_out, g_ffn=g_ffn, w_gate_up=w_gate_up, w_down=w_down, g_ple=g_ple, w_ple_gate=w_ple_gate, w_ple_proj=w_ple_proj, g_final=g_final, loss_target=loss_target, m_g_mix=m_g_mix, m_w_in=m_w_in, m_conv_w=m_conv_w, m_a_log=m_a_log, m_dt_bias=m_dt_bias, m_rel_bias=m_rel_bias, m_w_onorm=m_w_onorm, m_w_branch_a=m_w_branch_a, m_w_branch_b=m_w_branch_b, m_w_out=m_w_out, m_g_ffn=m_g_ffn, m_w_gate_up=m_w_gate_up, m_w_down=m_w_down, m_g_ple=m_g_ple, m_w_ple_gate=m_w_ple_gate, m_w_ple_proj=m_w_ple_proj, m_g_final=m_g_final, v_g_mix=v_g_mix, v_w_in=v_w_in, v_conv_w=v_conv_w, v_a_log=v_a_log, v_dt_bias=v_dt_bias, v_rel_bias=v_rel_bias, v_w_onorm=v_w_onorm, v_w_branch_a=v_w_branch_a, v_w_branch_b=v_w_branch_b, v_w_out=v_w_out, v_g_ffn=v_g_ffn, v_w_gate_up=v_w_gate_up, v_w_down=v_w_down, v_g_ple=v_g_ple, v_w_ple_gate=v_w_ple_gate, v_w_ple_proj=v_w_ple_proj, v_g_final=v_g_final)
    weights = {n: given[n] for n in TWIN_WEIGHTS}
    shared = {n: given[n] for n in SHARED_INPUTS}
    per_example = {n: given[n] for n in ['x', 'p']}
    grad_fn = _jax.value_and_grad(_loss, argnums=(0, 1))

    def one_microbatch(ex, loss_target):
        ex = dict(ex)
        diff = ex.pop(TWIN_DIFF_INPUT)
        return grad_fn(weights, diff, {**shared, **ex}, loss_target)

    if N_MICROBATCH == 1:
        loss, (grad_w, grad_x) = one_microbatch(per_example, given["loss_target"])
    else:
        def body(carry, xs):
            loss_sum, grad_sum = carry
            l_k, (gw_k, gx_k) = one_microbatch(xs[0], xs[1])
            with _jax.named_scope("update"):
                return (loss_sum + l_k, _jax.tree.map(_jnp.add, grad_sum, gw_k)), gx_k

        init = (_jnp.zeros((), _jnp.float32), _jax.tree.map(_jnp.zeros_like, weights))
        (loss, grad_w), grad_x = _jax.lax.scan(body, init, (per_example, given["loss_target"]))
    with _jax.named_scope("update"):
        delta_w, new_m, new_v = {}, {}, {}
        for n in TWIN_WEIGHTS:
            delta_w[n], new_m[n], new_v[n] = _adamw(weights[n], grad_w[n], given["m_" + n], given["v_" + n])
    return (loss, grad_x, *[grad_w[n] for n in TWIN_WEIGHTS], *[delta_w[n] for n in TWIN_WEIGHTS],
            *[new_m[n] for n in TWIN_WEIGHTS], *[new_v[n] for n in TWIN_WEIGHTS])
```

```python
import functools

import jax
import jax.numpy as jnp
from jax import lax
from jax.experimental import pallas as pl
from jax.experimental.pallas import tpu as pltpu

F32 = jnp.float32
BF16 = jnp.bfloat16
HI = lax.Precision.HIGHEST
MESH = pl.DeviceIdType.MESH

D_MODEL = 1024
CHUNK = 64
PLE_DIM = 256
EPS = 1e-6
A_HEADS = 8
A_HEAD_DIM = 64
A_WIDTH = 512
A_LOOKBACK = 8
BAND = (A_LOOKBACK + 1) * CHUNK
TAIL = 3 * CHUNK
REL_CLIP = 128
N_REL = 2 * REL_CLIP + 1
B_HEADS = 4
B_DIM = 128
B_WIDTH = 512
CONV_K = 4
CONV_CH = 1536
D_FF = 2816
SPLIT_Z = 3584
D_IN = 5640
ADAM_LR, ADAM_B1, ADAM_B2, ADAM_EPS, ADAM_WD, ADAM_STEP = 0.001, 0.9, 0.999, 1e-08, 0.01, 10

P_GATES, P_QA, P_KA, P_VA, P_CONV, P_Z, P_BD, P_WIDTH = 0, 2048, 2560, 3072, 3584, 5120, 5632, 5760

VMEM_LIMIT = 56 * 1024 * 1024


def _cp(sem, vmem=None, **kw):
    return pltpu.CompilerParams(dimension_semantics=sem, vmem_limit_bytes=vmem, **kw)


def _tile(n, cap):
    best = None
    for t in range(128, cap + 1, 128):
        if n % t == 0:
            best = t
    assert best is not None, (n, cap)
    return best


def _nn(a, b, prec=None):
    return lax.dot_general(a, b, (((1,), (0,)), ((), ())), preferred_element_type=F32, precision=prec)


def _nt(a, b, prec=None):
    return lax.dot_general(a, b, (((1,), (1,)), ((), ())), preferred_element_type=F32, precision=prec)


def _tn(a, b, prec=None):
    return lax.dot_general(a, b, (((0,), (0,)), ((), ())), preferred_element_type=F32, precision=prec)


def _bnn(a, b, prec=None):
    return lax.dot_general(a, b, (((2,), (1,)), ((0,), (0,))), preferred_element_type=F32, precision=prec)


def _bnt(a, b, prec=None):
    return lax.dot_general(a, b, (((2,), (2,)), ((0,), (0,))), preferred_element_type=F32, precision=prec)


def _bf(a):
    return a.astype(BF16)


def _sigmoid(x):
    return 1.0 / (1.0 + jnp.exp(-x))


def _softplus(x):
    return jnp.maximum(x, 0.0) + jnp.log(1.0 + jnp.exp(-jnp.abs(x)))


def rms_matmul(x, g, w, name, tm=512, tn_cap=1024):
    t, d = x.shape
    n = w.shape[1]
    tm = min(tm, t)
    tn = _tile(n, tn_cap)

    def body(x_ref, g_ref, w_ref, o_ref, h_ref):
        @pl.when(pl.program_id(1) == 0)
        def _():
            xv = x_ref[...]
            r = lax.rsqrt(jnp.mean(xv * xv, axis=-1, keepdims=True) + EPS)
            h_ref[...] = _bf(xv * r * g_ref[...])

        o_ref[...] = _nn(h_ref[...], w_ref[...])

    return pl.pallas_call(
        body, name=name, grid=(t // tm, n // tn),
        in_specs=[pl.BlockSpec((tm, d), lambda i, j: (i, 0)),
                  pl.BlockSpec((1, d), lambda i, j: (0, 0)),
                  pl.BlockSpec((d, tn), lambda i, j: (0, j))],
        out_specs=[pl.BlockSpec((tm, tn), lambda i, j: (i, j)),
                   pl.BlockSpec((tm, d), lambda i, j: (i, 0))],
        out_shape=[jax.ShapeDtypeStruct((t, n), F32), jax.ShapeDtypeStruct((t, d), BF16)],
        compiler_params=_cp(("parallel", "arbitrary"), VMEM_LIMIT),
    )(x, g, w)


def matmul_tn(a, b, name, tm=512, tk_cap=1024, tn_cap=512):
    m, k1 = a.shape
    n = b.shape[1]
    tm = min(tm, m)
    tk = _tile(k1, tk_cap)
    tn = _tile(n, tn_cap)

    def body(a_ref, b_ref, o_ref):
        @pl.when(pl.program_id(2) == 0)
        def _():
            o_ref[...] = jnp.zeros_like(o_ref)

        o_ref[...] += _tn(_bf(a_ref[...]), _bf(b_ref[...]))

    return pl.pallas_call(
        body, name=name, grid=(k1 // tk, n // tn, m // tm),
        in_specs=[pl.BlockSpec((tm, tk), lambda i, j, k: (k, i)),
                  pl.BlockSpec((tm, tn), lambda i, j, k: (k, j))],
        out_specs=pl.BlockSpec((tk, tn), lambda i, j, k: (i, j)),
        out_shape=jax.ShapeDtypeStruct((k1, n), F32),
        compiler_params=_cp(("parallel", "parallel", "arbitrary"), VMEM_LIMIT),
    )(a, b)


def _tail_onehot(qi):
    r = lax.broadcasted_iota(jnp.int32, (384, TAIL), 0)
    kj = lax.broadcasted_iota(jnp.int32, (384, TAIL), 1)
    return (r == jnp.minimum(REL_CLIP + qi - kj, REL_CLIP) + REL_CLIP).astype(F32)


def bias_tail(rel_pad):
    def body(rb_ref, o_ref):
        rb = rb_ref[...]
        for qi in range(CHUNK):
            o_ref[qi] = _nn(rb, _tail_onehot(qi), HI)

    return pl.pallas_call(
        body, name="bias_tail",
        out_shape=jax.ShapeDtypeStruct((CHUNK, A_HEADS, TAIL), F32),
    )(rel_pad)


def bias_grad(db_t, db_far):
    def body(t_ref, f_ref, o_ref):
        acc = jnp.zeros((A_HEADS, 384), F32)
        for qi in range(CHUNK):
            acc = acc + _nt(t_ref[qi], _tail_onehot(qi), HI)
        far = jnp.sum(jnp.sum(f_ref[...], axis=2), axis=1, keepdims=True)
        lane = lax.broadcasted_iota(jnp.int32, (A_HEADS, 384), 1)
        o_ref[...] = acc + jnp.where(lane == 2 * REL_CLIP, far, 0.0)

    return pl.pallas_call(
        body, name="bias_grad",
        out_shape=jax.ShapeDtypeStruct((A_HEADS, 384), F32),
    )(db_t, db_far)


def _fill_band_pads(k_ref, v_ref, kp, vp, s):
    z = jnp.zeros((A_LOOKBACK * CHUNK, 128), BF16)
    kp[pl.ds(0, A_LOOKBACK * CHUNK), :] = z
    vp[pl.ds(0, A_LOOKBACK * CHUNK), :] = z
    kp[pl.ds(A_LOOKBACK * CHUNK, s), :] = _bf(k_ref[...])
    vp[pl.ds(A_LOOKBACK * CHUNK, s), :] = _bf(v_ref[...])


def attn_fwd(proj, bias_band, b, s):
    t = b * s
    nc = s // CHUNK
    qb, kb_, vb_ = P_QA // 128, P_KA // 128, P_VA // 128

    def body(q_ref, k_ref, v_ref, b_ref, o_ref, lse_ref, kp, vp):
        n = pl.program_id(2)

        @pl.when(n == 0)
        def _():
            _fill_band_pads(k_ref, v_ref, kp, vp, s)

        start = pl.multiple_of(n * CHUNK, CHUNK)
        kb = kp[pl.ds(start, BAND), :]
        vb = vp[pl.ds(start, BAND), :]
        q = q_ref[...] * (A_HEAD_DIM ** -0.5)
        lane = lax.broadcasted_iota(jnp.int32, (CHUNK, 128), 1)
        col = lax.broadcasted_iota(jnp.int32, (CHUNK, BAND), 1)
        valid = col >= (A_LOOKBACK - n) * CHUNK
        outs, lses = [], []
        for hh in range(2):
            msk = (lane < 64) if hh == 0 else (lane >= 64)
            qh = _bf(jnp.where(msk, q, 0.0))
            sc = jnp.where(valid, _nt(qh, kb) + b_ref[hh], -1e30)
            mx = jnp.max(sc, axis=1, keepdims=True)
            p = jnp.exp(sc - mx)
            l = jnp.sum(p, axis=1, keepdims=True)
            outs.append(_nn(_bf(p), vb) / l)
            lses.append(mx + jnp.log(l))
        o_ref[...] = jnp.where(lane < 64, outs[0], outs[1])
        lse_ref[...] = jnp.where(lane < 64, lses[0], lses[1])

    return pl.pallas_call(
        body, name="attn_fwd", grid=(b, 4, nc),
        in_specs=[pl.BlockSpec((CHUNK, 128), lambda bb, m, n: (bb * nc + n, qb + m)),
                  pl.BlockSpec((s, 128), lambda bb, m, n: (bb, kb_ + m)),
                  pl.BlockSpec((s, 128), lambda bb, m, n: (bb, vb_ + m)),
                  pl.BlockSpec((2, CHUNK, BAND), lambda bb, m, n: (m, 0, 0))],
        out_specs=[pl.BlockSpec((CHUNK, 128), lambda bb, m, n: (bb * nc + n, m)),
                   pl.BlockSpec((CHUNK, 128), lambda bb, m, n: (bb * nc + n, m))],
        out_shape=[jax.ShapeDtypeStruct((t, A_WIDTH), F32), jax.ShapeDtypeStruct((t, A_WIDTH), F32)],
        scratch_shapes=[pltpu.VMEM((s + A_LOOKBACK * CHUNK, 128), BF16),
                        pltpu.VMEM((s + A_LOOKBACK * CHUNK, 128), BF16)],
        compiler_params=_cp(("parallel", "parallel", "arbitrary"), VMEM_LIMIT),
    )(proj, proj, proj, bias_band)


def attn_bwd(proj, bias_band, y_a, lse, dy_a, b, s):
    t = b * s
    nc = s // CHUNK
    qb, kb_, vb_ = P_QA // 128, P_KA // 128, P_VA // 128
    pad = A_LOOKBACK * CHUNK

    def body(q_ref, k_ref, v_ref, b_ref, do_ref, o_ref, lse_ref,
             dq_ref, dk_ref, dv_ref, dbt_ref, dbf_ref, kp, vp, dkp, dvp):
        bb = pl.program_id(1)
        n = pl.program_id(2)

        @pl.when(n == 0)
        def _():
            _fill_band_pads(k_ref, v_ref, kp, vp, s)
            dkp[...] = jnp.zeros_like(dkp)
            dvp[...] = jnp.zeros_like(dvp)

        @pl.when((n == 0) & (bb == 0))
        def _():
            dbt_ref[...] = jnp.zeros_like(dbt_ref)
            dbf_ref[...] = jnp.zeros_like(dbf_ref)

        start = pl.multiple_of(n * CHUNK, CHUNK)
        kb = kp[pl.ds(start, BAND), :]
        vb = vp[pl.ds(start, BAND), :]
        q = q_ref[...] * (A_HEAD_DIM ** -0.5)
        do = do_ref[...]
        o = o_ref[...]
        lsev = lse_ref[...]
        lane = lax.broadcasted_iota(jnp.int32, (CHUNK, 128), 1)
        col = lax.broadcasted_iota(jnp.int32, (CHUNK, BAND), 1)
        valid = col >= (A_LOOKBACK - n) * CHUNK
        dq = jnp.zeros((CHUNK, 128), F32)
        dkb = jnp.zeros((BAND, 128), F32)
        dvb = jnp.zeros((BAND, 128), F32)
        for hh in range(2):
            msk = (lane < 64) if hh == 0 else (lane >= 64)
            qh = _bf(jnp.where(msk, q, 0.0))
            doh = jnp.where(msk, do, 0.0)
            dohb = _bf(doh)
            lse_h = lsev[:, 0:1] if hh == 0 else lsev[:, 64:65]
            sc = jnp.where(valid, _nt(qh, kb) + b_ref[hh], -1e30)
            p = jnp.exp(sc - lse_h)
            dp = _nt(dohb, vb)
            delta = jnp.sum(doh * o, axis=1, keepdims=True)
            ds = p * (dp - delta)
            dsb = _bf(ds)
            dq = dq + jnp.where(msk, _nn(dsb, kb), 0.0)
            dkb = dkb + _tn(dsb, qh)
            dvb = dvb + _tn(_bf(p), dohb)
            dbt_ref[hh] += ds[:, BAND - TAIL:]
            dbf_ref[hh] += ds[:, 0:128] + ds[:, 128:256] + ds[:, 256:384]
        dq_ref[...] = dq * (A_HEAD_DIM ** -0.5)
        dkp[pl.ds(start, BAND), :] += dkb
        dvp[pl.ds(start, BAND), :] += dvb

        @pl.when(n == nc - 1)
        def _():
            dk_ref[...] = dkp[pl.ds(pad, s), :]
            dv_ref[...] = dvp[pl.ds(pad, s), :]

    return pl.pallas_call(
        body, name="attn_bwd", grid=(4, b, nc),
        in_specs=[pl.BlockSpec((CHUNK, 128), lambda m, bb, n: (bb * nc + n, qb + m)),
                  pl.BlockSpec((s, 128), lambda m, bb, n: (bb, kb_ + m)),
                  pl.BlockSpec((s, 128), lambda m, bb, n: (bb, vb_ + m)),
                  pl.BlockSpec((2, CHUNK, BAND), lambda m, bb, n: (m, 0, 0)),
                  pl.BlockSpec((CHUNK, 128), lambda m, bb, n: (bb * nc + n, m)),
                  pl.BlockSpec((CHUNK, 128), lambda m, bb, n: (bb * nc + n, m)),
                  pl.BlockSpec((CHUNK, 128), lambda m, bb, n: (bb * nc + n, m))],
        out_specs=[pl.BlockSpec((CHUNK, 128), lambda m, bb, n: (bb * nc + n, m)),
                   pl.BlockSpec((s, 128), lambda m, bb, n: (bb, m)),
                   pl.BlockSpec((s, 128), lambda m, bb, n: (bb, m)),
                   pl.BlockSpec((2, CHUNK, TAIL), lambda m, bb, n: (m, 0, 0)),
                   pl.BlockSpec((2, CHUNK, 128), lambda m, bb, n: (m, 0, 0))],
        out_shape=[jax.ShapeDtypeStruct((t, A_WIDTH), F32)] * 3
        + [jax.ShapeDtypeStruct((A_HEADS, CHUNK, TAIL), F32),
           jax.ShapeDtypeStruct((A_HEADS, CHUNK, 128), F32)],
        scratch_shapes=[pltpu.VMEM((s + pad, 128), BF16), pltpu.VMEM((s + pad, 128), BF16),
                        pltpu.VMEM((s + pad, 128), F32), pltpu.VMEM((s + pad, 128), F32)],
        compiler_params=_cp(("parallel", "arbitrary", "arbitrary"), VMEM_LIMIT),
    )(proj, proj, proj, bias_band, dy_a, y_a, lse)


def _conv_taps(x, w, s):
    row = lax.broadcasted_iota(jnp.int32, x.shape, 0)
    shifted = [x] + [jnp.where(row >= i, pltpu.roll(x, i, 0), 0.0) for i in range(1, CONV_K)]
    acc = shifted[0] * w[CONV_K - 1:CONV_K, :]
    for i in range(1, CONV_K):
        acc = acc + shifted[i] * w[CONV_K - 1 - i:CONV_K - i, :]
    return acc, shifted


def conv_fwd(proj, conv_w8, b, s):
    cb = 512
    c0 = P_CONV // cb

    def body(x_ref, w_ref, o_ref):
        a, _ = _conv_taps(x_ref[...], w_ref[...], s)
        o_ref[...] = a * _sigmoid(a)

    return pl.pallas_call(
        body, name="conv_fwd", grid=(b, CONV_CH // cb),
        in_specs=[pl.BlockSpec((s, cb), lambda bb, j: (bb, c0 + j)),
                  pl.BlockSpec((8, cb), lambda bb, j: (0, j))],
        out_specs=pl.BlockSpec((s, cb), lambda bb, j: (bb, j)),
        out_shape=jax.ShapeDtypeStruct((b * s, CONV_CH), F32),
        compiler_params=_cp(("parallel", "parallel"), VMEM_LIMIT),
    )(proj, conv_w8)


def conv_bwd(proj, conv_w8, dc3, b, s):
    cb = 512
    c0 = P_CONV // cb

    def body(x_ref, w_ref, dc_ref, dx_ref, dw_ref):
        @pl.when(pl.program_id(1) == 0)
        def _():
            dw_ref[...] = jnp.zeros_like(dw_ref)

        w = w_ref[...]
        a, shifted = _conv_taps(x_ref[...], w, s)
        sg = _sigmoid(a)
        da = dc_ref[...] * (sg * (1.0 + a * (1.0 - sg)))
        row = lax.broadcasted_iota(jnp.int32, da.shape, 0)
        dx = da * w[CONV_K - 1:CONV_K, :]
        for i in range(1, CONV_K):
            dx = dx + jnp.where(row < s - i, pltpu.roll(da, s - i, 0), 0.0) * w[CONV_K - 1 - i:CONV_K - i, :]
        dx_ref[...] = dx
        r8 = lax.broadcasted_iota(jnp.int32, (8, cb), 0)
        dw = jnp.zeros((8, cb), F32)
        for i in range(CONV_K):
            dw = dw + jnp.where(r8 == CONV_K - 1 - i, jnp.sum(da * shifted[i], axis=0, keepdims=True), 0.0)
        dw_ref[...] += dw

    return pl.pallas_call(
        body, name="conv_bwd", grid=(CONV_CH // cb, b),
        in_specs=[pl.BlockSpec((s, cb), lambda j, bb: (bb, c0 + j)),
                  pl.BlockSpec((8, cb), lambda j, bb: (0, j)),
                  pl.BlockSpec((None, s, cb), lambda j, bb: (j, bb, 0))],
        out_specs=[pl.BlockSpec((s, cb), lambda j, bb: (bb, j)),
                   pl.BlockSpec((8, cb), lambda j, bb: (0, j))],
        out_shape=[jax.ShapeDtypeStruct((b * s, CONV_CH), F32), jax.ShapeDtypeStruct((8, CONV_CH), F32)],
        compiler_params=_cp(("parallel", "arbitrary"), VMEM_LIMIT),
    )(proj, conv_w8, dc3)


def _pick_lane(v, k):
    lane = lax.broadcasted_iota(jnp.int32, v.shape, 1)
    return jnp.sum(jnp.where(lane == k, v, 0.0), axis=1, keepdims=True)


def _chunk_masks(ncb):
    i = lax.broadcasted_iota(jnp.int32, (ncb, CHUNK, CHUNK), 1)
    j = lax.broadcasted_iota(jnp.int32, (ncb, CHUNK, CHUNK), 2)
    return i, j


def _col_of_row(rowvec, eye):
    return jnp.sum(jnp.where(eye, rowvec, 0.0), axis=2, keepdims=True)


def _dn_chunk_math(cq, ck, cv, bd, al_row, dtb_row, h, ncb):
    r = ncb * CHUNK
    i, j = _chunk_masks(ncb)
    eye = i == j
    low = i >= j
    strict = i > j
    ones = jnp.ones((ncb, CHUNK, CHUNK), F32)

    braw = _pick_lane(bd, h)
    draw = _pick_lane(bd, B_HEADS + h)
    al = _pick_lane(al_row, h)
    dtb = _pick_lane(dtb_row, h)
    ea = jnp.exp(al)
    beta = _sigmoid(braw)
    sp_arg = draw + dtb
    g = -ea * _softplus(sp_arg)

    rq = lax.rsqrt(jnp.sum(cq * cq, axis=1, keepdims=True) + EPS)
    rk = lax.rsqrt(jnp.sum(ck * ck, axis=1, keepdims=True) + EPS)
    nq = cq * rq
    kn = ck * rk
    qn = nq * (B_DIM ** -0.5)

    def c3(a):
        return a.reshape(ncb, CHUNK, a.shape[-1])

    qn3, kn3, v3, beta3 = c3(qn), c3(kn), c3(cv), c3(beta)
    gb = jnp.broadcast_to(c3(g), (ncb, CHUNK, CHUNK))
    gc_b = _bnn(low.astype(F32), gb, HI)
    gr_b = _bnn(ones, jnp.where(eye, gc_b, 0.0), HI)
    dm = jnp.where(low, jnp.exp(jnp.where(low, gc_b - gr_b, 0.0)), 0.0)
    gc = gc_b[:, :, 0:1]
    gl = gc_b[:, CHUNK - 1:CHUNK, 0:1]
    gam = jnp.exp(gc)
    egl = jnp.exp(gl)
    edec = jnp.exp(gl - gc)

    knb = _bf(kn3)
    kk = _bnt(knb, knb)
    kd = jnp.where(strict, kk * dm, 0.0)
    a = beta3 * kd
    tm = eye.astype(F32)
    sz = 1
    while sz < CHUNK:
        off = jnp.where(((i // (2 * sz)) == (j // (2 * sz))) & ((i // sz) != (j // sz)), a, 0.0)
        tm = tm - _bnn(_bnn(tm, off, HI), tm, HI)
        sz *= 2
    bv = beta3 * v3
    bk = (beta3 * gam) * kn3
    u = _bnn(tm, bv, HI)
    wk = _bnn(tm, bk, HI)
    qk = _bnt(_bf(qn3), knb)
    p = jnp.where(low, qk * dm, 0.0)
    kdec = kn3 * edec
    qg = gam * qn3
    return dict(beta=beta3, g=c3(g), ea=ea, sp_arg=c3(sp_arg), rq=c3(rq), rk=c3(rk), nq=c3(nq),
                qn=qn3, kn=kn3, v=v3, gc=gc, gl=gl, gam=gam, egl=egl, edec=edec, dm=dm, kd=kd, a=a,
                tm=tm, u=u, wk=wk, qk=qk, p=p, kdec=kdec, qg=qg, eye=eye, low=low, strict=strict)


def dn_prep(c, proj, al_row, dtb_row, b, s, ncb=8):
    t = b * s
    r = ncb * CHUNK
    nblk = t // r
    bd_blk = P_BD // 128

    def body(cq_ref, ck_ref, cv_ref, bd_ref, al_ref, dtb_ref, u_ref, wk_ref, qg_ref, kdec_ref, p_ref, egl_ref):
        h = pl.program_id(1)
        m = _dn_chunk_math(cq_ref[...], ck_ref[...], cv_ref[...], bd_ref[...], al_ref[...], dtb_ref[...], h, ncb)
        u_ref[...] = m["u"].reshape(r, B_DIM)
        wk_ref[...] = m["wk"].reshape(r, B_DIM)
        qg_ref[...] = m["qg"].reshape(r, B_DIM)
        kdec_ref[...] = m["kdec"].reshape(r, B_DIM)
        p_ref[...] = m["p"].reshape(r, CHUNK)
        egl_ref[...] = jnp.broadcast_to(m["egl"], (ncb, 8, 128)).reshape(ncb * 8, 128)

    col = lambda k: pl.BlockSpec((r, 128), lambda i, h: (i, k * B_HEADS + h))
    out_col = pl.BlockSpec((r, 128), lambda i, h: (i, h))
    small = pl.BlockSpec((1, 128), lambda i, h: (0, 0))
    return pl.pallas_call(
        body, name="dn_prep", grid=(nblk, B_HEADS),
        in_specs=[col(0), col(1), col(2), pl.BlockSpec((r, 128), lambda i, h: (i, bd_blk)), small, small],
        out_specs=[out_col, out_col, out_col, out_col,
                   pl.BlockSpec((None, r, CHUNK), lambda i, h: (h, i, 0)),
                   pl.BlockSpec((None, ncb * 8, 128), lambda i, h: (h, i, 0))],
        out_shape=[jax.ShapeDtypeStruct((t, B_WIDTH), F32)] * 4
        + [jax.ShapeDtypeStruct((B_HEADS, t, CHUNK), F32),
           jax.ShapeDtypeStruct((B_HEADS, t // 8, 128), F32)],
        compiler_params=_cp(("parallel", "parallel"), VMEM_LIMIT),
    )(c, c, c, proj, al_row, dtb_row)


def dn_scan_fwd(u, wk, qg, kdec, p, egl, b, s):
    t = b * s
    nc = s // CHUNK
    row = lambda bb, n: (bb * nc + n, 0)

    def body(u_ref, wk_ref, qg_ref, kdec_ref, p_ref, egl_ref, o_ref, ss_ref, st):
        @pl.when(pl.program_id(1) == 0)
        def _():
            st[...] = jnp.zeros_like(st)

        for h in range(B_HEADS):
            sl = slice(h * B_DIM, (h + 1) * B_DIM)
            sh = st[h]
            ss_ref[h] = sh
            sb = _bf(sh)
            w = u_ref[:, sl] - _nt(_bf(wk_ref[:, sl]), sb)
            o_ref[:, sl] = _nt(_bf(qg_ref[:, sl]), sb) + _nn(_bf(p_ref[h]), _bf(w))
            st[h] = egl_ref[h][0:1, :] * sh + _tn(_bf(w), _bf(kdec_ref[:, sl]))

    act = pl.BlockSpec((CHUNK, B_WIDTH), row)
    return pl.pallas_call(
        body, name="dn_scan_fwd", grid=(b, nc),
        in_specs=[act, act, act, act,
                  pl.BlockSpec((B_HEADS, CHUNK, CHUNK), lambda bb, n: (0, bb * nc + n, 0)),
                  pl.BlockSpec((B_HEADS, 8, 128), lambda bb, n: (0, bb * nc + n, 0))],
        out_specs=[act, pl.BlockSpec((None, B_HEADS, B_DIM, B_DIM), lambda bb, n: (bb * nc + n, 0, 0, 0))],
        out_shape=[jax.ShapeDtypeStruct((t, B_WIDTH), F32),
                   jax.ShapeDtypeStruct((t // CHUNK, B_HEADS, B_DIM, B_DIM), F32)],
        scratch_shapes=[pltpu.VMEM((B_HEADS, B_DIM, B_DIM), F32)],
        compiler_params=_cp(("parallel", "arbitrary"), VMEM_LIMIT),
    )(u, wk, qg, kdec, p, egl)


def dn_scan_bwd(u, wk, qg, kdec, p, egl, states, do, b, s):
    t = b * s
    nc = s // CHUNK
    row = lambda bb, n: (bb * nc + nc - 1 - n, 0)
    row3 = lambda bb, n: (0, bb * nc + nc - 1 - n, 0)

    def body(u_ref, wk_ref, qg_ref, kdec_ref, p_ref, egl_ref, ss_ref, do_ref,
             dw_ref, dwk_ref, dqg_ref, dkdec_ref, dp_ref, degl_ref, dst):
        @pl.when(pl.program_id(1) == 0)
        def _():
            dst[...] = jnp.zeros_like(dst)

        for h in range(B_HEADS):
            sl = slice(h * B_DIM, (h + 1) * B_DIM)
            sh = ss_ref[h]
            sb = _bf(sh)
            dsp = dst[h]
            dsb = _bf(dsp)
            wkb = _bf(wk_ref[:, sl])
            kdb = _bf(kdec_ref[:, sl])
            pb = _bf(p_ref[h])
            dob = _bf(do_ref[:, sl])
            w = u_ref[:, sl] - _nt(wkb, sb)
            wb = _bf(w)
            dw = _tn(pb, dob) + _nt(kdb, dsb)
            dwb = _bf(dw)
            dw_ref[:, sl] = dw
            dqg_ref[:, sl] = _nn(dob, sb)
            dwk_ref[:, sl] = -_nn(dwb, sb)
            dkdec_ref[:, sl] = _nn(wb, dsb)
            dp_ref[h] = _nt(dob, wb)
            tot = jnp.sum(jnp.sum(sh * dsp, axis=1, keepdims=True), axis=0, keepdims=True)
            degl_ref[h] = jnp.broadcast_to(tot, (8, 128))
            dst[h] = egl_ref[h][0:1, :] * dsp + _tn(dob, _bf(qg_ref[:, sl])) - _tn(dwb, wkb)

    act = pl.BlockSpec((CHUNK, B_WIDTH), row)
    pspec = pl.BlockSpec((B_HEADS, CHUNK, CHUNK), row3)
    espec = pl.BlockSpec((B_HEADS, 8, 128), row3)
    return pl.pallas_call(
        body, name="dn_scan_bwd", grid=(b, nc),
        in_specs=[act, act, act, act, pspec, espec,
                  pl.BlockSpec((None, B_HEADS, B_DIM, B_DIM), lambda bb, n: (bb * nc + nc - 1 - n, 0, 0, 0)),
                  act],
        out_specs=[act, act, act, act, pspec, espec],
        out_shape=[jax.ShapeDtypeStruct((t, B_WIDTH), F32)] * 4
        + [jax.ShapeDtypeStruct((B_HEADS, t, CHUNK), F32),
           jax.ShapeDtypeStruct((B_HEADS, t // 8, 128), F32)],
        scratch_shapes=[pltpu.VMEM((B_HEADS, B_DIM, B_DIM), F32)],
        compiler_params=_cp(("parallel", "arbitrary"), VMEM_LIMIT),
    )(u, wk, qg, kdec, p, egl, states, do)


def dn_post_bwd(c, proj, al_row, dtb_row, dw, dwk, dqg, dkdec, dp, degl, b, s, ncb=8):
    t = b * s
    r = ncb * CHUNK
    nblk = t // r
    bd_blk = P_BD // 128

    def body(cq_ref, ck_ref, cv_ref, bd_ref, al_ref, dtb_ref, dw_ref, dwk_ref, dqg_ref, dkdec_ref, dp_ref,
             degl_ref, dc_ref, dbd_ref, dal_ref, ddtb_ref):
        h = pl.program_id(1)

        @pl.when((pl.program_id(0) == 0) & (h == 0))
        def _():
            dal_ref[...] = jnp.zeros_like(dal_ref)
            ddtb_ref[...] = jnp.zeros_like(ddtb_ref)

        m = _dn_chunk_math(cq_ref[...], ck_ref[...], cv_ref[...], bd_ref[...], al_ref[...], dtb_ref[...], h, ncb)
        eye, low, strict = m["eye"], m["low"], m["strict"]
        eyef = eye.astype(F32)

        def c3(a):
            return a.reshape(ncb, CHUNK, a.shape[-1])

        du, dwkv, dqg, dkdec = c3(dw_ref[...]), c3(dwk_ref[...]), c3(dqg_ref[...]), c3(dkdec_ref[...])
        dpm = jnp.where(low, c3(dp_ref[...]), 0.0)
        degl = degl_ref[...].reshape(ncb, 8, 128)[:, 0:1, 0:1]
        beta, gam, kn, qn, v = m["beta"], m["gam"], m["kn"], m["qn"], m["v"]
        dm, kd, a, p = m["dm"], m["kd"], m["a"], m["p"]
        knb, qnb = _bf(kn), _bf(qn)

        tt = _bnt(eyef, m["tm"], HI)
        x = _bnn(tt, du, HI)
        y = _bnn(tt, dwkv, HI)
        da = -jnp.where(strict, _bnt(_bf(x), _bf(m["u"])) + _bnt(_bf(y), _bf(m["wk"])), 0.0)
        dv = beta * x
        sy = jnp.sum(y * kn, axis=2, keepdims=True)
        dbeta = jnp.sum(x * v, axis=2, keepdims=True) + gam * sy + jnp.sum(da * kd, axis=2, keepdims=True)
        dgam = beta * sy + jnp.sum(dqg * qn, axis=2, keepdims=True)
        dkk = da * beta * dm
        dqk = dpm * dm
        dkkb, dqkb = _bf(dkk), _bf(dqk)
        eyeb = _bf(eyef)
        dkn = ((beta * gam) * y + _bnn(dkkb, knb) + _bnn(_bf(_bnt(eyeb, dkkb)), knb)
               + _bnn(_bf(_bnt(eyeb, dqkb)), qnb) + dkdec * m["edec"])
        dqn = gam * dqg + _bnn(dqkb, knb)
        mm = da * a + dpm * p
        ek = jnp.sum(dkdec * m["kdec"], axis=2, keepdims=True)
        dgc = (jnp.sum(mm, axis=2, keepdims=True) - _col_of_row(jnp.sum(mm, axis=1, keepdims=True), eye)
               + dgam * gam - ek)
        dgl = jnp.sum(ek, axis=1, keepdims=True) + degl * m["egl"]
        i, _ = _chunk_masks(ncb)
        dgc = dgc + jnp.where(i[:, :, 0:1] == CHUNK - 1, dgl, 0.0)
        upper = (i <= _chunk_masks(ncb)[1]).astype(F32)
        dg = _bnn(upper, jnp.broadcast_to(dgc, (ncb, CHUNK, CHUNK)), HI)[:, :, 0:1]

        nq = m["nq"]
        dnq = dqn * (B_DIM ** -0.5)
        dcq = m["rq"] * (dnq - nq * jnp.sum(nq * dnq, axis=2, keepdims=True))
        dck = m["rk"] * (dkn - kn * jnp.sum(kn * dkn, axis=2, keepdims=True))
        dc_ref[0] = dcq.reshape(r, B_DIM)
        dc_ref[1] = dck.reshape(r, B_DIM)
        dc_ref[2] = dv.reshape(r, B_DIM)

        dbraw = (dbeta * beta * (1.0 - beta)).reshape(r, 1)
        sgm = _sigmoid(m["sp_arg"])
        ddraw3 = dg * (-m["ea"]) * sgm
        ddraw = ddraw3.reshape(r, 1)
        lane = lax.broadcasted_iota(jnp.int32, (r, 128), 1)
        contrib = jnp.where(lane == h, dbraw, 0.0) + jnp.where(lane == B_HEADS + h, ddraw, 0.0)

        @pl.when(h == 0)
        def _():
            dbd_ref[...] = contrib

        @pl.when(h != 0)
        def _():
            dbd_ref[...] += contrib

        lane8 = lax.broadcasted_iota(jnp.int32, (8, 128), 1)
        tot_al = jnp.sum(jnp.sum(dg * m["g"], axis=1, keepdims=True), axis=0, keepdims=True).reshape(1, 1)
        tot_dtb = jnp.sum(jnp.sum(ddraw3, axis=1, keepdims=True), axis=0, keepdims=True).reshape(1, 1)
        dal_ref[...] += jnp.where(lane8 == h, tot_al, 0.0)
        ddtb_ref[...] += jnp.where(lane8 == h, tot_dtb, 0.0)

    col = lambda k: pl.BlockSpec((r, 128), lambda i, h: (i, k * B_HEADS + h))
    hcol = pl.BlockSpec((r, 128), lambda i, h: (i, h))
    small = pl.BlockSpec((1, 128), lambda i, h: (0, 0))
    acc = pl.BlockSpec((8, 128), lambda i, h: (0, 0))
    return pl.pallas_call(
        body, name="dn_post_bwd", grid=(nblk, B_HEADS),
        in_specs=[col(0), col(1), col(2), pl.BlockSpec((r, 128), lambda i, h: (i, bd_blk)), small, small,
                  hcol, hcol, hcol, hcol,
                  pl.BlockSpec((None, r, CHUNK), lambda i, h: (h, i, 0)),
                  pl.BlockSpec((None, ncb * 8, 128), lambda i, h: (h, i, 0))],
        out_specs=[pl.BlockSpec((3, r, 128), lambda i, h: (0, i, h)),
                   pl.BlockSpec((r, 128), lambda i, h: (i, 0)), acc, acc],
        out_shape=[jax.ShapeDtypeStruct((3, t, B_WIDTH), F32), jax.ShapeDtypeStruct((t, 128), F32),
                   jax.ShapeDtypeStruct((8, 128), F32), jax.ShapeDtypeStruct((8, 128), F32)],
        compiler_params=_cp(("arbitrary", "arbitrary"), VMEM_LIMIT),
    )(c, c, c, proj, al_row, dtb_row, dw, dwk, dqg, dkdec, dp, degl)


def make_bias_band(rel_bias):
    tail = bias_tail(jnp.pad(rel_bias, ((0, 0), (0, 384 - N_REL))))
    far = jnp.broadcast_to(rel_bias[:, 2 * REL_CLIP][:, None, None], (A_HEADS, CHUNK, BAND - TAIL))
    return jnp.concatenate([far, jnp.transpose(tail, (1, 0, 2))], axis=2)


def _rms(x):
    r = lax.rsqrt(jnp.mean(x * x, axis=-1, keepdims=True) + EPS)
    return r, x * r


def _rms_bwd(dh, g, r, n):
    dn = dh * g
    return r * (dn - n * jnp.mean(dn * n, axis=-1, keepdims=True)), dh * n


def _gated_onorm(o, z, w_on):
    parts = []
    for h in range(B_HEADS):
        sl = slice(h * B_DIM, (h + 1) * B_DIM)
        r, n = _rms(o[:, sl])
        parts.append((r, n))
    r4 = [p[0] for p in parts]
    n4 = jnp.concatenate([p[1] for p in parts], axis=1)
    w4 = jnp.concatenate([w_on] * B_HEADS, axis=1)
    sz = _sigmoid(z)
    silu = z * sz
    return n4 * w4 * silu, r4, n4, w4, sz, silu


def mid_fwd(x, y_a, o_b, proj, w_on, wa, wb, w_out, tm=256):
    t = x.shape[0]
    tm = min(tm, t)

    def body(x_ref, ya_ref, ob_ref, z_ref, ga_ref, gb_ref, won_ref, wa_ref, wb_ref, wo_ref, x1_ref, mg_ref):
        yb = _gated_onorm(ob_ref[...], z_ref[...], won_ref[...])[0]
        ua = _nn(_bf(ya_ref[...]), wa_ref[...])
        ub = _nn(_bf(yb), wb_ref[...])
        merged = _sigmoid(ga_ref[...]) * ua + _sigmoid(gb_ref[...]) * ub
        mb = _bf(merged)
        mg_ref[...] = mb
        x1_ref[...] = x_ref[...] + _nn(mb, wo_ref[...])

    rowd = pl.BlockSpec((tm, D_MODEL), lambda i: (i, 0))
    row5 = pl.BlockSpec((tm, 512), lambda i: (i, 0))
    full = lambda a: pl.BlockSpec(a.shape, lambda i: (0,) * a.ndim)
    return pl.pallas_call(
        body, name="mid_fwd", grid=(t // tm,),
        in_specs=[rowd, row5, row5,
                  pl.BlockSpec((tm, 512), lambda i: (i, P_Z // 512)),
                  pl.BlockSpec((tm, D_MODEL), lambda i: (i, 0)),
                  pl.BlockSpec((tm, D_MODEL), lambda i: (i, 1)),
                  full(w_on), full(wa), full(wb), full(w_out)],
        out_specs=[rowd, rowd],
        out_shape=[jax.ShapeDtypeStruct((t, D_MODEL), F32), jax.ShapeDtypeStruct((t, D_MODEL), BF16)],
        compiler_params=_cp(("parallel",), VMEM_LIMIT),
    )(x, y_a, o_b, proj, proj, proj, w_on, wa, wb, w_out)


def mid_bwd(dx1, merged, y_a, o_b, proj, w_on, wa, wb, w_out, tm=256):
    t = dx1.shape[0]
    tm = min(tm, t)

    def body(dx1_ref, mg_ref, ya_ref, ob_ref, z_ref, ga_ref, gb_ref, won_ref, wa_ref, wb_ref, wo_ref,
             dya_ref, dob_ref, dz_ref, dg_ref, dwo_ref, dwa_ref, dwb_ref, dwon_ref):
        @pl.when(pl.program_id(0) == 0)
        def _():
            dwo_ref[...] = jnp.zeros_like(dwo_ref)
            dwa_ref[...] = jnp.zeros_like(dwa_ref)
            dwb_ref[...] = jnp.zeros_like(dwb_ref)
            dwon_ref[...] = jnp.zeros_like(dwon_ref)

        dx1b = _bf(dx1_ref[...])
        dmerged = _nt(dx1b, wo_ref[...])
        dwo_ref[...] += _tn(mg_ref[...], dx1b)
        o = ob_ref[...]
        z = z_ref[...]
        yb, r4, n4, w4, sz, silu = _gated_onorm(o, z, won_ref[...])
        yab, ybb = _bf(ya_ref[...]), _bf(yb)
        ua = _nn(yab, wa_ref[...])
        ub = _nn(ybb, wb_ref[...])
        sa, sb = _sigmoid(ga_ref[...]), _sigmoid(gb_ref[...])
        dua, dub = _bf(dmerged * sa), _bf(dmerged * sb)
        dg_ref[:, 0:D_MODEL] = dmerged * ua * sa * (1.0 - sa)
        dg_ref[:, D_MODEL:2 * D_MODEL] = dmerged * ub * sb * (1.0 - sb)
        dwa_ref[...] += _tn(yab, dua)
        dwb_ref[...] += _tn(ybb, dub)
        dya_ref[...] = _nt(dua, wa_ref[...])
        dyb = _nt(dub, wb_ref[...])
        dz_ref[...] = dyb * (n4 * w4) * (sz * (1.0 + z * (1.0 - sz)))
        dnw = dyb * silu
        dwon = jnp.zeros((1, B_DIM), F32)
        for h in range(B_HEADS):
            sl = slice(h * B_DIM, (h + 1) * B_DIM)
            dxh, dgh = _rms_bwd(dnw[:, sl], won_ref[...], r4[h], n4[:, sl])
            dob_ref[:, sl] = dxh
            dwon = dwon + jnp.sum(dgh, axis=0, keepdims=True)
        dwon_ref[...] += jnp.broadcast_to(dwon, (8, B_DIM))

    rowd = pl.BlockSpec((tm, D_MODEL), lambda i: (i, 0))
    row5 = pl.BlockSpec((tm, 512), lambda i: (i, 0))
    full = lambda a: pl.BlockSpec(a.shape, lambda i: (0,) * a.ndim)
    fixed = lambda shp: pl.BlockSpec(shp, lambda i: (0,) * len(shp))
    return pl.pallas_call(
        body, name="mid_bwd", grid=(t // tm,),
        in_specs=[rowd, rowd, row5, row5,
                  pl.BlockSpec((tm, 512), lambda i: (i, P_Z // 512)),
                  pl.BlockSpec((tm, D_MODEL), lambda i: (i, 0)),
                  pl.BlockSpec((tm, D_MODEL), lambda i: (i, 1)),
                  full(w_on), full(wa), full(wb), full(w_out)],
        out_specs=[row5, row5, row5, pl.BlockSpec((tm, 2 * D_MODEL), lambda i: (i, 0)),
                   fixed((D_MODEL, D_MODEL)), fixed((A_WIDTH, D_MODEL)), fixed((B_WIDTH, D_MODEL)),
                   fixed((8, B_DIM))],
        out_shape=[jax.ShapeDtypeStruct((t, 512), F32)] * 3
        + [jax.ShapeDtypeStruct((t, 2 * D_MODEL), F32),
           jax.ShapeDtypeStruct((D_MODEL, D_MODEL), F32), jax.ShapeDtypeStruct((A_WIDTH, D_MODEL), F32),
           jax.ShapeDtypeStruct((B_WIDTH, D_MODEL), F32), jax.ShapeDtypeStruct((8, B_DIM), F32)],
        compiler_params=_cp(("arbitrary",), VMEM_LIMIT),
    )(dx1, merged, y_a, o_b, proj, proj, proj, w_on, wa, wb, w_out)


def ffn_fwd(x1, g, w_gu, w_down, tm=512, tf=256):
    t = x1.shape[0]
    tm = min(tm, t)
    nf = D_FF // tf

    def body(x_ref, g_ref, wg_ref, wu_ref, wd_ref, x2_ref, gate_ref, up_ref, h_ref):
        j = pl.program_id(1)

        @pl.when(j == 0)
        def _():
            xv = x_ref[...]
            r, n = _rms(xv)
            h_ref[...] = _bf(n * g_ref[...])
            x2_ref[...] = xv

        hb = h_ref[...]
        gate = _nn(hb, wg_ref[...])
        up = _nn(hb, wu_ref[...])
        gate_ref[...] = gate
        up_ref[...] = up
        act = gate * _sigmoid(gate) * up
        x2_ref[...] += _nn(_bf(act), wd_ref[...])

    return pl.pallas_call(
        body, name="ffn_fwd", grid=(t // tm, nf),
        in_specs=[pl.BlockSpec((tm, D_MODEL), lambda i, j: (i, 0)),
                  pl.BlockSpec((1, D_MODEL), lambda i, j: (0, 0)),
                  pl.BlockSpec((D_MODEL, tf), lambda i, j: (0, j)),
                  pl.BlockSpec((D_MODEL, tf), lambda i, j: (0, nf + j)),
                  pl.BlockSpec((tf, D_MODEL), lambda i, j: (j, 0))],
        out_specs=[pl.BlockSpec((tm, D_MODEL), lambda i, j: (i, 0)),
                   pl.BlockSpec((tm, tf), lambda i, j: (i, j)),
                   pl.BlockSpec((tm, tf), lambda i, j: (i, j)),
                   pl.BlockSpec((tm, D_MODEL), lambda i, j: (i, 0))],
        out_shape=[jax.ShapeDtypeStruct((t, D_MODEL), F32), jax.ShapeDtypeStruct((t, D_FF), F32),
                   jax.ShapeDtypeStruct((t, D_FF), F32), jax.ShapeDtypeStruct((t, D_MODEL), BF16)],
        compiler_params=_cp(("parallel", "arbitrary"), VMEM_LIMIT),
    )(x1, g, w_gu, w_gu, w_down)


def ffn_bwd(dx2, x1, g, gate, up, w_gu, w_down, tm=512, tf=256):
    t = x1.shape[0]
    tm = min(tm, t)
    nf = D_FF // tf

    def body(dx2_ref, x_ref, g_ref, gate_ref, up_ref, wg_ref, wu_ref, wd_ref,
             dx1_ref, act_ref, dgate_ref, dup_ref, dg_ref, dh_acc, dx2b_ref):
        i, j = pl.program_id(0), pl.program_id(1)

        @pl.when((i == 0) & (j == 0))
        def _():
            dg_ref[...] = jnp.zeros_like(dg_ref)

        @pl.when(j == 0)
        def _():
            dx2b_ref[...] = _bf(dx2_ref[...])
            dh_acc[...] = jnp.zeros_like(dh_acc)

        dact = _nt(dx2b_ref[...], wd_ref[...])
        gt, upv = gate_ref[...], up_ref[...]
        sg = _sigmoid(gt)
        silu = gt * sg
        act_ref[...] = _bf(silu * upv)
        dgt = _bf(dact * upv * (sg * (1.0 + gt * (1.0 - sg))))
        dupv = _bf(dact * silu)
        dgate_ref[...] = dgt
        dup_ref[...] = dupv
        dh_acc[...] += _nt(dgt, wg_ref[...]) + _nt(dupv, wu_ref[...])

        @pl.when(j == nf - 1)
        def _():
            r, n = _rms(x_ref[...])
            dx, dgc = _rms_bwd(dh_acc[...], g_ref[...], r, n)
            dx1_ref[...] = dx2_ref[...] + dx
            dg_ref[...] += jnp.broadcast_to(jnp.sum(dgc, axis=0, keepdims=True), (8, D_MODEL))

    rowd = pl.BlockSpec((tm, D_MODEL), lambda i, j: (i, 0))
    ff = pl.BlockSpec((tm, tf), lambda i, j: (i, j))
    return pl.pallas_call(
        body, name="ffn_bwd", grid=(t // tm, nf),
        in_specs=[rowd, rowd, pl.BlockSpec((1, D_MODEL), lambda i, j: (0, 0)), ff, ff,
                  pl.BlockSpec((D_MODEL, tf), lambda i, j: (0, j)),
                  pl.BlockSpec((D_MODEL, tf), lambda i, j: (0, nf + j)),
                  pl.BlockSpec((tf, D_MODEL), lambda i, j: (j, 0))],
        out_specs=[rowd, ff, ff, ff, pl.BlockSpec((8, D_MODEL), lambda i, j: (0, 0))],
        out_shape=[jax.ShapeDtypeStruct((t, D_MODEL), F32), jax.ShapeDtypeStruct((t, D_FF), BF16),
                   jax.ShapeDtypeStruct((t, D_FF), BF16), jax.ShapeDtypeStruct((t, D_FF), BF16),
                   jax.ShapeDtypeStruct((8, D_MODEL), F32)],
        scratch_shapes=[pltpu.VMEM((tm, D_MODEL), F32), pltpu.VMEM((tm, D_MODEL), BF16)],
        compiler_params=_cp(("arbitrary", "arbitrary"), VMEM_LIMIT),
    )(dx2, x1, g, gate, up, w_gu, w_gu, w_down)


def tail_fwd_bwd(x2, p, target, g_ple, g_final, w_pg, w_pp, tm=256):
    t = x2.shape[0]
    tm = min(tm, t)

    def body(x_ref, p_ref, t_ref, gp_ref, gf_ref, wpg_ref, wpp_ref,
             dx_ref, dwpg_ref, dwpp_ref, dgp_ref, dgf_ref, loss_ref):
        @pl.when(pl.program_id(0) == 0)
        def _():
            dwpg_ref[...] = jnp.zeros_like(dwpg_ref)
            dwpp_ref[...] = jnp.zeros_like(dwpp_ref)
            dgp_ref[...] = jnp.zeros_like(dgp_ref)
            dgf_ref[...] = jnp.zeros_like(dgf_ref)
            loss_ref[...] = jnp.zeros_like(loss_ref)

        x2v = x_ref[...]
        gp, gf = gp_ref[...], gf_ref[...]
        r3, n3 = _rms(x2v)
        h3b = _bf(n3 * gp)
        pb = _bf(p_ref[...])
        pg = _sigmoid(_nn(h3b, wpg_ref[...]))
        pp = _nn(pb, wpp_ref[...])
        x3 = x2v + pg * pp
        r4, n4 = _rms(x3)
        err = n4 * gf - t_ref[...]
        part = 0.5 * jnp.sum(jnp.sum(err * err, axis=1, keepdims=True), axis=0, keepdims=True) / D_MODEL
        loss_ref[...] += jnp.broadcast_to(part, (8, 128))
        dy = err * (1.0 / D_MODEL)
        dx3, dgf = _rms_bwd(dy, gf, r4, n4)
        dgf_ref[...] += jnp.broadcast_to(jnp.sum(dgf, axis=0, keepdims=True), (8, D_MODEL))
        dzp = _bf(dx3 * pp * pg * (1.0 - pg))
        dpp = _bf(dx3 * pg)
        dwpg_ref[...] += _tn(h3b, dzp)
        dwpp_ref[...] += _tn(pb, dpp)
        dh3 = _nt(dzp, wpg_ref[...])
        dx, dgp = _rms_bwd(dh3, gp, r3, n3)
        dgp_ref[...] += jnp.broadcast_to(jnp.sum(dgp, axis=0, keepdims=True), (8, D_MODEL))
        dx_ref[...] = dx3 + dx

    rowd = pl.BlockSpec((tm, D_MODEL), lambda i: (i, 0))
    fixed = lambda shp: pl.BlockSpec(shp, lambda i: (0,) * len(shp))
    return pl.pallas_call(
        body, name="tail_fwd_bwd", grid=(t // tm,),
        in_specs=[rowd, pl.BlockSpec((tm, PLE_DIM), lambda i: (i, 0)), rowd,
                  fixed((1, D_MODEL)), fixed((1, D_MODEL)), fixed((D_MODEL, D_MODEL)), fixed((PLE_DIM, D_MODEL))],
        out_specs=[rowd, fixed((D_MODEL, D_MODEL)), fixed((PLE_DIM, D_MODEL)),
                   fixed((8, D_MODEL)), fixed((8, D_MODEL)), fixed((8, 128))],
        out_shape=[jax.ShapeDtypeStruct((t, D_MODEL), F32), jax.ShapeDtypeStruct((D_MODEL, D_MODEL), F32),
                   jax.ShapeDtypeStruct((PLE_DIM, D_MODEL), F32), jax.ShapeDtypeStruct((8, D_MODEL), F32),
                   jax.ShapeDtypeStruct((8, D_MODEL), F32), jax.ShapeDtypeStruct((8, 128), F32)],
        compiler_params=_cp(("arbitrary",), VMEM_LIMIT),
    )(x2, p, target, g_ple, g_final, w_pg, w_pp)


def in_proj_bwd(pieces, weights, x, dx1, g, tm=256):
    t = x.shape[0]
    tm = min(tm, t)
    k = len(pieces)

    def body(*refs):
        p_refs, w_refs = refs[:k], refs[k:2 * k]
        x_ref, dx1_ref, g_ref, dx_ref, dg_ref = refs[2 * k:]

        @pl.when(pl.program_id(0) == 0)
        def _():
            dg_ref[...] = jnp.zeros_like(dg_ref)

        dh = _nt(_bf(p_refs[0][...]), w_refs[0][...])
        for pr, wr in zip(p_refs[1:], w_refs[1:]):
            dh = dh + _nt(_bf(pr[...]), wr[...])
        r, n = _rms(x_ref[...])
        dx, dgc = _rms_bwd(dh, g_ref[...], r, n)
        dx_ref[...] = dx1_ref[...] + dx
        dg_ref[...] += jnp.broadcast_to(jnp.sum(dgc, axis=0, keepdims=True), (8, D_MODEL))

    rowd = pl.BlockSpec((tm, D_MODEL), lambda i: (i, 0))
    return pl.pallas_call(
        body, name="in_proj_bwd", grid=(t // tm,),
        in_specs=[pl.BlockSpec((tm, a.shape[1]), lambda i: (i, 0)) for a in pieces]
        + [pl.BlockSpec(w.shape, lambda i: (0, 0)) for w in weights]
        + [rowd, rowd, pl.BlockSpec((1, D_MODEL), lambda i: (0, 0))],
        out_specs=[rowd, pl.BlockSpec((8, D_MODEL), lambda i: (0, 0))],
        out_shape=[jax.ShapeDtypeStruct((t, D_MODEL), F32), jax.ShapeDtypeStruct((8, D_MODEL), F32)],
        compiler_params=_cp(("arbitrary",), VMEM_LIMIT),
    )(*pieces, *weights, x, dx1, g)


def adamw(w, g, m, v, name, rows_cap=256):
    r, c = w.shape
    tr = r
    for cand in range(8, min(r, rows_cap) + 1, 8):
        if r % cand == 0:
            tr = cand

    def body(w_ref, g_ref, m_ref, v_ref, d_ref, mo_ref, vo_ref):
        gv = g_ref[...]
        mn = ADAM_B1 * m_ref[...] + (1.0 - ADAM_B1) * gv
        vn = ADAM_B2 * v_ref[...] + (1.0 - ADAM_B2) * (gv * gv)
        m_hat = mn / (1.0 - ADAM_B1 ** ADAM_STEP)
        v_hat = vn / (1.0 - ADAM_B2 ** ADAM_STEP)
        d_ref[...] = -ADAM_LR * (m_hat / (jnp.sqrt(v_hat) + ADAM_EPS) + ADAM_WD * w_ref[...])
        mo_ref[...] = mn
        vo_ref[...] = vn

    spec = pl.BlockSpec((tr, c), lambda i: (i, 0))
    return pl.pallas_call(
        body, name=name, grid=(r // tr,),
        in_specs=[spec] * 4, out_specs=[spec] * 3,
        out_shape=[jax.ShapeDtypeStruct((r, c), F32)] * 3,
        compiler_params=_cp(("parallel",), VMEM_LIMIT),
    )(w, g, m, v)


def local_step(x3d, p3d, target3d, wts, small):
    b, s, _ = x3d.shape
    t = b * s
    x = x3d.reshape(t, D_MODEL)
    p = p3d.reshape(t, PLE_DIM)
    target = target3d.reshape(t, D_MODEL)
    w_in = wts["w_in"]
    w_inp = jnp.concatenate([w_in[:, SPLIT_Z + 8:], w_in[:, :SPLIT_Z], w_in[:, SPLIT_Z:SPLIT_Z + 8],
                             jnp.zeros((D_MODEL, 120), BF16)], axis=1)
    al_row = jnp.pad(small["a_log"].reshape(1, B_HEADS), ((0, 0), (0, 128 - B_HEADS)))
    dtb_row = jnp.pad(small["dt_bias"].reshape(1, B_HEADS), ((0, 0), (0, 128 - B_HEADS)))
    conv_w8 = jnp.pad(small["conv_w"].reshape(CONV_K, CONV_CH), ((0, 8 - CONV_K), (0, 0)))
    w_on = small["w_onorm"].reshape(1, B_DIM)
    g_mix, g_ffn = small["g_mix"].reshape(1, D_MODEL), small["g_ffn"].reshape(1, D_MODEL)
    g_ple, g_final = small["g_ple"].reshape(1, D_MODEL), small["g_final"].reshape(1, D_MODEL)
    bias_band = make_bias_band(small["rel_bias"].reshape(A_HEADS, N_REL))

    proj, h1 = rms_matmul(x, g_mix, w_inp, "in_proj")
    y_a, lse = attn_fwd(proj, bias_band, b, s)
    c = conv_fwd(proj, conv_w8, b, s)
    u, wk, qg, kdec, pm, egl = dn_prep(c, proj, al_row, dtb_row, b, s)
    o_b, states = dn_scan_fwd(u, wk, qg, kdec, pm, egl, b, s)
    x1, merged = mid_fwd(x, y_a, o_b, proj, w_on, wts["w_branch_a"], wts["w_branch_b"], wts["w_out"])
    x2, gate, up, h2 = ffn_fwd(x1, g_ffn, wts["w_gate_up"], wts["w_down"])

    dx2, dw_pg, dw_pp, dg_ple, dg_final, loss = tail_fwd_bwd(
        x2, p, target, g_ple, g_final, wts["w_ple_gate"], wts["w_ple_proj"])
    dx1, act, dgate, dup, dg_ffn = ffn_bwd(dx2, x1, g_ffn, gate, up, wts["w_gate_up"], wts["w_down"])
    dw_down = matmul_tn(act, dx2, "dw_down")
    dw_gu = jnp.concatenate([matmul_tn(h2, dgate, "dw_gate"), matmul_tn(h2, dup, "dw_up")], axis=1)
    dy_a, do_b, dz, dgates, dw_out, dwa, dwb, dw_on = mid_bwd(
        dx1, merged, y_a, o_b, proj, w_on, wts["w_branch_a"], wts["w_branch_b"], wts["w_out"])
    ddw, ddwk, ddqg, ddkdec, ddp, ddegl = dn_scan_bwd(u, wk, qg, kdec, pm, egl, states, do_b, b, s)
    dc3, dbd, dal, ddtb = dn_post_bwd(c, proj, al_row, dtb_row, ddw, ddwk, ddqg, ddkdec, ddp, ddegl, b, s)
    dconv, dconv_w = conv_bwd(proj, conv_w8, dc3, b, s)
    dqa, dka, dva, dbt, dbf = attn_bwd(proj, bias_band, y_a, lse, dy_a, b, s)
    d_rel = bias_grad(jnp.transpose(dbt, (1, 0, 2)), dbf)[:, :N_REL]

    pieces = [dgates, dqa, dka, dva, dconv, dz, dbd]
    bounds = [0, 2048, 2560, 3072, 3584, 5120, 5632, 5760]
    w_pieces = [w_inp[:, lo:hi] for lo, hi in zip(bounds[:-1], bounds[1:])]
    dx, dg_mix = in_proj_bwd(pieces, w_pieces, x, dx1, g_mix)
    dwp = [matmul_tn(h1, pc, "dw_in_%d" % k) for k, pc in enumerate(pieces)]
    dw_in = jnp.concatenate([dwp[1], dwp[2], dwp[3], dwp[4], dwp[5], dwp[6][:, :8], dwp[0]], axis=1)

    grads = dict(w_in=dw_in, w_branch_a=dwa, w_branch_b=dwb, w_out=dw_out, w_gate_up=dw_gu, w_down=dw_down,
                 w_ple_gate=dw_pg, w_ple_proj=dw_pp)
    small_grads = dict(g_mix=dg_mix[0], g_ffn=dg_ffn[0], g_ple=dg_ple[0], g_final=dg_final[0],
                       conv_w=dconv_w[:CONV_K].reshape(-1), rel_bias=d_rel.reshape(-1), w_onorm=dw_on[0],
                       a_log=dal[0, :B_HEADS], dt_bias=ddtb[0, :B_HEADS], loss=loss[0, :1])
    return dx.reshape(b, s, D_MODEL), grads, small_grads


BIG = (("w_in", (D_MODEL, D_IN), 1), ("w_branch_a", (A_WIDTH, D_MODEL), 1), ("w_branch_b", (B_WIDTH, D_MODEL), 1),
       ("w_out", (D_MODEL, D_MODEL), 0), ("w_gate_up", (D_MODEL, 2 * D_FF), 1), ("w_down", (D_FF, D_MODEL), 0),
       ("w_ple_gate", (D_MODEL, D_MODEL), 0), ("w_ple_proj", (PLE_DIM, D_MODEL), 1))
N_CHIPS = 4
PACK_ROWS = -(-sum(sh[0] * sh[1] for _, sh, _ in BIG) // (N_CHIPS * 128 * 64)) * 64
HALF_ROWS = PACK_ROWS // 2


def _shard_shape(shape, axis):
    return (shape[0] // N_CHIPS, shape[1]) if axis == 0 else (shape[0], shape[1] // N_CHIPS)


def _pack_rows(parts, total):
    used = sum(a.shape[-2] for a in parts)
    pad = jnp.zeros(parts[0].shape[:-2] + (total - used, 128), parts[0].dtype)
    return jnp.concatenate(parts + [pad], axis=-2)


def pack_shards(shards, dtype):
    return _pack_rows([shards[n].astype(dtype).reshape(-1, 128) for n, _, _ in BIG], PACK_ROWS)


def unpack_full(wall):
    out, r0 = {}, 0
    for n, shape, axis in BIG:
        rs, cs = _shard_shape(shape, axis)
        nr = rs * cs // 128
        seg = wall[:, r0:r0 + nr, :].reshape(N_CHIPS, rs, cs)
        out[n] = seg.reshape(shape) if axis == 0 else jnp.transpose(seg, (1, 0, 2)).reshape(shape)
        r0 += nr
    return out


def pack_grads(grads):
    parts = []
    for n, shape, axis in BIG:
        rs, cs = _shard_shape(shape, axis)
        g = grads[n]
        seg = g.reshape(N_CHIPS, rs, cs) if axis == 0 else jnp.transpose(g.reshape(rs, N_CHIPS, cs), (1, 0, 2))
        parts.append(seg.reshape(N_CHIPS, -1, 128))
    return _pack_rows(parts, PACK_ROWS)


def unpack_shard(flat):
    out, r0 = {}, 0
    for n, shape, axis in BIG:
        rs, cs = _shard_shape(shape, axis)
        nr = rs * cs // 128
        out[n] = flat[r0:r0 + nr].reshape(rs, cs)
        r0 += nr
    return out


def _place():
    return lax.axis_index("x"), lax.axis_index("y"), lax.axis_index("c")


ANY = pl.BlockSpec(memory_space=pl.ANY)


def allgather_weights(wp):
    half = HALF_ROWS

    def body(w_ref, out_ref, send_sems, recv_sems, local_sem):
        x, y, c = _place()
        sibling = (x, y, 1 - c)
        chips = [(1 - x, y), (x, 1 - y), (1 - x, 1 - y)]

        def blk(cx, cy, hf):
            return out_ref.at[2 * cx + cy, pl.ds(hf * half, half), :]

        def copy(k, src, dst, to):
            return pltpu.make_async_remote_copy(src_ref=src, dst_ref=dst, send_sem=send_sems.at[k],
                                                recv_sem=recv_sems.at[k], device_id=to, device_id_type=MESH)

        mine = pltpu.make_async_copy(w_ref, out_ref.at[2 * x + y], local_sem)
        mine.start()
        my_half = w_ref.at[pl.ds(c * half, half), :]
        first = [copy(j, my_half, blk(x, y, c), (*chip, c)) for j, chip in enumerate(chips)]
        for cp in first:
            cp.start()
        passed = [copy(3 + j, blk(*chip, c), blk(*chip, c), sibling) for j, chip in enumerate(chips)]
        for j, chip in enumerate(chips):
            copy(j, my_half, blk(*chip, c), (*chip, c)).wait_recv()
            passed[j].start()
        for j, chip in enumerate(chips):
            copy(3 + j, my_half, blk(*chip, 1 - c), sibling).wait_recv()
        for cp in first + passed:
            cp.wait_send()
        mine.wait()

    return pl.pallas_call(
        body, name="allgather_weights",
        in_specs=[ANY], out_specs=ANY,
        out_shape=jax.ShapeDtypeStruct((N_CHIPS, PACK_ROWS, 128), wp.dtype),
        scratch_shapes=[pltpu.SemaphoreType.DMA((6,)), pltpu.SemaphoreType.DMA((6,)), pltpu.SemaphoreType.DMA],
    )(wp)


def small_allreduce(v, name):
    r = v.shape[0]

    def body(v_ref, o_ref, buf, send_sems, recv_sems):
        x, y, c = _place()
        me = 4 * x + 2 * y + c
        buf[me] = v_ref[...]
        flips = [(fx, fy, fc) for fx in (0, 1) for fy in (0, 1) for fc in (0, 1)][1:]
        peers = [((1 - x) if fx else x, (1 - y) if fy else y, (1 - c) if fc else c) for fx, fy, fc in flips]

        def copy(k, slot, to):
            return pltpu.make_async_remote_copy(src_ref=v_ref, dst_ref=buf.at[slot], send_sem=send_sems.at[k],
                                                recv_sem=recv_sems.at[k], device_id=to, device_id_type=MESH)

        sends = [copy(k, me, peer) for k, peer in enumerate(peers)]
        for cp in sends:
            cp.start()
        for k, (px, py, pc) in enumerate(peers):
            copy(k, 4 * px + 2 * py + pc, (px, py, pc)).wait_recv()
        for cp in sends:
            cp.wait_send()
        acc = buf[0]
        for d in range(1, 8):
            acc = acc + buf[d]
        o_ref[...] = acc

    return pl.pallas_call(
        body, name=name,
        in_specs=[pl.BlockSpec(memory_space=pltpu.VMEM)], out_specs=pl.BlockSpec(memory_space=pltpu.VMEM),
        out_shape=jax.ShapeDtypeStruct((r, 128), F32),
        scratch_shapes=[pltpu.VMEM((8, r, 128), F32), pltpu.SemaphoreType.DMA((7,)), pltpu.SemaphoreType.DMA((7,))],
    )(v)


def swap_halves(g):
    half = HALF_ROWS

    def body(g_ref, o_ref, send_sem, recv_sem):
        x, y, c = _place()
        cp = pltpu.make_async_remote_copy(
            src_ref=g_ref.at[:, pl.ds((1 - c) * half, half), :], dst_ref=o_ref, send_sem=send_sem,
            recv_sem=recv_sem, device_id=(x, y, 1 - c), device_id_type=MESH)
        cp.start()
        cp.wait()

    return pl.pallas_call(
        body, name="swap_halves", in_specs=[ANY], out_specs=ANY,
        out_shape=jax.ShapeDtypeStruct((N_CHIPS, half, 128), F32),
        scratch_shapes=[pltpu.SemaphoreType.DMA, pltpu.SemaphoreType.DMA],
    )(g)


def add_halves(g, other, place):
    half = HALF_ROWS
    tr = _tile_rows(half)
    nblk = half // tr

    def body(pref, g0, g1, g2, g3, o0, o1, o2, o3, pf_ref, pb_ref):
        pf_ref[...] = g0[...] + o0[...]
        pb_ref[0] = _bf(g1[...] + o1[...])
        pb_ref[1] = _bf(g2[...] + o2[...])
        pb_ref[2] = _bf(g3[...] + o3[...])

    gspec = lambda k: pl.BlockSpec((None, tr, 128), lambda i, pr: ((pr[0] + k) % N_CHIPS, pr[1] * nblk + i, 0))
    ospec = lambda k: pl.BlockSpec((None, tr, 128), lambda i, pr: ((pr[0] + k) % N_CHIPS, i, 0))
    return pl.pallas_call(
        body, name="add_halves",
        grid_spec=pltpu.PrefetchScalarGridSpec(
            num_scalar_prefetch=1, grid=(nblk,),
            in_specs=[gspec(0), gspec(1), gspec(2), gspec(3), ospec(0), ospec(1), ospec(2), ospec(3)],
            out_specs=[pl.BlockSpec((tr, 128), lambda i, pr: (i, 0)),
                       pl.BlockSpec((3, tr, 128), lambda i, pr: (0, i, 0))]),
        out_shape=[jax.ShapeDtypeStruct((half, 128), F32), jax.ShapeDtypeStruct((3, half, 128), BF16)],
        compiler_params=_cp(("parallel",), VMEM_LIMIT),
    )(place, g, g, g, g, other, other, other, other)


def _tile_rows(n, cap=2048):
    best = 16
    for t in range(16, cap + 1, 16):
        if n % t == 0:
            best = t
    assert n % best == 0
    return best


def exchange_partials(pb):
    def body(p_ref, o_ref, send_sems, recv_sems):
        x, y, c = _place()
        me = 2 * x + y
        cps = []
        for k in range(1, N_CHIPS):
            to = (me + k) % N_CHIPS
            cps.append(pltpu.make_async_remote_copy(
                src_ref=p_ref.at[k - 1], dst_ref=o_ref.at[k - 1], send_sem=send_sems.at[k - 1],
                recv_sem=recv_sems.at[k - 1], device_id=(to // 2, to % 2, c), device_id_type=MESH))
        for cp in cps:
            cp.start()
        for cp in cps:
            cp.wait()

    return pl.pallas_call(
        body, name="exchange_partials", in_specs=[ANY], out_specs=ANY,
        out_shape=jax.ShapeDtypeStruct(pb.shape, pb.dtype),
        scratch_shapes=[pltpu.SemaphoreType.DMA((3,)), pltpu.SemaphoreType.DMA((3,))],
    )(pb)


def add_partials(pf, got):
    half = HALF_ROWS
    tr = _tile_rows(half)

    def body(pf_ref, got_ref, o_ref):
        o_ref[...] = ((pf_ref[...] + got_ref[0].astype(F32)) + got_ref[1].astype(F32)) + got_ref[2].astype(F32)

    return pl.pallas_call(
        body, name="add_partials", grid=(half // tr,),
        in_specs=[pl.BlockSpec((tr, 128), lambda i: (i, 0)), pl.BlockSpec((3, tr, 128), lambda i: (0, i, 0))],
        out_specs=pl.BlockSpec((tr, 128), lambda i: (i, 0)),
        out_shape=jax.ShapeDtypeStruct((half, 128), F32),
        compiler_params=_cp(("parallel",), VMEM_LIMIT),
    )(pf, got)


def join_halves(rh):
    def body(r_ref, o_ref, send_sem, recv_sem, local_sem):
        x, y, c = _place()
        mine = pltpu.make_async_copy(r_ref, o_ref.at[c], local_sem)
        mine.start()
        cp = pltpu.make_async_remote_copy(src_ref=r_ref, dst_ref=o_ref.at[c], send_sem=send_sem, recv_sem=recv_sem,
                                          device_id=(x, y, 1 - c), device_id_type=MESH)
        cp.start()
        pltpu.make_async_remote_copy(src_ref=r_ref, dst_ref=o_ref.at[1 - c], send_sem=send_sem, recv_sem=recv_sem,
                                     device_id=(x, y, 1 - c), device_id_type=MESH).wait_recv()
        cp.wait_send()
        mine.wait()

    return pl.pallas_call(
        body, name="join_halves", in_specs=[ANY], out_specs=ANY,
        out_shape=jax.ShapeDtypeStruct((2,) + rh.shape, F32),
        scratch_shapes=[pltpu.SemaphoreType.DMA, pltpu.SemaphoreType.DMA, pltpu.SemaphoreType.DMA],
    )(rh)


def reduce_scatter_grads(gpack):
    x, y, c = _place()
    place = jnp.stack([2 * x + y, c]).astype(jnp.int32)
    other = swap_halves(gpack)
    pf, pb = add_halves(gpack, other, place)
    got = exchange_partials(pb)
    rh = add_partials(pf, got)
    return join_halves(rh).reshape(PACK_ROWS, 128)


SMALL = (("g_mix", D_MODEL), ("g_ffn", D_MODEL), ("g_ple", D_MODEL), ("g_final", D_MODEL),
         ("conv_w", CONV_K * CONV_CH), ("rel_bias", A_HEADS * N_REL), ("w_onorm", B_DIM),
         ("a_log", B_HEADS), ("dt_bias", B_HEADS), ("loss", 1))


def _pad128(v):
    v = v.reshape(-1)
    return jnp.pad(v, (0, -v.shape[0] % 128))


def pack_small(d, names, rows):
    flat = jnp.concatenate([_pad128(d[n]) for n in names]).reshape(-1, 128)
    return jnp.pad(flat, ((0, rows - flat.shape[0]), (0, 0)))


def unpack_small(flat, names_sizes):
    out, r0 = {}, 0
    v = flat.reshape(-1)
    for n, size in names_sizes:
        out[n] = v[r0:r0 + size]
        r0 += -(-size // 128) * 128
    return out


def kernel(x, p, g_mix, w_in, conv_w, a_log, dt_bias, rel_bias, w_onorm, w_branch_a, w_branch_b, w_out, g_ffn, w_gate_up, w_down, g_ple, w_ple_gate, w_ple_proj, g_final, loss_target, m_g_mix, m_w_in, m_conv_w, m_a_log, m_dt_bias, m_rel_bias, m_w_onorm, m_w_branch_a, m_w_branch_b, m_w_out, m_g_ffn, m_w_gate_up, m_w_down, m_g_ple, m_w_ple_gate, m_w_ple_proj, m_g_final, v_g_mix, v_w_in, v_conv_w, v_a_log, v_dt_bias, v_rel_bias, v_w_onorm, v_w_branch_a, v_w_branch_b, v_w_out, v_g_ffn, v_w_gate_up, v_w_down, v_g_ple, v_w_ple_gate, v_w_ple_proj, v_g_final):
    names = ["g_mix", "w_in", "conv_w", "a_log", "dt_bias", "rel_bias", "w_onorm", "w_branch_a", "w_branch_b",
             "w_out", "g_ffn", "w_gate_up", "w_down", "g_ple", "w_ple_gate", "w_ple_proj", "g_final"]
    w = dict(zip(names, [g_mix, w_in, conv_w, a_log, dt_bias, rel_bias, w_onorm, w_branch_a, w_branch_b, w_out,
                         g_ffn, w_gate_up, w_down, g_ple, w_ple_gate, w_ple_proj, g_final]))
    m = dict(zip(names, [m_g_mix, m_w_in, m_conv_w, m_a_log, m_dt_bias, m_rel_bias, m_w_onorm, m_w_branch_a,
                         m_w_branch_b, m_w_out, m_g_ffn, m_w_gate_up, m_w_down, m_g_ple, m_w_ple_gate,
                         m_w_ple_proj, m_g_final]))
    v = dict(zip(names, [v_g_mix, v_w_in, v_conv_w, v_a_log, v_dt_bias, v_rel_bias, v_w_onorm, v_w_branch_a,
                         v_w_branch_b, v_w_out, v_g_ffn, v_w_gate_up, v_w_down, v_g_ple, v_w_ple_gate,
                         v_w_ple_proj, v_g_final]))
    xi, yi, ci = _place()
    chip = 2 * xi + yi
    big_names = [n for n, _, _ in BIG]

    shards2d = {n: w[n].reshape(w[n].shape[-2:]) for n in big_names}
    wts = unpack_full(allgather_weights(pack_shards(shards2d, BF16)))
    conv_sh = jnp.where(ci == 0, w["conv_w"].reshape(CONV_K, CONV_CH // N_CHIPS), 0.0)
    conv_slots = lax.dynamic_update_slice(jnp.zeros((N_CHIPS, CONV_K, CONV_CH // N_CHIPS), F32), conv_sh[None],
                                          (chip, 0, 0))
    conv_all = small_allreduce(conv_slots.reshape(-1, 128), "gather_conv_w")
    conv_full = jnp.transpose(conv_all.reshape(N_CHIPS, CONV_K, CONV_CH // N_CHIPS), (1, 0, 2)).reshape(CONV_K, CONV_CH)
    small = {n: w[n] for n in names if n not in big_names}
    small["conv_w"] = conv_full

    grad_x, grads, small_grads = local_step(x, p[0], loss_target, wts, small)

    gshard = unpack_shard(reduce_scatter_grads(pack_grads(grads)))
    small_names = [n for n, _ in SMALL]
    red = unpack_small(small_allreduce(pack_small(small_grads, small_names, 112), "allreduce_small"), SMALL)
    loss = red["loss"][0]
    conv_g = lax.dynamic_slice(red["conv_w"].reshape(CONV_K, N_CHIPS, CONV_CH // N_CHIPS), (0, chip, 0),
                               (CONV_K, 1, CONV_CH // N_CHIPS))
    gsmall = {n: red[n].reshape(w[n].shape) for n in small_names if n not in ("loss", "conv_w")}
    gsmall["conv_w"] = conv_g.reshape(w["conv_w"].shape)

    grad, delta, new_m, new_v = {}, {}, {}, {}
    for n in big_names:
        shp = w[n].shape
        d_, m_, v_ = adamw(shards2d[n], gshard[n], m[n].reshape(shp[-2:]), v[n].reshape(shp[-2:]), "adamw_" + n)
        grad[n], delta[n], new_m[n], new_v[n] = gshard[n].reshape(shp), d_.reshape(shp), m_.reshape(shp), v_.reshape(shp)
    snames = [n for n in small_names if n != "loss"]
    ssizes = [(n, w[n].size) for n in snames]
    pk = lambda d: pack_small(d, snames, 64)
    d_, m_, v_ = adamw(pk(w), pk(gsmall), pk(m), pk(v), "adamw_small")
    ds, ms, vs = unpack_small(d_, ssizes), unpack_small(m_, ssizes), unpack_small(v_, ssizes)
    for n in snames:
        shp = w[n].shape
        grad[n], delta[n], new_m[n], new_v[n] = gsmall[n], ds[n].reshape(shp), ms[n].reshape(shp), vs[n].reshape(shp)

    return (loss, grad_x, *[grad[n] for n in names], *[delta[n] for n in names],
            *[new_m[n] for n in names], *[new_v[n] for n in names])
```

```python
import functools

import jax
import jax.numpy as jnp
from jax import lax
from jax.experimental import pallas as pl
from jax.experimental.pallas import tpu as pltpu

F32 = jnp.float32
BF16 = jnp.bfloat16
HI = lax.Precision.HIGHEST
MESH = pl.DeviceIdType.MESH

D_MODEL = 1024
CHUNK = 64
PLE_DIM = 256
EPS = 1e-6
A_HEADS = 8
A_HEAD_DIM = 64
A_WIDTH = 512
A_LOOKBACK = 8
BAND = (A_LOOKBACK + 1) * CHUNK
TAIL = 3 * CHUNK
REL_CLIP = 128
N_REL = 2 * REL_CLIP + 1
B_HEADS = 4
B_DIM = 128
B_WIDTH = 512
CONV_K = 4
CONV_CH = 1536
D_FF = 2816
SPLIT_Z = 3584
D_IN = 5640
ADAM_LR, ADAM_B1, ADAM_B2, ADAM_EPS, ADAM_WD, ADAM_STEP = 0.001, 0.9, 0.999, 1e-08, 0.01, 10

P_GATES, P_QA, P_KA, P_VA, P_CONV, P_Z, P_BD, P_WIDTH = 0, 2048, 2560, 3072, 3584, 5120, 5632, 5760

VMEM_LIMIT = 56 * 1024 * 1024


def _cp(sem, vmem=None, **kw):
    return pltpu.CompilerParams(dimension_semantics=sem, vmem_limit_bytes=vmem, **kw)


def _tile(n, cap):
    best = None
    for t in range(128, cap + 1, 128):
        if n % t == 0:
            best = t
    assert best is not None, (n, cap)
    return best


def _nn(a, b, prec=None):
    return lax.dot_general(a, b, (((1,), (0,)), ((), ())), preferred_element_type=F32, precision=prec)


def _nt(a, b, prec=None):
    return lax.dot_general(a, b, (((1,), (1,)), ((), ())), preferred_element_type=F32, precision=prec)


def _tn(a, b, prec=None):
    return lax.dot_general(a, b, (((0,), (0,)), ((), ())), preferred_element_type=F32, precision=prec)


def _bnn(a, b, prec=None):
    return lax.dot_general(a, b, (((2,), (1,)), ((0,), (0,))), preferred_element_type=F32, precision=prec)


def _bnt(a, b, prec=None):
    return lax.dot_general(a, b, (((2,), (2,)), ((0,), (0,))), preferred_element_type=F32, precision=prec)


def _bf(a):
    return a.astype(BF16)


def _split(a):
    hi = a.astype(BF16)
    return hi, (a - hi.astype(F32)).astype(BF16)


def _bnn3(a, b):
    ah, al = a if isinstance(a, tuple) else _split(a)
    bh, bl = b if isinstance(b, tuple) else _split(b)
    return _bnn(ah, bh) + (_bnn(ah, bl) + _bnn(al, bh))


def _sigmoid(x):
    return 1.0 / (1.0 + jnp.exp(-x))


def _softplus(x):
    return jnp.maximum(x, 0.0) + jnp.log(1.0 + jnp.exp(-jnp.abs(x)))


def rms_matmul(x, g, w, name, tm=512, tn_cap=1024):
    t, d = x.shape
    n = w.shape[1]
    tm = min(tm, t)
    tn = _tile(n, tn_cap)

    def body(x_ref, g_ref, w_ref, o_ref, h_ref):
        @pl.when(pl.program_id(1) == 0)
        def _():
            xv = x_ref[...]
            r = lax.rsqrt(jnp.mean(xv * xv, axis=-1, keepdims=True) + EPS)
            h_ref[...] = _bf(xv * r * g_ref[...])

        o_ref[...] = _nn(h_ref[...], w_ref[...])

    return pl.pallas_call(
        body, name=name, grid=(t // tm, n // tn),
        in_specs=[pl.BlockSpec((tm, d), lambda i, j: (i, 0)),
                  pl.BlockSpec((1, d), lambda i, j: (0, 0)),
                  pl.BlockSpec((d, tn), lambda i, j: (0, j))],
        out_specs=[pl.BlockSpec((tm, tn), lambda i, j: (i, j)),
                   pl.BlockSpec((tm, d), lambda i, j: (i, 0))],
        out_shape=[jax.ShapeDtypeStruct((t, n), F32), jax.ShapeDtypeStruct((t, d), BF16)],
        compiler_params=_cp(("parallel", "arbitrary"), VMEM_LIMIT),
    )(x, g, w)


def matmul_tn(a, b, name, into=None, col0=0, width=None, tm=512, tk_cap=1408, tn_cap=1408):
    m, k1 = a.shape
    n = b.shape[1]
    tm = min(tm, m)
    tk = _tile(k1, tk_cap)
    tn = _tile(n, tn_cap)
    while col0 % tn:
        tn = _tile(n, tn - 128)
    nk = m // tm
    c0 = col0 // tn

    def body(*refs):
        a_ref, b_ref, o_ref, acc = refs[0], refs[1], refs[-2], refs[-1]

        @pl.when(pl.program_id(2) == 0)
        def _():
            acc[...] = jnp.zeros_like(acc)

        acc[...] += _tn(_bf(a_ref[...]), _bf(b_ref[...]))

        @pl.when(pl.program_id(2) == nk - 1)
        def _():
            o_ref[...] = _bf(acc[...])

    in_specs = [pl.BlockSpec((tm, tk), lambda i, j, k: (k, i)),
                pl.BlockSpec((tm, tn), lambda i, j, k: (k, j))]
    args = [a, b]
    total = n if width is None else width
    aliases = {}
    if into is not None:
        in_specs.append(ANY)
        args.append(into)
        total = into.shape[1]
        aliases = {2: 0}
    return pl.pallas_call(
        body, name=name, grid=(k1 // tk, n // tn, nk),
        in_specs=in_specs,
        out_specs=pl.BlockSpec((tk, tn), lambda i, j, k: (i, c0 + j)),
        out_shape=jax.ShapeDtypeStruct((k1, total), BF16),
        scratch_shapes=[pltpu.VMEM((tk, tn), F32)],
        input_output_aliases=aliases,
        compiler_params=_cp(("parallel", "parallel", "arbitrary"), VMEM_LIMIT),
    )(*args)


def _tail_onehot(qi):
    r = lax.broadcasted_iota(jnp.int32, (384, TAIL), 0)
    kj = lax.broadcasted_iota(jnp.int32, (384, TAIL), 1)
    return (r == jnp.minimum(REL_CLIP + qi - kj, REL_CLIP) + REL_CLIP).astype(F32)


def bias_tail(rel_pad):
    def body(rb_ref, o_ref):
        rb = rb_ref[...]
        for qi in range(CHUNK):
            o_ref[qi] = _nn(rb, _tail_onehot(qi), HI)

    return pl.pallas_call(
        body, name="bias_tail",
        out_shape=jax.ShapeDtypeStruct((CHUNK, A_HEADS, TAIL), F32),
    )(rel_pad)


def bias_grad(db_t, db_far):
    def body(t_ref, f_ref, o_ref):
        acc = jnp.zeros((A_HEADS, 384), F32)
        for qi in range(CHUNK):
            acc = acc + _nt(t_ref[qi], _tail_onehot(qi), HI)
        far = jnp.sum(jnp.sum(f_ref[...], axis=2), axis=1, keepdims=True)
        lane = lax.broadcasted_iota(jnp.int32, (A_HEADS, 384), 1)
        o_ref[...] = acc + jnp.where(lane == 2 * REL_CLIP, far, 0.0)

    return pl.pallas_call(
        body, name="bias_grad",
        out_shape=jax.ShapeDtypeStruct((A_HEADS, 384), F32),
    )(db_t, db_far)


ATT_CB = 8


def _stack_heads(a, lane):
    return jnp.concatenate([jnp.where(lane < 64, a, 0.0), jnp.where(lane >= 64, a, 0.0)], axis=0)


def _fill_band_pads(k_ref, v_ref, kp, vp, s):
    z = jnp.zeros((A_LOOKBACK * CHUNK, 128), BF16)
    kp[pl.ds(0, A_LOOKBACK * CHUNK), :] = z
    vp[pl.ds(0, A_LOOKBACK * CHUNK), :] = z
    kp[pl.ds(A_LOOKBACK * CHUNK, s), :] = _bf(k_ref[...])
    vp[pl.ds(A_LOOKBACK * CHUNK, s), :] = _bf(v_ref[...])


def attn_fwd(proj, bias_band, b, s):
    t = b * s
    nc = s // CHUNK
    qb, kb_, vb_ = P_QA // 128, P_KA // 128, P_VA // 128

    nstep = nc // ATT_CB
    rows = ATT_CB * CHUNK

    def body(q_ref, k_ref, v_ref, b_ref, o_ref, lse_ref, kp, vp):
        n0 = pl.program_id(2) * ATT_CB

        @pl.when(n0 == 0)
        def _():
            _fill_band_pads(k_ref, v_ref, kp, vp, s)

        lane = lax.broadcasted_iota(jnp.int32, (CHUNK, 128), 1)
        col = lax.broadcasted_iota(jnp.int32, (2 * CHUNK, BAND), 1)
        bias2 = b_ref[...]

        def chunk(cc):
            n = n0 + cc
            r0 = pl.multiple_of(cc * CHUNK, CHUNK)
            start = pl.multiple_of(n * CHUNK, CHUNK)
            kb = kp[pl.ds(start, BAND), :]
            vb = vp[pl.ds(start, BAND), :]
            q2 = _stack_heads(q_ref[pl.ds(r0, CHUNK), :] * (A_HEAD_DIM ** -0.5), lane)
            sc = jnp.where(col >= (A_LOOKBACK - n) * CHUNK, _nt(_bf(q2), kb) + bias2, -1e30)
            mx = jnp.max(sc, axis=1, keepdims=True)
            p = jnp.exp(sc - mx)
            l = jnp.sum(p, axis=1, keepdims=True)
            o2 = _nn(_bf(p), vb) / l
            lse2 = mx + jnp.log(l)
            o_ref[pl.ds(r0, CHUNK), :] = jnp.where(lane < 64, o2[:CHUNK], o2[CHUNK:])
            lse_ref[pl.ds(r0, CHUNK), :] = jnp.where(lane < 64, lse2[:CHUNK], lse2[CHUNK:])

        def pair(i, carry):
            chunk(2 * i)
            chunk(2 * i + 1)
            return carry

        lax.fori_loop(0, ATT_CB // 2, pair, 0)

    return pl.pallas_call(
        body, name="attn_fwd", grid=(b, 4, nstep),
        in_specs=[pl.BlockSpec((rows, 128), lambda bb, m, n: (bb * nstep + n, qb + m)),
                  pl.BlockSpec((s, 128), lambda bb, m, n: (bb, kb_ + m)),
                  pl.BlockSpec((s, 128), lambda bb, m, n: (bb, vb_ + m)),
                  pl.BlockSpec((None, 2 * CHUNK, BAND), lambda bb, m, n: (m, 0, 0))],
        out_specs=[pl.BlockSpec((rows, 128), lambda bb, m, n: (bb * nstep + n, m)),
                   pl.BlockSpec((rows, 128), lambda bb, m, n: (bb * nstep + n, m))],
        out_shape=[jax.ShapeDtypeStruct((t, A_WIDTH), F32), jax.ShapeDtypeStruct((t, A_WIDTH), F32)],
        scratch_shapes=[pltpu.VMEM((s + A_LOOKBACK * CHUNK, 128), BF16),
                        pltpu.VMEM((s + A_LOOKBACK * CHUNK, 128), BF16)],
        compiler_params=_cp(("parallel", "parallel", "arbitrary"), VMEM_LIMIT),
    )(proj, proj, proj, bias_band)


def attn_bwd(proj, bias_band, y_a, lse, dy_a, b, s):
    t = b * s
    nc = s // CHUNK
    qb, kb_, vb_ = P_QA // 128, P_KA // 128, P_VA // 128
    pad = A_LOOKBACK * CHUNK
    nstep = nc // ATT_CB
    rows = ATT_CB * CHUNK

    def body(q_ref, k_ref, v_ref, b_ref, do_ref, o_ref, lse_ref,
             dq_ref, dk_ref, dv_ref, dbt_ref, dbf_ref, kp, vp, dkp, dvp):
        bb = pl.program_id(1)
        n0 = pl.program_id(2) * ATT_CB

        @pl.when(n0 == 0)
        def _():
            _fill_band_pads(k_ref, v_ref, kp, vp, s)
            dkp[...] = jnp.zeros_like(dkp)
            dvp[...] = jnp.zeros_like(dvp)

        @pl.when((n0 == 0) & (bb == 0))
        def _():
            dbt_ref[...] = jnp.zeros_like(dbt_ref)
            dbf_ref[...] = jnp.zeros_like(dbf_ref)

        lane = lax.broadcasted_iota(jnp.int32, (CHUNK, 128), 1)
        col = lax.broadcasted_iota(jnp.int32, (2 * CHUNK, BAND), 1)
        bias2 = b_ref[...]

        def chunk(cc):
            n = n0 + cc
            r0 = pl.multiple_of(cc * CHUNK, CHUNK)
            start = pl.multiple_of(n * CHUNK, CHUNK)
            kb = kp[pl.ds(start, BAND), :]
            vb = vp[pl.ds(start, BAND), :]
            q2b = _bf(_stack_heads(q_ref[pl.ds(r0, CHUNK), :] * (A_HEAD_DIM ** -0.5), lane))
            do2 = _stack_heads(do_ref[pl.ds(r0, CHUNK), :], lane)
            do2b = _bf(do2)
            o = o_ref[pl.ds(r0, CHUNK), :]
            lsev = lse_ref[pl.ds(r0, CHUNK), :]
            lse2 = jnp.concatenate([lsev[:, 0:1], lsev[:, 64:65]], axis=0)
            sc = jnp.where(col >= (A_LOOKBACK - n) * CHUNK, _nt(q2b, kb) + bias2, -1e30)
            p = jnp.exp(sc - lse2)
            dp = _nt(do2b, vb)
            delta = jnp.sum(do2 * jnp.concatenate([o, o], axis=0), axis=1, keepdims=True)
            ds = p * (dp - delta)
            dsb = _bf(ds)
            dq2 = _nn(dsb, kb)
            dq_ref[pl.ds(r0, CHUNK), :] = jnp.where(lane < 64, dq2[:CHUNK], dq2[CHUNK:]) * (A_HEAD_DIM ** -0.5)
            dkp[pl.ds(start, BAND), :] += _tn(dsb, q2b)
            dvp[pl.ds(start, BAND), :] += _tn(_bf(p), do2b)
            dbt_ref[...] += ds[:, BAND - TAIL:]
            dbf_ref[...] += ds[:, 0:128] + ds[:, 128:256] + ds[:, 256:384]

        def step(i, carry):
            chunk(i)
            return carry

        lax.fori_loop(0, ATT_CB, step, 0)

        @pl.when(n0 == nc - ATT_CB)
        def _():
            dk_ref[...] = dkp[pl.ds(pad, s), :]
            dv_ref[...] = dvp[pl.ds(pad, s), :]

    return pl.pallas_call(
        body, name="attn_bwd", grid=(4, b, nstep),
        in_specs=[pl.BlockSpec((rows, 128), lambda m, bb, n: (bb * nstep + n, qb + m)),
                  pl.BlockSpec((s, 128), lambda m, bb, n: (bb, kb_ + m)),
                  pl.BlockSpec((s, 128), lambda m, bb, n: (bb, vb_ + m)),
                  pl.BlockSpec((None, 2 * CHUNK, BAND), lambda m, bb, n: (m, 0, 0)),
                  pl.BlockSpec((rows, 128), lambda m, bb, n: (bb * nstep + n, m)),
                  pl.BlockSpec((rows, 128), lambda m, bb, n: (bb * nstep + n, m)),
                  pl.BlockSpec((rows, 128), lambda m, bb, n: (bb * nstep + n, m))],
        out_specs=[pl.BlockSpec((rows, 128), lambda m, bb, n: (bb * nstep + n, m)),
                   pl.BlockSpec((s, 128), lambda m, bb, n: (bb, m)),
                   pl.BlockSpec((s, 128), lambda m, bb, n: (bb, m)),
                   pl.BlockSpec((None, 2 * CHUNK, TAIL), lambda m, bb, n: (m, 0, 0)),
                   pl.BlockSpec((None, 2 * CHUNK, 128), lambda m, bb, n: (m, 0, 0))],
        out_shape=[jax.ShapeDtypeStruct((t, A_WIDTH), F32)] * 3
        + [jax.ShapeDtypeStruct((4, 2 * CHUNK, TAIL), F32),
           jax.ShapeDtypeStruct((4, 2 * CHUNK, 128), F32)],
        scratch_shapes=[pltpu.VMEM((s + pad, 128), BF16), pltpu.VMEM((s + pad, 128), BF16),
                        pltpu.VMEM((s + pad, 128), F32), pltpu.VMEM((s + pad, 128), F32)],
        compiler_params=_cp(("parallel", "arbitrary", "arbitrary"), VMEM_LIMIT),
    )(proj, proj, proj, bias_band, dy_a, y_a, lse)


def _conv_taps(x, w, s):
    row = lax.broadcasted_iota(jnp.int32, x.shape, 0)
    shifted = [x] + [jnp.where(row >= i, pltpu.roll(x, i, 0), 0.0) for i in range(1, CONV_K)]
    acc = shifted[0] * w[CONV_K - 1:CONV_K, :]
    for i in range(1, CONV_K):
        acc = acc + shifted[i] * w[CONV_K - 1 - i:CONV_K - i, :]
    return acc, shifted


def conv_fwd(proj, conv_w8, b, s):
    cb = 512
    c0 = P_CONV // cb

    def body(x_ref, w_ref, o_ref):
        a, _ = _conv_taps(x_ref[...], w_ref[...], s)
        o_ref[...] = a * _sigmoid(a)

    return pl.pallas_call(
        body, name="conv_fwd", grid=(b, CONV_CH // cb),
        in_specs=[pl.BlockSpec((s, cb), lambda bb, j: (bb, c0 + j)),
                  pl.BlockSpec((8, cb), lambda bb, j: (0, j))],
        out_specs=pl.BlockSpec((s, cb), lambda bb, j: (bb, j)),
        out_shape=jax.ShapeDtypeStruct((b * s, CONV_CH), F32),
        compiler_params=_cp(("parallel", "parallel"), VMEM_LIMIT),
    )(proj, conv_w8)


def conv_bwd(proj, conv_w8, dc3, b, s):
    cb = 512
    c0 = P_CONV // cb

    def body(x_ref, w_ref, dc_ref, dx_ref, dw_ref):
        @pl.when(pl.program_id(1) == 0)
        def _():
            dw_ref[...] = jnp.zeros_like(dw_ref)

        w = w_ref[...]
        a, shifted = _conv_taps(x_ref[...], w, s)
        sg = _sigmoid(a)
        da = dc_ref[...] * (sg * (1.0 + a * (1.0 - sg)))
        row = lax.broadcasted_iota(jnp.int32, da.shape, 0)
        dx = da * w[CONV_K - 1:CONV_K, :]
        for i in range(1, CONV_K):
            dx = dx + jnp.where(row < s - i, pltpu.roll(da, s - i, 0), 0.0) * w[CONV_K - 1 - i:CONV_K - i, :]
        dx_ref[...] = dx
        r8 = lax.broadcasted_iota(jnp.int32, (8, cb), 0)
        dw = jnp.zeros((8, cb), F32)
        for i in range(CONV_K):
            dw = dw + jnp.where(r8 == CONV_K - 1 - i, jnp.sum(da * shifted[i], axis=0, keepdims=True), 0.0)
        dw_ref[...] += dw

    return pl.pallas_call(
        body, name="conv_bwd", grid=(CONV_CH // cb, b),
        in_specs=[pl.BlockSpec((s, cb), lambda j, bb: (bb, c0 + j)),
                  pl.BlockSpec((8, cb), lambda j, bb: (0, j)),
                  pl.BlockSpec((None, s, cb), lambda j, bb: (j, bb, 0))],
        out_specs=[pl.BlockSpec((s, cb), lambda j, bb: (bb, j)),
                   pl.BlockSpec((8, cb), lambda j, bb: (0, j))],
        out_shape=[jax.ShapeDtypeStruct((b * s, CONV_CH), F32), jax.ShapeDtypeStruct((8, CONV_CH), F32)],
        compiler_params=_cp(("parallel", "arbitrary"), VMEM_LIMIT),
    )(proj, conv_w8, dc3)


def _pick_lane(v, k):
    lane = lax.broadcasted_iota(jnp.int32, v.shape, 1)
    return jnp.sum(jnp.where(lane == k, v, 0.0), axis=1, keepdims=True)


def _chunk_masks(ncb):
    i = lax.broadcasted_iota(jnp.int32, (ncb, CHUNK, CHUNK), 1)
    j = lax.broadcasted_iota(jnp.int32, (ncb, CHUNK, CHUNK), 2)
    return i, j


def _col_of_row(rowvec, eye):
    return jnp.sum(jnp.where(eye, rowvec, 0.0), axis=2, keepdims=True)


def _dn_chunk_math(cq, ck, cv, bd, al_row, dtb_row, h, ncb):
    r = ncb * CHUNK
    i, j = _chunk_masks(ncb)
    eye = i == j
    low = i >= j
    strict = i > j
    ones = jnp.ones((ncb, CHUNK, CHUNK), F32)

    braw = _pick_lane(bd, h)
    draw = _pick_lane(bd, B_HEADS + h)
    al = _pick_lane(al_row, h)
    dtb = _pick_lane(dtb_row, h)
    ea = jnp.exp(al)
    beta = _sigmoid(braw)
    sp_arg = draw + dtb
    g = -ea * _softplus(sp_arg)

    rq = lax.rsqrt(jnp.sum(cq * cq, axis=1, keepdims=True) + EPS)
    rk = lax.rsqrt(jnp.sum(ck * ck, axis=1, keepdims=True) + EPS)
    nq = cq * rq
    kn = ck * rk
    qn = nq * (B_DIM ** -0.5)

    def c3(a):
        return a.reshape(ncb, CHUNK, a.shape[-1])

    qn3, kn3, v3, beta3 = c3(qn), c3(kn), c3(cv), c3(beta)
    gb = jnp.broadcast_to(c3(g), (ncb, CHUNK, CHUNK))
    gc_b = _bnn(low.astype(F32), gb, HI)
    gr_b = _bnn(ones, jnp.where(eye, gc_b, 0.0), HI)
    dm = jnp.where(low, jnp.exp(jnp.where(low, gc_b - gr_b, 0.0)), 0.0)
    gc = gc_b[:, :, 0:1]
    gl = gc_b[:, CHUNK - 1:CHUNK, 0:1]
    gam = jnp.exp(gc)
    egl = jnp.exp(gl)
    edec = jnp.exp(gl - gc)

    knb = _bf(kn3)
    kk = _bnt(knb, knb)
    kd = jnp.where(strict, kk * dm, 0.0)
    a = beta3 * kd
    tm = eye.astype(F32)
    sz = 1
    while sz < CHUNK:
        off = jnp.where(((i // (2 * sz)) == (j // (2 * sz))) & ((i // sz) != (j // sz)), a, 0.0)
        tms = _split(tm)
        tm = tm - _bnn3(_bnn3(tms, off), tms)
        sz *= 2
    bv = beta3 * v3
    bk = (beta3 * gam) * kn3
    tms = _split(tm)
    u = _bnn3(tms, bv)
    wk = _bnn3(tms, bk)
    qk = _bnt(_bf(qn3), knb)
    p = jnp.where(low, qk * dm, 0.0)
    kdec = kn3 * edec
    qg = gam * qn3
    return dict(beta=beta3, g=c3(g), ea=ea, sp_arg=c3(sp_arg), rq=c3(rq), rk=c3(rk), nq=c3(nq),
                qn=qn3, kn=kn3, v=v3, gc=gc, gl=gl, gam=gam, egl=egl, edec=edec, dm=dm, kd=kd, a=a,
                tm=tm, u=u, wk=wk, qk=qk, p=p, kdec=kdec, qg=qg, eye=eye, low=low, strict=strict)


def dn_prep(c, proj, al_row, dtb_row, b, s, ncb=8):
    t = b * s
    r = ncb * CHUNK
    nblk = t // r
    bd_blk = P_BD // 128

    def body(cq_ref, ck_ref, cv_ref, bd_ref, al_ref, dtb_ref, u_ref, wk_ref, qg_ref, kdec_ref, p_ref, egl_ref):
        h = pl.program_id(1)
        m = _dn_chunk_math(cq_ref[...], ck_ref[...], cv_ref[...], bd_ref[...], al_ref[...], dtb_ref[...], h, ncb)
        u_ref[...] = m["u"].reshape(r, B_DIM)
        wk_ref[...] = m["wk"].reshape(r, B_DIM)
        qg_ref[...] = m["qg"].reshape(r, B_DIM)
        kdec_ref[...] = m["kdec"].reshape(r, B_DIM)
        p_ref[...] = m["p"].reshape(r, CHUNK)
        egl_ref[...] = jnp.broadcast_to(m["egl"], (ncb, 8, 128)).reshape(ncb * 8, 128)

    col = lambda k: pl.BlockSpec((r, 128), lambda i, h: (i, k * B_HEADS + h))
    out_col = pl.BlockSpec((r, 128), lambda i, h: (i, h))
    small = pl.BlockSpec((1, 128), lambda i, h: (0, 0))
    return pl.pallas_call(
        body, name="dn_prep", grid=(nblk, B_HEADS),
        in_specs=[col(0), col(1), col(2), pl.BlockSpec((r, 128), lambda i, h: (i, bd_blk)), small, small],
        out_specs=[out_col, out_col, out_col, out_col,
                   pl.BlockSpec((None, r, CHUNK), lambda i, h: (h, i, 0)),
                   pl.BlockSpec((None, ncb * 8, 128), lambda i, h: (h, i, 0))],
        out_shape=[jax.ShapeDtypeStruct((t, B_WIDTH), F32)] * 4
        + [jax.ShapeDtypeStruct((B_HEADS, t, CHUNK), F32),
           jax.ShapeDtypeStruct((B_HEADS, t // 8, 128), F32)],
        compiler_params=_cp(("parallel", "parallel"), VMEM_LIMIT),
    )(c, c, c, proj, al_row, dtb_row)


def dn_scan_fwd(u, wk, qg, kdec, p, egl, b, s):
    t = b * s
    nc = s // CHUNK
    row = lambda bb, n: (bb * nc + n, 0)

    def body(u_ref, wk_ref, qg_ref, kdec_ref, p_ref, egl_ref, o_ref, ss_ref, st):
        @pl.when(pl.program_id(1) == 0)
        def _():
            st[...] = jnp.zeros_like(st)

        for h in range(B_HEADS):
            sl = slice(h * B_DIM, (h + 1) * B_DIM)
            sh = st[h]
            ss_ref[h] = sh
            sb = _bf(sh)
            w = u_ref[:, sl] - _nt(_bf(wk_ref[:, sl]), sb)
            o_ref[:, sl] = _nt(_bf(qg_ref[:, sl]), sb) + _nn(_bf(p_ref[h]), _bf(w))
            st[h] = egl_ref[h][0:1, :] * sh + _tn(_bf(w), _bf(kdec_ref[:, sl]))

    act = pl.BlockSpec((CHUNK, B_WIDTH), row)
    return pl.pallas_call(
        body, name="dn_scan_fwd", grid=(b, nc),
        in_specs=[act, act, act, act,
                  pl.BlockSpec((B_HEADS, CHUNK, CHUNK), lambda bb, n: (0, bb * nc + n, 0)),
                  pl.BlockSpec((B_HEADS, 8, 128), lambda bb, n: (0, bb * nc + n, 0))],
        out_specs=[act, pl.BlockSpec((None, B_HEADS, B_DIM, B_DIM), lambda bb, n: (bb * nc + n, 0, 0, 0))],
        out_shape=[jax.ShapeDtypeStruct((t, B_WIDTH), F32),
                   jax.ShapeDtypeStruct((t // CHUNK, B_HEADS, B_DIM, B_DIM), F32)],
        scratch_shapes=[pltpu.VMEM((B_HEADS, B_DIM, B_DIM), F32)],
        compiler_params=_cp(("parallel", "arbitrary"), VMEM_LIMIT),
    )(u, wk, qg, kdec, p, egl)


def dn_scan_bwd(u, wk, qg, kdec, p, egl, states, do, b, s):
    t = b * s
    nc = s // CHUNK
    row = lambda bb, n: (bb * nc + nc - 1 - n, 0)
    row3 = lambda bb, n: (0, bb * nc + nc - 1 - n, 0)

    def body(u_ref, wk_ref, qg_ref, kdec_ref, p_ref, egl_ref, ss_ref, do_ref,
             dw_ref, dwk_ref, dqg_ref, dkdec_ref, dp_ref, degl_ref, dst):
        @pl.when(pl.program_id(1) == 0)
        def _():
            dst[...] = jnp.zeros_like(dst)

        for h in range(B_HEADS):
            sl = slice(h * B_DIM, (h + 1) * B_DIM)
            sh = ss_ref[h]
            sb = _bf(sh)
            dsp = dst[h]
            dsb = _bf(dsp)
            wkb = _bf(wk_ref[:, sl])
            kdb = _bf(kdec_ref[:, sl])
            pb = _bf(p_ref[h])
            dob = _bf(do_ref[:, sl])
            w = u_ref[:, sl] - _nt(wkb, sb)
            wb = _bf(w)
            dw = _tn(pb, dob) + _nt(kdb, dsb)
            dwb = _bf(dw)
            dw_ref[:, sl] = dw
            dqg_ref[:, sl] = _nn(dob, sb)
            dwk_ref[:, sl] = -_nn(dwb, sb)
            dkdec_ref[:, sl] = _nn(wb, dsb)
            dp_ref[h] = _nt(dob, wb)
            tot = jnp.sum(jnp.sum(sh * dsp, axis=1, keepdims=True), axis=0, keepdims=True)
            degl_ref[h] = jnp.broadcast_to(tot, (8, 128))
            dst[h] = egl_ref[h][0:1, :] * dsp + _tn(dob, _bf(qg_ref[:, sl])) - _tn(dwb, wkb)

    act = pl.BlockSpec((CHUNK, B_WIDTH), row)
    pspec = pl.BlockSpec((B_HEADS, CHUNK, CHUNK), row3)
    espec = pl.BlockSpec((B_HEADS, 8, 128), row3)
    return pl.pallas_call(
        body, name="dn_scan_bwd", grid=(b, nc),
        in_specs=[act, act, act, act, pspec, espec,
                  pl.BlockSpec((None, B_HEADS, B_DIM, B_DIM), lambda bb, n: (bb * nc + nc - 1 - n, 0, 0, 0)),
                  act],
        out_specs=[act, act, act, act, pspec, espec],
        out_shape=[jax.ShapeDtypeStruct((t, B_WIDTH), F32)] * 4
        + [jax.ShapeDtypeStruct((B_HEADS, t, CHUNK), F32),
           jax.ShapeDtypeStruct((B_HEADS, t // 8, 128), F32)],
        scratch_shapes=[pltpu.VMEM((B_HEADS, B_DIM, B_DIM), F32)],
        compiler_params=_cp(("parallel", "arbitrary"), VMEM_LIMIT),
    )(u, wk, qg, kdec, p, egl, states, do)


def dn_post_bwd(c, proj, al_row, dtb_row, dw, dwk, dqg, dkdec, dp, degl, b, s, ncb=8):
    t = b * s
    r = ncb * CHUNK
    nblk = t // r
    bd_blk = P_BD // 128

    def body(cq_ref, ck_ref, cv_ref, bd_ref, al_ref, dtb_ref, dw_ref, dwk_ref, dqg_ref, dkdec_ref, dp_ref,
             degl_ref, dc_ref, dbd_ref, dal_ref, ddtb_ref):
        h = pl.program_id(1)

        @pl.when((pl.program_id(0) == 0) & (h == 0))
        def _():
            dal_ref[...] = jnp.zeros_like(dal_ref)
            ddtb_ref[...] = jnp.zeros_like(ddtb_ref)

        m = _dn_chunk_math(cq_ref[...], ck_ref[...], cv_ref[...], bd_ref[...], al_ref[...], dtb_ref[...], h, ncb)
        eye, low, strict = m["eye"], m["low"], m["strict"]
        eyef = eye.astype(F32)

        def c3(a):
            return a.reshape(ncb, CHUNK, a.shape[-1])

        du, dwkv, dqg, dkdec = c3(dw_ref[...]), c3(dwk_ref[...]), c3(dqg_ref[...]), c3(dkdec_ref[...])
        dpm = jnp.where(low, c3(dp_ref[...]), 0.0)
        degl = degl_ref[...].reshape(ncb, 8, 128)[:, 0:1, 0:1]
        beta, gam, kn, qn, v = m["beta"], m["gam"], m["kn"], m["qn"], m["v"]
        dm, kd, a, p = m["dm"], m["kd"], m["a"], m["p"]
        knb, qnb = _bf(kn), _bf(qn)

        eyeb = _bf(eyef)
        th, tl = _split(m["tm"])
        tts = (_bf(_bnt(eyeb, th)), _bf(_bnt(eyeb, tl)))
        x = _bnn3(tts, du)
        y = _bnn3(tts, dwkv)
        da = -jnp.where(strict, _bnt(_bf(x), _bf(m["u"])) + _bnt(_bf(y), _bf(m["wk"])), 0.0)
        dv = beta * x
        sy = jnp.sum(y * kn, axis=2, keepdims=True)
        dbeta = jnp.sum(x * v, axis=2, keepdims=True) + gam * sy + jnp.sum(da * kd, axis=2, keepdims=True)
        dgam = beta * sy + jnp.sum(dqg * qn, axis=2, keepdims=True)
        dkk = da * beta * dm
        dqk = dpm * dm
        dkkb, dqkb = _bf(dkk), _bf(dqk)
        dkn = ((beta * gam) * y + _bnn(dkkb, knb) + _bnn(_bf(_bnt(eyeb, dkkb)), knb)
               + _bnn(_bf(_bnt(eyeb, dqkb)), qnb) + dkdec * m["edec"])
        dqn = gam * dqg + _bnn(dqkb, knb)
        mm = da * a + dpm * p
        ek = jnp.sum(dkdec * m["kdec"], axis=2, keepdims=True)
        dgc = (jnp.sum(mm, axis=2, keepdims=True) - _col_of_row(jnp.sum(mm, axis=1, keepdims=True), eye)
               + dgam * gam - ek)
        dgl = jnp.sum(ek, axis=1, keepdims=True) + degl * m["egl"]
        i, _ = _chunk_masks(ncb)
        dgc = dgc + jnp.where(i[:, :, 0:1] == CHUNK - 1, dgl, 0.0)
        upper = (i <= _chunk_masks(ncb)[1]).astype(F32)
        dg = _bnn(upper, jnp.broadcast_to(dgc, (ncb, CHUNK, CHUNK)), HI)[:, :, 0:1]

        nq = m["nq"]
        dnq = dqn * (B_DIM ** -0.5)
        dcq = m["rq"] * (dnq - nq * jnp.sum(nq * dnq, axis=2, keepdims=True))
        dck = m["rk"] * (dkn - kn * jnp.sum(kn * dkn, axis=2, keepdims=True))
        dc_ref[0] = dcq.reshape(r, B_DIM)
        dc_ref[1] = dck.reshape(r, B_DIM)
        dc_ref[2] = dv.reshape(r, B_DIM)

        dbraw = (dbeta * beta * (1.0 - beta)).reshape(r, 1)
        sgm = _sigmoid(m["sp_arg"])
        ddraw3 = dg * (-m["ea"]) * sgm
        ddraw = ddraw3.reshape(r, 1)
        lane = lax.broadcasted_iota(jnp.int32, (r, 128), 1)
        contrib = jnp.where(lane == h, dbraw, 0.0) + jnp.where(lane == B_HEADS + h, ddraw, 0.0)

        @pl.when(h == 0)
        def _():
            dbd_ref[...] = contrib

        @pl.when(h != 0)
        def _():
            dbd_ref[...] += contrib

        lane8 = lax.broadcasted_iota(jnp.int32, (8, 128), 1)
        tot_al = jnp.sum(jnp.sum(dg * m["g"], axis=1, keepdims=True), axis=0, keepdims=True).reshape(1, 1)
        tot_dtb = jnp.sum(jnp.sum(ddraw3, axis=1, keepdims=True), axis=0, keepdims=True).reshape(1, 1)
        dal_ref[...] += jnp.where(lane8 == h, tot_al, 0.0)
        ddtb_ref[...] += jnp.where(lane8 == h, tot_dtb, 0.0)

    col = lambda k: pl.BlockSpec((r, 128), lambda i, h: (i, k * B_HEADS + h))
    hcol = pl.BlockSpec((r, 128), lambda i, h: (i, h))
    small = pl.BlockSpec((1, 128), lambda i, h: (0, 0))
    acc = pl.BlockSpec((8, 128), lambda i, h: (0, 0))
    return pl.pallas_call(
        body, name="dn_post_bwd", grid=(nblk, B_HEADS),
        in_specs=[col(0), col(1), col(2), pl.BlockSpec((r, 128), lambda i, h: (i, bd_blk)), small, small,
                  hcol, hcol, hcol, hcol,
                  pl.BlockSpec((None, r, CHUNK), lambda i, h: (h, i, 0)),
                  pl.BlockSpec((None, ncb * 8, 128), lambda i, h: (h, i, 0))],
        out_specs=[pl.BlockSpec((3, r, 128), lambda i, h: (0, i, h)),
                   pl.BlockSpec((r, 128), lambda i, h: (i, 0)), acc, acc],
        out_shape=[jax.ShapeDtypeStruct((3, t, B_WIDTH), F32), jax.ShapeDtypeStruct((t, 128), F32),
                   jax.ShapeDtypeStruct((8, 128), F32), jax.ShapeDtypeStruct((8, 128), F32)],
        compiler_params=_cp(("arbitrary", "arbitrary"), VMEM_LIMIT),
    )(c, c, c, proj, al_row, dtb_row, dw, dwk, dqg, dkdec, dp, degl)


def make_bias_band(rel_bias):
    tail = bias_tail(jnp.pad(rel_bias, ((0, 0), (0, 384 - N_REL))))
    far = jnp.broadcast_to(rel_bias[:, 2 * REL_CLIP][:, None, None], (A_HEADS, CHUNK, BAND - TAIL))
    return jnp.concatenate([far, jnp.transpose(tail, (1, 0, 2))], axis=2).reshape(4, 2 * CHUNK, BAND)


def _rms(x):
    r = lax.rsqrt(jnp.mean(x * x, axis=-1, keepdims=True) + EPS)
    return r, x * r


def _rms_bwd(dh, g, r, n):
    dn = dh * g
    return r * (dn - n * jnp.mean(dn * n, axis=-1, keepdims=True)), dh * n


def _gated_onorm(o, z, w_on):
    parts = []
    for h in range(B_HEADS):
        sl = slice(h * B_DIM, (h + 1) * B_DIM)
        r, n = _rms(o[:, sl])
        parts.append((r, n))
    r4 = [p[0] for p in parts]
    n4 = jnp.concatenate([p[1] for p in parts], axis=1)
    w4 = jnp.concatenate([w_on] * B_HEADS, axis=1)
    sz = _sigmoid(z)
    silu = z * sz
    return n4 * w4 * silu, r4, n4, w4, sz, silu


def mid_fwd(x, y_a, o_b, proj, w_on, wa, wb, w_out, tm=256):
    t = x.shape[0]
    tm = min(tm, t)

    def body(x_ref, ya_ref, ob_ref, z_ref, ga_ref, gb_ref, won_ref, wa_ref, wb_ref, wo_ref, x1_ref, mg_ref):
        yb = _gated_onorm(ob_ref[...], z_ref[...], won_ref[...])[0]
        ua = _nn(_bf(ya_ref[...]), wa_ref[...])
        ub = _nn(_bf(yb), wb_ref[...])
        merged = _sigmoid(ga_ref[...]) * ua + _sigmoid(gb_ref[...]) * ub
        mb = _bf(merged)
        mg_ref[...] = mb
        x1_ref[...] = x_ref[...] + _nn(mb, wo_ref[...])

    rowd = pl.BlockSpec((tm, D_MODEL), lambda i: (i, 0))
    row5 = pl.BlockSpec((tm, 512), lambda i: (i, 0))
    full = lambda a: pl.BlockSpec(a.shape, lambda i: (0,) * a.ndim)
    return pl.pallas_call(
        body, name="mid_fwd", grid=(t // tm,),
        in_specs=[rowd, row5, row5,
                  pl.BlockSpec((tm, 512), lambda i: (i, P_Z // 512)),
                  pl.BlockSpec((tm, D_MODEL), lambda i: (i, 0)),
                  pl.BlockSpec((tm, D_MODEL), lambda i: (i, 1)),
                  full(w_on), full(wa), full(wb), full(w_out)],
        out_specs=[rowd, rowd],
        out_shape=[jax.ShapeDtypeStruct((t, D_MODEL), F32), jax.ShapeDtypeStruct((t, D_MODEL), BF16)],
        compiler_params=_cp(("parallel",), VMEM_LIMIT),
    )(x, y_a, o_b, proj, proj, proj, w_on, wa, wb, w_out)


def mid_bwd(dx1, merged, y_a, o_b, proj, w_on, wa, wb, w_out, tm=256):
    t = dx1.shape[0]
    tm = min(tm, t)

    def body(dx1_ref, mg_ref, ya_ref, ob_ref, z_ref, ga_ref, gb_ref, won_ref, wa_ref, wb_ref, wo_ref,
             dya_ref, dob_ref, dz_ref, dg_ref, dwo_ref, dwa_ref, dwb_ref, dwon_ref):
        @pl.when(pl.program_id(0) == 0)
        def _():
            dwo_ref[...] = jnp.zeros_like(dwo_ref)
            dwa_ref[...] = jnp.zeros_like(dwa_ref)
            dwb_ref[...] = jnp.zeros_like(dwb_ref)
            dwon_ref[...] = jnp.zeros_like(dwon_ref)

        dx1b = _bf(dx1_ref[...])
        dmerged = _nt(dx1b, wo_ref[...])
        dwo_ref[...] += _tn(mg_ref[...], dx1b)
        o = ob_ref[...]
        z = z_ref[...]
        yb, r4, n4, w4, sz, silu = _gated_onorm(o, z, won_ref[...])
        yab, ybb = _bf(ya_ref[...]), _bf(yb)
        ua = _nn(yab, wa_ref[...])
        ub = _nn(ybb, wb_ref[...])
        sa, sb = _sigmoid(ga_ref[...]), _sigmoid(gb_ref[...])
        dua, dub = _bf(dmerged * sa), _bf(dmerged * sb)
        dg_ref[:, 0:D_MODEL] = dmerged * ua * sa * (1.0 - sa)
        dg_ref[:, D_MODEL:2 * D_MODEL] = dmerged * ub * sb * (1.0 - sb)
        dwa_ref[...] += _tn(yab, dua)
        dwb_ref[...] += _tn(ybb, dub)
        dya_ref[...] = _nt(dua, wa_ref[...])
        dyb = _nt(dub, wb_ref[...])
        dz_ref[...] = dyb * (n4 * w4) * (sz * (1.0 + z * (1.0 - sz)))
        dnw = dyb * silu
        dwon = jnp.zeros((1, B_DIM), F32)
        for h in range(B_HEADS):
            sl = slice(h * B_DIM, (h + 1) * B_DIM)
            dxh, dgh = _rms_bwd(dnw[:, sl], won_ref[...], r4[h], n4[:, sl])
            dob_ref[:, sl] = dxh
            dwon = dwon + jnp.sum(dgh, axis=0, keepdims=True)
        dwon_ref[...] += jnp.broadcast_to(dwon, (8, B_DIM))

    rowd = pl.BlockSpec((tm, D_MODEL), lambda i: (i, 0))
    row5 = pl.BlockSpec((tm, 512), lambda i: (i, 0))
    full = lambda a: pl.BlockSpec(a.shape, lambda i: (0,) * a.ndim)
    fixed = lambda shp: pl.BlockSpec(shp, lambda i: (0,) * len(shp))
    return pl.pallas_call(
        body, name="mid_bwd", grid=(t // tm,),
        in_specs=[rowd, rowd, row5, row5,
                  pl.BlockSpec((tm, 512), lambda i: (i, P_Z // 512)),
                  pl.BlockSpec((tm, D_MODEL), lambda i: (i, 0)),
                  pl.BlockSpec((tm, D_MODEL), lambda i: (i, 1)),
                  full(w_on), full(wa), full(wb), full(w_out)],
        out_specs=[row5, row5, row5, pl.BlockSpec((tm, 2 * D_MODEL), lambda i: (i, 0)),
                   fixed((D_MODEL, D_MODEL)), fixed((A_WIDTH, D_MODEL)), fixed((B_WIDTH, D_MODEL)),
                   fixed((8, B_DIM))],
        out_shape=[jax.ShapeDtypeStruct((t, 512), F32)] * 3
        + [jax.ShapeDtypeStruct((t, 2 * D_MODEL), F32),
           jax.ShapeDtypeStruct((D_MODEL, D_MODEL), F32), jax.ShapeDtypeStruct((A_WIDTH, D_MODEL), F32),
           jax.ShapeDtypeStruct((B_WIDTH, D_MODEL), F32), jax.ShapeDtypeStruct((8, B_DIM), F32)],
        compiler_params=_cp(("arbitrary",), VMEM_LIMIT),
    )(dx1, merged, y_a, o_b, proj, proj, proj, w_on, wa, wb, w_out)


def ffn_fwd(x1, g, w_gu, w_down, tm=512, tf=256):
    t = x1.shape[0]
    tm = min(tm, t)
    nf = D_FF // tf

    def body(x_ref, g_ref, wg_ref, wu_ref, wd_ref, x2_ref, gate_ref, up_ref, h_ref):
        j = pl.program_id(1)

        @pl.when(j == 0)
        def _():
            xv = x_ref[...]
            r, n = _rms(xv)
            h_ref[...] = _bf(n * g_ref[...])
            x2_ref[...] = xv

        hb = h_ref[...]
        gate = _nn(hb, wg_ref[...])
        up = _nn(hb, wu_ref[...])
        gate_ref[...] = gate
        up_ref[...] = up
        act = gate * _sigmoid(gate) * up
        x2_ref[...] += _nn(_bf(act), wd_ref[...])

    return pl.pallas_call(
        body, name="ffn_fwd", grid=(t // tm, nf),
        in_specs=[pl.BlockSpec((tm, D_MODEL), lambda i, j: (i, 0)),
                  pl.BlockSpec((1, D_MODEL), lambda i, j: (0, 0)),
                  pl.BlockSpec((D_MODEL, tf), lambda i, j: (0, j)),
                  pl.BlockSpec((D_MODEL, tf), lambda i, j: (0, nf + j)),
                  pl.BlockSpec((tf, D_MODEL), lambda i, j: (j, 0))],
        out_specs=[pl.BlockSpec((tm, D_MODEL), lambda i, j: (i, 0)),
                   pl.BlockSpec((tm, tf), lambda i, j: (i, j)),
                   pl.BlockSpec((tm, tf), lambda i, j: (i, j)),
                   pl.BlockSpec((tm, D_MODEL), lambda i, j: (i, 0))],
        out_shape=[jax.ShapeDtypeStruct((t, D_MODEL), F32), jax.ShapeDtypeStruct((t, D_FF), F32),
                   jax.ShapeDtypeStruct((t, D_FF), F32), jax.ShapeDtypeStruct((t, D_MODEL), BF16)],
        compiler_params=_cp(("parallel", "arbitrary"), VMEM_LIMIT),
    )(x1, g, w_gu, w_gu, w_down)


def ffn_bwd(dx2, x1, g, gate, up, w_gu, w_down, tm=512, tf=256):
    t = x1.shape[0]
    tm = min(tm, t)
    nf = D_FF // tf

    def body(dx2_ref, x_ref, g_ref, gate_ref, up_ref, wg_ref, wu_ref, wd_ref,
             dx1_ref, act_ref, dgate_ref, dup_ref, dg_ref, dh_acc, dx2b_ref):
        i, j = pl.program_id(0), pl.program_id(1)

        @pl.when((i == 0) & (j == 0))
        def _():
            dg_ref[...] = jnp.zeros_like(dg_ref)

        @pl.when(j == 0)
        def _():
            dx2b_ref[...] = _bf(dx2_ref[...])
            dh_acc[...] = jnp.zeros_like(dh_acc)

        dact = _nt(dx2b_ref[...], wd_ref[...])
        gt, upv = gate_ref[...], up_ref[...]
        sg = _sigmoid(gt)
        silu = gt * sg
        act_ref[...] = _bf(silu * upv)
        dgt = _bf(dact * upv * (sg * (1.0 + gt * (1.0 - sg))))
        dupv = _bf(dact * silu)
        dgate_ref[...] = dgt
        dup_ref[...] = dupv
        dh_acc[...] += _nt(dgt, wg_ref[...]) + _nt(dupv, wu_ref[...])

        @pl.when(j == nf - 1)
        def _():
            r, n = _rms(x_ref[...])
            dx, dgc = _rms_bwd(dh_acc[...], g_ref[...], r, n)
            dx1_ref[...] = dx2_ref[...] + dx
            dg_ref[...] += jnp.broadcast_to(jnp.sum(dgc, axis=0, keepdims=True), (8, D_MODEL))

    rowd = pl.BlockSpec((tm, D_MODEL), lambda i, j: (i, 0))
    ff = pl.BlockSpec((tm, tf), lambda i, j: (i, j))
    return pl.pallas_call(
        body, name="ffn_bwd", grid=(t // tm, nf),
        in_specs=[rowd, rowd, pl.BlockSpec((1, D_MODEL), lambda i, j: (0, 0)), ff, ff,
                  pl.BlockSpec((D_MODEL, tf), lambda i, j: (0, j)),
                  pl.BlockSpec((D_MODEL, tf), lambda i, j: (0, nf + j)),
                  pl.BlockSpec((tf, D_MODEL), lambda i, j: (j, 0))],
        out_specs=[rowd, ff, ff, ff, pl.BlockSpec((8, D_MODEL), lambda i, j: (0, 0))],
        out_shape=[jax.ShapeDtypeStruct((t, D_MODEL), F32), jax.ShapeDtypeStruct((t, D_FF), BF16),
                   jax.ShapeDtypeStruct((t, D_FF), BF16), jax.ShapeDtypeStruct((t, D_FF), BF16),
                   jax.ShapeDtypeStruct((8, D_MODEL), F32)],
        scratch_shapes=[pltpu.VMEM((tm, D_MODEL), F32), pltpu.VMEM((tm, D_MODEL), BF16)],
        compiler_params=_cp(("arbitrary", "arbitrary"), VMEM_LIMIT),
    )(dx2, x1, g, gate, up, w_gu, w_gu, w_down)


def tail_fwd_bwd(x2, p, target, g_ple, g_final, w_pg, w_pp, tm=256):
    t = x2.shape[0]
    tm = min(tm, t)

    def body(x_ref, p_ref, t_ref, gp_ref, gf_ref, wpg_ref, wpp_ref,
             dx_ref, dwpg_ref, dwpp_ref, dgp_ref, dgf_ref, loss_ref):
        @pl.when(pl.program_id(0) == 0)
        def _():
            dwpg_ref[...] = jnp.zeros_like(dwpg_ref)
            dwpp_ref[...] = jnp.zeros_like(dwpp_ref)
            dgp_ref[...] = jnp.zeros_like(dgp_ref)
            dgf_ref[...] = jnp.zeros_like(dgf_ref)
            loss_ref[...] = jnp.zeros_like(loss_ref)

        x2v = x_ref[...]
        gp, gf = gp_ref[...], gf_ref[...]
        r3, n3 = _rms(x2v)
        h3b = _bf(n3 * gp)
        pb = _bf(p_ref[...])
        pg = _sigmoid(_nn(h3b, wpg_ref[...]))
        pp = _nn(pb, wpp_ref[...])
        x3 = x2v + pg * pp
        r4, n4 = _rms(x3)
        err = n4 * gf - t_ref[...]
        part = 0.5 * jnp.sum(jnp.sum(err * err, axis=1, keepdims=True), axis=0, keepdims=True) / D_MODEL
        loss_ref[...] += jnp.broadcast_to(part, (8, 128))
        dy = err * (1.0 / D_MODEL)
        dx3, dgf = _rms_bwd(dy, gf, r4, n4)
        dgf_ref[...] += jnp.broadcast_to(jnp.sum(dgf, axis=0, keepdims=True), (8, D_MODEL))
        dzp = _bf(dx3 * pp * pg * (1.0 - pg))
        dpp = _bf(dx3 * pg)
        dwpg_ref[...] += _tn(h3b, dzp)
        dwpp_ref[...] += _tn(pb, dpp)
        dh3 = _nt(dzp, wpg_ref[...])
        dx, dgp = _rms_bwd(dh3, gp, r3, n3)
        dgp_ref[...] += jnp.broadcast_to(jnp.sum(dgp, axis=0, keepdims=True), (8, D_MODEL))
        dx_ref[...] = dx3 + dx

    rowd = pl.BlockSpec((tm, D_MODEL), lambda i: (i, 0))
    fixed = lambda shp: pl.BlockSpec(shp, lambda i: (0,) * len(shp))
    return pl.pallas_call(
        body, name="tail_fwd_bwd", grid=(t // tm,),
        in_specs=[rowd, pl.BlockSpec((tm, PLE_DIM), lambda i: (i, 0)), rowd,
                  fixed((1, D_MODEL)), fixed((1, D_MODEL)), fixed((D_MODEL, D_MODEL)), fixed((PLE_DIM, D_MODEL))],
        out_specs=[rowd, fixed((D_MODEL, D_MODEL)), fixed((PLE_DIM, D_MODEL)),
                   fixed((8, D_MODEL)), fixed((8, D_MODEL)), fixed((8, 128))],
        out_shape=[jax.ShapeDtypeStruct((t, D_MODEL), F32), jax.ShapeDtypeStruct((D_MODEL, D_MODEL), F32),
                   jax.ShapeDtypeStruct((PLE_DIM, D_MODEL), F32), jax.ShapeDtypeStruct((8, D_MODEL), F32),
                   jax.ShapeDtypeStruct((8, D_MODEL), F32), jax.ShapeDtypeStruct((8, 128), F32)],
        compiler_params=_cp(("arbitrary",), VMEM_LIMIT),
    )(x2, p, target, g_ple, g_final, w_pg, w_pp)


def in_proj_bwd(pieces, weights, x, dx1, g, tm=256):
    t = x.shape[0]
    tm = min(tm, t)
    k = len(pieces)

    def body(*refs):
        p_refs, w_refs = refs[:k], refs[k:2 * k]
        x_ref, dx1_ref, g_ref, dx_ref, dg_ref = refs[2 * k:]

        @pl.when(pl.program_id(0) == 0)
        def _():
            dg_ref[...] = jnp.zeros_like(dg_ref)

        dh = _nt(_bf(p_refs[0][...]), w_refs[0][...])
        for pr, wr in zip(p_refs[1:], w_refs[1:]):
            dh = dh + _nt(_bf(pr[...]), wr[...])
        r, n = _rms(x_ref[...])
        dx, dgc = _rms_bwd(dh, g_ref[...], r, n)
        dx_ref[...] = dx1_ref[...] + dx
        dg_ref[...] += jnp.broadcast_to(jnp.sum(dgc, axis=0, keepdims=True), (8, D_MODEL))

    rowd = pl.BlockSpec((tm, D_MODEL), lambda i: (i, 0))
    return pl.pallas_call(
        body, name="in_proj_bwd", grid=(t // tm,),
        in_specs=[pl.BlockSpec((tm, a.shape[1]), lambda i: (i, 0)) for a in pieces]
        + [pl.BlockSpec(w.shape, lambda i: (0, 0)) for w in weights]
        + [rowd, rowd, pl.BlockSpec((1, D_MODEL), lambda i: (0, 0))],
        out_specs=[rowd, pl.BlockSpec((8, D_MODEL), lambda i: (0, 0))],
        out_shape=[jax.ShapeDtypeStruct((t, D_MODEL), F32), jax.ShapeDtypeStruct((8, D_MODEL), F32)],
        compiler_params=_cp(("arbitrary",), VMEM_LIMIT),
    )(*pieces, *weights, x, dx1, g)


def adamw(w, g, m, v, name, rows_cap=256):
    r, c = w.shape
    tr = r
    for cand in range(8, min(r, rows_cap) + 1, 8):
        if r % cand == 0:
            tr = cand

    def body(w_ref, g_ref, m_ref, v_ref, d_ref, mo_ref, vo_ref):
        gv = g_ref[...]
        mn = ADAM_B1 * m_ref[...] + (1.0 - ADAM_B1) * gv
        vn = ADAM_B2 * v_ref[...] + (1.0 - ADAM_B2) * (gv * gv)
        m_hat = mn / (1.0 - ADAM_B1 ** ADAM_STEP)
        v_hat = vn / (1.0 - ADAM_B2 ** ADAM_STEP)
        d_ref[...] = -ADAM_LR * (m_hat / (jnp.sqrt(v_hat) + ADAM_EPS) + ADAM_WD * w_ref[...])
        mo_ref[...] = mn
        vo_ref[...] = vn

    spec = pl.BlockSpec((tr, c), lambda i: (i, 0))
    return pl.pallas_call(
        body, name=name, grid=(r // tr,),
        in_specs=[spec] * 4, out_specs=[spec] * 3,
        out_shape=[jax.ShapeDtypeStruct((r, c), F32)] * 3,
        compiler_params=_cp(("parallel",), VMEM_LIMIT),
    )(w, g, m, v)


def local_step(x3d, p3d, target3d, wts, small):
    b, s, _ = x3d.shape
    t = b * s
    x = x3d.reshape(t, D_MODEL)
    p = p3d.reshape(t, PLE_DIM)
    target = target3d.reshape(t, D_MODEL)
    g4 = wts["w_in"]
    cut = SPLIT_Z - 2 * (D_IN // N_CHIPS)
    w_inp = jnp.concatenate([g4[2][:, cut + 8:], g4[3], g4[0], g4[1], g4[2][:, :cut], g4[2][:, cut:cut + 8],
                             jnp.zeros((D_MODEL, 120), BF16)], axis=1)
    al_row = jnp.pad(small["a_log"].reshape(1, B_HEADS), ((0, 0), (0, 128 - B_HEADS)))
    dtb_row = jnp.pad(small["dt_bias"].reshape(1, B_HEADS), ((0, 0), (0, 128 - B_HEADS)))
    conv_w8 = jnp.pad(small["conv_w"].reshape(CONV_K, CONV_CH), ((0, 8 - CONV_K), (0, 0)))
    w_on = small["w_onorm"].reshape(1, B_DIM)
    g_mix, g_ffn = small["g_mix"].reshape(1, D_MODEL), small["g_ffn"].reshape(1, D_MODEL)
    g_ple, g_final = small["g_ple"].reshape(1, D_MODEL), small["g_final"].reshape(1, D_MODEL)
    bias_band = make_bias_band(small["rel_bias"].reshape(A_HEADS, N_REL))

    proj, h1 = rms_matmul(x, g_mix, w_inp, "in_proj")
    y_a, lse = attn_fwd(proj, bias_band, b, s)
    c = conv_fwd(proj, conv_w8, b, s)
    u, wk, qg, kdec, pm, egl = dn_prep(c, proj, al_row, dtb_row, b, s)
    o_b, states = dn_scan_fwd(u, wk, qg, kdec, pm, egl, b, s)
    x1, merged = mid_fwd(x, y_a, o_b, proj, w_on, wts["w_branch_a"], wts["w_branch_b"], wts["w_out"])
    x2, gate, up, h2 = ffn_fwd(x1, g_ffn, wts["w_gate_up"], wts["w_down"])

    dx2, dw_pg, dw_pp, dg_ple, dg_final, loss = tail_fwd_bwd(
        x2, p, target, g_ple, g_final, wts["w_ple_gate"], wts["w_ple_proj"])
    dx1, act, dgate, dup, dg_ffn = ffn_bwd(dx2, x1, g_ffn, gate, up, wts["w_gate_up"], wts["w_down"])
    dw_down = matmul_tn(act, dx2, "dw_down")
    dw_gu = matmul_tn(h2, dgate, "dw_gate", width=2 * D_FF)
    dw_gu = matmul_tn(h2, dup, "dw_up", into=dw_gu, col0=D_FF)
    dy_a, do_b, dz, dgates, dw_out, dwa, dwb, dw_on = mid_bwd(
        dx1, merged, y_a, o_b, proj, w_on, wts["w_branch_a"], wts["w_branch_b"], wts["w_out"])
    ddw, ddwk, ddqg, ddkdec, ddp, ddegl = dn_scan_bwd(u, wk, qg, kdec, pm, egl, states, do_b, b, s)
    dc3, dbd, dal, ddtb = dn_post_bwd(c, proj, al_row, dtb_row, ddw, ddwk, ddqg, ddkdec, ddp, ddegl, b, s)
    dconv, dconv_w = conv_bwd(proj, conv_w8, dc3, b, s)
    dqa, dka, dva, dbt, dbf = attn_bwd(proj, bias_band, y_a, lse, dy_a, b, s)
    d_rel = bias_grad(jnp.transpose(dbt.reshape(A_HEADS, CHUNK, TAIL), (1, 0, 2)),
                      dbf.reshape(A_HEADS, CHUNK, 128))[:, :N_REL]

    pieces = [dgates, dqa, dka, dva, dconv, dz, dbd]
    bounds = [0, 2048, 2560, 3072, 3584, 5120, 5632, 5760]
    w_pieces = [w_inp[:, lo:hi] for lo, hi in zip(bounds[:-1], bounds[1:])]
    dx, dg_mix = in_proj_bwd(pieces, w_pieces, x, dx1, g_mix)
    dwp = None
    for k, pc in enumerate(pieces):
        dwp = matmul_tn(h1, pc, "dw_in_%d" % k, into=dwp, col0=bounds[k], width=P_WIDTH)
    dw_in = jnp.concatenate([dwp[:, P_QA:P_BD + 8], dwp[:, :P_QA]], axis=1)

    grads = dict(w_in=dw_in, w_branch_a=dwa, w_branch_b=dwb, w_out=dw_out, w_gate_up=dw_gu, w_down=dw_down,
                 w_ple_gate=dw_pg, w_ple_proj=dw_pp)
    small_grads = dict(g_mix=dg_mix[0], g_ffn=dg_ffn[0], g_ple=dg_ple[0], g_final=dg_final[0],
                       conv_w=dconv_w[:CONV_K].reshape(-1), rel_bias=d_rel.reshape(-1), w_onorm=dw_on[0],
                       a_log=dal[0, :B_HEADS], dt_bias=ddtb[0, :B_HEADS], loss=loss[0, :1])
    return dx.reshape(b, s, D_MODEL), grads, small_grads


BIG = (("w_in", (D_MODEL, D_IN), 1), ("w_branch_a", (A_WIDTH, D_MODEL), 1), ("w_branch_b", (B_WIDTH, D_MODEL), 1),
       ("w_out", (D_MODEL, D_MODEL), 0), ("w_gate_up", (D_MODEL, 2 * D_FF), 1), ("w_down", (D_FF, D_MODEL), 0),
       ("w_ple_gate", (D_MODEL, D_MODEL), 0), ("w_ple_proj", (PLE_DIM, D_MODEL), 1))
N_CHIPS = 4
PACK_ROWS = -(-sum(sh[0] * sh[1] for _, sh, _ in BIG) // (N_CHIPS * 128 * 64)) * 64
HALF_ROWS = PACK_ROWS // 2


def _shard_shape(shape, axis):
    return (shape[0] // N_CHIPS, shape[1]) if axis == 0 else (shape[0], shape[1] // N_CHIPS)


def _pack_rows(parts, total):
    used = sum(a.shape[-2] for a in parts)
    pad = jnp.zeros(parts[0].shape[:-2] + (total - used, 128), parts[0].dtype)
    return jnp.concatenate(parts + [pad], axis=-2)


def pack_grads(grads):
    parts = []
    for n, shape, axis in BIG:
        rs, cs = _shard_shape(shape, axis)
        g = grads[n].astype(BF16)
        seg = g.reshape(N_CHIPS, rs, cs) if axis == 0 else jnp.transpose(g.reshape(rs, N_CHIPS, cs), (1, 0, 2))
        parts.append(seg.reshape(N_CHIPS, -1, 128))
    return _pack_rows(parts, PACK_ROWS)


def unpack_shard(flat):
    out, r0 = {}, 0
    for n, shape, axis in BIG:
        rs, cs = _shard_shape(shape, axis)
        nr = rs * cs // 128
        out[n] = flat[r0:r0 + nr].reshape(rs, cs)
        r0 += nr
    return out


def _place():
    return lax.axis_index("x"), lax.axis_index("y"), lax.axis_index("c")


ANY = pl.BlockSpec(memory_space=pl.ANY)


def allgather_weights(shards, chip):
    nw = len(BIG)

    def gathered_shape(n, shape):
        return (N_CHIPS,) + _shard_shape(shape, 1) if n == "w_in" else shape

    def body(*refs):
        w_refs, o_refs = refs[:nw], refs[nw:2 * nw]
        send_sems, recv_sems = refs[2 * nw:]
        x, y, c = _place()
        sibling = (x, y, 1 - c)
        chips = [(1 - x, y), (x, 1 - y), (1 - x, 1 - y)]

        def copy(k, src, dst, to):
            return pltpu.make_async_remote_copy(src_ref=src, dst_ref=dst, send_sem=send_sems.at[k],
                                                recv_sem=recv_sems.at[k], device_id=to, device_id_type=MESH)

        def blk(i, cx, cy, hf):
            n, shape, axis = BIG[i]
            rs, cs = _shard_shape(shape, axis)
            hr = rs // 2
            ci = 2 * cx + cy
            if n == "w_in":
                return o_refs[i].at[ci, pl.ds(pl.multiple_of(hf * hr, 16), hr), :]
            if axis == 0:
                return o_refs[i].at[pl.ds(pl.multiple_of(ci * rs + hf * hr, 16), hr), :]
            return o_refs[i].at[pl.ds(pl.multiple_of(hf * hr, 16), hr), pl.ds(pl.multiple_of(ci * cs, 128), cs)]

        def my_half(i):
            hr = _shard_shape(BIG[i][1], BIG[i][2])[0] // 2
            return w_refs[i].at[pl.ds(pl.multiple_of(c * hr, 16), hr), :]

        first = [copy(6 * i + j, my_half(i), blk(i, x, y, c), (*chip_, c))
                 for i in range(nw) for j, chip_ in enumerate(chips)]
        for cp in first:
            cp.start()
        passed = []
        for i in range(nw):
            for j, chip_ in enumerate(chips):
                copy(6 * i + j, my_half(i), blk(i, *chip_, c), (*chip_, c)).wait_recv()
                fwd = copy(6 * i + 3 + j, blk(i, *chip_, c), blk(i, *chip_, c), sibling)
                fwd.start()
                passed.append(fwd)
        for i in range(nw):
            for j, chip_ in enumerate(chips):
                copy(6 * i + 3 + j, my_half(i), blk(i, *chip_, 1 - c), sibling).wait_recv()
        for cp in first + passed:
            cp.wait_send()

    outs = pl.pallas_call(
        body, name="allgather_weights",
        in_specs=[ANY] * nw, out_specs=[ANY] * nw,
        out_shape=[jax.ShapeDtypeStruct(gathered_shape(n, shape), BF16) for n, shape, _ in BIG],
        scratch_shapes=[pltpu.SemaphoreType.DMA((6 * nw,)), pltpu.SemaphoreType.DMA((6 * nw,))],
    )(*[shards[n] for n, _, _ in BIG])
    full = {}
    for (n, shape, axis), o in zip(BIG, outs):
        rs, cs = _shard_shape(shape, axis)
        if n == "w_in":
            full[n] = lax.dynamic_update_slice(o, shards[n][None], (chip, 0, 0))
        elif axis == 0:
            full[n] = lax.dynamic_update_slice(o, shards[n], (chip * rs, 0))
        else:
            full[n] = lax.dynamic_update_slice(o, shards[n], (0, chip * cs))
    return full


def small_allreduce(v, name):
    r = v.shape[0]

    def body(v_ref, o_ref, buf, send_sems, recv_sems):
        x, y, c = _place()
        me = 4 * x + 2 * y + c
        buf[me] = v_ref[...]
        flips = [(fx, fy, fc) for fx in (0, 1) for fy in (0, 1) for fc in (0, 1)][1:]
        peers = [((1 - x) if fx else x, (1 - y) if fy else y, (1 - c) if fc else c) for fx, fy, fc in flips]

        def copy(k, slot, to):
            return pltpu.make_async_remote_copy(src_ref=v_ref, dst_ref=buf.at[slot], send_sem=send_sems.at[k],
                                                recv_sem=recv_sems.at[k], device_id=to, device_id_type=MESH)

        sends = [copy(k, me, peer) for k, peer in enumerate(peers)]
        for cp in sends:
            cp.start()
        for k, (px, py, pc) in enumerate(peers):
            copy(k, 4 * px + 2 * py + pc, (px, py, pc)).wait_recv()
        for cp in sends:
            cp.wait_send()
        acc = buf[0]
        for d in range(1, 8):
            acc = acc + buf[d]
        o_ref[...] = acc

    return pl.pallas_call(
        body, name=name,
        in_specs=[pl.BlockSpec(memory_space=pltpu.VMEM)], out_specs=pl.BlockSpec(memory_space=pltpu.VMEM),
        out_shape=jax.ShapeDtypeStruct((r, 128), F32),
        scratch_shapes=[pltpu.VMEM((8, r, 128), F32), pltpu.SemaphoreType.DMA((7,)), pltpu.SemaphoreType.DMA((7,))],
    )(v)


def swap_halves(g):
    half = HALF_ROWS

    def body(g_ref, o_ref, send_sem, recv_sem):
        x, y, c = _place()
        cp = pltpu.make_async_remote_copy(
            src_ref=g_ref.at[:, pl.ds((1 - c) * half, half), :], dst_ref=o_ref, send_sem=send_sem,
            recv_sem=recv_sem, device_id=(x, y, 1 - c), device_id_type=MESH)
        cp.start()
        cp.wait()

    return pl.pallas_call(
        body, name="swap_halves", in_specs=[ANY], out_specs=ANY,
        out_shape=jax.ShapeDtypeStruct((N_CHIPS, half, 128), g.dtype),
        scratch_shapes=[pltpu.SemaphoreType.DMA, pltpu.SemaphoreType.DMA],
    )(g)


def add_halves(g, other, place):
    half = HALF_ROWS
    tr = _tile_rows(half)
    nblk = half // tr

    def body(pref, g0, g1, g2, g3, o0, o1, o2, o3, pf_ref, pb_ref):
        f = lambda r: r[...].astype(F32)
        pf_ref[...] = f(g0) + f(o0)
        pb_ref[0] = _bf(f(g1) + f(o1))
        pb_ref[1] = _bf(f(g2) + f(o2))
        pb_ref[2] = _bf(f(g3) + f(o3))

    gspec = lambda k: pl.BlockSpec((None, tr, 128), lambda i, pr: ((pr[0] + k) % N_CHIPS, pr[1] * nblk + i, 0))
    ospec = lambda k: pl.BlockSpec((None, tr, 128), lambda i, pr: ((pr[0] + k) % N_CHIPS, i, 0))
    return pl.pallas_call(
        body, name="add_halves",
        grid_spec=pltpu.PrefetchScalarGridSpec(
            num_scalar_prefetch=1, grid=(nblk,),
            in_specs=[gspec(0), gspec(1), gspec(2), gspec(3), ospec(0), ospec(1), ospec(2), ospec(3)],
            out_specs=[pl.BlockSpec((tr, 128), lambda i, pr: (i, 0)),
                       pl.BlockSpec((3, tr, 128), lambda i, pr: (0, i, 0))]),
        out_shape=[jax.ShapeDtypeStruct((half, 128), F32), jax.ShapeDtypeStruct((3, half, 128), BF16)],
        compiler_params=_cp(("parallel",), VMEM_LIMIT),
    )(place, g, g, g, g, other, other, other, other)


def _tile_rows(n, cap=2048):
    best = 16
    for t in range(16, cap + 1, 16):
        if n % t == 0:
            best = t
    assert n % best == 0
    return best


def exchange_partials(pb):
    def body(p_ref, o_ref, send_sems, recv_sems):
        x, y, c = _place()
        me = 2 * x + y
        cps = []
        for k in range(1, N_CHIPS):
            to = (me + k) % N_CHIPS
            cps.append(pltpu.make_async_remote_copy(
                src_ref=p_ref.at[k - 1], dst_ref=o_ref.at[k - 1], send_sem=send_sems.at[k - 1],
                recv_sem=recv_sems.at[k - 1], device_id=(to // 2, to % 2, c), device_id_type=MESH))
        for cp in cps:
            cp.start()
        for cp in cps:
            cp.wait()

    return pl.pallas_call(
        body, name="exchange_partials", in_specs=[ANY], out_specs=ANY,
        out_shape=jax.ShapeDtypeStruct(pb.shape, pb.dtype),
        scratch_shapes=[pltpu.SemaphoreType.DMA((3,)), pltpu.SemaphoreType.DMA((3,))],
    )(pb)


def add_partials(pf, got, place):
    half = HALF_ROWS
    tr = _tile_rows(half)

    def body(pref, pf_ref, got_ref, o_ref):
        o_ref[...] = ((pf_ref[...] + got_ref[0].astype(F32)) + got_ref[1].astype(F32)) + got_ref[2].astype(F32)

    return pl.pallas_call(
        body, name="add_partials",
        grid_spec=pltpu.PrefetchScalarGridSpec(
            num_scalar_prefetch=1, grid=(half // tr,),
            in_specs=[pl.BlockSpec((tr, 128), lambda i, pr: (i, 0)),
                      pl.BlockSpec((3, tr, 128), lambda i, pr: (0, i, 0))],
            out_specs=pl.BlockSpec((None, tr, 128), lambda i, pr: (pr[1], i, 0))),
        out_shape=jax.ShapeDtypeStruct((2, half, 128), F32),
        compiler_params=_cp(("parallel",), VMEM_LIMIT),
    )(place, pf, got)


def join_halves(both):
    def body(r_ref, o_ref, send_sem, recv_sem):
        x, y, c = _place()
        cp = pltpu.make_async_remote_copy(src_ref=r_ref.at[c], dst_ref=o_ref.at[c], send_sem=send_sem,
                                          recv_sem=recv_sem, device_id=(x, y, 1 - c), device_id_type=MESH)
        cp.start()
        pltpu.make_async_remote_copy(src_ref=r_ref.at[c], dst_ref=o_ref.at[1 - c], send_sem=send_sem,
                                     recv_sem=recv_sem, device_id=(x, y, 1 - c), device_id_type=MESH).wait_recv()
        cp.wait_send()

    return pl.pallas_call(
        body, name="join_halves", in_specs=[ANY], out_specs=ANY,
        out_shape=jax.ShapeDtypeStruct(both.shape, F32),
        scratch_shapes=[pltpu.SemaphoreType.DMA, pltpu.SemaphoreType.DMA],
        input_output_aliases={0: 0},
    )(both)


def reduce_scatter_grads(gpack):
    x, y, c = _place()
    place = jnp.stack([2 * x + y, c]).astype(jnp.int32)
    other = swap_halves(gpack)
    pf, pb = add_halves(gpack, other, place)
    got = exchange_partials(pb)
    return join_halves(add_partials(pf, got, place)).reshape(PACK_ROWS, 128)


SMALL = (("g_mix", D_MODEL), ("g_ffn", D_MODEL), ("g_ple", D_MODEL), ("g_final", D_MODEL),
         ("conv_w", CONV_K * CONV_CH), ("rel_bias", A_HEADS * N_REL), ("w_onorm", B_DIM),
         ("a_log", B_HEADS), ("dt_bias", B_HEADS), ("loss", 1))


def _pad128(v):
    v = v.reshape(-1)
    return jnp.pad(v, (0, -v.shape[0] % 128))


def pack_small(d, names, rows):
    flat = jnp.concatenate([_pad128(d[n]) for n in names]).reshape(-1, 128)
    return jnp.pad(flat, ((0, rows - flat.shape[0]), (0, 0)))


def unpack_small(flat, names_sizes):
    out, r0 = {}, 0
    v = flat.reshape(-1)
    for n, size in names_sizes:
        out[n] = v[r0:r0 + size]
        r0 += -(-size // 128) * 128
    return out


def kernel(x, p, g_mix, w_in, conv_w, a_log, dt_bias, rel_bias, w_onorm, w_branch_a, w_branch_b, w_out, g_ffn, w_gate_up, w_down, g_ple, w_ple_gate, w_ple_proj, g_final, loss_target, m_g_mix, m_w_in, m_conv_w, m_a_log, m_dt_bias, m_rel_bias, m_w_onorm, m_w_branch_a, m_w_branch_b, m_w_out, m_g_ffn, m_w_gate_up, m_w_down, m_g_ple, m_w_ple_gate, m_w_ple_proj, m_g_final, v_g_mix, v_w_in, v_conv_w, v_a_log, v_dt_bias, v_rel_bias, v_w_onorm, v_w_branch_a, v_w_branch_b, v_w_out, v_g_ffn, v_w_gate_up, v_w_down, v_g_ple, v_w_ple_gate, v_w_ple_proj, v_g_final):
    names = ["g_mix", "w_in", "conv_w", "a_log", "dt_bias", "rel_bias", "w_onorm", "w_branch_a", "w_branch_b",
             "w_out", "g_ffn", "w_gate_up", "w_down", "g_ple", "w_ple_gate", "w_ple_proj", "g_final"]
    w = dict(zip(names, [g_mix, w_in, conv_w, a_log, dt_bias, rel_bias, w_onorm, w_branch_a, w_branch_b, w_out,
                         g_ffn, w_gate_up, w_down, g_ple, w_ple_gate, w_ple_proj, g_final]))
    m = dict(zip(names, [m_g_mix, m_w_in, m_conv_w, m_a_log, m_dt_bias, m_rel_bias, m_w_onorm, m_w_branch_a,
                         m_w_branch_b, m_w_out, m_g_ffn, m_w_gate_up, m_w_down, m_g_ple, m_w_ple_gate,
                         m_w_ple_proj, m_g_final]))
    v = dict(zip(names, [v_g_mix, v_w_in, v_conv_w, v_a_log, v_dt_bias, v_rel_bias, v_w_onorm, v_w_branch_a,
                         v_w_branch_b, v_w_out, v_g_ffn, v_w_gate_up, v_w_down, v_g_ple, v_w_ple_gate,
                         v_w_ple_proj, v_g_final]))
    xi, yi, ci = _place()
    chip = 2 * xi + yi
    big_names = [n for n, _, _ in BIG]

    shards2d = {n: w[n].reshape(w[n].shape[-2:]) for n in big_names}
    wts = allgather_weights({n: a.astype(BF16) for n, a in shards2d.items()}, chip)
    conv_sh = jnp.where(ci == 0, w["conv_w"].reshape(CONV_K, CONV_CH // N_CHIPS), 0.0)
    conv_slots = lax.dynamic_update_slice(jnp.zeros((N_CHIPS, CONV_K, CONV_CH // N_CHIPS), F32), conv_sh[None],
                                          (chip, 0, 0))
    conv_all = small_allreduce(conv_slots.reshape(-1, 128), "gather_conv_w")
    conv_full = jnp.transpose(conv_all.reshape(N_CHIPS, CONV_K, CONV_CH // N_CHIPS), (1, 0, 2)).reshape(CONV_K, CONV_CH)
    small = {n: w[n] for n in names if n not in big_names}
    small["conv_w"] = conv_full

    grad_x, grads, small_grads = local_step(x, p[0], loss_target, wts, small)

    gshard = unpack_shard(reduce_scatter_grads(pack_grads(grads)))
    small_names = [n for n, _ in SMALL]
    red = unpack_small(small_allreduce(pack_small(small_grads, small_names, 112), "allreduce_small"), SMALL)
    loss = red["loss"][0]
    conv_g = lax.dynamic_slice(red["conv_w"].reshape(CONV_K, N_CHIPS, CONV_CH // N_CHIPS), (0, chip, 0),
                               (CONV_K, 1, CONV_CH // N_CHIPS))
    gsmall = {n: red[n].reshape(w[n].shape) for n in small_names if n not in ("loss", "conv_w")}
    gsmall["conv_w"] = conv_g.reshape(w["conv_w"].shape)

    grad, delta, new_m, new_v = {}, {}, {}, {}
    for n in big_names:
        shp = w[n].shape
        d_, m_, v_ = adamw(shards2d[n], gshard[n], m[n].reshape(shp[-2:]), v[n].reshape(shp[-2:]), "adamw_" + n)
        grad[n], delta[n], new_m[n], new_v[n] = gshard[n].reshape(shp), d_.reshape(shp), m_.reshape(shp), v_.reshape(shp)
    snames = [n for n in small_names if n != "loss"]
    ssizes = [(n, w[n].size) for n in snames]
    pk = lambda d: pack_small(d, snames, 64)
    d_, m_, v_ = adamw(pk(w), pk(gsmall), pk(m), pk(v), "adamw_small")
    ds, ms, vs = unpack_small(d_, ssizes), unpack_small(m_, ssizes), unpack_small(v_, ssizes)
    for n in snames:
        shp = w[n].shape
        grad[n], delta[n], new_m[n], new_v[n] = gsmall[n], ds[n].reshape(shp), ms[n].reshape(shp), vs[n].reshape(shp)

    return (loss, grad_x, *[grad[n] for n in names], *[delta[n] for n in names],
            *[new_m[n] for n in names], *[new_v[n] for n in names])
```

```python
import functools

import jax
import jax.numpy as jnp
from jax import lax
from jax.experimental import pallas as pl
from jax.experimental.pallas import tpu as pltpu

F32 = jnp.float32
BF16 = jnp.bfloat16
HI = lax.Precision.HIGHEST
MESH = pl.DeviceIdType.MESH

D_MODEL = 1024
CHUNK = 64
PLE_DIM = 256
EPS = 1e-6
A_HEADS = 8
A_HEAD_DIM = 64
A_WIDTH = 512
A_LOOKBACK = 8
BAND = (A_LOOKBACK + 1) * CHUNK
TAIL = 3 * CHUNK
REL_CLIP = 128
N_REL = 2 * REL_CLIP + 1
B_HEADS = 4
B_DIM = 128
B_WIDTH = 512
CONV_K = 4
CONV_CH = 1536
D_FF = 2816
SPLIT_Z = 3584
D_IN = 5640
ADAM_LR, ADAM_B1, ADAM_B2, ADAM_EPS, ADAM_WD, ADAM_STEP = 0.001, 0.9, 0.999, 1e-08, 0.01, 10

P_GATES, P_QA, P_KA, P_VA, P_CONV, P_Z, P_BD, P_WIDTH = 0, 2048, 2560, 3072, 3584, 5120, 5632, 5760

VMEM_LIMIT = 56 * 1024 * 1024


def _cp(sem, vmem=None, **kw):
    return pltpu.CompilerParams(dimension_semantics=sem, vmem_limit_bytes=vmem, **kw)


def _tile(n, cap):
    best = None
    for t in range(128, cap + 1, 128):
        if n % t == 0:
            best = t
    assert best is not None, (n, cap)
    return best


def _nn(a, b, prec=None):
    return lax.dot_general(a, b, (((1,), (0,)), ((), ())), preferred_element_type=F32, precision=prec)


def _nt(a, b, prec=None):
    return lax.dot_general(a, b, (((1,), (1,)), ((), ())), preferred_element_type=F32, precision=prec)


def _tn(a, b, prec=None):
    return lax.dot_general(a, b, (((0,), (0,)), ((), ())), preferred_element_type=F32, precision=prec)


def _bnn(a, b, prec=None):
    return lax.dot_general(a, b, (((2,), (1,)), ((0,), (0,))), preferred_element_type=F32, precision=prec)


def _bnt(a, b, prec=None):
    return lax.dot_general(a, b, (((2,), (2,)), ((0,), (0,))), preferred_element_type=F32, precision=prec)


def _bf(a):
    return a.astype(BF16)


def _split(a):
    hi = a.astype(BF16)
    return hi, (a - hi.astype(F32)).astype(BF16)


def _bnn_exact(lhs_b, rhs):
    h1 = _bf(rhs)
    r1 = rhs - h1.astype(F32)
    h2 = _bf(r1)
    h3 = _bf(r1 - h2.astype(F32))
    return _bnn(lhs_b, h1) + (_bnn(lhs_b, h2) + _bnn(lhs_b, h3))


def _bnn3(a, b):
    ah, al = a if isinstance(a, tuple) else _split(a)
    bh, bl = b if isinstance(b, tuple) else _split(b)
    return _bnn(ah, bh) + (_bnn(ah, bl) + _bnn(al, bh))


def _sigmoid(x):
    return 0.5 * jnp.tanh(0.5 * x) + 0.5


def _softplus(x):
    return jnp.maximum(x, 0.0) + jnp.log(1.0 + jnp.exp(-jnp.abs(x)))


def rms_matmul(x, g, w, name, tm=512, tn_cap=1024):
    t, d = x.shape
    n = w.shape[1]
    tm = min(tm, t)
    tn = _tile(n, tn_cap)

    def body(x_ref, g_ref, w_ref, o_ref, h_ref):
        @pl.when(pl.program_id(1) == 0)
        def _():
            xv = x_ref[...]
            r = lax.rsqrt(jnp.mean(xv * xv, axis=-1, keepdims=True) + EPS)
            h_ref[...] = _bf(xv * r * g_ref[...])

        o_ref[...] = _bf(_nn(h_ref[...], w_ref[...]))

    return pl.pallas_call(
        body, name=name, grid=(t // tm, n // tn),
        in_specs=[pl.BlockSpec((tm, d), lambda i, j: (i, 0)),
                  pl.BlockSpec((1, d), lambda i, j: (0, 0)),
                  pl.BlockSpec((d, tn), lambda i, j: (0, j))],
        out_specs=[pl.BlockSpec((tm, tn), lambda i, j: (i, j)),
                   pl.BlockSpec((tm, d), lambda i, j: (i, 0))],
        out_shape=[jax.ShapeDtypeStruct((t, n), BF16), jax.ShapeDtypeStruct((t, d), BF16)],
        compiler_params=_cp(("parallel", "arbitrary"), VMEM_LIMIT),
    )(x, g, w)


def matmul_tn(a, b, name, into=None, col0=0, width=None, tm=1024, tk_cap=1408, tn_cap=1408):
    m, k1 = a.shape
    n = b.shape[1]
    tm = min(tm, m)
    tk = _tile(k1, tk_cap)
    tn = _tile(n, tn_cap)
    while col0 % tn:
        tn = _tile(n, tn - 128)
    nk = m // tm
    c0 = col0 // tn

    def body(*refs):
        a_ref, b_ref, o_ref, acc = refs[0], refs[1], refs[-2], refs[-1]

        @pl.when(pl.program_id(2) == 0)
        def _():
            acc[...] = jnp.zeros_like(acc)

        acc[...] += _tn(_bf(a_ref[...]), _bf(b_ref[...]))

        @pl.when(pl.program_id(2) == nk - 1)
        def _():
            o_ref[...] = _bf(acc[...])

    in_specs = [pl.BlockSpec((tm, tk), lambda i, j, k: (k, i)),
                pl.BlockSpec((tm, tn), lambda i, j, k: (k, j))]
    args = [a, b]
    total = n if width is None else width
    aliases = {}
    if into is not None:
        in_specs.append(ANY)
        args.append(into)
        total = into.shape[1]
        aliases = {2: 0}
    return pl.pallas_call(
        body, name=name, grid=(k1 // tk, n // tn, nk),
        in_specs=in_specs,
        out_specs=pl.BlockSpec((tk, tn), lambda i, j, k: (i, c0 + j)),
        out_shape=jax.ShapeDtypeStruct((k1, total), BF16),
        scratch_shapes=[pltpu.VMEM((tk, tn), F32)],
        input_output_aliases=aliases,
        compiler_params=_cp(("parallel", "parallel", "arbitrary"), VMEM_LIMIT),
    )(*args)


def _tail_onehot(qi):
    r = lax.broadcasted_iota(jnp.int32, (384, TAIL), 0)
    kj = lax.broadcasted_iota(jnp.int32, (384, TAIL), 1)
    return (r == jnp.minimum(REL_CLIP + qi - kj, REL_CLIP) + REL_CLIP).astype(F32)


def bias_tail(rel_pad):
    def body(rb_ref, o_ref):
        rb = rb_ref[...]
        for qi in range(CHUNK):
            o_ref[qi] = _nn(rb, _tail_onehot(qi), HI)

    return pl.pallas_call(
        body, name="bias_tail",
        out_shape=jax.ShapeDtypeStruct((CHUNK, A_HEADS, TAIL), F32),
    )(rel_pad)


def bias_grad(db_t, db_far):
    def body(t_ref, f_ref, o_ref):
        acc = jnp.zeros((A_HEADS, 384), F32)
        for qi in range(CHUNK):
            acc = acc + _nt(t_ref[qi], _tail_onehot(qi), HI)
        far = jnp.sum(jnp.sum(f_ref[...], axis=2), axis=1, keepdims=True)
        lane = lax.broadcasted_iota(jnp.int32, (A_HEADS, 384), 1)
        o_ref[...] = acc + jnp.where(lane == 2 * REL_CLIP, far, 0.0)

    return pl.pallas_call(
        body, name="bias_grad",
        out_shape=jax.ShapeDtypeStruct((A_HEADS, 384), F32),
    )(db_t, db_far)


ATT_CB = 8


def _stack_heads(a, lane):
    return jnp.concatenate([jnp.where(lane < 64, a, 0.0), jnp.where(lane >= 64, a, 0.0)], axis=0)


def _fill_band_pads(k_ref, v_ref, kp, vp, s):
    z = jnp.zeros((A_LOOKBACK * CHUNK, 128), BF16)
    kp[pl.ds(0, A_LOOKBACK * CHUNK), :] = z
    vp[pl.ds(0, A_LOOKBACK * CHUNK), :] = z
    kp[pl.ds(A_LOOKBACK * CHUNK, s), :] = _bf(k_ref[...])
    vp[pl.ds(A_LOOKBACK * CHUNK, s), :] = _bf(v_ref[...])


def attn_fwd(proj, bias_band, b, s):
    t = b * s
    nc = s // CHUNK
    qb, kb_, vb_ = P_QA // 128, P_KA // 128, P_VA // 128

    nstep = nc // ATT_CB
    rows = ATT_CB * CHUNK

    def body(q_ref, k_ref, v_ref, b_ref, o_ref, lse_ref, kp, vp):
        n0 = pl.program_id(2) * ATT_CB

        @pl.when(n0 == 0)
        def _():
            _fill_band_pads(k_ref, v_ref, kp, vp, s)

        lane = lax.broadcasted_iota(jnp.int32, (CHUNK, 128), 1)
        col = lax.broadcasted_iota(jnp.int32, (2 * CHUNK, BAND), 1)
        bias2 = b_ref[...]

        def chunk(cc):
            n = n0 + cc
            r0 = pl.multiple_of(cc * CHUNK, CHUNK)
            start = pl.multiple_of(n * CHUNK, CHUNK)
            kb = kp[pl.ds(start, BAND), :]
            vb = vp[pl.ds(start, BAND), :]
            q2 = _stack_heads(q_ref[pl.ds(r0, CHUNK), :] * (A_HEAD_DIM ** -0.5), lane)
            sc = jnp.where(col >= (A_LOOKBACK - n) * CHUNK, _nt(_bf(q2), kb) + bias2, -1e30)
            mx = jnp.max(sc, axis=1, keepdims=True)
            p = jnp.exp(sc - mx)
            l = jnp.sum(p, axis=1, keepdims=True)
            o2 = _nn(_bf(p), vb) / l
            lse2 = mx + jnp.log(l)
            o_ref[pl.ds(r0, CHUNK), :] = jnp.where(lane < 64, o2[:CHUNK], o2[CHUNK:])
            lse_ref[pl.ds(r0, CHUNK), :] = jnp.where(lane < 64, lse2[:CHUNK], lse2[CHUNK:])

        def pair(i, carry):
            chunk(2 * i)
            chunk(2 * i + 1)
            return carry

        lax.fori_loop(0, ATT_CB // 2, pair, 0)

    return pl.pallas_call(
        body, name="attn_fwd", grid=(b, 4, nstep),
        in_specs=[pl.BlockSpec((rows, 128), lambda bb, m, n: (bb * nstep + n, qb + m)),
                  pl.BlockSpec((s, 128), lambda bb, m, n: (bb, kb_ + m)),
                  pl.BlockSpec((s, 128), lambda bb, m, n: (bb, vb_ + m)),
                  pl.BlockSpec((None, 2 * CHUNK, BAND), lambda bb, m, n: (m, 0, 0))],
        out_specs=[pl.BlockSpec((rows, 128), lambda bb, m, n: (bb * nstep + n, m)),
                   pl.BlockSpec((rows, 128), lambda bb, m, n: (bb * nstep + n, m))],
        out_shape=[jax.ShapeDtypeStruct((t, A_WIDTH), F32), jax.ShapeDtypeStruct((t, A_WIDTH), F32)],
        scratch_shapes=[pltpu.VMEM((s + A_LOOKBACK * CHUNK, 128), BF16),
                        pltpu.VMEM((s + A_LOOKBACK * CHUNK, 128), BF16)],
        compiler_params=_cp(("parallel", "parallel", "arbitrary"), VMEM_LIMIT),
    )(proj, proj, proj, bias_band)


def attn_bwd(proj, bias_band, y_a, lse, dy_a, b, s):
    t = b * s
    nc = s // CHUNK
    qb, kb_, vb_ = P_QA // 128, P_KA // 128, P_VA // 128
    pad = A_LOOKBACK * CHUNK
    nstep = nc // ATT_CB
    rows = ATT_CB * CHUNK

    def body(q_ref, k_ref, v_ref, b_ref, do_ref, o_ref, lse_ref,
             dq_ref, dk_ref, dv_ref, dbt_ref, dbf_ref, kp, vp, dkp, dvp):
        bb = pl.program_id(1)
        n0 = pl.program_id(2) * ATT_CB

        @pl.when(n0 == 0)
        def _():
            _fill_band_pads(k_ref, v_ref, kp, vp, s)
            dkp[...] = jnp.zeros_like(dkp)
            dvp[...] = jnp.zeros_like(dvp)

        @pl.when((n0 == 0) & (bb == 0))
        def _():
            dbt_ref[...] = jnp.zeros_like(dbt_ref)
            dbf_ref[...] = jnp.zeros_like(dbf_ref)

        lane = lax.broadcasted_iota(jnp.int32, (CHUNK, 128), 1)
        col = lax.broadcasted_iota(jnp.int32, (2 * CHUNK, BAND), 1)
        bias2 = b_ref[...]

        def chunk(cc):
            n = n0 + cc
            r0 = pl.multiple_of(cc * CHUNK, CHUNK)
            start = pl.multiple_of(n * CHUNK, CHUNK)
            kb = kp[pl.ds(start, BAND), :]
            vb = vp[pl.ds(start, BAND), :]
            q2b = _bf(_stack_heads(q_ref[pl.ds(r0, CHUNK), :] * (A_HEAD_DIM ** -0.5), lane))
            do2 = _stack_heads(do_ref[pl.ds(r0, CHUNK), :], lane)
            do2b = _bf(do2)
            o = o_ref[pl.ds(r0, CHUNK), :]
            lsev = lse_ref[pl.ds(r0, CHUNK), :]
            lse2 = jnp.concatenate([lsev[:, 0:1], lsev[:, 64:65]], axis=0)
            sc = jnp.where(col >= (A_LOOKBACK - n) * CHUNK, _nt(q2b, kb) + bias2, -1e30)
            p = jnp.exp(sc - lse2)
            dp = _nt(do2b, vb)
            delta = jnp.sum(do2 * jnp.concatenate([o, o], axis=0), axis=1, keepdims=True)
            ds = p * (dp - delta)
            dsb = _bf(ds)
            dq2 = _nn(dsb, kb)
            dq_ref[pl.ds(r0, CHUNK), :] = _bf(jnp.where(lane < 64, dq2[:CHUNK], dq2[CHUNK:]) * (A_HEAD_DIM ** -0.5))
            dkp[pl.ds(start, BAND), :] += _tn(dsb, q2b)
            dvp[pl.ds(start, BAND), :] += _tn(_bf(p), do2b)
            dbt_ref[...] += ds[:, BAND - TAIL:]
            dbf_ref[...] += ds[:, 0:128] + ds[:, 128:256] + ds[:, 256:384]

        def step(i, carry):
            chunk(i)
            return carry

        lax.fori_loop(0, ATT_CB, step, 0)

        @pl.when(n0 == nc - ATT_CB)
        def _():
            dk_ref[...] = _bf(dkp[pl.ds(pad, s), :])
            dv_ref[...] = _bf(dvp[pl.ds(pad, s), :])

    return pl.pallas_call(
        body, name="attn_bwd", grid=(4, b, nstep),
        in_specs=[pl.BlockSpec((rows, 128), lambda m, bb, n: (bb * nstep + n, qb + m)),
                  pl.BlockSpec((s, 128), lambda m, bb, n: (bb, kb_ + m)),
                  pl.BlockSpec((s, 128), lambda m, bb, n: (bb, vb_ + m)),
                  pl.BlockSpec((None, 2 * CHUNK, BAND), lambda m, bb, n: (m, 0, 0)),
                  pl.BlockSpec((rows, 128), lambda m, bb, n: (bb * nstep + n, m)),
                  pl.BlockSpec((rows, 128), lambda m, bb, n: (bb * nstep + n, m)),
                  pl.BlockSpec((rows, 128), lambda m, bb, n: (bb * nstep + n, m))],
        out_specs=[pl.BlockSpec((rows, 128), lambda m, bb, n: (bb * nstep + n, m)),
                   pl.BlockSpec((s, 128), lambda m, bb, n: (bb, m)),
                   pl.BlockSpec((s, 128), lambda m, bb, n: (bb, m)),
                   pl.BlockSpec((None, 2 * CHUNK, TAIL), lambda m, bb, n: (m, 0, 0)),
                   pl.BlockSpec((None, 2 * CHUNK, 128), lambda m, bb, n: (m, 0, 0))],
        out_shape=[jax.ShapeDtypeStruct((t, A_WIDTH), BF16)] * 3
        + [jax.ShapeDtypeStruct((4, 2 * CHUNK, TAIL), F32),
           jax.ShapeDtypeStruct((4, 2 * CHUNK, 128), F32)],
        scratch_shapes=[pltpu.VMEM((s + pad, 128), BF16), pltpu.VMEM((s + pad, 128), BF16),
                        pltpu.VMEM((s + pad, 128), F32), pltpu.VMEM((s + pad, 128), F32)],
        compiler_params=_cp(("parallel", "arbitrary", "arbitrary"), VMEM_LIMIT),
    )(proj, proj, proj, bias_band, dy_a, y_a, lse)


def _conv_taps(x, w, s):
    row = lax.broadcasted_iota(jnp.int32, x.shape, 0)
    shifted = [x] + [jnp.where(row >= i, pltpu.roll(x, i, 0), 0.0) for i in range(1, CONV_K)]
    acc = shifted[0] * w[CONV_K - 1:CONV_K, :]
    for i in range(1, CONV_K):
        acc = acc + shifted[i] * w[CONV_K - 1 - i:CONV_K - i, :]
    return acc, shifted


def conv_fwd(proj, conv_w8, b, s):
    cb = 512
    c0 = P_CONV // cb

    def body(x_ref, w_ref, o_ref):
        a, _ = _conv_taps(x_ref[...].astype(F32), w_ref[...], s)
        o_ref[...] = a * _sigmoid(a)

    return pl.pallas_call(
        body, name="conv_fwd", grid=(b, CONV_CH // cb),
        in_specs=[pl.BlockSpec((s, cb), lambda bb, j: (bb, c0 + j)),
                  pl.BlockSpec((8, cb), lambda bb, j: (0, j))],
        out_specs=pl.BlockSpec((s, cb), lambda bb, j: (bb, j)),
        out_shape=jax.ShapeDtypeStruct((b * s, CONV_CH), F32),
        compiler_params=_cp(("parallel", "parallel"), VMEM_LIMIT),
    )(proj, conv_w8)


def conv_bwd(proj, conv_w8, dc3, b, s):
    cb = 512
    c0 = P_CONV // cb

    def body(x_ref, w_ref, dc_ref, dx_ref, dw_ref):
        @pl.when(pl.program_id(1) == 0)
        def _():
            dw_ref[...] = jnp.zeros_like(dw_ref)

        w = w_ref[...]
        a, shifted = _conv_taps(x_ref[...].astype(F32), w, s)
        sg = _sigmoid(a)
        da = dc_ref[...] * (sg * (1.0 + a * (1.0 - sg)))
        row = lax.broadcasted_iota(jnp.int32, da.shape, 0)
        dx = da * w[CONV_K - 1:CONV_K, :]
        for i in range(1, CONV_K):
            dx = dx + jnp.where(row < s - i, pltpu.roll(da, s - i, 0), 0.0) * w[CONV_K - 1 - i:CONV_K - i, :]
        dx_ref[...] = _bf(dx)
        r8 =lax.broadcasted_iota(jnp.int32, (8, cb), 0)
        dw = jnp.zeros((8, cb), F32)
        for i in range(CONV_K):
            dw = dw + jnp.where(r8 == CONV_K - 1 - i, jnp.sum(da * shifted[i], axis=0, keepdims=True), 0.0)
        dw_ref[...] += dw

    return pl.pallas_call(
        body, name="conv_bwd", grid=(CONV_CH // cb, b),
        in_specs=[pl.BlockSpec((s, cb), lambda j, bb: (bb, c0 + j)),
                  pl.BlockSpec((8, cb), lambda j, bb: (0, j)),
                  pl.BlockSpec((None, s, cb), lambda j, bb: (j, bb, 0))],
        out_specs=[pl.BlockSpec((s, cb), lambda j, bb: (bb, j)),
                   pl.BlockSpec((8, cb), lambda j, bb: (0, j))],
        out_shape=[jax.ShapeDtypeStruct((b * s, CONV_CH), BF16), jax.ShapeDtypeStruct((8, CONV_CH), F32)],
        compiler_params=_cp(("parallel", "arbitrary"), VMEM_LIMIT),
    )(proj, conv_w8, dc3)


def _pick_lane(v, k):
    lane = lax.broadcasted_iota(jnp.int32, v.shape, 1)
    return jnp.sum(jnp.where(lane == k, v, 0.0), axis=1, keepdims=True)


def _chunk_masks(ncb):
    i = lax.broadcasted_iota(jnp.int32, (ncb, CHUNK, CHUNK), 1)
    j = lax.broadcasted_iota(jnp.int32, (ncb, CHUNK, CHUNK), 2)
    return i, j


def _col_of_row(rowvec, eye):
    return jnp.sum(jnp.where(eye, rowvec, 0.0), axis=2, keepdims=True)


def _dn_chunk_math(cq, ck, cv, bd, al_row, dtb_row, h, ncb):
    r = ncb * CHUNK
    i, j = _chunk_masks(ncb)
    eye = i == j
    low = i >= j
    strict = i > j
    ones = jnp.ones((ncb, CHUNK, CHUNK), F32)

    braw = _pick_lane(bd, h)
    draw = _pick_lane(bd, B_HEADS + h)
    al = _pick_lane(al_row, h)
    dtb = _pick_lane(dtb_row, h)
    ea = jnp.exp(al)
    beta = _sigmoid(braw)
    sp_arg = draw + dtb
    g = -ea * _softplus(sp_arg)

    rq = lax.rsqrt(jnp.sum(cq * cq, axis=1, keepdims=True) + EPS)
    rk = lax.rsqrt(jnp.sum(ck * ck, axis=1, keepdims=True) + EPS)
    nq = cq * rq
    kn = ck * rk
    qn = nq * (B_DIM ** -0.5)

    def c3(a):
        return a.reshape(ncb, CHUNK, a.shape[-1])

    qn3, kn3, v3, beta3 = c3(qn), c3(kn), c3(cv), c3(beta)
    gb = jnp.broadcast_to(c3(g), (ncb, CHUNK, CHUNK))
    gc_b = _bnn_exact(low.astype(BF16), gb)
    gr_b = _bnn_exact(_bf(ones), jnp.where(eye, gc_b, 0.0))
    dm = jnp.where(low, jnp.exp(jnp.where(low, gc_b - gr_b, 0.0)), 0.0)
    gc = gc_b[:, :, 0:1]
    gl = gc_b[:, CHUNK - 1:CHUNK, 0:1]
    gam = jnp.exp(gc)
    egl = jnp.exp(gl)
    edec = jnp.exp(gl - gc)

    knb = _bf(kn3)
    kk = _bnt(knb, knb)
    kd = jnp.where(strict, kk * dm, 0.0)
    a = beta3 * kd
    tm = eye.astype(F32)
    sz = 1
    while sz < CHUNK:
        off = jnp.where(((i // (2 * sz)) == (j // (2 * sz))) & ((i // sz) != (j // sz)), a, 0.0)
        tmb = _bf(tm)
        tm = tm - _bnn(_bf(_bnn(tmb, _bf(off))), tmb)
        sz *= 2
    bv = beta3 * v3
    bk = (beta3 * gam) * kn3
    sol = _bnn3(_split(tm), jnp.concatenate([bv, bk], axis=2))
    u, wk = sol[:, :, :B_DIM], sol[:, :, B_DIM:]
    qk = _bnt(_bf(qn3), knb)
    p = jnp.where(low, qk * dm, 0.0)
    kdec = kn3 * edec
    qg = gam * qn3
    return dict(beta=beta3, g=c3(g), ea=ea, sp_arg=c3(sp_arg), rq=c3(rq), rk=c3(rk), nq=c3(nq),
                qn=qn3, kn=kn3, v=v3, gc=gc, gl=gl, gam=gam, egl=egl, edec=edec, dm=dm, kd=kd, a=a,
                tm=tm, u=u, wk=wk, qk=qk, p=p, kdec=kdec, qg=qg, eye=eye, low=low, strict=strict)


def dn_prep(c, proj, al_row, dtb_row, b, s, ncb=8):
    t = b * s
    r = ncb * CHUNK
    nblk = t // r
    bd_blk = P_BD // 128

    def body(cq_ref, ck_ref, cv_ref, bd_ref, al_ref, dtb_ref, u_ref, wk_ref, qg_ref, kdec_ref, p_ref, egl_ref):
        h = pl.program_id(1)
        m = _dn_chunk_math(cq_ref[...], ck_ref[...], cv_ref[...], bd_ref[...].astype(F32), al_ref[...], dtb_ref[...], h, ncb)
        u_ref[...] = m["u"].reshape(r, B_DIM)
        wk_ref[...] = m["wk"].reshape(r, B_DIM)
        qg_ref[...] = m["qg"].reshape(r, B_DIM)
        kdec_ref[...] = m["kdec"].reshape(r, B_DIM)
        p_ref[...] = m["p"].reshape(r, CHUNK)
        egl_ref[...] = jnp.broadcast_to(m["egl"], (ncb, 8, 128)).reshape(ncb * 8, 128)

    col = lambda k: pl.BlockSpec((r, 128), lambda i, h: (i, k * B_HEADS + h))
    out_col = pl.BlockSpec((r, 128), lambda i, h: (i, h))
    small = pl.BlockSpec((1, 128), lambda i, h: (0, 0))
    return pl.pallas_call(
        body, name="dn_prep", grid=(nblk, B_HEADS),
        in_specs=[col(0), col(1), col(2), pl.BlockSpec((r, 128), lambda i, h: (i, bd_blk)), small, small],
        out_specs=[out_col, out_col, out_col, out_col,
                   pl.BlockSpec((None, r, CHUNK), lambda i, h: (h, i, 0)),
                   pl.BlockSpec((None, ncb * 8, 128), lambda i, h: (h, i, 0))],
        out_shape=[jax.ShapeDtypeStruct((t, B_WIDTH), F32)] * 4
        + [jax.ShapeDtypeStruct((B_HEADS, t, CHUNK), F32),
           jax.ShapeDtypeStruct((B_HEADS, t // 8, 128), F32)],
        compiler_params=_cp(("parallel", "parallel"), VMEM_LIMIT),
    )(c, c, c, proj, al_row, dtb_row)


def dn_scan_fwd(u, wk, qg, kdec, p, egl, b, s):
    t = b * s
    nc = s // CHUNK

    def body(u_ref, wk_ref, qg_ref, kdec_ref, p_ref, egl_ref, o_ref, ss_ref, st):
        @pl.when(pl.program_id(0) == 0)
        def _():
            st[...] = jnp.zeros_like(st)

        for bb in range(b):
            for h in range(B_HEADS):
                sl = slice(h * B_DIM, (h + 1) * B_DIM)
                sh = st[bb * B_HEADS + h]
                ss_ref[bb, h] = sh
                sb = _bf(sh)
                w = u_ref[bb, :, sl] - _nt(_bf(wk_ref[bb, :, sl]), sb)
                o_ref[bb, :, sl] = _nt(_bf(qg_ref[bb, :, sl]), sb) + _nn(_bf(p_ref[h, bb]), _bf(w))
                st[bb * B_HEADS + h] = egl_ref[h, bb][0:1, :] * sh + _tn(_bf(w), _bf(kdec_ref[bb, :, sl]))

    r3 = lambda a: a.reshape(b, s, B_WIDTH)
    act = pl.BlockSpec((b, CHUNK, B_WIDTH), lambda n: (0, n, 0))
    o, states = pl.pallas_call(
        body, name="dn_scan_fwd", grid=(nc,),
        in_specs=[act, act, act, act,
                  pl.BlockSpec((B_HEADS, b, CHUNK, CHUNK), lambda n: (0, 0, n, 0)),
                  pl.BlockSpec((B_HEADS, b, 8, 128), lambda n: (0, 0, n, 0))],
        out_specs=[act, pl.BlockSpec((b, None, B_HEADS, B_DIM, B_DIM), lambda n: (0, n, 0, 0, 0))],
        out_shape=[jax.ShapeDtypeStruct((b, s, B_WIDTH), F32),
                   jax.ShapeDtypeStruct((b, nc, B_HEADS, B_DIM, B_DIM), F32)],
        scratch_shapes=[pltpu.VMEM((b * B_HEADS, B_DIM, B_DIM), F32)],
        compiler_params=_cp(("arbitrary",), VMEM_LIMIT),
    )(r3(u), r3(wk), r3(qg), r3(kdec), p.reshape(B_HEADS, b, s, CHUNK), egl.reshape(B_HEADS, b, s // 8, 128))
    return o.reshape(t, B_WIDTH), states


def dn_scan_bwd(u, wk, qg, kdec, p, egl, states, do, b, s):
    t = b * s
    nc = s // CHUNK

    def body(u_ref, wk_ref, qg_ref, kdec_ref, p_ref, egl_ref, ss_ref, do_ref,
             dw_ref, dwk_ref, dqg_ref, dkdec_ref, dp_ref, degl_ref, dst):
        @pl.when(pl.program_id(0) == 0)
        def _():
            dst[...] = jnp.zeros_like(dst)

        for bb in range(b):
            for h in range(B_HEADS):
                sl = slice(h * B_DIM, (h + 1) * B_DIM)
                k = bb * B_HEADS + h
                sh = ss_ref[bb, h]
                sb = _bf(sh)
                dsp = dst[k]
                dsb = _bf(dsp)
                wkb = _bf(wk_ref[bb, :, sl])
                kdb = _bf(kdec_ref[bb, :, sl])
                pb = _bf(p_ref[h, bb])
                dob = _bf(do_ref[bb, :, sl])
                w = u_ref[bb, :, sl] - _nt(wkb, sb)
                wb = _bf(w)
                dw = _tn(pb, dob) + _nt(kdb, dsb)
                dwb = _bf(dw)
                dw_ref[bb, :, sl] = dw
                dqg_ref[bb, :, sl] = _nn(dob, sb)
                dwk_ref[bb, :, sl] = -_nn(dwb, sb)
                dkdec_ref[bb, :, sl] = _nn(wb, dsb)
                dp_ref[h, bb] = _nt(dob, wb)
                tot = jnp.sum(jnp.sum(sh * dsp, axis=1, keepdims=True), axis=0, keepdims=True)
                degl_ref[h, bb] = jnp.broadcast_to(tot, (8, 128))
                dst[k] = egl_ref[h, bb][0:1, :] * dsp + _tn(dob, _bf(qg_ref[bb, :, sl])) - _tn(dwb, wkb)

    r3 = lambda a: a.reshape(b, s, B_WIDTH)
    act = pl.BlockSpec((b, CHUNK, B_WIDTH), lambda n: (0, nc - 1 - n, 0))
    pspec = pl.BlockSpec((B_HEADS, b, CHUNK, CHUNK), lambda n: (0, 0, nc - 1 - n, 0))
    espec = pl.BlockSpec((B_HEADS, b, 8, 128), lambda n: (0, 0, nc - 1 - n, 0))
    outs = pl.pallas_call(
        body, name="dn_scan_bwd", grid=(nc,),
        in_specs=[act, act, act, act, pspec, espec,
                  pl.BlockSpec((b, None, B_HEADS, B_DIM, B_DIM), lambda n: (0, nc - 1 - n, 0, 0, 0)),
                  act],
        out_specs=[act, act, act, act, pspec, espec],
        out_shape=[jax.ShapeDtypeStruct((b, s, B_WIDTH), F32)] * 4
        + [jax.ShapeDtypeStruct((B_HEADS, b, s, CHUNK), F32),
           jax.ShapeDtypeStruct((B_HEADS, b, s // 8, 128), F32)],
        scratch_shapes=[pltpu.VMEM((b * B_HEADS, B_DIM, B_DIM), F32)],
        compiler_params=_cp(("arbitrary",), VMEM_LIMIT),
    )(r3(u), r3(wk), r3(qg), r3(kdec), p.reshape(B_HEADS, b, s, CHUNK), egl.reshape(B_HEADS, b, s // 8, 128),
      states, r3(do))
    return (*[a.reshape(t, B_WIDTH) for a in outs[:4]], outs[4].reshape(B_HEADS, t, CHUNK),
            outs[5].reshape(B_HEADS, t // 8, 128))


def dn_post_bwd(c, proj, al_row, dtb_row, dw, dwk, dqg, dkdec, dp, degl, b, s, ncb=8):
    t = b * s
    r = ncb * CHUNK
    nblk = t // r
    bd_blk = P_BD // 128

    def body(cq_ref, ck_ref, cv_ref, bd_ref, al_ref, dtb_ref, dw_ref, dwk_ref, dqg_ref, dkdec_ref, dp_ref,
             degl_ref, dc_ref, dbd_ref, dal_ref, ddtb_ref):
        h = pl.program_id(1)

        @pl.when((pl.program_id(0) == 0) & (h == 0))
        def _():
            dal_ref[...] = jnp.zeros_like(dal_ref)
            ddtb_ref[...] = jnp.zeros_like(ddtb_ref)

        m = _dn_chunk_math(cq_ref[...], ck_ref[...], cv_ref[...], bd_ref[...].astype(F32), al_ref[...], dtb_ref[...], h, ncb)
        eye, low, strict = m["eye"], m["low"], m["strict"]
        eyef = eye.astype(F32)

        def c3(a):
            return a.reshape(ncb, CHUNK, a.shape[-1])

        du, dwkv, dqg, dkdec = c3(dw_ref[...]), c3(dwk_ref[...]), c3(dqg_ref[...]), c3(dkdec_ref[...])
        dpm = jnp.where(low, c3(dp_ref[...]), 0.0)
        degl = degl_ref[...].reshape(ncb, 8, 128)[:, 0:1, 0:1]
        beta, gam, kn, qn, v = m["beta"], m["gam"], m["kn"], m["qn"], m["v"]
        dm, kd, a, p = m["dm"], m["kd"], m["a"], m["p"]
        knb, qnb = _bf(kn), _bf(qn)

        eyeb = _bf(eyef)
        th, tl = _split(m["tm"])
        tts = (_bf(_bnt(eyeb, th)), _bf(_bnt(eyeb, tl)))
        xy = _bnn3(tts, jnp.concatenate([du, dwkv], axis=2))
        x, y = xy[:, :, :B_DIM], xy[:, :, B_DIM:]
        da = -jnp.where(strict, _bnt(_bf(x), _bf(m["u"])) + _bnt(_bf(y), _bf(m["wk"])), 0.0)
        dv = beta * x
        sy = jnp.sum(y * kn, axis=2, keepdims=True)
        dbeta = jnp.sum(x * v, axis=2, keepdims=True) + gam * sy + jnp.sum(da * kd, axis=2, keepdims=True)
        dgam = beta * sy + jnp.sum(dqg * qn, axis=2, keepdims=True)
        dkk = da * beta * dm
        dqk = dpm * dm
        dkkb, dqkb = _bf(dkk), _bf(dqk)
        dkn = ((beta * gam) * y + _bnn(dkkb, knb) + _bnn(_bf(_bnt(eyeb, dkkb)), knb)
               + _bnn(_bf(_bnt(eyeb, dqkb)), qnb) + dkdec * m["edec"])
        dqn = gam * dqg + _bnn(dqkb, knb)
        mm = da * a + dpm * p
        ek = jnp.sum(dkdec * m["kdec"], axis=2, keepdims=True)
        dgc = (jnp.sum(mm, axis=2, keepdims=True) - _col_of_row(jnp.sum(mm, axis=1, keepdims=True), eye)
               + dgam * gam - ek)
        dgl = jnp.sum(ek, axis=1, keepdims=True) + degl * m["egl"]
        i, _ = _chunk_masks(ncb)
        dgc = dgc + jnp.where(i[:, :, 0:1] == CHUNK - 1, dgl, 0.0)
        upper = (i <= _chunk_masks(ncb)[1]).astype(BF16)
        dg = _bnn_exact(upper, jnp.broadcast_to(dgc, (ncb, CHUNK, CHUNK)))[:, :, 0:1]

        nq = m["nq"]
        dnq = dqn * (B_DIM ** -0.5)
        dcq = m["rq"] * (dnq - nq * jnp.sum(nq * dnq, axis=2, keepdims=True))
        dck = m["rk"] * (dkn - kn * jnp.sum(kn * dkn, axis=2, keepdims=True))
        dc_ref[0] = dcq.reshape(r, B_DIM)
        dc_ref[1] = dck.reshape(r, B_DIM)
        dc_ref[2] = dv.reshape(r, B_DIM)

        dbraw = (dbeta * beta * (1.0 - beta)).reshape(r, 1)
        sgm = _sigmoid(m["sp_arg"])
        ddraw3 = dg * (-m["ea"]) * sgm
        ddraw = ddraw3.reshape(r, 1)
        lane = lax.broadcasted_iota(jnp.int32, (r, 128), 1)
        contrib = jnp.where(lane == h, dbraw, 0.0) + jnp.where(lane == B_HEADS + h, ddraw, 0.0)

        @pl.when(h == 0)
        def _():
            dbd_ref[...] = contrib

        @pl.when(h != 0)
        def _():
            dbd_ref[...] += contrib

        lane8 = lax.broadcasted_iota(jnp.int32, (8, 128), 1)
        tot_al = jnp.sum(jnp.sum(dg * m["g"], axis=1, keepdims=True), axis=0, keepdims=True).reshape(1, 1)
        tot_dtb = jnp.sum(jnp.sum(ddraw3, axis=1, keepdims=True), axis=0, keepdims=True).reshape(1, 1)
        dal_ref[...] += jnp.where(lane8 == h, tot_al, 0.0)
        ddtb_ref[...] += jnp.where(lane8 == h, tot_dtb, 0.0)

    col = lambda k: pl.BlockSpec((r, 128), lambda i, h: (i, k * B_HEADS + h))
    hcol = pl.BlockSpec((r, 128), lambda i, h: (i, h))
    small = pl.BlockSpec((1, 128), lambda i, h: (0, 0))
    acc = pl.BlockSpec((8, 128), lambda i, h: (0, 0))
    return pl.pallas_call(
        body, name="dn_post_bwd", grid=(nblk, B_HEADS),
        in_specs=[col(0), col(1), col(2), pl.BlockSpec((r, 128), lambda i, h: (i, bd_blk)), small, small,
                  hcol, hcol, hcol, hcol,
                  pl.BlockSpec((None, r, CHUNK), lambda i, h: (h, i, 0)),
                  pl.BlockSpec((None, ncb * 8, 128), lambda i, h: (h, i, 0))],
        out_specs=[pl.BlockSpec((3, r, 128), lambda i, h: (0, i, h)),
                   pl.BlockSpec((r, 128), lambda i, h: (i, 0)), acc, acc],
        out_shape=[jax.ShapeDtypeStruct((3, t, B_WIDTH), F32), jax.ShapeDtypeStruct((t, 128), F32),
                   jax.ShapeDtypeStruct((8, 128), F32), jax.ShapeDtypeStruct((8, 128), F32)],
        compiler_params=_cp(("arbitrary", "arbitrary"), VMEM_LIMIT),
    )(c, c, c, proj, al_row, dtb_row, dw, dwk, dqg, dkdec, dp, degl)


def make_bias_band(rel_bias):
    tail = bias_tail(jnp.pad(rel_bias, ((0, 0), (0, 384 - N_REL))))
    far = jnp.broadcast_to(rel_bias[:, 2 * REL_CLIP][:, None, None], (A_HEADS, CHUNK, BAND - TAIL))
    return jnp.concatenate([far, jnp.transpose(tail, (1, 0, 2))], axis=2).reshape(4, 2 * CHUNK, BAND)


def _rms(x):
    r = lax.rsqrt(jnp.mean(x * x, axis=-1, keepdims=True) + EPS)
    return r, x * r


def _rms_bwd(dh, g, r, n):
    dn = dh * g
    return r * (dn - n * jnp.mean(dn * n, axis=-1, keepdims=True)), dh * n


def _gated_onorm(o, z, w_on):
    parts = []
    for h in range(B_HEADS):
        sl = slice(h * B_DIM, (h + 1) * B_DIM)
        r, n = _rms(o[:, sl])
        parts.append((r, n))
    r4 = [p[0] for p in parts]
    n4 = jnp.concatenate([p[1] for p in parts], axis=1)
    w4 = jnp.concatenate([w_on] * B_HEADS, axis=1)
    sz = _sigmoid(z)
    silu = z * sz
    return n4 * w4 * silu, r4, n4, w4, sz, silu


def mid_fwd(x, y_a, o_b, proj, w_on, wa, wb, w_out, tm=256):
    t = x.shape[0]
    tm = min(tm, t)

    def body(x_ref, ya_ref, ob_ref, z_ref, ga_ref, gb_ref, won_ref, wa_ref, wb_ref, wo_ref, x1_ref, mg_ref):
        yb = _gated_onorm(ob_ref[...], z_ref[...].astype(F32), won_ref[...])[0]
        ua = _nn(_bf(ya_ref[...]), wa_ref[...])
        ub = _nn(_bf(yb), wb_ref[...])
        merged = _sigmoid(ga_ref[...].astype(F32)) * ua + _sigmoid(gb_ref[...].astype(F32)) * ub
        mb = _bf(merged)
        mg_ref[...] = mb
        x1_ref[...] = x_ref[...] + _nn(mb, wo_ref[...])

    rowd = pl.BlockSpec((tm, D_MODEL), lambda i: (i, 0))
    row5 = pl.BlockSpec((tm, 512), lambda i: (i, 0))
    full = lambda a: pl.BlockSpec(a.shape, lambda i: (0,) * a.ndim)
    return pl.pallas_call(
        body, name="mid_fwd", grid=(t // tm,),
        in_specs=[rowd, row5, row5,
                  pl.BlockSpec((tm, 512), lambda i: (i, P_Z // 512)),
                  pl.BlockSpec((tm, D_MODEL), lambda i: (i, 0)),
                  pl.BlockSpec((tm, D_MODEL), lambda i: (i, 1)),
                  full(w_on), full(wa), full(wb), full(w_out)],
        out_specs=[rowd, rowd],
        out_shape=[jax.ShapeDtypeStruct((t, D_MODEL), F32), jax.ShapeDtypeStruct((t, D_MODEL), BF16)],
        compiler_params=_cp(("parallel",), VMEM_LIMIT),
    )(x, y_a, o_b, proj, proj, proj, w_on, wa, wb, w_out)


def mid_bwd(dx1, merged, y_a, o_b, proj, w_on, wa, wb, w_out, tm=256):
    t = dx1.shape[0]
    tm = min(tm, t)

    def body(dx1_ref, mg_ref, ya_ref, ob_ref, z_ref, ga_ref, gb_ref, won_ref, wa_ref, wb_ref, wo_ref,
             dya_ref, dob_ref, dz_ref, dg_ref, dwo_ref, dwa_ref, dwb_ref, dwon_ref):
        @pl.when(pl.program_id(0) == 0)
        def _():
            dwo_ref[...] = jnp.zeros_like(dwo_ref)
            dwa_ref[...] = jnp.zeros_like(dwa_ref)
            dwb_ref[...] = jnp.zeros_like(dwb_ref)
            dwon_ref[...] = jnp.zeros_like(dwon_ref)

        dx1b = _bf(dx1_ref[...])
        dmerged = _nt(dx1b, wo_ref[...])
        dwo_ref[...] += _tn(mg_ref[...], dx1b)
        o = ob_ref[...]
        z = z_ref[...].astype(F32)
        yb, r4, n4, w4, sz, silu = _gated_onorm(o, z, won_ref[...])
        yab, ybb = _bf(ya_ref[...]), _bf(yb)
        ua = _nn(yab, wa_ref[...])
        ub = _nn(ybb, wb_ref[...])
        sa, sb = _sigmoid(ga_ref[...].astype(F32)), _sigmoid(gb_ref[...].astype(F32))
        dua, dub = _bf(dmerged * sa), _bf(dmerged * sb)
        dg_ref[:, 0:D_MODEL] = _bf(dmerged * ua * sa * (1.0 - sa))
        dg_ref[:, D_MODEL:2 * D_MODEL] = _bf(dmerged * ub * sb * (1.0 - sb))
        dwa_ref[...] += _tn(yab, dua)
        dwb_ref[...] += _tn(ybb, dub)
        dya_ref[...] = _nt(dua, wa_ref[...])
        dyb = _nt(dub, wb_ref[...])
        dz_ref[...] = _bf(dyb * (n4 * w4) * (sz * (1.0 + z * (1.0 - sz))))
        dnw = dyb * silu
        dwon = jnp.zeros((1, B_DIM), F32)
        for h in range(B_HEADS):
            sl = slice(h * B_DIM, (h + 1) * B_DIM)
            dxh, dgh = _rms_bwd(dnw[:, sl], won_ref[...], r4[h], n4[:, sl])
            dob_ref[:, sl] = dxh
            dwon = dwon + jnp.sum(dgh, axis=0, keepdims=True)
        dwon_ref[...] += jnp.broadcast_to(dwon, (8, B_DIM))

    rowd = pl.BlockSpec((tm, D_MODEL), lambda i: (i, 0))
    row5 = pl.BlockSpec((tm, 512), lambda i: (i, 0))
    full = lambda a: pl.BlockSpec(a.shape, lambda i: (0,) * a.ndim)
    fixed = lambda shp: pl.BlockSpec(shp, lambda i: (0,) * len(shp))
    return pl.pallas_call(
        body, name="mid_bwd", grid=(t // tm,),
        in_specs=[rowd, rowd, row5, row5,
                  pl.BlockSpec((tm, 512), lambda i: (i, P_Z // 512)),
                  pl.BlockSpec((tm, D_MODEL), lambda i: (i, 0)),
                  pl.BlockSpec((tm, D_MODEL), lambda i: (i, 1)),
                  full(w_on), full(wa), full(wb), full(w_out)],
        out_specs=[row5, row5, row5, pl.BlockSpec((tm, 2 * D_MODEL), lambda i: (i, 0)),
                   fixed((D_MODEL, D_MODEL)), fixed((A_WIDTH, D_MODEL)), fixed((B_WIDTH, D_MODEL)),
                   fixed((8, B_DIM))],
        out_shape=[jax.ShapeDtypeStruct((t, 512), F32), jax.ShapeDtypeStruct((t, 512), F32),
                   jax.ShapeDtypeStruct((t, 512), BF16), jax.ShapeDtypeStruct((t, 2 * D_MODEL), BF16),
           jax.ShapeDtypeStruct((D_MODEL, D_MODEL), F32), jax.ShapeDtypeStruct((A_WIDTH, D_MODEL), F32),
           jax.ShapeDtypeStruct((B_WIDTH, D_MODEL), F32), jax.ShapeDtypeStruct((8, B_DIM), F32)],
        compiler_params=_cp(("arbitrary",), VMEM_LIMIT),
    )(dx1, merged, y_a, o_b, proj, proj, proj, w_on, wa, wb, w_out)


def ffn_fwd(x1, g, w_gu, w_down, tm=512, tf=256):
    t = x1.shape[0]
    tm = min(tm, t)
    nf = D_FF // tf

    def body(x_ref, g_ref, wg_ref, wu_ref, wd_ref, x2_ref, gate_ref, up_ref, h_ref):
        j = pl.program_id(1)

        @pl.when(j == 0)
        def _():
            xv = x_ref[...]
            r, n = _rms(xv)
            h_ref[...] = _bf(n * g_ref[...])
            x2_ref[...] = xv

        hb = h_ref[...]
        gate = _nn(hb, wg_ref[...])
        up = _nn(hb, wu_ref[...])
        gate_ref[...] = _bf(gate)
        up_ref[...] = _bf(up)
        act = gate * _sigmoid(gate) * up
        x2_ref[...] += _nn(_bf(act), wd_ref[...])

    return pl.pallas_call(
        body, name="ffn_fwd", grid=(t // tm, nf),
        in_specs=[pl.BlockSpec((tm, D_MODEL), lambda i, j: (i, 0)),
                  pl.BlockSpec((1, D_MODEL), lambda i, j: (0, 0)),
                  pl.BlockSpec((D_MODEL, tf), lambda i, j: (0, j)),
                  pl.BlockSpec((D_MODEL, tf), lambda i, j: (0, nf + j)),
                  pl.BlockSpec((tf, D_MODEL), lambda i, j: (j, 0))],
        out_specs=[pl.BlockSpec((tm, D_MODEL), lambda i, j: (i, 0)),
                   pl.BlockSpec((tm, tf), lambda i, j: (i, j)),
                   pl.BlockSpec((tm, tf), lambda i, j: (i, j)),
                   pl.BlockSpec((tm, D_MODEL), lambda i, j: (i, 0))],
        out_shape=[jax.ShapeDtypeStruct((t, D_MODEL), F32), jax.ShapeDtypeStruct((t, D_FF), BF16),
                   jax.ShapeDtypeStruct((t, D_FF), BF16), jax.ShapeDtypeStruct((t, D_MODEL), BF16)],
        compiler_params=_cp(("parallel", "arbitrary"), VMEM_LIMIT),
    )(x1, g, w_gu, w_gu, w_down)


def ffn_bwd(dx2, x1, g, gate, up, w_gu, w_down, tm=512, tf=256):
    t = x1.shape[0]
    tm = min(tm, t)
    nf = D_FF // tf

    def body(dx2_ref, x_ref, g_ref, gate_ref, up_ref, wg_ref, wu_ref, wd_ref,
             dx1_ref, act_ref, dgate_ref, dup_ref, dg_ref, dh_acc, dx2b_ref):
        i, j = pl.program_id(0), pl.program_id(1)

        @pl.when((i == 0) & (j == 0))
        def _():
            dg_ref[...] = jnp.zeros_like(dg_ref)

        @pl.when(j == 0)
        def _():
            dx2b_ref[...] = _bf(dx2_ref[...])
            dh_acc[...] = jnp.zeros_like(dh_acc)

        dact = _nt(dx2b_ref[...], wd_ref[...])
        gt, upv = gate_ref[...].astype(F32), up_ref[...].astype(F32)
        sg = _sigmoid(gt)
        silu = gt * sg
        act_ref[...] = _bf(silu * upv)
        dgt = _bf(dact * upv * (sg * (1.0 + gt * (1.0 - sg))))
        dupv = _bf(dact * silu)
        dgate_ref[...] = dgt
        dup_ref[...] = dupv
        dh_acc[...] += _nt(dgt, wg_ref[...]) + _nt(dupv, wu_ref[...])

        @pl.when(j == nf - 1)
        def _():
            r, n = _rms(x_ref[...])
            dx, dgc = _rms_bwd(dh_acc[...], g_ref[...], r, n)
            dx1_ref[...] = dx2_ref[...] + dx
            dg_ref[...] += jnp.broadcast_to(jnp.sum(dgc, axis=0, keepdims=True), (8, D_MODEL))

    rowd = pl.BlockSpec((tm, D_MODEL), lambda i, j: (i, 0))
    ff = pl.BlockSpec((tm, tf), lambda i, j: (i, j))
    return pl.pallas_call(
        body, name="ffn_bwd", grid=(t // tm, nf),
        in_specs=[rowd, rowd, pl.BlockSpec((1, D_MODEL), lambda i, j: (0, 0)), ff, ff,
                  pl.BlockSpec((D_MODEL, tf), lambda i, j: (0, j)),
                  pl.BlockSpec((D_MODEL, tf), lambda i, j: (0, nf + j)),
                  pl.BlockSpec((tf, D_MODEL), lambda i, j: (j, 0))],
        out_specs=[rowd, ff, ff, ff, pl.BlockSpec((8, D_MODEL), lambda i, j: (0, 0))],
        out_shape=[jax.ShapeDtypeStruct((t, D_MODEL), F32), jax.ShapeDtypeStruct((t, D_FF), BF16),
                   jax.ShapeDtypeStruct((t, D_FF), BF16), jax.ShapeDtypeStruct((t, D_FF), BF16),
                   jax.ShapeDtypeStruct((8, D_MODEL), F32)],
        scratch_shapes=[pltpu.VMEM((tm, D_MODEL), F32), pltpu.VMEM((tm, D_MODEL), BF16)],
        compiler_params=_cp(("arbitrary", "arbitrary"), VMEM_LIMIT),
    )(dx2, x1, g, gate, up, w_gu, w_gu, w_down)


def tail_fwd_bwd(x2, p, target, g_ple, g_final, w_pg, w_pp, tm=256):
    t = x2.shape[0]
    tm = min(tm, t)

    def body(x_ref, p_ref, t_ref, gp_ref, gf_ref, wpg_ref, wpp_ref,
             dx_ref, dwpg_ref, dwpp_ref, dgp_ref, dgf_ref, loss_ref):
        @pl.when(pl.program_id(0) == 0)
        def _():
            dwpg_ref[...] = jnp.zeros_like(dwpg_ref)
            dwpp_ref[...] = jnp.zeros_like(dwpp_ref)
            dgp_ref[...] = jnp.zeros_like(dgp_ref)
            dgf_ref[...] = jnp.zeros_like(dgf_ref)
            loss_ref[...] = jnp.zeros_like(loss_ref)

        x2v = x_ref[...]
        gp, gf = gp_ref[...], gf_ref[...]
        r3, n3 = _rms(x2v)
        h3b = _bf(n3 * gp)
        pb = _bf(p_ref[...])
        pg = _sigmoid(_nn(h3b, wpg_ref[...]))
        pp = _nn(pb, wpp_ref[...])
        x3 = x2v + pg * pp
        r4, n4 = _rms(x3)
        err = n4 * gf - t_ref[...]
        part = 0.5 * jnp.sum(jnp.sum(err * err, axis=1, keepdims=True), axis=0, keepdims=True) / D_MODEL
        loss_ref[...] += jnp.broadcast_to(part, (8, 128))
        dy = err * (1.0 / D_MODEL)
        dx3, dgf = _rms_bwd(dy, gf, r4, n4)
        dgf_ref[...] += jnp.broadcast_to(jnp.sum(dgf, axis=0, keepdims=True), (8, D_MODEL))
        dzp = _bf(dx3 * pp * pg * (1.0 - pg))
        dpp = _bf(dx3 * pg)
        dwpg_ref[...] += _tn(h3b, dzp)
        dwpp_ref[...] += _tn(pb, dpp)
        dh3 = _nt(dzp, wpg_ref[...])
        dx, dgp = _rms_bwd(dh3, gp, r3, n3)
        dgp_ref[...] += jnp.broadcast_to(jnp.sum(dgp, axis=0, keepdims=True), (8, D_MODEL))
        dx_ref[...] = dx3 + dx

    rowd = pl.BlockSpec((tm, D_MODEL), lambda i: (i, 0))
    fixed = lambda shp: pl.BlockSpec(shp, lambda i: (0,) * len(shp))
    return pl.pallas_call(
        body, name="tail_fwd_bwd", grid=(t // tm,),
        in_specs=[rowd, pl.BlockSpec((tm, PLE_DIM), lambda i: (i, 0)), rowd,
                  fixed((1, D_MODEL)), fixed((1, D_MODEL)), fixed((D_MODEL, D_MODEL)), fixed((PLE_DIM, D_MODEL))],
        out_specs=[rowd, fixed((D_MODEL, D_MODEL)), fixed((PLE_DIM, D_MODEL)),
                   fixed((8, D_MODEL)), fixed((8, D_MODEL)), fixed((8, 128))],
        out_shape=[jax.ShapeDtypeStruct((t, D_MODEL), F32), jax.ShapeDtypeStruct((D_MODEL, D_MODEL), F32),
                   jax.ShapeDtypeStruct((PLE_DIM, D_MODEL), F32), jax.ShapeDtypeStruct((8, D_MODEL), F32),
                   jax.ShapeDtypeStruct((8, D_MODEL), F32), jax.ShapeDtypeStruct((8, 128), F32)],
        compiler_params=_cp(("arbitrary",), VMEM_LIMIT),
    )(x2, p, target, g_ple, g_final, w_pg, w_pp)


def in_proj_bwd(pieces, weights, x, dx1, g, tm=256):
    t = x.shape[0]
    tm = min(tm, t)
    k = len(pieces)

    def body(*refs):
        p_refs, w_refs = refs[:k], refs[k:2 * k]
        x_ref, dx1_ref, g_ref, dx_ref, dg_ref = refs[2 * k:]

        @pl.when(pl.program_id(0) == 0)
        def _():
            dg_ref[...] = jnp.zeros_like(dg_ref)

        dh = _nt(_bf(p_refs[0][...]), w_refs[0][...])
        for pr, wr in zip(p_refs[1:], w_refs[1:]):
            dh = dh + _nt(_bf(pr[...]), wr[...])
        r, n = _rms(x_ref[...])
        dx, dgc = _rms_bwd(dh, g_ref[...], r, n)
        dx_ref[...] = dx1_ref[...] + dx
        dg_ref[...] += jnp.broadcast_to(jnp.sum(dgc, axis=0, keepdims=True), (8, D_MODEL))

    rowd = pl.BlockSpec((tm, D_MODEL), lambda i: (i, 0))
    return pl.pallas_call(
        body, name="in_proj_bwd", grid=(t // tm,),
        in_specs=[pl.BlockSpec((tm, a.shape[1]), lambda i: (i, 0)) for a in pieces]
        + [pl.BlockSpec(w.shape, lambda i: (0, 0)) for w in weights]
        + [rowd, rowd, pl.BlockSpec((1, D_MODEL), lambda i: (0, 0))],
        out_specs=[rowd, pl.BlockSpec((8, D_MODEL), lambda i: (0, 0))],
        out_shape=[jax.ShapeDtypeStruct((t, D_MODEL), F32), jax.ShapeDtypeStruct((8, D_MODEL), F32)],
        compiler_params=_cp(("arbitrary",), VMEM_LIMIT),
    )(*pieces, *weights, x, dx1, g)


def adamw(w, g, m, v, name, rows_cap=256):
    r, c = w.shape
    tr = r
    for cand in range(8, min(r, rows_cap) + 1, 8):
        if r % cand == 0:
            tr = cand

    def body(w_ref, g_ref, m_ref, v_ref, d_ref, mo_ref, vo_ref):
        gv = g_ref[...]
        mn = ADAM_B1 * m_ref[...] + (1.0 - ADAM_B1) * gv
        vn = ADAM_B2 * v_ref[...] + (1.0 - ADAM_B2) * (gv * gv)
        m_hat = mn / (1.0 - ADAM_B1 ** ADAM_STEP)
        v_hat = vn / (1.0 - ADAM_B2 ** ADAM_STEP)
        d_ref[...] = -ADAM_LR * (m_hat / (jnp.sqrt(v_hat) + ADAM_EPS) + ADAM_WD * w_ref[...])
        mo_ref[...] = mn
        vo_ref[...] = vn

    spec = pl.BlockSpec((tr, c), lambda i: (i, 0))
    return pl.pallas_call(
        body, name=name, grid=(r // tr,),
        in_specs=[spec] * 4, out_specs=[spec] * 3,
        out_shape=[jax.ShapeDtypeStruct((r, c), F32)] * 3,
        compiler_params=_cp(("parallel",), VMEM_LIMIT),
    )(w, g, m, v)


def local_step(x3d, p3d, target3d, wts, small):
    b, s, _ = x3d.shape
    t = b * s
    x = x3d.reshape(t, D_MODEL)
    p = p3d.reshape(t, PLE_DIM)
    target = target3d.reshape(t, D_MODEL)
    g4 = wts["w_in"]
    cut = SPLIT_Z - 2 * (D_IN // N_CHIPS)
    w_inp = jnp.concatenate([g4[2][:, cut + 8:], g4[3], g4[0], g4[1], g4[2][:, :cut], g4[2][:, cut:cut + 8],
                             jnp.zeros((D_MODEL, 120), BF16)], axis=1)
    al_row = jnp.pad(small["a_log"].reshape(1, B_HEADS), ((0, 0), (0, 128 - B_HEADS)))
    dtb_row = jnp.pad(small["dt_bias"].reshape(1, B_HEADS), ((0, 0), (0, 128 - B_HEADS)))
    conv_w8 = jnp.pad(small["conv_w"].reshape(CONV_K, CONV_CH), ((0, 8 - CONV_K), (0, 0)))
    w_on = small["w_onorm"].reshape(1, B_DIM)
    g_mix, g_ffn = small["g_mix"].reshape(1, D_MODEL), small["g_ffn"].reshape(1, D_MODEL)
    g_ple, g_final = small["g_ple"].reshape(1, D_MODEL), small["g_final"].reshape(1, D_MODEL)
    bias_band = make_bias_band(small["rel_bias"].reshape(A_HEADS, N_REL))

    proj, h1 = rms_matmul(x, g_mix, w_inp, "in_proj")
    y_a, lse = attn_fwd(proj, bias_band, b, s)
    c = conv_fwd(proj, conv_w8, b, s)
    u, wk, qg, kdec, pm, egl = dn_prep(c, proj, al_row, dtb_row, b, s)
    o_b, states = dn_scan_fwd(u, wk, qg, kdec, pm, egl, b, s)
    x1, merged = mid_fwd(x, y_a, o_b, proj, w_on, wts["w_branch_a"], wts["w_branch_b"], wts["w_out"])
    x2, gate, up, h2 = ffn_fwd(x1, g_ffn, wts["w_gate_up"], wts["w_down"])

    dx2, dw_pg, dw_pp, dg_ple, dg_final, loss = tail_fwd_bwd(
        x2, p, target, g_ple, g_final, wts["w_ple_gate"], wts["w_ple_proj"])
    dx1, act, dgate, dup, dg_ffn = ffn_bwd(dx2, x1, g_ffn, gate, up, wts["w_gate_up"], wts["w_down"])
    dw_down = matmul_tn(act, dx2, "dw_down")
    dw_gu = matmul_tn(h2, dgate, "dw_gate", width=2 * D_FF)
    dw_gu = matmul_tn(h2, dup, "dw_up", into=dw_gu, col0=D_FF)
    dy_a, do_b, dz, dgates, dw_out, dwa, dwb, dw_on = mid_bwd(
        dx1, merged, y_a, o_b, proj, w_on, wts["w_branch_a"], wts["w_branch_b"], wts["w_out"])
    ddw, ddwk, ddqg, ddkdec, ddp, ddegl = dn_scan_bwd(u, wk, qg, kdec, pm, egl, states, do_b, b, s)
    dc3, dbd, dal, ddtb = dn_post_bwd(c, proj, al_row, dtb_row, ddw, ddwk, ddqg, ddkdec, ddp, ddegl, b, s)
    dconv, dconv_w = conv_bwd(proj, conv_w8, dc3, b, s)
    dqa, dka, dva, dbt, dbf = attn_bwd(proj, bias_band, y_a, lse, dy_a, b, s)
    d_rel = bias_grad(jnp.transpose(dbt.reshape(A_HEADS, CHUNK, TAIL), (1, 0, 2)),
                      dbf.reshape(A_HEADS, CHUNK, 128))[:, :N_REL]

    pieces = [dgates, dqa, dka, dva, dconv, dz, dbd]
    bounds = [0, 2048, 2560, 3072, 3584, 5120, 5632, 5760]
    w_pieces = [w_inp[:, lo:hi] for lo, hi in zip(bounds[:-1], bounds[1:])]
    dx, dg_mix = in_proj_bwd(pieces, w_pieces, x, dx1, g_mix)
    dwp = None
    for k, pc in enumerate(pieces):
        dwp = matmul_tn(h1, pc, "dw_in_%d" % k, into=dwp, col0=bounds[k], width=P_WIDTH)
    dw_in = jnp.concatenate([dwp[:, P_QA:P_BD + 8], dwp[:, :P_QA]], axis=1)

    grads = dict(w_in=dw_in, w_branch_a=dwa, w_branch_b=dwb, w_out=dw_out, w_gate_up=dw_gu, w_down=dw_down,
                 w_ple_gate=dw_pg, w_ple_proj=dw_pp)
    small_grads = dict(g_mix=dg_mix[0], g_ffn=dg_ffn[0], g_ple=dg_ple[0], g_final=dg_final[0],
                       conv_w=dconv_w[:CONV_K].reshape(-1), rel_bias=d_rel.reshape(-1), w_onorm=dw_on[0],
                       a_log=dal[0, :B_HEADS], dt_bias=ddtb[0, :B_HEADS], loss=loss[0, :1])
    return dx.reshape(b, s, D_MODEL), grads, small_grads


BIG = (("w_in", (D_MODEL, D_IN), 1), ("w_branch_a", (A_WIDTH, D_MODEL), 1), ("w_branch_b", (B_WIDTH, D_MODEL), 1),
       ("w_out", (D_MODEL, D_MODEL), 0), ("w_gate_up", (D_MODEL, 2 * D_FF), 1), ("w_down", (D_FF, D_MODEL), 0),
       ("w_ple_gate", (D_MODEL, D_MODEL), 0), ("w_ple_proj", (PLE_DIM, D_MODEL), 1))
N_CHIPS = 4
PACK_ROWS = -(-sum(sh[0] * sh[1] for _, sh, _ in BIG) // (N_CHIPS * 128 * 64)) * 64
HALF_ROWS = PACK_ROWS // 2


def _shard_shape(shape, axis):
    return (shape[0] // N_CHIPS, shape[1]) if axis == 0 else (shape[0], shape[1] // N_CHIPS)


def _pack_rows(parts, total):
    used = sum(a.shape[-2] for a in parts)
    pad = jnp.zeros(parts[0].shape[:-2] + (total - used, 128), parts[0].dtype)
    return jnp.concatenate(parts + [pad], axis=-2)


def pack_grads(grads):
    parts = []
    for n, shape, axis in BIG:
        rs, cs = _shard_shape(shape, axis)
        g = grads[n].astype(BF16)
        seg = g.reshape(N_CHIPS, rs, cs) if axis == 0 else jnp.transpose(g.reshape(rs, N_CHIPS, cs), (1, 0, 2))
        parts.append(seg.reshape(N_CHIPS, -1, 128))
    return _pack_rows(parts, PACK_ROWS)


def unpack_shard(flat):
    out, r0 = {}, 0
    for n, shape, axis in BIG:
        rs, cs = _shard_shape(shape, axis)
        nr = rs * cs // 128
        out[n] = flat[r0:r0 + nr].reshape(rs, cs)
        r0 += nr
    return out


def _place():
    return lax.axis_index("x"), lax.axis_index("y"), lax.axis_index("c")


ANY = pl.BlockSpec(memory_space=pl.ANY)


def allgather_weights(shards, chip):
    nw = len(BIG)

    def gathered_shape(n, shape):
        return (N_CHIPS,) + _shard_shape(shape, 1) if n == "w_in" else shape

    def body(*refs):
        w_refs, o_refs = refs[:nw], refs[nw:2 * nw]
        send_sems, recv_sems = refs[2 * nw:]
        x, y, c = _place()
        sibling = (x, y, 1 - c)
        chips = [(1 - x, y), (x, 1 - y), (1 - x, 1 - y)]

        def copy(k, src, dst, to):
            return pltpu.make_async_remote_copy(src_ref=src, dst_ref=dst, send_sem=send_sems.at[k],
                                                recv_sem=recv_sems.at[k], device_id=to, device_id_type=MESH)

        def blk(i, cx, cy, hf):
            n, shape, axis = BIG[i]
            rs, cs = _shard_shape(shape, axis)
            hr = rs // 2
            ci = 2 * cx + cy
            if n == "w_in":
                return o_refs[i].at[ci, pl.ds(pl.multiple_of(hf * hr, 16), hr), :]
            if axis == 0:
                return o_refs[i].at[pl.ds(pl.multiple_of(ci * rs + hf * hr, 16), hr), :]
            return o_refs[i].at[pl.ds(pl.multiple_of(hf * hr, 16), hr), pl.ds(pl.multiple_of(ci * cs, 128), cs)]

        def my_half(i):
            hr = _shard_shape(BIG[i][1], BIG[i][2])[0] // 2
            return w_refs[i].at[pl.ds(pl.multiple_of(c * hr, 16), hr), :]

        first = [copy(6 * i + j, my_half(i), blk(i, x, y, c), (*chip_, c))
                 for i in range(nw) for j, chip_ in enumerate(chips)]
        for cp in first:
            cp.start()
        passed = []
        for i in range(nw):
            for j, chip_ in enumerate(chips):
                copy(6 * i + j, my_half(i), blk(i, *chip_, c), (*chip_, c)).wait_recv()
                fwd = copy(6 * i + 3 + j, blk(i, *chip_, c), blk(i, *chip_, c), sibling)
                fwd.start()
                passed.append(fwd)
        for i in range(nw):
            for j, chip_ in enumerate(chips):
                copy(6 * i + 3 + j, my_half(i), blk(i, *chip_, 1 - c), sibling).wait_recv()
        for cp in first + passed:
            cp.wait_send()

    outs = pl.pallas_call(
        body, name="allgather_weights",
        in_specs=[ANY] * nw, out_specs=[ANY] * nw,
        out_shape=[jax.ShapeDtypeStruct(gathered_shape(n, shape), BF16) for n, shape, _ in BIG],
        scratch_shapes=[pltpu.SemaphoreType.DMA((6 * nw,)), pltpu.SemaphoreType.DMA((6 * nw,))],
    )(*[shards[n] for n, _, _ in BIG])
    full = {}
    for (n, shape, axis), o in zip(BIG, outs):
        rs, cs = _shard_shape(shape, axis)
        if n == "w_in":
            full[n] = lax.dynamic_update_slice(o, shards[n][None], (chip, 0, 0))
        elif axis == 0:
            full[n] = lax.dynamic_update_slice(o, shards[n], (chip * rs, 0))
        else:
            full[n] = lax.dynamic_update_slice(o, shards[n], (0, chip * cs))
    return full


def small_allreduce(v, name):
    r = v.shape[0]

    def body(v_ref, o_ref, buf, send_sems, recv_sems):
        x, y, c = _place()
        me = 4 * x + 2 * y + c
        buf[me] = v_ref[...]
        flips = [(fx, fy, fc) for fx in (0, 1) for fy in (0, 1) for fc in (0, 1)][1:]
        peers = [((1 - x) if fx else x, (1 - y) if fy else y, (1 - c) if fc else c) for fx, fy, fc in flips]

        def copy(k, slot, to):
            return pltpu.make_async_remote_copy(src_ref=v_ref, dst_ref=buf.at[slot], send_sem=send_sems.at[k],
                                                recv_sem=recv_sems.at[k], device_id=to, device_id_type=MESH)

        sends = [copy(k, me, peer) for k, peer in enumerate(peers)]
        for cp in sends:
            cp.start()
        for k, (px, py, pc) in enumerate(peers):
            copy(k, 4 * px + 2 * py + pc, (px, py, pc)).wait_recv()
        for cp in sends:
            cp.wait_send()
        acc = buf[0]
        for d in range(1, 8):
            acc = acc + buf[d]
        o_ref[...] = acc

    return pl.pallas_call(
        body, name=name,
        in_specs=[pl.BlockSpec(memory_space=pltpu.VMEM)], out_specs=pl.BlockSpec(memory_space=pltpu.VMEM),
        out_shape=jax.ShapeDtypeStruct((r, 128), F32),
        scratch_shapes=[pltpu.VMEM((8, r, 128), F32), pltpu.SemaphoreType.DMA((7,)), pltpu.SemaphoreType.DMA((7,))],
    )(v)


def swap_halves(g):
    half = HALF_ROWS

    def body(g_ref, o_ref, send_sem, recv_sem):
        x, y, c = _place()
        cp = pltpu.make_async_remote_copy(
            src_ref=g_ref.at[:, pl.ds((1 - c) * half, half), :], dst_ref=o_ref, send_sem=send_sem,
            recv_sem=recv_sem, device_id=(x, y, 1 - c), device_id_type=MESH)
        cp.start()
        cp.wait()

    return pl.pallas_call(
        body, name="swap_halves", in_specs=[ANY], out_specs=ANY,
        out_shape=jax.ShapeDtypeStruct((N_CHIPS, half, 128), g.dtype),
        scratch_shapes=[pltpu.SemaphoreType.DMA, pltpu.SemaphoreType.DMA],
    )(g)


def add_halves(g, other, place):
    half = HALF_ROWS
    tr = _tile_rows(half)
    nblk = half // tr

    def body(pref, g0, g1, g2, g3, o0, o1, o2, o3, pf_ref, pb_ref):
        f = lambda r: r[...].astype(F32)
        pf_ref[...] = f(g0) + f(o0)
        pb_ref[0] = _bf(f(g1) + f(o1))
        pb_ref[1] = _bf(f(g2) + f(o2))
        pb_ref[2] = _bf(f(g3) + f(o3))

    gspec = lambda k: pl.BlockSpec((None, tr, 128), lambda i, pr: ((pr[0] + k) % N_CHIPS, pr[1] * nblk + i, 0))
    ospec = lambda k: pl.BlockSpec((None, tr, 128), lambda i, pr: ((pr[0] + k) % N_CHIPS, i, 0))
    return pl.pallas_call(
        body, name="add_halves",
        grid_spec=pltpu.PrefetchScalarGridSpec(
            num_scalar_prefetch=1, grid=(nblk,),
            in_specs=[gspec(0), gspec(1), gspec(2), gspec(3), ospec(0), ospec(1), ospec(2), ospec(3)],
            out_specs=[pl.BlockSpec((tr, 128), lambda i, pr: (i, 0)),
                       pl.BlockSpec((3, tr, 128), lambda i, pr: (0, i, 0))]),
        out_shape=[jax.ShapeDtypeStruct((half, 128), F32), jax.ShapeDtypeStruct((3, half, 128), BF16)],
        compiler_params=_cp(("parallel",), VMEM_LIMIT),
    )(place, g, g, g, g, other, other, other, other)


def _tile_rows(n, cap=2048):
    best = 16
    for t in range(16, cap + 1, 16):
        if n % t == 0:
            best = t
    assert n % best == 0
    return best


def exchange_partials(pb):
    def body(p_ref, o_ref, send_sems, recv_sems):
        x, y, c = _place()
        me = 2 * x + y
        cps = []
        for k in range(1, N_CHIPS):
            to = (me + k) % N_CHIPS
            cps.append(pltpu.make_async_remote_copy(
                src_ref=p_ref.at[k - 1], dst_ref=o_ref.at[k - 1], send_sem=send_sems.at[k - 1],
                recv_sem=recv_sems.at[k - 1], device_id=(to // 2, to % 2, c), device_id_type=MESH))
        for cp in cps:
            cp.start()
        for cp in cps:
            cp.wait()

    return pl.pallas_call(
        body, name="exchange_partials", in_specs=[ANY], out_specs=ANY,
        out_shape=jax.ShapeDtypeStruct(pb.shape, pb.dtype),
        scratch_shapes=[pltpu.SemaphoreType.DMA((3,)), pltpu.SemaphoreType.DMA((3,))],
    )(pb)


def add_partials(pf, got, place):
    half = HALF_ROWS
    tr = _tile_rows(half)

    def body(pref, pf_ref, got_ref, o_ref):
        o_ref[...] = ((pf_ref[...] + got_ref[0].astype(F32)) + got_ref[1].astype(F32)) + got_ref[2].astype(F32)

    return pl.pallas_call(
        body, name="add_partials",
        grid_spec=pltpu.PrefetchScalarGridSpec(
            num_scalar_prefetch=1, grid=(half // tr,),
            in_specs=[pl.BlockSpec((tr, 128), lambda i, pr: (i, 0)),
                      pl.BlockSpec((3, tr, 128), lambda i, pr: (0, i, 0))],
            out_specs=pl.BlockSpec((None, tr, 128), lambda i, pr: (pr[1], i, 0))),
        out_shape=jax.ShapeDtypeStruct((2, half, 128), F32),
        compiler_params=_cp(("parallel",), VMEM_LIMIT),
    )(place, pf, got)


def join_halves(both):
    def body(r_ref, o_ref, send_sem, recv_sem):
        x, y, c = _place()
        cp = pltpu.make_async_remote_copy(src_ref=r_ref.at[c], dst_ref=o_ref.at[c], send_sem=send_sem,
                                          recv_sem=recv_sem, device_id=(x, y, 1 - c), device_id_type=MESH)
        cp.start()
        pltpu.make_async_remote_copy(src_ref=r_ref.at[c], dst_ref=o_ref.at[1 - c], send_sem=send_sem,
                                     recv_sem=recv_sem, device_id=(x, y, 1 - c), device_id_type=MESH).wait_recv()
        cp.wait_send()

    return pl.pallas_call(
        body, name="join_halves", in_specs=[ANY], out_specs=ANY,
        out_shape=jax.ShapeDtypeStruct(both.shape, F32),
        scratch_shapes=[pltpu.SemaphoreType.DMA, pltpu.SemaphoreType.DMA],
        input_output_aliases={0: 0},
    )(both)


def reduce_scatter_grads(gpack):
    x, y, c = _place()
    place = jnp.stack([2 * x + y, c]).astype(jnp.int32)
    other = swap_halves(gpack)
    pf, pb = add_halves(gpack, other, place)
    got = exchange_partials(pb)
    return join_halves(add_partials(pf, got, place)).reshape(PACK_ROWS, 128)


SMALL = (("g_mix", D_MODEL), ("g_ffn", D_MODEL), ("g_ple", D_MODEL), ("g_final", D_MODEL),
         ("conv_w", CONV_K * CONV_CH), ("rel_bias", A_HEADS * N_REL), ("w_onorm", B_DIM),
         ("a_log", B_HEADS), ("dt_bias", B_HEADS), ("loss", 1))


def _pad128(v):
    v = v.reshape(-1)
    return jnp.pad(v, (0, -v.shape[0] % 128))


def pack_small(d, names, rows):
    flat = jnp.concatenate([_pad128(d[n]) for n in names]).reshape(-1, 128)
    return jnp.pad(flat, ((0, rows - flat.shape[0]), (0, 0)))


def unpack_small(flat, names_sizes):
    out, r0 = {}, 0
    v = flat.reshape(-1)
    for n, size in names_sizes:
        out[n] = v[r0:r0 + size]
        r0 += -(-size // 128) * 128
    return out


def kernel(x, p, g_mix, w_in, conv_w, a_log, dt_bias, rel_bias, w_onorm, w_branch_a, w_branch_b, w_out, g_ffn, w_gate_up, w_down, g_ple, w_ple_gate, w_ple_proj, g_final, loss_target, m_g_mix, m_w_in, m_conv_w, m_a_log, m_dt_bias, m_rel_bias, m_w_onorm, m_w_branch_a, m_w_branch_b, m_w_out, m_g_ffn, m_w_gate_up, m_w_down, m_g_ple, m_w_ple_gate, m_w_ple_proj, m_g_final, v_g_mix, v_w_in, v_conv_w, v_a_log, v_dt_bias, v_rel_bias, v_w_onorm, v_w_branch_a, v_w_branch_b, v_w_out, v_g_ffn, v_w_gate_up, v_w_down, v_g_ple, v_w_ple_gate, v_w_ple_proj, v_g_final):
    names = ["g_mix", "w_in", "conv_w", "a_log", "dt_bias", "rel_bias", "w_onorm", "w_branch_a", "w_branch_b",
             "w_out", "g_ffn", "w_gate_up", "w_down", "g_ple", "w_ple_gate", "w_ple_proj", "g_final"]
    w = dict(zip(names, [g_mix, w_in, conv_w, a_log, dt_bias, rel_bias, w_onorm, w_branch_a, w_branch_b, w_out,
                         g_ffn, w_gate_up, w_down, g_ple, w_ple_gate, w_ple_proj, g_final]))
    m = dict(zip(names, [m_g_mix, m_w_in, m_conv_w, m_a_log, m_dt_bias, m_rel_bias, m_w_onorm, m_w_branch_a,
                         m_w_branch_b, m_w_out, m_g_ffn, m_w_gate_up, m_w_down, m_g_ple, m_w_ple_gate,
                         m_w_ple_proj, m_g_final]))
    v = dict(zip(names, [v_g_mix, v_w_in, v_conv_w, v_a_log, v_dt_bias, v_rel_bias, v_w_onorm, v_w_branch_a,
                         v_w_branch_b, v_w_out, v_g_ffn, v_w_gate_up, v_w_down, v_g_ple, v_w_ple_gate,
                         v_w_ple_proj, v_g_final]))
    xi, yi, ci = _place()
    chip = 2 * xi + yi
    big_names = [n for n, _, _ in BIG]

    shards2d = {n: w[n].reshape(w[n].shape[-2:]) for n in big_names}
    wts = allgather_weights({n: a.astype(BF16) for n, a in shards2d.items()}, chip)
    conv_sh = jnp.where(ci == 0, w["conv_w"].reshape(CONV_K, CONV_CH // N_CHIPS), 0.0)
    conv_slots = lax.dynamic_update_slice(jnp.zeros((N_CHIPS, CONV_K, CONV_CH // N_CHIPS), F32), conv_sh[None],
                                          (chip, 0, 0))
    conv_all = small_allreduce(conv_slots.reshape(-1, 128), "gather_conv_w")
    conv_full = jnp.transpose(conv_all.reshape(N_CHIPS, CONV_K, CONV_CH // N_CHIPS), (1, 0, 2)).reshape(CONV_K, CONV_CH)
    small = {n: w[n] for n in names if n not in big_names}
    small["conv_w"] = conv_full

    grad_x, grads, small_grads = local_step(x, p[0], loss_target, wts, small)

    gshard = unpack_shard(reduce_scatter_grads(pack_grads(grads)))
    small_names = [n for n, _ in SMALL]
    red = unpack_small(small_allreduce(pack_small(small_grads, small_names, 112), "allreduce_small"), SMALL)
    loss = red["loss"][0]
    conv_g = lax.dynamic_slice(red["conv_w"].reshape(CONV_K, N_CHIPS, CONV_CH // N_CHIPS), (0, chip, 0),
                               (CONV_K, 1, CONV_CH // N_CHIPS))
    gsmall = {n: red[n].reshape(w[n].shape) for n in small_names if n not in ("loss", "conv_w")}
    gsmall["conv_w"] = conv_g.reshape(w["conv_w"].shape)

    grad, delta, new_m, new_v = {}, {}, {}, {}
    for n in big_names:
        shp = w[n].shape
        d_, m_, v_ = adamw(shards2d[n], gshard[n], m[n].reshape(shp[-2:]), v[n].reshape(shp[-2:]), "adamw_" + n)
        grad[n], delta[n], new_m[n], new_v[n] = gshard[n].reshape(shp), d_.reshape(shp), m_.reshape(shp), v_.reshape(shp)
    snames = [n for n in small_names if n != "loss"]
    ssizes = [(n, w[n].size) for n in snames]
    pk = lambda d: pack_small(d, snames, 64)
    d_, m_, v_ = adamw(pk(w), pk(gsmall), pk(m), pk(v), "adamw_small")
    ds, ms, vs = unpack_small(d_, ssizes), unpack_small(m_, ssizes), unpack_small(v_, ssizes)
    for n in snames:
        shp = w[n].shape
        grad[n], delta[n], new_m[n], new_v[n] = gsmall[n], ds[n].reshape(shp), ms[n].reshape(shp), vs[n].reshape(shp)

    return (loss, grad_x, *[grad[n] for n in names], *[delta[n] for n in names],
            *[new_m[n] for n in names], *[new_v[n] for n in names])
```

```python
import functools

import jax
import jax.numpy as jnp
from jax import lax
from jax.experimental import pallas as pl
from jax.experimental.pallas import tpu as pltpu

F32 = jnp.float32
BF16 = jnp.bfloat16
HI = lax.Precision.HIGHEST
MESH = pl.DeviceIdType.MESH

D_MODEL = 1024
CHUNK = 64
PLE_DIM = 256
EPS = 1e-6
A_HEADS = 8
A_HEAD_DIM = 64
A_WIDTH = 512
A_LOOKBACK = 8
BAND = (A_LOOKBACK + 1) * CHUNK
TAIL = 3 * CHUNK
REL_CLIP = 128
N_REL = 2 * REL_CLIP + 1
B_HEADS = 4
B_DIM = 128
B_WIDTH = 512
CONV_K = 4
CONV_CH = 1536
D_FF = 2816
SPLIT_Z = 3584
D_IN = 5640
ADAM_LR, ADAM_B1, ADAM_B2, ADAM_EPS, ADAM_WD, ADAM_STEP = 0.001, 0.9, 0.999, 1e-08, 0.01, 10

P_GATES, P_QA, P_KA, P_VA, P_CONV, P_Z, P_BD, P_WIDTH = 0, 2048, 2560, 3072, 3584, 5120, 5632, 5760

VMEM_LIMIT = 56 * 1024 * 1024


def _cp(sem, vmem=None, **kw):
    return pltpu.CompilerParams(dimension_semantics=sem, vmem_limit_bytes=vmem, **kw)


def _tile(n, cap):
    best = None
    for t in range(128, cap + 1, 128):
        if n % t == 0:
            best = t
    assert best is not None, (n, cap)
    return best


def _nn(a, b, prec=None):
    return lax.dot_general(a, b, (((1,), (0,)), ((), ())), preferred_element_type=F32, precision=prec)


def _nt(a, b, prec=None):
    return lax.dot_general(a, b, (((1,), (1,)), ((), ())), preferred_element_type=F32, precision=prec)


def _tn(a, b, prec=None):
    return lax.dot_general(a, b, (((0,), (0,)), ((), ())), preferred_element_type=F32, precision=prec)


def _bnn(a, b, prec=None):
    return lax.dot_general(a, b, (((2,), (1,)), ((0,), (0,))), preferred_element_type=F32, precision=prec)


def _bnt(a, b, prec=None):
    return lax.dot_general(a, b, (((2,), (2,)), ((0,), (0,))), preferred_element_type=F32, precision=prec)


def _bf(a):
    return a.astype(BF16)


def _split(a):
    hi = a.astype(BF16)
    return hi, (a - hi.astype(F32)).astype(BF16)


def _bnn_exact(lhs_b, rhs):
    h1 = _bf(rhs)
    r1 = rhs - h1.astype(F32)
    h2 = _bf(r1)
    h3 = _bf(r1 - h2.astype(F32))
    return _bnn(lhs_b, h1) + (_bnn(lhs_b, h2) + _bnn(lhs_b, h3))


def _bnn3(a, b):
    ah, al = a if isinstance(a, tuple) else _split(a)
    bh, bl = b if isinstance(b, tuple) else _split(b)
    return _bnn(ah, bh) + (_bnn(ah, bl) + _bnn(al, bh))


def _sigmoid(x):
    return 0.5 * jnp.tanh(0.5 * x) + 0.5


def _softplus(x):
    return jnp.maximum(x, 0.0) + jnp.log(1.0 + jnp.exp(-jnp.abs(x)))


def rms_matmul(x, g, w, name, tm=512, tn_cap=1024):
    t, d = x.shape
    n = w.shape[1]
    tm = min(tm, t)
    tn = _tile(n, tn_cap)

    def body(x_ref, g_ref, w_ref, o_ref, h_ref):
        @pl.when(pl.program_id(1) == 0)
        def _():
            xv = x_ref[...]
            r = lax.rsqrt(jnp.mean(xv * xv, axis=-1, keepdims=True) + EPS)
            h_ref[...] = _bf(xv * r * g_ref[...])

        o_ref[...] = _bf(_nn(h_ref[...], w_ref[...]))

    return pl.pallas_call(
        body, name=name, grid=(t // tm, n // tn),
        in_specs=[pl.BlockSpec((tm, d), lambda i, j: (i, 0)),
                  pl.BlockSpec((1, d), lambda i, j: (0, 0)),
                  pl.BlockSpec((d, tn), lambda i, j: (0, j))],
        out_specs=[pl.BlockSpec((tm, tn), lambda i, j: (i, j)),
                   pl.BlockSpec((tm, d), lambda i, j: (i, 0))],
        out_shape=[jax.ShapeDtypeStruct((t, n), BF16), jax.ShapeDtypeStruct((t, d), BF16)],
        compiler_params=_cp(("parallel", "arbitrary"), VMEM_LIMIT),
    )(x, g, w)


def matmul_tn(a, b, name, into=None, col0=0, width=None, tm=1024, tk_cap=1408, tn_cap=1408):
    m, k1 = a.shape
    n = b.shape[1]
    tm = min(tm, m)
    tk = _tile(k1, tk_cap)
    tn = _tile(n, tn_cap)
    while col0 % tn:
        tn = _tile(n, tn - 128)
    nk = m // tm
    c0 = col0 // tn

    def body(*refs):
        a_ref, b_ref, o_ref, acc = refs[0], refs[1], refs[-2], refs[-1]

        @pl.when(pl.program_id(2) == 0)
        def _():
            acc[...] = jnp.zeros_like(acc)

        acc[...] += _tn(_bf(a_ref[...]), _bf(b_ref[...]))

        @pl.when(pl.program_id(2) == nk - 1)
        def _():
            o_ref[...] = _bf(acc[...])

    in_specs = [pl.BlockSpec((tm, tk), lambda i, j, k: (k, i)),
                pl.BlockSpec((tm, tn), lambda i, j, k: (k, j))]
    args = [a, b]
    total = n if width is None else width
    aliases = {}
    if into is not None:
        in_specs.append(ANY)
        args.append(into)
        total = into.shape[1]
        aliases = {2: 0}
    return pl.pallas_call(
        body, name=name, grid=(k1 // tk, n // tn, nk),
        in_specs=in_specs,
        out_specs=pl.BlockSpec((tk, tn), lambda i, j, k: (i, c0 + j)),
        out_shape=jax.ShapeDtypeStruct((k1, total), BF16),
        scratch_shapes=[pltpu.VMEM((tk, tn), F32)],
        input_output_aliases=aliases,
        compiler_params=_cp(("parallel", "parallel", "arbitrary"), VMEM_LIMIT),
    )(*args)


def _tail_onehot(qi):
    r = lax.broadcasted_iota(jnp.int32, (384, TAIL), 0)
    kj = lax.broadcasted_iota(jnp.int32, (384, TAIL), 1)
    return (r == jnp.minimum(REL_CLIP + qi - kj, REL_CLIP) + REL_CLIP).astype(F32)


def bias_tail(rel_pad):
    def body(rb_ref, o_ref):
        rb = rb_ref[...]
        for qi in range(CHUNK):
            o_ref[qi] = _nn(rb, _tail_onehot(qi), HI)

    return pl.pallas_call(
        body, name="bias_tail",
        out_shape=jax.ShapeDtypeStruct((CHUNK, A_HEADS, TAIL), F32),
    )(rel_pad)


def bias_grad(db_t, db_far):
    def body(t_ref, f_ref, o_ref):
        acc = jnp.zeros((A_HEADS, 384), F32)
        for qi in range(CHUNK):
            acc = acc + _nt(t_ref[qi], _tail_onehot(qi), HI)
        far = jnp.sum(jnp.sum(f_ref[...], axis=2), axis=1, keepdims=True)
        lane = lax.broadcasted_iota(jnp.int32, (A_HEADS, 384), 1)
        o_ref[...] = acc + jnp.where(lane == 2 * REL_CLIP, far, 0.0)

    return pl.pallas_call(
        body, name="bias_grad",
        out_shape=jax.ShapeDtypeStruct((A_HEADS, 384), F32),
    )(db_t, db_far)


ATT_CB = 8


def _stack_heads(a, lane):
    return jnp.concatenate([jnp.where(lane < 64, a, 0.0), jnp.where(lane >= 64, a, 0.0)], axis=0)


def _fill_band_pads(k_ref, v_ref, kp, vp, s):
    z = jnp.zeros((A_LOOKBACK * CHUNK, 128), BF16)
    kp[pl.ds(0, A_LOOKBACK * CHUNK), :] = z
    vp[pl.ds(0, A_LOOKBACK * CHUNK), :] = z
    kp[pl.ds(A_LOOKBACK * CHUNK, s), :] = _bf(k_ref[...])
    vp[pl.ds(A_LOOKBACK * CHUNK, s), :] = _bf(v_ref[...])


def attn_fwd(proj, bias_band, b, s):
    t = b * s
    nc = s // CHUNK
    qb, kb_, vb_ = P_QA // 128, P_KA // 128, P_VA // 128

    nstep = nc // ATT_CB
    rows = ATT_CB * CHUNK

    def body(q_ref, k_ref, v_ref, b_ref, o_ref, lse_ref, kp, vp):
        n0 = pl.program_id(2) * ATT_CB

        @pl.when(n0 == 0)
        def _():
            _fill_band_pads(k_ref, v_ref, kp, vp, s)

        lane = lax.broadcasted_iota(jnp.int32, (CHUNK, 128), 1)
        col = lax.broadcasted_iota(jnp.int32, (2 * CHUNK, BAND), 1)
        bias2 = b_ref[...]

        def chunk(cc):
            n = n0 + cc
            r0 = pl.multiple_of(cc * CHUNK, CHUNK)
            start = pl.multiple_of(n * CHUNK, CHUNK)
            kb = kp[pl.ds(start, BAND), :]
            vb = vp[pl.ds(start, BAND), :]
            q2 = _stack_heads(q_ref[pl.ds(r0, CHUNK), :] * (A_HEAD_DIM ** -0.5), lane)
            sc = jnp.where(col >= (A_LOOKBACK - n) * CHUNK, _nt(_bf(q2), kb) + bias2, -1e30)
            mx = jnp.max(sc, axis=1, keepdims=True)
            p = jnp.exp(sc - mx)
            l = jnp.sum(p, axis=1, keepdims=True)
            o2 = _nn(_bf(p), vb) / l
            lse2 = mx + jnp.log(l)
            o_ref[pl.ds(r0, CHUNK), :] = jnp.where(lane < 64, o2[:CHUNK], o2[CHUNK:])
            lse_ref[pl.ds(r0, CHUNK), :] = jnp.where(lane < 64, lse2[:CHUNK], lse2[CHUNK:])

        def pair(i, carry):
            chunk(2 * i)
            chunk(2 * i + 1)
            return carry

        lax.fori_loop(0, ATT_CB // 2, pair, 0)

    return pl.pallas_call(
        body, name="attn_fwd", grid=(b, 4, nstep),
        in_specs=[pl.BlockSpec((rows, 128), lambda bb, m, n: (bb * nstep + n, qb + m)),
                  pl.BlockSpec((s, 128), lambda bb, m, n: (bb, kb_ + m)),
                  pl.BlockSpec((s, 128), lambda bb, m, n: (bb, vb_ + m)),
                  pl.BlockSpec((None, 2 * CHUNK, BAND), lambda bb, m, n: (m, 0, 0))],
        out_specs=[pl.BlockSpec((rows, 128), lambda bb, m, n: (bb * nstep + n, m)),
                   pl.BlockSpec((rows, 128), lambda bb, m, n: (bb * nstep + n, m))],
        out_shape=[jax.ShapeDtypeStruct((t, A_WIDTH), F32), jax.ShapeDtypeStruct((t, A_WIDTH), F32)],
        scratch_shapes=[pltpu.VMEM((s + A_LOOKBACK * CHUNK, 128), BF16),
                        pltpu.VMEM((s + A_LOOKBACK * CHUNK, 128), BF16)],
        compiler_params=_cp(("parallel", "parallel", "arbitrary"), VMEM_LIMIT),
    )(proj, proj, proj, bias_band)


def attn_bwd(proj, bias_band, y_a, lse, dy_a, b, s):
    t = b * s
    nc = s // CHUNK
    qb, kb_, vb_ = P_QA // 128, P_KA // 128, P_VA // 128
    pad = A_LOOKBACK * CHUNK
    nstep = nc // ATT_CB
    rows = ATT_CB * CHUNK

    def body(q_ref, k_ref, v_ref, b_ref, do_ref, o_ref, lse_ref,
             dq_ref, dk_ref, dv_ref, dbt_ref, dbf_ref, kp, vp, dkp, dvp):
        bb = pl.program_id(1)
        n0 = pl.program_id(2) * ATT_CB

        @pl.when(n0 == 0)
        def _():
            _fill_band_pads(k_ref, v_ref, kp, vp, s)
            dkp[...] = jnp.zeros_like(dkp)
            dvp[...] = jnp.zeros_like(dvp)

        @pl.when((n0 == 0) & (bb == 0))
        def _():
            dbt_ref[...] = jnp.zeros_like(dbt_ref)
            dbf_ref[...] = jnp.zeros_like(dbf_ref)

        lane = lax.broadcasted_iota(jnp.int32, (CHUNK, 128), 1)
        col = lax.broadcasted_iota(jnp.int32, (2 * CHUNK, BAND), 1)
        bias2 = b_ref[...]

        def chunk(cc):
            n = n0 + cc
            r0 = pl.multiple_of(cc * CHUNK, CHUNK)
            start = pl.multiple_of(n * CHUNK, CHUNK)
            kb = kp[pl.ds(start, BAND), :]
            vb = vp[pl.ds(start, BAND), :]
            q2b = _bf(_stack_heads(q_ref[pl.ds(r0, CHUNK), :] * (A_HEAD_DIM ** -0.5), lane))
            do2 = _stack_heads(do_ref[pl.ds(r0, CHUNK), :], lane)
            do2b = _bf(do2)
            o = o_ref[pl.ds(r0, CHUNK), :]
            lsev = lse_ref[pl.ds(r0, CHUNK), :]
            lse2 = jnp.concatenate([lsev[:, 0:1], lsev[:, 64:65]], axis=0)
            sc = jnp.where(col >= (A_LOOKBACK - n) * CHUNK, _nt(q2b, kb) + bias2, -1e30)
            p = jnp.exp(sc - lse2)
            dp = _nt(do2b, vb)
            delta = jnp.sum(do2 * jnp.concatenate([o, o], axis=0), axis=1, keepdims=True)
            ds = p * (dp - delta)
            dsb = _bf(ds)
            dq2 = _nn(dsb, kb)
            dq_ref[pl.ds(r0, CHUNK), :] = _bf(jnp.where(lane < 64, dq2[:CHUNK], dq2[CHUNK:]) * (A_HEAD_DIM ** -0.5))
            dkp[pl.ds(start, BAND), :] += _tn(dsb, q2b)
            dvp[pl.ds(start, BAND), :] += _tn(_bf(p), do2b)
            dbt_ref[...] += ds[:, BAND - TAIL:]
            dbf_ref[...] += ds[:, 0:128] + ds[:, 128:256] + ds[:, 256:384]

        def step(i, carry):
            chunk(i)
            return carry

        lax.fori_loop(0, ATT_CB, step, 0)

        @pl.when(n0 == nc - ATT_CB)
        def _():
            dk_ref[...] = _bf(dkp[pl.ds(pad, s), :])
            dv_ref[...] = _bf(dvp[pl.ds(pad, s), :])

    return pl.pallas_call(
        body, name="attn_bwd", grid=(4, b, nstep),
        in_specs=[pl.BlockSpec((rows, 128), lambda m, bb, n: (bb * nstep + n, qb + m)),
                  pl.BlockSpec((s, 128), lambda m, bb, n: (bb, kb_ + m)),
                  pl.BlockSpec((s, 128), lambda m, bb, n: (bb, vb_ + m)),
                  pl.BlockSpec((None, 2 * CHUNK, BAND), lambda m, bb, n: (m, 0, 0)),
                  pl.BlockSpec((rows, 128), lambda m, bb, n: (bb * nstep + n, m)),
                  pl.BlockSpec((rows, 128), lambda m, bb, n: (bb * nstep + n, m)),
                  pl.BlockSpec((rows, 128), lambda m, bb, n: (bb * nstep + n, m))],
        out_specs=[pl.BlockSpec((rows, 128), lambda m, bb, n: (bb * nstep + n, m)),
                   pl.BlockSpec((s, 128), lambda m, bb, n: (bb, m)),
                   pl.BlockSpec((s, 128), lambda m, bb, n: (bb, m)),
                   pl.BlockSpec((None, 2 * CHUNK, TAIL), lambda m, bb, n: (m, 0, 0)),
                   pl.BlockSpec((None, 2 * CHUNK, 128), lambda m, bb, n: (m, 0, 0))],
        out_shape=[jax.ShapeDtypeStruct((t, A_WIDTH), BF16)] * 3
        + [jax.ShapeDtypeStruct((4, 2 * CHUNK, TAIL), F32),
           jax.ShapeDtypeStruct((4, 2 * CHUNK, 128), F32)],
        scratch_shapes=[pltpu.VMEM((s + pad, 128), BF16), pltpu.VMEM((s + pad, 128), BF16),
                        pltpu.VMEM((s + pad, 128), F32), pltpu.VMEM((s + pad, 128), F32)],
        compiler_params=_cp(("parallel", "arbitrary", "arbitrary"), VMEM_LIMIT),
    )(proj, proj, proj, bias_band, dy_a, y_a, lse)


def _conv_taps(x, w, s):
    row = lax.broadcasted_iota(jnp.int32, x.shape, 0)
    shifted = [x] + [jnp.where(row >= i, pltpu.roll(x, i, 0), 0.0) for i in range(1, CONV_K)]
    acc = shifted[0] * w[CONV_K - 1:CONV_K, :]
    for i in range(1, CONV_K):
        acc = acc + shifted[i] * w[CONV_K - 1 - i:CONV_K - i, :]
    return acc, shifted


def conv_fwd(proj, conv_w8, b, s):
    cb = 512
    c0 = P_CONV // cb

    def body(x_ref, w_ref, o_ref):
        a, _ = _conv_taps(x_ref[...].astype(F32), w_ref[...], s)
        o_ref[...] = a * _sigmoid(a)

    return pl.pallas_call(
        body, name="conv_fwd", grid=(b, CONV_CH // cb),
        in_specs=[pl.BlockSpec((s, cb), lambda bb, j: (bb, c0 + j)),
                  pl.BlockSpec((8, cb), lambda bb, j: (0, j))],
        out_specs=pl.BlockSpec((s, cb), lambda bb, j: (bb, j)),
        out_shape=jax.ShapeDtypeStruct((b * s, CONV_CH), F32),
        compiler_params=_cp(("parallel", "parallel"), VMEM_LIMIT),
    )(proj, conv_w8)


def conv_bwd(proj, conv_w8, dc3, b, s):
    cb = 512
    c0 = P_CONV // cb

    def body(x_ref, w_ref, dc_ref, dx_ref, dw_ref):
        @pl.when(pl.program_id(1) == 0)
        def _():
            dw_ref[...] = jnp.zeros_like(dw_ref)

        w = w_ref[...]
        a, shifted = _conv_taps(x_ref[...].astype(F32), w, s)
        sg = _sigmoid(a)
        da = dc_ref[...] * (sg * (1.0 + a * (1.0 - sg)))
        row = lax.broadcasted_iota(jnp.int32, da.shape, 0)
        dx = da * w[CONV_K - 1:CONV_K, :]
        for i in range(1, CONV_K):
            dx = dx + jnp.where(row < s - i, pltpu.roll(da, s - i, 0), 0.0) * w[CONV_K - 1 - i:CONV_K - i, :]
        dx_ref[...] = _bf(dx)
        r8 =lax.broadcasted_iota(jnp.int32, (8, cb), 0)
        dw = jnp.zeros((8, cb), F32)
        for i in range(CONV_K):
            dw = dw + jnp.where(r8 == CONV_K - 1 - i, jnp.sum(da * shifted[i], axis=0, keepdims=True), 0.0)
        dw_ref[...] += dw

    return pl.pallas_call(
        body, name="conv_bwd", grid=(CONV_CH // cb, b),
        in_specs=[pl.BlockSpec((s, cb), lambda j, bb: (bb, c0 + j)),
                  pl.BlockSpec((8, cb), lambda j, bb: (0, j)),
                  pl.BlockSpec((None, s, cb), lambda j, bb: (j, bb, 0))],
        out_specs=[pl.BlockSpec((s, cb), lambda j, bb: (bb, j)),
                   pl.BlockSpec((8, cb), lambda j, bb: (0, j))],
        out_shape=[jax.ShapeDtypeStruct((b * s, CONV_CH), BF16), jax.ShapeDtypeStruct((8, CONV_CH), F32)],
        compiler_params=_cp(("parallel", "arbitrary"), VMEM_LIMIT),
    )(proj, conv_w8, dc3)


def _pick_lane(v, k):
    lane = lax.broadcasted_iota(jnp.int32, v.shape, 1)
    return jnp.sum(jnp.where(lane == k, v, 0.0), axis=1, keepdims=True)


def _chunk_masks(ncb):
    i = lax.broadcasted_iota(jnp.int32, (ncb, CHUNK, CHUNK), 1)
    j = lax.broadcasted_iota(jnp.int32, (ncb, CHUNK, CHUNK), 2)
    return i, j


def _col_of_row(rowvec, eye):
    return jnp.sum(jnp.where(eye, rowvec, 0.0), axis=2, keepdims=True)


def _dn_chunk_math(cq, ck, cv, bd, al_row, dtb_row, h, ncb):
    r = ncb * CHUNK
    i, j = _chunk_masks(ncb)
    eye = i == j
    low = i >= j
    strict = i > j
    ones = jnp.ones((ncb, CHUNK, CHUNK), F32)

    braw = _pick_lane(bd, h)
    draw = _pick_lane(bd, B_HEADS + h)
    al = _pick_lane(al_row, h)
    dtb = _pick_lane(dtb_row, h)
    ea = jnp.exp(al)
    beta = _sigmoid(braw)
    sp_arg = draw + dtb
    g = -ea * _softplus(sp_arg)

    rq = lax.rsqrt(jnp.sum(cq * cq, axis=1, keepdims=True) + EPS)
    rk = lax.rsqrt(jnp.sum(ck * ck, axis=1, keepdims=True) + EPS)
    nq = cq * rq
    kn = ck * rk
    qn = nq * (B_DIM ** -0.5)

    def c3(a):
        return a.reshape(ncb, CHUNK, a.shape[-1])

    qn3, kn3, v3, beta3 = c3(qn), c3(kn), c3(cv), c3(beta)
    gb = jnp.broadcast_to(c3(g), (ncb, CHUNK, CHUNK))
    gc_b = _bnn_exact(low.astype(BF16), gb)
    gr_b = _bnn_exact(_bf(ones), jnp.where(eye, gc_b, 0.0))
    dm = jnp.where(low, jnp.exp(jnp.where(low, gc_b - gr_b, 0.0)), 0.0)
    gc = gc_b[:, :, 0:1]
    gl = gc_b[:, CHUNK - 1:CHUNK, 0:1]
    gam = jnp.exp(gc)
    egl = jnp.exp(gl)
    edec = jnp.exp(gl - gc)

    knb = _bf(kn3)
    kk = _bnt(knb, knb)
    kd = jnp.where(strict, kk * dm, 0.0)
    a = beta3 * kd
    tm = eye.astype(F32)
    sz = 1
    while sz < CHUNK:
        off = jnp.where(((i // (2 * sz)) == (j // (2 * sz))) & ((i // sz) != (j // sz)), a, 0.0)
        tmb = _bf(tm)
        tm = tm - _bnn(_bf(_bnn(tmb, _bf(off))), tmb)
        sz *= 2
    bv = beta3 * v3
    bk = (beta3 * gam) * kn3
    sol = _bnn3(_split(tm), jnp.concatenate([bv, bk], axis=2))
    u, wk = sol[:, :, :B_DIM], sol[:, :, B_DIM:]
    qk = _bnt(_bf(qn3), knb)
    p = jnp.where(low, qk * dm, 0.0)
    kdec = kn3 * edec
    qg = gam * qn3
    return dict(beta=beta3, g=c3(g), ea=ea, sp_arg=c3(sp_arg), rq=c3(rq), rk=c3(rk), nq=c3(nq),
                qn=qn3, kn=kn3, v=v3, gc=gc, gl=gl, gam=gam, egl=egl, edec=edec, dm=dm, kd=kd, a=a,
                tm=tm, u=u, wk=wk, qk=qk, p=p, kdec=kdec, qg=qg, eye=eye, low=low, strict=strict)


def dn_prep(c, proj, al_row, dtb_row, b, s, ncb=8):
    t = b * s
    r = ncb * CHUNK
    nblk = t // r
    bd_blk = P_BD // 128

    def body(cq_ref, ck_ref, cv_ref, bd_ref, al_ref, dtb_ref, u_ref, wk_ref, qg_ref, kdec_ref, p_ref, egl_ref):
        h = pl.program_id(1)
        m = _dn_chunk_math(cq_ref[...], ck_ref[...], cv_ref[...], bd_ref[...].astype(F32), al_ref[...], dtb_ref[...], h, ncb)
        u_ref[...] = m["u"].reshape(r, B_DIM)
        wk_ref[...] = m["wk"].reshape(r, B_DIM)
        qg_ref[...] = m["qg"].reshape(r, B_DIM)
        kdec_ref[...] = m["kdec"].reshape(r, B_DIM)
        p_ref[...] = m["p"].reshape(r, CHUNK)
        egl_ref[...] = jnp.broadcast_to(m["egl"], (ncb, 8, 128)).reshape(ncb * 8, 128)

    col = lambda k: pl.BlockSpec((r, 128), lambda i, h: (i, k * B_HEADS + h))
    out_col = pl.BlockSpec((r, 128), lambda i, h: (i, h))
    small = pl.BlockSpec((1, 128), lambda i, h: (0, 0))
    return pl.pallas_call(
        body, name="dn_prep", grid=(nblk, B_HEADS),
        in_specs=[col(0), col(1), col(2), pl.BlockSpec((r, 128), lambda i, h: (i, bd_blk)), small, small],
        out_specs=[out_col, out_col, out_col, out_col,
                   pl.BlockSpec((None, r, CHUNK), lambda i, h: (h, i, 0)),
                   pl.BlockSpec((None, ncb * 8, 128), lambda i, h: (h, i, 0))],
        out_shape=[jax.ShapeDtypeStruct((t, B_WIDTH), F32)] * 4
        + [jax.ShapeDtypeStruct((B_HEADS, t, CHUNK), F32),
           jax.ShapeDtypeStruct((B_HEADS, t // 8, 128), F32)],
        compiler_params=_cp(("parallel", "parallel"), VMEM_LIMIT),
    )(c, c, c, proj, al_row, dtb_row)


def dn_scan_fwd(u, wk, qg, kdec, p, egl, b, s):
    t = b * s
    nc = s // CHUNK

    def body(u_ref, wk_ref, qg_ref, kdec_ref, p_ref, egl_ref, o_ref, ss_ref, st):
        @pl.when(pl.program_id(0) == 0)
        def _():
            st[...] = jnp.zeros_like(st)

        for bb in range(b):
            for h in range(B_HEADS):
                sl = slice(h * B_DIM, (h + 1) * B_DIM)
                sh = st[bb * B_HEADS + h]
                ss_ref[bb, h] = sh
                sb = _bf(sh)
                w = u_ref[bb, :, sl] - _nt(_bf(wk_ref[bb, :, sl]), sb)
                o_ref[bb, :, sl] = _nt(_bf(qg_ref[bb, :, sl]), sb) + _nn(_bf(p_ref[h, bb]), _bf(w))
                st[bb * B_HEADS + h] = egl_ref[h, bb][0:1, :] * sh + _tn(_bf(w), _bf(kdec_ref[bb, :, sl]))

    r3 = lambda a: a.reshape(b, s, B_WIDTH)
    act = pl.BlockSpec((b, CHUNK, B_WIDTH), lambda n: (0, n, 0))
    o, states = pl.pallas_call(
        body, name="dn_scan_fwd", grid=(nc,),
        in_specs=[act, act, act, act,
                  pl.BlockSpec((B_HEADS, b, CHUNK, CHUNK), lambda n: (0, 0, n, 0)),
                  pl.BlockSpec((B_HEADS, b, 8, 128), lambda n: (0, 0, n, 0))],
        out_specs=[act, pl.BlockSpec((b, None, B_HEADS, B_DIM, B_DIM), lambda n: (0, n, 0, 0, 0))],
        out_shape=[jax.ShapeDtypeStruct((b, s, B_WIDTH), F32),
                   jax.ShapeDtypeStruct((b, nc, B_HEADS, B_DIM, B_DIM), F32)],
        scratch_shapes=[pltpu.VMEM((b * B_HEADS, B_DIM, B_DIM), F32)],
        compiler_params=_cp(("arbitrary",), VMEM_LIMIT),
    )(r3(u), r3(wk), r3(qg), r3(kdec), p.reshape(B_HEADS, b, s, CHUNK), egl.reshape(B_HEADS, b, s // 8, 128))
    return o.reshape(t, B_WIDTH), states


def dn_scan_bwd(u, wk, qg, kdec, p, egl, states, do, b, s):
    t = b * s
    nc = s // CHUNK

    def body(u_ref, wk_ref, qg_ref, kdec_ref, p_ref, egl_ref, ss_ref, do_ref,
             dw_ref, dwk_ref, dqg_ref, dkdec_ref, dp_ref, degl_ref, dst):
        @pl.when(pl.program_id(0) == 0)
        def _():
            dst[...] = jnp.zeros_like(dst)

        for bb in range(b):
            for h in range(B_HEADS):
                sl = slice(h * B_DIM, (h + 1) * B_DIM)
                k = bb * B_HEADS + h
                sh = ss_ref[bb, h]
                sb = _bf(sh)
                dsp = dst[k]
                dsb = _bf(dsp)
                wkb = _bf(wk_ref[bb, :, sl])
                kdb = _bf(kdec_ref[bb, :, sl])
                pb = _bf(p_ref[h, bb])
                dob = _bf(do_ref[bb, :, sl])
                w = u_ref[bb, :, sl] - _nt(wkb, sb)
                wb = _bf(w)
                dw = _tn(pb, dob) + _nt(kdb, dsb)
                dwb = _bf(dw)
                dw_ref[bb, :, sl] = dw
                dqg_ref[bb, :, sl] = _nn(dob, sb)
                dwk_ref[bb, :, sl] = -_nn(dwb, sb)
                dkdec_ref[bb, :, sl] = _nn(wb, dsb)
                dp_ref[h, bb] = _nt(dob, wb)
                tot = jnp.sum(jnp.sum(sh * dsp, axis=1, keepdims=True), axis=0, keepdims=True)
                degl_ref[h, bb] = jnp.broadcast_to(tot, (8, 128))
                dst[k] = egl_ref[h, bb][0:1, :] * dsp + _tn(dob, _bf(qg_ref[bb, :, sl])) - _tn(dwb, wkb)

    r3 = lambda a: a.reshape(b, s, B_WIDTH)
    act = pl.BlockSpec((b, CHUNK, B_WIDTH), lambda n: (0, nc - 1 - n, 0))
    pspec = pl.BlockSpec((B_HEADS, b, CHUNK, CHUNK), lambda n: (0, 0, nc - 1 - n, 0))
    espec = pl.BlockSpec((B_HEADS, b, 8, 128), lambda n: (0, 0, nc - 1 - n, 0))
    outs = pl.pallas_call(
        body, name="dn_scan_bwd", grid=(nc,),
        in_specs=[act, act, act, act, pspec, espec,
                  pl.BlockSpec((b, None, B_HEADS, B_DIM, B_DIM), lambda n: (0, nc - 1 - n, 0, 0, 0)),
                  act],
        out_specs=[act, act, act, act, pspec, espec],
        out_shape=[jax.ShapeDtypeStruct((b, s, B_WIDTH), F32)] * 4
        + [jax.ShapeDtypeStruct((B_HEADS, b, s, CHUNK), F32),
           jax.ShapeDtypeStruct((B_HEADS, b, s // 8, 128), F32)],
        scratch_shapes=[pltpu.VMEM((b * B_HEADS, B_DIM, B_DIM), F32)],
        compiler_params=_cp(("arbitrary",), VMEM_LIMIT),
    )(r3(u), r3(wk), r3(qg), r3(kdec), p.reshape(B_HEADS, b, s, CHUNK), egl.reshape(B_HEADS, b, s // 8, 128),
      states, r3(do))
    return (*[a.reshape(t, B_WIDTH) for a in outs[:4]], outs[4].reshape(B_HEADS, t, CHUNK),
            outs[5].reshape(B_HEADS, t // 8, 128))


def dn_post_bwd(c, proj, al_row, dtb_row, dw, dwk, dqg, dkdec, dp, degl, b, s, ncb=8):
    t = b * s
    r = ncb * CHUNK
    nblk = t // r
    bd_blk = P_BD // 128

    def body(cq_ref, ck_ref, cv_ref, bd_ref, al_ref, dtb_ref, dw_ref, dwk_ref, dqg_ref, dkdec_ref, dp_ref,
             degl_ref, dc_ref, dbd_ref, dal_ref, ddtb_ref):
        h = pl.program_id(1)

        @pl.when((pl.program_id(0) == 0) & (h == 0))
        def _():
            dal_ref[...] = jnp.zeros_like(dal_ref)
            ddtb_ref[...] = jnp.zeros_like(ddtb_ref)

        m = _dn_chunk_math(cq_ref[...], ck_ref[...], cv_ref[...], bd_ref[...].astype(F32), al_ref[...], dtb_ref[...], h, ncb)
        eye, low, strict = m["eye"], m["low"], m["strict"]
        eyef = eye.astype(F32)

        def c3(a):
            return a.reshape(ncb, CHUNK, a.shape[-1])

        du, dwkv, dqg, dkdec = c3(dw_ref[...]), c3(dwk_ref[...]), c3(dqg_ref[...]), c3(dkdec_ref[...])
        dpm = jnp.where(low, c3(dp_ref[...]), 0.0)
        degl = degl_ref[...].reshape(ncb, 8, 128)[:, 0:1, 0:1]
        beta, gam, kn, qn, v = m["beta"], m["gam"], m["kn"], m["qn"], m["v"]
        dm, kd, a, p = m["dm"], m["kd"], m["a"], m["p"]
        knb, qnb = _bf(kn), _bf(qn)

        eyeb = _bf(eyef)
        th, tl = _split(m["tm"])
        tts = (_bf(_bnt(eyeb, th)), _bf(_bnt(eyeb, tl)))
        xy = _bnn3(tts, jnp.concatenate([du, dwkv], axis=2))
        x, y = xy[:, :, :B_DIM], xy[:, :, B_DIM:]
        da = -jnp.where(strict, _bnt(_bf(x), _bf(m["u"])) + _bnt(_bf(y), _bf(m["wk"])), 0.0)
        dv = beta * x
        sy = jnp.sum(y * kn, axis=2, keepdims=True)
        dbeta = jnp.sum(x * v, axis=2, keepdims=True) + gam * sy + jnp.sum(da * kd, axis=2, keepdims=True)
        dgam = beta * sy + jnp.sum(dqg * qn, axis=2, keepdims=True)
        dkk = da * beta * dm
        dqk = dpm * dm
        dkkb, dqkb = _bf(dkk), _bf(dqk)
        dkn = ((beta * gam) * y + _bnn(dkkb, knb) + _bnn(_bf(_bnt(eyeb, dkkb)), knb)
               + _bnn(_bf(_bnt(eyeb, dqkb)), qnb) + dkdec * m["edec"])
        dqn = gam * dqg + _bnn(dqkb, knb)
        mm = da * a + dpm * p
        ek = jnp.sum(dkdec * m["kdec"], axis=2, keepdims=True)
        dgc = (jnp.sum(mm, axis=2, keepdims=True) - _col_of_row(jnp.sum(mm, axis=1, keepdims=True), eye)
               + dgam * gam - ek)
        dgl = jnp.sum(ek, axis=1, keepdims=True) + degl * m["egl"]
        i, _ = _chunk_masks(ncb)
        dgc = dgc + jnp.where(i[:, :, 0:1] == CHUNK - 1, dgl, 0.0)
        upper = (i <= _chunk_masks(ncb)[1]).astype(BF16)
        dg = _bnn_exact(upper, jnp.broadcast_to(dgc, (ncb, CHUNK, CHUNK)))[:, :, 0:1]

        nq = m["nq"]
        dnq = dqn * (B_DIM ** -0.5)
        dcq = m["rq"] * (dnq - nq * jnp.sum(nq * dnq, axis=2, keepdims=True))
        dck = m["rk"] * (dkn - kn * jnp.sum(kn * dkn, axis=2, keepdims=True))
        dc_ref[0] = dcq.reshape(r, B_DIM)
        dc_ref[1] = dck.reshape(r, B_DIM)
        dc_ref[2] = dv.reshape(r, B_DIM)

        dbraw = (dbeta * beta * (1.0 - beta)).reshape(r, 1)
        sgm = _sigmoid(m["sp_arg"])
        ddraw3 = dg * (-m["ea"]) * sgm
        ddraw = ddraw3.reshape(r, 1)
        lane = lax.broadcasted_iota(jnp.int32, (r, 128), 1)
        contrib = jnp.where(lane == h, dbraw, 0.0) + jnp.where(lane == B_HEADS + h, ddraw, 0.0)

        @pl.when(h == 0)
        def _():
            dbd_ref[...] = contrib

        @pl.when(h != 0)
        def _():
            dbd_ref[...] += contrib

        lane8 = lax.broadcasted_iota(jnp.int32, (8, 128), 1)
        tot_al = jnp.sum(jnp.sum(dg * m["g"], axis=1, keepdims=True), axis=0, keepdims=True).reshape(1, 1)
        tot_dtb = jnp.sum(jnp.sum(ddraw3, axis=1, keepdims=True), axis=0, keepdims=True).reshape(1, 1)
        dal_ref[...] += jnp.where(lane8 == h, tot_al, 0.0)
        ddtb_ref[...] += jnp.where(lane8 == h, tot_dtb, 0.0)

    col = lambda k: pl.BlockSpec((r, 128), lambda i, h: (i, k * B_HEADS + h))
    hcol = pl.BlockSpec((r, 128), lambda i, h: (i, h))
    small = pl.BlockSpec((1, 128), lambda i, h: (0, 0))
    acc = pl.BlockSpec((8, 128), lambda i, h: (0, 0))
    return pl.pallas_call(
        body, name="dn_post_bwd", grid=(nblk, B_HEADS),
        in_specs=[col(0), col(1), col(2), pl.BlockSpec((r, 128), lambda i, h: (i, bd_blk)), small, small,
                  hcol, hcol, hcol, hcol,
                  pl.BlockSpec((None, r, CHUNK), lambda i, h: (h, i, 0)),
                  pl.BlockSpec((None, ncb * 8, 128), lambda i, h: (h, i, 0))],
        out_specs=[pl.BlockSpec((3, r, 128), lambda i, h: (0, i, h)),
                   pl.BlockSpec((r, 128), lambda i, h: (i, 0)), acc, acc],
        out_shape=[jax.ShapeDtypeStruct((3, t, B_WIDTH), F32), jax.ShapeDtypeStruct((t, 128), F32),
                   jax.ShapeDtypeStruct((8, 128), F32), jax.ShapeDtypeStruct((8, 128), F32)],
        compiler_params=_cp(("arbitrary", "arbitrary"), VMEM_LIMIT),
    )(c, c, c, proj, al_row, dtb_row, dw, dwk, dqg, dkdec, dp, degl)


def make_bias_band(rel_bias):
    tail = bias_tail(jnp.pad(rel_bias, ((0, 0), (0, 384 - N_REL))))
    far = jnp.broadcast_to(rel_bias[:, 2 * REL_CLIP][:, None, None], (A_HEADS, CHUNK, BAND - TAIL))
    return jnp.concatenate([far, jnp.transpose(tail, (1, 0, 2))], axis=2).reshape(4, 2 * CHUNK, BAND)


def _rms(x):
    r = lax.rsqrt(jnp.mean(x * x, axis=-1, keepdims=True) + EPS)
    return r, x * r


def _rms_bwd(dh, g, r, n):
    dn = dh * g
    return r * (dn - n * jnp.mean(dn * n, axis=-1, keepdims=True)), dh * n


def _gated_onorm(o, z, w_on):
    parts = []
    for h in range(B_HEADS):
        sl = slice(h * B_DIM, (h + 1) * B_DIM)
        r, n = _rms(o[:, sl])
        parts.append((r, n))
    r4 = [p[0] for p in parts]
    n4 = jnp.concatenate([p[1] for p in parts], axis=1)
    w4 = jnp.concatenate([w_on] * B_HEADS, axis=1)
    sz = _sigmoid(z)
    silu = z * sz
    return n4 * w4 * silu, r4, n4, w4, sz, silu


def mid_fwd(x, y_a, o_b, proj, w_on, wa, wb, w_out, tm=256):
    t = x.shape[0]
    tm = min(tm, t)

    def body(x_ref, ya_ref, ob_ref, z_ref, ga_ref, gb_ref, won_ref, wa_ref, wb_ref, wo_ref, x1_ref, mg_ref):
        yb = _gated_onorm(ob_ref[...], z_ref[...].astype(F32), won_ref[...])[0]
        ua = _nn(_bf(ya_ref[...]), wa_ref[...])
        ub = _nn(_bf(yb), wb_ref[...])
        merged = _sigmoid(ga_ref[...].astype(F32)) * ua + _sigmoid(gb_ref[...].astype(F32)) * ub
        mb = _bf(merged)
        mg_ref[...] = mb
        x1_ref[...] = x_ref[...] + _nn(mb, wo_ref[...])

    rowd = pl.BlockSpec((tm, D_MODEL), lambda i: (i, 0))
    row5 = pl.BlockSpec((tm, 512), lambda i: (i, 0))
    full = lambda a: pl.BlockSpec(a.shape, lambda i: (0,) * a.ndim)
    return pl.pallas_call(
        body, name="mid_fwd", grid=(t // tm,),
        in_specs=[rowd, row5, row5,
                  pl.BlockSpec((tm, 512), lambda i: (i, P_Z // 512)),
                  pl.BlockSpec((tm, D_MODEL), lambda i: (i, 0)),
                  pl.BlockSpec((tm, D_MODEL), lambda i: (i, 1)),
                  full(w_on), full(wa), full(wb), full(w_out)],
        out_specs=[rowd, rowd],
        out_shape=[jax.ShapeDtypeStruct((t, D_MODEL), F32), jax.ShapeDtypeStruct((t, D_MODEL), BF16)],
        compiler_params=_cp(("parallel",), VMEM_LIMIT),
    )(x, y_a, o_b, proj, proj, proj, w_on, wa, wb, w_out)


def mid_bwd(dx1, merged, y_a, o_b, proj, w_on, wa, wb, w_out, tm=256):
    t = dx1.shape[0]
    tm = min(tm, t)

    def body(dx1_ref, mg_ref, ya_ref, ob_ref, z_ref, ga_ref, gb_ref, won_ref, wa_ref, wb_ref, wo_ref,
             dya_ref, dob_ref, dz_ref, dg_ref, dwo_ref, dwa_ref, dwb_ref, dwon_ref):
        @pl.when(pl.program_id(0) == 0)
        def _():
            dwo_ref[...] = jnp.zeros_like(dwo_ref)
            dwa_ref[...] = jnp.zeros_like(dwa_ref)
            dwb_ref[...] = jnp.zeros_like(dwb_ref)
            dwon_ref[...] = jnp.zeros_like(dwon_ref)

        dx1b = _bf(dx1_ref[...])
        dmerged = _nt(dx1b, wo_ref[...])
        dwo_ref[...] += _tn(mg_ref[...], dx1b)
        o = ob_ref[...]
        z = z_ref[...].astype(F32)
        yb, r4, n4, w4, sz, silu = _gated_onorm(o, z, won_ref[...])
        yab, ybb = _bf(ya_ref[...]), _bf(yb)
        ua = _nn(yab, wa_ref[...])
        ub = _nn(ybb, wb_ref[...])
        sa, sb = _sigmoid(ga_ref[...].astype(F32)), _sigmoid(gb_ref[...].astype(F32))
        dua, dub = _bf(dmerged * sa), _bf(dmerged * sb)
        dg_ref[:, 0:D_MODEL] = _bf(dmerged * ua * sa * (1.0 - sa))
        dg_ref[:, D_MODEL:2 * D_MODEL] = _bf(dmerged * ub * sb * (1.0 - sb))
        dwa_ref[...] += _tn(yab, dua)
        dwb_ref[...] += _tn(ybb, dub)
        dya_ref[...] = _nt(dua, wa_ref[...])
        dyb = _nt(dub, wb_ref[...])
        dz_ref[...] = _bf(dyb * (n4 * w4) * (sz * (1.0 + z * (1.0 - sz))))
        dnw = dyb * silu
        dwon = jnp.zeros((1, B_DIM), F32)
        for h in range(B_HEADS):
            sl = slice(h * B_DIM, (h + 1) * B_DIM)
            dxh, dgh = _rms_bwd(dnw[:, sl], won_ref[...], r4[h], n4[:, sl])
            dob_ref[:, sl] = dxh
            dwon = dwon + jnp.sum(dgh, axis=0, keepdims=True)
        dwon_ref[...] += jnp.broadcast_to(dwon, (8, B_DIM))

    rowd = pl.BlockSpec((tm, D_MODEL), lambda i: (i, 0))
    row5 = pl.BlockSpec((tm, 512), lambda i: (i, 0))
    full = lambda a: pl.BlockSpec(a.shape, lambda i: (0,) * a.ndim)
    fixed = lambda shp: pl.BlockSpec(shp, lambda i: (0,) * len(shp))
    return pl.pallas_call(
        body, name="mid_bwd", grid=(t // tm,),
        in_specs=[rowd, rowd, row5, row5,
                  pl.BlockSpec((tm, 512), lambda i: (i, P_Z // 512)),
                  pl.BlockSpec((tm, D_MODEL), lambda i: (i, 0)),
                  pl.BlockSpec((tm, D_MODEL), lambda i: (i, 1)),
                  full(w_on), full(wa), full(wb), full(w_out)],
        out_specs=[row5, row5, row5, pl.BlockSpec((tm, 2 * D_MODEL), lambda i: (i, 0)),
                   fixed((D_MODEL, D_MODEL)), fixed((A_WIDTH, D_MODEL)), fixed((B_WIDTH, D_MODEL)),
                   fixed((8, B_DIM))],
        out_shape=[jax.ShapeDtypeStruct((t, 512), F32), jax.ShapeDtypeStruct((t, 512), F32),
                   jax.ShapeDtypeStruct((t, 512), BF16), jax.ShapeDtypeStruct((t, 2 * D_MODEL), BF16),
           jax.ShapeDtypeStruct((D_MODEL, D_MODEL), F32), jax.ShapeDtypeStruct((A_WIDTH, D_MODEL), F32),
           jax.ShapeDtypeStruct((B_WIDTH, D_MODEL), F32), jax.ShapeDtypeStruct((8, B_DIM), F32)],
        compiler_params=_cp(("arbitrary",), VMEM_LIMIT),
    )(dx1, merged, y_a, o_b, proj, proj, proj, w_on, wa, wb, w_out)


def ffn_fwd(x1, g, w_gu, w_down, tm=512, tf=256):
    t = x1.shape[0]
    tm = min(tm, t)
    nf = D_FF // tf

    def body(x_ref, g_ref, wg_ref, wu_ref, wd_ref, x2_ref, gate_ref, up_ref, h_ref):
        j = pl.program_id(1)

        @pl.when(j == 0)
        def _():
            xv = x_ref[...]
            r, n = _rms(xv)
            h_ref[...] = _bf(n * g_ref[...])
            x2_ref[...] = xv

        hb = h_ref[...]
        gate = _nn(hb, wg_ref[...])
        up = _nn(hb, wu_ref[...])
        gate_ref[...] = _bf(gate)
        up_ref[...] = _bf(up)
        act = gate * _sigmoid(gate) * up
        x2_ref[...] += _nn(_bf(act), wd_ref[...])

    return pl.pallas_call(
        body, name="ffn_fwd", grid=(t // tm, nf),
        in_specs=[pl.BlockSpec((tm, D_MODEL), lambda i, j: (i, 0)),
                  pl.BlockSpec((1, D_MODEL), lambda i, j: (0, 0)),
                  pl.BlockSpec((D_MODEL, tf), lambda i, j: (0, j)),
                  pl.BlockSpec((D_MODEL, tf), lambda i, j: (0, nf + j)),
                  pl.BlockSpec((tf, D_MODEL), lambda i, j: (j, 0))],
        out_specs=[pl.BlockSpec((tm, D_MODEL), lambda i, j: (i, 0)),
                   pl.BlockSpec((tm, tf), lambda i, j: (i, j)),
                   pl.BlockSpec((tm, tf), lambda i, j: (i, j)),
                   pl.BlockSpec((tm, D_MODEL), lambda i, j: (i, 0))],
        out_shape=[jax.ShapeDtypeStruct((t, D_MODEL), F32), jax.ShapeDtypeStruct((t, D_FF), BF16),
                   jax.ShapeDtypeStruct((t, D_FF), BF16), jax.ShapeDtypeStruct((t, D_MODEL), BF16)],
        compiler_params=_cp(("parallel", "arbitrary"), VMEM_LIMIT),
    )(x1, g, w_gu, w_gu, w_down)


def ffn_bwd(dx2, x1, g, gate, up, w_gu, w_down, tm=512, tf=256):
    t = x1.shape[0]
    tm = min(tm, t)
    nf = D_FF // tf

    def body(dx2_ref, x_ref, g_ref, gate_ref, up_ref, wg_ref, wu_ref, wd_ref,
             dx1_ref, act_ref, dgate_ref, dup_ref, dg_ref, dh_acc, dx2b_ref):
        i, j = pl.program_id(0), pl.program_id(1)

        @pl.when((i == 0) & (j == 0))
        def _():
            dg_ref[...] = jnp.zeros_like(dg_ref)

        @pl.when(j == 0)
        def _():
            dx2b_ref[...] = _bf(dx2_ref[...])
            dh_acc[...] = jnp.zeros_like(dh_acc)

        dact = _nt(dx2b_ref[...], wd_ref[...])
        gt, upv = gate_ref[...].astype(F32), up_ref[...].astype(F32)
        sg = _sigmoid(gt)
        silu = gt * sg
        act_ref[...] = _bf(silu * upv)
        dgt = _bf(dact * upv * (sg * (1.0 + gt * (1.0 - sg))))
        dupv = _bf(dact * silu)
        dgate_ref[...] = dgt
        dup_ref[...] = dupv
        dh_acc[...] += _nt(dgt, wg_ref[...]) + _nt(dupv, wu_ref[...])

        @pl.when(j == nf - 1)
        def _():
            r, n = _rms(x_ref[...])
            dx, dgc = _rms_bwd(dh_acc[...], g_ref[...], r, n)
            dx1_ref[...] = dx2_ref[...] + dx
            dg_ref[...] += jnp.broadcast_to(jnp.sum(dgc, axis=0, keepdims=True), (8, D_MODEL))

    rowd = pl.BlockSpec((tm, D_MODEL), lambda i, j: (i, 0))
    ff = pl.BlockSpec((tm, tf), lambda i, j: (i, j))
    return pl.pallas_call(
        body, name="ffn_bwd", grid=(t // tm, nf),
        in_specs=[rowd, rowd, pl.BlockSpec((1, D_MODEL), lambda i, j: (0, 0)), ff, ff,
                  pl.BlockSpec((D_MODEL, tf), lambda i, j: (0, j)),
                  pl.BlockSpec((D_MODEL, tf), lambda i, j: (0, nf + j)),
                  pl.BlockSpec((tf, D_MODEL), lambda i, j: (j, 0))],
        out_specs=[rowd, ff, ff, ff, pl.BlockSpec((8, D_MODEL), lambda i, j: (0, 0))],
        out_shape=[jax.ShapeDtypeStruct((t, D_MODEL), F32), jax.ShapeDtypeStruct((t, D_FF), BF16),
                   jax.ShapeDtypeStruct((t, D_FF), BF16), jax.ShapeDtypeStruct((t, D_FF), BF16),
                   jax.ShapeDtypeStruct((8, D_MODEL), F32)],
        scratch_shapes=[pltpu.VMEM((tm, D_MODEL), F32), pltpu.VMEM((tm, D_MODEL), BF16)],
        compiler_params=_cp(("arbitrary", "arbitrary"), VMEM_LIMIT),
    )(dx2, x1, g, gate, up, w_gu, w_gu, w_down)


def tail_fwd_bwd(x2, p, target, g_ple, g_final, w_pg, w_pp, tm=256):
    t = x2.shape[0]
    tm = min(tm, t)

    def body(x_ref, p_ref, t_ref, gp_ref, gf_ref, wpg_ref, wpp_ref,
             dx_ref, dwpg_ref, dwpp_ref, dgp_ref, dgf_ref, loss_ref):
        @pl.when(pl.program_id(0) == 0)
        def _():
            dwpg_ref[...] = jnp.zeros_like(dwpg_ref)
            dwpp_ref[...] = jnp.zeros_like(dwpp_ref)
            dgp_ref[...] = jnp.zeros_like(dgp_ref)
            dgf_ref[...] = jnp.zeros_like(dgf_ref)
            loss_ref[...] = jnp.zeros_like(loss_ref)

        x2v = x_ref[...]
        gp, gf = gp_ref[...], gf_ref[...]
        r3, n3 = _rms(x2v)
        h3b = _bf(n3 * gp)
        pb = _bf(p_ref[...])
        pg = _sigmoid(_nn(h3b, wpg_ref[...]))
        pp = _nn(pb, wpp_ref[...])
        x3 = x2v + pg * pp
        r4, n4 = _rms(x3)
        err = n4 * gf - t_ref[...]
        part = 0.5 * jnp.sum(jnp.sum(err * err, axis=1, keepdims=True), axis=0, keepdims=True) / D_MODEL
        loss_ref[...] += jnp.broadcast_to(part, (8, 128))
        dy = err * (1.0 / D_MODEL)
        dx3, dgf = _rms_bwd(dy, gf, r4, n4)
        dgf_ref[...] += jnp.broadcast_to(jnp.sum(dgf, axis=0, keepdims=True), (8, D_MODEL))
        dzp = _bf(dx3 * pp * pg * (1.0 - pg))
        dpp = _bf(dx3 * pg)
        dwpg_ref[...] += _tn(h3b, dzp)
        dwpp_ref[...] += _tn(pb, dpp)
        dh3 = _nt(dzp, wpg_ref[...])
        dx, dgp = _rms_bwd(dh3, gp, r3, n3)
        dgp_ref[...] += jnp.broadcast_to(jnp.sum(dgp, axis=0, keepdims=True), (8, D_MODEL))
        dx_ref[...] = dx3 + dx

    rowd = pl.BlockSpec((tm, D_MODEL), lambda i: (i, 0))
    fixed = lambda shp: pl.BlockSpec(shp, lambda i: (0,) * len(shp))
    return pl.pallas_call(
        body, name="tail_fwd_bwd", grid=(t // tm,),
        in_specs=[rowd, pl.BlockSpec((tm, PLE_DIM), lambda i: (i, 0)), rowd,
                  fixed((1, D_MODEL)), fixed((1, D_MODEL)), fixed((D_MODEL, D_MODEL)), fixed((PLE_DIM, D_MODEL))],
        out_specs=[rowd, fixed((D_MODEL, D_MODEL)), fixed((PLE_DIM, D_MODEL)),
                   fixed((8, D_MODEL)), fixed((8, D_MODEL)), fixed((8, 128))],
        out_shape=[jax.ShapeDtypeStruct((t, D_MODEL), F32), jax.ShapeDtypeStruct((D_MODEL, D_MODEL), F32),
                   jax.ShapeDtypeStruct((PLE_DIM, D_MODEL), F32), jax.ShapeDtypeStruct((8, D_MODEL), F32),
                   jax.ShapeDtypeStruct((8, D_MODEL), F32), jax.ShapeDtypeStruct((8, 128), F32)],
        compiler_params=_cp(("arbitrary",), VMEM_LIMIT),
    )(x2, p, target, g_ple, g_final, w_pg, w_pp)


def in_proj_bwd(pieces, weights, x, dx1, g, tm=256):
    t = x.shape[0]
    tm = min(tm, t)
    k = len(pieces)

    def body(*refs):
        p_refs, w_refs = refs[:k], refs[k:2 * k]
        x_ref, dx1_ref, g_ref, dx_ref, dg_ref = refs[2 * k:]

        @pl.when(pl.program_id(0) == 0)
        def _():
            dg_ref[...] = jnp.zeros_like(dg_ref)

        dh = _nt(_bf(p_refs[0][...]), w_refs[0][...])
        for pr, wr in zip(p_refs[1:], w_refs[1:]):
            dh = dh + _nt(_bf(pr[...]), wr[...])
        r, n = _rms(x_ref[...])
        dx, dgc = _rms_bwd(dh, g_ref[...], r, n)
        dx_ref[...] = dx1_ref[...] + dx
        dg_ref[...] += jnp.broadcast_to(jnp.sum(dgc, axis=0, keepdims=True), (8, D_MODEL))

    rowd = pl.BlockSpec((tm, D_MODEL), lambda i: (i, 0))
    return pl.pallas_call(
        body, name="in_proj_bwd", grid=(t // tm,),
        in_specs=[pl.BlockSpec((tm, a.shape[1]), lambda i: (i, 0)) for a in pieces]
        + [pl.BlockSpec(w.shape, lambda i: (0, 0)) for w in weights]
        + [rowd, rowd, pl.BlockSpec((1, D_MODEL), lambda i: (0, 0))],
        out_specs=[rowd, pl.BlockSpec((8, D_MODEL), lambda i: (0, 0))],
        out_shape=[jax.ShapeDtypeStruct((t, D_MODEL), F32), jax.ShapeDtypeStruct((8, D_MODEL), F32)],
        compiler_params=_cp(("arbitrary",), VMEM_LIMIT),
    )(*pieces, *weights, x, dx1, g)


def adamw(w, g, m, v, name, rows_cap=256):
    r, c = w.shape
    tr = r
    for cand in range(8, min(r, rows_cap) + 1, 8):
        if r % cand == 0:
            tr = cand

    def body(w_ref, g_ref, m_ref, v_ref, d_ref, mo_ref, vo_ref):
        gv = g_ref[...]
        mn = ADAM_B1 * m_ref[...] + (1.0 - ADAM_B1) * gv
        vn = ADAM_B2 * v_ref[...] + (1.0 - ADAM_B2) * (gv * gv)
        m_hat = mn / (1.0 - ADAM_B1 ** ADAM_STEP)
        v_hat = vn / (1.0 - ADAM_B2 ** ADAM_STEP)
        d_ref[...] = -ADAM_LR * (m_hat / (jnp.sqrt(v_hat) + ADAM_EPS) + ADAM_WD * w_ref[...])
        mo_ref[...] = mn
        vo_ref[...] = vn

    spec = pl.BlockSpec((tr, c), lambda i: (i, 0))
    return pl.pallas_call(
        body, name=name, grid=(r // tr,),
        in_specs=[spec] * 4, out_specs=[spec] * 3,
        out_shape=[jax.ShapeDtypeStruct((r, c), F32)] * 3,
        compiler_params=_cp(("parallel",), VMEM_LIMIT),
    )(w, g, m, v)


class Standalone:
    def __init__(self, later_weights):
        self.later_weights = later_weights

    def begin(self, *a):
        return 0.0

    forward = exchange = join = begin

    def finish(self, after):
        return self.later_weights


def local_step(x3d, p3d, target3d, g4, small, later, early):
    b, s, _ = x3d.shape
    t = b * s
    x = x3d.reshape(t, D_MODEL)
    p = p3d.reshape(t, PLE_DIM)
    target = target3d.reshape(t, D_MODEL)
    cut = SPLIT_Z - 2 * (D_IN // N_CHIPS)
    w_inp = jnp.concatenate([g4[2][:, cut + 8:], g4[3], g4[0], g4[1], g4[2][:, :cut], g4[2][:, cut:cut + 8],
                             jnp.zeros((D_MODEL, 120), BF16)], axis=1)
    al_row = jnp.pad(small["a_log"].reshape(1, B_HEADS), ((0, 0), (0, 128 - B_HEADS)))
    dtb_row = jnp.pad(small["dt_bias"].reshape(1, B_HEADS), ((0, 0), (0, 128 - B_HEADS)))
    conv_w8 = jnp.pad(small["conv_w"].reshape(CONV_K, CONV_CH), ((0, 8 - CONV_K), (0, 0)))
    w_on = small["w_onorm"].reshape(1, B_DIM)
    g_mix, g_ffn = small["g_mix"].reshape(1, D_MODEL), small["g_ffn"].reshape(1, D_MODEL)
    g_ple, g_final = small["g_ple"].reshape(1, D_MODEL), small["g_final"].reshape(1, D_MODEL)
    bias_band = make_bias_band(small["rel_bias"].reshape(A_HEADS, N_REL))

    tok = later.begin()
    proj, h1 = rms_matmul(x, g_mix + tok, w_inp, "in_proj")
    y_a, lse = attn_fwd(proj, bias_band, b, s)
    tok = later.forward(lse)
    c = conv_fwd(proj, conv_w8 + tok, b, s)
    u, wk, qg, kdec, pm, egl = dn_prep(c, proj, al_row, dtb_row, b, s)
    o_b, states = dn_scan_fwd(u, wk, qg, kdec, pm, egl, b, s)
    wts = later.finish(o_b)
    x1, merged = mid_fwd(x, y_a, o_b, proj, w_on, wts["w_branch_a"], wts["w_branch_b"], wts["w_out"])
    x2, gate, up, h2 = ffn_fwd(x1, g_ffn, wts["w_gate_up"], wts["w_down"])

    dx2, dw_pg, dw_pp, dg_ple, dg_final, loss = tail_fwd_bwd(
        x2, p, target, g_ple, g_final, wts["w_ple_gate"], wts["w_ple_proj"])
    dx1, act, dgate, dup, dg_ffn = ffn_bwd(dx2, x1, g_ffn, gate, up, wts["w_gate_up"], wts["w_down"])
    dw_down = matmul_tn(act, dx2, "dw_down")
    dw_gu = matmul_tn(h2, dgate, "dw_gate", width=2 * D_FF)
    dw_gu = matmul_tn(h2, dup, "dw_up", into=dw_gu, col0=D_FF)
    tok = early.begin(dict(w_gate_up=dw_gu, w_down=dw_down, w_ple_gate=dw_pg, w_ple_proj=dw_pp))
    dy_a, do_b, dz, dgates, dw_out, dwa, dwb, dw_on = mid_bwd(
        dx1, merged, y_a, o_b, proj, w_on + tok, wts["w_branch_a"], wts["w_branch_b"], wts["w_out"])
    tok = early.exchange(dz)
    ddw, ddwk, ddqg, ddkdec, ddp, ddegl = dn_scan_bwd(u, wk, qg, kdec, pm, egl + tok, states, do_b, b, s)
    dc3, dbd, dal, ddtb = dn_post_bwd(c, proj, al_row, dtb_row, ddw, ddwk, ddqg, ddkdec, ddp, ddegl, b, s)
    dconv, dconv_w = conv_bwd(proj, conv_w8, dc3, b, s)
    dqa, dka, dva, dbt, dbf = attn_bwd(proj, bias_band, y_a, lse, dy_a, b, s)
    tok = early.join(dqa)
    d_rel = bias_grad(jnp.transpose(dbt.reshape(A_HEADS, CHUNK, TAIL), (1, 0, 2)),
                      dbf.reshape(A_HEADS, CHUNK, 128))[:, :N_REL]

    pieces = [dgates, dqa, dka, dva, dconv, dz, dbd]
    bounds = [0, 2048, 2560, 3072, 3584, 5120, 5632, 5760]
    w_pieces = [w_inp[:, lo:hi] for lo, hi in zip(bounds[:-1], bounds[1:])]
    dx, dg_mix = in_proj_bwd(pieces, w_pieces, x, dx1, g_mix + tok)
    dwp = None
    for k, pc in enumerate(pieces):
        dwp = matmul_tn(h1, pc, "dw_in_%d" % k, into=dwp, col0=bounds[k], width=P_WIDTH)
    reduced_early = early.finish(dwp)
    dw_in = jnp.concatenate([dwp[:, P_QA:P_BD + 8], dwp[:, :P_QA]], axis=1)

    grads = dict(w_in=dw_in, w_branch_a=dwa, w_branch_b=dwb, w_out=dw_out, w_gate_up=dw_gu, w_down=dw_down,
                 w_ple_gate=dw_pg, w_ple_proj=dw_pp)
    small_grads = dict(g_mix=dg_mix[0], g_ffn=dg_ffn[0], g_ple=dg_ple[0], g_final=dg_final[0],
                       conv_w=dconv_w[:CONV_K].reshape(-1), rel_bias=d_rel.reshape(-1), w_onorm=dw_on[0],
                       a_log=dal[0, :B_HEADS], dt_bias=ddtb[0, :B_HEADS], loss=loss[0, :1])
    return dx.reshape(b, s, D_MODEL), grads, small_grads, reduced_early


BIG = (("w_in", (D_MODEL, D_IN), 1), ("w_branch_a", (A_WIDTH, D_MODEL), 1), ("w_branch_b", (B_WIDTH, D_MODEL), 1),
       ("w_out", (D_MODEL, D_MODEL), 0), ("w_gate_up", (D_MODEL, 2 * D_FF), 1), ("w_down", (D_FF, D_MODEL), 0),
       ("w_ple_gate", (D_MODEL, D_MODEL), 0), ("w_ple_proj", (PLE_DIM, D_MODEL), 1))
N_CHIPS = 4
FIRST_WEIGHTS = ("w_in",)
LATER_WEIGHTS = ("w_branch_a", "w_branch_b", "w_out", "w_gate_up", "w_down", "w_ple_gate", "w_ple_proj")
EARLY_GRADS = ("w_gate_up", "w_down", "w_ple_gate", "w_ple_proj")
LATE_GRADS = ("w_in", "w_branch_a", "w_branch_b", "w_out")


def _items(names):
    return [it for it in BIG if it[0] in names]


def _shard_shape(shape, axis):
    return (shape[0] // N_CHIPS, shape[1]) if axis == 0 else (shape[0], shape[1] // N_CHIPS)


def _pack_rows_of(names):
    return -(-sum(sh[0] * sh[1] for _, sh, _ in _items(names)) // (N_CHIPS * 128 * 512)) * 512


def _pack_rows(parts, total):
    used = sum(a.shape[-2] for a in parts)
    pad = jnp.zeros(parts[0].shape[:-2] + (total - used, 128), parts[0].dtype)
    return jnp.concatenate(parts + [pad], axis=-2)


def pack_grads(grads, names):
    parts = []
    for n, shape, axis in _items(names):
        rs, cs = _shard_shape(shape, axis)
        g = grads[n].astype(BF16)
        seg = g.reshape(N_CHIPS, rs, cs) if axis == 0 else jnp.transpose(g.reshape(rs, N_CHIPS, cs), (1, 0, 2))
        parts.append(seg.reshape(N_CHIPS, -1, 128))
    return _pack_rows(parts, _pack_rows_of(names))


def unpack_shard(flat, names):
    out, r0 = {}, 0
    for n, shape, axis in _items(names):
        rs, cs = _shard_shape(shape, axis)
        nr = rs * cs // 128
        out[n] = flat[r0:r0 + nr].reshape(rs, cs)
        r0 += nr
    return out


def _place():
    return lax.axis_index("x"), lax.axis_index("y"), lax.axis_index("c")


ANY = pl.BlockSpec(memory_space=pl.ANY)


def _gathered_shape(item):
    n, shape, _ = item
    return (N_CHIPS,) + _shard_shape(shape, 1) if n == "w_in" else shape


def _gather_block(o_ref, item, cx, cy, hf):
    n, shape, axis = item
    rs, cs = _shard_shape(shape, axis)
    hr = rs // 2
    ci = 2 * cx + cy
    if n == "w_in":
        return o_ref.at[ci, pl.ds(pl.multiple_of(hf * hr, 16), hr), :]
    if axis == 0:
        return o_ref.at[pl.ds(pl.multiple_of(ci * rs + hf * hr, 16), hr), :]
    return o_ref.at[pl.ds(pl.multiple_of(hf * hr, 16), hr), pl.ds(pl.multiple_of(ci * cs, 128), cs)]


def _own_half(w_ref, item, c):
    hr = _shard_shape(item[1], item[2])[0] // 2
    return w_ref.at[pl.ds(pl.multiple_of(c * hr, 16), hr), :]


def _fill_own(items, outs, shards, chip):
    full = {}
    for (n, shape, axis), o in zip(items, outs):
        rs, cs = _shard_shape(shape, axis)
        if n == "w_in":
            full[n] = lax.dynamic_update_slice(o, shards[n][None], (chip, 0, 0))
        elif axis == 0:
            full[n] = lax.dynamic_update_slice(o, shards[n], (chip * rs, 0))
        else:
            full[n] = lax.dynamic_update_slice(o, shards[n], (0, chip * cs))
    return full


def _other_chips(x, y):
    return [(1 - x, y), (x, 1 - y), (1 - x, 1 - y)]


def allgather_weights(shards, names, chip):
    items = _items(names)
    nw = len(items)

    def body(*refs):
        w_refs, o_refs = refs[:nw], refs[nw:2 * nw]
        send_sems, recv_sems = refs[2 * nw:]
        x, y, c = _place()
        sibling = (x, y, 1 - c)
        chips = _other_chips(x, y)

        def copy(k, src, dst, to):
            return pltpu.make_async_remote_copy(src_ref=src, dst_ref=dst, send_sem=send_sems.at[k],
                                                recv_sem=recv_sems.at[k], device_id=to, device_id_type=MESH)

        def blk(i, cx, cy, hf):
            return _gather_block(o_refs[i], items[i], cx, cy, hf)

        def my_half(i):
            return _own_half(w_refs[i], items[i], c)

        first = [copy(6 * i + j, my_half(i), blk(i, x, y, c), (*chip_, c))
                 for i in range(nw) for j, chip_ in enumerate(chips)]
        for cp in first:
            cp.start()
        passed = []
        for i in range(nw):
            for j, chip_ in enumerate(chips):
                copy(6 * i + j, my_half(i), blk(i, *chip_, c), (*chip_, c)).wait_recv()
                fwd = copy(6 * i + 3 + j, blk(i, *chip_, c), blk(i, *chip_, c), sibling)
                fwd.start()
                passed.append(fwd)
        for i in range(nw):
            for j, chip_ in enumerate(chips):
                copy(6 * i + 3 + j, my_half(i), blk(i, *chip_, 1 - c), sibling).wait_recv()
        for cp in first + passed:
            cp.wait_send()

    outs = pl.pallas_call(
        body, name="allgather_weights",
        in_specs=[ANY] * nw, out_specs=[ANY] * nw,
        out_shape=[jax.ShapeDtypeStruct(_gathered_shape(it), BF16) for it in items],
        scratch_shapes=[pltpu.SemaphoreType.DMA((6 * nw,)), pltpu.SemaphoreType.DMA((6 * nw,))],
    )(*[shards[it[0]] for it in items])
    return _fill_own(items, outs, shards, chip)


HBM_SPEC = pl.BlockSpec(memory_space=pltpu.HBM)
SEM_SPEC = pl.BlockSpec(memory_space=pltpu.SEMAPHORE)
EFFECT = pltpu.SideEffectType.DATAFLOW_SIDE_EFFECTING


def _in_hbm(a):
    return pltpu.with_memory_space_constraint(a, pltpu.HBM)


def copies_start(name, bufs, ncopies, plan):
    nb = len(bufs)

    def body(*refs):
        in_refs, send_sems, recv_sems, token = refs[:nb], refs[nb], refs[nb + 1], refs[-1]
        for k, (src, dst, to) in enumerate(plan(in_refs)):
            pltpu.make_async_remote_copy(src_ref=src, dst_ref=dst, send_sem=send_sems.at[k],
                                         recv_sem=recv_sems.at[k], device_id=to, device_id_type=MESH).start()
        token[...] = jnp.zeros_like(token)

    outs = pl.pallas_call(
        body, name=name,
        in_specs=[HBM_SPEC] * nb,
        out_specs=(SEM_SPEC, SEM_SPEC, *[HBM_SPEC] * nb, pl.BlockSpec(memory_space=pltpu.VMEM)),
        out_shape=(pltpu.SemaphoreType.DMA((ncopies,)), pltpu.SemaphoreType.DMA((ncopies,)),
                   *[pltpu.HBM(b.shape, b.dtype) for b in bufs], jax.ShapeDtypeStruct((8, 128), F32)),
        input_output_aliases={i: 2 + i for i in range(nb)},
        compiler_params=pltpu.CompilerParams(has_side_effects=EFFECT),
    )(*[_in_hbm(b) for b in bufs])
    return outs[0], outs[1], list(outs[2:2 + nb]), outs[-1][0, 0]


def copies_wait(name, send_sems, recv_sems, bufs, after, plan):
    nb = len(bufs)

    def body(*refs):
        in_refs, s_sems, r_sems = refs[:nb], refs[nb], refs[nb + 1]
        for k, (src, dst, to) in enumerate(plan(in_refs)):
            cp = pltpu.make_async_remote_copy(src_ref=src, dst_ref=dst, send_sem=s_sems.at[k],
                                              recv_sem=r_sems.at[k], device_id=to, device_id_type=MESH)
            cp.wait_send()
            cp.wait_recv()

    return list(pl.pallas_call(
        body, name=name,
        in_specs=[HBM_SPEC] * nb + [SEM_SPEC, SEM_SPEC, ANY],
        out_specs=tuple([HBM_SPEC] * nb),
        out_shape=tuple(pltpu.HBM(b.shape, b.dtype) for b in bufs),
        input_output_aliases={i: i for i in range(nb)},
        compiler_params=pltpu.CompilerParams(has_side_effects=EFFECT),
    )(*bufs, send_sems, recv_sems, after))


def _landing(shape, dtype):
    return _in_hbm(lax.empty(shape, dtype))


class LaterWeights:
    def __init__(self, shards, chip):
        self.items = _items(LATER_WEIGHTS)
        self.shards, self.chip = shards, chip
        self.nw = len(self.items)

    def _ici_plan(self, refs):
        x, y, c = _place()
        w_refs, o_refs = refs[:self.nw], refs[self.nw:]
        return [(_own_half(w_refs[i], it, c), _gather_block(o_refs[i], it, x, y, c), (*chip_, c))
                for i, it in enumerate(self.items) for chip_ in _other_chips(x, y)]

    def _d2d_plan(self, refs):
        x, y, c = _place()
        return [(_gather_block(refs[i], it, *chip_, c), _gather_block(refs[i], it, *chip_, c), (x, y, 1 - c))
                for i, it in enumerate(self.items) for chip_ in _other_chips(x, y)]

    def _d2d_wait_plan(self, refs):
        x, y, c = _place()
        return [(_gather_block(refs[i], it, *chip_, c), _gather_block(refs[i], it, *chip_, 1 - c), (x, y, 1 - c))
                for i, it in enumerate(self.items) for chip_ in _other_chips(x, y)]

    def _ici_wait_plan(self, refs):
        x, y, c = _place()
        w_refs, o_refs = refs[:self.nw], refs[self.nw:]
        return [(_own_half(w_refs[i], it, c), _gather_block(o_refs[i], it, *chip_, c), (*chip_, c))
                for i, it in enumerate(self.items) for chip_ in _other_chips(x, y)]

    def begin(self):
        srcs = [self.shards[it[0]] for it in self.items]
        lands = [_landing(_gathered_shape(it), BF16) for it in self.items]
        self.s1, self.r1, self.b1, tok = copies_start("gather_ici_start", srcs + lands, 3 * self.nw, self._ici_plan)
        return tok

    def forward(self, after):
        b1 = copies_wait("gather_ici_wait", self.s1, self.r1, self.b1, after, self._ici_wait_plan)
        self.s2, self.r2, self.b2, tok = copies_start("gather_d2d_start", b1[self.nw:], 3 * self.nw, self._d2d_plan)
        return tok

    def finish(self, after):
        outs = copies_wait("gather_d2d_wait", self.s2, self.r2, self.b2, after, self._d2d_wait_plan)
        return _fill_own(self.items, outs, self.shards, self.chip)


def small_allreduce(v, name):
    r = v.shape[0]

    def body(v_ref, o_ref, buf, send_sems, recv_sems):
        x, y, c = _place()
        me = 4 * x + 2 * y + c
        buf[me] = v_ref[...]
        flips = [(fx, fy, fc) for fx in (0, 1) for fy in (0, 1) for fc in (0, 1)][1:]
        peers = [((1 - x) if fx else x, (1 - y) if fy else y, (1 - c) if fc else c) for fx, fy, fc in flips]

        def copy(k, slot, to):
            return pltpu.make_async_remote_copy(src_ref=v_ref, dst_ref=buf.at[slot], send_sem=send_sems.at[k],
                                                recv_sem=recv_sems.at[k], device_id=to, device_id_type=MESH)

        sends = [copy(k, me, peer) for k, peer in enumerate(peers)]
        for cp in sends:
            cp.start()
        for k, (px, py, pc) in enumerate(peers):
            copy(k, 4 * px + 2 * py + pc, (px, py, pc)).wait_recv()
        for cp in sends:
            cp.wait_send()
        acc = buf[0]
        for d in range(1, 8):
            acc = acc + buf[d]
        o_ref[...] = acc

    return pl.pallas_call(
        body, name=name,
        in_specs=[pl.BlockSpec(memory_space=pltpu.VMEM)], out_specs=pl.BlockSpec(memory_space=pltpu.VMEM),
        out_shape=jax.ShapeDtypeStruct((r, 128), F32),
        scratch_shapes=[pltpu.VMEM((8, r, 128), F32), pltpu.SemaphoreType.DMA((7,)), pltpu.SemaphoreType.DMA((7,))],
    )(v)


def swap_halves(g):
    half = g.shape[1] // 2

    def body(g_ref, o_ref, send_sem, recv_sem):
        x, y, c = _place()
        cp = pltpu.make_async_remote_copy(
            src_ref=g_ref.at[:, pl.ds((1 - c) * half, half), :], dst_ref=o_ref, send_sem=send_sem,
            recv_sem=recv_sem, device_id=(x, y, 1 - c), device_id_type=MESH)
        cp.start()
        cp.wait()

    return pl.pallas_call(
        body, name="swap_halves", in_specs=[ANY], out_specs=ANY,
        out_shape=jax.ShapeDtypeStruct((N_CHIPS, half, 128), g.dtype),
        scratch_shapes=[pltpu.SemaphoreType.DMA, pltpu.SemaphoreType.DMA],
    )(g)


def add_halves(g, other, place):
    half = other.shape[1]
    tr = _tile_rows(half)
    nblk = half // tr

    def body(pref, g0, g1, g2, g3, o0, o1, o2, o3, pf_ref, pb_ref):
        f = lambda r: r[...].astype(F32)
        pf_ref[...] = f(g0) + f(o0)
        pb_ref[0] = _bf(f(g1) + f(o1))
        pb_ref[1] = _bf(f(g2) + f(o2))
        pb_ref[2] = _bf(f(g3) + f(o3))

    gspec = lambda k: pl.BlockSpec((None, tr, 128), lambda i, pr: ((pr[0] + k) % N_CHIPS, pr[1] * nblk + i, 0))
    ospec = lambda k: pl.BlockSpec((None, tr, 128), lambda i, pr: ((pr[0] + k) % N_CHIPS, i, 0))
    return pl.pallas_call(
        body, name="add_halves",
        grid_spec=pltpu.PrefetchScalarGridSpec(
            num_scalar_prefetch=1, grid=(nblk,),
            in_specs=[gspec(0), gspec(1), gspec(2), gspec(3), ospec(0), ospec(1), ospec(2), ospec(3)],
            out_specs=[pl.BlockSpec((tr, 128), lambda i, pr: (i, 0)),
                       pl.BlockSpec((3, tr, 128), lambda i, pr: (0, i, 0))]),
        out_shape=[jax.ShapeDtypeStruct((half, 128), F32), jax.ShapeDtypeStruct((3, half, 128), BF16)],
        compiler_params=_cp(("parallel",), VMEM_LIMIT),
    )(place, g, g, g, g, other, other, other, other)


def _tile_rows(n, cap=2048):
    best = 16
    for t in range(16, cap + 1, 16):
        if n % t == 0:
            best = t
    assert n % best == 0
    return best


def exchange_partials(pb):
    def body(p_ref, o_ref, send_sems, recv_sems):
        x, y, c = _place()
        me = 2 * x + y
        cps = []
        for k in range(1, N_CHIPS):
            to = (me + k) % N_CHIPS
            cps.append(pltpu.make_async_remote_copy(
                src_ref=p_ref.at[k - 1], dst_ref=o_ref.at[k - 1], send_sem=send_sems.at[k - 1],
                recv_sem=recv_sems.at[k - 1], device_id=(to // 2, to % 2, c), device_id_type=MESH))
        for cp in cps:
            cp.start()
        for cp in cps:
            cp.wait()

    return pl.pallas_call(
        body, name="exchange_partials", in_specs=[ANY], out_specs=ANY,
        out_shape=jax.ShapeDtypeStruct(pb.shape, pb.dtype),
        scratch_shapes=[pltpu.SemaphoreType.DMA((3,)), pltpu.SemaphoreType.DMA((3,))],
    )(pb)


def add_partials(pf, got, place):
    half = pf.shape[0]
    tr = _tile_rows(half)

    def body(pref, pf_ref, got_ref, o_ref):
        o_ref[...] = ((pf_ref[...] + got_ref[0].astype(F32)) + got_ref[1].astype(F32)) + got_ref[2].astype(F32)

    return pl.pallas_call(
        body, name="add_partials",
        grid_spec=pltpu.PrefetchScalarGridSpec(
            num_scalar_prefetch=1, grid=(half // tr,),
            in_specs=[pl.BlockSpec((tr, 128), lambda i, pr: (i, 0)),
                      pl.BlockSpec((3, tr, 128), lambda i, pr: (0, i, 0))],
            out_specs=pl.BlockSpec((None, tr, 128), lambda i, pr: (pr[1], i, 0))),
        out_shape=jax.ShapeDtypeStruct((2, half, 128), F32),
        compiler_params=_cp(("parallel",), VMEM_LIMIT),
    )(place, pf, got)


def join_halves(both):
    def body(r_ref, o_ref, send_sem, recv_sem):
        x, y, c = _place()
        cp = pltpu.make_async_remote_copy(src_ref=r_ref.at[c], dst_ref=o_ref.at[c], send_sem=send_sem,
                                          recv_sem=recv_sem, device_id=(x, y, 1 - c), device_id_type=MESH)
        cp.start()
        pltpu.make_async_remote_copy(src_ref=r_ref.at[c], dst_ref=o_ref.at[1 - c], send_sem=send_sem,
                                     recv_sem=recv_sem, device_id=(x, y, 1 - c), device_id_type=MESH).wait_recv()
        cp.wait_send()

    return pl.pallas_call(
        body, name="join_halves", in_specs=[ANY], out_specs=ANY,
        out_shape=jax.ShapeDtypeStruct(both.shape, F32),
        scratch_shapes=[pltpu.SemaphoreType.DMA, pltpu.SemaphoreType.DMA],
        input_output_aliases={0: 0},
    )(both)


def reduce_scatter_grads(gpack, place):
    other = swap_halves(gpack)
    pf, pb = add_halves(gpack, other, place)
    got = exchange_partials(pb)
    return join_halves(add_partials(pf, got, place)).reshape(gpack.shape[1], 128)


class EarlyGrads:
    def __init__(self, place):
        self.place = place

    @staticmethod
    def _swap_plan(refs):
        x, y, c = _place()
        g_ref, o_ref = refs
        half = o_ref.shape[1]
        return [(g_ref.at[:, pl.ds(pl.multiple_of((1 - c) * half, 16), half), :], o_ref, (x, y, 1 - c))]

    @staticmethod
    def _exchange_plan(refs):
        x, y, c = _place()
        p_ref, o_ref = refs
        me = 2 * x + y
        return [(p_ref.at[k - 1], o_ref.at[k - 1], (((me + k) % N_CHIPS) // 2, ((me + k) % N_CHIPS) % 2, c))
                for k in range(1, N_CHIPS)]

    @staticmethod
    def _join_plan(refs):
        x, y, c = _place()
        return [(refs[0].at[c], refs[0].at[c], (x, y, 1 - c))]

    @staticmethod
    def _join_wait_plan(refs):
        x, y, c = _place()
        return [(refs[0].at[c], refs[0].at[1 - c], (x, y, 1 - c))]

    def begin(self, grads):
        g = pack_grads(grads, EARLY_GRADS)
        land = _landing((N_CHIPS, g.shape[1] // 2, 128), BF16)
        self.s1, self.r1, self.b1, tok = copies_start("grads_swap_start", [g, land], 1, self._swap_plan)
        return tok

    def exchange(self, after):
        g, other = copies_wait("grads_swap_wait", self.s1, self.r1, self.b1, after, self._swap_plan)
        self.pf, pb = add_halves(g, other, self.place)
        land = _landing(pb.shape, BF16)
        self.s2, self.r2, self.b2, tok = copies_start("grads_exchange_start", [pb, land], 3, self._exchange_plan)
        return tok

    def join(self, after):
        _, got = copies_wait("grads_exchange_wait", self.s2, self.r2, self.b2, after, self._exchange_plan)
        both = add_partials(self.pf, got, self.place)
        self.s3, self.r3, self.b3, tok = copies_start("grads_join_start", [both], 1, self._join_plan)
        return tok

    def finish(self, after):
        (both,) = copies_wait("grads_join_wait", self.s3, self.r3, self.b3, after, self._join_wait_plan)
        return both.reshape(-1, 128)


SMALL = (("g_mix", D_MODEL), ("g_ffn", D_MODEL), ("g_ple", D_MODEL), ("g_final", D_MODEL),
         ("conv_w", CONV_K * CONV_CH), ("rel_bias", A_HEADS * N_REL), ("w_onorm", B_DIM),
         ("a_log", B_HEADS), ("dt_bias", B_HEADS), ("loss", 1))


def _pad128(v):
    v = v.reshape(-1)
    return jnp.pad(v, (0, -v.shape[0] % 128))


def pack_small(d, names, rows):
    flat = jnp.concatenate([_pad128(d[n]) for n in names]).reshape(-1, 128)
    return jnp.pad(flat, ((0, rows - flat.shape[0]), (0, 0)))


def unpack_small(flat, names_sizes):
    out, r0 = {}, 0
    v = flat.reshape(-1)
    for n, size in names_sizes:
        out[n] = v[r0:r0 + size]
        r0 += -(-size // 128) * 128
    return out


def kernel(x, p, g_mix, w_in, conv_w, a_log, dt_bias, rel_bias, w_onorm, w_branch_a, w_branch_b, w_out, g_ffn, w_gate_up, w_down, g_ple, w_ple_gate, w_ple_proj, g_final, loss_target, m_g_mix, m_w_in, m_conv_w, m_a_log, m_dt_bias, m_rel_bias, m_w_onorm, m_w_branch_a, m_w_branch_b, m_w_out, m_g_ffn, m_w_gate_up, m_w_down, m_g_ple, m_w_ple_gate, m_w_ple_proj, m_g_final, v_g_mix, v_w_in, v_conv_w, v_a_log, v_dt_bias, v_rel_bias, v_w_onorm, v_w_branch_a, v_w_branch_b, v_w_out, v_g_ffn, v_w_gate_up, v_w_down, v_g_ple, v_w_ple_gate, v_w_ple_proj, v_g_final):
    names = ["g_mix", "w_in", "conv_w", "a_log", "dt_bias", "rel_bias", "w_onorm", "w_branch_a", "w_branch_b",
             "w_out", "g_ffn", "w_gate_up", "w_down", "g_ple", "w_ple_gate", "w_ple_proj", "g_final"]
    w = dict(zip(names, [g_mix, w_in, conv_w, a_log, dt_bias, rel_bias, w_onorm, w_branch_a, w_branch_b, w_out,
                         g_ffn, w_gate_up, w_down, g_ple, w_ple_gate, w_ple_proj, g_final]))
    m = dict(zip(names, [m_g_mix, m_w_in, m_conv_w, m_a_log, m_dt_bias, m_rel_bias, m_w_onorm, m_w_branch_a,
                         m_w_branch_b, m_w_out, m_g_ffn, m_w_gate_up, m_w_down, m_g_ple, m_w_ple_gate,
                         m_w_ple_proj, m_g_final]))
    v = dict(zip(names, [v_g_mix, v_w_in, v_conv_w, v_a_log, v_dt_bias, v_rel_bias, v_w_onorm, v_w_branch_a,
                         v_w_branch_b, v_w_out, v_g_ffn, v_w_gate_up, v_w_down, v_g_ple, v_w_ple_gate,
                         v_w_ple_proj, v_g_final]))
    xi, yi, ci = _place()
    chip = 2 * xi + yi
    big_names = [n for n, _, _ in BIG]

    shards2d = {n: w[n].reshape(w[n].shape[-2:]) for n in big_names}
    shards_bf = {n: a.astype(BF16) for n, a in shards2d.items()}
    g4 = allgather_weights(shards_bf, FIRST_WEIGHTS, chip)["w_in"]
    place = jnp.stack([chip, ci]).astype(jnp.int32)
    conv_sh = jnp.where(ci == 0, w["conv_w"].reshape(CONV_K, CONV_CH // N_CHIPS), 0.0)
    conv_slots = lax.dynamic_update_slice(jnp.zeros((N_CHIPS, CONV_K, CONV_CH // N_CHIPS), F32), conv_sh[None],
                                          (chip, 0, 0))
    conv_all = small_allreduce(conv_slots.reshape(-1, 128), "gather_conv_w")
    conv_full = jnp.transpose(conv_all.reshape(N_CHIPS, CONV_K, CONV_CH // N_CHIPS), (1, 0, 2)).reshape(CONV_K, CONV_CH)
    small = {n: w[n] for n in names if n not in big_names}
    small["conv_w"] = conv_full

    grad_x, grads, small_grads, reduced_early = local_step(
        x, p[0], loss_target, g4, small, LaterWeights(shards_bf, chip), EarlyGrads(place))

    reduced_late = reduce_scatter_grads(pack_grads(grads, LATE_GRADS), place)
    gshard = {**unpack_shard(reduced_early, EARLY_GRADS), **unpack_shard(reduced_late, LATE_GRADS)}
    small_names = [n for n, _ in SMALL]
    red = unpack_small(small_allreduce(pack_small(small_grads, small_names, 112), "allreduce_small"), SMALL)
    loss = red["loss"][0]
    conv_g = lax.dynamic_slice(red["conv_w"].reshape(CONV_K, N_CHIPS, CONV_CH // N_CHIPS), (0, chip, 0),
                               (CONV_K, 1, CONV_CH // N_CHIPS))
    gsmall = {n: red[n].reshape(w[n].shape) for n in small_names if n not in ("loss", "conv_w")}
    gsmall["conv_w"] = conv_g.reshape(w["conv_w"].shape)

    grad, delta, new_m, new_v = {}, {}, {}, {}
    for n in big_names:
        shp = w[n].shape
        d_, m_, v_ = adamw(shards2d[n], gshard[n], m[n].reshape(shp[-2:]), v[n].reshape(shp[-2:]), "adamw_" + n)
        grad[n], delta[n], new_m[n], new_v[n] = gshard[n].reshape(shp), d_.reshape(shp), m_.reshape(shp), v_.reshape(shp)
    snames = [n for n in small_names if n != "loss"]
    ssizes = [(n, w[n].size) for n in snames]
    pk = lambda d: pack_small(d, snames, 64)
    d_, m_, v_ = adamw(pk(w), pk(gsmall), pk(m), pk(v), "adamw_small")
    ds, ms, vs = unpack_small(d_, ssizes), unpack_small(m_, ssizes), unpack_small(v_, ssizes)
    for n in snames:
        shp = w[n].shape
        grad[n], delta[n], new_m[n], new_v[n] = gsmall[n], ds[n].reshape(shp), ms[n].reshape(shp), vs[n].reshape(shp)

    return (loss, grad_x, *[grad[n] for n in names], *[delta[n] for n in names],
            *[new_m[n] for n in names], *[new_v[n] for n in names])
```

```python
import functools

import jax
import jax.numpy as jnp
from jax import lax
from jax.experimental import pallas as pl
from jax.experimental.pallas import tpu as pltpu

F32 = jnp.float32
BF16 = jnp.bfloat16
HI = lax.Precision.HIGHEST
MESH = pl.DeviceIdType.MESH

D_MODEL = 1024
CHUNK = 64
PLE_DIM = 256
EPS = 1e-6
A_HEADS = 8
A_HEAD_DIM = 64
A_WIDTH = 512
A_LOOKBACK = 8
BAND = (A_LOOKBACK + 1) * CHUNK
TAIL = 3 * CHUNK
REL_CLIP = 128
N_REL = 2 * REL_CLIP + 1
B_HEADS = 4
B_DIM = 128
B_WIDTH = 512
CONV_K = 4
CONV_CH = 1536
D_FF = 2816
SPLIT_Z = 3584
D_IN = 5640
ADAM_LR, ADAM_B1, ADAM_B2, ADAM_EPS, ADAM_WD, ADAM_STEP = 0.001, 0.9, 0.999, 1e-08, 0.01, 10

P_GATES, P_QA, P_KA, P_VA, P_CONV, P_Z, P_BD, P_WIDTH = 0, 2048, 2560, 3072, 3584, 5120, 5632, 5760

VMEM_LIMIT = 56 * 1024 * 1024


def _cp(sem, vmem=None, **kw):
    return pltpu.CompilerParams(dimension_semantics=sem, vmem_limit_bytes=vmem, **kw)


def _tile(n, cap):
    best = None
    for t in range(128, cap + 1, 128):
        if n % t == 0:
            best = t
    assert best is not None, (n, cap)
    return best


def _nn(a, b, prec=None):
    return lax.dot_general(a, b, (((1,), (0,)), ((), ())), preferred_element_type=F32, precision=prec)


def _nt(a, b, prec=None):
    return lax.dot_general(a, b, (((1,), (1,)), ((), ())), preferred_element_type=F32, precision=prec)


def _tn(a, b, prec=None):
    return lax.dot_general(a, b, (((0,), (0,)), ((), ())), preferred_element_type=F32, precision=prec)


def _bnn(a, b, prec=None):
    return lax.dot_general(a, b, (((2,), (1,)), ((0,), (0,))), preferred_element_type=F32, precision=prec)


def _bnt(a, b, prec=None):
    return lax.dot_general(a, b, (((2,), (2,)), ((0,), (0,))), preferred_element_type=F32, precision=prec)


def _bf(a):
    return a.astype(BF16)


def _split(a):
    hi = a.astype(BF16)
    return hi, (a - hi.astype(F32)).astype(BF16)


def _bnn_exact(lhs_b, rhs):
    h1 = _bf(rhs)
    r1 = rhs - h1.astype(F32)
    h2 = _bf(r1)
    h3 = _bf(r1 - h2.astype(F32))
    return _bnn(lhs_b, h1) + (_bnn(lhs_b, h2) + _bnn(lhs_b, h3))


def _bnn3(a, b):
    ah, al = a if isinstance(a, tuple) else _split(a)
    bh, bl = b if isinstance(b, tuple) else _split(b)
    return _bnn(ah, bh) + (_bnn(ah, bl) + _bnn(al, bh))


def _sigmoid(x):
    return 0.5 * jnp.tanh(0.5 * x) + 0.5


def _softplus(x):
    return jnp.maximum(x, 0.0) + jnp.log(1.0 + jnp.exp(-jnp.abs(x)))


def rms_matmul(x, g, w, name, tm=512, tn_cap=1024):
    t, d = x.shape
    n = w.shape[1]
    tm = min(tm, t)
    tn = _tile(n, tn_cap)

    def body(x_ref, g_ref, w_ref, o_ref, h_ref):
        @pl.when(pl.program_id(1) == 0)
        def _():
            xv = x_ref[...]
            r = lax.rsqrt(jnp.mean(xv * xv, axis=-1, keepdims=True) + EPS)
            h_ref[...] = _bf(xv * r * g_ref[...])

        o_ref[...] = _bf(_nn(h_ref[...], w_ref[...]))

    return pl.pallas_call(
        body, name=name, grid=(t // tm, n // tn),
        in_specs=[pl.BlockSpec((tm, d), lambda i, j: (i, 0)),
                  pl.BlockSpec((1, d), lambda i, j: (0, 0)),
                  pl.BlockSpec((d, tn), lambda i, j: (0, j))],
        out_specs=[pl.BlockSpec((tm, tn), lambda i, j: (i, j)),
                   pl.BlockSpec((tm, d), lambda i, j: (i, 0))],
        out_shape=[jax.ShapeDtypeStruct((t, n), BF16), jax.ShapeDtypeStruct((t, d), BF16)],
        compiler_params=_cp(("parallel", "arbitrary"), VMEM_LIMIT),
    )(x, g, w)


def matmul_tn(a, b, name, into=None, col0=0, width=None, tm=1024, tk_cap=1408, tn_cap=1408):
    m, k1 = a.shape
    n = b.shape[1]
    tm = min(tm, m)
    tk = _tile(k1, tk_cap)
    tn = _tile(n, tn_cap)
    while col0 % tn:
        tn = _tile(n, tn - 128)
    nk = m // tm
    c0 = col0 // tn

    def body(*refs):
        a_ref, b_ref, o_ref, acc = refs[0], refs[1], refs[-2], refs[-1]

        @pl.when(pl.program_id(2) == 0)
        def _():
            acc[...] = jnp.zeros_like(acc)

        acc[...] += _tn(_bf(a_ref[...]), _bf(b_ref[...]))

        @pl.when(pl.program_id(2) == nk - 1)
        def _():
            o_ref[...] = _bf(acc[...])

    in_specs = [pl.BlockSpec((tm, tk), lambda i, j, k: (k, i)),
                pl.BlockSpec((tm, tn), lambda i, j, k: (k, j))]
    args = [a, b]
    total = n if width is None else width
    aliases = {}
    if into is not None:
        in_specs.append(ANY)
        args.append(into)
        total = into.shape[1]
        aliases = {2: 0}
    return pl.pallas_call(
        body, name=name, grid=(k1 // tk, n // tn, nk),
        in_specs=in_specs,
        out_specs=pl.BlockSpec((tk, tn), lambda i, j, k: (i, c0 + j)),
        out_shape=jax.ShapeDtypeStruct((k1, total), BF16),
        scratch_shapes=[pltpu.VMEM((tk, tn), F32)],
        input_output_aliases=aliases,
        compiler_params=_cp(("parallel", "parallel", "arbitrary"), VMEM_LIMIT),
    )(*args)


def _tail_onehot(qi):
    r = lax.broadcasted_iota(jnp.int32, (384, TAIL), 0)
    kj = lax.broadcasted_iota(jnp.int32, (384, TAIL), 1)
    return (r == jnp.minimum(REL_CLIP + qi - kj, REL_CLIP) + REL_CLIP).astype(F32)


def bias_tail(rel_pad):
    def body(rb_ref, o_ref):
        rb = rb_ref[...]
        for qi in range(CHUNK):
            o_ref[qi] = _nn(rb, _tail_onehot(qi), HI)

    return pl.pallas_call(
        body, name="bias_tail",
        out_shape=jax.ShapeDtypeStruct((CHUNK, A_HEADS, TAIL), F32),
    )(rel_pad)


def bias_grad(db_t, db_far):
    def body(t_ref, f_ref, o_ref):
        acc = jnp.zeros((A_HEADS, 384), F32)
        for qi in range(CHUNK):
            acc = acc + _nt(t_ref[qi], _tail_onehot(qi), HI)
        far = jnp.sum(jnp.sum(f_ref[...], axis=2), axis=1, keepdims=True)
        lane = lax.broadcasted_iota(jnp.int32, (A_HEADS, 384), 1)
        o_ref[...] = acc + jnp.where(lane == 2 * REL_CLIP, far, 0.0)

    return pl.pallas_call(
        body, name="bias_grad",
        out_shape=jax.ShapeDtypeStruct((A_HEADS, 384), F32),
    )(db_t, db_far)


ATT_CB = 8


def _stack_heads(a, lane):
    return jnp.concatenate([jnp.where(lane < 64, a, 0.0), jnp.where(lane >= 64, a, 0.0)], axis=0)


def _fill_band_pads(k_ref, v_ref, kp, vp, s):
    z = jnp.zeros((A_LOOKBACK * CHUNK, 128), BF16)
    kp[pl.ds(0, A_LOOKBACK * CHUNK), :] = z
    vp[pl.ds(0, A_LOOKBACK * CHUNK), :] = z
    kp[pl.ds(A_LOOKBACK * CHUNK, s), :] = _bf(k_ref[...])
    vp[pl.ds(A_LOOKBACK * CHUNK, s), :] = _bf(v_ref[...])


def attn_fwd(proj, bias_band, b, s):
    t = b * s
    nc = s // CHUNK
    qb, kb_, vb_ = P_QA // 128, P_KA // 128, P_VA // 128

    nstep = nc // ATT_CB
    rows = ATT_CB * CHUNK

    def body(q_ref, k_ref, v_ref, b_ref, o_ref, lse_ref, kp, vp):
        n0 = pl.program_id(2) * ATT_CB

        @pl.when(n0 == 0)
        def _():
            _fill_band_pads(k_ref, v_ref, kp, vp, s)

        lane = lax.broadcasted_iota(jnp.int32, (CHUNK, 128), 1)
        col = lax.broadcasted_iota(jnp.int32, (2 * CHUNK, BAND), 1)
        bias2 = b_ref[...]

        def chunk(cc):
            n = n0 + cc
            r0 = pl.multiple_of(cc * CHUNK, CHUNK)
            start = pl.multiple_of(n * CHUNK, CHUNK)
            kb = kp[pl.ds(start, BAND), :]
            vb = vp[pl.ds(start, BAND), :]
            q2 = _stack_heads(q_ref[pl.ds(r0, CHUNK), :] * (A_HEAD_DIM ** -0.5), lane)
            sc = jnp.where(col >= (A_LOOKBACK - n) * CHUNK, _nt(_bf(q2), kb) + bias2, -1e30)
            mx = jnp.max(sc, axis=1, keepdims=True)
            p = jnp.exp(sc - mx)
            l = jnp.sum(p, axis=1, keepdims=True)
            o2 = _nn(_bf(p), vb) / l
            lse2 = mx + jnp.log(l)
            o_ref[pl.ds(r0, CHUNK), :] = jnp.where(lane < 64, o2[:CHUNK], o2[CHUNK:])
            lse_ref[pl.ds(r0, CHUNK), :] = jnp.where(lane < 64, lse2[:CHUNK], lse2[CHUNK:])

        def pair(i, carry):
            chunk(2 * i)
            chunk(2 * i + 1)
            return carry

        lax.fori_loop(0, ATT_CB // 2, pair, 0)

    return pl.pallas_call(
        body, name="attn_fwd", grid=(b, 4, nstep),
        in_specs=[pl.BlockSpec((rows, 128), lambda bb, m, n: (bb * nstep + n, qb + m)),
                  pl.BlockSpec((s, 128), lambda bb, m, n: (bb, kb_ + m)),
                  pl.BlockSpec((s, 128), lambda bb, m, n: (bb, vb_ + m)),
                  pl.BlockSpec((None, 2 * CHUNK, BAND), lambda bb, m, n: (m, 0, 0))],
        out_specs=[pl.BlockSpec((rows, 128), lambda bb, m, n: (bb * nstep + n, m)),
                   pl.BlockSpec((rows, 128), lambda bb, m, n: (bb * nstep + n, m))],
        out_shape=[jax.ShapeDtypeStruct((t, A_WIDTH), F32), jax.ShapeDtypeStruct((t, A_WIDTH), F32)],
        scratch_shapes=[pltpu.VMEM((s + A_LOOKBACK * CHUNK, 128), BF16),
                        pltpu.VMEM((s + A_LOOKBACK * CHUNK, 128), BF16)],
        compiler_params=_cp(("parallel", "parallel", "arbitrary"), VMEM_LIMIT),
    )(proj, proj, proj, bias_band)


def attn_bwd(proj, bias_band, y_a, lse, dy_a, b, s):
    t = b * s
    nc = s // CHUNK
    qb, kb_, vb_ = P_QA // 128, P_KA // 128, P_VA // 128
    pad = A_LOOKBACK * CHUNK
    nstep = nc // ATT_CB
    rows = ATT_CB * CHUNK

    def body(q_ref, k_ref, v_ref, b_ref, do_ref, o_ref, lse_ref,
             dq_ref, dk_ref, dv_ref, dbt_ref, dbf_ref, kp, vp, dkp, dvp):
        bb = pl.program_id(1)
        n0 = pl.program_id(2) * ATT_CB

        @pl.when(n0 == 0)
        def _():
            _fill_band_pads(k_ref, v_ref, kp, vp, s)
            dkp[...] = jnp.zeros_like(dkp)
            dvp[...] = jnp.zeros_like(dvp)

        @pl.when((n0 == 0) & (bb == 0))
        def _():
            dbt_ref[...] = jnp.zeros_like(dbt_ref)
            dbf_ref[...] = jnp.zeros_like(dbf_ref)

        lane = lax.broadcasted_iota(jnp.int32, (CHUNK, 128), 1)
        col = lax.broadcasted_iota(jnp.int32, (2 * CHUNK, BAND), 1)
        bias2 = b_ref[...]

        def chunk(cc):
            n = n0 + cc
            r0 = pl.multiple_of(cc * CHUNK, CHUNK)
            start = pl.multiple_of(n * CHUNK, CHUNK)
            kb = kp[pl.ds(start, BAND), :]
            vb = vp[pl.ds(start, BAND), :]
            q2b = _bf(_stack_heads(q_ref[pl.ds(r0, CHUNK), :] * (A_HEAD_DIM ** -0.5), lane))
            do2 = _stack_heads(do_ref[pl.ds(r0, CHUNK), :], lane)
            do2b = _bf(do2)
            o = o_ref[pl.ds(r0, CHUNK), :]
            lsev = lse_ref[pl.ds(r0, CHUNK), :]
            lse2 = jnp.concatenate([lsev[:, 0:1], lsev[:, 64:65]], axis=0)
            sc = jnp.where(col >= (A_LOOKBACK - n) * CHUNK, _nt(q2b, kb) + bias2, -1e30)
            p = jnp.exp(sc - lse2)
            dp = _nt(do2b, vb)
            delta = jnp.sum(do2 * jnp.concatenate([o, o], axis=0), axis=1, keepdims=True)
            ds = p * (dp - delta)
            dsb = _bf(ds)
            dq2 = _nn(dsb, kb)
            dq_ref[pl.ds(r0, CHUNK), :] = _bf(jnp.where(lane < 64, dq2[:CHUNK], dq2[CHUNK:]) * (A_HEAD_DIM ** -0.5))
            dkp[pl.ds(start, BAND), :] += _tn(dsb, q2b)
            dvp[pl.ds(start, BAND), :] += _tn(_bf(p), do2b)
            dbt_ref[...] += ds[:, BAND - TAIL:]
            dbf_ref[...] += ds[:, 0:128] + ds[:, 128:256] + ds[:, 256:384]

        def step(i, carry):
            chunk(i)
            return carry

        lax.fori_loop(0, ATT_CB, step, 0)

        @pl.when(n0 == nc - ATT_CB)
        def _():
            dk_ref[...] = _bf(dkp[pl.ds(pad, s), :])
            dv_ref[...] = _bf(dvp[pl.ds(pad, s), :])

    return pl.pallas_call(
        body, name="attn_bwd", grid=(4, b, nstep),
        in_specs=[pl.BlockSpec((rows, 128), lambda m, bb, n: (bb * nstep + n, qb + m)),
                  pl.BlockSpec((s, 128), lambda m, bb, n: (bb, kb_ + m)),
                  pl.BlockSpec((s, 128), lambda m, bb, n: (bb, vb_ + m)),
                  pl.BlockSpec((None, 2 * CHUNK, BAND), lambda m, bb, n: (m, 0, 0)),
                  pl.BlockSpec((rows, 128), lambda m, bb, n: (bb * nstep + n, m)),
                  pl.BlockSpec((rows, 128), lambda m, bb, n: (bb * nstep + n, m)),
                  pl.BlockSpec((rows, 128), lambda m, bb, n: (bb * nstep + n, m))],
        out_specs=[pl.BlockSpec((rows, 128), lambda m, bb, n: (bb * nstep + n, m)),
                   pl.BlockSpec((s, 128), lambda m, bb, n: (bb, m)),
                   pl.BlockSpec((s, 128), lambda m, bb, n: (bb, m)),
                   pl.BlockSpec((None, 2 * CHUNK, TAIL), lambda m, bb, n: (m, 0, 0)),
                   pl.BlockSpec((None, 2 * CHUNK, 128), lambda m, bb, n: (m, 0, 0))],
        out_shape=[jax.ShapeDtypeStruct((t, A_WIDTH), BF16)] * 3
        + [jax.ShapeDtypeStruct((4, 2 * CHUNK, TAIL), F32),
           jax.ShapeDtypeStruct((4, 2 * CHUNK, 128), F32)],
        scratch_shapes=[pltpu.VMEM((s + pad, 128), BF16), pltpu.VMEM((s + pad, 128), BF16),
                        pltpu.VMEM((s + pad, 128), F32), pltpu.VMEM((s + pad, 128), F32)],
        compiler_params=_cp(("parallel", "arbitrary", "arbitrary"), VMEM_LIMIT),
    )(proj, proj, proj, bias_band, dy_a, y_a, lse)


def _conv_taps(x, w, s):
    row = lax.broadcasted_iota(jnp.int32, x.shape, 0)
    shifted = [x] + [jnp.where(row >= i, pltpu.roll(x, i, 0), 0.0) for i in range(1, CONV_K)]
    acc = shifted[0] * w[CONV_K - 1:CONV_K, :]
    for i in range(1, CONV_K):
        acc = acc + shifted[i] * w[CONV_K - 1 - i:CONV_K - i, :]
    return acc, shifted


def conv_fwd(proj, conv_w8, b, s):
    cb = 512
    c0 = P_CONV // cb

    def body(x_ref, w_ref, o_ref):
        a, _ = _conv_taps(x_ref[...].astype(F32), w_ref[...], s)
        o_ref[...] = a * _sigmoid(a)

    return pl.pallas_call(
        body, name="conv_fwd", grid=(b, CONV_CH // cb),
        in_specs=[pl.BlockSpec((s, cb), lambda bb, j: (bb, c0 + j)),
                  pl.BlockSpec((8, cb), lambda bb, j: (0, j))],
        out_specs=pl.BlockSpec((s, cb), lambda bb, j: (bb, j)),
        out_shape=jax.ShapeDtypeStruct((b * s, CONV_CH), F32),
        compiler_params=_cp(("parallel", "parallel"), VMEM_LIMIT),
    )(proj, conv_w8)


def conv_bwd(proj, conv_w8, dc3, b, s):
    cb = 512
    c0 = P_CONV // cb

    def body(x_ref, w_ref, dc_ref, dx_ref, dw_ref):
        @pl.when(pl.program_id(1) == 0)
        def _():
            dw_ref[...] = jnp.zeros_like(dw_ref)

        w = w_ref[...]
        a, shifted = _conv_taps(x_ref[...].astype(F32), w, s)
        sg = _sigmoid(a)
        da = dc_ref[...] * (sg * (1.0 + a * (1.0 - sg)))
        row = lax.broadcasted_iota(jnp.int32, da.shape, 0)
        dx = da * w[CONV_K - 1:CONV_K, :]
        for i in range(1, CONV_K):
            dx = dx + jnp.where(row < s - i, pltpu.roll(da, s - i, 0), 0.0) * w[CONV_K - 1 - i:CONV_K - i, :]
        dx_ref[...] = _bf(dx)
        r8 =lax.broadcasted_iota(jnp.int32, (8, cb), 0)
        dw = jnp.zeros((8, cb), F32)
        for i in range(CONV_K):
            dw = dw + jnp.where(r8 == CONV_K - 1 - i, jnp.sum(da * shifted[i], axis=0, keepdims=True), 0.0)
        dw_ref[...] += dw

    return pl.pallas_call(
        body, name="conv_bwd", grid=(CONV_CH // cb, b),
        in_specs=[pl.BlockSpec((s, cb), lambda j, bb: (bb, c0 + j)),
                  pl.BlockSpec((8, cb), lambda j, bb: (0, j)),
                  pl.BlockSpec((None, s, cb), lambda j, bb: (j, bb, 0))],
        out_specs=[pl.BlockSpec((s, cb), lambda j, bb: (bb, j)),
                   pl.BlockSpec((8, cb), lambda j, bb: (0, j))],
        out_shape=[jax.ShapeDtypeStruct((b * s, CONV_CH), BF16), jax.ShapeDtypeStruct((8, CONV_CH), F32)],
        compiler_params=_cp(("parallel", "arbitrary"), VMEM_LIMIT),
    )(proj, conv_w8, dc3)


def _pick_lane(v, k):
    lane = lax.broadcasted_iota(jnp.int32, v.shape, 1)
    return jnp.sum(jnp.where(lane == k, v, 0.0), axis=1, keepdims=True)


def _chunk_masks(ncb):
    i = lax.broadcasted_iota(jnp.int32, (ncb, CHUNK, CHUNK), 1)
    j = lax.broadcasted_iota(jnp.int32, (ncb, CHUNK, CHUNK), 2)
    return i, j


def _col_of_row(rowvec, eye):
    return jnp.sum(jnp.where(eye, rowvec, 0.0), axis=2, keepdims=True)


def _dn_chunk_math(cq, ck, cv, bd, al_row, dtb_row, h, ncb):
    r = ncb * CHUNK
    i, j = _chunk_masks(ncb)
    eye = i == j
    low = i >= j
    strict = i > j
    ones = jnp.ones((ncb, CHUNK, CHUNK), F32)

    braw = _pick_lane(bd, h)
    draw = _pick_lane(bd, B_HEADS + h)
    al = _pick_lane(al_row, h)
    dtb = _pick_lane(dtb_row, h)
    ea = jnp.exp(al)
    beta = _sigmoid(braw)
    sp_arg = draw + dtb
    g = -ea * _softplus(sp_arg)

    rq = lax.rsqrt(jnp.sum(cq * cq, axis=1, keepdims=True) + EPS)
    rk = lax.rsqrt(jnp.sum(ck * ck, axis=1, keepdims=True) + EPS)
    nq = cq * rq
    kn = ck * rk
    qn = nq * (B_DIM ** -0.5)

    def c3(a):
        return a.reshape(ncb, CHUNK, a.shape[-1])

    qn3, kn3, v3, beta3 = c3(qn), c3(kn), c3(cv), c3(beta)
    gb = jnp.broadcast_to(c3(g), (ncb, CHUNK, CHUNK))
    gc_b = _bnn_exact(low.astype(BF16), gb)
    gr_b = _bnn_exact(_bf(ones), jnp.where(eye, gc_b, 0.0))
    dm = jnp.where(low, jnp.exp(jnp.where(low, gc_b - gr_b, 0.0)), 0.0)
    gc = gc_b[:, :, 0:1]
    gl = gc_b[:, CHUNK - 1:CHUNK, 0:1]
    gam = jnp.exp(gc)
    egl = jnp.exp(gl)
    edec = jnp.exp(gl - gc)

    knb = _bf(kn3)
    kk = _bnt(knb, knb)
    kd = jnp.where(strict, kk * dm, 0.0)
    a = beta3 * kd
    tm = eye.astype(F32)
    sz = 1
    while sz < CHUNK:
        off = jnp.where(((i // (2 * sz)) == (j // (2 * sz))) & ((i // sz) != (j // sz)), a, 0.0)
        tmb = _bf(tm)
        tm = tm - _bnn(_bf(_bnn(tmb, _bf(off))), tmb)
        sz *= 2
    bv = beta3 * v3
    bk = (beta3 * gam) * kn3
    sol = _bnn3(_split(tm), jnp.concatenate([bv, bk], axis=2))
    u, wk = sol[:, :, :B_DIM], sol[:, :, B_DIM:]
    qk = _bnt(_bf(qn3), knb)
    p = jnp.where(low, qk * dm, 0.0)
    kdec = kn3 * edec
    qg = gam * qn3
    return dict(beta=beta3, g=c3(g), ea=ea, sp_arg=c3(sp_arg), rq=c3(rq), rk=c3(rk), nq=c3(nq),
                qn=qn3, kn=kn3, v=v3, gc=gc, gl=gl, gam=gam, egl=egl, edec=edec, dm=dm, kd=kd, a=a,
                tm=tm, u=u, wk=wk, qk=qk, p=p, kdec=kdec, qg=qg, eye=eye, low=low, strict=strict)


def dn_prep(c, proj, al_row, dtb_row, b, s, ncb=8):
    t = b * s
    r = ncb * CHUNK
    nblk = t // r
    bd_blk = P_BD // 128

    def body(cq_ref, ck_ref, cv_ref, bd_ref, al_ref, dtb_ref, u_ref, wk_ref, qg_ref, kdec_ref, p_ref, egl_ref):
        h = pl.program_id(1)
        m = _dn_chunk_math(cq_ref[...], ck_ref[...], cv_ref[...], bd_ref[...].astype(F32), al_ref[...], dtb_ref[...], h, ncb)
        u_ref[...] = m["u"].reshape(r, B_DIM)
        wk_ref[...] = m["wk"].reshape(r, B_DIM)
        qg_ref[...] = m["qg"].reshape(r, B_DIM)
        kdec_ref[...] = m["kdec"].reshape(r, B_DIM)
        p_ref[...] = m["p"].reshape(r, CHUNK)
        egl_ref[...] = jnp.broadcast_to(m["egl"], (ncb, 8, 128)).reshape(ncb * 8, 128)

    col = lambda k: pl.BlockSpec((r, 128), lambda i, h: (i, k * B_HEADS + h))
    out_col = pl.BlockSpec((r, 128), lambda i, h: (i, h))
    small = pl.BlockSpec((1, 128), lambda i, h: (0, 0))
    return pl.pallas_call(
        body, name="dn_prep", grid=(nblk, B_HEADS),
        in_specs=[col(0), col(1), col(2), pl.BlockSpec((r, 128), lambda i, h: (i, bd_blk)), small, small],
        out_specs=[out_col, out_col, out_col, out_col,
                   pl.BlockSpec((None, r, CHUNK), lambda i, h: (h, i, 0)),
                   pl.BlockSpec((None, ncb * 8, 128), lambda i, h: (h, i, 0))],
        out_shape=[jax.ShapeDtypeStruct((t, B_WIDTH), F32)] * 4
        + [jax.ShapeDtypeStruct((B_HEADS, t, CHUNK), F32),
           jax.ShapeDtypeStruct((B_HEADS, t // 8, 128), F32)],
        compiler_params=_cp(("parallel", "parallel"), VMEM_LIMIT),
    )(c, c, c, proj, al_row, dtb_row)


def dn_scan_fwd(u, wk, qg, kdec, p, egl, b, s):
    t = b * s
    nc = s // CHUNK

    def body(u_ref, wk_ref, qg_ref, kdec_ref, p_ref, egl_ref, o_ref, ss_ref, st):
        @pl.when(pl.program_id(0) == 0)
        def _():
            st[...] = jnp.zeros_like(st)

        for bb in range(b):
            for h in range(B_HEADS):
                sl = slice(h * B_DIM, (h + 1) * B_DIM)
                sh = st[bb * B_HEADS + h]
                ss_ref[bb, h] = sh
                sb = _bf(sh)
                w = u_ref[bb, :, sl] - _nt(_bf(wk_ref[bb, :, sl]), sb)
                o_ref[bb, :, sl] = _nt(_bf(qg_ref[bb, :, sl]), sb) + _nn(_bf(p_ref[h, bb]), _bf(w))
                st[bb * B_HEADS + h] = egl_ref[h, bb][0:1, :] * sh + _tn(_bf(w), _bf(kdec_ref[bb, :, sl]))

    r3 = lambda a: a.reshape(b, s, B_WIDTH)
    act = pl.BlockSpec((b, CHUNK, B_WIDTH), lambda n: (0, n, 0))
    o, states = pl.pallas_call(
        body, name="dn_scan_fwd", grid=(nc,),
        in_specs=[act, act, act, act,
                  pl.BlockSpec((B_HEADS, b, CHUNK, CHUNK), lambda n: (0, 0, n, 0)),
                  pl.BlockSpec((B_HEADS, b, 8, 128), lambda n: (0, 0, n, 0))],
        out_specs=[act, pl.BlockSpec((b, None, B_HEADS, B_DIM, B_DIM), lambda n: (0, n, 0, 0, 0))],
        out_shape=[jax.ShapeDtypeStruct((b, s, B_WIDTH), F32),
                   jax.ShapeDtypeStruct((b, nc, B_HEADS, B_DIM, B_DIM), F32)],
        scratch_shapes=[pltpu.VMEM((b * B_HEADS, B_DIM, B_DIM), F32)],
        compiler_params=_cp(("arbitrary",), VMEM_LIMIT),
    )(r3(u), r3(wk), r3(qg), r3(kdec), p.reshape(B_HEADS, b, s, CHUNK), egl.reshape(B_HEADS, b, s // 8, 128))
    return o.reshape(t, B_WIDTH), states


def dn_scan_bwd(u, wk, qg, kdec, p, egl, states, do, b, s):
    t = b * s
    nc = s // CHUNK

    def body(u_ref, wk_ref, qg_ref, kdec_ref, p_ref, egl_ref, ss_ref, do_ref,
             dw_ref, dwk_ref, dqg_ref, dkdec_ref, dp_ref, degl_ref, dst):
        @pl.when(pl.program_id(0) == 0)
        def _():
            dst[...] = jnp.zeros_like(dst)

        for bb in range(b):
            for h in range(B_HEADS):
                sl = slice(h * B_DIM, (h + 1) * B_DIM)
                k = bb * B_HEADS + h
                sh = ss_ref[bb, h]
                sb = _bf(sh)
                dsp = dst[k]
                dsb = _bf(dsp)
                wkb = _bf(wk_ref[bb, :, sl])
                kdb = _bf(kdec_ref[bb, :, sl])
                pb = _bf(p_ref[h, bb])
                dob = _bf(do_ref[bb, :, sl])
                w = u_ref[bb, :, sl] - _nt(wkb, sb)
                wb = _bf(w)
                dw = _tn(pb, dob) + _nt(kdb, dsb)
                dwb = _bf(dw)
                dw_ref[bb, :, sl] = dw
                dqg_ref[bb, :, sl] = _nn(dob, sb)
                dwk_ref[bb, :, sl] = -_nn(dwb, sb)
                dkdec_ref[bb, :, sl] = _nn(wb, dsb)
                dp_ref[h, bb] = _nt(dob, wb)
                tot = jnp.sum(jnp.sum(sh * dsp, axis=1, keepdims=True), axis=0, keepdims=True)
                degl_ref[h, bb] = jnp.broadcast_to(tot, (8, 128))
                dst[k] = egl_ref[h, bb][0:1, :] * dsp + _tn(dob, _bf(qg_ref[bb, :, sl])) - _tn(dwb, wkb)

    r3 = lambda a: a.reshape(b, s, B_WIDTH)
    act = pl.BlockSpec((b, CHUNK, B_WIDTH), lambda n: (0, nc - 1 - n, 0))
    pspec = pl.BlockSpec((B_HEADS, b, CHUNK, CHUNK), lambda n: (0, 0, nc - 1 - n, 0))
    espec = pl.BlockSpec((B_HEADS, b, 8, 128), lambda n: (0, 0, nc - 1 - n, 0))
    outs = pl.pallas_call(
        body, name="dn_scan_bwd", grid=(nc,),
        in_specs=[act, act, act, act, pspec, espec,
                  pl.BlockSpec((b, None, B_HEADS, B_DIM, B_DIM), lambda n: (0, nc - 1 - n, 0, 0, 0)),
                  act],
        out_specs=[act, act, act, act, pspec, espec],
        out_shape=[jax.ShapeDtypeStruct((b, s, B_WIDTH), F32)] * 4
        + [jax.ShapeDtypeStruct((B_HEADS, b, s, CHUNK), F32),
           jax.ShapeDtypeStruct((B_HEADS, b, s // 8, 128), F32)],
        scratch_shapes=[pltpu.VMEM((b * B_HEADS, B_DIM, B_DIM), F32)],
        compiler_params=_cp(("arbitrary",), VMEM_LIMIT),
    )(r3(u), r3(wk), r3(qg), r3(kdec), p.reshape(B_HEADS, b, s, CHUNK), egl.reshape(B_HEADS, b, s // 8, 128),
      states, r3(do))
    return (*[a.reshape(t, B_WIDTH) for a in outs[:4]], outs[4].reshape(B_HEADS, t, CHUNK),
            outs[5].reshape(B_HEADS, t // 8, 128))


def dn_post_bwd(c, proj, al_row, dtb_row, dw, dwk, dqg, dkdec, dp, degl, b, s, ncb=8):
    t = b * s
    r = ncb * CHUNK
    nblk = t // r
    bd_blk = P_BD // 128

    def body(cq_ref, ck_ref, cv_ref, bd_ref, al_ref, dtb_ref, dw_ref, dwk_ref, dqg_ref, dkdec_ref, dp_ref,
             degl_ref, dc_ref, dbd_ref, dal_ref, ddtb_ref):
        h = pl.program_id(1)

        @pl.when((pl.program_id(0) == 0) & (h == 0))
        def _():
            dal_ref[...] = jnp.zeros_like(dal_ref)
            ddtb_ref[...] = jnp.zeros_like(ddtb_ref)

        m = _dn_chunk_math(cq_ref[...], ck_ref[...], cv_ref[...], bd_ref[...].astype(F32), al_ref[...], dtb_ref[...], h, ncb)
        eye, low, strict = m["eye"], m["low"], m["strict"]
        eyef = eye.astype(F32)

        def c3(a):
            return a.reshape(ncb, CHUNK, a.shape[-1])

        du, dwkv, dqg, dkdec = c3(dw_ref[...]), c3(dwk_ref[...]), c3(dqg_ref[...]), c3(dkdec_ref[...])
        dpm = jnp.where(low, c3(dp_ref[...]), 0.0)
        degl = degl_ref[...].reshape(ncb, 8, 128)[:, 0:1, 0:1]
        beta, gam, kn, qn, v = m["beta"], m["gam"], m["kn"], m["qn"], m["v"]
        dm, kd, a, p = m["dm"], m["kd"], m["a"], m["p"]
        knb, qnb = _bf(kn), _bf(qn)

        eyeb = _bf(eyef)
        th, tl = _split(m["tm"])
        tts = (_bf(_bnt(eyeb, th)), _bf(_bnt(eyeb, tl)))
        xy = _bnn3(tts, jnp.concatenate([du, dwkv], axis=2))
        x, y = xy[:, :, :B_DIM], xy[:, :, B_DIM:]
        da = -jnp.where(strict, _bnt(_bf(x), _bf(m["u"])) + _bnt(_bf(y), _bf(m["wk"])), 0.0)
        dv = beta * x
        sy = jnp.sum(y * kn, axis=2, keepdims=True)
        dbeta = jnp.sum(x * v, axis=2, keepdims=True) + gam * sy + jnp.sum(da * kd, axis=2, keepdims=True)
        dgam = beta * sy + jnp.sum(dqg * qn, axis=2, keepdims=True)
        dkk = da * beta * dm
        dqk = dpm * dm
        dkkb, dqkb = _bf(dkk), _bf(dqk)
        dkn = ((beta * gam) * y + _bnn(dkkb, knb) + _bnn(_bf(_bnt(eyeb, dkkb)), knb)
               + _bnn(_bf(_bnt(eyeb, dqkb)), qnb) + dkdec * m["edec"])
        dqn = gam * dqg + _bnn(dqkb, knb)
        mm = da * a + dpm * p
        ek = jnp.sum(dkdec * m["kdec"], axis=2, keepdims=True)
        dgc = (jnp.sum(mm, axis=2, keepdims=True) - _col_of_row(jnp.sum(mm, axis=1, keepdims=True), eye)
               + dgam * gam - ek)
        dgl = jnp.sum(ek, axis=1, keepdims=True) + degl * m["egl"]
        i, _ = _chunk_masks(ncb)
        dgc = dgc + jnp.where(i[:, :, 0:1] == CHUNK - 1, dgl, 0.0)
        upper = (i <= _chunk_masks(ncb)[1]).astype(BF16)
        dg = _bnn_exact(upper, jnp.broadcast_to(dgc, (ncb, CHUNK, CHUNK)))[:, :, 0:1]

        nq = m["nq"]
        dnq = dqn * (B_DIM ** -0.5)
        dcq = m["rq"] * (dnq - nq * jnp.sum(nq * dnq, axis=2, keepdims=True))
        dck = m["rk"] * (dkn - kn * jnp.sum(kn * dkn, axis=2, keepdims=True))
        dc_ref[0] = dcq.reshape(r, B_DIM)
        dc_ref[1] = dck.reshape(r, B_DIM)
        dc_ref[2] = dv.reshape(r, B_DIM)

        dbraw = (dbeta * beta * (1.0 - beta)).reshape(r, 1)
        sgm = _sigmoid(m["sp_arg"])
        ddraw3 = dg * (-m["ea"]) * sgm
        ddraw = ddraw3.reshape(r, 1)
        lane = lax.broadcasted_iota(jnp.int32, (r, 128), 1)
        contrib = jnp.where(lane == h, dbraw, 0.0) + jnp.where(lane == B_HEADS + h, ddraw, 0.0)

        @pl.when(h == 0)
        def _():
            dbd_ref[...] = contrib

        @pl.when(h != 0)
        def _():
            dbd_ref[...] += contrib

        lane8 = lax.broadcasted_iota(jnp.int32, (8, 128), 1)
        tot_al = jnp.sum(jnp.sum(dg * m["g"], axis=1, keepdims=True), axis=0, keepdims=True).reshape(1, 1)
        tot_dtb = jnp.sum(jnp.sum(ddraw3, axis=1, keepdims=True), axis=0, keepdims=True).reshape(1, 1)
        dal_ref[...] += jnp.where(lane8 == h, tot_al, 0.0)
        ddtb_ref[...] += jnp.where(lane8 == h, tot_dtb, 0.0)

    col = lambda k: pl.BlockSpec((r, 128), lambda i, h: (i, k * B_HEADS + h))
    hcol = pl.BlockSpec((r, 128), lambda i, h: (i, h))
    small = pl.BlockSpec((1, 128), lambda i, h: (0, 0))
    acc = pl.BlockSpec((8, 128), lambda i, h: (0, 0))
    return pl.pallas_call(
        body, name="dn_post_bwd", grid=(nblk, B_HEADS),
        in_specs=[col(0), col(1), col(2), pl.BlockSpec((r, 128), lambda i, h: (i, bd_blk)), small, small,
                  hcol, hcol, hcol, hcol,
                  pl.BlockSpec((None, r, CHUNK), lambda i, h: (h, i, 0)),
                  pl.BlockSpec((None, ncb * 8, 128), lambda i, h: (h, i, 0))],
        out_specs=[pl.BlockSpec((3, r, 128), lambda i, h: (0, i, h)),
                   pl.BlockSpec((r, 128), lambda i, h: (i, 0)), acc, acc],
        out_shape=[jax.ShapeDtypeStruct((3, t, B_WIDTH), F32), jax.ShapeDtypeStruct((t, 128), F32),
                   jax.ShapeDtypeStruct((8, 128), F32), jax.ShapeDtypeStruct((8, 128), F32)],
        compiler_params=_cp(("arbitrary", "arbitrary"), VMEM_LIMIT),
    )(c, c, c, proj, al_row, dtb_row, dw, dwk, dqg, dkdec, dp, degl)


def make_bias_band(rel_bias):
    tail = bias_tail(jnp.pad(rel_bias, ((0, 0), (0, 384 - N_REL))))
    far = jnp.broadcast_to(rel_bias[:, 2 * REL_CLIP][:, None, None], (A_HEADS, CHUNK, BAND - TAIL))
    return jnp.concatenate([far, jnp.transpose(tail, (1, 0, 2))], axis=2).reshape(4, 2 * CHUNK, BAND)


def _rms(x):
    r = lax.rsqrt(jnp.mean(x * x, axis=-1, keepdims=True) + EPS)
    return r, x * r


def _rms_bwd(dh, g, r, n):
    dn = dh * g
    return r * (dn - n * jnp.mean(dn * n, axis=-1, keepdims=True)), dh * n


def _gated_onorm(o, z, w_on):
    parts = []
    for h in range(B_HEADS):
        sl = slice(h * B_DIM, (h + 1) * B_DIM)
        r, n = _rms(o[:, sl])
        parts.append((r, n))
    r4 = [p[0] for p in parts]
    n4 = jnp.concatenate([p[1] for p in parts], axis=1)
    w4 = jnp.concatenate([w_on] * B_HEADS, axis=1)
    sz = _sigmoid(z)
    silu = z * sz
    return n4 * w4 * silu, r4, n4, w4, sz, silu


def mid_fwd(x, y_a, o_b, proj, w_on, wa, wb, w_out, tm=256):
    t = x.shape[0]
    tm = min(tm, t)

    def body(x_ref, ya_ref, ob_ref, z_ref, ga_ref, gb_ref, won_ref, wa_ref, wb_ref, wo_ref, x1_ref, mg_ref):
        yb = _gated_onorm(ob_ref[...], z_ref[...].astype(F32), won_ref[...])[0]
        ua = _nn(_bf(ya_ref[...]), wa_ref[...])
        ub = _nn(_bf(yb), wb_ref[...])
        merged = _sigmoid(ga_ref[...].astype(F32)) * ua + _sigmoid(gb_ref[...].astype(F32)) * ub
        mb = _bf(merged)
        mg_ref[...] = mb
        x1_ref[...] = x_ref[...] + _nn(mb, wo_ref[...])

    rowd = pl.BlockSpec((tm, D_MODEL), lambda i: (i, 0))
    row5 = pl.BlockSpec((tm, 512), lambda i: (i, 0))
    full = lambda a: pl.BlockSpec(a.shape, lambda i: (0,) * a.ndim)
    return pl.pallas_call(
        body, name="mid_fwd", grid=(t // tm,),
        in_specs=[rowd, row5, row5,
                  pl.BlockSpec((tm, 512), lambda i: (i, P_Z // 512)),
                  pl.BlockSpec((tm, D_MODEL), lambda i: (i, 0)),
                  pl.BlockSpec((tm, D_MODEL), lambda i: (i, 1)),
                  full(w_on), full(wa), full(wb), full(w_out)],
        out_specs=[rowd, rowd],
        out_shape=[jax.ShapeDtypeStruct((t, D_MODEL), F32), jax.ShapeDtypeStruct((t, D_MODEL), BF16)],
        compiler_params=_cp(("parallel",), VMEM_LIMIT),
    )(x, y_a, o_b, proj, proj, proj, w_on, wa, wb, w_out)


def mid_bwd(dx1, merged, y_a, o_b, proj, w_on, wa, wb, w_out, tm=256):
    t = dx1.shape[0]
    tm = min(tm, t)

    def body(dx1_ref, mg_ref, ya_ref, ob_ref, z_ref, ga_ref, gb_ref, won_ref, wa_ref, wb_ref, wo_ref,
             dya_ref, dob_ref, dz_ref, dg_ref, dwo_ref, dwa_ref, dwb_ref, dwon_ref):
        @pl.when(pl.program_id(0) == 0)
        def _():
            dwo_ref[...] = jnp.zeros_like(dwo_ref)
            dwa_ref[...] = jnp.zeros_like(dwa_ref)
            dwb_ref[...] = jnp.zeros_like(dwb_ref)
            dwon_ref[...] = jnp.zeros_like(dwon_ref)

        dx1b = _bf(dx1_ref[...])
        dmerged = _nt(dx1b, wo_ref[...])
        dwo_ref[...] += _tn(mg_ref[...], dx1b)
        o = ob_ref[...]
        z = z_ref[...].astype(F32)
        yb, r4, n4, w4, sz, silu = _gated_onorm(o, z, won_ref[...])
        yab, ybb = _bf(ya_ref[...]), _bf(yb)
        ua = _nn(yab, wa_ref[...])
        ub = _nn(ybb, wb_ref[...])
        sa, sb = _sigmoid(ga_ref[...].astype(F32)), _sigmoid(gb_ref[...].astype(F32))
        dua, dub = _bf(dmerged * sa), _bf(dmerged * sb)
        dg_ref[:, 0:D_MODEL] = _bf(dmerged * ua * sa * (1.0 - sa))
        dg_ref[:, D_MODEL:2 * D_MODEL] = _bf(dmerged * ub * sb * (1.0 - sb))
        dwa_ref[...] += _tn(yab, dua)
        dwb_ref[...] += _tn(ybb, dub)
        dya_ref[...] = _nt(dua, wa_ref[...])
        dyb = _nt(dub, wb_ref[...])
        dz_ref[...] = _bf(dyb * (n4 * w4) * (sz * (1.0 + z * (1.0 - sz))))
        dnw = dyb * silu
        dwon = jnp.zeros((1, B_DIM), F32)
        for h in range(B_HEADS):
            sl = slice(h * B_DIM, (h + 1) * B_DIM)
            dxh, dgh = _rms_bwd(dnw[:, sl], won_ref[...], r4[h], n4[:, sl])
            dob_ref[:, sl] = dxh
            dwon = dwon + jnp.sum(dgh, axis=0, keepdims=True)
        dwon_ref[...] += jnp.broadcast_to(dwon, (8, B_DIM))

    rowd = pl.BlockSpec((tm, D_MODEL), lambda i: (i, 0))
    row5 = pl.BlockSpec((tm, 512), lambda i: (i, 0))
    full = lambda a: pl.BlockSpec(a.shape, lambda i: (0,) * a.ndim)
    fixed = lambda shp: pl.BlockSpec(shp, lambda i: (0,) * len(shp))
    return pl.pallas_call(
        body, name="mid_bwd", grid=(t // tm,),
        in_specs=[rowd, rowd, row5, row5,
                  pl.BlockSpec((tm, 512), lambda i: (i, P_Z // 512)),
                  pl.BlockSpec((tm, D_MODEL), lambda i: (i, 0)),
                  pl.BlockSpec((tm, D_MODEL), lambda i: (i, 1)),
                  full(w_on), full(wa), full(wb), full(w_out)],
        out_specs=[row5, row5, row5, pl.BlockSpec((tm, 2 * D_MODEL), lambda i: (i, 0)),
                   fixed((D_MODEL, D_MODEL)), fixed((A_WIDTH, D_MODEL)), fixed((B_WIDTH, D_MODEL)),
                   fixed((8, B_DIM))],
        out_shape=[jax.ShapeDtypeStruct((t, 512), F32), jax.ShapeDtypeStruct((t, 512), F32),
                   jax.ShapeDtypeStruct((t, 512), BF16), jax.ShapeDtypeStruct((t, 2 * D_MODEL), BF16),
           jax.ShapeDtypeStruct((D_MODEL, D_MODEL), F32), jax.ShapeDtypeStruct((A_WIDTH, D_MODEL), F32),
           jax.ShapeDtypeStruct((B_WIDTH, D_MODEL), F32), jax.ShapeDtypeStruct((8, B_DIM), F32)],
        compiler_params=_cp(("arbitrary",), VMEM_LIMIT),
    )(dx1, merged, y_a, o_b, proj, proj, proj, w_on, wa, wb, w_out)


FFN_TF = 1408


def ffn_up(x1, g, w_gu, tm=512, tf=FFN_TF):
    t = x1.shape[0]
    tm = min(tm, t)
    nf = D_FF // tf

    def body(x_ref, g_ref, wg_ref, wu_ref, gate_ref, up_ref, act_ref, h_ref):
        @pl.when(pl.program_id(1) == 0)
        def _():
            r, n = _rms(x_ref[...])
            h_ref[...] = _bf(n * g_ref[...])

        hb = h_ref[...]
        gate = _nn(hb, wg_ref[...])
        up = _nn(hb, wu_ref[...])
        gate_ref[...] = _bf(gate)
        up_ref[...] = _bf(up)
        act_ref[...] = _bf(gate * _sigmoid(gate) * up)

    ff = pl.BlockSpec((tm, tf), lambda i, j: (i, j))
    return pl.pallas_call(
        body, name="ffn_up", grid=(t // tm, nf),
        in_specs=[pl.BlockSpec((tm, D_MODEL), lambda i, j: (i, 0)),
                  pl.BlockSpec((1, D_MODEL), lambda i, j: (0, 0)),
                  pl.BlockSpec((D_MODEL, tf), lambda i, j: (0, j)),
                  pl.BlockSpec((D_MODEL, tf), lambda i, j: (0, nf + j))],
        out_specs=[ff, ff, ff, pl.BlockSpec((tm, D_MODEL), lambda i, j: (i, 0))],
        out_shape=[jax.ShapeDtypeStruct((t, D_FF), BF16)] * 3 + [jax.ShapeDtypeStruct((t, D_MODEL), BF16)],
        compiler_params=_cp(("parallel", "arbitrary"), VMEM_LIMIT),
    )(x1, g, w_gu, w_gu)


def matmul_residual(a, w, res, name, tm=512, tk=FFN_TF):
    t, k = a.shape
    n = w.shape[1]
    tm = min(tm, t)

    def body(a_ref, w_ref, r_ref, o_ref):
        @pl.when(pl.program_id(1) == 0)
        def _():
            o_ref[...] = r_ref[...]

        o_ref[...] += _nn(a_ref[...], w_ref[...])

    return pl.pallas_call(
        body, name=name, grid=(t // tm, k // tk),
        in_specs=[pl.BlockSpec((tm, tk), lambda i, j: (i, j)),
                  pl.BlockSpec((tk, n), lambda i, j: (j, 0)),
                  pl.BlockSpec((tm, n), lambda i, j: (i, 0))],
        out_specs=pl.BlockSpec((tm, n), lambda i, j: (i, 0)),
        out_shape=jax.ShapeDtypeStruct((t, n), F32),
        compiler_params=_cp(("parallel", "arbitrary"), VMEM_LIMIT),
    )(a, w, res)


def ffn_act_bwd(dx2, gate, up, w_down, tm=512, tf=FFN_TF):
    t = dx2.shape[0]
    tm = min(tm, t)

    def body(dx2_ref, gate_ref, up_ref, wd_ref, dgate_ref, dup_ref, dx2b_ref):
        @pl.when(pl.program_id(1) == 0)
        def _():
            dx2b_ref[...] = _bf(dx2_ref[...])

        dact = _nt(dx2b_ref[...], wd_ref[...])
        gt, upv = gate_ref[...].astype(F32), up_ref[...].astype(F32)
        sg = _sigmoid(gt)
        dgate_ref[...] = _bf(dact * upv * (sg * (1.0 + gt * (1.0 - sg))))
        dup_ref[...] = _bf(dact * (gt * sg))

    ff = pl.BlockSpec((tm, tf), lambda i, j: (i, j))
    return pl.pallas_call(
        body, name="ffn_act_bwd", grid=(t // tm, D_FF // tf),
        in_specs=[pl.BlockSpec((tm, D_MODEL), lambda i, j: (i, 0)), ff, ff,
                  pl.BlockSpec((tf, D_MODEL), lambda i, j: (j, 0))],
        out_specs=[ff, ff],
        out_shape=[jax.ShapeDtypeStruct((t, D_FF), BF16)] * 2,
        scratch_shapes=[pltpu.VMEM((tm, D_MODEL), BF16)],
        compiler_params=_cp(("parallel", "arbitrary"), VMEM_LIMIT),
    )(dx2, gate, up, w_down)


def tail_fwd_bwd(x2, p, target, g_ple, g_final, w_pg, w_pp, tm=256):
    t = x2.shape[0]
    tm = min(tm, t)

    def body(x_ref, p_ref, t_ref, gp_ref, gf_ref, wpg_ref, wpp_ref,
             dx_ref, dwpg_ref, dwpp_ref, dgp_ref, dgf_ref, loss_ref):
        @pl.when(pl.program_id(0) == 0)
        def _():
            dwpg_ref[...] = jnp.zeros_like(dwpg_ref)
            dwpp_ref[...] = jnp.zeros_like(dwpp_ref)
            dgp_ref[...] = jnp.zeros_like(dgp_ref)
            dgf_ref[...] = jnp.zeros_like(dgf_ref)
            loss_ref[...] = jnp.zeros_like(loss_ref)

        x2v = x_ref[...]
        gp, gf = gp_ref[...], gf_ref[...]
        r3, n3 = _rms(x2v)
        h3b = _bf(n3 * gp)
        pb = _bf(p_ref[...])
        pg = _sigmoid(_nn(h3b, wpg_ref[...]))
        pp = _nn(pb, wpp_ref[...])
        x3 = x2v + pg * pp
        r4, n4 = _rms(x3)
        err = n4 * gf - t_ref[...]
        part = 0.5 * jnp.sum(jnp.sum(err * err, axis=1, keepdims=True), axis=0, keepdims=True) / D_MODEL
        loss_ref[...] += jnp.broadcast_to(part, (8, 128))
        dy = err * (1.0 / D_MODEL)
        dx3, dgf = _rms_bwd(dy, gf, r4, n4)
        dgf_ref[...] += jnp.broadcast_to(jnp.sum(dgf, axis=0, keepdims=True), (8, D_MODEL))
        dzp = _bf(dx3 * pp * pg * (1.0 - pg))
        dpp = _bf(dx3 * pg)
        dwpg_ref[...] += _tn(h3b, dzp)
        dwpp_ref[...] += _tn(pb, dpp)
        dh3 = _nt(dzp, wpg_ref[...])
        dx, dgp = _rms_bwd(dh3, gp, r3, n3)
        dgp_ref[...] += jnp.broadcast_to(jnp.sum(dgp, axis=0, keepdims=True), (8, D_MODEL))
        dx_ref[...] = dx3 + dx

    rowd = pl.BlockSpec((tm, D_MODEL), lambda i: (i, 0))
    fixed = lambda shp: pl.BlockSpec(shp, lambda i: (0,) * len(shp))
    return pl.pallas_call(
        body, name="tail_fwd_bwd", grid=(t // tm,),
        in_specs=[rowd, pl.BlockSpec((tm, PLE_DIM), lambda i: (i, 0)), rowd,
                  fixed((1, D_MODEL)), fixed((1, D_MODEL)), fixed((D_MODEL, D_MODEL)), fixed((PLE_DIM, D_MODEL))],
        out_specs=[rowd, fixed((D_MODEL, D_MODEL)), fixed((PLE_DIM, D_MODEL)),
                   fixed((8, D_MODEL)), fixed((8, D_MODEL)), fixed((8, 128))],
        out_shape=[jax.ShapeDtypeStruct((t, D_MODEL), F32), jax.ShapeDtypeStruct((D_MODEL, D_MODEL), F32),
                   jax.ShapeDtypeStruct((PLE_DIM, D_MODEL), F32), jax.ShapeDtypeStruct((8, D_MODEL), F32),
                   jax.ShapeDtypeStruct((8, D_MODEL), F32), jax.ShapeDtypeStruct((8, 128), F32)],
        compiler_params=_cp(("arbitrary",), VMEM_LIMIT),
    )(x2, p, target, g_ple, g_final, w_pg, w_pp)


def in_proj_bwd(pieces, weights, x, dx1, g, name="in_proj_bwd", tm=256):
    t = x.shape[0]
    tm = min(tm, t)
    k = len(pieces)

    def body(*refs):
        p_refs, w_refs = refs[:k], refs[k:2 * k]
        x_ref, dx1_ref, g_ref, dx_ref, dg_ref = refs[2 * k:]

        @pl.when(pl.program_id(0) == 0)
        def _():
            dg_ref[...] = jnp.zeros_like(dg_ref)

        dh = _nt(_bf(p_refs[0][...]), w_refs[0][...])
        for pr, wr in zip(p_refs[1:], w_refs[1:]):
            dh = dh + _nt(_bf(pr[...]), wr[...])
        r, n = _rms(x_ref[...])
        dx, dgc = _rms_bwd(dh, g_ref[...], r, n)
        dx_ref[...] = dx1_ref[...] + dx
        dg_ref[...] += jnp.broadcast_to(jnp.sum(dgc, axis=0, keepdims=True), (8, D_MODEL))

    rowd = pl.BlockSpec((tm, D_MODEL), lambda i: (i, 0))
    return pl.pallas_call(
        body, name=name, grid=(t // tm,),
        in_specs=[pl.BlockSpec((tm, a.shape[1]), lambda i: (i, 0)) for a in pieces]
        + [pl.BlockSpec(w.shape, lambda i: (0, 0)) for w in weights]
        + [rowd, rowd, pl.BlockSpec((1, D_MODEL), lambda i: (0, 0))],
        out_specs=[rowd, pl.BlockSpec((8, D_MODEL), lambda i: (0, 0))],
        out_shape=[jax.ShapeDtypeStruct((t, D_MODEL), F32), jax.ShapeDtypeStruct((8, D_MODEL), F32)],
        compiler_params=_cp(("arbitrary",), VMEM_LIMIT),
    )(*pieces, *weights, x, dx1, g)


def adamw(w, g, m, v, name, rows_cap=256):
    r, c = w.shape
    tr = r
    for cand in range(8, min(r, rows_cap) + 1, 8):
        if r % cand == 0:
            tr = cand

    def body(w_ref, g_ref, m_ref, v_ref, d_ref, mo_ref, vo_ref):
        gv = g_ref[...]
        mn = ADAM_B1 * m_ref[...] + (1.0 - ADAM_B1) * gv
        vn = ADAM_B2 * v_ref[...] + (1.0 - ADAM_B2) * (gv * gv)
        m_hat = mn / (1.0 - ADAM_B1 ** ADAM_STEP)
        v_hat = vn / (1.0 - ADAM_B2 ** ADAM_STEP)
        d_ref[...] = -ADAM_LR * (m_hat / (jnp.sqrt(v_hat) + ADAM_EPS) + ADAM_WD * w_ref[...])
        mo_ref[...] = mn
        vo_ref[...] = vn

    spec = pl.BlockSpec((tr, c), lambda i: (i, 0))
    return pl.pallas_call(
        body, name=name, grid=(r // tr,),
        in_specs=[spec] * 4, out_specs=[spec] * 3,
        out_shape=[jax.ShapeDtypeStruct((r, c), F32)] * 3,
        compiler_params=_cp(("parallel",), VMEM_LIMIT),
    )(w, g, m, v)


class Standalone:
    def __init__(self, later_weights):
        self.later_weights = later_weights

    def begin(self, *a):
        return 0.0

    forward = exchange = join = begin

    def finish(self, after):
        return self.later_weights


def local_step(x3d, p3d, target3d, g4, small, later, early):
    b, s, _ = x3d.shape
    t = b * s
    x = x3d.reshape(t, D_MODEL)
    p = p3d.reshape(t, PLE_DIM)
    target = target3d.reshape(t, D_MODEL)
    cut = SPLIT_Z - 2 * (D_IN // N_CHIPS)
    w_inp = jnp.concatenate([g4[2][:, cut + 8:], g4[3], g4[0], g4[1], g4[2][:, :cut], g4[2][:, cut:cut + 8],
                             jnp.zeros((D_MODEL, 120), BF16)], axis=1)
    al_row = jnp.pad(small["a_log"].reshape(1, B_HEADS), ((0, 0), (0, 128 - B_HEADS)))
    dtb_row = jnp.pad(small["dt_bias"].reshape(1, B_HEADS), ((0, 0), (0, 128 - B_HEADS)))
    conv_w8 = jnp.pad(small["conv_w"].reshape(CONV_K, CONV_CH), ((0, 8 - CONV_K), (0, 0)))
    w_on = small["w_onorm"].reshape(1, B_DIM)
    g_mix, g_ffn = small["g_mix"].reshape(1, D_MODEL), small["g_ffn"].reshape(1, D_MODEL)
    g_ple, g_final = small["g_ple"].reshape(1, D_MODEL), small["g_final"].reshape(1, D_MODEL)
    bias_band = make_bias_band(small["rel_bias"].reshape(A_HEADS, N_REL))

    tok = later.begin()
    proj, h1 = rms_matmul(x, g_mix + tok, w_inp, "in_proj", tm=1024)
    y_a, lse = attn_fwd(proj, bias_band, b, s)
    tok = later.forward(lse)
    c = conv_fwd(proj, conv_w8 + tok, b, s)
    u, wk, qg, kdec, pm, egl = dn_prep(c, proj, al_row, dtb_row, b, s)
    o_b, states = dn_scan_fwd(u, wk, qg, kdec, pm, egl, b, s)
    wts = later.finish(o_b)
    x1, merged = mid_fwd(x, y_a, o_b, proj, w_on, wts["w_branch_a"], wts["w_branch_b"], wts["w_out"])
    gate, up, act, h2 = ffn_up(x1, g_ffn, wts["w_gate_up"])
    x2 = matmul_residual(act, wts["w_down"], x1, "ffn_down")

    dx2, dw_pg, dw_pp, dg_ple, dg_final, loss = tail_fwd_bwd(
        x2, p, target, g_ple, g_final, wts["w_ple_gate"], wts["w_ple_proj"])
    dgate, dup = ffn_act_bwd(dx2, gate, up, wts["w_down"])
    w_gu = wts["w_gate_up"]
    dx1, dg_ffn = in_proj_bwd([dgate, dup], [w_gu[:, :D_FF], w_gu[:, D_FF:]], x1, dx2, g_ffn, name="ffn_in_bwd")
    dw_down = matmul_tn(act, dx2, "dw_down")
    dw_gu = matmul_tn(h2, dgate, "dw_gate", width=2 * D_FF)
    dw_gu = matmul_tn(h2, dup, "dw_up", into=dw_gu, col0=D_FF)
    tok = early.begin(dict(w_gate_up=dw_gu, w_down=dw_down, w_ple_gate=dw_pg, w_ple_proj=dw_pp))
    dy_a, do_b, dz, dgates, dw_out, dwa, dwb, dw_on = mid_bwd(
        dx1, merged, y_a, o_b, proj, w_on + tok, wts["w_branch_a"], wts["w_branch_b"], wts["w_out"])
    tok = early.exchange(dz)
    ddw, ddwk, ddqg, ddkdec, ddp, ddegl = dn_scan_bwd(u, wk, qg, kdec, pm, egl + tok, states, do_b, b, s)
    dc3, dbd, dal, ddtb = dn_post_bwd(c, proj, al_row, dtb_row, ddw, ddwk, ddqg, ddkdec, ddp, ddegl, b, s)
    dconv, dconv_w = conv_bwd(proj, conv_w8, dc3, b, s)
    dqa, dka, dva, dbt, dbf = attn_bwd(proj, bias_band, y_a, lse, dy_a, b, s)
    tok = early.join(dqa)
    d_rel = bias_grad(jnp.transpose(dbt.reshape(A_HEADS, CHUNK, TAIL), (1, 0, 2)),
                      dbf.reshape(A_HEADS, CHUNK, 128))[:, :N_REL]

    pieces = [dgates, dqa, dka, dva, dconv, dz, dbd]
    bounds = [0, 2048, 2560, 3072, 3584, 5120, 5632, 5760]
    w_pieces = [w_inp[:, lo:hi] for lo, hi in zip(bounds[:-1], bounds[1:])]
    dx, dg_mix = in_proj_bwd(pieces, w_pieces, x, dx1, g_mix + tok)
    dwp = None
    for k, pc in enumerate(pieces):
        dwp = matmul_tn(h1, pc, "dw_in_%d" % k, into=dwp, col0=bounds[k], width=P_WIDTH)
    reduced_early = early.finish(dwp)
    dw_in = jnp.concatenate([dwp[:, P_QA:P_BD + 8], dwp[:, :P_QA]], axis=1)

    grads = dict(w_in=dw_in, w_branch_a=dwa, w_branch_b=dwb, w_out=dw_out, w_gate_up=dw_gu, w_down=dw_down,
                 w_ple_gate=dw_pg, w_ple_proj=dw_pp)
    small_grads = dict(g_mix=dg_mix[0], g_ffn=dg_ffn[0], g_ple=dg_ple[0], g_final=dg_final[0],
                       conv_w=dconv_w[:CONV_K].reshape(-1), rel_bias=d_rel.reshape(-1), w_onorm=dw_on[0],
                       a_log=dal[0, :B_HEADS], dt_bias=ddtb[0, :B_HEADS], loss=loss[0, :1])
    return dx.reshape(b, s, D_MODEL), grads, small_grads, reduced_early


BIG = (("w_in", (D_MODEL, D_IN), 1), ("w_branch_a", (A_WIDTH, D_MODEL), 1), ("w_branch_b", (B_WIDTH, D_MODEL), 1),
       ("w_out", (D_MODEL, D_MODEL), 0), ("w_gate_up", (D_MODEL, 2 * D_FF), 1), ("w_down", (D_FF, D_MODEL), 0),
       ("w_ple_gate", (D_MODEL, D_MODEL), 0), ("w_ple_proj", (PLE_DIM, D_MODEL), 1))
N_CHIPS = 4
FIRST_WEIGHTS = ("w_in",)
LATER_WEIGHTS = ("w_branch_a", "w_branch_b", "w_out", "w_gate_up", "w_down", "w_ple_gate", "w_ple_proj")
EARLY_GRADS = ("w_gate_up", "w_down", "w_ple_gate", "w_ple_proj")
LATE_GRADS = ("w_in", "w_branch_a", "w_branch_b", "w_out")


def _items(names):
    return [it for it in BIG if it[0] in names]


def _shard_shape(shape, axis):
    return (shape[0] // N_CHIPS, shape[1]) if axis == 0 else (shape[0], shape[1] // N_CHIPS)


def _pack_rows_of(names):
    return -(-sum(sh[0] * sh[1] for _, sh, _ in _items(names)) // (N_CHIPS * 128 * 512)) * 512


def _pack_rows(parts, total):
    used = sum(a.shape[-2] for a in parts)
    pad = jnp.zeros(parts[0].shape[:-2] + (total - used, 128), parts[0].dtype)
    return jnp.concatenate(parts + [pad], axis=-2)


def pack_grads(grads, names):
    parts = []
    for n, shape, axis in _items(names):
        rs, cs = _shard_shape(shape, axis)
        g = grads[n].astype(BF16)
        seg = g.reshape(N_CHIPS, rs, cs) if axis == 0 else jnp.transpose(g.reshape(rs, N_CHIPS, cs), (1, 0, 2))
        parts.append(seg.reshape(N_CHIPS, -1, 128))
    return _pack_rows(parts, _pack_rows_of(names))


def unpack_shard(flat, names):
    out, r0 = {}, 0
    for n, shape, axis in _items(names):
        rs, cs = _shard_shape(shape, axis)
        nr = rs * cs // 128
        out[n] = flat[r0:r0 + nr].reshape(rs, cs)
        r0 += nr
    return out


def _place():
    return lax.axis_index("x"), lax.axis_index("y"), lax.axis_index("c")


ANY = pl.BlockSpec(memory_space=pl.ANY)


def _gathered_shape(item):
    n, shape, _ = item
    return (N_CHIPS,) + _shard_shape(shape, 1) if n == "w_in" else shape


def _gather_block(o_ref, item, cx, cy, hf):
    n, shape, axis = item
    rs, cs = _shard_shape(shape, axis)
    hr = rs // 2
    ci = 2 * cx + cy
    if n == "w_in":
        return o_ref.at[ci, pl.ds(pl.multiple_of(hf * hr, 16), hr), :]
    if axis == 0:
        return o_ref.at[pl.ds(pl.multiple_of(ci * rs + hf * hr, 16), hr), :]
    return o_ref.at[pl.ds(pl.multiple_of(hf * hr, 16), hr), pl.ds(pl.multiple_of(ci * cs, 128), cs)]


def _own_half(w_ref, item, c):
    hr = _shard_shape(item[1], item[2])[0] // 2
    return w_ref.at[pl.ds(pl.multiple_of(c * hr, 16), hr), :]


def _gather_slot(o_ref, item, cx, cy):
    n, shape, axis = item
    rs, cs = _shard_shape(shape, axis)
    ci = 2 * cx + cy
    if n == "w_in":
        return o_ref.at[ci]
    if axis == 0:
        return o_ref.at[pl.ds(pl.multiple_of(ci * rs, 16), rs), :]
    return o_ref.at[:, pl.ds(pl.multiple_of(ci * cs, 128), cs)]


def _other_chips(x, y):
    return [(1 - x, y), (x, 1 - y), (1 - x, 1 - y)]


def allgather_weights(shards, names, chip):
    items = _items(names)
    nw = len(items)

    def body(*refs):
        w_refs, o_refs = refs[:nw], refs[nw:2 * nw]
        send_sems, recv_sems = refs[2 * nw:]
        x, y, c = _place()
        sibling = (x, y, 1 - c)
        chips = _other_chips(x, y)

        def copy(k, src, dst, to):
            return pltpu.make_async_remote_copy(src_ref=src, dst_ref=dst, send_sem=send_sems.at[k],
                                                recv_sem=recv_sems.at[k], device_id=to, device_id_type=MESH)

        def blk(i, cx, cy, hf):
            return _gather_block(o_refs[i], items[i], cx, cy, hf)

        def my_half(i):
            return _own_half(w_refs[i], items[i], c)

        def own(i):
            return _gather_slot(o_refs[i], items[i], x, y)

        first = [copy(7 * i + j, my_half(i), blk(i, x, y, c), (*chip_, c))
                 for i in range(nw) for j, chip_ in enumerate(chips)]
        first += [copy(7 * i + 6, w_refs[i], own(i), sibling) for i in range(nw)]
        for cp in first:
            cp.start()
        passed = []
        for i in range(nw):
            for j, chip_ in enumerate(chips):
                copy(7 * i + j, my_half(i), blk(i, *chip_, c), (*chip_, c)).wait_recv()
                fwd = copy(7 * i + 3 + j, blk(i, *chip_, c), blk(i, *chip_, c), sibling)
                fwd.start()
                passed.append(fwd)
        for i in range(nw):
            for j, chip_ in enumerate(chips):
                copy(7 * i + 3 + j, my_half(i), blk(i, *chip_, 1 - c), sibling).wait_recv()
            copy(7 * i + 6, w_refs[i], own(i), sibling).wait_recv()
        for cp in first + passed:
            cp.wait_send()

    outs = pl.pallas_call(
        body, name="allgather_weights",
        in_specs=[ANY] * nw, out_specs=[ANY] * nw,
        out_shape=[jax.ShapeDtypeStruct(_gathered_shape(it), BF16) for it in items],
        scratch_shapes=[pltpu.SemaphoreType.DMA((7 * nw,)), pltpu.SemaphoreType.DMA((7 * nw,))],
    )(*[shards[it[0]] for it in items])
    return {it[0]: o for it, o in zip(items, outs)}


HBM_SPEC = pl.BlockSpec(memory_space=pltpu.HBM)
SEM_SPEC = pl.BlockSpec(memory_space=pltpu.SEMAPHORE)
EFFECT = pltpu.SideEffectType.DATAFLOW_SIDE_EFFECTING


def _in_hbm(a):
    return pltpu.with_memory_space_constraint(a, pltpu.HBM)


def copies_start(name, bufs, ncopies, plan):
    nb = len(bufs)

    def body(*refs):
        in_refs, send_sems, recv_sems, token = refs[:nb], refs[nb], refs[nb + 1], refs[-1]
        for k, (src, dst, to) in enumerate(plan(in_refs)):
            pltpu.make_async_remote_copy(src_ref=src, dst_ref=dst, send_sem=send_sems.at[k],
                                         recv_sem=recv_sems.at[k], device_id=to, device_id_type=MESH).start()
        token[...] = jnp.zeros_like(token)

    outs = pl.pallas_call(
        body, name=name,
        in_specs=[HBM_SPEC] * nb,
        out_specs=(SEM_SPEC, SEM_SPEC, *[HBM_SPEC] * nb, pl.BlockSpec(memory_space=pltpu.VMEM)),
        out_shape=(pltpu.SemaphoreType.DMA((ncopies,)), pltpu.SemaphoreType.DMA((ncopies,)),
                   *[pltpu.HBM(b.shape, b.dtype) for b in bufs], jax.ShapeDtypeStruct((8, 128), F32)),
        input_output_aliases={i: 2 + i for i in range(nb)},
        compiler_params=pltpu.CompilerParams(has_side_effects=EFFECT),
    )(*[_in_hbm(b) for b in bufs])
    return outs[0], outs[1], list(outs[2:2 + nb]), outs[-1][0, 0]


def copies_wait(name, send_sems, recv_sems, bufs, after, plan):
    nb = len(bufs)

    def body(*refs):
        in_refs, s_sems, r_sems = refs[:nb], refs[nb], refs[nb + 1]
        for k, (src, dst, to) in enumerate(plan(in_refs)):
            cp = pltpu.make_async_remote_copy(src_ref=src, dst_ref=dst, send_sem=s_sems.at[k],
                                              recv_sem=r_sems.at[k], device_id=to, device_id_type=MESH)
            cp.wait_send()
            cp.wait_recv()

    return list(pl.pallas_call(
        body, name=name,
        in_specs=[HBM_SPEC] * nb + [SEM_SPEC, SEM_SPEC, ANY],
        out_specs=tuple([HBM_SPEC] * nb),
        out_shape=tuple(pltpu.HBM(b.shape, b.dtype) for b in bufs),
        input_output_aliases={i: i for i in range(nb)},
        compiler_params=pltpu.CompilerParams(has_side_effects=EFFECT),
    )(*bufs, send_sems, recv_sems, after))


def _landing(shape, dtype):
    return _in_hbm(lax.empty(shape, dtype))


class LaterWeights:
    def __init__(self, shards, chip):
        self.items = _items(LATER_WEIGHTS)
        self.shards, self.chip = shards, chip
        self.nw = len(self.items)

    def _ici_plan(self, refs):
        x, y, c = _place()
        w_refs, o_refs = refs[:self.nw], refs[self.nw:]
        plan = [(_own_half(w_refs[i], it, c), _gather_block(o_refs[i], it, x, y, c), (*chip_, c))
                for i, it in enumerate(self.items) for chip_ in _other_chips(x, y)]
        return plan + [(w_refs[i], _gather_slot(o_refs[i], it, x, y), (x, y, 1 - c))
                       for i, it in enumerate(self.items)]

    def _d2d_plan(self, refs):
        x, y, c = _place()
        return [(_gather_block(refs[i], it, *chip_, c), _gather_block(refs[i], it, *chip_, c), (x, y, 1 - c))
                for i, it in enumerate(self.items) for chip_ in _other_chips(x, y)]

    def _d2d_wait_plan(self, refs):
        x, y, c = _place()
        return [(_gather_block(refs[i], it, *chip_, c), _gather_block(refs[i], it, *chip_, 1 - c), (x, y, 1 - c))
                for i, it in enumerate(self.items) for chip_ in _other_chips(x, y)]

    def _ici_wait_plan(self, refs):
        x, y, c = _place()
        w_refs, o_refs = refs[:self.nw], refs[self.nw:]
        plan = [(_own_half(w_refs[i], it, c), _gather_block(o_refs[i], it, *chip_, c), (*chip_, c))
                for i, it in enumerate(self.items) for chip_ in _other_chips(x, y)]
        return plan + [(w_refs[i], _gather_slot(o_refs[i], it, x, y), (x, y, 1 - c))
                       for i, it in enumerate(self.items)]

    def begin(self):
        srcs = [self.shards[it[0]] for it in self.items]
        lands = [_landing(_gathered_shape(it), BF16) for it in self.items]
        self.s1, self.r1, self.b1, tok = copies_start("gather_ici_start", srcs + lands, 4 * self.nw, self._ici_plan)
        return tok

    def forward(self, after):
        b1 = copies_wait("gather_ici_wait", self.s1, self.r1, self.b1, after, self._ici_wait_plan)
        self.s2, self.r2, self.b2, tok = copies_start("gather_d2d_start", b1[self.nw:], 3 * self.nw, self._d2d_plan)
        return tok

    def finish(self, after):
        outs = copies_wait("gather_d2d_wait", self.s2, self.r2, self.b2, after, self._d2d_wait_plan)
        return {it[0]: o for it, o in zip(self.items, outs)}


def small_allreduce(v, name):
    r = v.shape[0]

    def body(v_ref, o_ref, buf, send_sems, recv_sems):
        x, y, c = _place()
        me = 4 * x + 2 * y + c
        buf[me] = v_ref[...]
        flips = [(fx, fy, fc) for fx in (0, 1) for fy in (0, 1) for fc in (0, 1)][1:]
        peers = [((1 - x) if fx else x, (1 - y) if fy else y, (1 - c) if fc else c) for fx, fy, fc in flips]

        def copy(k, slot, to):
            return pltpu.make_async_remote_copy(src_ref=v_ref, dst_ref=buf.at[slot], send_sem=send_sems.at[k],
                                                recv_sem=recv_sems.at[k], device_id=to, device_id_type=MESH)

        sends = [copy(k, me, peer) for k, peer in enumerate(peers)]
        for cp in sends:
            cp.start()
        for k, (px, py, pc) in enumerate(peers):
            copy(k, 4 * px + 2 * py + pc, (px, py, pc)).wait_recv()
        for cp in sends:
            cp.wait_send()
        acc = buf[0]
        for d in range(1, 8):
            acc = acc + buf[d]
        o_ref[...] = acc

    return pl.pallas_call(
        body, name=name,
        in_specs=[pl.BlockSpec(memory_space=pltpu.VMEM)], out_specs=pl.BlockSpec(memory_space=pltpu.VMEM),
        out_shape=jax.ShapeDtypeStruct((r, 128), F32),
        scratch_shapes=[pltpu.VMEM((8, r, 128), F32), pltpu.SemaphoreType.DMA((7,)), pltpu.SemaphoreType.DMA((7,))],
    )(v)


def swap_halves(g):
    half = g.shape[1] // 2

    def body(g_ref, o_ref, send_sem, recv_sem):
        x, y, c = _place()
        cp = pltpu.make_async_remote_copy(
            src_ref=g_ref.at[:, pl.ds((1 - c) * half, half), :], dst_ref=o_ref, send_sem=send_sem,
            recv_sem=recv_sem, device_id=(x, y, 1 - c), device_id_type=MESH)
        cp.start()
        cp.wait()

    return pl.pallas_call(
        body, name="swap_halves", in_specs=[ANY], out_specs=ANY,
        out_shape=jax.ShapeDtypeStruct((N_CHIPS, half, 128), g.dtype),
        scratch_shapes=[pltpu.SemaphoreType.DMA, pltpu.SemaphoreType.DMA],
    )(g)


def add_halves(g, other, place):
    half = other.shape[1]
    tr = _tile_rows(half)
    nblk = half // tr

    def body(pref, g0, g1, g2, g3, o0, o1, o2, o3, pf_ref, pb_ref):
        f = lambda r: r[...].astype(F32)
        pf_ref[...] = f(g0) + f(o0)
        pb_ref[0] = _bf(f(g1) + f(o1))
        pb_ref[1] = _bf(f(g2) + f(o2))
        pb_ref[2] = _bf(f(g3) + f(o3))

    gspec = lambda k: pl.BlockSpec((None, tr, 128), lambda i, pr: ((pr[0] + k) % N_CHIPS, pr[1] * nblk + i, 0))
    ospec = lambda k: pl.BlockSpec((None, tr, 128), lambda i, pr: ((pr[0] + k) % N_CHIPS, i, 0))
    return pl.pallas_call(
        body, name="add_halves",
        grid_spec=pltpu.PrefetchScalarGridSpec(
            num_scalar_prefetch=1, grid=(nblk,),
            in_specs=[gspec(0), gspec(1), gspec(2), gspec(3), ospec(0), ospec(1), ospec(2), ospec(3)],
            out_specs=[pl.BlockSpec((tr, 128), lambda i, pr: (i, 0)),
                       pl.BlockSpec((3, tr, 128), lambda i, pr: (0, i, 0))]),
        out_shape=[jax.ShapeDtypeStruct((half, 128), F32), jax.ShapeDtypeStruct((3, half, 128), BF16)],
        compiler_params=_cp(("parallel",), VMEM_LIMIT),
    )(place, g, g, g, g, other, other, other, other)


def _tile_rows(n, cap=2048):
    best = 16
    for t in range(16, cap + 1, 16):
        if n % t == 0:
            best = t
    assert n % best == 0
    return best


def exchange_partials(pb):
    def body(p_ref, o_ref, send_sems, recv_sems):
        x, y, c = _place()
        me = 2 * x + y
        cps = []
        for k in range(1, N_CHIPS):
            to = (me + k) % N_CHIPS
            cps.append(pltpu.make_async_remote_copy(
                src_ref=p_ref.at[k - 1], dst_ref=o_ref.at[k - 1], send_sem=send_sems.at[k - 1],
                recv_sem=recv_sems.at[k - 1], device_id=(to // 2, to % 2, c), device_id_type=MESH))
        for cp in cps:
            cp.start()
        for cp in cps:
            cp.wait()

    return pl.pallas_call(
        body, name="exchange_partials", in_specs=[ANY], out_specs=ANY,
        out_shape=jax.ShapeDtypeStruct(pb.shape, pb.dtype),
        scratch_shapes=[pltpu.SemaphoreType.DMA((3,)), pltpu.SemaphoreType.DMA((3,))],
    )(pb)


def add_partials(pf, got, place):
    half = pf.shape[0]
    tr = _tile_rows(half)

    def body(pref, pf_ref, got_ref, o_ref):
        o_ref[...] = ((pf_ref[...] + got_ref[0].astype(F32)) + got_ref[1].astype(F32)) + got_ref[2].astype(F32)

    return pl.pallas_call(
        body, name="add_partials",
        grid_spec=pltpu.PrefetchScalarGridSpec(
            num_scalar_prefetch=1, grid=(half // tr,),
            in_specs=[pl.BlockSpec((tr, 128), lambda i, pr: (i, 0)),
                      pl.BlockSpec((3, tr, 128), lambda i, pr: (0, i, 0))],
            out_specs=pl.BlockSpec((None, tr, 128), lambda i, pr: (pr[1], i, 0))),
        out_shape=jax.ShapeDtypeStruct((2, half, 128), F32),
        compiler_params=_cp(("parallel",), VMEM_LIMIT),
    )(place, pf, got)


def join_halves(both):
    def body(r_ref, o_ref, send_sem, recv_sem):
        x, y, c = _place()
        cp = pltpu.make_async_remote_copy(src_ref=r_ref.at[c], dst_ref=o_ref.at[c], send_sem=send_sem,
                                          recv_sem=recv_sem, device_id=(x, y, 1 - c), device_id_type=MESH)
        cp.start()
        pltpu.make_async_remote_copy(src_ref=r_ref.at[c], dst_ref=o_ref.at[1 - c], send_sem=send_sem,
                                     recv_sem=recv_sem, device_id=(x, y, 1 - c), device_id_type=MESH).wait_recv()
        cp.wait_send()

    return pl.pallas_call(
        body, name="join_halves", in_specs=[ANY], out_specs=ANY,
        out_shape=jax.ShapeDtypeStruct(both.shape, F32),
        scratch_shapes=[pltpu.SemaphoreType.DMA, pltpu.SemaphoreType.DMA],
        input_output_aliases={0: 0},
    )(both)


def reduce_scatter_grads(gpack, place):
    other = swap_halves(gpack)
    pf, pb = add_halves(gpack, other, place)
    got = exchange_partials(pb)
    return join_halves(add_partials(pf, got, place)).reshape(gpack.shape[1], 128)


class EarlyGrads:
    def __init__(self, place):
        self.place = place

    @staticmethod
    def _swap_plan(refs):
        x, y, c = _place()
        g_ref, o_ref = refs
        half = o_ref.shape[1]
        return [(g_ref.at[:, pl.ds(pl.multiple_of((1 - c) * half, 16), half), :], o_ref, (x, y, 1 - c))]

    @staticmethod
    def _exchange_plan(refs):
        x, y, c = _place()
        p_ref, o_ref = refs
        me = 2 * x + y
        return [(p_ref.at[k - 1], o_ref.at[k - 1], (((me + k) % N_CHIPS) // 2, ((me + k) % N_CHIPS) % 2, c))
                for k in range(1, N_CHIPS)]

    @staticmethod
    def _join_plan(refs):
        x, y, c = _place()
        return [(refs[0].at[c], refs[0].at[c], (x, y, 1 - c))]

    @staticmethod
    def _join_wait_plan(refs):
        x, y, c = _place()
        return [(refs[0].at[c], refs[0].at[1 - c], (x, y, 1 - c))]

    def begin(self, grads):
        g = pack_grads(grads, EARLY_GRADS)
        land = _landing((N_CHIPS, g.shape[1] // 2, 128), BF16)
        self.s1, self.r1, self.b1, tok = copies_start("grads_swap_start", [g, land], 1, self._swap_plan)
        return tok

    def exchange(self, after):
        g, other = copies_wait("grads_swap_wait", self.s1, self.r1, self.b1, after, self._swap_plan)
        self.pf, pb = add_halves(g, other, self.place)
        land = _landing(pb.shape, BF16)
        self.s2, self.r2, self.b2, tok = copies_start("grads_exchange_start", [pb, land], 3, self._exchange_plan)
        return tok

    def join(self, after):
        _, got = copies_wait("grads_exchange_wait", self.s2, self.r2, self.b2, after, self._exchange_plan)
        both = add_partials(self.pf, got, self.place)
        self.s3, self.r3, self.b3, tok = copies_start("grads_join_start", [both], 1, self._join_plan)
        return tok

    def finish(self, after):
        (both,) = copies_wait("grads_join_wait", self.s3, self.r3, self.b3, after, self._join_wait_plan)
        return both.reshape(-1, 128)


SMALL = (("g_mix", D_MODEL), ("g_ffn", D_MODEL), ("g_ple", D_MODEL), ("g_final", D_MODEL),
         ("conv_w", CONV_K * CONV_CH), ("rel_bias", A_HEADS * N_REL), ("w_onorm", B_DIM),
         ("a_log", B_HEADS), ("dt_bias", B_HEADS), ("loss", 1))


def _pad128(v):
    v = v.reshape(-1)
    return jnp.pad(v, (0, -v.shape[0] % 128))


def pack_small(d, names, rows):
    flat = jnp.concatenate([_pad128(d[n]) for n in names]).reshape(-1, 128)
    return jnp.pad(flat, ((0, rows - flat.shape[0]), (0, 0)))


def unpack_small(flat, names_sizes):
    out, r0 = {}, 0
    v = flat.reshape(-1)
    for n, size in names_sizes:
        out[n] = v[r0:r0 + size]
        r0 += -(-size // 128) * 128
    return out


def kernel(x, p, g_mix, w_in, conv_w, a_log, dt_bias, rel_bias, w_onorm, w_branch_a, w_branch_b, w_out, g_ffn, w_gate_up, w_down, g_ple, w_ple_gate, w_ple_proj, g_final, loss_target, m_g_mix, m_w_in, m_conv_w, m_a_log, m_dt_bias, m_rel_bias, m_w_onorm, m_w_branch_a, m_w_branch_b, m_w_out, m_g_ffn, m_w_gate_up, m_w_down, m_g_ple, m_w_ple_gate, m_w_ple_proj, m_g_final, v_g_mix, v_w_in, v_conv_w, v_a_log, v_dt_bias, v_rel_bias, v_w_onorm, v_w_branch_a, v_w_branch_b, v_w_out, v_g_ffn, v_w_gate_up, v_w_down, v_g_ple, v_w_ple_gate, v_w_ple_proj, v_g_final):
    names = ["g_mix", "w_in", "conv_w", "a_log", "dt_bias", "rel_bias", "w_onorm", "w_branch_a", "w_branch_b",
             "w_out", "g_ffn", "w_gate_up", "w_down", "g_ple", "w_ple_gate", "w_ple_proj", "g_final"]
    w = dict(zip(names, [g_mix, w_in, conv_w, a_log, dt_bias, rel_bias, w_onorm, w_branch_a, w_branch_b, w_out,
                         g_ffn, w_gate_up, w_down, g_ple, w_ple_gate, w_ple_proj, g_final]))
    m = dict(zip(names, [m_g_mix, m_w_in, m_conv_w, m_a_log, m_dt_bias, m_rel_bias, m_w_onorm, m_w_branch_a,
                         m_w_branch_b, m_w_out, m_g_ffn, m_w_gate_up, m_w_down, m_g_ple, m_w_ple_gate,
                         m_w_ple_proj, m_g_final]))
    v = dict(zip(names, [v_g_mix, v_w_in, v_conv_w, v_a_log, v_dt_bias, v_rel_bias, v_w_onorm, v_w_branch_a,
                         v_w_branch_b, v_w_out, v_g_ffn, v_w_gate_up, v_w_down, v_g_ple, v_w_ple_gate,
                         v_w_ple_proj, v_g_final]))
    xi, yi, ci = _place()
    chip = 2 * xi + yi
    big_names = [n for n, _, _ in BIG]

    shards2d = {n: w[n].reshape(w[n].shape[-2:]) for n in big_names}
    shards_bf = {n: a.astype(BF16) for n, a in shards2d.items()}
    g4 = allgather_weights(shards_bf, FIRST_WEIGHTS, chip)["w_in"]
    place = jnp.stack([chip, ci]).astype(jnp.int32)
    conv_sh = jnp.where(ci == 0, w["conv_w"].reshape(CONV_K, CONV_CH // N_CHIPS), 0.0)
    conv_slots = lax.dynamic_update_slice(jnp.zeros((N_CHIPS, CONV_K, CONV_CH // N_CHIPS), F32), conv_sh[None],
                                          (chip, 0, 0))
    conv_all = small_allreduce(conv_slots.reshape(-1, 128), "gather_conv_w")
    conv_full = jnp.transpose(conv_all.reshape(N_CHIPS, CONV_K, CONV_CH // N_CHIPS), (1, 0, 2)).reshape(CONV_K, CONV_CH)
    small = {n: w[n] for n in names if n not in big_names}
    small["conv_w"] = conv_full

    grad_x, grads, small_grads, reduced_early = local_step(
        x, p[0], loss_target, g4, small, LaterWeights(shards_bf, chip), EarlyGrads(place))

    reduced_late = reduce_scatter_grads(pack_grads(grads, LATE_GRADS), place)
    gshard = {**unpack_shard(reduced_early, EARLY_GRADS), **unpack_shard(reduced_late, LATE_GRADS)}
    small_names = [n for n, _ in SMALL]
    red = unpack_small(small_allreduce(pack_small(small_grads, small_names, 112), "allreduce_small"), SMALL)
    loss = red["loss"][0]
    conv_g = lax.dynamic_slice(red["conv_w"].reshape(CONV_K, N_CHIPS, CONV_CH // N_CHIPS), (0, chip, 0),
                               (CONV_K, 1, CONV_CH // N_CHIPS))
    gsmall = {n: red[n].reshape(w[n].shape) for n in small_names if n not in ("loss", "conv_w")}
    gsmall["conv_w"] = conv_g.reshape(w["conv_w"].shape)

    grad, delta, new_m, new_v = {}, {}, {}, {}
    for n in big_names:
        shp = w[n].shape
        d_, m_, v_ = adamw(shards2d[n], gshard[n], m[n].reshape(shp[-2:]), v[n].reshape(shp[-2:]), "adamw_" + n)
        grad[n], delta[n], new_m[n], new_v[n] = gshard[n].reshape(shp), d_.reshape(shp), m_.reshape(shp), v_.reshape(shp)
    snames = [n for n in small_names if n != "loss"]
    ssizes = [(n, w[n].size) for n in snames]
    pk = lambda d: pack_small(d, snames, 64)
    d_, m_, v_ = adamw(pk(w), pk(gsmall), pk(m), pk(v), "adamw_small")
    ds, ms, vs = unpack_small(d_, ssizes), unpack_small(m_, ssizes), unpack_small(v_, ssizes)
    for n in snames:
        shp = w[n].shape
        grad[n], delta[n], new_m[n], new_v[n] = gsmall[n], ds[n].reshape(shp), ms[n].reshape(shp), vs[n].reshape(shp)

    return (loss, grad_x, *[grad[n] for n in names], *[delta[n] for n in names],
            *[new_m[n] for n in names], *[new_v[n] for n in names])
```

```python
import functools

import jax
import jax.numpy as jnp
from jax import lax
from jax.experimental import pallas as pl
from jax.experimental.pallas import tpu as pltpu

F32 = jnp.float32
BF16 = jnp.bfloat16
HI = lax.Precision.HIGHEST
MESH = pl.DeviceIdType.MESH

D_MODEL = 1024
CHUNK = 64
PLE_DIM = 256
EPS = 1e-6
A_HEADS = 8
A_HEAD_DIM = 64
A_WIDTH = 512
A_LOOKBACK = 8
BAND = (A_LOOKBACK + 1) * CHUNK
TAIL = 3 * CHUNK
REL_CLIP = 128
N_REL = 2 * REL_CLIP + 1
B_HEADS = 4
B_DIM = 128
B_WIDTH = 512
CONV_K = 4
CONV_CH = 1536
D_FF = 2816
SPLIT_Z = 3584
D_IN = 5640
ADAM_LR, ADAM_B1, ADAM_B2, ADAM_EPS, ADAM_WD, ADAM_STEP = 0.001, 0.9, 0.999, 1e-08, 0.01, 10

P_GATES, P_QA, P_KA, P_VA, P_CONV, P_Z, P_BD, P_WIDTH = 0, 2048, 2560, 3072, 3584, 5120, 5632, 5760

VMEM_LIMIT = 56 * 1024 * 1024


def _cp(sem, vmem=None, **kw):
    return pltpu.CompilerParams(dimension_semantics=sem, vmem_limit_bytes=vmem, **kw)


def _tile(n, cap):
    best = None
    for t in range(128, cap + 1, 128):
        if n % t == 0:
            best = t
    assert best is not None, (n, cap)
    return best


def _nn(a, b, prec=None):
    return lax.dot_general(a, b, (((1,), (0,)), ((), ())), preferred_element_type=F32, precision=prec)


def _nt(a, b, prec=None):
    return lax.dot_general(a, b, (((1,), (1,)), ((), ())), preferred_element_type=F32, precision=prec)


def _tn(a, b, prec=None):
    return lax.dot_general(a, b, (((0,), (0,)), ((), ())), preferred_element_type=F32, precision=prec)


def _bnn(a, b, prec=None):
    return lax.dot_general(a, b, (((2,), (1,)), ((0,), (0,))), preferred_element_type=F32, precision=prec)


def _bnt(a, b, prec=None):
    return lax.dot_general(a, b, (((2,), (2,)), ((0,), (0,))), preferred_element_type=F32, precision=prec)


def _bf(a):
    return a.astype(BF16)


def _split(a):
    hi = a.astype(BF16)
    return hi, (a - hi.astype(F32)).astype(BF16)


def _bnn_exact(lhs_b, rhs):
    h1 = _bf(rhs)
    r1 = rhs - h1.astype(F32)
    h2 = _bf(r1)
    h3 = _bf(r1 - h2.astype(F32))
    return _bnn(lhs_b, h1) + (_bnn(lhs_b, h2) + _bnn(lhs_b, h3))


def _bnn3(a, b):
    ah, al = a if isinstance(a, tuple) else _split(a)
    bh, bl = b if isinstance(b, tuple) else _split(b)
    return _bnn(ah, bh) + (_bnn(ah, bl) + _bnn(al, bh))


def _sigmoid(x):
    return 0.5 * jnp.tanh(0.5 * x) + 0.5


def _softplus(x):
    return jnp.maximum(x, 0.0) + jnp.log(1.0 + jnp.exp(-jnp.abs(x)))


def rms_matmul(x, g, w, name, tm=512, tn_cap=1024):
    t, d = x.shape
    n = w.shape[1]
    tm = min(tm, t)
    tn = _tile(n, tn_cap)

    def body(x_ref, g_ref, w_ref, o_ref, h_ref):
        @pl.when(pl.program_id(1) == 0)
        def _():
            xv = x_ref[...]
            r = lax.rsqrt(jnp.mean(xv * xv, axis=-1, keepdims=True) + EPS)
            h_ref[...] = _bf(xv * r * g_ref[...])

        o_ref[...] = _bf(_nn(h_ref[...], w_ref[...]))

    return pl.pallas_call(
        body, name=name, grid=(t // tm, n // tn),
        in_specs=[pl.BlockSpec((tm, d), lambda i, j: (i, 0)),
                  pl.BlockSpec((1, d), lambda i, j: (0, 0)),
                  pl.BlockSpec((d, tn), lambda i, j: (0, j))],
        out_specs=[pl.BlockSpec((tm, tn), lambda i, j: (i, j)),
                   pl.BlockSpec((tm, d), lambda i, j: (i, 0))],
        out_shape=[jax.ShapeDtypeStruct((t, n), BF16), jax.ShapeDtypeStruct((t, d), BF16)],
        compiler_params=_cp(("parallel", "arbitrary"), VMEM_LIMIT),
    )(x, g, w)


def matmul_tn(a, b, name, into=None, col0=0, width=None, tm=1024, tk_cap=1408, tn_cap=1408):
    m, k1 = a.shape
    n = b.shape[1]
    tm = min(tm, m)
    tk = _tile(k1, tk_cap)
    tn = _tile(n, tn_cap)
    while col0 % tn:
        tn = _tile(n, tn - 128)
    nk = m // tm
    c0 = col0 // tn

    def body(*refs):
        a_ref, b_ref, o_ref, acc = refs[0], refs[1], refs[-2], refs[-1]

        @pl.when(pl.program_id(2) == 0)
        def _():
            acc[...] = jnp.zeros_like(acc)

        acc[...] += _tn(_bf(a_ref[...]), _bf(b_ref[...]))

        @pl.when(pl.program_id(2) == nk - 1)
        def _():
            o_ref[...] = _bf(acc[...])

    in_specs = [pl.BlockSpec((tm, tk), lambda i, j, k: (k, i)),
                pl.BlockSpec((tm, tn), lambda i, j, k: (k, j))]
    args = [a, b]
    total = n if width is None else width
    aliases = {}
    if into is not None:
        in_specs.append(ANY)
        args.append(into)
        total = into.shape[1]
        aliases = {2: 0}
    return pl.pallas_call(
        body, name=name, grid=(k1 // tk, n // tn, nk),
        in_specs=in_specs,
        out_specs=pl.BlockSpec((tk, tn), lambda i, j, k: (i, c0 + j)),
        out_shape=jax.ShapeDtypeStruct((k1, total), BF16),
        scratch_shapes=[pltpu.VMEM((tk, tn), F32)],
        input_output_aliases=aliases,
        compiler_params=_cp(("parallel", "parallel", "arbitrary"), VMEM_LIMIT),
    )(*args)


def _tail_onehot(qi):
    r = lax.broadcasted_iota(jnp.int32, (384, TAIL), 0)
    kj = lax.broadcasted_iota(jnp.int32, (384, TAIL), 1)
    return (r == jnp.minimum(REL_CLIP + qi - kj, REL_CLIP) + REL_CLIP).astype(F32)


def bias_tail(rel_pad):
    def body(rb_ref, o_ref):
        rb = rb_ref[...]
        for qi in range(CHUNK):
            o_ref[qi] = _nn(rb, _tail_onehot(qi), HI)

    return pl.pallas_call(
        body, name="bias_tail",
        out_shape=jax.ShapeDtypeStruct((CHUNK, A_HEADS, TAIL), F32),
    )(rel_pad)


def bias_grad(db_t, db_far):
    def body(t_ref, f_ref, o_ref):
        acc = jnp.zeros((A_HEADS, 384), F32)
        for qi in range(CHUNK):
            acc = acc + _nt(t_ref[qi], _tail_onehot(qi), HI)
        far = jnp.sum(jnp.sum(f_ref[...], axis=2), axis=1, keepdims=True)
        lane = lax.broadcasted_iota(jnp.int32, (A_HEADS, 384), 1)
        o_ref[...] = acc + jnp.where(lane == 2 * REL_CLIP, far, 0.0)

    return pl.pallas_call(
        body, name="bias_grad",
        out_shape=jax.ShapeDtypeStruct((A_HEADS, 384), F32),
    )(db_t, db_far)


ATT_CB = 8


WIN = BAND + CHUNK


def _stack_heads(a, lane):
    return jnp.concatenate([jnp.where(lane < 64, a, 0.0), jnp.where(lane >= 64, a, 0.0)], axis=0)


def _fill_band_pads(k_ref, v_ref, kp, vp, s):
    z = jnp.zeros((A_LOOKBACK * CHUNK, 128), BF16)
    kp[pl.ds(0, A_LOOKBACK * CHUNK), :] = z
    vp[pl.ds(0, A_LOOKBACK * CHUNK), :] = z
    kp[pl.ds(A_LOOKBACK * CHUNK, s), :] = _bf(k_ref[...])
    vp[pl.ds(A_LOOKBACK * CHUNK, s), :] = _bf(v_ref[...])


def attn_fwd(proj, bias_band, b, s):
    t = b * s
    nc = s // CHUNK
    qb, kb_, vb_ = P_QA // 128, P_KA // 128, P_VA // 128

    nstep = nc // ATT_CB
    rows = ATT_CB * CHUNK

    def body(q_ref, k_ref, v_ref, b_ref, o_ref, lse_ref, kp, vp):
        n0 = pl.program_id(2) * ATT_CB

        @pl.when(n0 == 0)
        def _():
            _fill_band_pads(k_ref, v_ref, kp, vp, s)

        lane = lax.broadcasted_iota(jnp.int32, (2 * CHUNK, 128), 1)
        col = lax.broadcasted_iota(jnp.int32, (4 * CHUNK, WIN), 1)
        bias4 = b_ref[...]

        def pair(pp, carry):
            n = n0 + 2 * pp
            r0 = pl.multiple_of(pp * 2 * CHUNK, 2 * CHUNK)
            start = pl.multiple_of(n * CHUNK, CHUNK)
            kb = kp[pl.ds(start, WIN), :]
            vb = vp[pl.ds(start, WIN), :]
            q4 = _stack_heads(q_ref[pl.ds(r0, 2 * CHUNK), :] * (A_HEAD_DIM ** -0.5), lane)
            sc = jnp.where(col >= (A_LOOKBACK - n) * CHUNK, _nt(_bf(q4), kb) + bias4, -1e30)
            mx = jnp.max(sc, axis=1, keepdims=True)
            p = jnp.exp(sc - mx)
            l = jnp.sum(p, axis=1, keepdims=True)
            o4 = _nn(_bf(p), vb) / l
            lse4 = mx + jnp.log(l)
            o_ref[pl.ds(r0, 2 * CHUNK), :] = jnp.where(lane < 64, o4[:2 * CHUNK], o4[2 * CHUNK:])
            lse_ref[pl.ds(r0, 2 * CHUNK), :] = jnp.where(lane < 64, lse4[:2 * CHUNK], lse4[2 * CHUNK:])
            return carry

        lax.fori_loop(0, ATT_CB // 2, pair, 0)

    return pl.pallas_call(
        body, name="attn_fwd", grid=(b, 4, nstep),
        in_specs=[pl.BlockSpec((rows, 128), lambda bb, m, n: (bb * nstep + n, qb + m)),
                  pl.BlockSpec((s, 128), lambda bb, m, n: (bb, kb_ + m)),
                  pl.BlockSpec((s, 128), lambda bb, m, n: (bb, vb_ + m)),
                  pl.BlockSpec((None, 4 * CHUNK, WIN), lambda bb, m, n: (m, 0, 0))],
        out_specs=[pl.BlockSpec((rows, 128), lambda bb, m, n: (bb * nstep + n, m)),
                   pl.BlockSpec((rows, 128), lambda bb, m, n: (bb * nstep + n, m))],
        out_shape=[jax.ShapeDtypeStruct((t, A_WIDTH), F32), jax.ShapeDtypeStruct((t, A_WIDTH), F32)],
        scratch_shapes=[pltpu.VMEM((s + A_LOOKBACK * CHUNK, 128), BF16),
                        pltpu.VMEM((s + A_LOOKBACK * CHUNK, 128), BF16)],
        compiler_params=_cp(("parallel", "parallel", "arbitrary"), VMEM_LIMIT),
    )(proj, proj, proj, bias_band)


def attn_bwd(proj, bias_band, y_a, lse, dy_a, b, s):
    t = b * s
    nc = s // CHUNK
    qb, kb_, vb_ = P_QA // 128, P_KA // 128, P_VA // 128
    pad = A_LOOKBACK * CHUNK
    nstep = nc // ATT_CB
    rows = ATT_CB * CHUNK

    def body(q_ref, k_ref, v_ref, b_ref, do_ref, o_ref, lse_ref,
             dq_ref, dk_ref, dv_ref, dbt_ref, dbf_ref, kp, vp, dkp, dvp):
        bb = pl.program_id(1)
        n0 = pl.program_id(2) * ATT_CB

        @pl.when(n0 == 0)
        def _():
            _fill_band_pads(k_ref, v_ref, kp, vp, s)
            dkp[...] = jnp.zeros_like(dkp)
            dvp[...] = jnp.zeros_like(dvp)

        @pl.when((n0 == 0) & (bb == 0))
        def _():
            dbt_ref[...] = jnp.zeros_like(dbt_ref)
            dbf_ref[...] = jnp.zeros_like(dbf_ref)

        lane = lax.broadcasted_iota(jnp.int32, (2 * CHUNK, 128), 1)
        col = lax.broadcasted_iota(jnp.int32, (4 * CHUNK, WIN), 1)
        bias4 = b_ref[...]

        def pair(pp, carry):
            n = n0 + 2 * pp
            r0 = pl.multiple_of(pp * 2 * CHUNK, 2 * CHUNK)
            start = pl.multiple_of(n * CHUNK, CHUNK)
            kb = kp[pl.ds(start, WIN), :]
            vb = vp[pl.ds(start, WIN), :]
            q4b = _bf(_stack_heads(q_ref[pl.ds(r0, 2 * CHUNK), :] * (A_HEAD_DIM ** -0.5), lane))
            do4 = _stack_heads(do_ref[pl.ds(r0, 2 * CHUNK), :], lane)
            do4b = _bf(do4)
            o = o_ref[pl.ds(r0, 2 * CHUNK), :]
            lsev = lse_ref[pl.ds(r0, 2 * CHUNK), :]
            lse4 = jnp.concatenate([lsev[:, 0:1], lsev[:, 64:65]], axis=0)
            sc = jnp.where(col >= (A_LOOKBACK - n) * CHUNK, _nt(q4b, kb) + bias4, -1e30)
            p = jnp.exp(sc - lse4)
            dp = _nt(do4b, vb)
            delta = jnp.sum(do4 * jnp.concatenate([o, o], axis=0), axis=1, keepdims=True)
            ds = p * (dp - delta)
            dsb = _bf(ds)
            dq4 = _nn(dsb, kb)
            dq_ref[pl.ds(r0, 2 * CHUNK), :] = _bf(
                jnp.where(lane < 64, dq4[:2 * CHUNK], dq4[2 * CHUNK:]) * (A_HEAD_DIM ** -0.5))
            dkp[pl.ds(start, WIN), :] += _tn(dsb, q4b)
            dvp[pl.ds(start, WIN), :] += _tn(_bf(p), do4b)
            dbt_ref[...] += ds[:, WIN - 256:]
            dbf_ref[...] += ds[:, 0:128] + ds[:, 128:256] + ds[:, 256:384]
            return carry

        lax.fori_loop(0, ATT_CB // 2, pair, 0)

        @pl.when(n0 == nc - ATT_CB)
        def _():
            dk_ref[...] = _bf(dkp[pl.ds(pad, s), :])
            dv_ref[...] = _bf(dvp[pl.ds(pad, s), :])

    return pl.pallas_call(
        body, name="attn_bwd", grid=(4, b, nstep),
        in_specs=[pl.BlockSpec((rows, 128), lambda m, bb, n: (bb * nstep + n, qb + m)),
                  pl.BlockSpec((s, 128), lambda m, bb, n: (bb, kb_ + m)),
                  pl.BlockSpec((s, 128), lambda m, bb, n: (bb, vb_ + m)),
                  pl.BlockSpec((None, 4 * CHUNK, WIN), lambda m, bb, n: (m, 0, 0)),
                  pl.BlockSpec((rows, 128), lambda m, bb, n: (bb * nstep + n, m)),
                  pl.BlockSpec((rows, 128), lambda m, bb, n: (bb * nstep + n, m)),
                  pl.BlockSpec((rows, 128), lambda m, bb, n: (bb * nstep + n, m))],
        out_specs=[pl.BlockSpec((rows, 128), lambda m, bb, n: (bb * nstep + n, m)),
                   pl.BlockSpec((s, 128), lambda m, bb, n: (bb, m)),
                   pl.BlockSpec((s, 128), lambda m, bb, n: (bb, m)),
                   pl.BlockSpec((None, 4 * CHUNK, 256), lambda m, bb, n: (m, 0, 0)),
                   pl.BlockSpec((None, 4 * CHUNK, 128), lambda m, bb, n: (m, 0, 0))],
        out_shape=[jax.ShapeDtypeStruct((t, A_WIDTH), BF16)] * 3
        + [jax.ShapeDtypeStruct((4, 4 * CHUNK, 256), F32),
           jax.ShapeDtypeStruct((4, 4 * CHUNK, 128), F32)],
        scratch_shapes=[pltpu.VMEM((s + pad, 128), BF16), pltpu.VMEM((s + pad, 128), BF16),
                        pltpu.VMEM((s + pad, 128), F32), pltpu.VMEM((s + pad, 128), F32)],
        compiler_params=_cp(("parallel", "arbitrary", "arbitrary"), VMEM_LIMIT),
    )(proj, proj, proj, bias_band, dy_a, y_a, lse)


def _conv_taps(x, w, s):
    row = lax.broadcasted_iota(jnp.int32, x.shape, 0)
    shifted = [x] + [jnp.where(row >= i, pltpu.roll(x, i, 0), 0.0) for i in range(1, CONV_K)]
    acc = shifted[0] * w[CONV_K - 1:CONV_K, :]
    for i in range(1, CONV_K):
        acc = acc + shifted[i] * w[CONV_K - 1 - i:CONV_K - i, :]
    return acc, shifted


def conv_fwd(proj, conv_w8, b, s):
    cb = 512
    c0 = P_CONV // cb

    def body(x_ref, w_ref, o_ref):
        a, _ = _conv_taps(x_ref[...].astype(F32), w_ref[...], s)
        o_ref[...] = a * _sigmoid(a)

    return pl.pallas_call(
        body, name="conv_fwd", grid=(b, CONV_CH // cb),
        in_specs=[pl.BlockSpec((s, cb), lambda bb, j: (bb, c0 + j)),
                  pl.BlockSpec((8, cb), lambda bb, j: (0, j))],
        out_specs=pl.BlockSpec((s, cb), lambda bb, j: (bb, j)),
        out_shape=jax.ShapeDtypeStruct((b * s, CONV_CH), F32),
        compiler_params=_cp(("parallel", "parallel"), VMEM_LIMIT),
    )(proj, conv_w8)


def conv_bwd(proj, conv_w8, dc3, b, s):
    cb = 512
    c0 = P_CONV // cb

    def body(x_ref, w_ref, dc_ref, dx_ref, dw_ref):
        @pl.when(pl.program_id(1) == 0)
        def _():
            dw_ref[...] = jnp.zeros_like(dw_ref)

        w = w_ref[...]
        a, shifted = _conv_taps(x_ref[...].astype(F32), w, s)
        sg = _sigmoid(a)
        da = dc_ref[...] * (sg * (1.0 + a * (1.0 - sg)))
        row = lax.broadcasted_iota(jnp.int32, da.shape, 0)
        dx = da * w[CONV_K - 1:CONV_K, :]
        for i in range(1, CONV_K):
            dx = dx + jnp.where(row < s - i, pltpu.roll(da, s - i, 0), 0.0) * w[CONV_K - 1 - i:CONV_K - i, :]
        dx_ref[...] = _bf(dx)
        r8 =lax.broadcasted_iota(jnp.int32, (8, cb), 0)
        dw = jnp.zeros((8, cb), F32)
        for i in range(CONV_K):
            dw = dw + jnp.where(r8 == CONV_K - 1 - i, jnp.sum(da * shifted[i], axis=0, keepdims=True), 0.0)
        dw_ref[...] += dw

    return pl.pallas_call(
        body, name="conv_bwd", grid=(CONV_CH // cb, b),
        in_specs=[pl.BlockSpec((s, cb), lambda j, bb: (bb, c0 + j)),
                  pl.BlockSpec((8, cb), lambda j, bb: (0, j)),
                  pl.BlockSpec((None, s, cb), lambda j, bb: (j, bb, 0))],
        out_specs=[pl.BlockSpec((s, cb), lambda j, bb: (bb, j)),
                   pl.BlockSpec((8, cb), lambda j, bb: (0, j))],
        out_shape=[jax.ShapeDtypeStruct((b * s, CONV_CH), BF16), jax.ShapeDtypeStruct((8, CONV_CH), F32)],
        compiler_params=_cp(("parallel", "arbitrary"), VMEM_LIMIT),
    )(proj, conv_w8, dc3)


def _pick_lane(v, k):
    lane = lax.broadcasted_iota(jnp.int32, v.shape, 1)
    return jnp.sum(jnp.where(lane == k, v, 0.0), axis=1, keepdims=True)


def _chunk_masks(ncb):
    i = lax.broadcasted_iota(jnp.int32, (ncb, CHUNK, CHUNK), 1)
    j = lax.broadcasted_iota(jnp.int32, (ncb, CHUNK, CHUNK), 2)
    return i, j


def _col_of_row(rowvec, eye):
    return jnp.sum(jnp.where(eye, rowvec, 0.0), axis=2, keepdims=True)


def _dn_chunk_math(cq, ck, cv, bd, al_row, dtb_row, h, ncb, tm=None):
    r = ncb * CHUNK
    i, j = _chunk_masks(ncb)
    eye = i == j
    low = i >= j
    strict = i > j
    ones = jnp.ones((ncb, CHUNK, CHUNK), F32)

    braw = _pick_lane(bd, h)
    draw = _pick_lane(bd, B_HEADS + h)
    al = _pick_lane(al_row, h)
    dtb = _pick_lane(dtb_row, h)
    ea = jnp.exp(al)
    beta = _sigmoid(braw)
    sp_arg = draw + dtb
    g = -ea * _softplus(sp_arg)

    rq = lax.rsqrt(jnp.sum(cq * cq, axis=1, keepdims=True) + EPS)
    rk = lax.rsqrt(jnp.sum(ck * ck, axis=1, keepdims=True) + EPS)
    nq = cq * rq
    kn = ck * rk
    qn = nq * (B_DIM ** -0.5)

    def c3(a):
        return a.reshape(ncb, CHUNK, a.shape[-1])

    qn3, kn3, v3, beta3 = c3(qn), c3(kn), c3(cv), c3(beta)
    gb = jnp.broadcast_to(c3(g), (ncb, CHUNK, CHUNK))
    gc_b = _bnn_exact(low.astype(BF16), gb)
    gr_b = _bnn_exact(_bf(ones), jnp.where(eye, gc_b, 0.0))
    dm = jnp.where(low, jnp.exp(jnp.where(low, gc_b - gr_b, 0.0)), 0.0)
    gc = gc_b[:, :, 0:1]
    gl = gc_b[:, CHUNK - 1:CHUNK, 0:1]
    gam = jnp.exp(gc)
    egl = jnp.exp(gl)
    edec = jnp.exp(gl - gc)

    knb = _bf(kn3)
    kk = _bnt(knb, knb)
    kd = jnp.where(strict, kk * dm, 0.0)
    a = beta3 * kd
    sz = 1 if tm is None else CHUNK
    if tm is None:
        tm = eye.astype(F32)
    while sz < CHUNK:
        off = jnp.where(((i // (2 * sz)) == (j // (2 * sz))) & ((i // sz) != (j // sz)), a, 0.0)
        tmb = _bf(tm)
        tm = tm - _bnn(_bf(_bnn(tmb, _bf(off))), tmb)
        sz *= 2
    bv = beta3 * v3
    bk = (beta3 * gam) * kn3
    sol = _bnn3(_split(tm), jnp.concatenate([bv, bk], axis=2))
    u, wk = sol[:, :, :B_DIM], sol[:, :, B_DIM:]
    qk = _bnt(_bf(qn3), knb)
    p = jnp.where(low, qk * dm, 0.0)
    kdec = kn3 * edec
    qg = gam * qn3
    return dict(beta=beta3, g=c3(g), ea=ea, sp_arg=c3(sp_arg), rq=c3(rq), rk=c3(rk), nq=c3(nq),
                qn=qn3, kn=kn3, v=v3, gc=gc, gl=gl, gam=gam, egl=egl, edec=edec, dm=dm, kd=kd, a=a,
                tm=tm, u=u, wk=wk, qk=qk, p=p, kdec=kdec, qg=qg, eye=eye, low=low, strict=strict)


def dn_prep(c, proj, al_row, dtb_row, b, s, ncb=8):
    t = b * s
    r = ncb * CHUNK
    nblk = t // r
    bd_blk = P_BD // 128

    def body(cq_ref, ck_ref, cv_ref, bd_ref, al_ref, dtb_ref, u_ref, wk_ref, qg_ref, kdec_ref, p_ref, egl_ref,
             tm_ref):
        h = pl.program_id(1)
        m = _dn_chunk_math(cq_ref[...], ck_ref[...], cv_ref[...], bd_ref[...].astype(F32), al_ref[...], dtb_ref[...], h, ncb)
        tm_ref[...] = m["tm"].reshape(r, CHUNK)
        u_ref[...] = m["u"].reshape(r, B_DIM)
        wk_ref[...] = m["wk"].reshape(r, B_DIM)
        qg_ref[...] = m["qg"].reshape(r, B_DIM)
        kdec_ref[...] = m["kdec"].reshape(r, B_DIM)
        p_ref[...] = m["p"].reshape(r, CHUNK)
        egl_ref[...] = jnp.broadcast_to(m["egl"], (ncb, 8, 128)).reshape(ncb * 8, 128)

    col = lambda k: pl.BlockSpec((r, 128), lambda i, h: (i, k * B_HEADS + h))
    out_col = pl.BlockSpec((r, 128), lambda i, h: (i, h))
    small = pl.BlockSpec((1, 128), lambda i, h: (0, 0))
    return pl.pallas_call(
        body, name="dn_prep", grid=(nblk, B_HEADS),
        in_specs=[col(0), col(1), col(2), pl.BlockSpec((r, 128), lambda i, h: (i, bd_blk)), small, small],
        out_specs=[out_col, out_col, out_col, out_col,
                   pl.BlockSpec((None, r, CHUNK), lambda i, h: (h, i, 0)),
                   pl.BlockSpec((None, ncb * 8, 128), lambda i, h: (h, i, 0)),
                   pl.BlockSpec((None, r, CHUNK), lambda i, h: (h, i, 0))],
        out_shape=[jax.ShapeDtypeStruct((t, B_WIDTH), F32)] * 4
        + [jax.ShapeDtypeStruct((B_HEADS, t, CHUNK), F32),
           jax.ShapeDtypeStruct((B_HEADS, t // 8, 128), F32),
           jax.ShapeDtypeStruct((B_HEADS, t, CHUNK), F32)],
        compiler_params=_cp(("parallel", "parallel"), VMEM_LIMIT),
    )(c, c, c, proj, al_row, dtb_row)


def dn_scan_fwd(u, wk, qg, kdec, p, egl, b, s):
    t = b * s
    nc = s // CHUNK

    def body(u_ref, wk_ref, qg_ref, kdec_ref, p_ref, egl_ref, o_ref, ss_ref, st):
        @pl.when(pl.program_id(0) == 0)
        def _():
            st[...] = jnp.zeros_like(st)

        for bb in range(b):
            for h in range(B_HEADS):
                sl = slice(h * B_DIM, (h + 1) * B_DIM)
                sh = st[bb * B_HEADS + h]
                ss_ref[bb, h] = sh
                sb = _bf(sh)
                w = u_ref[bb, :, sl] - _nt(_bf(wk_ref[bb, :, sl]), sb)
                o_ref[bb, :, sl] = _nt(_bf(qg_ref[bb, :, sl]), sb) + _nn(_bf(p_ref[h, bb]), _bf(w))
                st[bb * B_HEADS + h] = egl_ref[h, bb][0:1, :] * sh + _tn(_bf(w), _bf(kdec_ref[bb, :, sl]))

    r3 = lambda a: a.reshape(b, s, B_WIDTH)
    act = pl.BlockSpec((b, CHUNK, B_WIDTH), lambda n: (0, n, 0))
    o, states = pl.pallas_call(
        body, name="dn_scan_fwd", grid=(nc,),
        in_specs=[act, act, act, act,
                  pl.BlockSpec((B_HEADS, b, CHUNK, CHUNK), lambda n: (0, 0, n, 0)),
                  pl.BlockSpec((B_HEADS, b, 8, 128), lambda n: (0, 0, n, 0))],
        out_specs=[act, pl.BlockSpec((b, None, B_HEADS, B_DIM, B_DIM), lambda n: (0, n, 0, 0, 0))],
        out_shape=[jax.ShapeDtypeStruct((b, s, B_WIDTH), F32),
                   jax.ShapeDtypeStruct((b, nc, B_HEADS, B_DIM, B_DIM), F32)],
        scratch_shapes=[pltpu.VMEM((b * B_HEADS, B_DIM, B_DIM), F32)],
        compiler_params=_cp(("arbitrary",), VMEM_LIMIT),
    )(r3(u), r3(wk), r3(qg), r3(kdec), p.reshape(B_HEADS, b, s, CHUNK), egl.reshape(B_HEADS, b, s // 8, 128))
    return o.reshape(t, B_WIDTH), states


def dn_scan_bwd(u, wk, qg, kdec, p, egl, states, do, b, s):
    t = b * s
    nc = s // CHUNK

    def body(u_ref, wk_ref, qg_ref, kdec_ref, p_ref, egl_ref, ss_ref, do_ref,
             dw_ref, dwk_ref, dqg_ref, dkdec_ref, dp_ref, degl_ref, dst):
        @pl.when(pl.program_id(0) == 0)
        def _():
            dst[...] = jnp.zeros_like(dst)

        for bb in range(b):
            for h in range(B_HEADS):
                sl = slice(h * B_DIM, (h + 1) * B_DIM)
                k = bb * B_HEADS + h
                sh = ss_ref[bb, h]
                sb = _bf(sh)
                dsp = dst[k]
                dsb = _bf(dsp)
                wkb = _bf(wk_ref[bb, :, sl])
                kdb = _bf(kdec_ref[bb, :, sl])
                pb = _bf(p_ref[h, bb])
                dob = _bf(do_ref[bb, :, sl])
                w = u_ref[bb, :, sl] - _nt(wkb, sb)
                wb = _bf(w)
                dw = _tn(pb, dob) + _nt(kdb, dsb)
                dwb = _bf(dw)
                dw_ref[bb, :, sl] = dw
                dqg_ref[bb, :, sl] = _nn(dob, sb)
                dwk_ref[bb, :, sl] = -_nn(dwb, sb)
                dkdec_ref[bb, :, sl] = _nn(wb, dsb)
                dp_ref[h, bb] = _nt(dob, wb)
                tot = jnp.sum(jnp.sum(sh * dsp, axis=1, keepdims=True), axis=0, keepdims=True)
                degl_ref[h, bb] = jnp.broadcast_to(tot, (8, 128))
                dst[k] = egl_ref[h, bb][0:1, :] * dsp + _tn(dob, _bf(qg_ref[bb, :, sl])) - _tn(dwb, wkb)

    r3 = lambda a: a.reshape(b, s, B_WIDTH)
    act = pl.BlockSpec((b, CHUNK, B_WIDTH), lambda n: (0, nc - 1 - n, 0))
    pspec = pl.BlockSpec((B_HEADS, b, CHUNK, CHUNK), lambda n: (0, 0, nc - 1 - n, 0))
    espec = pl.BlockSpec((B_HEADS, b, 8, 128), lambda n: (0, 0, nc - 1 - n, 0))
    outs = pl.pallas_call(
        body, name="dn_scan_bwd", grid=(nc,),
        in_specs=[act, act, act, act, pspec, espec,
                  pl.BlockSpec((b, None, B_HEADS, B_DIM, B_DIM), lambda n: (0, nc - 1 - n, 0, 0, 0)),
                  act],
        out_specs=[act, act, act, act, pspec, espec],
        out_shape=[jax.ShapeDtypeStruct((b, s, B_WIDTH), F32)] * 4
        + [jax.ShapeDtypeStruct((B_HEADS, b, s, CHUNK), F32),
           jax.ShapeDtypeStruct((B_HEADS, b, s // 8, 128), F32)],
        scratch_shapes=[pltpu.VMEM((b * B_HEADS, B_DIM, B_DIM), F32)],
        compiler_params=_cp(("arbitrary",), VMEM_LIMIT),
    )(r3(u), r3(wk), r3(qg), r3(kdec), p.reshape(B_HEADS, b, s, CHUNK), egl.reshape(B_HEADS, b, s // 8, 128),
      states, r3(do))
    return (*[a.reshape(t, B_WIDTH) for a in outs[:4]], outs[4].reshape(B_HEADS, t, CHUNK),
            outs[5].reshape(B_HEADS, t // 8, 128))


def dn_post_bwd(c, proj, al_row, dtb_row, tmat, dw, dwk, dqg, dkdec, dp, degl, b, s, ncb=8):
    t = b * s
    r = ncb * CHUNK
    nblk = t // r
    bd_blk = P_BD // 128

    def body(cq_ref, ck_ref, cv_ref, bd_ref, al_ref, dtb_ref, tm_ref, dw_ref, dwk_ref, dqg_ref, dkdec_ref, dp_ref,
             degl_ref, dc_ref, dbd_ref, dal_ref, ddtb_ref):
        h = pl.program_id(1)

        @pl.when((pl.program_id(0) == 0) & (h == 0))
        def _():
            dal_ref[...] = jnp.zeros_like(dal_ref)
            ddtb_ref[...] = jnp.zeros_like(ddtb_ref)

        m = _dn_chunk_math(cq_ref[...], ck_ref[...], cv_ref[...], bd_ref[...].astype(F32), al_ref[...], dtb_ref[...], h, ncb,
                           tm=tm_ref[...].reshape(ncb, CHUNK, CHUNK))
        eye, low, strict = m["eye"], m["low"], m["strict"]
        eyef = eye.astype(F32)

        def c3(a):
            return a.reshape(ncb, CHUNK, a.shape[-1])

        du, dwkv, dqg, dkdec = c3(dw_ref[...]), c3(dwk_ref[...]), c3(dqg_ref[...]), c3(dkdec_ref[...])
        dpm = jnp.where(low, c3(dp_ref[...]), 0.0)
        degl = degl_ref[...].reshape(ncb, 8, 128)[:, 0:1, 0:1]
        beta, gam, kn, qn, v = m["beta"], m["gam"], m["kn"], m["qn"], m["v"]
        dm, kd, a, p = m["dm"], m["kd"], m["a"], m["p"]
        knb, qnb = _bf(kn), _bf(qn)

        eyeb = _bf(eyef)
        th, tl = _split(m["tm"])
        tts = (_bf(_bnt(eyeb, th)), _bf(_bnt(eyeb, tl)))
        xy = _bnn3(tts, jnp.concatenate([du, dwkv], axis=2))
        x, y = xy[:, :, :B_DIM], xy[:, :, B_DIM:]
        da = -jnp.where(strict, _bnt(_bf(x), _bf(m["u"])) + _bnt(_bf(y), _bf(m["wk"])), 0.0)
        dv = beta * x
        sy = jnp.sum(y * kn, axis=2, keepdims=True)
        dbeta = jnp.sum(x * v, axis=2, keepdims=True) + gam * sy + jnp.sum(da * kd, axis=2, keepdims=True)
        dgam = beta * sy + jnp.sum(dqg * qn, axis=2, keepdims=True)
        dkk = da * beta * dm
        dqk = dpm * dm
        dkkb, dqkb = _bf(dkk), _bf(dqk)
        dkn = ((beta * gam) * y + _bnn(dkkb, knb) + _bnn(_bf(_bnt(eyeb, dkkb)), knb)
               + _bnn(_bf(_bnt(eyeb, dqkb)), qnb) + dkdec * m["edec"])
        dqn = gam * dqg + _bnn(dqkb, knb)
        mm = da * a + dpm * p
        ek = jnp.sum(dkdec * m["kdec"], axis=2, keepdims=True)
        dgc = (jnp.sum(mm, axis=2, keepdims=True) - _col_of_row(jnp.sum(mm, axis=1, keepdims=True), eye)
               + dgam * gam - ek)
        dgl = jnp.sum(ek, axis=1, keepdims=True) + degl * m["egl"]
        i, _ = _chunk_masks(ncb)
        dgc = dgc + jnp.where(i[:, :, 0:1] == CHUNK - 1, dgl, 0.0)
        upper = (i <= _chunk_masks(ncb)[1]).astype(BF16)
        dg = _bnn_exact(upper, jnp.broadcast_to(dgc, (ncb, CHUNK, CHUNK)))[:, :, 0:1]

        nq = m["nq"]
        dnq = dqn * (B_DIM ** -0.5)
        dcq = m["rq"] * (dnq - nq * jnp.sum(nq * dnq, axis=2, keepdims=True))
        dck = m["rk"] * (dkn - kn * jnp.sum(kn * dkn, axis=2, keepdims=True))
        dc_ref[0] = dcq.reshape(r, B_DIM)
        dc_ref[1] = dck.reshape(r, B_DIM)
        dc_ref[2] = dv.reshape(r, B_DIM)

        dbraw = (dbeta * beta * (1.0 - beta)).reshape(r, 1)
        sgm = _sigmoid(m["sp_arg"])
        ddraw3 = dg * (-m["ea"]) * sgm
        ddraw = ddraw3.reshape(r, 1)
        lane = lax.broadcasted_iota(jnp.int32, (r, 128), 1)
        contrib = jnp.where(lane == h, dbraw, 0.0) + jnp.where(lane == B_HEADS + h, ddraw, 0.0)

        @pl.when(h == 0)
        def _():
            dbd_ref[...] = contrib

        @pl.when(h != 0)
        def _():
            dbd_ref[...] += contrib

        lane8 = lax.broadcasted_iota(jnp.int32, (8, 128), 1)
        tot_al = jnp.sum(jnp.sum(dg * m["g"], axis=1, keepdims=True), axis=0, keepdims=True).reshape(1, 1)
        tot_dtb = jnp.sum(jnp.sum(ddraw3, axis=1, keepdims=True), axis=0, keepdims=True).reshape(1, 1)
        dal_ref[...] += jnp.where(lane8 == h, tot_al, 0.0)
        ddtb_ref[...] += jnp.where(lane8 == h, tot_dtb, 0.0)

    col = lambda k: pl.BlockSpec((r, 128), lambda i, h: (i, k * B_HEADS + h))
    hcol = pl.BlockSpec((r, 128), lambda i, h: (i, h))
    small = pl.BlockSpec((1, 128), lambda i, h: (0, 0))
    acc = pl.BlockSpec((8, 128), lambda i, h: (0, 0))
    return pl.pallas_call(
        body, name="dn_post_bwd", grid=(nblk, B_HEADS),
        in_specs=[col(0), col(1), col(2), pl.BlockSpec((r, 128), lambda i, h: (i, bd_blk)), small, small,
                  pl.BlockSpec((None, r, CHUNK), lambda i, h: (h, i, 0)),
                  hcol, hcol, hcol, hcol,
                  pl.BlockSpec((None, r, CHUNK), lambda i, h: (h, i, 0)),
                  pl.BlockSpec((None, ncb * 8, 128), lambda i, h: (h, i, 0))],
        out_specs=[pl.BlockSpec((3, r, 128), lambda i, h: (0, i, h)),
                   pl.BlockSpec((r, 128), lambda i, h: (i, 0)), acc, acc],
        out_shape=[jax.ShapeDtypeStruct((3, t, B_WIDTH), F32), jax.ShapeDtypeStruct((t, 128), F32),
                   jax.ShapeDtypeStruct((8, 128), F32), jax.ShapeDtypeStruct((8, 128), F32)],
        compiler_params=_cp(("arbitrary", "arbitrary"), VMEM_LIMIT),
    )(c, c, c, proj, al_row, dtb_row, tmat, dw, dwk, dqg, dkdec, dp, degl)


def make_bias_band(rel_bias):
    tail = bias_tail(jnp.pad(rel_bias, ((0, 0), (0, 384 - N_REL))))
    far = jnp.broadcast_to(rel_bias[:, 2 * REL_CLIP][:, None, None], (A_HEADS, CHUNK, BAND - TAIL))
    band = jnp.concatenate([far, jnp.transpose(tail, (1, 0, 2))], axis=2)
    off = jnp.full((A_HEADS, CHUNK, CHUNK), -1e30, F32)
    both = jnp.stack([jnp.concatenate([band, off], axis=2), jnp.concatenate([off, band], axis=2)], axis=1)
    return both.reshape(4, 4 * CHUNK, WIN)


def bias_band_grad(dbt, dbf):
    t5 = dbt.reshape(A_HEADS, 2, CHUNK, 256)
    tail = t5[:, 0, :, :TAIL] + t5[:, 1, :, CHUNK:]
    far = dbf.reshape(A_HEADS, 2, CHUNK, 128).sum(axis=1) + jnp.pad(t5[:, 1, :, :CHUNK], ((0, 0), (0, 0), (0, CHUNK)))
    return bias_grad(jnp.transpose(tail, (1, 0, 2)), far)[:, :N_REL]


def _rms(x):
    r = lax.rsqrt(jnp.mean(x * x, axis=-1, keepdims=True) + EPS)
    return r, x * r


def _rms_bwd(dh, g, r, n):
    dn = dh * g
    return r * (dn - n * jnp.mean(dn * n, axis=-1, keepdims=True)), dh * n


def _gated_onorm(o, z, w_on):
    parts = []
    for h in range(B_HEADS):
        sl = slice(h * B_DIM, (h + 1) * B_DIM)
        r, n = _rms(o[:, sl])
        parts.append((r, n))
    r4 = [p[0] for p in parts]
    n4 = jnp.concatenate([p[1] for p in parts], axis=1)
    w4 = jnp.concatenate([w_on] * B_HEADS, axis=1)
    sz = _sigmoid(z)
    silu = z * sz
    return n4 * w4 * silu, r4, n4, w4, sz, silu


def mid_fwd(x, y_a, o_b, proj, w_on, wa, wb, w_out, tm=256):
    t = x.shape[0]
    tm = min(tm, t)

    def body(x_ref, ya_ref, ob_ref, z_ref, ga_ref, gb_ref, won_ref, wa_ref, wb_ref, wo_ref, x1_ref, mg_ref):
        yb = _gated_onorm(ob_ref[...], z_ref[...].astype(F32), won_ref[...])[0]
        ua = _nn(_bf(ya_ref[...]), wa_ref[...])
        ub = _nn(_bf(yb), wb_ref[...])
        merged = _sigmoid(ga_ref[...].astype(F32)) * ua + _sigmoid(gb_ref[...].astype(F32)) * ub
        mb = _bf(merged)
        mg_ref[...] = mb
        x1_ref[...] = x_ref[...] + _nn(mb, wo_ref[...])

    rowd = pl.BlockSpec((tm, D_MODEL), lambda i: (i, 0))
    row5 = pl.BlockSpec((tm, 512), lambda i: (i, 0))
    full = lambda a: pl.BlockSpec(a.shape, lambda i: (0,) * a.ndim)
    return pl.pallas_call(
        body, name="mid_fwd", grid=(t // tm,),
        in_specs=[rowd, row5, row5,
                  pl.BlockSpec((tm, 512), lambda i: (i, P_Z // 512)),
                  pl.BlockSpec((tm, D_MODEL), lambda i: (i, 0)),
                  pl.BlockSpec((tm, D_MODEL), lambda i: (i, 1)),
                  full(w_on), full(wa), full(wb), full(w_out)],
        out_specs=[rowd, rowd],
        out_shape=[jax.ShapeDtypeStruct((t, D_MODEL), F32), jax.ShapeDtypeStruct((t, D_MODEL), BF16)],
        compiler_params=_cp(("parallel",), VMEM_LIMIT),
    )(x, y_a, o_b, proj, proj, proj, w_on, wa, wb, w_out)


def mid_bwd(dx1, merged, y_a, o_b, proj, w_on, wa, wb, w_out, tm=256):
    t = dx1.shape[0]
    tm = min(tm, t)

    def body(dx1_ref, mg_ref, ya_ref, ob_ref, z_ref, ga_ref, gb_ref, won_ref, wa_ref, wb_ref, wo_ref,
             dya_ref, dob_ref, dz_ref, dg_ref, dwo_ref, dwa_ref, dwb_ref, dwon_ref):
        @pl.when(pl.program_id(0) == 0)
        def _():
            dwo_ref[...] = jnp.zeros_like(dwo_ref)
            dwa_ref[...] = jnp.zeros_like(dwa_ref)
            dwb_ref[...] = jnp.zeros_like(dwb_ref)
            dwon_ref[...] = jnp.zeros_like(dwon_ref)

        dx1b = _bf(dx1_ref[...])
        dmerged = _nt(dx1b, wo_ref[...])
        dwo_ref[...] += _tn(mg_ref[...], dx1b)
        o = ob_ref[...]
        z = z_ref[...].astype(F32)
        yb, r4, n4, w4, sz, silu = _gated_onorm(o, z, won_ref[...])
        yab, ybb = _bf(ya_ref[...]), _bf(yb)
        ua = _nn(yab, wa_ref[...])
        ub = _nn(ybb, wb_ref[...])
        sa, sb = _sigmoid(ga_ref[...].astype(F32)), _sigmoid(gb_ref[...].astype(F32))
        dua, dub = _bf(dmerged * sa), _bf(dmerged * sb)
        dg_ref[:, 0:D_MODEL] = _bf(dmerged * ua * sa * (1.0 - sa))
        dg_ref[:, D_MODEL:2 * D_MODEL] = _bf(dmerged * ub * sb * (1.0 - sb))
        dwa_ref[...] += _tn(yab, dua)
        dwb_ref[...] += _tn(ybb, dub)
        dya_ref[...] = _nt(dua, wa_ref[...])
        dyb = _nt(dub, wb_ref[...])
        dz_ref[...] = _bf(dyb * (n4 * w4) * (sz * (1.0 + z * (1.0 - sz))))
        dnw = dyb * silu
        dwon = jnp.zeros((1, B_DIM), F32)
        for h in range(B_HEADS):
            sl = slice(h * B_DIM, (h + 1) * B_DIM)
            dxh, dgh = _rms_bwd(dnw[:, sl], won_ref[...], r4[h], n4[:, sl])
            dob_ref[:, sl] = dxh
            dwon = dwon + jnp.sum(dgh, axis=0, keepdims=True)
        dwon_ref[...] += jnp.broadcast_to(dwon, (8, B_DIM))

    rowd = pl.BlockSpec((tm, D_MODEL), lambda i: (i, 0))
    row5 = pl.BlockSpec((tm, 512), lambda i: (i, 0))
    full = lambda a: pl.BlockSpec(a.shape, lambda i: (0,) * a.ndim)
    fixed = lambda shp: pl.BlockSpec(shp, lambda i: (0,) * len(shp))
    return pl.pallas_call(
        body, name="mid_bwd", grid=(t // tm,),
        in_specs=[rowd, rowd, row5, row5,
                  pl.BlockSpec((tm, 512), lambda i: (i, P_Z // 512)),
                  pl.BlockSpec((tm, D_MODEL), lambda i: (i, 0)),
                  pl.BlockSpec((tm, D_MODEL), lambda i: (i, 1)),
                  full(w_on), full(wa), full(wb), full(w_out)],
        out_specs=[row5, row5, row5, pl.BlockSpec((tm, 2 * D_MODEL), lambda i: (i, 0)),
                   fixed((D_MODEL, D_MODEL)), fixed((A_WIDTH, D_MODEL)), fixed((B_WIDTH, D_MODEL)),
                   fixed((8, B_DIM))],
        out_shape=[jax.ShapeDtypeStruct((t, 512), F32), jax.ShapeDtypeStruct((t, 512), F32),
                   jax.ShapeDtypeStruct((t, 512), BF16), jax.ShapeDtypeStruct((t, 2 * D_MODEL), BF16),
           jax.ShapeDtypeStruct((D_MODEL, D_MODEL), F32), jax.ShapeDtypeStruct((A_WIDTH, D_MODEL), F32),
           jax.ShapeDtypeStruct((B_WIDTH, D_MODEL), F32), jax.ShapeDtypeStruct((8, B_DIM), F32)],
        compiler_params=_cp(("arbitrary",), VMEM_LIMIT),
    )(dx1, merged, y_a, o_b, proj, proj, proj, w_on, wa, wb, w_out)


FFN_TF = 1408


def ffn_up(x1, g, w_gu, tm=512, tf=FFN_TF):
    t = x1.shape[0]
    tm = min(tm, t)
    nf = D_FF // tf

    def body(x_ref, g_ref, wg_ref, wu_ref, gate_ref, up_ref, act_ref, h_ref):
        @pl.when(pl.program_id(1) == 0)
        def _():
            r, n = _rms(x_ref[...])
            h_ref[...] = _bf(n * g_ref[...])

        hb = h_ref[...]
        gate = _nn(hb, wg_ref[...])
        up = _nn(hb, wu_ref[...])
        gate_ref[...] = _bf(gate)
        up_ref[...] = _bf(up)
        act_ref[...] = _bf(gate * _sigmoid(gate) * up)

    ff = pl.BlockSpec((tm, tf), lambda i, j: (i, j))
    return pl.pallas_call(
        body, name="ffn_up", grid=(t // tm, nf),
        in_specs=[pl.BlockSpec((tm, D_MODEL), lambda i, j: (i, 0)),
                  pl.BlockSpec((1, D_MODEL), lambda i, j: (0, 0)),
                  pl.BlockSpec((D_MODEL, tf), lambda i, j: (0, j)),
                  pl.BlockSpec((D_MODEL, tf), lambda i, j: (0, nf + j))],
        out_specs=[ff, ff, ff, pl.BlockSpec((tm, D_MODEL), lambda i, j: (i, 0))],
        out_shape=[jax.ShapeDtypeStruct((t, D_FF), BF16)] * 3 + [jax.ShapeDtypeStruct((t, D_MODEL), BF16)],
        compiler_params=_cp(("parallel", "arbitrary"), VMEM_LIMIT),
    )(x1, g, w_gu, w_gu)


def matmul_residual(a, w, res, name, tm=512, tk=FFN_TF):
    t, k = a.shape
    n = w.shape[1]
    tm = min(tm, t)

    def body(a_ref, w_ref, r_ref, o_ref):
        @pl.when(pl.program_id(1) == 0)
        def _():
            o_ref[...] = r_ref[...]

        o_ref[...] += _nn(a_ref[...], w_ref[...])

    return pl.pallas_call(
        body, name=name, grid=(t // tm, k // tk),
        in_specs=[pl.BlockSpec((tm, tk), lambda i, j: (i, j)),
                  pl.BlockSpec((tk, n), lambda i, j: (j, 0)),
                  pl.BlockSpec((tm, n), lambda i, j: (i, 0))],
        out_specs=pl.BlockSpec((tm, n), lambda i, j: (i, 0)),
        out_shape=jax.ShapeDtypeStruct((t, n), F32),
        compiler_params=_cp(("parallel", "arbitrary"), VMEM_LIMIT),
    )(a, w, res)


def ffn_act_bwd(dx2, gate, up, w_down, tm=512, tf=FFN_TF):
    t = dx2.shape[0]
    tm = min(tm, t)

    def body(dx2_ref, gate_ref, up_ref, wd_ref, dgate_ref, dup_ref, dx2b_ref):
        @pl.when(pl.program_id(1) == 0)
        def _():
            dx2b_ref[...] = _bf(dx2_ref[...])

        dact = _nt(dx2b_ref[...], wd_ref[...])
        gt, upv = gate_ref[...].astype(F32), up_ref[...].astype(F32)
        sg = _sigmoid(gt)
        dgate_ref[...] = _bf(dact * upv * (sg * (1.0 + gt * (1.0 - sg))))
        dup_ref[...] = _bf(dact * (gt * sg))

    ff = pl.BlockSpec((tm, tf), lambda i, j: (i, j))
    return pl.pallas_call(
        body, name="ffn_act_bwd", grid=(t // tm, D_FF // tf),
        in_specs=[pl.BlockSpec((tm, D_MODEL), lambda i, j: (i, 0)), ff, ff,
                  pl.BlockSpec((tf, D_MODEL), lambda i, j: (j, 0))],
        out_specs=[ff, ff],
        out_shape=[jax.ShapeDtypeStruct((t, D_FF), BF16)] * 2,
        scratch_shapes=[pltpu.VMEM((tm, D_MODEL), BF16)],
        compiler_params=_cp(("parallel", "arbitrary"), VMEM_LIMIT),
    )(dx2, gate, up, w_down)


def tail_fwd_bwd(x2, p, target, g_ple, g_final, w_pg, w_pp, tm=256):
    t = x2.shape[0]
    tm = min(tm, t)

    def body(x_ref, p_ref, t_ref, gp_ref, gf_ref, wpg_ref, wpp_ref,
             dx_ref, dwpg_ref, dwpp_ref, dgp_ref, dgf_ref, loss_ref):
        @pl.when(pl.program_id(0) == 0)
        def _():
            dwpg_ref[...] = jnp.zeros_like(dwpg_ref)
            dwpp_ref[...] = jnp.zeros_like(dwpp_ref)
            dgp_ref[...] = jnp.zeros_like(dgp_ref)
            dgf_ref[...] = jnp.zeros_like(dgf_ref)
            loss_ref[...] = jnp.zeros_like(loss_ref)

        x2v = x_ref[...]
        gp, gf = gp_ref[...], gf_ref[...]
        r3, n3 = _rms(x2v)
        h3b = _bf(n3 * gp)
        pb = _bf(p_ref[...])
        pg = _sigmoid(_nn(h3b, wpg_ref[...]))
        pp = _nn(pb, wpp_ref[...])
        x3 = x2v + pg * pp
        r4, n4 = _rms(x3)
        err = n4 * gf - t_ref[...]
        part = 0.5 * jnp.sum(jnp.sum(err * err, axis=1, keepdims=True), axis=0, keepdims=True) / D_MODEL
        loss_ref[...] += jnp.broadcast_to(part, (8, 128))
        dy = err * (1.0 / D_MODEL)
        dx3, dgf = _rms_bwd(dy, gf, r4, n4)
        dgf_ref[...] += jnp.broadcast_to(jnp.sum(dgf, axis=0, keepdims=True), (8, D_MODEL))
        dzp = _bf(dx3 * pp * pg * (1.0 - pg))
        dpp = _bf(dx3 * pg)
        dwpg_ref[...] += _tn(h3b, dzp)
        dwpp_ref[...] += _tn(pb, dpp)
        dh3 = _nt(dzp, wpg_ref[...])
        dx, dgp = _rms_bwd(dh3, gp, r3, n3)
        dgp_ref[...] += jnp.broadcast_to(jnp.sum(dgp, axis=0, keepdims=True), (8, D_MODEL))
        dx_ref[...] = dx3 + dx

    rowd = pl.BlockSpec((tm, D_MODEL), lambda i: (i, 0))
    fixed = lambda shp: pl.BlockSpec(shp, lambda i: (0,) * len(shp))
    return pl.pallas_call(
        body, name="tail_fwd_bwd", grid=(t // tm,),
        in_specs=[rowd, pl.BlockSpec((tm, PLE_DIM), lambda i: (i, 0)), rowd,
                  fixed((1, D_MODEL)), fixed((1, D_MODEL)), fixed((D_MODEL, D_MODEL)), fixed((PLE_DIM, D_MODEL))],
        out_specs=[rowd, fixed((D_MODEL, D_MODEL)), fixed((PLE_DIM, D_MODEL)),
                   fixed((8, D_MODEL)), fixed((8, D_MODEL)), fixed((8, 128))],
        out_shape=[jax.ShapeDtypeStruct((t, D_MODEL), F32), jax.ShapeDtypeStruct((D_MODEL, D_MODEL), F32),
                   jax.ShapeDtypeStruct((PLE_DIM, D_MODEL), F32), jax.ShapeDtypeStruct((8, D_MODEL), F32),
                   jax.ShapeDtypeStruct((8, D_MODEL), F32), jax.ShapeDtypeStruct((8, 128), F32)],
        compiler_params=_cp(("arbitrary",), VMEM_LIMIT),
    )(x2, p, target, g_ple, g_final, w_pg, w_pp)


def in_proj_bwd(pieces, weights, x, dx1, g, name="in_proj_bwd", tm=256):
    t = x.shape[0]
    tm = min(tm, t)
    k = len(pieces)

    def body(*refs):
        p_refs, w_refs = refs[:k], refs[k:2 * k]
        x_ref, dx1_ref, g_ref, dx_ref, dg_ref = refs[2 * k:]

        @pl.when(pl.program_id(0) == 0)
        def _():
            dg_ref[...] = jnp.zeros_like(dg_ref)

        dh = _nt(_bf(p_refs[0][...]), w_refs[0][...])
        for pr, wr in zip(p_refs[1:], w_refs[1:]):
            dh = dh + _nt(_bf(pr[...]), wr[...])
        r, n = _rms(x_ref[...])
        dx, dgc = _rms_bwd(dh, g_ref[...], r, n)
        dx_ref[...] = dx1_ref[...] + dx
        dg_ref[...] += jnp.broadcast_to(jnp.sum(dgc, axis=0, keepdims=True), (8, D_MODEL))

    rowd = pl.BlockSpec((tm, D_MODEL), lambda i: (i, 0))
    return pl.pallas_call(
        body, name=name, grid=(t // tm,),
        in_specs=[pl.BlockSpec((tm, a.shape[1]), lambda i: (i, 0)) for a in pieces]
        + [pl.BlockSpec(w.shape, lambda i: (0, 0)) for w in weights]
        + [rowd, rowd, pl.BlockSpec((1, D_MODEL), lambda i: (0, 0))],
        out_specs=[rowd, pl.BlockSpec((8, D_MODEL), lambda i: (0, 0))],
        out_shape=[jax.ShapeDtypeStruct((t, D_MODEL), F32), jax.ShapeDtypeStruct((8, D_MODEL), F32)],
        compiler_params=_cp(("arbitrary",), VMEM_LIMIT),
    )(*pieces, *weights, x, dx1, g)


def adamw(w, g, m, v, name, rows_cap=256):
    r, c = w.shape
    tr = r
    for cand in range(8, min(r, rows_cap) + 1, 8):
        if r % cand == 0:
            tr = cand

    def body(w_ref, g_ref, m_ref, v_ref, d_ref, mo_ref, vo_ref):
        gv = g_ref[...]
        mn = ADAM_B1 * m_ref[...] + (1.0 - ADAM_B1) * gv
        vn = ADAM_B2 * v_ref[...] + (1.0 - ADAM_B2) * (gv * gv)
        m_hat = mn / (1.0 - ADAM_B1 ** ADAM_STEP)
        v_hat = vn / (1.0 - ADAM_B2 ** ADAM_STEP)
        d_ref[...] = -ADAM_LR * (m_hat / (jnp.sqrt(v_hat) + ADAM_EPS) + ADAM_WD * w_ref[...])
        mo_ref[...] = mn
        vo_ref[...] = vn

    spec = pl.BlockSpec((tr, c), lambda i: (i, 0))
    return pl.pallas_call(
        body, name=name, grid=(r // tr,),
        in_specs=[spec] * 4, out_specs=[spec] * 3,
        out_shape=[jax.ShapeDtypeStruct((r, c), F32)] * 3,
        compiler_params=_cp(("parallel",), VMEM_LIMIT),
    )(w, g, m, v)


class Standalone:
    def __init__(self, later_weights):
        self.later_weights = later_weights

    def begin(self, *a):
        return 0.0

    forward = exchange = join = begin

    def finish(self, after):
        return self.later_weights


def local_step(x3d, p3d, target3d, g4, small, later, early):
    b, s, _ = x3d.shape
    t = b * s
    x = x3d.reshape(t, D_MODEL)
    p = p3d.reshape(t, PLE_DIM)
    target = target3d.reshape(t, D_MODEL)
    cut = SPLIT_Z - 2 * (D_IN // N_CHIPS)
    w_inp = jnp.concatenate([g4[2][:, cut + 8:], g4[3], g4[0], g4[1], g4[2][:, :cut], g4[2][:, cut:cut + 8],
                             jnp.zeros((D_MODEL, 120), BF16)], axis=1)
    al_row = jnp.pad(small["a_log"].reshape(1, B_HEADS), ((0, 0), (0, 128 - B_HEADS)))
    dtb_row = jnp.pad(small["dt_bias"].reshape(1, B_HEADS), ((0, 0), (0, 128 - B_HEADS)))
    conv_w8 = jnp.pad(small["conv_w"].reshape(CONV_K, CONV_CH), ((0, 8 - CONV_K), (0, 0)))
    w_on = small["w_onorm"].reshape(1, B_DIM)
    g_mix, g_ffn = small["g_mix"].reshape(1, D_MODEL), small["g_ffn"].reshape(1, D_MODEL)
    g_ple, g_final = small["g_ple"].reshape(1, D_MODEL), small["g_final"].reshape(1, D_MODEL)
    bias_band = make_bias_band(small["rel_bias"].reshape(A_HEADS, N_REL))

    tok = later.begin()
    proj, h1 = rms_matmul(x, g_mix + tok, w_inp, "in_proj", tm=1024)
    y_a, lse = attn_fwd(proj, bias_band, b, s)
    tok = later.forward(lse)
    c = conv_fwd(proj, conv_w8 + tok, b, s)
    u, wk, qg, kdec, pm, egl, tmat = dn_prep(c, proj, al_row, dtb_row, b, s)
    o_b, states = dn_scan_fwd(u, wk, qg, kdec, pm, egl, b, s)
    wts = later.finish(o_b)
    x1, merged = mid_fwd(x, y_a, o_b, proj, w_on, wts["w_branch_a"], wts["w_branch_b"], wts["w_out"])
    gate, up, act, h2 = ffn_up(x1, g_ffn, wts["w_gate_up"])
    x2 = matmul_residual(act, wts["w_down"], x1, "ffn_down")

    dx2, dw_pg, dw_pp, dg_ple, dg_final, loss = tail_fwd_bwd(
        x2, p, target, g_ple, g_final, wts["w_ple_gate"], wts["w_ple_proj"])
    dgate, dup = ffn_act_bwd(dx2, gate, up, wts["w_down"])
    w_gu = wts["w_gate_up"]
    dx1, dg_ffn = in_proj_bwd([dgate, dup], [w_gu[:, :D_FF], w_gu[:, D_FF:]], x1, dx2, g_ffn, name="ffn_in_bwd")
    dw_down = matmul_tn(act, dx2, "dw_down")
    dw_gu = matmul_tn(h2, dgate, "dw_gate", width=2 * D_FF)
    dw_gu = matmul_tn(h2, dup, "dw_up", into=dw_gu, col0=D_FF)
    dy_a, do_b, dz, dgates, dw_out, dwa, dwb, dw_on = mid_bwd(
        dx1, merged, y_a, o_b, proj, w_on, wts["w_branch_a"], wts["w_branch_b"], wts["w_out"])
    tok = early.begin(dict(w_branch_a=dwa, w_branch_b=dwb, w_out=dw_out, w_gate_up=dw_gu, w_down=dw_down,
                           w_ple_gate=dw_pg, w_ple_proj=dw_pp))
    ddw, ddwk, ddqg, ddkdec, ddp, ddegl = dn_scan_bwd(u, wk, qg, kdec, pm, egl + tok, states, do_b, b, s)
    tok = early.exchange(ddegl)
    dc3, dbd, dal, ddtb = dn_post_bwd(c, proj, al_row + tok, dtb_row, tmat, ddw, ddwk, ddqg, ddkdec, ddp, ddegl, b, s)
    dconv, dconv_w = conv_bwd(proj, conv_w8, dc3, b, s)
    dqa, dka, dva, dbt, dbf = attn_bwd(proj, bias_band, y_a, lse, dy_a, b, s)
    tok = early.join(dqa)
    d_rel = bias_band_grad(dbt, dbf)

    pieces = [dgates, dqa, dka, dva, dconv, dz, dbd]
    bounds = [0, 2048, 2560, 3072, 3584, 5120, 5632, 5760]
    w_pieces = [w_inp[:, lo:hi] for lo, hi in zip(bounds[:-1], bounds[1:])]
    dx, dg_mix = in_proj_bwd(pieces, w_pieces, x, dx1, g_mix + tok)
    dwp = None
    for k, pc in enumerate(pieces):
        dwp = matmul_tn(h1, pc, "dw_in_%d" % k, into=dwp, col0=bounds[k], width=P_WIDTH)
    reduced_early = early.finish(dwp)
    dw_in = jnp.concatenate([dwp[:, P_QA:P_BD + 8], dwp[:, :P_QA]], axis=1)

    grads = dict(w_in=dw_in, w_branch_a=dwa, w_branch_b=dwb, w_out=dw_out, w_gate_up=dw_gu, w_down=dw_down,
                 w_ple_gate=dw_pg, w_ple_proj=dw_pp)
    small_grads = dict(g_mix=dg_mix[0], g_ffn=dg_ffn[0], g_ple=dg_ple[0], g_final=dg_final[0],
                       conv_w=dconv_w[:CONV_K].reshape(-1), rel_bias=d_rel.reshape(-1), w_onorm=dw_on[0],
                       a_log=dal[0, :B_HEADS], dt_bias=ddtb[0, :B_HEADS], loss=loss[0, :1])
    return dx.reshape(b, s, D_MODEL), grads, small_grads, reduced_early


BIG = (("w_in", (D_MODEL, D_IN), 1), ("w_branch_a", (A_WIDTH, D_MODEL), 1), ("w_branch_b", (B_WIDTH, D_MODEL), 1),
       ("w_out", (D_MODEL, D_MODEL), 0), ("w_gate_up", (D_MODEL, 2 * D_FF), 1), ("w_down", (D_FF, D_MODEL), 0),
       ("w_ple_gate", (D_MODEL, D_MODEL), 0), ("w_ple_proj", (PLE_DIM, D_MODEL), 1))
N_CHIPS = 4
FIRST_WEIGHTS = ("w_in",)
LATER_WEIGHTS = ("w_branch_a", "w_branch_b", "w_out", "w_gate_up", "w_down", "w_ple_gate", "w_ple_proj")
LATE_GRADS = ("w_in",)
EARLY_GRADS = ("w_branch_a", "w_branch_b", "w_out", "w_gate_up", "w_down", "w_ple_gate", "w_ple_proj")


def _items(names):
    return [it for it in BIG if it[0] in names]


def _shard_shape(shape, axis):
    return (shape[0] // N_CHIPS, shape[1]) if axis == 0 else (shape[0], shape[1] // N_CHIPS)


def _pack_rows_of(names):
    return -(-sum(sh[0] * sh[1] for _, sh, _ in _items(names)) // (N_CHIPS * 128 * 512)) * 512


def _pack_rows(parts, total):
    used = sum(a.shape[-2] for a in parts)
    pad = jnp.zeros(parts[0].shape[:-2] + (total - used, 128), parts[0].dtype)
    return jnp.concatenate(parts + [pad], axis=-2)


def pack_grads(grads, names):
    parts = []
    for n, shape, axis in _items(names):
        rs, cs = _shard_shape(shape, axis)
        g = grads[n].astype(BF16)
        seg = g.reshape(N_CHIPS, rs, cs) if axis == 0 else jnp.transpose(g.reshape(rs, N_CHIPS, cs), (1, 0, 2))
        parts.append(seg.reshape(N_CHIPS, -1, 128))
    return _pack_rows(parts, _pack_rows_of(names))


def unpack_shard(flat, names):
    out, r0 = {}, 0
    for n, shape, axis in _items(names):
        rs, cs = _shard_shape(shape, axis)
        nr = rs * cs // 128
        out[n] = flat[r0:r0 + nr].reshape(rs, cs)
        r0 += nr
    return out


def _place():
    return lax.axis_index("x"), lax.axis_index("y"), lax.axis_index("c")


ANY = pl.BlockSpec(memory_space=pl.ANY)


def _gathered_shape(item):
    n, shape, _ = item
    return (N_CHIPS,) + _shard_shape(shape, 1) if n == "w_in" else shape


def _gather_block(o_ref, item, cx, cy, hf):
    n, shape, axis = item
    rs, cs = _shard_shape(shape, axis)
    hr = rs // 2
    ci = 2 * cx + cy
    if n == "w_in":
        return o_ref.at[ci, pl.ds(pl.multiple_of(hf * hr, 16), hr), :]
    if axis == 0:
        return o_ref.at[pl.ds(pl.multiple_of(ci * rs + hf * hr, 16), hr), :]
    return o_ref.at[pl.ds(pl.multiple_of(hf * hr, 16), hr), pl.ds(pl.multiple_of(ci * cs, 128), cs)]


def _own_half(w_ref, item, c):
    hr = _shard_shape(item[1], item[2])[0] // 2
    return w_ref.at[pl.ds(pl.multiple_of(c * hr, 16), hr), :]


def _gather_slot(o_ref, item, cx, cy):
    n, shape, axis = item
    rs, cs = _shard_shape(shape, axis)
    ci = 2 * cx + cy
    if n == "w_in":
        return o_ref.at[ci]
    if axis == 0:
        return o_ref.at[pl.ds(pl.multiple_of(ci * rs, 16), rs), :]
    return o_ref.at[:, pl.ds(pl.multiple_of(ci * cs, 128), cs)]


def _other_chips(x, y):
    return [(1 - x, y), (x, 1 - y), (1 - x, 1 - y)]


def allgather_weights(shards, names, chip):
    items = _items(names)
    nw = len(items)

    def body(*refs):
        w_refs, o_refs = refs[:nw], refs[nw:2 * nw]
        send_sems, recv_sems = refs[2 * nw:]
        x, y, c = _place()
        sibling = (x, y, 1 - c)
        chips = _other_chips(x, y)

        def copy(k, src, dst, to):
            return pltpu.make_async_remote_copy(src_ref=src, dst_ref=dst, send_sem=send_sems.at[k],
                                                recv_sem=recv_sems.at[k], device_id=to, device_id_type=MESH)

        def blk(i, cx, cy, hf):
            return _gather_block(o_refs[i], items[i], cx, cy, hf)

        def my_half(i):
            return _own_half(w_refs[i], items[i], c)

        def own(i):
            return _gather_slot(o_refs[i], items[i], x, y)

        first = [copy(7 * i + j, my_half(i), blk(i, x, y, c), (*chip_, c))
                 for i in range(nw) for j, chip_ in enumerate(chips)]
        first += [copy(7 * i + 6, w_refs[i], own(i), sibling) for i in range(nw)]
        for cp in first:
            cp.start()
        passed = []
        for i in range(nw):
            for j, chip_ in enumerate(chips):
                copy(7 * i + j, my_half(i), blk(i, *chip_, c), (*chip_, c)).wait_recv()
                fwd = copy(7 * i + 3 + j, blk(i, *chip_, c), blk(i, *chip_, c), sibling)
                fwd.start()
                passed.append(fwd)
        for i in range(nw):
            for j, chip_ in enumerate(chips):
                copy(7 * i + 3 + j, my_half(i), blk(i, *chip_, 1 - c), sibling).wait_recv()
            copy(7 * i + 6, w_refs[i], own(i), sibling).wait_recv()
        for cp in first + passed:
            cp.wait_send()

    outs = pl.pallas_call(
        body, name="allgather_weights",
        in_specs=[ANY] * nw, out_specs=[ANY] * nw,
        out_shape=[jax.ShapeDtypeStruct(_gathered_shape(it), BF16) for it in items],
        scratch_shapes=[pltpu.SemaphoreType.DMA((7 * nw,)), pltpu.SemaphoreType.DMA((7 * nw,))],
    )(*[shards[it[0]] for it in items])
    return {it[0]: o for it, o in zip(items, outs)}


HBM_SPEC = pl.BlockSpec(memory_space=pltpu.HBM)
SEM_SPEC = pl.BlockSpec(memory_space=pltpu.SEMAPHORE)
EFFECT = pltpu.SideEffectType.DATAFLOW_SIDE_EFFECTING


def _in_hbm(a):
    return pltpu.with_memory_space_constraint(a, pltpu.HBM)


def copies_start(name, bufs, ncopies, plan):
    nb = len(bufs)

    def body(*refs):
        in_refs, send_sems, recv_sems, token = refs[:nb], refs[nb], refs[nb + 1], refs[-1]
        for k, (src, dst, to) in enumerate(plan(in_refs)):
            pltpu.make_async_remote_copy(src_ref=src, dst_ref=dst, send_sem=send_sems.at[k],
                                         recv_sem=recv_sems.at[k], device_id=to, device_id_type=MESH).start()
        token[...] = jnp.zeros_like(token)

    outs = pl.pallas_call(
        body, name=name,
        in_specs=[HBM_SPEC] * nb,
        out_specs=(SEM_SPEC, SEM_SPEC, *[HBM_SPEC] * nb, pl.BlockSpec(memory_space=pltpu.VMEM)),
        out_shape=(pltpu.SemaphoreType.DMA((ncopies,)), pltpu.SemaphoreType.DMA((ncopies,)),
                   *[pltpu.HBM(b.shape, b.dtype) for b in bufs], jax.ShapeDtypeStruct((8, 128), F32)),
        input_output_aliases={i: 2 + i for i in range(nb)},
        compiler_params=pltpu.CompilerParams(has_side_effects=EFFECT),
    )(*[_in_hbm(b) for b in bufs])
    return outs[0], outs[1], list(outs[2:2 + nb]), outs[-1][0, 0]


def copies_wait(name, send_sems, recv_sems, bufs, after, plan):
    nb = len(bufs)

    def body(*refs):
        in_refs, s_sems, r_sems = refs[:nb], refs[nb], refs[nb + 1]
        for k, (src, dst, to) in enumerate(plan(in_refs)):
            cp = pltpu.make_async_remote_copy(src_ref=src, dst_ref=dst, send_sem=s_sems.at[k],
                                              recv_sem=r_sems.at[k], device_id=to, device_id_type=MESH)
            cp.wait_send()
            cp.wait_recv()

    return list(pl.pallas_call(
        body, name=name,
        in_specs=[HBM_SPEC] * nb + [SEM_SPEC, SEM_SPEC, ANY],
        out_specs=tuple([HBM_SPEC] * nb),
        out_shape=tuple(pltpu.HBM(b.shape, b.dtype) for b in bufs),
        input_output_aliases={i: i for i in range(nb)},
        compiler_params=pltpu.CompilerParams(has_side_effects=EFFECT),
    )(*bufs, send_sems, recv_sems, after))


def _landing(shape, dtype):
    return _in_hbm(lax.empty(shape, dtype))


class LaterWeights:
    def __init__(self, shards, chip):
        self.items = _items(LATER_WEIGHTS)
        self.shards, self.chip = shards, chip
        self.nw = len(self.items)

    def _ici_plan(self, refs):
        x, y, c = _place()
        w_refs, o_refs = refs[:self.nw], refs[self.nw:]
        plan = [(_own_half(w_refs[i], it, c), _gather_block(o_refs[i], it, x, y, c), (*chip_, c))
                for i, it in enumerate(self.items) for chip_ in _other_chips(x, y)]
        return plan + [(w_refs[i], _gather_slot(o_refs[i], it, x, y), (x, y, 1 - c))
                       for i, it in enumerate(self.items)]

    def _d2d_plan(self, refs):
        x, y, c = _place()
        return [(_gather_block(refs[i], it, *chip_, c), _gather_block(refs[i], it, *chip_, c), (x, y, 1 - c))
                for i, it in enumerate(self.items) for chip_ in _other_chips(x, y)]

    def _d2d_wait_plan(self, refs):
        x, y, c = _place()
        return [(_gather_block(refs[i], it, *chip_, c), _gather_block(refs[i], it, *chip_, 1 - c), (x, y, 1 - c))
                for i, it in enumerate(self.items) for chip_ in _other_chips(x, y)]

    def _ici_wait_plan(self, refs):
        x, y, c = _place()
        w_refs, o_refs = refs[:self.nw], refs[self.nw:]
        plan = [(_own_half(w_refs[i], it, c), _gather_block(o_refs[i], it, *chip_, c), (*chip_, c))
                for i, it in enumerate(self.items) for chip_ in _other_chips(x, y)]
        return plan + [(w_refs[i], _gather_slot(o_refs[i], it, x, y), (x, y, 1 - c))
                       for i, it in enumerate(self.items)]

    def begin(self):
        srcs = [self.shards[it[0]] for it in self.items]
        lands = [_landing(_gathered_shape(it), BF16) for it in self.items]
        self.s1, self.r1, self.b1, tok = copies_start("gather_ici_start", srcs + lands, 4 * self.nw, self._ici_plan)
        return tok

    def forward(self, after):
        b1 = copies_wait("gather_ici_wait", self.s1, self.r1, self.b1, after, self._ici_wait_plan)
        self.s2, self.r2, self.b2, tok = copies_start("gather_d2d_start", b1[self.nw:], 3 * self.nw, self._d2d_plan)
        return tok

    def finish(self, after):
        outs = copies_wait("gather_d2d_wait", self.s2, self.r2, self.b2, after, self._d2d_wait_plan)
        return {it[0]: o for it, o in zip(self.items, outs)}


def small_allreduce(v, name):
    r = v.shape[0]

    def body(v_ref, o_ref, buf, send_sems, recv_sems):
        x, y, c = _place()
        me = 4 * x + 2 * y + c
        buf[me] = v_ref[...]
        flips = [(fx, fy, fc) for fx in (0, 1) for fy in (0, 1) for fc in (0, 1)][1:]
        peers = [((1 - x) if fx else x, (1 - y) if fy else y, (1 - c) if fc else c) for fx, fy, fc in flips]

        def copy(k, slot, to):
            return pltpu.make_async_remote_copy(src_ref=v_ref, dst_ref=buf.at[slot], send_sem=send_sems.at[k],
                                                recv_sem=recv_sems.at[k], device_id=to, device_id_type=MESH)

        sends = [copy(k, me, peer) for k, peer in enumerate(peers)]
        for cp in sends:
            cp.start()
        for k, (px, py, pc) in enumerate(peers):
            copy(k, 4 * px + 2 * py + pc, (px, py, pc)).wait_recv()
        for cp in sends:
            cp.wait_send()
        acc = buf[0]
        for d in range(1, 8):
            acc = acc + buf[d]
        o_ref[...] = acc

    return pl.pallas_call(
        body, name=name,
        in_specs=[pl.BlockSpec(memory_space=pltpu.VMEM)], out_specs=pl.BlockSpec(memory_space=pltpu.VMEM),
        out_shape=jax.ShapeDtypeStruct((r, 128), F32),
        scratch_shapes=[pltpu.VMEM((8, r, 128), F32), pltpu.SemaphoreType.DMA((7,)), pltpu.SemaphoreType.DMA((7,))],
    )(v)


def swap_halves(g):
    half = g.shape[1] // 2

    def body(g_ref, o_ref, send_sem, recv_sem):
        x, y, c = _place()
        cp = pltpu.make_async_remote_copy(
            src_ref=g_ref.at[:, pl.ds((1 - c) * half, half), :], dst_ref=o_ref, send_sem=send_sem,
            recv_sem=recv_sem, device_id=(x, y, 1 - c), device_id_type=MESH)
        cp.start()
        cp.wait()

    return pl.pallas_call(
        body, name="swap_halves", in_specs=[ANY], out_specs=ANY,
        out_shape=jax.ShapeDtypeStruct((N_CHIPS, half, 128), g.dtype),
        scratch_shapes=[pltpu.SemaphoreType.DMA, pltpu.SemaphoreType.DMA],
    )(g)


def add_halves(g, other, place):
    half = other.shape[1]
    tr = _tile_rows(half)
    nblk = half // tr

    def body(pref, g0, g1, g2, g3, o0, o1, o2, o3, pf_ref, pb_ref):
        f = lambda r: r[...].astype(F32)
        pf_ref[...] = f(g0) + f(o0)
        pb_ref[0] = _bf(f(g1) + f(o1))
        pb_ref[1] = _bf(f(g2) + f(o2))
        pb_ref[2] = _bf(f(g3) + f(o3))

    gspec = lambda k: pl.BlockSpec((None, tr, 128), lambda i, pr: ((pr[0] + k) % N_CHIPS, pr[1] * nblk + i, 0))
    ospec = lambda k: pl.BlockSpec((None, tr, 128), lambda i, pr: ((pr[0] + k) % N_CHIPS, i, 0))
    return pl.pallas_call(
        body, name="add_halves",
        grid_spec=pltpu.PrefetchScalarGridSpec(
            num_scalar_prefetch=1, grid=(nblk,),
            in_specs=[gspec(0), gspec(1), gspec(2), gspec(3), ospec(0), ospec(1), ospec(2), ospec(3)],
            out_specs=[pl.BlockSpec((tr, 128), lambda i, pr: (i, 0)),
                       pl.BlockSpec((3, tr, 128), lambda i, pr: (0, i, 0))]),
        out_shape=[jax.ShapeDtypeStruct((half, 128), F32), jax.ShapeDtypeStruct((3, half, 128), BF16)],
        compiler_params=_cp(("parallel",), VMEM_LIMIT),
    )(place, g, g, g, g, other, other, other, other)


def _tile_rows(n, cap=2048):
    best = 16
    for t in range(16, cap + 1, 16):
        if n % t == 0:
            best = t
    assert n % best == 0
    return best


def exchange_partials(pb):
    def body(p_ref, o_ref, send_sems, recv_sems):
        x, y, c = _place()
        me = 2 * x + y
        cps = []
        for k in range(1, N_CHIPS):
            to = (me + k) % N_CHIPS
            cps.append(pltpu.make_async_remote_copy(
                src_ref=p_ref.at[k - 1], dst_ref=o_ref.at[k - 1], send_sem=send_sems.at[k - 1],
                recv_sem=recv_sems.at[k - 1], device_id=(to // 2, to % 2, c), device_id_type=MESH))
        for cp in cps:
            cp.start()
        for cp in cps:
            cp.wait()

    return pl.pallas_call(
        body, name="exchange_partials", in_specs=[ANY], out_specs=ANY,
        out_shape=jax.ShapeDtypeStruct(pb.shape, pb.dtype),
        scratch_shapes=[pltpu.SemaphoreType.DMA((3,)), pltpu.SemaphoreType.DMA((3,))],
    )(pb)


def add_partials(pf, got, place):
    half = pf.shape[0]
    tr = _tile_rows(half)

    def body(pref, pf_ref, got_ref, o_ref):
        o_ref[...] = ((pf_ref[...] + got_ref[0].astype(F32)) + got_ref[1].astype(F32)) + got_ref[2].astype(F32)

    return pl.pallas_call(
        body, name="add_partials",
        grid_spec=pltpu.PrefetchScalarGridSpec(
            num_scalar_prefetch=1, grid=(half // tr,),
            in_specs=[pl.BlockSpec((tr, 128), lambda i, pr: (i, 0)),
                      pl.BlockSpec((3, tr, 128), lambda i, pr: (0, i, 0))],
            out_specs=pl.BlockSpec((None, tr, 128), lambda i, pr: (pr[1], i, 0))),
        out_shape=jax.ShapeDtypeStruct((2, half, 128), F32),
        compiler_params=_cp(("parallel",), VMEM_LIMIT),
    )(place, pf, got)


def join_halves(both):
    def body(r_ref, o_ref, send_sem, recv_sem):
        x, y, c = _place()
        cp = pltpu.make_async_remote_copy(src_ref=r_ref.at[c], dst_ref=o_ref.at[c], send_sem=send_sem,
                                          recv_sem=recv_sem, device_id=(x, y, 1 - c), device_id_type=MESH)
        cp.start()
        pltpu.make_async_remote_copy(src_ref=r_ref.at[c], dst_ref=o_ref.at[1 - c], send_sem=send_sem,
                                     recv_sem=recv_sem, device_id=(x, y, 1 - c), device_id_type=MESH).wait_recv()
        cp.wait_send()

    return pl.pallas_call(
        body, name="join_halves", in_specs=[ANY], out_specs=ANY,
        out_shape=jax.ShapeDtypeStruct(both.shape, F32),
        scratch_shapes=[pltpu.SemaphoreType.DMA, pltpu.SemaphoreType.DMA],
        input_output_aliases={0: 0},
    )(both)


def reduce_scatter_grads(gpack, place):
    other = swap_halves(gpack)
    pf, pb = add_halves(gpack, other, place)
    got = exchange_partials(pb)
    return join_halves(add_partials(pf, got, place)).reshape(gpack.shape[1], 128)


class EarlyGrads:
    def __init__(self, place):
        self.place = place

    @staticmethod
    def _swap_plan(refs):
        x, y, c = _place()
        g_ref, o_ref = refs
        half = o_ref.shape[1]
        return [(g_ref.at[:, pl.ds(pl.multiple_of((1 - c) * half, 16), half), :], o_ref, (x, y, 1 - c))]

    @staticmethod
    def _exchange_plan(refs):
        x, y, c = _place()
        p_ref, o_ref = refs
        me = 2 * x + y
        return [(p_ref.at[k - 1], o_ref.at[k - 1], (((me + k) % N_CHIPS) // 2, ((me + k) % N_CHIPS) % 2, c))
                for k in range(1, N_CHIPS)]

    @staticmethod
    def _join_plan(refs):
        x, y, c = _place()
        return [(refs[0].at[c], refs[0].at[c], (x, y, 1 - c))]

    @staticmethod
    def _join_wait_plan(refs):
        x, y, c = _place()
        return [(refs[0].at[c], refs[0].at[1 - c], (x, y, 1 - c))]

    def begin(self, grads):
        g = pack_grads(grads, EARLY_GRADS)
        land = _landing((N_CHIPS, g.shape[1] // 2, 128), BF16)
        self.s1, self.r1, self.b1, tok = copies_start("grads_swap_start", [g, land], 1, self._swap_plan)
        return tok

    def exchange(self, after):
        g, other = copies_wait("grads_swap_wait", self.s1, self.r1, self.b1, after, self._swap_plan)
        self.pf, pb = add_halves(g, other, self.place)
        land = _landing(pb.shape, BF16)
        self.s2, self.r2, self.b2, tok = copies_start("grads_exchange_start", [pb, land], 3, self._exchange_plan)
        return tok

    def join(self, after):
        _, got = copies_wait("grads_exchange_wait", self.s2, self.r2, self.b2, after, self._exchange_plan)
        both = add_partials(self.pf, got, self.place)
        self.s3, self.r3, self.b3, tok = copies_start("grads_join_start", [both], 1, self._join_plan)
        return tok

    def finish(self, after):
        (both,) = copies_wait("grads_join_wait", self.s3, self.r3, self.b3, after, self._join_wait_plan)
        return both.reshape(-1, 128)


SMALL = (("g_mix", D_MODEL), ("g_ffn", D_MODEL), ("g_ple", D_MODEL), ("g_final", D_MODEL),
         ("conv_w", CONV_K * CONV_CH), ("rel_bias", A_HEADS * N_REL), ("w_onorm", B_DIM),
         ("a_log", B_HEADS), ("dt_bias", B_HEADS), ("loss", 1))


def _pad128(v):
    v = v.reshape(-1)
    return jnp.pad(v, (0, -v.shape[0] % 128))


def pack_small(d, names, rows):
    flat = jnp.concatenate([_pad128(d[n]) for n in names]).reshape(-1, 128)
    return jnp.pad(flat, ((0, rows - flat.shape[0]), (0, 0)))


def unpack_small(flat, names_sizes):
    out, r0 = {}, 0
    v = flat.reshape(-1)
    for n, size in names_sizes:
        out[n] = v[r0:r0 + size]
        r0 += -(-size // 128) * 128
    return out


def kernel(x, p, g_mix, w_in, conv_w, a_log, dt_bias, rel_bias, w_onorm, w_branch_a, w_branch_b, w_out, g_ffn, w_gate_up, w_down, g_ple, w_ple_gate, w_ple_proj, g_final, loss_target, m_g_mix, m_w_in, m_conv_w, m_a_log, m_dt_bias, m_rel_bias, m_w_onorm, m_w_branch_a, m_w_branch_b, m_w_out, m_g_ffn, m_w_gate_up, m_w_down, m_g_ple, m_w_ple_gate, m_w_ple_proj, m_g_final, v_g_mix, v_w_in, v_conv_w, v_a_log, v_dt_bias, v_rel_bias, v_w_onorm, v_w_branch_a, v_w_branch_b, v_w_out, v_g_ffn, v_w_gate_up, v_w_down, v_g_ple, v_w_ple_gate, v_w_ple_proj, v_g_final):
    names = ["g_mix", "w_in", "conv_w", "a_log", "dt_bias", "rel_bias", "w_onorm", "w_branch_a", "w_branch_b",
             "w_out", "g_ffn", "w_gate_up", "w_down", "g_ple", "w_ple_gate", "w_ple_proj", "g_final"]
    w = dict(zip(names, [g_mix, w_in, conv_w, a_log, dt_bias, rel_bias, w_onorm, w_branch_a, w_branch_b, w_out,
                         g_ffn, w_gate_up, w_down, g_ple, w_ple_gate, w_ple_proj, g_final]))
    m = dict(zip(names, [m_g_mix, m_w_in, m_conv_w, m_a_log, m_dt_bias, m_rel_bias, m_w_onorm, m_w_branch_a,
                         m_w_branch_b, m_w_out, m_g_ffn, m_w_gate_up, m_w_down, m_g_ple, m_w_ple_gate,
                         m_w_ple_proj, m_g_final]))
    v = dict(zip(names, [v_g_mix, v_w_in, v_conv_w, v_a_log, v_dt_bias, v_rel_bias, v_w_onorm, v_w_branch_a,
                         v_w_branch_b, v_w_out, v_g_ffn, v_w_gate_up, v_w_down, v_g_ple, v_w_ple_gate,
                         v_w_ple_proj, v_g_final]))
    xi, yi, ci = _place()
    chip = 2 * xi + yi
    big_names = [n for n, _, _ in BIG]

    shards2d = {n: w[n].reshape(w[n].shape[-2:]) for n in big_names}
    shards_bf = {n: a.astype(BF16) for n, a in shards2d.items()}
    g4 = allgather_weights(shards_bf, FIRST_WEIGHTS, chip)["w_in"]
    place = jnp.stack([chip, ci]).astype(jnp.int32)
    conv_sh = jnp.where(ci == 0, w["conv_w"].reshape(CONV_K, CONV_CH // N_CHIPS), 0.0)
    conv_slots = lax.dynamic_update_slice(jnp.zeros((N_CHIPS, CONV_K, CONV_CH // N_CHIPS), F32), conv_sh[None],
                                          (chip, 0, 0))
    conv_all = small_allreduce(conv_slots.reshape(-1, 128), "gather_conv_w")
    conv_full = jnp.transpose(conv_all.reshape(N_CHIPS, CONV_K, CONV_CH // N_CHIPS), (1, 0, 2)).reshape(CONV_K, CONV_CH)
    small = {n: w[n] for n in names if n not in big_names}
    small["conv_w"] = conv_full

    grad_x, grads, small_grads, reduced_early = local_step(
        x, p[0], loss_target, g4, small, LaterWeights(shards_bf, chip), EarlyGrads(place))

    reduced_late = reduce_scatter_grads(pack_grads(grads, LATE_GRADS), place)
    gshard = {**unpack_shard(reduced_early, EARLY_GRADS), **unpack_shard(reduced_late, LATE_GRADS)}
    small_names = [n for n, _ in SMALL]
    red = unpack_small(small_allreduce(pack_small(small_grads, small_names, 112), "allreduce_small"), SMALL)
    loss = red["loss"][0]
    conv_g = lax.dynamic_slice(red["conv_w"].reshape(CONV_K, N_CHIPS, CONV_CH // N_CHIPS), (0, chip, 0),
                               (CONV_K, 1, CONV_CH // N_CHIPS))
    gsmall = {n: red[n].reshape(w[n].shape) for n in small_names if n not in ("loss", "conv_w")}
    gsmall["conv_w"] = conv_g.reshape(w["conv_w"].shape)

    grad, delta, new_m, new_v = {}, {}, {}, {}
    for n in big_names:
        shp = w[n].shape
        d_, m_, v_ = adamw(shards2d[n], gshard[n], m[n].reshape(shp[-2:]), v[n].reshape(shp[-2:]), "adamw_" + n)
        grad[n], delta[n], new_m[n], new_v[n] = gshard[n].reshape(shp), d_.reshape(shp), m_.reshape(shp), v_.reshape(shp)
    snames = [n for n in small_names if n != "loss"]
    ssizes = [(n, w[n].size) for n in snames]
    pk = lambda d: pack_small(d, snames, 64)
    d_, m_, v_ = adamw(pk(w), pk(gsmall), pk(m), pk(v), "adamw_small")
    ds, ms, vs = unpack_small(d_, ssizes), unpack_small(m_, ssizes), unpack_small(v_, ssizes)
    for n in snames:
        shp = w[n].shape
        grad[n], delta[n], new_m[n], new_v[n] = gsmall[n], ds[n].reshape(shp), ms[n].reshape(shp), vs[n].reshape(shp)

    return (loss, grad_x, *[grad[n] for n in names], *[delta[n] for n in names],
            *[new_m[n] for n in names], *[new_v[n] for n in names])
```

```python
import functools

import jax
import jax.numpy as jnp
from jax import lax
from jax.experimental import pallas as pl
from jax.experimental.pallas import tpu as pltpu

F32 = jnp.float32
BF16 = jnp.bfloat16
HI = lax.Precision.HIGHEST
MESH = pl.DeviceIdType.MESH

D_MODEL = 1024
CHUNK = 64
PLE_DIM = 256
EPS = 1e-6
A_HEADS = 8
A_HEAD_DIM = 64
A_WIDTH = 512
A_LOOKBACK = 8
BAND = (A_LOOKBACK + 1) * CHUNK
TAIL = 3 * CHUNK
REL_CLIP = 128
N_REL = 2 * REL_CLIP + 1
B_HEADS = 4
B_DIM = 128
B_WIDTH = 512
CONV_K = 4
CONV_CH = 1536
D_FF = 2816
SPLIT_Z = 3584
D_IN = 5640
ADAM_LR, ADAM_B1, ADAM_B2, ADAM_EPS, ADAM_WD, ADAM_STEP = 0.001, 0.9, 0.999, 1e-08, 0.01, 10

P_GATES, P_QA, P_KA, P_VA, P_CONV, P_Z, P_BD, P_WIDTH = 0, 2048, 2560, 3072, 3584, 5120, 5632, 5760

VMEM_LIMIT = 56 * 1024 * 1024


def _cp(sem, vmem=None, **kw):
    return pltpu.CompilerParams(dimension_semantics=sem, vmem_limit_bytes=vmem, **kw)


def _tile(n, cap):
    best = None
    for t in range(128, cap + 1, 128):
        if n % t == 0:
            best = t
    assert best is not None, (n, cap)
    return best


def _nn(a, b, prec=None):
    return lax.dot_general(a, b, (((1,), (0,)), ((), ())), preferred_element_type=F32, precision=prec)


def _nt(a, b, prec=None):
    return lax.dot_general(a, b, (((1,), (1,)), ((), ())), preferred_element_type=F32, precision=prec)


def _tn(a, b, prec=None):
    return lax.dot_general(a, b, (((0,), (0,)), ((), ())), preferred_element_type=F32, precision=prec)


def _bnn(a, b, prec=None):
    return lax.dot_general(a, b, (((2,), (1,)), ((0,), (0,))), preferred_element_type=F32, precision=prec)


def _bnt(a, b, prec=None):
    return lax.dot_general(a, b, (((2,), (2,)), ((0,), (0,))), preferred_element_type=F32, precision=prec)


def _bf(a):
    return a.astype(BF16)


def _split(a):
    hi = a.astype(BF16)
    return hi, (a - hi.astype(F32)).astype(BF16)


def _bnn_exact(lhs_b, rhs):
    h1 = _bf(rhs)
    r1 = rhs - h1.astype(F32)
    h2 = _bf(r1)
    h3 = _bf(r1 - h2.astype(F32))
    return _bnn(lhs_b, h1) + (_bnn(lhs_b, h2) + _bnn(lhs_b, h3))


def _bnn3(a, b):
    ah, al = a if isinstance(a, tuple) else _split(a)
    bh, bl = b if isinstance(b, tuple) else _split(b)
    return _bnn(ah, bh) + (_bnn(ah, bl) + _bnn(al, bh))


def _sigmoid(x):
    return 0.5 * jnp.tanh(0.5 * x) + 0.5


def _softplus(x):
    return jnp.maximum(x, 0.0) + jnp.log(1.0 + jnp.exp(-jnp.abs(x)))


def rms_matmul(x, g, w, name, tm=512, tn_cap=1024):
    t, d = x.shape
    n = w.shape[1]
    tm = min(tm, t)
    tn = _tile(n, tn_cap)

    nj = n // tn

    def body(x_ref, g_ref, w_ref, o_ref, h_ref, tail_ref):
        @pl.when(pl.program_id(1) == 0)
        def _():
            xv = x_ref[...]
            r = lax.rsqrt(jnp.mean(xv * xv, axis=-1, keepdims=True) + EPS)
            h_ref[...] = _bf(xv * r * g_ref[...])

        res = _nn(h_ref[...], w_ref[...])
        o_ref[...] = _bf(res)

        @pl.when(pl.program_id(1) == nj - 1)
        def _():
            tail_ref[...] = res[:, tn - 128:]

    return pl.pallas_call(
        body, name=name, grid=(t // tm, nj),
        in_specs=[pl.BlockSpec((tm, d), lambda i, j: (i, 0)),
                  pl.BlockSpec((1, d), lambda i, j: (0, 0)),
                  pl.BlockSpec((d, tn), lambda i, j: (0, j))],
        out_specs=[pl.BlockSpec((tm, tn), lambda i, j: (i, j)),
                   pl.BlockSpec((tm, d), lambda i, j: (i, 0)),
                   pl.BlockSpec((tm, 128), lambda i, j: (i, 0))],
        out_shape=[jax.ShapeDtypeStruct((t, n), BF16), jax.ShapeDtypeStruct((t, d), BF16),
                   jax.ShapeDtypeStruct((t, 128), F32)],
        compiler_params=_cp(("parallel", "arbitrary"), VMEM_LIMIT),
    )(x, g, w)


def matmul_tn(a, b, name, into=None, col0=0, width=None, tm=1024, tk_cap=1408, tn_cap=1408):
    m, k1 = a.shape
    n = b.shape[1]
    tm = min(tm, m)
    tk = _tile(k1, tk_cap)
    tn = _tile(n, tn_cap)
    while col0 % tn:
        tn = _tile(n, tn - 128)
    nk = m // tm
    c0 = col0 // tn

    def body(*refs):
        a_ref, b_ref, o_ref, acc = refs[0], refs[1], refs[-2], refs[-1]

        @pl.when(pl.program_id(2) == 0)
        def _():
            acc[...] = jnp.zeros_like(acc)

        acc[...] += _tn(_bf(a_ref[...]), _bf(b_ref[...]))

        @pl.when(pl.program_id(2) == nk - 1)
        def _():
            o_ref[...] = _bf(acc[...])

    in_specs = [pl.BlockSpec((tm, tk), lambda i, j, k: (k, i)),
                pl.BlockSpec((tm, tn), lambda i, j, k: (k, j))]
    args = [a, b]
    total = n if width is None else width
    aliases = {}
    if into is not None:
        in_specs.append(ANY)
        args.append(into)
        total = into.shape[1]
        aliases = {2: 0}
    return pl.pallas_call(
        body, name=name, grid=(k1 // tk, n // tn, nk),
        in_specs=in_specs,
        out_specs=pl.BlockSpec((tk, tn), lambda i, j, k: (i, c0 + j)),
        out_shape=jax.ShapeDtypeStruct((k1, total), BF16),
        scratch_shapes=[pltpu.VMEM((tk, tn), F32)],
        input_output_aliases=aliases,
        compiler_params=_cp(("parallel", "parallel", "arbitrary"), VMEM_LIMIT),
    )(*args)


def _tail_onehot(qi):
    r = lax.broadcasted_iota(jnp.int32, (384, TAIL), 0)
    kj = lax.broadcasted_iota(jnp.int32, (384, TAIL), 1)
    return (r == jnp.minimum(REL_CLIP + qi - kj, REL_CLIP) + REL_CLIP).astype(F32)


def bias_tail(rel_pad):
    def body(rb_ref, o_ref):
        rb = rb_ref[...]
        for qi in range(CHUNK):
            o_ref[qi] = _nn(rb, _tail_onehot(qi), HI)

    return pl.pallas_call(
        body, name="bias_tail",
        out_shape=jax.ShapeDtypeStruct((CHUNK, A_HEADS, TAIL), F32),
    )(rel_pad)


def bias_grad(db_t, db_far):
    def body(t_ref, f_ref, o_ref):
        acc = jnp.zeros((A_HEADS, 384), F32)
        for qi in range(CHUNK):
            acc = acc + _nt(t_ref[qi], _tail_onehot(qi), HI)
        far = jnp.sum(jnp.sum(f_ref[...], axis=2), axis=1, keepdims=True)
        lane = lax.broadcasted_iota(jnp.int32, (A_HEADS, 384), 1)
        o_ref[...] = acc + jnp.where(lane == 2 * REL_CLIP, far, 0.0)

    return pl.pallas_call(
        body, name="bias_grad",
        out_shape=jax.ShapeDtypeStruct((A_HEADS, 384), F32),
    )(db_t, db_far)


ATT_CB = 8


WIN = BAND + CHUNK


def _stack_heads(a, lane):
    return jnp.concatenate([jnp.where(lane < 64, a, 0.0), jnp.where(lane >= 64, a, 0.0)], axis=0)


def _fill_band_pads(k_ref, v_ref, kp, vp, s):
    z = jnp.zeros((A_LOOKBACK * CHUNK, 128), BF16)
    kp[pl.ds(0, A_LOOKBACK * CHUNK), :] = z
    vp[pl.ds(0, A_LOOKBACK * CHUNK), :] = z
    kp[pl.ds(A_LOOKBACK * CHUNK, s), :] = _bf(k_ref[...])
    vp[pl.ds(A_LOOKBACK * CHUNK, s), :] = _bf(v_ref[...])


def attn_fwd(proj, bias_band, b, s):
    t = b * s
    nc = s // CHUNK
    qb, kb_, vb_ = P_QA // 128, P_KA // 128, P_VA // 128

    nstep = nc // ATT_CB
    rows = ATT_CB * CHUNK

    def body(q_ref, k_ref, v_ref, b_ref, o_ref, lse_ref, kp, vp):
        n0 = pl.program_id(2) * ATT_CB

        @pl.when(n0 == 0)
        def _():
            _fill_band_pads(k_ref, v_ref, kp, vp, s)

        lane = lax.broadcasted_iota(jnp.int32, (2 * CHUNK, 128), 1)
        col = lax.broadcasted_iota(jnp.int32, (4 * CHUNK, WIN), 1)
        bias4 = b_ref[...]

        def pair(pp, carry):
            n = n0 + 2 * pp
            r0 = pl.multiple_of(pp * 2 * CHUNK, 2 * CHUNK)
            start = pl.multiple_of(n * CHUNK, CHUNK)
            kb = kp[pl.ds(start, WIN), :]
            vb = vp[pl.ds(start, WIN), :]
            q4 = _stack_heads(q_ref[pl.ds(r0, 2 * CHUNK), :] * (A_HEAD_DIM ** -0.5), lane)
            sc = jnp.where(col >= (A_LOOKBACK - n) * CHUNK, _nt(_bf(q4), kb) + bias4, -1e30)
            mx = jnp.max(sc, axis=1, keepdims=True)
            p = jnp.exp(sc - mx)
            l = jnp.sum(p, axis=1, keepdims=True)
            o4 = _nn(_bf(p), vb) / l
            lse4 = mx + jnp.log(l)
            o_ref[pl.ds(r0, 2 * CHUNK), :] = jnp.where(lane < 64, o4[:2 * CHUNK], o4[2 * CHUNK:])
            lse_ref[pl.ds(r0, 2 * CHUNK), :] = jnp.where(lane < 64, lse4[:2 * CHUNK], lse4[2 * CHUNK:])
            return carry

        lax.fori_loop(0, ATT_CB // 2, pair, 0)

    return pl.pallas_call(
        body, name="attn_fwd", grid=(b, 4, nstep),
        in_specs=[pl.BlockSpec((rows, 128), lambda bb, m, n: (bb * nstep + n, qb + m)),
                  pl.BlockSpec((s, 128), lambda bb, m, n: (bb, kb_ + m)),
                  pl.BlockSpec((s, 128), lambda bb, m, n: (bb, vb_ + m)),
                  pl.BlockSpec((None, 4 * CHUNK, WIN), lambda bb, m, n: (m, 0, 0))],
        out_specs=[pl.BlockSpec((rows, 128), lambda bb, m, n: (bb * nstep + n, m)),
                   pl.BlockSpec((rows, 128), lambda bb, m, n: (bb * nstep + n, m))],
        out_shape=[jax.ShapeDtypeStruct((t, A_WIDTH), F32), jax.ShapeDtypeStruct((t, A_WIDTH), F32)],
        scratch_shapes=[pltpu.VMEM((s + A_LOOKBACK * CHUNK, 128), BF16),
                        pltpu.VMEM((s + A_LOOKBACK * CHUNK, 128), BF16)],
        compiler_params=_cp(("parallel", "parallel", "arbitrary"), VMEM_LIMIT),
    )(proj, proj, proj, bias_band)


def attn_bwd(proj, bias_band, y_a, lse, dy_a, b, s):
    t = b * s
    nc = s // CHUNK
    qb, kb_, vb_ = P_QA // 128, P_KA // 128, P_VA // 128
    pad = A_LOOKBACK * CHUNK
    nstep = nc // ATT_CB
    rows = ATT_CB * CHUNK

    def body(q_ref, k_ref, v_ref, b_ref, do_ref, o_ref, lse_ref,
             dq_ref, dk_ref, dv_ref, dbt_ref, dbf_ref, kp, vp, dkp, dvp):
        bb = pl.program_id(1)
        n0 = pl.program_id(2) * ATT_CB

        @pl.when(n0 == 0)
        def _():
            _fill_band_pads(k_ref, v_ref, kp, vp, s)
            dkp[...] = jnp.zeros_like(dkp)
            dvp[...] = jnp.zeros_like(dvp)

        @pl.when((n0 == 0) & (bb == 0))
        def _():
            dbt_ref[...] = jnp.zeros_like(dbt_ref)
            dbf_ref[...] = jnp.zeros_like(dbf_ref)

        lane = lax.broadcasted_iota(jnp.int32, (2 * CHUNK, 128), 1)
        col = lax.broadcasted_iota(jnp.int32, (4 * CHUNK, WIN), 1)
        bias4 = b_ref[...]

        def pair(pp, carry):
            n = n0 + 2 * pp
            r0 = pl.multiple_of(pp * 2 * CHUNK, 2 * CHUNK)
            start = pl.multiple_of(n * CHUNK, CHUNK)
            kb = kp[pl.ds(start, WIN), :]
            vb = vp[pl.ds(start, WIN), :]
            q4b = _bf(_stack_heads(q_ref[pl.ds(r0, 2 * CHUNK), :] * (A_HEAD_DIM ** -0.5), lane))
            do4 = _stack_heads(do_ref[pl.ds(r0, 2 * CHUNK), :], lane)
            do4b = _bf(do4)
            o = o_ref[pl.ds(r0, 2 * CHUNK), :]
            lsev = lse_ref[pl.ds(r0, 2 * CHUNK), :]
            lse4 = jnp.concatenate([lsev[:, 0:1], lsev[:, 64:65]], axis=0)
            sc = jnp.where(col >= (A_LOOKBACK - n) * CHUNK, _nt(q4b, kb) + bias4, -1e30)
            p = jnp.exp(sc - lse4)
            dp = _nt(do4b, vb)
            delta = jnp.sum(do4 * jnp.concatenate([o, o], axis=0), axis=1, keepdims=True)
            ds = p * (dp - delta)
            dsb = _bf(ds)
            dq4 = _nn(dsb, kb)
            dq_ref[pl.ds(r0, 2 * CHUNK), :] = _bf(
                jnp.where(lane < 64, dq4[:2 * CHUNK], dq4[2 * CHUNK:]) * (A_HEAD_DIM ** -0.5))
            dkp[pl.ds(start, WIN), :] += _tn(dsb, q4b)
            dvp[pl.ds(start, WIN), :] += _tn(_bf(p), do4b)
            dbt_ref[...] += ds[:, WIN - 256:]
            dbf_ref[...] += ds[:, 0:128] + ds[:, 128:256] + ds[:, 256:384]
            return carry

        lax.fori_loop(0, ATT_CB // 2, pair, 0)

        @pl.when(n0 == nc - ATT_CB)
        def _():
            dk_ref[...] = _bf(dkp[pl.ds(pad, s), :])
            dv_ref[...] = _bf(dvp[pl.ds(pad, s), :])

    return pl.pallas_call(
        body, name="attn_bwd", grid=(4, b, nstep),
        in_specs=[pl.BlockSpec((rows, 128), lambda m, bb, n: (bb * nstep + n, qb + m)),
                  pl.BlockSpec((s, 128), lambda m, bb, n: (bb, kb_ + m)),
                  pl.BlockSpec((s, 128), lambda m, bb, n: (bb, vb_ + m)),
                  pl.BlockSpec((None, 4 * CHUNK, WIN), lambda m, bb, n: (m, 0, 0)),
                  pl.BlockSpec((rows, 128), lambda m, bb, n: (bb * nstep + n, m)),
                  pl.BlockSpec((rows, 128), lambda m, bb, n: (bb * nstep + n, m)),
                  pl.BlockSpec((rows, 128), lambda m, bb, n: (bb * nstep + n, m))],
        out_specs=[pl.BlockSpec((rows, 128), lambda m, bb, n: (bb * nstep + n, m)),
                   pl.BlockSpec((s, 128), lambda m, bb, n: (bb, m)),
                   pl.BlockSpec((s, 128), lambda m, bb, n: (bb, m)),
                   pl.BlockSpec((None, 4 * CHUNK, 256), lambda m, bb, n: (m, 0, 0)),
                   pl.BlockSpec((None, 4 * CHUNK, 128), lambda m, bb, n: (m, 0, 0))],
        out_shape=[jax.ShapeDtypeStruct((t, A_WIDTH), BF16)] * 3
        + [jax.ShapeDtypeStruct((4, 4 * CHUNK, 256), F32),
           jax.ShapeDtypeStruct((4, 4 * CHUNK, 128), F32)],
        scratch_shapes=[pltpu.VMEM((s + pad, 128), BF16), pltpu.VMEM((s + pad, 128), BF16),
                        pltpu.VMEM((s + pad, 128), F32), pltpu.VMEM((s + pad, 128), F32)],
        compiler_params=_cp(("parallel", "arbitrary", "arbitrary"), VMEM_LIMIT),
    )(proj, proj, proj, bias_band, dy_a, y_a, lse)


def _conv_taps(x, w, s):
    row = lax.broadcasted_iota(jnp.int32, x.shape, 0)
    shifted = [x] + [jnp.where(row >= i, pltpu.roll(x, i, 0), 0.0) for i in range(1, CONV_K)]
    acc = shifted[0] * w[CONV_K - 1:CONV_K, :]
    for i in range(1, CONV_K):
        acc = acc + shifted[i] * w[CONV_K - 1 - i:CONV_K - i, :]
    return acc, shifted


def conv_fwd(proj, conv_w8, b, s):
    cb = 512
    c0 = P_CONV // cb

    def body(x_ref, w_ref, o_ref):
        a, _ = _conv_taps(x_ref[...].astype(F32), w_ref[...], s)
        o_ref[...] = a * _sigmoid(a)

    return pl.pallas_call(
        body, name="conv_fwd", grid=(b, CONV_CH // cb),
        in_specs=[pl.BlockSpec((s, cb), lambda bb, j: (bb, c0 + j)),
                  pl.BlockSpec((8, cb), lambda bb, j: (0, j))],
        out_specs=pl.BlockSpec((s, cb), lambda bb, j: (bb, j)),
        out_shape=jax.ShapeDtypeStruct((b * s, CONV_CH), F32),
        compiler_params=_cp(("parallel", "parallel"), VMEM_LIMIT),
    )(proj, conv_w8)


def conv_bwd(proj, conv_w8, dc3, b, s):
    cb = 512
    c0 = P_CONV // cb

    def body(x_ref, w_ref, dc_ref, dx_ref, dw_ref):
        @pl.when(pl.program_id(1) == 0)
        def _():
            dw_ref[...] = jnp.zeros_like(dw_ref)

        w = w_ref[...]
        a, shifted = _conv_taps(x_ref[...].astype(F32), w, s)
        sg = _sigmoid(a)
        da = dc_ref[...] * (sg * (1.0 + a * (1.0 - sg)))
        row = lax.broadcasted_iota(jnp.int32, da.shape, 0)
        dx = da * w[CONV_K - 1:CONV_K, :]
        for i in range(1, CONV_K):
            dx = dx + jnp.where(row < s - i, pltpu.roll(da, s - i, 0), 0.0) * w[CONV_K - 1 - i:CONV_K - i, :]
        dx_ref[...] = _bf(dx)
        r8 =lax.broadcasted_iota(jnp.int32, (8, cb), 0)
        dw = jnp.zeros((8, cb), F32)
        for i in range(CONV_K):
            dw = dw + jnp.where(r8 == CONV_K - 1 - i, jnp.sum(da * shifted[i], axis=0, keepdims=True), 0.0)
        dw_ref[...] += dw

    return pl.pallas_call(
        body, name="conv_bwd", grid=(CONV_CH // cb, b),
        in_specs=[pl.BlockSpec((s, cb), lambda j, bb: (bb, c0 + j)),
                  pl.BlockSpec((8, cb), lambda j, bb: (0, j)),
                  pl.BlockSpec((None, s, cb), lambda j, bb: (j, bb, 0))],
        out_specs=[pl.BlockSpec((s, cb), lambda j, bb: (bb, j)),
                   pl.BlockSpec((8, cb), lambda j, bb: (0, j))],
        out_shape=[jax.ShapeDtypeStruct((b * s, CONV_CH), BF16), jax.ShapeDtypeStruct((8, CONV_CH), F32)],
        compiler_params=_cp(("parallel", "arbitrary"), VMEM_LIMIT),
    )(proj, conv_w8, dc3)


def _pick_lane(v, k):
    lane = lax.broadcasted_iota(jnp.int32, v.shape, 1)
    return jnp.sum(jnp.where(lane == k, v, 0.0), axis=1, keepdims=True)


def _chunk_masks(ncb):
    i = lax.broadcasted_iota(jnp.int32, (ncb, CHUNK, CHUNK), 1)
    j = lax.broadcasted_iota(jnp.int32, (ncb, CHUNK, CHUNK), 2)
    return i, j


def _col_of_row(rowvec, eye):
    return jnp.sum(jnp.where(eye, rowvec, 0.0), axis=2, keepdims=True)


def _dn_chunk_math(cq, ck, cv, bd, al_row, dtb_row, h, ncb, tm=None):
    r = ncb * CHUNK
    i, j = _chunk_masks(ncb)
    eye = i == j
    low = i >= j
    strict = i > j
    ones = jnp.ones((ncb, CHUNK, CHUNK), F32)

    braw = _pick_lane(bd, h)
    draw = _pick_lane(bd, B_HEADS + h)
    al = _pick_lane(al_row, h)
    dtb = _pick_lane(dtb_row, h)
    ea = jnp.exp(al)
    beta = _sigmoid(braw)
    sp_arg = draw + dtb
    g = -ea * _softplus(sp_arg)

    rq = lax.rsqrt(jnp.sum(cq * cq, axis=1, keepdims=True) + EPS)
    rk = lax.rsqrt(jnp.sum(ck * ck, axis=1, keepdims=True) + EPS)
    nq = cq * rq
    kn = ck * rk
    qn = nq * (B_DIM ** -0.5)

    def c3(a):
        return a.reshape(ncb, CHUNK, a.shape[-1])

    qn3, kn3, v3, beta3 = c3(qn), c3(kn), c3(cv), c3(beta)
    gb = jnp.broadcast_to(c3(g), (ncb, CHUNK, CHUNK))
    gc_b = _bnn_exact(low.astype(BF16), gb)
    gr_b = _bnn_exact(_bf(ones), jnp.where(eye, gc_b, 0.0))
    dm = jnp.where(low, jnp.exp(jnp.where(low, gc_b - gr_b, 0.0)), 0.0)
    gc = gc_b[:, :, 0:1]
    gl = gc_b[:, CHUNK - 1:CHUNK, 0:1]
    gam = jnp.exp(gc)
    egl = jnp.exp(gl)
    edec = jnp.exp(gl - gc)

    knb = _bf(kn3)
    kk = _bnt(knb, knb)
    kd = jnp.where(strict, kk * dm, 0.0)
    a = beta3 * kd
    sz = 1 if tm is None else CHUNK
    if tm is None:
        tm = eye.astype(F32)
    while sz < CHUNK:
        off = jnp.where(((i // (2 * sz)) == (j // (2 * sz))) & ((i // sz) != (j // sz)), a, 0.0)
        tmb = _bf(tm)
        tm = tm - _bnn(_bf(_bnn(tmb, _bf(off))), tmb)
        sz *= 2
    bv = beta3 * v3
    bk = (beta3 * gam) * kn3
    sol = _bnn3(_split(tm), jnp.concatenate([bv, bk], axis=2))
    u, wk = sol[:, :, :B_DIM], sol[:, :, B_DIM:]
    qk = _bnt(_bf(qn3), knb)
    p = jnp.where(low, qk * dm, 0.0)
    kdec = kn3 * edec
    qg = gam * qn3
    return dict(beta=beta3, g=c3(g), ea=ea, sp_arg=c3(sp_arg), rq=c3(rq), rk=c3(rk), nq=c3(nq),
                qn=qn3, kn=kn3, v=v3, gc=gc, gl=gl, gam=gam, egl=egl, edec=edec, dm=dm, kd=kd, a=a,
                tm=tm, u=u, wk=wk, qk=qk, p=p, kdec=kdec, qg=qg, eye=eye, low=low, strict=strict)


def dn_prep(c, proj, al_row, dtb_row, b, s, ncb=8):
    t = b * s
    r = ncb * CHUNK
    nblk = t // r
    bd_blk = 0

    def body(cq_ref, ck_ref, cv_ref, bd_ref, al_ref, dtb_ref, u_ref, wk_ref, qg_ref, kdec_ref, p_ref, egl_ref,
             tm_ref):
        h = pl.program_id(1)
        m = _dn_chunk_math(cq_ref[...], ck_ref[...], cv_ref[...], bd_ref[...].astype(F32), al_ref[...], dtb_ref[...], h, ncb)
        tm_ref[...] = m["tm"].reshape(r, CHUNK)
        u_ref[...] = m["u"].reshape(r, B_DIM)
        wk_ref[...] = m["wk"].reshape(r, B_DIM)
        qg_ref[...] = m["qg"].reshape(r, B_DIM)
        kdec_ref[...] = m["kdec"].reshape(r, B_DIM)
        p_ref[...] = m["p"].reshape(r, CHUNK)
        egl_ref[...] = jnp.broadcast_to(m["egl"], (ncb, 8, 128)).reshape(ncb * 8, 128)

    col = lambda k: pl.BlockSpec((r, 128), lambda i, h: (i, k * B_HEADS + h))
    out_col = pl.BlockSpec((r, 128), lambda i, h: (i, h))
    small = pl.BlockSpec((1, 128), lambda i, h: (0, 0))
    return pl.pallas_call(
        body, name="dn_prep", grid=(nblk, B_HEADS),
        in_specs=[col(0), col(1), col(2), pl.BlockSpec((r, 128), lambda i, h: (i, bd_blk)), small, small],
        out_specs=[out_col, out_col, out_col, out_col,
                   pl.BlockSpec((None, r, CHUNK), lambda i, h: (h, i, 0)),
                   pl.BlockSpec((None, ncb * 8, 128), lambda i, h: (h, i, 0)),
                   pl.BlockSpec((None, r, CHUNK), lambda i, h: (h, i, 0))],
        out_shape=[jax.ShapeDtypeStruct((t, B_WIDTH), F32)] * 4
        + [jax.ShapeDtypeStruct((B_HEADS, t, CHUNK), F32),
           jax.ShapeDtypeStruct((B_HEADS, t // 8, 128), F32),
           jax.ShapeDtypeStruct((B_HEADS, t, CHUNK), F32)],
        compiler_params=_cp(("parallel", "parallel"), VMEM_LIMIT),
    )(c, c, c, proj, al_row, dtb_row)


def dn_scan_fwd(u, wk, qg, kdec, p, egl, b, s):
    t = b * s
    nc = s // CHUNK

    def body(u_ref, wk_ref, qg_ref, kdec_ref, p_ref, egl_ref, o_ref, ss_ref, st):
        @pl.when(pl.program_id(0) == 0)
        def _():
            st[...] = jnp.zeros_like(st)

        for bb in range(b):
            for h in range(B_HEADS):
                sl = slice(h * B_DIM, (h + 1) * B_DIM)
                sh = st[bb * B_HEADS + h]
                ss_ref[bb, h] = sh
                sb = _bf(sh)
                w = u_ref[bb, :, sl] - _nt(_bf(wk_ref[bb, :, sl]), sb)
                o_ref[bb, :, sl] = _nt(_bf(qg_ref[bb, :, sl]), sb) + _nn(_bf(p_ref[h, bb]), _bf(w))
                st[bb * B_HEADS + h] = egl_ref[h, bb][0:1, :] * sh + _tn(_bf(w), _bf(kdec_ref[bb, :, sl]))

    r3 = lambda a: a.reshape(b, s, B_WIDTH)
    act = pl.BlockSpec((b, CHUNK, B_WIDTH), lambda n: (0, n, 0))
    o, states = pl.pallas_call(
        body, name="dn_scan_fwd", grid=(nc,),
        in_specs=[act, act, act, act,
                  pl.BlockSpec((B_HEADS, b, CHUNK, CHUNK), lambda n: (0, 0, n, 0)),
                  pl.BlockSpec((B_HEADS, b, 8, 128), lambda n: (0, 0, n, 0))],
        out_specs=[act, pl.BlockSpec((b, None, B_HEADS, B_DIM, B_DIM), lambda n: (0, n, 0, 0, 0))],
        out_shape=[jax.ShapeDtypeStruct((b, s, B_WIDTH), F32),
                   jax.ShapeDtypeStruct((b, nc, B_HEADS, B_DIM, B_DIM), F32)],
        scratch_shapes=[pltpu.VMEM((b * B_HEADS, B_DIM, B_DIM), F32)],
        compiler_params=_cp(("arbitrary",), VMEM_LIMIT),
    )(r3(u), r3(wk), r3(qg), r3(kdec), p.reshape(B_HEADS, b, s, CHUNK), egl.reshape(B_HEADS, b, s // 8, 128))
    return o.reshape(t, B_WIDTH), states


def dn_scan_bwd(u, wk, qg, kdec, p, egl, states, do, b, s):
    t = b * s
    nc = s // CHUNK

    def body(u_ref, wk_ref, qg_ref, kdec_ref, p_ref, egl_ref, ss_ref, do_ref,
             dw_ref, dwk_ref, dqg_ref, dkdec_ref, dp_ref, degl_ref, dst):
        @pl.when(pl.program_id(0) == 0)
        def _():
            dst[...] = jnp.zeros_like(dst)

        for bb in range(b):
            for h in range(B_HEADS):
                sl = slice(h * B_DIM, (h + 1) * B_DIM)
                k = bb * B_HEADS + h
                sh = ss_ref[bb, h]
                sb = _bf(sh)
                dsp = dst[k]
                dsb = _bf(dsp)
                wkb = _bf(wk_ref[bb, :, sl])
                kdb = _bf(kdec_ref[bb, :, sl])
                pb = _bf(p_ref[h, bb])
                dob = _bf(do_ref[bb, :, sl])
                w = u_ref[bb, :, sl] - _nt(wkb, sb)
                wb = _bf(w)
                dw = _tn(pb, dob) + _nt(kdb, dsb)
                dwb = _bf(dw)
                dw_ref[bb, :, sl] = dw
                dqg_ref[bb, :, sl] = _nn(dob, sb)
                dwk_ref[bb, :, sl] = -_nn(dwb, sb)
                dkdec_ref[bb, :, sl] = _nn(wb, dsb)
                dp_ref[h, bb] = _nt(dob, wb)
                tot = jnp.sum(jnp.sum(sh * dsp, axis=1, keepdims=True), axis=0, keepdims=True)
                degl_ref[h, bb] = jnp.broadcast_to(tot, (8, 128))
                dst[k] = egl_ref[h, bb][0:1, :] * dsp + _tn(dob, _bf(qg_ref[bb, :, sl])) - _tn(dwb, wkb)

    r3 = lambda a: a.reshape(b, s, B_WIDTH)
    act = pl.BlockSpec((b, CHUNK, B_WIDTH), lambda n: (0, nc - 1 - n, 0))
    pspec = pl.BlockSpec((B_HEADS, b, CHUNK, CHUNK), lambda n: (0, 0, nc - 1 - n, 0))
    espec = pl.BlockSpec((B_HEADS, b, 8, 128), lambda n: (0, 0, nc - 1 - n, 0))
    outs = pl.pallas_call(
        body, name="dn_scan_bwd", grid=(nc,),
        in_specs=[act, act, act, act, pspec, espec,
                  pl.BlockSpec((b, None, B_HEADS, B_DIM, B_DIM), lambda n: (0, nc - 1 - n, 0, 0, 0)),
                  act],
        out_specs=[act, act, act, act, pspec, espec],
        out_shape=[jax.ShapeDtypeStruct((b, s, B_WIDTH), F32)] * 4
        + [jax.ShapeDtypeStruct((B_HEADS, b, s, CHUNK), F32),
           jax.ShapeDtypeStruct((B_HEADS, b, s // 8, 128), F32)],
        scratch_shapes=[pltpu.VMEM((b * B_HEADS, B_DIM, B_DIM), F32)],
        compiler_params=_cp(("arbitrary",), VMEM_LIMIT),
    )(r3(u), r3(wk), r3(qg), r3(kdec), p.reshape(B_HEADS, b, s, CHUNK), egl.reshape(B_HEADS, b, s // 8, 128),
      states, r3(do))
    return (*[a.reshape(t, B_WIDTH) for a in outs[:4]], outs[4].reshape(B_HEADS, t, CHUNK),
            outs[5].reshape(B_HEADS, t // 8, 128))


def dn_post_bwd(c, proj, al_row, dtb_row, tmat, dw, dwk, dqg, dkdec, dp, degl, b, s, ncb=8):
    t = b * s
    r = ncb * CHUNK
    nblk = t // r
    bd_blk = 0

    def body(cq_ref, ck_ref, cv_ref, bd_ref, al_ref, dtb_ref, tm_ref, dw_ref, dwk_ref, dqg_ref, dkdec_ref, dp_ref,
             degl_ref, dc_ref, dbd_ref, dal_ref, ddtb_ref):
        h = pl.program_id(1)

        @pl.when((pl.program_id(0) == 0) & (h == 0))
        def _():
            dal_ref[...] = jnp.zeros_like(dal_ref)
            ddtb_ref[...] = jnp.zeros_like(ddtb_ref)

        m = _dn_chunk_math(cq_ref[...], ck_ref[...], cv_ref[...], bd_ref[...].astype(F32), al_ref[...], dtb_ref[...], h, ncb,
                           tm=tm_ref[...].reshape(ncb, CHUNK, CHUNK))
        eye, low, strict = m["eye"], m["low"], m["strict"]
        eyef = eye.astype(F32)

        def c3(a):
            return a.reshape(ncb, CHUNK, a.shape[-1])

        du, dwkv, dqg, dkdec = c3(dw_ref[...]), c3(dwk_ref[...]), c3(dqg_ref[...]), c3(dkdec_ref[...])
        dpm = jnp.where(low, c3(dp_ref[...]), 0.0)
        degl = degl_ref[...].reshape(ncb, 8, 128)[:, 0:1, 0:1]
        beta, gam, kn, qn, v = m["beta"], m["gam"], m["kn"], m["qn"], m["v"]
        dm, kd, a, p = m["dm"], m["kd"], m["a"], m["p"]
        knb, qnb = _bf(kn), _bf(qn)

        eyeb = _bf(eyef)
        th, tl = _split(m["tm"])
        tts = (_bf(_bnt(eyeb, th)), _bf(_bnt(eyeb, tl)))
        xy = _bnn3(tts, jnp.concatenate([du, dwkv], axis=2))
        x, y = xy[:, :, :B_DIM], xy[:, :, B_DIM:]
        da = -jnp.where(strict, _bnt(_bf(x), _bf(m["u"])) + _bnt(_bf(y), _bf(m["wk"])), 0.0)
        dv = beta * x
        sy = jnp.sum(y * kn, axis=2, keepdims=True)
        dbeta = jnp.sum(x * v, axis=2, keepdims=True) + gam * sy + jnp.sum(da * kd, axis=2, keepdims=True)
        dgam = beta * sy + jnp.sum(dqg * qn, axis=2, keepdims=True)
        dkk = da * beta * dm
        dqk = dpm * dm
        dkkb, dqkb = _bf(dkk), _bf(dqk)
        dkn = ((beta * gam) * y + _bnn(dkkb, knb) + _bnn(_bf(_bnt(eyeb, dkkb)), knb)
               + _bnn(_bf(_bnt(eyeb, dqkb)), qnb) + dkdec * m["edec"])
        dqn = gam * dqg + _bnn(dqkb, knb)
        mm = da * a + dpm * p
        ek = jnp.sum(dkdec * m["kdec"], axis=2, keepdims=True)
        dgc = (jnp.sum(mm, axis=2, keepdims=True) - _col_of_row(jnp.sum(mm, axis=1, keepdims=True), eye)
               + dgam * gam - ek)
        dgl = jnp.sum(ek, axis=1, keepdims=True) + degl * m["egl"]
        i, _ = _chunk_masks(ncb)
        dgc = dgc + jnp.where(i[:, :, 0:1] == CHUNK - 1, dgl, 0.0)
        upper = (i <= _chunk_masks(ncb)[1]).astype(BF16)
        dg = _bnn_exact(upper, jnp.broadcast_to(dgc, (ncb, CHUNK, CHUNK)))[:, :, 0:1]

        nq = m["nq"]
        dnq = dqn * (B_DIM ** -0.5)
        dcq = m["rq"] * (dnq - nq * jnp.sum(nq * dnq, axis=2, keepdims=True))
        dck = m["rk"] * (dkn - kn * jnp.sum(kn * dkn, axis=2, keepdims=True))
        dc_ref[0] = dcq.reshape(r, B_DIM)
        dc_ref[1] = dck.reshape(r, B_DIM)
        dc_ref[2] = dv.reshape(r, B_DIM)

        dbraw = (dbeta * beta * (1.0 - beta)).reshape(r, 1)
        sgm = _sigmoid(m["sp_arg"])
        ddraw3 = dg * (-m["ea"]) * sgm
        ddraw = ddraw3.reshape(r, 1)
        lane = lax.broadcasted_iota(jnp.int32, (r, 128), 1)
        contrib = jnp.where(lane == h, dbraw, 0.0) + jnp.where(lane == B_HEADS + h, ddraw, 0.0)

        @pl.when(h == 0)
        def _():
            dbd_ref[...] = contrib

        @pl.when(h != 0)
        def _():
            dbd_ref[...] += contrib

        lane8 = lax.broadcasted_iota(jnp.int32, (8, 128), 1)
        tot_al = jnp.sum(jnp.sum(dg * m["g"], axis=1, keepdims=True), axis=0, keepdims=True).reshape(1, 1)
        tot_dtb = jnp.sum(jnp.sum(ddraw3, axis=1, keepdims=True), axis=0, keepdims=True).reshape(1, 1)
        dal_ref[...] += jnp.where(lane8 == h, tot_al, 0.0)
        ddtb_ref[...] += jnp.where(lane8 == h, tot_dtb, 0.0)

    col = lambda k: pl.BlockSpec((r, 128), lambda i, h: (i, k * B_HEADS + h))
    hcol = pl.BlockSpec((r, 128), lambda i, h: (i, h))
    small = pl.BlockSpec((1, 128), lambda i, h: (0, 0))
    acc = pl.BlockSpec((8, 128), lambda i, h: (0, 0))
    return pl.pallas_call(
        body, name="dn_post_bwd", grid=(nblk, B_HEADS),
        in_specs=[col(0), col(1), col(2), pl.BlockSpec((r, 128), lambda i, h: (i, bd_blk)), small, small,
                  pl.BlockSpec((None, r, CHUNK), lambda i, h: (h, i, 0)),
                  hcol, hcol, hcol, hcol,
                  pl.BlockSpec((None, r, CHUNK), lambda i, h: (h, i, 0)),
                  pl.BlockSpec((None, ncb * 8, 128), lambda i, h: (h, i, 0))],
        out_specs=[pl.BlockSpec((3, r, 128), lambda i, h: (0, i, h)),
                   pl.BlockSpec((r, 128), lambda i, h: (i, 0)), acc, acc],
        out_shape=[jax.ShapeDtypeStruct((3, t, B_WIDTH), F32), jax.ShapeDtypeStruct((t, 128), F32),
                   jax.ShapeDtypeStruct((8, 128), F32), jax.ShapeDtypeStruct((8, 128), F32)],
        compiler_params=_cp(("arbitrary", "arbitrary"), VMEM_LIMIT),
    )(c, c, c, proj, al_row, dtb_row, tmat, dw, dwk, dqg, dkdec, dp, degl)


def make_bias_band(rel_bias):
    tail = bias_tail(jnp.pad(rel_bias, ((0, 0), (0, 384 - N_REL))))
    far = jnp.broadcast_to(rel_bias[:, 2 * REL_CLIP][:, None, None], (A_HEADS, CHUNK, BAND - TAIL))
    band = jnp.concatenate([far, jnp.transpose(tail, (1, 0, 2))], axis=2)
    off = jnp.full((A_HEADS, CHUNK, CHUNK), -1e30, F32)
    both = jnp.stack([jnp.concatenate([band, off], axis=2), jnp.concatenate([off, band], axis=2)], axis=1)
    return both.reshape(4, 4 * CHUNK, WIN)


def bias_band_grad(dbt, dbf):
    t5 = dbt.reshape(A_HEADS, 2, CHUNK, 256)
    tail = t5[:, 0, :, :TAIL] + t5[:, 1, :, CHUNK:]
    far = dbf.reshape(A_HEADS, 2, CHUNK, 128).sum(axis=1) + jnp.pad(t5[:, 1, :, :CHUNK], ((0, 0), (0, 0), (0, CHUNK)))
    return bias_grad(jnp.transpose(tail, (1, 0, 2)), far)[:, :N_REL]


def _rms(x):
    r = lax.rsqrt(jnp.mean(x * x, axis=-1, keepdims=True) + EPS)
    return r, x * r


def _rms_bwd(dh, g, r, n):
    dn = dh * g
    return r * (dn - n * jnp.mean(dn * n, axis=-1, keepdims=True)), dh * n


def _gated_onorm(o, z, w_on):
    parts = []
    for h in range(B_HEADS):
        sl = slice(h * B_DIM, (h + 1) * B_DIM)
        r, n = _rms(o[:, sl])
        parts.append((r, n))
    r4 = [p[0] for p in parts]
    n4 = jnp.concatenate([p[1] for p in parts], axis=1)
    w4 = jnp.concatenate([w_on] * B_HEADS, axis=1)
    sz = _sigmoid(z)
    silu = z * sz
    return n4 * w4 * silu, r4, n4, w4, sz, silu


def mid_fwd(x, y_a, o_b, proj, w_on, wa, wb, w_out, tm=256):
    t = x.shape[0]
    tm = min(tm, t)

    def body(x_ref, ya_ref, ob_ref, z_ref, ga_ref, gb_ref, won_ref, wa_ref, wb_ref, wo_ref, x1_ref, mg_ref):
        yb = _gated_onorm(ob_ref[...], z_ref[...].astype(F32), won_ref[...])[0]
        ua = _nn(_bf(ya_ref[...]), wa_ref[...])
        ub = _nn(_bf(yb), wb_ref[...])
        merged = _sigmoid(ga_ref[...].astype(F32)) * ua + _sigmoid(gb_ref[...].astype(F32)) * ub
        mb = _bf(merged)
        mg_ref[...] = mb
        x1_ref[...] = x_ref[...] + _nn(mb, wo_ref[...])

    rowd = pl.BlockSpec((tm, D_MODEL), lambda i: (i, 0))
    row5 = pl.BlockSpec((tm, 512), lambda i: (i, 0))
    full = lambda a: pl.BlockSpec(a.shape, lambda i: (0,) * a.ndim)
    return pl.pallas_call(
        body, name="mid_fwd", grid=(t // tm,),
        in_specs=[rowd, row5, row5,
                  pl.BlockSpec((tm, 512), lambda i: (i, P_Z // 512)),
                  pl.BlockSpec((tm, D_MODEL), lambda i: (i, 0)),
                  pl.BlockSpec((tm, D_MODEL), lambda i: (i, 1)),
                  full(w_on), full(wa), full(wb), full(w_out)],
        out_specs=[rowd, rowd],
        out_shape=[jax.ShapeDtypeStruct((t, D_MODEL), F32), jax.ShapeDtypeStruct((t, D_MODEL), BF16)],
        compiler_params=_cp(("parallel",), VMEM_LIMIT),
    )(x, y_a, o_b, proj, proj, proj, w_on, wa, wb, w_out)


def mid_bwd(dx1, merged, y_a, o_b, proj, w_on, wa, wb, w_out, tm=256):
    t = dx1.shape[0]
    tm = min(tm, t)

    def body(dx1_ref, mg_ref, ya_ref, ob_ref, z_ref, ga_ref, gb_ref, won_ref, wa_ref, wb_ref, wo_ref,
             dya_ref, dob_ref, dz_ref, dg_ref, dwo_ref, dwa_ref, dwb_ref, dwon_ref):
        @pl.when(pl.program_id(0) == 0)
        def _():
            dwo_ref[...] = jnp.zeros_like(dwo_ref)
            dwa_ref[...] = jnp.zeros_like(dwa_ref)
            dwb_ref[...] = jnp.zeros_like(dwb_ref)
            dwon_ref[...] = jnp.zeros_like(dwon_ref)

        dx1b = _bf(dx1_ref[...])
        dmerged = _nt(dx1b, wo_ref[...])
        dwo_ref[...] += _tn(mg_ref[...], dx1b)
        o = ob_ref[...]
        z = z_ref[...].astype(F32)
        yb, r4, n4, w4, sz, silu = _gated_onorm(o, z, won_ref[...])
        yab, ybb = _bf(ya_ref[...]), _bf(yb)
        ua = _nn(yab, wa_ref[...])
        ub = _nn(ybb, wb_ref[...])
        sa, sb = _sigmoid(ga_ref[...].astype(F32)), _sigmoid(gb_ref[...].astype(F32))
        dua, dub = _bf(dmerged * sa), _bf(dmerged * sb)
        dg_ref[:, 0:D_MODEL] = _bf(dmerged * ua * sa * (1.0 - sa))
        dg_ref[:, D_MODEL:2 * D_MODEL] = _bf(dmerged * ub * sb * (1.0 - sb))
        dwa_ref[...] += _tn(yab, dua)
        dwb_ref[...] += _tn(ybb, dub)
        dya_ref[...] = _nt(dua, wa_ref[...])
        dyb = _nt(dub, wb_ref[...])
        dz_ref[...] = _bf(dyb * (n4 * w4) * (sz * (1.0 + z * (1.0 - sz))))
        dnw = dyb * silu
        dwon = jnp.zeros((1, B_DIM), F32)
        for h in range(B_HEADS):
            sl = slice(h * B_DIM, (h + 1) * B_DIM)
            dxh, dgh = _rms_bwd(dnw[:, sl], won_ref[...], r4[h], n4[:, sl])
            dob_ref[:, sl] = dxh
            dwon = dwon + jnp.sum(dgh, axis=0, keepdims=True)
        dwon_ref[...] += jnp.broadcast_to(dwon, (8, B_DIM))

    rowd = pl.BlockSpec((tm, D_MODEL), lambda i: (i, 0))
    row5 = pl.BlockSpec((tm, 512), lambda i: (i, 0))
    full = lambda a: pl.BlockSpec(a.shape, lambda i: (0,) * a.ndim)
    fixed = lambda shp: pl.BlockSpec(shp, lambda i: (0,) * len(shp))
    return pl.pallas_call(
        body, name="mid_bwd", grid=(t // tm,),
        in_specs=[rowd, rowd, row5, row5,
                  pl.BlockSpec((tm, 512), lambda i: (i, P_Z // 512)),
                  pl.BlockSpec((tm, D_MODEL), lambda i: (i, 0)),
                  pl.BlockSpec((tm, D_MODEL), lambda i: (i, 1)),
                  full(w_on), full(wa), full(wb), full(w_out)],
        out_specs=[row5, row5, row5, pl.BlockSpec((tm, 2 * D_MODEL), lambda i: (i, 0)),
                   fixed((D_MODEL, D_MODEL)), fixed((A_WIDTH, D_MODEL)), fixed((B_WIDTH, D_MODEL)),
                   fixed((8, B_DIM))],
        out_shape=[jax.ShapeDtypeStruct((t, 512), F32), jax.ShapeDtypeStruct((t, 512), F32),
                   jax.ShapeDtypeStruct((t, 512), BF16), jax.ShapeDtypeStruct((t, 2 * D_MODEL), BF16),
           jax.ShapeDtypeStruct((D_MODEL, D_MODEL), F32), jax.ShapeDtypeStruct((A_WIDTH, D_MODEL), F32),
           jax.ShapeDtypeStruct((B_WIDTH, D_MODEL), F32), jax.ShapeDtypeStruct((8, B_DIM), F32)],
        compiler_params=_cp(("arbitrary",), VMEM_LIMIT),
    )(dx1, merged, y_a, o_b, proj, proj, proj, w_on, wa, wb, w_out)


FFN_TF = 1408


def ffn_up(x1, g, w_gu, tm=512, tf=FFN_TF):
    t = x1.shape[0]
    tm = min(tm, t)
    nf = D_FF // tf

    def body(x_ref, g_ref, wg_ref, wu_ref, gate_ref, up_ref, act_ref, h_ref):
        @pl.when(pl.program_id(1) == 0)
        def _():
            r, n = _rms(x_ref[...])
            h_ref[...] = _bf(n * g_ref[...])

        hb = h_ref[...]
        gate = _nn(hb, wg_ref[...])
        up = _nn(hb, wu_ref[...])
        gate_ref[...] = _bf(gate)
        up_ref[...] = _bf(up)
        act_ref[...] = _bf(gate * _sigmoid(gate) * up)

    ff = pl.BlockSpec((tm, tf), lambda i, j: (i, j))
    return pl.pallas_call(
        body, name="ffn_up", grid=(t // tm, nf),
        in_specs=[pl.BlockSpec((tm, D_MODEL), lambda i, j: (i, 0)),
                  pl.BlockSpec((1, D_MODEL), lambda i, j: (0, 0)),
                  pl.BlockSpec((D_MODEL, tf), lambda i, j: (0, j)),
                  pl.BlockSpec((D_MODEL, tf), lambda i, j: (0, nf + j))],
        out_specs=[ff, ff, ff, pl.BlockSpec((tm, D_MODEL), lambda i, j: (i, 0))],
        out_shape=[jax.ShapeDtypeStruct((t, D_FF), BF16)] * 3 + [jax.ShapeDtypeStruct((t, D_MODEL), BF16)],
        compiler_params=_cp(("parallel", "arbitrary"), VMEM_LIMIT),
    )(x1, g, w_gu, w_gu)


def matmul_residual(a, w, res, name, tm=512, tk=FFN_TF):
    t, k = a.shape
    n = w.shape[1]
    tm = min(tm, t)

    def body(a_ref, w_ref, r_ref, o_ref):
        @pl.when(pl.program_id(1) == 0)
        def _():
            o_ref[...] = r_ref[...]

        o_ref[...] += _nn(a_ref[...], w_ref[...])

    return pl.pallas_call(
        body, name=name, grid=(t // tm, k // tk),
        in_specs=[pl.BlockSpec((tm, tk), lambda i, j: (i, j)),
                  pl.BlockSpec((tk, n), lambda i, j: (j, 0)),
                  pl.BlockSpec((tm, n), lambda i, j: (i, 0))],
        out_specs=pl.BlockSpec((tm, n), lambda i, j: (i, 0)),
        out_shape=jax.ShapeDtypeStruct((t, n), F32),
        compiler_params=_cp(("parallel", "arbitrary"), VMEM_LIMIT),
    )(a, w, res)


def ffn_act_bwd(dx2, gate, up, w_down, tm=512, tf=FFN_TF):
    t = dx2.shape[0]
    tm = min(tm, t)

    def body(dx2_ref, gate_ref, up_ref, wd_ref, dgate_ref, dup_ref, dx2b_ref):
        @pl.when(pl.program_id(1) == 0)
        def _():
            dx2b_ref[...] = _bf(dx2_ref[...])

        dact = _nt(dx2b_ref[...], wd_ref[...])
        gt, upv = gate_ref[...].astype(F32), up_ref[...].astype(F32)
        sg = _sigmoid(gt)
        dgate_ref[...] = _bf(dact * upv * (sg * (1.0 + gt * (1.0 - sg))))
        dup_ref[...] = _bf(dact * (gt * sg))

    ff = pl.BlockSpec((tm, tf), lambda i, j: (i, j))
    return pl.pallas_call(
        body, name="ffn_act_bwd", grid=(t // tm, D_FF // tf),
        in_specs=[pl.BlockSpec((tm, D_MODEL), lambda i, j: (i, 0)), ff, ff,
                  pl.BlockSpec((tf, D_MODEL), lambda i, j: (j, 0))],
        out_specs=[ff, ff],
        out_shape=[jax.ShapeDtypeStruct((t, D_FF), BF16)] * 2,
        scratch_shapes=[pltpu.VMEM((tm, D_MODEL), BF16)],
        compiler_params=_cp(("parallel", "arbitrary"), VMEM_LIMIT),
    )(dx2, gate, up, w_down)


def tail_fwd_bwd(x2, p, target, g_ple, g_final, w_pg, w_pp, tm=256):
    t = x2.shape[0]
    tm = min(tm, t)

    def body(x_ref, p_ref, t_ref, gp_ref, gf_ref, wpg_ref, wpp_ref,
             dx_ref, dwpg_ref, dwpp_ref, dgp_ref, dgf_ref, loss_ref):
        @pl.when(pl.program_id(0) == 0)
        def _():
            dwpg_ref[...] = jnp.zeros_like(dwpg_ref)
            dwpp_ref[...] = jnp.zeros_like(dwpp_ref)
            dgp_ref[...] = jnp.zeros_like(dgp_ref)
            dgf_ref[...] = jnp.zeros_like(dgf_ref)
            loss_ref[...] = jnp.zeros_like(loss_ref)

        x2v = x_ref[...]
        gp, gf = gp_ref[...], gf_ref[...]
        r3, n3 = _rms(x2v)
        h3b = _bf(n3 * gp)
        pb = _bf(p_ref[...])
        pg = _sigmoid(_nn(h3b, wpg_ref[...]))
        pp = _nn(pb, wpp_ref[...])
        x3 = x2v + pg * pp
        r4, n4 = _rms(x3)
        err = n4 * gf - t_ref[...]
        part = 0.5 * jnp.sum(jnp.sum(err * err, axis=1, keepdims=True), axis=0, keepdims=True) / D_MODEL
        loss_ref[...] += jnp.broadcast_to(part, (8, 128))
        dy = err * (1.0 / D_MODEL)
        dx3, dgf = _rms_bwd(dy, gf, r4, n4)
        dgf_ref[...] += jnp.broadcast_to(jnp.sum(dgf, axis=0, keepdims=True), (8, D_MODEL))
        dzp = _bf(dx3 * pp * pg * (1.0 - pg))
        dpp = _bf(dx3 * pg)
        dwpg_ref[...] += _tn(h3b, dzp)
        dwpp_ref[...] += _tn(pb, dpp)
        dh3 = _nt(dzp, wpg_ref[...])
        dx, dgp = _rms_bwd(dh3, gp, r3, n3)
        dgp_ref[...] += jnp.broadcast_to(jnp.sum(dgp, axis=0, keepdims=True), (8, D_MODEL))
        dx_ref[...] = dx3 + dx

    rowd = pl.BlockSpec((tm, D_MODEL), lambda i: (i, 0))
    fixed = lambda shp: pl.BlockSpec(shp, lambda i: (0,) * len(shp))
    return pl.pallas_call(
        body, name="tail_fwd_bwd", grid=(t // tm,),
        in_specs=[rowd, pl.BlockSpec((tm, PLE_DIM), lambda i: (i, 0)), rowd,
                  fixed((1, D_MODEL)), fixed((1, D_MODEL)), fixed((D_MODEL, D_MODEL)), fixed((PLE_DIM, D_MODEL))],
        out_specs=[rowd, fixed((D_MODEL, D_MODEL)), fixed((PLE_DIM, D_MODEL)),
                   fixed((8, D_MODEL)), fixed((8, D_MODEL)), fixed((8, 128))],
        out_shape=[jax.ShapeDtypeStruct((t, D_MODEL), F32), jax.ShapeDtypeStruct((D_MODEL, D_MODEL), F32),
                   jax.ShapeDtypeStruct((PLE_DIM, D_MODEL), F32), jax.ShapeDtypeStruct((8, D_MODEL), F32),
                   jax.ShapeDtypeStruct((8, D_MODEL), F32), jax.ShapeDtypeStruct((8, 128), F32)],
        compiler_params=_cp(("arbitrary",), VMEM_LIMIT),
    )(x2, p, target, g_ple, g_final, w_pg, w_pp)


def in_proj_bwd(pieces, weights, x, dx1, g, name="in_proj_bwd", tm=256):
    t = x.shape[0]
    tm = min(tm, t)
    k = len(pieces)
    assert all(c0 % wd == 0 and w0 % wd == 0 for (_, c0, wd), (_, w0) in zip(pieces, weights))

    def body(*refs):
        p_refs, w_refs = refs[:k], refs[k:2 * k]
        x_ref, dx1_ref, g_ref, dx_ref, dg_ref = refs[2 * k:]

        @pl.when(pl.program_id(0) == 0)
        def _():
            dg_ref[...] = jnp.zeros_like(dg_ref)

        dh = _nt(_bf(p_refs[0][...]), w_refs[0][...])
        for pr, wr in zip(p_refs[1:], w_refs[1:]):
            dh = dh + _nt(_bf(pr[...]), wr[...])
        r, n = _rms(x_ref[...])
        dx, dgc = _rms_bwd(dh, g_ref[...], r, n)
        dx_ref[...] = dx1_ref[...] + dx
        dg_ref[...] += jnp.broadcast_to(jnp.sum(dgc, axis=0, keepdims=True), (8, D_MODEL))

    rowd = pl.BlockSpec((tm, D_MODEL), lambda i: (i, 0))
    return pl.pallas_call(
        body, name=name, grid=(t // tm,),
        in_specs=[pl.BlockSpec((tm, wd), functools.partial(lambda i, cb: (i, cb), cb=c0 // wd))
                  for _, c0, wd in pieces]
        + [pl.BlockSpec((w.shape[0], wd), functools.partial(lambda i, cb: (0, cb), cb=w0 // wd))
           for (w, w0), (_, _, wd) in zip(weights, pieces)]
        + [rowd, rowd, pl.BlockSpec((1, D_MODEL), lambda i: (0, 0))],
        out_specs=[rowd, pl.BlockSpec((8, D_MODEL), lambda i: (0, 0))],
        out_shape=[jax.ShapeDtypeStruct((t, D_MODEL), F32), jax.ShapeDtypeStruct((8, D_MODEL), F32)],
        compiler_params=_cp(("arbitrary",), VMEM_LIMIT),
    )(*[a for a, _, _ in pieces], *[w for w, _ in weights], x, dx1, g)


def adamw(w, g, m, v, name, rows_cap=256):
    lead = w.shape[:-2]
    r, c = w.shape[-2:]
    tr = r
    for cand in range(8, min(r, rows_cap) + 1, 8):
        if r % cand == 0:
            tr = cand

    def body(w_ref, g_ref, m_ref, v_ref, d_ref, mo_ref, vo_ref):
        gv = g_ref[...]
        mn = ADAM_B1 * m_ref[...] + (1.0 - ADAM_B1) * gv
        vn = ADAM_B2 * v_ref[...] + (1.0 - ADAM_B2) * (gv * gv)
        m_hat = mn / (1.0 - ADAM_B1 ** ADAM_STEP)
        v_hat = vn / (1.0 - ADAM_B2 ** ADAM_STEP)
        d_ref[...] = -ADAM_LR * (m_hat / (jnp.sqrt(v_hat) + ADAM_EPS) + ADAM_WD * w_ref[...])
        mo_ref[...] = mn
        vo_ref[...] = vn

    spec = pl.BlockSpec((None,) * len(lead) + (tr, c), lambda i: (0,) * len(lead) + (i, 0))
    return pl.pallas_call(
        body, name=name, grid=(r // tr,),
        in_specs=[spec] * 4, out_specs=[spec] * 3,
        out_shape=[jax.ShapeDtypeStruct(w.shape, F32)] * 3,
        compiler_params=_cp(("parallel",), VMEM_LIMIT),
    )(w, g.reshape(w.shape), m, v)


class Standalone:
    def __init__(self, later_weights):
        self.later_weights = later_weights

    def begin(self, *a):
        return 0.0

    forward = exchange = join = begin

    def finish(self, after):
        return self.later_weights


def local_step(x3d, p3d, target3d, g4, small, later, early):
    b, s, _ = x3d.shape
    t = b * s
    x = x3d.reshape(t, D_MODEL)
    p = p3d.reshape(t, PLE_DIM)
    target = target3d.reshape(t, D_MODEL)
    cut = SPLIT_Z - 2 * (D_IN // N_CHIPS)
    w_inp = jnp.concatenate([g4[2][:, cut + 8:], g4[3], g4[0], g4[1], g4[2][:, :cut], g4[2][:, cut:cut + 8],
                             jnp.zeros((D_MODEL, 120), BF16)], axis=1)
    al_row = jnp.pad(small["a_log"].reshape(1, B_HEADS), ((0, 0), (0, 128 - B_HEADS)))
    dtb_row = jnp.pad(small["dt_bias"].reshape(1, B_HEADS), ((0, 0), (0, 128 - B_HEADS)))
    conv_w8 = jnp.pad(small["conv_w"].reshape(CONV_K, CONV_CH), ((0, 8 - CONV_K), (0, 0)))
    w_on = small["w_onorm"].reshape(1, B_DIM)
    g_mix, g_ffn = small["g_mix"].reshape(1, D_MODEL), small["g_ffn"].reshape(1, D_MODEL)
    g_ple, g_final = small["g_ple"].reshape(1, D_MODEL), small["g_final"].reshape(1, D_MODEL)
    bias_band = make_bias_band(small["rel_bias"].reshape(A_HEADS, N_REL))

    tok = later.begin()
    proj, h1, bd32 = rms_matmul(x, g_mix + tok, w_inp, "in_proj", tm=1024)
    y_a, lse = attn_fwd(proj, bias_band, b, s)
    tok = later.forward(lse)
    c = conv_fwd(proj, conv_w8 + tok, b, s)
    u, wk, qg, kdec, pm, egl, tmat = dn_prep(c, bd32, al_row, dtb_row, b, s)
    o_b, states = dn_scan_fwd(u, wk, qg, kdec, pm, egl, b, s)
    wts = later.finish(o_b)
    x1, merged = mid_fwd(x, y_a, o_b, proj, w_on, wts["w_branch_a"], wts["w_branch_b"], wts["w_out"])
    gate, up, act, h2 = ffn_up(x1, g_ffn, wts["w_gate_up"])
    x2 = matmul_residual(act, wts["w_down"], x1, "ffn_down")

    dx2, dw_pg, dw_pp, dg_ple, dg_final, loss = tail_fwd_bwd(
        x2, p, target, g_ple, g_final, wts["w_ple_gate"], wts["w_ple_proj"])
    dgate, dup = ffn_act_bwd(dx2, gate, up, wts["w_down"])
    w_gu = wts["w_gate_up"]
    dx1, dg_ffn = in_proj_bwd([(dgate, 0, D_FF), (dup, 0, D_FF)], [(w_gu, 0), (w_gu, D_FF)], x1, dx2, g_ffn,
                              name="ffn_in_bwd")
    dw_down = matmul_tn(act, dx2, "dw_down")
    dw_gu = matmul_tn(h2, dgate, "dw_gate", width=2 * D_FF)
    dw_gu = matmul_tn(h2, dup, "dw_up", into=dw_gu, col0=D_FF)
    dy_a, do_b, dz, dgates, dw_out, dwa, dwb, dw_on = mid_bwd(
        dx1, merged, y_a, o_b, proj, w_on, wts["w_branch_a"], wts["w_branch_b"], wts["w_out"])
    tok = early.begin(dict(w_branch_a=dwa, w_branch_b=dwb, w_out=dw_out, w_gate_up=dw_gu, w_down=dw_down,
                           w_ple_gate=dw_pg, w_ple_proj=dw_pp))
    ddw, ddwk, ddqg, ddkdec, ddp, ddegl = dn_scan_bwd(u, wk, qg, kdec, pm, egl + tok, states, do_b, b, s)
    tok = early.exchange(ddegl)
    dc3, dbd, dal, ddtb = dn_post_bwd(c, bd32, al_row + tok, dtb_row, tmat, ddw, ddwk, ddqg, ddkdec, ddp, ddegl, b, s)
    dconv, dconv_w = conv_bwd(proj, conv_w8, dc3, b, s)
    dqa, dka, dva, dbt, dbf = attn_bwd(proj, bias_band, y_a, lse, dy_a, b, s)
    tok = early.join(dqa)
    d_rel = bias_band_grad(dbt, dbf)

    pieces = [dgates, dqa, dka, dva, dconv, dz, dbd]
    bounds = [0, 2048, 2560, 3072, 3584, 5120, 5632, 5760]
    windows = [(dgates, 0, 2048), (dqa, 0, 512), (dka, 0, 512), (dva, 0, 512), (dconv, 0, 512), (dconv, 512, 512),
               (dconv, 1024, 512), (dz, 0, 512), (dbd, 0, 128)]
    w_cols = [0, P_QA, P_KA, P_VA, P_CONV, P_CONV + 512, P_CONV + 1024, P_Z, P_BD]
    dx, dg_mix = in_proj_bwd(windows, [(w_inp, c0) for c0 in w_cols], x, dx1, g_mix + tok)
    dwp = None
    for k, pc in enumerate(pieces):
        dwp = matmul_tn(h1, pc, "dw_in_%d" % k, into=dwp, col0=bounds[k], width=P_WIDTH)
    reduced_early = early.finish(dwp)
    dw_in = jnp.concatenate([dwp[:, P_QA:P_BD + 8], dwp[:, :P_QA]], axis=1)

    grads = dict(w_in=dw_in, w_branch_a=dwa, w_branch_b=dwb, w_out=dw_out, w_gate_up=dw_gu, w_down=dw_down,
                 w_ple_gate=dw_pg, w_ple_proj=dw_pp)
    small_grads = dict(g_mix=dg_mix[0], g_ffn=dg_ffn[0], g_ple=dg_ple[0], g_final=dg_final[0],
                       conv_w=dconv_w[:CONV_K].reshape(-1), rel_bias=d_rel.reshape(-1), w_onorm=dw_on[0],
                       a_log=dal[0, :B_HEADS], dt_bias=ddtb[0, :B_HEADS], loss=loss[0, :1])
    return dx.reshape(b, s, D_MODEL), grads, small_grads, reduced_early


BIG = (("w_in", (D_MODEL, D_IN), 1), ("w_branch_a", (A_WIDTH, D_MODEL), 1), ("w_branch_b", (B_WIDTH, D_MODEL), 1),
       ("w_out", (D_MODEL, D_MODEL), 0), ("w_gate_up", (D_MODEL, 2 * D_FF), 1), ("w_down", (D_FF, D_MODEL), 0),
       ("w_ple_gate", (D_MODEL, D_MODEL), 0), ("w_ple_proj", (PLE_DIM, D_MODEL), 1))
N_CHIPS = 4
FIRST_WEIGHTS = ("w_in",)
LATER_WEIGHTS = ("w_branch_a", "w_branch_b", "w_out", "w_gate_up", "w_down", "w_ple_gate", "w_ple_proj")
LATE_GRADS = ("w_in",)
EARLY_GRADS = ("w_branch_a", "w_branch_b", "w_out", "w_gate_up", "w_down", "w_ple_gate", "w_ple_proj")


def _items(names):
    return [it for it in BIG if it[0] in names]


def _shard_shape(shape, axis):
    return (shape[0] // N_CHIPS, shape[1]) if axis == 0 else (shape[0], shape[1] // N_CHIPS)


def _pack_rows_of(names):
    return -(-sum(sh[0] * sh[1] for _, sh, _ in _items(names)) // (N_CHIPS * 128 * 512)) * 512


def _pack_rows(parts, total):
    used = sum(a.shape[-2] for a in parts)
    pad = jnp.zeros(parts[0].shape[:-2] + (total - used, 128), parts[0].dtype)
    return jnp.concatenate(parts + [pad], axis=-2)


def pack_grads(grads, names):
    parts = []
    for n, shape, axis in _items(names):
        rs, cs = _shard_shape(shape, axis)
        g = grads[n].astype(BF16)
        seg = g.reshape(N_CHIPS, rs, cs) if axis == 0 else jnp.transpose(g.reshape(rs, N_CHIPS, cs), (1, 0, 2))
        parts.append(seg.reshape(N_CHIPS, -1, 128))
    return _pack_rows(parts, _pack_rows_of(names))


def unpack_shard(flat, names):
    out, r0 = {}, 0
    for n, shape, axis in _items(names):
        rs, cs = _shard_shape(shape, axis)
        nr = rs * cs // 128
        out[n] = flat[r0:r0 + nr].reshape(rs, cs)
        r0 += nr
    return out


def _place():
    return lax.axis_index("x"), lax.axis_index("y"), lax.axis_index("c")


ANY = pl.BlockSpec(memory_space=pl.ANY)


def _gathered_shape(item):
    n, shape, _ = item
    return (N_CHIPS,) + _shard_shape(shape, 1) if n == "w_in" else shape


def _gather_block(o_ref, item, cx, cy, hf):
    n, shape, axis = item
    rs, cs = _shard_shape(shape, axis)
    hr = rs // 2
    ci = 2 * cx + cy
    if n == "w_in":
        return o_ref.at[ci, pl.ds(pl.multiple_of(hf * hr, 16), hr), :]
    if axis == 0:
        return o_ref.at[pl.ds(pl.multiple_of(ci * rs + hf * hr, 16), hr), :]
    return o_ref.at[pl.ds(pl.multiple_of(hf * hr, 16), hr), pl.ds(pl.multiple_of(ci * cs, 128), cs)]


def _own_half(w_ref, item, c):
    hr = _shard_shape(item[1], item[2])[0] // 2
    return w_ref.at[pl.ds(pl.multiple_of(c * hr, 16), hr), :]


def _gather_slot(o_ref, item, cx, cy):
    n, shape, axis = item
    rs, cs = _shard_shape(shape, axis)
    ci = 2 * cx + cy
    if n == "w_in":
        return o_ref.at[ci]
    if axis == 0:
        return o_ref.at[pl.ds(pl.multiple_of(ci * rs, 16), rs), :]
    return o_ref.at[:, pl.ds(pl.multiple_of(ci * cs, 128), cs)]


def _other_chips(x, y):
    return [(1 - x, y), (x, 1 - y), (1 - x, 1 - y)]


def allgather_weights(shards, names, chip):
    items = _items(names)
    nw = len(items)

    def body(*refs):
        w_refs, o_refs = refs[:nw], refs[nw:2 * nw]
        send_sems, recv_sems = refs[2 * nw:]
        x, y, c = _place()
        sibling = (x, y, 1 - c)
        chips = _other_chips(x, y)

        def copy(k, src, dst, to):
            return pltpu.make_async_remote_copy(src_ref=src, dst_ref=dst, send_sem=send_sems.at[k],
                                                recv_sem=recv_sems.at[k], device_id=to, device_id_type=MESH)

        def blk(i, cx, cy, hf):
            return _gather_block(o_refs[i], items[i], cx, cy, hf)

        def my_half(i):
            return _own_half(w_refs[i], items[i], c)

        def own(i):
            return _gather_slot(o_refs[i], items[i], x, y)

        first = [copy(7 * i + j, my_half(i), blk(i, x, y, c), (*chip_, c))
                 for i in range(nw) for j, chip_ in enumerate(chips)]
        first += [copy(7 * i + 6, w_refs[i], own(i), sibling) for i in range(nw)]
        for cp in first:
            cp.start()
        passed = []
        for i in range(nw):
            for j, chip_ in enumerate(chips):
                copy(7 * i + j, my_half(i), blk(i, *chip_, c), (*chip_, c)).wait_recv()
                fwd = copy(7 * i + 3 + j, blk(i, *chip_, c), blk(i, *chip_, c), sibling)
                fwd.start()
                passed.append(fwd)
        for i in range(nw):
            for j, chip_ in enumerate(chips):
                copy(7 * i + 3 + j, my_half(i), blk(i, *chip_, 1 - c), sibling).wait_recv()
            copy(7 * i + 6, w_refs[i], own(i), sibling).wait_recv()
        for cp in first + passed:
            cp.wait_send()

    outs = pl.pallas_call(
        body, name="allgather_weights",
        in_specs=[ANY] * nw, out_specs=[ANY] * nw,
        out_shape=[jax.ShapeDtypeStruct(_gathered_shape(it), BF16) for it in items],
        scratch_shapes=[pltpu.SemaphoreType.DMA((7 * nw,)), pltpu.SemaphoreType.DMA((7 * nw,))],
    )(*[shards[it[0]] for it in items])
    return {it[0]: o for it, o in zip(items, outs)}


HBM_SPEC = pl.BlockSpec(memory_space=pltpu.HBM)
SEM_SPEC = pl.BlockSpec(memory_space=pltpu.SEMAPHORE)
EFFECT = pltpu.SideEffectType.DATAFLOW_SIDE_EFFECTING


def _in_hbm(a):
    return pltpu.with_memory_space_constraint(a, pltpu.HBM)


def copies_start(name, bufs, ncopies, plan):
    nb = len(bufs)

    def body(*refs):
        in_refs, send_sems, recv_sems, token = refs[:nb], refs[nb], refs[nb + 1], refs[-1]
        for k, (src, dst, to) in enumerate(plan(in_refs)):
            pltpu.make_async_remote_copy(src_ref=src, dst_ref=dst, send_sem=send_sems.at[k],
                                         recv_sem=recv_sems.at[k], device_id=to, device_id_type=MESH).start()
        token[...] = jnp.zeros_like(token)

    outs = pl.pallas_call(
        body, name=name,
        in_specs=[HBM_SPEC] * nb,
        out_specs=(SEM_SPEC, SEM_SPEC, *[HBM_SPEC] * nb, pl.BlockSpec(memory_space=pltpu.VMEM)),
        out_shape=(pltpu.SemaphoreType.DMA((ncopies,)), pltpu.SemaphoreType.DMA((ncopies,)),
                   *[pltpu.HBM(b.shape, b.dtype) for b in bufs], jax.ShapeDtypeStruct((8, 128), F32)),
        input_output_aliases={i: 2 + i for i in range(nb)},
        compiler_params=pltpu.CompilerParams(has_side_effects=EFFECT),
    )(*[_in_hbm(b) for b in bufs])
    return outs[0], outs[1], list(outs[2:2 + nb]), outs[-1][0, 0]


def copies_wait(name, send_sems, recv_sems, bufs, after, plan):
    nb = len(bufs)

    def body(*refs):
        in_refs, s_sems, r_sems = refs[:nb], refs[nb], refs[nb + 1]
        for k, (src, dst, to) in enumerate(plan(in_refs)):
            cp = pltpu.make_async_remote_copy(src_ref=src, dst_ref=dst, send_sem=s_sems.at[k],
                                              recv_sem=r_sems.at[k], device_id=to, device_id_type=MESH)
            cp.wait_send()
            cp.wait_recv()

    return list(pl.pallas_call(
        body, name=name,
        in_specs=[HBM_SPEC] * nb + [SEM_SPEC, SEM_SPEC, ANY],
        out_specs=tuple([HBM_SPEC] * nb),
        out_shape=tuple(pltpu.HBM(b.shape, b.dtype) for b in bufs),
        input_output_aliases={i: i for i in range(nb)},
        compiler_params=pltpu.CompilerParams(has_side_effects=EFFECT),
    )(*bufs, send_sems, recv_sems, after))


def _landing(shape, dtype):
    return _in_hbm(lax.empty(shape, dtype))


class LaterWeights:
    def __init__(self, shards, chip):
        self.items = _items(LATER_WEIGHTS)
        self.shards, self.chip = shards, chip
        self.nw = len(self.items)

    def _ici_plan(self, refs):
        x, y, c = _place()
        w_refs, o_refs = refs[:self.nw], refs[self.nw:]
        plan = [(_own_half(w_refs[i], it, c), _gather_block(o_refs[i], it, x, y, c), (*chip_, c))
                for i, it in enumerate(self.items) for chip_ in _other_chips(x, y)]
        return plan + [(w_refs[i], _gather_slot(o_refs[i], it, x, y), (x, y, 1 - c))
                       for i, it in enumerate(self.items)]

    def _d2d_plan(self, refs):
        x, y, c = _place()
        return [(_gather_block(refs[i], it, *chip_, c), _gather_block(refs[i], it, *chip_, c), (x, y, 1 - c))
                for i, it in enumerate(self.items) for chip_ in _other_chips(x, y)]

    def _d2d_wait_plan(self, refs):
        x, y, c = _place()
        return [(_gather_block(refs[i], it, *chip_, c), _gather_block(refs[i], it, *chip_, 1 - c), (x, y, 1 - c))
                for i, it in enumerate(self.items) for chip_ in _other_chips(x, y)]

    def _ici_wait_plan(self, refs):
        x, y, c = _place()
        w_refs, o_refs = refs[:self.nw], refs[self.nw:]
        plan = [(_own_half(w_refs[i], it, c), _gather_block(o_refs[i], it, *chip_, c), (*chip_, c))
                for i, it in enumerate(self.items) for chip_ in _other_chips(x, y)]
        return plan + [(w_refs[i], _gather_slot(o_refs[i], it, x, y), (x, y, 1 - c))
                       for i, it in enumerate(self.items)]

    def begin(self):
        srcs = [self.shards[it[0]] for it in self.items]
        lands = [_landing(_gathered_shape(it), BF16) for it in self.items]
        self.s1, self.r1, self.b1, tok = copies_start("gather_ici_start", srcs + lands, 4 * self.nw, self._ici_plan)
        return tok

    def forward(self, after):
        b1 = copies_wait("gather_ici_wait", self.s1, self.r1, self.b1, after, self._ici_wait_plan)
        self.s2, self.r2, self.b2, tok = copies_start("gather_d2d_start", b1[self.nw:], 3 * self.nw, self._d2d_plan)
        return tok

    def finish(self, after):
        outs = copies_wait("gather_d2d_wait", self.s2, self.r2, self.b2, after, self._d2d_wait_plan)
        return {it[0]: o for it, o in zip(self.items, outs)}


def small_allreduce(v, name):
    r = v.shape[0]

    def body(v_ref, o_ref, buf, send_sems, recv_sems):
        x, y, c = _place()
        me = 4 * x + 2 * y + c
        buf[me] = v_ref[...]
        flips = [(fx, fy, fc) for fx in (0, 1) for fy in (0, 1) for fc in (0, 1)][1:]
        peers = [((1 - x) if fx else x, (1 - y) if fy else y, (1 - c) if fc else c) for fx, fy, fc in flips]

        def copy(k, slot, to):
            return pltpu.make_async_remote_copy(src_ref=v_ref, dst_ref=buf.at[slot], send_sem=send_sems.at[k],
                                                recv_sem=recv_sems.at[k], device_id=to, device_id_type=MESH)

        sends = [copy(k, me, peer) for k, peer in enumerate(peers)]
        for cp in sends:
            cp.start()
        for k, (px, py, pc) in enumerate(peers):
            copy(k, 4 * px + 2 * py + pc, (px, py, pc)).wait_recv()
        for cp in sends:
            cp.wait_send()
        acc = buf[0]
        for d in range(1, 8):
            acc = acc + buf[d]
        o_ref[...] = acc

    return pl.pallas_call(
        body, name=name,
        in_specs=[pl.BlockSpec(memory_space=pltpu.VMEM)], out_specs=pl.BlockSpec(memory_space=pltpu.VMEM),
        out_shape=jax.ShapeDtypeStruct((r, 128), F32),
        scratch_shapes=[pltpu.VMEM((8, r, 128), F32), pltpu.SemaphoreType.DMA((7,)), pltpu.SemaphoreType.DMA((7,))],
    )(v)


def swap_halves(g):
    half = g.shape[1] // 2

    def body(g_ref, o_ref, send_sem, recv_sem):
        x, y, c = _place()
        cp = pltpu.make_async_remote_copy(
            src_ref=g_ref.at[:, pl.ds((1 - c) * half, half), :], dst_ref=o_ref, send_sem=send_sem,
            recv_sem=recv_sem, device_id=(x, y, 1 - c), device_id_type=MESH)
        cp.start()
        cp.wait()

    return pl.pallas_call(
        body, name="swap_halves", in_specs=[ANY], out_specs=ANY,
        out_shape=jax.ShapeDtypeStruct((N_CHIPS, half, 128), g.dtype),
        scratch_shapes=[pltpu.SemaphoreType.DMA, pltpu.SemaphoreType.DMA],
    )(g)


def add_halves(g, other, place):
    half = other.shape[1]
    tr = _tile_rows(half)
    nblk = half // tr

    def body(pref, g0, g1, g2, g3, o0, o1, o2, o3, pf_ref, pb_ref):
        f = lambda r: r[...].astype(F32)
        pf_ref[...] = f(g0) + f(o0)
        pb_ref[0] = _bf(f(g1) + f(o1))
        pb_ref[1] = _bf(f(g2) + f(o2))
        pb_ref[2] = _bf(f(g3) + f(o3))

    gspec = lambda k: pl.BlockSpec((None, tr, 128), lambda i, pr: ((pr[0] + k) % N_CHIPS, pr[1] * nblk + i, 0))
    ospec = lambda k: pl.BlockSpec((None, tr, 128), lambda i, pr: ((pr[0] + k) % N_CHIPS, i, 0))
    return pl.pallas_call(
        body, name="add_halves",
        grid_spec=pltpu.PrefetchScalarGridSpec(
            num_scalar_prefetch=1, grid=(nblk,),
            in_specs=[gspec(0), gspec(1), gspec(2), gspec(3), ospec(0), ospec(1), ospec(2), ospec(3)],
            out_specs=[pl.BlockSpec((tr, 128), lambda i, pr: (i, 0)),
                       pl.BlockSpec((3, tr, 128), lambda i, pr: (0, i, 0))]),
        out_shape=[jax.ShapeDtypeStruct((half, 128), F32), jax.ShapeDtypeStruct((3, half, 128), BF16)],
        compiler_params=_cp(("parallel",), VMEM_LIMIT),
    )(place, g, g, g, g, other, other, other, other)


def _tile_rows(n, cap=2048):
    best = 16
    for t in range(16, cap + 1, 16):
        if n % t == 0:
            best = t
    assert n % best == 0
    return best


def exchange_partials(pb):
    def body(p_ref, o_ref, send_sems, recv_sems):
        x, y, c = _place()
        me = 2 * x + y
        cps = []
        for k in range(1, N_CHIPS):
            to = (me + k) % N_CHIPS
            cps.append(pltpu.make_async_remote_copy(
                src_ref=p_ref.at[k - 1], dst_ref=o_ref.at[k - 1], send_sem=send_sems.at[k - 1],
                recv_sem=recv_sems.at[k - 1], device_id=(to // 2, to % 2, c), device_id_type=MESH))
        for cp in cps:
            cp.start()
        for cp in cps:
            cp.wait()

    return pl.pallas_call(
        body, name="exchange_partials", in_specs=[ANY], out_specs=ANY,
        out_shape=jax.ShapeDtypeStruct(pb.shape, pb.dtype),
        scratch_shapes=[pltpu.SemaphoreType.DMA((3,)), pltpu.SemaphoreType.DMA((3,))],
    )(pb)


def add_partials(pf, got, place):
    half = pf.shape[0]
    tr = _tile_rows(half)

    def body(pref, pf_ref, got_ref, o_ref):
        o_ref[...] = ((pf_ref[...] + got_ref[0].astype(F32)) + got_ref[1].astype(F32)) + got_ref[2].astype(F32)

    return pl.pallas_call(
        body, name="add_partials",
        grid_spec=pltpu.PrefetchScalarGridSpec(
            num_scalar_prefetch=1, grid=(half // tr,),
            in_specs=[pl.BlockSpec((tr, 128), lambda i, pr: (i, 0)),
                      pl.BlockSpec((3, tr, 128), lambda i, pr: (0, i, 0))],
            out_specs=pl.BlockSpec((None, tr, 128), lambda i, pr: (pr[1], i, 0))),
        out_shape=jax.ShapeDtypeStruct((2, half, 128), F32),
        compiler_params=_cp(("parallel",), VMEM_LIMIT),
    )(place, pf, got)


def join_halves(both):
    def body(r_ref, o_ref, send_sem, recv_sem):
        x, y, c = _place()
        cp = pltpu.make_async_remote_copy(src_ref=r_ref.at[c], dst_ref=o_ref.at[c], send_sem=send_sem,
                                          recv_sem=recv_sem, device_id=(x, y, 1 - c), device_id_type=MESH)
        cp.start()
        pltpu.make_async_remote_copy(src_ref=r_ref.at[c], dst_ref=o_ref.at[1 - c], send_sem=send_sem,
                                     recv_sem=recv_sem, device_id=(x, y, 1 - c), device_id_type=MESH).wait_recv()
        cp.wait_send()

    return pl.pallas_call(
        body, name="join_halves", in_specs=[ANY], out_specs=ANY,
        out_shape=jax.ShapeDtypeStruct(both.shape, F32),
        scratch_shapes=[pltpu.SemaphoreType.DMA, pltpu.SemaphoreType.DMA],
        input_output_aliases={0: 0},
    )(both)


def reduce_scatter_grads(gpack, place):
    other = swap_halves(gpack)
    pf, pb = add_halves(gpack, other, place)
    got = exchange_partials(pb)
    return join_halves(add_partials(pf, got, place)).reshape(gpack.shape[1], 128)


class EarlyGrads:
    def __init__(self, place):
        self.place = place

    @staticmethod
    def _swap_plan(refs):
        x, y, c = _place()
        g_ref, o_ref = refs
        half = o_ref.shape[1]
        return [(g_ref.at[:, pl.ds(pl.multiple_of((1 - c) * half, 16), half), :], o_ref, (x, y, 1 - c))]

    @staticmethod
    def _exchange_plan(refs):
        x, y, c = _place()
        p_ref, o_ref = refs
        me = 2 * x + y
        return [(p_ref.at[k - 1], o_ref.at[k - 1], (((me + k) % N_CHIPS) // 2, ((me + k) % N_CHIPS) % 2, c))
                for k in range(1, N_CHIPS)]

    @staticmethod
    def _join_plan(refs):
        x, y, c = _place()
        return [(refs[0].at[c], refs[0].at[c], (x, y, 1 - c))]

    @staticmethod
    def _join_wait_plan(refs):
        x, y, c = _place()
        return [(refs[0].at[c], refs[0].at[1 - c], (x, y, 1 - c))]

    def begin(self, grads):
        g = pack_grads(grads, EARLY_GRADS)
        land = _landing((N_CHIPS, g.shape[1] // 2, 128), BF16)
        self.s1, self.r1, self.b1, tok = copies_start("grads_swap_start", [g, land], 1, self._swap_plan)
        return tok

    def exchange(self, after):
        g, other = copies_wait("grads_swap_wait", self.s1, self.r1, self.b1, after, self._swap_plan)
        self.pf, pb = add_halves(g, other, self.place)
        land = _landing(pb.shape, BF16)
        self.s2, self.r2, self.b2, tok = copies_start("grads_exchange_start", [pb, land], 3, self._exchange_plan)
        return tok

    def join(self, after):
        _, got = copies_wait("grads_exchange_wait", self.s2, self.r2, self.b2, after, self._exchange_plan)
        both = add_partials(self.pf, got, self.place)
        self.s3, self.r3, self.b3, tok = copies_start("grads_join_start", [both], 1, self._join_plan)
        return tok

    def finish(self, after):
        (both,) = copies_wait("grads_join_wait", self.s3, self.r3, self.b3, after, self._join_wait_plan)
        return both.reshape(-1, 128)


SMALL = (("g_mix", D_MODEL), ("g_ffn", D_MODEL), ("g_ple", D_MODEL), ("g_final", D_MODEL),
         ("conv_w", CONV_K * CONV_CH), ("rel_bias", A_HEADS * N_REL), ("w_onorm", B_DIM),
         ("a_log", B_HEADS), ("dt_bias", B_HEADS), ("loss", 1))


def _pad128(v):
    v = v.reshape(-1)
    return jnp.pad(v, (0, -v.shape[0] % 128))


def pack_small(d, names, rows):
    flat = jnp.concatenate([_pad128(d[n]) for n in names]).reshape(-1, 128)
    return jnp.pad(flat, ((0, rows - flat.shape[0]), (0, 0)))


def unpack_small(flat, names_sizes):
    out, r0 = {}, 0
    v = flat.reshape(-1)
    for n, size in names_sizes:
        out[n] = v[r0:r0 + size]
        r0 += -(-size // 128) * 128
    return out


def kernel(x, p, g_mix, w_in, conv_w, a_log, dt_bias, rel_bias, w_onorm, w_branch_a, w_branch_b, w_out, g_ffn, w_gate_up, w_down, g_ple, w_ple_gate, w_ple_proj, g_final, loss_target, m_g_mix, m_w_in, m_conv_w, m_a_log, m_dt_bias, m_rel_bias, m_w_onorm, m_w_branch_a, m_w_branch_b, m_w_out, m_g_ffn, m_w_gate_up, m_w_down, m_g_ple, m_w_ple_gate, m_w_ple_proj, m_g_final, v_g_mix, v_w_in, v_conv_w, v_a_log, v_dt_bias, v_rel_bias, v_w_onorm, v_w_branch_a, v_w_branch_b, v_w_out, v_g_ffn, v_w_gate_up, v_w_down, v_g_ple, v_w_ple_gate, v_w_ple_proj, v_g_final):
    names = ["g_mix", "w_in", "conv_w", "a_log", "dt_bias", "rel_bias", "w_onorm", "w_branch_a", "w_branch_b",
             "w_out", "g_ffn", "w_gate_up", "w_down", "g_ple", "w_ple_gate", "w_ple_proj", "g_final"]
    w = dict(zip(names, [g_mix, w_in, conv_w, a_log, dt_bias, rel_bias, w_onorm, w_branch_a, w_branch_b, w_out,
                         g_ffn, w_gate_up, w_down, g_ple, w_ple_gate, w_ple_proj, g_final]))
    m = dict(zip(names, [m_g_mix, m_w_in, m_conv_w, m_a_log, m_dt_bias, m_rel_bias, m_w_onorm, m_w_branch_a,
                         m_w_branch_b, m_w_out, m_g_ffn, m_w_gate_up, m_w_down, m_g_ple, m_w_ple_gate,
                         m_w_ple_proj, m_g_final]))
    v = dict(zip(names, [v_g_mix, v_w_in, v_conv_w, v_a_log, v_dt_bias, v_rel_bias, v_w_onorm, v_w_branch_a,
                         v_w_branch_b, v_w_out, v_g_ffn, v_w_gate_up, v_w_down, v_g_ple, v_w_ple_gate,
                         v_w_ple_proj, v_g_final]))
    xi, yi, ci = _place()
    chip = 2 * xi + yi
    big_names = [n for n, _, _ in BIG]

    shards2d = {n: w[n].reshape(w[n].shape[-2:]) for n in big_names}
    shards_bf = {n: a.astype(BF16) for n, a in shards2d.items()}
    g4 = allgather_weights(shards_bf, FIRST_WEIGHTS, chip)["w_in"]
    place = jnp.stack([chip, ci]).astype(jnp.int32)
    conv_sh = jnp.where(ci == 0, w["conv_w"].reshape(CONV_K, CONV_CH // N_CHIPS), 0.0)
    conv_slots = lax.dynamic_update_slice(jnp.zeros((N_CHIPS, CONV_K, CONV_CH // N_CHIPS), F32), conv_sh[None],
                                          (chip, 0, 0))
    conv_all = small_allreduce(conv_slots.reshape(-1, 128), "gather_conv_w")
    conv_full = jnp.transpose(conv_all.reshape(N_CHIPS, CONV_K, CONV_CH // N_CHIPS), (1, 0, 2)).reshape(CONV_K, CONV_CH)
    small = {n: w[n] for n in names if n not in big_names}
    small["conv_w"] = conv_full

    grad_x, grads, small_grads, reduced_early = local_step(
        x, p[0], loss_target, g4, small, LaterWeights(shards_bf, chip), EarlyGrads(place))

    reduced_late = reduce_scatter_grads(pack_grads(grads, LATE_GRADS), place)
    gshard = {**unpack_shard(reduced_early, EARLY_GRADS), **unpack_shard(reduced_late, LATE_GRADS)}
    small_names = [n for n, _ in SMALL]
    red = unpack_small(small_allreduce(pack_small(small_grads, small_names, 112), "allreduce_small"), SMALL)
    loss = red["loss"][0]
    conv_g = lax.dynamic_slice(red["conv_w"].reshape(CONV_K, N_CHIPS, CONV_CH // N_CHIPS), (0, chip, 0),
                               (CONV_K, 1, CONV_CH // N_CHIPS))
    gsmall = {n: red[n].reshape(w[n].shape) for n in small_names if n not in ("loss", "conv_w")}
    gsmall["conv_w"] = conv_g.reshape(w["conv_w"].shape)

    grad, delta, new_m, new_v = {}, {}, {}, {}
    for n in big_names:
        shp = w[n].shape
        d_, m_, v_ = adamw(w[n], gshard[n], m[n], v[n], "adamw_" + n)
        grad[n], delta[n], new_m[n], new_v[n] = gshard[n].reshape(shp), d_.reshape(shp), m_.reshape(shp), v_.reshape(shp)
    snames = [n for n in small_names if n != "loss"]
    ssizes = [(n, w[n].size) for n in snames]
    pk = lambda d: pack_small(d, snames, 64)
    d_, m_, v_ = adamw(pk(w), pk(gsmall), pk(m), pk(v), "adamw_small")
    ds, ms, vs = unpack_small(d_, ssizes), unpack_small(m_, ssizes), unpack_small(v_, ssizes)
    for n in snames:
        shp = w[n].shape
        grad[n], delta[n], new_m[n], new_v[n] = gsmall[n], ds[n].reshape(shp), ms[n].reshape(shp), vs[n].reshape(shp)

    return (loss, grad_x, *[grad[n] for n in names], *[delta[n] for n in names],
            *[new_m[n] for n in names], *[new_v[n] for n in names])
```

```python
import functools

import jax
import jax.numpy as jnp
from jax import lax
from jax.experimental import pallas as pl
from jax.experimental.pallas import tpu as pltpu

F32 = jnp.float32
BF16 = jnp.bfloat16
HI = lax.Precision.HIGHEST
MESH = pl.DeviceIdType.MESH

D_MODEL = 1024
CHUNK = 64
PLE_DIM = 256
EPS = 1e-6
A_HEADS = 8
A_HEAD_DIM = 64
A_WIDTH = 512
A_LOOKBACK = 8
BAND = (A_LOOKBACK + 1) * CHUNK
TAIL = 3 * CHUNK
REL_CLIP = 128
N_REL = 2 * REL_CLIP + 1
B_HEADS = 4
B_DIM = 128
B_WIDTH = 512
CONV_K = 4
CONV_CH = 1536
D_FF = 2816
SPLIT_Z = 3584
D_IN = 5640
ADAM_LR, ADAM_B1, ADAM_B2, ADAM_EPS, ADAM_WD, ADAM_STEP = 0.001, 0.9, 0.999, 1e-08, 0.01, 10

P_GATES, P_QA, P_KA, P_VA, P_CONV, P_Z, P_BD, P_WIDTH = 0, 2048, 2560, 3072, 3584, 5120, 5632, 5760

VMEM_LIMIT = 56 * 1024 * 1024


def _cp(sem, vmem=None, **kw):
    return pltpu.CompilerParams(dimension_semantics=sem, vmem_limit_bytes=vmem, **kw)


def _tile(n, cap):
    best = None
    for t in range(128, cap + 1, 128):
        if n % t == 0:
            best = t
    assert best is not None, (n, cap)
    return best


def _nn(a, b, prec=None):
    return lax.dot_general(a, b, (((1,), (0,)), ((), ())), preferred_element_type=F32, precision=prec)


def _nt(a, b, prec=None):
    return lax.dot_general(a, b, (((1,), (1,)), ((), ())), preferred_element_type=F32, precision=prec)


def _tn(a, b, prec=None):
    return lax.dot_general(a, b, (((0,), (0,)), ((), ())), preferred_element_type=F32, precision=prec)


def _bnn(a, b, prec=None):
    return lax.dot_general(a, b, (((2,), (1,)), ((0,), (0,))), preferred_element_type=F32, precision=prec)


def _bnt(a, b, prec=None):
    return lax.dot_general(a, b, (((2,), (2,)), ((0,), (0,))), preferred_element_type=F32, precision=prec)


def _bf(a):
    return a.astype(BF16)


def _split(a):
    hi = a.astype(BF16)
    return hi, (a - hi.astype(F32)).astype(BF16)


def _bnn_exact(lhs_b, rhs):
    h1 = _bf(rhs)
    r1 = rhs - h1.astype(F32)
    h2 = _bf(r1)
    h3 = _bf(r1 - h2.astype(F32))
    return _bnn(lhs_b, h1) + (_bnn(lhs_b, h2) + _bnn(lhs_b, h3))


def _bnn3(a, b):
    ah, al = a if isinstance(a, tuple) else _split(a)
    bh, bl = b if isinstance(b, tuple) else _split(b)
    return _bnn(ah, bh) + (_bnn(ah, bl) + _bnn(al, bh))


def _sigmoid(x):
    return 0.5 * jnp.tanh(0.5 * x) + 0.5


def _softplus(x):
    return jnp.maximum(x, 0.0) + jnp.log(1.0 + jnp.exp(-jnp.abs(x)))


def rms_matmul(x, g, w, name, tm=512, tn_cap=1024):
    t, d = x.shape
    n = w.shape[1]
    tm = min(tm, t)
    tn = _tile(n, tn_cap)

    nj = n // tn

    def body(x_ref, g_ref, w_ref, o_ref, h_ref, tail_ref):
        @pl.when(pl.program_id(1) == 0)
        def _():
            xv = x_ref[...]
            r = lax.rsqrt(jnp.mean(xv * xv, axis=-1, keepdims=True) + EPS)
            h_ref[...] = _bf(xv * r * g_ref[...])

        res = _nn(h_ref[...], w_ref[...])
        o_ref[...] = _bf(res)

        @pl.when(pl.program_id(1) == nj - 1)
        def _():
            tail_ref[...] = res[:, tn - 128:]

    return pl.pallas_call(
        body, name=name, grid=(t // tm, nj),
        in_specs=[pl.BlockSpec((tm, d), lambda i, j: (i, 0)),
                  pl.BlockSpec((1, d), lambda i, j: (0, 0)),
                  pl.BlockSpec((d, tn), lambda i, j: (0, j))],
        out_specs=[pl.BlockSpec((tm, tn), lambda i, j: (i, j)),
                   pl.BlockSpec((tm, d), lambda i, j: (i, 0)),
                   pl.BlockSpec((tm, 128), lambda i, j: (i, 0))],
        out_shape=[jax.ShapeDtypeStruct((t, n), BF16), jax.ShapeDtypeStruct((t, d), BF16),
                   jax.ShapeDtypeStruct((t, 128), F32)],
        compiler_params=_cp(("parallel", "arbitrary"), VMEM_LIMIT),
    )(x, g, w)


def matmul_tn(a, b, name, into=None, col0=0, width=None, tm=1024, tk_cap=1408, tn_cap=1408):
    m, k1 = a.shape
    n = b.shape[1]
    tm = min(tm, m)
    tk = _tile(k1, tk_cap)
    tn = _tile(n, tn_cap)
    while col0 % tn:
        tn = _tile(n, tn - 128)
    nk = m // tm
    c0 = col0 // tn

    def body(*refs):
        a_ref, b_ref, o_ref, acc = refs[0], refs[1], refs[-2], refs[-1]

        @pl.when(pl.program_id(2) == 0)
        def _():
            acc[...] = jnp.zeros_like(acc)

        acc[...] += _tn(_bf(a_ref[...]), _bf(b_ref[...]))

        @pl.when(pl.program_id(2) == nk - 1)
        def _():
            o_ref[...] = _bf(acc[...])

    in_specs = [pl.BlockSpec((tm, tk), lambda i, j, k: (k, i)),
                pl.BlockSpec((tm, tn), lambda i, j, k: (k, j))]
    args = [a, b]
    total = n if width is None else width
    aliases = {}
    if into is not None:
        in_specs.append(ANY)
        args.append(into)
        total = into.shape[1]
        aliases = {2: 0}
    return pl.pallas_call(
        body, name=name, grid=(k1 // tk, n // tn, nk),
        in_specs=in_specs,
        out_specs=pl.BlockSpec((tk, tn), lambda i, j, k: (i, c0 + j)),
        out_shape=jax.ShapeDtypeStruct((k1, total), BF16),
        scratch_shapes=[pltpu.VMEM((tk, tn), F32)],
        input_output_aliases=aliases,
        compiler_params=_cp(("parallel", "parallel", "arbitrary"), VMEM_LIMIT),
    )(*args)


def _tail_onehot(qi):
    r = lax.broadcasted_iota(jnp.int32, (384, TAIL), 0)
    kj = lax.broadcasted_iota(jnp.int32, (384, TAIL), 1)
    return (r == jnp.minimum(REL_CLIP + qi - kj, REL_CLIP) + REL_CLIP).astype(F32)


def bias_tail(rel_pad):
    def body(rb_ref, o_ref):
        rb = rb_ref[...]
        for qi in range(CHUNK):
            o_ref[qi] = _nn(rb, _tail_onehot(qi), HI)

    return pl.pallas_call(
        body, name="bias_tail",
        out_shape=jax.ShapeDtypeStruct((CHUNK, A_HEADS, TAIL), F32),
    )(rel_pad)


def bias_grad(db_t, db_far):
    def body(t_ref, f_ref, o_ref):
        acc = jnp.zeros((A_HEADS, 384), F32)
        for qi in range(CHUNK):
            acc = acc + _nt(t_ref[qi], _tail_onehot(qi), HI)
        far = jnp.sum(jnp.sum(f_ref[...], axis=2), axis=1, keepdims=True)
        lane = lax.broadcasted_iota(jnp.int32, (A_HEADS, 384), 1)
        o_ref[...] = acc + jnp.where(lane == 2 * REL_CLIP, far, 0.0)

    return pl.pallas_call(
        body, name="bias_grad",
        out_shape=jax.ShapeDtypeStruct((A_HEADS, 384), F32),
    )(db_t, db_far)


ATT_CB = 8


WIN = BAND + CHUNK


def _stack_heads(a, lane):
    return jnp.concatenate([jnp.where(lane < 64, a, 0.0), jnp.where(lane >= 64, a, 0.0)], axis=0)


def _fill_band_pads(k_ref, v_ref, kp, vp, s):
    z = jnp.zeros((A_LOOKBACK * CHUNK, 128), BF16)
    kp[pl.ds(0, A_LOOKBACK * CHUNK), :] = z
    vp[pl.ds(0, A_LOOKBACK * CHUNK), :] = z
    kp[pl.ds(A_LOOKBACK * CHUNK, s), :] = _bf(k_ref[...])
    vp[pl.ds(A_LOOKBACK * CHUNK, s), :] = _bf(v_ref[...])


def attn_fwd(proj, bias_band, b, s):
    t = b * s
    nc = s // CHUNK
    qb, kb_, vb_ = P_QA // 128, P_KA // 128, P_VA // 128

    nstep = nc // ATT_CB
    rows = ATT_CB * CHUNK

    def body(q_ref, k_ref, v_ref, b_ref, o_ref, lse_ref, kp, vp):
        n0 = pl.program_id(2) * ATT_CB

        @pl.when(n0 == 0)
        def _():
            _fill_band_pads(k_ref, v_ref, kp, vp, s)

        lane = lax.broadcasted_iota(jnp.int32, (2 * CHUNK, 128), 1)
        col = lax.broadcasted_iota(jnp.int32, (4 * CHUNK, WIN), 1)
        bias4 = b_ref[...]

        def pair(pp, carry):
            n = n0 + 2 * pp
            r0 = pl.multiple_of(pp * 2 * CHUNK, 2 * CHUNK)
            start = pl.multiple_of(n * CHUNK, CHUNK)
            kb = kp[pl.ds(start, WIN), :]
            vb = vp[pl.ds(start, WIN), :]
            q4 = _stack_heads(q_ref[pl.ds(r0, 2 * CHUNK), :] * (A_HEAD_DIM ** -0.5), lane)
            sc = jnp.where(col >= (A_LOOKBACK - n) * CHUNK, _nt(_bf(q4), kb) + bias4, -1e30)
            mx = jnp.max(sc, axis=1, keepdims=True)
            p = jnp.exp(sc - mx)
            l = jnp.sum(p, axis=1, keepdims=True)
            o4 = _nn(_bf(p), vb) / l
            lse4 = mx + jnp.log(l)
            o_ref[pl.ds(r0, 2 * CHUNK), :] = jnp.where(lane < 64, o4[:2 * CHUNK], o4[2 * CHUNK:])
            lse_ref[pl.ds(r0, 2 * CHUNK), :] = jnp.where(lane < 64, lse4[:2 * CHUNK], lse4[2 * CHUNK:])
            return carry

        lax.fori_loop(0, ATT_CB // 2, pair, 0, unroll=2)

    return pl.pallas_call(
        body, name="attn_fwd", grid=(b, 4, nstep),
        in_specs=[pl.BlockSpec((rows, 128), lambda bb, m, n: (bb * nstep + n, qb + m)),
                  pl.BlockSpec((s, 128), lambda bb, m, n: (bb, kb_ + m)),
                  pl.BlockSpec((s, 128), lambda bb, m, n: (bb, vb_ + m)),
                  pl.BlockSpec((None, 4 * CHUNK, WIN), lambda bb, m, n: (m, 0, 0))],
        out_specs=[pl.BlockSpec((rows, 128), lambda bb, m, n: (bb * nstep + n, m)),
                   pl.BlockSpec((rows, 128), lambda bb, m, n: (bb * nstep + n, m))],
        out_shape=[jax.ShapeDtypeStruct((t, A_WIDTH), F32), jax.ShapeDtypeStruct((t, A_WIDTH), F32)],
        scratch_shapes=[pltpu.VMEM((s + A_LOOKBACK * CHUNK, 128), BF16),
                        pltpu.VMEM((s + A_LOOKBACK * CHUNK, 128), BF16)],
        compiler_params=_cp(("parallel", "parallel", "arbitrary"), VMEM_LIMIT),
    )(proj, proj, proj, bias_band)


def attn_bwd(proj, bias_band, y_a, lse, dy_a, b, s):
    t = b * s
    nc = s // CHUNK
    qb, kb_, vb_ = P_QA // 128, P_KA // 128, P_VA // 128
    pad = A_LOOKBACK * CHUNK
    nstep = nc // ATT_CB
    rows = ATT_CB * CHUNK

    def body(q_ref, k_ref, v_ref, b_ref, do_ref, o_ref, lse_ref,
             dq_ref, dk_ref, dv_ref, dbt_ref, dbf_ref, kp, vp, dkp, dvp):
        bb = pl.program_id(1)
        n0 = pl.program_id(2) * ATT_CB

        @pl.when(n0 == 0)
        def _():
            _fill_band_pads(k_ref, v_ref, kp, vp, s)
            dkp[...] = jnp.zeros_like(dkp)
            dvp[...] = jnp.zeros_like(dvp)

        @pl.when((n0 == 0) & (bb == 0))
        def _():
            dbt_ref[...] = jnp.zeros_like(dbt_ref)
            dbf_ref[...] = jnp.zeros_like(dbf_ref)

        lane = lax.broadcasted_iota(jnp.int32, (2 * CHUNK, 128), 1)
        col = lax.broadcasted_iota(jnp.int32, (4 * CHUNK, WIN), 1)
        bias4 = b_ref[...]

        def pair(pp, carry):
            n = n0 + 2 * pp
            r0 = pl.multiple_of(pp * 2 * CHUNK, 2 * CHUNK)
            start = pl.multiple_of(n * CHUNK, CHUNK)
            kb = kp[pl.ds(start, WIN), :]
            vb = vp[pl.ds(start, WIN), :]
            q4b = _bf(_stack_heads(q_ref[pl.ds(r0, 2 * CHUNK), :] * (A_HEAD_DIM ** -0.5), lane))
            do4 = _stack_heads(do_ref[pl.ds(r0, 2 * CHUNK), :], lane)
            do4b = _bf(do4)
            o = o_ref[pl.ds(r0, 2 * CHUNK), :]
            lsev = lse_ref[pl.ds(r0, 2 * CHUNK), :]
            lse4 = jnp.concatenate([lsev[:, 0:1], lsev[:, 64:65]], axis=0)
            sc = jnp.where(col >= (A_LOOKBACK - n) * CHUNK, _nt(q4b, kb) + bias4, -1e30)
            p = jnp.exp(sc - lse4)
            dp = _nt(do4b, vb)
            delta = jnp.sum(do4 * jnp.concatenate([o, o], axis=0), axis=1, keepdims=True)
            ds = p * (dp - delta)
            dsb = _bf(ds)
            dq4 = _nn(dsb, kb)
            dq_ref[pl.ds(r0, 2 * CHUNK), :] = _bf(
                jnp.where(lane < 64, dq4[:2 * CHUNK], dq4[2 * CHUNK:]) * (A_HEAD_DIM ** -0.5))
            dkp[pl.ds(start, WIN), :] += _tn(dsb, q4b)
            dvp[pl.ds(start, WIN), :] += _tn(_bf(p), do4b)
            dbt_ref[...] += ds[:, WIN - 256:]
            dbf_ref[...] += ds[:, 0:128] + ds[:, 128:256] + ds[:, 256:384]
            return carry

        lax.fori_loop(0, ATT_CB // 2, pair, 0, unroll=2)

        @pl.when(n0 == nc - ATT_CB)
        def _():
            dk_ref[...] = _bf(dkp[pl.ds(pad, s), :])
            dv_ref[...] = _bf(dvp[pl.ds(pad, s), :])

    return pl.pallas_call(
        body, name="attn_bwd", grid=(4, b, nstep),
        in_specs=[pl.BlockSpec((rows, 128), lambda m, bb, n: (bb * nstep + n, qb + m)),
                  pl.BlockSpec((s, 128), lambda m, bb, n: (bb, kb_ + m)),
                  pl.BlockSpec((s, 128), lambda m, bb, n: (bb, vb_ + m)),
                  pl.BlockSpec((None, 4 * CHUNK, WIN), lambda m, bb, n: (m, 0, 0)),
                  pl.BlockSpec((rows, 128), lambda m, bb, n: (bb * nstep + n, m)),
                  pl.BlockSpec((rows, 128), lambda m, bb, n: (bb * nstep + n, m)),
                  pl.BlockSpec((rows, 128), lambda m, bb, n: (bb * nstep + n, m))],
        out_specs=[pl.BlockSpec((rows, 128), lambda m, bb, n: (bb * nstep + n, m)),
                   pl.BlockSpec((s, 128), lambda m, bb, n: (bb, m)),
                   pl.BlockSpec((s, 128), lambda m, bb, n: (bb, m)),
                   pl.BlockSpec((None, 4 * CHUNK, 256), lambda m, bb, n: (m, 0, 0)),
                   pl.BlockSpec((None, 4 * CHUNK, 128), lambda m, bb, n: (m, 0, 0))],
        out_shape=[jax.ShapeDtypeStruct((t, A_WIDTH), BF16)] * 3
        + [jax.ShapeDtypeStruct((4, 4 * CHUNK, 256), F32),
           jax.ShapeDtypeStruct((4, 4 * CHUNK, 128), F32)],
        scratch_shapes=[pltpu.VMEM((s + pad, 128), BF16), pltpu.VMEM((s + pad, 128), BF16),
                        pltpu.VMEM((s + pad, 128), F32), pltpu.VMEM((s + pad, 128), F32)],
        compiler_params=_cp(("parallel", "arbitrary", "arbitrary"), VMEM_LIMIT),
    )(proj, proj, proj, bias_band, dy_a, y_a, lse)


def _conv_taps(x, w, s):
    row = lax.broadcasted_iota(jnp.int32, x.shape, 0)
    shifted = [x] + [jnp.where(row >= i, pltpu.roll(x, i, 0), 0.0) for i in range(1, CONV_K)]
    acc = shifted[0] * w[CONV_K - 1:CONV_K, :]
    for i in range(1, CONV_K):
        acc = acc + shifted[i] * w[CONV_K - 1 - i:CONV_K - i, :]
    return acc, shifted


def conv_fwd(proj, conv_w8, b, s):
    cb = 512
    c0 = P_CONV // cb

    def body(x_ref, w_ref, o_ref):
        a, _ = _conv_taps(x_ref[...].astype(F32), w_ref[...], s)
        o_ref[...] = a * _sigmoid(a)

    return pl.pallas_call(
        body, name="conv_fwd", grid=(b, CONV_CH // cb),
        in_specs=[pl.BlockSpec((s, cb), lambda bb, j: (bb, c0 + j)),
                  pl.BlockSpec((8, cb), lambda bb, j: (0, j))],
        out_specs=pl.BlockSpec((s, cb), lambda bb, j: (bb, j)),
        out_shape=jax.ShapeDtypeStruct((b * s, CONV_CH), F32),
        compiler_params=_cp(("parallel", "parallel"), VMEM_LIMIT),
    )(proj, conv_w8)


def conv_bwd(proj, conv_w8, dc3, b, s):
    cb = 512
    c0 = P_CONV // cb

    def body(x_ref, w_ref, dc_ref, dx_ref, dw_ref):
        @pl.when(pl.program_id(1) == 0)
        def _():
            dw_ref[...] = jnp.zeros_like(dw_ref)

        w = w_ref[...]
        a, shifted = _conv_taps(x_ref[...].astype(F32), w, s)
        sg = _sigmoid(a)
        da = dc_ref[...] * (sg * (1.0 + a * (1.0 - sg)))
        row = lax.broadcasted_iota(jnp.int32, da.shape, 0)
        dx = da * w[CONV_K - 1:CONV_K, :]
        for i in range(1, CONV_K):
            dx = dx + jnp.where(row < s - i, pltpu.roll(da, s - i, 0), 0.0) * w[CONV_K - 1 - i:CONV_K - i, :]
        dx_ref[...] = _bf(dx)
        r8 =lax.broadcasted_iota(jnp.int32, (8, cb), 0)
        dw = jnp.zeros((8, cb), F32)
        for i in range(CONV_K):
            dw = dw + jnp.where(r8 == CONV_K - 1 - i, jnp.sum(da * shifted[i], axis=0, keepdims=True), 0.0)
        dw_ref[...] += dw

    return pl.pallas_call(
        body, name="conv_bwd", grid=(CONV_CH // cb, b),
        in_specs=[pl.BlockSpec((s, cb), lambda j, bb: (bb, c0 + j)),
                  pl.BlockSpec((8, cb), lambda j, bb: (0, j)),
                  pl.BlockSpec((None, s, cb), lambda j, bb: (j, bb, 0))],
        out_specs=[pl.BlockSpec((s, cb), lambda j, bb: (bb, j)),
                   pl.BlockSpec((8, cb), lambda j, bb: (0, j))],
        out_shape=[jax.ShapeDtypeStruct((b * s, CONV_CH), BF16), jax.ShapeDtypeStruct((8, CONV_CH), F32)],
        compiler_params=_cp(("parallel", "arbitrary"), VMEM_LIMIT),
    )(proj, conv_w8, dc3)


def _pick_lane(v, k):
    lane = lax.broadcasted_iota(jnp.int32, v.shape, 1)
    return jnp.sum(jnp.where(lane == k, v, 0.0), axis=1, keepdims=True)


def _chunk_masks(ncb):
    i = lax.broadcasted_iota(jnp.int32, (ncb, CHUNK, CHUNK), 1)
    j = lax.broadcasted_iota(jnp.int32, (ncb, CHUNK, CHUNK), 2)
    return i, j


def _col_of_row(rowvec, eye):
    return jnp.sum(jnp.where(eye, rowvec, 0.0), axis=2, keepdims=True)


def _dn_chunk_math(cq, ck, cv, bd, al_row, dtb_row, h, ncb, tm=None):
    r = ncb * CHUNK
    i, j = _chunk_masks(ncb)
    eye = i == j
    low = i >= j
    strict = i > j
    ones = jnp.ones((ncb, CHUNK, CHUNK), F32)

    braw = _pick_lane(bd, h)
    draw = _pick_lane(bd, B_HEADS + h)
    al = _pick_lane(al_row, h)
    dtb = _pick_lane(dtb_row, h)
    ea = jnp.exp(al)
    beta = _sigmoid(braw)
    sp_arg = draw + dtb
    g = -ea * _softplus(sp_arg)

    rq = lax.rsqrt(jnp.sum(cq * cq, axis=1, keepdims=True) + EPS)
    rk = lax.rsqrt(jnp.sum(ck * ck, axis=1, keepdims=True) + EPS)
    nq = cq * rq
    kn = ck * rk
    qn = nq * (B_DIM ** -0.5)

    def c3(a):
        return a.reshape(ncb, CHUNK, a.shape[-1])

    qn3, kn3, v3, beta3 = c3(qn), c3(kn), c3(cv), c3(beta)
    gb = jnp.broadcast_to(c3(g), (ncb, CHUNK, CHUNK))
    gc_b = _bnn_exact(low.astype(BF16), gb)
    gr_b = _bnn_exact(_bf(ones), jnp.where(eye, gc_b, 0.0))
    dm = jnp.where(low, jnp.exp(jnp.where(low, gc_b - gr_b, 0.0)), 0.0)
    gc = gc_b[:, :, 0:1]
    gl = gc_b[:, CHUNK - 1:CHUNK, 0:1]
    gam = jnp.exp(gc)
    egl = jnp.exp(gl)
    edec = jnp.exp(gl - gc)

    knb = _bf(kn3)
    kk = _bnt(knb, knb)
    kd = jnp.where(strict, kk * dm, 0.0)
    a = beta3 * kd
    sz = 1 if tm is None else CHUNK
    if tm is None:
        tm = eye.astype(F32)
    while sz < CHUNK:
        off = jnp.where(((i // (2 * sz)) == (j // (2 * sz))) & ((i // sz) != (j // sz)), a, 0.0)
        tmb = _bf(tm)
        tm = tm - _bnn(_bf(_bnn(tmb, _bf(off))), tmb)
        sz *= 2
    bv = beta3 * v3
    bk = (beta3 * gam) * kn3
    sol = _bnn3(_split(tm), jnp.concatenate([bv, bk], axis=2))
    u, wk = sol[:, :, :B_DIM], sol[:, :, B_DIM:]
    qk = _bnt(_bf(qn3), knb)
    p = jnp.where(low, qk * dm, 0.0)
    kdec = kn3 * edec
    qg = gam * qn3
    return dict(beta=beta3, g=c3(g), ea=ea, sp_arg=c3(sp_arg), rq=c3(rq), rk=c3(rk), nq=c3(nq),
                qn=qn3, kn=kn3, v=v3, gc=gc, gl=gl, gam=gam, egl=egl, edec=edec, dm=dm, kd=kd, a=a,
                tm=tm, u=u, wk=wk, qk=qk, p=p, kdec=kdec, qg=qg, eye=eye, low=low, strict=strict)


def dn_prep(c, proj, al_row, dtb_row, b, s, ncb=16):
    t = b * s
    r = ncb * CHUNK
    nblk = t // r
    bd_blk = 0

    def body(cq_ref, ck_ref, cv_ref, bd_ref, al_ref, dtb_ref, u_ref, wk_ref, qg_ref, kdec_ref, p_ref, egl_ref,
             tm_ref):
        h = pl.program_id(1)
        m = _dn_chunk_math(cq_ref[...], ck_ref[...], cv_ref[...], bd_ref[...].astype(F32), al_ref[...], dtb_ref[...], h, ncb)
        tm_ref[...] = m["tm"].reshape(r, CHUNK)
        u_ref[...] = m["u"].reshape(r, B_DIM)
        wk_ref[...] = m["wk"].reshape(r, B_DIM)
        qg_ref[...] = m["qg"].reshape(r, B_DIM)
        kdec_ref[...] = m["kdec"].reshape(r, B_DIM)
        p_ref[...] = m["p"].reshape(r, CHUNK)
        egl_ref[...] = jnp.broadcast_to(m["egl"], (ncb, 8, 128)).reshape(ncb * 8, 128)

    col = lambda k: pl.BlockSpec((r, 128), lambda i, h: (i, k * B_HEADS + h))
    out_col = pl.BlockSpec((r, 128), lambda i, h: (i, h))
    small = pl.BlockSpec((1, 128), lambda i, h: (0, 0))
    return pl.pallas_call(
        body, name="dn_prep", grid=(nblk, B_HEADS),
        in_specs=[col(0), col(1), col(2), pl.BlockSpec((r, 128), lambda i, h: (i, bd_blk)), small, small],
        out_specs=[out_col, out_col, out_col, out_col,
                   pl.BlockSpec((None, r, CHUNK), lambda i, h: (h, i, 0)),
                   pl.BlockSpec((None, ncb * 8, 128), lambda i, h: (h, i, 0)),
                   pl.BlockSpec((None, r, CHUNK), lambda i, h: (h, i, 0))],
        out_shape=[jax.ShapeDtypeStruct((t, B_WIDTH), F32)] * 4
        + [jax.ShapeDtypeStruct((B_HEADS, t, CHUNK), F32),
           jax.ShapeDtypeStruct((B_HEADS, t // 8, 128), F32),
           jax.ShapeDtypeStruct((B_HEADS, t, CHUNK), F32)],
        compiler_params=_cp(("parallel", "parallel"), VMEM_LIMIT),
    )(c, c, c, proj, al_row, dtb_row)


def dn_scan_fwd(u, wk, qg, kdec, p, egl, b, s):
    t = b * s
    nc = s // CHUNK

    def body(u_ref, wk_ref, qg_ref, kdec_ref, p_ref, egl_ref, o_ref, ss_ref, st):
        @pl.when(pl.program_id(0) == 0)
        def _():
            st[...] = jnp.zeros_like(st)

        for bb in range(b):
            for h in range(B_HEADS):
                sl = slice(h * B_DIM, (h + 1) * B_DIM)
                sh = st[bb * B_HEADS + h]
                ss_ref[bb, h] = sh
                sb = _bf(sh)
                w = u_ref[bb, :, sl] - _nt(_bf(wk_ref[bb, :, sl]), sb)
                o_ref[bb, :, sl] = _nt(_bf(qg_ref[bb, :, sl]), sb) + _nn(_bf(p_ref[h, bb]), _bf(w))
                st[bb * B_HEADS + h] = egl_ref[h, bb][0:1, :] * sh + _tn(_bf(w), _bf(kdec_ref[bb, :, sl]))

    r3 = lambda a: a.reshape(b, s, B_WIDTH)
    act = pl.BlockSpec((b, CHUNK, B_WIDTH), lambda n: (0, n, 0))
    o, states = pl.pallas_call(
        body, name="dn_scan_fwd", grid=(nc,),
        in_specs=[act, act, act, act,
                  pl.BlockSpec((B_HEADS, b, CHUNK, CHUNK), lambda n: (0, 0, n, 0)),
                  pl.BlockSpec((B_HEADS, b, 8, 128), lambda n: (0, 0, n, 0))],
        out_specs=[act, pl.BlockSpec((b, None, B_HEADS, B_DIM, B_DIM), lambda n: (0, n, 0, 0, 0))],
        out_shape=[jax.ShapeDtypeStruct((b, s, B_WIDTH), F32),
                   jax.ShapeDtypeStruct((b, nc, B_HEADS, B_DIM, B_DIM), F32)],
        scratch_shapes=[pltpu.VMEM((b * B_HEADS, B_DIM, B_DIM), F32)],
        compiler_params=_cp(("arbitrary",), VMEM_LIMIT),
    )(r3(u), r3(wk), r3(qg), r3(kdec), p.reshape(B_HEADS, b, s, CHUNK), egl.reshape(B_HEADS, b, s // 8, 128))
    return o.reshape(t, B_WIDTH), states


def dn_scan_bwd(u, wk, qg, kdec, p, egl, states, do, b, s):
    t = b * s
    nc = s // CHUNK

    def body(u_ref, wk_ref, qg_ref, kdec_ref, p_ref, egl_ref, ss_ref, do_ref,
             dw_ref, dwk_ref, dqg_ref, dkdec_ref, dp_ref, degl_ref, dst):
        @pl.when(pl.program_id(0) == 0)
        def _():
            dst[...] = jnp.zeros_like(dst)

        for bb in range(b):
            for h in range(B_HEADS):
                sl = slice(h * B_DIM, (h + 1) * B_DIM)
                k = bb * B_HEADS + h
                sh = ss_ref[bb, h]
                sb = _bf(sh)
                dsp = dst[k]
                dsb = _bf(dsp)
                wkb = _bf(wk_ref[bb, :, sl])
                kdb = _bf(kdec_ref[bb, :, sl])
                pb = _bf(p_ref[h, bb])
                dob = _bf(do_ref[bb, :, sl])
                w = u_ref[bb, :, sl] - _nt(wkb, sb)
                wb = _bf(w)
                dw = _tn(pb, dob) + _nt(kdb, dsb)
                dwb = _bf(dw)
                dw_ref[bb, :, sl] = dw
                dqg_ref[bb, :, sl] = _nn(dob, sb)
                dwk_ref[bb, :, sl] = -_nn(dwb, sb)
                dkdec_ref[bb, :, sl] = _nn(wb, dsb)
                dp_ref[h, bb] = _nt(dob, wb)
                tot = jnp.sum(jnp.sum(sh * dsp, axis=1, keepdims=True), axis=0, keepdims=True)
                degl_ref[h, bb] = jnp.broadcast_to(tot, (8, 128))
                dst[k] = egl_ref[h, bb][0:1, :] * dsp + _tn(dob, _bf(qg_ref[bb, :, sl])) - _tn(dwb, wkb)

    r3 = lambda a: a.reshape(b, s, B_WIDTH)
    act = pl.BlockSpec((b, CHUNK, B_WIDTH), lambda n: (0, nc - 1 - n, 0))
    pspec = pl.BlockSpec((B_HEADS, b, CHUNK, CHUNK), lambda n: (0, 0, nc - 1 - n, 0))
    espec = pl.BlockSpec((B_HEADS, b, 8, 128), lambda n: (0, 0, nc - 1 - n, 0))
    outs = pl.pallas_call(
        body, name="dn_scan_bwd", grid=(nc,),
        in_specs=[act, act, act, act, pspec, espec,
                  pl.BlockSpec((b, None, B_HEADS, B_DIM, B_DIM), lambda n: (0, nc - 1 - n, 0, 0, 0)),
                  act],
        out_specs=[act, act, act, act, pspec, espec],
        out_shape=[jax.ShapeDtypeStruct((b, s, B_WIDTH), F32)] * 4
        + [jax.ShapeDtypeStruct((B_HEADS, b, s, CHUNK), F32),
           jax.ShapeDtypeStruct((B_HEADS, b, s // 8, 128), F32)],
        scratch_shapes=[pltpu.VMEM((b * B_HEADS, B_DIM, B_DIM), F32)],
        compiler_params=_cp(("arbitrary",), VMEM_LIMIT),
    )(r3(u), r3(wk), r3(qg), r3(kdec), p.reshape(B_HEADS, b, s, CHUNK), egl.reshape(B_HEADS, b, s // 8, 128),
      states, r3(do))
    return (*[a.reshape(t, B_WIDTH) for a in outs[:4]], outs[4].reshape(B_HEADS, t, CHUNK),
            outs[5].reshape(B_HEADS, t // 8, 128))


def dn_post_bwd(c, proj, al_row, dtb_row, tmat, dw, dwk, dqg, dkdec, dp, degl, b, s, ncb=16):
    t = b * s
    r = ncb * CHUNK
    nblk = t // r
    bd_blk = 0

    def body(cq_ref, ck_ref, cv_ref, bd_ref, al_ref, dtb_ref, tm_ref, dw_ref, dwk_ref, dqg_ref, dkdec_ref, dp_ref,
             degl_ref, dc_ref, dbd_ref, dal_ref, ddtb_ref):
        h = pl.program_id(1)

        @pl.when((pl.program_id(0) == 0) & (h == 0))
        def _():
            dal_ref[...] = jnp.zeros_like(dal_ref)
            ddtb_ref[...] = jnp.zeros_like(ddtb_ref)

        m = _dn_chunk_math(cq_ref[...], ck_ref[...], cv_ref[...], bd_ref[...].astype(F32), al_ref[...], dtb_ref[...], h, ncb,
                           tm=tm_ref[...].reshape(ncb, CHUNK, CHUNK))
        eye, low, strict = m["eye"], m["low"], m["strict"]
        eyef = eye.astype(F32)

        def c3(a):
            return a.reshape(ncb, CHUNK, a.shape[-1])

        du, dwkv, dqg, dkdec = c3(dw_ref[...]), c3(dwk_ref[...]), c3(dqg_ref[...]), c3(dkdec_ref[...])
        dpm = jnp.where(low, c3(dp_ref[...]), 0.0)
        degl = degl_ref[...].reshape(ncb, 8, 128)[:, 0:1, 0:1]
        beta, gam, kn, qn, v = m["beta"], m["gam"], m["kn"], m["qn"], m["v"]
        dm, kd, a, p = m["dm"], m["kd"], m["a"], m["p"]
        knb, qnb = _bf(kn), _bf(qn)

        eyeb = _bf(eyef)
        th, tl = _split(m["tm"])
        tts = (_bf(_bnt(eyeb, th)), _bf(_bnt(eyeb, tl)))
        xy = _bnn3(tts, jnp.concatenate([du, dwkv], axis=2))
        x, y = xy[:, :, :B_DIM], xy[:, :, B_DIM:]
        da = -jnp.where(strict, _bnt(_bf(x), _bf(m["u"])) + _bnt(_bf(y), _bf(m["wk"])), 0.0)
        dv = beta * x
        sy = jnp.sum(y * kn, axis=2, keepdims=True)
        dbeta = jnp.sum(x * v, axis=2, keepdims=True) + gam * sy + jnp.sum(da * kd, axis=2, keepdims=True)
        dgam = beta * sy + jnp.sum(dqg * qn, axis=2, keepdims=True)
        dkk = da * beta * dm
        dqk = dpm * dm
        dkkb, dqkb = _bf(dkk), _bf(dqk)
        dkn = ((beta * gam) * y + _bnn(dkkb, knb) + _bnn(_bf(_bnt(eyeb, dkkb)), knb)
               + _bnn(_bf(_bnt(eyeb, dqkb)), qnb) + dkdec * m["edec"])
        dqn = gam * dqg + _bnn(dqkb, knb)
        mm = da * a + dpm * p
        ek = jnp.sum(dkdec * m["kdec"], axis=2, keepdims=True)
        dgc = (jnp.sum(mm, axis=2, keepdims=True) - _col_of_row(jnp.sum(mm, axis=1, keepdims=True), eye)
               + dgam * gam - ek)
        dgl = jnp.sum(ek, axis=1, keepdims=True) + degl * m["egl"]
        i, _ = _chunk_masks(ncb)
        dgc = dgc + jnp.where(i[:, :, 0:1] == CHUNK - 1, dgl, 0.0)
        upper = (i <= _chunk_masks(ncb)[1]).astype(BF16)
        dg = _bnn_exact(upper, jnp.broadcast_to(dgc, (ncb, CHUNK, CHUNK)))[:, :, 0:1]

        nq = m["nq"]
        dnq = dqn * (B_DIM ** -0.5)
        dcq = m["rq"] * (dnq - nq * jnp.sum(nq * dnq, axis=2, keepdims=True))
        dck = m["rk"] * (dkn - kn * jnp.sum(kn * dkn, axis=2, keepdims=True))
        dc_ref[0] = dcq.reshape(r, B_DIM)
        dc_ref[1] = dck.reshape(r, B_DIM)
        dc_ref[2] = dv.reshape(r, B_DIM)

        dbraw = (dbeta * beta * (1.0 - beta)).reshape(r, 1)
        sgm = _sigmoid(m["sp_arg"])
        ddraw3 = dg * (-m["ea"]) * sgm
        ddraw = ddraw3.reshape(r, 1)
        lane = lax.broadcasted_iota(jnp.int32, (r, 128), 1)
        contrib = jnp.where(lane == h, dbraw, 0.0) + jnp.where(lane == B_HEADS + h, ddraw, 0.0)

        @pl.when(h == 0)
        def _():
            dbd_ref[...] = contrib

        @pl.when(h != 0)
        def _():
            dbd_ref[...] += contrib

        lane8 = lax.broadcasted_iota(jnp.int32, (8, 128), 1)
        tot_al = jnp.sum(jnp.sum(dg * m["g"], axis=1, keepdims=True), axis=0, keepdims=True).reshape(1, 1)
        tot_dtb = jnp.sum(jnp.sum(ddraw3, axis=1, keepdims=True), axis=0, keepdims=True).reshape(1, 1)
        dal_ref[...] += jnp.where(lane8 == h, tot_al, 0.0)
        ddtb_ref[...] += jnp.where(lane8 == h, tot_dtb, 0.0)

    col = lambda k: pl.BlockSpec((r, 128), lambda i, h: (i, k * B_HEADS + h))
    hcol = pl.BlockSpec((r, 128), lambda i, h: (i, h))
    small = pl.BlockSpec((1, 128), lambda i, h: (0, 0))
    acc = pl.BlockSpec((8, 128), lambda i, h: (0, 0))
    return pl.pallas_call(
        body, name="dn_post_bwd", grid=(nblk, B_HEADS),
        in_specs=[col(0), col(1), col(2), pl.BlockSpec((r, 128), lambda i, h: (i, bd_blk)), small, small,
                  pl.BlockSpec((None, r, CHUNK), lambda i, h: (h, i, 0)),
                  hcol, hcol, hcol, hcol,
                  pl.BlockSpec((None, r, CHUNK), lambda i, h: (h, i, 0)),
                  pl.BlockSpec((None, ncb * 8, 128), lambda i, h: (h, i, 0))],
        out_specs=[pl.BlockSpec((3, r, 128), lambda i, h: (0, i, h)),
                   pl.BlockSpec((r, 128), lambda i, h: (i, 0)), acc, acc],
        out_shape=[jax.ShapeDtypeStruct((3, t, B_WIDTH), F32), jax.ShapeDtypeStruct((t, 128), F32),
                   jax.ShapeDtypeStruct((8, 128), F32), jax.ShapeDtypeStruct((8, 128), F32)],
        compiler_params=_cp(("arbitrary", "arbitrary"), VMEM_LIMIT),
    )(c, c, c, proj, al_row, dtb_row, tmat, dw, dwk, dqg, dkdec, dp, degl)


def make_bias_band(rel_bias):
    tail = bias_tail(jnp.pad(rel_bias, ((0, 0), (0, 384 - N_REL))))
    far = jnp.broadcast_to(rel_bias[:, 2 * REL_CLIP][:, None, None], (A_HEADS, CHUNK, BAND - TAIL))
    band = jnp.concatenate([far, jnp.transpose(tail, (1, 0, 2))], axis=2)
    off = jnp.full((A_HEADS, CHUNK, CHUNK), -1e30, F32)
    both = jnp.stack([jnp.concatenate([band, off], axis=2), jnp.concatenate([off, band], axis=2)], axis=1)
    return both.reshape(4, 4 * CHUNK, WIN)


def bias_band_grad(dbt, dbf):
    t5 = dbt.reshape(A_HEADS, 2, CHUNK, 256)
    tail = t5[:, 0, :, :TAIL] + t5[:, 1, :, CHUNK:]
    far = dbf.reshape(A_HEADS, 2, CHUNK, 128).sum(axis=1) + jnp.pad(t5[:, 1, :, :CHUNK], ((0, 0), (0, 0), (0, CHUNK)))
    return bias_grad(jnp.transpose(tail, (1, 0, 2)), far)[:, :N_REL]


def _rms(x):
    r = lax.rsqrt(jnp.mean(x * x, axis=-1, keepdims=True) + EPS)
    return r, x * r


def _rms_bwd(dh, g, r, n):
    dn = dh * g
    return r * (dn - n * jnp.mean(dn * n, axis=-1, keepdims=True)), dh * n


def _gated_onorm(o, z, w_on):
    parts = []
    for h in range(B_HEADS):
        sl = slice(h * B_DIM, (h + 1) * B_DIM)
        r, n = _rms(o[:, sl])
        parts.append((r, n))
    r4 = [p[0] for p in parts]
    n4 = jnp.concatenate([p[1] for p in parts], axis=1)
    w4 = jnp.concatenate([w_on] * B_HEADS, axis=1)
    sz = _sigmoid(z)
    silu = z * sz
    return n4 * w4 * silu, r4, n4, w4, sz, silu


def mid_fwd(x, y_a, o_b, proj, w_on, wa, wb, w_out, tm=256):
    t = x.shape[0]
    tm = min(tm, t)

    def body(x_ref, ya_ref, ob_ref, z_ref, ga_ref, gb_ref, won_ref, wa_ref, wb_ref, wo_ref, x1_ref, mg_ref):
        yb = _gated_onorm(ob_ref[...], z_ref[...].astype(F32), won_ref[...])[0]
        ua = _nn(_bf(ya_ref[...]), wa_ref[...])
        ub = _nn(_bf(yb), wb_ref[...])
        merged = _sigmoid(ga_ref[...].astype(F32)) * ua + _sigmoid(gb_ref[...].astype(F32)) * ub
        mb = _bf(merged)
        mg_ref[...] = mb
        x1_ref[...] = x_ref[...] + _nn(mb, wo_ref[...])

    rowd = pl.BlockSpec((tm, D_MODEL), lambda i: (i, 0))
    row5 = pl.BlockSpec((tm, 512), lambda i: (i, 0))
    full = lambda a: pl.BlockSpec(a.shape, lambda i: (0,) * a.ndim)
    return pl.pallas_call(
        body, name="mid_fwd", grid=(t // tm,),
        in_specs=[rowd, row5, row5,
                  pl.BlockSpec((tm, 512), lambda i: (i, P_Z // 512)),
                  pl.BlockSpec((tm, D_MODEL), lambda i: (i, 0)),
                  pl.BlockSpec((tm, D_MODEL), lambda i: (i, 1)),
                  full(w_on), full(wa), full(wb), full(w_out)],
        out_specs=[rowd, rowd],
        out_shape=[jax.ShapeDtypeStruct((t, D_MODEL), F32), jax.ShapeDtypeStruct((t, D_MODEL), BF16)],
        compiler_params=_cp(("parallel",), VMEM_LIMIT),
    )(x, y_a, o_b, proj, proj, proj, w_on, wa, wb, w_out)


def mid_bwd(dx1, merged, y_a, o_b, proj, w_on, wa, wb, w_out, tm=256):
    t = dx1.shape[0]
    tm = min(tm, t)

    def body(dx1_ref, mg_ref, ya_ref, ob_ref, z_ref, ga_ref, gb_ref, won_ref, wa_ref, wb_ref, wo_ref,
             dya_ref, dob_ref, dz_ref, dg_ref, dwo_ref, dwa_ref, dwb_ref, dwon_ref):
        @pl.when(pl.program_id(0) == 0)
        def _():
            dwo_ref[...] = jnp.zeros_like(dwo_ref)
            dwa_ref[...] = jnp.zeros_like(dwa_ref)
            dwb_ref[...] = jnp.zeros_like(dwb_ref)
            dwon_ref[...] = jnp.zeros_like(dwon_ref)

        dx1b = _bf(dx1_ref[...])
        dmerged = _nt(dx1b, wo_ref[...])
        dwo_ref[...] += _tn(mg_ref[...], dx1b)
        o = ob_ref[...]
        z = z_ref[...].astype(F32)
        yb, r4, n4, w4, sz, silu = _gated_onorm(o, z, won_ref[...])
        yab, ybb = _bf(ya_ref[...]), _bf(yb)
        ua = _nn(yab, wa_ref[...])
        ub = _nn(ybb, wb_ref[...])
        sa, sb = _sigmoid(ga_ref[...].astype(F32)), _sigmoid(gb_ref[...].astype(F32))
        dua, dub = _bf(dmerged * sa), _bf(dmerged * sb)
        dg_ref[:, 0:D_MODEL] = _bf(dmerged * ua * sa * (1.0 - sa))
        dg_ref[:, D_MODEL:2 * D_MODEL] = _bf(dmerged * ub * sb * (1.0 - sb))
        dwa_ref[...] += _tn(yab, dua)
        dwb_ref[...] += _tn(ybb, dub)
        dya_ref[...] = _nt(dua, wa_ref[...])
        dyb = _nt(dub, wb_ref[...])
        dz_ref[...] = _bf(dyb * (n4 * w4) * (sz * (1.0 + z * (1.0 - sz))))
        dnw = dyb * silu
        dwon = jnp.zeros((1, B_DIM), F32)
        for h in range(B_HEADS):
            sl = slice(h * B_DIM, (h + 1) * B_DIM)
            dxh, dgh = _rms_bwd(dnw[:, sl], won_ref[...], r4[h], n4[:, sl])
            dob_ref[:, sl] = dxh
            dwon = dwon + jnp.sum(dgh, axis=0, keepdims=True)
        dwon_ref[...] += jnp.broadcast_to(dwon, (8, B_DIM))

    rowd = pl.BlockSpec((tm, D_MODEL), lambda i: (i, 0))
    row5 = pl.BlockSpec((tm, 512), lambda i: (i, 0))
    full = lambda a: pl.BlockSpec(a.shape, lambda i: (0,) * a.ndim)
    fixed = lambda shp: pl.BlockSpec(shp, lambda i: (0,) * len(shp))
    return pl.pallas_call(
        body, name="mid_bwd", grid=(t // tm,),
        in_specs=[rowd, rowd, row5, row5,
                  pl.BlockSpec((tm, 512), lambda i: (i, P_Z // 512)),
                  pl.BlockSpec((tm, D_MODEL), lambda i: (i, 0)),
                  pl.BlockSpec((tm, D_MODEL), lambda i: (i, 1)),
                  full(w_on), full(wa), full(wb), full(w_out)],
        out_specs=[row5, row5, row5, pl.BlockSpec((tm, 2 * D_MODEL), lambda i: (i, 0)),
                   fixed((D_MODEL, D_MODEL)), fixed((A_WIDTH, D_MODEL)), fixed((B_WIDTH, D_MODEL)),
                   fixed((8, B_DIM))],
        out_shape=[jax.ShapeDtypeStruct((t, 512), F32), jax.ShapeDtypeStruct((t, 512), F32),
                   jax.ShapeDtypeStruct((t, 512), BF16), jax.ShapeDtypeStruct((t, 2 * D_MODEL), BF16),
           jax.ShapeDtypeStruct((D_MODEL, D_MODEL), F32), jax.ShapeDtypeStruct((A_WIDTH, D_MODEL), F32),
           jax.ShapeDtypeStruct((B_WIDTH, D_MODEL), F32), jax.ShapeDtypeStruct((8, B_DIM), F32)],
        compiler_params=_cp(("arbitrary",), VMEM_LIMIT),
    )(dx1, merged, y_a, o_b, proj, proj, proj, w_on, wa, wb, w_out)


FFN_TF = 1408


def ffn_up(x1, g, w_gu, tm=512, tf=FFN_TF):
    t = x1.shape[0]
    tm = min(tm, t)
    nf = D_FF // tf

    def body(x_ref, g_ref, wg_ref, wu_ref, gate_ref, up_ref, act_ref, h_ref):
        @pl.when(pl.program_id(1) == 0)
        def _():
            r, n = _rms(x_ref[...])
            h_ref[...] = _bf(n * g_ref[...])

        hb = h_ref[...]
        gate = _nn(hb, wg_ref[...])
        up = _nn(hb, wu_ref[...])
        gate_ref[...] = _bf(gate)
        up_ref[...] = _bf(up)
        act_ref[...] = _bf(gate * _sigmoid(gate) * up)

    ff = pl.BlockSpec((tm, tf), lambda i, j: (i, j))
    return pl.pallas_call(
        body, name="ffn_up", grid=(t // tm, nf),
        in_specs=[pl.BlockSpec((tm, D_MODEL), lambda i, j: (i, 0)),
                  pl.BlockSpec((1, D_MODEL), lambda i, j: (0, 0)),
                  pl.BlockSpec((D_MODEL, tf), lambda i, j: (0, j)),
                  pl.BlockSpec((D_MODEL, tf), lambda i, j: (0, nf + j))],
        out_specs=[ff, ff, ff, pl.BlockSpec((tm, D_MODEL), lambda i, j: (i, 0))],
        out_shape=[jax.ShapeDtypeStruct((t, D_FF), BF16)] * 3 + [jax.ShapeDtypeStruct((t, D_MODEL), BF16)],
        compiler_params=_cp(("parallel", "arbitrary"), VMEM_LIMIT),
    )(x1, g, w_gu, w_gu)


def matmul_residual(a, w, res, name, tm=512, tk=FFN_TF):
    t, k = a.shape
    n = w.shape[1]
    tm = min(tm, t)

    def body(a_ref, w_ref, r_ref, o_ref):
        @pl.when(pl.program_id(1) == 0)
        def _():
            o_ref[...] = r_ref[...]

        o_ref[...] += _nn(a_ref[...], w_ref[...])

    return pl.pallas_call(
        body, name=name, grid=(t // tm, k // tk),
        in_specs=[pl.BlockSpec((tm, tk), lambda i, j: (i, j)),
                  pl.BlockSpec((tk, n), lambda i, j: (j, 0)),
                  pl.BlockSpec((tm, n), lambda i, j: (i, 0))],
        out_specs=pl.BlockSpec((tm, n), lambda i, j: (i, 0)),
        out_shape=jax.ShapeDtypeStruct((t, n), F32),
        compiler_params=_cp(("parallel", "arbitrary"), VMEM_LIMIT),
    )(a, w, res)


def ffn_act_bwd(dx2, gate, up, w_down, tm=512, tf=FFN_TF):
    t = dx2.shape[0]
    tm = min(tm, t)

    def body(dx2_ref, gate_ref, up_ref, wd_ref, dgate_ref, dup_ref, dx2b_ref):
        @pl.when(pl.program_id(1) == 0)
        def _():
            dx2b_ref[...] = _bf(dx2_ref[...])

        dact = _nt(dx2b_ref[...], wd_ref[...])
        gt, upv = gate_ref[...].astype(F32), up_ref[...].astype(F32)
        sg = _sigmoid(gt)
        dgate_ref[...] = _bf(dact * upv * (sg * (1.0 + gt * (1.0 - sg))))
        dup_ref[...] = _bf(dact * (gt * sg))

    ff = pl.BlockSpec((tm, tf), lambda i, j: (i, j))
    return pl.pallas_call(
        body, name="ffn_act_bwd", grid=(t // tm, D_FF // tf),
        in_specs=[pl.BlockSpec((tm, D_MODEL), lambda i, j: (i, 0)), ff, ff,
                  pl.BlockSpec((tf, D_MODEL), lambda i, j: (j, 0))],
        out_specs=[ff, ff],
        out_shape=[jax.ShapeDtypeStruct((t, D_FF), BF16)] * 2,
        scratch_shapes=[pltpu.VMEM((tm, D_MODEL), BF16)],
        compiler_params=_cp(("parallel", "arbitrary"), VMEM_LIMIT),
    )(dx2, gate, up, w_down)


def tail_fwd_bwd(x2, p, target, g_ple, g_final, w_pg, w_pp, tm=256):
    t = x2.shape[0]
    tm = min(tm, t)

    def body(x_ref, p_ref, t_ref, gp_ref, gf_ref, wpg_ref, wpp_ref,
             dx_ref, dwpg_ref, dwpp_ref, dgp_ref, dgf_ref, loss_ref):
        @pl.when(pl.program_id(0) == 0)
        def _():
            dwpg_ref[...] = jnp.zeros_like(dwpg_ref)
            dwpp_ref[...] = jnp.zeros_like(dwpp_ref)
            dgp_ref[...] = jnp.zeros_like(dgp_ref)
            dgf_ref[...] = jnp.zeros_like(dgf_ref)
            loss_ref[...] = jnp.zeros_like(loss_ref)

        x2v = x_ref[...]
        gp, gf = gp_ref[...], gf_ref[...]
        r3, n3 = _rms(x2v)
        h3b = _bf(n3 * gp)
        pb = _bf(p_ref[...])
        pg = _sigmoid(_nn(h3b, wpg_ref[...]))
        pp = _nn(pb, wpp_ref[...])
        x3 = x2v + pg * pp
        r4, n4 = _rms(x3)
        err = n4 * gf - t_ref[...]
        part = 0.5 * jnp.sum(jnp.sum(err * err, axis=1, keepdims=True), axis=0, keepdims=True) / D_MODEL
        loss_ref[...] += jnp.broadcast_to(part, (8, 128))
        dy = err * (1.0 / D_MODEL)
        dx3, dgf = _rms_bwd(dy, gf, r4, n4)
        dgf_ref[...] += jnp.broadcast_to(jnp.sum(dgf, axis=0, keepdims=True), (8, D_MODEL))
        dzp = _bf(dx3 * pp * pg * (1.0 - pg))
        dpp = _bf(dx3 * pg)
        dwpg_ref[...] += _tn(h3b, dzp)
        dwpp_ref[...] += _tn(pb, dpp)
        dh3 = _nt(dzp, wpg_ref[...])
        dx, dgp = _rms_bwd(dh3, gp, r3, n3)
        dgp_ref[...] += jnp.broadcast_to(jnp.sum(dgp, axis=0, keepdims=True), (8, D_MODEL))
        dx_ref[...] = dx3 + dx

    rowd = pl.BlockSpec((tm, D_MODEL), lambda i: (i, 0))
    fixed = lambda shp: pl.BlockSpec(shp, lambda i: (0,) * len(shp))
    return pl.pallas_call(
        body, name="tail_fwd_bwd", grid=(t // tm,),
        in_specs=[rowd, pl.BlockSpec((tm, PLE_DIM), lambda i: (i, 0)), rowd,
                  fixed((1, D_MODEL)), fixed((1, D_MODEL)), fixed((D_MODEL, D_MODEL)), fixed((PLE_DIM, D_MODEL))],
        out_specs=[rowd, fixed((D_MODEL, D_MODEL)), fixed((PLE_DIM, D_MODEL)),
                   fixed((8, D_MODEL)), fixed((8, D_MODEL)), fixed((8, 128))],
        out_shape=[jax.ShapeDtypeStruct((t, D_MODEL), F32), jax.ShapeDtypeStruct((D_MODEL, D_MODEL), F32),
                   jax.ShapeDtypeStruct((PLE_DIM, D_MODEL), F32), jax.ShapeDtypeStruct((8, D_MODEL), F32),
                   jax.ShapeDtypeStruct((8, D_MODEL), F32), jax.ShapeDtypeStruct((8, 128), F32)],
        compiler_params=_cp(("arbitrary",), VMEM_LIMIT),
    )(x2, p, target, g_ple, g_final, w_pg, w_pp)


def in_proj_bwd(pieces, weights, x, dx1, g, name="in_proj_bwd", tm=256):
    t = x.shape[0]
    tm = min(tm, t)
    k = len(pieces)
    assert all(c0 % wd == 0 and w0 % wd == 0 for (_, c0, wd), (_, w0) in zip(pieces, weights))

    def body(*refs):
        p_refs, w_refs = refs[:k], refs[k:2 * k]
        x_ref, dx1_ref, g_ref, dx_ref, dg_ref = refs[2 * k:]

        @pl.when(pl.program_id(0) == 0)
        def _():
            dg_ref[...] = jnp.zeros_like(dg_ref)

        dh = _nt(_bf(p_refs[0][...]), w_refs[0][...])
        for pr, wr in zip(p_refs[1:], w_refs[1:]):
            dh = dh + _nt(_bf(pr[...]), wr[...])
        r, n = _rms(x_ref[...])
        dx, dgc = _rms_bwd(dh, g_ref[...], r, n)
        dx_ref[...] = dx1_ref[...] + dx
        dg_ref[...] += jnp.broadcast_to(jnp.sum(dgc, axis=0, keepdims=True), (8, D_MODEL))

    rowd = pl.BlockSpec((tm, D_MODEL), lambda i: (i, 0))
    return pl.pallas_call(
        body, name=name, grid=(t // tm,),
        in_specs=[pl.BlockSpec((tm, wd), functools.partial(lambda i, cb: (i, cb), cb=c0 // wd))
                  for _, c0, wd in pieces]
        + [pl.BlockSpec((w.shape[0], wd), functools.partial(lambda i, cb: (0, cb), cb=w0 // wd))
           for (w, w0), (_, _, wd) in zip(weights, pieces)]
        + [rowd, rowd, pl.BlockSpec((1, D_MODEL), lambda i: (0, 0))],
        out_specs=[rowd, pl.BlockSpec((8, D_MODEL), lambda i: (0, 0))],
        out_shape=[jax.ShapeDtypeStruct((t, D_MODEL), F32), jax.ShapeDtypeStruct((8, D_MODEL), F32)],
        compiler_params=_cp(("arbitrary",), VMEM_LIMIT),
    )(*[a for a, _, _ in pieces], *[w for w, _ in weights], x, dx1, g)


def adamw(w, g, m, v, name, rows_cap=256, dep=None):
    lead = w.shape[:-2]
    r, c = w.shape[-2:]
    tr = r
    for cand in range(8, min(r, rows_cap) + 1, 8):
        if r % cand == 0:
            tr = cand

    def body(w_ref, g_ref, m_ref, v_ref, *rest):
        d_ref, mo_ref, vo_ref = rest[-3:]
        gv = g_ref[...]
        mn = ADAM_B1 * m_ref[...] + (1.0 - ADAM_B1) * gv
        vn = ADAM_B2 * v_ref[...] + (1.0 - ADAM_B2) * (gv * gv)
        m_hat = mn / (1.0 - ADAM_B1 ** ADAM_STEP)
        v_hat = vn / (1.0 - ADAM_B2 ** ADAM_STEP)
        d_ref[...] = -ADAM_LR * (m_hat / (jnp.sqrt(v_hat) + ADAM_EPS) + ADAM_WD * w_ref[...])
        mo_ref[...] = mn
        vo_ref[...] = vn

    spec = pl.BlockSpec((None,) * len(lead) + (tr, c), lambda i: (0,) * len(lead) + (i, 0))
    extra = [] if dep is None else [dep]
    return pl.pallas_call(
        body, name=name, grid=(r // tr,),
        in_specs=[spec] * 4 + [pl.BlockSpec((8, 128), lambda i: (0, 0))] * len(extra), out_specs=[spec] * 3,
        out_shape=[jax.ShapeDtypeStruct(w.shape, F32)] * 3,
        compiler_params=_cp(("parallel",), VMEM_LIMIT),
    )(w, g.reshape(w.shape), m, v, *extra)


class Standalone:
    def __init__(self, later_weights):
        self.later_weights = later_weights

    def begin(self, *a):
        return 0.0

    forward = exchange = join = begin

    def finish(self, after):
        return self.later_weights


def local_step(x3d, p3d, target3d, g4, small, later, early):
    b, s, _ = x3d.shape
    t = b * s
    x = x3d.reshape(t, D_MODEL)
    p = p3d.reshape(t, PLE_DIM)
    target = target3d.reshape(t, D_MODEL)
    cut = SPLIT_Z - 2 * (D_IN // N_CHIPS)
    w_inp = jnp.concatenate([g4[2][:, cut + 8:], g4[3], g4[0], g4[1], g4[2][:, :cut], g4[2][:, cut:cut + 8],
                             jnp.zeros((D_MODEL, 120), BF16)], axis=1)
    al_row = jnp.pad(small["a_log"].reshape(1, B_HEADS), ((0, 0), (0, 128 - B_HEADS)))
    dtb_row = jnp.pad(small["dt_bias"].reshape(1, B_HEADS), ((0, 0), (0, 128 - B_HEADS)))
    conv_w8 = jnp.pad(small["conv_w"].reshape(CONV_K, CONV_CH), ((0, 8 - CONV_K), (0, 0)))
    w_on = small["w_onorm"].reshape(1, B_DIM)
    g_mix, g_ffn = small["g_mix"].reshape(1, D_MODEL), small["g_ffn"].reshape(1, D_MODEL)
    g_ple, g_final = small["g_ple"].reshape(1, D_MODEL), small["g_final"].reshape(1, D_MODEL)
    bias_band = make_bias_band(small["rel_bias"].reshape(A_HEADS, N_REL))

    tok = later.begin()
    proj, h1, bd32 = rms_matmul(x, g_mix + tok, w_inp, "in_proj", tm=1024)
    y_a, lse = attn_fwd(proj, bias_band, b, s)
    tok = later.forward(lse)
    c = conv_fwd(proj, conv_w8 + tok, b, s)
    u, wk, qg, kdec, pm, egl, tmat = dn_prep(c, bd32, al_row, dtb_row, b, s)
    o_b, states = dn_scan_fwd(u, wk, qg, kdec, pm, egl, b, s)
    wts = later.finish(o_b)
    x1, merged = mid_fwd(x, y_a, o_b, proj, w_on, wts["w_branch_a"], wts["w_branch_b"], wts["w_out"])
    gate, up, act, h2 = ffn_up(x1, g_ffn, wts["w_gate_up"])
    x2 = matmul_residual(act, wts["w_down"], x1, "ffn_down")

    dx2, dw_pg, dw_pp, dg_ple, dg_final, loss = tail_fwd_bwd(
        x2, p, target, g_ple, g_final, wts["w_ple_gate"], wts["w_ple_proj"])
    dgate, dup = ffn_act_bwd(dx2, gate, up, wts["w_down"])
    w_gu = wts["w_gate_up"]
    dx1, dg_ffn = in_proj_bwd([(dgate, 0, D_FF), (dup, 0, D_FF)], [(w_gu, 0), (w_gu, D_FF)], x1, dx2, g_ffn,
                              name="ffn_in_bwd")
    dw_down = matmul_tn(act, dx2, "dw_down")
    dw_gu = matmul_tn(h2, dgate, "dw_gate", width=2 * D_FF)
    dw_gu = matmul_tn(h2, dup, "dw_up", into=dw_gu, col0=D_FF)
    dy_a, do_b, dz, dgates, dw_out, dwa, dwb, dw_on = mid_bwd(
        dx1, merged, y_a, o_b, proj, w_on, wts["w_branch_a"], wts["w_branch_b"], wts["w_out"])
    tok = early.begin(dict(w_branch_a=dwa, w_branch_b=dwb, w_out=dw_out, w_gate_up=dw_gu, w_down=dw_down,
                           w_ple_gate=dw_pg, w_ple_proj=dw_pp))
    ddw, ddwk, ddqg, ddkdec, ddp, ddegl = dn_scan_bwd(u, wk, qg, kdec, pm, egl + tok, states, do_b, b, s)
    tok = early.exchange(ddegl)
    dc3, dbd, dal, ddtb = dn_post_bwd(c, bd32, al_row + tok, dtb_row, tmat, ddw, ddwk, ddqg, ddkdec, ddp, ddegl, b, s)
    dconv, dconv_w = conv_bwd(proj, conv_w8, dc3, b, s)
    dqa, dka, dva, dbt, dbf = attn_bwd(proj, bias_band, y_a, lse, dy_a, b, s)
    tok = early.join(dqa)
    d_rel = bias_band_grad(dbt, dbf)

    pieces = [dgates, dqa, dka, dva, dconv, dz, dbd]
    bounds = [0, 2048, 2560, 3072, 3584, 5120, 5632, 5760]
    windows = [(dgates, 0, 2048), (dqa, 0, 512), (dka, 0, 512), (dva, 0, 512), (dconv, 0, 512), (dconv, 512, 512),
               (dconv, 1024, 512), (dz, 0, 512), (dbd, 0, 128)]
    w_cols = [0, P_QA, P_KA, P_VA, P_CONV, P_CONV + 512, P_CONV + 1024, P_Z, P_BD]
    dx, dg_mix = in_proj_bwd(windows, [(w_inp, c0) for c0 in w_cols], x, dx1, g_mix + tok)
    dwp = None
    for k, pc in enumerate(pieces):
        dwp = matmul_tn(h1, pc, "dw_in_%d" % k, into=dwp, col0=bounds[k], width=P_WIDTH)
    reduced_early = early.finish(dwp)
    dw_in = jnp.concatenate([dwp[:, P_QA:P_BD + 8], dwp[:, :P_QA]], axis=1)

    grads = dict(w_in=dw_in, w_branch_a=dwa, w_branch_b=dwb, w_out=dw_out, w_gate_up=dw_gu, w_down=dw_down,
                 w_ple_gate=dw_pg, w_ple_proj=dw_pp)
    small_grads = dict(g_mix=dg_mix[0], g_ffn=dg_ffn[0], g_ple=dg_ple[0], g_final=dg_final[0],
                       conv_w=dconv_w[:CONV_K].reshape(-1), rel_bias=d_rel.reshape(-1), w_onorm=dw_on[0],
                       a_log=dal[0, :B_HEADS], dt_bias=ddtb[0, :B_HEADS], loss=loss[0, :1])
    return dx.reshape(b, s, D_MODEL), grads, small_grads, reduced_early


BIG = (("w_in", (D_MODEL, D_IN), 1), ("w_branch_a", (A_WIDTH, D_MODEL), 1), ("w_branch_b", (B_WIDTH, D_MODEL), 1),
       ("w_out", (D_MODEL, D_MODEL), 0), ("w_gate_up", (D_MODEL, 2 * D_FF), 1), ("w_down", (D_FF, D_MODEL), 0),
       ("w_ple_gate", (D_MODEL, D_MODEL), 0), ("w_ple_proj", (PLE_DIM, D_MODEL), 1))
N_CHIPS = 4
FIRST_WEIGHTS = ("w_in",)
LATER_WEIGHTS = ("w_branch_a", "w_branch_b", "w_out", "w_gate_up", "w_down", "w_ple_gate", "w_ple_proj")
LATE_GRADS = ("w_in",)
EARLY_GRADS = ("w_branch_a", "w_branch_b", "w_out", "w_gate_up", "w_down", "w_ple_gate", "w_ple_proj")


def _items(names):
    return [it for it in BIG if it[0] in names]


def _shard_shape(shape, axis):
    return (shape[0] // N_CHIPS, shape[1]) if axis == 0 else (shape[0], shape[1] // N_CHIPS)


def _pack_rows_of(names):
    return -(-sum(sh[0] * sh[1] for _, sh, _ in _items(names)) // (N_CHIPS * 128 * 512)) * 512


def _pack_rows(parts, total):
    used = sum(a.shape[-2] for a in parts)
    pad = jnp.zeros(parts[0].shape[:-2] + (total - used, 128), parts[0].dtype)
    return jnp.concatenate(parts + [pad], axis=-2)


def pack_grads(grads, names):
    parts = []
    for n, shape, axis in _items(names):
        rs, cs = _shard_shape(shape, axis)
        g = grads[n].astype(BF16)
        seg = g.reshape(N_CHIPS, rs, cs) if axis == 0 else jnp.transpose(g.reshape(rs, N_CHIPS, cs), (1, 0, 2))
        parts.append(seg.reshape(N_CHIPS, -1, 128))
    return _pack_rows(parts, _pack_rows_of(names))


def unpack_shard(flat, names):
    out, r0 = {}, 0
    for n, shape, axis in _items(names):
        rs, cs = _shard_shape(shape, axis)
        nr = rs * cs // 128
        out[n] = flat[r0:r0 + nr].reshape(rs, cs)
        r0 += nr
    return out


def _place():
    return lax.axis_index("x"), lax.axis_index("y"), lax.axis_index("c")


ANY = pl.BlockSpec(memory_space=pl.ANY)


def _gathered_shape(item):
    n, shape, _ = item
    return (N_CHIPS,) + _shard_shape(shape, 1) if n == "w_in" else shape


def _gather_block(o_ref, item, cx, cy, hf):
    n, shape, axis = item
    rs, cs = _shard_shape(shape, axis)
    hr = rs // 2
    ci = 2 * cx + cy
    if n == "w_in":
        return o_ref.at[ci, pl.ds(pl.multiple_of(hf * hr, 16), hr), :]
    if axis == 0:
        return o_ref.at[pl.ds(pl.multiple_of(ci * rs + hf * hr, 16), hr), :]
    return o_ref.at[pl.ds(pl.multiple_of(hf * hr, 16), hr), pl.ds(pl.multiple_of(ci * cs, 128), cs)]


def _own_half(w_ref, item, c):
    hr = _shard_shape(item[1], item[2])[0] // 2
    return w_ref.at[pl.ds(pl.multiple_of(c * hr, 16), hr), :]


def _gather_slot(o_ref, item, cx, cy):
    n, shape, axis = item
    rs, cs = _shard_shape(shape, axis)
    ci = 2 * cx + cy
    if n == "w_in":
        return o_ref.at[ci]
    if axis == 0:
        return o_ref.at[pl.ds(pl.multiple_of(ci * rs, 16), rs), :]
    return o_ref.at[:, pl.ds(pl.multiple_of(ci * cs, 128), cs)]


def _other_chips(x, y):
    return [(1 - x, y), (x, 1 - y), (1 - x, 1 - y)]


def allgather_weights(shards, names, chip):
    items = _items(names)
    nw = len(items)

    def body(*refs):
        w_refs, o_refs = refs[:nw], refs[nw:2 * nw]
        send_sems, recv_sems = refs[2 * nw:]
        x, y, c = _place()
        sibling = (x, y, 1 - c)
        chips = _other_chips(x, y)

        def copy(k, src, dst, to):
            return pltpu.make_async_remote_copy(src_ref=src, dst_ref=dst, send_sem=send_sems.at[k],
                                                recv_sem=recv_sems.at[k], device_id=to, device_id_type=MESH)

        def blk(i, cx, cy, hf):
            return _gather_block(o_refs[i], items[i], cx, cy, hf)

        def my_half(i):
            return _own_half(w_refs[i], items[i], c)

        def own(i):
            return _gather_slot(o_refs[i], items[i], x, y)

        first = [copy(7 * i + j, my_half(i), blk(i, x, y, c), (*chip_, c))
                 for i in range(nw) for j, chip_ in enumerate(chips)]
        first += [copy(7 * i + 6, w_refs[i], own(i), sibling) for i in range(nw)]
        for cp in first:
            cp.start()
        passed = []
        for i in range(nw):
            for j, chip_ in enumerate(chips):
                copy(7 * i + j, my_half(i), blk(i, *chip_, c), (*chip_, c)).wait_recv()
                fwd = copy(7 * i + 3 + j, blk(i, *chip_, c), blk(i, *chip_, c), sibling)
                fwd.start()
                passed.append(fwd)
        for i in range(nw):
            for j, chip_ in enumerate(chips):
                copy(7 * i + 3 + j, my_half(i), blk(i, *chip_, 1 - c), sibling).wait_recv()
            copy(7 * i + 6, w_refs[i], own(i), sibling).wait_recv()
        for cp in first + passed:
            cp.wait_send()

    outs = pl.pallas_call(
        body, name="allgather_weights",
        in_specs=[ANY] * nw, out_specs=[ANY] * nw,
        out_shape=[jax.ShapeDtypeStruct(_gathered_shape(it), BF16) for it in items],
        scratch_shapes=[pltpu.SemaphoreType.DMA((7 * nw,)), pltpu.SemaphoreType.DMA((7 * nw,))],
    )(*[shards[it[0]] for it in items])
    return {it[0]: o for it, o in zip(items, outs)}


HBM_SPEC = pl.BlockSpec(memory_space=pltpu.HBM)
SEM_SPEC = pl.BlockSpec(memory_space=pltpu.SEMAPHORE)
EFFECT = pltpu.SideEffectType.DATAFLOW_SIDE_EFFECTING


def _in_hbm(a):
    return pltpu.with_memory_space_constraint(a, pltpu.HBM)


def copies_start(name, bufs, ncopies, plan):
    nb = len(bufs)

    def body(*refs):
        in_refs, send_sems, recv_sems, token = refs[:nb], refs[nb], refs[nb + 1], refs[-1]
        for k, (src, dst, to) in enumerate(plan(in_refs)):
            pltpu.make_async_remote_copy(src_ref=src, dst_ref=dst, send_sem=send_sems.at[k],
                                         recv_sem=recv_sems.at[k], device_id=to, device_id_type=MESH).start()
        token[...] = jnp.zeros_like(token)

    outs = pl.pallas_call(
        body, name=name,
        in_specs=[HBM_SPEC] * nb,
        out_specs=(SEM_SPEC, SEM_SPEC, *[HBM_SPEC] * nb, pl.BlockSpec(memory_space=pltpu.VMEM)),
        out_shape=(pltpu.SemaphoreType.DMA((ncopies,)), pltpu.SemaphoreType.DMA((ncopies,)),
                   *[pltpu.HBM(b.shape, b.dtype) for b in bufs], jax.ShapeDtypeStruct((8, 128), F32)),
        input_output_aliases={i: 2 + i for i in range(nb)},
        compiler_params=pltpu.CompilerParams(has_side_effects=EFFECT),
    )(*[_in_hbm(b) for b in bufs])
    return outs[0], outs[1], list(outs[2:2 + nb]), outs[-1][0, 0]


def copies_wait(name, send_sems, recv_sems, bufs, after, plan):
    nb = len(bufs)

    def body(*refs):
        in_refs, s_sems, r_sems = refs[:nb], refs[nb], refs[nb + 1]
        for k, (src, dst, to) in enumerate(plan(in_refs)):
            cp = pltpu.make_async_remote_copy(src_ref=src, dst_ref=dst, send_sem=s_sems.at[k],
                                              recv_sem=r_sems.at[k], device_id=to, device_id_type=MESH)
            cp.wait_send()
            cp.wait_recv()

    return list(pl.pallas_call(
        body, name=name,
        in_specs=[HBM_SPEC] * nb + [SEM_SPEC, SEM_SPEC, ANY],
        out_specs=tuple([HBM_SPEC] * nb),
        out_shape=tuple(pltpu.HBM(b.shape, b.dtype) for b in bufs),
        input_output_aliases={i: i for i in range(nb)},
        compiler_params=pltpu.CompilerParams(has_side_effects=EFFECT),
    )(*bufs, send_sems, recv_sems, after))


def _landing(shape, dtype):
    return _in_hbm(lax.empty(shape, dtype))


class LaterWeights:
    def __init__(self, shards, chip):
        self.items = _items(LATER_WEIGHTS)
        self.shards, self.chip = shards, chip
        self.nw = len(self.items)

    def _ici_plan(self, refs):
        x, y, c = _place()
        w_refs, o_refs = refs[:self.nw], refs[self.nw:]
        plan = [(_own_half(w_refs[i], it, c), _gather_block(o_refs[i], it, x, y, c), (*chip_, c))
                for i, it in enumerate(self.items) for chip_ in _other_chips(x, y)]
        return plan + [(w_refs[i], _gather_slot(o_refs[i], it, x, y), (x, y, 1 - c))
                       for i, it in enumerate(self.items)]

    def _d2d_plan(self, refs):
        x, y, c = _place()
        return [(_gather_block(refs[i], it, *chip_, c), _gather_block(refs[i], it, *chip_, c), (x, y, 1 - c))
                for i, it in enumerate(self.items) for chip_ in _other_chips(x, y)]

    def _d2d_wait_plan(self, refs):
        x, y, c = _place()
        return [(_gather_block(refs[i], it, *chip_, c), _gather_block(refs[i], it, *chip_, 1 - c), (x, y, 1 - c))
                for i, it in enumerate(self.items) for chip_ in _other_chips(x, y)]

    def _ici_wait_plan(self, refs):
        x, y, c = _place()
        w_refs, o_refs = refs[:self.nw], refs[self.nw:]
        plan = [(_own_half(w_refs[i], it, c), _gather_block(o_refs[i], it, *chip_, c), (*chip_, c))
                for i, it in enumerate(self.items) for chip_ in _other_chips(x, y)]
        return plan + [(w_refs[i], _gather_slot(o_refs[i], it, x, y), (x, y, 1 - c))
                       for i, it in enumerate(self.items)]

    def begin(self):
        srcs = [self.shards[it[0]] for it in self.items]
        lands = [_landing(_gathered_shape(it), BF16) for it in self.items]
        self.s1, self.r1, self.b1, tok = copies_start("gather_ici_start", srcs + lands, 4 * self.nw, self._ici_plan)
        return tok

    def forward(self, after):
        b1 = copies_wait("gather_ici_wait", self.s1, self.r1, self.b1, after, self._ici_wait_plan)
        self.s2, self.r2, self.b2, tok = copies_start("gather_d2d_start", b1[self.nw:], 3 * self.nw, self._d2d_plan)
        return tok

    def finish(self, after):
        outs = copies_wait("gather_d2d_wait", self.s2, self.r2, self.b2, after, self._d2d_wait_plan)
        return {it[0]: o for it, o in zip(self.items, outs)}


def small_allreduce(v, name):
    r = v.shape[0]

    def body(v_ref, o_ref, buf, send_sems, recv_sems):
        x, y, c = _place()
        me = 4 * x + 2 * y + c
        buf[me] = v_ref[...]
        flips = [(fx, fy, fc) for fx in (0, 1) for fy in (0, 1) for fc in (0, 1)][1:]
        peers = [((1 - x) if fx else x, (1 - y) if fy else y, (1 - c) if fc else c) for fx, fy, fc in flips]

        def copy(k, slot, to):
            return pltpu.make_async_remote_copy(src_ref=v_ref, dst_ref=buf.at[slot], send_sem=send_sems.at[k],
                                                recv_sem=recv_sems.at[k], device_id=to, device_id_type=MESH)

        sends = [copy(k, me, peer) for k, peer in enumerate(peers)]
        for cp in sends:
            cp.start()
        for k, (px, py, pc) in enumerate(peers):
            copy(k, 4 * px + 2 * py + pc, (px, py, pc)).wait_recv()
        for cp in sends:
            cp.wait_send()
        acc = buf[0]
        for d in range(1, 8):
            acc = acc + buf[d]
        o_ref[...] = acc

    return pl.pallas_call(
        body, name=name,
        in_specs=[pl.BlockSpec(memory_space=pltpu.VMEM)], out_specs=pl.BlockSpec(memory_space=pltpu.VMEM),
        out_shape=jax.ShapeDtypeStruct((r, 128), F32),
        scratch_shapes=[pltpu.VMEM((8, r, 128), F32), pltpu.SemaphoreType.DMA((7,)), pltpu.SemaphoreType.DMA((7,))],
    )(v)


def swap_halves(g):
    half = g.shape[1] // 2

    def body(g_ref, o_ref, send_sem, recv_sem):
        x, y, c = _place()
        cp = pltpu.make_async_remote_copy(
            src_ref=g_ref.at[:, pl.ds((1 - c) * half, half), :], dst_ref=o_ref, send_sem=send_sem,
            recv_sem=recv_sem, device_id=(x, y, 1 - c), device_id_type=MESH)
        cp.start()
        cp.wait()

    return pl.pallas_call(
        body, name="swap_halves", in_specs=[ANY], out_specs=ANY,
        out_shape=jax.ShapeDtypeStruct((N_CHIPS, half, 128), g.dtype),
        scratch_shapes=[pltpu.SemaphoreType.DMA, pltpu.SemaphoreType.DMA],
    )(g)


def add_halves(g, other, place):
    half = other.shape[1]
    tr = _tile_rows(half)
    nblk = half // tr

    def body(pref, g0, g1, g2, g3, o0, o1, o2, o3, pf_ref, pb_ref):
        f = lambda r: r[...].astype(F32)
        pf_ref[...] = f(g0) + f(o0)
        pb_ref[0] = _bf(f(g1) + f(o1))
        pb_ref[1] = _bf(f(g2) + f(o2))
        pb_ref[2] = _bf(f(g3) + f(o3))

    gspec = lambda k: pl.BlockSpec((None, tr, 128), lambda i, pr: ((pr[0] + k) % N_CHIPS, pr[1] * nblk + i, 0))
    ospec = lambda k: pl.BlockSpec((None, tr, 128), lambda i, pr: ((pr[0] + k) % N_CHIPS, i, 0))
    return pl.pallas_call(
        body, name="add_halves",
        grid_spec=pltpu.PrefetchScalarGridSpec(
            num_scalar_prefetch=1, grid=(nblk,),
            in_specs=[gspec(0), gspec(1), gspec(2), gspec(3), ospec(0), ospec(1), ospec(2), ospec(3)],
            out_specs=[pl.BlockSpec((tr, 128), lambda i, pr: (i, 0)),
                       pl.BlockSpec((3, tr, 128), lambda i, pr: (0, i, 0))]),
        out_shape=[jax.ShapeDtypeStruct((half, 128), F32), jax.ShapeDtypeStruct((3, half, 128), BF16)],
        compiler_params=_cp(("parallel",), VMEM_LIMIT),
    )(place, g, g, g, g, other, other, other, other)


def _tile_rows(n, cap=2048):
    best = 16
    for t in range(16, cap + 1, 16):
        if n % t == 0:
            best = t
    assert n % best == 0
    return best


def exchange_partials(pb):
    def body(p_ref, o_ref, send_sems, recv_sems):
        x, y, c = _place()
        me = 2 * x + y
        cps = []
        for k in range(1, N_CHIPS):
            to = (me + k) % N_CHIPS
            cps.append(pltpu.make_async_remote_copy(
                src_ref=p_ref.at[k - 1], dst_ref=o_ref.at[k - 1], send_sem=send_sems.at[k - 1],
                recv_sem=recv_sems.at[k - 1], device_id=(to // 2, to % 2, c), device_id_type=MESH))
        for cp in cps:
            cp.start()
        for cp in cps:
            cp.wait()

    return pl.pallas_call(
        body, name="exchange_partials", in_specs=[ANY], out_specs=ANY,
        out_shape=jax.ShapeDtypeStruct(pb.shape, pb.dtype),
        scratch_shapes=[pltpu.SemaphoreType.DMA((3,)), pltpu.SemaphoreType.DMA((3,))],
    )(pb)


def add_partials(pf, got, place):
    half = pf.shape[0]
    tr = _tile_rows(half)

    def body(pref, pf_ref, got_ref, o_ref):
        o_ref[...] = ((pf_ref[...] + got_ref[0].astype(F32)) + got_ref[1].astype(F32)) + got_ref[2].astype(F32)

    return pl.pallas_call(
        body, name="add_partials",
        grid_spec=pltpu.PrefetchScalarGridSpec(
            num_scalar_prefetch=1, grid=(half // tr,),
            in_specs=[pl.BlockSpec((tr, 128), lambda i, pr: (i, 0)),
                      pl.BlockSpec((3, tr, 128), lambda i, pr: (0, i, 0))],
            out_specs=pl.BlockSpec((None, tr, 128), lambda i, pr: (pr[1], i, 0))),
        out_shape=jax.ShapeDtypeStruct((2, half, 128), F32),
        compiler_params=_cp(("parallel",), VMEM_LIMIT),
    )(place, pf, got)


def join_halves(both):
    def body(r_ref, o_ref, send_sem, recv_sem):
        x, y, c = _place()
        cp = pltpu.make_async_remote_copy(src_ref=r_ref.at[c], dst_ref=o_ref.at[c], send_sem=send_sem,
                                          recv_sem=recv_sem, device_id=(x, y, 1 - c), device_id_type=MESH)
        cp.start()
        pltpu.make_async_remote_copy(src_ref=r_ref.at[c], dst_ref=o_ref.at[1 - c], send_sem=send_sem,
                                     recv_sem=recv_sem, device_id=(x, y, 1 - c), device_id_type=MESH).wait_recv()
        cp.wait_send()

    return pl.pallas_call(
        body, name="join_halves", in_specs=[ANY], out_specs=ANY,
        out_shape=jax.ShapeDtypeStruct(both.shape, F32),
        scratch_shapes=[pltpu.SemaphoreType.DMA, pltpu.SemaphoreType.DMA],
        input_output_aliases={0: 0},
    )(both)


def reduce_scatter_grads(gpack, place):
    other = swap_halves(gpack)
    pf, pb = add_halves(gpack, other, place)
    got = exchange_partials(pb)
    return join_halves(add_partials(pf, got, place)).reshape(gpack.shape[1], 128)


class EarlyGrads:
    def __init__(self, place, names=EARLY_GRADS, tag="grads"):
        self.place, self.names, self.tag = place, names, tag

    @staticmethod
    def _swap_plan(refs):
        x, y, c = _place()
        g_ref, o_ref = refs
        half = o_ref.shape[1]
        return [(g_ref.at[:, pl.ds(pl.multiple_of((1 - c) * half, 16), half), :], o_ref, (x, y, 1 - c))]

    @staticmethod
    def _exchange_plan(refs):
        x, y, c = _place()
        p_ref, o_ref = refs
        me = 2 * x + y
        return [(p_ref.at[k - 1], o_ref.at[k - 1], (((me + k) % N_CHIPS) // 2, ((me + k) % N_CHIPS) % 2, c))
                for k in range(1, N_CHIPS)]

    @staticmethod
    def _join_plan(refs):
        x, y, c = _place()
        return [(refs[0].at[c], refs[0].at[c], (x, y, 1 - c))]

    @staticmethod
    def _join_wait_plan(refs):
        x, y, c = _place()
        return [(refs[0].at[c], refs[0].at[1 - c], (x, y, 1 - c))]

    def begin(self, grads):
        g = pack_grads(grads, self.names)
        land = _landing((N_CHIPS, g.shape[1] // 2, 128), BF16)
        self.s1, self.r1, self.b1, tok = copies_start(self.tag + "_swap_start", [g, land], 1, self._swap_plan)
        return tok

    def exchange(self, after):
        g, other = copies_wait(self.tag + "_swap_wait", self.s1, self.r1, self.b1, after, self._swap_plan)
        self.pf, pb = add_halves(g, other, self.place)
        land = _landing(pb.shape, BF16)
        self.s2, self.r2, self.b2, tok = copies_start(self.tag + "_exchange_start", [pb, land], 3,
                                                      self._exchange_plan)
        return tok

    def join(self, after):
        _, got = copies_wait(self.tag + "_exchange_wait", self.s2, self.r2, self.b2, after, self._exchange_plan)
        both = add_partials(self.pf, got, self.place)
        self.s3, self.r3, self.b3, tok = copies_start(self.tag + "_join_start", [both], 1, self._join_plan)
        return tok

    def finish(self, after):
        (both,) = copies_wait(self.tag + "_join_wait", self.s3, self.r3, self.b3, after, self._join_wait_plan)
        return both.reshape(-1, 128)


SMALL = (("g_mix", D_MODEL), ("g_ffn", D_MODEL), ("g_ple", D_MODEL), ("g_final", D_MODEL),
         ("conv_w", CONV_K * CONV_CH), ("rel_bias", A_HEADS * N_REL), ("w_onorm", B_DIM),
         ("a_log", B_HEADS), ("dt_bias", B_HEADS), ("loss", 1))


def _pad128(v):
    v = v.reshape(-1)
    return jnp.pad(v, (0, -v.shape[0] % 128))


def pack_small(d, names, rows):
    flat = jnp.concatenate([_pad128(d[n]) for n in names]).reshape(-1, 128)
    return jnp.pad(flat, ((0, rows - flat.shape[0]), (0, 0)))


def unpack_small(flat, names_sizes):
    out, r0 = {}, 0
    v = flat.reshape(-1)
    for n, size in names_sizes:
        out[n] = v[r0:r0 + size]
        r0 += -(-size // 128) * 128
    return out


def kernel(x, p, g_mix, w_in, conv_w, a_log, dt_bias, rel_bias, w_onorm, w_branch_a, w_branch_b, w_out, g_ffn, w_gate_up, w_down, g_ple, w_ple_gate, w_ple_proj, g_final, loss_target, m_g_mix, m_w_in, m_conv_w, m_a_log, m_dt_bias, m_rel_bias, m_w_onorm, m_w_branch_a, m_w_branch_b, m_w_out, m_g_ffn, m_w_gate_up, m_w_down, m_g_ple, m_w_ple_gate, m_w_ple_proj, m_g_final, v_g_mix, v_w_in, v_conv_w, v_a_log, v_dt_bias, v_rel_bias, v_w_onorm, v_w_branch_a, v_w_branch_b, v_w_out, v_g_ffn, v_w_gate_up, v_w_down, v_g_ple, v_w_ple_gate, v_w_ple_proj, v_g_final):
    names = ["g_mix", "w_in", "conv_w", "a_log", "dt_bias", "rel_bias", "w_onorm", "w_branch_a", "w_branch_b",
             "w_out", "g_ffn", "w_gate_up", "w_down", "g_ple", "w_ple_gate", "w_ple_proj", "g_final"]
    w = dict(zip(names, [g_mix, w_in, conv_w, a_log, dt_bias, rel_bias, w_onorm, w_branch_a, w_branch_b, w_out,
                         g_ffn, w_gate_up, w_down, g_ple, w_ple_gate, w_ple_proj, g_final]))
    m = dict(zip(names, [m_g_mix, m_w_in, m_conv_w, m_a_log, m_dt_bias, m_rel_bias, m_w_onorm, m_w_branch_a,
                         m_w_branch_b, m_w_out, m_g_ffn, m_w_gate_up, m_w_down, m_g_ple, m_w_ple_gate,
                         m_w_ple_proj, m_g_final]))
    v = dict(zip(names, [v_g_mix, v_w_in, v_conv_w, v_a_log, v_dt_bias, v_rel_bias, v_w_onorm, v_w_branch_a,
                         v_w_branch_b, v_w_out, v_g_ffn, v_w_gate_up, v_w_down, v_g_ple, v_w_ple_gate,
                         v_w_ple_proj, v_g_final]))
    xi, yi, ci = _place()
    chip = 2 * xi + yi
    big_names = [n for n, _, _ in BIG]

    shards2d = {n: w[n].reshape(w[n].shape[-2:]) for n in big_names}
    shards_bf = {n: a.astype(BF16) for n, a in shards2d.items()}
    g4 = allgather_weights(shards_bf, FIRST_WEIGHTS, chip)["w_in"]
    place = jnp.stack([chip, ci]).astype(jnp.int32)
    conv_sh = jnp.where(ci == 0, w["conv_w"].reshape(CONV_K, CONV_CH // N_CHIPS), 0.0)
    conv_slots = lax.dynamic_update_slice(jnp.zeros((N_CHIPS, CONV_K, CONV_CH // N_CHIPS), F32), conv_sh[None],
                                          (chip, 0, 0))
    conv_all = small_allreduce(conv_slots.reshape(-1, 128), "gather_conv_w")
    conv_full = jnp.transpose(conv_all.reshape(N_CHIPS, CONV_K, CONV_CH // N_CHIPS), (1, 0, 2)).reshape(CONV_K, CONV_CH)
    small = {n: w[n] for n in names if n not in big_names}
    small["conv_w"] = conv_full

    grad_x, grads, small_grads, reduced_early = local_step(
        x, p[0], loss_target, g4, small, LaterWeights(shards_bf, chip), EarlyGrads(place))

    late = EarlyGrads(place, LATE_GRADS, "late")
    late.begin(grads)
    dep = jnp.full((8, 128), late.exchange(grads["w_in"]), F32)
    gshard = unpack_shard(reduced_early, EARLY_GRADS)
    small_names = [n for n, _ in SMALL]
    red = unpack_small(small_allreduce(pack_small(small_grads, small_names, 112), "allreduce_small"), SMALL)
    loss = red["loss"][0]
    conv_g = lax.dynamic_slice(red["conv_w"].reshape(CONV_K, N_CHIPS, CONV_CH // N_CHIPS), (0, chip, 0),
                               (CONV_K, 1, CONV_CH // N_CHIPS))
    gsmall = {n: red[n].reshape(w[n].shape) for n in small_names if n not in ("loss", "conv_w")}
    gsmall["conv_w"] = conv_g.reshape(w["conv_w"].shape)

    grad, delta, new_m, new_v = {}, {}, {}, {}
    for n in list(EARLY_GRADS) + list(LATE_GRADS):
        if n in LATE_GRADS:
            late.join(v_)
            gshard.update(unpack_shard(late.finish(v_), LATE_GRADS))
        shp = w[n].shape
        d_, m_, v_ = adamw(shards2d[n], gshard[n], m[n].reshape(shp[-2:]), v[n].reshape(shp[-2:]), "adamw_" + n,
                           dep=dep if n in EARLY_GRADS else None)
        grad[n], delta[n], new_m[n], new_v[n] = gshard[n].reshape(shp), d_.reshape(shp), m_.reshape(shp), v_.reshape(shp)
    snames = [n for n in small_names if n != "loss"]
    ssizes = [(n, w[n].size) for n in snames]
    pk = lambda d: pack_small(d, snames, 64)
    d_, m_, v_ = adamw(pk(w), pk(gsmall), pk(m), pk(v), "adamw_small")
    ds, ms, vs = unpack_small(d_, ssizes), unpack_small(m_, ssizes), unpack_small(v_, ssizes)
    for n in snames:
        shp = w[n].shape
        grad[n], delta[n], new_m[n], new_v[n] = gsmall[n], ds[n].reshape(shp), ms[n].reshape(shp), vs[n].reshape(shp)

    return (loss, grad_x, *[grad[n] for n in names], *[delta[n] for n in names],
            *[new_m[n] for n in names], *[new_v[n] for n in names])
```

```python
import functools

import jax
import jax.numpy as jnp
from jax import lax
from jax.experimental import pallas as pl
from jax.experimental.pallas import tpu as pltpu

F32 = jnp.float32
BF16 = jnp.bfloat16
HI = lax.Precision.HIGHEST
MESH = pl.DeviceIdType.MESH

D_MODEL = 1024
CHUNK = 64
PLE_DIM = 256
EPS = 1e-6
A_HEADS = 8
A_HEAD_DIM = 64
A_WIDTH = 512
A_LOOKBACK = 8
BAND = (A_LOOKBACK + 1) * CHUNK
TAIL = 3 * CHUNK
REL_CLIP = 128
N_REL = 2 * REL_CLIP + 1
B_HEADS = 4
B_DIM = 128
B_WIDTH = 512
CONV_K = 4
CONV_CH = 1536
D_FF = 2816
SPLIT_Z = 3584
D_IN = 5640
ADAM_LR, ADAM_B1, ADAM_B2, ADAM_EPS, ADAM_WD, ADAM_STEP = 0.001, 0.9, 0.999, 1e-08, 0.01, 10

P_GATES, P_QA, P_KA, P_VA, P_CONV, P_Z, P_BD, P_WIDTH = 0, 2048, 2560, 3072, 3584, 5120, 5632, 5760

VMEM_LIMIT = 56 * 1024 * 1024


def _cp(sem, vmem=None, **kw):
    return pltpu.CompilerParams(dimension_semantics=sem, vmem_limit_bytes=vmem, **kw)


def _tile(n, cap):
    best = None
    for t in range(128, cap + 1, 128):
        if n % t == 0:
            best = t
    assert best is not None, (n, cap)
    return best


def _nn(a, b, prec=None):
    return lax.dot_general(a, b, (((1,), (0,)), ((), ())), preferred_element_type=F32, precision=prec)


def _nt(a, b, prec=None):
    return lax.dot_general(a, b, (((1,), (1,)), ((), ())), preferred_element_type=F32, precision=prec)


def _tn(a, b, prec=None):
    return lax.dot_general(a, b, (((0,), (0,)), ((), ())), preferred_element_type=F32, precision=prec)


def _bnn(a, b, prec=None):
    return lax.dot_general(a, b, (((2,), (1,)), ((0,), (0,))), preferred_element_type=F32, precision=prec)


def _bnt(a, b, prec=None):
    return lax.dot_general(a, b, (((2,), (2,)), ((0,), (0,))), preferred_element_type=F32, precision=prec)


def _bf(a):
    return a.astype(BF16)


def _split(a):
    hi = a.astype(BF16)
    return hi, (a - hi.astype(F32)).astype(BF16)


def _bnn_exact(lhs_b, rhs):
    h1 = _bf(rhs)
    r1 = rhs - h1.astype(F32)
    h2 = _bf(r1)
    h3 = _bf(r1 - h2.astype(F32))
    return _bnn(lhs_b, h1) + (_bnn(lhs_b, h2) + _bnn(lhs_b, h3))


def _bnn3(a, b):
    ah, al = a if isinstance(a, tuple) else _split(a)
    bh, bl = b if isinstance(b, tuple) else _split(b)
    return _bnn(ah, bh) + (_bnn(ah, bl) + _bnn(al, bh))


def _sigmoid(x):
    return 0.5 * jnp.tanh(0.5 * x) + 0.5


def _softplus(x):
    return jnp.maximum(x, 0.0) + jnp.log(1.0 + jnp.exp(-jnp.abs(x)))


def rms_matmul(x, g, w, name, tm=512, tn_cap=1024):
    t, d = x.shape
    n = w.shape[1]
    tm = min(tm, t)
    tn = _tile(n, tn_cap)

    nj = n // tn

    def body(x_ref, g_ref, w_ref, o_ref, h_ref, tail_ref):
        @pl.when(pl.program_id(1) == 0)
        def _():
            xv = x_ref[...]
            r = lax.rsqrt(jnp.mean(xv * xv, axis=-1, keepdims=True) + EPS)
            h_ref[...] = _bf(xv * r * g_ref[...])

        res = _nn(h_ref[...], w_ref[...])
        o_ref[...] = _bf(res)

        @pl.when(pl.program_id(1) == nj - 1)
        def _():
            tail_ref[...] = res[:, tn - 128:]

    return pl.pallas_call(
        body, name=name, grid=(t // tm, nj),
        in_specs=[pl.BlockSpec((tm, d), lambda i, j: (i, 0)),
                  pl.BlockSpec((1, d), lambda i, j: (0, 0)),
                  pl.BlockSpec((d, tn), lambda i, j: (0, j))],
        out_specs=[pl.BlockSpec((tm, tn), lambda i, j: (i, j)),
                   pl.BlockSpec((tm, d), lambda i, j: (i, 0)),
                   pl.BlockSpec((tm, 128), lambda i, j: (i, 0))],
        out_shape=[jax.ShapeDtypeStruct((t, n), BF16), jax.ShapeDtypeStruct((t, d), BF16),
                   jax.ShapeDtypeStruct((t, 128), F32)],
        compiler_params=_cp(("parallel", "arbitrary"), VMEM_LIMIT),
    )(x, g, w)


def matmul_tn(a, b, name, into=None, col0=0, width=None, tm=1024, tk_cap=1408, tn_cap=1408):
    m, k1 = a.shape
    n = b.shape[1]
    tm = min(tm, m)
    tk = _tile(k1, tk_cap)
    tn = _tile(n, tn_cap)
    while col0 % tn:
        tn = _tile(n, tn - 128)
    nk = m // tm
    c0 = col0 // tn

    def body(*refs):
        a_ref, b_ref, o_ref, acc = refs[0], refs[1], refs[-2], refs[-1]

        @pl.when(pl.program_id(2) == 0)
        def _():
            acc[...] = jnp.zeros_like(acc)

        acc[...] += _tn(_bf(a_ref[...]), _bf(b_ref[...]))

        @pl.when(pl.program_id(2) == nk - 1)
        def _():
            o_ref[...] = _bf(acc[...])

    in_specs = [pl.BlockSpec((tm, tk), lambda i, j, k: (k, i)),
                pl.BlockSpec((tm, tn), lambda i, j, k: (k, j))]
    args = [a, b]
    total = n if width is None else width
    aliases = {}
    if into is not None:
        in_specs.append(ANY)
        args.append(into)
        total = into.shape[1]
        aliases = {2: 0}
    return pl.pallas_call(
        body, name=name, grid=(k1 // tk, n // tn, nk),
        in_specs=in_specs,
        out_specs=pl.BlockSpec((tk, tn), lambda i, j, k: (i, c0 + j)),
        out_shape=jax.ShapeDtypeStruct((k1, total), BF16),
        scratch_shapes=[pltpu.VMEM((tk, tn), F32)],
        input_output_aliases=aliases,
        compiler_params=_cp(("parallel", "parallel", "arbitrary"), VMEM_LIMIT),
    )(*args)


def _tail_onehot(qi):
    r = lax.broadcasted_iota(jnp.int32, (384, TAIL), 0)
    kj = lax.broadcasted_iota(jnp.int32, (384, TAIL), 1)
    return (r == jnp.minimum(REL_CLIP + qi - kj, REL_CLIP) + REL_CLIP).astype(F32)


def bias_tail(rel_pad):
    def body(rb_ref, o_ref):
        rb = rb_ref[...]
        for qi in range(CHUNK):
            o_ref[qi] = _nn(rb, _tail_onehot(qi), HI)

    return pl.pallas_call(
        body, name="bias_tail",
        out_shape=jax.ShapeDtypeStruct((CHUNK, A_HEADS, TAIL), F32),
    )(rel_pad)


def bias_grad(db_t, db_far):
    def body(t_ref, f_ref, o_ref):
        acc = jnp.zeros((A_HEADS, 384), F32)
        for qi in range(CHUNK):
            acc = acc + _nt(t_ref[qi], _tail_onehot(qi), HI)
        far = jnp.sum(jnp.sum(f_ref[...], axis=2), axis=1, keepdims=True)
        lane = lax.broadcasted_iota(jnp.int32, (A_HEADS, 384), 1)
        o_ref[...] = acc + jnp.where(lane == 2 * REL_CLIP, far, 0.0)

    return pl.pallas_call(
        body, name="bias_grad",
        out_shape=jax.ShapeDtypeStruct((A_HEADS, 384), F32),
    )(db_t, db_far)


ATT_CB = 8


WIN = BAND + CHUNK


def _stack_heads(a, lane):
    return jnp.concatenate([jnp.where(lane < 64, a, 0.0), jnp.where(lane >= 64, a, 0.0)], axis=0)


def _fill_band_pads(k_ref, v_ref, kp, vp, s):
    z = jnp.zeros((A_LOOKBACK * CHUNK, 128), BF16)
    kp[pl.ds(0, A_LOOKBACK * CHUNK), :] = z
    vp[pl.ds(0, A_LOOKBACK * CHUNK), :] = z
    kp[pl.ds(A_LOOKBACK * CHUNK, s), :] = _bf(k_ref[...])
    vp[pl.ds(A_LOOKBACK * CHUNK, s), :] = _bf(v_ref[...])


def attn_fwd(proj, bias_band, b, s):
    t = b * s
    nc = s // CHUNK
    qb, kb_, vb_ = P_QA // 128, P_KA // 128, P_VA // 128

    nstep = nc // ATT_CB
    rows = ATT_CB * CHUNK

    def body(q_ref, k_ref, v_ref, b_ref, o_ref, lse_ref, kp, vp):
        n0 = pl.program_id(2) * ATT_CB

        @pl.when(n0 == 0)
        def _():
            _fill_band_pads(k_ref, v_ref, kp, vp, s)

        lane = lax.broadcasted_iota(jnp.int32, (2 * CHUNK, 128), 1)
        col = lax.broadcasted_iota(jnp.int32, (4 * CHUNK, WIN), 1)
        bias4 = b_ref[...]

        def pair(pp, carry):
            n = n0 + 2 * pp
            r0 = pl.multiple_of(pp * 2 * CHUNK, 2 * CHUNK)
            start = pl.multiple_of(n * CHUNK, CHUNK)
            kb = kp[pl.ds(start, WIN), :]
            vb = vp[pl.ds(start, WIN), :]
            q4 = _stack_heads(q_ref[pl.ds(r0, 2 * CHUNK), :] * (A_HEAD_DIM ** -0.5), lane)
            sc = jnp.where(col >= (A_LOOKBACK - n) * CHUNK, _nt(_bf(q4), kb) + bias4, -1e30)
            mx = jnp.max(sc, axis=1, keepdims=True)
            p = jnp.exp(sc - mx)
            l = jnp.sum(p, axis=1, keepdims=True)
            o4 = _nn(_bf(p), vb) / l
            lse4 = mx + jnp.log(l)
            o_ref[pl.ds(r0, 2 * CHUNK), :] = jnp.where(lane < 64, o4[:2 * CHUNK], o4[2 * CHUNK:])
            lse_ref[pl.ds(r0, 2 * CHUNK), :] = jnp.where(lane < 64, lse4[:2 * CHUNK], lse4[2 * CHUNK:])
            return carry

        lax.fori_loop(0, ATT_CB // 2, pair, 0, unroll=2)

    return pl.pallas_call(
        body, name="attn_fwd", grid=(b, 4, nstep),
        in_specs=[pl.BlockSpec((rows, 128), lambda bb, m, n: (bb * nstep + n, qb + m)),
                  pl.BlockSpec((s, 128), lambda bb, m, n: (bb, kb_ + m)),
                  pl.BlockSpec((s, 128), lambda bb, m, n: (bb, vb_ + m)),
                  pl.BlockSpec((None, 4 * CHUNK, WIN), lambda bb, m, n: (m, 0, 0))],
        out_specs=[pl.BlockSpec((rows, 128), lambda bb, m, n: (bb * nstep + n, m)),
                   pl.BlockSpec((rows, 128), lambda bb, m, n: (bb * nstep + n, m))],
        out_shape=[jax.ShapeDtypeStruct((t, A_WIDTH), F32), jax.ShapeDtypeStruct((t, A_WIDTH), F32)],
        scratch_shapes=[pltpu.VMEM((s + A_LOOKBACK * CHUNK, 128), BF16),
                        pltpu.VMEM((s + A_LOOKBACK * CHUNK, 128), BF16)],
        compiler_params=_cp(("parallel", "parallel", "arbitrary"), VMEM_LIMIT),
    )(proj, proj, proj, bias_band)


def attn_bwd(proj, bias_band, y_a, lse, dy_a, b, s):
    t = b * s
    nc = s // CHUNK
    qb, kb_, vb_ = P_QA // 128, P_KA // 128, P_VA // 128
    pad = A_LOOKBACK * CHUNK
    nstep = nc // ATT_CB
    rows = ATT_CB * CHUNK

    def body(q_ref, k_ref, v_ref, b_ref, do_ref, o_ref, lse_ref,
             dq_ref, dk_ref, dv_ref, dbt_ref, dbf_ref, kp, vp, dkp, dvp):
        bb = pl.program_id(1)
        n0 = pl.program_id(2) * ATT_CB

        @pl.when(n0 == 0)
        def _():
            _fill_band_pads(k_ref, v_ref, kp, vp, s)
            dkp[...] = jnp.zeros_like(dkp)
            dvp[...] = jnp.zeros_like(dvp)

        @pl.when((n0 == 0) & (bb == 0))
        def _():
            dbt_ref[...] = jnp.zeros_like(dbt_ref)
            dbf_ref[...] = jnp.zeros_like(dbf_ref)

        lane = lax.broadcasted_iota(jnp.int32, (2 * CHUNK, 128), 1)
        col = lax.broadcasted_iota(jnp.int32, (4 * CHUNK, WIN), 1)
        bias4 = b_ref[...]

        def pair(pp, carry):
            n = n0 + 2 * pp
            r0 = pl.multiple_of(pp * 2 * CHUNK, 2 * CHUNK)
            start = pl.multiple_of(n * CHUNK, CHUNK)
            kb = kp[pl.ds(start, WIN), :]
            vb = vp[pl.ds(start, WIN), :]
            q4b = _bf(_stack_heads(q_ref[pl.ds(r0, 2 * CHUNK), :] * (A_HEAD_DIM ** -0.5), lane))
            do4 = _stack_heads(do_ref[pl.ds(r0, 2 * CHUNK), :], lane)
            do4b = _bf(do4)
            o = o_ref[pl.ds(r0, 2 * CHUNK), :]
            lsev = lse_ref[pl.ds(r0, 2 * CHUNK), :]
            lse4 = jnp.concatenate([lsev[:, 0:1], lsev[:, 64:65]], axis=0)
            sc = jnp.where(col >= (A_LOOKBACK - n) * CHUNK, _nt(q4b, kb) + bias4, -1e30)
            p = jnp.exp(sc - lse4)
            dp = _nt(do4b, vb)
            delta = jnp.sum(do4 * jnp.concatenate([o, o], axis=0), axis=1, keepdims=True)
            ds = p * (dp - delta)
            dsb = _bf(ds)
            dq4 = _nn(dsb, kb)
            dq_ref[pl.ds(r0, 2 * CHUNK), :] = _bf(
                jnp.where(lane < 64, dq4[:2 * CHUNK], dq4[2 * CHUNK:]) * (A_HEAD_DIM ** -0.5))
            dkp[pl.ds(start, WIN), :] += _tn(dsb, q4b)
            dvp[pl.ds(start, WIN), :] += _tn(_bf(p), do4b)
            dbt_ref[...] += ds[:, WIN - 256:]
            dbf_ref[...] += ds[:, 0:128] + ds[:, 128:256] + ds[:, 256:384]
            return carry

        lax.fori_loop(0, ATT_CB // 2, pair, 0, unroll=2)

        @pl.when(n0 == nc - ATT_CB)
        def _():
            dk_ref[...] = _bf(dkp[pl.ds(pad, s), :])
            dv_ref[...] = _bf(dvp[pl.ds(pad, s), :])

    return pl.pallas_call(
        body, name="attn_bwd", grid=(4, b, nstep),
        in_specs=[pl.BlockSpec((rows, 128), lambda m, bb, n: (bb * nstep + n, qb + m)),
                  pl.BlockSpec((s, 128), lambda m, bb, n: (bb, kb_ + m)),
                  pl.BlockSpec((s, 128), lambda m, bb, n: (bb, vb_ + m)),
                  pl.BlockSpec((None, 4 * CHUNK, WIN), lambda m, bb, n: (m, 0, 0)),
                  pl.BlockSpec((rows, 128), lambda m, bb, n: (bb * nstep + n, m)),
                  pl.BlockSpec((rows, 128), lambda m, bb, n: (bb * nstep + n, m)),
                  pl.BlockSpec((rows, 128), lambda m, bb, n: (bb * nstep + n, m))],
        out_specs=[pl.BlockSpec((rows, 128), lambda m, bb, n: (bb * nstep + n, m)),
                   pl.BlockSpec((s, 128), lambda m, bb, n: (bb, m)),
                   pl.BlockSpec((s, 128), lambda m, bb, n: (bb, m)),
                   pl.BlockSpec((None, 4 * CHUNK, 256), lambda m, bb, n: (m, 0, 0)),
                   pl.BlockSpec((None, 4 * CHUNK, 128), lambda m, bb, n: (m, 0, 0))],
        out_shape=[jax.ShapeDtypeStruct((t, A_WIDTH), BF16)] * 3
        + [jax.ShapeDtypeStruct((4, 4 * CHUNK, 256), F32),
           jax.ShapeDtypeStruct((4, 4 * CHUNK, 128), F32)],
        scratch_shapes=[pltpu.VMEM((s + pad, 128), BF16), pltpu.VMEM((s + pad, 128), BF16),
                        pltpu.VMEM((s + pad, 128), F32), pltpu.VMEM((s + pad, 128), F32)],
        compiler_params=_cp(("parallel", "arbitrary", "arbitrary"), VMEM_LIMIT),
    )(proj, proj, proj, bias_band, dy_a, y_a, lse)


def _conv_taps(x, w, s):
    row = lax.broadcasted_iota(jnp.int32, x.shape, 0)
    shifted = [x] + [jnp.where(row >= i, pltpu.roll(x, i, 0), 0.0) for i in range(1, CONV_K)]
    acc = shifted[0] * w[CONV_K - 1:CONV_K, :]
    for i in range(1, CONV_K):
        acc = acc + shifted[i] * w[CONV_K - 1 - i:CONV_K - i, :]
    return acc, shifted


def conv_fwd(proj, conv_w8, b, s):
    cb = 512
    c0 = P_CONV // cb

    def body(x_ref, w_ref, o_ref):
        a, _ = _conv_taps(x_ref[...].astype(F32), w_ref[...], s)
        o_ref[...] = a * _sigmoid(a)

    return pl.pallas_call(
        body, name="conv_fwd", grid=(b, CONV_CH // cb),
        in_specs=[pl.BlockSpec((s, cb), lambda bb, j: (bb, c0 + j)),
                  pl.BlockSpec((8, cb), lambda bb, j: (0, j))],
        out_specs=pl.BlockSpec((s, cb), lambda bb, j: (bb, j)),
        out_shape=jax.ShapeDtypeStruct((b * s, CONV_CH), F32),
        compiler_params=_cp(("parallel", "parallel"), VMEM_LIMIT),
    )(proj, conv_w8)


def conv_bwd(proj, conv_w8, dc3, b, s):
    cb = 512
    c0 = P_CONV // cb

    def body(x_ref, w_ref, dc_ref, dx_ref, dw_ref):
        @pl.when(pl.program_id(1) == 0)
        def _():
            dw_ref[...] = jnp.zeros_like(dw_ref)

        w = w_ref[...]
        a, shifted = _conv_taps(x_ref[...].astype(F32), w, s)
        sg = _sigmoid(a)
        da = dc_ref[...] * (sg * (1.0 + a * (1.0 - sg)))
        row = lax.broadcasted_iota(jnp.int32, da.shape, 0)
        dx = da * w[CONV_K - 1:CONV_K, :]
        for i in range(1, CONV_K):
            dx = dx + jnp.where(row < s - i, pltpu.roll(da, s - i, 0), 0.0) * w[CONV_K - 1 - i:CONV_K - i, :]
        dx_ref[...] = _bf(dx)
        r8 =lax.broadcasted_iota(jnp.int32, (8, cb), 0)
        dw = jnp.zeros((8, cb), F32)
        for i in range(CONV_K):
            dw = dw + jnp.where(r8 == CONV_K - 1 - i, jnp.sum(da * shifted[i], axis=0, keepdims=True), 0.0)
        dw_ref[...] += dw

    return pl.pallas_call(
        body, name="conv_bwd", grid=(CONV_CH // cb, b),
        in_specs=[pl.BlockSpec((s, cb), lambda j, bb: (bb, c0 + j)),
                  pl.BlockSpec((8, cb), lambda j, bb: (0, j)),
                  pl.BlockSpec((None, s, cb), lambda j, bb: (j, bb, 0))],
        out_specs=[pl.BlockSpec((s, cb), lambda j, bb: (bb, j)),
                   pl.BlockSpec((8, cb), lambda j, bb: (0, j))],
        out_shape=[jax.ShapeDtypeStruct((b * s, CONV_CH), BF16), jax.ShapeDtypeStruct((8, CONV_CH), F32)],
        compiler_params=_cp(("parallel", "arbitrary"), VMEM_LIMIT),
    )(proj, conv_w8, dc3)


def _pick_lane(v, k):
    lane = lax.broadcasted_iota(jnp.int32, v.shape, 1)
    return jnp.sum(jnp.where(lane == k, v, 0.0), axis=1, keepdims=True)


def _chunk_masks(ncb):
    i = lax.broadcasted_iota(jnp.int32, (ncb, CHUNK, CHUNK), 1)
    j = lax.broadcasted_iota(jnp.int32, (ncb, CHUNK, CHUNK), 2)
    return i, j


def _col_of_row(rowvec, eye):
    return jnp.sum(jnp.where(eye, rowvec, 0.0), axis=2, keepdims=True)


def _dn_chunk_math(cq, ck, cv, bd, al_row, dtb_row, h, ncb, tm=None):
    r = ncb * CHUNK
    i, j = _chunk_masks(ncb)
    eye = i == j
    low = i >= j
    strict = i > j
    ones = jnp.ones((ncb, CHUNK, CHUNK), F32)

    braw = _pick_lane(bd, h)
    draw = _pick_lane(bd, B_HEADS + h)
    al = _pick_lane(al_row, h)
    dtb = _pick_lane(dtb_row, h)
    ea = jnp.exp(al)
    beta = _sigmoid(braw)
    sp_arg = draw + dtb
    g = -ea * _softplus(sp_arg)

    rq = lax.rsqrt(jnp.sum(cq * cq, axis=1, keepdims=True) + EPS)
    rk = lax.rsqrt(jnp.sum(ck * ck, axis=1, keepdims=True) + EPS)
    nq = cq * rq
    kn = ck * rk
    qn = nq * (B_DIM ** -0.5)

    def c3(a):
        return a.reshape(ncb, CHUNK, a.shape[-1])

    qn3, kn3, v3, beta3 = c3(qn), c3(kn), c3(cv), c3(beta)
    gb = jnp.broadcast_to(c3(g), (ncb, CHUNK, CHUNK))
    gc_b = _bnn_exact(low.astype(BF16), gb)
    gr_b = _bnn_exact(_bf(ones), jnp.where(eye, gc_b, 0.0))
    dm = jnp.where(low, jnp.exp(jnp.where(low, gc_b - gr_b, 0.0)), 0.0)
    gc = gc_b[:, :, 0:1]
    gl = gc_b[:, CHUNK - 1:CHUNK, 0:1]
    gam = jnp.exp(gc)
    egl = jnp.exp(gl)
    edec = jnp.exp(gl - gc)

    knb = _bf(kn3)
    kk = _bnt(knb, knb)
    kd = jnp.where(strict, kk * dm, 0.0)
    a = beta3 * kd
    sz = 1 if tm is None else CHUNK
    if tm is None:
        tm = eye.astype(F32)
    while sz < CHUNK:
        off = jnp.where(((i // (2 * sz)) == (j // (2 * sz))) & ((i // sz) != (j // sz)), a, 0.0)
        tmb = _bf(tm)
        tm = tm - _bnn(_bf(_bnn(tmb, _bf(off))), tmb)
        sz *= 2
    bv = beta3 * v3
    bk = (beta3 * gam) * kn3
    sol = _bnn3(_split(tm), jnp.concatenate([bv, bk], axis=2))
    u, wk = sol[:, :, :B_DIM], sol[:, :, B_DIM:]
    qk = _bnt(_bf(qn3), knb)
    p = jnp.where(low, qk * dm, 0.0)
    kdec = kn3 * edec
    qg = gam * qn3
    return dict(beta=beta3, g=c3(g), ea=ea, sp_arg=c3(sp_arg), rq=c3(rq), rk=c3(rk), nq=c3(nq),
                qn=qn3, kn=kn3, v=v3, gc=gc, gl=gl, gam=gam, egl=egl, edec=edec, dm=dm, kd=kd, a=a,
                tm=tm, u=u, wk=wk, qk=qk, p=p, kdec=kdec, qg=qg, eye=eye, low=low, strict=strict)


def dn_prep(c, proj, al_row, dtb_row, b, s, ncb=16):
    t = b * s
    r = ncb * CHUNK
    nblk = t // r
    bd_blk = 0

    def body(cq_ref, ck_ref, cv_ref, bd_ref, al_ref, dtb_ref, u_ref, wk_ref, qg_ref, kdec_ref, p_ref, egl_ref,
             tm_ref):
        h = pl.program_id(1)
        m = _dn_chunk_math(cq_ref[...], ck_ref[...], cv_ref[...], bd_ref[...].astype(F32), al_ref[...], dtb_ref[...], h, ncb)
        tm_ref[...] = m["tm"].reshape(r, CHUNK)
        u_ref[...] = m["u"].reshape(r, B_DIM)
        wk_ref[...] = m["wk"].reshape(r, B_DIM)
        qg_ref[...] = m["qg"].reshape(r, B_DIM)
        kdec_ref[...] = m["kdec"].reshape(r, B_DIM)
        p_ref[...] = m["p"].reshape(r, CHUNK)
        egl_ref[...] = jnp.broadcast_to(m["egl"], (ncb, 8, 128)).reshape(ncb * 8, 128)

    col = lambda k: pl.BlockSpec((r, 128), lambda i, h: (i, k * B_HEADS + h))
    out_col = pl.BlockSpec((r, 128), lambda i, h: (i, h))
    small = pl.BlockSpec((1, 128), lambda i, h: (0, 0))
    return pl.pallas_call(
        body, name="dn_prep", grid=(nblk, B_HEADS),
        in_specs=[col(0), col(1), col(2), pl.BlockSpec((r, 128), lambda i, h: (i, bd_blk)), small, small],
        out_specs=[out_col, out_col, out_col, out_col,
                   pl.BlockSpec((None, r, CHUNK), lambda i, h: (h, i, 0)),
                   pl.BlockSpec((None, ncb * 8, 128), lambda i, h: (h, i, 0)),
                   pl.BlockSpec((None, r, CHUNK), lambda i, h: (h, i, 0))],
        out_shape=[jax.ShapeDtypeStruct((t, B_WIDTH), F32)] * 4
        + [jax.ShapeDtypeStruct((B_HEADS, t, CHUNK), F32),
           jax.ShapeDtypeStruct((B_HEADS, t // 8, 128), F32),
           jax.ShapeDtypeStruct((B_HEADS, t, CHUNK), F32)],
        compiler_params=_cp(("parallel", "parallel"), VMEM_LIMIT),
    )(c, c, c, proj, al_row, dtb_row)


def dn_scan_fwd(u, wk, qg, kdec, p, egl, b, s):
    t = b * s
    nc = s // CHUNK

    def body(u_ref, wk_ref, qg_ref, kdec_ref, p_ref, egl_ref, o_ref, ss_ref, st):
        @pl.when(pl.program_id(0) == 0)
        def _():
            st[...] = jnp.zeros_like(st)

        chains = [(bb, h) for bb in range(b) for h in range(B_HEADS)]
        states = [st[bb * B_HEADS + h] for bb, h in chains]
        outs, new_states = [], []
        for (bb, h), sh in zip(chains, states):
            sl = slice(h * B_DIM, (h + 1) * B_DIM)
            sb = _bf(sh)
            w = u_ref[bb, :, sl] - _nt(_bf(wk_ref[bb, :, sl]), sb)
            outs.append(_nt(_bf(qg_ref[bb, :, sl]), sb) + _nn(_bf(p_ref[h, bb]), _bf(w)))
            new_states.append(egl_ref[h, bb][0:1, :] * sh + _tn(_bf(w), _bf(kdec_ref[bb, :, sl])))
        for (bb, h), sh, o, ns in zip(chains, states, outs, new_states):
            ss_ref[bb, h] = sh
            o_ref[bb, :, h * B_DIM:(h + 1) * B_DIM] = o
            st[bb * B_HEADS + h] = ns

    r3 = lambda a: a.reshape(b, s, B_WIDTH)
    act = pl.BlockSpec((b, CHUNK, B_WIDTH), lambda n: (0, n, 0))
    o, states = pl.pallas_call(
        body, name="dn_scan_fwd", grid=(nc,),
        in_specs=[act, act, act, act,
                  pl.BlockSpec((B_HEADS, b, CHUNK, CHUNK), lambda n: (0, 0, n, 0)),
                  pl.BlockSpec((B_HEADS, b, 8, 128), lambda n: (0, 0, n, 0))],
        out_specs=[act, pl.BlockSpec((b, None, B_HEADS, B_DIM, B_DIM), lambda n: (0, n, 0, 0, 0))],
        out_shape=[jax.ShapeDtypeStruct((b, s, B_WIDTH), F32),
                   jax.ShapeDtypeStruct((b, nc, B_HEADS, B_DIM, B_DIM), F32)],
        scratch_shapes=[pltpu.VMEM((b * B_HEADS, B_DIM, B_DIM), F32)],
        compiler_params=_cp(("arbitrary",), VMEM_LIMIT),
    )(r3(u), r3(wk), r3(qg), r3(kdec), p.reshape(B_HEADS, b, s, CHUNK), egl.reshape(B_HEADS, b, s // 8, 128))
    return o.reshape(t, B_WIDTH), states


def dn_scan_bwd(u, wk, qg, kdec, p, egl, states, do, b, s):
    t = b * s
    nc = s // CHUNK

    def body(u_ref, wk_ref, qg_ref, kdec_ref, p_ref, egl_ref, ss_ref, do_ref,
             dw_ref, dwk_ref, dqg_ref, dkdec_ref, dp_ref, degl_ref, dst):
        @pl.when(pl.program_id(0) == 0)
        def _():
            dst[...] = jnp.zeros_like(dst)

        chains = [(bb, h) for bb in range(b) for h in range(B_HEADS)]
        dstates = [dst[bb * B_HEADS + h] for bb, h in chains]
        results = []
        for (bb, h), dsp in zip(chains, dstates):
            sl = slice(h * B_DIM, (h + 1) * B_DIM)
            sh = ss_ref[bb, h]
            sb = _bf(sh)
            dsb = _bf(dsp)
            wkb = _bf(wk_ref[bb, :, sl])
            kdb = _bf(kdec_ref[bb, :, sl])
            pb = _bf(p_ref[h, bb])
            dob = _bf(do_ref[bb, :, sl])
            w = u_ref[bb, :, sl] - _nt(wkb, sb)
            wb = _bf(w)
            dw = _tn(pb, dob) + _nt(kdb, dsb)
            dwb = _bf(dw)
            tot = jnp.sum(jnp.sum(sh * dsp, axis=1, keepdims=True), axis=0, keepdims=True)
            new_ds = egl_ref[h, bb][0:1, :] * dsp + _tn(dob, _bf(qg_ref[bb, :, sl])) - _tn(dwb, wkb)
            results.append((dw, _nn(dob, sb), -_nn(dwb, sb), _nn(wb, dsb), _nt(dob, wb), tot, new_ds))
        for (bb, h), (dw, dqg, dwk, dkdec, dpm, tot, new_ds) in zip(chains, results):
            sl = slice(h * B_DIM, (h + 1) * B_DIM)
            dw_ref[bb, :, sl] = dw
            dqg_ref[bb, :, sl] = dqg
            dwk_ref[bb, :, sl] = dwk
            dkdec_ref[bb, :, sl] = dkdec
            dp_ref[h, bb] = dpm
            degl_ref[h, bb] = jnp.broadcast_to(tot, (8, 128))
            dst[bb * B_HEADS + h] = new_ds

    r3 = lambda a: a.reshape(b, s, B_WIDTH)
    act = pl.BlockSpec((b, CHUNK, B_WIDTH), lambda n: (0, nc - 1 - n, 0))
    pspec = pl.BlockSpec((B_HEADS, b, CHUNK, CHUNK), lambda n: (0, 0, nc - 1 - n, 0))
    espec = pl.BlockSpec((B_HEADS, b, 8, 128), lambda n: (0, 0, nc - 1 - n, 0))
    outs = pl.pallas_call(
        body, name="dn_scan_bwd", grid=(nc,),
        in_specs=[act, act, act, act, pspec, espec,
                  pl.BlockSpec((b, None, B_HEADS, B_DIM, B_DIM), lambda n: (0, nc - 1 - n, 0, 0, 0)),
                  act],
        out_specs=[act, act, act, act, pspec, espec],
        out_shape=[jax.ShapeDtypeStruct((b, s, B_WIDTH), F32)] * 4
        + [jax.ShapeDtypeStruct((B_HEADS, b, s, CHUNK), F32),
           jax.ShapeDtypeStruct((B_HEADS, b, s // 8, 128), F32)],
        scratch_shapes=[pltpu.VMEM((b * B_HEADS, B_DIM, B_DIM), F32)],
        compiler_params=_cp(("arbitrary",), VMEM_LIMIT),
    )(r3(u), r3(wk), r3(qg), r3(kdec), p.reshape(B_HEADS, b, s, CHUNK), egl.reshape(B_HEADS, b, s // 8, 128),
      states, r3(do))
    return (*[a.reshape(t, B_WIDTH) for a in outs[:4]], outs[4].reshape(B_HEADS, t, CHUNK),
            outs[5].reshape(B_HEADS, t // 8, 128))


def dn_post_bwd(c, proj, al_row, dtb_row, tmat, dw, dwk, dqg, dkdec, dp, degl, b, s, ncb=16):
    t = b * s
    r = ncb * CHUNK
    nblk = t // r
    bd_blk = 0

    def body(cq_ref, ck_ref, cv_ref, bd_ref, al_ref, dtb_ref, tm_ref, dw_ref, dwk_ref, dqg_ref, dkdec_ref, dp_ref,
             degl_ref, dc_ref, dbd_ref, dal_ref, ddtb_ref):
        h = pl.program_id(1)

        @pl.when((pl.program_id(0) == 0) & (h == 0))
        def _():
            dal_ref[...] = jnp.zeros_like(dal_ref)
            ddtb_ref[...] = jnp.zeros_like(ddtb_ref)

        m = _dn_chunk_math(cq_ref[...], ck_ref[...], cv_ref[...], bd_ref[...].astype(F32), al_ref[...], dtb_ref[...], h, ncb,
                           tm=tm_ref[...].reshape(ncb, CHUNK, CHUNK))
        eye, low, strict = m["eye"], m["low"], m["strict"]
        eyef = eye.astype(F32)

        def c3(a):
            return a.reshape(ncb, CHUNK, a.shape[-1])

        du, dwkv, dqg, dkdec = c3(dw_ref[...]), c3(dwk_ref[...]), c3(dqg_ref[...]), c3(dkdec_ref[...])
        dpm = jnp.where(low, c3(dp_ref[...]), 0.0)
        degl = degl_ref[...].reshape(ncb, 8, 128)[:, 0:1, 0:1]
        beta, gam, kn, qn, v = m["beta"], m["gam"], m["kn"], m["qn"], m["v"]
        dm, kd, a, p = m["dm"], m["kd"], m["a"], m["p"]
        knb, qnb = _bf(kn), _bf(qn)

        eyeb = _bf(eyef)
        th, tl = _split(m["tm"])
        tts = (_bf(_bnt(eyeb, th)), _bf(_bnt(eyeb, tl)))
        xy = _bnn3(tts, jnp.concatenate([du, dwkv], axis=2))
        x, y = xy[:, :, :B_DIM], xy[:, :, B_DIM:]
        da = -jnp.where(strict, _bnt(_bf(x), _bf(m["u"])) + _bnt(_bf(y), _bf(m["wk"])), 0.0)
        dv = beta * x
        sy = jnp.sum(y * kn, axis=2, keepdims=True)
        dbeta = jnp.sum(x * v, axis=2, keepdims=True) + gam * sy + jnp.sum(da * kd, axis=2, keepdims=True)
        dgam = beta * sy + jnp.sum(dqg * qn, axis=2, keepdims=True)
        dkk = da * beta * dm
        dqk = dpm * dm
        dkkb, dqkb = _bf(dkk), _bf(dqk)
        dkn = ((beta * gam) * y + _bnn(dkkb, knb) + _bnn(_bf(_bnt(eyeb, dkkb)), knb)
               + _bnn(_bf(_bnt(eyeb, dqkb)), qnb) + dkdec * m["edec"])
        dqn = gam * dqg + _bnn(dqkb, knb)
        mm = da * a + dpm * p
        ek = jnp.sum(dkdec * m["kdec"], axis=2, keepdims=True)
        dgc = (jnp.sum(mm, axis=2, keepdims=True) - _col_of_row(jnp.sum(mm, axis=1, keepdims=True), eye)
               + dgam * gam - ek)
        dgl = jnp.sum(ek, axis=1, keepdims=True) + degl * m["egl"]
        i, _ = _chunk_masks(ncb)
        dgc = dgc + jnp.where(i[:, :, 0:1] == CHUNK - 1, dgl, 0.0)
        upper = (i <= _chunk_masks(ncb)[1]).astype(BF16)
        dg = _bnn_exact(upper, jnp.broadcast_to(dgc, (ncb, CHUNK, CHUNK)))[:, :, 0:1]

        nq = m["nq"]
        dnq = dqn * (B_DIM ** -0.5)
        dcq = m["rq"] * (dnq - nq * jnp.sum(nq * dnq, axis=2, keepdims=True))
        dck = m["rk"] * (dkn - kn * jnp.sum(kn * dkn, axis=2, keepdims=True))
        dc_ref[0] = dcq.reshape(r, B_DIM)
        dc_ref[1] = dck.reshape(r, B_DIM)
        dc_ref[2] = dv.reshape(r, B_DIM)

        dbraw = (dbeta * beta * (1.0 - beta)).reshape(r, 1)
        sgm = _sigmoid(m["sp_arg"])
        ddraw3 = dg * (-m["ea"]) * sgm
        ddraw = ddraw3.reshape(r, 1)
        lane = lax.broadcasted_iota(jnp.int32, (r, 128), 1)
        contrib = jnp.where(lane == h, dbraw, 0.0) + jnp.where(lane == B_HEADS + h, ddraw, 0.0)

        @pl.when(h == 0)
        def _():
            dbd_ref[...] = contrib

        @pl.when(h != 0)
        def _():
            dbd_ref[...] += contrib

        lane8 = lax.broadcasted_iota(jnp.int32, (8, 128), 1)
        tot_al = jnp.sum(jnp.sum(dg * m["g"], axis=1, keepdims=True), axis=0, keepdims=True).reshape(1, 1)
        tot_dtb = jnp.sum(jnp.sum(ddraw3, axis=1, keepdims=True), axis=0, keepdims=True).reshape(1, 1)
        dal_ref[...] += jnp.where(lane8 == h, tot_al, 0.0)
        ddtb_ref[...] += jnp.where(lane8 == h, tot_dtb, 0.0)

    col = lambda k: pl.BlockSpec((r, 128), lambda i, h: (i, k * B_HEADS + h))
    hcol = pl.BlockSpec((r, 128), lambda i, h: (i, h))
    small = pl.BlockSpec((1, 128), lambda i, h: (0, 0))
    acc = pl.BlockSpec((8, 128), lambda i, h: (0, 0))
    return pl.pallas_call(
        body, name="dn_post_bwd", grid=(nblk, B_HEADS),
        in_specs=[col(0), col(1), col(2), pl.BlockSpec((r, 128), lambda i, h: (i, bd_blk)), small, small,
                  pl.BlockSpec((None, r, CHUNK), lambda i, h: (h, i, 0)),
                  hcol, hcol, hcol, hcol,
                  pl.BlockSpec((None, r, CHUNK), lambda i, h: (h, i, 0)),
                  pl.BlockSpec((None, ncb * 8, 128), lambda i, h: (h, i, 0))],
        out_specs=[pl.BlockSpec((3, r, 128), lambda i, h: (0, i, h)),
                   pl.BlockSpec((r, 128), lambda i, h: (i, 0)), acc, acc],
        out_shape=[jax.ShapeDtypeStruct((3, t, B_WIDTH), F32), jax.ShapeDtypeStruct((t, 128), F32),
                   jax.ShapeDtypeStruct((8, 128), F32), jax.ShapeDtypeStruct((8, 128), F32)],
        compiler_params=_cp(("arbitrary", "arbitrary"), VMEM_LIMIT),
    )(c, c, c, proj, al_row, dtb_row, tmat, dw, dwk, dqg, dkdec, dp, degl)


def make_bias_band(rel_bias):
    tail = bias_tail(jnp.pad(rel_bias, ((0, 0), (0, 384 - N_REL))))
    far = jnp.broadcast_to(rel_bias[:, 2 * REL_CLIP][:, None, None], (A_HEADS, CHUNK, BAND - TAIL))
    band = jnp.concatenate([far, jnp.transpose(tail, (1, 0, 2))], axis=2)
    off = jnp.full((A_HEADS, CHUNK, CHUNK), -1e30, F32)
    both = jnp.stack([jnp.concatenate([band, off], axis=2), jnp.concatenate([off, band], axis=2)], axis=1)
    return both.reshape(4, 4 * CHUNK, WIN)


def bias_band_grad(dbt, dbf):
    t5 = dbt.reshape(A_HEADS, 2, CHUNK, 256)
    tail = t5[:, 0, :, :TAIL] + t5[:, 1, :, CHUNK:]
    far = dbf.reshape(A_HEADS, 2, CHUNK, 128).sum(axis=1) + jnp.pad(t5[:, 1, :, :CHUNK], ((0, 0), (0, 0), (0, CHUNK)))
    return bias_grad(jnp.transpose(tail, (1, 0, 2)), far)[:, :N_REL]


def _rms(x):
    r = lax.rsqrt(jnp.mean(x * x, axis=-1, keepdims=True) + EPS)
    return r, x * r


def _rms_bwd(dh, g, r, n):
    dn = dh * g
    return r * (dn - n * jnp.mean(dn * n, axis=-1, keepdims=True)), dh * n


def _gated_onorm(o, z, w_on):
    parts = []
    for h in range(B_HEADS):
        sl = slice(h * B_DIM, (h + 1) * B_DIM)
        r, n = _rms(o[:, sl])
        parts.append((r, n))
    r4 = [p[0] for p in parts]
    n4 = jnp.concatenate([p[1] for p in parts], axis=1)
    w4 = jnp.concatenate([w_on] * B_HEADS, axis=1)
    sz = _sigmoid(z)
    silu = z * sz
    return n4 * w4 * silu, r4, n4, w4, sz, silu


def mid_fwd(x, y_a, o_b, proj, w_on, wa, wb, w_out, tm=256):
    t = x.shape[0]
    tm = min(tm, t)

    def body(x_ref, ya_ref, ob_ref, z_ref, ga_ref, gb_ref, won_ref, wa_ref, wb_ref, wo_ref, x1_ref, mg_ref):
        yb = _gated_onorm(ob_ref[...], z_ref[...].astype(F32), won_ref[...])[0]
        ua = _nn(_bf(ya_ref[...]), wa_ref[...])
        ub = _nn(_bf(yb), wb_ref[...])
        merged = _sigmoid(ga_ref[...].astype(F32)) * ua + _sigmoid(gb_ref[...].astype(F32)) * ub
        mb = _bf(merged)
        mg_ref[...] = mb
        x1_ref[...] = x_ref[...] + _nn(mb, wo_ref[...])

    rowd = pl.BlockSpec((tm, D_MODEL), lambda i: (i, 0))
    row5 = pl.BlockSpec((tm, 512), lambda i: (i, 0))
    full = lambda a: pl.BlockSpec(a.shape, lambda i: (0,) * a.ndim)
    return pl.pallas_call(
        body, name="mid_fwd", grid=(t // tm,),
        in_specs=[rowd, row5, row5,
                  pl.BlockSpec((tm, 512), lambda i: (i, P_Z // 512)),
                  pl.BlockSpec((tm, D_MODEL), lambda i: (i, 0)),
                  pl.BlockSpec((tm, D_MODEL), lambda i: (i, 1)),
                  full(w_on), full(wa), full(wb), full(w_out)],
        out_specs=[rowd, rowd],
        out_shape=[jax.ShapeDtypeStruct((t, D_MODEL), F32), jax.ShapeDtypeStruct((t, D_MODEL), BF16)],
        compiler_params=_cp(("parallel",), VMEM_LIMIT),
    )(x, y_a, o_b, proj, proj, proj, w_on, wa, wb, w_out)


def mid_bwd(dx1, merged, y_a, o_b, proj, w_on, wa, wb, w_out, tm=256):
    t = dx1.shape[0]
    tm = min(tm, t)

    def body(dx1_ref, mg_ref, ya_ref, ob_ref, z_ref, ga_ref, gb_ref, won_ref, wa_ref, wb_ref, wo_ref,
             dya_ref, dob_ref, dz_ref, dg_ref, dwo_ref, dwa_ref, dwb_ref, dwon_ref):
        @pl.when(pl.program_id(0) == 0)
        def _():
            dwo_ref[...] = jnp.zeros_like(dwo_ref)
            dwa_ref[...] = jnp.zeros_like(dwa_ref)
            dwb_ref[...] = jnp.zeros_like(dwb_ref)
            dwon_ref[...] = jnp.zeros_like(dwon_ref)

        dx1b = _bf(dx1_ref[...])
        dmerged = _nt(dx1b, wo_ref[...])
        dwo_ref[...] += _tn(mg_ref[...], dx1b)
        o = ob_ref[...]
        z = z_ref[...].astype(F32)
        yb, r4, n4, w4, sz, silu = _gated_onorm(o, z, won_ref[...])
        yab, ybb = _bf(ya_ref[...]), _bf(yb)
        ua = _nn(yab, wa_ref[...])
        ub = _nn(ybb, wb_ref[...])
        sa, sb = _sigmoid(ga_ref[...].astype(F32)), _sigmoid(gb_ref[...].astype(F32))
        dua, dub = _bf(dmerged * sa), _bf(dmerged * sb)
        dg_ref[:, 0:D_MODEL] = _bf(dmerged * ua * sa * (1.0 - sa))
        dg_ref[:, D_MODEL:2 * D_MODEL] = _bf(dmerged * ub * sb * (1.0 - sb))
        dwa_ref[...] += _tn(yab, dua)
        dwb_ref[...] += _tn(ybb, dub)
        dya_ref[...] = _nt(dua, wa_ref[...])
        dyb = _nt(dub, wb_ref[...])
        dz_ref[...] = _bf(dyb * (n4 * w4) * (sz * (1.0 + z * (1.0 - sz))))
        dnw = dyb * silu
        dwon = jnp.zeros((1, B_DIM), F32)
        for h in range(B_HEADS):
            sl = slice(h * B_DIM, (h + 1) * B_DIM)
            dxh, dgh = _rms_bwd(dnw[:, sl], won_ref[...], r4[h], n4[:, sl])
            dob_ref[:, sl] = dxh
            dwon = dwon + jnp.sum(dgh, axis=0, keepdims=True)
        dwon_ref[...] += jnp.broadcast_to(dwon, (8, B_DIM))

    rowd = pl.BlockSpec((tm, D_MODEL), lambda i: (i, 0))
    row5 = pl.BlockSpec((tm, 512), lambda i: (i, 0))
    full = lambda a: pl.BlockSpec(a.shape, lambda i: (0,) * a.ndim)
    fixed = lambda shp: pl.BlockSpec(shp, lambda i: (0,) * len(shp))
    return pl.pallas_call(
        body, name="mid_bwd", grid=(t // tm,),
        in_specs=[rowd, rowd, row5, row5,
                  pl.BlockSpec((tm, 512), lambda i: (i, P_Z // 512)),
                  pl.BlockSpec((tm, D_MODEL), lambda i: (i, 0)),
                  pl.BlockSpec((tm, D_MODEL), lambda i: (i, 1)),
                  full(w_on), full(wa), full(wb), full(w_out)],
        out_specs=[row5, row5, row5, pl.BlockSpec((tm, 2 * D_MODEL), lambda i: (i, 0)),
                   fixed((D_MODEL, D_MODEL)), fixed((A_WIDTH, D_MODEL)), fixed((B_WIDTH, D_MODEL)),
                   fixed((8, B_DIM))],
        out_shape=[jax.ShapeDtypeStruct((t, 512), F32), jax.ShapeDtypeStruct((t, 512), F32),
                   jax.ShapeDtypeStruct((t, 512), BF16), jax.ShapeDtypeStruct((t, 2 * D_MODEL), BF16),
           jax.ShapeDtypeStruct((D_MODEL, D_MODEL), F32), jax.ShapeDtypeStruct((A_WIDTH, D_MODEL), F32),
           jax.ShapeDtypeStruct((B_WIDTH, D_MODEL), F32), jax.ShapeDtypeStruct((8, B_DIM), F32)],
        compiler_params=_cp(("arbitrary",), VMEM_LIMIT),
    )(dx1, merged, y_a, o_b, proj, proj, proj, w_on, wa, wb, w_out)


FFN_TF = 1408


def ffn_up(x1, g, w_gu, tm=512, tf=FFN_TF):
    t = x1.shape[0]
    tm = min(tm, t)
    nf = D_FF // tf

    def body(x_ref, g_ref, wg_ref, wu_ref, gate_ref, up_ref, act_ref, h_ref):
        @pl.when(pl.program_id(1) == 0)
        def _():
            r, n = _rms(x_ref[...])
            h_ref[...] = _bf(n * g_ref[...])

        hb = h_ref[...]
        gate = _nn(hb, wg_ref[...])
        up = _nn(hb, wu_ref[...])
        gate_ref[...] = _bf(gate)
        up_ref[...] = _bf(up)
        act_ref[...] = _bf(gate * _sigmoid(gate) * up)

    ff = pl.BlockSpec((tm, tf), lambda i, j: (i, j))
    return pl.pallas_call(
        body, name="ffn_up", grid=(t // tm, nf),
        in_specs=[pl.BlockSpec((tm, D_MODEL), lambda i, j: (i, 0)),
                  pl.BlockSpec((1, D_MODEL), lambda i, j: (0, 0)),
                  pl.BlockSpec((D_MODEL, tf), lambda i, j: (0, j)),
                  pl.BlockSpec((D_MODEL, tf), lambda i, j: (0, nf + j))],
        out_specs=[ff, ff, ff, pl.BlockSpec((tm, D_MODEL), lambda i, j: (i, 0))],
        out_shape=[jax.ShapeDtypeStruct((t, D_FF), BF16)] * 3 + [jax.ShapeDtypeStruct((t, D_MODEL), BF16)],
        compiler_params=_cp(("parallel", "arbitrary"), VMEM_LIMIT),
    )(x1, g, w_gu, w_gu)


def matmul_residual(a, w, res, name, tm=512, tk=FFN_TF):
    t, k = a.shape
    n = w.shape[1]
    tm = min(tm, t)

    def body(a_ref, w_ref, r_ref, o_ref):
        @pl.when(pl.program_id(1) == 0)
        def _():
            o_ref[...] = r_ref[...]

        o_ref[...] += _nn(a_ref[...], w_ref[...])

    return pl.pallas_call(
        body, name=name, grid=(t // tm, k // tk),
        in_specs=[pl.BlockSpec((tm, tk), lambda i, j: (i, j)),
                  pl.BlockSpec((tk, n), lambda i, j: (j, 0)),
                  pl.BlockSpec((tm, n), lambda i, j: (i, 0))],
        out_specs=pl.BlockSpec((tm, n), lambda i, j: (i, 0)),
        out_shape=jax.ShapeDtypeStruct((t, n), F32),
        compiler_params=_cp(("parallel", "arbitrary"), VMEM_LIMIT),
    )(a, w, res)


def ffn_act_bwd(dx2, gate, up, w_down, tm=512, tf=FFN_TF):
    t = dx2.shape[0]
    tm = min(tm, t)

    def body(dx2_ref, gate_ref, up_ref, wd_ref, dgate_ref, dup_ref, dx2b_ref):
        @pl.when(pl.program_id(1) == 0)
        def _():
            dx2b_ref[...] = _bf(dx2_ref[...])

        dact = _nt(dx2b_ref[...], wd_ref[...])
        gt, upv = gate_ref[...].astype(F32), up_ref[...].astype(F32)
        sg = _sigmoid(gt)
        dgate_ref[...] = _bf(dact * upv * (sg * (1.0 + gt * (1.0 - sg))))
        dup_ref[...] = _bf(dact * (gt * sg))

    ff = pl.BlockSpec((tm, tf), lambda i, j: (i, j))
    return pl.pallas_call(
        body, name="ffn_act_bwd", grid=(t // tm, D_FF // tf),
        in_specs=[pl.BlockSpec((tm, D_MODEL), lambda i, j: (i, 0)), ff, ff,
                  pl.BlockSpec((tf, D_MODEL), lambda i, j: (j, 0))],
        out_specs=[ff, ff],
        out_shape=[jax.ShapeDtypeStruct((t, D_FF), BF16)] * 2,
        scratch_shapes=[pltpu.VMEM((tm, D_MODEL), BF16)],
        compiler_params=_cp(("parallel", "arbitrary"), VMEM_LIMIT),
    )(dx2, gate, up, w_down)


def tail_fwd_bwd(x2, p, target, g_ple, g_final, w_pg, w_pp, tm=256):
    t = x2.shape[0]
    tm = min(tm, t)

    def body(x_ref, p_ref, t_ref, gp_ref, gf_ref, wpg_ref, wpp_ref,
             dx_ref, dwpg_ref, dwpp_ref, dgp_ref, dgf_ref, loss_ref):
        @pl.when(pl.program_id(0) == 0)
        def _():
            dwpg_ref[...] = jnp.zeros_like(dwpg_ref)
            dwpp_ref[...] = jnp.zeros_like(dwpp_ref)
            dgp_ref[...] = jnp.zeros_like(dgp_ref)
            dgf_ref[...] = jnp.zeros_like(dgf_ref)
            loss_ref[...] = jnp.zeros_like(loss_ref)

        x2v = x_ref[...]
        gp, gf = gp_ref[...], gf_ref[...]
        r3, n3 = _rms(x2v)
        h3b = _bf(n3 * gp)
        pb = _bf(p_ref[...])
        pg = _sigmoid(_nn(h3b, wpg_ref[...]))
        pp = _nn(pb, wpp_ref[...])
        x3 = x2v + pg * pp
        r4, n4 = _rms(x3)
        err = n4 * gf - t_ref[...]
        part = 0.5 * jnp.sum(jnp.sum(err * err, axis=1, keepdims=True), axis=0, keepdims=True) / D_MODEL
        loss_ref[...] += jnp.broadcast_to(part, (8, 128))
        dy = err * (1.0 / D_MODEL)
        dx3, dgf = _rms_bwd(dy, gf, r4, n4)
        dgf_ref[...] += jnp.broadcast_to(jnp.sum(dgf, axis=0, keepdims=True), (8, D_MODEL))
        dzp = _bf(dx3 * pp * pg * (1.0 - pg))
        dpp = _bf(dx3 * pg)
        dwpg_ref[...] += _tn(h3b, dzp)
        dwpp_ref[...] += _tn(pb, dpp)
        dh3 = _nt(dzp, wpg_ref[...])
        dx, dgp = _rms_bwd(dh3, gp, r3, n3)
        dgp_ref[...] += jnp.broadcast_to(jnp.sum(dgp, axis=0, keepdims=True), (8, D_MODEL))
        dx_ref[...] = dx3 + dx

    rowd = pl.BlockSpec((tm, D_MODEL), lambda i: (i, 0))
    fixed = lambda shp: pl.BlockSpec(shp, lambda i: (0,) * len(shp))
    return pl.pallas_call(
        body, name="tail_fwd_bwd", grid=(t // tm,),
        in_specs=[rowd, pl.BlockSpec((tm, PLE_DIM), lambda i: (i, 0)), rowd,
                  fixed((1, D_MODEL)), fixed((1, D_MODEL)), fixed((D_MODEL, D_MODEL)), fixed((PLE_DIM, D_MODEL))],
        out_specs=[rowd, fixed((D_MODEL, D_MODEL)), fixed((PLE_DIM, D_MODEL)),
                   fixed((8, D_MODEL)), fixed((8, D_MODEL)), fixed((8, 128))],
        out_shape=[jax.ShapeDtypeStruct((t, D_MODEL), F32), jax.ShapeDtypeStruct((D_MODEL, D_MODEL), F32),
                   jax.ShapeDtypeStruct((PLE_DIM, D_MODEL), F32), jax.ShapeDtypeStruct((8, D_MODEL), F32),
                   jax.ShapeDtypeStruct((8, D_MODEL), F32), jax.ShapeDtypeStruct((8, 128), F32)],
        compiler_params=_cp(("arbitrary",), VMEM_LIMIT),
    )(x2, p, target, g_ple, g_final, w_pg, w_pp)


def in_proj_bwd(pieces, weights, x, dx1, g, name="in_proj_bwd", tm=256):
    t = x.shape[0]
    tm = min(tm, t)
    k = len(pieces)
    assert all(c0 % wd == 0 and w0 % wd == 0 for (_, c0, wd), (_, w0) in zip(pieces, weights))

    def body(*refs):
        p_refs, w_refs = refs[:k], refs[k:2 * k]
        x_ref, dx1_ref, g_ref, dx_ref, dg_ref = refs[2 * k:]

        @pl.when(pl.program_id(0) == 0)
        def _():
            dg_ref[...] = jnp.zeros_like(dg_ref)

        dh = _nt(_bf(p_refs[0][...]), w_refs[0][...])
        for pr, wr in zip(p_refs[1:], w_refs[1:]):
            dh = dh + _nt(_bf(pr[...]), wr[...])
        r, n = _rms(x_ref[...])
        dx, dgc = _rms_bwd(dh, g_ref[...], r, n)
        dx_ref[...] = dx1_ref[...] + dx
        dg_ref[...] += jnp.broadcast_to(jnp.sum(dgc, axis=0, keepdims=True), (8, D_MODEL))

    rowd = pl.BlockSpec((tm, D_MODEL), lambda i: (i, 0))
    return pl.pallas_call(
        body, name=name, grid=(t // tm,),
        in_specs=[pl.BlockSpec((tm, wd), functools.partial(lambda i, cb: (i, cb), cb=c0 // wd))
                  for _, c0, wd in pieces]
        + [pl.BlockSpec((w.shape[0], wd), functools.partial(lambda i, cb: (0, cb), cb=w0 // wd))
           for (w, w0), (_, _, wd) in zip(weights, pieces)]
        + [rowd, rowd, pl.BlockSpec((1, D_MODEL), lambda i: (0, 0))],
        out_specs=[rowd, pl.BlockSpec((8, D_MODEL), lambda i: (0, 0))],
        out_shape=[jax.ShapeDtypeStruct((t, D_MODEL), F32), jax.ShapeDtypeStruct((8, D_MODEL), F32)],
        compiler_params=_cp(("arbitrary",), VMEM_LIMIT),
    )(*[a for a, _, _ in pieces], *[w for w, _ in weights], x, dx1, g)


def adamw(w, g, m, v, name, rows_cap=256, dep=None):
    lead = w.shape[:-2]
    r, c = w.shape[-2:]
    tr = r
    for cand in range(8, min(r, rows_cap) + 1, 8):
        if r % cand == 0:
            tr = cand

    def body(w_ref, g_ref, m_ref, v_ref, *rest):
        d_ref, mo_ref, vo_ref = rest[-3:]
        gv = g_ref[...]
        mn = ADAM_B1 * m_ref[...] + (1.0 - ADAM_B1) * gv
        vn = ADAM_B2 * v_ref[...] + (1.0 - ADAM_B2) * (gv * gv)
        m_hat = mn / (1.0 - ADAM_B1 ** ADAM_STEP)
        v_hat = vn / (1.0 - ADAM_B2 ** ADAM_STEP)
        d_ref[...] = -ADAM_LR * (m_hat / (jnp.sqrt(v_hat) + ADAM_EPS) + ADAM_WD * w_ref[...])
        mo_ref[...] = mn
        vo_ref[...] = vn

    spec = pl.BlockSpec((None,) * len(lead) + (tr, c), lambda i: (0,) * len(lead) + (i, 0))
    extra = [] if dep is None else [dep]
    return pl.pallas_call(
        body, name=name, grid=(r // tr,),
        in_specs=[spec] * 4 + [pl.BlockSpec((8, 128), lambda i: (0, 0))] * len(extra), out_specs=[spec] * 3,
        out_shape=[jax.ShapeDtypeStruct(w.shape, F32)] * 3,
        compiler_params=_cp(("parallel",), VMEM_LIMIT),
    )(w, g.reshape(w.shape), m, v, *extra)


class Standalone:
    def __init__(self, later_weights):
        self.later_weights = later_weights

    def begin(self, *a):
        return 0.0

    forward = exchange = join = begin

    def finish(self, after):
        return self.later_weights


def local_step(x3d, p3d, target3d, g4, small, later, early):
    b, s, _ = x3d.shape
    t = b * s
    x = x3d.reshape(t, D_MODEL)
    p = p3d.reshape(t, PLE_DIM)
    target = target3d.reshape(t, D_MODEL)
    cut = SPLIT_Z - 2 * (D_IN // N_CHIPS)
    w_inp = jnp.concatenate([g4[2][:, cut + 8:], g4[3], g4[0], g4[1], g4[2][:, :cut], g4[2][:, cut:cut + 8],
                             jnp.zeros((D_MODEL, 120), BF16)], axis=1)
    al_row = jnp.pad(small["a_log"].reshape(1, B_HEADS), ((0, 0), (0, 128 - B_HEADS)))
    dtb_row = jnp.pad(small["dt_bias"].reshape(1, B_HEADS), ((0, 0), (0, 128 - B_HEADS)))
    conv_w8 = jnp.pad(small["conv_w"].reshape(CONV_K, CONV_CH), ((0, 8 - CONV_K), (0, 0)))
    w_on = small["w_onorm"].reshape(1, B_DIM)
    g_mix, g_ffn = small["g_mix"].reshape(1, D_MODEL), small["g_ffn"].reshape(1, D_MODEL)
    g_ple, g_final = small["g_ple"].reshape(1, D_MODEL), small["g_final"].reshape(1, D_MODEL)
    bias_band = make_bias_band(small["rel_bias"].reshape(A_HEADS, N_REL))

    tok = later.begin()
    proj, h1, bd32 = rms_matmul(x, g_mix + tok, w_inp, "in_proj", tm=1024)
    y_a, lse = attn_fwd(proj, bias_band, b, s)
    tok = later.forward(lse)
    c = conv_fwd(proj, conv_w8 + tok, b, s)
    u, wk, qg, kdec, pm, egl, tmat = dn_prep(c, bd32, al_row, dtb_row, b, s)
    o_b, states = dn_scan_fwd(u, wk, qg, kdec, pm, egl, b, s)
    wts = later.finish(o_b)
    x1, merged = mid_fwd(x, y_a, o_b, proj, w_on, wts["w_branch_a"], wts["w_branch_b"], wts["w_out"])
    gate, up, act, h2 = ffn_up(x1, g_ffn, wts["w_gate_up"])
    x2 = matmul_residual(act, wts["w_down"], x1, "ffn_down")

    dx2, dw_pg, dw_pp, dg_ple, dg_final, loss = tail_fwd_bwd(
        x2, p, target, g_ple, g_final, wts["w_ple_gate"], wts["w_ple_proj"])
    dgate, dup = ffn_act_bwd(dx2, gate, up, wts["w_down"])
    w_gu = wts["w_gate_up"]
    dx1, dg_ffn = in_proj_bwd([(dgate, 0, D_FF), (dup, 0, D_FF)], [(w_gu, 0), (w_gu, D_FF)], x1, dx2, g_ffn,
                              name="ffn_in_bwd")
    dw_down = matmul_tn(act, dx2, "dw_down")
    dw_gu = matmul_tn(h2, dgate, "dw_gate", width=2 * D_FF)
    dw_gu = matmul_tn(h2, dup, "dw_up", into=dw_gu, col0=D_FF)
    dy_a, do_b, dz, dgates, dw_out, dwa, dwb, dw_on = mid_bwd(
        dx1, merged, y_a, o_b, proj, w_on, wts["w_branch_a"], wts["w_branch_b"], wts["w_out"])
    tok = early.begin(dict(w_branch_a=dwa, w_branch_b=dwb, w_out=dw_out, w_gate_up=dw_gu, w_down=dw_down,
                           w_ple_gate=dw_pg, w_ple_proj=dw_pp))
    ddw, ddwk, ddqg, ddkdec, ddp, ddegl = dn_scan_bwd(u, wk, qg, kdec, pm, egl + tok, states, do_b, b, s)
    tok = early.exchange(ddegl)
    dc3, dbd, dal, ddtb = dn_post_bwd(c, bd32, al_row + tok, dtb_row, tmat, ddw, ddwk, ddqg, ddkdec, ddp, ddegl, b, s)
    dconv, dconv_w = conv_bwd(proj, conv_w8, dc3, b, s)
    dqa, dka, dva, dbt, dbf = attn_bwd(proj, bias_band, y_a, lse, dy_a, b, s)
    tok = early.join(dqa)
    d_rel = bias_band_grad(dbt, dbf)

    pieces = [dgates, dqa, dka, dva, dconv, dz, dbd]
    bounds = [0, 2048, 2560, 3072, 3584, 5120, 5632, 5760]
    windows = [(dgates, 0, 2048), (dqa, 0, 512), (dka, 0, 512), (dva, 0, 512), (dconv, 0, 512), (dconv, 512, 512),
               (dconv, 1024, 512), (dz, 0, 512), (dbd, 0, 128)]
    w_cols = [0, P_QA, P_KA, P_VA, P_CONV, P_CONV + 512, P_CONV + 1024, P_Z, P_BD]
    dx, dg_mix = in_proj_bwd(windows, [(w_inp, c0) for c0 in w_cols], x, dx1, g_mix + tok)
    dwp = None
    for k, pc in enumerate(pieces):
        dwp = matmul_tn(h1, pc, "dw_in_%d" % k, into=dwp, col0=bounds[k], width=P_WIDTH)
    reduced_early = early.finish(dwp)
    dw_in = jnp.concatenate([dwp[:, P_QA:P_BD + 8], dwp[:, :P_QA]], axis=1)

    grads = dict(w_in=dw_in, w_branch_a=dwa, w_branch_b=dwb, w_out=dw_out, w_gate_up=dw_gu, w_down=dw_down,
                 w_ple_gate=dw_pg, w_ple_proj=dw_pp)
    small_grads = dict(g_mix=dg_mix[0], g_ffn=dg_ffn[0], g_ple=dg_ple[0], g_final=dg_final[0],
                       conv_w=dconv_w[:CONV_K].reshape(-1), rel_bias=d_rel.reshape(-1), w_onorm=dw_on[0],
                       a_log=dal[0, :B_HEADS], dt_bias=ddtb[0, :B_HEADS], loss=loss[0, :1])
    return dx.reshape(b, s, D_MODEL), grads, small_grads, reduced_early


BIG = (("w_in", (D_MODEL, D_IN), 1), ("w_branch_a", (A_WIDTH, D_MODEL), 1), ("w_branch_b", (B_WIDTH, D_MODEL), 1),
       ("w_out", (D_MODEL, D_MODEL), 0), ("w_gate_up", (D_MODEL, 2 * D_FF), 1), ("w_down", (D_FF, D_MODEL), 0),
       ("w_ple_gate", (D_MODEL, D_MODEL), 0), ("w_ple_proj", (PLE_DIM, D_MODEL), 1))
N_CHIPS = 4
FIRST_WEIGHTS = ("w_in",)
LATER_WEIGHTS = ("w_branch_a", "w_branch_b", "w_out", "w_gate_up", "w_down", "w_ple_gate", "w_ple_proj")
LATE_GRADS = ("w_in",)
EARLY_GRADS = ("w_branch_a", "w_branch_b", "w_out", "w_gate_up", "w_down", "w_ple_gate", "w_ple_proj")


def _items(names):
    return [it for it in BIG if it[0] in names]


def _shard_shape(shape, axis):
    return (shape[0] // N_CHIPS, shape[1]) if axis == 0 else (shape[0], shape[1] // N_CHIPS)


def _pack_rows_of(names):
    return -(-sum(sh[0] * sh[1] for _, sh, _ in _items(names)) // (N_CHIPS * 128 * 512)) * 512


def _pack_rows(parts, total):
    used = sum(a.shape[-2] for a in parts)
    pad = jnp.zeros(parts[0].shape[:-2] + (total - used, 128), parts[0].dtype)
    return jnp.concatenate(parts + [pad], axis=-2)


def pack_grads(grads, names):
    parts = []
    for n, shape, axis in _items(names):
        rs, cs = _shard_shape(shape, axis)
        g = grads[n].astype(BF16)
        seg = g.reshape(N_CHIPS, rs, cs) if axis == 0 else jnp.transpose(g.reshape(rs, N_CHIPS, cs), (1, 0, 2))
        parts.append(seg.reshape(N_CHIPS, -1, 128))
    return _pack_rows(parts, _pack_rows_of(names))


def unpack_shard(flat, names):
    out, r0 = {}, 0
    for n, shape, axis in _items(names):
        rs, cs = _shard_shape(shape, axis)
        nr = rs * cs // 128
        out[n] = flat[r0:r0 + nr].reshape(rs, cs)
        r0 += nr
    return out


def _place():
    return lax.axis_index("x"), lax.axis_index("y"), lax.axis_index("c")


ANY = pl.BlockSpec(memory_space=pl.ANY)


def _gathered_shape(item):
    n, shape, _ = item
    return (N_CHIPS,) + _shard_shape(shape, 1) if n == "w_in" else shape


def _gather_block(o_ref, item, cx, cy, hf):
    n, shape, axis = item
    rs, cs = _shard_shape(shape, axis)
    hr = rs // 2
    ci = 2 * cx + cy
    if n == "w_in":
        return o_ref.at[ci, pl.ds(pl.multiple_of(hf * hr, 16), hr), :]
    if axis == 0:
        return o_ref.at[pl.ds(pl.multiple_of(ci * rs + hf * hr, 16), hr), :]
    return o_ref.at[pl.ds(pl.multiple_of(hf * hr, 16), hr), pl.ds(pl.multiple_of(ci * cs, 128), cs)]


def _own_half(w_ref, item, c):
    hr = _shard_shape(item[1], item[2])[0] // 2
    return w_ref.at[pl.ds(pl.multiple_of(c * hr, 16), hr), :]


def _gather_slot(o_ref, item, cx, cy):
    n, shape, axis = item
    rs, cs = _shard_shape(shape, axis)
    ci = 2 * cx + cy
    if n == "w_in":
        return o_ref.at[ci]
    if axis == 0:
        return o_ref.at[pl.ds(pl.multiple_of(ci * rs, 16), rs), :]
    return o_ref.at[:, pl.ds(pl.multiple_of(ci * cs, 128), cs)]


def _other_chips(x, y):
    return [(1 - x, y), (x, 1 - y), (1 - x, 1 - y)]


def allgather_weights(shards, names, chip):
    items = _items(names)
    nw = len(items)

    def body(*refs):
        w_refs, o_refs = refs[:nw], refs[nw:2 * nw]
        send_sems, recv_sems = refs[2 * nw:]
        x, y, c = _place()
        sibling = (x, y, 1 - c)
        chips = _other_chips(x, y)

        def copy(k, src, dst, to):
            return pltpu.make_async_remote_copy(src_ref=src, dst_ref=dst, send_sem=send_sems.at[k],
                                                recv_sem=recv_sems.at[k], device_id=to, device_id_type=MESH)

        def blk(i, cx, cy, hf):
            return _gather_block(o_refs[i], items[i], cx, cy, hf)

        def my_half(i):
            return _own_half(w_refs[i], items[i], c)

        def own(i):
            return _gather_slot(o_refs[i], items[i], x, y)

        first = [copy(7 * i + j, my_half(i), blk(i, x, y, c), (*chip_, c))
                 for i in range(nw) for j, chip_ in enumerate(chips)]
        first += [copy(7 * i + 6, w_refs[i], own(i), sibling) for i in range(nw)]
        for cp in first:
            cp.start()
        passed = []
        for i in range(nw):
            for j, chip_ in enumerate(chips):
                copy(7 * i + j, my_half(i), blk(i, *chip_, c), (*chip_, c)).wait_recv()
                fwd = copy(7 * i + 3 + j, blk(i, *chip_, c), blk(i, *chip_, c), sibling)
                fwd.start()
                passed.append(fwd)
        for i in range(nw):
            for j, chip_ in enumerate(chips):
                copy(7 * i + 3 + j, my_half(i), blk(i, *chip_, 1 - c), sibling).wait_recv()
            copy(7 * i + 6, w_refs[i], own(i), sibling).wait_recv()
        for cp in first + passed:
            cp.wait_send()

    outs = pl.pallas_call(
        body, name="allgather_weights",
        in_specs=[ANY] * nw, out_specs=[ANY] * nw,
        out_shape=[jax.ShapeDtypeStruct(_gathered_shape(it), BF16) for it in items],
        scratch_shapes=[pltpu.SemaphoreType.DMA((7 * nw,)), pltpu.SemaphoreType.DMA((7 * nw,))],
    )(*[shards[it[0]] for it in items])
    return {it[0]: o for it, o in zip(items, outs)}


HBM_SPEC = pl.BlockSpec(memory_space=pltpu.HBM)
SEM_SPEC = pl.BlockSpec(memory_space=pltpu.SEMAPHORE)
EFFECT = pltpu.SideEffectType.DATAFLOW_SIDE_EFFECTING


def _in_hbm(a):
    return pltpu.with_memory_space_constraint(a, pltpu.HBM)


def copies_start(name, bufs, ncopies, plan):
    nb = len(bufs)

    def body(*refs):
        in_refs, send_sems, recv_sems, token = refs[:nb], refs[nb], refs[nb + 1], refs[-1]
        for k, (src, dst, to) in enumerate(plan(in_refs)):
            pltpu.make_async_remote_copy(src_ref=src, dst_ref=dst, send_sem=send_sems.at[k],
                                         recv_sem=recv_sems.at[k], device_id=to, device_id_type=MESH).start()
        token[...] = jnp.zeros_like(token)

    outs = pl.pallas_call(
        body, name=name,
        in_specs=[HBM_SPEC] * nb,
        out_specs=(SEM_SPEC, SEM_SPEC, *[HBM_SPEC] * nb, pl.BlockSpec(memory_space=pltpu.VMEM)),
        out_shape=(pltpu.SemaphoreType.DMA((ncopies,)), pltpu.SemaphoreType.DMA((ncopies,)),
                   *[pltpu.HBM(b.shape, b.dtype) for b in bufs], jax.ShapeDtypeStruct((8, 128), F32)),
        input_output_aliases={i: 2 + i for i in range(nb)},
        compiler_params=pltpu.CompilerParams(has_side_effects=EFFECT),
    )(*[_in_hbm(b) for b in bufs])
    return outs[0], outs[1], list(outs[2:2 + nb]), outs[-1][0, 0]


def copies_wait(name, send_sems, recv_sems, bufs, after, plan):
    nb = len(bufs)

    def body(*refs):
        in_refs, s_sems, r_sems = refs[:nb], refs[nb], refs[nb + 1]
        for k, (src, dst, to) in enumerate(plan(in_refs)):
            cp = pltpu.make_async_remote_copy(src_ref=src, dst_ref=dst, send_sem=s_sems.at[k],
                                              recv_sem=r_sems.at[k], device_id=to, device_id_type=MESH)
            cp.wait_send()
            cp.wait_recv()

    return list(pl.pallas_call(
        body, name=name,
        in_specs=[HBM_SPEC] * nb + [SEM_SPEC, SEM_SPEC, ANY],
        out_specs=tuple([HBM_SPEC] * nb),
        out_shape=tuple(pltpu.HBM(b.shape, b.dtype) for b in bufs),
        input_output_aliases={i: i for i in range(nb)},
        compiler_params=pltpu.CompilerParams(has_side_effects=EFFECT),
    )(*bufs, send_sems, recv_sems, after))


def _landing(shape, dtype):
    return _in_hbm(lax.empty(shape, dtype))


class LaterWeights:
    def __init__(self, shards, chip):
        self.items = _items(LATER_WEIGHTS)
        self.shards, self.chip = shards, chip
        self.nw = len(self.items)

    def _ici_plan(self, refs):
        x, y, c = _place()
        w_refs, o_refs = refs[:self.nw], refs[self.nw:]
        plan = [(_own_half(w_refs[i], it, c), _gather_block(o_refs[i], it, x, y, c), (*chip_, c))
                for i, it in enumerate(self.items) for chip_ in _other_chips(x, y)]
        return plan + [(w_refs[i], _gather_slot(o_refs[i], it, x, y), (x, y, 1 - c))
                       for i, it in enumerate(self.items)]

    def _d2d_plan(self, refs):
        x, y, c = _place()
        return [(_gather_block(refs[i], it, *chip_, c), _gather_block(refs[i], it, *chip_, c), (x, y, 1 - c))
                for i, it in enumerate(self.items) for chip_ in _other_chips(x, y)]

    def _d2d_wait_plan(self, refs):
        x, y, c = _place()
        return [(_gather_block(refs[i], it, *chip_, c), _gather_block(refs[i], it, *chip_, 1 - c), (x, y, 1 - c))
                for i, it in enumerate(self.items) for chip_ in _other_chips(x, y)]

    def _ici_wait_plan(self, refs):
        x, y, c = _place()
        w_refs, o_refs = refs[:self.nw], refs[self.nw:]
        plan = [(_own_half(w_refs[i], it, c), _gather_block(o_refs[i], it, *chip_, c), (*chip_, c))
                for i, it in enumerate(self.items) for chip_ in _other_chips(x, y)]
        return plan + [(w_refs[i], _gather_slot(o_refs[i], it, x, y), (x, y, 1 - c))
                       for i, it in enumerate(self.items)]

    def begin(self):
        srcs = [self.shards[it[0]] for it in self.items]
        lands = [_landing(_gathered_shape(it), BF16) for it in self.items]
        self.s1, self.r1, self.b1, tok = copies_start("gather_ici_start", srcs + lands, 4 * self.nw, self._ici_plan)
        return tok

    def forward(self, after):
        b1 = copies_wait("gather_ici_wait", self.s1, self.r1, self.b1, after, self._ici_wait_plan)
        self.s2, self.r2, self.b2, tok = copies_start("gather_d2d_start", b1[self.nw:], 3 * self.nw, self._d2d_plan)
        return tok

    def finish(self, after):
        outs = copies_wait("gather_d2d_wait", self.s2, self.r2, self.b2, after, self._d2d_wait_plan)
        return {it[0]: o for it, o in zip(self.items, outs)}


def small_allreduce(v, name):
    r = v.shape[0]

    def body(v_ref, o_ref, buf, send_sems, recv_sems):
        x, y, c = _place()
        me = 4 * x + 2 * y + c
        buf[me] = v_ref[...]
        flips = [(fx, fy, fc) for fx in (0, 1) for fy in (0, 1) for fc in (0, 1)][1:]
        peers = [((1 - x) if fx else x, (1 - y) if fy else y, (1 - c) if fc else c) for fx, fy, fc in flips]

        def copy(k, slot, to):
            return pltpu.make_async_remote_copy(src_ref=v_ref, dst_ref=buf.at[slot], send_sem=send_sems.at[k],
                                                recv_sem=recv_sems.at[k], device_id=to, device_id_type=MESH)

        sends = [copy(k, me, peer) for k, peer in enumerate(peers)]
        for cp in sends:
            cp.start()
        for k, (px, py, pc) in enumerate(peers):
            copy(k, 4 * px + 2 * py + pc, (px, py, pc)).wait_recv()
        for cp in sends:
            cp.wait_send()
        acc = buf[0]
        for d in range(1, 8):
            acc = acc + buf[d]
        o_ref[...] = acc

    return pl.pallas_call(
        body, name=name,
        in_specs=[pl.BlockSpec(memory_space=pltpu.VMEM)], out_specs=pl.BlockSpec(memory_space=pltpu.VMEM),
        out_shape=jax.ShapeDtypeStruct((r, 128), F32),
        scratch_shapes=[pltpu.VMEM((8, r, 128), F32), pltpu.SemaphoreType.DMA((7,)), pltpu.SemaphoreType.DMA((7,))],
    )(v)


def swap_halves(g):
    half = g.shape[1] // 2

    def body(g_ref, o_ref, send_sem, recv_sem):
        x, y, c = _place()
        cp = pltpu.make_async_remote_copy(
            src_ref=g_ref.at[:, pl.ds((1 - c) * half, half), :], dst_ref=o_ref, send_sem=send_sem,
            recv_sem=recv_sem, device_id=(x, y, 1 - c), device_id_type=MESH)
        cp.start()
        cp.wait()

    return pl.pallas_call(
        body, name="swap_halves", in_specs=[ANY], out_specs=ANY,
        out_shape=jax.ShapeDtypeStruct((N_CHIPS, half, 128), g.dtype),
        scratch_shapes=[pltpu.SemaphoreType.DMA, pltpu.SemaphoreType.DMA],
    )(g)


def add_halves(g, other, place):
    half = other.shape[1]
    tr = _tile_rows(half)
    nblk = half // tr

    def body(pref, g0, g1, g2, g3, o0, o1, o2, o3, pf_ref, pb_ref):
        f = lambda r: r[...].astype(F32)
        pf_ref[...] = f(g0) + f(o0)
        pb_ref[0] = _bf(f(g1) + f(o1))
        pb_ref[1] = _bf(f(g2) + f(o2))
        pb_ref[2] = _bf(f(g3) + f(o3))

    gspec = lambda k: pl.BlockSpec((None, tr, 128), lambda i, pr: ((pr[0] + k) % N_CHIPS, pr[1] * nblk + i, 0))
    ospec = lambda k: pl.BlockSpec((None, tr, 128), lambda i, pr: ((pr[0] + k) % N_CHIPS, i, 0))
    return pl.pallas_call(
        body, name="add_halves",
        grid_spec=pltpu.PrefetchScalarGridSpec(
            num_scalar_prefetch=1, grid=(nblk,),
            in_specs=[gspec(0), gspec(1), gspec(2), gspec(3), ospec(0), ospec(1), ospec(2), ospec(3)],
            out_specs=[pl.BlockSpec((tr, 128), lambda i, pr: (i, 0)),
                       pl.BlockSpec((3, tr, 128), lambda i, pr: (0, i, 0))]),
        out_shape=[jax.ShapeDtypeStruct((half, 128), F32), jax.ShapeDtypeStruct((3, half, 128), BF16)],
        compiler_params=_cp(("parallel",), VMEM_LIMIT),
    )(place, g, g, g, g, other, other, other, other)


def _tile_rows(n, cap=2048):
    best = 16
    for t in range(16, cap + 1, 16):
        if n % t == 0:
            best = t
    assert n % best == 0
    return best


def exchange_partials(pb):
    def body(p_ref, o_ref, send_sems, recv_sems):
        x, y, c = _place()
        me = 2 * x + y
        cps = []
        for k in range(1, N_CHIPS):
            to = (me + k) % N_CHIPS
            cps.append(pltpu.make_async_remote_copy(
                src_ref=p_ref.at[k - 1], dst_ref=o_ref.at[k - 1], send_sem=send_sems.at[k - 1],
                recv_sem=recv_sems.at[k - 1], device_id=(to // 2, to % 2, c), device_id_type=MESH))
        for cp in cps:
            cp.start()
        for cp in cps:
            cp.wait()

    return pl.pallas_call(
        body, name="exchange_partials", in_specs=[ANY], out_specs=ANY,
        out_shape=jax.ShapeDtypeStruct(pb.shape, pb.dtype),
        scratch_shapes=[pltpu.SemaphoreType.DMA((3,)), pltpu.SemaphoreType.DMA((3,))],
    )(pb)


def add_partials(pf, got, place):
    half = pf.shape[0]
    tr = _tile_rows(half)

    def body(pref, pf_ref, got_ref, o_ref):
        o_ref[...] = ((pf_ref[...] + got_ref[0].astype(F32)) + got_ref[1].astype(F32)) + got_ref[2].astype(F32)

    return pl.pallas_call(
        body, name="add_partials",
        grid_spec=pltpu.PrefetchScalarGridSpec(
            num_scalar_prefetch=1, grid=(half // tr,),
            in_specs=[pl.BlockSpec((tr, 128), lambda i, pr: (i, 0)),
                      pl.BlockSpec((3, tr, 128), lambda i, pr: (0, i, 0))],
            out_specs=pl.BlockSpec((None, tr, 128), lambda i, pr: (pr[1], i, 0))),
        out_shape=jax.ShapeDtypeStruct((2, half, 128), F32),
        compiler_params=_cp(("parallel",), VMEM_LIMIT),
    )(place, pf, got)


def join_halves(both):
    def body(r_ref, o_ref, send_sem, recv_sem):
        x, y, c = _place()
        cp = pltpu.make_async_remote_copy(src_ref=r_ref.at[c], dst_ref=o_ref.at[c], send_sem=send_sem,
                                          recv_sem=recv_sem, device_id=(x, y, 1 - c), device_id_type=MESH)
        cp.start()
        pltpu.make_async_remote_copy(src_ref=r_ref.at[c], dst_ref=o_ref.at[1 - c], send_sem=send_sem,
                                     recv_sem=recv_sem, device_id=(x, y, 1 - c), device_id_type=MESH).wait_recv()
        cp.wait_send()

    return pl.pallas_call(
        body, name="join_halves", in_specs=[ANY], out_specs=ANY,
        out_shape=jax.ShapeDtypeStruct(both.shape, F32),
        scratch_shapes=[pltpu.SemaphoreType.DMA, pltpu.SemaphoreType.DMA],
        input_output_aliases={0: 0},
    )(both)


def reduce_scatter_grads(gpack, place):
    other = swap_halves(gpack)
    pf, pb = add_halves(gpack, other, place)
    got = exchange_partials(pb)
    return join_halves(add_partials(pf, got, place)).reshape(gpack.shape[1], 128)


class EarlyGrads:
    def __init__(self, place, names=EARLY_GRADS, tag="grads"):
        self.place, self.names, self.tag = place, names, tag

    @staticmethod
    def _swap_plan(refs):
        x, y, c = _place()
        g_ref, o_ref = refs
        half = o_ref.shape[1]
        return [(g_ref.at[:, pl.ds(pl.multiple_of((1 - c) * half, 16), half), :], o_ref, (x, y, 1 - c))]

    @staticmethod
    def _exchange_plan(refs):
        x, y, c = _place()
        p_ref, o_ref = refs
        me = 2 * x + y
        return [(p_ref.at[k - 1], o_ref.at[k - 1], (((me + k) % N_CHIPS) // 2, ((me + k) % N_CHIPS) % 2, c))
                for k in range(1, N_CHIPS)]

    @staticmethod
    def _join_plan(refs):
        x, y, c = _place()
        return [(refs[0].at[c], refs[0].at[c], (x, y, 1 - c))]

    @staticmethod
    def _join_wait_plan(refs):
        x, y, c = _place()
        return [(refs[0].at[c], refs[0].at[1 - c], (x, y, 1 - c))]

    def begin(self, grads):
        g = pack_grads(grads, self.names)
        land = _landing((N_CHIPS, g.shape[1] // 2, 128), BF16)
        self.s1, self.r1, self.b1, tok = copies_start(self.tag + "_swap_start", [g, land], 1, self._swap_plan)
        return tok

    def exchange(self, after):
        g, other = copies_wait(self.tag + "_swap_wait", self.s1, self.r1, self.b1, after, self._swap_plan)
        self.pf, pb = add_halves(g, other, self.place)
        land = _landing(pb.shape, BF16)
        self.s2, self.r2, self.b2, tok = copies_start(self.tag + "_exchange_start", [pb, land], 3,
                                                      self._exchange_plan)
        return tok

    def join(self, after):
        _, got = copies_wait(self.tag + "_exchange_wait", self.s2, self.r2, self.b2, after, self._exchange_plan)
        both = add_partials(self.pf, got, self.place)
        self.s3, self.r3, self.b3, tok = copies_start(self.tag + "_join_start", [both], 1, self._join_plan)
        return tok

    def finish(self, after):
        (both,) = copies_wait(self.tag + "_join_wait", self.s3, self.r3, self.b3, after, self._join_wait_plan)
        return both.reshape(-1, 128)


SMALL = (("g_mix", D_MODEL), ("g_ffn", D_MODEL), ("g_ple", D_MODEL), ("g_final", D_MODEL),
         ("conv_w", CONV_K * CONV_CH), ("rel_bias", A_HEADS * N_REL), ("w_onorm", B_DIM),
         ("a_log", B_HEADS), ("dt_bias", B_HEADS), ("loss", 1))


def _pad128(v):
    v = v.reshape(-1)
    return jnp.pad(v, (0, -v.shape[0] % 128))


def pack_small(d, names, rows):
    flat = jnp.concatenate([_pad128(d[n]) for n in names]).reshape(-1, 128)
    return jnp.pad(flat, ((0, rows - flat.shape[0]), (0, 0)))


def unpack_small(flat, names_sizes):
    out, r0 = {}, 0
    v = flat.reshape(-1)
    for n, size in names_sizes:
        out[n] = v[r0:r0 + size]
        r0 += -(-size // 128) * 128
    return out


def kernel(x, p, g_mix, w_in, conv_w, a_log, dt_bias, rel_bias, w_onorm, w_branch_a, w_branch_b, w_out, g_ffn, w_gate_up, w_down, g_ple, w_ple_gate, w_ple_proj, g_final, loss_target, m_g_mix, m_w_in, m_conv_w, m_a_log, m_dt_bias, m_rel_bias, m_w_onorm, m_w_branch_a, m_w_branch_b, m_w_out, m_g_ffn, m_w_gate_up, m_w_down, m_g_ple, m_w_ple_gate, m_w_ple_proj, m_g_final, v_g_mix, v_w_in, v_conv_w, v_a_log, v_dt_bias, v_rel_bias, v_w_onorm, v_w_branch_a, v_w_branch_b, v_w_out, v_g_ffn, v_w_gate_up, v_w_down, v_g_ple, v_w_ple_gate, v_w_ple_proj, v_g_final):
    names = ["g_mix", "w_in", "conv_w", "a_log", "dt_bias", "rel_bias", "w_onorm", "w_branch_a", "w_branch_b",
             "w_out", "g_ffn", "w_gate_up", "w_down", "g_ple", "w_ple_gate", "w_ple_proj", "g_final"]
    w = dict(zip(names, [g_mix, w_in, conv_w, a_log, dt_bias, rel_bias, w_onorm, w_branch_a, w_branch_b, w_out,
                         g_ffn, w_gate_up, w_down, g_ple, w_ple_gate, w_ple_proj, g_final]))
    m = dict(zip(names, [m_g_mix, m_w_in, m_conv_w, m_a_log, m_dt_bias, m_rel_bias, m_w_onorm, m_w_branch_a,
                         m_w_branch_b, m_w_out, m_g_ffn, m_w_gate_up, m_w_down, m_g_ple, m_w_ple_gate,
                         m_w_ple_proj, m_g_final]))
    v = dict(zip(names, [v_g_mix, v_w_in, v_conv_w, v_a_log, v_dt_bias, v_rel_bias, v_w_onorm, v_w_branch_a,
                         v_w_branch_b, v_w_out, v_g_ffn, v_w_gate_up, v_w_down, v_g_ple, v_w_ple_gate,
                         v_w_ple_proj, v_g_final]))
    xi, yi, ci = _place()
    chip = 2 * xi + yi
    big_names = [n for n, _, _ in BIG]

    shards2d = {n: w[n].reshape(w[n].shape[-2:]) for n in big_names}
    shards_bf = {n: a.astype(BF16) for n, a in shards2d.items()}
    g4 = allgather_weights(shards_bf, FIRST_WEIGHTS, chip)["w_in"]
    place = jnp.stack([chip, ci]).astype(jnp.int32)
    conv_sh = jnp.where(ci == 0, w["conv_w"].reshape(CONV_K, CONV_CH // N_CHIPS), 0.0)
    conv_slots = lax.dynamic_update_slice(jnp.zeros((N_CHIPS, CONV_K, CONV_CH // N_CHIPS), F32), conv_sh[None],
                                          (chip, 0, 0))
    conv_all = small_allreduce(conv_slots.reshape(-1, 128), "gather_conv_w")
    conv_full = jnp.transpose(conv_all.reshape(N_CHIPS, CONV_K, CONV_CH // N_CHIPS), (1, 0, 2)).reshape(CONV_K, CONV_CH)
    small = {n: w[n] for n in names if n not in big_names}
    small["conv_w"] = conv_full

    grad_x, grads, small_grads, reduced_early = local_step(
        x, p[0], loss_target, g4, small, LaterWeights(shards_bf, chip), EarlyGrads(place))

    late = EarlyGrads(place, LATE_GRADS, "late")
    late.begin(grads)
    dep = jnp.full((8, 128), late.exchange(grads["w_in"]), F32)
    gshard = unpack_shard(reduced_early, EARLY_GRADS)
    small_names = [n for n, _ in SMALL]
    red = unpack_small(small_allreduce(pack_small(small_grads, small_names, 112), "allreduce_small"), SMALL)
    loss = red["loss"][0]
    conv_g = lax.dynamic_slice(red["conv_w"].reshape(CONV_K, N_CHIPS, CONV_CH // N_CHIPS), (0, chip, 0),
                               (CONV_K, 1, CONV_CH // N_CHIPS))
    gsmall = {n: red[n].reshape(w[n].shape) for n in small_names if n not in ("loss", "conv_w")}
    gsmall["conv_w"] = conv_g.reshape(w["conv_w"].shape)

    grad, delta, new_m, new_v = {}, {}, {}, {}
    for n in list(EARLY_GRADS) + list(LATE_GRADS):
        if n in LATE_GRADS:
            late.join(v_)
            gshard.update(unpack_shard(late.finish(v_), LATE_GRADS))
        shp = w[n].shape
        d_, m_, v_ = adamw(shards2d[n], gshard[n], m[n].reshape(shp[-2:]), v[n].reshape(shp[-2:]), "adamw_" + n,
                           dep=dep if n in EARLY_GRADS else None)
        dep, v_ = lax.optimization_barrier((dep, v_))
        grad[n], delta[n], new_m[n], new_v[n] = gshard[n].reshape(shp), d_.reshape(shp), m_.reshape(shp), v_.reshape(shp)
    snames = [n for n in small_names if n != "loss"]
    ssizes = [(n, w[n].size) for n in snames]
    pk = lambda d: pack_small(d, snames, 64)
    d_, m_, v_ = adamw(pk(w), pk(gsmall), pk(m), pk(v), "adamw_small")
    ds, ms, vs = unpack_small(d_, ssizes), unpack_small(m_, ssizes), unpack_small(v_, ssizes)
    for n in snames:
        shp = w[n].shape
        grad[n], delta[n], new_m[n], new_v[n] = gsmall[n], ds[n].reshape(shp), ms[n].reshape(shp), vs[n].reshape(shp)

    return (loss, grad_x, *[grad[n] for n in names], *[delta[n] for n in names],
            *[new_m[n] for n in names], *[new_v[n] for n in names])
```

```python
import functools

import jax
import jax.numpy as jnp
from jax import lax
from jax.experimental import pallas as pl
from jax.experimental.pallas import tpu as pltpu

F32 = jnp.float32
BF16 = jnp.bfloat16
HI = lax.Precision.HIGHEST
MESH = pl.DeviceIdType.MESH

D_MODEL = 1024
CHUNK = 64
PLE_DIM = 256
EPS = 1e-6
A_HEADS = 8
A_HEAD_DIM = 64
A_WIDTH = 512
A_LOOKBACK = 8
BAND = (A_LOOKBACK + 1) * CHUNK
TAIL = 3 * CHUNK
REL_CLIP = 128
N_REL = 2 * REL_CLIP + 1
B_HEADS = 4
B_DIM = 128
B_WIDTH = 512
CONV_K = 4
CONV_CH = 1536
D_FF = 2816
SPLIT_Z = 3584
D_IN = 5640
ADAM_LR, ADAM_B1, ADAM_B2, ADAM_EPS, ADAM_WD, ADAM_STEP = 0.001, 0.9, 0.999, 1e-08, 0.01, 10

P_GATES, P_QA, P_KA, P_VA, P_CONV, P_Z, P_BD, P_WIDTH = 0, 2048, 2560, 3072, 3584, 5120, 5632, 5760

VMEM_LIMIT = 56 * 1024 * 1024


def _cp(sem, vmem=None, **kw):
    return pltpu.CompilerParams(dimension_semantics=sem, vmem_limit_bytes=vmem, **kw)


def _tile(n, cap):
    best = None
    for t in range(128, cap + 1, 128):
        if n % t == 0:
            best = t
    assert best is not None, (n, cap)
    return best


def _nn(a, b, prec=None):
    return lax.dot_general(a, b, (((1,), (0,)), ((), ())), preferred_element_type=F32, precision=prec)


def _nt(a, b, prec=None):
    return lax.dot_general(a, b, (((1,), (1,)), ((), ())), preferred_element_type=F32, precision=prec)


def _tn(a, b, prec=None):
    return lax.dot_general(a, b, (((0,), (0,)), ((), ())), preferred_element_type=F32, precision=prec)


def _bnn(a, b, prec=None):
    return lax.dot_general(a, b, (((2,), (1,)), ((0,), (0,))), preferred_element_type=F32, precision=prec)


def _bnt(a, b, prec=None):
    return lax.dot_general(a, b, (((2,), (2,)), ((0,), (0,))), preferred_element_type=F32, precision=prec)


def _bf(a):
    return a.astype(BF16)


def _split(a):
    hi = a.astype(BF16)
    return hi, (a - hi.astype(F32)).astype(BF16)


def _bnn_exact(lhs_b, rhs):
    h1 = _bf(rhs)
    r1 = rhs - h1.astype(F32)
    h2 = _bf(r1)
    h3 = _bf(r1 - h2.astype(F32))
    return _bnn(lhs_b, h1) + (_bnn(lhs_b, h2) + _bnn(lhs_b, h3))


def _bnn3(a, b):
    ah, al = a if isinstance(a, tuple) else _split(a)
    bh, bl = b if isinstance(b, tuple) else _split(b)
    return _bnn(ah, bh) + (_bnn(ah, bl) + _bnn(al, bh))


def _sigmoid(x):
    return 0.5 * jnp.tanh(0.5 * x) + 0.5


def _softplus(x):
    return jnp.maximum(x, 0.0) + jnp.log(1.0 + jnp.exp(-jnp.abs(x)))


def rms_matmul(x, g, w, name, tm=512, tn_cap=1024):
    t, d = x.shape
    n = w.shape[1]
    tm = min(tm, t)
    tn = _tile(n, tn_cap)

    nj = n // tn

    def body(x_ref, g_ref, w_ref, o_ref, h_ref, tail_ref):
        @pl.when(pl.program_id(1) == 0)
        def _():
            xv = x_ref[...]
            r = lax.rsqrt(jnp.mean(xv * xv, axis=-1, keepdims=True) + EPS)
            h_ref[...] = _bf(xv * r * g_ref[...])

        res = _nn(h_ref[...], w_ref[...])
        o_ref[...] = _bf(res)

        @pl.when(pl.program_id(1) == nj - 1)
        def _():
            tail_ref[...] = res[:, tn - 128:]

    return pl.pallas_call(
        body, name=name, grid=(t // tm, nj),
        in_specs=[pl.BlockSpec((tm, d), lambda i, j: (i, 0)),
                  pl.BlockSpec((1, d), lambda i, j: (0, 0)),
                  pl.BlockSpec((d, tn), lambda i, j: (0, j))],
        out_specs=[pl.BlockSpec((tm, tn), lambda i, j: (i, j)),
                   pl.BlockSpec((tm, d), lambda i, j: (i, 0)),
                   pl.BlockSpec((tm, 128), lambda i, j: (i, 0))],
        out_shape=[jax.ShapeDtypeStruct((t, n), BF16), jax.ShapeDtypeStruct((t, d), BF16),
                   jax.ShapeDtypeStruct((t, 128), F32)],
        compiler_params=_cp(("parallel", "arbitrary"), VMEM_LIMIT),
    )(x, g, w)


def matmul_tn(a, b, name, into=None, col0=0, width=None, tm=1024, tk_cap=1408, tn_cap=1408):
    m, k1 = a.shape
    n = b.shape[1]
    tm = min(tm, m)
    tk = _tile(k1, tk_cap)
    tn = _tile(n, tn_cap)
    while col0 % tn:
        tn = _tile(n, tn - 128)
    nk = m // tm
    c0 = col0 // tn

    def body(*refs):
        a_ref, b_ref, o_ref, acc = refs[0], refs[1], refs[-2], refs[-1]

        @pl.when(pl.program_id(2) == 0)
        def _():
            acc[...] = jnp.zeros_like(acc)

        acc[...] += _tn(_bf(a_ref[...]), _bf(b_ref[...]))

        @pl.when(pl.program_id(2) == nk - 1)
        def _():
            o_ref[...] = _bf(acc[...])

    in_specs = [pl.BlockSpec((tm, tk), lambda i, j, k: (k, i)),
                pl.BlockSpec((tm, tn), lambda i, j, k: (k, j))]
    args = [a, b]
    total = n if width is None else width
    aliases = {}
    if into is not None:
        in_specs.append(ANY)
        args.append(into)
        total = into.shape[1]
        aliases = {2: 0}
    return pl.pallas_call(
        body, name=name, grid=(k1 // tk, n // tn, nk),
        in_specs=in_specs,
        out_specs=pl.BlockSpec((tk, tn), lambda i, j, k: (i, c0 + j)),
        out_shape=jax.ShapeDtypeStruct((k1, total), BF16),
        scratch_shapes=[pltpu.VMEM((tk, tn), F32)],
        input_output_aliases=aliases,
        compiler_params=_cp(("parallel", "parallel", "arbitrary"), VMEM_LIMIT),
    )(*args)


def _tail_onehot(qi):
    r = lax.broadcasted_iota(jnp.int32, (384, TAIL), 0)
    kj = lax.broadcasted_iota(jnp.int32, (384, TAIL), 1)
    return (r == jnp.minimum(REL_CLIP + qi - kj, REL_CLIP) + REL_CLIP).astype(F32)


def bias_tail(rel_pad):
    def body(rb_ref, o_ref):
        rb = rb_ref[...]
        for qi in range(CHUNK):
            o_ref[qi] = _nn(rb, _tail_onehot(qi), HI)

    return pl.pallas_call(
        body, name="bias_tail",
        out_shape=jax.ShapeDtypeStruct((CHUNK, A_HEADS, TAIL), F32),
    )(rel_pad)


def bias_grad(db_t, db_far):
    def body(t_ref, f_ref, o_ref):
        acc = jnp.zeros((A_HEADS, 384), F32)
        for qi in range(CHUNK):
            acc = acc + _nt(t_ref[qi], _tail_onehot(qi), HI)
        far = jnp.sum(jnp.sum(f_ref[...], axis=2), axis=1, keepdims=True)
        lane = lax.broadcasted_iota(jnp.int32, (A_HEADS, 384), 1)
        o_ref[...] = acc + jnp.where(lane == 2 * REL_CLIP, far, 0.0)

    return pl.pallas_call(
        body, name="bias_grad",
        out_shape=jax.ShapeDtypeStruct((A_HEADS, 384), F32),
    )(db_t, db_far)


ATT_CB = 8


WIN = BAND + CHUNK


def _stack_heads(a, lane):
    return jnp.concatenate([jnp.where(lane < 64, a, 0.0), jnp.where(lane >= 64, a, 0.0)], axis=0)


def _fill_band_pads(k_ref, v_ref, kp, vp, s):
    z = jnp.zeros((A_LOOKBACK * CHUNK, 128), BF16)
    kp[pl.ds(0, A_LOOKBACK * CHUNK), :] = z
    vp[pl.ds(0, A_LOOKBACK * CHUNK), :] = z
    kp[pl.ds(A_LOOKBACK * CHUNK, s), :] = _bf(k_ref[...])
    vp[pl.ds(A_LOOKBACK * CHUNK, s), :] = _bf(v_ref[...])


def attn_fwd(proj, bias_band, b, s):
    t = b * s
    nc = s // CHUNK
    qb, kb_, vb_ = P_QA // 128, P_KA // 128, P_VA // 128

    nstep = nc // ATT_CB
    rows = ATT_CB * CHUNK

    def body(q_ref, k_ref, v_ref, b_ref, o_ref, lse_ref, kp, vp):
        n0 = pl.program_id(2) * ATT_CB

        @pl.when(n0 == 0)
        def _():
            _fill_band_pads(k_ref, v_ref, kp, vp, s)

        lane = lax.broadcasted_iota(jnp.int32, (2 * CHUNK, 128), 1)
        col = lax.broadcasted_iota(jnp.int32, (4 * CHUNK, WIN), 1)
        bias4 = b_ref[...]

        def pair(pp, carry):
            n = n0 + 2 * pp
            r0 = pl.multiple_of(pp * 2 * CHUNK, 2 * CHUNK)
            start = pl.multiple_of(n * CHUNK, CHUNK)
            kb = kp[pl.ds(start, WIN), :]
            vb = vp[pl.ds(start, WIN), :]
            q4 = _stack_heads(q_ref[pl.ds(r0, 2 * CHUNK), :] * (A_HEAD_DIM ** -0.5), lane)
            sc = jnp.where(col >= (A_LOOKBACK - n) * CHUNK, _nt(_bf(q4), kb) + bias4, -1e30)
            mx = jnp.max(sc, axis=1, keepdims=True)
            p = jnp.exp(sc - mx)
            l = jnp.sum(p, axis=1, keepdims=True)
            o4 = _nn(_bf(p), vb) / l
            lse4 = mx + jnp.log(l)
            o_ref[pl.ds(r0, 2 * CHUNK), :] = jnp.where(lane < 64, o4[:2 * CHUNK], o4[2 * CHUNK:])
            lse_ref[pl.ds(r0, 2 * CHUNK), :] = jnp.where(lane < 64, lse4[:2 * CHUNK], lse4[2 * CHUNK:])
            return carry

        lax.fori_loop(0, ATT_CB // 2, pair, 0, unroll=2)

    return pl.pallas_call(
        body, name="attn_fwd", grid=(b, 4, nstep),
        in_specs=[pl.BlockSpec((rows, 128), lambda bb, m, n: (bb * nstep + n, qb + m)),
                  pl.BlockSpec((s, 128), lambda bb, m, n: (bb, kb_ + m)),
                  pl.BlockSpec((s, 128), lambda bb, m, n: (bb, vb_ + m)),
                  pl.BlockSpec((None, 4 * CHUNK, WIN), lambda bb, m, n: (m, 0, 0))],
        out_specs=[pl.BlockSpec((rows, 128), lambda bb, m, n: (bb * nstep + n, m)),
                   pl.BlockSpec((rows, 128), lambda bb, m, n: (bb * nstep + n, m))],
        out_shape=[jax.ShapeDtypeStruct((t, A_WIDTH), F32), jax.ShapeDtypeStruct((t, A_WIDTH), F32)],
        scratch_shapes=[pltpu.VMEM((s + A_LOOKBACK * CHUNK, 128), BF16),
                        pltpu.VMEM((s + A_LOOKBACK * CHUNK, 128), BF16)],
        compiler_params=_cp(("parallel", "parallel", "arbitrary"), VMEM_LIMIT),
    )(proj, proj, proj, bias_band)


def attn_bwd(proj, bias_band, y_a, lse, dy_a, b, s):
    t = b * s
    nc = s // CHUNK
    qb, kb_, vb_ = P_QA // 128, P_KA // 128, P_VA // 128
    pad = A_LOOKBACK * CHUNK
    nstep = nc // ATT_CB
    rows = ATT_CB * CHUNK

    def body(q_ref, k_ref, v_ref, b_ref, do_ref, o_ref, lse_ref,
             dq_ref, dk_ref, dv_ref, dbt_ref, dbf_ref, kp, vp, dkp, dvp):
        bb = pl.program_id(1)
        n0 = pl.program_id(2) * ATT_CB

        @pl.when(n0 == 0)
        def _():
            _fill_band_pads(k_ref, v_ref, kp, vp, s)
            dkp[...] = jnp.zeros_like(dkp)
            dvp[...] = jnp.zeros_like(dvp)

        @pl.when((n0 == 0) & (bb == 0))
        def _():
            dbt_ref[...] = jnp.zeros_like(dbt_ref)
            dbf_ref[...] = jnp.zeros_like(dbf_ref)

        lane = lax.broadcasted_iota(jnp.int32, (2 * CHUNK, 128), 1)
        col = lax.broadcasted_iota(jnp.int32, (4 * CHUNK, WIN), 1)
        bias4 = b_ref[...]

        def pair(pp, carry):
            n = n0 + 2 * pp
            r0 = pl.multiple_of(pp * 2 * CHUNK, 2 * CHUNK)
            start = pl.multiple_of(n * CHUNK, CHUNK)
            kb = kp[pl.ds(start, WIN), :]
            vb = vp[pl.ds(start, WIN), :]
            q4b = _bf(_stack_heads(q_ref[pl.ds(r0, 2 * CHUNK), :] * (A_HEAD_DIM ** -0.5), lane))
            do4 = _stack_heads(do_ref[pl.ds(r0, 2 * CHUNK), :], lane)
            do4b = _bf(do4)
            o = o_ref[pl.ds(r0, 2 * CHUNK), :]
            lsev = lse_ref[pl.ds(r0, 2 * CHUNK), :]
            lse4 = jnp.concatenate([lsev[:, 0:1], lsev[:, 64:65]], axis=0)
            sc = jnp.where(col >= (A_LOOKBACK - n) * CHUNK, _nt(q4b, kb) + bias4, -1e30)
            p = jnp.exp(sc - lse4)
            dp = _nt(do4b, vb)
            delta = jnp.sum(do4 * jnp.concatenate([o, o], axis=0), axis=1, keepdims=True)
            ds = p * (dp - delta)
            dsb = _bf(ds)
            dq4 = _nn(dsb, kb)
            dq_ref[pl.ds(r0, 2 * CHUNK), :] = _bf(
                jnp.where(lane < 64, dq4[:2 * CHUNK], dq4[2 * CHUNK:]) * (A_HEAD_DIM ** -0.5))
            dkp[pl.ds(start, WIN), :] += _tn(dsb, q4b)
            dvp[pl.ds(start, WIN), :] += _tn(_bf(p), do4b)
            dbt_ref[...] += ds[:, WIN - 256:]
            dbf_ref[...] += ds[:, 0:128] + ds[:, 128:256] + ds[:, 256:384]
            return carry

        lax.fori_loop(0, ATT_CB // 2, pair, 0, unroll=2)

        @pl.when(n0 == nc - ATT_CB)
        def _():
            dk_ref[...] = _bf(dkp[pl.ds(pad, s), :])
            dv_ref[...] = _bf(dvp[pl.ds(pad, s), :])

    return pl.pallas_call(
        body, name="attn_bwd", grid=(4, b, nstep),
        in_specs=[pl.BlockSpec((rows, 128), lambda m, bb, n: (bb * nstep + n, qb + m)),
                  pl.BlockSpec((s, 128), lambda m, bb, n: (bb, kb_ + m)),
                  pl.BlockSpec((s, 128), lambda m, bb, n: (bb, vb_ + m)),
                  pl.BlockSpec((None, 4 * CHUNK, WIN), lambda m, bb, n: (m, 0, 0)),
                  pl.BlockSpec((rows, 128), lambda m, bb, n: (bb * nstep + n, m)),
                  pl.BlockSpec((rows, 128), lambda m, bb, n: (bb * nstep + n, m)),
                  pl.BlockSpec((rows, 128), lambda m, bb, n: (bb * nstep + n, m))],
        out_specs=[pl.BlockSpec((rows, 128), lambda m, bb, n: (bb * nstep + n, m)),
                   pl.BlockSpec((s, 128), lambda m, bb, n: (bb, m)),
                   pl.BlockSpec((s, 128), lambda m, bb, n: (bb, m)),
                   pl.BlockSpec((None, 4 * CHUNK, 256), lambda m, bb, n: (m, 0, 0)),
                   pl.BlockSpec((None, 4 * CHUNK, 128), lambda m, bb, n: (m, 0, 0))],
        out_shape=[jax.ShapeDtypeStruct((t, A_WIDTH), BF16)] * 3
        + [jax.ShapeDtypeStruct((4, 4 * CHUNK, 256), F32),
           jax.ShapeDtypeStruct((4, 4 * CHUNK, 128), F32)],
        scratch_shapes=[pltpu.VMEM((s + pad, 128), BF16), pltpu.VMEM((s + pad, 128), BF16),
                        pltpu.VMEM((s + pad, 128), F32), pltpu.VMEM((s + pad, 128), F32)],
        compiler_params=_cp(("parallel", "arbitrary", "arbitrary"), VMEM_LIMIT),
    )(proj, proj, proj, bias_band, dy_a, y_a, lse)


def _conv_taps(x, w, s):
    row = lax.broadcasted_iota(jnp.int32, x.shape, 0)
    shifted = [x] + [jnp.where(row >= i, pltpu.roll(x, i, 0), 0.0) for i in range(1, CONV_K)]
    acc = shifted[0] * w[CONV_K - 1:CONV_K, :]
    for i in range(1, CONV_K):
        acc = acc + shifted[i] * w[CONV_K - 1 - i:CONV_K - i, :]
    return acc, shifted


def conv_fwd(proj, conv_w8, b, s):
    cb = 512
    c0 = P_CONV // cb

    def body(x_ref, w_ref, o_ref):
        a, _ = _conv_taps(x_ref[...].astype(F32), w_ref[...], s)
        o_ref[...] = a * _sigmoid(a)

    return pl.pallas_call(
        body, name="conv_fwd", grid=(b, CONV_CH // cb),
        in_specs=[pl.BlockSpec((s, cb), lambda bb, j: (bb, c0 + j)),
                  pl.BlockSpec((8, cb), lambda bb, j: (0, j))],
        out_specs=pl.BlockSpec((s, cb), lambda bb, j: (bb, j)),
        out_shape=jax.ShapeDtypeStruct((b * s, CONV_CH), F32),
        compiler_params=_cp(("parallel", "parallel"), VMEM_LIMIT),
    )(proj, conv_w8)


def conv_bwd(proj, conv_w8, dc3, b, s):
    cb = 512
    c0 = P_CONV // cb

    def body(x_ref, w_ref, dc_ref, dx_ref, dw_ref):
        @pl.when(pl.program_id(1) == 0)
        def _():
            dw_ref[...] = jnp.zeros_like(dw_ref)

        w = w_ref[...]
        a, shifted = _conv_taps(x_ref[...].astype(F32), w, s)
        sg = _sigmoid(a)
        da = dc_ref[...] * (sg * (1.0 + a * (1.0 - sg)))
        row = lax.broadcasted_iota(jnp.int32, da.shape, 0)
        dx = da * w[CONV_K - 1:CONV_K, :]
        for i in range(1, CONV_K):
            dx = dx + jnp.where(row < s - i, pltpu.roll(da, s - i, 0), 0.0) * w[CONV_K - 1 - i:CONV_K - i, :]
        dx_ref[...] = _bf(dx)
        r8 =lax.broadcasted_iota(jnp.int32, (8, cb), 0)
        dw = jnp.zeros((8, cb), F32)
        for i in range(CONV_K):
            dw = dw + jnp.where(r8 == CONV_K - 1 - i, jnp.sum(da * shifted[i], axis=0, keepdims=True), 0.0)
        dw_ref[...] += dw

    return pl.pallas_call(
        body, name="conv_bwd", grid=(CONV_CH // cb, b),
        in_specs=[pl.BlockSpec((s, cb), lambda j, bb: (bb, c0 + j)),
                  pl.BlockSpec((8, cb), lambda j, bb: (0, j)),
                  pl.BlockSpec((None, s, cb), lambda j, bb: (j, bb, 0))],
        out_specs=[pl.BlockSpec((s, cb), lambda j, bb: (bb, j)),
                   pl.BlockSpec((8, cb), lambda j, bb: (0, j))],
        out_shape=[jax.ShapeDtypeStruct((b * s, CONV_CH), BF16), jax.ShapeDtypeStruct((8, CONV_CH), F32)],
        compiler_params=_cp(("parallel", "arbitrary"), VMEM_LIMIT),
    )(proj, conv_w8, dc3)


def _pick_lane(v, k):
    lane = lax.broadcasted_iota(jnp.int32, v.shape, 1)
    return jnp.sum(jnp.where(lane == k, v, 0.0), axis=1, keepdims=True)


def _chunk_masks(ncb):
    i = lax.broadcasted_iota(jnp.int32, (ncb, CHUNK, CHUNK), 1)
    j = lax.broadcasted_iota(jnp.int32, (ncb, CHUNK, CHUNK), 2)
    return i, j


def _col_of_row(rowvec, eye):
    return jnp.sum(jnp.where(eye, rowvec, 0.0), axis=2, keepdims=True)


def _dn_chunk_math(cq, ck, cv, bd, al_row, dtb_row, h, ncb, tm=None):
    r = ncb * CHUNK
    i, j = _chunk_masks(ncb)
    eye = i == j
    low = i >= j
    strict = i > j
    ones = jnp.ones((ncb, CHUNK, CHUNK), F32)

    braw = _pick_lane(bd, h)
    draw = _pick_lane(bd, B_HEADS + h)
    al = _pick_lane(al_row, h)
    dtb = _pick_lane(dtb_row, h)
    ea = jnp.exp(al)
    beta = _sigmoid(braw)
    sp_arg = draw + dtb
    g = -ea * _softplus(sp_arg)

    rq = lax.rsqrt(jnp.sum(cq * cq, axis=1, keepdims=True) + EPS)
    rk = lax.rsqrt(jnp.sum(ck * ck, axis=1, keepdims=True) + EPS)
    nq = cq * rq
    kn = ck * rk
    qn = nq * (B_DIM ** -0.5)

    def c3(a):
        return a.reshape(ncb, CHUNK, a.shape[-1])

    qn3, kn3, v3, beta3 = c3(qn), c3(kn), c3(cv), c3(beta)
    gb = jnp.broadcast_to(c3(g), (ncb, CHUNK, CHUNK))
    gc_b = _bnn_exact(low.astype(BF16), gb)
    gr_b = _bnn_exact(_bf(ones), jnp.where(eye, gc_b, 0.0))
    dm = jnp.where(low, jnp.exp(jnp.where(low, gc_b - gr_b, 0.0)), 0.0)
    gc = gc_b[:, :, 0:1]
    gl = gc_b[:, CHUNK - 1:CHUNK, 0:1]
    gam = jnp.exp(gc)
    egl = jnp.exp(gl)
    edec = jnp.exp(gl - gc)

    knb = _bf(kn3)
    kk = _bnt(knb, knb)
    kd = jnp.where(strict, kk * dm, 0.0)
    a = beta3 * kd
    sz = 1 if tm is None else CHUNK
    if tm is None:
        tm = eye.astype(F32)
    while sz < CHUNK:
        off = jnp.where(((i // (2 * sz)) == (j // (2 * sz))) & ((i // sz) != (j // sz)), a, 0.0)
        tmb = _bf(tm)
        tm = tm - _bnn(_bf(_bnn(tmb, _bf(off))), tmb)
        sz *= 2
    bv = beta3 * v3
    bk = (beta3 * gam) * kn3
    sol = _bnn3(_split(tm), jnp.concatenate([bv, bk], axis=2))
    u, wk = sol[:, :, :B_DIM], sol[:, :, B_DIM:]
    qk = _bnt(_bf(qn3), knb)
    p = jnp.where(low, qk * dm, 0.0)
    kdec = kn3 * edec
    qg = gam * qn3
    return dict(beta=beta3, g=c3(g), ea=ea, sp_arg=c3(sp_arg), rq=c3(rq), rk=c3(rk), nq=c3(nq),
                qn=qn3, kn=kn3, v=v3, gc=gc, gl=gl, gam=gam, egl=egl, edec=edec, dm=dm, kd=kd, a=a,
                tm=tm, u=u, wk=wk, qk=qk, p=p, kdec=kdec, qg=qg, eye=eye, low=low, strict=strict)


def dn_prep(c, proj, al_row, dtb_row, b, s, ncb=16):
    t = b * s
    r = ncb * CHUNK
    nblk = t // r
    bd_blk = 0

    def body(cq_ref, ck_ref, cv_ref, bd_ref, al_ref, dtb_ref, u_ref, wk_ref, qg_ref, kdec_ref, p_ref, egl_ref,
             tm_ref):
        h = pl.program_id(1)
        m = _dn_chunk_math(cq_ref[...], ck_ref[...], cv_ref[...], bd_ref[...].astype(F32), al_ref[...], dtb_ref[...], h, ncb)
        tm_ref[...] = m["tm"].reshape(r, CHUNK)
        u_ref[...] = m["u"].reshape(r, B_DIM)
        wk_ref[...] = m["wk"].reshape(r, B_DIM)
        qg_ref[...] = m["qg"].reshape(r, B_DIM)
        kdec_ref[...] = m["kdec"].reshape(r, B_DIM)
        p_ref[...] = m["p"].reshape(r, CHUNK)
        egl_ref[...] = jnp.broadcast_to(m["egl"], (ncb, 8, 128)).reshape(ncb * 8, 128)

    col = lambda k: pl.BlockSpec((r, 128), lambda i, h: (i, k * B_HEADS + h))
    out_col = pl.BlockSpec((r, 128), lambda i, h: (i, h))
    small = pl.BlockSpec((1, 128), lambda i, h: (0, 0))
    return pl.pallas_call(
        body, name="dn_prep", grid=(nblk, B_HEADS),
        in_specs=[col(0), col(1), col(2), pl.BlockSpec((r, 128), lambda i, h: (i, bd_blk)), small, small],
        out_specs=[out_col, out_col, out_col, out_col,
                   pl.BlockSpec((None, r, CHUNK), lambda i, h: (h, i, 0)),
                   pl.BlockSpec((None, ncb * 8, 128), lambda i, h: (h, i, 0)),
                   pl.BlockSpec((None, r, CHUNK), lambda i, h: (h, i, 0))],
        out_shape=[jax.ShapeDtypeStruct((t, B_WIDTH), F32)] * 4
        + [jax.ShapeDtypeStruct((B_HEADS, t, CHUNK), F32),
           jax.ShapeDtypeStruct((B_HEADS, t // 8, 128), F32),
           jax.ShapeDtypeStruct((B_HEADS, t, CHUNK), F32)],
        compiler_params=_cp(("parallel", "parallel"), VMEM_LIMIT),
    )(c, c, c, proj, al_row, dtb_row)


def dn_scan_fwd(u, wk, qg, kdec, p, egl, b, s):
    t = b * s
    nc = s // CHUNK

    def body(u_ref, wk_ref, qg_ref, kdec_ref, p_ref, egl_ref, o_ref, ss_ref, st):
        @pl.when(pl.program_id(0) == 0)
        def _():
            st[...] = jnp.zeros_like(st)

        chains = [(bb, h) for bb in range(b) for h in range(B_HEADS)]
        states = [st[bb * B_HEADS + h] for bb, h in chains]
        sls = [slice(h * B_DIM, (h + 1) * B_DIM) for _, h in chains]
        sbs = [_bf(sh) for sh in states]
        ws = [u_ref[bb, :, sl] - _nt(_bf(wk_ref[bb, :, sl]), sb) for (bb, _), sl, sb in zip(chains, sls, sbs)]
        qs = [_nt(_bf(qg_ref[bb, :, sl]), sb) for (bb, _), sl, sb in zip(chains, sls, sbs)]
        wbs = [_bf(w) for w in ws]
        outs = [q + _nn(_bf(p_ref[h, bb]), wb) for (bb, h), q, wb in zip(chains, qs, wbs)]
        new_states = [egl_ref[h, bb][0:1, :] * sh + _tn(wb, _bf(kdec_ref[bb, :, sl]))
                      for (bb, h), sl, sh, wb in zip(chains, sls, states, wbs)]
        for (bb, h), sh, o, ns in zip(chains, states, outs, new_states):
            ss_ref[bb, h] = sh
            o_ref[bb, :, h * B_DIM:(h + 1) * B_DIM] = o
            st[bb * B_HEADS + h] = ns

    r3 = lambda a: a.reshape(b, s, B_WIDTH)
    act = pl.BlockSpec((b, CHUNK, B_WIDTH), lambda n: (0, n, 0))
    o, states = pl.pallas_call(
        body, name="dn_scan_fwd", grid=(nc,),
        in_specs=[act, act, act, act,
                  pl.BlockSpec((B_HEADS, b, CHUNK, CHUNK), lambda n: (0, 0, n, 0)),
                  pl.BlockSpec((B_HEADS, b, 8, 128), lambda n: (0, 0, n, 0))],
        out_specs=[act, pl.BlockSpec((b, None, B_HEADS, B_DIM, B_DIM), lambda n: (0, n, 0, 0, 0))],
        out_shape=[jax.ShapeDtypeStruct((b, s, B_WIDTH), F32),
                   jax.ShapeDtypeStruct((b, nc, B_HEADS, B_DIM, B_DIM), F32)],
        scratch_shapes=[pltpu.VMEM((b * B_HEADS, B_DIM, B_DIM), F32)],
        compiler_params=_cp(("arbitrary",), VMEM_LIMIT),
    )(r3(u), r3(wk), r3(qg), r3(kdec), p.reshape(B_HEADS, b, s, CHUNK), egl.reshape(B_HEADS, b, s // 8, 128))
    return o.reshape(t, B_WIDTH), states


def dn_scan_bwd(u, wk, qg, kdec, p, egl, states, do, b, s):
    t = b * s
    nc = s // CHUNK

    def body(u_ref, wk_ref, qg_ref, kdec_ref, p_ref, egl_ref, ss_ref, do_ref,
             dw_ref, dwk_ref, dqg_ref, dkdec_ref, dp_ref, degl_ref, dst):
        @pl.when(pl.program_id(0) == 0)
        def _():
            dst[...] = jnp.zeros_like(dst)

        chains = [(bb, h) for bb in range(b) for h in range(B_HEADS)]
        dstates = [dst[bb * B_HEADS + h] for bb, h in chains]
        n8 = range(len(chains))
        sls = [slice(h * B_DIM, (h + 1) * B_DIM) for _, h in chains]
        shs = [ss_ref[bb, h] for bb, h in chains]
        sbs = [_bf(sh) for sh in shs]
        dsbs = [_bf(dsp) for dsp in dstates]
        wkbs = [_bf(wk_ref[bb, :, sl]) for (bb, _), sl in zip(chains, sls)]
        dobs = [_bf(do_ref[bb, :, sl]) for (bb, _), sl in zip(chains, sls)]
        t1 = [_nt(wkbs[i], sbs[i]) for i in n8]
        dwa = [_tn(_bf(p_ref[h, bb]), dobs[i]) for i, (bb, h) in enumerate(chains)]
        dwb_ = [_nt(_bf(kdec_ref[bb, :, sls[i]]), dsbs[i]) for i, (bb, _) in enumerate(chains)]
        dqgs = [_nn(dobs[i], sbs[i]) for i in n8]
        dsq = [_tn(dobs[i], _bf(qg_ref[bb, :, sls[i]])) for i, (bb, _) in enumerate(chains)]
        wbs = [_bf(u_ref[bb, :, sls[i]] - t1[i]) for i, (bb, _) in enumerate(chains)]
        dws = [dwa[i] + dwb_[i] for i in n8]
        dwbs = [_bf(dw) for dw in dws]
        dwks = [-_nn(dwbs[i], sbs[i]) for i in n8]
        dkdecs = [_nn(wbs[i], dsbs[i]) for i in n8]
        dpms = [_nt(dobs[i], wbs[i]) for i in n8]
        dsw = [_tn(dwbs[i], wkbs[i]) for i in n8]
        tots = [jnp.sum(jnp.sum(shs[i] * dstates[i], axis=1, keepdims=True), axis=0, keepdims=True) for i in n8]
        new_dss = [egl_ref[h, bb][0:1, :] * dstates[i] + dsq[i] - dsw[i] for i, (bb, h) in enumerate(chains)]
        results = [(dws[i], dqgs[i], dwks[i], dkdecs[i], dpms[i], tots[i], new_dss[i]) for i in n8]
        for (bb, h), (dw, dqg, dwk, dkdec, dpm, tot, new_ds) in zip(chains, results):
            sl = slice(h * B_DIM, (h + 1) * B_DIM)
            dw_ref[bb, :, sl] = dw
            dqg_ref[bb, :, sl] = dqg
            dwk_ref[bb, :, sl] = dwk
            dkdec_ref[bb, :, sl] = dkdec
            dp_ref[h, bb] = dpm
            degl_ref[h, bb] = jnp.broadcast_to(tot, (8, 128))
            dst[bb * B_HEADS + h] = new_ds

    r3 = lambda a: a.reshape(b, s, B_WIDTH)
    act = pl.BlockSpec((b, CHUNK, B_WIDTH), lambda n: (0, nc - 1 - n, 0))
    pspec = pl.BlockSpec((B_HEADS, b, CHUNK, CHUNK), lambda n: (0, 0, nc - 1 - n, 0))
    espec = pl.BlockSpec((B_HEADS, b, 8, 128), lambda n: (0, 0, nc - 1 - n, 0))
    outs = pl.pallas_call(
        body, name="dn_scan_bwd", grid=(nc,),
        in_specs=[act, act, act, act, pspec, espec,
                  pl.BlockSpec((b, None, B_HEADS, B_DIM, B_DIM), lambda n: (0, nc - 1 - n, 0, 0, 0)),
                  act],
        out_specs=[act, act, act, act, pspec, espec],
        out_shape=[jax.ShapeDtypeStruct((b, s, B_WIDTH), F32)] * 4
        + [jax.ShapeDtypeStruct((B_HEADS, b, s, CHUNK), F32),
           jax.ShapeDtypeStruct((B_HEADS, b, s // 8, 128), F32)],
        scratch_shapes=[pltpu.VMEM((b * B_HEADS, B_DIM, B_DIM), F32)],
        compiler_params=_cp(("arbitrary",), VMEM_LIMIT),
    )(r3(u), r3(wk), r3(qg), r3(kdec), p.reshape(B_HEADS, b, s, CHUNK), egl.reshape(B_HEADS, b, s // 8, 128),
      states, r3(do))
    return (*[a.reshape(t, B_WIDTH) for a in outs[:4]], outs[4].reshape(B_HEADS, t, CHUNK),
            outs[5].reshape(B_HEADS, t // 8, 128))


def dn_post_bwd(c, proj, al_row, dtb_row, tmat, dw, dwk, dqg, dkdec, dp, degl, b, s, ncb=16):
    t = b * s
    r = ncb * CHUNK
    nblk = t // r
    bd_blk = 0

    def body(cq_ref, ck_ref, cv_ref, bd_ref, al_ref, dtb_ref, tm_ref, dw_ref, dwk_ref, dqg_ref, dkdec_ref, dp_ref,
             degl_ref, dc_ref, dbd_ref, dal_ref, ddtb_ref):
        h = pl.program_id(1)

        @pl.when((pl.program_id(0) == 0) & (h == 0))
        def _():
            dal_ref[...] = jnp.zeros_like(dal_ref)
            ddtb_ref[...] = jnp.zeros_like(ddtb_ref)

        m = _dn_chunk_math(cq_ref[...], ck_ref[...], cv_ref[...], bd_ref[...].astype(F32), al_ref[...], dtb_ref[...], h, ncb,
                           tm=tm_ref[...].reshape(ncb, CHUNK, CHUNK))
        eye, low, strict = m["eye"], m["low"], m["strict"]
        eyef = eye.astype(F32)

        def c3(a):
            return a.reshape(ncb, CHUNK, a.shape[-1])

        du, dwkv, dqg, dkdec = c3(dw_ref[...]), c3(dwk_ref[...]), c3(dqg_ref[...]), c3(dkdec_ref[...])
        dpm = jnp.where(low, c3(dp_ref[...]), 0.0)
        degl = degl_ref[...].reshape(ncb, 8, 128)[:, 0:1, 0:1]
        beta, gam, kn, qn, v = m["beta"], m["gam"], m["kn"], m["qn"], m["v"]
        dm, kd, a, p = m["dm"], m["kd"], m["a"], m["p"]
        knb, qnb = _bf(kn), _bf(qn)

        eyeb = _bf(eyef)
        th, tl = _split(m["tm"])
        tts = (_bf(_bnt(eyeb, th)), _bf(_bnt(eyeb, tl)))
        xy = _bnn3(tts, jnp.concatenate([du, dwkv], axis=2))
        x, y = xy[:, :, :B_DIM], xy[:, :, B_DIM:]
        da = -jnp.where(strict, _bnt(_bf(x), _bf(m["u"])) + _bnt(_bf(y), _bf(m["wk"])), 0.0)
        dv = beta * x
        sy = jnp.sum(y * kn, axis=2, keepdims=True)
        dbeta = jnp.sum(x * v, axis=2, keepdims=True) + gam * sy + jnp.sum(da * kd, axis=2, keepdims=True)
        dgam = beta * sy + jnp.sum(dqg * qn, axis=2, keepdims=True)
        dkk = da * beta * dm
        dqk = dpm * dm
        dkkb, dqkb = _bf(dkk), _bf(dqk)
        dkn = ((beta * gam) * y + _bnn(dkkb, knb) + _bnn(_bf(_bnt(eyeb, dkkb)), knb)
               + _bnn(_bf(_bnt(eyeb, dqkb)), qnb) + dkdec * m["edec"])
        dqn = gam * dqg + _bnn(dqkb, knb)
        mm = da * a + dpm * p
        ek = jnp.sum(dkdec * m["kdec"], axis=2, keepdims=True)
        dgc = (jnp.sum(mm, axis=2, keepdims=True) - _col_of_row(jnp.sum(mm, axis=1, keepdims=True), eye)
               + dgam * gam - ek)
        dgl = jnp.sum(ek, axis=1, keepdims=True) + degl * m["egl"]
        i, _ = _chunk_masks(ncb)
        dgc = dgc + jnp.where(i[:, :, 0:1] == CHUNK - 1, dgl, 0.0)
        upper = (i <= _chunk_masks(ncb)[1]).astype(BF16)
        dg = _bnn_exact(upper, jnp.broadcast_to(dgc, (ncb, CHUNK, CHUNK)))[:, :, 0:1]

        nq = m["nq"]
        dnq = dqn * (B_DIM ** -0.5)
        dcq = m["rq"] * (dnq - nq * jnp.sum(nq * dnq, axis=2, keepdims=True))
        dck = m["rk"] * (dkn - kn * jnp.sum(kn * dkn, axis=2, keepdims=True))
        dc_ref[0] = dcq.reshape(r, B_DIM)
        dc_ref[1] = dck.reshape(r, B_DIM)
        dc_ref[2] = dv.reshape(r, B_DIM)

        dbraw = (dbeta * beta * (1.0 - beta)).reshape(r, 1)
        sgm = _sigmoid(m["sp_arg"])
        ddraw3 = dg * (-m["ea"]) * sgm
        ddraw = ddraw3.reshape(r, 1)
        lane = lax.broadcasted_iota(jnp.int32, (r, 128), 1)
        contrib = jnp.where(lane == h, dbraw, 0.0) + jnp.where(lane == B_HEADS + h, ddraw, 0.0)

        @pl.when(h == 0)
        def _():
            dbd_ref[...] = contrib

        @pl.when(h != 0)
        def _():
            dbd_ref[...] += contrib

        lane8 = lax.broadcasted_iota(jnp.int32, (8, 128), 1)
        tot_al = jnp.sum(jnp.sum(dg * m["g"], axis=1, keepdims=True), axis=0, keepdims=True).reshape(1, 1)
        tot_dtb = jnp.sum(jnp.sum(ddraw3, axis=1, keepdims=True), axis=0, keepdims=True).reshape(1, 1)
        dal_ref[...] += jnp.where(lane8 == h, tot_al, 0.0)
        ddtb_ref[...] += jnp.where(lane8 == h, tot_dtb, 0.0)

    col = lambda k: pl.BlockSpec((r, 128), lambda i, h: (i, k * B_HEADS + h))
    hcol = pl.BlockSpec((r, 128), lambda i, h: (i, h))
    small = pl.BlockSpec((1, 128), lambda i, h: (0, 0))
    acc = pl.BlockSpec((8, 128), lambda i, h: (0, 0))
    return pl.pallas_call(
        body, name="dn_post_bwd", grid=(nblk, B_HEADS),
        in_specs=[col(0), col(1), col(2), pl.BlockSpec((r, 128), lambda i, h: (i, bd_blk)), small, small,
                  pl.BlockSpec((None, r, CHUNK), lambda i, h: (h, i, 0)),
                  hcol, hcol, hcol, hcol,
                  pl.BlockSpec((None, r, CHUNK), lambda i, h: (h, i, 0)),
                  pl.BlockSpec((None, ncb * 8, 128), lambda i, h: (h, i, 0))],
        out_specs=[pl.BlockSpec((3, r, 128), lambda i, h: (0, i, h)),
                   pl.BlockSpec((r, 128), lambda i, h: (i, 0)), acc, acc],
        out_shape=[jax.ShapeDtypeStruct((3, t, B_WIDTH), F32), jax.ShapeDtypeStruct((t, 128), F32),
                   jax.ShapeDtypeStruct((8, 128), F32), jax.ShapeDtypeStruct((8, 128), F32)],
        compiler_params=_cp(("arbitrary", "arbitrary"), VMEM_LIMIT),
    )(c, c, c, proj, al_row, dtb_row, tmat, dw, dwk, dqg, dkdec, dp, degl)


def make_bias_band(rel_bias):
    tail = bias_tail(jnp.pad(rel_bias, ((0, 0), (0, 384 - N_REL))))
    far = jnp.broadcast_to(rel_bias[:, 2 * REL_CLIP][:, None, None], (A_HEADS, CHUNK, BAND - TAIL))
    band = jnp.concatenate([far, jnp.transpose(tail, (1, 0, 2))], axis=2)
    off = jnp.full((A_HEADS, CHUNK, CHUNK), -1e30, F32)
    both = jnp.stack([jnp.concatenate([band, off], axis=2), jnp.concatenate([off, band], axis=2)], axis=1)
    return both.reshape(4, 4 * CHUNK, WIN)


def bias_band_grad(dbt, dbf):
    t5 = dbt.reshape(A_HEADS, 2, CHUNK, 256)
    tail = t5[:, 0, :, :TAIL] + t5[:, 1, :, CHUNK:]
    far = dbf.reshape(A_HEADS, 2, CHUNK, 128).sum(axis=1) + jnp.pad(t5[:, 1, :, :CHUNK], ((0, 0), (0, 0), (0, CHUNK)))
    return bias_grad(jnp.transpose(tail, (1, 0, 2)), far)[:, :N_REL]


def _rms(x):
    r = lax.rsqrt(jnp.mean(x * x, axis=-1, keepdims=True) + EPS)
    return r, x * r


def _rms_bwd(dh, g, r, n):
    dn = dh * g
    return r * (dn - n * jnp.mean(dn * n, axis=-1, keepdims=True)), dh * n


def _gated_onorm(o, z, w_on):
    parts = []
    for h in range(B_HEADS):
        sl = slice(h * B_DIM, (h + 1) * B_DIM)
        r, n = _rms(o[:, sl])
        parts.append((r, n))
    r4 = [p[0] for p in parts]
    n4 = jnp.concatenate([p[1] for p in parts], axis=1)
    w4 = jnp.concatenate([w_on] * B_HEADS, axis=1)
    sz = _sigmoid(z)
    silu = z * sz
    return n4 * w4 * silu, r4, n4, w4, sz, silu


def mid_fwd(x, y_a, o_b, proj, w_on, wa, wb, w_out, tm=256):
    t = x.shape[0]
    tm = min(tm, t)

    def body(x_ref, ya_ref, ob_ref, z_ref, ga_ref, gb_ref, won_ref, wa_ref, wb_ref, wo_ref, x1_ref, mg_ref):
        yb = _gated_onorm(ob_ref[...], z_ref[...].astype(F32), won_ref[...])[0]
        ua = _nn(_bf(ya_ref[...]), wa_ref[...])
        ub = _nn(_bf(yb), wb_ref[...])
        merged = _sigmoid(ga_ref[...].astype(F32)) * ua + _sigmoid(gb_ref[...].astype(F32)) * ub
        mb = _bf(merged)
        mg_ref[...] = mb
        x1_ref[...] = x_ref[...] + _nn(mb, wo_ref[...])

    rowd = pl.BlockSpec((tm, D_MODEL), lambda i: (i, 0))
    row5 = pl.BlockSpec((tm, 512), lambda i: (i, 0))
    full = lambda a: pl.BlockSpec(a.shape, lambda i: (0,) * a.ndim)
    return pl.pallas_call(
        body, name="mid_fwd", grid=(t // tm,),
        in_specs=[rowd, row5, row5,
                  pl.BlockSpec((tm, 512), lambda i: (i, P_Z // 512)),
                  pl.BlockSpec((tm, D_MODEL), lambda i: (i, 0)),
                  pl.BlockSpec((tm, D_MODEL), lambda i: (i, 1)),
                  full(w_on), full(wa), full(wb), full(w_out)],
        out_specs=[rowd, rowd],
        out_shape=[jax.ShapeDtypeStruct((t, D_MODEL), F32), jax.ShapeDtypeStruct((t, D_MODEL), BF16)],
        compiler_params=_cp(("parallel",), VMEM_LIMIT),
    )(x, y_a, o_b, proj, proj, proj, w_on, wa, wb, w_out)


def mid_bwd(dx1, merged, y_a, o_b, proj, w_on, wa, wb, w_out, tm=256):
    t = dx1.shape[0]
    tm = min(tm, t)

    def body(dx1_ref, mg_ref, ya_ref, ob_ref, z_ref, ga_ref, gb_ref, won_ref, wa_ref, wb_ref, wo_ref,
             dya_ref, dob_ref, dz_ref, dg_ref, dwo_ref, dwa_ref, dwb_ref, dwon_ref):
        @pl.when(pl.program_id(0) == 0)
        def _():
            dwo_ref[...] = jnp.zeros_like(dwo_ref)
            dwa_ref[...] = jnp.zeros_like(dwa_ref)
            dwb_ref[...] = jnp.zeros_like(dwb_ref)
            dwon_ref[...] = jnp.zeros_like(dwon_ref)

        dx1b = _bf(dx1_ref[...])
        dmerged = _nt(dx1b, wo_ref[...])
        dwo_ref[...] += _tn(mg_ref[...], dx1b)
        o = ob_ref[...]
        z = z_ref[...].astype(F32)
        yb, r4, n4, w4, sz, silu = _gated_onorm(o, z, won_ref[...])
        yab, ybb = _bf(ya_ref[...]), _bf(yb)
        ua = _nn(yab, wa_ref[...])
        ub = _nn(ybb, wb_ref[...])
        sa, sb = _sigmoid(ga_ref[...].astype(F32)), _sigmoid(gb_ref[...].astype(F32))
        dua, dub = _bf(dmerged * sa), _bf(dmerged * sb)
        dg_ref[:, 0:D_MODEL] = _bf(dmerged * ua * sa * (1.0 - sa))
        dg_ref[:, D_MODEL:2 * D_MODEL] = _bf(dmerged * ub * sb * (1.0 - sb))
        dwa_ref[...] += _tn(yab, dua)
        dwb_ref[...] += _tn(ybb, dub)
        dya_ref[...] = _nt(dua, wa_ref[...])
        dyb = _nt(dub, wb_ref[...])
        dz_ref[...] = _bf(dyb * (n4 * w4) * (sz * (1.0 + z * (1.0 - sz))))
        dnw = dyb * silu
        dwon = jnp.zeros((1, B_DIM), F32)
        for h in range(B_HEADS):
            sl = slice(h * B_DIM, (h + 1) * B_DIM)
            dxh, dgh = _rms_bwd(dnw[:, sl], won_ref[...], r4[h], n4[:, sl])
            dob_ref[:, sl] = dxh
            dwon = dwon + jnp.sum(dgh, axis=0, keepdims=True)
        dwon_ref[...] += jnp.broadcast_to(dwon, (8, B_DIM))

    rowd = pl.BlockSpec((tm, D_MODEL), lambda i: (i, 0))
    row5 = pl.BlockSpec((tm, 512), lambda i: (i, 0))
    full = lambda a: pl.BlockSpec(a.shape, lambda i: (0,) * a.ndim)
    fixed = lambda shp: pl.BlockSpec(shp, lambda i: (0,) * len(shp))
    return pl.pallas_call(
        body, name="mid_bwd", grid=(t // tm,),
        in_specs=[rowd, rowd, row5, row5,
                  pl.BlockSpec((tm, 512), lambda i: (i, P_Z // 512)),
                  pl.BlockSpec((tm, D_MODEL), lambda i: (i, 0)),
                  pl.BlockSpec((tm, D_MODEL), lambda i: (i, 1)),
                  full(w_on), full(wa), full(wb), full(w_out)],
        out_specs=[row5, row5, row5, pl.BlockSpec((tm, 2 * D_MODEL), lambda i: (i, 0)),
                   fixed((D_MODEL, D_MODEL)), fixed((A_WIDTH, D_MODEL)), fixed((B_WIDTH, D_MODEL)),
                   fixed((8, B_DIM))],
        out_shape=[jax.ShapeDtypeStruct((t, 512), F32), jax.ShapeDtypeStruct((t, 512), F32),
                   jax.ShapeDtypeStruct((t, 512), BF16), jax.ShapeDtypeStruct((t, 2 * D_MODEL), BF16),
           jax.ShapeDtypeStruct((D_MODEL, D_MODEL), F32), jax.ShapeDtypeStruct((A_WIDTH, D_MODEL), F32),
           jax.ShapeDtypeStruct((B_WIDTH, D_MODEL), F32), jax.ShapeDtypeStruct((8, B_DIM), F32)],
        compiler_params=_cp(("arbitrary",), VMEM_LIMIT),
    )(dx1, merged, y_a, o_b, proj, proj, proj, w_on, wa, wb, w_out)


FFN_TF = 1408


def ffn_up(x1, g, w_gu, tm=512, tf=FFN_TF):
    t = x1.shape[0]
    tm = min(tm, t)
    nf = D_FF // tf

    def body(x_ref, g_ref, wg_ref, wu_ref, gate_ref, up_ref, act_ref, h_ref):
        @pl.when(pl.program_id(1) == 0)
        def _():
            r, n = _rms(x_ref[...])
            h_ref[...] = _bf(n * g_ref[...])

        hb = h_ref[...]
        gate = _nn(hb, wg_ref[...])
        up = _nn(hb, wu_ref[...])
        gate_ref[...] = _bf(gate)
        up_ref[...] = _bf(up)
        act_ref[...] = _bf(gate * _sigmoid(gate) * up)

    ff = pl.BlockSpec((tm, tf), lambda i, j: (i, j))
    return pl.pallas_call(
        body, name="ffn_up", grid=(t // tm, nf),
        in_specs=[pl.BlockSpec((tm, D_MODEL), lambda i, j: (i, 0)),
                  pl.BlockSpec((1, D_MODEL), lambda i, j: (0, 0)),
                  pl.BlockSpec((D_MODEL, tf), lambda i, j: (0, j)),
                  pl.BlockSpec((D_MODEL, tf), lambda i, j: (0, nf + j))],
        out_specs=[ff, ff, ff, pl.BlockSpec((tm, D_MODEL), lambda i, j: (i, 0))],
        out_shape=[jax.ShapeDtypeStruct((t, D_FF), BF16)] * 3 + [jax.ShapeDtypeStruct((t, D_MODEL), BF16)],
        compiler_params=_cp(("parallel", "arbitrary"), VMEM_LIMIT),
    )(x1, g, w_gu, w_gu)


def matmul_residual(a, w, res, name, tm=512, tk=FFN_TF):
    t, k = a.shape
    n = w.shape[1]
    tm = min(tm, t)

    def body(a_ref, w_ref, r_ref, o_ref):
        @pl.when(pl.program_id(1) == 0)
        def _():
            o_ref[...] = r_ref[...]

        o_ref[...] += _nn(a_ref[...], w_ref[...])

    return pl.pallas_call(
        body, name=name, grid=(t // tm, k // tk),
        in_specs=[pl.BlockSpec((tm, tk), lambda i, j: (i, j)),
                  pl.BlockSpec((tk, n), lambda i, j: (j, 0)),
                  pl.BlockSpec((tm, n), lambda i, j: (i, 0))],
        out_specs=pl.BlockSpec((tm, n), lambda i, j: (i, 0)),
        out_shape=jax.ShapeDtypeStruct((t, n), F32),
        compiler_params=_cp(("parallel", "arbitrary"), VMEM_LIMIT),
    )(a, w, res)


def ffn_act_bwd(dx2, gate, up, w_down, tm=512, tf=FFN_TF):
    t = dx2.shape[0]
    tm = min(tm, t)

    def body(dx2_ref, gate_ref, up_ref, wd_ref, dgate_ref, dup_ref, dx2b_ref):
        @pl.when(pl.program_id(1) == 0)
        def _():
            dx2b_ref[...] = _bf(dx2_ref[...])

        dact = _nt(dx2b_ref[...], wd_ref[...])
        gt, upv = gate_ref[...].astype(F32), up_ref[...].astype(F32)
        sg = _sigmoid(gt)
        dgate_ref[...] = _bf(dact * upv * (sg * (1.0 + gt * (1.0 - sg))))
        dup_ref[...] = _bf(dact * (gt * sg))

    ff = pl.BlockSpec((tm, tf), lambda i, j: (i, j))
    return pl.pallas_call(
        body, name="ffn_act_bwd", grid=(t // tm, D_FF // tf),
        in_specs=[pl.BlockSpec((tm, D_MODEL), lambda i, j: (i, 0)), ff, ff,
                  pl.BlockSpec((tf, D_MODEL), lambda i, j: (j, 0))],
        out_specs=[ff, ff],
        out_shape=[jax.ShapeDtypeStruct((t, D_FF), BF16)] * 2,
        scratch_shapes=[pltpu.VMEM((tm, D_MODEL), BF16)],
        compiler_params=_cp(("parallel", "arbitrary"), VMEM_LIMIT),
    )(dx2, gate, up, w_down)


def tail_fwd_bwd(x2, p, target, g_ple, g_final, w_pg, w_pp, tm=256):
    t = x2.shape[0]
    tm = min(tm, t)

    def body(x_ref, p_ref, t_ref, gp_ref, gf_ref, wpg_ref, wpp_ref,
             dx_ref, dwpg_ref, dwpp_ref, dgp_ref, dgf_ref, loss_ref):
        @pl.when(pl.program_id(0) == 0)
        def _():
            dwpg_ref[...] = jnp.zeros_like(dwpg_ref)
            dwpp_ref[...] = jnp.zeros_like(dwpp_ref)
            dgp_ref[...] = jnp.zeros_like(dgp_ref)
            dgf_ref[...] = jnp.zeros_like(dgf_ref)
            loss_ref[...] = jnp.zeros_like(loss_ref)

        x2v = x_ref[...]
        gp, gf = gp_ref[...], gf_ref[...]
        r3, n3 = _rms(x2v)
        h3b = _bf(n3 * gp)
        pb = _bf(p_ref[...])
        pg = _sigmoid(_nn(h3b, wpg_ref[...]))
        pp = _nn(pb, wpp_ref[...])
        x3 = x2v + pg * pp
        r4, n4 = _rms(x3)
        err = n4 * gf - t_ref[...]
        part = 0.5 * jnp.sum(jnp.sum(err * err, axis=1, keepdims=True), axis=0, keepdims=True) / D_MODEL
        loss_ref[...] += jnp.broadcast_to(part, (8, 128))
        dy = err * (1.0 / D_MODEL)
        dx3, dgf = _rms_bwd(dy, gf, r4, n4)
        dgf_ref[...] += jnp.broadcast_to(jnp.sum(dgf, axis=0, keepdims=True), (8, D_MODEL))
        dzp = _bf(dx3 * pp * pg * (1.0 - pg))
        dpp = _bf(dx3 * pg)
        dwpg_ref[...] += _tn(h3b, dzp)
        dwpp_ref[...] += _tn(pb, dpp)
        dh3 = _nt(dzp, wpg_ref[...])
        dx, dgp = _rms_bwd(dh3, gp, r3, n3)
        dgp_ref[...] += jnp.broadcast_to(jnp.sum(dgp, axis=0, keepdims=True), (8, D_MODEL))
        dx_ref[...] = dx3 + dx

    rowd = pl.BlockSpec((tm, D_MODEL), lambda i: (i, 0))
    fixed = lambda shp: pl.BlockSpec(shp, lambda i: (0,) * len(shp))
    return pl.pallas_call(
        body, name="tail_fwd_bwd", grid=(t // tm,),
        in_specs=[rowd, pl.BlockSpec((tm, PLE_DIM), lambda i: (i, 0)), rowd,
                  fixed((1, D_MODEL)), fixed((1, D_MODEL)), fixed((D_MODEL, D_MODEL)), fixed((PLE_DIM, D_MODEL))],
        out_specs=[rowd, fixed((D_MODEL, D_MODEL)), fixed((PLE_DIM, D_MODEL)),
                   fixed((8, D_MODEL)), fixed((8, D_MODEL)), fixed((8, 128))],
        out_shape=[jax.ShapeDtypeStruct((t, D_MODEL), F32), jax.ShapeDtypeStruct((D_MODEL, D_MODEL), F32),
                   jax.ShapeDtypeStruct((PLE_DIM, D_MODEL), F32), jax.ShapeDtypeStruct((8, D_MODEL), F32),
                   jax.ShapeDtypeStruct((8, D_MODEL), F32), jax.ShapeDtypeStruct((8, 128), F32)],
        compiler_params=_cp(("arbitrary",), VMEM_LIMIT),
    )(x2, p, target, g_ple, g_final, w_pg, w_pp)


def in_proj_bwd(pieces, weights, x, dx1, g, name="in_proj_bwd", tm=256):
    t = x.shape[0]
    tm = min(tm, t)
    k = len(pieces)
    assert all(c0 % wd == 0 and w0 % wd == 0 for (_, c0, wd), (_, w0) in zip(pieces, weights))

    def body(*refs):
        p_refs, w_refs = refs[:k], refs[k:2 * k]
        x_ref, dx1_ref, g_ref, dx_ref, dg_ref = refs[2 * k:]

        @pl.when(pl.program_id(0) == 0)
        def _():
            dg_ref[...] = jnp.zeros_like(dg_ref)

        dh = _nt(_bf(p_refs[0][...]), w_refs[0][...])
        for pr, wr in zip(p_refs[1:], w_refs[1:]):
            dh = dh + _nt(_bf(pr[...]), wr[...])
        r, n = _rms(x_ref[...])
        dx, dgc = _rms_bwd(dh, g_ref[...], r, n)
        dx_ref[...] = dx1_ref[...] + dx
        dg_ref[...] += jnp.broadcast_to(jnp.sum(dgc, axis=0, keepdims=True), (8, D_MODEL))

    rowd = pl.BlockSpec((tm, D_MODEL), lambda i: (i, 0))
    return pl.pallas_call(
        body, name=name, grid=(t // tm,),
        in_specs=[pl.BlockSpec((tm, wd), functools.partial(lambda i, cb: (i, cb), cb=c0 // wd))
                  for _, c0, wd in pieces]
        + [pl.BlockSpec((w.shape[0], wd), functools.partial(lambda i, cb: (0, cb), cb=w0 // wd))
           for (w, w0), (_, _, wd) in zip(weights, pieces)]
        + [rowd, rowd, pl.BlockSpec((1, D_MODEL), lambda i: (0, 0))],
        out_specs=[rowd, pl.BlockSpec((8, D_MODEL), lambda i: (0, 0))],
        out_shape=[jax.ShapeDtypeStruct((t, D_MODEL), F32), jax.ShapeDtypeStruct((8, D_MODEL), F32)],
        compiler_params=_cp(("arbitrary",), VMEM_LIMIT),
    )(*[a for a, _, _ in pieces], *[w for w, _ in weights], x, dx1, g)


def adamw(w, g, m, v, name, rows_cap=256, dep=None):
    lead = w.shape[:-2]
    r, c = w.shape[-2:]
    tr = r
    for cand in range(8, min(r, rows_cap) + 1, 8):
        if r % cand == 0:
            tr = cand

    def body(w_ref, g_ref, m_ref, v_ref, *rest):
        d_ref, mo_ref, vo_ref = rest[-3:]
        gv = g_ref[...]
        mn = ADAM_B1 * m_ref[...] + (1.0 - ADAM_B1) * gv
        vn = ADAM_B2 * v_ref[...] + (1.0 - ADAM_B2) * (gv * gv)
        m_hat = mn / (1.0 - ADAM_B1 ** ADAM_STEP)
        v_hat = vn / (1.0 - ADAM_B2 ** ADAM_STEP)
        d_ref[...] = -ADAM_LR * (m_hat / (jnp.sqrt(v_hat) + ADAM_EPS) + ADAM_WD * w_ref[...])
        mo_ref[...] = mn
        vo_ref[...] = vn

    spec = pl.BlockSpec((None,) * len(lead) + (tr, c), lambda i: (0,) * len(lead) + (i, 0))
    extra = [] if dep is None else [dep]
    return pl.pallas_call(
        body, name=name, grid=(r // tr,),
        in_specs=[spec] * 4 + [pl.BlockSpec((8, 128), lambda i: (0, 0))] * len(extra), out_specs=[spec] * 3,
        out_shape=[jax.ShapeDtypeStruct(w.shape, F32)] * 3,
        compiler_params=_cp(("parallel",), VMEM_LIMIT),
    )(w, g.reshape(w.shape), m, v, *extra)


class Standalone:
    def __init__(self, later_weights):
        self.later_weights = later_weights

    def begin(self, *a):
        return 0.0

    forward = exchange = join = begin

    def finish(self, after):
        return self.later_weights


def local_step(x3d, p3d, target3d, g4, small, later, early):
    b, s, _ = x3d.shape
    t = b * s
    x = x3d.reshape(t, D_MODEL)
    p = p3d.reshape(t, PLE_DIM)
    target = target3d.reshape(t, D_MODEL)
    cut = SPLIT_Z - 2 * (D_IN // N_CHIPS)
    w_inp = jnp.concatenate([g4[2][:, cut + 8:], g4[3], g4[0], g4[1], g4[2][:, :cut], g4[2][:, cut:cut + 8],
                             jnp.zeros((D_MODEL, 120), BF16)], axis=1)
    al_row = jnp.pad(small["a_log"].reshape(1, B_HEADS), ((0, 0), (0, 128 - B_HEADS)))
    dtb_row = jnp.pad(small["dt_bias"].reshape(1, B_HEADS), ((0, 0), (0, 128 - B_HEADS)))
    conv_w8 = jnp.pad(small["conv_w"].reshape(CONV_K, CONV_CH), ((0, 8 - CONV_K), (0, 0)))
    w_on = small["w_onorm"].reshape(1, B_DIM)
    g_mix, g_ffn = small["g_mix"].reshape(1, D_MODEL), small["g_ffn"].reshape(1, D_MODEL)
    g_ple, g_final = small["g_ple"].reshape(1, D_MODEL), small["g_final"].reshape(1, D_MODEL)
    bias_band = make_bias_band(small["rel_bias"].reshape(A_HEADS, N_REL))

    tok = later.begin()
    proj, h1, bd32 = rms_matmul(x, g_mix + tok, w_inp, "in_proj", tm=1024)
    y_a, lse = attn_fwd(proj, bias_band, b, s)
    tok = later.forward(lse)
    c = conv_fwd(proj, conv_w8 + tok, b, s)
    u, wk, qg, kdec, pm, egl, tmat = dn_prep(c, bd32, al_row, dtb_row, b, s)
    o_b, states = dn_scan_fwd(u, wk, qg, kdec, pm, egl, b, s)
    wts = later.finish(o_b)
    x1, merged = mid_fwd(x, y_a, o_b, proj, w_on, wts["w_branch_a"], wts["w_branch_b"], wts["w_out"])
    gate, up, act, h2 = ffn_up(x1, g_ffn, wts["w_gate_up"])
    x2 = matmul_residual(act, wts["w_down"], x1, "ffn_down")

    dx2, dw_pg, dw_pp, dg_ple, dg_final, loss = tail_fwd_bwd(
        x2, p, target, g_ple, g_final, wts["w_ple_gate"], wts["w_ple_proj"])
    dgate, dup = ffn_act_bwd(dx2, gate, up, wts["w_down"])
    w_gu = wts["w_gate_up"]
    dx1, dg_ffn = in_proj_bwd([(dgate, 0, D_FF), (dup, 0, D_FF)], [(w_gu, 0), (w_gu, D_FF)], x1, dx2, g_ffn,
                              name="ffn_in_bwd")
    dw_down = matmul_tn(act, dx2, "dw_down")
    dw_gu = matmul_tn(h2, dgate, "dw_gate", width=2 * D_FF)
    dw_gu = matmul_tn(h2, dup, "dw_up", into=dw_gu, col0=D_FF)
    dy_a, do_b, dz, dgates, dw_out, dwa, dwb, dw_on = mid_bwd(
        dx1, merged, y_a, o_b, proj, w_on, wts["w_branch_a"], wts["w_branch_b"], wts["w_out"])
    tok = early.begin(dict(w_branch_a=dwa, w_branch_b=dwb, w_out=dw_out, w_gate_up=dw_gu, w_down=dw_down,
                           w_ple_gate=dw_pg, w_ple_proj=dw_pp))
    ddw, ddwk, ddqg, ddkdec, ddp, ddegl = dn_scan_bwd(u, wk, qg, kdec, pm, egl + tok, states, do_b, b, s)
    tok = early.exchange(ddegl)
    dc3, dbd, dal, ddtb = dn_post_bwd(c, bd32, al_row + tok, dtb_row, tmat, ddw, ddwk, ddqg, ddkdec, ddp, ddegl, b, s)
    dconv, dconv_w = conv_bwd(proj, conv_w8, dc3, b, s)
    dqa, dka, dva, dbt, dbf = attn_bwd(proj, bias_band, y_a, lse, dy_a, b, s)
    tok = early.join(dqa)
    d_rel = bias_band_grad(dbt, dbf)

    pieces = [dgates, dqa, dka, dva, dconv, dz, dbd]
    bounds = [0, 2048, 2560, 3072, 3584, 5120, 5632, 5760]
    windows = [(dgates, 0, 2048), (dqa, 0, 512), (dka, 0, 512), (dva, 0, 512), (dconv, 0, 512), (dconv, 512, 512),
               (dconv, 1024, 512), (dz, 0, 512), (dbd, 0, 128)]
    w_cols = [0, P_QA, P_KA, P_VA, P_CONV, P_CONV + 512, P_CONV + 1024, P_Z, P_BD]
    dx, dg_mix = in_proj_bwd(windows, [(w_inp, c0) for c0 in w_cols], x, dx1, g_mix + tok)
    dwp = None
    for k, pc in enumerate(pieces):
        dwp = matmul_tn(h1, pc, "dw_in_%d" % k, into=dwp, col0=bounds[k], width=P_WIDTH)
    reduced_early = early.finish(dwp)
    dw_in = jnp.concatenate([dwp[:, P_QA:P_BD + 8], dwp[:, :P_QA]], axis=1)

    grads = dict(w_in=dw_in, w_branch_a=dwa, w_branch_b=dwb, w_out=dw_out, w_gate_up=dw_gu, w_down=dw_down,
                 w_ple_gate=dw_pg, w_ple_proj=dw_pp)
    small_grads = dict(g_mix=dg_mix[0], g_ffn=dg_ffn[0], g_ple=dg_ple[0], g_final=dg_final[0],
                       conv_w=dconv_w[:CONV_K].reshape(-1), rel_bias=d_rel.reshape(-1), w_onorm=dw_on[0],
                       a_log=dal[0, :B_HEADS], dt_bias=ddtb[0, :B_HEADS], loss=loss[0, :1])
    return dx.reshape(b, s, D_MODEL), grads, small_grads, reduced_early


BIG = (("w_in", (D_MODEL, D_IN), 1), ("w_branch_a", (A_WIDTH, D_MODEL), 1), ("w_branch_b", (B_WIDTH, D_MODEL), 1),
       ("w_out", (D_MODEL, D_MODEL), 0), ("w_gate_up", (D_MODEL, 2 * D_FF), 1), ("w_down", (D_FF, D_MODEL), 0),
       ("w_ple_gate", (D_MODEL, D_MODEL), 0), ("w_ple_proj", (PLE_DIM, D_MODEL), 1))
N_CHIPS = 4
FIRST_WEIGHTS = ("w_in",)
LATER_WEIGHTS = ("w_branch_a", "w_branch_b", "w_out", "w_gate_up", "w_down", "w_ple_gate", "w_ple_proj")
LATE_GRADS = ("w_in",)
EARLY_GRADS = ("w_branch_a", "w_branch_b", "w_out", "w_gate_up", "w_down", "w_ple_gate", "w_ple_proj")


def _items(names):
    return [it for it in BIG if it[0] in names]


def _shard_shape(shape, axis):
    return (shape[0] // N_CHIPS, shape[1]) if axis == 0 else (shape[0], shape[1] // N_CHIPS)


def _pack_rows_of(names):
    return -(-sum(sh[0] * sh[1] for _, sh, _ in _items(names)) // (N_CHIPS * 128 * 512)) * 512


def _pack_rows(parts, total):
    used = sum(a.shape[-2] for a in parts)
    pad = jnp.zeros(parts[0].shape[:-2] + (total - used, 128), parts[0].dtype)
    return jnp.concatenate(parts + [pad], axis=-2)


def pack_grads(grads, names):
    parts = []
    for n, shape, axis in _items(names):
        rs, cs = _shard_shape(shape, axis)
        g = grads[n].astype(BF16)
        seg = g.reshape(N_CHIPS, rs, cs) if axis == 0 else jnp.transpose(g.reshape(rs, N_CHIPS, cs), (1, 0, 2))
        parts.append(seg.reshape(N_CHIPS, -1, 128))
    return _pack_rows(parts, _pack_rows_of(names))


def unpack_shard(flat, names):
    out, r0 = {}, 0
    for n, shape, axis in _items(names):
        rs, cs = _shard_shape(shape, axis)
        nr = rs * cs // 128
        out[n] = flat[r0:r0 + nr].reshape(rs, cs)
        r0 += nr
    return out


def _place():
    return lax.axis_index("x"), lax.axis_index("y"), lax.axis_index("c")


ANY = pl.BlockSpec(memory_space=pl.ANY)


def _gathered_shape(item):
    n, shape, _ = item
    return (N_CHIPS,) + _shard_shape(shape, 1) if n == "w_in" else shape


def _gather_block(o_ref, item, cx, cy, hf):
    n, shape, axis = item
    rs, cs = _shard_shape(shape, axis)
    hr = rs // 2
    ci = 2 * cx + cy
    if n == "w_in":
        return o_ref.at[ci, pl.ds(pl.multiple_of(hf * hr, 16), hr), :]
    if axis == 0:
        return o_ref.at[pl.ds(pl.multiple_of(ci * rs + hf * hr, 16), hr), :]
    return o_ref.at[pl.ds(pl.multiple_of(hf * hr, 16), hr), pl.ds(pl.multiple_of(ci * cs, 128), cs)]


def _own_half(w_ref, item, c):
    hr = _shard_shape(item[1], item[2])[0] // 2
    return w_ref.at[pl.ds(pl.multiple_of(c * hr, 16), hr), :]


def _gather_slot(o_ref, item, cx, cy):
    n, shape, axis = item
    rs, cs = _shard_shape(shape, axis)
    ci = 2 * cx + cy
    if n == "w_in":
        return o_ref.at[ci]
    if axis == 0:
        return o_ref.at[pl.ds(pl.multiple_of(ci * rs, 16), rs), :]
    return o_ref.at[:, pl.ds(pl.multiple_of(ci * cs, 128), cs)]


def _other_chips(x, y):
    return [(1 - x, y), (x, 1 - y), (1 - x, 1 - y)]


def allgather_weights(shards, names, chip):
    items = _items(names)
    nw = len(items)

    def body(*refs):
        w_refs, o_refs = refs[:nw], refs[nw:2 * nw]
        send_sems, recv_sems = refs[2 * nw:]
        x, y, c = _place()
        sibling = (x, y, 1 - c)
        chips = _other_chips(x, y)

        def copy(k, src, dst, to):
            return pltpu.make_async_remote_copy(src_ref=src, dst_ref=dst, send_sem=send_sems.at[k],
                                                recv_sem=recv_sems.at[k], device_id=to, device_id_type=MESH)

        def blk(i, cx, cy, hf):
            return _gather_block(o_refs[i], items[i], cx, cy, hf)

        def my_half(i):
            return _own_half(w_refs[i], items[i], c)

        def own(i):
            return _gather_slot(o_refs[i], items[i], x, y)

        first = [copy(7 * i + j, my_half(i), blk(i, x, y, c), (*chip_, c))
                 for i in range(nw) for j, chip_ in enumerate(chips)]
        first += [copy(7 * i + 6, w_refs[i], own(i), sibling) for i in range(nw)]
        for cp in first:
            cp.start()
        passed = []
        for i in range(nw):
            for j, chip_ in enumerate(chips):
                copy(7 * i + j, my_half(i), blk(i, *chip_, c), (*chip_, c)).wait_recv()
                fwd = copy(7 * i + 3 + j, blk(i, *chip_, c), blk(i, *chip_, c), sibling)
                fwd.start()
                passed.append(fwd)
        for i in range(nw):
            for j, chip_ in enumerate(chips):
                copy(7 * i + 3 + j, my_half(i), blk(i, *chip_, 1 - c), sibling).wait_recv()
            copy(7 * i + 6, w_refs[i], own(i), sibling).wait_recv()
        for cp in first + passed:
            cp.wait_send()

    outs = pl.pallas_call(
        body, name="allgather_weights",
        in_specs=[ANY] * nw, out_specs=[ANY] * nw,
        out_shape=[jax.ShapeDtypeStruct(_gathered_shape(it), BF16) for it in items],
        scratch_shapes=[pltpu.SemaphoreType.DMA((7 * nw,)), pltpu.SemaphoreType.DMA((7 * nw,))],
    )(*[shards[it[0]] for it in items])
    return {it[0]: o for it, o in zip(items, outs)}


HBM_SPEC = pl.BlockSpec(memory_space=pltpu.HBM)
SEM_SPEC = pl.BlockSpec(memory_space=pltpu.SEMAPHORE)
EFFECT = pltpu.SideEffectType.DATAFLOW_SIDE_EFFECTING


def _in_hbm(a):
    return pltpu.with_memory_space_constraint(a, pltpu.HBM)


def copies_start(name, bufs, ncopies, plan):
    nb = len(bufs)

    def body(*refs):
        in_refs, send_sems, recv_sems, token = refs[:nb], refs[nb], refs[nb + 1], refs[-1]
        for k, (src, dst, to) in enumerate(plan(in_refs)):
            pltpu.make_async_remote_copy(src_ref=src, dst_ref=dst, send_sem=send_sems.at[k],
                                         recv_sem=recv_sems.at[k], device_id=to, device_id_type=MESH).start()
        token[...] = jnp.zeros_like(token)

    outs = pl.pallas_call(
        body, name=name,
        in_specs=[HBM_SPEC] * nb,
        out_specs=(SEM_SPEC, SEM_SPEC, *[HBM_SPEC] * nb, pl.BlockSpec(memory_space=pltpu.VMEM)),
        out_shape=(pltpu.SemaphoreType.DMA((ncopies,)), pltpu.SemaphoreType.DMA((ncopies,)),
                   *[pltpu.HBM(b.shape, b.dtype) for b in bufs], jax.ShapeDtypeStruct((8, 128), F32)),
        input_output_aliases={i: 2 + i for i in range(nb)},
        compiler_params=pltpu.CompilerParams(has_side_effects=EFFECT),
    )(*[_in_hbm(b) for b in bufs])
    return outs[0], outs[1], list(outs[2:2 + nb]), outs[-1][0, 0]


def copies_wait(name, send_sems, recv_sems, bufs, after, plan):
    nb = len(bufs)

    def body(*refs):
        in_refs, s_sems, r_sems = refs[:nb], refs[nb], refs[nb + 1]
        for k, (src, dst, to) in enumerate(plan(in_refs)):
            cp = pltpu.make_async_remote_copy(src_ref=src, dst_ref=dst, send_sem=s_sems.at[k],
                                              recv_sem=r_sems.at[k], device_id=to, device_id_type=MESH)
            cp.wait_send()
            cp.wait_recv()

    return list(pl.pallas_call(
        body, name=name,
        in_specs=[HBM_SPEC] * nb + [SEM_SPEC, SEM_SPEC, ANY],
        out_specs=tuple([HBM_SPEC] * nb),
        out_shape=tuple(pltpu.HBM(b.shape, b.dtype) for b in bufs),
        input_output_aliases={i: i for i in range(nb)},
        compiler_params=pltpu.CompilerParams(has_side_effects=EFFECT),
    )(*bufs, send_sems, recv_sems, after))


def _landing(shape, dtype):
    return _in_hbm(lax.empty(shape, dtype))


class LaterWeights:
    def __init__(self, shards, chip):
        self.items = _items(LATER_WEIGHTS)
        self.shards, self.chip = shards, chip
        self.nw = len(self.items)

    def _ici_plan(self, refs):
        x, y, c = _place()
        w_refs, o_refs = refs[:self.nw], refs[self.nw:]
        plan = [(_own_half(w_refs[i], it, c), _gather_block(o_refs[i], it, x, y, c), (*chip_, c))
                for i, it in enumerate(self.items) for chip_ in _other_chips(x, y)]
        return plan + [(w_refs[i], _gather_slot(o_refs[i], it, x, y), (x, y, 1 - c))
                       for i, it in enumerate(self.items)]

    def _d2d_plan(self, refs):
        x, y, c = _place()
        return [(_gather_block(refs[i], it, *chip_, c), _gather_block(refs[i], it, *chip_, c), (x, y, 1 - c))
                for i, it in enumerate(self.items) for chip_ in _other_chips(x, y)]

    def _d2d_wait_plan(self, refs):
        x, y, c = _place()
        return [(_gather_block(refs[i], it, *chip_, c), _gather_block(refs[i], it, *chip_, 1 - c), (x, y, 1 - c))
                for i, it in enumerate(self.items) for chip_ in _other_chips(x, y)]

    def _ici_wait_plan(self, refs):
        x, y, c = _place()
        w_refs, o_refs = refs[:self.nw], refs[self.nw:]
        plan = [(_own_half(w_refs[i], it, c), _gather_block(o_refs[i], it, *chip_, c), (*chip_, c))
                for i, it in enumerate(self.items) for chip_ in _other_chips(x, y)]
        return plan + [(w_refs[i], _gather_slot(o_refs[i], it, x, y), (x, y, 1 - c))
                       for i, it in enumerate(self.items)]

    def begin(self):
        srcs = [self.shards[it[0]] for it in self.items]
        lands = [_landing(_gathered_shape(it), BF16) for it in self.items]
        self.s1, self.r1, self.b1, tok = copies_start("gather_ici_start", srcs + lands, 4 * self.nw, self._ici_plan)
        return tok

    def forward(self, after):
        b1 = copies_wait("gather_ici_wait", self.s1, self.r1, self.b1, after, self._ici_wait_plan)
        self.s2, self.r2, self.b2, tok = copies_start("gather_d2d_start", b1[self.nw:], 3 * self.nw, self._d2d_plan)
        return tok

    def finish(self, after):
        outs = copies_wait("gather_d2d_wait", self.s2, self.r2, self.b2, after, self._d2d_wait_plan)
        return {it[0]: o for it, o in zip(self.items, outs)}


def small_allreduce(v, name):
    r = v.shape[0]

    def body(v_ref, o_ref, buf, send_sems, recv_sems):
        x, y, c = _place()
        me = 4 * x + 2 * y + c
        buf[me] = v_ref[...]
        flips = [(fx, fy, fc) for fx in (0, 1) for fy in (0, 1) for fc in (0, 1)][1:]
        peers = [((1 - x) if fx else x, (1 - y) if fy else y, (1 - c) if fc else c) for fx, fy, fc in flips]

        def copy(k, slot, to):
            return pltpu.make_async_remote_copy(src_ref=v_ref, dst_ref=buf.at[slot], send_sem=send_sems.at[k],
                                                recv_sem=recv_sems.at[k], device_id=to, device_id_type=MESH)

        sends = [copy(k, me, peer) for k, peer in enumerate(peers)]
        for cp in sends:
            cp.start()
        for k, (px, py, pc) in enumerate(peers):
            copy(k, 4 * px + 2 * py + pc, (px, py, pc)).wait_recv()
        for cp in sends:
            cp.wait_send()
        acc = buf[0]
        for d in range(1, 8):
            acc = acc + buf[d]
        o_ref[...] = acc

    return pl.pallas_call(
        body, name=name,
        in_specs=[pl.BlockSpec(memory_space=pltpu.VMEM)], out_specs=pl.BlockSpec(memory_space=pltpu.VMEM),
        out_shape=jax.ShapeDtypeStruct((r, 128), F32),
        scratch_shapes=[pltpu.VMEM((8, r, 128), F32), pltpu.SemaphoreType.DMA((7,)), pltpu.SemaphoreType.DMA((7,))],
    )(v)


def swap_halves(g):
    half = g.shape[1] // 2

    def body(g_ref, o_ref, send_sem, recv_sem):
        x, y, c = _place()
        cp = pltpu.make_async_remote_copy(
            src_ref=g_ref.at[:, pl.ds((1 - c) * half, half), :], dst_ref=o_ref, send_sem=send_sem,
            recv_sem=recv_sem, device_id=(x, y, 1 - c), device_id_type=MESH)
        cp.start()
        cp.wait()

    return pl.pallas_call(
        body, name="swap_halves", in_specs=[ANY], out_specs=ANY,
        out_shape=jax.ShapeDtypeStruct((N_CHIPS, half, 128), g.dtype),
        scratch_shapes=[pltpu.SemaphoreType.DMA, pltpu.SemaphoreType.DMA],
    )(g)


def add_halves(g, other, place):
    half = other.shape[1]
    tr = _tile_rows(half)
    nblk = half // tr

    def body(pref, g0, g1, g2, g3, o0, o1, o2, o3, pf_ref, pb_ref):
        f = lambda r: r[...].astype(F32)
        pf_ref[...] = f(g0) + f(o0)
        pb_ref[0] = _bf(f(g1) + f(o1))
        pb_ref[1] = _bf(f(g2) + f(o2))
        pb_ref[2] = _bf(f(g3) + f(o3))

    gspec = lambda k: pl.BlockSpec((None, tr, 128), lambda i, pr: ((pr[0] + k) % N_CHIPS, pr[1] * nblk + i, 0))
    ospec = lambda k: pl.BlockSpec((None, tr, 128), lambda i, pr: ((pr[0] + k) % N_CHIPS, i, 0))
    return pl.pallas_call(
        body, name="add_halves",
        grid_spec=pltpu.PrefetchScalarGridSpec(
            num_scalar_prefetch=1, grid=(nblk,),
            in_specs=[gspec(0), gspec(1), gspec(2), gspec(3), ospec(0), ospec(1), ospec(2), ospec(3)],
            out_specs=[pl.BlockSpec((tr, 128), lambda i, pr: (i, 0)),
                       pl.BlockSpec((3, tr, 128), lambda i, pr: (0, i, 0))]),
        out_shape=[jax.ShapeDtypeStruct((half, 128), F32), jax.ShapeDtypeStruct((3, half, 128), BF16)],
        compiler_params=_cp(("parallel",), VMEM_LIMIT),
    )(place, g, g, g, g, other, other, other, other)


def _tile_rows(n, cap=2048):
    best = 16
    for t in range(16, cap + 1, 16):
        if n % t == 0:
            best = t
    assert n % best == 0
    return best


def exchange_partials(pb):
    def body(p_ref, o_ref, send_sems, recv_sems):
        x, y, c = _place()
        me = 2 * x + y
        cps = []
        for k in range(1, N_CHIPS):
            to = (me + k) % N_CHIPS
            cps.append(pltpu.make_async_remote_copy(
                src_ref=p_ref.at[k - 1], dst_ref=o_ref.at[k - 1], send_sem=send_sems.at[k - 1],
                recv_sem=recv_sems.at[k - 1], device_id=(to // 2, to % 2, c), device_id_type=MESH))
        for cp in cps:
            cp.start()
        for cp in cps:
            cp.wait()

    return pl.pallas_call(
        body, name="exchange_partials", in_specs=[ANY], out_specs=ANY,
        out_shape=jax.ShapeDtypeStruct(pb.shape, pb.dtype),
        scratch_shapes=[pltpu.SemaphoreType.DMA((3,)), pltpu.SemaphoreType.DMA((3,))],
    )(pb)


def add_partials(pf, got, place):
    half = pf.shape[0]
    tr = _tile_rows(half)

    def body(pref, pf_ref, got_ref, o_ref):
        o_ref[...] = ((pf_ref[...] + got_ref[0].astype(F32)) + got_ref[1].astype(F32)) + got_ref[2].astype(F32)

    return pl.pallas_call(
        body, name="add_partials",
        grid_spec=pltpu.PrefetchScalarGridSpec(
            num_scalar_prefetch=1, grid=(half // tr,),
            in_specs=[pl.BlockSpec((tr, 128), lambda i, pr: (i, 0)),
                      pl.BlockSpec((3, tr, 128), lambda i, pr: (0, i, 0))],
            out_specs=pl.BlockSpec((None, tr, 128), lambda i, pr: (pr[1], i, 0))),
        out_shape=jax.ShapeDtypeStruct((2, half, 128), F32),
        compiler_params=_cp(("parallel",), VMEM_LIMIT),
    )(place, pf, got)


def join_halves(both):
    def body(r_ref, o_ref, send_sem, recv_sem):
        x, y, c = _place()
        cp = pltpu.make_async_remote_copy(src_ref=r_ref.at[c], dst_ref=o_ref.at[c], send_sem=send_sem,
                                          recv_sem=recv_sem, device_id=(x, y, 1 - c), device_id_type=MESH)
        cp.start()
        pltpu.make_async_remote_copy(src_ref=r_ref.at[c], dst_ref=o_ref.at[1 - c], send_sem=send_sem,
                                     recv_sem=recv_sem, device_id=(x, y, 1 - c), device_id_type=MESH).wait_recv()
        cp.wait_send()

    return pl.pallas_call(
        body, name="join_halves", in_specs=[ANY], out_specs=ANY,
        out_shape=jax.ShapeDtypeStruct(both.shape, F32),
        scratch_shapes=[pltpu.SemaphoreType.DMA, pltpu.SemaphoreType.DMA],
        input_output_aliases={0: 0},
    )(both)


def reduce_scatter_grads(gpack, place):
    other = swap_halves(gpack)
    pf, pb = add_halves(gpack, other, place)
    got = exchange_partials(pb)
    return join_halves(add_partials(pf, got, place)).reshape(gpack.shape[1], 128)


class EarlyGrads:
    def __init__(self, place, names=EARLY_GRADS, tag="grads"):
        self.place, self.names, self.tag = place, names, tag

    @staticmethod
    def _swap_plan(refs):
        x, y, c = _place()
        g_ref, o_ref = refs
        half = o_ref.shape[1]
        return [(g_ref.at[:, pl.ds(pl.multiple_of((1 - c) * half, 16), half), :], o_ref, (x, y, 1 - c))]

    @staticmethod
    def _exchange_plan(refs):
        x, y, c = _place()
        p_ref, o_ref = refs
        me = 2 * x + y
        return [(p_ref.at[k - 1], o_ref.at[k - 1], (((me + k) % N_CHIPS) // 2, ((me + k) % N_CHIPS) % 2, c))
                for k in range(1, N_CHIPS)]

    @staticmethod
    def _join_plan(refs):
        x, y, c = _place()
        return [(refs[0].at[c], refs[0].at[c], (x, y, 1 - c))]

    @staticmethod
    def _join_wait_plan(refs):
        x, y, c = _place()
        return [(refs[0].at[c], refs[0].at[1 - c], (x, y, 1 - c))]

    def begin(self, grads):
        g = pack_grads(grads, self.names)
        land = _landing((N_CHIPS, g.shape[1] // 2, 128), BF16)
        self.s1, self.r1, self.b1, tok = copies_start(self.tag + "_swap_start", [g, land], 1, self._swap_plan)
        return tok

    def exchange(self, after):
        g, other = copies_wait(self.tag + "_swap_wait", self.s1, self.r1, self.b1, after, self._swap_plan)
        self.pf, pb = add_halves(g, other, self.place)
        land = _landing(pb.shape, BF16)
        self.s2, self.r2, self.b2, tok = copies_start(self.tag + "_exchange_start", [pb, land], 3,
                                                      self._exchange_plan)
        return tok

    def join(self, after):
        _, got = copies_wait(self.tag + "_exchange_wait", self.s2, self.r2, self.b2, after, self._exchange_plan)
        both = add_partials(self.pf, got, self.place)
        self.s3, self.r3, self.b3, tok = copies_start(self.tag + "_join_start", [both], 1, self._join_plan)
        return tok

    def finish(self, after):
        (both,) = copies_wait(self.tag + "_join_wait", self.s3, self.r3, self.b3, after, self._join_wait_plan)
        return both.reshape(-1, 128)


SMALL = (("g_mix", D_MODEL), ("g_ffn", D_MODEL), ("g_ple", D_MODEL), ("g_final", D_MODEL),
         ("conv_w", CONV_K * CONV_CH), ("rel_bias", A_HEADS * N_REL), ("w_onorm", B_DIM),
         ("a_log", B_HEADS), ("dt_bias", B_HEADS), ("loss", 1))


def _pad128(v):
    v = v.reshape(-1)
    return jnp.pad(v, (0, -v.shape[0] % 128))


def pack_small(d, names, rows):
    flat = jnp.concatenate([_pad128(d[n]) for n in names]).reshape(-1, 128)
    return jnp.pad(flat, ((0, rows - flat.shape[0]), (0, 0)))


def unpack_small(flat, names_sizes):
    out, r0 = {}, 0
    v = flat.reshape(-1)
    for n, size in names_sizes:
        out[n] = v[r0:r0 + size]
        r0 += -(-size // 128) * 128
    return out


def kernel(x, p, g_mix, w_in, conv_w, a_log, dt_bias, rel_bias, w_onorm, w_branch_a, w_branch_b, w_out, g_ffn, w_gate_up, w_down, g_ple, w_ple_gate, w_ple_proj, g_final, loss_target, m_g_mix, m_w_in, m_conv_w, m_a_log, m_dt_bias, m_rel_bias, m_w_onorm, m_w_branch_a, m_w_branch_b, m_w_out, m_g_ffn, m_w_gate_up, m_w_down, m_g_ple, m_w_ple_gate, m_w_ple_proj, m_g_final, v_g_mix, v_w_in, v_conv_w, v_a_log, v_dt_bias, v_rel_bias, v_w_onorm, v_w_branch_a, v_w_branch_b, v_w_out, v_g_ffn, v_w_gate_up, v_w_down, v_g_ple, v_w_ple_gate, v_w_ple_proj, v_g_final):
    names = ["g_mix", "w_in", "conv_w", "a_log", "dt_bias", "rel_bias", "w_onorm", "w_branch_a", "w_branch_b",
             "w_out", "g_ffn", "w_gate_up", "w_down", "g_ple", "w_ple_gate", "w_ple_proj", "g_final"]
    w = dict(zip(names, [g_mix, w_in, conv_w, a_log, dt_bias, rel_bias, w_onorm, w_branch_a, w_branch_b, w_out,
                         g_ffn, w_gate_up, w_down, g_ple, w_ple_gate, w_ple_proj, g_final]))
    m = dict(zip(names, [m_g_mix, m_w_in, m_conv_w, m_a_log, m_dt_bias, m_rel_bias, m_w_onorm, m_w_branch_a,
                         m_w_branch_b, m_w_out, m_g_ffn, m_w_gate_up, m_w_down, m_g_ple, m_w_ple_gate,
                         m_w_ple_proj, m_g_final]))
    v = dict(zip(names, [v_g_mix, v_w_in, v_conv_w, v_a_log, v_dt_bias, v_rel_bias, v_w_onorm, v_w_branch_a,
                         v_w_branch_b, v_w_out, v_g_ffn, v_w_gate_up, v_w_down, v_g_ple, v_w_ple_gate,
                         v_w_ple_proj, v_g_final]))
    xi, yi, ci = _place()
    chip = 2 * xi + yi
    big_names = [n for n, _, _ in BIG]

    shards2d = {n: w[n].reshape(w[n].shape[-2:]) for n in big_names}
    shards_bf = {n: a.astype(BF16) for n, a in shards2d.items()}
    g4 = allgather_weights(shards_bf, FIRST_WEIGHTS, chip)["w_in"]
    place = jnp.stack([chip, ci]).astype(jnp.int32)
    conv_sh = jnp.where(ci == 0, w["conv_w"].reshape(CONV_K, CONV_CH // N_CHIPS), 0.0)
    conv_slots = lax.dynamic_update_slice(jnp.zeros((N_CHIPS, CONV_K, CONV_CH // N_CHIPS), F32), conv_sh[None],
                                          (chip, 0, 0))
    conv_all = small_allreduce(conv_slots.reshape(-1, 128), "gather_conv_w")
    conv_full = jnp.transpose(conv_all.reshape(N_CHIPS, CONV_K, CONV_CH // N_CHIPS), (1, 0, 2)).reshape(CONV_K, CONV_CH)
    small = {n: w[n] for n in names if n not in big_names}
    small["conv_w"] = conv_full

    grad_x, grads, small_grads, reduced_early = local_step(
        x, p[0], loss_target, g4, small, LaterWeights(shards_bf, chip), EarlyGrads(place))

    late = EarlyGrads(place, LATE_GRADS, "late")
    late.begin(grads)
    dep = jnp.full((8, 128), late.exchange(grads["w_in"]), F32)
    gshard = unpack_shard(reduced_early, EARLY_GRADS)
    small_names = [n for n, _ in SMALL]
    red = unpack_small(small_allreduce(pack_small(small_grads, small_names, 112), "allreduce_small"), SMALL)
    loss = red["loss"][0]
    conv_g = lax.dynamic_slice(red["conv_w"].reshape(CONV_K, N_CHIPS, CONV_CH // N_CHIPS), (0, chip, 0),
                               (CONV_K, 1, CONV_CH // N_CHIPS))
    gsmall = {n: red[n].reshape(w[n].shape) for n in small_names if n not in ("loss", "conv_w")}
    gsmall["conv_w"] = conv_g.reshape(w["conv_w"].shape)

    grad, delta, new_m, new_v = {}, {}, {}, {}
    for n in list(EARLY_GRADS) + list(LATE_GRADS):
        if n in LATE_GRADS:
            late.join(v_)
            gshard.update(unpack_shard(late.finish(v_), LATE_GRADS))
        shp = w[n].shape
        d_, m_, v_ = adamw(shards2d[n], gshard[n], m[n].reshape(shp[-2:]), v[n].reshape(shp[-2:]), "adamw_" + n,
                           dep=dep if n in EARLY_GRADS else None)
        dep, v_ = lax.optimization_barrier((dep, v_))
        grad[n], delta[n], new_m[n], new_v[n] = gshard[n].reshape(shp), d_.reshape(shp), m_.reshape(shp), v_.reshape(shp)
    snames = [n for n in small_names if n != "loss"]
    ssizes = [(n, w[n].size) for n in snames]
    pk = lambda d: pack_small(d, snames, 64)
    d_, m_, v_ = adamw(pk(w), pk(gsmall), pk(m), pk(v), "adamw_small")
    ds, ms, vs = unpack_small(d_, ssizes), unpack_small(m_, ssizes), unpack_small(v_, ssizes)
    for n in snames:
        shp = w[n].shape
        grad[n], delta[n], new_m[n], new_v[n] = gsmall[n], ds[n].reshape(shp), ms[n].reshape(shp), vs[n].reshape(shp)

    return (loss, grad_x, *[grad[n] for n in names], *[delta[n] for n in names],
            *[new_m[n] for n in names], *[new_v[n] for n in names])
```

```python
import functools

import jax
import jax.numpy as jnp
from jax import lax
from jax.experimental import pallas as pl
from jax.experimental.pallas import tpu as pltpu

F32 = jnp.float32
BF16 = jnp.bfloat16
HI = lax.Precision.HIGHEST
MESH = pl.DeviceIdType.MESH

D_MODEL = 1024
CHUNK = 64
PLE_DIM = 256
EPS = 1e-6
A_HEADS = 8
A_HEAD_DIM = 64
A_WIDTH = 512
A_LOOKBACK = 8
BAND = (A_LOOKBACK + 1) * CHUNK
TAIL = 3 * CHUNK
REL_CLIP = 128
N_REL = 2 * REL_CLIP + 1
B_HEADS = 4
B_DIM = 128
B_WIDTH = 512
CONV_K = 4
CONV_CH = 1536
D_FF = 2816
SPLIT_Z = 3584
D_IN = 5640
ADAM_LR, ADAM_B1, ADAM_B2, ADAM_EPS, ADAM_WD, ADAM_STEP = 0.001, 0.9, 0.999, 1e-08, 0.01, 10

P_GATES, P_QA, P_KA, P_VA, P_CONV, P_Z, P_BD, P_WIDTH = 0, 2048, 2560, 3072, 3584, 5120, 5632, 5760

VMEM_LIMIT = 56 * 1024 * 1024


def _cp(sem, vmem=None, **kw):
    return pltpu.CompilerParams(dimension_semantics=sem, vmem_limit_bytes=vmem, **kw)


def _tile(n, cap):
    best = None
    for t in range(128, cap + 1, 128):
        if n % t == 0:
            best = t
    assert best is not None, (n, cap)
    return best


def _nn(a, b, prec=None):
    return lax.dot_general(a, b, (((1,), (0,)), ((), ())), preferred_element_type=F32, precision=prec)


def _nt(a, b, prec=None):
    return lax.dot_general(a, b, (((1,), (1,)), ((), ())), preferred_element_type=F32, precision=prec)


def _tn(a, b, prec=None):
    return lax.dot_general(a, b, (((0,), (0,)), ((), ())), preferred_element_type=F32, precision=prec)


def _bnn(a, b, prec=None):
    return lax.dot_general(a, b, (((2,), (1,)), ((0,), (0,))), preferred_element_type=F32, precision=prec)


def _bnt(a, b, prec=None):
    return lax.dot_general(a, b, (((2,), (2,)), ((0,), (0,))), preferred_element_type=F32, precision=prec)


def _bf(a):
    return a.astype(BF16)


def _split(a):
    hi = a.astype(BF16)
    return hi, (a - hi.astype(F32)).astype(BF16)


def _bnn_exact(lhs_b, rhs):
    h1 = _bf(rhs)
    r1 = rhs - h1.astype(F32)
    h2 = _bf(r1)
    h3 = _bf(r1 - h2.astype(F32))
    return _bnn(lhs_b, h1) + (_bnn(lhs_b, h2) + _bnn(lhs_b, h3))


def _bnn3(a, b):
    ah, al = a if isinstance(a, tuple) else _split(a)
    bh, bl = b if isinstance(b, tuple) else _split(b)
    return _bnn(ah, bh) + (_bnn(ah, bl) + _bnn(al, bh))


def _sigmoid(x):
    return 0.5 * jnp.tanh(0.5 * x) + 0.5


def _softplus(x):
    return jnp.maximum(x, 0.0) + jnp.log(1.0 + jnp.exp(-jnp.abs(x)))


def rms_matmul(x, g, w, name, tm=512, tn_cap=1024):
    t, d = x.shape
    n = w.shape[1]
    tm = min(tm, t)
    tn = _tile(n, tn_cap)

    nj = n // tn

    def body(x_ref, g_ref, w_ref, o_ref, h_ref, tail_ref):
        @pl.when(pl.program_id(1) == 0)
        def _():
            xv = x_ref[...]
            r = lax.rsqrt(jnp.mean(xv * xv, axis=-1, keepdims=True) + EPS)
            h_ref[...] = _bf(xv * r * g_ref[...])

        res = _nn(h_ref[...], w_ref[...])
        o_ref[...] = _bf(res)

        @pl.when(pl.program_id(1) == nj - 1)
        def _():
            tail_ref[...] = res[:, tn - 128:]

    return pl.pallas_call(
        body, name=name, grid=(t // tm, nj),
        in_specs=[pl.BlockSpec((tm, d), lambda i, j: (i, 0)),
                  pl.BlockSpec((1, d), lambda i, j: (0, 0)),
                  pl.BlockSpec((d, tn), lambda i, j: (0, j))],
        out_specs=[pl.BlockSpec((tm, tn), lambda i, j: (i, j)),
                   pl.BlockSpec((tm, d), lambda i, j: (i, 0)),
                   pl.BlockSpec((tm, 128), lambda i, j: (i, 0))],
        out_shape=[jax.ShapeDtypeStruct((t, n), BF16), jax.ShapeDtypeStruct((t, d), BF16),
                   jax.ShapeDtypeStruct((t, 128), F32)],
        compiler_params=_cp(("parallel", "arbitrary"), VMEM_LIMIT),
    )(x, g, w)


def matmul_tn(a, b, name, into=None, col0=0, width=None, tm=1024, tk_cap=1408, tn_cap=1408):
    m, k1 = a.shape
    n = b.shape[1]
    tm = min(tm, m)
    tk = _tile(k1, tk_cap)
    tn = _tile(n, tn_cap)
    while col0 % tn:
        tn = _tile(n, tn - 128)
    nk = m // tm
    c0 = col0 // tn

    def body(*refs):
        a_ref, b_ref, o_ref, acc = refs[0], refs[1], refs[-2], refs[-1]

        @pl.when(pl.program_id(2) == 0)
        def _():
            acc[...] = jnp.zeros_like(acc)

        acc[...] += _tn(_bf(a_ref[...]), _bf(b_ref[...]))

        @pl.when(pl.program_id(2) == nk - 1)
        def _():
            o_ref[...] = _bf(acc[...])

    in_specs = [pl.BlockSpec((tm, tk), lambda i, j, k: (k, i)),
                pl.BlockSpec((tm, tn), lambda i, j, k: (k, j))]
    args = [a, b]
    total = n if width is None else width
    aliases = {}
    if into is not None:
        in_specs.append(ANY)
        args.append(into)
        total = into.shape[1]
        aliases = {2: 0}
    return pl.pallas_call(
        body, name=name, grid=(k1 // tk, n // tn, nk),
        in_specs=in_specs,
        out_specs=pl.BlockSpec((tk, tn), lambda i, j, k: (i, c0 + j)),
        out_shape=jax.ShapeDtypeStruct((k1, total), BF16),
        scratch_shapes=[pltpu.VMEM((tk, tn), F32)],
        input_output_aliases=aliases,
        compiler_params=_cp(("parallel", "parallel", "arbitrary"), VMEM_LIMIT),
    )(*args)


def _tail_onehot(qi):
    r = lax.broadcasted_iota(jnp.int32, (384, TAIL), 0)
    kj = lax.broadcasted_iota(jnp.int32, (384, TAIL), 1)
    return (r == jnp.minimum(REL_CLIP + qi - kj, REL_CLIP) + REL_CLIP).astype(F32)


def bias_tail(rel_pad):
    def body(rb_ref, o_ref):
        rb = rb_ref[...]
        for qi in range(CHUNK):
            o_ref[qi] = _nn(rb, _tail_onehot(qi), HI)

    return pl.pallas_call(
        body, name="bias_tail",
        out_shape=jax.ShapeDtypeStruct((CHUNK, A_HEADS, TAIL), F32),
    )(rel_pad)


def bias_grad(db_t, db_far):
    def body(t_ref, f_ref, o_ref):
        acc = jnp.zeros((A_HEADS, 384), F32)
        for qi in range(CHUNK):
            acc = acc + _nt(t_ref[qi], _tail_onehot(qi), HI)
        far = jnp.sum(jnp.sum(f_ref[...], axis=2), axis=1, keepdims=True)
        lane = lax.broadcasted_iota(jnp.int32, (A_HEADS, 384), 1)
        o_ref[...] = acc + jnp.where(lane == 2 * REL_CLIP, far, 0.0)

    return pl.pallas_call(
        body, name="bias_grad",
        out_shape=jax.ShapeDtypeStruct((A_HEADS, 384), F32),
    )(db_t, db_far)


ATT_CB = 8


WIN = BAND + CHUNK


def _stack_heads(a, lane):
    return jnp.concatenate([jnp.where(lane < 64, a, 0.0), jnp.where(lane >= 64, a, 0.0)], axis=0)


def _fill_band_pads(k_ref, v_ref, kp, vp, s):
    z = jnp.zeros((A_LOOKBACK * CHUNK, 128), BF16)
    kp[pl.ds(0, A_LOOKBACK * CHUNK), :] = z
    vp[pl.ds(0, A_LOOKBACK * CHUNK), :] = z
    kp[pl.ds(A_LOOKBACK * CHUNK, s), :] = _bf(k_ref[...])
    vp[pl.ds(A_LOOKBACK * CHUNK, s), :] = _bf(v_ref[...])


def attn_fwd(proj, bias_band, b, s):
    t = b * s
    nc = s // CHUNK
    qb, kb_, vb_ = P_QA // 128, P_KA // 128, P_VA // 128

    nstep = nc // ATT_CB
    rows = ATT_CB * CHUNK

    def body(q_ref, k_ref, v_ref, b_ref, o_ref, lse_ref, kp, vp):
        n0 = pl.program_id(2) * ATT_CB

        @pl.when(n0 == 0)
        def _():
            _fill_band_pads(k_ref, v_ref, kp, vp, s)

        lane = lax.broadcasted_iota(jnp.int32, (2 * CHUNK, 128), 1)
        col = lax.broadcasted_iota(jnp.int32, (4 * CHUNK, WIN), 1)
        bias4 = b_ref[...]

        def pair(pp, carry):
            n = n0 + 2 * pp
            r0 = pl.multiple_of(pp * 2 * CHUNK, 2 * CHUNK)
            start = pl.multiple_of(n * CHUNK, CHUNK)
            kb = kp[pl.ds(start, WIN), :]
            vb = vp[pl.ds(start, WIN), :]
            q4 = _stack_heads(q_ref[pl.ds(r0, 2 * CHUNK), :] * (A_HEAD_DIM ** -0.5), lane)
            sc = jnp.where(col >= (A_LOOKBACK - n) * CHUNK, _nt(_bf(q4), kb) + bias4, -1e30)
            mx = jnp.max(sc, axis=1, keepdims=True)
            p = jnp.exp(sc - mx)
            l = jnp.sum(p, axis=1, keepdims=True)
            o4 = _nn(_bf(p), vb) / l
            lse4 = mx + jnp.log(l)
            o_ref[pl.ds(r0, 2 * CHUNK), :] = jnp.where(lane < 64, o4[:2 * CHUNK], o4[2 * CHUNK:])
            lse_ref[pl.ds(r0, 2 * CHUNK), :] = jnp.where(lane < 64, lse4[:2 * CHUNK], lse4[2 * CHUNK:])
            return carry

        lax.fori_loop(0, ATT_CB // 2, pair, 0, unroll=2)

    return pl.pallas_call(
        body, name="attn_fwd", grid=(b, 4, nstep),
        in_specs=[pl.BlockSpec((rows, 128), lambda bb, m, n: (bb * nstep + n, qb + m)),
                  pl.BlockSpec((s, 128), lambda bb, m, n: (bb, kb_ + m)),
                  pl.BlockSpec((s, 128), lambda bb, m, n: (bb, vb_ + m)),
                  pl.BlockSpec((None, 4 * CHUNK, WIN), lambda bb, m, n: (m, 0, 0))],
        out_specs=[pl.BlockSpec((rows, 128), lambda bb, m, n: (bb * nstep + n, m)),
                   pl.BlockSpec((rows, 128), lambda bb, m, n: (bb * nstep + n, m))],
        out_shape=[jax.ShapeDtypeStruct((t, A_WIDTH), F32), jax.ShapeDtypeStruct((t, A_WIDTH), F32)],
        scratch_shapes=[pltpu.VMEM((s + A_LOOKBACK * CHUNK, 128), BF16),
                        pltpu.VMEM((s + A_LOOKBACK * CHUNK, 128), BF16)],
        compiler_params=_cp(("parallel", "parallel", "arbitrary"), VMEM_LIMIT),
    )(proj, proj, proj, bias_band)


def attn_bwd(proj, bias_band, y_a, lse, dy_a, b, s):
    t = b * s
    nc = s // CHUNK
    qb, kb_, vb_ = P_QA // 128, P_KA // 128, P_VA // 128
    pad = A_LOOKBACK * CHUNK
    nstep = nc // ATT_CB
    rows = ATT_CB * CHUNK

    def body(q_ref, k_ref, v_ref, b_ref, do_ref, o_ref, lse_ref,
             dq_ref, dk_ref, dv_ref, dbt_ref, dbf_ref, kp, vp, dkp, dvp):
        bb = pl.program_id(1)
        n0 = pl.program_id(2) * ATT_CB

        @pl.when(n0 == 0)
        def _():
            _fill_band_pads(k_ref, v_ref, kp, vp, s)
            dkp[...] = jnp.zeros_like(dkp)
            dvp[...] = jnp.zeros_like(dvp)

        @pl.when((n0 == 0) & (bb == 0))
        def _():
            dbt_ref[...] = jnp.zeros_like(dbt_ref)
            dbf_ref[...] = jnp.zeros_like(dbf_ref)

        lane = lax.broadcasted_iota(jnp.int32, (2 * CHUNK, 128), 1)
        col = lax.broadcasted_iota(jnp.int32, (4 * CHUNK, WIN), 1)
        bias4 = b_ref[...]

        def pair(pp, carry):
            n = n0 + 2 * pp
            r0 = pl.multiple_of(pp * 2 * CHUNK, 2 * CHUNK)
            start = pl.multiple_of(n * CHUNK, CHUNK)
            kb = kp[pl.ds(start, WIN), :]
            vb = vp[pl.ds(start, WIN), :]
            q4b = _bf(_stack_heads(q_ref[pl.ds(r0, 2 * CHUNK), :] * (A_HEAD_DIM ** -0.5), lane))
            do4 = _stack_heads(do_ref[pl.ds(r0, 2 * CHUNK), :], lane)
            do4b = _bf(do4)
            o = o_ref[pl.ds(r0, 2 * CHUNK), :]
            lsev = lse_ref[pl.ds(r0, 2 * CHUNK), :]
            lse4 = jnp.concatenate([lsev[:, 0:1], lsev[:, 64:65]], axis=0)
            sc = jnp.where(col >= (A_LOOKBACK - n) * CHUNK, _nt(q4b, kb) + bias4, -1e30)
            p = jnp.exp(sc - lse4)
            dp = _nt(do4b, vb)
            delta = jnp.sum(do4 * jnp.concatenate([o, o], axis=0), axis=1, keepdims=True)
            ds = p * (dp - delta)
            dsb = _bf(ds)
            dq4 = _nn(dsb, kb)
            dq_ref[pl.ds(r0, 2 * CHUNK), :] = _bf(
                jnp.where(lane < 64, dq4[:2 * CHUNK], dq4[2 * CHUNK:]) * (A_HEAD_DIM ** -0.5))
            dkp[pl.ds(start, WIN), :] += _tn(dsb, q4b)
            dvp[pl.ds(start, WIN), :] += _tn(_bf(p), do4b)
            dbt_ref[...] += ds[:, WIN - 256:]
            dbf_ref[...] += ds[:, 0:128] + ds[:, 128:256] + ds[:, 256:384]
            return carry

        lax.fori_loop(0, ATT_CB // 2, pair, 0, unroll=2)

        @pl.when(n0 == nc - ATT_CB)
        def _():
            dk_ref[...] = _bf(dkp[pl.ds(pad, s), :])
            dv_ref[...] = _bf(dvp[pl.ds(pad, s), :])

    return pl.pallas_call(
        body, name="attn_bwd", grid=(4, b, nstep),
        in_specs=[pl.BlockSpec((rows, 128), lambda m, bb, n: (bb * nstep + n, qb + m)),
                  pl.BlockSpec((s, 128), lambda m, bb, n: (bb, kb_ + m)),
                  pl.BlockSpec((s, 128), lambda m, bb, n: (bb, vb_ + m)),
                  pl.BlockSpec((None, 4 * CHUNK, WIN), lambda m, bb, n: (m, 0, 0)),
                  pl.BlockSpec((rows, 128), lambda m, bb, n: (bb * nstep + n, m)),
                  pl.BlockSpec((rows, 128), lambda m, bb, n: (bb * nstep + n, m)),
                  pl.BlockSpec((rows, 128), lambda m, bb, n: (bb * nstep + n, m))],
        out_specs=[pl.BlockSpec((rows, 128), lambda m, bb, n: (bb * nstep + n, m)),
                   pl.BlockSpec((s, 128), lambda m, bb, n: (bb, m)),
                   pl.BlockSpec((s, 128), lambda m, bb, n: (bb, m)),
                   pl.BlockSpec((None, 4 * CHUNK, 256), lambda m, bb, n: (m, 0, 0)),
                   pl.BlockSpec((None, 4 * CHUNK, 128), lambda m, bb, n: (m, 0, 0))],
        out_shape=[jax.ShapeDtypeStruct((t, A_WIDTH), BF16)] * 3
        + [jax.ShapeDtypeStruct((4, 4 * CHUNK, 256), F32),
           jax.ShapeDtypeStruct((4, 4 * CHUNK, 128), F32)],
        scratch_shapes=[pltpu.VMEM((s + pad, 128), BF16), pltpu.VMEM((s + pad, 128), BF16),
                        pltpu.VMEM((s + pad, 128), F32), pltpu.VMEM((s + pad, 128), F32)],
        compiler_params=_cp(("parallel", "arbitrary", "arbitrary"), VMEM_LIMIT),
    )(proj, proj, proj, bias_band, dy_a, y_a, lse)


def _conv_taps(x, w, s):
    row = lax.broadcasted_iota(jnp.int32, x.shape, 0)
    shifted = [x] + [jnp.where(row >= i, pltpu.roll(x, i, 0), 0.0) for i in range(1, CONV_K)]
    acc = shifted[0] * w[CONV_K - 1:CONV_K, :]
    for i in range(1, CONV_K):
        acc = acc + shifted[i] * w[CONV_K - 1 - i:CONV_K - i, :]
    return acc, shifted


def conv_fwd(proj, conv_w8, b, s):
    cb = 512
    c0 = P_CONV // cb

    def body(x_ref, w_ref, o_ref):
        a, _ = _conv_taps(x_ref[...].astype(F32), w_ref[...], s)
        o_ref[...] = a * _sigmoid(a)

    return pl.pallas_call(
        body, name="conv_fwd", grid=(b, CONV_CH // cb),
        in_specs=[pl.BlockSpec((s, cb), lambda bb, j: (bb, c0 + j)),
                  pl.BlockSpec((8, cb), lambda bb, j: (0, j))],
        out_specs=pl.BlockSpec((s, cb), lambda bb, j: (bb, j)),
        out_shape=jax.ShapeDtypeStruct((b * s, CONV_CH), F32),
        compiler_params=_cp(("parallel", "parallel"), VMEM_LIMIT),
    )(proj, conv_w8)


def conv_bwd(proj, conv_w8, dc3, b, s):
    cb = 512
    c0 = P_CONV // cb

    def body(x_ref, w_ref, dc_ref, dx_ref, dw_ref):
        @pl.when(pl.program_id(1) == 0)
        def _():
            dw_ref[...] = jnp.zeros_like(dw_ref)

        w = w_ref[...]
        a, shifted = _conv_taps(x_ref[...].astype(F32), w, s)
        sg = _sigmoid(a)
        da = dc_ref[...] * (sg * (1.0 + a * (1.0 - sg)))
        row = lax.broadcasted_iota(jnp.int32, da.shape, 0)
        dx = da * w[CONV_K - 1:CONV_K, :]
        for i in range(1, CONV_K):
            dx = dx + jnp.where(row < s - i, pltpu.roll(da, s - i, 0), 0.0) * w[CONV_K - 1 - i:CONV_K - i, :]
        dx_ref[...] = _bf(dx)
        r8 =lax.broadcasted_iota(jnp.int32, (8, cb), 0)
        dw = jnp.zeros((8, cb), F32)
        for i in range(CONV_K):
            dw = dw + jnp.where(r8 == CONV_K - 1 - i, jnp.sum(da * shifted[i], axis=0, keepdims=True), 0.0)
        dw_ref[...] += dw

    return pl.pallas_call(
        body, name="conv_bwd", grid=(CONV_CH // cb, b),
        in_specs=[pl.BlockSpec((s, cb), lambda j, bb: (bb, c0 + j)),
                  pl.BlockSpec((8, cb), lambda j, bb: (0, j)),
                  pl.BlockSpec((None, s, cb), lambda j, bb: (j, bb, 0))],
        out_specs=[pl.BlockSpec((s, cb), lambda j, bb: (bb, j)),
                   pl.BlockSpec((8, cb), lambda j, bb: (0, j))],
        out_shape=[jax.ShapeDtypeStruct((b * s, CONV_CH), BF16), jax.ShapeDtypeStruct((8, CONV_CH), F32)],
        compiler_params=_cp(("parallel", "arbitrary"), VMEM_LIMIT),
    )(proj, conv_w8, dc3)


def _pick_lane(v, k):
    lane = lax.broadcasted_iota(jnp.int32, v.shape, 1)
    return jnp.sum(jnp.where(lane == k, v, 0.0), axis=1, keepdims=True)


def _chunk_masks(ncb):
    i = lax.broadcasted_iota(jnp.int32, (ncb, CHUNK, CHUNK), 1)
    j = lax.broadcasted_iota(jnp.int32, (ncb, CHUNK, CHUNK), 2)
    return i, j


def _col_of_row(rowvec, eye):
    return jnp.sum(jnp.where(eye, rowvec, 0.0), axis=2, keepdims=True)


def _dn_chunk_math(cq, ck, cv, bd, al_row, dtb_row, h, ncb, tm=None):
    r = ncb * CHUNK
    i, j = _chunk_masks(ncb)
    eye = i == j
    low = i >= j
    strict = i > j
    ones = jnp.ones((ncb, CHUNK, CHUNK), F32)

    braw = _pick_lane(bd, h)
    draw = _pick_lane(bd, B_HEADS + h)
    al = _pick_lane(al_row, h)
    dtb = _pick_lane(dtb_row, h)
    ea = jnp.exp(al)
    beta = _sigmoid(braw)
    sp_arg = draw + dtb
    g = -ea * _softplus(sp_arg)

    rq = lax.rsqrt(jnp.sum(cq * cq, axis=1, keepdims=True) + EPS)
    rk = lax.rsqrt(jnp.sum(ck * ck, axis=1, keepdims=True) + EPS)
    nq = cq * rq
    kn = ck * rk
    qn = nq * (B_DIM ** -0.5)

    def c3(a):
        return a.reshape(ncb, CHUNK, a.shape[-1])

    qn3, kn3, v3, beta3 = c3(qn), c3(kn), c3(cv), c3(beta)
    gb = jnp.broadcast_to(c3(g), (ncb, CHUNK, CHUNK))
    gc_b = _bnn_exact(low.astype(BF16), gb)
    gr_b = _bnn_exact(_bf(ones), jnp.where(eye, gc_b, 0.0))
    dm = jnp.where(low, jnp.exp(jnp.where(low, gc_b - gr_b, 0.0)), 0.0)
    gc = gc_b[:, :, 0:1]
    gl = gc_b[:, CHUNK - 1:CHUNK, 0:1]
    gam = jnp.exp(gc)
    egl = jnp.exp(gl)
    edec = jnp.exp(gl - gc)

    knb = _bf(kn3)
    kk = _bnt(knb, knb)
    kd = jnp.where(strict, kk * dm, 0.0)
    a = beta3 * kd
    sz = 1 if tm is None else CHUNK
    if tm is None:
        tm = eye.astype(F32)
    while sz < CHUNK:
        off = jnp.where(((i // (2 * sz)) == (j // (2 * sz))) & ((i // sz) != (j // sz)), a, 0.0)
        tmb = _bf(tm)
        tm = tm - _bnn(_bf(_bnn(tmb, _bf(off))), tmb)
        sz *= 2
    bv = beta3 * v3
    bk = (beta3 * gam) * kn3
    sol = _bnn3(_split(tm), jnp.concatenate([bv, bk], axis=2))
    u, wk = sol[:, :, :B_DIM], sol[:, :, B_DIM:]
    qk = _bnt(_bf(qn3), knb)
    p = jnp.where(low, qk * dm, 0.0)
    kdec = kn3 * edec
    qg = gam * qn3
    return dict(beta=beta3, g=c3(g), ea=ea, sp_arg=c3(sp_arg), rq=c3(rq), rk=c3(rk), nq=c3(nq),
                qn=qn3, kn=kn3, v=v3, gc=gc, gl=gl, gam=gam, egl=egl, edec=edec, dm=dm, kd=kd, a=a,
                tm=tm, u=u, wk=wk, qk=qk, p=p, kdec=kdec, qg=qg, eye=eye, low=low, strict=strict)


def dn_prep(c, proj, al_row, dtb_row, b, s, ncb=16):
    t = b * s
    r = ncb * CHUNK
    nblk = t // r
    bd_blk = 0

    def body(cq_ref, ck_ref, cv_ref, bd_ref, al_ref, dtb_ref, u_ref, wk_ref, qg_ref, kdec_ref, p_ref, egl_ref,
             tm_ref):
        h = pl.program_id(1)
        m = _dn_chunk_math(cq_ref[...], ck_ref[...], cv_ref[...], bd_ref[...].astype(F32), al_ref[...], dtb_ref[...], h, ncb)
        tm_ref[...] = m["tm"].reshape(r, CHUNK)
        u_ref[...] = m["u"].reshape(r, B_DIM)
        wk_ref[...] = m["wk"].reshape(r, B_DIM)
        qg_ref[...] = m["qg"].reshape(r, B_DIM)
        kdec_ref[...] = m["kdec"].reshape(r, B_DIM)
        p_ref[...] = m["p"].reshape(r, CHUNK)
        egl_ref[...] = jnp.broadcast_to(m["egl"], (ncb, 8, 128)).reshape(ncb * 8, 128)

    col = lambda k: pl.BlockSpec((r, 128), lambda i, h: (i, k * B_HEADS + h))
    out_col = pl.BlockSpec((r, 128), lambda i, h: (i, h))
    small = pl.BlockSpec((1, 128), lambda i, h: (0, 0))
    return pl.pallas_call(
        body, name="dn_prep", grid=(nblk, B_HEADS),
        in_specs=[col(0), col(1), col(2), pl.BlockSpec((r, 128), lambda i, h: (i, bd_blk)), small, small],
        out_specs=[out_col, out_col, out_col, out_col,
                   pl.BlockSpec((None, r, CHUNK), lambda i, h: (h, i, 0)),
                   pl.BlockSpec((None, ncb * 8, 128), lambda i, h: (h, i, 0)),
                   pl.BlockSpec((None, r, CHUNK), lambda i, h: (h, i, 0))],
        out_shape=[jax.ShapeDtypeStruct((t, B_WIDTH), F32)] * 4
        + [jax.ShapeDtypeStruct((B_HEADS, t, CHUNK), F32),
           jax.ShapeDtypeStruct((B_HEADS, t // 8, 128), F32),
           jax.ShapeDtypeStruct((B_HEADS, t, CHUNK), F32)],
        compiler_params=_cp(("parallel", "parallel"), VMEM_LIMIT),
    )(c, c, c, proj, al_row, dtb_row)


def dn_scan_fwd(u, wk, qg, kdec, p, egl, b, s):
    t = b * s
    nc = s // CHUNK

    def body(u_ref, wk_ref, qg_ref, kdec_ref, p_ref, egl_ref, o_ref, ss_ref, st):
        @pl.when(pl.program_id(0) == 0)
        def _():
            st[...] = jnp.zeros_like(st)

        chains = [(bb, h) for bb in range(b) for h in range(B_HEADS)]
        states = [st[bb * B_HEADS + h] for bb, h in chains]
        sls = [slice(h * B_DIM, (h + 1) * B_DIM) for _, h in chains]
        sbs = [_bf(sh) for sh in states]
        ws = [u_ref[bb, :, sl] - _nt(_bf(wk_ref[bb, :, sl]), sb) for (bb, _), sl, sb in zip(chains, sls, sbs)]
        qs = [_nt(_bf(qg_ref[bb, :, sl]), sb) for (bb, _), sl, sb in zip(chains, sls, sbs)]
        wbs = [_bf(w) for w in ws]
        outs = [q + _nn(_bf(p_ref[h, bb]), wb) for (bb, h), q, wb in zip(chains, qs, wbs)]
        new_states = [egl_ref[h, bb][0:1, :] * sh + _tn(wb, _bf(kdec_ref[bb, :, sl]))
                      for (bb, h), sl, sh, wb in zip(chains, sls, states, wbs)]
        for (bb, h), sh, o, ns in zip(chains, states, outs, new_states):
            ss_ref[bb, h] = sh
            o_ref[bb, :, h * B_DIM:(h + 1) * B_DIM] = o
            st[bb * B_HEADS + h] = ns

    r3 = lambda a: a.reshape(b, s, B_WIDTH)
    act = pl.BlockSpec((b, CHUNK, B_WIDTH), lambda n: (0, n, 0))
    o, states = pl.pallas_call(
        body, name="dn_scan_fwd", grid=(nc,),
        in_specs=[act, act, act, act,
                  pl.BlockSpec((B_HEADS, b, CHUNK, CHUNK), lambda n: (0, 0, n, 0)),
                  pl.BlockSpec((B_HEADS, b, 8, 128), lambda n: (0, 0, n, 0))],
        out_specs=[act, pl.BlockSpec((b, None, B_HEADS, B_DIM, B_DIM), lambda n: (0, n, 0, 0, 0))],
        out_shape=[jax.ShapeDtypeStruct((b, s, B_WIDTH), F32),
                   jax.ShapeDtypeStruct((b, nc, B_HEADS, B_DIM, B_DIM), F32)],
        scratch_shapes=[pltpu.VMEM((b * B_HEADS, B_DIM, B_DIM), F32)],
        compiler_params=_cp(("arbitrary",), VMEM_LIMIT),
    )(r3(u), r3(wk), r3(qg), r3(kdec), p.reshape(B_HEADS, b, s, CHUNK), egl.reshape(B_HEADS, b, s // 8, 128))
    return o.reshape(t, B_WIDTH), states


def dn_scan_bwd(u, wk, qg, kdec, p, egl, states, do, b, s):
    t = b * s
    nc = s // CHUNK

    def body(u_ref, wk_ref, qg_ref, kdec_ref, p_ref, egl_ref, ss_ref, do_ref,
             dw_ref, dwk_ref, dqg_ref, dkdec_ref, dp_ref, degl_ref, dst):
        @pl.when(pl.program_id(0) == 0)
        def _():
            dst[...] = jnp.zeros_like(dst)

        chains = [(bb, h) for bb in range(b) for h in range(B_HEADS)]
        dstates = [dst[bb * B_HEADS + h] for bb, h in chains]
        n8 = range(len(chains))
        sls = [slice(h * B_DIM, (h + 1) * B_DIM) for _, h in chains]
        shs = [ss_ref[bb, h] for bb, h in chains]
        sbs = [_bf(sh) for sh in shs]
        dsbs = [_bf(dsp) for dsp in dstates]
        wkbs = [_bf(wk_ref[bb, :, sl]) for (bb, _), sl in zip(chains, sls)]
        dobs = [_bf(do_ref[bb, :, sl]) for (bb, _), sl in zip(chains, sls)]
        t1 = [_nt(wkbs[i], sbs[i]) for i in n8]
        dwa = [_tn(_bf(p_ref[h, bb]), dobs[i]) for i, (bb, h) in enumerate(chains)]
        dwb_ = [_nt(_bf(kdec_ref[bb, :, sls[i]]), dsbs[i]) for i, (bb, _) in enumerate(chains)]
        dqgs = [_nn(dobs[i], sbs[i]) for i in n8]
        dsq = [_tn(dobs[i], _bf(qg_ref[bb, :, sls[i]])) for i, (bb, _) in enumerate(chains)]
        wbs = [_bf(u_ref[bb, :, sls[i]] - t1[i]) for i, (bb, _) in enumerate(chains)]
        dws = [dwa[i] + dwb_[i] for i in n8]
        dwbs = [_bf(dw) for dw in dws]
        dwks = [-_nn(dwbs[i], sbs[i]) for i in n8]
        dkdecs = [_nn(wbs[i], dsbs[i]) for i in n8]
        dpms = [_nt(dobs[i], wbs[i]) for i in n8]
        dsw = [_tn(dwbs[i], wkbs[i]) for i in n8]
        tots = [jnp.sum(jnp.sum(shs[i] * dstates[i], axis=1, keepdims=True), axis=0, keepdims=True) for i in n8]
        new_dss = [egl_ref[h, bb][0:1, :] * dstates[i] + dsq[i] - dsw[i] for i, (bb, h) in enumerate(chains)]
        results = [(dws[i], dqgs[i], dwks[i], dkdecs[i], dpms[i], tots[i], new_dss[i]) for i in n8]
        for (bb, h), (dw, dqg, dwk, dkdec, dpm, tot, new_ds) in zip(chains, results):
            sl = slice(h * B_DIM, (h + 1) * B_DIM)
            dw_ref[bb, :, sl] = dw
            dqg_ref[bb, :, sl] = dqg
            dwk_ref[bb, :, sl] = dwk
            dkdec_ref[bb, :, sl] = dkdec
            dp_ref[h, bb] = dpm
            degl_ref[h, bb] = jnp.broadcast_to(tot, (8, 128))
            dst[bb * B_HEADS + h] = new_ds

    r3 = lambda a: a.reshape(b, s, B_WIDTH)
    act = pl.BlockSpec((b, CHUNK, B_WIDTH), lambda n: (0, nc - 1 - n, 0))
    pspec = pl.BlockSpec((B_HEADS, b, CHUNK, CHUNK), lambda n: (0, 0, nc - 1 - n, 0))
    espec = pl.BlockSpec((B_HEADS, b, 8, 128), lambda n: (0, 0, nc - 1 - n, 0))
    outs = pl.pallas_call(
        body, name="dn_scan_bwd", grid=(nc,),
        in_specs=[act, act, act, act, pspec, espec,
                  pl.BlockSpec((b, None, B_HEADS, B_DIM, B_DIM), lambda n: (0, nc - 1 - n, 0, 0, 0)),
                  act],
        out_specs=[act, act, act, act, pspec, espec],
        out_shape=[jax.ShapeDtypeStruct((b, s, B_WIDTH), F32)] * 4
        + [jax.ShapeDtypeStruct((B_HEADS, b, s, CHUNK), F32),
           jax.ShapeDtypeStruct((B_HEADS, b, s // 8, 128), F32)],
        scratch_shapes=[pltpu.VMEM((b * B_HEADS, B_DIM, B_DIM), F32)],
        compiler_params=_cp(("arbitrary",), VMEM_LIMIT),
    )(r3(u), r3(wk), r3(qg), r3(kdec), p.reshape(B_HEADS, b, s, CHUNK), egl.reshape(B_HEADS, b, s // 8, 128),
      states, r3(do))
    return (*[a.reshape(t, B_WIDTH) for a in outs[:4]], outs[4].reshape(B_HEADS, t, CHUNK),
            outs[5].reshape(B_HEADS, t // 8, 128))


def dn_post_bwd(c, proj, al_row, dtb_row, tmat, dw, dwk, dqg, dkdec, dp, degl, b, s, ncb=16):
    t = b * s
    r = ncb * CHUNK
    nblk = t // r
    bd_blk = 0

    def body(cq_ref, ck_ref, cv_ref, bd_ref, al_ref, dtb_ref, tm_ref, dw_ref, dwk_ref, dqg_ref, dkdec_ref, dp_ref,
             degl_ref, dc_ref, dbd_ref, dal_ref, ddtb_ref):
        h = pl.program_id(1)

        @pl.when((pl.program_id(0) == 0) & (h == 0))
        def _():
            dal_ref[...] = jnp.zeros_like(dal_ref)
            ddtb_ref[...] = jnp.zeros_like(ddtb_ref)

        m = _dn_chunk_math(cq_ref[...], ck_ref[...], cv_ref[...], bd_ref[...].astype(F32), al_ref[...], dtb_ref[...], h, ncb,
                           tm=tm_ref[...].reshape(ncb, CHUNK, CHUNK))
        eye, low, strict = m["eye"], m["low"], m["strict"]
        eyef = eye.astype(F32)

        def c3(a):
            return a.reshape(ncb, CHUNK, a.shape[-1])

        du, dwkv, dqg, dkdec = c3(dw_ref[...]), c3(dwk_ref[...]), c3(dqg_ref[...]), c3(dkdec_ref[...])
        dpm = jnp.where(low, c3(dp_ref[...]), 0.0)
        degl = degl_ref[...].reshape(ncb, 8, 128)[:, 0:1, 0:1]
        beta, gam, kn, qn, v = m["beta"], m["gam"], m["kn"], m["qn"], m["v"]
        dm, kd, a, p = m["dm"], m["kd"], m["a"], m["p"]
        knb, qnb = _bf(kn), _bf(qn)

        eyeb = _bf(eyef)
        th, tl = _split(m["tm"])
        tts = (_bf(_bnt(eyeb, th)), _bf(_bnt(eyeb, tl)))
        xy = _bnn3(tts, jnp.concatenate([du, dwkv], axis=2))
        x, y = xy[:, :, :B_DIM], xy[:, :, B_DIM:]
        da = -jnp.where(strict, _bnt(_bf(x), _bf(m["u"])) + _bnt(_bf(y), _bf(m["wk"])), 0.0)
        dv = beta * x
        sy = jnp.sum(y * kn, axis=2, keepdims=True)
        dbeta = jnp.sum(x * v, axis=2, keepdims=True) + gam * sy + jnp.sum(da * kd, axis=2, keepdims=True)
        dgam = beta * sy + jnp.sum(dqg * qn, axis=2, keepdims=True)
        dkk = da * beta * dm
        dqk = dpm * dm
        dkkb, dqkb = _bf(dkk), _bf(dqk)
        dkn = ((beta * gam) * y + _bnn(dkkb, knb) + _bnn(_bf(_bnt(eyeb, dkkb)), knb)
               + _bnn(_bf(_bnt(eyeb, dqkb)), qnb) + dkdec * m["edec"])
        dqn = gam * dqg + _bnn(dqkb, knb)
        mm = da * a + dpm * p
        ek = jnp.sum(dkdec * m["kdec"], axis=2, keepdims=True)
        dgc = (jnp.sum(mm, axis=2, keepdims=True) - _col_of_row(jnp.sum(mm, axis=1, keepdims=True), eye)
               + dgam * gam - ek)
        dgl = jnp.sum(ek, axis=1, keepdims=True) + degl * m["egl"]
        i, _ = _chunk_masks(ncb)
        dgc = dgc + jnp.where(i[:, :, 0:1] == CHUNK - 1, dgl, 0.0)
        upper = (i <= _chunk_masks(ncb)[1]).astype(BF16)
        dg = _bnn_exact(upper, jnp.broadcast_to(dgc, (ncb, CHUNK, CHUNK)))[:, :, 0:1]

        nq = m["nq"]
        dnq = dqn * (B_DIM ** -0.5)
        dcq = m["rq"] * (dnq - nq * jnp.sum(nq * dnq, axis=2, keepdims=True))
        dck = m["rk"] * (dkn - kn * jnp.sum(kn * dkn, axis=2, keepdims=True))
        dc_ref[0] = dcq.reshape(r, B_DIM)
        dc_ref[1] = dck.reshape(r, B_DIM)
        dc_ref[2] = dv.reshape(r, B_DIM)

        dbraw = (dbeta * beta * (1.0 - beta)).reshape(r, 1)
        sgm = _sigmoid(m["sp_arg"])
        ddraw3 = dg * (-m["ea"]) * sgm
        ddraw = ddraw3.reshape(r, 1)
        lane = lax.broadcasted_iota(jnp.int32, (r, 128), 1)
        contrib = jnp.where(lane == h, dbraw, 0.0) + jnp.where(lane == B_HEADS + h, ddraw, 0.0)

        @pl.when(h == 0)
        def _():
            dbd_ref[...] = contrib

        @pl.when(h != 0)
        def _():
            dbd_ref[...] += contrib

        lane8 = lax.broadcasted_iota(jnp.int32, (8, 128), 1)
        tot_al = jnp.sum(jnp.sum(dg * m["g"], axis=1, keepdims=True), axis=0, keepdims=True).reshape(1, 1)
        tot_dtb = jnp.sum(jnp.sum(ddraw3, axis=1, keepdims=True), axis=0, keepdims=True).reshape(1, 1)
        dal_ref[...] += jnp.where(lane8 == h, tot_al, 0.0)
        ddtb_ref[...] += jnp.where(lane8 == h, tot_dtb, 0.0)

    col = lambda k: pl.BlockSpec((r, 128), lambda i, h: (i, k * B_HEADS + h))
    hcol = pl.BlockSpec((r, 128), lambda i, h: (i, h))
    small = pl.BlockSpec((1, 128), lambda i, h: (0, 0))
    acc = pl.BlockSpec((8, 128), lambda i, h: (0, 0))
    return pl.pallas_call(
        body, name="dn_post_bwd", grid=(nblk, B_HEADS),
        in_specs=[col(0), col(1), col(2), pl.BlockSpec((r, 128), lambda i, h: (i, bd_blk)), small, small,
                  pl.BlockSpec((None, r, CHUNK), lambda i, h: (h, i, 0)),
                  hcol, hcol, hcol, hcol,
                  pl.BlockSpec((None, r, CHUNK), lambda i, h: (h, i, 0)),
                  pl.BlockSpec((None, ncb * 8, 128), lambda i, h: (h, i, 0))],
        out_specs=[pl.BlockSpec((3, r, 128), lambda i, h: (0, i, h)),
                   pl.BlockSpec((r, 128), lambda i, h: (i, 0)), acc, acc],
        out_shape=[jax.ShapeDtypeStruct((3, t, B_WIDTH), F32), jax.ShapeDtypeStruct((t, 128), F32),
                   jax.ShapeDtypeStruct((8, 128), F32), jax.ShapeDtypeStruct((8, 128), F32)],
        compiler_params=_cp(("arbitrary", "arbitrary"), VMEM_LIMIT),
    )(c, c, c, proj, al_row, dtb_row, tmat, dw, dwk, dqg, dkdec, dp, degl)


def make_bias_band(rel_bias):
    tail = bias_tail(jnp.pad(rel_bias, ((0, 0), (0, 384 - N_REL))))
    far = jnp.broadcast_to(rel_bias[:, 2 * REL_CLIP][:, None, None], (A_HEADS, CHUNK, BAND - TAIL))
    band = jnp.concatenate([far, jnp.transpose(tail, (1, 0, 2))], axis=2)
    off = jnp.full((A_HEADS, CHUNK, CHUNK), -1e30, F32)
    both = jnp.stack([jnp.concatenate([band, off], axis=2), jnp.concatenate([off, band], axis=2)], axis=1)
    return both.reshape(4, 4 * CHUNK, WIN)


def bias_band_grad(dbt, dbf):
    t5 = dbt.reshape(A_HEADS, 2, CHUNK, 256)
    tail = t5[:, 0, :, :TAIL] + t5[:, 1, :, CHUNK:]
    far = dbf.reshape(A_HEADS, 2, CHUNK, 128).sum(axis=1) + jnp.pad(t5[:, 1, :, :CHUNK], ((0, 0), (0, 0), (0, CHUNK)))
    return bias_grad(jnp.transpose(tail, (1, 0, 2)), far)[:, :N_REL]


def _rms(x):
    r = lax.rsqrt(jnp.mean(x * x, axis=-1, keepdims=True) + EPS)
    return r, x * r


def _rms_bwd(dh, g, r, n):
    dn = dh * g
    return r * (dn - n * jnp.mean(dn * n, axis=-1, keepdims=True)), dh * n


def _gated_onorm(o, z, w_on):
    parts = []
    for h in range(B_HEADS):
        sl = slice(h * B_DIM, (h + 1) * B_DIM)
        r, n = _rms(o[:, sl])
        parts.append((r, n))
    r4 = [p[0] for p in parts]
    n4 = jnp.concatenate([p[1] for p in parts], axis=1)
    w4 = jnp.concatenate([w_on] * B_HEADS, axis=1)
    sz = _sigmoid(z)
    silu = z * sz
    return n4 * w4 * silu, r4, n4, w4, sz, silu


def mid_fwd(x, y_a, o_b, proj, w_on, wa, wb, w_out, tm=256):
    t = x.shape[0]
    tm = min(tm, t)

    def body(x_ref, ya_ref, ob_ref, z_ref, ga_ref, gb_ref, won_ref, wa_ref, wb_ref, wo_ref, x1_ref, mg_ref):
        yb = _gated_onorm(ob_ref[...], z_ref[...].astype(F32), won_ref[...])[0]
        ua = _nn(_bf(ya_ref[...]), wa_ref[...])
        ub = _nn(_bf(yb), wb_ref[...])
        merged = _sigmoid(ga_ref[...].astype(F32)) * ua + _sigmoid(gb_ref[...].astype(F32)) * ub
        mb = _bf(merged)
        mg_ref[...] = mb
        x1_ref[...] = x_ref[...] + _nn(mb, wo_ref[...])

    rowd = pl.BlockSpec((tm, D_MODEL), lambda i: (i, 0))
    row5 = pl.BlockSpec((tm, 512), lambda i: (i, 0))
    full = lambda a: pl.BlockSpec(a.shape, lambda i: (0,) * a.ndim)
    return pl.pallas_call(
        body, name="mid_fwd", grid=(t // tm,),
        in_specs=[rowd, row5, row5,
                  pl.BlockSpec((tm, 512), lambda i: (i, P_Z // 512)),
                  pl.BlockSpec((tm, D_MODEL), lambda i: (i, 0)),
                  pl.BlockSpec((tm, D_MODEL), lambda i: (i, 1)),
                  full(w_on), full(wa), full(wb), full(w_out)],
        out_specs=[rowd, rowd],
        out_shape=[jax.ShapeDtypeStruct((t, D_MODEL), F32), jax.ShapeDtypeStruct((t, D_MODEL), BF16)],
        compiler_params=_cp(("parallel",), VMEM_LIMIT),
    )(x, y_a, o_b, proj, proj, proj, w_on, wa, wb, w_out)


def mid_bwd(dx1, merged, y_a, o_b, proj, w_on, wa, wb, w_out, tm=256):
    t = dx1.shape[0]
    tm = min(tm, t)

    def body(dx1_ref, mg_ref, ya_ref, ob_ref, z_ref, ga_ref, gb_ref, won_ref, wa_ref, wb_ref, wo_ref,
             dya_ref, dob_ref, dz_ref, dg_ref, dwo_ref, dwa_ref, dwb_ref, dwon_ref):
        @pl.when(pl.program_id(0) == 0)
        def _():
            dwo_ref[...] = jnp.zeros_like(dwo_ref)
            dwa_ref[...] = jnp.zeros_like(dwa_ref)
            dwb_ref[...] = jnp.zeros_like(dwb_ref)
            dwon_ref[...] = jnp.zeros_like(dwon_ref)

        dx1b = _bf(dx1_ref[...])
        dmerged = _nt(dx1b, wo_ref[...])
        dwo_ref[...] += _tn(mg_ref[...], dx1b)
        o = ob_ref[...]
        z = z_ref[...].astype(F32)
        yb, r4, n4, w4, sz, silu = _gated_onorm(o, z, won_ref[...])
        yab, ybb = _bf(ya_ref[...]), _bf(yb)
        ua = _nn(yab, wa_ref[...])
        ub = _nn(ybb, wb_ref[...])
        sa, sb = _sigmoid(ga_ref[...].astype(F32)), _sigmoid(gb_ref[...].astype(F32))
        dua, dub = _bf(dmerged * sa), _bf(dmerged * sb)
        dg_ref[:, 0:D_MODEL] = _bf(dmerged * ua * sa * (1.0 - sa))
        dg_ref[:, D_MODEL:2 * D_MODEL] = _bf(dmerged * ub * sb * (1.0 - sb))
        dwa_ref[...] += _tn(yab, dua)
        dwb_ref[...] += _tn(ybb, dub)
        dya_ref[...] = _nt(dua, wa_ref[...])
        dyb = _nt(dub, wb_ref[...])
        dz_ref[...] = _bf(dyb * (n4 * w4) * (sz * (1.0 + z * (1.0 - sz))))
        dnw = dyb * silu
        dwon = jnp.zeros((1, B_DIM), F32)
        for h in range(B_HEADS):
            sl = slice(h * B_DIM, (h + 1) * B_DIM)
            dxh, dgh = _rms_bwd(dnw[:, sl], won_ref[...], r4[h], n4[:, sl])
            dob_ref[:, sl] = dxh
            dwon = dwon + jnp.sum(dgh, axis=0, keepdims=True)
        dwon_ref[...] += jnp.broadcast_to(dwon, (8, B_DIM))

    rowd = pl.BlockSpec((tm, D_MODEL), lambda i: (i, 0))
    row5 = pl.BlockSpec((tm, 512), lambda i: (i, 0))
    full = lambda a: pl.BlockSpec(a.shape, lambda i: (0,) * a.ndim)
    fixed = lambda shp: pl.BlockSpec(shp, lambda i: (0,) * len(shp))
    return pl.pallas_call(
        body, name="mid_bwd", grid=(t // tm,),
        in_specs=[rowd, rowd, row5, row5,
                  pl.BlockSpec((tm, 512), lambda i: (i, P_Z // 512)),
                  pl.BlockSpec((tm, D_MODEL), lambda i: (i, 0)),
                  pl.BlockSpec((tm, D_MODEL), lambda i: (i, 1)),
                  full(w_on), full(wa), full(wb), full(w_out)],
        out_specs=[row5, row5, row5, pl.BlockSpec((tm, 2 * D_MODEL), lambda i: (i, 0)),
                   fixed((D_MODEL, D_MODEL)), fixed((A_WIDTH, D_MODEL)), fixed((B_WIDTH, D_MODEL)),
                   fixed((8, B_DIM))],
        out_shape=[jax.ShapeDtypeStruct((t, 512), F32), jax.ShapeDtypeStruct((t, 512), F32),
                   jax.ShapeDtypeStruct((t, 512), BF16), jax.ShapeDtypeStruct((t, 2 * D_MODEL), BF16),
           jax.ShapeDtypeStruct((D_MODEL, D_MODEL), F32), jax.ShapeDtypeStruct((A_WIDTH, D_MODEL), F32),
           jax.ShapeDtypeStruct((B_WIDTH, D_MODEL), F32), jax.ShapeDtypeStruct((8, B_DIM), F32)],
        compiler_params=_cp(("arbitrary",), VMEM_LIMIT),
    )(dx1, merged, y_a, o_b, proj, proj, proj, w_on, wa, wb, w_out)


FFN_TF = 1408


def ffn_up(x1, g, w_gu, tm=512, tf=FFN_TF):
    t = x1.shape[0]
    tm = min(tm, t)
    nf = D_FF // tf

    def body(x_ref, g_ref, wg_ref, wu_ref, gate_ref, up_ref, act_ref, h_ref):
        @pl.when(pl.program_id(1) == 0)
        def _():
            r, n = _rms(x_ref[...])
            h_ref[...] = _bf(n * g_ref[...])

        hb = h_ref[...]
        gate = _nn(hb, wg_ref[...])
        up = _nn(hb, wu_ref[...])
        gate_ref[...] = _bf(gate)
        up_ref[...] = _bf(up)
        act_ref[...] = _bf(gate * _sigmoid(gate) * up)

    ff = pl.BlockSpec((tm, tf), lambda i, j: (i, j))
    return pl.pallas_call(
        body, name="ffn_up", grid=(t // tm, nf),
        in_specs=[pl.BlockSpec((tm, D_MODEL), lambda i, j: (i, 0)),
                  pl.BlockSpec((1, D_MODEL), lambda i, j: (0, 0)),
                  pl.BlockSpec((D_MODEL, tf), lambda i, j: (0, j)),
                  pl.BlockSpec((D_MODEL, tf), lambda i, j: (0, nf + j))],
        out_specs=[ff, ff, ff, pl.BlockSpec((tm, D_MODEL), lambda i, j: (i, 0))],
        out_shape=[jax.ShapeDtypeStruct((t, D_FF), BF16)] * 3 + [jax.ShapeDtypeStruct((t, D_MODEL), BF16)],
        compiler_params=_cp(("parallel", "arbitrary"), VMEM_LIMIT),
    )(x1, g, w_gu, w_gu)


def matmul_residual(a, w, res, name, tm=512, tk=FFN_TF):
    t, k = a.shape
    n = w.shape[1]
    tm = min(tm, t)

    def body(a_ref, w_ref, r_ref, o_ref):
        @pl.when(pl.program_id(1) == 0)
        def _():
            o_ref[...] = r_ref[...]

        o_ref[...] += _nn(a_ref[...], w_ref[...])

    return pl.pallas_call(
        body, name=name, grid=(t // tm, k // tk),
        in_specs=[pl.BlockSpec((tm, tk), lambda i, j: (i, j)),
                  pl.BlockSpec((tk, n), lambda i, j: (j, 0)),
                  pl.BlockSpec((tm, n), lambda i, j: (i, 0))],
        out_specs=pl.BlockSpec((tm, n), lambda i, j: (i, 0)),
        out_shape=jax.ShapeDtypeStruct((t, n), F32),
        compiler_params=_cp(("parallel", "arbitrary"), VMEM_LIMIT),
    )(a, w, res)


def ffn_act_bwd(dx2, gate, up, w_down, tm=512, tf=FFN_TF):
    t = dx2.shape[0]
    tm = min(tm, t)

    def body(dx2_ref, gate_ref, up_ref, wd_ref, dgate_ref, dup_ref, dx2b_ref):
        @pl.when(pl.program_id(1) == 0)
        def _():
            dx2b_ref[...] = _bf(dx2_ref[...])

        dact = _nt(dx2b_ref[...], wd_ref[...])
        gt, upv = gate_ref[...].astype(F32), up_ref[...].astype(F32)
        sg = _sigmoid(gt)
        dgate_ref[...] = _bf(dact * upv * (sg * (1.0 + gt * (1.0 - sg))))
        dup_ref[...] = _bf(dact * (gt * sg))

    ff = pl.BlockSpec((tm, tf), lambda i, j: (i, j))
    return pl.pallas_call(
        body, name="ffn_act_bwd", grid=(t // tm, D_FF // tf),
        in_specs=[pl.BlockSpec((tm, D_MODEL), lambda i, j: (i, 0)), ff, ff,
                  pl.BlockSpec((tf, D_MODEL), lambda i, j: (j, 0))],
        out_specs=[ff, ff],
        out_shape=[jax.ShapeDtypeStruct((t, D_FF), BF16)] * 2,
        scratch_shapes=[pltpu.VMEM((tm, D_MODEL), BF16)],
        compiler_params=_cp(("parallel", "arbitrary"), VMEM_LIMIT),
    )(dx2, gate, up, w_down)


def tail_fwd_bwd(x2, p, target, g_ple, g_final, w_pg, w_pp, tm=256):
    t = x2.shape[0]
    tm = min(tm, t)

    def body(x_ref, p_ref, t_ref, gp_ref, gf_ref, wpg_ref, wpp_ref,
             dx_ref, dwpg_ref, dwpp_ref, dgp_ref, dgf_ref, loss_ref):
        @pl.when(pl.program_id(0) == 0)
        def _():
            dwpg_ref[...] = jnp.zeros_like(dwpg_ref)
            dwpp_ref[...] = jnp.zeros_like(dwpp_ref)
            dgp_ref[...] = jnp.zeros_like(dgp_ref)
            dgf_ref[...] = jnp.zeros_like(dgf_ref)
            loss_ref[...] = jnp.zeros_like(loss_ref)

        x2v = x_ref[...]
        gp, gf = gp_ref[...], gf_ref[...]
        r3, n3 = _rms(x2v)
        h3b = _bf(n3 * gp)
        pb = _bf(p_ref[...])
        pg = _sigmoid(_nn(h3b, wpg_ref[...]))
        pp = _nn(pb, wpp_ref[...])
        x3 = x2v + pg * pp
        r4, n4 = _rms(x3)
        err = n4 * gf - t_ref[...]
        part = 0.5 * jnp.sum(jnp.sum(err * err, axis=1, keepdims=True), axis=0, keepdims=True) / D_MODEL
        loss_ref[...] += jnp.broadcast_to(part, (8, 128))
        dy = err * (1.0 / D_MODEL)
        dx3, dgf = _rms_bwd(dy, gf, r4, n4)
        dgf_ref[...] += jnp.broadcast_to(jnp.sum(dgf, axis=0, keepdims=True), (8, D_MODEL))
        dzp = _bf(dx3 * pp * pg * (1.0 - pg))
        dpp = _bf(dx3 * pg)
        dwpg_ref[...] += _tn(h3b, dzp)
        dwpp_ref[...] += _tn(pb, dpp)
        dh3 = _nt(dzp, wpg_ref[...])
        dx, dgp = _rms_bwd(dh3, gp, r3, n3)
        dgp_ref[...] += jnp.broadcast_to(jnp.sum(dgp, axis=0, keepdims=True), (8, D_MODEL))
        dx_ref[...] = dx3 + dx

    rowd = pl.BlockSpec((tm, D_MODEL), lambda i: (i, 0))
    fixed = lambda shp: pl.BlockSpec(shp, lambda i: (0,) * len(shp))
    return pl.pallas_call(
        body, name="tail_fwd_bwd", grid=(t // tm,),
        in_specs=[rowd, pl.BlockSpec((tm, PLE_DIM), lambda i: (i, 0)), rowd,
                  fixed((1, D_MODEL)), fixed((1, D_MODEL)), fixed((D_MODEL, D_MODEL)), fixed((PLE_DIM, D_MODEL))],
        out_specs=[rowd, fixed((D_MODEL, D_MODEL)), fixed((PLE_DIM, D_MODEL)),
                   fixed((8, D_MODEL)), fixed((8, D_MODEL)), fixed((8, 128))],
        out_shape=[jax.ShapeDtypeStruct((t, D_MODEL), F32), jax.ShapeDtypeStruct((D_MODEL, D_MODEL), F32),
                   jax.ShapeDtypeStruct((PLE_DIM, D_MODEL), F32), jax.ShapeDtypeStruct((8, D_MODEL), F32),
                   jax.ShapeDtypeStruct((8, D_MODEL), F32), jax.ShapeDtypeStruct((8, 128), F32)],
        compiler_params=_cp(("arbitrary",), VMEM_LIMIT),
    )(x2, p, target, g_ple, g_final, w_pg, w_pp)


def in_proj_bwd(pieces, weights, x, dx1, g, name="in_proj_bwd", tm=256):
    t = x.shape[0]
    tm = min(tm, t)
    k = len(pieces)
    assert all(c0 % wd == 0 and w0 % wd == 0 for (_, c0, wd), (_, w0) in zip(pieces, weights))

    def body(*refs):
        p_refs, w_refs = refs[:k], refs[k:2 * k]
        x_ref, dx1_ref, g_ref, dx_ref, dg_ref = refs[2 * k:]

        @pl.when(pl.program_id(0) == 0)
        def _():
            dg_ref[...] = jnp.zeros_like(dg_ref)

        dh = _nt(_bf(p_refs[0][...]), w_refs[0][...])
        for pr, wr in zip(p_refs[1:], w_refs[1:]):
            dh = dh + _nt(_bf(pr[...]), wr[...])
        r, n = _rms(x_ref[...])
        dx, dgc = _rms_bwd(dh, g_ref[...], r, n)
        dx_ref[...] = dx1_ref[...] + dx
        dg_ref[...] += jnp.broadcast_to(jnp.sum(dgc, axis=0, keepdims=True), (8, D_MODEL))

    rowd = pl.BlockSpec((tm, D_MODEL), lambda i: (i, 0))
    return pl.pallas_call(
        body, name=name, grid=(t // tm,),
        in_specs=[pl.BlockSpec((tm, wd), functools.partial(lambda i, cb: (i, cb), cb=c0 // wd))
                  for _, c0, wd in pieces]
        + [pl.BlockSpec((w.shape[0], wd), functools.partial(lambda i, cb: (0, cb), cb=w0 // wd))
           for (w, w0), (_, _, wd) in zip(weights, pieces)]
        + [rowd, rowd, pl.BlockSpec((1, D_MODEL), lambda i: (0, 0))],
        out_specs=[rowd, pl.BlockSpec((8, D_MODEL), lambda i: (0, 0))],
        out_shape=[jax.ShapeDtypeStruct((t, D_MODEL), F32), jax.ShapeDtypeStruct((8, D_MODEL), F32)],
        compiler_params=_cp(("arbitrary",), VMEM_LIMIT),
    )(*[a for a, _, _ in pieces], *[w for w, _ in weights], x, dx1, g)


def adamw(w, g, m, v, name, rows_cap=256, dep=None):
    lead = w.shape[:-2]
    r, c = w.shape[-2:]
    tr = r
    for cand in range(8, min(r, rows_cap) + 1, 8):
        if r % cand == 0:
            tr = cand

    def body(w_ref, g_ref, m_ref, v_ref, *rest):
        d_ref, mo_ref, vo_ref = rest[-3:]
        gv = g_ref[...]
        mn = ADAM_B1 * m_ref[...] + (1.0 - ADAM_B1) * gv
        vn = ADAM_B2 * v_ref[...] + (1.0 - ADAM_B2) * (gv * gv)
        m_hat = mn / (1.0 - ADAM_B1 ** ADAM_STEP)
        v_hat = vn / (1.0 - ADAM_B2 ** ADAM_STEP)
        d_ref[...] = -ADAM_LR * (m_hat / (jnp.sqrt(v_hat) + ADAM_EPS) + ADAM_WD * w_ref[...])
        mo_ref[...] = mn
        vo_ref[...] = vn

    spec = pl.BlockSpec((None,) * len(lead) + (tr, c), lambda i: (0,) * len(lead) + (i, 0))
    extra = [] if dep is None else [dep]
    return pl.pallas_call(
        body, name=name, grid=(r // tr,),
        in_specs=[spec] * 4 + [pl.BlockSpec((8, 128), lambda i: (0, 0))] * len(extra), out_specs=[spec] * 3,
        out_shape=[jax.ShapeDtypeStruct(w.shape, F32)] * 3,
        compiler_params=_cp(("parallel",), VMEM_LIMIT),
    )(w, g.reshape(w.shape), m, v, *extra)


class Standalone:
    def __init__(self, later_weights):
        self.later_weights = later_weights

    def begin(self, *a):
        return 0.0

    forward = exchange = join = begin

    def finish(self, after):
        return self.later_weights


def local_step(x3d, p3d, target3d, g4, small, later, early):
    b, s, _ = x3d.shape
    t = b * s
    x = x3d.reshape(t, D_MODEL)
    p = p3d.reshape(t, PLE_DIM)
    target = target3d.reshape(t, D_MODEL)
    cut = SPLIT_Z - 2 * (D_IN // N_CHIPS)
    w_inp = jnp.concatenate([g4[2][:, cut + 8:], g4[3], g4[0], g4[1], g4[2][:, :cut], g4[2][:, cut:cut + 8],
                             jnp.zeros((D_MODEL, 120), BF16)], axis=1)
    al_row = jnp.pad(small["a_log"].reshape(1, B_HEADS), ((0, 0), (0, 128 - B_HEADS)))
    dtb_row = jnp.pad(small["dt_bias"].reshape(1, B_HEADS), ((0, 0), (0, 128 - B_HEADS)))
    conv_w8 = jnp.pad(small["conv_w"].reshape(CONV_K, CONV_CH), ((0, 8 - CONV_K), (0, 0)))
    w_on = small["w_onorm"].reshape(1, B_DIM)
    g_mix, g_ffn = small["g_mix"].reshape(1, D_MODEL), small["g_ffn"].reshape(1, D_MODEL)
    g_ple, g_final = small["g_ple"].reshape(1, D_MODEL), small["g_final"].reshape(1, D_MODEL)
    bias_band = make_bias_band(small["rel_bias"].reshape(A_HEADS, N_REL))

    tok = later.begin()
    proj, h1, bd32 = rms_matmul(x, g_mix + tok, w_inp, "in_proj", tm=1024)
    y_a, lse = attn_fwd(proj, bias_band, b, s)
    tok = later.forward(lse)
    c = conv_fwd(proj, conv_w8 + tok, b, s)
    u, wk, qg, kdec, pm, egl, tmat = dn_prep(c, bd32, al_row, dtb_row, b, s)
    o_b, states = dn_scan_fwd(u, wk, qg, kdec, pm, egl, b, s)
    wts = later.finish(o_b)
    x1, merged = mid_fwd(x, y_a, o_b, proj, w_on, wts["w_branch_a"], wts["w_branch_b"], wts["w_out"])
    gate, up, act, h2 = ffn_up(x1, g_ffn, wts["w_gate_up"])
    x2 = matmul_residual(act, wts["w_down"], x1, "ffn_down")

    dx2, dw_pg, dw_pp, dg_ple, dg_final, loss = tail_fwd_bwd(
        x2, p, target, g_ple, g_final, wts["w_ple_gate"], wts["w_ple_proj"])
    dgate, dup = ffn_act_bwd(dx2, gate, up, wts["w_down"])
    w_gu = wts["w_gate_up"]
    dx1, dg_ffn = in_proj_bwd([(dgate, 0, D_FF), (dup, 0, D_FF)], [(w_gu, 0), (w_gu, D_FF)], x1, dx2, g_ffn,
                              name="ffn_in_bwd")
    dw_down = matmul_tn(act, dx2, "dw_down")
    dw_gu = matmul_tn(h2, dgate, "dw_gate", width=2 * D_FF)
    dw_gu = matmul_tn(h2, dup, "dw_up", into=dw_gu, col0=D_FF)
    dy_a, do_b, dz, dgates, dw_out, dwa, dwb, dw_on = mid_bwd(
        dx1, merged, y_a, o_b, proj, w_on, wts["w_branch_a"], wts["w_branch_b"], wts["w_out"])
    tok = early.begin(dict(w_branch_a=dwa, w_branch_b=dwb, w_out=dw_out, w_gate_up=dw_gu, w_down=dw_down,
                           w_ple_gate=dw_pg, w_ple_proj=dw_pp))
    ddw, ddwk, ddqg, ddkdec, ddp, ddegl = dn_scan_bwd(u, wk, qg, kdec, pm, egl + tok, states, do_b, b, s)
    tok = early.exchange(ddegl)
    dc3, dbd, dal, ddtb = dn_post_bwd(c, bd32, al_row + tok, dtb_row, tmat, ddw, ddwk, ddqg, ddkdec, ddp, ddegl, b, s)
    dconv, dconv_w = conv_bwd(proj, conv_w8, dc3, b, s)
    dqa, dka, dva, dbt, dbf = attn_bwd(proj, bias_band, y_a, lse, dy_a, b, s)
    tok = early.join(dqa)
    d_rel = bias_band_grad(dbt, dbf)

    pieces = [dgates, dqa, dka, dva, dconv, dz, dbd]
    bounds = [0, 2048, 2560, 3072, 3584, 5120, 5632, 5760]
    windows = [(dgates, 0, 2048), (dqa, 0, 512), (dka, 0, 512), (dva, 0, 512), (dconv, 0, 512), (dconv, 512, 512),
               (dconv, 1024, 512), (dz, 0, 512), (dbd, 0, 128)]
    w_cols = [0, P_QA, P_KA, P_VA, P_CONV, P_CONV + 512, P_CONV + 1024, P_Z, P_BD]
    dx, dg_mix = in_proj_bwd(windows, [(w_inp, c0) for c0 in w_cols], x, dx1, g_mix + tok)
    dwp = None
    for k, pc in enumerate(pieces):
        dwp = matmul_tn(h1, pc, "dw_in_%d" % k, into=dwp, col0=bounds[k], width=P_WIDTH)
    reduced_early = early.finish(dwp)
    dw_in = jnp.concatenate([dwp[:, P_QA:P_BD + 8], dwp[:, :P_QA]], axis=1)

    grads = dict(w_in=dw_in, w_branch_a=dwa, w_branch_b=dwb, w_out=dw_out, w_gate_up=dw_gu, w_down=dw_down,
                 w_ple_gate=dw_pg, w_ple_proj=dw_pp)
    small_grads = dict(g_mix=dg_mix[0], g_ffn=dg_ffn[0], g_ple=dg_ple[0], g_final=dg_final[0],
                       conv_w=dconv_w[:CONV_K].reshape(-1), rel_bias=d_rel.reshape(-1), w_onorm=dw_on[0],
                       a_log=dal[0, :B_HEADS], dt_bias=ddtb[0, :B_HEADS], loss=loss[0, :1])
    return dx.reshape(b, s, D_MODEL), grads, small_grads, reduced_early


BIG = (("w_in", (D_MODEL, D_IN), 1), ("w_branch_a", (A_WIDTH, D_MODEL), 1), ("w_branch_b", (B_WIDTH, D_MODEL), 1),
       ("w_out", (D_MODEL, D_MODEL), 0), ("w_gate_up", (D_MODEL, 2 * D_FF), 1), ("w_down", (D_FF, D_MODEL), 0),
       ("w_ple_gate", (D_MODEL, D_MODEL), 0), ("w_ple_proj", (PLE_DIM, D_MODEL), 1))
N_CHIPS = 4
FIRST_WEIGHTS = ("w_in",)
LATER_WEIGHTS = ("w_branch_a", "w_branch_b", "w_out", "w_gate_up", "w_down", "w_ple_gate", "w_ple_proj")
LATE_GRADS = ("w_in",)
EARLY_GRADS = ("w_branch_a", "w_branch_b", "w_out", "w_gate_up", "w_down", "w_ple_gate", "w_ple_proj")


def _items(names):
    return [it for it in BIG if it[0] in names]


def _shard_shape(shape, axis):
    return (shape[0] // N_CHIPS, shape[1]) if axis == 0 else (shape[0], shape[1] // N_CHIPS)


def _width_groups(names):
    groups = {}
    for n, shape, axis in _items(names):
        rs, cs = _shard_shape(shape, axis)
        groups.setdefault(cs, []).append((n, rs))
    return sorted(groups.items())


def grad_buffers(grads, names):
    info = {n: (shape, axis) for n, shape, axis in _items(names)}
    bufs = []
    for cs, members in _width_groups(names):
        segs = []
        for n, rs in members:
            g = grads[n].astype(BF16)
            segs.append(g.reshape(N_CHIPS, rs, cs) if info[n][1] == 0
                        else jnp.transpose(g.reshape(rs, N_CHIPS, cs), (1, 0, 2)))
        bufs.append(segs[0] if len(segs) == 1 else jnp.concatenate(segs, axis=1))
    return bufs


def split_buffers(reduced, names):
    out = {}
    for (cs, members), buf in zip(_width_groups(names), reduced):
        r0 = 0
        for n, rs in members:
            out[n] = buf[r0:r0 + rs]
            r0 += rs
    return out


def _place():
    return lax.axis_index("x"), lax.axis_index("y"), lax.axis_index("c")


ANY = pl.BlockSpec(memory_space=pl.ANY)


def _gathered_shape(item):
    n, shape, _ = item
    return (N_CHIPS,) + _shard_shape(shape, 1) if n == "w_in" else shape


def _gather_block(o_ref, item, cx, cy, hf):
    n, shape, axis = item
    rs, cs = _shard_shape(shape, axis)
    hr = rs // 2
    ci = 2 * cx + cy
    if n == "w_in":
        return o_ref.at[ci, pl.ds(pl.multiple_of(hf * hr, 16), hr), :]
    if axis == 0:
        return o_ref.at[pl.ds(pl.multiple_of(ci * rs + hf * hr, 16), hr), :]
    return o_ref.at[pl.ds(pl.multiple_of(hf * hr, 16), hr), pl.ds(pl.multiple_of(ci * cs, 128), cs)]


def _own_half(w_ref, item, c):
    hr = _shard_shape(item[1], item[2])[0] // 2
    return w_ref.at[pl.ds(pl.multiple_of(c * hr, 16), hr), :]


def _gather_slot(o_ref, item, cx, cy):
    n, shape, axis = item
    rs, cs = _shard_shape(shape, axis)
    ci = 2 * cx + cy
    if n == "w_in":
        return o_ref.at[ci]
    if axis == 0:
        return o_ref.at[pl.ds(pl.multiple_of(ci * rs, 16), rs), :]
    return o_ref.at[:, pl.ds(pl.multiple_of(ci * cs, 128), cs)]


def _other_chips(x, y):
    return [(1 - x, y), (x, 1 - y), (1 - x, 1 - y)]


def allgather_weights(shards, names, chip):
    items = _items(names)
    nw = len(items)

    def body(*refs):
        w_refs, o_refs = refs[:nw], refs[nw:2 * nw]
        send_sems, recv_sems = refs[2 * nw:]
        x, y, c = _place()
        sibling = (x, y, 1 - c)
        chips = _other_chips(x, y)

        def copy(k, src, dst, to):
            return pltpu.make_async_remote_copy(src_ref=src, dst_ref=dst, send_sem=send_sems.at[k],
                                                recv_sem=recv_sems.at[k], device_id=to, device_id_type=MESH)

        def blk(i, cx, cy, hf):
            return _gather_block(o_refs[i], items[i], cx, cy, hf)

        def my_half(i):
            return _own_half(w_refs[i], items[i], c)

        def own(i):
            return _gather_slot(o_refs[i], items[i], x, y)

        first = [copy(7 * i + j, my_half(i), blk(i, x, y, c), (*chip_, c))
                 for i in range(nw) for j, chip_ in enumerate(chips)]
        first += [copy(7 * i + 6, w_refs[i], own(i), sibling) for i in range(nw)]
        for cp in first:
            cp.start()
        passed = []
        for i in range(nw):
            for j, chip_ in enumerate(chips):
                copy(7 * i + j, my_half(i), blk(i, *chip_, c), (*chip_, c)).wait_recv()
                fwd = copy(7 * i + 3 + j, blk(i, *chip_, c), blk(i, *chip_, c), sibling)
                fwd.start()
                passed.append(fwd)
        for i in range(nw):
            for j, chip_ in enumerate(chips):
                copy(7 * i + 3 + j, my_half(i), blk(i, *chip_, 1 - c), sibling).wait_recv()
            copy(7 * i + 6, w_refs[i], own(i), sibling).wait_recv()
        for cp in first + passed:
            cp.wait_send()

    outs = pl.pallas_call(
        body, name="allgather_weights",
        in_specs=[ANY] * nw, out_specs=[ANY] * nw,
        out_shape=[jax.ShapeDtypeStruct(_gathered_shape(it), BF16) for it in items],
        scratch_shapes=[pltpu.SemaphoreType.DMA((7 * nw,)), pltpu.SemaphoreType.DMA((7 * nw,))],
    )(*[shards[it[0]] for it in items])
    return {it[0]: o for it, o in zip(items, outs)}


HBM_SPEC = pl.BlockSpec(memory_space=pltpu.HBM)
SEM_SPEC = pl.BlockSpec(memory_space=pltpu.SEMAPHORE)
EFFECT = pltpu.SideEffectType.DATAFLOW_SIDE_EFFECTING


def _in_hbm(a):
    return pltpu.with_memory_space_constraint(a, pltpu.HBM)


def copies_start(name, bufs, ncopies, plan):
    nb = len(bufs)

    def body(*refs):
        in_refs, send_sems, recv_sems, token = refs[:nb], refs[nb], refs[nb + 1], refs[-1]
        for k, (src, dst, to) in enumerate(plan(in_refs)):
            pltpu.make_async_remote_copy(src_ref=src, dst_ref=dst, send_sem=send_sems.at[k],
                                         recv_sem=recv_sems.at[k], device_id=to, device_id_type=MESH).start()
        token[...] = jnp.zeros_like(token)

    outs = pl.pallas_call(
        body, name=name,
        in_specs=[HBM_SPEC] * nb,
        out_specs=(SEM_SPEC, SEM_SPEC, *[HBM_SPEC] * nb, pl.BlockSpec(memory_space=pltpu.VMEM)),
        out_shape=(pltpu.SemaphoreType.DMA((ncopies,)), pltpu.SemaphoreType.DMA((ncopies,)),
                   *[pltpu.HBM(b.shape, b.dtype) for b in bufs], jax.ShapeDtypeStruct((8, 128), F32)),
        input_output_aliases={i: 2 + i for i in range(nb)},
        compiler_params=pltpu.CompilerParams(has_side_effects=EFFECT),
    )(*[_in_hbm(b) for b in bufs])
    return outs[0], outs[1], list(outs[2:2 + nb]), outs[-1][0, 0]


def copies_wait(name, send_sems, recv_sems, bufs, after, plan):
    nb = len(bufs)

    def body(*refs):
        in_refs, s_sems, r_sems = refs[:nb], refs[nb], refs[nb + 1]
        for k, (src, dst, to) in enumerate(plan(in_refs)):
            cp = pltpu.make_async_remote_copy(src_ref=src, dst_ref=dst, send_sem=s_sems.at[k],
                                              recv_sem=r_sems.at[k], device_id=to, device_id_type=MESH)
            cp.wait_send()
            cp.wait_recv()

    return list(pl.pallas_call(
        body, name=name,
        in_specs=[HBM_SPEC] * nb + [SEM_SPEC, SEM_SPEC, ANY],
        out_specs=tuple([HBM_SPEC] * nb),
        out_shape=tuple(pltpu.HBM(b.shape, b.dtype) for b in bufs),
        input_output_aliases={i: i for i in range(nb)},
        compiler_params=pltpu.CompilerParams(has_side_effects=EFFECT),
    )(*bufs, send_sems, recv_sems, after))


def _landing(shape, dtype):
    return _in_hbm(lax.empty(shape, dtype))


class LaterWeights:
    def __init__(self, shards, chip):
        self.items = _items(LATER_WEIGHTS)
        self.shards, self.chip = shards, chip
        self.nw = len(self.items)

    def _ici_plan(self, refs):
        x, y, c = _place()
        w_refs, o_refs = refs[:self.nw], refs[self.nw:]
        plan = [(_own_half(w_refs[i], it, c), _gather_block(o_refs[i], it, x, y, c), (*chip_, c))
                for i, it in enumerate(self.items) for chip_ in _other_chips(x, y)]
        return plan + [(w_refs[i], _gather_slot(o_refs[i], it, x, y), (x, y, 1 - c))
                       for i, it in enumerate(self.items)]

    def _d2d_plan(self, refs):
        x, y, c = _place()
        return [(_gather_block(refs[i], it, *chip_, c), _gather_block(refs[i], it, *chip_, c), (x, y, 1 - c))
                for i, it in enumerate(self.items) for chip_ in _other_chips(x, y)]

    def _d2d_wait_plan(self, refs):
        x, y, c = _place()
        return [(_gather_block(refs[i], it, *chip_, c), _gather_block(refs[i], it, *chip_, 1 - c), (x, y, 1 - c))
                for i, it in enumerate(self.items) for chip_ in _other_chips(x, y)]

    def _ici_wait_plan(self, refs):
        x, y, c = _place()
        w_refs, o_refs = refs[:self.nw], refs[self.nw:]
        plan = [(_own_half(w_refs[i], it, c), _gather_block(o_refs[i], it, *chip_, c), (*chip_, c))
                for i, it in enumerate(self.items) for chip_ in _other_chips(x, y)]
        return plan + [(w_refs[i], _gather_slot(o_refs[i], it, x, y), (x, y, 1 - c))
                       for i, it in enumerate(self.items)]

    def begin(self):
        srcs = [self.shards[it[0]] for it in self.items]
        lands = [_landing(_gathered_shape(it), BF16) for it in self.items]
        self.s1, self.r1, self.b1, tok = copies_start("gather_ici_start", srcs + lands, 4 * self.nw, self._ici_plan)
        return tok

    def forward(self, after):
        b1 = copies_wait("gather_ici_wait", self.s1, self.r1, self.b1, after, self._ici_wait_plan)
        self.s2, self.r2, self.b2, tok = copies_start("gather_d2d_start", b1[self.nw:], 3 * self.nw, self._d2d_plan)
        return tok

    def finish(self, after):
        outs = copies_wait("gather_d2d_wait", self.s2, self.r2, self.b2, after, self._d2d_wait_plan)
        return {it[0]: o for it, o in zip(self.items, outs)}


def small_allreduce(v, name):
    r = v.shape[0]

    def body(v_ref, o_ref, buf, send_sems, recv_sems):
        x, y, c = _place()
        me = 4 * x + 2 * y + c
        buf[me] = v_ref[...]
        flips = [(fx, fy, fc) for fx in (0, 1) for fy in (0, 1) for fc in (0, 1)][1:]
        peers = [((1 - x) if fx else x, (1 - y) if fy else y, (1 - c) if fc else c) for fx, fy, fc in flips]

        def copy(k, slot, to):
            return pltpu.make_async_remote_copy(src_ref=v_ref, dst_ref=buf.at[slot], send_sem=send_sems.at[k],
                                                recv_sem=recv_sems.at[k], device_id=to, device_id_type=MESH)

        sends = [copy(k, me, peer) for k, peer in enumerate(peers)]
        for cp in sends:
            cp.start()
        for k, (px, py, pc) in enumerate(peers):
            copy(k, 4 * px + 2 * py + pc, (px, py, pc)).wait_recv()
        for cp in sends:
            cp.wait_send()
        acc = buf[0]
        for d in range(1, 8):
            acc = acc + buf[d]
        o_ref[...] = acc

    return pl.pallas_call(
        body, name=name,
        in_specs=[pl.BlockSpec(memory_space=pltpu.VMEM)], out_specs=pl.BlockSpec(memory_space=pltpu.VMEM),
        out_shape=jax.ShapeDtypeStruct((r, 128), F32),
        scratch_shapes=[pltpu.VMEM((8, r, 128), F32), pltpu.SemaphoreType.DMA((7,)), pltpu.SemaphoreType.DMA((7,))],
    )(v)


def add_halves(g, other, place):
    half, wd = other.shape[1:]
    tr = _tile_rows(half, wd)
    nblk = half // tr

    def body(pref, g0, g1, g2, g3, o0, o1, o2, o3, pf_ref, pb_ref):
        f = lambda r: r[...].astype(F32)
        pf_ref[...] = f(g0) + f(o0)
        pb_ref[0] = _bf(f(g1) + f(o1))
        pb_ref[1] = _bf(f(g2) + f(o2))
        pb_ref[2] = _bf(f(g3) + f(o3))

    gspec = lambda k: pl.BlockSpec((None, tr, wd), lambda i, pr: ((pr[0] + k) % N_CHIPS, pr[1] * nblk + i, 0))
    ospec = lambda k: pl.BlockSpec((None, tr, wd), lambda i, pr: ((pr[0] + k) % N_CHIPS, i, 0))
    return pl.pallas_call(
        body, name="add_halves",
        grid_spec=pltpu.PrefetchScalarGridSpec(
            num_scalar_prefetch=1, grid=(nblk,),
            in_specs=[gspec(0), gspec(1), gspec(2), gspec(3), ospec(0), ospec(1), ospec(2), ospec(3)],
            out_specs=[pl.BlockSpec((tr, wd), lambda i, pr: (i, 0)),
                       pl.BlockSpec((3, tr, wd), lambda i, pr: (0, i, 0))]),
        out_shape=[jax.ShapeDtypeStruct((half, wd), F32), jax.ShapeDtypeStruct((3, half, wd), BF16)],
        compiler_params=_cp(("parallel",), VMEM_LIMIT),
    )(place, g, g, g, g, other, other, other, other)


def _tile_rows(n, width):
    best = 16
    for t in range(16, max(16, (384 * 1024) // width) + 1, 16):
        if n % t == 0:
            best = t
    assert n % best == 0
    return best


def add_partials(pf, got, place):
    half, wd = pf.shape
    tr = _tile_rows(half, wd)

    def body(pref, pf_ref, got_ref, o_ref):
        o_ref[...] = ((pf_ref[...] + got_ref[0].astype(F32)) + got_ref[1].astype(F32)) + got_ref[2].astype(F32)

    return pl.pallas_call(
        body, name="add_partials",
        grid_spec=pltpu.PrefetchScalarGridSpec(
            num_scalar_prefetch=1, grid=(half // tr,),
            in_specs=[pl.BlockSpec((tr, wd), lambda i, pr: (i, 0)),
                      pl.BlockSpec((3, tr, wd), lambda i, pr: (0, i, 0))],
            out_specs=pl.BlockSpec((None, tr, wd), lambda i, pr: (pr[1], i, 0))),
        out_shape=jax.ShapeDtypeStruct((2, half, wd), F32),
        compiler_params=_cp(("parallel",), VMEM_LIMIT),
    )(place, pf, got)


class GradReduce:
    def __init__(self, place, names, tag):
        self.place, self.names, self.tag = place, names, tag
        self.nb = len(_width_groups(names))

    def _swap_plan(self, refs):
        x, y, c = _place()
        plan = []
        for g_ref, o_ref in zip(refs[:self.nb], refs[self.nb:]):
            half = o_ref.shape[1]
            plan.append((g_ref.at[:, pl.ds(pl.multiple_of((1 - c) * half, 16), half), :], o_ref, (x, y, 1 - c)))
        return plan

    def _exchange_plan(self, refs):
        x, y, c = _place()
        me = 2 * x + y
        return [(p_ref.at[k - 1], o_ref.at[k - 1], (((me + k) % N_CHIPS) // 2, ((me + k) % N_CHIPS) % 2, c))
                for p_ref, o_ref in zip(refs[:self.nb], refs[self.nb:]) for k in range(1, N_CHIPS)]

    def _join_plan(self, refs):
        x, y, c = _place()
        return [(r.at[c], r.at[c], (x, y, 1 - c)) for r in refs]

    def _join_wait_plan(self, refs):
        x, y, c = _place()
        return [(r.at[c], r.at[1 - c], (x, y, 1 - c)) for r in refs]

    def begin(self, grads):
        gs = grad_buffers(grads, self.names)
        lands = [_landing((N_CHIPS, g.shape[1] // 2, g.shape[2]), BF16) for g in gs]
        self.s1, self.r1, self.b1, tok = copies_start(self.tag + "_swap_start", gs + lands, self.nb, self._swap_plan)
        return tok

    def exchange(self, after):
        b1 = copies_wait(self.tag + "_swap_wait", self.s1, self.r1, self.b1, after, self._swap_plan)
        sums = [add_halves(g, other, self.place) for g, other in zip(b1[:self.nb], b1[self.nb:])]
        self.pfs = [pf for pf, _ in sums]
        pbs = [pb for _, pb in sums]
        lands = [_landing(pb.shape, BF16) for pb in pbs]
        self.s2, self.r2, self.b2, tok = copies_start(self.tag + "_exchange_start", pbs + lands, 3 * self.nb,
                                                      self._exchange_plan)
        return tok

    def join(self, after):
        b2 = copies_wait(self.tag + "_exchange_wait", self.s2, self.r2, self.b2, after, self._exchange_plan)
        boths = [add_partials(pf, got, self.place) for pf, got in zip(self.pfs, b2[self.nb:])]
        self.s3, self.r3, self.b3, tok = copies_start(self.tag + "_join_start", boths, self.nb, self._join_plan)
        return tok

    def finish(self, after):
        boths = copies_wait(self.tag + "_join_wait", self.s3, self.r3, self.b3, after, self._join_wait_plan)
        return split_buffers([b.reshape(-1, b.shape[2]) for b in boths], self.names)


SMALL = (("g_mix", D_MODEL), ("g_ffn", D_MODEL), ("g_ple", D_MODEL), ("g_final", D_MODEL),
         ("conv_w", CONV_K * CONV_CH), ("rel_bias", A_HEADS * N_REL), ("w_onorm", B_DIM),
         ("a_log", B_HEADS), ("dt_bias", B_HEADS), ("loss", 1))


def _pad128(v):
    v = v.reshape(-1)
    return jnp.pad(v, (0, -v.shape[0] % 128))


def pack_small(d, names, rows):
    flat = jnp.concatenate([_pad128(d[n]) for n in names]).reshape(-1, 128)
    return jnp.pad(flat, ((0, rows - flat.shape[0]), (0, 0)))


def unpack_small(flat, names_sizes):
    out, r0 = {}, 0
    v = flat.reshape(-1)
    for n, size in names_sizes:
        out[n] = v[r0:r0 + size]
        r0 += -(-size // 128) * 128
    return out


def kernel(x, p, g_mix, w_in, conv_w, a_log, dt_bias, rel_bias, w_onorm, w_branch_a, w_branch_b, w_out, g_ffn, w_gate_up, w_down, g_ple, w_ple_gate, w_ple_proj, g_final, loss_target, m_g_mix, m_w_in, m_conv_w, m_a_log, m_dt_bias, m_rel_bias, m_w_onorm, m_w_branch_a, m_w_branch_b, m_w_out, m_g_ffn, m_w_gate_up, m_w_down, m_g_ple, m_w_ple_gate, m_w_ple_proj, m_g_final, v_g_mix, v_w_in, v_conv_w, v_a_log, v_dt_bias, v_rel_bias, v_w_onorm, v_w_branch_a, v_w_branch_b, v_w_out, v_g_ffn, v_w_gate_up, v_w_down, v_g_ple, v_w_ple_gate, v_w_ple_proj, v_g_final):
    names = ["g_mix", "w_in", "conv_w", "a_log", "dt_bias", "rel_bias", "w_onorm", "w_branch_a", "w_branch_b",
             "w_out", "g_ffn", "w_gate_up", "w_down", "g_ple", "w_ple_gate", "w_ple_proj", "g_final"]
    w = dict(zip(names, [g_mix, w_in, conv_w, a_log, dt_bias, rel_bias, w_onorm, w_branch_a, w_branch_b, w_out,
                         g_ffn, w_gate_up, w_down, g_ple, w_ple_gate, w_ple_proj, g_final]))
    m = dict(zip(names, [m_g_mix, m_w_in, m_conv_w, m_a_log, m_dt_bias, m_rel_bias, m_w_onorm, m_w_branch_a,
                         m_w_branch_b, m_w_out, m_g_ffn, m_w_gate_up, m_w_down, m_g_ple, m_w_ple_gate,
                         m_w_ple_proj, m_g_final]))
    v = dict(zip(names, [v_g_mix, v_w_in, v_conv_w, v_a_log, v_dt_bias, v_rel_bias, v_w_onorm, v_w_branch_a,
                         v_w_branch_b, v_w_out, v_g_ffn, v_w_gate_up, v_w_down, v_g_ple, v_w_ple_gate,
                         v_w_ple_proj, v_g_final]))
    xi, yi, ci = _place()
    chip = 2 * xi + yi
    big_names = [n for n, _, _ in BIG]

    shards2d = {n: w[n].reshape(w[n].shape[-2:]) for n in big_names}
    shards_bf = {n: a.astype(BF16) for n, a in shards2d.items()}
    g4 = allgather_weights(shards_bf, FIRST_WEIGHTS, chip)["w_in"]
    place = jnp.stack([chip, ci]).astype(jnp.int32)
    conv_sh = jnp.where(ci == 0, w["conv_w"].reshape(CONV_K, CONV_CH // N_CHIPS), 0.0)
    conv_slots = lax.dynamic_update_slice(jnp.zeros((N_CHIPS, CONV_K, CONV_CH // N_CHIPS), F32), conv_sh[None],
                                          (chip, 0, 0))
    conv_all = small_allreduce(conv_slots.reshape(-1, 128), "gather_conv_w")
    conv_full = jnp.transpose(conv_all.reshape(N_CHIPS, CONV_K, CONV_CH // N_CHIPS), (1, 0, 2)).reshape(CONV_K, CONV_CH)
    small = {n: w[n] for n in names if n not in big_names}
    small["conv_w"] = conv_full

    grad_x, grads, small_grads, reduced_early = local_step(
        x, p[0], loss_target, g4, small, LaterWeights(shards_bf, chip), GradReduce(place, EARLY_GRADS, "grads"))

    late = GradReduce(place, LATE_GRADS, "late")
    late.begin(grads)
    dep = jnp.full((8, 128), late.exchange(grads["w_in"]), F32)
    gshard = dict(reduced_early)
    small_names = [n for n, _ in SMALL]
    red = unpack_small(small_allreduce(pack_small(small_grads, small_names, 112), "allreduce_small"), SMALL)
    loss = red["loss"][0]
    conv_g = lax.dynamic_slice(red["conv_w"].reshape(CONV_K, N_CHIPS, CONV_CH // N_CHIPS), (0, chip, 0),
                               (CONV_K, 1, CONV_CH // N_CHIPS))
    gsmall = {n: red[n].reshape(w[n].shape) for n in small_names if n not in ("loss", "conv_w")}
    gsmall["conv_w"] = conv_g.reshape(w["conv_w"].shape)

    grad, delta, new_m, new_v = {}, {}, {}, {}
    for n in list(EARLY_GRADS) + list(LATE_GRADS):
        if n in LATE_GRADS:
            late.join(v_)
            gshard.update(late.finish(v_))
        shp = w[n].shape
        d_, m_, v_ = adamw(shards2d[n], gshard[n], m[n].reshape(shp[-2:]), v[n].reshape(shp[-2:]), "adamw_" + n,
                           dep=dep if n in EARLY_GRADS else None)
        dep, v_ = lax.optimization_barrier((dep, v_))
        grad[n], delta[n], new_m[n], new_v[n] = gshard[n].reshape(shp), d_.reshape(shp), m_.reshape(shp), v_.reshape(shp)
    snames = [n for n in small_names if n != "loss"]
    ssizes = [(n, w[n].size) for n in snames]
    pk = lambda d: pack_small(d, snames, 64)
    d_, m_, v_ = adamw(pk(w), pk(gsmall), pk(m), pk(v), "adamw_small")
    ds, ms, vs = unpack_small(d_, ssizes), unpack_small(m_, ssizes), unpack_small(v_, ssizes)
    for n in snames:
        shp = w[n].shape
        grad[n], delta[n], new_m[n], new_v[n] = gsmall[n], ds[n].reshape(shp), ms[n].reshape(shp), vs[n].reshape(shp)

    return (loss, grad_x, *[grad[n] for n in names], *[delta[n] for n in names],
            *[new_m[n] for n in names], *[new_v[n] for n in names])
```

```python
import functools

import jax
import jax.numpy as jnp
from jax import lax
from jax.experimental import pallas as pl
from jax.experimental.pallas import tpu as pltpu

F32 = jnp.float32
BF16 = jnp.bfloat16
HI = lax.Precision.HIGHEST
MESH = pl.DeviceIdType.MESH

D_MODEL = 1024
CHUNK = 64
PLE_DIM = 256
EPS = 1e-6
A_HEADS = 8
A_HEAD_DIM = 64
A_WIDTH = 512
A_LOOKBACK = 8
BAND = (A_LOOKBACK + 1) * CHUNK
TAIL = 3 * CHUNK
REL_CLIP = 128
N_REL = 2 * REL_CLIP + 1
B_HEADS = 4
B_DIM = 128
B_WIDTH = 512
CONV_K = 4
CONV_CH = 1536
D_FF = 2816
SPLIT_Z = 3584
D_IN = 5640
ADAM_LR, ADAM_B1, ADAM_B2, ADAM_EPS, ADAM_WD, ADAM_STEP = 0.001, 0.9, 0.999, 1e-08, 0.01, 10

P_GATES, P_QA, P_KA, P_VA, P_CONV, P_Z, P_BD, P_WIDTH = 0, 2048, 2560, 3072, 3584, 5120, 5632, 5760

VMEM_LIMIT = 56 * 1024 * 1024


def _cp(sem, vmem=None, **kw):
    return pltpu.CompilerParams(dimension_semantics=sem, vmem_limit_bytes=vmem, **kw)


def _tile(n, cap):
    best = None
    for t in range(128, cap + 1, 128):
        if n % t == 0:
            best = t
    assert best is not None, (n, cap)
    return best


def _nn(a, b, prec=None):
    return lax.dot_general(a, b, (((1,), (0,)), ((), ())), preferred_element_type=F32, precision=prec)


def _nt(a, b, prec=None):
    return lax.dot_general(a, b, (((1,), (1,)), ((), ())), preferred_element_type=F32, precision=prec)


def _tn(a, b, prec=None):
    return lax.dot_general(a, b, (((0,), (0,)), ((), ())), preferred_element_type=F32, precision=prec)


def _bnn(a, b, prec=None):
    return lax.dot_general(a, b, (((2,), (1,)), ((0,), (0,))), preferred_element_type=F32, precision=prec)


def _bnt(a, b, prec=None):
    return lax.dot_general(a, b, (((2,), (2,)), ((0,), (0,))), preferred_element_type=F32, precision=prec)


def _bf(a):
    return a.astype(BF16)


def _split(a):
    hi = a.astype(BF16)
    return hi, (a - hi.astype(F32)).astype(BF16)


def _split3(a):
    h1 = _bf(a)
    r1 = a - h1.astype(F32)
    h2 = _bf(r1)
    return h1, h2, _bf(r1 - h2.astype(F32))


def _bnn_exact(lhs_b, rhs):
    h1, h2, h3 = _split3(rhs)
    return _bnn(lhs_b, h1) + (_bnn(lhs_b, h2) + _bnn(lhs_b, h3))


def _bnn3(a, b):
    ah, al = a if isinstance(a, tuple) else _split(a)
    bh, bl = b if isinstance(b, tuple) else _split(b)
    return _bnn(ah, bh) + (_bnn(ah, bl) + _bnn(al, bh))


def _sigmoid(x):
    return 0.5 * jnp.tanh(0.5 * x) + 0.5


def _softplus(x):
    return jnp.maximum(x, 0.0) + jnp.log(1.0 + jnp.exp(-jnp.abs(x)))


def rms_matmul(x, g, w, name, tm=512, tn_cap=1024):
    t, d = x.shape
    n = w.shape[1]
    tm = min(tm, t)
    tn = _tile(n, tn_cap)

    nj = n // tn

    def body(x_ref, g_ref, w_ref, o_ref, h_ref, tail_ref):
        @pl.when(pl.program_id(1) == 0)
        def _():
            xv = x_ref[...]
            r = lax.rsqrt(jnp.mean(xv * xv, axis=-1, keepdims=True) + EPS)
            h_ref[...] = _bf(xv * r * g_ref[...])

        res = _nn(h_ref[...], w_ref[...])
        o_ref[...] = _bf(res)

        @pl.when(pl.program_id(1) == nj - 1)
        def _():
            tail_ref[...] = res[:, tn - 128:]

    return pl.pallas_call(
        body, name=name, grid=(t // tm, nj),
        in_specs=[pl.BlockSpec((tm, d), lambda i, j: (i, 0)),
                  pl.BlockSpec((1, d), lambda i, j: (0, 0)),
                  pl.BlockSpec((d, tn), lambda i, j: (0, j))],
        out_specs=[pl.BlockSpec((tm, tn), lambda i, j: (i, j)),
                   pl.BlockSpec((tm, d), lambda i, j: (i, 0)),
                   pl.BlockSpec((tm, 128), lambda i, j: (i, 0))],
        out_shape=[jax.ShapeDtypeStruct((t, n), BF16), jax.ShapeDtypeStruct((t, d), BF16),
                   jax.ShapeDtypeStruct((t, 128), F32)],
        compiler_params=_cp(("parallel", "arbitrary"), VMEM_LIMIT),
    )(x, g, w)


def matmul_tn(a, b, name, into=None, col0=0, width=None, tm=1024, tk_cap=1408, tn_cap=1408):
    m, k1 = a.shape
    n = b.shape[1]
    tm = min(tm, m)
    tk = _tile(k1, tk_cap)
    tn = _tile(n, tn_cap)
    while col0 % tn:
        tn = _tile(n, tn - 128)
    nk = m // tm
    c0 = col0 // tn

    def body(*refs):
        a_ref, b_ref, o_ref, acc = refs[0], refs[1], refs[-2], refs[-1]

        @pl.when(pl.program_id(2) == 0)
        def _():
            acc[...] = jnp.zeros_like(acc)

        acc[...] += _tn(_bf(a_ref[...]), _bf(b_ref[...]))

        @pl.when(pl.program_id(2) == nk - 1)
        def _():
            o_ref[...] = _bf(acc[...])

    in_specs = [pl.BlockSpec((tm, tk), lambda i, j, k: (k, i)),
                pl.BlockSpec((tm, tn), lambda i, j, k: (k, j))]
    args = [a, b]
    total = n if width is None else width
    aliases = {}
    if into is not None:
        in_specs.append(ANY)
        args.append(into)
        total = into.shape[1]
        aliases = {2: 0}
    return pl.pallas_call(
        body, name=name, grid=(k1 // tk, n // tn, nk),
        in_specs=in_specs,
        out_specs=pl.BlockSpec((tk, tn), lambda i, j, k: (i, c0 + j)),
        out_shape=jax.ShapeDtypeStruct((k1, total), BF16),
        scratch_shapes=[pltpu.VMEM((tk, tn), F32)],
        input_output_aliases=aliases,
        compiler_params=_cp(("parallel", "parallel", "arbitrary"), VMEM_LIMIT),
    )(*args)


def _tail_onehot(qi):
    r = lax.broadcasted_iota(jnp.int32, (384, TAIL), 0)
    kj = lax.broadcasted_iota(jnp.int32, (384, TAIL), 1)
    return (r == jnp.minimum(REL_CLIP + qi - kj, REL_CLIP) + REL_CLIP).astype(F32)


def bias_tail(rel_pad):
    def body(rb_ref, o_ref):
        parts = _split3(rb_ref[...])
        for qi in range(CHUNK):
            oh = _bf(_tail_onehot(qi))
            o_ref[qi] = _nn(parts[0], oh) + (_nn(parts[1], oh) + _nn(parts[2], oh))

    return pl.pallas_call(
        body, name="bias_tail",
        out_shape=jax.ShapeDtypeStruct((CHUNK, A_HEADS, TAIL), F32),
    )(rel_pad)


def bias_grad(db_t, db_far):
    def body(t_ref, f_ref, o_ref):
        acc = jnp.zeros((A_HEADS, 384), F32)
        for qi in range(CHUNK):
            oh = _bf(_tail_onehot(qi))
            parts = _split3(t_ref[qi])
            acc = acc + (_nt(parts[0], oh) + (_nt(parts[1], oh) + _nt(parts[2], oh)))
        far = jnp.sum(jnp.sum(f_ref[...], axis=2), axis=1, keepdims=True)
        lane = lax.broadcasted_iota(jnp.int32, (A_HEADS, 384), 1)
        o_ref[...] = acc + jnp.where(lane == 2 * REL_CLIP, far, 0.0)

    return pl.pallas_call(
        body, name="bias_grad",
        out_shape=jax.ShapeDtypeStruct((A_HEADS, 384), F32),
    )(db_t, db_far)


ATT_CB = 8


WIN = BAND + CHUNK


def _stack_heads(a, lane):
    return jnp.concatenate([jnp.where(lane < 64, a, 0.0), jnp.where(lane >= 64, a, 0.0)], axis=0)


def _fill_band_pads(k_ref, v_ref, kp, vp, s):
    z = jnp.zeros((A_LOOKBACK * CHUNK, 128), BF16)
    kp[pl.ds(0, A_LOOKBACK * CHUNK), :] = z
    vp[pl.ds(0, A_LOOKBACK * CHUNK), :] = z
    kp[pl.ds(A_LOOKBACK * CHUNK, s), :] = _bf(k_ref[...])
    vp[pl.ds(A_LOOKBACK * CHUNK, s), :] = _bf(v_ref[...])


def attn_fwd(proj, bias_band, b, s):
    t = b * s
    nc = s // CHUNK
    qb, kb_, vb_ = P_QA // 128, P_KA // 128, P_VA // 128

    nstep = nc // ATT_CB
    rows = ATT_CB * CHUNK

    def body(q_ref, k_ref, v_ref, b_ref, o_ref, lse_ref, kp, vp):
        n0 = pl.program_id(2) * ATT_CB

        @pl.when(n0 == 0)
        def _():
            _fill_band_pads(k_ref, v_ref, kp, vp, s)

        lane = lax.broadcasted_iota(jnp.int32, (2 * CHUNK, 128), 1)
        col = lax.broadcasted_iota(jnp.int32, (4 * CHUNK, WIN), 1)
        bias4 = b_ref[...]

        def pair(pp, carry):
            n = n0 + 2 * pp
            r0 = pl.multiple_of(pp * 2 * CHUNK, 2 * CHUNK)
            start = pl.multiple_of(n * CHUNK, CHUNK)
            kb = kp[pl.ds(start, WIN), :]
            vb = vp[pl.ds(start, WIN), :]
            q4 = _stack_heads(q_ref[pl.ds(r0, 2 * CHUNK), :] * (A_HEAD_DIM ** -0.5), lane)
            sc = jnp.where(col >= (A_LOOKBACK - n) * CHUNK, _nt(_bf(q4), kb) + bias4, -1e30)
            mx = jnp.max(sc, axis=1, keepdims=True)
            p = jnp.exp(sc - mx)
            l = jnp.sum(p, axis=1, keepdims=True)
            o4 = _nn(_bf(p), vb) / l
            lse4 = mx + jnp.log(l)
            o_ref[pl.ds(r0, 2 * CHUNK), :] = jnp.where(lane < 64, o4[:2 * CHUNK], o4[2 * CHUNK:])
            lse_ref[pl.ds(r0, 2 * CHUNK), :] = jnp.where(lane < 64, lse4[:2 * CHUNK], lse4[2 * CHUNK:])
            return carry

        lax.fori_loop(0, ATT_CB // 2, pair, 0, unroll=2)

    return pl.pallas_call(
        body, name="attn_fwd", grid=(b, 4, nstep),
        in_specs=[pl.BlockSpec((rows, 128), lambda bb, m, n: (bb * nstep + n, qb + m)),
                  pl.BlockSpec((s, 128), lambda bb, m, n: (bb, kb_ + m)),
                  pl.BlockSpec((s, 128), lambda bb, m, n: (bb, vb_ + m)),
                  pl.BlockSpec((None, 4 * CHUNK, WIN), lambda bb, m, n: (m, 0, 0))],
        out_specs=[pl.BlockSpec((rows, 128), lambda bb, m, n: (bb * nstep + n, m)),
                   pl.BlockSpec((rows, 128), lambda bb, m, n: (bb * nstep + n, m))],
        out_shape=[jax.ShapeDtypeStruct((t, A_WIDTH), F32), jax.ShapeDtypeStruct((t, A_WIDTH), F32)],
        scratch_shapes=[pltpu.VMEM((s + A_LOOKBACK * CHUNK, 128), BF16),
                        pltpu.VMEM((s + A_LOOKBACK * CHUNK, 128), BF16)],
        compiler_params=_cp(("parallel", "parallel", "arbitrary"), VMEM_LIMIT),
    )(proj, proj, proj, bias_band)


def attn_bwd(proj, bias_band, y_a, lse, dy_a, b, s):
    t = b * s
    nc = s // CHUNK
    qb, kb_, vb_ = P_QA // 128, P_KA // 128, P_VA // 128
    pad = A_LOOKBACK * CHUNK
    nstep = nc // ATT_CB
    rows = ATT_CB * CHUNK

    def body(q_ref, k_ref, v_ref, b_ref, do_ref, o_ref, lse_ref,
             dq_ref, dk_ref, dv_ref, dbt_ref, dbf_ref, kp, vp, dkp, dvp):
        bb = pl.program_id(1)
        n0 = pl.program_id(2) * ATT_CB

        @pl.when(n0 == 0)
        def _():
            _fill_band_pads(k_ref, v_ref, kp, vp, s)
            dkp[...] = jnp.zeros_like(dkp)
            dvp[...] = jnp.zeros_like(dvp)

        @pl.when((n0 == 0) & (bb == 0))
        def _():
            dbt_ref[...] = jnp.zeros_like(dbt_ref)
            dbf_ref[...] = jnp.zeros_like(dbf_ref)

        lane = lax.broadcasted_iota(jnp.int32, (2 * CHUNK, 128), 1)
        col = lax.broadcasted_iota(jnp.int32, (4 * CHUNK, WIN), 1)
        bias4 = b_ref[...]

        def pair(pp, carry):
            n = n0 + 2 * pp
            r0 = pl.multiple_of(pp * 2 * CHUNK, 2 * CHUNK)
            start = pl.multiple_of(n * CHUNK, CHUNK)
            kb = kp[pl.ds(start, WIN), :]
            vb = vp[pl.ds(start, WIN), :]
            q4b = _bf(_stack_heads(q_ref[pl.ds(r0, 2 * CHUNK), :] * (A_HEAD_DIM ** -0.5), lane))
            do4 = _stack_heads(do_ref[pl.ds(r0, 2 * CHUNK), :], lane)
            do4b = _bf(do4)
            o = o_ref[pl.ds(r0, 2 * CHUNK), :]
            lsev = lse_ref[pl.ds(r0, 2 * CHUNK), :]
            lse4 = jnp.concatenate([lsev[:, 0:1], lsev[:, 64:65]], axis=0)
            sc = jnp.where(col >= (A_LOOKBACK - n) * CHUNK, _nt(q4b, kb) + bias4, -1e30)
            p = jnp.exp(sc - lse4)
            dp = _nt(do4b, vb)
            delta = jnp.sum(do4 * jnp.concatenate([o, o], axis=0), axis=1, keepdims=True)
            ds = p * (dp - delta)
            dsb = _bf(ds)
            dq4 = _nn(dsb, kb)
            dq_ref[pl.ds(r0, 2 * CHUNK), :] = _bf(
                jnp.where(lane < 64, dq4[:2 * CHUNK], dq4[2 * CHUNK:]) * (A_HEAD_DIM ** -0.5))
            dkp[pl.ds(start, WIN), :] += _tn(dsb, q4b)
            dvp[pl.ds(start, WIN), :] += _tn(_bf(p), do4b)
            dbt_ref[...] += ds[:, WIN - 256:]
            dbf_ref[...] += ds[:, 0:128] + ds[:, 128:256] + ds[:, 256:384]
            return carry

        lax.fori_loop(0, ATT_CB // 2, pair, 0, unroll=2)

        @pl.when(n0 == nc - ATT_CB)
        def _():
            dk_ref[...] = _bf(dkp[pl.ds(pad, s), :])
            dv_ref[...] = _bf(dvp[pl.ds(pad, s), :])

    return pl.pallas_call(
        body, name="attn_bwd", grid=(4, b, nstep),
        in_specs=[pl.BlockSpec((rows, 128), lambda m, bb, n: (bb * nstep + n, qb + m)),
                  pl.BlockSpec((s, 128), lambda m, bb, n: (bb, kb_ + m)),
                  pl.BlockSpec((s, 128), lambda m, bb, n: (bb, vb_ + m)),
                  pl.BlockSpec((None, 4 * CHUNK, WIN), lambda m, bb, n: (m, 0, 0)),
                  pl.BlockSpec((rows, 128), lambda m, bb, n: (bb * nstep + n, m)),
                  pl.BlockSpec((rows, 128), lambda m, bb, n: (bb * nstep + n, m)),
                  pl.BlockSpec((rows, 128), lambda m, bb, n: (bb * nstep + n, m))],
        out_specs=[pl.BlockSpec((rows, 128), lambda m, bb, n: (bb * nstep + n, m)),
                   pl.BlockSpec((s, 128), lambda m, bb, n: (bb, m)),
                   pl.BlockSpec((s, 128), lambda m, bb, n: (bb, m)),
                   pl.BlockSpec((None, 4 * CHUNK, 256), lambda m, bb, n: (m, 0, 0)),
                   pl.BlockSpec((None, 4 * CHUNK, 128), lambda m, bb, n: (m, 0, 0))],
        out_shape=[jax.ShapeDtypeStruct((t, A_WIDTH), BF16)] * 3
        + [jax.ShapeDtypeStruct((4, 4 * CHUNK, 256), F32),
           jax.ShapeDtypeStruct((4, 4 * CHUNK, 128), F32)],
        scratch_shapes=[pltpu.VMEM((s + pad, 128), BF16), pltpu.VMEM((s + pad, 128), BF16),
                        pltpu.VMEM((s + pad, 128), F32), pltpu.VMEM((s + pad, 128), F32)],
        compiler_params=_cp(("parallel", "arbitrary", "arbitrary"), VMEM_LIMIT),
    )(proj, proj, proj, bias_band, dy_a, y_a, lse)


def _conv_taps(x, w, s):
    row = lax.broadcasted_iota(jnp.int32, x.shape, 0)
    shifted = [x] + [jnp.where(row >= i, pltpu.roll(x, i, 0), 0.0) for i in range(1, CONV_K)]
    acc = shifted[0] * w[CONV_K - 1:CONV_K, :]
    for i in range(1, CONV_K):
        acc = acc + shifted[i] * w[CONV_K - 1 - i:CONV_K - i, :]
    return acc, shifted


def conv_fwd(proj, conv_w8, b, s):
    cb = 512
    c0 = P_CONV // cb

    def body(x_ref, w_ref, o_ref):
        a, _ = _conv_taps(x_ref[...].astype(F32), w_ref[...], s)
        o_ref[...] = a * _sigmoid(a)

    return pl.pallas_call(
        body, name="conv_fwd", grid=(b, CONV_CH // cb),
        in_specs=[pl.BlockSpec((s, cb), lambda bb, j: (bb, c0 + j)),
                  pl.BlockSpec((8, cb), lambda bb, j: (0, j))],
        out_specs=pl.BlockSpec((s, cb), lambda bb, j: (bb, j)),
        out_shape=jax.ShapeDtypeStruct((b * s, CONV_CH), F32),
        compiler_params=_cp(("parallel", "parallel"), VMEM_LIMIT),
    )(proj, conv_w8)


def conv_bwd(proj, conv_w8, dc3, b, s):
    cb = 512
    c0 = P_CONV // cb

    def body(x_ref, w_ref, dc_ref, dx_ref, dw_ref):
        @pl.when(pl.program_id(1) == 0)
        def _():
            dw_ref[...] = jnp.zeros_like(dw_ref)

        w = w_ref[...]
        a, shifted = _conv_taps(x_ref[...].astype(F32), w, s)
        sg = _sigmoid(a)
        da = dc_ref[...] * (sg * (1.0 + a * (1.0 - sg)))
        row = lax.broadcasted_iota(jnp.int32, da.shape, 0)
        dx = da * w[CONV_K - 1:CONV_K, :]
        for i in range(1, CONV_K):
            dx = dx + jnp.where(row < s - i, pltpu.roll(da, s - i, 0), 0.0) * w[CONV_K - 1 - i:CONV_K - i, :]
        dx_ref[...] = _bf(dx)
        r8 =lax.broadcasted_iota(jnp.int32, (8, cb), 0)
        dw = jnp.zeros((8, cb), F32)
        for i in range(CONV_K):
            dw = dw + jnp.where(r8 == CONV_K - 1 - i, jnp.sum(da * shifted[i], axis=0, keepdims=True), 0.0)
        dw_ref[...] += dw

    return pl.pallas_call(
        body, name="conv_bwd", grid=(CONV_CH // cb, b),
        in_specs=[pl.BlockSpec((s, cb), lambda j, bb: (bb, c0 + j)),
                  pl.BlockSpec((8, cb), lambda j, bb: (0, j)),
                  pl.BlockSpec((None, s, cb), lambda j, bb: (j, bb, 0))],
        out_specs=[pl.BlockSpec((s, cb), lambda j, bb: (bb, j)),
                   pl.BlockSpec((8, cb), lambda j, bb: (0, j))],
        out_shape=[jax.ShapeDtypeStruct((b * s, CONV_CH), BF16), jax.ShapeDtypeStruct((8, CONV_CH), F32)],
        compiler_params=_cp(("parallel", "arbitrary"), VMEM_LIMIT),
    )(proj, conv_w8, dc3)


def _pick_lane(v, k):
    lane = lax.broadcasted_iota(jnp.int32, v.shape, 1)
    return jnp.sum(jnp.where(lane == k, v, 0.0), axis=1, keepdims=True)


def _chunk_masks(ncb):
    i = lax.broadcasted_iota(jnp.int32, (ncb, CHUNK, CHUNK), 1)
    j = lax.broadcasted_iota(jnp.int32, (ncb, CHUNK, CHUNK), 2)
    return i, j


def _col_of_row(rowvec, eye):
    return jnp.sum(jnp.where(eye, rowvec, 0.0), axis=2, keepdims=True)


def _dn_chunk_math(cq, ck, cv, bd, al_row, dtb_row, h, ncb, tm=None):
    r = ncb * CHUNK
    i, j = _chunk_masks(ncb)
    eye = i == j
    low = i >= j
    strict = i > j
    ones = jnp.ones((ncb, CHUNK, CHUNK), F32)

    braw = _pick_lane(bd, h)
    draw = _pick_lane(bd, B_HEADS + h)
    al = _pick_lane(al_row, h)
    dtb = _pick_lane(dtb_row, h)
    ea = jnp.exp(al)
    beta = _sigmoid(braw)
    sp_arg = draw + dtb
    g = -ea * _softplus(sp_arg)

    rq = lax.rsqrt(jnp.sum(cq * cq, axis=1, keepdims=True) + EPS)
    rk = lax.rsqrt(jnp.sum(ck * ck, axis=1, keepdims=True) + EPS)
    nq = cq * rq
    kn = ck * rk
    qn = nq * (B_DIM ** -0.5)

    def c3(a):
        return a.reshape(ncb, CHUNK, a.shape[-1])

    qn3, kn3, v3, beta3 = c3(qn), c3(kn), c3(cv), c3(beta)
    gb = jnp.broadcast_to(c3(g), (ncb, CHUNK, CHUNK))
    gc_b = _bnn_exact(low.astype(BF16), gb)
    gr_b = _bnn_exact(_bf(ones), jnp.where(eye, gc_b, 0.0))
    dm = jnp.where(low, jnp.exp(jnp.where(low, gc_b - gr_b, 0.0)), 0.0)
    gc = gc_b[:, :, 0:1]
    gl = gc_b[:, CHUNK - 1:CHUNK, 0:1]
    gam = jnp.exp(gc)
    egl = jnp.exp(gl)
    edec = jnp.exp(gl - gc)

    knb = _bf(kn3)
    kk = _bnt(knb, knb)
    kd = jnp.where(strict, kk * dm, 0.0)
    a = beta3 * kd
    sz = 1 if tm is None else CHUNK
    if tm is None:
        tm = eye.astype(F32)
    while sz < CHUNK:
        off = jnp.where(((i // (2 * sz)) == (j // (2 * sz))) & ((i // sz) != (j // sz)), a, 0.0)
        tmb = _bf(tm)
        tm = tm - _bnn(_bf(_bnn(tmb, _bf(off))), tmb)
        sz *= 2
    bv = beta3 * v3
    bk = (beta3 * gam) * kn3
    sol = _bnn3(_split(tm), jnp.concatenate([bv, bk], axis=2))
    u, wk = sol[:, :, :B_DIM], sol[:, :, B_DIM:]
    qk = _bnt(_bf(qn3), knb)
    p = jnp.where(low, qk * dm, 0.0)
    kdec = kn3 * edec
    qg = gam * qn3
    return dict(beta=beta3, g=c3(g), ea=ea, sp_arg=c3(sp_arg), rq=c3(rq), rk=c3(rk), nq=c3(nq),
                qn=qn3, kn=kn3, v=v3, gc=gc, gl=gl, gam=gam, egl=egl, edec=edec, dm=dm, kd=kd, a=a,
                tm=tm, u=u, wk=wk, qk=qk, p=p, kdec=kdec, qg=qg, eye=eye, low=low, strict=strict)


def dn_prep(c, proj, al_row, dtb_row, b, s, ncb=16):
    t = b * s
    r = ncb * CHUNK
    nblk = t // r
    bd_blk = 0

    def body(cq_ref, ck_ref, cv_ref, bd_ref, al_ref, dtb_ref, u_ref, wk_ref, qg_ref, kdec_ref, p_ref, egl_ref,
             tm_ref):
        h = pl.program_id(1)
        m = _dn_chunk_math(cq_ref[...], ck_ref[...], cv_ref[...], bd_ref[...].astype(F32), al_ref[...], dtb_ref[...], h, ncb)
        tm_ref[...] = m["tm"].reshape(r, CHUNK)
        u_ref[...] = m["u"].reshape(r, B_DIM)
        wk_ref[...] = m["wk"].reshape(r, B_DIM)
        qg_ref[...] = m["qg"].reshape(r, B_DIM)
        kdec_ref[...] = m["kdec"].reshape(r, B_DIM)
        p_ref[...] = m["p"].reshape(r, CHUNK)
        egl_ref[...] = jnp.broadcast_to(m["egl"], (ncb, 8, 128)).reshape(ncb * 8, 128)

    col = lambda k: pl.BlockSpec((r, 128), lambda i, h: (i, k * B_HEADS + h))
    out_col = pl.BlockSpec((r, 128), lambda i, h: (i, h))
    small = pl.BlockSpec((1, 128), lambda i, h: (0, 0))
    return pl.pallas_call(
        body, name="dn_prep", grid=(nblk, B_HEADS),
        in_specs=[col(0), col(1), col(2), pl.BlockSpec((r, 128), lambda i, h: (i, bd_blk)), small, small],
        out_specs=[out_col, out_col, out_col, out_col,
                   pl.BlockSpec((None, r, CHUNK), lambda i, h: (h, i, 0)),
                   pl.BlockSpec((None, ncb * 8, 128), lambda i, h: (h, i, 0)),
                   pl.BlockSpec((None, r, CHUNK), lambda i, h: (h, i, 0))],
        out_shape=[jax.ShapeDtypeStruct((t, B_WIDTH), F32)] * 4
        + [jax.ShapeDtypeStruct((B_HEADS, t, CHUNK), F32),
           jax.ShapeDtypeStruct((B_HEADS, t // 8, 128), F32),
           jax.ShapeDtypeStruct((B_HEADS, t, CHUNK), F32)],
        compiler_params=_cp(("parallel", "parallel"), VMEM_LIMIT),
    )(c, c, c, proj, al_row, dtb_row)


def dn_scan_fwd(u, wk, qg, kdec, p, egl, b, s):
    t = b * s
    nc = s // CHUNK

    def body(u_ref, wk_ref, qg_ref, kdec_ref, p_ref, egl_ref, o_ref, ss_ref, st):
        @pl.when(pl.program_id(0) == 0)
        def _():
            st[...] = jnp.zeros_like(st)

        chains = [(bb, h) for bb in range(b) for h in range(B_HEADS)]
        states = [st[bb * B_HEADS + h] for bb, h in chains]
        sls = [slice(h * B_DIM, (h + 1) * B_DIM) for _, h in chains]
        sbs = [_bf(sh) for sh in states]
        ws = [u_ref[bb, :, sl] - _nt(_bf(wk_ref[bb, :, sl]), sb) for (bb, _), sl, sb in zip(chains, sls, sbs)]
        qs = [_nt(_bf(qg_ref[bb, :, sl]), sb) for (bb, _), sl, sb in zip(chains, sls, sbs)]
        wbs = [_bf(w) for w in ws]
        outs = [q + _nn(_bf(p_ref[h, bb]), wb) for (bb, h), q, wb in zip(chains, qs, wbs)]
        new_states = [egl_ref[h, bb][0:1, :] * sh + _tn(wb, _bf(kdec_ref[bb, :, sl]))
                      for (bb, h), sl, sh, wb in zip(chains, sls, states, wbs)]
        for (bb, h), sh, o, ns in zip(chains, states, outs, new_states):
            ss_ref[bb, h] = sh
            o_ref[bb, :, h * B_DIM:(h + 1) * B_DIM] = o
            st[bb * B_HEADS + h] = ns

    r3 = lambda a: a.reshape(b, s, B_WIDTH)
    act = pl.BlockSpec((b, CHUNK, B_WIDTH), lambda n: (0, n, 0))
    o, states = pl.pallas_call(
        body, name="dn_scan_fwd", grid=(nc,),
        in_specs=[act, act, act, act,
                  pl.BlockSpec((B_HEADS, b, CHUNK, CHUNK), lambda n: (0, 0, n, 0)),
                  pl.BlockSpec((B_HEADS, b, 8, 128), lambda n: (0, 0, n, 0))],
        out_specs=[act, pl.BlockSpec((b, None, B_HEADS, B_DIM, B_DIM), lambda n: (0, n, 0, 0, 0))],
        out_shape=[jax.ShapeDtypeStruct((b, s, B_WIDTH), F32),
                   jax.ShapeDtypeStruct((b, nc, B_HEADS, B_DIM, B_DIM), F32)],
        scratch_shapes=[pltpu.VMEM((b * B_HEADS, B_DIM, B_DIM), F32)],
        compiler_params=_cp(("arbitrary",), VMEM_LIMIT),
    )(r3(u), r3(wk), r3(qg), r3(kdec), p.reshape(B_HEADS, b, s, CHUNK), egl.reshape(B_HEADS, b, s // 8, 128))
    return o.reshape(t, B_WIDTH), states


def dn_scan_bwd(u, wk, qg, kdec, p, egl, states, do, b, s):
    t = b * s
    nc = s // CHUNK

    def body(u_ref, wk_ref, qg_ref, kdec_ref, p_ref, egl_ref, ss_ref, do_ref,
             dw_ref, dwk_ref, dqg_ref, dkdec_ref, dp_ref, degl_ref, dst):
        @pl.when(pl.program_id(0) == 0)
        def _():
            dst[...] = jnp.zeros_like(dst)

        chains = [(bb, h) for bb in range(b) for h in range(B_HEADS)]
        dstates = [dst[bb * B_HEADS + h] for bb, h in chains]
        n8 = range(len(chains))
        sls = [slice(h * B_DIM, (h + 1) * B_DIM) for _, h in chains]
        shs = [ss_ref[bb, h] for bb, h in chains]
        sbs = [_bf(sh) for sh in shs]
        dsbs = [_bf(dsp) for dsp in dstates]
        wkbs = [_bf(wk_ref[bb, :, sl]) for (bb, _), sl in zip(chains, sls)]
        dobs = [_bf(do_ref[bb, :, sl]) for (bb, _), sl in zip(chains, sls)]
        t1 = [_nt(wkbs[i], sbs[i]) for i in n8]
        dwa = [_tn(_bf(p_ref[h, bb]), dobs[i]) for i, (bb, h) in enumerate(chains)]
        dwb_ = [_nt(_bf(kdec_ref[bb, :, sls[i]]), dsbs[i]) for i, (bb, _) in enumerate(chains)]
        dqgs = [_nn(dobs[i], sbs[i]) for i in n8]
        dsq = [_tn(dobs[i], _bf(qg_ref[bb, :, sls[i]])) for i, (bb, _) in enumerate(chains)]
        wbs = [_bf(u_ref[bb, :, sls[i]] - t1[i]) for i, (bb, _) in enumerate(chains)]
        dws = [dwa[i] + dwb_[i] for i in n8]
        dwbs = [_bf(dw) for dw in dws]
        dwks = [-_nn(dwbs[i], sbs[i]) for i in n8]
        dkdecs = [_nn(wbs[i], dsbs[i]) for i in n8]
        dpms = [_nt(dobs[i], wbs[i]) for i in n8]
        dsw = [_tn(dwbs[i], wkbs[i]) for i in n8]
        tots = [jnp.sum(jnp.sum(shs[i] * dstates[i], axis=1, keepdims=True), axis=0, keepdims=True) for i in n8]
        new_dss = [egl_ref[h, bb][0:1, :] * dstates[i] + dsq[i] - dsw[i] for i, (bb, h) in enumerate(chains)]
        results = [(dws[i], dqgs[i], dwks[i], dkdecs[i], dpms[i], tots[i], new_dss[i]) for i in n8]
        for (bb, h), (dw, dqg, dwk, dkdec, dpm, tot, new_ds) in zip(chains, results):
            sl = slice(h * B_DIM, (h + 1) * B_DIM)
            dw_ref[bb, :, sl] = dw
            dqg_ref[bb, :, sl] = dqg
            dwk_ref[bb, :, sl] = dwk
            dkdec_ref[bb, :, sl] = dkdec
            dp_ref[h, bb] = dpm
            degl_ref[h, bb] = jnp.broadcast_to(tot, (8, 128))
            dst[bb * B_HEADS + h] = new_ds

    r3 = lambda a: a.reshape(b, s, B_WIDTH)
    act = pl.BlockSpec((b, CHUNK, B_WIDTH), lambda n: (0, nc - 1 - n, 0))
    pspec = pl.BlockSpec((B_HEADS, b, CHUNK, CHUNK), lambda n: (0, 0, nc - 1 - n, 0))
    espec = pl.BlockSpec((B_HEADS, b, 8, 128), lambda n: (0, 0, nc - 1 - n, 0))
    outs = pl.pallas_call(
        body, name="dn_scan_bwd", grid=(nc,),
        in_specs=[act, act, act, act, pspec, espec,
                  pl.BlockSpec((b, None, B_HEADS, B_DIM, B_DIM), lambda n: (0, nc - 1 - n, 0, 0, 0)),
                  act],
        out_specs=[act, act, act, act, pspec, espec],
        out_shape=[jax.ShapeDtypeStruct((b, s, B_WIDTH), F32)] * 4
        + [jax.ShapeDtypeStruct((B_HEADS, b, s, CHUNK), F32),
           jax.ShapeDtypeStruct((B_HEADS, b, s // 8, 128), F32)],
        scratch_shapes=[pltpu.VMEM((b * B_HEADS, B_DIM, B_DIM), F32)],
        compiler_params=_cp(("arbitrary",), VMEM_LIMIT),
    )(r3(u), r3(wk), r3(qg), r3(kdec), p.reshape(B_HEADS, b, s, CHUNK), egl.reshape(B_HEADS, b, s // 8, 128),
      states, r3(do))
    return (*[a.reshape(t, B_WIDTH) for a in outs[:4]], outs[4].reshape(B_HEADS, t, CHUNK),
            outs[5].reshape(B_HEADS, t // 8, 128))


def dn_post_bwd(c, proj, al_row, dtb_row, tmat, dw, dwk, dqg, dkdec, dp, degl, b, s, ncb=16):
    t = b * s
    r = ncb * CHUNK
    nblk = t // r
    bd_blk = 0

    def body(cq_ref, ck_ref, cv_ref, bd_ref, al_ref, dtb_ref, tm_ref, dw_ref, dwk_ref, dqg_ref, dkdec_ref, dp_ref,
             degl_ref, dc_ref, dbd_ref, dal_ref, ddtb_ref):
        h = pl.program_id(1)

        @pl.when((pl.program_id(0) == 0) & (h == 0))
        def _():
            dal_ref[...] = jnp.zeros_like(dal_ref)
            ddtb_ref[...] = jnp.zeros_like(ddtb_ref)

        m = _dn_chunk_math(cq_ref[...], ck_ref[...], cv_ref[...], bd_ref[...].astype(F32), al_ref[...], dtb_ref[...], h, ncb,
                           tm=tm_ref[...].reshape(ncb, CHUNK, CHUNK))
        eye, low, strict = m["eye"], m["low"], m["strict"]
        eyef = eye.astype(F32)

        def c3(a):
            return a.reshape(ncb, CHUNK, a.shape[-1])

        du, dwkv, dqg, dkdec = c3(dw_ref[...]), c3(dwk_ref[...]), c3(dqg_ref[...]), c3(dkdec_ref[...])
        dpm = jnp.where(low, c3(dp_ref[...]), 0.0)
        degl = degl_ref[...].reshape(ncb, 8, 128)[:, 0:1, 0:1]
        beta, gam, kn, qn, v = m["beta"], m["gam"], m["kn"], m["qn"], m["v"]
        dm, kd, a, p = m["dm"], m["kd"], m["a"], m["p"]
        knb, qnb = _bf(kn), _bf(qn)

        eyeb = _bf(eyef)
        th, tl = _split(m["tm"])
        tts = (_bf(_bnt(eyeb, th)), _bf(_bnt(eyeb, tl)))
        xy = _bnn3(tts, jnp.concatenate([du, dwkv], axis=2))
        x, y = xy[:, :, :B_DIM], xy[:, :, B_DIM:]
        da = -jnp.where(strict, _bnt(_bf(x), _bf(m["u"])) + _bnt(_bf(y), _bf(m["wk"])), 0.0)
        dv = beta * x
        sy = jnp.sum(y * kn, axis=2, keepdims=True)
        dbeta = jnp.sum(x * v, axis=2, keepdims=True) + gam * sy + jnp.sum(da * kd, axis=2, keepdims=True)
        dgam = beta * sy + jnp.sum(dqg * qn, axis=2, keepdims=True)
        dkk = da * beta * dm
        dqk = dpm * dm
        dkkb, dqkb = _bf(dkk), _bf(dqk)
        dkn = ((beta * gam) * y + _bnn(dkkb, knb) + _bnn(_bf(_bnt(eyeb, dkkb)), knb)
               + _bnn(_bf(_bnt(eyeb, dqkb)), qnb) + dkdec * m["edec"])
        dqn = gam * dqg + _bnn(dqkb, knb)
        mm = da * a + dpm * p
        ek = jnp.sum(dkdec * m["kdec"], axis=2, keepdims=True)
        dgc = (jnp.sum(mm, axis=2, keepdims=True) - _col_of_row(jnp.sum(mm, axis=1, keepdims=True), eye)
               + dgam * gam - ek)
        dgl = jnp.sum(ek, axis=1, keepdims=True) + degl * m["egl"]
        i, _ = _chunk_masks(ncb)
        dgc = dgc + jnp.where(i[:, :, 0:1] == CHUNK - 1, dgl, 0.0)
        upper = (i <= _chunk_masks(ncb)[1]).astype(BF16)
        dg = _bnn_exact(upper, jnp.broadcast_to(dgc, (ncb, CHUNK, CHUNK)))[:, :, 0:1]

        nq = m["nq"]
        dnq = dqn * (B_DIM ** -0.5)
        dcq = m["rq"] * (dnq - nq * jnp.sum(nq * dnq, axis=2, keepdims=True))
        dck = m["rk"] * (dkn - kn * jnp.sum(kn * dkn, axis=2, keepdims=True))
        dc_ref[0] = dcq.reshape(r, B_DIM)
        dc_ref[1] = dck.reshape(r, B_DIM)
        dc_ref[2] = dv.reshape(r, B_DIM)

        dbraw = (dbeta * beta * (1.0 - beta)).reshape(r, 1)
        sgm = _sigmoid(m["sp_arg"])
        ddraw3 = dg * (-m["ea"]) * sgm
        ddraw = ddraw3.reshape(r, 1)
        lane = lax.broadcasted_iota(jnp.int32, (r, 128), 1)
        contrib = jnp.where(lane == h, dbraw, 0.0) + jnp.where(lane == B_HEADS + h, ddraw, 0.0)

        @pl.when(h == 0)
        def _():
            dbd_ref[...] = contrib

        @pl.when(h != 0)
        def _():
            dbd_ref[...] += contrib

        lane8 = lax.broadcasted_iota(jnp.int32, (8, 128), 1)
        tot_al = jnp.sum(jnp.sum(dg * m["g"], axis=1, keepdims=True), axis=0, keepdims=True).reshape(1, 1)
        tot_dtb = jnp.sum(jnp.sum(ddraw3, axis=1, keepdims=True), axis=0, keepdims=True).reshape(1, 1)
        dal_ref[...] += jnp.where(lane8 == h, tot_al, 0.0)
        ddtb_ref[...] += jnp.where(lane8 == h, tot_dtb, 0.0)

    col = lambda k: pl.BlockSpec((r, 128), lambda i, h: (i, k * B_HEADS + h))
    hcol = pl.BlockSpec((r, 128), lambda i, h: (i, h))
    small = pl.BlockSpec((1, 128), lambda i, h: (0, 0))
    acc = pl.BlockSpec((8, 128), lambda i, h: (0, 0))
    return pl.pallas_call(
        body, name="dn_post_bwd", grid=(nblk, B_HEADS),
        in_specs=[col(0), col(1), col(2), pl.BlockSpec((r, 128), lambda i, h: (i, bd_blk)), small, small,
                  pl.BlockSpec((None, r, CHUNK), lambda i, h: (h, i, 0)),
                  hcol, hcol, hcol, hcol,
                  pl.BlockSpec((None, r, CHUNK), lambda i, h: (h, i, 0)),
                  pl.BlockSpec((None, ncb * 8, 128), lambda i, h: (h, i, 0))],
        out_specs=[pl.BlockSpec((3, r, 128), lambda i, h: (0, i, h)),
                   pl.BlockSpec((r, 128), lambda i, h: (i, 0)), acc, acc],
        out_shape=[jax.ShapeDtypeStruct((3, t, B_WIDTH), F32), jax.ShapeDtypeStruct((t, 128), F32),
                   jax.ShapeDtypeStruct((8, 128), F32), jax.ShapeDtypeStruct((8, 128), F32)],
        compiler_params=_cp(("arbitrary", "arbitrary"), VMEM_LIMIT),
    )(c, c, c, proj, al_row, dtb_row, tmat, dw, dwk, dqg, dkdec, dp, degl)


def make_bias_band(rel_bias):
    tail = bias_tail(jnp.pad(rel_bias, ((0, 0), (0, 384 - N_REL))))
    far = jnp.broadcast_to(rel_bias[:, 2 * REL_CLIP][:, None, None], (A_HEADS, CHUNK, BAND - TAIL))
    band = jnp.concatenate([far, jnp.transpose(tail, (1, 0, 2))], axis=2)
    off = jnp.full((A_HEADS, CHUNK, CHUNK), -1e30, F32)
    both = jnp.stack([jnp.concatenate([band, off], axis=2), jnp.concatenate([off, band], axis=2)], axis=1)
    return both.reshape(4, 4 * CHUNK, WIN)


def bias_band_grad(dbt, dbf):
    t5 = dbt.reshape(A_HEADS, 2, CHUNK, 256)
    tail = t5[:, 0, :, :TAIL] + t5[:, 1, :, CHUNK:]
    far = dbf.reshape(A_HEADS, 2, CHUNK, 128).sum(axis=1) + jnp.pad(t5[:, 1, :, :CHUNK], ((0, 0), (0, 0), (0, CHUNK)))
    return bias_grad(jnp.transpose(tail, (1, 0, 2)), far)[:, :N_REL]


def _rms(x):
    r = lax.rsqrt(jnp.mean(x * x, axis=-1, keepdims=True) + EPS)
    return r, x * r


def _rms_bwd(dh, g, r, n):
    dn = dh * g
    return r * (dn - n * jnp.mean(dn * n, axis=-1, keepdims=True)), dh * n


def _gated_onorm(o, z, w_on):
    parts = []
    for h in range(B_HEADS):
        sl = slice(h * B_DIM, (h + 1) * B_DIM)
        r, n = _rms(o[:, sl])
        parts.append((r, n))
    r4 = [p[0] for p in parts]
    n4 = jnp.concatenate([p[1] for p in parts], axis=1)
    w4 = jnp.concatenate([w_on] * B_HEADS, axis=1)
    sz = _sigmoid(z)
    silu = z * sz
    return n4 * w4 * silu, r4, n4, w4, sz, silu


def mid_fwd(x, y_a, o_b, proj, w_on, wa, wb, w_out, tm=256):
    t = x.shape[0]
    tm = min(tm, t)

    def body(x_ref, ya_ref, ob_ref, z_ref, ga_ref, gb_ref, won_ref, wa_ref, wb_ref, wo_ref, x1_ref, mg_ref):
        yb = _gated_onorm(ob_ref[...], z_ref[...].astype(F32), won_ref[...])[0]
        ua = _nn(_bf(ya_ref[...]), wa_ref[...])
        ub = _nn(_bf(yb), wb_ref[...])
        merged = _sigmoid(ga_ref[...].astype(F32)) * ua + _sigmoid(gb_ref[...].astype(F32)) * ub
        mb = _bf(merged)
        mg_ref[...] = mb
        x1_ref[...] = x_ref[...] + _nn(mb, wo_ref[...])

    rowd = pl.BlockSpec((tm, D_MODEL), lambda i: (i, 0))
    row5 = pl.BlockSpec((tm, 512), lambda i: (i, 0))
    full = lambda a: pl.BlockSpec(a.shape, lambda i: (0,) * a.ndim)
    return pl.pallas_call(
        body, name="mid_fwd", grid=(t // tm,),
        in_specs=[rowd, row5, row5,
                  pl.BlockSpec((tm, 512), lambda i: (i, P_Z // 512)),
                  pl.BlockSpec((tm, D_MODEL), lambda i: (i, 0)),
                  pl.BlockSpec((tm, D_MODEL), lambda i: (i, 1)),
                  full(w_on), full(wa), full(wb), full(w_out)],
        out_specs=[rowd, rowd],
        out_shape=[jax.ShapeDtypeStruct((t, D_MODEL), F32), jax.ShapeDtypeStruct((t, D_MODEL), BF16)],
        compiler_params=_cp(("parallel",), VMEM_LIMIT),
    )(x, y_a, o_b, proj, proj, proj, w_on, wa, wb, w_out)


def mid_bwd(dx1, merged, y_a, o_b, proj, w_on, wa, wb, w_out, tm=256):
    t = dx1.shape[0]
    tm = min(tm, t)

    def body(dx1_ref, mg_ref, ya_ref, ob_ref, z_ref, ga_ref, gb_ref, won_ref, wa_ref, wb_ref, wo_ref,
             dya_ref, dob_ref, dz_ref, dg_ref, dwo_ref, dwa_ref, dwb_ref, dwon_ref):
        @pl.when(pl.program_id(0) == 0)
        def _():
            dwo_ref[...] = jnp.zeros_like(dwo_ref)
            dwa_ref[...] = jnp.zeros_like(dwa_ref)
            dwb_ref[...] = jnp.zeros_like(dwb_ref)
            dwon_ref[...] = jnp.zeros_like(dwon_ref)

        dx1b = _bf(dx1_ref[...])
        dmerged = _nt(dx1b, wo_ref[...])
        dwo_ref[...] += _tn(mg_ref[...], dx1b)
        o = ob_ref[...]
        z = z_ref[...].astype(F32)
        yb, r4, n4, w4, sz, silu = _gated_onorm(o, z, won_ref[...])
        yab, ybb = _bf(ya_ref[...]), _bf(yb)
        ua = _nn(yab, wa_ref[...])
        ub = _nn(ybb, wb_ref[...])
        sa, sb = _sigmoid(ga_ref[...].astype(F32)), _sigmoid(gb_ref[...].astype(F32))
        dua, dub = _bf(dmerged * sa), _bf(dmerged * sb)
        dg_ref[:, 0:D_MODEL] = _bf(dmerged * ua * sa * (1.0 - sa))
        dg_ref[:, D_MODEL:2 * D_MODEL] = _bf(dmerged * ub * sb * (1.0 - sb))
        dwa_ref[...] += _tn(yab, dua)
        dwb_ref[...] += _tn(ybb, dub)
        dya_ref[...] = _nt(dua, wa_ref[...])
        dyb = _nt(dub, wb_ref[...])
        dz_ref[...] = _bf(dyb * (n4 * w4) * (sz * (1.0 + z * (1.0 - sz))))
        dnw = dyb * silu
        dwon = jnp.zeros((1, B_DIM), F32)
        for h in range(B_HEADS):
            sl = slice(h * B_DIM, (h + 1) * B_DIM)
            dxh, dgh = _rms_bwd(dnw[:, sl], won_ref[...], r4[h], n4[:, sl])
            dob_ref[:, sl] = dxh
            dwon = dwon + jnp.sum(dgh, axis=0, keepdims=True)
        dwon_ref[...] += jnp.broadcast_to(dwon, (8, B_DIM))

    rowd = pl.BlockSpec((tm, D_MODEL), lambda i: (i, 0))
    row5 = pl.BlockSpec((tm, 512), lambda i: (i, 0))
    full = lambda a: pl.BlockSpec(a.shape, lambda i: (0,) * a.ndim)
    fixed = lambda shp: pl.BlockSpec(shp, lambda i: (0,) * len(shp))
    return pl.pallas_call(
        body, name="mid_bwd", grid=(t // tm,),
        in_specs=[rowd, rowd, row5, row5,
                  pl.BlockSpec((tm, 512), lambda i: (i, P_Z // 512)),
                  pl.BlockSpec((tm, D_MODEL), lambda i: (i, 0)),
                  pl.BlockSpec((tm, D_MODEL), lambda i: (i, 1)),
                  full(w_on), full(wa), full(wb), full(w_out)],
        out_specs=[row5, row5, row5, pl.BlockSpec((tm, 2 * D_MODEL), lambda i: (i, 0)),
                   fixed((D_MODEL, D_MODEL)), fixed((A_WIDTH, D_MODEL)), fixed((B_WIDTH, D_MODEL)),
                   fixed((8, B_DIM))],
        out_shape=[jax.ShapeDtypeStruct((t, 512), F32), jax.ShapeDtypeStruct((t, 512), F32),
                   jax.ShapeDtypeStruct((t, 512), BF16), jax.ShapeDtypeStruct((t, 2 * D_MODEL), BF16),
           jax.ShapeDtypeStruct((D_MODEL, D_MODEL), F32), jax.ShapeDtypeStruct((A_WIDTH, D_MODEL), F32),
           jax.ShapeDtypeStruct((B_WIDTH, D_MODEL), F32), jax.ShapeDtypeStruct((8, B_DIM), F32)],
        compiler_params=_cp(("arbitrary",), VMEM_LIMIT),
    )(dx1, merged, y_a, o_b, proj, proj, proj, w_on, wa, wb, w_out)


FFN_TF = 1408


def ffn_up(x1, g, w_gu, tm=512, tf=FFN_TF):
    t = x1.shape[0]
    tm = min(tm, t)
    nf = D_FF // tf

    def body(x_ref, g_ref, wg_ref, wu_ref, gate_ref, up_ref, act_ref, h_ref):
        @pl.when(pl.program_id(1) == 0)
        def _():
            r, n = _rms(x_ref[...])
            h_ref[...] = _bf(n * g_ref[...])

        hb = h_ref[...]
        gate = _nn(hb, wg_ref[...])
        up = _nn(hb, wu_ref[...])
        gate_ref[...] = _bf(gate)
        up_ref[...] = _bf(up)
        act_ref[...] = _bf(gate * _sigmoid(gate) * up)

    ff = pl.BlockSpec((tm, tf), lambda i, j: (i, j))
    return pl.pallas_call(
        body, name="ffn_up", grid=(t // tm, nf),
        in_specs=[pl.BlockSpec((tm, D_MODEL), lambda i, j: (i, 0)),
                  pl.BlockSpec((1, D_MODEL), lambda i, j: (0, 0)),
                  pl.BlockSpec((D_MODEL, tf), lambda i, j: (0, j)),
                  pl.BlockSpec((D_MODEL, tf), lambda i, j: (0, nf + j))],
        out_specs=[ff, ff, ff, pl.BlockSpec((tm, D_MODEL), lambda i, j: (i, 0))],
        out_shape=[jax.ShapeDtypeStruct((t, D_FF), BF16)] * 3 + [jax.ShapeDtypeStruct((t, D_MODEL), BF16)],
        compiler_params=_cp(("parallel", "arbitrary"), VMEM_LIMIT),
    )(x1, g, w_gu, w_gu)


def matmul_residual(a, w, res, name, tm=512, tk=FFN_TF):
    t, k = a.shape
    n = w.shape[1]
    tm = min(tm, t)

    def body(a_ref, w_ref, r_ref, o_ref):
        @pl.when(pl.program_id(1) == 0)
        def _():
            o_ref[...] = r_ref[...]

        o_ref[...] += _nn(a_ref[...], w_ref[...])

    return pl.pallas_call(
        body, name=name, grid=(t // tm, k // tk),
        in_specs=[pl.BlockSpec((tm, tk), lambda i, j: (i, j)),
                  pl.BlockSpec((tk, n), lambda i, j: (j, 0)),
                  pl.BlockSpec((tm, n), lambda i, j: (i, 0))],
        out_specs=pl.BlockSpec((tm, n), lambda i, j: (i, 0)),
        out_shape=jax.ShapeDtypeStruct((t, n), F32),
        compiler_params=_cp(("parallel", "arbitrary"), VMEM_LIMIT),
    )(a, w, res)


def ffn_act_bwd(dx2, gate, up, w_down, tm=512, tf=FFN_TF):
    t = dx2.shape[0]
    tm = min(tm, t)

    def body(dx2_ref, gate_ref, up_ref, wd_ref, dgate_ref, dup_ref, dx2b_ref):
        @pl.when(pl.program_id(1) == 0)
        def _():
            dx2b_ref[...] = _bf(dx2_ref[...])

        dact = _nt(dx2b_ref[...], wd_ref[...])
        gt, upv = gate_ref[...].astype(F32), up_ref[...].astype(F32)
        sg = _sigmoid(gt)
        dgate_ref[...] = _bf(dact * upv * (sg * (1.0 + gt * (1.0 - sg))))
        dup_ref[...] = _bf(dact * (gt * sg))

    ff = pl.BlockSpec((tm, tf), lambda i, j: (i, j))
    return pl.pallas_call(
        body, name="ffn_act_bwd", grid=(t // tm, D_FF // tf),
        in_specs=[pl.BlockSpec((tm, D_MODEL), lambda i, j: (i, 0)), ff, ff,
                  pl.BlockSpec((tf, D_MODEL), lambda i, j: (j, 0))],
        out_specs=[ff, ff],
        out_shape=[jax.ShapeDtypeStruct((t, D_FF), BF16)] * 2,
        scratch_shapes=[pltpu.VMEM((tm, D_MODEL), BF16)],
        compiler_params=_cp(("parallel", "arbitrary"), VMEM_LIMIT),
    )(dx2, gate, up, w_down)


def tail_fwd_bwd(x2, p, target, g_ple, g_final, w_pg, w_pp, tm=256):
    t = x2.shape[0]
    tm = min(tm, t)

    def body(x_ref, p_ref, t_ref, gp_ref, gf_ref, wpg_ref, wpp_ref,
             dx_ref, dwpg_ref, dwpp_ref, dgp_ref, dgf_ref, loss_ref):
        @pl.when(pl.program_id(0) == 0)
        def _():
            dwpg_ref[...] = jnp.zeros_like(dwpg_ref)
            dwpp_ref[...] = jnp.zeros_like(dwpp_ref)
            dgp_ref[...] = jnp.zeros_like(dgp_ref)
            dgf_ref[...] = jnp.zeros_like(dgf_ref)
            loss_ref[...] = jnp.zeros_like(loss_ref)

        x2v = x_ref[...]
        gp, gf = gp_ref[...], gf_ref[...]
        r3, n3 = _rms(x2v)
        h3b = _bf(n3 * gp)
        pb = _bf(p_ref[...])
        pg = _sigmoid(_nn(h3b, wpg_ref[...]))
        pp = _nn(pb, wpp_ref[...])
        x3 = x2v + pg * pp
        r4, n4 = _rms(x3)
        err = n4 * gf - t_ref[...]
        part = 0.5 * jnp.sum(jnp.sum(err * err, axis=1, keepdims=True), axis=0, keepdims=True) / D_MODEL
        loss_ref[...] += jnp.broadcast_to(part, (8, 128))
        dy = err * (1.0 / D_MODEL)
        dx3, dgf = _rms_bwd(dy, gf, r4, n4)
        dgf_ref[...] += jnp.broadcast_to(jnp.sum(dgf, axis=0, keepdims=True), (8, D_MODEL))
        dzp = _bf(dx3 * pp * pg * (1.0 - pg))
        dpp = _bf(dx3 * pg)
        dwpg_ref[...] += _tn(h3b, dzp)
        dwpp_ref[...] += _tn(pb, dpp)
        dh3 = _nt(dzp, wpg_ref[...])
        dx, dgp = _rms_bwd(dh3, gp, r3, n3)
        dgp_ref[...] += jnp.broadcast_to(jnp.sum(dgp, axis=0, keepdims=True), (8, D_MODEL))
        dx_ref[...] = dx3 + dx

    rowd = pl.BlockSpec((tm, D_MODEL), lambda i: (i, 0))
    fixed = lambda shp: pl.BlockSpec(shp, lambda i: (0,) * len(shp))
    return pl.pallas_call(
        body, name="tail_fwd_bwd", grid=(t // tm,),
        in_specs=[rowd, pl.BlockSpec((tm, PLE_DIM), lambda i: (i, 0)), rowd,
                  fixed((1, D_MODEL)), fixed((1, D_MODEL)), fixed((D_MODEL, D_MODEL)), fixed((PLE_DIM, D_MODEL))],
        out_specs=[rowd, fixed((D_MODEL, D_MODEL)), fixed((PLE_DIM, D_MODEL)),
                   fixed((8, D_MODEL)), fixed((8, D_MODEL)), fixed((8, 128))],
        out_shape=[jax.ShapeDtypeStruct((t, D_MODEL), F32), jax.ShapeDtypeStruct((D_MODEL, D_MODEL), F32),
                   jax.ShapeDtypeStruct((PLE_DIM, D_MODEL), F32), jax.ShapeDtypeStruct((8, D_MODEL), F32),
                   jax.ShapeDtypeStruct((8, D_MODEL), F32), jax.ShapeDtypeStruct((8, 128), F32)],
        compiler_params=_cp(("arbitrary",), VMEM_LIMIT),
    )(x2, p, target, g_ple, g_final, w_pg, w_pp)


def in_proj_bwd(pieces, weights, x, dx1, g, name="in_proj_bwd", tm=256):
    t = x.shape[0]
    tm = min(tm, t)
    k = len(pieces)
    assert all(c0 % wd == 0 and w0 % wd == 0 for (_, c0, wd), (_, w0) in zip(pieces, weights))

    def body(*refs):
        p_refs, w_refs = refs[:k], refs[k:2 * k]
        x_ref, dx1_ref, g_ref, dx_ref, dg_ref = refs[2 * k:]

        @pl.when(pl.program_id(0) == 0)
        def _():
            dg_ref[...] = jnp.zeros_like(dg_ref)

        dh = _nt(_bf(p_refs[0][...]), w_refs[0][...])
        for pr, wr in zip(p_refs[1:], w_refs[1:]):
            dh = dh + _nt(_bf(pr[...]), wr[...])
        r, n = _rms(x_ref[...])
        dx, dgc = _rms_bwd(dh, g_ref[...], r, n)
        dx_ref[...] = dx1_ref[...] + dx
        dg_ref[...] += jnp.broadcast_to(jnp.sum(dgc, axis=0, keepdims=True), (8, D_MODEL))

    rowd = pl.BlockSpec((tm, D_MODEL), lambda i: (i, 0))
    return pl.pallas_call(
        body, name=name, grid=(t // tm,),
        in_specs=[pl.BlockSpec((tm, wd), functools.partial(lambda i, cb: (i, cb), cb=c0 // wd))
                  for _, c0, wd in pieces]
        + [pl.BlockSpec((w.shape[0], wd), functools.partial(lambda i, cb: (0, cb), cb=w0 // wd))
           for (w, w0), (_, _, wd) in zip(weights, pieces)]
        + [rowd, rowd, pl.BlockSpec((1, D_MODEL), lambda i: (0, 0))],
        out_specs=[rowd, pl.BlockSpec((8, D_MODEL), lambda i: (0, 0))],
        out_shape=[jax.ShapeDtypeStruct((t, D_MODEL), F32), jax.ShapeDtypeStruct((8, D_MODEL), F32)],
        compiler_params=_cp(("arbitrary",), VMEM_LIMIT),
    )(*[a for a, _, _ in pieces], *[w for w, _ in weights], x, dx1, g)


def adamw(w, g, m, v, name, rows_cap=256, dep=None):
    lead = w.shape[:-2]
    r, c = w.shape[-2:]
    tr = r
    for cand in range(8, min(r, rows_cap) + 1, 8):
        if r % cand == 0:
            tr = cand

    def body(w_ref, g_ref, m_ref, v_ref, *rest):
        d_ref, mo_ref, vo_ref = rest[-3:]
        gv = g_ref[...]
        mn = ADAM_B1 * m_ref[...] + (1.0 - ADAM_B1) * gv
        vn = ADAM_B2 * v_ref[...] + (1.0 - ADAM_B2) * (gv * gv)
        m_hat = mn / (1.0 - ADAM_B1 ** ADAM_STEP)
        v_hat = vn / (1.0 - ADAM_B2 ** ADAM_STEP)
        d_ref[...] = -ADAM_LR * (m_hat / (jnp.sqrt(v_hat) + ADAM_EPS) + ADAM_WD * w_ref[...])
        mo_ref[...] = mn
        vo_ref[...] = vn

    spec = pl.BlockSpec((None,) * len(lead) + (tr, c), lambda i: (0,) * len(lead) + (i, 0))
    extra = [] if dep is None else [dep]
    return pl.pallas_call(
        body, name=name, grid=(r // tr,),
        in_specs=[spec] * 4 + [pl.BlockSpec((8, 128), lambda i: (0, 0))] * len(extra), out_specs=[spec] * 3,
        out_shape=[jax.ShapeDtypeStruct(w.shape, F32)] * 3,
        compiler_params=_cp(("parallel",), VMEM_LIMIT),
    )(w, g.reshape(w.shape), m, v, *extra)


def _w_in_shards(dwp):
    cs = D_IN // N_CHIPS
    regions = ((0, SPLIT_Z, P_QA), (SPLIT_Z, SPLIT_Z + 8, P_BD - SPLIT_Z), (SPLIT_Z + 8, D_IN, -(SPLIT_Z + 8)))

    def original(lo, hi):
        parts = [dwp[:, max(lo, a) + off:min(hi, e) + off] for a, e, off in regions if max(lo, a) < min(hi, e)]
        return parts[0] if len(parts) == 1 else jnp.concatenate(parts, axis=1)

    return jnp.stack([original(s * cs, (s + 1) * cs) for s in range(N_CHIPS)])


class Standalone:
    def __init__(self, later_weights):
        self.later_weights = later_weights

    def begin(self, *a):
        return 0.0

    forward = exchange = join = begin

    def finish(self, after):
        return self.later_weights


def local_step(x3d, p3d, target3d, g4, small, later, early):
    b, s, _ = x3d.shape
    t = b * s
    x = x3d.reshape(t, D_MODEL)
    p = p3d.reshape(t, PLE_DIM)
    target = target3d.reshape(t, D_MODEL)
    cut = SPLIT_Z - 2 * (D_IN // N_CHIPS)
    w_inp = jnp.concatenate([g4[2][:, cut + 8:], g4[3], g4[0], g4[1], g4[2][:, :cut], g4[2][:, cut:cut + 8],
                             jnp.zeros((D_MODEL, 120), BF16)], axis=1)
    al_row = jnp.pad(small["a_log"].reshape(1, B_HEADS), ((0, 0), (0, 128 - B_HEADS)))
    dtb_row = jnp.pad(small["dt_bias"].reshape(1, B_HEADS), ((0, 0), (0, 128 - B_HEADS)))
    conv_w8 = jnp.pad(small["conv_w"].reshape(CONV_K, CONV_CH), ((0, 8 - CONV_K), (0, 0)))
    w_on = small["w_onorm"].reshape(1, B_DIM)
    g_mix, g_ffn = small["g_mix"].reshape(1, D_MODEL), small["g_ffn"].reshape(1, D_MODEL)
    g_ple, g_final = small["g_ple"].reshape(1, D_MODEL), small["g_final"].reshape(1, D_MODEL)
    bias_band = make_bias_band(small["rel_bias"].reshape(A_HEADS, N_REL))

    tok = later.begin()
    proj, h1, bd32 = rms_matmul(x, g_mix + tok, w_inp, "in_proj", tm=1024)
    y_a, lse = attn_fwd(proj, bias_band, b, s)
    tok = later.forward(lse)
    c = conv_fwd(proj, conv_w8 + tok, b, s)
    u, wk, qg, kdec, pm, egl, tmat = dn_prep(c, bd32, al_row, dtb_row, b, s)
    o_b, states = dn_scan_fwd(u, wk, qg, kdec, pm, egl, b, s)
    wts = later.finish(o_b)
    x1, merged = mid_fwd(x, y_a, o_b, proj, w_on, wts["w_branch_a"], wts["w_branch_b"], wts["w_out"])
    gate, up, act, h2 = ffn_up(x1, g_ffn, wts["w_gate_up"])
    x2 = matmul_residual(act, wts["w_down"], x1, "ffn_down")

    dx2, dw_pg, dw_pp, dg_ple, dg_final, loss = tail_fwd_bwd(
        x2, p, target, g_ple, g_final, wts["w_ple_gate"], wts["w_ple_proj"])
    dgate, dup = ffn_act_bwd(dx2, gate, up, wts["w_down"])
    w_gu = wts["w_gate_up"]
    dx1, dg_ffn = in_proj_bwd([(dgate, 0, D_FF), (dup, 0, D_FF)], [(w_gu, 0), (w_gu, D_FF)], x1, dx2, g_ffn,
                              name="ffn_in_bwd")
    dw_down = matmul_tn(act, dx2, "dw_down")
    dw_gu = matmul_tn(h2, dgate, "dw_gate", width=2 * D_FF)
    dw_gu = matmul_tn(h2, dup, "dw_up", into=dw_gu, col0=D_FF)
    dy_a, do_b, dz, dgates, dw_out, dwa, dwb, dw_on = mid_bwd(
        dx1, merged, y_a, o_b, proj, w_on, wts["w_branch_a"], wts["w_branch_b"], wts["w_out"])
    tok = early.begin(dict(w_branch_a=dwa, w_branch_b=dwb, w_out=dw_out, w_gate_up=dw_gu, w_down=dw_down,
                           w_ple_gate=dw_pg, w_ple_proj=dw_pp))
    ddw, ddwk, ddqg, ddkdec, ddp, ddegl = dn_scan_bwd(u, wk, qg, kdec, pm, egl + tok, states, do_b, b, s)
    tok = early.exchange(ddegl)
    dc3, dbd, dal, ddtb = dn_post_bwd(c, bd32, al_row + tok, dtb_row, tmat, ddw, ddwk, ddqg, ddkdec, ddp, ddegl, b, s)
    dconv, dconv_w = conv_bwd(proj, conv_w8, dc3, b, s)
    dqa, dka, dva, dbt, dbf = attn_bwd(proj, bias_band, y_a, lse, dy_a, b, s)
    tok = early.join(dqa)
    d_rel = bias_band_grad(dbt, dbf)

    pieces = [dgates, dqa, dka, dva, dconv, dz, dbd]
    bounds = [0, 2048, 2560, 3072, 3584, 5120, 5632, 5760]
    windows = [(dgates, 0, 2048), (dqa, 0, 512), (dka, 0, 512), (dva, 0, 512), (dconv, 0, 512), (dconv, 512, 512),
               (dconv, 1024, 512), (dz, 0, 512), (dbd, 0, 128)]
    w_cols = [0, P_QA, P_KA, P_VA, P_CONV, P_CONV + 512, P_CONV + 1024, P_Z, P_BD]
    dx, dg_mix = in_proj_bwd(windows, [(w_inp, c0) for c0 in w_cols], x, dx1, g_mix + tok)
    dwp = None
    for k, pc in enumerate(pieces):
        dwp = matmul_tn(h1, pc, "dw_in_%d" % k, into=dwp, col0=bounds[k], width=P_WIDTH)
    reduced_early = early.finish(dwp)
    dw_in = _w_in_shards(dwp)

    grads = dict(w_in=dw_in, w_branch_a=dwa, w_branch_b=dwb, w_out=dw_out, w_gate_up=dw_gu, w_down=dw_down,
                 w_ple_gate=dw_pg, w_ple_proj=dw_pp)
    small_grads = dict(g_mix=dg_mix[0], g_ffn=dg_ffn[0], g_ple=dg_ple[0], g_final=dg_final[0],
                       conv_w=dconv_w[:CONV_K].reshape(-1), rel_bias=d_rel.reshape(-1), w_onorm=dw_on[0],
                       a_log=dal[0, :B_HEADS], dt_bias=ddtb[0, :B_HEADS], loss=loss[0, :1])
    return dx.reshape(b, s, D_MODEL), grads, small_grads, reduced_early


BIG = (("w_in", (D_MODEL, D_IN), 1), ("w_branch_a", (A_WIDTH, D_MODEL), 1), ("w_branch_b", (B_WIDTH, D_MODEL), 1),
       ("w_out", (D_MODEL, D_MODEL), 0), ("w_gate_up", (D_MODEL, 2 * D_FF), 1), ("w_down", (D_FF, D_MODEL), 0),
       ("w_ple_gate", (D_MODEL, D_MODEL), 0), ("w_ple_proj", (PLE_DIM, D_MODEL), 1))
N_CHIPS = 4
FIRST_WEIGHTS = ("w_in",)
LATER_WEIGHTS = ("w_branch_a", "w_branch_b", "w_out", "w_gate_up", "w_down", "w_ple_gate", "w_ple_proj")
LATE_GRADS = ("w_in",)
EARLY_GRADS = ("w_branch_a", "w_branch_b", "w_out", "w_gate_up", "w_down", "w_ple_gate", "w_ple_proj")


def _items(names):
    return [it for it in BIG if it[0] in names]


def _shard_shape(shape, axis):
    return (shape[0] // N_CHIPS, shape[1]) if axis == 0 else (shape[0], shape[1] // N_CHIPS)


def _width_groups(names):
    groups = {}
    for n, shape, axis in _items(names):
        rs, cs = _shard_shape(shape, axis)
        groups.setdefault(cs, []).append((n, rs))
    return sorted(groups.items())


def grad_buffers(grads, names):
    info = {n: (shape, axis) for n, shape, axis in _items(names)}
    bufs = []
    for cs, members in _width_groups(names):
        segs = []
        for n, rs in members:
            g = grads[n].astype(BF16)
            if g.ndim == 2:
                g = (g.reshape(N_CHIPS, rs, cs) if info[n][1] == 0
                     else jnp.transpose(g.reshape(rs, N_CHIPS, cs), (1, 0, 2)))
            segs.append(g)
        bufs.append(segs[0] if len(segs) == 1 else jnp.concatenate(segs, axis=1))
    return bufs


def split_buffers(reduced, names):
    out = {}
    for (cs, members), buf in zip(_width_groups(names), reduced):
        r0 = 0
        for n, rs in members:
            out[n] = buf[r0:r0 + rs]
            r0 += rs
    return out


def _place():
    return lax.axis_index("x"), lax.axis_index("y"), lax.axis_index("c")


ANY = pl.BlockSpec(memory_space=pl.ANY)


def _gathered_shape(item):
    n, shape, _ = item
    return (N_CHIPS,) + _shard_shape(shape, 1) if n == "w_in" else shape


def _gather_block(o_ref, item, cx, cy, hf):
    n, shape, axis = item
    rs, cs = _shard_shape(shape, axis)
    hr = rs // 2
    ci = 2 * cx + cy
    if n == "w_in":
        return o_ref.at[ci, pl.ds(pl.multiple_of(hf * hr, 16), hr), :]
    if axis == 0:
        return o_ref.at[pl.ds(pl.multiple_of(ci * rs + hf * hr, 16), hr), :]
    return o_ref.at[pl.ds(pl.multiple_of(hf * hr, 16), hr), pl.ds(pl.multiple_of(ci * cs, 128), cs)]


def _own_half(w_ref, item, c):
    hr = _shard_shape(item[1], item[2])[0] // 2
    return w_ref.at[pl.ds(pl.multiple_of(c * hr, 16), hr), :]


def _gather_slot(o_ref, item, cx, cy):
    n, shape, axis = item
    rs, cs = _shard_shape(shape, axis)
    ci = 2 * cx + cy
    if n == "w_in":
        return o_ref.at[ci]
    if axis == 0:
        return o_ref.at[pl.ds(pl.multiple_of(ci * rs, 16), rs), :]
    return o_ref.at[:, pl.ds(pl.multiple_of(ci * cs, 128), cs)]


def _other_chips(x, y):
    return [(1 - x, y), (x, 1 - y), (1 - x, 1 - y)]


def allgather_weights(shards, names, chip):
    items = _items(names)
    nw = len(items)

    def body(*refs):
        w_refs, o_refs = refs[:nw], refs[nw:2 * nw]
        send_sems, recv_sems = refs[2 * nw:]
        x, y, c = _place()
        sibling = (x, y, 1 - c)
        chips = _other_chips(x, y)

        def copy(k, src, dst, to):
            return pltpu.make_async_remote_copy(src_ref=src, dst_ref=dst, send_sem=send_sems.at[k],
                                                recv_sem=recv_sems.at[k], device_id=to, device_id_type=MESH)

        def blk(i, cx, cy, hf):
            return _gather_block(o_refs[i], items[i], cx, cy, hf)

        def my_half(i):
            return _own_half(w_refs[i], items[i], c)

        def own(i):
            return _gather_slot(o_refs[i], items[i], x, y)

        first = [copy(7 * i + j, my_half(i), blk(i, x, y, c), (*chip_, c))
                 for i in range(nw) for j, chip_ in enumerate(chips)]
        first += [copy(7 * i + 6, w_refs[i], own(i), sibling) for i in range(nw)]
        for cp in first:
            cp.start()
        passed = []
        for i in range(nw):
            for j, chip_ in enumerate(chips):
                copy(7 * i + j, my_half(i), blk(i, *chip_, c), (*chip_, c)).wait_recv()
                fwd = copy(7 * i + 3 + j, blk(i, *chip_, c), blk(i, *chip_, c), sibling)
                fwd.start()
                passed.append(fwd)
        for i in range(nw):
            for j, chip_ in enumerate(chips):
                copy(7 * i + 3 + j, my_half(i), blk(i, *chip_, 1 - c), sibling).wait_recv()
            copy(7 * i + 6, w_refs[i], own(i), sibling).wait_recv()
        for cp in first + passed:
            cp.wait_send()

    outs = pl.pallas_call(
        body, name="allgather_weights",
        in_specs=[ANY] * nw, out_specs=[ANY] * nw,
        out_shape=[jax.ShapeDtypeStruct(_gathered_shape(it), BF16) for it in items],
        scratch_shapes=[pltpu.SemaphoreType.DMA((7 * nw,)), pltpu.SemaphoreType.DMA((7 * nw,))],
    )(*[shards[it[0]] for it in items])
    return {it[0]: o for it, o in zip(items, outs)}


HBM_SPEC = pl.BlockSpec(memory_space=pltpu.HBM)
SEM_SPEC = pl.BlockSpec(memory_space=pltpu.SEMAPHORE)
EFFECT = pltpu.SideEffectType.DATAFLOW_SIDE_EFFECTING


def _in_hbm(a):
    return pltpu.with_memory_space_constraint(a, pltpu.HBM)


def copies_start(name, bufs, ncopies, plan):
    nb = len(bufs)

    def body(*refs):
        in_refs, send_sems, recv_sems, token = refs[:nb], refs[nb], refs[nb + 1], refs[-1]
        for k, (src, dst, to) in enumerate(plan(in_refs)):
            pltpu.make_async_remote_copy(src_ref=src, dst_ref=dst, send_sem=send_sems.at[k],
                                         recv_sem=recv_sems.at[k], device_id=to, device_id_type=MESH).start()
        token[...] = jnp.zeros_like(token)

    outs = pl.pallas_call(
        body, name=name,
        in_specs=[HBM_SPEC] * nb,
        out_specs=(SEM_SPEC, SEM_SPEC, *[HBM_SPEC] * nb, pl.BlockSpec(memory_space=pltpu.VMEM)),
        out_shape=(pltpu.SemaphoreType.DMA((ncopies,)), pltpu.SemaphoreType.DMA((ncopies,)),
                   *[pltpu.HBM(b.shape, b.dtype) for b in bufs], jax.ShapeDtypeStruct((8, 128), F32)),
        input_output_aliases={i: 2 + i for i in range(nb)},
        compiler_params=pltpu.CompilerParams(has_side_effects=EFFECT),
    )(*[_in_hbm(b) for b in bufs])
    return outs[0], outs[1], list(outs[2:2 + nb]), outs[-1][0, 0]


def copies_wait(name, send_sems, recv_sems, bufs, after, plan):
    nb = len(bufs)

    def body(*refs):
        in_refs, s_sems, r_sems = refs[:nb], refs[nb], refs[nb + 1]
        for k, (src, dst, to) in enumerate(plan(in_refs)):
            cp = pltpu.make_async_remote_copy(src_ref=src, dst_ref=dst, send_sem=s_sems.at[k],
                                              recv_sem=r_sems.at[k], device_id=to, device_id_type=MESH)
            cp.wait_send()
            cp.wait_recv()

    return list(pl.pallas_call(
        body, name=name,
        in_specs=[HBM_SPEC] * nb + [SEM_SPEC, SEM_SPEC, ANY],
        out_specs=tuple([HBM_SPEC] * nb),
        out_shape=tuple(pltpu.HBM(b.shape, b.dtype) for b in bufs),
        input_output_aliases={i: i for i in range(nb)},
        compiler_params=pltpu.CompilerParams(has_side_effects=EFFECT),
    )(*bufs, send_sems, recv_sems, after))


def _landing(shape, dtype):
    return _in_hbm(lax.empty(shape, dtype))


class LaterWeights:
    def __init__(self, shards, chip):
        self.items = _items(LATER_WEIGHTS)
        self.shards, self.chip = shards, chip
        self.nw = len(self.items)

    def _ici_plan(self, refs):
        x, y, c = _place()
        w_refs, o_refs = refs[:self.nw], refs[self.nw:]
        plan = [(_own_half(w_refs[i], it, c), _gather_block(o_refs[i], it, x, y, c), (*chip_, c))
                for i, it in enumerate(self.items) for chip_ in _other_chips(x, y)]
        return plan + [(w_refs[i], _gather_slot(o_refs[i], it, x, y), (x, y, 1 - c))
                       for i, it in enumerate(self.items)]

    def _d2d_plan(self, refs):
        x, y, c = _place()
        return [(_gather_block(refs[i], it, *chip_, c), _gather_block(refs[i], it, *chip_, c), (x, y, 1 - c))
                for i, it in enumerate(self.items) for chip_ in _other_chips(x, y)]

    def _d2d_wait_plan(self, refs):
        x, y, c = _place()
        return [(_gather_block(refs[i], it, *chip_, c), _gather_block(refs[i], it, *chip_, 1 - c), (x, y, 1 - c))
                for i, it in enumerate(self.items) for chip_ in _other_chips(x, y)]

    def _ici_wait_plan(self, refs):
        x, y, c = _place()
        w_refs, o_refs = refs[:self.nw], refs[self.nw:]
        plan = [(_own_half(w_refs[i], it, c), _gather_block(o_refs[i], it, *chip_, c), (*chip_, c))
                for i, it in enumerate(self.items) for chip_ in _other_chips(x, y)]
        return plan + [(w_refs[i], _gather_slot(o_refs[i], it, x, y), (x, y, 1 - c))
                       for i, it in enumerate(self.items)]

    def begin(self):
        srcs = [self.shards[it[0]] for it in self.items]
        lands = [_landing(_gathered_shape(it), BF16) for it in self.items]
        self.s1, self.r1, self.b1, tok = copies_start("gather_ici_start", srcs + lands, 4 * self.nw, self._ici_plan)
        return tok

    def forward(self, after):
        b1 = copies_wait("gather_ici_wait", self.s1, self.r1, self.b1, after, self._ici_wait_plan)
        self.s2, self.r2, self.b2, tok = copies_start("gather_d2d_start", b1[self.nw:], 3 * self.nw, self._d2d_plan)
        return tok

    def finish(self, after):
        outs = copies_wait("gather_d2d_wait", self.s2, self.r2, self.b2, after, self._d2d_wait_plan)
        return {it[0]: o for it, o in zip(self.items, outs)}


def small_allreduce(v, name):
    r = v.shape[0]

    def body(v_ref, o_ref, buf, send_sems, recv_sems):
        x, y, c = _place()
        me = 4 * x + 2 * y + c
        buf[me] = v_ref[...]
        flips = [(fx, fy, fc) for fx in (0, 1) for fy in (0, 1) for fc in (0, 1)][1:]
        peers = [((1 - x) if fx else x, (1 - y) if fy else y, (1 - c) if fc else c) for fx, fy, fc in flips]

        def copy(k, slot, to):
            return pltpu.make_async_remote_copy(src_ref=v_ref, dst_ref=buf.at[slot], send_sem=send_sems.at[k],
                                                recv_sem=recv_sems.at[k], device_id=to, device_id_type=MESH)

        sends = [copy(k, me, peer) for k, peer in enumerate(peers)]
        for cp in sends:
            cp.start()
        for k, (px, py, pc) in enumerate(peers):
            copy(k, 4 * px + 2 * py + pc, (px, py, pc)).wait_recv()
        for cp in sends:
            cp.wait_send()
        acc = buf[0]
        for d in range(1, 8):
            acc = acc + buf[d]
        o_ref[...] = acc

    return pl.pallas_call(
        body, name=name,
        in_specs=[pl.BlockSpec(memory_space=pltpu.VMEM)], out_specs=pl.BlockSpec(memory_space=pltpu.VMEM),
        out_shape=jax.ShapeDtypeStruct((r, 128), F32),
        scratch_shapes=[pltpu.VMEM((8, r, 128), F32), pltpu.SemaphoreType.DMA((7,)), pltpu.SemaphoreType.DMA((7,))],
    )(v)


def add_halves(g, other, place):
    half, wd = other.shape[1:]
    tr = _tile_rows(half, wd)
    nblk = half // tr

    def body(pref, g0, g1, g2, g3, o0, o1, o2, o3, pf_ref, pb_ref):
        f = lambda r: r[...].astype(F32)
        pf_ref[...] = f(g0) + f(o0)
        pb_ref[0] = _bf(f(g1) + f(o1))
        pb_ref[1] = _bf(f(g2) + f(o2))
        pb_ref[2] = _bf(f(g3) + f(o3))

    gspec = lambda k: pl.BlockSpec((None, tr, wd), lambda i, pr: ((pr[0] + k) % N_CHIPS, pr[1] * nblk + i, 0))
    ospec = lambda k: pl.BlockSpec((None, tr, wd), lambda i, pr: ((pr[0] + k) % N_CHIPS, i, 0))
    return pl.pallas_call(
        body, name="add_halves",
        grid_spec=pltpu.PrefetchScalarGridSpec(
            num_scalar_prefetch=1, grid=(nblk,),
            in_specs=[gspec(0), gspec(1), gspec(2), gspec(3), ospec(0), ospec(1), ospec(2), ospec(3)],
            out_specs=[pl.BlockSpec((tr, wd), lambda i, pr: (i, 0)),
                       pl.BlockSpec((3, tr, wd), lambda i, pr: (0, i, 0))]),
        out_shape=[jax.ShapeDtypeStruct((half, wd), F32), jax.ShapeDtypeStruct((3, half, wd), BF16)],
        compiler_params=_cp(("parallel",), VMEM_LIMIT),
    )(place, g, g, g, g, other, other, other, other)


def _tile_rows(n, width):
    best = 16
    for t in range(16, max(16, (384 * 1024) // width) + 1, 16):
        if n % t == 0:
            best = t
    assert n % best == 0
    return best


def add_partials(pf, got, place):
    half, wd = pf.shape
    tr = _tile_rows(half, wd)

    def body(pref, pf_ref, got_ref, o_ref):
        o_ref[...] = ((pf_ref[...] + got_ref[0].astype(F32)) + got_ref[1].astype(F32)) + got_ref[2].astype(F32)

    return pl.pallas_call(
        body, name="add_partials",
        grid_spec=pltpu.PrefetchScalarGridSpec(
            num_scalar_prefetch=1, grid=(half // tr,),
            in_specs=[pl.BlockSpec((tr, wd), lambda i, pr: (i, 0)),
                      pl.BlockSpec((3, tr, wd), lambda i, pr: (0, i, 0))],
            out_specs=pl.BlockSpec((None, tr, wd), lambda i, pr: (pr[1], i, 0))),
        out_shape=jax.ShapeDtypeStruct((2, half, wd), F32),
        compiler_params=_cp(("parallel",), VMEM_LIMIT),
    )(place, pf, got)


class GradReduce:
    def __init__(self, place, names, tag):
        self.place, self.names, self.tag = place, names, tag
        self.nb = len(_width_groups(names))

    def _swap_plan(self, refs):
        x, y, c = _place()
        plan = []
        for g_ref, o_ref in zip(refs[:self.nb], refs[self.nb:]):
            half = o_ref.shape[1]
            plan.append((g_ref.at[:, pl.ds(pl.multiple_of((1 - c) * half, 16), half), :], o_ref, (x, y, 1 - c)))
        return plan

    def _exchange_plan(self, refs):
        x, y, c = _place()
        me = 2 * x + y
        return [(p_ref.at[k - 1], o_ref.at[k - 1], (((me + k) % N_CHIPS) // 2, ((me + k) % N_CHIPS) % 2, c))
                for p_ref, o_ref in zip(refs[:self.nb], refs[self.nb:]) for k in range(1, N_CHIPS)]

    def _join_plan(self, refs):
        x, y, c = _place()
        return [(r.at[c], r.at[c], (x, y, 1 - c)) for r in refs]

    def _join_wait_plan(self, refs):
        x, y, c = _place()
        return [(r.at[c], r.at[1 - c], (x, y, 1 - c)) for r in refs]

    def begin(self, grads):
        gs = grad_buffers(grads, self.names)
        lands = [_landing((N_CHIPS, g.shape[1] // 2, g.shape[2]), BF16) for g in gs]
        self.s1, self.r1, self.b1, tok = copies_start(self.tag + "_swap_start", gs + lands, self.nb, self._swap_plan)
        return tok

    def exchange(self, after):
        b1 = copies_wait(self.tag + "_swap_wait", self.s1, self.r1, self.b1, after, self._swap_plan)
        sums = [add_halves(g, other, self.place) for g, other in zip(b1[:self.nb], b1[self.nb:])]
        self.pfs = [pf for pf, _ in sums]
        pbs = [pb for _, pb in sums]
        lands = [_landing(pb.shape, BF16) for pb in pbs]
        self.s2, self.r2, self.b2, tok = copies_start(self.tag + "_exchange_start", pbs + lands, 3 * self.nb,
                                                      self._exchange_plan)
        return tok

    def join(self, after):
        b2 = copies_wait(self.tag + "_exchange_wait", self.s2, self.r2, self.b2, after, self._exchange_plan)
        boths = [add_partials(pf, got, self.place) for pf, got in zip(self.pfs, b2[self.nb:])]
        self.s3, self.r3, self.b3, tok = copies_start(self.tag + "_join_start", boths, self.nb, self._join_plan)
        return tok

    def finish(self, after):
        boths = copies_wait(self.tag + "_join_wait", self.s3, self.r3, self.b3, after, self._join_wait_plan)
        return split_buffers([b.reshape(-1, b.shape[2]) for b in boths], self.names)


SMALL = (("g_mix", D_MODEL), ("g_ffn", D_MODEL), ("g_ple", D_MODEL), ("g_final", D_MODEL),
         ("conv_w", CONV_K * CONV_CH), ("rel_bias", A_HEADS * N_REL), ("w_onorm", B_DIM),
         ("a_log", B_HEADS), ("dt_bias", B_HEADS), ("loss", 1))


def _pad128(v):
    v = v.reshape(-1)
    return jnp.pad(v, (0, -v.shape[0] % 128))


def pack_small(d, names, rows):
    flat = jnp.concatenate([_pad128(d[n]) for n in names]).reshape(-1, 128)
    return jnp.pad(flat, ((0, rows - flat.shape[0]), (0, 0)))


def unpack_small(flat, names_sizes):
    out, r0 = {}, 0
    v = flat.reshape(-1)
    for n, size in names_sizes:
        out[n] = v[r0:r0 + size]
        r0 += -(-size // 128) * 128
    return out


def kernel(x, p, g_mix, w_in, conv_w, a_log, dt_bias, rel_bias, w_onorm, w_branch_a, w_branch_b, w_out, g_ffn, w_gate_up, w_down, g_ple, w_ple_gate, w_ple_proj, g_final, loss_target, m_g_mix, m_w_in, m_conv_w, m_a_log, m_dt_bias, m_rel_bias, m_w_onorm, m_w_branch_a, m_w_branch_b, m_w_out, m_g_ffn, m_w_gate_up, m_w_down, m_g_ple, m_w_ple_gate, m_w_ple_proj, m_g_final, v_g_mix, v_w_in, v_conv_w, v_a_log, v_dt_bias, v_rel_bias, v_w_onorm, v_w_branch_a, v_w_branch_b, v_w_out, v_g_ffn, v_w_gate_up, v_w_down, v_g_ple, v_w_ple_gate, v_w_ple_proj, v_g_final):
    names = ["g_mix", "w_in", "conv_w", "a_log", "dt_bias", "rel_bias", "w_onorm", "w_branch_a", "w_branch_b",
             "w_out", "g_ffn", "w_gate_up", "w_down", "g_ple", "w_ple_gate", "w_ple_proj", "g_final"]
    w = dict(zip(names, [g_mix, w_in, conv_w, a_log, dt_bias, rel_bias, w_onorm, w_branch_a, w_branch_b, w_out,
                         g_ffn, w_gate_up, w_down, g_ple, w_ple_gate, w_ple_proj, g_final]))
    m = dict(zip(names, [m_g_mix, m_w_in, m_conv_w, m_a_log, m_dt_bias, m_rel_bias, m_w_onorm, m_w_branch_a,
                         m_w_branch_b, m_w_out, m_g_ffn, m_w_gate_up, m_w_down, m_g_ple, m_w_ple_gate,
                         m_w_ple_proj, m_g_final]))
    v = dict(zip(names, [v_g_mix, v_w_in, v_conv_w, v_a_log, v_dt_bias, v_rel_bias, v_w_onorm, v_w_branch_a,
                         v_w_branch_b, v_w_out, v_g_ffn, v_w_gate_up, v_w_down, v_g_ple, v_w_ple_gate,
                         v_w_ple_proj, v_g_final]))
    xi, yi, ci = _place()
    chip = 2 * xi + yi
    big_names = [n for n, _, _ in BIG]

    shards2d = {n: w[n].reshape(w[n].shape[-2:]) for n in big_names}
    shards_bf = {n: a.astype(BF16) for n, a in shards2d.items()}
    g4 = allgather_weights(shards_bf, FIRST_WEIGHTS, chip)["w_in"]
    place = jnp.stack([chip, ci]).astype(jnp.int32)
    conv_sh = jnp.where(ci == 0, w["conv_w"].reshape(CONV_K, CONV_CH // N_CHIPS), 0.0)
    conv_slots = lax.dynamic_update_slice(jnp.zeros((N_CHIPS, CONV_K, CONV_CH // N_CHIPS), F32), conv_sh[None],
                                          (chip, 0, 0))
    conv_all = small_allreduce(conv_slots.reshape(-1, 128), "gather_conv_w")
    conv_full = jnp.transpose(conv_all.reshape(N_CHIPS, CONV_K, CONV_CH // N_CHIPS), (1, 0, 2)).reshape(CONV_K, CONV_CH)
    small = {n: w[n] for n in names if n not in big_names}
    small["conv_w"] = conv_full

    grad_x, grads, small_grads, reduced_early = local_step(
        x, p[0], loss_target, g4, small, LaterWeights(shards_bf, chip), GradReduce(place, EARLY_GRADS, "grads"))

    late = GradReduce(place, LATE_GRADS, "late")
    late.begin(grads)
    dep = jnp.full((8, 128), late.exchange(grads["w_in"]), F32)
    gshard = dict(reduced_early)
    small_names = [n for n, _ in SMALL]
    red = unpack_small(small_allreduce(pack_small(small_grads, small_names, 112), "allreduce_small"), SMALL)
    loss = red["loss"][0]
    conv_g = lax.dynamic_slice(red["conv_w"].reshape(CONV_K, N_CHIPS, CONV_CH // N_CHIPS), (0, chip, 0),
                               (CONV_K, 1, CONV_CH // N_CHIPS))
    gsmall = {n: red[n].reshape(w[n].shape) for n in small_names if n not in ("loss", "conv_w")}
    gsmall["conv_w"] = conv_g.reshape(w["conv_w"].shape)

    grad, delta, new_m, new_v = {}, {}, {}, {}
    for n in list(EARLY_GRADS) + list(LATE_GRADS):
        if n in LATE_GRADS:
            late.join(v_)
            gshard.update(late.finish(v_))
        shp = w[n].shape
        d_, m_, v_ = adamw(shards2d[n], gshard[n], m[n].reshape(shp[-2:]), v[n].reshape(shp[-2:]), "adamw_" + n,
                           dep=dep if n in EARLY_GRADS else None)
        dep, v_ = lax.optimization_barrier((dep, v_))
        grad[n], delta[n], new_m[n], new_v[n] = gshard[n].reshape(shp), d_.reshape(shp), m_.reshape(shp), v_.reshape(shp)
    snames = [n for n in small_names if n != "loss"]
    ssizes = [(n, w[n].size) for n in snames]
    pk = lambda d: pack_small(d, snames, 64)
    d_, m_, v_ = adamw(pk(w), pk(gsmall), pk(m), pk(v), "adamw_small")
    ds, ms, vs = unpack_small(d_, ssizes), unpack_small(m_, ssizes), unpack_small(v_, ssizes)
    for n in snames:
        shp = w[n].shape
        grad[n], delta[n], new_m[n], new_v[n] = gsmall[n], ds[n].reshape(shp), ms[n].reshape(shp), vs[n].reshape(shp)

    return (loss, grad_x, *[grad[n] for n in names], *[delta[n] for n in names],
            *[new_m[n] for n in names], *[new_v[n] for n in names])
```

```python
import functools

import jax
import jax.numpy as jnp
from jax import lax
from jax.experimental import pallas as pl
from jax.experimental.pallas import tpu as pltpu

F32 = jnp.float32
BF16 = jnp.bfloat16
HI = lax.Precision.HIGHEST
MESH = pl.DeviceIdType.MESH

D_MODEL = 1024
CHUNK = 64
PLE_DIM = 256
EPS = 1e-6
A_HEADS = 8
A_HEAD_DIM = 64
A_WIDTH = 512
A_LOOKBACK = 8
BAND = (A_LOOKBACK + 1) * CHUNK
TAIL = 3 * CHUNK
REL_CLIP = 128
N_REL = 2 * REL_CLIP + 1
B_HEADS = 4
B_DIM = 128
B_WIDTH = 512
CONV_K = 4
CONV_CH = 1536
D_FF = 2816
SPLIT_Z = 3584
D_IN = 5640
ADAM_LR, ADAM_B1, ADAM_B2, ADAM_EPS, ADAM_WD, ADAM_STEP = 0.001, 0.9, 0.999, 1e-08, 0.01, 10

P_GATES, P_QA, P_KA, P_VA, P_CONV, P_Z, P_BD, P_WIDTH = 0, 2048, 2560, 3072, 3584, 5120, 5632, 5760

VMEM_LIMIT = 56 * 1024 * 1024


def _cp(sem, vmem=None, **kw):
    return pltpu.CompilerParams(dimension_semantics=sem, vmem_limit_bytes=vmem, **kw)


def _tile(n, cap):
    best = None
    for t in range(128, cap + 1, 128):
        if n % t == 0:
            best = t
    assert best is not None, (n, cap)
    return best


def _nn(a, b, prec=None):
    return lax.dot_general(a, b, (((1,), (0,)), ((), ())), preferred_element_type=F32, precision=prec)


def _nt(a, b, prec=None):
    return lax.dot_general(a, b, (((1,), (1,)), ((), ())), preferred_element_type=F32, precision=prec)


def _tn(a, b, prec=None):
    return lax.dot_general(a, b, (((0,), (0,)), ((), ())), preferred_element_type=F32, precision=prec)


def _bnn(a, b, prec=None):
    return lax.dot_general(a, b, (((2,), (1,)), ((0,), (0,))), preferred_element_type=F32, precision=prec)


def _bnt(a, b, prec=None):
    return lax.dot_general(a, b, (((2,), (2,)), ((0,), (0,))), preferred_element_type=F32, precision=prec)


def _bf(a):
    return a.astype(BF16)


def _split(a):
    hi = a.astype(BF16)
    return hi, (a - hi.astype(F32)).astype(BF16)


def _split3(a):
    h1 = _bf(a)
    r1 = a - h1.astype(F32)
    h2 = _bf(r1)
    return h1, h2, _bf(r1 - h2.astype(F32))


def _bnn_exact(lhs_b, rhs):
    h1, h2, h3 = _split3(rhs)
    return _bnn(lhs_b, h1) + (_bnn(lhs_b, h2) + _bnn(lhs_b, h3))


def _bnn3(a, b):
    ah, al = a if isinstance(a, tuple) else _split(a)
    bh, bl = b if isinstance(b, tuple) else _split(b)
    return _bnn(ah, bh) + (_bnn(ah, bl) + _bnn(al, bh))


def _sigmoid(x):
    return 0.5 * jnp.tanh(0.5 * x) + 0.5


def _softplus(x):
    return jnp.maximum(x, 0.0) + jnp.log(1.0 + jnp.exp(-jnp.abs(x)))


def rms_matmul(x, g, w, name, tm=512, tn_cap=1024):
    t, d = x.shape
    n = w.shape[1]
    tm = min(tm, t)
    tn = _tile(n, tn_cap)

    nj = n // tn

    def body(x_ref, g_ref, w_ref, o_ref, h_ref, tail_ref):
        @pl.when(pl.program_id(1) == 0)
        def _():
            xv = x_ref[...]
            r = lax.rsqrt(jnp.mean(xv * xv, axis=-1, keepdims=True) + EPS)
            h_ref[...] = _bf(xv * r * g_ref[...])

        res = _nn(h_ref[...], w_ref[...])
        o_ref[...] = _bf(res)

        @pl.when(pl.program_id(1) == nj - 1)
        def _():
            tail_ref[...] = res[:, tn - 128:]

    return pl.pallas_call(
        body, name=name, grid=(t // tm, nj),
        in_specs=[pl.BlockSpec((tm, d), lambda i, j: (i, 0)),
                  pl.BlockSpec((1, d), lambda i, j: (0, 0)),
                  pl.BlockSpec((d, tn), lambda i, j: (0, j))],
        out_specs=[pl.BlockSpec((tm, tn), lambda i, j: (i, j)),
                   pl.BlockSpec((tm, d), lambda i, j: (i, 0)),
                   pl.BlockSpec((tm, 128), lambda i, j: (i, 0))],
        out_shape=[jax.ShapeDtypeStruct((t, n), BF16), jax.ShapeDtypeStruct((t, d), BF16),
                   jax.ShapeDtypeStruct((t, 128), F32)],
        compiler_params=_cp(("parallel", "arbitrary"), VMEM_LIMIT),
    )(x, g, w)


def matmul_tn(a, b, name, into=None, col0=0, width=None, tm=1024, tk_cap=1408, tn_cap=1408):
    m, k1 = a.shape
    n = b.shape[1]
    tm = min(tm, m)
    tk = _tile(k1, tk_cap)
    tn = _tile(n, tn_cap)
    while col0 % tn:
        tn = _tile(n, tn - 128)
    nk = m // tm
    c0 = col0 // tn

    def body(*refs):
        a_ref, b_ref, o_ref, acc = refs[0], refs[1], refs[-2], refs[-1]

        @pl.when(pl.program_id(2) == 0)
        def _():
            acc[...] = jnp.zeros_like(acc)

        acc[...] += _tn(_bf(a_ref[...]), _bf(b_ref[...]))

        @pl.when(pl.program_id(2) == nk - 1)
        def _():
            o_ref[...] = _bf(acc[...])

    in_specs = [pl.BlockSpec((tm, tk), lambda i, j, k: (k, i)),
                pl.BlockSpec((tm, tn), lambda i, j, k: (k, j))]
    args = [a, b]
    total = n if width is None else width
    aliases = {}
    if into is not None:
        in_specs.append(ANY)
        args.append(into)
        total = into.shape[1]
        aliases = {2: 0}
    return pl.pallas_call(
        body, name=name, grid=(k1 // tk, n // tn, nk),
        in_specs=in_specs,
        out_specs=pl.BlockSpec((tk, tn), lambda i, j, k: (i, c0 + j)),
        out_shape=jax.ShapeDtypeStruct((k1, total), BF16),
        scratch_shapes=[pltpu.VMEM((tk, tn), F32)],
        input_output_aliases=aliases,
        compiler_params=_cp(("parallel", "parallel", "arbitrary"), VMEM_LIMIT),
    )(*args)


def _tail_onehot(qi):
    r = lax.broadcasted_iota(jnp.int32, (384, TAIL), 0)
    kj = lax.broadcasted_iota(jnp.int32, (384, TAIL), 1)
    return (r == jnp.minimum(REL_CLIP + qi - kj, REL_CLIP) + REL_CLIP).astype(F32)


def bias_tail(rel_pad):
    def body(rb_ref, o_ref):
        parts = _split3(rb_ref[...])
        for qi in range(CHUNK):
            oh = _bf(_tail_onehot(qi))
            o_ref[qi] = _nn(parts[0], oh) + (_nn(parts[1], oh) + _nn(parts[2], oh))

    return pl.pallas_call(
        body, name="bias_tail",
        out_shape=jax.ShapeDtypeStruct((CHUNK, A_HEADS, TAIL), F32),
    )(rel_pad)


def bias_grad(db_t, db_far):
    def body(t_ref, f_ref, o_ref):
        acc = jnp.zeros((A_HEADS, 384), F32)
        for qi in range(CHUNK):
            oh = _bf(_tail_onehot(qi))
            parts = _split3(t_ref[qi])
            acc = acc + (_nt(parts[0], oh) + (_nt(parts[1], oh) + _nt(parts[2], oh)))
        far = jnp.sum(jnp.sum(f_ref[...], axis=2), axis=1, keepdims=True)
        lane = lax.broadcasted_iota(jnp.int32, (A_HEADS, 384), 1)
        o_ref[...] = acc + jnp.where(lane == 2 * REL_CLIP, far, 0.0)

    return pl.pallas_call(
        body, name="bias_grad",
        out_shape=jax.ShapeDtypeStruct((A_HEADS, 384), F32),
    )(db_t, db_far)


ATT_CB = 8


WIN = BAND + CHUNK


def _stack_heads(a, lane):
    return jnp.concatenate([jnp.where(lane < 64, a, 0.0), jnp.where(lane >= 64, a, 0.0)], axis=0)


def _fill_band_pads(k_ref, v_ref, kp, vp, s):
    z = jnp.zeros((A_LOOKBACK * CHUNK, 128), BF16)
    kp[pl.ds(0, A_LOOKBACK * CHUNK), :] = z
    vp[pl.ds(0, A_LOOKBACK * CHUNK), :] = z
    kp[pl.ds(A_LOOKBACK * CHUNK, s), :] = _bf(k_ref[...])
    vp[pl.ds(A_LOOKBACK * CHUNK, s), :] = _bf(v_ref[...])


def attn_fwd(proj, bias_band, b, s):
    t = b * s
    nc = s // CHUNK
    qb, kb_, vb_ = P_QA // 128, P_KA // 128, P_VA // 128

    nstep = nc // ATT_CB
    rows = ATT_CB * CHUNK

    def body(q_ref, k_ref, v_ref, b_ref, o_ref, lse_ref, kp, vp):
        n0 = pl.program_id(2) * ATT_CB

        @pl.when(n0 == 0)
        def _():
            _fill_band_pads(k_ref, v_ref, kp, vp, s)

        lane = lax.broadcasted_iota(jnp.int32, (2 * CHUNK, 128), 1)
        col = lax.broadcasted_iota(jnp.int32, (4 * CHUNK, WIN), 1)
        bias4 = b_ref[...]

        def pair(pp, carry):
            n = n0 + 2 * pp
            r0 = pl.multiple_of(pp * 2 * CHUNK, 2 * CHUNK)
            start = pl.multiple_of(n * CHUNK, CHUNK)
            kb = kp[pl.ds(start, WIN), :]
            vb = vp[pl.ds(start, WIN), :]
            q4 = _stack_heads(q_ref[pl.ds(r0, 2 * CHUNK), :] * (A_HEAD_DIM ** -0.5), lane)
            sc = jnp.where(col >= (A_LOOKBACK - n) * CHUNK, _nt(_bf(q4), kb) + bias4, -1e30)
            mx = jnp.max(sc, axis=1, keepdims=True)
            p = jnp.exp(sc - mx)
            l = jnp.sum(p, axis=1, keepdims=True)
            o4 = _nn(_bf(p), vb) / l
            lse4 = mx + jnp.log(l)
            o_ref[pl.ds(r0, 2 * CHUNK), :] = jnp.where(lane < 64, o4[:2 * CHUNK], o4[2 * CHUNK:])
            lse_ref[pl.ds(r0, 2 * CHUNK), :] = jnp.where(lane < 64, lse4[:2 * CHUNK], lse4[2 * CHUNK:])
            return carry

        lax.fori_loop(0, ATT_CB // 2, pair, 0, unroll=2)

    return pl.pallas_call(
        body, name="attn_fwd", grid=(b, 4, nstep),
        in_specs=[pl.BlockSpec((rows, 128), lambda bb, m, n: (bb * nstep + n, qb + m)),
                  pl.BlockSpec((s, 128), lambda bb, m, n: (bb, kb_ + m)),
                  pl.BlockSpec((s, 128), lambda bb, m, n: (bb, vb_ + m)),
                  pl.BlockSpec((None, 4 * CHUNK, WIN), lambda bb, m, n: (m, 0, 0))],
        out_specs=[pl.BlockSpec((rows, 128), lambda bb, m, n: (bb * nstep + n, m)),
                   pl.BlockSpec((rows, 128), lambda bb, m, n: (bb * nstep + n, m))],
        out_shape=[jax.ShapeDtypeStruct((t, A_WIDTH), F32), jax.ShapeDtypeStruct((t, A_WIDTH), F32)],
        scratch_shapes=[pltpu.VMEM((s + A_LOOKBACK * CHUNK, 128), BF16),
                        pltpu.VMEM((s + A_LOOKBACK * CHUNK, 128), BF16)],
        compiler_params=_cp(("parallel", "parallel", "arbitrary"), VMEM_LIMIT),
    )(proj, proj, proj, bias_band)


def attn_bwd(proj, bias_band, y_a, lse, dy_a, b, s):
    t = b * s
    nc = s // CHUNK
    qb, kb_, vb_ = P_QA // 128, P_KA // 128, P_VA // 128
    pad = A_LOOKBACK * CHUNK
    nstep = nc // ATT_CB
    rows = ATT_CB * CHUNK

    def body(q_ref, k_ref, v_ref, b_ref, do_ref, o_ref, lse_ref,
             dq_ref, dk_ref, dv_ref, dbt_ref, dbf_ref, kp, vp, dkp, dvp):
        bb = pl.program_id(1)
        n0 = pl.program_id(2) * ATT_CB

        @pl.when(n0 == 0)
        def _():
            _fill_band_pads(k_ref, v_ref, kp, vp, s)
            dkp[...] = jnp.zeros_like(dkp)
            dvp[...] = jnp.zeros_like(dvp)

        @pl.when((n0 == 0) & (bb == 0))
        def _():
            dbt_ref[...] = jnp.zeros_like(dbt_ref)
            dbf_ref[...] = jnp.zeros_like(dbf_ref)

        lane = lax.broadcasted_iota(jnp.int32, (2 * CHUNK, 128), 1)
        col = lax.broadcasted_iota(jnp.int32, (4 * CHUNK, WIN), 1)
        bias4 = b_ref[...]

        def pair(pp, carry):
            n = n0 + 2 * pp
            r0 = pl.multiple_of(pp * 2 * CHUNK, 2 * CHUNK)
            start = pl.multiple_of(n * CHUNK, CHUNK)
            kb = kp[pl.ds(start, WIN), :]
            vb = vp[pl.ds(start, WIN), :]
            q4b = _bf(_stack_heads(q_ref[pl.ds(r0, 2 * CHUNK), :] * (A_HEAD_DIM ** -0.5), lane))
            do4 = _stack_heads(do_ref[pl.ds(r0, 2 * CHUNK), :], lane)
            do4b = _bf(do4)
            o = o_ref[pl.ds(r0, 2 * CHUNK), :]
            lsev = lse_ref[pl.ds(r0, 2 * CHUNK), :]
            lse4 = jnp.concatenate([lsev[:, 0:1], lsev[:, 64:65]], axis=0)
            sc = jnp.where(col >= (A_LOOKBACK - n) * CHUNK, _nt(q4b, kb) + bias4, -1e30)
            p = jnp.exp(sc - lse4)
            dp = _nt(do4b, vb)
            delta = jnp.sum(do4 * jnp.concatenate([o, o], axis=0), axis=1, keepdims=True)
            ds = p * (dp - delta)
            dsb = _bf(ds)
            dq4 = _nn(dsb, kb)
            dq_ref[pl.ds(r0, 2 * CHUNK), :] = _bf(
                jnp.where(lane < 64, dq4[:2 * CHUNK], dq4[2 * CHUNK:]) * (A_HEAD_DIM ** -0.5))
            dkp[pl.ds(start, WIN), :] += _tn(dsb, q4b)
            dvp[pl.ds(start, WIN), :] += _tn(_bf(p), do4b)
            dbt_ref[...] += ds[:, WIN - 256:]
            dbf_ref[...] += ds[:, 0:128] + ds[:, 128:256] + ds[:, 256:384]
            return carry

        lax.fori_loop(0, ATT_CB // 2, pair, 0, unroll=2)

        @pl.when(n0 == nc - ATT_CB)
        def _():
            dk_ref[...] = _bf(dkp[pl.ds(pad, s), :])
            dv_ref[...] = _bf(dvp[pl.ds(pad, s), :])

    return pl.pallas_call(
        body, name="attn_bwd", grid=(4, b, nstep),
        in_specs=[pl.BlockSpec((rows, 128), lambda m, bb, n: (bb * nstep + n, qb + m)),
                  pl.BlockSpec((s, 128), lambda m, bb, n: (bb, kb_ + m)),
                  pl.BlockSpec((s, 128), lambda m, bb, n: (bb, vb_ + m)),
                  pl.BlockSpec((None, 4 * CHUNK, WIN), lambda m, bb, n: (m, 0, 0)),
                  pl.BlockSpec((rows, 128), lambda m, bb, n: (bb * nstep + n, m)),
                  pl.BlockSpec((rows, 128), lambda m, bb, n: (bb * nstep + n, m)),
                  pl.BlockSpec((rows, 128), lambda m, bb, n: (bb * nstep + n, m))],
        out_specs=[pl.BlockSpec((rows, 128), lambda m, bb, n: (bb * nstep + n, m)),
                   pl.BlockSpec((s, 128), lambda m, bb, n: (bb, m)),
                   pl.BlockSpec((s, 128), lambda m, bb, n: (bb, m)),
                   pl.BlockSpec((None, 4 * CHUNK, 256), lambda m, bb, n: (m, 0, 0)),
                   pl.BlockSpec((None, 4 * CHUNK, 128), lambda m, bb, n: (m, 0, 0))],
        out_shape=[jax.ShapeDtypeStruct((t, A_WIDTH), BF16)] * 3
        + [jax.ShapeDtypeStruct((4, 4 * CHUNK, 256), F32),
           jax.ShapeDtypeStruct((4, 4 * CHUNK, 128), F32)],
        scratch_shapes=[pltpu.VMEM((s + pad, 128), BF16), pltpu.VMEM((s + pad, 128), BF16),
                        pltpu.VMEM((s + pad, 128), F32), pltpu.VMEM((s + pad, 128), F32)],
        compiler_params=_cp(("parallel", "arbitrary", "arbitrary"), VMEM_LIMIT),
    )(proj, proj, proj, bias_band, dy_a, y_a, lse)


def _conv_taps(x, w, s):
    row = lax.broadcasted_iota(jnp.int32, x.shape, 0)
    shifted = [x] + [jnp.where(row >= i, pltpu.roll(x, i, 0), 0.0) for i in range(1, CONV_K)]
    acc = shifted[0] * w[CONV_K - 1:CONV_K, :]
    for i in range(1, CONV_K):
        acc = acc + shifted[i] * w[CONV_K - 1 - i:CONV_K - i, :]
    return acc, shifted


def conv_fwd(proj, conv_w8, b, s):
    cb = 512
    c0 = P_CONV // cb

    def body(x_ref, w_ref, o_ref):
        a, _ = _conv_taps(x_ref[...].astype(F32), w_ref[...], s)
        o_ref[...] = a * _sigmoid(a)

    return pl.pallas_call(
        body, name="conv_fwd", grid=(b, CONV_CH // cb),
        in_specs=[pl.BlockSpec((s, cb), lambda bb, j: (bb, c0 + j)),
                  pl.BlockSpec((8, cb), lambda bb, j: (0, j))],
        out_specs=pl.BlockSpec((s, cb), lambda bb, j: (bb, j)),
        out_shape=jax.ShapeDtypeStruct((b * s, CONV_CH), F32),
        compiler_params=_cp(("parallel", "parallel"), VMEM_LIMIT),
    )(proj, conv_w8)


def conv_bwd(proj, conv_w8, dc3, b, s):
    cb = 512
    c0 = P_CONV // cb

    def body(x_ref, w_ref, dc_ref, dx_ref, dw_ref):
        @pl.when(pl.program_id(1) == 0)
        def _():
            dw_ref[...] = jnp.zeros_like(dw_ref)

        w = w_ref[...]
        a, shifted = _conv_taps(x_ref[...].astype(F32), w, s)
        sg = _sigmoid(a)
        da = dc_ref[...] * (sg * (1.0 + a * (1.0 - sg)))
        row = lax.broadcasted_iota(jnp.int32, da.shape, 0)
        dx = da * w[CONV_K - 1:CONV_K, :]
        for i in range(1, CONV_K):
            dx = dx + jnp.where(row < s - i, pltpu.roll(da, s - i, 0), 0.0) * w[CONV_K - 1 - i:CONV_K - i, :]
        dx_ref[...] = _bf(dx)
        r8 =lax.broadcasted_iota(jnp.int32, (8, cb), 0)
        dw = jnp.zeros((8, cb), F32)
        for i in range(CONV_K):
            dw = dw + jnp.where(r8 == CONV_K - 1 - i, jnp.sum(da * shifted[i], axis=0, keepdims=True), 0.0)
        dw_ref[...] += dw

    return pl.pallas_call(
        body, name="conv_bwd", grid=(CONV_CH // cb, b),
        in_specs=[pl.BlockSpec((s, cb), lambda j, bb: (bb, c0 + j)),
                  pl.BlockSpec((8, cb), lambda j, bb: (0, j)),
                  pl.BlockSpec((None, s, cb), lambda j, bb: (j, bb, 0))],
        out_specs=[pl.BlockSpec((s, cb), lambda j, bb: (bb, j)),
                   pl.BlockSpec((8, cb), lambda j, bb: (0, j))],
        out_shape=[jax.ShapeDtypeStruct((b * s, CONV_CH), BF16), jax.ShapeDtypeStruct((8, CONV_CH), F32)],
        compiler_params=_cp(("parallel", "arbitrary"), VMEM_LIMIT),
    )(proj, conv_w8, dc3)


def _pick_lane(v, k):
    lane = lax.broadcasted_iota(jnp.int32, v.shape, 1)
    return jnp.sum(jnp.where(lane == k, v, 0.0), axis=1, keepdims=True)


def _chunk_masks(ncb):
    i = lax.broadcasted_iota(jnp.int32, (ncb, CHUNK, CHUNK), 1)
    j = lax.broadcasted_iota(jnp.int32, (ncb, CHUNK, CHUNK), 2)
    return i, j


def _col_of_row(rowvec, eye):
    return jnp.sum(jnp.where(eye, rowvec, 0.0), axis=2, keepdims=True)


def _dn_chunk_math(cq, ck, cv, bd, al_row, dtb_row, h, ncb, tm=None):
    r = ncb * CHUNK
    i, j = _chunk_masks(ncb)
    eye = i == j
    low = i >= j
    strict = i > j
    ones = jnp.ones((ncb, CHUNK, CHUNK), F32)

    braw = _pick_lane(bd, h)
    draw = _pick_lane(bd, B_HEADS + h)
    al = _pick_lane(al_row, h)
    dtb = _pick_lane(dtb_row, h)
    ea = jnp.exp(al)
    beta = _sigmoid(braw)
    sp_arg = draw + dtb
    g = -ea * _softplus(sp_arg)

    rq = lax.rsqrt(jnp.sum(cq * cq, axis=1, keepdims=True) + EPS)
    rk = lax.rsqrt(jnp.sum(ck * ck, axis=1, keepdims=True) + EPS)
    nq = cq * rq
    kn = ck * rk
    qn = nq * (B_DIM ** -0.5)

    def c3(a):
        return a.reshape(ncb, CHUNK, a.shape[-1])

    qn3, kn3, v3, beta3 = c3(qn), c3(kn), c3(cv), c3(beta)
    gb = jnp.broadcast_to(c3(g), (ncb, CHUNK, CHUNK))
    gc_b = _bnn_exact(low.astype(BF16), gb)
    gr_b = _bnn_exact(_bf(ones), jnp.where(eye, gc_b, 0.0))
    dm = jnp.where(low, jnp.exp(jnp.where(low, gc_b - gr_b, 0.0)), 0.0)
    gc = gc_b[:, :, 0:1]
    gl = gc_b[:, CHUNK - 1:CHUNK, 0:1]
    gam = jnp.exp(gc)
    egl = jnp.exp(gl)
    edec = jnp.exp(gl - gc)

    knb = _bf(kn3)
    kk = _bnt(knb, knb)
    kd = jnp.where(strict, kk * dm, 0.0)
    a = beta3 * kd
    sz = 1 if tm is None else CHUNK
    if tm is None:
        tm = eye.astype(F32)
    while sz < CHUNK:
        off = jnp.where(((i // (2 * sz)) == (j // (2 * sz))) & ((i // sz) != (j // sz)), a, 0.0)
        tmb = _bf(tm)
        tm = tm - _bnn(_bf(_bnn(tmb, _bf(off))), tmb)
        sz *= 2
    bv = beta3 * v3
    bk = (beta3 * gam) * kn3
    sol = _bnn3(_split(tm), jnp.concatenate([bv, bk], axis=2))
    u, wk = sol[:, :, :B_DIM], sol[:, :, B_DIM:]
    qk = _bnt(_bf(qn3), knb)
    p = jnp.where(low, qk * dm, 0.0)
    kdec = kn3 * edec
    qg = gam * qn3
    return dict(beta=beta3, g=c3(g), ea=ea, sp_arg=c3(sp_arg), rq=c3(rq), rk=c3(rk), nq=c3(nq),
                qn=qn3, kn=kn3, v=v3, gc=gc, gl=gl, gam=gam, egl=egl, edec=edec, dm=dm, kd=kd, a=a,
                tm=tm, u=u, wk=wk, qk=qk, p=p, kdec=kdec, qg=qg, eye=eye, low=low, strict=strict)


def dn_prep(c, proj, al_row, dtb_row, b, s, ncb=16):
    t = b * s
    r = ncb * CHUNK
    nblk = t // r
    bd_blk = 0

    def body(cq_ref, ck_ref, cv_ref, bd_ref, al_ref, dtb_ref, u_ref, wk_ref, qg_ref, kdec_ref, p_ref, egl_ref,
             tm_ref):
        h = pl.program_id(1)
        m = _dn_chunk_math(cq_ref[...], ck_ref[...], cv_ref[...], bd_ref[...].astype(F32), al_ref[...], dtb_ref[...], h, ncb)
        tm_ref[...] = m["tm"].reshape(r, CHUNK)
        u_ref[...] = m["u"].reshape(r, B_DIM)
        wk_ref[...] = _bf(m["wk"].reshape(r, B_DIM))
        qg_ref[...] = _bf(m["qg"].reshape(r, B_DIM))
        kdec_ref[...] = _bf(m["kdec"].reshape(r, B_DIM))
        p_ref[...] = m["p"].reshape(r, CHUNK)
        egl_ref[...] = jnp.broadcast_to(m["egl"], (ncb, 8, 128)).reshape(ncb * 8, 128)

    col = lambda k: pl.BlockSpec((r, 128), lambda i, h: (i, k * B_HEADS + h))
    out_col = pl.BlockSpec((r, 128), lambda i, h: (i, h))
    small = pl.BlockSpec((1, 128), lambda i, h: (0, 0))
    return pl.pallas_call(
        body, name="dn_prep", grid=(nblk, B_HEADS),
        in_specs=[col(0), col(1), col(2), pl.BlockSpec((r, 128), lambda i, h: (i, bd_blk)), small, small],
        out_specs=[out_col, out_col, out_col, out_col,
                   pl.BlockSpec((None, r, CHUNK), lambda i, h: (h, i, 0)),
                   pl.BlockSpec((None, ncb * 8, 128), lambda i, h: (h, i, 0)),
                   pl.BlockSpec((None, r, CHUNK), lambda i, h: (h, i, 0))],
        out_shape=[jax.ShapeDtypeStruct((t, B_WIDTH), F32)] + [jax.ShapeDtypeStruct((t, B_WIDTH), BF16)] * 3
        + [jax.ShapeDtypeStruct((B_HEADS, t, CHUNK), F32),
           jax.ShapeDtypeStruct((B_HEADS, t // 8, 128), F32),
           jax.ShapeDtypeStruct((B_HEADS, t, CHUNK), F32)],
        compiler_params=_cp(("parallel", "parallel"), VMEM_LIMIT),
    )(c, c, c, proj, al_row, dtb_row)


def dn_scan_fwd(u, wk, qg, kdec, p, egl, b, s):
    t = b * s
    nc = s // CHUNK

    def body(u_ref, wk_ref, qg_ref, kdec_ref, p_ref, egl_ref, o_ref, ss_ref, st):
        @pl.when(pl.program_id(0) == 0)
        def _():
            st[...] = jnp.zeros_like(st)

        chains = [(bb, h) for bb in range(b) for h in range(B_HEADS)]
        states = [st[bb * B_HEADS + h] for bb, h in chains]
        sls = [slice(h * B_DIM, (h + 1) * B_DIM) for _, h in chains]
        sbs = [_bf(sh) for sh in states]
        ws = [u_ref[bb, :, sl] - _nt(_bf(wk_ref[bb, :, sl]), sb) for (bb, _), sl, sb in zip(chains, sls, sbs)]
        qs = [_nt(_bf(qg_ref[bb, :, sl]), sb) for (bb, _), sl, sb in zip(chains, sls, sbs)]
        wbs = [_bf(w) for w in ws]
        outs = [q + _nn(_bf(p_ref[h, bb]), wb) for (bb, h), q, wb in zip(chains, qs, wbs)]
        new_states = [egl_ref[h, bb][0:1, :] * sh + _tn(wb, _bf(kdec_ref[bb, :, sl]))
                      for (bb, h), sl, sh, wb in zip(chains, sls, states, wbs)]
        for (bb, h), sh, o, ns in zip(chains, states, outs, new_states):
            ss_ref[bb, h] = sh
            o_ref[bb, :, h * B_DIM:(h + 1) * B_DIM] = o
            st[bb * B_HEADS + h] = ns

    r3 = lambda a: a.reshape(b, s, B_WIDTH)
    act = pl.BlockSpec((b, CHUNK, B_WIDTH), lambda n: (0, n, 0))
    o, states = pl.pallas_call(
        body, name="dn_scan_fwd", grid=(nc,),
        in_specs=[act, act, act, act,
                  pl.BlockSpec((B_HEADS, b, CHUNK, CHUNK), lambda n: (0, 0, n, 0)),
                  pl.BlockSpec((B_HEADS, b, 8, 128), lambda n: (0, 0, n, 0))],
        out_specs=[act, pl.BlockSpec((b, None, B_HEADS, B_DIM, B_DIM), lambda n: (0, n, 0, 0, 0))],
        out_shape=[jax.ShapeDtypeStruct((b, s, B_WIDTH), F32),
                   jax.ShapeDtypeStruct((b, nc, B_HEADS, B_DIM, B_DIM), F32)],
        scratch_shapes=[pltpu.VMEM((b * B_HEADS, B_DIM, B_DIM), F32)],
        compiler_params=_cp(("arbitrary",), VMEM_LIMIT),
    )(r3(u), r3(wk), r3(qg), r3(kdec), p.reshape(B_HEADS, b, s, CHUNK), egl.reshape(B_HEADS, b, s // 8, 128))
    return o.reshape(t, B_WIDTH), states


def dn_scan_bwd(u, wk, qg, kdec, p, egl, states, do, b, s):
    t = b * s
    nc = s // CHUNK

    def body(u_ref, wk_ref, qg_ref, kdec_ref, p_ref, egl_ref, ss_ref, do_ref,
             dw_ref, dwk_ref, dqg_ref, dkdec_ref, dp_ref, degl_ref, dst):
        @pl.when(pl.program_id(0) == 0)
        def _():
            dst[...] = jnp.zeros_like(dst)

        chains = [(bb, h) for bb in range(b) for h in range(B_HEADS)]
        dstates = [dst[bb * B_HEADS + h] for bb, h in chains]
        n8 = range(len(chains))
        sls = [slice(h * B_DIM, (h + 1) * B_DIM) for _, h in chains]
        shs = [ss_ref[bb, h] for bb, h in chains]
        sbs = [_bf(sh) for sh in shs]
        dsbs = [_bf(dsp) for dsp in dstates]
        wkbs = [_bf(wk_ref[bb, :, sl]) for (bb, _), sl in zip(chains, sls)]
        dobs = [_bf(do_ref[bb, :, sl]) for (bb, _), sl in zip(chains, sls)]
        t1 = [_nt(wkbs[i], sbs[i]) for i in n8]
        dwa = [_tn(_bf(p_ref[h, bb]), dobs[i]) for i, (bb, h) in enumerate(chains)]
        dwb_ = [_nt(_bf(kdec_ref[bb, :, sls[i]]), dsbs[i]) for i, (bb, _) in enumerate(chains)]
        dqgs = [_nn(dobs[i], sbs[i]) for i in n8]
        dsq = [_tn(dobs[i], _bf(qg_ref[bb, :, sls[i]])) for i, (bb, _) in enumerate(chains)]
        wbs = [_bf(u_ref[bb, :, sls[i]] - t1[i]) for i, (bb, _) in enumerate(chains)]
        dws = [dwa[i] + dwb_[i] for i in n8]
        dwbs = [_bf(dw) for dw in dws]
        dwks = [-_nn(dwbs[i], sbs[i]) for i in n8]
        dkdecs = [_nn(wbs[i], dsbs[i]) for i in n8]
        dpms = [_nt(dobs[i], wbs[i]) for i in n8]
        dsw = [_tn(dwbs[i], wkbs[i]) for i in n8]
        tots = [jnp.sum(jnp.sum(shs[i] * dstates[i], axis=1, keepdims=True), axis=0, keepdims=True) for i in n8]
        new_dss = [egl_ref[h, bb][0:1, :] * dstates[i] + dsq[i] - dsw[i] for i, (bb, h) in enumerate(chains)]
        results = [(dws[i], dqgs[i], dwks[i], dkdecs[i], dpms[i], tots[i], new_dss[i]) for i in n8]
        for (bb, h), (dw, dqg, dwk, dkdec, dpm, tot, new_ds) in zip(chains, results):
            sl = slice(h * B_DIM, (h + 1) * B_DIM)
            dw_ref[bb, :, sl] = dw
            dqg_ref[bb, :, sl] = dqg
            dwk_ref[bb, :, sl] = dwk
            dkdec_ref[bb, :, sl] = dkdec
            dp_ref[h, bb] = dpm
            degl_ref[h, bb] = jnp.broadcast_to(tot, (8, 128))
            dst[bb * B_HEADS + h] = new_ds

    r3 = lambda a: a.reshape(b, s, B_WIDTH)
    act = pl.BlockSpec((b, CHUNK, B_WIDTH), lambda n: (0, nc - 1 - n, 0))
    pspec = pl.BlockSpec((B_HEADS, b, CHUNK, CHUNK), lambda n: (0, 0, nc - 1 - n, 0))
    espec = pl.BlockSpec((B_HEADS, b, 8, 128), lambda n: (0, 0, nc - 1 - n, 0))
    outs = pl.pallas_call(
        body, name="dn_scan_bwd", grid=(nc,),
        in_specs=[act, act, act, act, pspec, espec,
                  pl.BlockSpec((b, None, B_HEADS, B_DIM, B_DIM), lambda n: (0, nc - 1 - n, 0, 0, 0)),
                  act],
        out_specs=[act, act, act, act, pspec, espec],
        out_shape=[jax.ShapeDtypeStruct((b, s, B_WIDTH), F32)] * 4
        + [jax.ShapeDtypeStruct((B_HEADS, b, s, CHUNK), F32),
           jax.ShapeDtypeStruct((B_HEADS, b, s // 8, 128), F32)],
        scratch_shapes=[pltpu.VMEM((b * B_HEADS, B_DIM, B_DIM), F32)],
        compiler_params=_cp(("arbitrary",), VMEM_LIMIT),
    )(r3(u), r3(wk), r3(qg), r3(kdec), p.reshape(B_HEADS, b, s, CHUNK), egl.reshape(B_HEADS, b, s // 8, 128),
      states, r3(do))
    return (*[a.reshape(t, B_WIDTH) for a in outs[:4]], outs[4].reshape(B_HEADS, t, CHUNK),
            outs[5].reshape(B_HEADS, t // 8, 128))


def dn_post_bwd(c, proj, al_row, dtb_row, tmat, dw, dwk, dqg, dkdec, dp, degl, b, s, ncb=16):
    t = b * s
    r = ncb * CHUNK
    nblk = t // r
    bd_blk = 0

    def body(cq_ref, ck_ref, cv_ref, bd_ref, al_ref, dtb_ref, tm_ref, dw_ref, dwk_ref, dqg_ref, dkdec_ref, dp_ref,
             degl_ref, dc_ref, dbd_ref, dal_ref, ddtb_ref):
        h = pl.program_id(1)

        @pl.when((pl.program_id(0) == 0) & (h == 0))
        def _():
            dal_ref[...] = jnp.zeros_like(dal_ref)
            ddtb_ref[...] = jnp.zeros_like(ddtb_ref)

        m = _dn_chunk_math(cq_ref[...], ck_ref[...], cv_ref[...], bd_ref[...].astype(F32), al_ref[...], dtb_ref[...], h, ncb,
                           tm=tm_ref[...].reshape(ncb, CHUNK, CHUNK))
        eye, low, strict = m["eye"], m["low"], m["strict"]
        eyef = eye.astype(F32)

        def c3(a):
            return a.reshape(ncb, CHUNK, a.shape[-1])

        du, dwkv, dqg, dkdec = c3(dw_ref[...]), c3(dwk_ref[...]), c3(dqg_ref[...]), c3(dkdec_ref[...])
        dpm = jnp.where(low, c3(dp_ref[...]), 0.0)
        degl = degl_ref[...].reshape(ncb, 8, 128)[:, 0:1, 0:1]
        beta, gam, kn, qn, v = m["beta"], m["gam"], m["kn"], m["qn"], m["v"]
        dm, kd, a, p = m["dm"], m["kd"], m["a"], m["p"]
        knb, qnb = _bf(kn), _bf(qn)

        eyeb = _bf(eyef)
        th, tl = _split(m["tm"])
        tts = (_bf(_bnt(eyeb, th)), _bf(_bnt(eyeb, tl)))
        xy = _bnn3(tts, jnp.concatenate([du, dwkv], axis=2))
        x, y = xy[:, :, :B_DIM], xy[:, :, B_DIM:]
        da = -jnp.where(strict, _bnt(_bf(x), _bf(m["u"])) + _bnt(_bf(y), _bf(m["wk"])), 0.0)
        dv = beta * x
        sy = jnp.sum(y * kn, axis=2, keepdims=True)
        dbeta = jnp.sum(x * v, axis=2, keepdims=True) + gam * sy + jnp.sum(da * kd, axis=2, keepdims=True)
        dgam = beta * sy + jnp.sum(dqg * qn, axis=2, keepdims=True)
        dkk = da * beta * dm
        dqk = dpm * dm
        dkkb, dqkb = _bf(dkk), _bf(dqk)
        dkn = ((beta * gam) * y + _bnn(dkkb, knb) + _bnn(_bf(_bnt(eyeb, dkkb)), knb)
               + _bnn(_bf(_bnt(eyeb, dqkb)), qnb) + dkdec * m["edec"])
        dqn = gam * dqg + _bnn(dqkb, knb)
        mm = da * a + dpm * p
        ek = jnp.sum(dkdec * m["kdec"], axis=2, keepdims=True)
        dgc = (jnp.sum(mm, axis=2, keepdims=True) - _col_of_row(jnp.sum(mm, axis=1, keepdims=True), eye)
               + dgam * gam - ek)
        dgl = jnp.sum(ek, axis=1, keepdims=True) + degl * m["egl"]
        i, _ = _chunk_masks(ncb)
        dgc = dgc + jnp.where(i[:, :, 0:1] == CHUNK - 1, dgl, 0.0)
        upper = (i <= _chunk_masks(ncb)[1]).astype(BF16)
        dg = _bnn_exact(upper, jnp.broadcast_to(dgc, (ncb, CHUNK, CHUNK)))[:, :, 0:1]

        nq = m["nq"]
        dnq = dqn * (B_DIM ** -0.5)
        dcq = m["rq"] * (dnq - nq * jnp.sum(nq * dnq, axis=2, keepdims=True))
        dck = m["rk"] * (dkn - kn * jnp.sum(kn * dkn, axis=2, keepdims=True))
        dc_ref[0] = dcq.reshape(r, B_DIM)
        dc_ref[1] = dck.reshape(r, B_DIM)
        dc_ref[2] = dv.reshape(r, B_DIM)

        dbraw = (dbeta * beta * (1.0 - beta)).reshape(r, 1)
        sgm = _sigmoid(m["sp_arg"])
        ddraw3 = dg * (-m["ea"]) * sgm
        ddraw = ddraw3.reshape(r, 1)
        lane = lax.broadcasted_iota(jnp.int32, (r, 128), 1)
        contrib = jnp.where(lane == h, dbraw, 0.0) + jnp.where(lane == B_HEADS + h, ddraw, 0.0)

        @pl.when(h == 0)
        def _():
            dbd_ref[...] = contrib

        @pl.when(h != 0)
        def _():
            dbd_ref[...] += contrib

        lane8 = lax.broadcasted_iota(jnp.int32, (8, 128), 1)
        tot_al = jnp.sum(jnp.sum(dg * m["g"], axis=1, keepdims=True), axis=0, keepdims=True).reshape(1, 1)
        tot_dtb = jnp.sum(jnp.sum(ddraw3, axis=1, keepdims=True), axis=0, keepdims=True).reshape(1, 1)
        dal_ref[...] += jnp.where(lane8 == h, tot_al, 0.0)
        ddtb_ref[...] += jnp.where(lane8 == h, tot_dtb, 0.0)

    col = lambda k: pl.BlockSpec((r, 128), lambda i, h: (i, k * B_HEADS + h))
    hcol = pl.BlockSpec((r, 128), lambda i, h: (i, h))
    small = pl.BlockSpec((1, 128), lambda i, h: (0, 0))
    acc = pl.BlockSpec((8, 128), lambda i, h: (0, 0))
    return pl.pallas_call(
        body, name="dn_post_bwd", grid=(nblk, B_HEADS),
        in_specs=[col(0), col(1), col(2), pl.BlockSpec((r, 128), lambda i, h: (i, bd_blk)), small, small,
                  pl.BlockSpec((None, r, CHUNK), lambda i, h: (h, i, 0)),
                  hcol, hcol, hcol, hcol,
                  pl.BlockSpec((None, r, CHUNK), lambda i, h: (h, i, 0)),
                  pl.BlockSpec((None, ncb * 8, 128), lambda i, h: (h, i, 0))],
        out_specs=[pl.BlockSpec((3, r, 128), lambda i, h: (0, i, h)),
                   pl.BlockSpec((r, 128), lambda i, h: (i, 0)), acc, acc],
        out_shape=[jax.ShapeDtypeStruct((3, t, B_WIDTH), F32), jax.ShapeDtypeStruct((t, 128), F32),
                   jax.ShapeDtypeStruct((8, 128), F32), jax.ShapeDtypeStruct((8, 128), F32)],
        compiler_params=_cp(("arbitrary", "arbitrary"), VMEM_LIMIT),
    )(c, c, c, proj, al_row, dtb_row, tmat, dw, dwk, dqg, dkdec, dp, degl)


def make_bias_band(rel_bias):
    tail = bias_tail(jnp.pad(rel_bias, ((0, 0), (0, 384 - N_REL))))
    far = jnp.broadcast_to(rel_bias[:, 2 * REL_CLIP][:, None, None], (A_HEADS, CHUNK, BAND - TAIL))
    band = jnp.concatenate([far, jnp.transpose(tail, (1, 0, 2))], axis=2)
    off = jnp.full((A_HEADS, CHUNK, CHUNK), -1e30, F32)
    both = jnp.stack([jnp.concatenate([band, off], axis=2), jnp.concatenate([off, band], axis=2)], axis=1)
    return both.reshape(4, 4 * CHUNK, WIN)


def bias_band_grad(dbt, dbf):
    t5 = dbt.reshape(A_HEADS, 2, CHUNK, 256)
    tail = t5[:, 0, :, :TAIL] + t5[:, 1, :, CHUNK:]
    far = dbf.reshape(A_HEADS, 2, CHUNK, 128).sum(axis=1) + jnp.pad(t5[:, 1, :, :CHUNK], ((0, 0), (0, 0), (0, CHUNK)))
    return bias_grad(jnp.transpose(tail, (1, 0, 2)), far)[:, :N_REL]


def _rms(x):
    r = lax.rsqrt(jnp.mean(x * x, axis=-1, keepdims=True) + EPS)
    return r, x * r


def _rms_bwd(dh, g, r, n):
    dn = dh * g
    return r * (dn - n * jnp.mean(dn * n, axis=-1, keepdims=True)), dh * n


def _gated_onorm(o, z, w_on):
    parts = []
    for h in range(B_HEADS):
        sl = slice(h * B_DIM, (h + 1) * B_DIM)
        r, n = _rms(o[:, sl])
        parts.append((r, n))
    r4 = [p[0] for p in parts]
    n4 = jnp.concatenate([p[1] for p in parts], axis=1)
    w4 = jnp.concatenate([w_on] * B_HEADS, axis=1)
    sz = _sigmoid(z)
    silu = z * sz
    return n4 * w4 * silu, r4, n4, w4, sz, silu


def mid_fwd(x, y_a, o_b, proj, w_on, wa, wb, w_out, tm=256):
    t = x.shape[0]
    tm = min(tm, t)

    def body(x_ref, ya_ref, ob_ref, z_ref, ga_ref, gb_ref, won_ref, wa_ref, wb_ref, wo_ref, x1_ref, mg_ref):
        yb = _gated_onorm(ob_ref[...], z_ref[...].astype(F32), won_ref[...])[0]
        ua = _nn(_bf(ya_ref[...]), wa_ref[...])
        ub = _nn(_bf(yb), wb_ref[...])
        merged = _sigmoid(ga_ref[...].astype(F32)) * ua + _sigmoid(gb_ref[...].astype(F32)) * ub
        mb = _bf(merged)
        mg_ref[...] = mb
        x1_ref[...] = x_ref[...] + _nn(mb, wo_ref[...])

    rowd = pl.BlockSpec((tm, D_MODEL), lambda i: (i, 0))
    row5 = pl.BlockSpec((tm, 512), lambda i: (i, 0))
    full = lambda a: pl.BlockSpec(a.shape, lambda i: (0,) * a.ndim)
    return pl.pallas_call(
        body, name="mid_fwd", grid=(t // tm,),
        in_specs=[rowd, row5, row5,
                  pl.BlockSpec((tm, 512), lambda i: (i, P_Z // 512)),
                  pl.BlockSpec((tm, D_MODEL), lambda i: (i, 0)),
                  pl.BlockSpec((tm, D_MODEL), lambda i: (i, 1)),
                  full(w_on), full(wa), full(wb), full(w_out)],
        out_specs=[rowd, rowd],
        out_shape=[jax.ShapeDtypeStruct((t, D_MODEL), F32), jax.ShapeDtypeStruct((t, D_MODEL), BF16)],
        compiler_params=_cp(("parallel",), VMEM_LIMIT),
    )(x, y_a, o_b, proj, proj, proj, w_on, wa, wb, w_out)


def mid_bwd(dx1, merged, y_a, o_b, proj, w_on, wa, wb, w_out, tm=256):
    t = dx1.shape[0]
    tm = min(tm, t)

    def body(dx1_ref, mg_ref, ya_ref, ob_ref, z_ref, ga_ref, gb_ref, won_ref, wa_ref, wb_ref, wo_ref,
             dya_ref, dob_ref, dz_ref, dg_ref, dwo_ref, dwa_ref, dwb_ref, dwon_ref):
        @pl.when(pl.program_id(0) == 0)
        def _():
            dwo_ref[...] = jnp.zeros_like(dwo_ref)
            dwa_ref[...] = jnp.zeros_like(dwa_ref)
            dwb_ref[...] = jnp.zeros_like(dwb_ref)
            dwon_ref[...] = jnp.zeros_like(dwon_ref)

        dx1b = _bf(dx1_ref[...])
        dmerged = _nt(dx1b, wo_ref[...])
        dwo_ref[...] += _tn(mg_ref[...], dx1b)
        o = ob_ref[...]
        z = z_ref[...].astype(F32)
        yb, r4, n4, w4, sz, silu = _gated_onorm(o, z, won_ref[...])
        yab, ybb = _bf(ya_ref[...]), _bf(yb)
        ua = _nn(yab, wa_ref[...])
        ub = _nn(ybb, wb_ref[...])
        sa, sb = _sigmoid(ga_ref[...].astype(F32)), _sigmoid(gb_ref[...].astype(F32))
        dua, dub = _bf(dmerged * sa), _bf(dmerged * sb)
        dg_ref[:, 0:D_MODEL] = _bf(dmerged * ua * sa * (1.0 - sa))
        dg_ref[:, D_MODEL:2 * D_MODEL] = _bf(dmerged * ub * sb * (1.0 - sb))
        dwa_ref[...] += _tn(yab, dua)
        dwb_ref[...] += _tn(ybb, dub)
        dya_ref[...] = _nt(dua, wa_ref[...])
        dyb = _nt(dub, wb_ref[...])
        dz_ref[...] = _bf(dyb * (n4 * w4) * (sz * (1.0 + z * (1.0 - sz))))
        dnw = dyb * silu
        dwon = jnp.zeros((1, B_DIM), F32)
        for h in range(B_HEADS):
            sl = slice(h * B_DIM, (h + 1) * B_DIM)
            dxh, dgh = _rms_bwd(dnw[:, sl], won_ref[...], r4[h], n4[:, sl])
            dob_ref[:, sl] = dxh
            dwon = dwon + jnp.sum(dgh, axis=0, keepdims=True)
        dwon_ref[...] += jnp.broadcast_to(dwon, (8, B_DIM))

    rowd = pl.BlockSpec((tm, D_MODEL), lambda i: (i, 0))
    row5 = pl.BlockSpec((tm, 512), lambda i: (i, 0))
    full = lambda a: pl.BlockSpec(a.shape, lambda i: (0,) * a.ndim)
    fixed = lambda shp: pl.BlockSpec(shp, lambda i: (0,) * len(shp))
    return pl.pallas_call(
        body, name="mid_bwd", grid=(t // tm,),
        in_specs=[rowd, rowd, row5, row5,
                  pl.BlockSpec((tm, 512), lambda i: (i, P_Z // 512)),
                  pl.BlockSpec((tm, D_MODEL), lambda i: (i, 0)),
                  pl.BlockSpec((tm, D_MODEL), lambda i: (i, 1)),
                  full(w_on), full(wa), full(wb), full(w_out)],
        out_specs=[row5, row5, row5, pl.BlockSpec((tm, 2 * D_MODEL), lambda i: (i, 0)),
                   fixed((D_MODEL, D_MODEL)), fixed((A_WIDTH, D_MODEL)), fixed((B_WIDTH, D_MODEL)),
                   fixed((8, B_DIM))],
        out_shape=[jax.ShapeDtypeStruct((t, 512), F32), jax.ShapeDtypeStruct((t, 512), F32),
                   jax.ShapeDtypeStruct((t, 512), BF16), jax.ShapeDtypeStruct((t, 2 * D_MODEL), BF16),
           jax.ShapeDtypeStruct((D_MODEL, D_MODEL), F32), jax.ShapeDtypeStruct((A_WIDTH, D_MODEL), F32),
           jax.ShapeDtypeStruct((B_WIDTH, D_MODEL), F32), jax.ShapeDtypeStruct((8, B_DIM), F32)],
        compiler_params=_cp(("arbitrary",), VMEM_LIMIT),
    )(dx1, merged, y_a, o_b, proj, proj, proj, w_on, wa, wb, w_out)


FFN_TF = 1408


def ffn_up(x1, g, w_gu, tm=512, tf=FFN_TF):
    t = x1.shape[0]
    tm = min(tm, t)
    nf = D_FF // tf

    def body(x_ref, g_ref, wg_ref, wu_ref, gate_ref, up_ref, act_ref, h_ref):
        @pl.when(pl.program_id(1) == 0)
        def _():
            r, n = _rms(x_ref[...])
            h_ref[...] = _bf(n * g_ref[...])

        hb = h_ref[...]
        gate = _nn(hb, wg_ref[...])
        up = _nn(hb, wu_ref[...])
        gate_ref[...] = _bf(gate)
        up_ref[...] = _bf(up)
        act_ref[...] = _bf(gate * _sigmoid(gate) * up)

    ff = pl.BlockSpec((tm, tf), lambda i, j: (i, j))
    return pl.pallas_call(
        body, name="ffn_up", grid=(t // tm, nf),
        in_specs=[pl.BlockSpec((tm, D_MODEL), lambda i, j: (i, 0)),
                  pl.BlockSpec((1, D_MODEL), lambda i, j: (0, 0)),
                  pl.BlockSpec((D_MODEL, tf), lambda i, j: (0, j)),
                  pl.BlockSpec((D_MODEL, tf), lambda i, j: (0, nf + j))],
        out_specs=[ff, ff, ff, pl.BlockSpec((tm, D_MODEL), lambda i, j: (i, 0))],
        out_shape=[jax.ShapeDtypeStruct((t, D_FF), BF16)] * 3 + [jax.ShapeDtypeStruct((t, D_MODEL), BF16)],
        compiler_params=_cp(("parallel", "arbitrary"), VMEM_LIMIT),
    )(x1, g, w_gu, w_gu)


def matmul_residual(a, w, res, name, tm=512, tk=FFN_TF):
    t, k = a.shape
    n = w.shape[1]
    tm = min(tm, t)

    def body(a_ref, w_ref, r_ref, o_ref):
        @pl.when(pl.program_id(1) == 0)
        def _():
            o_ref[...] = r_ref[...]

        o_ref[...] += _nn(a_ref[...], w_ref[...])

    return pl.pallas_call(
        body, name=name, grid=(t // tm, k // tk),
        in_specs=[pl.BlockSpec((tm, tk), lambda i, j: (i, j)),
                  pl.BlockSpec((tk, n), lambda i, j: (j, 0)),
                  pl.BlockSpec((tm, n), lambda i, j: (i, 0))],
        out_specs=pl.BlockSpec((tm, n), lambda i, j: (i, 0)),
        out_shape=jax.ShapeDtypeStruct((t, n), F32),
        compiler_params=_cp(("parallel", "arbitrary"), VMEM_LIMIT),
    )(a, w, res)


def ffn_act_bwd(dx2, gate, up, w_down, tm=512, tf=FFN_TF):
    t = dx2.shape[0]
    tm = min(tm, t)

    def body(dx2_ref, gate_ref, up_ref, wd_ref, dgate_ref, dup_ref, dx2b_ref):
        @pl.when(pl.program_id(1) == 0)
        def _():
            dx2b_ref[...] = _bf(dx2_ref[...])

        dact = _nt(dx2b_ref[...], wd_ref[...])
        gt, upv = gate_ref[...].astype(F32), up_ref[...].astype(F32)
        sg = _sigmoid(gt)
        dgate_ref[...] = _bf(dact * upv * (sg * (1.0 + gt * (1.0 - sg))))
        dup_ref[...] = _bf(dact * (gt * sg))

    ff = pl.BlockSpec((tm, tf), lambda i, j: (i, j))
    return pl.pallas_call(
        body, name="ffn_act_bwd", grid=(t // tm, D_FF // tf),
        in_specs=[pl.BlockSpec((tm, D_MODEL), lambda i, j: (i, 0)), ff, ff,
                  pl.BlockSpec((tf, D_MODEL), lambda i, j: (j, 0))],
        out_specs=[ff, ff],
        out_shape=[jax.ShapeDtypeStruct((t, D_FF), BF16)] * 2,
        scratch_shapes=[pltpu.VMEM((tm, D_MODEL), BF16)],
        compiler_params=_cp(("parallel", "arbitrary"), VMEM_LIMIT),
    )(dx2, gate, up, w_down)


def tail_fwd_bwd(x2, p, target, g_ple, g_final, w_pg, w_pp, tm=256):
    t = x2.shape[0]
    tm = min(tm, t)

    def body(x_ref, p_ref, t_ref, gp_ref, gf_ref, wpg_ref, wpp_ref,
             dx_ref, dwpg_ref, dwpp_ref, dgp_ref, dgf_ref, loss_ref):
        @pl.when(pl.program_id(0) == 0)
        def _():
            dwpg_ref[...] = jnp.zeros_like(dwpg_ref)
            dwpp_ref[...] = jnp.zeros_like(dwpp_ref)
            dgp_ref[...] = jnp.zeros_like(dgp_ref)
            dgf_ref[...] = jnp.zeros_like(dgf_ref)
            loss_ref[...] = jnp.zeros_like(loss_ref)

        x2v = x_ref[...]
        gp, gf = gp_ref[...], gf_ref[...]
        r3, n3 = _rms(x2v)
        h3b = _bf(n3 * gp)
        pb = _bf(p_ref[...])
        pg = _sigmoid(_nn(h3b, wpg_ref[...]))
        pp = _nn(pb, wpp_ref[...])
        x3 = x2v + pg * pp
        r4, n4 = _rms(x3)
        err = n4 * gf - t_ref[...]
        part = 0.5 * jnp.sum(jnp.sum(err * err, axis=1, keepdims=True), axis=0, keepdims=True) / D_MODEL
        loss_ref[...] += jnp.broadcast_to(part, (8, 128))
        dy = err * (1.0 / D_MODEL)
        dx3, dgf = _rms_bwd(dy, gf, r4, n4)
        dgf_ref[...] += jnp.broadcast_to(jnp.sum(dgf, axis=0, keepdims=True), (8, D_MODEL))
        dzp = _bf(dx3 * pp * pg * (1.0 - pg))
        dpp = _bf(dx3 * pg)
        dwpg_ref[...] += _tn(h3b, dzp)
        dwpp_ref[...] += _tn(pb, dpp)
        dh3 = _nt(dzp, wpg_ref[...])
        dx, dgp = _rms_bwd(dh3, gp, r3, n3)
        dgp_ref[...] += jnp.broadcast_to(jnp.sum(dgp, axis=0, keepdims=True), (8, D_MODEL))
        dx_ref[...] = dx3 + dx

    rowd = pl.BlockSpec((tm, D_MODEL), lambda i: (i, 0))
    fixed = lambda shp: pl.BlockSpec(shp, lambda i: (0,) * len(shp))
    return pl.pallas_call(
        body, name="tail_fwd_bwd", grid=(t // tm,),
        in_specs=[rowd, pl.BlockSpec((tm, PLE_DIM), lambda i: (i, 0)), rowd,
                  fixed((1, D_MODEL)), fixed((1, D_MODEL)), fixed((D_MODEL, D_MODEL)), fixed((PLE_DIM, D_MODEL))],
        out_specs=[rowd, fixed((D_MODEL, D_MODEL)), fixed((PLE_DIM, D_MODEL)),
                   fixed((8, D_MODEL)), fixed((8, D_MODEL)), fixed((8, 128))],
        out_shape=[jax.ShapeDtypeStruct((t, D_MODEL), F32), jax.ShapeDtypeStruct((D_MODEL, D_MODEL), F32),
                   jax.ShapeDtypeStruct((PLE_DIM, D_MODEL), F32), jax.ShapeDtypeStruct((8, D_MODEL), F32),
                   jax.ShapeDtypeStruct((8, D_MODEL), F32), jax.ShapeDtypeStruct((8, 128), F32)],
        compiler_params=_cp(("arbitrary",), VMEM_LIMIT),
    )(x2, p, target, g_ple, g_final, w_pg, w_pp)


def in_proj_bwd(pieces, weights, x, dx1, g, name="in_proj_bwd", tm=256):
    t = x.shape[0]
    tm = min(tm, t)
    k = len(pieces)
    assert all(c0 % wd == 0 and w0 % wd == 0 for (_, c0, wd), (_, w0) in zip(pieces, weights))

    def body(*refs):
        p_refs, w_refs = refs[:k], refs[k:2 * k]
        x_ref, dx1_ref, g_ref, dx_ref, dg_ref = refs[2 * k:]

        @pl.when(pl.program_id(0) == 0)
        def _():
            dg_ref[...] = jnp.zeros_like(dg_ref)

        dh = _nt(_bf(p_refs[0][...]), w_refs[0][...])
        for pr, wr in zip(p_refs[1:], w_refs[1:]):
            dh = dh + _nt(_bf(pr[...]), wr[...])
        r, n = _rms(x_ref[...])
        dx, dgc = _rms_bwd(dh, g_ref[...], r, n)
        dx_ref[...] = dx1_ref[...] + dx
        dg_ref[...] += jnp.broadcast_to(jnp.sum(dgc, axis=0, keepdims=True), (8, D_MODEL))

    rowd = pl.BlockSpec((tm, D_MODEL), lambda i: (i, 0))
    return pl.pallas_call(
        body, name=name, grid=(t // tm,),
        in_specs=[pl.BlockSpec((tm, wd), functools.partial(lambda i, cb: (i, cb), cb=c0 // wd))
                  for _, c0, wd in pieces]
        + [pl.BlockSpec((w.shape[0], wd), functools.partial(lambda i, cb: (0, cb), cb=w0 // wd))
           for (w, w0), (_, _, wd) in zip(weights, pieces)]
        + [rowd, rowd, pl.BlockSpec((1, D_MODEL), lambda i: (0, 0))],
        out_specs=[rowd, pl.BlockSpec((8, D_MODEL), lambda i: (0, 0))],
        out_shape=[jax.ShapeDtypeStruct((t, D_MODEL), F32), jax.ShapeDtypeStruct((8, D_MODEL), F32)],
        compiler_params=_cp(("arbitrary",), VMEM_LIMIT),
    )(*[a for a, _, _ in pieces], *[w for w, _ in weights], x, dx1, g)


def adamw(w, g, m, v, name, rows_cap=256, dep=None):
    lead = w.shape[:-2]
    r, c = w.shape[-2:]
    tr = r
    for cand in range(8, min(r, rows_cap) + 1, 8):
        if r % cand == 0:
            tr = cand

    def body(w_ref, g_ref, m_ref, v_ref, *rest):
        d_ref, mo_ref, vo_ref = rest[-3:]
        gv = g_ref[...]
        mn = ADAM_B1 * m_ref[...] + (1.0 - ADAM_B1) * gv
        vn = ADAM_B2 * v_ref[...] + (1.0 - ADAM_B2) * (gv * gv)
        m_hat = mn / (1.0 - ADAM_B1 ** ADAM_STEP)
        v_hat = vn / (1.0 - ADAM_B2 ** ADAM_STEP)
        d_ref[...] = -ADAM_LR * (m_hat / (jnp.sqrt(v_hat) + ADAM_EPS) + ADAM_WD * w_ref[...])
        mo_ref[...] = mn
        vo_ref[...] = vn

    spec = pl.BlockSpec((None,) * len(lead) + (tr, c), lambda i: (0,) * len(lead) + (i, 0))
    extra = [] if dep is None else [dep]
    return pl.pallas_call(
        body, name=name, grid=(r // tr,),
        in_specs=[spec] * 4 + [pl.BlockSpec((8, 128), lambda i: (0, 0))] * len(extra), out_specs=[spec] * 3,
        out_shape=[jax.ShapeDtypeStruct(w.shape, F32)] * 3,
        compiler_params=_cp(("parallel",), VMEM_LIMIT),
    )(w, g.reshape(w.shape), m, v, *extra)


def _w_in_shards(dwp):
    cs = D_IN // N_CHIPS
    regions = ((0, SPLIT_Z, P_QA), (SPLIT_Z, SPLIT_Z + 8, P_BD - SPLIT_Z), (SPLIT_Z + 8, D_IN, -(SPLIT_Z + 8)))

    def original(lo, hi):
        parts = [dwp[:, max(lo, a) + off:min(hi, e) + off] for a, e, off in regions if max(lo, a) < min(hi, e)]
        return parts[0] if len(parts) == 1 else jnp.concatenate(parts, axis=1)

    return jnp.stack([original(s * cs, (s + 1) * cs) for s in range(N_CHIPS)])


class Standalone:
    def __init__(self, later_weights):
        self.later_weights = later_weights

    def begin(self, *a):
        return 0.0

    forward = exchange = join = begin

    def finish(self, after):
        return self.later_weights


def local_step(x3d, p3d, target3d, g4, small, later, early):
    b, s, _ = x3d.shape
    t = b * s
    x = x3d.reshape(t, D_MODEL)
    p = p3d.reshape(t, PLE_DIM)
    target = target3d.reshape(t, D_MODEL)
    cut = SPLIT_Z - 2 * (D_IN // N_CHIPS)
    w_inp = jnp.concatenate([g4[2][:, cut + 8:], g4[3], g4[0], g4[1], g4[2][:, :cut], g4[2][:, cut:cut + 8],
                             jnp.zeros((D_MODEL, 120), BF16)], axis=1)
    al_row = jnp.pad(small["a_log"].reshape(1, B_HEADS), ((0, 0), (0, 128 - B_HEADS)))
    dtb_row = jnp.pad(small["dt_bias"].reshape(1, B_HEADS), ((0, 0), (0, 128 - B_HEADS)))
    conv_w8 = jnp.pad(small["conv_w"].reshape(CONV_K, CONV_CH), ((0, 8 - CONV_K), (0, 0)))
    w_on = small["w_onorm"].reshape(1, B_DIM)
    g_mix, g_ffn = small["g_mix"].reshape(1, D_MODEL), small["g_ffn"].reshape(1, D_MODEL)
    g_ple, g_final = small["g_ple"].reshape(1, D_MODEL), small["g_final"].reshape(1, D_MODEL)
    bias_band = make_bias_band(small["rel_bias"].reshape(A_HEADS, N_REL))

    tok = later.begin()
    proj, h1, bd32 = rms_matmul(x, g_mix + tok, w_inp, "in_proj", tm=1024, tn_cap=1152)
    y_a, lse = attn_fwd(proj, bias_band, b, s)
    tok = later.forward(lse)
    c = conv_fwd(proj, conv_w8 + tok, b, s)
    u, wk, qg, kdec, pm, egl, tmat = dn_prep(c, bd32, al_row, dtb_row, b, s)
    o_b, states = dn_scan_fwd(u, wk, qg, kdec, pm, egl, b, s)
    wts = later.finish(o_b)
    x1, merged = mid_fwd(x, y_a, o_b, proj, w_on, wts["w_branch_a"], wts["w_branch_b"], wts["w_out"])
    gate, up, act, h2 = ffn_up(x1, g_ffn, wts["w_gate_up"])
    x2 = matmul_residual(act, wts["w_down"], x1, "ffn_down")

    dx2, dw_pg, dw_pp, dg_ple, dg_final, loss = tail_fwd_bwd(
        x2, p, target, g_ple, g_final, wts["w_ple_gate"], wts["w_ple_proj"])
    dgate, dup = ffn_act_bwd(dx2, gate, up, wts["w_down"])
    w_gu = wts["w_gate_up"]
    dx1, dg_ffn = in_proj_bwd([(dgate, 0, D_FF), (dup, 0, D_FF)], [(w_gu, 0), (w_gu, D_FF)], x1, dx2, g_ffn,
                              name="ffn_in_bwd")
    dw_down = matmul_tn(act, dx2, "dw_down")
    dw_gu = matmul_tn(h2, dgate, "dw_gate", width=2 * D_FF)
    dw_gu = matmul_tn(h2, dup, "dw_up", into=dw_gu, col0=D_FF)
    dy_a, do_b, dz, dgates, dw_out, dwa, dwb, dw_on = mid_bwd(
        dx1, merged, y_a, o_b, proj, w_on, wts["w_branch_a"], wts["w_branch_b"], wts["w_out"])
    tok = early.begin(dict(w_branch_a=dwa, w_branch_b=dwb, w_out=dw_out, w_gate_up=dw_gu, w_down=dw_down,
                           w_ple_gate=dw_pg, w_ple_proj=dw_pp))
    ddw, ddwk, ddqg, ddkdec, ddp, ddegl = dn_scan_bwd(u, wk, qg, kdec, pm, egl + tok, states, do_b, b, s)
    tok = early.exchange(ddegl)
    dc3, dbd, dal, ddtb = dn_post_bwd(c, bd32, al_row + tok, dtb_row, tmat, ddw, ddwk, ddqg, ddkdec, ddp, ddegl, b, s)
    dconv, dconv_w = conv_bwd(proj, conv_w8, dc3, b, s)
    dqa, dka, dva, dbt, dbf = attn_bwd(proj, bias_band, y_a, lse, dy_a, b, s)
    tok = early.join(dqa)
    d_rel = bias_band_grad(dbt, dbf)

    pieces = [dgates, dqa, dka, dva, dconv, dz, dbd]
    bounds = [0, 2048, 2560, 3072, 3584, 5120, 5632, 5760]
    windows = [(dgates, 0, 2048), (dqa, 0, 512), (dka, 0, 512), (dva, 0, 512), (dconv, 0, 512), (dconv, 512, 512),
               (dconv, 1024, 512), (dz, 0, 512), (dbd, 0, 128)]
    w_cols = [0, P_QA, P_KA, P_VA, P_CONV, P_CONV + 512, P_CONV + 1024, P_Z, P_BD]
    dx, dg_mix = in_proj_bwd(windows, [(w_inp, c0) for c0 in w_cols], x, dx1, g_mix + tok)
    dwp = None
    for k, pc in enumerate(pieces):
        dwp = matmul_tn(h1, pc, "dw_in_%d" % k, into=dwp, col0=bounds[k], width=P_WIDTH)
    reduced_early = early.finish(dwp)
    dw_in = _w_in_shards(dwp)

    grads = dict(w_in=dw_in, w_branch_a=dwa, w_branch_b=dwb, w_out=dw_out, w_gate_up=dw_gu, w_down=dw_down,
                 w_ple_gate=dw_pg, w_ple_proj=dw_pp)
    small_grads = dict(g_mix=dg_mix[0], g_ffn=dg_ffn[0], g_ple=dg_ple[0], g_final=dg_final[0],
                       conv_w=dconv_w[:CONV_K].reshape(-1), rel_bias=d_rel.reshape(-1), w_onorm=dw_on[0],
                       a_log=dal[0, :B_HEADS], dt_bias=ddtb[0, :B_HEADS], loss=loss[0, :1])
    return dx.reshape(b, s, D_MODEL), grads, small_grads, reduced_early


BIG = (("w_in", (D_MODEL, D_IN), 1), ("w_branch_a", (A_WIDTH, D_MODEL), 1), ("w_branch_b", (B_WIDTH, D_MODEL), 1),
       ("w_out", (D_MODEL, D_MODEL), 0), ("w_gate_up", (D_MODEL, 2 * D_FF), 1), ("w_down", (D_FF, D_MODEL), 0),
       ("w_ple_gate", (D_MODEL, D_MODEL), 0), ("w_ple_proj", (PLE_DIM, D_MODEL), 1))
N_CHIPS = 4
FIRST_WEIGHTS = ("w_in",)
LATER_WEIGHTS = ("w_branch_a", "w_branch_b", "w_out", "w_gate_up", "w_down", "w_ple_gate", "w_ple_proj")
LATE_GRADS = ("w_in",)
EARLY_GRADS = ("w_branch_a", "w_branch_b", "w_out", "w_gate_up", "w_down", "w_ple_gate", "w_ple_proj")


def _items(names):
    return [it for it in BIG if it[0] in names]


def _shard_shape(shape, axis):
    return (shape[0] // N_CHIPS, shape[1]) if axis == 0 else (shape[0], shape[1] // N_CHIPS)


def _width_groups(names):
    groups = {}
    for n, shape, axis in _items(names):
        rs, cs = _shard_shape(shape, axis)
        groups.setdefault(cs, []).append((n, rs))
    return sorted(groups.items())


def grad_buffers(grads, names):
    info = {n: (shape, axis) for n, shape, axis in _items(names)}
    bufs = []
    for cs, members in _width_groups(names):
        segs = []
        for n, rs in members:
            g = grads[n].astype(BF16)
            if g.ndim == 2:
                g = (g.reshape(N_CHIPS, rs, cs) if info[n][1] == 0
                     else jnp.transpose(g.reshape(rs, N_CHIPS, cs), (1, 0, 2)))
            segs.append(g)
        bufs.append(segs[0] if len(segs) == 1 else jnp.concatenate(segs, axis=1))
    return bufs


def split_buffers(reduced, names):
    out = {}
    for (cs, members), buf in zip(_width_groups(names), reduced):
        r0 = 0
        for n, rs in members:
            out[n] = buf[r0:r0 + rs]
            r0 += rs
    return out


def _place():
    return lax.axis_index("x"), lax.axis_index("y"), lax.axis_index("c")


ANY = pl.BlockSpec(memory_space=pl.ANY)


def _gathered_shape(item):
    n, shape, _ = item
    return (N_CHIPS,) + _shard_shape(shape, 1) if n == "w_in" else shape


def _gather_block(o_ref, item, cx, cy, hf):
    n, shape, axis = item
    rs, cs = _shard_shape(shape, axis)
    hr = rs // 2
    ci = 2 * cx + cy
    if n == "w_in":
        return o_ref.at[ci, pl.ds(pl.multiple_of(hf * hr, 16), hr), :]
    if axis == 0:
        return o_ref.at[pl.ds(pl.multiple_of(ci * rs + hf * hr, 16), hr), :]
    return o_ref.at[pl.ds(pl.multiple_of(hf * hr, 16), hr), pl.ds(pl.multiple_of(ci * cs, 128), cs)]


def _own_half(w_ref, item, c):
    hr = _shard_shape(item[1], item[2])[0] // 2
    return w_ref.at[pl.ds(pl.multiple_of(c * hr, 16), hr), :]


def _gather_slot(o_ref, item, cx, cy):
    n, shape, axis = item
    rs, cs = _shard_shape(shape, axis)
    ci = 2 * cx + cy
    if n == "w_in":
        return o_ref.at[ci]
    if axis == 0:
        return o_ref.at[pl.ds(pl.multiple_of(ci * rs, 16), rs), :]
    return o_ref.at[:, pl.ds(pl.multiple_of(ci * cs, 128), cs)]


def _other_chips(x, y):
    return [(1 - x, y), (x, 1 - y), (1 - x, 1 - y)]


def allgather_weights(shards, names, chip):
    items = _items(names)
    nw = len(items)

    def body(*refs):
        w_refs, o_refs = refs[:nw], refs[nw:2 * nw]
        send_sems, recv_sems = refs[2 * nw:]
        x, y, c = _place()
        sibling = (x, y, 1 - c)
        chips = _other_chips(x, y)

        def copy(k, src, dst, to):
            return pltpu.make_async_remote_copy(src_ref=src, dst_ref=dst, send_sem=send_sems.at[k],
                                                recv_sem=recv_sems.at[k], device_id=to, device_id_type=MESH)

        def blk(i, cx, cy, hf):
            return _gather_block(o_refs[i], items[i], cx, cy, hf)

        def my_half(i):
            return _own_half(w_refs[i], items[i], c)

        def own(i):
            return _gather_slot(o_refs[i], items[i], x, y)

        first = [copy(7 * i + j, my_half(i), blk(i, x, y, c), (*chip_, c))
                 for i in range(nw) for j, chip_ in enumerate(chips)]
        first += [copy(7 * i + 6, w_refs[i], own(i), sibling) for i in range(nw)]
        for cp in first:
            cp.start()
        passed = []
        for i in range(nw):
            for j, chip_ in enumerate(chips):
                copy(7 * i + j, my_half(i), blk(i, *chip_, c), (*chip_, c)).wait_recv()
                fwd = copy(7 * i + 3 + j, blk(i, *chip_, c), blk(i, *chip_, c), sibling)
                fwd.start()
                passed.append(fwd)
        for i in range(nw):
            for j, chip_ in enumerate(chips):
                copy(7 * i + 3 + j, my_half(i), blk(i, *chip_, 1 - c), sibling).wait_recv()
            copy(7 * i + 6, w_refs[i], own(i), sibling).wait_recv()
        for cp in first + passed:
            cp.wait_send()

    outs = pl.pallas_call(
        body, name="allgather_weights",
        in_specs=[ANY] * nw, out_specs=[ANY] * nw,
        out_shape=[jax.ShapeDtypeStruct(_gathered_shape(it), BF16) for it in items],
        scratch_shapes=[pltpu.SemaphoreType.DMA((7 * nw,)), pltpu.SemaphoreType.DMA((7 * nw,))],
    )(*[shards[it[0]] for it in items])
    return {it[0]: o for it, o in zip(items, outs)}


HBM_SPEC = pl.BlockSpec(memory_space=pltpu.HBM)
SEM_SPEC = pl.BlockSpec(memory_space=pltpu.SEMAPHORE)
EFFECT = pltpu.SideEffectType.DATAFLOW_SIDE_EFFECTING


def _in_hbm(a):
    return pltpu.with_memory_space_constraint(a, pltpu.HBM)


def copies_start(name, bufs, ncopies, plan):
    nb = len(bufs)

    def body(*refs):
        in_refs, send_sems, recv_sems, token = refs[:nb], refs[nb], refs[nb + 1], refs[-1]
        for k, (src, dst, to) in enumerate(plan(in_refs)):
            pltpu.make_async_remote_copy(src_ref=src, dst_ref=dst, send_sem=send_sems.at[k],
                                         recv_sem=recv_sems.at[k], device_id=to, device_id_type=MESH).start()
        token[...] = jnp.zeros_like(token)

    outs = pl.pallas_call(
        body, name=name,
        in_specs=[HBM_SPEC] * nb,
        out_specs=(SEM_SPEC, SEM_SPEC, *[HBM_SPEC] * nb, pl.BlockSpec(memory_space=pltpu.VMEM)),
        out_shape=(pltpu.SemaphoreType.DMA((ncopies,)), pltpu.SemaphoreType.DMA((ncopies,)),
                   *[pltpu.HBM(b.shape, b.dtype) for b in bufs], jax.ShapeDtypeStruct((8, 128), F32)),
        input_output_aliases={i: 2 + i for i in range(nb)},
        compiler_params=pltpu.CompilerParams(has_side_effects=EFFECT),
    )(*[_in_hbm(b) for b in bufs])
    return outs[0], outs[1], list(outs[2:2 + nb]), outs[-1][0, 0]


def copies_wait(name, send_sems, recv_sems, bufs, after, plan):
    nb = len(bufs)

    def body(*refs):
        in_refs, s_sems, r_sems = refs[:nb], refs[nb], refs[nb + 1]
        for k, (src, dst, to) in enumerate(plan(in_refs)):
            cp = pltpu.make_async_remote_copy(src_ref=src, dst_ref=dst, send_sem=s_sems.at[k],
                                              recv_sem=r_sems.at[k], device_id=to, device_id_type=MESH)
            cp.wait_send()
            cp.wait_recv()

    return list(pl.pallas_call(
        body, name=name,
        in_specs=[HBM_SPEC] * nb + [SEM_SPEC, SEM_SPEC, ANY],
        out_specs=tuple([HBM_SPEC] * nb),
        out_shape=tuple(pltpu.HBM(b.shape, b.dtype) for b in bufs),
        input_output_aliases={i: i for i in range(nb)},
        compiler_params=pltpu.CompilerParams(has_side_effects=EFFECT),
    )(*bufs, send_sems, recv_sems, after))


def _landing(shape, dtype):
    return _in_hbm(lax.empty(shape, dtype))


class LaterWeights:
    def __init__(self, shards, chip):
        self.items = _items(LATER_WEIGHTS)
        self.shards, self.chip = shards, chip
        self.nw = len(self.items)

    def _ici_plan(self, refs):
        x, y, c = _place()
        w_refs, o_refs = refs[:self.nw], refs[self.nw:]
        plan = [(_own_half(w_refs[i], it, c), _gather_block(o_refs[i], it, x, y, c), (*chip_, c))
                for i, it in enumerate(self.items) for chip_ in _other_chips(x, y)]
        return plan + [(w_refs[i], _gather_slot(o_refs[i], it, x, y), (x, y, 1 - c))
                       for i, it in enumerate(self.items)]

    def _d2d_plan(self, refs):
        x, y, c = _place()
        return [(_gather_block(refs[i], it, *chip_, c), _gather_block(refs[i], it, *chip_, c), (x, y, 1 - c))
                for i, it in enumerate(self.items) for chip_ in _other_chips(x, y)]

    def _d2d_wait_plan(self, refs):
        x, y, c = _place()
        return [(_gather_block(refs[i], it, *chip_, c), _gather_block(refs[i], it, *chip_, 1 - c), (x, y, 1 - c))
                for i, it in enumerate(self.items) for chip_ in _other_chips(x, y)]

    def _ici_wait_plan(self, refs):
        x, y, c = _place()
        w_refs, o_refs = refs[:self.nw], refs[self.nw:]
        plan = [(_own_half(w_refs[i], it, c), _gather_block(o_refs[i], it, *chip_, c), (*chip_, c))
                for i, it in enumerate(self.items) for chip_ in _other_chips(x, y)]
        return plan + [(w_refs[i], _gather_slot(o_refs[i], it, x, y), (x, y, 1 - c))
                       for i, it in enumerate(self.items)]

    def begin(self):
        srcs = [self.shards[it[0]] for it in self.items]
        lands = [_landing(_gathered_shape(it), BF16) for it in self.items]
        self.s1, self.r1, self.b1, tok = copies_start("gather_ici_start", srcs + lands, 4 * self.nw, self._ici_plan)
        return tok

    def forward(self, after):
        b1 = copies_wait("gather_ici_wait", self.s1, self.r1, self.b1, after, self._ici_wait_plan)
        self.s2, self.r2, self.b2, tok = copies_start("gather_d2d_start", b1[self.nw:], 3 * self.nw, self._d2d_plan)
        return tok

    def finish(self, after):
        outs = copies_wait("gather_d2d_wait", self.s2, self.r2, self.b2, after, self._d2d_wait_plan)
        return {it[0]: o for it, o in zip(self.items, outs)}


def small_allreduce(v, name):
    r = v.shape[0]

    def body(v_ref, o_ref, buf, send_sems, recv_sems):
        x, y, c = _place()
        me = 4 * x + 2 * y + c
        buf[me] = v_ref[...]
        flips = [(fx, fy, fc) for fx in (0, 1) for fy in (0, 1) for fc in (0, 1)][1:]
        peers = [((1 - x) if fx else x, (1 - y) if fy else y, (1 - c) if fc else c) for fx, fy, fc in flips]

        def copy(k, slot, to):
            return pltpu.make_async_remote_copy(src_ref=v_ref, dst_ref=buf.at[slot], send_sem=send_sems.at[k],
                                                recv_sem=recv_sems.at[k], device_id=to, device_id_type=MESH)

        sends = [copy(k, me, peer) for k, peer in enumerate(peers)]
        for cp in sends:
            cp.start()
        for k, (px, py, pc) in enumerate(peers):
            copy(k, 4 * px + 2 * py + pc, (px, py, pc)).wait_recv()
        for cp in sends:
            cp.wait_send()
        acc = buf[0]
        for d in range(1, 8):
            acc = acc + buf[d]
        o_ref[...] = acc

    return pl.pallas_call(
        body, name=name,
        in_specs=[pl.BlockSpec(memory_space=pltpu.VMEM)], out_specs=pl.BlockSpec(memory_space=pltpu.VMEM),
        out_shape=jax.ShapeDtypeStruct((r, 128), F32),
        scratch_shapes=[pltpu.VMEM((8, r, 128), F32), pltpu.SemaphoreType.DMA((7,)), pltpu.SemaphoreType.DMA((7,))],
    )(v)


def add_halves(g, other, place):
    half, wd = other.shape[1:]
    tr = _tile_rows(half, wd)
    nblk = half // tr

    def body(pref, g0, g1, g2, g3, o0, o1, o2, o3, pf_ref, pb_ref):
        f = lambda r: r[...].astype(F32)
        pf_ref[...] = f(g0) + f(o0)
        pb_ref[0] = _bf(f(g1) + f(o1))
        pb_ref[1] = _bf(f(g2) + f(o2))
        pb_ref[2] = _bf(f(g3) + f(o3))

    gspec = lambda k: pl.BlockSpec((None, tr, wd), lambda i, pr: ((pr[0] + k) % N_CHIPS, pr[1] * nblk + i, 0))
    ospec = lambda k: pl.BlockSpec((None, tr, wd), lambda i, pr: ((pr[0] + k) % N_CHIPS, i, 0))
    return pl.pallas_call(
        body, name="add_halves",
        grid_spec=pltpu.PrefetchScalarGridSpec(
            num_scalar_prefetch=1, grid=(nblk,),
            in_specs=[gspec(0), gspec(1), gspec(2), gspec(3), ospec(0), ospec(1), ospec(2), ospec(3)],
            out_specs=[pl.BlockSpec((tr, wd), lambda i, pr: (i, 0)),
                       pl.BlockSpec((3, tr, wd), lambda i, pr: (0, i, 0))]),
        out_shape=[jax.ShapeDtypeStruct((half, wd), F32), jax.ShapeDtypeStruct((3, half, wd), BF16)],
        compiler_params=_cp(("parallel",), VMEM_LIMIT),
    )(place, g, g, g, g, other, other, other, other)


def _tile_rows(n, width):
    best = 16
    for t in range(16, max(16, (384 * 1024) // width) + 1, 16):
        if n % t == 0:
            best = t
    assert n % best == 0
    return best


def add_partials(pf, got, place):
    half, wd = pf.shape
    tr = _tile_rows(half, wd)

    def body(pref, pf_ref, got_ref, o_ref):
        o_ref[...] = ((pf_ref[...] + got_ref[0].astype(F32)) + got_ref[1].astype(F32)) + got_ref[2].astype(F32)

    return pl.pallas_call(
        body, name="add_partials",
        grid_spec=pltpu.PrefetchScalarGridSpec(
            num_scalar_prefetch=1, grid=(half // tr,),
            in_specs=[pl.BlockSpec((tr, wd), lambda i, pr: (i, 0)),
                      pl.BlockSpec((3, tr, wd), lambda i, pr: (0, i, 0))],
            out_specs=pl.BlockSpec((None, tr, wd), lambda i, pr: (pr[1], i, 0))),
        out_shape=jax.ShapeDtypeStruct((2, half, wd), F32),
        compiler_params=_cp(("parallel",), VMEM_LIMIT),
    )(place, pf, got)


class GradReduce:
    def __init__(self, place, names, tag):
        self.place, self.names, self.tag = place, names, tag
        self.nb = len(_width_groups(names))

    def _swap_plan(self, refs):
        x, y, c = _place()
        plan = []
        for g_ref, o_ref in zip(refs[:self.nb], refs[self.nb:]):
            half = o_ref.shape[1]
            plan.append((g_ref.at[:, pl.ds(pl.multiple_of((1 - c) * half, 16), half), :], o_ref, (x, y, 1 - c)))
        return plan

    def _exchange_plan(self, refs):
        x, y, c = _place()
        me = 2 * x + y
        return [(p_ref.at[k - 1], o_ref.at[k - 1], (((me + k) % N_CHIPS) // 2, ((me + k) % N_CHIPS) % 2, c))
                for p_ref, o_ref in zip(refs[:self.nb], refs[self.nb:]) for k in range(1, N_CHIPS)]

    def _join_plan(self, refs):
        x, y, c = _place()
        return [(r.at[c], r.at[c], (x, y, 1 - c)) for r in refs]

    def _join_wait_plan(self, refs):
        x, y, c = _place()
        return [(r.at[c], r.at[1 - c], (x, y, 1 - c)) for r in refs]

    def begin(self, grads):
        gs = grad_buffers(grads, self.names)
        lands = [_landing((N_CHIPS, g.shape[1] // 2, g.shape[2]), BF16) for g in gs]
        self.s1, self.r1, self.b1, tok = copies_start(self.tag + "_swap_start", gs + lands, self.nb, self._swap_plan)
        return tok

    def exchange(self, after):
        b1 = copies_wait(self.tag + "_swap_wait", self.s1, self.r1, self.b1, after, self._swap_plan)
        sums = [add_halves(g, other, self.place) for g, other in zip(b1[:self.nb], b1[self.nb:])]
        self.pfs = [pf for pf, _ in sums]
        pbs = [pb for _, pb in sums]
        lands = [_landing(pb.shape, BF16) for pb in pbs]
        self.s2, self.r2, self.b2, tok = copies_start(self.tag + "_exchange_start", pbs + lands, 3 * self.nb,
                                                      self._exchange_plan)
        return tok

    def join(self, after):
        b2 = copies_wait(self.tag + "_exchange_wait", self.s2, self.r2, self.b2, after, self._exchange_plan)
        boths = [add_partials(pf, got, self.place) for pf, got in zip(self.pfs, b2[self.nb:])]
        self.s3, self.r3, self.b3, tok = copies_start(self.tag + "_join_start", boths, self.nb, self._join_plan)
        return tok

    def finish(self, after):
        boths = copies_wait(self.tag + "_join_wait", self.s3, self.r3, self.b3, after, self._join_wait_plan)
        return split_buffers([b.reshape(-1, b.shape[2]) for b in boths], self.names)


SMALL = (("g_mix", D_MODEL), ("g_ffn", D_MODEL), ("g_ple", D_MODEL), ("g_final", D_MODEL),
         ("conv_w", CONV_K * CONV_CH), ("rel_bias", A_HEADS * N_REL), ("w_onorm", B_DIM),
         ("a_log", B_HEADS), ("dt_bias", B_HEADS), ("loss", 1))


def _pad128(v):
    v = v.reshape(-1)
    return jnp.pad(v, (0, -v.shape[0] % 128))


def pack_small(d, names, rows):
    flat = jnp.concatenate([_pad128(d[n]) for n in names]).reshape(-1, 128)
    return jnp.pad(flat, ((0, rows - flat.shape[0]), (0, 0)))


def unpack_small(flat, names_sizes):
    out, r0 = {}, 0
    v = flat.reshape(-1)
    for n, size in names_sizes:
        out[n] = v[r0:r0 + size]
        r0 += -(-size // 128) * 128
    return out


def kernel(x, p, g_mix, w_in, conv_w, a_log, dt_bias, rel_bias, w_onorm, w_branch_a, w_branch_b, w_out, g_ffn, w_gate_up, w_down, g_ple, w_ple_gate, w_ple_proj, g_final, loss_target, m_g_mix, m_w_in, m_conv_w, m_a_log, m_dt_bias, m_rel_bias, m_w_onorm, m_w_branch_a, m_w_branch_b, m_w_out, m_g_ffn, m_w_gate_up, m_w_down, m_g_ple, m_w_ple_gate, m_w_ple_proj, m_g_final, v_g_mix, v_w_in, v_conv_w, v_a_log, v_dt_bias, v_rel_bias, v_w_onorm, v_w_branch_a, v_w_branch_b, v_w_out, v_g_ffn, v_w_gate_up, v_w_down, v_g_ple, v_w_ple_gate, v_w_ple_proj, v_g_final):
    names = ["g_mix", "w_in", "conv_w", "a_log", "dt_bias", "rel_bias", "w_onorm", "w_branch_a", "w_branch_b",
             "w_out", "g_ffn", "w_gate_up", "w_down", "g_ple", "w_ple_gate", "w_ple_proj", "g_final"]
    w = dict(zip(names, [g_mix, w_in, conv_w, a_log, dt_bias, rel_bias, w_onorm, w_branch_a, w_branch_b, w_out,
                         g_ffn, w_gate_up, w_down, g_ple, w_ple_gate, w_ple_proj, g_final]))
    m = dict(zip(names, [m_g_mix, m_w_in, m_conv_w, m_a_log, m_dt_bias, m_rel_bias, m_w_onorm, m_w_branch_a,
                         m_w_branch_b, m_w_out, m_g_ffn, m_w_gate_up, m_w_down, m_g_ple, m_w_ple_gate,
                         m_w_ple_proj, m_g_final]))
    v = dict(zip(names, [v_g_mix, v_w_in, v_conv_w, v_a_log, v_dt_bias, v_rel_bias, v_w_onorm, v_w_branch_a,
                         v_w_branch_b, v_w_out, v_g_ffn, v_w_gate_up, v_w_down, v_g_ple, v_w_ple_gate,
                         v_w_ple_proj, v_g_final]))
    xi, yi, ci = _place()
    chip = 2 * xi + yi
    big_names = [n for n, _, _ in BIG]

    shards2d = {n: w[n].reshape(w[n].shape[-2:]) for n in big_names}
    shards_bf = {n: a.astype(BF16) for n, a in shards2d.items()}
    g4 = allgather_weights(shards_bf, FIRST_WEIGHTS, chip)["w_in"]
    place = jnp.stack([chip, ci]).astype(jnp.int32)
    conv_sh = jnp.where(ci == 0, w["conv_w"].reshape(CONV_K, CONV_CH // N_CHIPS), 0.0)
    conv_slots = lax.dynamic_update_slice(jnp.zeros((N_CHIPS, CONV_K, CONV_CH // N_CHIPS), F32), conv_sh[None],
                                          (chip, 0, 0))
    conv_all = small_allreduce(conv_slots.reshape(-1, 128), "gather_conv_w")
    conv_full = jnp.transpose(conv_all.reshape(N_CHIPS, CONV_K, CONV_CH // N_CHIPS), (1, 0, 2)).reshape(CONV_K, CONV_CH)
    small = {n: w[n] for n in names if n not in big_names}
    small["conv_w"] = conv_full

    grad_x, grads, small_grads, reduced_early = local_step(
        x, p[0], loss_target, g4, small, LaterWeights(shards_bf, chip), GradReduce(place, EARLY_GRADS, "grads"))

    late = GradReduce(place, LATE_GRADS, "late")
    tok = late.begin(grads)
    small_names = [n for n, _ in SMALL]
    small_grads["loss"] = small_grads["loss"] + tok
    red_flat = small_allreduce(pack_small(small_grads, small_names, 112), "allreduce_small")
    red = unpack_small(red_flat, SMALL)
    dep = jnp.full((8, 128), late.exchange(red_flat), F32)
    gshard = dict(reduced_early)
    loss = red["loss"][0]
    conv_g = lax.dynamic_slice(red["conv_w"].reshape(CONV_K, N_CHIPS, CONV_CH // N_CHIPS), (0, chip, 0),
                               (CONV_K, 1, CONV_CH // N_CHIPS))
    gsmall = {n: red[n].reshape(w[n].shape) for n in small_names if n not in ("loss", "conv_w")}
    gsmall["conv_w"] = conv_g.reshape(w["conv_w"].shape)

    grad, delta, new_m, new_v = {}, {}, {}, {}
    for n in list(EARLY_GRADS) + list(LATE_GRADS):
        if n in LATE_GRADS:
            late.join(v_)
            gshard.update(late.finish(v_))
        shp = w[n].shape
        d_, m_, v_ = adamw(shards2d[n], gshard[n], m[n].reshape(shp[-2:]), v[n].reshape(shp[-2:]), "adamw_" + n,
                           dep=dep if n in EARLY_GRADS else None)
        dep, v_ = lax.optimization_barrier((dep, v_))
        grad[n], delta[n], new_m[n], new_v[n] = gshard[n].reshape(shp), d_.reshape(shp), m_.reshape(shp), v_.reshape(shp)
    snames = [n for n in small_names if n != "loss"]
    ssizes = [(n, w[n].size) for n in snames]
    pk = lambda d: pack_small(d, snames, 64)
    d_, m_, v_ = adamw(pk(w), pk(gsmall), pk(m), pk(v), "adamw_small")
    ds, ms, vs = unpack_small(d_, ssizes), unpack_small(m_, ssizes), unpack_small(v_, ssizes)
    for n in snames:
        shp = w[n].shape
        grad[n], delta[n], new_m[n], new_v[n] = gsmall[n], ds[n].reshape(shp), ms[n].reshape(shp), vs[n].reshape(shp)

    return (loss, grad_x, *[grad[n] for n in names], *[delta[n] for n in names],
            *[new_m[n] for n in names], *[new_v[n] for n in names])
```

```python
import functools

import jax
import jax.numpy as jnp
from jax import lax
from jax.experimental import pallas as pl
from jax.experimental.pallas import tpu as pltpu

F32 = jnp.float32
BF16 = jnp.bfloat16
HI = lax.Precision.HIGHEST
MESH = pl.DeviceIdType.MESH

D_MODEL = 1024
CHUNK = 64
PLE_DIM = 256
EPS = 1e-6
A_HEADS = 8
A_HEAD_DIM = 64
A_WIDTH = 512
A_LOOKBACK = 8
BAND = (A_LOOKBACK + 1) * CHUNK
TAIL = 3 * CHUNK
REL_CLIP = 128
N_REL = 2 * REL_CLIP + 1
B_HEADS = 4
B_DIM = 128
B_WIDTH = 512
CONV_K = 4
CONV_CH = 1536
D_FF = 2816
SPLIT_Z = 3584
D_IN = 5640
ADAM_LR, ADAM_B1, ADAM_B2, ADAM_EPS, ADAM_WD, ADAM_STEP = 0.001, 0.9, 0.999, 1e-08, 0.01, 10

P_GATES, P_QA, P_KA, P_VA, P_CONV, P_Z, P_BD, P_WIDTH = 0, 2048, 2560, 3072, 3584, 5120, 5632, 5760

VMEM_LIMIT = 56 * 1024 * 1024


def _cp(sem, vmem=None, **kw):
    return pltpu.CompilerParams(dimension_semantics=sem, vmem_limit_bytes=vmem, **kw)


def _tile(n, cap):
    best = None
    for t in range(128, cap + 1, 128):
        if n % t == 0:
            best = t
    assert best is not None, (n, cap)
    return best


def _nn(a, b, prec=None):
    return lax.dot_general(a, b, (((1,), (0,)), ((), ())), preferred_element_type=F32, precision=prec)


def _nt(a, b, prec=None):
    return lax.dot_general(a, b, (((1,), (1,)), ((), ())), preferred_element_type=F32, precision=prec)


def _tn(a, b, prec=None):
    return lax.dot_general(a, b, (((0,), (0,)), ((), ())), preferred_element_type=F32, precision=prec)


def _bnn(a, b, prec=None):
    return lax.dot_general(a, b, (((2,), (1,)), ((0,), (0,))), preferred_element_type=F32, precision=prec)


def _bnt(a, b, prec=None):
    return lax.dot_general(a, b, (((2,), (2,)), ((0,), (0,))), preferred_element_type=F32, precision=prec)


def _bf(a):
    return a.astype(BF16)


def _split(a):
    hi = a.astype(BF16)
    return hi, (a - hi.astype(F32)).astype(BF16)


def _split3(a):
    h1 = _bf(a)
    r1 = a - h1.astype(F32)
    h2 = _bf(r1)
    return h1, h2, _bf(r1 - h2.astype(F32))


def _bnn_exact(lhs_b, rhs):
    h1, h2, h3 = _split3(rhs)
    return _bnn(lhs_b, h1) + (_bnn(lhs_b, h2) + _bnn(lhs_b, h3))


def _bnn3(a, b):
    ah, al = a if isinstance(a, tuple) else _split(a)
    bh, bl = b if isinstance(b, tuple) else _split(b)
    return _bnn(ah, bh) + (_bnn(ah, bl) + _bnn(al, bh))


def _sigmoid(x):
    return 0.5 * jnp.tanh(0.5 * x) + 0.5


def _softplus(x):
    return jnp.maximum(x, 0.0) + jnp.log(1.0 + jnp.exp(-jnp.abs(x)))


def rms_matmul(x, g, w, name, tm=512, tn_cap=1024):
    t, d = x.shape
    n = w.shape[1]
    tm = min(tm, t)
    tn = _tile(n, tn_cap)

    nj = n // tn

    def body(x_ref, g_ref, w_ref, o_ref, h_ref, tail_ref):
        @pl.when(pl.program_id(1) == 0)
        def _():
            xv = x_ref[...]
            r = lax.rsqrt(jnp.mean(xv * xv, axis=-1, keepdims=True) + EPS)
            h_ref[...] = _bf(xv * r * g_ref[...])

        res = _nn(h_ref[...], w_ref[...])
        o_ref[...] = _bf(res)

        @pl.when(pl.program_id(1) == nj - 1)
        def _():
            tail_ref[...] = res[:, tn - 128:]

    return pl.pallas_call(
        body, name=name, grid=(t // tm, nj),
        in_specs=[pl.BlockSpec((tm, d), lambda i, j: (i, 0)),
                  pl.BlockSpec((1, d), lambda i, j: (0, 0)),
                  pl.BlockSpec((d, tn), lambda i, j: (0, j))],
        out_specs=[pl.BlockSpec((tm, tn), lambda i, j: (i, j)),
                   pl.BlockSpec((tm, d), lambda i, j: (i, 0)),
                   pl.BlockSpec((tm, 128), lambda i, j: (i, 0))],
        out_shape=[jax.ShapeDtypeStruct((t, n), BF16), jax.ShapeDtypeStruct((t, d), BF16),
                   jax.ShapeDtypeStruct((t, 128), F32)],
        compiler_params=_cp(("parallel", "arbitrary"), VMEM_LIMIT),
    )(x, g, w)


def matmul_tn(a, b, name, into=None, col0=0, width=None, tm=1024, tk_cap=1408, tn_cap=1408, tiles_major=False):
    m, k1 = a.shape
    n = b.shape[1]
    tm = min(tm, m)
    tk = _tile(k1, tk_cap)
    tn = _tile(n, tn_cap)
    while col0 % tn:
        tn = _tile(n, tn - 128)
    nk = m // tm
    c0 = col0 // tn

    def body(*refs):
        a_ref, b_ref, o_ref, acc = refs[0], refs[1], refs[-2], refs[-1]

        @pl.when(pl.program_id(2) == 0)
        def _():
            acc[...] = jnp.zeros_like(acc)

        acc[...] += _tn(_bf(a_ref[...]), _bf(b_ref[...]))

        @pl.when(pl.program_id(2) == nk - 1)
        def _():
            o_ref[...] = _bf(acc[...])

    in_specs = [pl.BlockSpec((tm, tk), lambda i, j, k: (k, i)),
                pl.BlockSpec((tm, tn), lambda i, j, k: (k, j))]
    args = [a, b]
    total = n if width is None else width
    aliases = {}
    if into is not None:
        in_specs.append(ANY)
        args.append(into)
        aliases = {2: 0}
    if tiles_major:
        out_spec = pl.BlockSpec((None, tk, tn), lambda i, j, k: (c0 + j, i, 0))
        out_shape = jax.ShapeDtypeStruct((total // tn, k1, tn), BF16)
    else:
        out_spec = pl.BlockSpec((tk, tn), lambda i, j, k: (i, c0 + j))
        out_shape = jax.ShapeDtypeStruct((k1, total), BF16)
    return pl.pallas_call(
        body, name=name, grid=(k1 // tk, n // tn, nk),
        in_specs=in_specs,
        out_specs=out_spec,
        out_shape=out_shape,
        scratch_shapes=[pltpu.VMEM((tk, tn), F32)],
        input_output_aliases=aliases,
        compiler_params=_cp(("parallel", "parallel", "arbitrary"), VMEM_LIMIT),
    )(*args)


def _tail_onehot(qi):
    r = lax.broadcasted_iota(jnp.int32, (384, TAIL), 0)
    kj = lax.broadcasted_iota(jnp.int32, (384, TAIL), 1)
    return (r == jnp.minimum(REL_CLIP + qi - kj, REL_CLIP) + REL_CLIP).astype(F32)


def bias_tail(rel_pad):
    def body(rb_ref, o_ref):
        parts = _split3(rb_ref[...])
        for qi in range(CHUNK):
            oh = _bf(_tail_onehot(qi))
            o_ref[qi] = _nn(parts[0], oh) + (_nn(parts[1], oh) + _nn(parts[2], oh))

    return pl.pallas_call(
        body, name="bias_tail",
        out_shape=jax.ShapeDtypeStruct((CHUNK, A_HEADS, TAIL), F32),
    )(rel_pad)


def bias_grad(db_t, db_far):
    def body(t_ref, f_ref, o_ref):
        acc = jnp.zeros((A_HEADS, 384), F32)
        for qi in range(CHUNK):
            oh = _bf(_tail_onehot(qi))
            parts = _split3(t_ref[qi])
            acc = acc + (_nt(parts[0], oh) + (_nt(parts[1], oh) + _nt(parts[2], oh)))
        far = jnp.sum(jnp.sum(f_ref[...], axis=2), axis=1, keepdims=True)
        lane = lax.broadcasted_iota(jnp.int32, (A_HEADS, 384), 1)
        o_ref[...] = acc + jnp.where(lane == 2 * REL_CLIP, far, 0.0)

    return pl.pallas_call(
        body, name="bias_grad",
        out_shape=jax.ShapeDtypeStruct((A_HEADS, 384), F32),
    )(db_t, db_far)


ATT_CB = 8


WIN = BAND + CHUNK


def _stack_heads(a, lane):
    return jnp.concatenate([jnp.where(lane < 64, a, 0.0), jnp.where(lane >= 64, a, 0.0)], axis=0)


def _fill_band_pads(k_ref, v_ref, kp, vp, s):
    z = jnp.zeros((A_LOOKBACK * CHUNK, 128), BF16)
    kp[pl.ds(0, A_LOOKBACK * CHUNK), :] = z
    vp[pl.ds(0, A_LOOKBACK * CHUNK), :] = z
    kp[pl.ds(A_LOOKBACK * CHUNK, s), :] = _bf(k_ref[...])
    vp[pl.ds(A_LOOKBACK * CHUNK, s), :] = _bf(v_ref[...])


def attn_fwd(proj, bias_band, b, s):
    t = b * s
    nc = s // CHUNK
    qb, kb_, vb_ = P_QA // 128, P_KA // 128, P_VA // 128

    nstep = nc // ATT_CB
    rows = ATT_CB * CHUNK

    def body(q_ref, k_ref, v_ref, b_ref, o_ref, lse_ref, kp, vp):
        n0 = pl.program_id(2) * ATT_CB

        @pl.when(n0 == 0)
        def _():
            _fill_band_pads(k_ref, v_ref, kp, vp, s)

        lane = lax.broadcasted_iota(jnp.int32, (2 * CHUNK, 128), 1)
        col = lax.broadcasted_iota(jnp.int32, (4 * CHUNK, WIN), 1)
        bias4 = b_ref[...]

        def pair(pp, carry):
            n = n0 + 2 * pp
            r0 = pl.multiple_of(pp * 2 * CHUNK, 2 * CHUNK)
            start = pl.multiple_of(n * CHUNK, CHUNK)
            kb = kp[pl.ds(start, WIN), :]
            vb = vp[pl.ds(start, WIN), :]
            q4 = _stack_heads(q_ref[pl.ds(r0, 2 * CHUNK), :] * (A_HEAD_DIM ** -0.5), lane)
            sc = jnp.where(col >= (A_LOOKBACK - n) * CHUNK, _nt(_bf(q4), kb) + bias4, -1e30)
            mx = jnp.max(sc, axis=1, keepdims=True)
            p = jnp.exp(sc - mx)
            l = jnp.sum(p, axis=1, keepdims=True)
            o4 = _nn(_bf(p), vb) / l
            lse4 = mx + jnp.log(l)
            o_ref[pl.ds(r0, 2 * CHUNK), :] = jnp.where(lane < 64, o4[:2 * CHUNK], o4[2 * CHUNK:])
            lse_ref[pl.ds(r0, 2 * CHUNK), :] = jnp.where(lane < 64, lse4[:2 * CHUNK], lse4[2 * CHUNK:])
            return carry

        lax.fori_loop(0, ATT_CB // 2, pair, 0, unroll=2)

    return pl.pallas_call(
        body, name="attn_fwd", grid=(b, 4, nstep),
        in_specs=[pl.BlockSpec((rows, 128), lambda bb, m, n: (bb * nstep + n, qb + m)),
                  pl.BlockSpec((s, 128), lambda bb, m, n: (bb, kb_ + m)),
                  pl.BlockSpec((s, 128), lambda bb, m, n: (bb, vb_ + m)),
                  pl.BlockSpec((None, 4 * CHUNK, WIN), lambda bb, m, n: (m, 0, 0))],
        out_specs=[pl.BlockSpec((rows, 128), lambda bb, m, n: (bb * nstep + n, m)),
                   pl.BlockSpec((rows, 128), lambda bb, m, n: (bb * nstep + n, m))],
        out_shape=[jax.ShapeDtypeStruct((t, A_WIDTH), F32), jax.ShapeDtypeStruct((t, A_WIDTH), F32)],
        scratch_shapes=[pltpu.VMEM((s + A_LOOKBACK * CHUNK, 128), BF16),
                        pltpu.VMEM((s + A_LOOKBACK * CHUNK, 128), BF16)],
        compiler_params=_cp(("parallel", "parallel", "arbitrary"), VMEM_LIMIT),
    )(proj, proj, proj, bias_band)


def attn_bwd(proj, bias_band, y_a, lse, dy_a, b, s):
    t = b * s
    nc = s // CHUNK
    qb, kb_, vb_ = P_QA // 128, P_KA // 128, P_VA // 128
    pad = A_LOOKBACK * CHUNK
    nstep = nc // ATT_CB
    rows = ATT_CB * CHUNK

    def body(q_ref, k_ref, v_ref, b_ref, do_ref, o_ref, lse_ref,
             dq_ref, dk_ref, dv_ref, dbt_ref, dbf_ref, kp, vp, dkp, dvp):
        bb = pl.program_id(1)
        n0 = pl.program_id(2) * ATT_CB

        @pl.when(n0 == 0)
        def _():
            _fill_band_pads(k_ref, v_ref, kp, vp, s)
            dkp[...] = jnp.zeros_like(dkp)
            dvp[...] = jnp.zeros_like(dvp)

        @pl.when((n0 == 0) & (bb == 0))
        def _():
            dbt_ref[...] = jnp.zeros_like(dbt_ref)
            dbf_ref[...] = jnp.zeros_like(dbf_ref)

        lane = lax.broadcasted_iota(jnp.int32, (2 * CHUNK, 128), 1)
        col = lax.broadcasted_iota(jnp.int32, (4 * CHUNK, WIN), 1)
        bias4 = b_ref[...]

        def pair(pp, carry):
            n = n0 + 2 * pp
            r0 = pl.multiple_of(pp * 2 * CHUNK, 2 * CHUNK)
            start = pl.multiple_of(n * CHUNK, CHUNK)
            kb = kp[pl.ds(start, WIN), :]
            vb = vp[pl.ds(start, WIN), :]
            q4b = _bf(_stack_heads(q_ref[pl.ds(r0, 2 * CHUNK), :] * (A_HEAD_DIM ** -0.5), lane))
            do4 = _stack_heads(do_ref[pl.ds(r0, 2 * CHUNK), :], lane)
            do4b = _bf(do4)
            o = o_ref[pl.ds(r0, 2 * CHUNK), :]
            lsev = lse_ref[pl.ds(r0, 2 * CHUNK), :]
            lse4 = jnp.concatenate([lsev[:, 0:1], lsev[:, 64:65]], axis=0)
            sc = jnp.where(col >= (A_LOOKBACK - n) * CHUNK, _nt(q4b, kb) + bias4, -1e30)
            p = jnp.exp(sc - lse4)
            dp = _nt(do4b, vb)
            delta = jnp.sum(do4 * jnp.concatenate([o, o], axis=0), axis=1, keepdims=True)
            ds = p * (dp - delta)
            dsb = _bf(ds)
            dq4 = _nn(dsb, kb)
            dq_ref[pl.ds(r0, 2 * CHUNK), :] = _bf(
                jnp.where(lane < 64, dq4[:2 * CHUNK], dq4[2 * CHUNK:]) * (A_HEAD_DIM ** -0.5))
            dkp[pl.ds(start, WIN), :] += _tn(dsb, q4b)
            dvp[pl.ds(start, WIN), :] += _tn(_bf(p), do4b)
            dbt_ref[...] += ds[:, WIN - 256:]
            dbf_ref[...] += ds[:, 0:128] + ds[:, 128:256] + ds[:, 256:384]
            return carry

        lax.fori_loop(0, ATT_CB // 2, pair, 0, unroll=2)

        @pl.when(n0 == nc - ATT_CB)
        def _():
            dk_ref[...] = _bf(dkp[pl.ds(pad, s), :])
            dv_ref[...] = _bf(dvp[pl.ds(pad, s), :])

    return pl.pallas_call(
        body, name="attn_bwd", grid=(4, b, nstep),
        in_specs=[pl.BlockSpec((rows, 128), lambda m, bb, n: (bb * nstep + n, qb + m)),
                  pl.BlockSpec((s, 128), lambda m, bb, n: (bb, kb_ + m)),
                  pl.BlockSpec((s, 128), lambda m, bb, n: (bb, vb_ + m)),
                  pl.BlockSpec((None, 4 * CHUNK, WIN), lambda m, bb, n: (m, 0, 0)),
                  pl.BlockSpec((rows, 128), lambda m, bb, n: (bb * nstep + n, m)),
                  pl.BlockSpec((rows, 128), lambda m, bb, n: (bb * nstep + n, m)),
                  pl.BlockSpec((rows, 128), lambda m, bb, n: (bb * nstep + n, m))],
        out_specs=[pl.BlockSpec((rows, 128), lambda m, bb, n: (bb * nstep + n, m)),
                   pl.BlockSpec((s, 128), lambda m, bb, n: (bb, m)),
                   pl.BlockSpec((s, 128), lambda m, bb, n: (bb, m)),
                   pl.BlockSpec((None, 4 * CHUNK, 256), lambda m, bb, n: (m, 0, 0)),
                   pl.BlockSpec((None, 4 * CHUNK, 128), lambda m, bb, n: (m, 0, 0))],
        out_shape=[jax.ShapeDtypeStruct((t, A_WIDTH), BF16)] * 3
        + [jax.ShapeDtypeStruct((4, 4 * CHUNK, 256), F32),
           jax.ShapeDtypeStruct((4, 4 * CHUNK, 128), F32)],
        scratch_shapes=[pltpu.VMEM((s + pad, 128), BF16), pltpu.VMEM((s + pad, 128), BF16),
                        pltpu.VMEM((s + pad, 128), F32), pltpu.VMEM((s + pad, 128), F32)],
        compiler_params=_cp(("parallel", "arbitrary", "arbitrary"), VMEM_LIMIT),
    )(proj, proj, proj, bias_band, dy_a, y_a, lse)


def _conv_taps(x, w, s):
    row = lax.broadcasted_iota(jnp.int32, x.shape, 0)
    shifted = [x] + [jnp.where(row >= i, pltpu.roll(x, i, 0), 0.0) for i in range(1, CONV_K)]
    acc = shifted[0] * w[CONV_K - 1:CONV_K, :]
    for i in range(1, CONV_K):
        acc = acc + shifted[i] * w[CONV_K - 1 - i:CONV_K - i, :]
    return acc, shifted


def conv_fwd(proj, conv_w8, b, s):
    cb = 512
    c0 = P_CONV // cb

    def body(x_ref, w_ref, o_ref):
        a, _ = _conv_taps(x_ref[...].astype(F32), w_ref[...], s)
        o_ref[...] = a * _sigmoid(a)

    return pl.pallas_call(
        body, name="conv_fwd", grid=(b, CONV_CH // cb),
        in_specs=[pl.BlockSpec((s, cb), lambda bb, j: (bb, c0 + j)),
                  pl.BlockSpec((8, cb), lambda bb, j: (0, j))],
        out_specs=pl.BlockSpec((s, cb), lambda bb, j: (bb, j)),
        out_shape=jax.ShapeDtypeStruct((b * s, CONV_CH), F32),
        compiler_params=_cp(("parallel", "parallel"), VMEM_LIMIT),
    )(proj, conv_w8)


def conv_bwd(proj, conv_w8, dc3, b, s):
    cb = 512
    c0 = P_CONV // cb

    def body(x_ref, w_ref, dc_ref, dx_ref, dw_ref):
        @pl.when(pl.program_id(1) == 0)
        def _():
            dw_ref[...] = jnp.zeros_like(dw_ref)

        w = w_ref[...]
        a, shifted = _conv_taps(x_ref[...].astype(F32), w, s)
        sg = _sigmoid(a)
        da = dc_ref[...] * (sg * (1.0 + a * (1.0 - sg)))
        row = lax.broadcasted_iota(jnp.int32, da.shape, 0)
        dx = da * w[CONV_K - 1:CONV_K, :]
        for i in range(1, CONV_K):
            dx = dx + jnp.where(row < s - i, pltpu.roll(da, s - i, 0), 0.0) * w[CONV_K - 1 - i:CONV_K - i, :]
        dx_ref[...] = _bf(dx)
        r8 =lax.broadcasted_iota(jnp.int32, (8, cb), 0)
        dw = jnp.zeros((8, cb), F32)
        for i in range(CONV_K):
            dw = dw + jnp.where(r8 == CONV_K - 1 - i, jnp.sum(da * shifted[i], axis=0, keepdims=True), 0.0)
        dw_ref[...] += dw

    return pl.pallas_call(
        body, name="conv_bwd", grid=(CONV_CH // cb, b),
        in_specs=[pl.BlockSpec((s, cb), lambda j, bb: (bb, c0 + j)),
                  pl.BlockSpec((8, cb), lambda j, bb: (0, j)),
                  pl.BlockSpec((None, s, cb), lambda j, bb: (j, bb, 0))],
        out_specs=[pl.BlockSpec((s, cb), lambda j, bb: (bb, j)),
                   pl.BlockSpec((8, cb), lambda j, bb: (0, j))],
        out_shape=[jax.ShapeDtypeStruct((b * s, CONV_CH), BF16), jax.ShapeDtypeStruct((8, CONV_CH), F32)],
        compiler_params=_cp(("parallel", "arbitrary"), VMEM_LIMIT),
    )(proj, conv_w8, dc3)


def _pick_lane(v, k):
    lane = lax.broadcasted_iota(jnp.int32, v.shape, 1)
    return jnp.sum(jnp.where(lane == k, v, 0.0), axis=1, keepdims=True)


def _chunk_masks(ncb):
    i = lax.broadcasted_iota(jnp.int32, (ncb, CHUNK, CHUNK), 1)
    j = lax.broadcasted_iota(jnp.int32, (ncb, CHUNK, CHUNK), 2)
    return i, j


def _col_of_row(rowvec, eye):
    return jnp.sum(jnp.where(eye, rowvec, 0.0), axis=2, keepdims=True)


def _dn_chunk_math(cq, ck, cv, bd, al_row, dtb_row, h, ncb, tm=None):
    r = ncb * CHUNK
    i, j = _chunk_masks(ncb)
    eye = i == j
    low = i >= j
    strict = i > j
    ones = jnp.ones((ncb, CHUNK, CHUNK), F32)

    braw = _pick_lane(bd, h)
    draw = _pick_lane(bd, B_HEADS + h)
    al = _pick_lane(al_row, h)
    dtb = _pick_lane(dtb_row, h)
    ea = jnp.exp(al)
    beta = _sigmoid(braw)
    sp_arg = draw + dtb
    g = -ea * _softplus(sp_arg)

    rq = lax.rsqrt(jnp.sum(cq * cq, axis=1, keepdims=True) + EPS)
    rk = lax.rsqrt(jnp.sum(ck * ck, axis=1, keepdims=True) + EPS)
    nq = cq * rq
    kn = ck * rk
    qn = nq * (B_DIM ** -0.5)

    def c3(a):
        return a.reshape(ncb, CHUNK, a.shape[-1])

    qn3, kn3, v3, beta3 = c3(qn), c3(kn), c3(cv), c3(beta)
    gb = jnp.broadcast_to(c3(g), (ncb, CHUNK, CHUNK))
    gc_b = _bnn_exact(low.astype(BF16), gb)
    gr_b = _bnn_exact(_bf(ones), jnp.where(eye, gc_b, 0.0))
    dm = jnp.where(low, jnp.exp(jnp.where(low, gc_b - gr_b, 0.0)), 0.0)
    gc = gc_b[:, :, 0:1]
    gl = gc_b[:, CHUNK - 1:CHUNK, 0:1]
    gam = jnp.exp(gc)
    egl = jnp.exp(gl)
    edec = jnp.exp(gl - gc)

    knb = _bf(kn3)
    kk = _bnt(knb, knb)
    kd = jnp.where(strict, kk * dm, 0.0)
    a = beta3 * kd
    sz = 1 if tm is None else CHUNK
    if tm is None:
        tm = eye.astype(F32)
    while sz < CHUNK:
        off = jnp.where(((i // (2 * sz)) == (j // (2 * sz))) & ((i // sz) != (j // sz)), a, 0.0)
        tmb = _bf(tm)
        tm = tm - _bnn(_bf(_bnn(tmb, _bf(off))), tmb)
        sz *= 2
    bv = beta3 * v3
    bk = (beta3 * gam) * kn3
    sol = _bnn3(_split(tm), jnp.concatenate([bv, bk], axis=2))
    u, wk = sol[:, :, :B_DIM], sol[:, :, B_DIM:]
    qk = _bnt(_bf(qn3), knb)
    p = jnp.where(low, qk * dm, 0.0)
    kdec = kn3 * edec
    qg = gam * qn3
    return dict(beta=beta3, g=c3(g), ea=ea, sp_arg=c3(sp_arg), rq=c3(rq), rk=c3(rk), nq=c3(nq),
                qn=qn3, kn=kn3, v=v3, gc=gc, gl=gl, gam=gam, egl=egl, edec=edec, dm=dm, kd=kd, a=a,
                tm=tm, u=u, wk=wk, qk=qk, p=p, kdec=kdec, qg=qg, eye=eye, low=low, strict=strict)


def dn_prep(c, proj, al_row, dtb_row, b, s, ncb=16):
    t = b * s
    r = ncb * CHUNK
    nblk = t // r
    bd_blk = 0

    def body(cq_ref, ck_ref, cv_ref, bd_ref, al_ref, dtb_ref, u_ref, wk_ref, qg_ref, kdec_ref, p_ref, egl_ref,
             tm_ref):
        h = pl.program_id(1)
        m = _dn_chunk_math(cq_ref[...], ck_ref[...], cv_ref[...], bd_ref[...].astype(F32), al_ref[...], dtb_ref[...], h, ncb)
        tm_ref[...] = m["tm"].reshape(r, CHUNK)
        u_ref[...] = m["u"].reshape(r, B_DIM)
        wk_ref[...] = _bf(m["wk"].reshape(r, B_DIM))
        qg_ref[...] = _bf(m["qg"].reshape(r, B_DIM))
        kdec_ref[...] = _bf(m["kdec"].reshape(r, B_DIM))
        p_ref[...] = m["p"].reshape(r, CHUNK)
        egl_ref[...] = jnp.broadcast_to(m["egl"], (ncb, 8, 128)).reshape(ncb * 8, 128)

    col = lambda k: pl.BlockSpec((r, 128), lambda i, h: (i, k * B_HEADS + h))
    out_col = pl.BlockSpec((r, 128), lambda i, h: (i, h))
    small = pl.BlockSpec((1, 128), lambda i, h: (0, 0))
    return pl.pallas_call(
        body, name="dn_prep", grid=(nblk, B_HEADS),
        in_specs=[col(0), col(1), col(2), pl.BlockSpec((r, 128), lambda i, h: (i, bd_blk)), small, small],
        out_specs=[out_col, out_col, out_col, out_col,
                   pl.BlockSpec((None, r, CHUNK), lambda i, h: (h, i, 0)),
                   pl.BlockSpec((None, ncb * 8, 128), lambda i, h: (h, i, 0)),
                   pl.BlockSpec((None, r, CHUNK), lambda i, h: (h, i, 0))],
        out_shape=[jax.ShapeDtypeStruct((t, B_WIDTH), F32)] + [jax.ShapeDtypeStruct((t, B_WIDTH), BF16)] * 3
        + [jax.ShapeDtypeStruct((B_HEADS, t, CHUNK), F32),
           jax.ShapeDtypeStruct((B_HEADS, t // 8, 128), F32),
           jax.ShapeDtypeStruct((B_HEADS, t, CHUNK), F32)],
        compiler_params=_cp(("parallel", "parallel"), VMEM_LIMIT),
    )(c, c, c, proj, al_row, dtb_row)


def dn_scan_fwd(u, wk, qg, kdec, p, egl, b, s):
    t = b * s
    nc = s // CHUNK

    def body(u_ref, wk_ref, qg_ref, kdec_ref, p_ref, egl_ref, o_ref, ss_ref, st):
        @pl.when(pl.program_id(0) == 0)
        def _():
            st[...] = jnp.zeros_like(st)

        chains = [(bb, h) for bb in range(b) for h in range(B_HEADS)]
        states = [st[bb * B_HEADS + h] for bb, h in chains]
        sls = [slice(h * B_DIM, (h + 1) * B_DIM) for _, h in chains]
        sbs = [_bf(sh) for sh in states]
        ws = [u_ref[bb, :, sl] - _nt(_bf(wk_ref[bb, :, sl]), sb) for (bb, _), sl, sb in zip(chains, sls, sbs)]
        qs = [_nt(_bf(qg_ref[bb, :, sl]), sb) for (bb, _), sl, sb in zip(chains, sls, sbs)]
        wbs = [_bf(w) for w in ws]
        outs = [q + _nn(_bf(p_ref[h, bb]), wb) for (bb, h), q, wb in zip(chains, qs, wbs)]
        new_states = [egl_ref[h, bb][0:1, :] * sh + _tn(wb, _bf(kdec_ref[bb, :, sl]))
                      for (bb, h), sl, sh, wb in zip(chains, sls, states, wbs)]
        for (bb, h), sh, o, ns in zip(chains, states, outs, new_states):
            ss_ref[bb, h] = sh
            o_ref[bb, :, h * B_DIM:(h + 1) * B_DIM] = o
            st[bb * B_HEADS + h] = ns

    r3 = lambda a: a.reshape(b, s, B_WIDTH)
    act = pl.BlockSpec((b, CHUNK, B_WIDTH), lambda n: (0, n, 0))
    o, states = pl.pallas_call(
        body, name="dn_scan_fwd", grid=(nc,),
        in_specs=[act, act, act, act,
                  pl.BlockSpec((B_HEADS, b, CHUNK, CHUNK), lambda n: (0, 0, n, 0)),
                  pl.BlockSpec((B_HEADS, b, 8, 128), lambda n: (0, 0, n, 0))],
        out_specs=[act, pl.BlockSpec((b, None, B_HEADS, B_DIM, B_DIM), lambda n: (0, n, 0, 0, 0))],
        out_shape=[jax.ShapeDtypeStruct((b, s, B_WIDTH), F32),
                   jax.ShapeDtypeStruct((b, nc, B_HEADS, B_DIM, B_DIM), F32)],
        scratch_shapes=[pltpu.VMEM((b * B_HEADS, B_DIM, B_DIM), F32)],
        compiler_params=_cp(("arbitrary",), VMEM_LIMIT),
    )(r3(u), r3(wk), r3(qg), r3(kdec), p.reshape(B_HEADS, b, s, CHUNK), egl.reshape(B_HEADS, b, s // 8, 128))
    return o.reshape(t, B_WIDTH), states


def dn_scan_bwd(u, wk, qg, kdec, p, egl, states, do, b, s):
    t = b * s
    nc = s // CHUNK

    def body(u_ref, wk_ref, qg_ref, kdec_ref, p_ref, egl_ref, ss_ref, do_ref,
             dw_ref, dwk_ref, dqg_ref, dkdec_ref, dp_ref, degl_ref, dst):
        @pl.when(pl.program_id(0) == 0)
        def _():
            dst[...] = jnp.zeros_like(dst)

        chains = [(bb, h) for bb in range(b) for h in range(B_HEADS)]
        dstates = [dst[bb * B_HEADS + h] for bb, h in chains]
        n8 = range(len(chains))
        sls = [slice(h * B_DIM, (h + 1) * B_DIM) for _, h in chains]
        shs = [ss_ref[bb, h] for bb, h in chains]
        sbs = [_bf(sh) for sh in shs]
        dsbs = [_bf(dsp) for dsp in dstates]
        wkbs = [_bf(wk_ref[bb, :, sl]) for (bb, _), sl in zip(chains, sls)]
        dobs = [_bf(do_ref[bb, :, sl]) for (bb, _), sl in zip(chains, sls)]
        t1 = [_nt(wkbs[i], sbs[i]) for i in n8]
        dwa = [_tn(_bf(p_ref[h, bb]), dobs[i]) for i, (bb, h) in enumerate(chains)]
        dwb_ = [_nt(_bf(kdec_ref[bb, :, sls[i]]), dsbs[i]) for i, (bb, _) in enumerate(chains)]
        dqgs = [_nn(dobs[i], sbs[i]) for i in n8]
        dsq = [_tn(dobs[i], _bf(qg_ref[bb, :, sls[i]])) for i, (bb, _) in enumerate(chains)]
        wbs = [_bf(u_ref[bb, :, sls[i]] - t1[i]) for i, (bb, _) in enumerate(chains)]
        dws = [dwa[i] + dwb_[i] for i in n8]
        dwbs = [_bf(dw) for dw in dws]
        dwks = [-_nn(dwbs[i], sbs[i]) for i in n8]
        dkdecs = [_nn(wbs[i], dsbs[i]) for i in n8]
        dpms = [_nt(dobs[i], wbs[i]) for i in n8]
        dsw = [_tn(dwbs[i], wkbs[i]) for i in n8]
        tots = [jnp.sum(jnp.sum(shs[i] * dstates[i], axis=1, keepdims=True), axis=0, keepdims=True) for i in n8]
        new_dss = [egl_ref[h, bb][0:1, :] * dstates[i] + dsq[i] - dsw[i] for i, (bb, h) in enumerate(chains)]
        results = [(dws[i], dqgs[i], dwks[i], dkdecs[i], dpms[i], tots[i], new_dss[i]) for i in n8]
        for (bb, h), (dw, dqg, dwk, dkdec, dpm, tot, new_ds) in zip(chains, results):
            sl = slice(h * B_DIM, (h + 1) * B_DIM)
            dw_ref[bb, :, sl] = dw
            dqg_ref[bb, :, sl] = dqg
            dwk_ref[bb, :, sl] = dwk
            dkdec_ref[bb, :, sl] = dkdec
            dp_ref[h, bb] = dpm
            degl_ref[h, bb] = jnp.broadcast_to(tot, (8, 128))
            dst[bb * B_HEADS + h] = new_ds

    r3 = lambda a: a.reshape(b, s, B_WIDTH)
    act = pl.BlockSpec((b, CHUNK, B_WIDTH), lambda n: (0, nc - 1 - n, 0))
    pspec = pl.BlockSpec((B_HEADS, b, CHUNK, CHUNK), lambda n: (0, 0, nc - 1 - n, 0))
    espec = pl.BlockSpec((B_HEADS, b, 8, 128), lambda n: (0, 0, nc - 1 - n, 0))
    outs = pl.pallas_call(
        body, name="dn_scan_bwd", grid=(nc,),
        in_specs=[act, act, act, act, pspec, espec,
                  pl.BlockSpec((b, None, B_HEADS, B_DIM, B_DIM), lambda n: (0, nc - 1 - n, 0, 0, 0)),
                  act],
        out_specs=[act, act, act, act, pspec, espec],
        out_shape=[jax.ShapeDtypeStruct((b, s, B_WIDTH), F32)] * 4
        + [jax.ShapeDtypeStruct((B_HEADS, b, s, CHUNK), F32),
           jax.ShapeDtypeStruct((B_HEADS, b, s // 8, 128), F32)],
        scratch_shapes=[pltpu.VMEM((b * B_HEADS, B_DIM, B_DIM), F32)],
        compiler_params=_cp(("arbitrary",), VMEM_LIMIT),
    )(r3(u), r3(wk), r3(qg), r3(kdec), p.reshape(B_HEADS, b, s, CHUNK), egl.reshape(B_HEADS, b, s // 8, 128),
      states, r3(do))
    return (*[a.reshape(t, B_WIDTH) for a in outs[:4]], outs[4].reshape(B_HEADS, t, CHUNK),
            outs[5].reshape(B_HEADS, t // 8, 128))


def dn_post_bwd(c, proj, al_row, dtb_row, tmat, dw, dwk, dqg, dkdec, dp, degl, b, s, ncb=16):
    t = b * s
    r = ncb * CHUNK
    nblk = t // r
    bd_blk = 0

    def body(cq_ref, ck_ref, cv_ref, bd_ref, al_ref, dtb_ref, tm_ref, dw_ref, dwk_ref, dqg_ref, dkdec_ref, dp_ref,
             degl_ref, dc_ref, dbd_ref, dal_ref, ddtb_ref):
        h = pl.program_id(1)

        @pl.when((pl.program_id(0) == 0) & (h == 0))
        def _():
            dal_ref[...] = jnp.zeros_like(dal_ref)
            ddtb_ref[...] = jnp.zeros_like(ddtb_ref)

        m = _dn_chunk_math(cq_ref[...], ck_ref[...], cv_ref[...], bd_ref[...].astype(F32), al_ref[...], dtb_ref[...], h, ncb,
                           tm=tm_ref[...].reshape(ncb, CHUNK, CHUNK))
        eye, low, strict = m["eye"], m["low"], m["strict"]
        eyef = eye.astype(F32)

        def c3(a):
            return a.reshape(ncb, CHUNK, a.shape[-1])

        du, dwkv, dqg, dkdec = c3(dw_ref[...]), c3(dwk_ref[...]), c3(dqg_ref[...]), c3(dkdec_ref[...])
        dpm = jnp.where(low, c3(dp_ref[...]), 0.0)
        degl = degl_ref[...].reshape(ncb, 8, 128)[:, 0:1, 0:1]
        beta, gam, kn, qn, v = m["beta"], m["gam"], m["kn"], m["qn"], m["v"]
        dm, kd, a, p = m["dm"], m["kd"], m["a"], m["p"]
        knb, qnb = _bf(kn), _bf(qn)

        eyeb = _bf(eyef)
        th, tl = _split(m["tm"])
        tts = (_bf(_bnt(eyeb, th)), _bf(_bnt(eyeb, tl)))
        xy = _bnn3(tts, jnp.concatenate([du, dwkv], axis=2))
        x, y = xy[:, :, :B_DIM], xy[:, :, B_DIM:]
        da = -jnp.where(strict, _bnt(_bf(x), _bf(m["u"])) + _bnt(_bf(y), _bf(m["wk"])), 0.0)
        dv = beta * x
        sy = jnp.sum(y * kn, axis=2, keepdims=True)
        dbeta = jnp.sum(x * v, axis=2, keepdims=True) + gam * sy + jnp.sum(da * kd, axis=2, keepdims=True)
        dgam = beta * sy + jnp.sum(dqg * qn, axis=2, keepdims=True)
        dkk = da * beta * dm
        dqk = dpm * dm
        dkkb, dqkb = _bf(dkk), _bf(dqk)
        dkn = ((beta * gam) * y + _bnn(dkkb, knb) + _bnn(_bf(_bnt(eyeb, dkkb)), knb)
               + _bnn(_bf(_bnt(eyeb, dqkb)), qnb) + dkdec * m["edec"])
        dqn = gam * dqg + _bnn(dqkb, knb)
        mm = da * a + dpm * p
        ek = jnp.sum(dkdec * m["kdec"], axis=2, keepdims=True)
        dgc = (jnp.sum(mm, axis=2, keepdims=True) - _col_of_row(jnp.sum(mm, axis=1, keepdims=True), eye)
               + dgam * gam - ek)
        dgl = jnp.sum(ek, axis=1, keepdims=True) + degl * m["egl"]
        i, _ = _chunk_masks(ncb)
        dgc = dgc + jnp.where(i[:, :, 0:1] == CHUNK - 1, dgl, 0.0)
        upper = (i <= _chunk_masks(ncb)[1]).astype(BF16)
        dg = _bnn_exact(upper, jnp.broadcast_to(dgc, (ncb, CHUNK, CHUNK)))[:, :, 0:1]

        nq = m["nq"]
        dnq = dqn * (B_DIM ** -0.5)
        dcq = m["rq"] * (dnq - nq * jnp.sum(nq * dnq, axis=2, keepdims=True))
        dck = m["rk"] * (dkn - kn * jnp.sum(kn * dkn, axis=2, keepdims=True))
        dc_ref[0] = dcq.reshape(r, B_DIM)
        dc_ref[1] = dck.reshape(r, B_DIM)
        dc_ref[2] = dv.reshape(r, B_DIM)

        dbraw = (dbeta * beta * (1.0 - beta)).reshape(r, 1)
        sgm = _sigmoid(m["sp_arg"])
        ddraw3 = dg * (-m["ea"]) * sgm
        ddraw = ddraw3.reshape(r, 1)
        lane = lax.broadcasted_iota(jnp.int32, (r, 128), 1)
        contrib = jnp.where(lane == h, dbraw, 0.0) + jnp.where(lane == B_HEADS + h, ddraw, 0.0)

        @pl.when(h == 0)
        def _():
            dbd_ref[...] = contrib

        @pl.when(h != 0)
        def _():
            dbd_ref[...] += contrib

        lane8 = lax.broadcasted_iota(jnp.int32, (8, 128), 1)
        tot_al = jnp.sum(jnp.sum(dg * m["g"], axis=1, keepdims=True), axis=0, keepdims=True).reshape(1, 1)
        tot_dtb = jnp.sum(jnp.sum(ddraw3, axis=1, keepdims=True), axis=0, keepdims=True).reshape(1, 1)
        dal_ref[...] += jnp.where(lane8 == h, tot_al, 0.0)
        ddtb_ref[...] += jnp.where(lane8 == h, tot_dtb, 0.0)

    col = lambda k: pl.BlockSpec((r, 128), lambda i, h: (i, k * B_HEADS + h))
    hcol = pl.BlockSpec((r, 128), lambda i, h: (i, h))
    small = pl.BlockSpec((1, 128), lambda i, h: (0, 0))
    acc = pl.BlockSpec((8, 128), lambda i, h: (0, 0))
    return pl.pallas_call(
        body, name="dn_post_bwd", grid=(nblk, B_HEADS),
        in_specs=[col(0), col(1), col(2), pl.BlockSpec((r, 128), lambda i, h: (i, bd_blk)), small, small,
                  pl.BlockSpec((None, r, CHUNK), lambda i, h: (h, i, 0)),
                  hcol, hcol, hcol, hcol,
                  pl.BlockSpec((None, r, CHUNK), lambda i, h: (h, i, 0)),
                  pl.BlockSpec((None, ncb * 8, 128), lambda i, h: (h, i, 0))],
        out_specs=[pl.BlockSpec((3, r, 128), lambda i, h: (0, i, h)),
                   pl.BlockSpec((r, 128), lambda i, h: (i, 0)), acc, acc],
        out_shape=[jax.ShapeDtypeStruct((3, t, B_WIDTH), F32), jax.ShapeDtypeStruct((t, 128), F32),
                   jax.ShapeDtypeStruct((8, 128), F32), jax.ShapeDtypeStruct((8, 128), F32)],
        compiler_params=_cp(("arbitrary", "arbitrary"), VMEM_LIMIT),
    )(c, c, c, proj, al_row, dtb_row, tmat, dw, dwk, dqg, dkdec, dp, degl)


def make_bias_band(rel_bias):
    tail = bias_tail(jnp.pad(rel_bias, ((0, 0), (0, 384 - N_REL))))
    far = jnp.broadcast_to(rel_bias[:, 2 * REL_CLIP][:, None, None], (A_HEADS, CHUNK, BAND - TAIL))
    band = jnp.concatenate([far, jnp.transpose(tail, (1, 0, 2))], axis=2)
    off = jnp.full((A_HEADS, CHUNK, CHUNK), -1e30, F32)
    both = jnp.stack([jnp.concatenate([band, off], axis=2), jnp.concatenate([off, band], axis=2)], axis=1)
    return both.reshape(4, 4 * CHUNK, WIN)


def bias_band_grad(dbt, dbf):
    t5 = dbt.reshape(A_HEADS, 2, CHUNK, 256)
    tail = t5[:, 0, :, :TAIL] + t5[:, 1, :, CHUNK:]
    far = dbf.reshape(A_HEADS, 2, CHUNK, 128).sum(axis=1) + jnp.pad(t5[:, 1, :, :CHUNK], ((0, 0), (0, 0), (0, CHUNK)))
    return bias_grad(jnp.transpose(tail, (1, 0, 2)), far)[:, :N_REL]


def _rms(x):
    r = lax.rsqrt(jnp.mean(x * x, axis=-1, keepdims=True) + EPS)
    return r, x * r


def _rms_bwd(dh, g, r, n):
    dn = dh * g
    return r * (dn - n * jnp.mean(dn * n, axis=-1, keepdims=True)), dh * n


def _gated_onorm(o, z, w_on):
    parts = []
    for h in range(B_HEADS):
        sl = slice(h * B_DIM, (h + 1) * B_DIM)
        r, n = _rms(o[:, sl])
        parts.append((r, n))
    r4 = [p[0] for p in parts]
    n4 = jnp.concatenate([p[1] for p in parts], axis=1)
    w4 = jnp.concatenate([w_on] * B_HEADS, axis=1)
    sz = _sigmoid(z)
    silu = z * sz
    return n4 * w4 * silu, r4, n4, w4, sz, silu


def mid_fwd(x, y_a, o_b, proj, w_on, wa, wb, w_out, tm=256):
    t = x.shape[0]
    tm = min(tm, t)

    def body(x_ref, ya_ref, ob_ref, z_ref, ga_ref, gb_ref, won_ref, wa_ref, wb_ref, wo_ref, x1_ref, mg_ref):
        yb = _gated_onorm(ob_ref[...], z_ref[...].astype(F32), won_ref[...])[0]
        ua = _nn(_bf(ya_ref[...]), wa_ref[...])
        ub = _nn(_bf(yb), wb_ref[...])
        merged = _sigmoid(ga_ref[...].astype(F32)) * ua + _sigmoid(gb_ref[...].astype(F32)) * ub
        mb = _bf(merged)
        mg_ref[...] = mb
        x1_ref[...] = x_ref[...] + _nn(mb, wo_ref[...])

    rowd = pl.BlockSpec((tm, D_MODEL), lambda i: (i, 0))
    row5 = pl.BlockSpec((tm, 512), lambda i: (i, 0))
    full = lambda a: pl.BlockSpec(a.shape, lambda i: (0,) * a.ndim)
    return pl.pallas_call(
        body, name="mid_fwd", grid=(t // tm,),
        in_specs=[rowd, row5, row5,
                  pl.BlockSpec((tm, 512), lambda i: (i, P_Z // 512)),
                  pl.BlockSpec((tm, D_MODEL), lambda i: (i, 0)),
                  pl.BlockSpec((tm, D_MODEL), lambda i: (i, 1)),
                  full(w_on), full(wa), full(wb), full(w_out)],
        out_specs=[rowd, rowd],
        out_shape=[jax.ShapeDtypeStruct((t, D_MODEL), F32), jax.ShapeDtypeStruct((t, D_MODEL), BF16)],
        compiler_params=_cp(("parallel",), VMEM_LIMIT),
    )(x, y_a, o_b, proj, proj, proj, w_on, wa, wb, w_out)


def mid_bwd(dx1, merged, y_a, o_b, proj, w_on, wa, wb, w_out, tm=256):
    t = dx1.shape[0]
    tm = min(tm, t)

    def body(dx1_ref, mg_ref, ya_ref, ob_ref, z_ref, ga_ref, gb_ref, won_ref, wa_ref, wb_ref, wo_ref,
             dya_ref, dob_ref, dz_ref, dg_ref, dwo_ref, dwa_ref, dwb_ref, dwon_ref):
        @pl.when(pl.program_id(0) == 0)
        def _():
            dwo_ref[...] = jnp.zeros_like(dwo_ref)
            dwa_ref[...] = jnp.zeros_like(dwa_ref)
            dwb_ref[...] = jnp.zeros_like(dwb_ref)
            dwon_ref[...] = jnp.zeros_like(dwon_ref)

        dx1b = _bf(dx1_ref[...])
        dmerged = _nt(dx1b, wo_ref[...])
        dwo_ref[...] += _tn(mg_ref[...], dx1b)
        o = ob_ref[...]
        z = z_ref[...].astype(F32)
        yb, r4, n4, w4, sz, silu = _gated_onorm(o, z, won_ref[...])
        yab, ybb = _bf(ya_ref[...]), _bf(yb)
        ua = _nn(yab, wa_ref[...])
        ub = _nn(ybb, wb_ref[...])
        sa, sb = _sigmoid(ga_ref[...].astype(F32)), _sigmoid(gb_ref[...].astype(F32))
        dua, dub = _bf(dmerged * sa), _bf(dmerged * sb)
        dg_ref[:, 0:D_MODEL] = _bf(dmerged * ua * sa * (1.0 - sa))
        dg_ref[:, D_MODEL:2 * D_MODEL] = _bf(dmerged * ub * sb * (1.0 - sb))
        dwa_ref[...] += _tn(yab, dua)
        dwb_ref[...] += _tn(ybb, dub)
        dya_ref[...] = _nt(dua, wa_ref[...])
        dyb = _nt(dub, wb_ref[...])
        dz_ref[...] = _bf(dyb * (n4 * w4) * (sz * (1.0 + z * (1.0 - sz))))
        dnw = dyb * silu
        dwon = jnp.zeros((1, B_DIM), F32)
        for h in range(B_HEADS):
            sl = slice(h * B_DIM, (h + 1) * B_DIM)
            dxh, dgh = _rms_bwd(dnw[:, sl], won_ref[...], r4[h], n4[:, sl])
            dob_ref[:, sl] = dxh
            dwon = dwon + jnp.sum(dgh, axis=0, keepdims=True)
        dwon_ref[...] += jnp.broadcast_to(dwon, (8, B_DIM))

    rowd = pl.BlockSpec((tm, D_MODEL), lambda i: (i, 0))
    row5 = pl.BlockSpec((tm, 512), lambda i: (i, 0))
    full = lambda a: pl.BlockSpec(a.shape, lambda i: (0,) * a.ndim)
    fixed = lambda shp: pl.BlockSpec(shp, lambda i: (0,) * len(shp))
    return pl.pallas_call(
        body, name="mid_bwd", grid=(t // tm,),
        in_specs=[rowd, rowd, row5, row5,
                  pl.BlockSpec((tm, 512), lambda i: (i, P_Z // 512)),
                  pl.BlockSpec((tm, D_MODEL), lambda i: (i, 0)),
                  pl.BlockSpec((tm, D_MODEL), lambda i: (i, 1)),
                  full(w_on), full(wa), full(wb), full(w_out)],
        out_specs=[row5, row5, row5, pl.BlockSpec((tm, 2 * D_MODEL), lambda i: (i, 0)),
                   fixed((D_MODEL, D_MODEL)), fixed((A_WIDTH, D_MODEL)), fixed((B_WIDTH, D_MODEL)),
                   fixed((8, B_DIM))],
        out_shape=[jax.ShapeDtypeStruct((t, 512), F32), jax.ShapeDtypeStruct((t, 512), F32),
                   jax.ShapeDtypeStruct((t, 512), BF16), jax.ShapeDtypeStruct((t, 2 * D_MODEL), BF16),
           jax.ShapeDtypeStruct((D_MODEL, D_MODEL), F32), jax.ShapeDtypeStruct((A_WIDTH, D_MODEL), F32),
           jax.ShapeDtypeStruct((B_WIDTH, D_MODEL), F32), jax.ShapeDtypeStruct((8, B_DIM), F32)],
        compiler_params=_cp(("arbitrary",), VMEM_LIMIT),
    )(dx1, merged, y_a, o_b, proj, proj, proj, w_on, wa, wb, w_out)


FFN_TF = 1408


def ffn_up(x1, g, w_gu, tm=512, tf=FFN_TF):
    t = x1.shape[0]
    tm = min(tm, t)
    nf = D_FF // tf

    def body(x_ref, g_ref, wg_ref, wu_ref, gate_ref, up_ref, act_ref, h_ref):
        @pl.when(pl.program_id(1) == 0)
        def _():
            r, n = _rms(x_ref[...])
            h_ref[...] = _bf(n * g_ref[...])

        hb = h_ref[...]
        gate = _nn(hb, wg_ref[...])
        up = _nn(hb, wu_ref[...])
        gate_ref[...] = _bf(gate)
        up_ref[...] = _bf(up)
        act_ref[...] = _bf(gate * _sigmoid(gate) * up)

    ff = pl.BlockSpec((tm, tf), lambda i, j: (i, j))
    return pl.pallas_call(
        body, name="ffn_up", grid=(t // tm, nf),
        in_specs=[pl.BlockSpec((tm, D_MODEL), lambda i, j: (i, 0)),
                  pl.BlockSpec((1, D_MODEL), lambda i, j: (0, 0)),
                  pl.BlockSpec((D_MODEL, tf), lambda i, j: (0, j)),
                  pl.BlockSpec((D_MODEL, tf), lambda i, j: (0, nf + j))],
        out_specs=[ff, ff, ff, pl.BlockSpec((tm, D_MODEL), lambda i, j: (i, 0))],
        out_shape=[jax.ShapeDtypeStruct((t, D_FF), BF16)] * 3 + [jax.ShapeDtypeStruct((t, D_MODEL), BF16)],
        compiler_params=_cp(("parallel", "arbitrary"), VMEM_LIMIT),
    )(x1, g, w_gu, w_gu)


def matmul_residual(a, w, res, name, tm=512, tk=FFN_TF):
    t, k = a.shape
    n = w.shape[1]
    tm = min(tm, t)

    def body(a_ref, w_ref, r_ref, o_ref):
        @pl.when(pl.program_id(1) == 0)
        def _():
            o_ref[...] = r_ref[...]

        o_ref[...] += _nn(a_ref[...], w_ref[...])

    return pl.pallas_call(
        body, name=name, grid=(t // tm, k // tk),
        in_specs=[pl.BlockSpec((tm, tk), lambda i, j: (i, j)),
                  pl.BlockSpec((tk, n), lambda i, j: (j, 0)),
                  pl.BlockSpec((tm, n), lambda i, j: (i, 0))],
        out_specs=pl.BlockSpec((tm, n), lambda i, j: (i, 0)),
        out_shape=jax.ShapeDtypeStruct((t, n), F32),
        compiler_params=_cp(("parallel", "arbitrary"), VMEM_LIMIT),
    )(a, w, res)


def ffn_act_bwd(dx2, gate, up, w_down, tm=512, tf=FFN_TF):
    t = dx2.shape[0]
    tm = min(tm, t)

    def body(dx2_ref, gate_ref, up_ref, wd_ref, dgate_ref, dup_ref, dx2b_ref):
        @pl.when(pl.program_id(1) == 0)
        def _():
            dx2b_ref[...] = _bf(dx2_ref[...])

        dact = _nt(dx2b_ref[...], wd_ref[...])
        gt, upv = gate_ref[...].astype(F32), up_ref[...].astype(F32)
        sg = _sigmoid(gt)
        dgate_ref[...] = _bf(dact * upv * (sg * (1.0 + gt * (1.0 - sg))))
        dup_ref[...] = _bf(dact * (gt * sg))

    ff = pl.BlockSpec((tm, tf), lambda i, j: (i, j))
    return pl.pallas_call(
        body, name="ffn_act_bwd", grid=(t // tm, D_FF // tf),
        in_specs=[pl.BlockSpec((tm, D_MODEL), lambda i, j: (i, 0)), ff, ff,
                  pl.BlockSpec((tf, D_MODEL), lambda i, j: (j, 0))],
        out_specs=[ff, ff],
        out_shape=[jax.ShapeDtypeStruct((t, D_FF), BF16)] * 2,
        scratch_shapes=[pltpu.VMEM((tm, D_MODEL), BF16)],
        compiler_params=_cp(("parallel", "arbitrary"), VMEM_LIMIT),
    )(dx2, gate, up, w_down)


def tail_fwd_bwd(x2, p, target, g_ple, g_final, w_pg, w_pp, tm=256):
    t = x2.shape[0]
    tm = min(tm, t)

    def body(x_ref, p_ref, t_ref, gp_ref, gf_ref, wpg_ref, wpp_ref,
             dx_ref, dwpg_ref, dwpp_ref, dgp_ref, dgf_ref, loss_ref):
        @pl.when(pl.program_id(0) == 0)
        def _():
            dwpg_ref[...] = jnp.zeros_like(dwpg_ref)
            dwpp_ref[...] = jnp.zeros_like(dwpp_ref)
            dgp_ref[...] = jnp.zeros_like(dgp_ref)
            dgf_ref[...] = jnp.zeros_like(dgf_ref)
            loss_ref[...] = jnp.zeros_like(loss_ref)

        x2v = x_ref[...]
        gp, gf = gp_ref[...], gf_ref[...]
        r3, n3 = _rms(x2v)
        h3b = _bf(n3 * gp)
        pb = _bf(p_ref[...])
        pg = _sigmoid(_nn(h3b, wpg_ref[...]))
        pp = _nn(pb, wpp_ref[...])
        x3 = x2v + pg * pp
        r4, n4 = _rms(x3)
        err = n4 * gf - t_ref[...]
        part = 0.5 * jnp.sum(jnp.sum(err * err, axis=1, keepdims=True), axis=0, keepdims=True) / D_MODEL
        loss_ref[...] += jnp.broadcast_to(part, (8, 128))
        dy = err * (1.0 / D_MODEL)
        dx3, dgf = _rms_bwd(dy, gf, r4, n4)
        dgf_ref[...] += jnp.broadcast_to(jnp.sum(dgf, axis=0, keepdims=True), (8, D_MODEL))
        dzp = _bf(dx3 * pp * pg * (1.0 - pg))
        dpp = _bf(dx3 * pg)
        dwpg_ref[...] += _tn(h3b, dzp)
        dwpp_ref[...] += _tn(pb, dpp)
        dh3 = _nt(dzp, wpg_ref[...])
        dx, dgp = _rms_bwd(dh3, gp, r3, n3)
        dgp_ref[...] += jnp.broadcast_to(jnp.sum(dgp, axis=0, keepdims=True), (8, D_MODEL))
        dx_ref[...] = dx3 + dx

    rowd = pl.BlockSpec((tm, D_MODEL), lambda i: (i, 0))
    fixed = lambda shp: pl.BlockSpec(shp, lambda i: (0,) * len(shp))
    return pl.pallas_call(
        body, name="tail_fwd_bwd", grid=(t // tm,),
        in_specs=[rowd, pl.BlockSpec((tm, PLE_DIM), lambda i: (i, 0)), rowd,
                  fixed((1, D_MODEL)), fixed((1, D_MODEL)), fixed((D_MODEL, D_MODEL)), fixed((PLE_DIM, D_MODEL))],
        out_specs=[rowd, fixed((D_MODEL, D_MODEL)), fixed((PLE_DIM, D_MODEL)),
                   fixed((8, D_MODEL)), fixed((8, D_MODEL)), fixed((8, 128))],
        out_shape=[jax.ShapeDtypeStruct((t, D_MODEL), F32), jax.ShapeDtypeStruct((D_MODEL, D_MODEL), F32),
                   jax.ShapeDtypeStruct((PLE_DIM, D_MODEL), F32), jax.ShapeDtypeStruct((8, D_MODEL), F32),
                   jax.ShapeDtypeStruct((8, D_MODEL), F32), jax.ShapeDtypeStruct((8, 128), F32)],
        compiler_params=_cp(("arbitrary",), VMEM_LIMIT),
    )(x2, p, target, g_ple, g_final, w_pg, w_pp)


def in_proj_bwd(pieces, weights, x, dx1, g, name="in_proj_bwd", tm=256):
    t = x.shape[0]
    tm = min(tm, t)
    k = len(pieces)
    assert all(c0 % wd == 0 and w0 % wd == 0 for (_, c0, wd), (_, w0) in zip(pieces, weights))

    def body(*refs):
        p_refs, w_refs = refs[:k], refs[k:2 * k]
        x_ref, dx1_ref, g_ref, dx_ref, dg_ref = refs[2 * k:]

        @pl.when(pl.program_id(0) == 0)
        def _():
            dg_ref[...] = jnp.zeros_like(dg_ref)

        dh = _nt(_bf(p_refs[0][...]), w_refs[0][...])
        for pr, wr in zip(p_refs[1:], w_refs[1:]):
            dh = dh + _nt(_bf(pr[...]), wr[...])
        r, n = _rms(x_ref[...])
        dx, dgc = _rms_bwd(dh, g_ref[...], r, n)
        dx_ref[...] = dx1_ref[...] + dx
        dg_ref[...] += jnp.broadcast_to(jnp.sum(dgc, axis=0, keepdims=True), (8, D_MODEL))

    rowd = pl.BlockSpec((tm, D_MODEL), lambda i: (i, 0))
    return pl.pallas_call(
        body, name=name, grid=(t // tm,),
        in_specs=[pl.BlockSpec((tm, wd), functools.partial(lambda i, cb: (i, cb), cb=c0 // wd))
                  for _, c0, wd in pieces]
        + [pl.BlockSpec((w.shape[0], wd), functools.partial(lambda i, cb: (0, cb), cb=w0 // wd))
           for (w, w0), (_, _, wd) in zip(weights, pieces)]
        + [rowd, rowd, pl.BlockSpec((1, D_MODEL), lambda i: (0, 0))],
        out_specs=[rowd, pl.BlockSpec((8, D_MODEL), lambda i: (0, 0))],
        out_shape=[jax.ShapeDtypeStruct((t, D_MODEL), F32), jax.ShapeDtypeStruct((8, D_MODEL), F32)],
        compiler_params=_cp(("arbitrary",), VMEM_LIMIT),
    )(*[a for a, _, _ in pieces], *[w for w, _ in weights], x, dx1, g)


def adamw(w, g, m, v, name, rows_cap=256, dep=None):
    lead = w.shape[:-2]
    r, c = w.shape[-2:]
    tr = r
    for cand in range(8, min(r, rows_cap) + 1, 8):
        if r % cand == 0:
            tr = cand

    def body(w_ref, g_ref, m_ref, v_ref, *rest):
        d_ref, mo_ref, vo_ref = rest[-3:]
        gv = g_ref[...]
        mn = ADAM_B1 * m_ref[...] + (1.0 - ADAM_B1) * gv
        vn = ADAM_B2 * v_ref[...] + (1.0 - ADAM_B2) * (gv * gv)
        m_hat = mn / (1.0 - ADAM_B1 ** ADAM_STEP)
        v_hat = vn / (1.0 - ADAM_B2 ** ADAM_STEP)
        d_ref[...] = -ADAM_LR * (m_hat / (jnp.sqrt(v_hat) + ADAM_EPS) + ADAM_WD * w_ref[...])
        mo_ref[...] = mn
        vo_ref[...] = vn

    spec = pl.BlockSpec((None,) * len(lead) + (tr, c), lambda i: (0,) * len(lead) + (i, 0))
    extra = [] if dep is None else [dep]
    return pl.pallas_call(
        body, name=name, grid=(r // tr,),
        in_specs=[spec] * 4 + [pl.BlockSpec((8, 128), lambda i: (0, 0))] * len(extra), out_specs=[spec] * 3,
        out_shape=[jax.ShapeDtypeStruct(w.shape, F32)] * 3,
        compiler_params=_cp(("parallel",), VMEM_LIMIT),
    )(w, g.reshape(w.shape), m, v, *extra)


def _w_in_shards(dwp):
    cs = D_IN // N_CHIPS
    regions = ((0, SPLIT_Z, P_QA), (SPLIT_Z, SPLIT_Z + 8, P_BD - SPLIT_Z), (SPLIT_Z + 8, D_IN, -(SPLIT_Z + 8)))

    def original(lo, hi):
        parts = [dwp[:, max(lo, a) + off:min(hi, e) + off] for a, e, off in regions if max(lo, a) < min(hi, e)]
        return parts[0] if len(parts) == 1 else jnp.concatenate(parts, axis=1)

    return jnp.stack([original(s * cs, (s + 1) * cs) for s in range(N_CHIPS)])


class Standalone:
    def __init__(self, later_weights):
        self.later_weights = later_weights

    def begin(self, *a):
        return 0.0

    forward = exchange = join = begin

    def finish(self, after):
        return self.later_weights


def local_step(x3d, p3d, target3d, g4, small, later, early):
    b, s, _ = x3d.shape
    t = b * s
    x = x3d.reshape(t, D_MODEL)
    p = p3d.reshape(t, PLE_DIM)
    target = target3d.reshape(t, D_MODEL)
    cut = SPLIT_Z - 2 * (D_IN // N_CHIPS)
    w_inp = jnp.concatenate([g4[2][:, cut + 8:], g4[3], g4[0], g4[1], g4[2][:, :cut], g4[2][:, cut:cut + 8],
                             jnp.zeros((D_MODEL, 120), BF16)], axis=1)
    al_row = jnp.pad(small["a_log"].reshape(1, B_HEADS), ((0, 0), (0, 128 - B_HEADS)))
    dtb_row = jnp.pad(small["dt_bias"].reshape(1, B_HEADS), ((0, 0), (0, 128 - B_HEADS)))
    conv_w8 = jnp.pad(small["conv_w"].reshape(CONV_K, CONV_CH), ((0, 8 - CONV_K), (0, 0)))
    w_on = small["w_onorm"].reshape(1, B_DIM)
    g_mix, g_ffn = small["g_mix"].reshape(1, D_MODEL), small["g_ffn"].reshape(1, D_MODEL)
    g_ple, g_final = small["g_ple"].reshape(1, D_MODEL), small["g_final"].reshape(1, D_MODEL)
    bias_band = make_bias_band(small["rel_bias"].reshape(A_HEADS, N_REL))

    tok = later.begin()
    proj, h1, bd32 = rms_matmul(x, g_mix + tok, w_inp, "in_proj", tm=1024, tn_cap=1152)
    y_a, lse = attn_fwd(proj, bias_band, b, s)
    tok = later.forward(lse)
    c = conv_fwd(proj, conv_w8 + tok, b, s)
    u, wk, qg, kdec, pm, egl, tmat = dn_prep(c, bd32, al_row, dtb_row, b, s)
    o_b, states = dn_scan_fwd(u, wk, qg, kdec, pm, egl, b, s)
    wts = later.finish(o_b)
    x1, merged = mid_fwd(x, y_a, o_b, proj, w_on, wts["w_branch_a"], wts["w_branch_b"], wts["w_out"])
    gate, up, act, h2 = ffn_up(x1, g_ffn, wts["w_gate_up"])
    x2 = matmul_residual(act, wts["w_down"], x1, "ffn_down")

    dx2, dw_pg, dw_pp, dg_ple, dg_final, loss = tail_fwd_bwd(
        x2, p, target, g_ple, g_final, wts["w_ple_gate"], wts["w_ple_proj"])
    dgate, dup = ffn_act_bwd(dx2, gate, up, wts["w_down"])
    w_gu = wts["w_gate_up"]
    dx1, dg_ffn = in_proj_bwd([(dgate, 0, D_FF), (dup, 0, D_FF)], [(w_gu, 0), (w_gu, D_FF)], x1, dx2, g_ffn,
                              name="ffn_in_bwd")
    dw_down = matmul_tn(act, dx2, "dw_down")
    dw_gu = matmul_tn(h2, dgate, "dw_gate", width=2 * D_FF, tiles_major=True)
    dw_gu = matmul_tn(h2, dup, "dw_up", into=dw_gu, col0=D_FF, width=2 * D_FF, tiles_major=True)
    dy_a, do_b, dz, dgates, dw_out, dwa, dwb, dw_on = mid_bwd(
        dx1, merged, y_a, o_b, proj, w_on, wts["w_branch_a"], wts["w_branch_b"], wts["w_out"])
    tok = early.begin(dict(w_branch_a=dwa, w_branch_b=dwb, w_out=dw_out, w_gate_up=dw_gu, w_down=dw_down,
                           w_ple_gate=dw_pg, w_ple_proj=dw_pp))
    ddw, ddwk, ddqg, ddkdec, ddp, ddegl = dn_scan_bwd(u, wk, qg, kdec, pm, egl + tok, states, do_b, b, s)
    tok = early.exchange(ddegl)
    dc3, dbd, dal, ddtb = dn_post_bwd(c, bd32, al_row + tok, dtb_row, tmat, ddw, ddwk, ddqg, ddkdec, ddp, ddegl, b, s)
    dconv, dconv_w = conv_bwd(proj, conv_w8, dc3, b, s)
    dqa, dka, dva, dbt, dbf = attn_bwd(proj, bias_band, y_a, lse, dy_a, b, s)
    tok = early.join(dqa)
    d_rel = bias_band_grad(dbt, dbf)

    pieces = [dgates, dqa, dka, dva, dconv, dz, dbd]
    bounds = [0, 2048, 2560, 3072, 3584, 5120, 5632, 5760]
    windows = [(dgates, 0, 2048), (dqa, 0, 512), (dka, 0, 512), (dva, 0, 512), (dconv, 0, 512), (dconv, 512, 512),
               (dconv, 1024, 512), (dz, 0, 512), (dbd, 0, 128)]
    w_cols = [0, P_QA, P_KA, P_VA, P_CONV, P_CONV + 512, P_CONV + 1024, P_Z, P_BD]
    dx, dg_mix = in_proj_bwd(windows, [(w_inp, c0) for c0 in w_cols], x, dx1, g_mix + tok)
    dwp = None
    for k, pc in enumerate(pieces):
        dwp = matmul_tn(h1, pc, "dw_in_%d" % k, into=dwp, col0=bounds[k], width=P_WIDTH)
    reduced_early = early.finish(dwp)
    dw_in = _w_in_shards(dwp)

    grads = dict(w_in=dw_in, w_branch_a=dwa, w_branch_b=dwb, w_out=dw_out, w_gate_up=dw_gu, w_down=dw_down,
                 w_ple_gate=dw_pg, w_ple_proj=dw_pp)
    small_grads = dict(g_mix=dg_mix[0], g_ffn=dg_ffn[0], g_ple=dg_ple[0], g_final=dg_final[0],
                       conv_w=dconv_w[:CONV_K].reshape(-1), rel_bias=d_rel.reshape(-1), w_onorm=dw_on[0],
                       a_log=dal[0, :B_HEADS], dt_bias=ddtb[0, :B_HEADS], loss=loss[0, :1])
    return dx.reshape(b, s, D_MODEL), grads, small_grads, reduced_early


BIG = (("w_in", (D_MODEL, D_IN), 1), ("w_branch_a", (A_WIDTH, D_MODEL), 1), ("w_branch_b", (B_WIDTH, D_MODEL), 1),
       ("w_out", (D_MODEL, D_MODEL), 0), ("w_gate_up", (D_MODEL, 2 * D_FF), 1), ("w_down", (D_FF, D_MODEL), 0),
       ("w_ple_gate", (D_MODEL, D_MODEL), 0), ("w_ple_proj", (PLE_DIM, D_MODEL), 1))
N_CHIPS = 4
FIRST_WEIGHTS = ("w_in",)
LATER_WEIGHTS = ("w_branch_a", "w_branch_b", "w_out", "w_gate_up", "w_down", "w_ple_gate", "w_ple_proj")
LATE_GRADS = ("w_in",)
EARLY_GRADS = ("w_branch_a", "w_branch_b", "w_out", "w_gate_up", "w_down", "w_ple_gate", "w_ple_proj")


def _items(names):
    return [it for it in BIG if it[0] in names]


def _shard_shape(shape, axis):
    return (shape[0] // N_CHIPS, shape[1]) if axis == 0 else (shape[0], shape[1] // N_CHIPS)


def _width_groups(names):
    groups = {}
    for n, shape, axis in _items(names):
        rs, cs = _shard_shape(shape, axis)
        groups.setdefault(cs, []).append((n, rs))
    return sorted(groups.items())


def grad_buffers(grads, names):
    info = {n: (shape, axis) for n, shape, axis in _items(names)}
    bufs = []
    for cs, members in _width_groups(names):
        segs = []
        for n, rs in members:
            g = grads[n].astype(BF16)
            if g.ndim == 2:
                g = (g.reshape(N_CHIPS, rs, cs) if info[n][1] == 0
                     else jnp.transpose(g.reshape(rs, N_CHIPS, cs), (1, 0, 2)))
            segs.append(g)
        bufs.append(segs[0] if len(segs) == 1 else jnp.concatenate(segs, axis=1))
    return bufs


def split_buffers(reduced, names):
    out = {}
    for (cs, members), buf in zip(_width_groups(names), reduced):
        r0 = 0
        for n, rs in members:
            out[n] = buf[r0:r0 + rs]
            r0 += rs
    return out


def _place():
    return lax.axis_index("x"), lax.axis_index("y"), lax.axis_index("c")


ANY = pl.BlockSpec(memory_space=pl.ANY)


def _gathered_shape(item):
    n, shape, _ = item
    return (N_CHIPS,) + _shard_shape(shape, 1) if n == "w_in" else shape


def _gather_block(o_ref, item, cx, cy, hf):
    n, shape, axis = item
    rs, cs = _shard_shape(shape, axis)
    hr = rs // 2
    ci = 2 * cx + cy
    if n == "w_in":
        return o_ref.at[ci, pl.ds(pl.multiple_of(hf * hr, 16), hr), :]
    if axis == 0:
        return o_ref.at[pl.ds(pl.multiple_of(ci * rs + hf * hr, 16), hr), :]
    return o_ref.at[pl.ds(pl.multiple_of(hf * hr, 16), hr), pl.ds(pl.multiple_of(ci * cs, 128), cs)]


def _own_half(w_ref, item, c):
    hr = _shard_shape(item[1], item[2])[0] // 2
    return w_ref.at[pl.ds(pl.multiple_of(c * hr, 16), hr), :]


def _gather_slot(o_ref, item, cx, cy):
    n, shape, axis = item
    rs, cs = _shard_shape(shape, axis)
    ci = 2 * cx + cy
    if n == "w_in":
        return o_ref.at[ci]
    if axis == 0:
        return o_ref.at[pl.ds(pl.multiple_of(ci * rs, 16), rs), :]
    return o_ref.at[:, pl.ds(pl.multiple_of(ci * cs, 128), cs)]


def _other_chips(x, y):
    return [(1 - x, y), (x, 1 - y), (1 - x, 1 - y)]


def allgather_weights(shards, names, chip):
    items = _items(names)
    nw = len(items)

    def body(*refs):
        w_refs, o_refs = refs[:nw], refs[nw:2 * nw]
        send_sems, recv_sems = refs[2 * nw:]
        x, y, c = _place()
        sibling = (x, y, 1 - c)
        chips = _other_chips(x, y)

        def copy(k, src, dst, to):
            return pltpu.make_async_remote_copy(src_ref=src, dst_ref=dst, send_sem=send_sems.at[k],
                                                recv_sem=recv_sems.at[k], device_id=to, device_id_type=MESH)

        def blk(i, cx, cy, hf):
            return _gather_block(o_refs[i], items[i], cx, cy, hf)

        def my_half(i):
            return _own_half(w_refs[i], items[i], c)

        def own(i):
            return _gather_slot(o_refs[i], items[i], x, y)

        first = [copy(7 * i + j, my_half(i), blk(i, x, y, c), (*chip_, c))
                 for i in range(nw) for j, chip_ in enumerate(chips)]
        first += [copy(7 * i + 6, w_refs[i], own(i), sibling) for i in range(nw)]
        for cp in first:
            cp.start()
        passed = []
        for i in range(nw):
            for j, chip_ in enumerate(chips):
                copy(7 * i + j, my_half(i), blk(i, *chip_, c), (*chip_, c)).wait_recv()
                fwd = copy(7 * i + 3 + j, blk(i, *chip_, c), blk(i, *chip_, c), sibling)
                fwd.start()
                passed.append(fwd)
        for i in range(nw):
            for j, chip_ in enumerate(chips):
                copy(7 * i + 3 + j, my_half(i), blk(i, *chip_, 1 - c), sibling).wait_recv()
            copy(7 * i + 6, w_refs[i], own(i), sibling).wait_recv()
        for cp in first + passed:
            cp.wait_send()

    outs = pl.pallas_call(
        body, name="allgather_weights",
        in_specs=[ANY] * nw, out_specs=[ANY] * nw,
        out_shape=[jax.ShapeDtypeStruct(_gathered_shape(it), BF16) for it in items],
        scratch_shapes=[pltpu.SemaphoreType.DMA((7 * nw,)), pltpu.SemaphoreType.DMA((7 * nw,))],
    )(*[shards[it[0]] for it in items])
    return {it[0]: o for it, o in zip(items, outs)}


HBM_SPEC = pl.BlockSpec(memory_space=pltpu.HBM)
SEM_SPEC = pl.BlockSpec(memory_space=pltpu.SEMAPHORE)
EFFECT = pltpu.SideEffectType.DATAFLOW_SIDE_EFFECTING


def _in_hbm(a):
    return pltpu.with_memory_space_constraint(a, pltpu.HBM)


def copies_start(name, bufs, ncopies, plan):
    nb = len(bufs)

    def body(*refs):
        in_refs, send_sems, recv_sems, token = refs[:nb], refs[nb], refs[nb + 1], refs[-1]
        for k, (src, dst, to) in enumerate(plan(in_refs)):
            pltpu.make_async_remote_copy(src_ref=src, dst_ref=dst, send_sem=send_sems.at[k],
                                         recv_sem=recv_sems.at[k], device_id=to, device_id_type=MESH).start()
        token[...] = jnp.zeros_like(token)

    outs = pl.pallas_call(
        body, name=name,
        in_specs=[HBM_SPEC] * nb,
        out_specs=(SEM_SPEC, SEM_SPEC, *[HBM_SPEC] * nb, pl.BlockSpec(memory_space=pltpu.VMEM)),
        out_shape=(pltpu.SemaphoreType.DMA((ncopies,)), pltpu.SemaphoreType.DMA((ncopies,)),
                   *[pltpu.HBM(b.shape, b.dtype) for b in bufs], jax.ShapeDtypeStruct((8, 128), F32)),
        input_output_aliases={i: 2 + i for i in range(nb)},
        compiler_params=pltpu.CompilerParams(has_side_effects=EFFECT),
    )(*[_in_hbm(b) for b in bufs])
    return outs[0], outs[1], list(outs[2:2 + nb]), outs[-1][0, 0]


def copies_wait(name, send_sems, recv_sems, bufs, after, plan):
    nb = len(bufs)

    def body(*refs):
        in_refs, s_sems, r_sems = refs[:nb], refs[nb], refs[nb + 1]
        for k, (src, dst, to) in enumerate(plan(in_refs)):
            cp = pltpu.make_async_remote_copy(src_ref=src, dst_ref=dst, send_sem=s_sems.at[k],
                                              recv_sem=r_sems.at[k], device_id=to, device_id_type=MESH)
            cp.wait_send()
            cp.wait_recv()

    return list(pl.pallas_call(
        body, name=name,
        in_specs=[HBM_SPEC] * nb + [SEM_SPEC, SEM_SPEC, ANY],
        out_specs=tuple([HBM_SPEC] * nb),
        out_shape=tuple(pltpu.HBM(b.shape, b.dtype) for b in bufs),
        input_output_aliases={i: i for i in range(nb)},
        compiler_params=pltpu.CompilerParams(has_side_effects=EFFECT),
    )(*bufs, send_sems, recv_sems, after))


def _landing(shape, dtype):
    return _in_hbm(lax.empty(shape, dtype))


class LaterWeights:
    def __init__(self, shards, chip):
        self.items = _items(LATER_WEIGHTS)
        self.shards, self.chip = shards, chip
        self.nw = len(self.items)

    def _ici_plan(self, refs):
        x, y, c = _place()
        w_refs, o_refs = refs[:self.nw], refs[self.nw:]
        plan = [(_own_half(w_refs[i], it, c), _gather_block(o_refs[i], it, x, y, c), (*chip_, c))
                for i, it in enumerate(self.items) for chip_ in _other_chips(x, y)]
        return plan + [(w_refs[i], _gather_slot(o_refs[i], it, x, y), (x, y, 1 - c))
                       for i, it in enumerate(self.items)]

    def _d2d_plan(self, refs):
        x, y, c = _place()
        return [(_gather_block(refs[i], it, *chip_, c), _gather_block(refs[i], it, *chip_, c), (x, y, 1 - c))
                for i, it in enumerate(self.items) for chip_ in _other_chips(x, y)]

    def _d2d_wait_plan(self, refs):
        x, y, c = _place()
        return [(_gather_block(refs[i], it, *chip_, c), _gather_block(refs[i], it, *chip_, 1 - c), (x, y, 1 - c))
                for i, it in enumerate(self.items) for chip_ in _other_chips(x, y)]

    def _ici_wait_plan(self, refs):
        x, y, c = _place()
        w_refs, o_refs = refs[:self.nw], refs[self.nw:]
        plan = [(_own_half(w_refs[i], it, c), _gather_block(o_refs[i], it, *chip_, c), (*chip_, c))
                for i, it in enumerate(self.items) for chip_ in _other_chips(x, y)]
        return plan + [(w_refs[i], _gather_slot(o_refs[i], it, x, y), (x, y, 1 - c))
                       for i, it in enumerate(self.items)]

    def begin(self):
        srcs = [self.shards[it[0]] for it in self.items]
        lands = [_landing(_gathered_shape(it), BF16) for it in self.items]
        self.s1, self.r1, self.b1, tok = copies_start("gather_ici_start", srcs + lands, 4 * self.nw, self._ici_plan)
        return tok

    def forward(self, after):
        b1 = copies_wait("gather_ici_wait", self.s1, self.r1, self.b1, after, self._ici_wait_plan)
        self.s2, self.r2, self.b2, tok = copies_start("gather_d2d_start", b1[self.nw:], 3 * self.nw, self._d2d_plan)
        return tok

    def finish(self, after):
        outs = copies_wait("gather_d2d_wait", self.s2, self.r2, self.b2, after, self._d2d_wait_plan)
        return {it[0]: o for it, o in zip(self.items, outs)}


def small_allreduce(v, name):
    r = v.shape[0]

    def body(v_ref, o_ref, buf, send_sems, recv_sems):
        x, y, c = _place()
        me = 4 * x + 2 * y + c
        buf[me] = v_ref[...]
        flips = [(fx, fy, fc) for fx in (0, 1) for fy in (0, 1) for fc in (0, 1)][1:]
        peers = [((1 - x) if fx else x, (1 - y) if fy else y, (1 - c) if fc else c) for fx, fy, fc in flips]

        def copy(k, slot, to):
            return pltpu.make_async_remote_copy(src_ref=v_ref, dst_ref=buf.at[slot], send_sem=send_sems.at[k],
                                                recv_sem=recv_sems.at[k], device_id=to, device_id_type=MESH)

        sends = [copy(k, me, peer) for k, peer in enumerate(peers)]
        for cp in sends:
            cp.start()
        for k, (px, py, pc) in enumerate(peers):
            copy(k, 4 * px + 2 * py + pc, (px, py, pc)).wait_recv()
        for cp in sends:
            cp.wait_send()
        acc = buf[0]
        for d in range(1, 8):
            acc = acc + buf[d]
        o_ref[...] = acc

    return pl.pallas_call(
        body, name=name,
        in_specs=[pl.BlockSpec(memory_space=pltpu.VMEM)], out_specs=pl.BlockSpec(memory_space=pltpu.VMEM),
        out_shape=jax.ShapeDtypeStruct((r, 128), F32),
        scratch_shapes=[pltpu.VMEM((8, r, 128), F32), pltpu.SemaphoreType.DMA((7,)), pltpu.SemaphoreType.DMA((7,))],
    )(v)


def add_halves(g, other, place):
    half, wd = other.shape[1:]
    tr = _tile_rows(half, wd)
    nblk = half // tr

    def body(pref, g0, g1, g2, g3, o0, o1, o2, o3, pf_ref, pb_ref):
        f = lambda r: r[...].astype(F32)
        pf_ref[...] = f(g0) + f(o0)
        pb_ref[0] = _bf(f(g1) + f(o1))
        pb_ref[1] = _bf(f(g2) + f(o2))
        pb_ref[2] = _bf(f(g3) + f(o3))

    gspec = lambda k: pl.BlockSpec((None, tr, wd), lambda i, pr: ((pr[0] + k) % N_CHIPS, pr[1] * nblk + i, 0))
    ospec = lambda k: pl.BlockSpec((None, tr, wd), lambda i, pr: ((pr[0] + k) % N_CHIPS, i, 0))
    return pl.pallas_call(
        body, name="add_halves",
        grid_spec=pltpu.PrefetchScalarGridSpec(
            num_scalar_prefetch=1, grid=(nblk,),
            in_specs=[gspec(0), gspec(1), gspec(2), gspec(3), ospec(0), ospec(1), ospec(2), ospec(3)],
            out_specs=[pl.BlockSpec((tr, wd), lambda i, pr: (i, 0)),
                       pl.BlockSpec((3, tr, wd), lambda i, pr: (0, i, 0))]),
        out_shape=[jax.ShapeDtypeStruct((half, wd), F32), jax.ShapeDtypeStruct((3, half, wd), BF16)],
        compiler_params=_cp(("parallel",), VMEM_LIMIT),
    )(place, g, g, g, g, other, other, other, other)


def _tile_rows(n, width):
    best = 16
    for t in range(16, max(16, (384 * 1024) // width) + 1, 16):
        if n % t == 0:
            best = t
    assert n % best == 0
    return best


def add_partials(pf, got, place):
    half, wd = pf.shape
    tr = _tile_rows(half, wd)

    def body(pref, pf_ref, got_ref, o_ref):
        o_ref[...] = ((pf_ref[...] + got_ref[0].astype(F32)) + got_ref[1].astype(F32)) + got_ref[2].astype(F32)

    return pl.pallas_call(
        body, name="add_partials",
        grid_spec=pltpu.PrefetchScalarGridSpec(
            num_scalar_prefetch=1, grid=(half // tr,),
            in_specs=[pl.BlockSpec((tr, wd), lambda i, pr: (i, 0)),
                      pl.BlockSpec((3, tr, wd), lambda i, pr: (0, i, 0))],
            out_specs=pl.BlockSpec((None, tr, wd), lambda i, pr: (pr[1], i, 0))),
        out_shape=jax.ShapeDtypeStruct((2, half, wd), F32),
        compiler_params=_cp(("parallel",), VMEM_LIMIT),
    )(place, pf, got)


class GradReduce:
    def __init__(self, place, names, tag):
        self.place, self.names, self.tag = place, names, tag
        self.nb = len(_width_groups(names))

    def _swap_plan(self, refs):
        x, y, c = _place()
        plan = []
        for g_ref, o_ref in zip(refs[:self.nb], refs[self.nb:]):
            half = o_ref.shape[1]
            plan.append((g_ref.at[:, pl.ds(pl.multiple_of((1 - c) * half, 16), half), :], o_ref, (x, y, 1 - c)))
        return plan

    def _exchange_plan(self, refs):
        x, y, c = _place()
        me = 2 * x + y
        return [(p_ref.at[k - 1], o_ref.at[k - 1], (((me + k) % N_CHIPS) // 2, ((me + k) % N_CHIPS) % 2, c))
                for p_ref, o_ref in zip(refs[:self.nb], refs[self.nb:]) for k in range(1, N_CHIPS)]

    def _join_plan(self, refs):
        x, y, c = _place()
        return [(r.at[c], r.at[c], (x, y, 1 - c)) for r in refs]

    def _join_wait_plan(self, refs):
        x, y, c = _place()
        return [(r.at[c], r.at[1 - c], (x, y, 1 - c)) for r in refs]

    def begin(self, grads):
        gs = grad_buffers(grads, self.names)
        lands = [_landing((N_CHIPS, g.shape[1] // 2, g.shape[2]), BF16) for g in gs]
        self.s1, self.r1, self.b1, tok = copies_start(self.tag + "_swap_start", gs + lands, self.nb, self._swap_plan)
        return tok

    def exchange(self, after):
        b1 = copies_wait(self.tag + "_swap_wait", self.s1, self.r1, self.b1, after, self._swap_plan)
        sums = [add_halves(g, other, self.place) for g, other in zip(b1[:self.nb], b1[self.nb:])]
        self.pfs = [pf for pf, _ in sums]
        pbs = [pb for _, pb in sums]
        lands = [_landing(pb.shape, BF16) for pb in pbs]
        self.s2, self.r2, self.b2, tok = copies_start(self.tag + "_exchange_start", pbs + lands, 3 * self.nb,
                                                      self._exchange_plan)
        return tok

    def join(self, after):
        b2 = copies_wait(self.tag + "_exchange_wait", self.s2, self.r2, self.b2, after, self._exchange_plan)
        boths = [add_partials(pf, got, self.place) for pf, got in zip(self.pfs, b2[self.nb:])]
        self.s3, self.r3, self.b3, tok = copies_start(self.tag + "_join_start", boths, self.nb, self._join_plan)
        return tok

    def finish(self, after):
        boths = copies_wait(self.tag + "_join_wait", self.s3, self.r3, self.b3, after, self._join_wait_plan)
        return split_buffers([b.reshape(-1, b.shape[2]) for b in boths], self.names)


SMALL = (("g_mix", D_MODEL), ("g_ffn", D_MODEL), ("g_ple", D_MODEL), ("g_final", D_MODEL),
         ("conv_w", CONV_K * CONV_CH), ("rel_bias", A_HEADS * N_REL), ("w_onorm", B_DIM),
         ("a_log", B_HEADS), ("dt_bias", B_HEADS), ("loss", 1))


def _pad128(v):
    v = v.reshape(-1)
    return jnp.pad(v, (0, -v.shape[0] % 128))


def pack_small(d, names, rows):
    flat = jnp.concatenate([_pad128(d[n]) for n in names]).reshape(-1, 128)
    return jnp.pad(flat, ((0, rows - flat.shape[0]), (0, 0)))


def unpack_small(flat, names_sizes):
    out, r0 = {}, 0
    v = flat.reshape(-1)
    for n, size in names_sizes:
        out[n] = v[r0:r0 + size]
        r0 += -(-size // 128) * 128
    return out


def kernel(x, p, g_mix, w_in, conv_w, a_log, dt_bias, rel_bias, w_onorm, w_branch_a, w_branch_b, w_out, g_ffn, w_gate_up, w_down, g_ple, w_ple_gate, w_ple_proj, g_final, loss_target, m_g_mix, m_w_in, m_conv_w, m_a_log, m_dt_bias, m_rel_bias, m_w_onorm, m_w_branch_a, m_w_branch_b, m_w_out, m_g_ffn, m_w_gate_up, m_w_down, m_g_ple, m_w_ple_gate, m_w_ple_proj, m_g_final, v_g_mix, v_w_in, v_conv_w, v_a_log, v_dt_bias, v_rel_bias, v_w_onorm, v_w_branch_a, v_w_branch_b, v_w_out, v_g_ffn, v_w_gate_up, v_w_down, v_g_ple, v_w_ple_gate, v_w_ple_proj, v_g_final):
    names = ["g_mix", "w_in", "conv_w", "a_log", "dt_bias", "rel_bias", "w_onorm", "w_branch_a", "w_branch_b",
             "w_out", "g_ffn", "w_gate_up", "w_down", "g_ple", "w_ple_gate", "w_ple_proj", "g_final"]
    w = dict(zip(names, [g_mix, w_in, conv_w, a_log, dt_bias, rel_bias, w_onorm, w_branch_a, w_branch_b, w_out,
                         g_ffn, w_gate_up, w_down, g_ple, w_ple_gate, w_ple_proj, g_final]))
    m = dict(zip(names, [m_g_mix, m_w_in, m_conv_w, m_a_log, m_dt_bias, m_rel_bias, m_w_onorm, m_w_branch_a,
                         m_w_branch_b, m_w_out, m_g_ffn, m_w_gate_up, m_w_down, m_g_ple, m_w_ple_gate,
                         m_w_ple_proj, m_g_final]))
    v = dict(zip(names, [v_g_mix, v_w_in, v_conv_w, v_a_log, v_dt_bias, v_rel_bias, v_w_onorm, v_w_branch_a,
                         v_w_branch_b, v_w_out, v_g_ffn, v_w_gate_up, v_w_down, v_g_ple, v_w_ple_gate,
                         v_w_ple_proj, v_g_final]))
    xi, yi, ci = _place()
    chip = 2 * xi + yi
    big_names = [n for n, _, _ in BIG]

    shards2d = {n: w[n].reshape(w[n].shape[-2:]) for n in big_names}
    shards_bf = {n: a.astype(BF16) for n, a in shards2d.items()}
    g4 = allgather_weights(shards_bf, FIRST_WEIGHTS, chip)["w_in"]
    place = jnp.stack([chip, ci]).astype(jnp.int32)
    conv_sh = jnp.where(ci == 0, w["conv_w"].reshape(CONV_K, CONV_CH // N_CHIPS), 0.0)
    conv_slots = lax.dynamic_update_slice(jnp.zeros((N_CHIPS, CONV_K, CONV_CH // N_CHIPS), F32), conv_sh[None],
                                          (chip, 0, 0))
    conv_all = small_allreduce(conv_slots.reshape(-1, 128), "gather_conv_w")
    conv_full = jnp.transpose(conv_all.reshape(N_CHIPS, CONV_K, CONV_CH // N_CHIPS), (1, 0, 2)).reshape(CONV_K, CONV_CH)
    small = {n: w[n] for n in names if n not in big_names}
    small["conv_w"] = conv_full

    grad_x, grads, small_grads, reduced_early = local_step(
        x, p[0], loss_target, g4, small, LaterWeights(shards_bf, chip), GradReduce(place, EARLY_GRADS, "grads"))

    late = GradReduce(place, LATE_GRADS, "late")
    tok = late.begin(grads)
    small_names = [n for n, _ in SMALL]
    small_grads["loss"] = small_grads["loss"] + tok
    red_flat = small_allreduce(pack_small(small_grads, small_names, 112), "allreduce_small")
    red = unpack_small(red_flat, SMALL)
    dep = jnp.full((8, 128), late.exchange(red_flat), F32)
    gshard = dict(reduced_early)
    loss = red["loss"][0]
    conv_g = lax.dynamic_slice(red["conv_w"].reshape(CONV_K, N_CHIPS, CONV_CH // N_CHIPS), (0, chip, 0),
                               (CONV_K, 1, CONV_CH // N_CHIPS))
    gsmall = {n: red[n].reshape(w[n].shape) for n in small_names if n not in ("loss", "conv_w")}
    gsmall["conv_w"] = conv_g.reshape(w["conv_w"].shape)

    grad, delta, new_m, new_v = {}, {}, {}, {}
    for n in list(EARLY_GRADS) + list(LATE_GRADS):
        if n in LATE_GRADS:
            late.join(v_)
            gshard.update(late.finish(v_))
        shp = w[n].shape
        d_, m_, v_ = adamw(shards2d[n], gshard[n], m[n].reshape(shp[-2:]), v[n].reshape(shp[-2:]), "adamw_" + n,
                           dep=dep if n in EARLY_GRADS else None)
        dep, v_ = lax.optimization_barrier((dep, v_))
        grad[n], delta[n], new_m[n], new_v[n] = gshard[n].reshape(shp), d_.reshape(shp), m_.reshape(shp), v_.reshape(shp)
    snames = [n for n in small_names if n != "loss"]
    ssizes = [(n, w[n].size) for n in snames]
    pk = lambda d: pack_small(d, snames, 64)
    d_, m_, v_ = adamw(pk(w), pk(gsmall), pk(m), pk(v), "adamw_small")
    ds, ms, vs = unpack_small(d_, ssizes), unpack_small(m_, ssizes), unpack_small(v_, ssizes)
    for n in snames:
        shp = w[n].shape
        grad[n], delta[n], new_m[n], new_v[n] = gsmall[n], ds[n].reshape(shp), ms[n].reshape(shp), vs[n].reshape(shp)

    return (loss, grad_x, *[grad[n] for n in names], *[delta[n] for n in names],
            *[new_m[n] for n in names], *[new_v[n] for n in names])
```

```python
import functools

import jax
import jax.numpy as jnp
from jax import lax
from jax.experimental import pallas as pl
from jax.experimental.pallas import tpu as pltpu

F32 = jnp.float32
BF16 = jnp.bfloat16
MESH = pl.DeviceIdType.MESH

D_MODEL = 1024
CHUNK = 64
PLE_DIM = 256
EPS = 1e-6
A_HEADS = 8
A_HEAD_DIM = 64
A_WIDTH = 512
A_LOOKBACK = 8
BAND = (A_LOOKBACK + 1) * CHUNK
TAIL = 3 * CHUNK
REL_CLIP = 128
N_REL = 2 * REL_CLIP + 1
B_HEADS = 4
B_DIM = 128
B_WIDTH = 512
CONV_K = 4
CONV_CH = 1536
D_FF = 2816
SPLIT_Z = 3584
D_IN = 5640
ADAM_LR, ADAM_B1, ADAM_B2, ADAM_EPS, ADAM_WD, ADAM_STEP = 0.001, 0.9, 0.999, 1e-08, 0.01, 10

P_GATES, P_QA, P_KA, P_VA, P_CONV, P_Z, P_BD, P_WIDTH = 0, 2048, 2560, 3072, 3584, 5120, 5632, 5760

VMEM_LIMIT = 56 * 1024 * 1024


def _cp(sem, vmem=None, **kw):
    return pltpu.CompilerParams(dimension_semantics=sem, vmem_limit_bytes=vmem, **kw)


def _tile(n, cap):
    best = None
    for t in range(128, cap + 1, 128):
        if n % t == 0:
            best = t
    assert best is not None, (n, cap)
    return best


def _nn(a, b, prec=None):
    return lax.dot_general(a, b, (((1,), (0,)), ((), ())), preferred_element_type=F32, precision=prec)


def _nt(a, b, prec=None):
    return lax.dot_general(a, b, (((1,), (1,)), ((), ())), preferred_element_type=F32, precision=prec)


def _tn(a, b, prec=None):
    return lax.dot_general(a, b, (((0,), (0,)), ((), ())), preferred_element_type=F32, precision=prec)


def _bnn(a, b, prec=None):
    return lax.dot_general(a, b, (((2,), (1,)), ((0,), (0,))), preferred_element_type=F32, precision=prec)


def _bnt(a, b, prec=None):
    return lax.dot_general(a, b, (((2,), (2,)), ((0,), (0,))), preferred_element_type=F32, precision=prec)


def _bf(a):
    return a.astype(BF16)


def _split(a):
    hi = a.astype(BF16)
    return hi, (a - hi.astype(F32)).astype(BF16)


def _split3(a):
    h1 = _bf(a)
    r1 = a - h1.astype(F32)
    h2 = _bf(r1)
    return h1, h2, _bf(r1 - h2.astype(F32))


def _bnn_exact(lhs_b, rhs):
    h1, h2, h3 = _split3(rhs)
    return _bnn(lhs_b, h1) + (_bnn(lhs_b, h2) + _bnn(lhs_b, h3))


def _bnn3(a, b):
    ah, al = a if isinstance(a, tuple) else _split(a)
    bh, bl = b if isinstance(b, tuple) else _split(b)
    return _bnn(ah, bh) + (_bnn(ah, bl) + _bnn(al, bh))


def _sigmoid(x):
    return 0.5 * jnp.tanh(0.5 * x) + 0.5


def _softplus(x):
    return jnp.maximum(x, 0.0) + jnp.log(1.0 + jnp.exp(-jnp.abs(x)))


def rms_matmul(x, g, w, name, tm=512, tn_cap=1024):
    t, d = x.shape
    n = w.shape[1]
    tm = min(tm, t)
    tn = _tile(n, tn_cap)

    nj = n // tn

    def body(x_ref, g_ref, w_ref, o_ref, h_ref, tail_ref):
        @pl.when(pl.program_id(1) == 0)
        def _():
            xv = x_ref[...]
            r = lax.rsqrt(jnp.mean(xv * xv, axis=-1, keepdims=True) + EPS)
            h_ref[...] = _bf(xv * r * g_ref[...])

        res = _nn(h_ref[...], w_ref[...])
        o_ref[...] = _bf(res)

        @pl.when(pl.program_id(1) == nj - 1)
        def _():
            tail_ref[...] = res[:, tn - 128:]

    return pl.pallas_call(
        body, name=name, grid=(t // tm, nj),
        in_specs=[pl.BlockSpec((tm, d), lambda i, j: (i, 0)),
                  pl.BlockSpec((1, d), lambda i, j: (0, 0)),
                  pl.BlockSpec((d, tn), lambda i, j: (0, j))],
        out_specs=[pl.BlockSpec((tm, tn), lambda i, j: (i, j)),
                   pl.BlockSpec((tm, d), lambda i, j: (i, 0)),
                   pl.BlockSpec((tm, 128), lambda i, j: (i, 0))],
        out_shape=[jax.ShapeDtypeStruct((t, n), BF16), jax.ShapeDtypeStruct((t, d), BF16),
                   jax.ShapeDtypeStruct((t, 128), F32)],
        compiler_params=_cp(("parallel", "arbitrary"), VMEM_LIMIT),
    )(x, g, w)


def matmul_tn(a, b, name, into=None, col0=0, width=None, tm=1024, tk_cap=1408, tn_cap=1408, tiles_major=False):
    m, k1 = a.shape
    n = b.shape[1]
    tm = min(tm, m)
    tk = _tile(k1, tk_cap)
    tn = _tile(n, tn_cap)
    while col0 % tn:
        tn = _tile(n, tn - 128)
    nk = m // tm
    c0 = col0 // tn

    def body(*refs):
        a_ref, b_ref, o_ref, acc = refs[0], refs[1], refs[-2], refs[-1]

        @pl.when(pl.program_id(2) == 0)
        def _():
            acc[...] = jnp.zeros_like(acc)

        acc[...] += _tn(_bf(a_ref[...]), _bf(b_ref[...]))

        @pl.when(pl.program_id(2) == nk - 1)
        def _():
            o_ref[...] = _bf(acc[...])

    in_specs = [pl.BlockSpec((tm, tk), lambda i, j, k: (k, i)),
                pl.BlockSpec((tm, tn), lambda i, j, k: (k, j))]
    args = [a, b]
    total = n if width is None else width
    aliases = {}
    if into is not None:
        in_specs.append(ANY)
        args.append(into)
        aliases = {2: 0}
    if tiles_major:
        out_spec = pl.BlockSpec((None, tk, tn), lambda i, j, k: (c0 + j, i, 0))
        out_shape = jax.ShapeDtypeStruct((total // tn, k1, tn), BF16)
    else:
        out_spec = pl.BlockSpec((tk, tn), lambda i, j, k: (i, c0 + j))
        out_shape = jax.ShapeDtypeStruct((k1, total), BF16)
    return pl.pallas_call(
        body, name=name, grid=(k1 // tk, n // tn, nk),
        in_specs=in_specs,
        out_specs=out_spec,
        out_shape=out_shape,
        scratch_shapes=[pltpu.VMEM((tk, tn), F32)],
        input_output_aliases=aliases,
        compiler_params=_cp(("parallel", "parallel", "arbitrary"), VMEM_LIMIT),
    )(*args)


def _tail_onehot(qi):
    r = lax.broadcasted_iota(jnp.int32, (384, TAIL), 0)
    kj = lax.broadcasted_iota(jnp.int32, (384, TAIL), 1)
    return (r == jnp.minimum(REL_CLIP + qi - kj, REL_CLIP) + REL_CLIP).astype(F32)


def bias_tail(rel_pad):
    def body(rb_ref, o_ref):
        parts = _split3(rb_ref[...])
        for qi in range(CHUNK):
            oh = _bf(_tail_onehot(qi))
            o_ref[qi] = _nn(parts[0], oh) + (_nn(parts[1], oh) + _nn(parts[2], oh))

    return pl.pallas_call(
        body, name="bias_tail",
        out_shape=jax.ShapeDtypeStruct((CHUNK, A_HEADS, TAIL), F32),
    )(rel_pad)


def bias_grad(db_t, db_far):
    def body(t_ref, f_ref, o_ref):
        acc = jnp.zeros((A_HEADS, 384), F32)
        for qi in range(CHUNK):
            oh = _bf(_tail_onehot(qi))
            parts = _split3(t_ref[qi])
            acc = acc + (_nt(parts[0], oh) + (_nt(parts[1], oh) + _nt(parts[2], oh)))
        far = jnp.sum(jnp.sum(f_ref[...], axis=2), axis=1, keepdims=True)
        lane = lax.broadcasted_iota(jnp.int32, (A_HEADS, 384), 1)
        o_ref[...] = acc + jnp.where(lane == 2 * REL_CLIP, far, 0.0)

    return pl.pallas_call(
        body, name="bias_grad",
        out_shape=jax.ShapeDtypeStruct((A_HEADS, 384), F32),
    )(db_t, db_far)


ATT_CB = 8


WIN = BAND + CHUNK


def _stack_heads(a, lane):
    return jnp.concatenate([jnp.where(lane < 64, a, 0.0), jnp.where(lane >= 64, a, 0.0)], axis=0)


def _fill_band_pads(k_ref, v_ref, kp, vp, s):
    z = jnp.zeros((A_LOOKBACK * CHUNK, 128), BF16)
    kp[pl.ds(0, A_LOOKBACK * CHUNK), :] = z
    vp[pl.ds(0, A_LOOKBACK * CHUNK), :] = z
    kp[pl.ds(A_LOOKBACK * CHUNK, s), :] = _bf(k_ref[...])
    vp[pl.ds(A_LOOKBACK * CHUNK, s), :] = _bf(v_ref[...])


def attn_fwd(proj, bias_band, b, s):
    t = b * s
    nc = s // CHUNK
    qb, kb_, vb_ = P_QA // 128, P_KA // 128, P_VA // 128

    nstep = nc // ATT_CB
    rows = ATT_CB * CHUNK

    def body(q_ref, k_ref, v_ref, b_ref, o_ref, lse_ref, kp, vp):
        n0 = pl.program_id(2) * ATT_CB

        @pl.when(n0 == 0)
        def _():
            _fill_band_pads(k_ref, v_ref, kp, vp, s)

        lane = lax.broadcasted_iota(jnp.int32, (2 * CHUNK, 128), 1)
        col = lax.broadcasted_iota(jnp.int32, (4 * CHUNK, WIN), 1)
        bias4 = b_ref[...]

        def pair(pp, carry):
            n = n0 + 2 * pp
            r0 = pl.multiple_of(pp * 2 * CHUNK, 2 * CHUNK)
            start = pl.multiple_of(n * CHUNK, CHUNK)
            kb = kp[pl.ds(start, WIN), :]
            vb = vp[pl.ds(start, WIN), :]
            q4 = _stack_heads(q_ref[pl.ds(r0, 2 * CHUNK), :] * (A_HEAD_DIM ** -0.5), lane)
            sc = jnp.where(col >= (A_LOOKBACK - n) * CHUNK, _nt(_bf(q4), kb) + bias4, -1e30)
            mx = jnp.max(sc, axis=1, keepdims=True)
            p = jnp.exp(sc - mx)
            l = jnp.sum(p, axis=1, keepdims=True)
            o4 = _nn(_bf(p), vb) / l
            lse4 = mx + jnp.log(l)
            o_ref[pl.ds(r0, 2 * CHUNK), :] = jnp.where(lane < 64, o4[:2 * CHUNK], o4[2 * CHUNK:])
            lse_ref[pl.ds(r0, 2 * CHUNK), :] = jnp.where(lane < 64, lse4[:2 * CHUNK], lse4[2 * CHUNK:])
            return carry

        lax.fori_loop(0, ATT_CB // 2, pair, 0, unroll=2)

    return pl.pallas_call(
        body, name="attn_fwd", grid=(b, 4, nstep),
        in_specs=[pl.BlockSpec((rows, 128), lambda bb, m, n: (bb * nstep + n, qb + m)),
                  pl.BlockSpec((s, 128), lambda bb, m, n: (bb, kb_ + m)),
                  pl.BlockSpec((s, 128), lambda bb, m, n: (bb, vb_ + m)),
                  pl.BlockSpec((None, 4 * CHUNK, WIN), lambda bb, m, n: (m, 0, 0))],
        out_specs=[pl.BlockSpec((rows, 128), lambda bb, m, n: (bb * nstep + n, m)),
                   pl.BlockSpec((rows, 128), lambda bb, m, n: (bb * nstep + n, m))],
        out_shape=[jax.ShapeDtypeStruct((t, A_WIDTH), F32), jax.ShapeDtypeStruct((t, A_WIDTH), F32)],
        scratch_shapes=[pltpu.VMEM((s + A_LOOKBACK * CHUNK, 128), BF16),
                        pltpu.VMEM((s + A_LOOKBACK * CHUNK, 128), BF16)],
        compiler_params=_cp(("parallel", "parallel", "arbitrary"), VMEM_LIMIT),
    )(proj, proj, proj, bias_band)


def attn_bwd(proj, bias_band, y_a, lse, dy_a, b, s):
    t = b * s
    nc = s // CHUNK
    qb, kb_, vb_ = P_QA // 128, P_KA // 128, P_VA // 128
    pad = A_LOOKBACK * CHUNK
    nstep = nc // ATT_CB
    rows = ATT_CB * CHUNK

    def body(q_ref, k_ref, v_ref, b_ref, do_ref, o_ref, lse_ref,
             dq_ref, dk_ref, dv_ref, dbt_ref, dbf_ref, kp, vp, dkp, dvp):
        bb = pl.program_id(1)
        n0 = pl.program_id(2) * ATT_CB

        @pl.when(n0 == 0)
        def _():
            _fill_band_pads(k_ref, v_ref, kp, vp, s)
            dkp[...] = jnp.zeros_like(dkp)
            dvp[...] = jnp.zeros_like(dvp)

        @pl.when((n0 == 0) & (bb == 0))
        def _():
            dbt_ref[...] = jnp.zeros_like(dbt_ref)
            dbf_ref[...] = jnp.zeros_like(dbf_ref)

        lane = lax.broadcasted_iota(jnp.int32, (2 * CHUNK, 128), 1)
        col = lax.broadcasted_iota(jnp.int32, (4 * CHUNK, WIN), 1)
        bias4 = b_ref[...]

        def pair(pp, carry):
            n = n0 + 2 * pp
            r0 = pl.multiple_of(pp * 2 * CHUNK, 2 * CHUNK)
            start = pl.multiple_of(n * CHUNK, CHUNK)
            kb = kp[pl.ds(start, WIN), :]
            vb = vp[pl.ds(start, WIN), :]
            q4b = _bf(_stack_heads(q_ref[pl.ds(r0, 2 * CHUNK), :] * (A_HEAD_DIM ** -0.5), lane))
            do4 = _stack_heads(do_ref[pl.ds(r0, 2 * CHUNK), :], lane)
            do4b = _bf(do4)
            o = o_ref[pl.ds(r0, 2 * CHUNK), :]
            lsev = lse_ref[pl.ds(r0, 2 * CHUNK), :]
            lse4 = jnp.concatenate([lsev[:, 0:1], lsev[:, 64:65]], axis=0)
            qk = _nt(q4b, kb)
            dp = _nt(do4b, vb)
            delta = jnp.sum(do4 * jnp.concatenate([o, o], axis=0), axis=1, keepdims=True)
            sc = jnp.where(col >= (A_LOOKBACK - n) * CHUNK, qk + bias4, -1e30)
            p = jnp.exp(sc - lse4)
            pb = _bf(p)
            ds = p * (dp - delta)
            dsb = _bf(ds)
            dv_w = _tn(pb, do4b)
            dq4 = _nn(dsb, kb)
            dk_w = _tn(dsb, q4b)
            dq_ref[pl.ds(r0, 2 * CHUNK), :] = _bf(
                jnp.where(lane < 64, dq4[:2 * CHUNK], dq4[2 * CHUNK:]) * (A_HEAD_DIM ** -0.5))
            dkp[pl.ds(start, WIN), :] += dk_w
            dvp[pl.ds(start, WIN), :] += dv_w
            dbt_ref[...] += ds[:, WIN - 256:]
            dbf_ref[...] += ds[:, 0:128] + ds[:, 128:256] + ds[:, 256:384]
            return carry

        lax.fori_loop(0, ATT_CB // 2, pair, 0, unroll=2)

        @pl.when(n0 == nc - ATT_CB)
        def _():
            dk_ref[...] = _bf(dkp[pl.ds(pad, s), :])
            dv_ref[...] = _bf(dvp[pl.ds(pad, s), :])

    return pl.pallas_call(
        body, name="attn_bwd", grid=(4, b, nstep),
        in_specs=[pl.BlockSpec((rows, 128), lambda m, bb, n: (bb * nstep + n, qb + m)),
                  pl.BlockSpec((s, 128), lambda m, bb, n: (bb, kb_ + m)),
                  pl.BlockSpec((s, 128), lambda m, bb, n: (bb, vb_ + m)),
                  pl.BlockSpec((None, 4 * CHUNK, WIN), lambda m, bb, n: (m, 0, 0)),
                  pl.BlockSpec((rows, 128), lambda m, bb, n: (bb * nstep + n, m)),
                  pl.BlockSpec((rows, 128), lambda m, bb, n: (bb * nstep + n, m)),
                  pl.BlockSpec((rows, 128), lambda m, bb, n: (bb * nstep + n, m))],
        out_specs=[pl.BlockSpec((rows, 128), lambda m, bb, n: (bb * nstep + n, m)),
                   pl.BlockSpec((s, 128), lambda m, bb, n: (bb, m)),
                   pl.BlockSpec((s, 128), lambda m, bb, n: (bb, m)),
                   pl.BlockSpec((None, 4 * CHUNK, 256), lambda m, bb, n: (m, 0, 0)),
                   pl.BlockSpec((None, 4 * CHUNK, 128), lambda m, bb, n: (m, 0, 0))],
        out_shape=[jax.ShapeDtypeStruct((t, A_WIDTH), BF16)] * 3
        + [jax.ShapeDtypeStruct((4, 4 * CHUNK, 256), F32),
           jax.ShapeDtypeStruct((4, 4 * CHUNK, 128), F32)],
        scratch_shapes=[pltpu.VMEM((s + pad, 128), BF16), pltpu.VMEM((s + pad, 128), BF16),
                        pltpu.VMEM((s + pad, 128), F32), pltpu.VMEM((s + pad, 128), F32)],
        compiler_params=_cp(("parallel", "arbitrary", "arbitrary"), VMEM_LIMIT),
    )(proj, proj, proj, bias_band, dy_a, y_a, lse)


def _conv_taps(x, w, s):
    row = lax.broadcasted_iota(jnp.int32, x.shape, 0)
    shifted = [x] + [jnp.where(row >= i, pltpu.roll(x, i, 0), 0.0) for i in range(1, CONV_K)]
    acc = shifted[0] * w[CONV_K - 1:CONV_K, :]
    for i in range(1, CONV_K):
        acc = acc + shifted[i] * w[CONV_K - 1 - i:CONV_K - i, :]
    return acc, shifted


def conv_fwd(proj, conv_w8, b, s):
    cb = 512
    c0 = P_CONV // cb

    def body(x_ref, w_ref, o_ref):
        a, _ = _conv_taps(x_ref[...].astype(F32), w_ref[...], s)
        o_ref[...] = a * _sigmoid(a)

    return pl.pallas_call(
        body, name="conv_fwd", grid=(b, CONV_CH // cb),
        in_specs=[pl.BlockSpec((s, cb), lambda bb, j: (bb, c0 + j)),
                  pl.BlockSpec((8, cb), lambda bb, j: (0, j))],
        out_specs=pl.BlockSpec((s, cb), lambda bb, j: (bb, j)),
        out_shape=jax.ShapeDtypeStruct((b * s, CONV_CH), F32),
        compiler_params=_cp(("parallel", "parallel"), VMEM_LIMIT),
    )(proj, conv_w8)


def conv_bwd(proj, conv_w8, dc3, b, s):
    cb = 512
    c0 = P_CONV // cb

    def body(x_ref, w_ref, dc_ref, dx_ref, dw_ref):
        @pl.when(pl.program_id(1) == 0)
        def _():
            dw_ref[...] = jnp.zeros_like(dw_ref)

        w = w_ref[...]
        a, shifted = _conv_taps(x_ref[...].astype(F32), w, s)
        sg = _sigmoid(a)
        da = dc_ref[...] * (sg * (1.0 + a * (1.0 - sg)))
        row = lax.broadcasted_iota(jnp.int32, da.shape, 0)
        dx = da * w[CONV_K - 1:CONV_K, :]
        for i in range(1, CONV_K):
            dx = dx + jnp.where(row < s - i, pltpu.roll(da, s - i, 0), 0.0) * w[CONV_K - 1 - i:CONV_K - i, :]
        dx_ref[...] = _bf(dx)
        r8 =lax.broadcasted_iota(jnp.int32, (8, cb), 0)
        dw = jnp.zeros((8, cb), F32)
        for i in range(CONV_K):
            dw = dw + jnp.where(r8 == CONV_K - 1 - i, jnp.sum(da * shifted[i], axis=0, keepdims=True), 0.0)
        dw_ref[...] += dw

    return pl.pallas_call(
        body, name="conv_bwd", grid=(CONV_CH // cb, b),
        in_specs=[pl.BlockSpec((s, cb), lambda j, bb: (bb, c0 + j)),
                  pl.BlockSpec((8, cb), lambda j, bb: (0, j)),
                  pl.BlockSpec((None, s, cb), lambda j, bb: (j, bb, 0))],
        out_specs=[pl.BlockSpec((s, cb), lambda j, bb: (bb, j)),
                   pl.BlockSpec((8, cb), lambda j, bb: (0, j))],
        out_shape=[jax.ShapeDtypeStruct((b * s, CONV_CH), BF16), jax.ShapeDtypeStruct((8, CONV_CH), F32)],
        compiler_params=_cp(("parallel", "arbitrary"), VMEM_LIMIT),
    )(proj, conv_w8, dc3)


def _pick_lane(v, k):
    lane = lax.broadcasted_iota(jnp.int32, v.shape, 1)
    return jnp.sum(jnp.where(lane == k, v, 0.0), axis=1, keepdims=True)


def _chunk_masks(ncb):
    i = lax.broadcasted_iota(jnp.int32, (ncb, CHUNK, CHUNK), 1)
    j = lax.broadcasted_iota(jnp.int32, (ncb, CHUNK, CHUNK), 2)
    return i, j


def _col_of_row(rowvec, eye):
    return jnp.sum(jnp.where(eye, rowvec, 0.0), axis=2, keepdims=True)


def _dn_chunk_math(cq, ck, cv, bd, al_row, dtb_row, h, ncb, tm=None):
    r = ncb * CHUNK
    i, j = _chunk_masks(ncb)
    eye = i == j
    low = i >= j
    strict = i > j
    ones = jnp.ones((ncb, CHUNK, CHUNK), F32)

    braw = _pick_lane(bd, h)
    draw = _pick_lane(bd, B_HEADS + h)
    al = _pick_lane(al_row, h)
    dtb = _pick_lane(dtb_row, h)
    ea = jnp.exp(al)
    beta = _sigmoid(braw)
    sp_arg = draw + dtb
    g = -ea * _softplus(sp_arg)

    rq = lax.rsqrt(jnp.sum(cq * cq, axis=1, keepdims=True) + EPS)
    rk = lax.rsqrt(jnp.sum(ck * ck, axis=1, keepdims=True) + EPS)
    nq = cq * rq
    kn = ck * rk
    qn = nq * (B_DIM ** -0.5)

    def c3(a):
        return a.reshape(ncb, CHUNK, a.shape[-1])

    qn3, kn3, v3, beta3 = c3(qn), c3(kn), c3(cv), c3(beta)
    gb = jnp.broadcast_to(c3(g), (ncb, CHUNK, CHUNK))
    gc_b = _bnn_exact(low.astype(BF16), gb)
    gr_b = _bnn_exact(_bf(ones), jnp.where(eye, gc_b, 0.0))
    dm = jnp.where(low, jnp.exp(jnp.where(low, gc_b - gr_b, 0.0)), 0.0)
    gc = gc_b[:, :, 0:1]
    gl = gc_b[:, CHUNK - 1:CHUNK, 0:1]
    gam = jnp.exp(gc)
    egl = jnp.exp(gl)
    edec = jnp.exp(gl - gc)

    knb = _bf(kn3)
    kk = _bnt(knb, knb)
    kd = jnp.where(strict, kk * dm, 0.0)
    a = beta3 * kd
    sz = 1 if tm is None else CHUNK
    if tm is None:
        tm = eye.astype(F32)
    while sz < CHUNK:
        off = jnp.where(((i // (2 * sz)) == (j // (2 * sz))) & ((i // sz) != (j // sz)), a, 0.0)
        tmb = _bf(tm)
        tm = tm - _bnn(_bf(_bnn(tmb, _bf(off))), tmb)
        sz *= 2
    bv = beta3 * v3
    bk = (beta3 * gam) * kn3
    sol = _bnn3(_split(tm), jnp.concatenate([bv, bk], axis=2))
    u, wk = sol[:, :, :B_DIM], sol[:, :, B_DIM:]
    qk = _bnt(_bf(qn3), knb)
    p = jnp.where(low, qk * dm, 0.0)
    kdec = kn3 * edec
    qg = gam * qn3
    return dict(beta=beta3, g=c3(g), ea=ea, sp_arg=c3(sp_arg), rq=c3(rq), rk=c3(rk), nq=c3(nq),
                qn=qn3, kn=kn3, v=v3, gc=gc, gl=gl, gam=gam, egl=egl, edec=edec, dm=dm, kd=kd, a=a,
                tm=tm, u=u, wk=wk, qk=qk, p=p, kdec=kdec, qg=qg, eye=eye, low=low, strict=strict)


def dn_prep(c, proj, al_row, dtb_row, b, s, ncb=16):
    t = b * s
    r = ncb * CHUNK
    nblk = t // r
    bd_blk = 0

    def body(cq_ref, ck_ref, cv_ref, bd_ref, al_ref, dtb_ref, u_ref, wk_ref, qg_ref, kdec_ref, p_ref, egl_ref,
             tm_ref):
        h = pl.program_id(1)
        m = _dn_chunk_math(cq_ref[...], ck_ref[...], cv_ref[...], bd_ref[...].astype(F32), al_ref[...], dtb_ref[...], h, ncb)
        tm_ref[...] = m["tm"].reshape(r, CHUNK)
        u_ref[...] = m["u"].reshape(r, B_DIM)
        wk_ref[...] = _bf(m["wk"].reshape(r, B_DIM))
        qg_ref[...] = _bf(m["qg"].reshape(r, B_DIM))
        kdec_ref[...] = _bf(m["kdec"].reshape(r, B_DIM))
        p_ref[...] = m["p"].reshape(r, CHUNK)
        egl_ref[...] = jnp.broadcast_to(m["egl"], (ncb, 8, 128)).reshape(ncb * 8, 128)

    col = lambda k: pl.BlockSpec((r, 128), lambda i, h: (i, k * B_HEADS + h))
    out_col = pl.BlockSpec((r, 128), lambda i, h: (i, h))
    small = pl.BlockSpec((1, 128), lambda i, h: (0, 0))
    return pl.pallas_call(
        body, name="dn_prep", grid=(nblk, B_HEADS),
        in_specs=[col(0), col(1), col(2), pl.BlockSpec((r, 128), lambda i, h: (i, bd_blk)), small, small],
        out_specs=[out_col, out_col, out_col, out_col,
                   pl.BlockSpec((None, r, CHUNK), lambda i, h: (h, i, 0)),
                   pl.BlockSpec((None, ncb * 8, 128), lambda i, h: (h, i, 0)),
                   pl.BlockSpec((None, r, CHUNK), lambda i, h: (h, i, 0))],
        out_shape=[jax.ShapeDtypeStruct((t, B_WIDTH), F32)] + [jax.ShapeDtypeStruct((t, B_WIDTH), BF16)] * 3
        + [jax.ShapeDtypeStruct((B_HEADS, t, CHUNK), F32),
           jax.ShapeDtypeStruct((B_HEADS, t // 8, 128), F32),
           jax.ShapeDtypeStruct((B_HEADS, t, CHUNK), F32)],
        compiler_params=_cp(("parallel", "parallel"), VMEM_LIMIT),
    )(c, c, c, proj, al_row, dtb_row)


def dn_scan_fwd(u, wk, qg, kdec, p, egl, b, s):
    t = b * s
    nc = s // CHUNK

    def body(u_ref, wk_ref, qg_ref, kdec_ref, p_ref, egl_ref, o_ref, ss_ref, st):
        @pl.when(pl.program_id(0) == 0)
        def _():
            st[...] = jnp.zeros_like(st)

        chains = [(bb, h) for bb in range(b) for h in range(B_HEADS)]
        states = [st[bb * B_HEADS + h] for bb, h in chains]
        sls = [slice(h * B_DIM, (h + 1) * B_DIM) for _, h in chains]
        sbs = [_bf(sh) for sh in states]
        ws = [u_ref[bb, :, sl] - _nt(_bf(wk_ref[bb, :, sl]), sb) for (bb, _), sl, sb in zip(chains, sls, sbs)]
        qs = [_nt(_bf(qg_ref[bb, :, sl]), sb) for (bb, _), sl, sb in zip(chains, sls, sbs)]
        wbs = [_bf(w) for w in ws]
        outs = [q + _nn(_bf(p_ref[h, bb]), wb) for (bb, h), q, wb in zip(chains, qs, wbs)]
        new_states = [egl_ref[h, bb][0:1, :] * sh + _tn(wb, _bf(kdec_ref[bb, :, sl]))
                      for (bb, h), sl, sh, wb in zip(chains, sls, states, wbs)]
        for (bb, h), sh, o, ns in zip(chains, states, outs, new_states):
            ss_ref[bb, h] = sh
            o_ref[bb, :, h * B_DIM:(h + 1) * B_DIM] = o
            st[bb * B_HEADS + h] = ns

    r3 = lambda a: a.reshape(b, s, B_WIDTH)
    act = pl.BlockSpec((b, CHUNK, B_WIDTH), lambda n: (0, n, 0))
    o, states = pl.pallas_call(
        body, name="dn_scan_fwd", grid=(nc,),
        in_specs=[act, act, act, act,
                  pl.BlockSpec((B_HEADS, b, CHUNK, CHUNK), lambda n: (0, 0, n, 0)),
                  pl.BlockSpec((B_HEADS, b, 8, 128), lambda n: (0, 0, n, 0))],
        out_specs=[act, pl.BlockSpec((b, None, B_HEADS, B_DIM, B_DIM), lambda n: (0, n, 0, 0, 0))],
        out_shape=[jax.ShapeDtypeStruct((b, s, B_WIDTH), F32),
                   jax.ShapeDtypeStruct((b, nc, B_HEADS, B_DIM, B_DIM), F32)],
        scratch_shapes=[pltpu.VMEM((b * B_HEADS, B_DIM, B_DIM), F32)],
        compiler_params=_cp(("arbitrary",), VMEM_LIMIT),
    )(r3(u), r3(wk), r3(qg), r3(kdec), p.reshape(B_HEADS, b, s, CHUNK), egl.reshape(B_HEADS, b, s // 8, 128))
    return o.reshape(t, B_WIDTH), states


def dn_scan_bwd(u, wk, qg, kdec, p, egl, states, do, b, s):
    t = b * s
    nc = s // CHUNK

    def body(u_ref, wk_ref, qg_ref, kdec_ref, p_ref, egl_ref, ss_ref, do_ref,
             dw_ref, dwk_ref, dqg_ref, dkdec_ref, dp_ref, degl_ref, dst):
        @pl.when(pl.program_id(0) == 0)
        def _():
            dst[...] = jnp.zeros_like(dst)

        chains = [(bb, h) for bb in range(b) for h in range(B_HEADS)]
        dstates = [dst[bb * B_HEADS + h] for bb, h in chains]
        n8 = range(len(chains))
        sls = [slice(h * B_DIM, (h + 1) * B_DIM) for _, h in chains]
        shs = [ss_ref[bb, h] for bb, h in chains]
        sbs = [_bf(sh) for sh in shs]
        dsbs = [_bf(dsp) for dsp in dstates]
        wkbs = [_bf(wk_ref[bb, :, sl]) for (bb, _), sl in zip(chains, sls)]
        dobs = [_bf(do_ref[bb, :, sl]) for (bb, _), sl in zip(chains, sls)]
        t1 = [_nt(wkbs[i], sbs[i]) for i in n8]
        dwa = [_tn(_bf(p_ref[h, bb]), dobs[i]) for i, (bb, h) in enumerate(chains)]
        dwb_ = [_nt(_bf(kdec_ref[bb, :, sls[i]]), dsbs[i]) for i, (bb, _) in enumerate(chains)]
        dqgs = [_nn(dobs[i], sbs[i]) for i in n8]
        dsq = [_tn(dobs[i], _bf(qg_ref[bb, :, sls[i]])) for i, (bb, _) in enumerate(chains)]
        wbs = [_bf(u_ref[bb, :, sls[i]] - t1[i]) for i, (bb, _) in enumerate(chains)]
        dws = [dwa[i] + dwb_[i] for i in n8]
        dwbs = [_bf(dw) for dw in dws]
        dwks = [-_nn(dwbs[i], sbs[i]) for i in n8]
        dkdecs = [_nn(wbs[i], dsbs[i]) for i in n8]
        dpms = [_nt(dobs[i], wbs[i]) for i in n8]
        dsw = [_tn(dwbs[i], wkbs[i]) for i in n8]
        tots = [jnp.sum(jnp.sum(shs[i] * dstates[i], axis=1, keepdims=True), axis=0, keepdims=True) for i in n8]
        new_dss = [egl_ref[h, bb][0:1, :] * dstates[i] + dsq[i] - dsw[i] for i, (bb, h) in enumerate(chains)]
        results = [(dws[i], dqgs[i], dwks[i], dkdecs[i], dpms[i], tots[i], new_dss[i]) for i in n8]
        for (bb, h), (dw, dqg, dwk, dkdec, dpm, tot, new_ds) in zip(chains, results):
            sl = slice(h * B_DIM, (h + 1) * B_DIM)
            dw_ref[bb, :, sl] = dw
            dqg_ref[bb, :, sl] = dqg
            dwk_ref[bb, :, sl] = dwk
            dkdec_ref[bb, :, sl] = dkdec
            dp_ref[h, bb] = dpm
            degl_ref[h, bb] = jnp.broadcast_to(tot, (8, 128))
            dst[bb * B_HEADS + h] = new_ds

    r3 = lambda a: a.reshape(b, s, B_WIDTH)
    act = pl.BlockSpec((b, CHUNK, B_WIDTH), lambda n: (0, nc - 1 - n, 0))
    pspec = pl.BlockSpec((B_HEADS, b, CHUNK, CHUNK), lambda n: (0, 0, nc - 1 - n, 0))
    espec = pl.BlockSpec((B_HEADS, b, 8, 128), lambda n: (0, 0, nc - 1 - n, 0))
    outs = pl.pallas_call(
        body, name="dn_scan_bwd", grid=(nc,),
        in_specs=[act, act, act, act, pspec, espec,
                  pl.BlockSpec((b, None, B_HEADS, B_DIM, B_DIM), lambda n: (0, nc - 1 - n, 0, 0, 0)),
                  act],
        out_specs=[act, act, act, act, pspec, espec],
        out_shape=[jax.ShapeDtypeStruct((b, s, B_WIDTH), F32)] * 4
        + [jax.ShapeDtypeStruct((B_HEADS, b, s, CHUNK), F32),
           jax.ShapeDtypeStruct((B_HEADS, b, s // 8, 128), F32)],
        scratch_shapes=[pltpu.VMEM((b * B_HEADS, B_DIM, B_DIM), F32)],
        compiler_params=_cp(("arbitrary",), VMEM_LIMIT),
    )(r3(u), r3(wk), r3(qg), r3(kdec), p.reshape(B_HEADS, b, s, CHUNK), egl.reshape(B_HEADS, b, s // 8, 128),
      states, r3(do))
    return (*[a.reshape(t, B_WIDTH) for a in outs[:4]], outs[4].reshape(B_HEADS, t, CHUNK),
            outs[5].reshape(B_HEADS, t // 8, 128))


def dn_post_bwd(c, proj, al_row, dtb_row, tmat, dw, dwk, dqg, dkdec, dp, degl, b, s, ncb=16):
    t = b * s
    r = ncb * CHUNK
    nblk = t // r
    bd_blk = 0

    def body(cq_ref, ck_ref, cv_ref, bd_ref, al_ref, dtb_ref, tm_ref, dw_ref, dwk_ref, dqg_ref, dkdec_ref, dp_ref,
             degl_ref, dc_ref, dbd_ref, dal_ref, ddtb_ref):
        h = pl.program_id(1)

        @pl.when((pl.program_id(0) == 0) & (h == 0))
        def _():
            dal_ref[...] = jnp.zeros_like(dal_ref)
            ddtb_ref[...] = jnp.zeros_like(ddtb_ref)

        m = _dn_chunk_math(cq_ref[...], ck_ref[...], cv_ref[...], bd_ref[...].astype(F32), al_ref[...], dtb_ref[...], h, ncb,
                           tm=tm_ref[...].reshape(ncb, CHUNK, CHUNK))
        eye, low, strict = m["eye"], m["low"], m["strict"]
        eyef = eye.astype(F32)

        def c3(a):
            return a.reshape(ncb, CHUNK, a.shape[-1])

        du, dwkv, dqg, dkdec = c3(dw_ref[...]), c3(dwk_ref[...]), c3(dqg_ref[...]), c3(dkdec_ref[...])
        dpm = jnp.where(low, c3(dp_ref[...]), 0.0)
        degl = degl_ref[...].reshape(ncb, 8, 128)[:, 0:1, 0:1]
        beta, gam, kn, qn, v = m["beta"], m["gam"], m["kn"], m["qn"], m["v"]
        dm, kd, a, p = m["dm"], m["kd"], m["a"], m["p"]
        knb, qnb = _bf(kn), _bf(qn)

        eyeb = _bf(eyef)
        th, tl = _split(m["tm"])
        tts = (_bf(_bnt(eyeb, th)), _bf(_bnt(eyeb, tl)))
        xy = _bnn3(tts, jnp.concatenate([du, dwkv], axis=2))
        x, y = xy[:, :, :B_DIM], xy[:, :, B_DIM:]
        da = -jnp.where(strict, _bnt(_bf(x), _bf(m["u"])) + _bnt(_bf(y), _bf(m["wk"])), 0.0)
        dv = beta * x
        sy = jnp.sum(y * kn, axis=2, keepdims=True)
        dbeta = jnp.sum(x * v, axis=2, keepdims=True) + gam * sy + jnp.sum(da * kd, axis=2, keepdims=True)
        dgam = beta * sy + jnp.sum(dqg * qn, axis=2, keepdims=True)
        dkk = da * beta * dm
        dqk = dpm * dm
        dkkb, dqkb = _bf(dkk), _bf(dqk)
        dkn = ((beta * gam) * y + _bnn(dkkb, knb) + _bnn(_bf(_bnt(eyeb, dkkb)), knb)
               + _bnn(_bf(_bnt(eyeb, dqkb)), qnb) + dkdec * m["edec"])
        dqn = gam * dqg + _bnn(dqkb, knb)
        mm = da * a + dpm * p
        ek = jnp.sum(dkdec * m["kdec"], axis=2, keepdims=True)
        dgc = (jnp.sum(mm, axis=2, keepdims=True) - _col_of_row(jnp.sum(mm, axis=1, keepdims=True), eye)
               + dgam * gam - ek)
        dgl = jnp.sum(ek, axis=1, keepdims=True) + degl * m["egl"]
        i, _ = _chunk_masks(ncb)
        dgc = dgc + jnp.where(i[:, :, 0:1] == CHUNK - 1, dgl, 0.0)
        upper = (i <= _chunk_masks(ncb)[1]).astype(BF16)
        dg = _bnn_exact(upper, jnp.broadcast_to(dgc, (ncb, CHUNK, CHUNK)))[:, :, 0:1]

        nq = m["nq"]
        dnq = dqn * (B_DIM ** -0.5)
        dcq = m["rq"] * (dnq - nq * jnp.sum(nq * dnq, axis=2, keepdims=True))
        dck = m["rk"] * (dkn - kn * jnp.sum(kn * dkn, axis=2, keepdims=True))
        dc_ref[0] = dcq.reshape(r, B_DIM)
        dc_ref[1] = dck.reshape(r, B_DIM)
        dc_ref[2] = dv.reshape(r, B_DIM)

        dbraw = (dbeta * beta * (1.0 - beta)).reshape(r, 1)
        sgm = _sigmoid(m["sp_arg"])
        ddraw3 = dg * (-m["ea"]) * sgm
        ddraw = ddraw3.reshape(r, 1)
        lane = lax.broadcasted_iota(jnp.int32, (r, 128), 1)
        contrib = jnp.where(lane == h, dbraw, 0.0) + jnp.where(lane == B_HEADS + h, ddraw, 0.0)

        @pl.when(h == 0)
        def _():
            dbd_ref[...] = contrib

        @pl.when(h != 0)
        def _():
            dbd_ref[...] += contrib

        lane8 = lax.broadcasted_iota(jnp.int32, (8, 128), 1)
        tot_al = jnp.sum(jnp.sum(dg * m["g"], axis=1, keepdims=True), axis=0, keepdims=True).reshape(1, 1)
        tot_dtb = jnp.sum(jnp.sum(ddraw3, axis=1, keepdims=True), axis=0, keepdims=True).reshape(1, 1)
        dal_ref[...] += jnp.where(lane8 == h, tot_al, 0.0)
        ddtb_ref[...] += jnp.where(lane8 == h, tot_dtb, 0.0)

    col = lambda k: pl.BlockSpec((r, 128), lambda i, h: (i, k * B_HEADS + h))
    hcol = pl.BlockSpec((r, 128), lambda i, h: (i, h))
    small = pl.BlockSpec((1, 128), lambda i, h: (0, 0))
    acc = pl.BlockSpec((8, 128), lambda i, h: (0, 0))
    return pl.pallas_call(
        body, name="dn_post_bwd", grid=(nblk, B_HEADS),
        in_specs=[col(0), col(1), col(2), pl.BlockSpec((r, 128), lambda i, h: (i, bd_blk)), small, small,
                  pl.BlockSpec((None, r, CHUNK), lambda i, h: (h, i, 0)),
                  hcol, hcol, hcol, hcol,
                  pl.BlockSpec((None, r, CHUNK), lambda i, h: (h, i, 0)),
                  pl.BlockSpec((None, ncb * 8, 128), lambda i, h: (h, i, 0))],
        out_specs=[pl.BlockSpec((3, r, 128), lambda i, h: (0, i, h)),
                   pl.BlockSpec((r, 128), lambda i, h: (i, 0)), acc, acc],
        out_shape=[jax.ShapeDtypeStruct((3, t, B_WIDTH), F32), jax.ShapeDtypeStruct((t, 128), F32),
                   jax.ShapeDtypeStruct((8, 128), F32), jax.ShapeDtypeStruct((8, 128), F32)],
        compiler_params=_cp(("arbitrary", "arbitrary"), VMEM_LIMIT),
    )(c, c, c, proj, al_row, dtb_row, tmat, dw, dwk, dqg, dkdec, dp, degl)


def make_bias_band(rel_bias):
    tail = bias_tail(jnp.pad(rel_bias, ((0, 0), (0, 384 - N_REL))))
    far = jnp.broadcast_to(rel_bias[:, 2 * REL_CLIP][:, None, None], (A_HEADS, CHUNK, BAND - TAIL))
    band = jnp.concatenate([far, jnp.transpose(tail, (1, 0, 2))], axis=2)
    off = jnp.full((A_HEADS, CHUNK, CHUNK), -1e30, F32)
    both = jnp.stack([jnp.concatenate([band, off], axis=2), jnp.concatenate([off, band], axis=2)], axis=1)
    return both.reshape(4, 4 * CHUNK, WIN)


def bias_band_grad(dbt, dbf):
    t5 = dbt.reshape(A_HEADS, 2, CHUNK, 256)
    tail = t5[:, 0, :, :TAIL] + t5[:, 1, :, CHUNK:]
    far = dbf.reshape(A_HEADS, 2, CHUNK, 128).sum(axis=1) + jnp.pad(t5[:, 1, :, :CHUNK], ((0, 0), (0, 0), (0, CHUNK)))
    return bias_grad(jnp.transpose(tail, (1, 0, 2)), far)[:, :N_REL]


def _rms(x):
    r = lax.rsqrt(jnp.mean(x * x, axis=-1, keepdims=True) + EPS)
    return r, x * r


def _rms_bwd(dh, g, r, n):
    dn = dh * g
    return r * (dn - n * jnp.mean(dn * n, axis=-1, keepdims=True)), dh * n


def _gated_onorm(o, z, w_on):
    parts = []
    for h in range(B_HEADS):
        sl = slice(h * B_DIM, (h + 1) * B_DIM)
        r, n = _rms(o[:, sl])
        parts.append((r, n))
    r4 = [p[0] for p in parts]
    n4 = jnp.concatenate([p[1] for p in parts], axis=1)
    w4 = jnp.concatenate([w_on] * B_HEADS, axis=1)
    sz = _sigmoid(z)
    silu = z * sz
    return n4 * w4 * silu, r4, n4, w4, sz, silu


def mid_fwd(x, y_a, o_b, proj, w_on, wa, wb, w_out, tm=256):
    t = x.shape[0]
    tm = min(tm, t)

    def body(x_ref, ya_ref, ob_ref, z_ref, ga_ref, gb_ref, won_ref, wa_ref, wb_ref, wo_ref, x1_ref, mg_ref):
        yb = _gated_onorm(ob_ref[...], z_ref[...].astype(F32), won_ref[...])[0]
        ua = _nn(_bf(ya_ref[...]), wa_ref[...])
        ub = _nn(_bf(yb), wb_ref[...])
        merged = _sigmoid(ga_ref[...].astype(F32)) * ua + _sigmoid(gb_ref[...].astype(F32)) * ub
        mb = _bf(merged)
        mg_ref[...] = mb
        x1_ref[...] = x_ref[...] + _nn(mb, wo_ref[...])

    rowd = pl.BlockSpec((tm, D_MODEL), lambda i: (i, 0))
    row5 = pl.BlockSpec((tm, 512), lambda i: (i, 0))
    full = lambda a: pl.BlockSpec(a.shape, lambda i: (0,) * a.ndim)
    return pl.pallas_call(
        body, name="mid_fwd", grid=(t // tm,),
        in_specs=[rowd, row5, row5,
                  pl.BlockSpec((tm, 512), lambda i: (i, P_Z // 512)),
                  pl.BlockSpec((tm, D_MODEL), lambda i: (i, 0)),
                  pl.BlockSpec((tm, D_MODEL), lambda i: (i, 1)),
                  full(w_on), full(wa), full(wb), full(w_out)],
        out_specs=[rowd, rowd],
        out_shape=[jax.ShapeDtypeStruct((t, D_MODEL), F32), jax.ShapeDtypeStruct((t, D_MODEL), BF16)],
        compiler_params=_cp(("parallel",), VMEM_LIMIT),
    )(x, y_a, o_b, proj, proj, proj, w_on, wa, wb, w_out)


def mid_bwd(dx1, merged, y_a, o_b, proj, w_on, wa, wb, w_out, tm=256):
    t = dx1.shape[0]
    tm = min(tm, t)

    def body(dx1_ref, mg_ref, ya_ref, ob_ref, z_ref, ga_ref, gb_ref, won_ref, wa_ref, wb_ref, wo_ref,
             dya_ref, dob_ref, dz_ref, dg_ref, dwo_ref, dwa_ref, dwb_ref, dwon_ref):
        @pl.when(pl.program_id(0) == 0)
        def _():
            dwo_ref[...] = jnp.zeros_like(dwo_ref)
            dwa_ref[...] = jnp.zeros_like(dwa_ref)
            dwb_ref[...] = jnp.zeros_like(dwb_ref)
            dwon_ref[...] = jnp.zeros_like(dwon_ref)

        dx1b = _bf(dx1_ref[...])
        dmerged = _nt(dx1b, wo_ref[...])
        dwo_ref[...] += _tn(mg_ref[...], dx1b)
        o = ob_ref[...]
        z = z_ref[...].astype(F32)
        yb, r4, n4, w4, sz, silu = _gated_onorm(o, z, won_ref[...])
        yab, ybb = _bf(ya_ref[...]), _bf(yb)
        ua = _nn(yab, wa_ref[...])
        ub = _nn(ybb, wb_ref[...])
        sa, sb = _sigmoid(ga_ref[...].astype(F32)), _sigmoid(gb_ref[...].astype(F32))
        dua, dub = _bf(dmerged * sa), _bf(dmerged * sb)
        dg_ref[:, 0:D_MODEL] = _bf(dmerged * ua * sa * (1.0 - sa))
        dg_ref[:, D_MODEL:2 * D_MODEL] = _bf(dmerged * ub * sb * (1.0 - sb))
        dwa_ref[...] += _tn(yab, dua)
        dwb_ref[...] += _tn(ybb, dub)
        dya_ref[...] = _nt(dua, wa_ref[...])
        dyb = _nt(dub, wb_ref[...])
        dz_ref[...] = _bf(dyb * (n4 * w4) * (sz * (1.0 + z * (1.0 - sz))))
        dnw = dyb * silu
        dwon = jnp.zeros((1, B_DIM), F32)
        for h in range(B_HEADS):
            sl = slice(h * B_DIM, (h + 1) * B_DIM)
            dxh, dgh = _rms_bwd(dnw[:, sl], won_ref[...], r4[h], n4[:, sl])
            dob_ref[:, sl] = dxh
            dwon = dwon + jnp.sum(dgh, axis=0, keepdims=True)
        dwon_ref[...] += jnp.broadcast_to(dwon, (8, B_DIM))

    rowd = pl.BlockSpec((tm, D_MODEL), lambda i: (i, 0))
    row5 = pl.BlockSpec((tm, 512), lambda i: (i, 0))
    full = lambda a: pl.BlockSpec(a.shape, lambda i: (0,) * a.ndim)
    fixed = lambda shp: pl.BlockSpec(shp, lambda i: (0,) * len(shp))
    return pl.pallas_call(
        body, name="mid_bwd", grid=(t // tm,),
        in_specs=[rowd, rowd, row5, row5,
                  pl.BlockSpec((tm, 512), lambda i: (i, P_Z // 512)),
                  pl.BlockSpec((tm, D_MODEL), lambda i: (i, 0)),
                  pl.BlockSpec((tm, D_MODEL), lambda i: (i, 1)),
                  full(w_on), full(wa), full(wb), full(w_out)],
        out_specs=[row5, row5, row5, pl.BlockSpec((tm, 2 * D_MODEL), lambda i: (i, 0)),
                   fixed((D_MODEL, D_MODEL)), fixed((A_WIDTH, D_MODEL)), fixed((B_WIDTH, D_MODEL)),
                   fixed((8, B_DIM))],
        out_shape=[jax.ShapeDtypeStruct((t, 512), F32), jax.ShapeDtypeStruct((t, 512), F32),
                   jax.ShapeDtypeStruct((t, 512), BF16), jax.ShapeDtypeStruct((t, 2 * D_MODEL), BF16),
           jax.ShapeDtypeStruct((D_MODEL, D_MODEL), F32), jax.ShapeDtypeStruct((A_WIDTH, D_MODEL), F32),
           jax.ShapeDtypeStruct((B_WIDTH, D_MODEL), F32), jax.ShapeDtypeStruct((8, B_DIM), F32)],
        compiler_params=_cp(("arbitrary",), VMEM_LIMIT),
    )(dx1, merged, y_a, o_b, proj, proj, proj, w_on, wa, wb, w_out)


FFN_TF = 1408


def ffn_up(x1, g, w_gu, tm=512, tf=FFN_TF):
    t = x1.shape[0]
    tm = min(tm, t)
    nf = D_FF // tf

    def body(x_ref, g_ref, wg_ref, wu_ref, gate_ref, up_ref, act_ref, h_ref):
        @pl.when(pl.program_id(1) == 0)
        def _():
            r, n = _rms(x_ref[...])
            h_ref[...] = _bf(n * g_ref[...])

        hb = h_ref[...]
        gate = _nn(hb, wg_ref[...])
        up = _nn(hb, wu_ref[...])
        gate_ref[...] = _bf(gate)
        up_ref[...] = _bf(up)
        act_ref[...] = _bf(gate * _sigmoid(gate) * up)

    ff = pl.BlockSpec((tm, tf), lambda i, j: (i, j))
    return pl.pallas_call(
        body, name="ffn_up", grid=(t // tm, nf),
        in_specs=[pl.BlockSpec((tm, D_MODEL), lambda i, j: (i, 0)),
                  pl.BlockSpec((1, D_MODEL), lambda i, j: (0, 0)),
                  pl.BlockSpec((D_MODEL, tf), lambda i, j: (0, j)),
                  pl.BlockSpec((D_MODEL, tf), lambda i, j: (0, nf + j))],
        out_specs=[ff, ff, ff, pl.BlockSpec((tm, D_MODEL), lambda i, j: (i, 0))],
        out_shape=[jax.ShapeDtypeStruct((t, D_FF), BF16)] * 3 + [jax.ShapeDtypeStruct((t, D_MODEL), BF16)],
        compiler_params=_cp(("parallel", "arbitrary"), VMEM_LIMIT),
    )(x1, g, w_gu, w_gu)


def matmul_residual(a, w, res, name, tm=512, tk=FFN_TF):
    t, k = a.shape
    n = w.shape[1]
    tm = min(tm, t)

    def body(a_ref, w_ref, r_ref, o_ref):
        @pl.when(pl.program_id(1) == 0)
        def _():
            o_ref[...] = r_ref[...]

        o_ref[...] += _nn(a_ref[...], w_ref[...])

    return pl.pallas_call(
        body, name=name, grid=(t // tm, k // tk),
        in_specs=[pl.BlockSpec((tm, tk), lambda i, j: (i, j)),
                  pl.BlockSpec((tk, n), lambda i, j: (j, 0)),
                  pl.BlockSpec((tm, n), lambda i, j: (i, 0))],
        out_specs=pl.BlockSpec((tm, n), lambda i, j: (i, 0)),
        out_shape=jax.ShapeDtypeStruct((t, n), F32),
        compiler_params=_cp(("parallel", "arbitrary"), VMEM_LIMIT),
    )(a, w, res)


def ffn_act_bwd(dx2, gate, up, w_down, tm=512, tf=FFN_TF):
    t = dx2.shape[0]
    tm = min(tm, t)

    def body(dx2_ref, gate_ref, up_ref, wd_ref, dgate_ref, dup_ref, dx2b_ref):
        @pl.when(pl.program_id(1) == 0)
        def _():
            dx2b_ref[...] = _bf(dx2_ref[...])

        dact = _nt(dx2b_ref[...], wd_ref[...])
        gt, upv = gate_ref[...].astype(F32), up_ref[...].astype(F32)
        sg = _sigmoid(gt)
        t = dact * sg
        dgate_ref[...] = _bf(t * upv * (1.0 + gt * (1.0 - sg)))
        dup_ref[...] = _bf(t * gt)

    ff = pl.BlockSpec((tm, tf), lambda i, j: (i, j))
    return pl.pallas_call(
        body, name="ffn_act_bwd", grid=(t // tm, D_FF // tf),
        in_specs=[pl.BlockSpec((tm, D_MODEL), lambda i, j: (i, 0)), ff, ff,
                  pl.BlockSpec((tf, D_MODEL), lambda i, j: (j, 0))],
        out_specs=[ff, ff],
        out_shape=[jax.ShapeDtypeStruct((t, D_FF), BF16)] * 2,
        scratch_shapes=[pltpu.VMEM((tm, D_MODEL), BF16)],
        compiler_params=_cp(("parallel", "arbitrary"), VMEM_LIMIT),
    )(dx2, gate, up, w_down)


def tail_fwd_bwd(x2, p, target, g_ple, g_final, w_pg, w_pp, tm=256):
    t = x2.shape[0]
    tm = min(tm, t)

    def body(x_ref, p_ref, t_ref, gp_ref, gf_ref, wpg_ref, wpp_ref,
             dx_ref, dwpg_ref, dwpp_ref, dgp_ref, dgf_ref, loss_ref):
        @pl.when(pl.program_id(0) == 0)
        def _():
            dwpg_ref[...] = jnp.zeros_like(dwpg_ref)
            dwpp_ref[...] = jnp.zeros_like(dwpp_ref)
            dgp_ref[...] = jnp.zeros_like(dgp_ref)
            dgf_ref[...] = jnp.zeros_like(dgf_ref)
            loss_ref[...] = jnp.zeros_like(loss_ref)

        x2v = x_ref[...]
        gp, gf = gp_ref[...], gf_ref[...]
        r3, n3 = _rms(x2v)
        h3b = _bf(n3 * gp)
        pb = _bf(p_ref[...])
        pg = _sigmoid(_nn(h3b, wpg_ref[...]))
        pp = _nn(pb, wpp_ref[...])
        x3 = x2v + pg * pp
        r4, n4 = _rms(x3)
        err = n4 * gf - t_ref[...]
        part = 0.5 * jnp.sum(jnp.sum(err * err, axis=1, keepdims=True), axis=0, keepdims=True) / D_MODEL
        loss_ref[...] += jnp.broadcast_to(part, (8, 128))
        dy = err * (1.0 / D_MODEL)
        dx3, dgf = _rms_bwd(dy, gf, r4, n4)
        dgf_ref[...] += jnp.broadcast_to(jnp.sum(dgf, axis=0, keepdims=True), (8, D_MODEL))
        dzp = _bf(dx3 * pp * pg * (1.0 - pg))
        dpp = _bf(dx3 * pg)
        dwpg_ref[...] += _tn(h3b, dzp)
        dwpp_ref[...] += _tn(pb, dpp)
        dh3 = _nt(dzp, wpg_ref[...])
        dx, dgp = _rms_bwd(dh3, gp, r3, n3)
        dgp_ref[...] += jnp.broadcast_to(jnp.sum(dgp, axis=0, keepdims=True), (8, D_MODEL))
        dx_ref[...] = dx3 + dx

    rowd = pl.BlockSpec((tm, D_MODEL), lambda i: (i, 0))
    fixed = lambda shp: pl.BlockSpec(shp, lambda i: (0,) * len(shp))
    return pl.pallas_call(
        body, name="tail_fwd_bwd", grid=(t // tm,),
        in_specs=[rowd, pl.BlockSpec((tm, PLE_DIM), lambda i: (i, 0)), rowd,
                  fixed((1, D_MODEL)), fixed((1, D_MODEL)), fixed((D_MODEL, D_MODEL)), fixed((PLE_DIM, D_MODEL))],
        out_specs=[rowd, fixed((D_MODEL, D_MODEL)), fixed((PLE_DIM, D_MODEL)),
                   fixed((8, D_MODEL)), fixed((8, D_MODEL)), fixed((8, 128))],
        out_shape=[jax.ShapeDtypeStruct((t, D_MODEL), F32), jax.ShapeDtypeStruct((D_MODEL, D_MODEL), F32),
                   jax.ShapeDtypeStruct((PLE_DIM, D_MODEL), F32), jax.ShapeDtypeStruct((8, D_MODEL), F32),
                   jax.ShapeDtypeStruct((8, D_MODEL), F32), jax.ShapeDtypeStruct((8, 128), F32)],
        compiler_params=_cp(("arbitrary",), VMEM_LIMIT),
    )(x2, p, target, g_ple, g_final, w_pg, w_pp)


def in_proj_bwd(pieces, weights, x, dx1, g, name="in_proj_bwd", tm=256):
    t = x.shape[0]
    tm = min(tm, t)
    k = len(pieces)
    assert all(c0 % wd == 0 and w0 % wd == 0 for (_, c0, wd), (_, w0) in zip(pieces, weights))

    def body(*refs):
        p_refs, w_refs = refs[:k], refs[k:2 * k]
        x_ref, dx1_ref, g_ref, dx_ref, dg_ref = refs[2 * k:]

        @pl.when(pl.program_id(0) == 0)
        def _():
            dg_ref[...] = jnp.zeros_like(dg_ref)

        dh = _nt(_bf(p_refs[0][...]), w_refs[0][...])
        for pr, wr in zip(p_refs[1:], w_refs[1:]):
            dh = dh + _nt(_bf(pr[...]), wr[...])
        r, n = _rms(x_ref[...])
        dx, dgc = _rms_bwd(dh, g_ref[...], r, n)
        dx_ref[...] = dx1_ref[...] + dx
        dg_ref[...] += jnp.broadcast_to(jnp.sum(dgc, axis=0, keepdims=True), (8, D_MODEL))

    rowd = pl.BlockSpec((tm, D_MODEL), lambda i: (i, 0))
    return pl.pallas_call(
        body, name=name, grid=(t // tm,),
        in_specs=[pl.BlockSpec((tm, wd), functools.partial(lambda i, cb: (i, cb), cb=c0 // wd))
                  for _, c0, wd in pieces]
        + [pl.BlockSpec((w.shape[0], wd), functools.partial(lambda i, cb: (0, cb), cb=w0 // wd))
           for (w, w0), (_, _, wd) in zip(weights, pieces)]
        + [rowd, rowd, pl.BlockSpec((1, D_MODEL), lambda i: (0, 0))],
        out_specs=[rowd, pl.BlockSpec((8, D_MODEL), lambda i: (0, 0))],
        out_shape=[jax.ShapeDtypeStruct((t, D_MODEL), F32), jax.ShapeDtypeStruct((8, D_MODEL), F32)],
        compiler_params=_cp(("arbitrary",), VMEM_LIMIT),
    )(*[a for a, _, _ in pieces], *[w for w, _ in weights], x, dx1, g)


def adamw(w, g, m, v, name, rows_cap=256, dep=None):
    lead = w.shape[:-2]
    r, c = w.shape[-2:]
    tr = r
    for cand in range(8, min(r, rows_cap) + 1, 8):
        if r % cand == 0:
            tr = cand

    def body(w_ref, g_ref, m_ref, v_ref, *rest):
        d_ref, mo_ref, vo_ref = rest[-3:]
        gv = g_ref[...]
        mn = ADAM_B1 * m_ref[...] + (1.0 - ADAM_B1) * gv
        vn = ADAM_B2 * v_ref[...] + (1.0 - ADAM_B2) * (gv * gv)
        m_hat = mn / (1.0 - ADAM_B1 ** ADAM_STEP)
        v_hat = vn / (1.0 - ADAM_B2 ** ADAM_STEP)
        d_ref[...] = -ADAM_LR * (m_hat / (jnp.sqrt(v_hat) + ADAM_EPS) + ADAM_WD * w_ref[...])
        mo_ref[...] = mn
        vo_ref[...] = vn

    spec = pl.BlockSpec((None,) * len(lead) + (tr, c), lambda i: (0,) * len(lead) + (i, 0))
    extra = [] if dep is None else [dep]
    return pl.pallas_call(
        body, name=name, grid=(r // tr,),
        in_specs=[spec] * 4 + [pl.BlockSpec((8, 128), lambda i: (0, 0))] * len(extra), out_specs=[spec] * 3,
        out_shape=[jax.ShapeDtypeStruct(w.shape, F32)] * 3,
        compiler_params=_cp(("parallel",), VMEM_LIMIT),
    )(w, g.reshape(w.shape), m, v, *extra)


def _w_in_shards(dwp):
    cs = D_IN // N_CHIPS
    regions = ((0, SPLIT_Z, P_QA), (SPLIT_Z, SPLIT_Z + 8, P_BD - SPLIT_Z), (SPLIT_Z + 8, D_IN, -(SPLIT_Z + 8)))

    def original(lo, hi):
        parts = [dwp[:, max(lo, a) + off:min(hi, e) + off] for a, e, off in regions if max(lo, a) < min(hi, e)]
        return parts[0] if len(parts) == 1 else jnp.concatenate(parts, axis=1)

    return jnp.stack([original(s * cs, (s + 1) * cs) for s in range(N_CHIPS)])


class Standalone:
    def __init__(self, later_weights):
        self.later_weights = later_weights

    def begin(self, *a):
        return 0.0

    forward = exchange = join = begin

    def finish(self, after):
        return self.later_weights


def local_step(x3d, p3d, target3d, g4, small, later, early):
    b, s, _ = x3d.shape
    t = b * s
    x = x3d.reshape(t, D_MODEL)
    p = p3d.reshape(t, PLE_DIM)
    target = target3d.reshape(t, D_MODEL)
    cut = SPLIT_Z - 2 * (D_IN // N_CHIPS)
    w_inp = jnp.concatenate([g4[2][:, cut + 8:], g4[3], g4[0], g4[1], g4[2][:, :cut], g4[2][:, cut:cut + 8],
                             jnp.zeros((D_MODEL, 120), BF16)], axis=1)
    al_row = jnp.pad(small["a_log"].reshape(1, B_HEADS), ((0, 0), (0, 128 - B_HEADS)))
    dtb_row = jnp.pad(small["dt_bias"].reshape(1, B_HEADS), ((0, 0), (0, 128 - B_HEADS)))
    conv_w8 = jnp.pad(small["conv_w"].reshape(CONV_K, CONV_CH), ((0, 8 - CONV_K), (0, 0)))
    w_on = small["w_onorm"].reshape(1, B_DIM)
    g_mix, g_ffn = small["g_mix"].reshape(1, D_MODEL), small["g_ffn"].reshape(1, D_MODEL)
    g_ple, g_final = small["g_ple"].reshape(1, D_MODEL), small["g_final"].reshape(1, D_MODEL)
    bias_band = make_bias_band(small["rel_bias"].reshape(A_HEADS, N_REL))

    tok = later.begin()
    proj, h1, bd32 = rms_matmul(x, g_mix + tok, w_inp, "in_proj", tm=1024, tn_cap=1152)
    y_a, lse = attn_fwd(proj, bias_band, b, s)
    tok = later.forward(lse)
    c = conv_fwd(proj, conv_w8 + tok, b, s)
    u, wk, qg, kdec, pm, egl, tmat = dn_prep(c, bd32, al_row, dtb_row, b, s)
    o_b, states = dn_scan_fwd(u, wk, qg, kdec, pm, egl, b, s)
    wts = later.finish(o_b)
    x1, merged = mid_fwd(x, y_a, o_b, proj, w_on, wts["w_branch_a"], wts["w_branch_b"], wts["w_out"])
    gate, up, act, h2 = ffn_up(x1, g_ffn, wts["w_gate_up"])
    x2 = matmul_residual(act, wts["w_down"], x1, "ffn_down")

    dx2, dw_pg, dw_pp, dg_ple, dg_final, loss = tail_fwd_bwd(
        x2, p, target, g_ple, g_final, wts["w_ple_gate"], wts["w_ple_proj"])
    dgate, dup = ffn_act_bwd(dx2, gate, up, wts["w_down"])
    w_gu = wts["w_gate_up"]
    dx1, dg_ffn = in_proj_bwd([(dgate, 0, D_FF), (dup, 0, D_FF)], [(w_gu, 0), (w_gu, D_FF)], x1, dx2, g_ffn,
                              name="ffn_in_bwd")
    dw_down = matmul_tn(act, dx2, "dw_down")
    dw_gu = matmul_tn(h2, dgate, "dw_gate", width=2 * D_FF, tiles_major=True)
    dw_gu = matmul_tn(h2, dup, "dw_up", into=dw_gu, col0=D_FF, width=2 * D_FF, tiles_major=True)
    dy_a, do_b, dz, dgates, dw_out, dwa, dwb, dw_on = mid_bwd(
        dx1, merged, y_a, o_b, proj, w_on, wts["w_branch_a"], wts["w_branch_b"], wts["w_out"])
    tok = early.begin(dict(w_branch_a=dwa, w_branch_b=dwb, w_out=dw_out, w_gate_up=dw_gu, w_down=dw_down,
                           w_ple_gate=dw_pg, w_ple_proj=dw_pp))
    ddw, ddwk, ddqg, ddkdec, ddp, ddegl = dn_scan_bwd(u, wk, qg, kdec, pm, egl + tok, states, do_b, b, s)
    tok = early.exchange(ddegl)
    dc3, dbd, dal, ddtb = dn_post_bwd(c, bd32, al_row + tok, dtb_row, tmat, ddw, ddwk, ddqg, ddkdec, ddp, ddegl, b, s)
    dconv, dconv_w = conv_bwd(proj, conv_w8, dc3, b, s)
    dqa, dka, dva, dbt, dbf = attn_bwd(proj, bias_band, y_a, lse, dy_a, b, s)
    tok = early.join(dqa)
    d_rel = bias_band_grad(dbt, dbf)

    pieces = [dgates, dqa, dka, dva, dconv, dz, dbd]
    bounds = [0, 2048, 2560, 3072, 3584, 5120, 5632, 5760]
    windows = [(dgates, 0, 2048), (dqa, 0, 512), (dka, 0, 512), (dva, 0, 512), (dconv, 0, 512), (dconv, 512, 512),
               (dconv, 1024, 512), (dz, 0, 512), (dbd, 0, 128)]
    w_cols = [0, P_QA, P_KA, P_VA, P_CONV, P_CONV + 512, P_CONV + 1024, P_Z, P_BD]
    dx, dg_mix = in_proj_bwd(windows, [(w_inp, c0) for c0 in w_cols], x, dx1, g_mix + tok)
    dwp = None
    for k, pc in enumerate(pieces):
        dwp = matmul_tn(h1, pc, "dw_in_%d" % k, into=dwp, col0=bounds[k], width=P_WIDTH)
    reduced_early = early.finish(dwp)
    dw_in = _w_in_shards(dwp)

    grads = dict(w_in=dw_in, w_branch_a=dwa, w_branch_b=dwb, w_out=dw_out, w_gate_up=dw_gu, w_down=dw_down,
                 w_ple_gate=dw_pg, w_ple_proj=dw_pp)
    small_grads = dict(g_mix=dg_mix[0], g_ffn=dg_ffn[0], g_ple=dg_ple[0], g_final=dg_final[0],
                       conv_w=dconv_w[:CONV_K].reshape(-1), rel_bias=d_rel.reshape(-1), w_onorm=dw_on[0],
                       a_log=dal[0, :B_HEADS], dt_bias=ddtb[0, :B_HEADS], loss=loss[0, :1])
    return dx.reshape(b, s, D_MODEL), grads, small_grads, reduced_early


BIG = (("w_in", (D_MODEL, D_IN), 1), ("w_branch_a", (A_WIDTH, D_MODEL), 1), ("w_branch_b", (B_WIDTH, D_MODEL), 1),
       ("w_out", (D_MODEL, D_MODEL), 0), ("w_gate_up", (D_MODEL, 2 * D_FF), 1), ("w_down", (D_FF, D_MODEL), 0),
       ("w_ple_gate", (D_MODEL, D_MODEL), 0), ("w_ple_proj", (PLE_DIM, D_MODEL), 1))
N_CHIPS = 4
FIRST_WEIGHTS = ("w_in",)
LATER_WEIGHTS = ("w_branch_a", "w_branch_b", "w_out", "w_gate_up", "w_down", "w_ple_gate", "w_ple_proj")
LATE_GRADS = ("w_in",)
EARLY_GRADS = ("w_branch_a", "w_branch_b", "w_out", "w_gate_up", "w_down", "w_ple_gate", "w_ple_proj")


def _items(names):
    return [it for it in BIG if it[0] in names]


def _shard_shape(shape, axis):
    return (shape[0] // N_CHIPS, shape[1]) if axis == 0 else (shape[0], shape[1] // N_CHIPS)


def _width_groups(names):
    groups = {}
    for n, shape, axis in _items(names):
        rs, cs = _shard_shape(shape, axis)
        groups.setdefault(cs, []).append((n, rs))
    return sorted(groups.items())


def grad_buffers(grads, names):
    info = {n: (shape, axis) for n, shape, axis in _items(names)}
    bufs = []
    for cs, members in _width_groups(names):
        segs = []
        for n, rs in members:
            g = grads[n].astype(BF16)
            if g.ndim == 2:
                g = (g.reshape(N_CHIPS, rs, cs) if info[n][1] == 0
                     else jnp.transpose(g.reshape(rs, N_CHIPS, cs), (1, 0, 2)))
            segs.append(g)
        bufs.append(segs[0] if len(segs) == 1 else jnp.concatenate(segs, axis=1))
    return bufs


def split_buffers(reduced, names):
    out = {}
    for (cs, members), buf in zip(_width_groups(names), reduced):
        r0 = 0
        for n, rs in members:
            out[n] = buf[r0:r0 + rs]
            r0 += rs
    return out


def _place():
    return lax.axis_index("x"), lax.axis_index("y"), lax.axis_index("c")


ANY = pl.BlockSpec(memory_space=pl.ANY)


def _gathered_shape(item):
    n, shape, _ = item
    return (N_CHIPS,) + _shard_shape(shape, 1) if n == "w_in" else shape


def _gather_block(o_ref, item, cx, cy, hf):
    n, shape, axis = item
    rs, cs = _shard_shape(shape, axis)
    hr = rs // 2
    ci = 2 * cx + cy
    if n == "w_in":
        return o_ref.at[ci, pl.ds(pl.multiple_of(hf * hr, 16), hr), :]
    if axis == 0:
        return o_ref.at[pl.ds(pl.multiple_of(ci * rs + hf * hr, 16), hr), :]
    return o_ref.at[pl.ds(pl.multiple_of(hf * hr, 16), hr), pl.ds(pl.multiple_of(ci * cs, 128), cs)]


def _own_half(w_ref, item, c):
    hr = _shard_shape(item[1], item[2])[0] // 2
    return w_ref.at[pl.ds(pl.multiple_of(c * hr, 16), hr), :]


def _gather_slot(o_ref, item, cx, cy):
    n, shape, axis = item
    rs, cs = _shard_shape(shape, axis)
    ci = 2 * cx + cy
    if n == "w_in":
        return o_ref.at[ci]
    if axis == 0:
        return o_ref.at[pl.ds(pl.multiple_of(ci * rs, 16), rs), :]
    return o_ref.at[:, pl.ds(pl.multiple_of(ci * cs, 128), cs)]


def _other_chips(x, y):
    return [(1 - x, y), (x, 1 - y), (1 - x, 1 - y)]


def allgather_weights(shards, names):
    items = _items(names)
    nw = len(items)

    def body(*refs):
        w_refs, o_refs = refs[:nw], refs[nw:2 * nw]
        send_sems, recv_sems = refs[2 * nw:]
        x, y, c = _place()
        sibling = (x, y, 1 - c)
        chips = _other_chips(x, y)

        def copy(k, src, dst, to):
            return pltpu.make_async_remote_copy(src_ref=src, dst_ref=dst, send_sem=send_sems.at[k],
                                                recv_sem=recv_sems.at[k], device_id=to, device_id_type=MESH)

        def blk(i, cx, cy, hf):
            return _gather_block(o_refs[i], items[i], cx, cy, hf)

        def my_half(i):
            return _own_half(w_refs[i], items[i], c)

        def own(i):
            return _gather_slot(o_refs[i], items[i], x, y)

        first = [copy(7 * i + j, my_half(i), blk(i, x, y, c), (*chip_, c))
                 for i in range(nw) for j, chip_ in enumerate(chips)]
        first += [copy(7 * i + 6, w_refs[i], own(i), sibling) for i in range(nw)]
        for cp in first:
            cp.start()
        passed = []
        for i in range(nw):
            for j, chip_ in enumerate(chips):
                copy(7 * i + j, my_half(i), blk(i, *chip_, c), (*chip_, c)).wait_recv()
                fwd = copy(7 * i + 3 + j, blk(i, *chip_, c), blk(i, *chip_, c), sibling)
                fwd.start()
                passed.append(fwd)
        for i in range(nw):
            for j, chip_ in enumerate(chips):
                copy(7 * i + 3 + j, my_half(i), blk(i, *chip_, 1 - c), sibling).wait_recv()
            copy(7 * i + 6, w_refs[i], own(i), sibling).wait_recv()
        for cp in first + passed:
            cp.wait_send()

    outs = pl.pallas_call(
        body, name="allgather_weights",
        in_specs=[ANY] * nw, out_specs=[ANY] * nw,
        out_shape=[jax.ShapeDtypeStruct(_gathered_shape(it), BF16) for it in items],
        scratch_shapes=[pltpu.SemaphoreType.DMA((7 * nw,)), pltpu.SemaphoreType.DMA((7 * nw,))],
    )(*[shards[it[0]] for it in items])
    return {it[0]: o for it, o in zip(items, outs)}


HBM_SPEC = pl.BlockSpec(memory_space=pltpu.HBM)
SEM_SPEC = pl.BlockSpec(memory_space=pltpu.SEMAPHORE)
EFFECT = pltpu.SideEffectType.DATAFLOW_SIDE_EFFECTING


def _in_hbm(a):
    return pltpu.with_memory_space_constraint(a, pltpu.HBM)


def copies_start(name, bufs, ncopies, plan):
    nb = len(bufs)

    def body(*refs):
        in_refs, send_sems, recv_sems, token = refs[:nb], refs[nb], refs[nb + 1], refs[-1]
        for k, (src, dst, to) in enumerate(plan(in_refs)):
            pltpu.make_async_remote_copy(src_ref=src, dst_ref=dst, send_sem=send_sems.at[k],
                                         recv_sem=recv_sems.at[k], device_id=to, device_id_type=MESH).start()
        token[...] = jnp.zeros_like(token)

    outs = pl.pallas_call(
        body, name=name,
        in_specs=[HBM_SPEC] * nb,
        out_specs=(SEM_SPEC, SEM_SPEC, *[HBM_SPEC] * nb, pl.BlockSpec(memory_space=pltpu.VMEM)),
        out_shape=(pltpu.SemaphoreType.DMA((ncopies,)), pltpu.SemaphoreType.DMA((ncopies,)),
                   *[pltpu.HBM(b.shape, b.dtype) for b in bufs], jax.ShapeDtypeStruct((8, 128), F32)),
        input_output_aliases={i: 2 + i for i in range(nb)},
        compiler_params=pltpu.CompilerParams(has_side_effects=EFFECT),
    )(*[_in_hbm(b) for b in bufs])
    return outs[0], outs[1], list(outs[2:2 + nb]), outs[-1][0, 0]


def copies_wait(name, send_sems, recv_sems, bufs, after, plan):
    nb = len(bufs)

    def body(*refs):
        in_refs, s_sems, r_sems = refs[:nb], refs[nb], refs[nb + 1]
        for k, (src, dst, to) in enumerate(plan(in_refs)):
            cp = pltpu.make_async_remote_copy(src_ref=src, dst_ref=dst, send_sem=s_sems.at[k],
                                              recv_sem=r_sems.at[k], device_id=to, device_id_type=MESH)
            cp.wait_send()
            cp.wait_recv()

    return list(pl.pallas_call(
        body, name=name,
        in_specs=[HBM_SPEC] * nb + [SEM_SPEC, SEM_SPEC, ANY],
        out_specs=tuple([HBM_SPEC] * nb),
        out_shape=tuple(pltpu.HBM(b.shape, b.dtype) for b in bufs),
        input_output_aliases={i: i for i in range(nb)},
        compiler_params=pltpu.CompilerParams(has_side_effects=EFFECT),
    )(*bufs, send_sems, recv_sems, after))


def _landing(shape, dtype):
    return _in_hbm(lax.empty(shape, dtype))


class LaterWeights:
    def __init__(self, shards):
        self.items = _items(LATER_WEIGHTS)
        self.shards = shards
        self.nw = len(self.items)

    def _ici_plan(self, refs):
        x, y, c = _place()
        w_refs, o_refs = refs[:self.nw], refs[self.nw:]
        plan = [(_own_half(w_refs[i], it, c), _gather_block(o_refs[i], it, x, y, c), (*chip_, c))
                for i, it in enumerate(self.items) for chip_ in _other_chips(x, y)]
        return plan + [(w_refs[i], _gather_slot(o_refs[i], it, x, y), (x, y, 1 - c))
                       for i, it in enumerate(self.items)]

    def _d2d_plan(self, refs):
        x, y, c = _place()
        return [(_gather_block(refs[i], it, *chip_, c), _gather_block(refs[i], it, *chip_, c), (x, y, 1 - c))
                for i, it in enumerate(self.items) for chip_ in _other_chips(x, y)]

    def _d2d_wait_plan(self, refs):
        x, y, c = _place()
        return [(_gather_block(refs[i], it, *chip_, c), _gather_block(refs[i], it, *chip_, 1 - c), (x, y, 1 - c))
                for i, it in enumerate(self.items) for chip_ in _other_chips(x, y)]

    def _ici_wait_plan(self, refs):
        x, y, c = _place()
        w_refs, o_refs = refs[:self.nw], refs[self.nw:]
        plan = [(_own_half(w_refs[i], it, c), _gather_block(o_refs[i], it, *chip_, c), (*chip_, c))
                for i, it in enumerate(self.items) for chip_ in _other_chips(x, y)]
        return plan + [(w_refs[i], _gather_slot(o_refs[i], it, x, y), (x, y, 1 - c))
                       for i, it in enumerate(self.items)]

    def begin(self):
        srcs = [self.shards[it[0]] for it in self.items]
        lands = [_landing(_gathered_shape(it), BF16) for it in self.items]
        self.s1, self.r1, self.b1, tok = copies_start("gather_ici_start", srcs + lands, 4 * self.nw, self._ici_plan)
        return tok

    def forward(self, after):
        b1 = copies_wait("gather_ici_wait", self.s1, self.r1, self.b1, after, self._ici_wait_plan)
        self.s2, self.r2, self.b2, tok = copies_start("gather_d2d_start", b1[self.nw:], 3 * self.nw, self._d2d_plan)
        return tok

    def finish(self, after):
        outs = copies_wait("gather_d2d_wait", self.s2, self.r2, self.b2, after, self._d2d_wait_plan)
        return {it[0]: o for it, o in zip(self.items, outs)}


def small_allreduce(v, name):
    r = v.shape[0]

    def body(v_ref, o_ref, buf, send_sems, recv_sems):
        x, y, c = _place()
        me = 4 * x + 2 * y + c
        buf[me] = v_ref[...]
        flips = [(fx, fy, fc) for fx in (0, 1) for fy in (0, 1) for fc in (0, 1)][1:]
        peers = [((1 - x) if fx else x, (1 - y) if fy else y, (1 - c) if fc else c) for fx, fy, fc in flips]

        def copy(k, slot, to):
            return pltpu.make_async_remote_copy(src_ref=v_ref, dst_ref=buf.at[slot], send_sem=send_sems.at[k],
                                                recv_sem=recv_sems.at[k], device_id=to, device_id_type=MESH)

        sends = [copy(k, me, peer) for k, peer in enumerate(peers)]
        for cp in sends:
            cp.start()
        for k, (px, py, pc) in enumerate(peers):
            copy(k, 4 * px + 2 * py + pc, (px, py, pc)).wait_recv()
        for cp in sends:
            cp.wait_send()
        acc = buf[0]
        for d in range(1, 8):
            acc = acc + buf[d]
        o_ref[...] = acc

    return pl.pallas_call(
        body, name=name,
        in_specs=[pl.BlockSpec(memory_space=pltpu.VMEM)], out_specs=pl.BlockSpec(memory_space=pltpu.VMEM),
        out_shape=jax.ShapeDtypeStruct((r, 128), F32),
        scratch_shapes=[pltpu.VMEM((8, r, 128), F32), pltpu.SemaphoreType.DMA((7,)), pltpu.SemaphoreType.DMA((7,))],
    )(v)


def add_halves(g, other, place):
    half, wd = other.shape[1:]
    tr = _tile_rows(half, wd)
    nblk = half // tr

    def body(pref, g0, g1, g2, g3, o0, o1, o2, o3, pf_ref, pb_ref):
        f = lambda r: r[...].astype(F32)
        pf_ref[...] = f(g0) + f(o0)
        pb_ref[0] = _bf(f(g1) + f(o1))
        pb_ref[1] = _bf(f(g2) + f(o2))
        pb_ref[2] = _bf(f(g3) + f(o3))

    gspec = lambda k: pl.BlockSpec((None, tr, wd), lambda i, pr: ((pr[0] + k) % N_CHIPS, pr[1] * nblk + i, 0))
    ospec = lambda k: pl.BlockSpec((None, tr, wd), lambda i, pr: ((pr[0] + k) % N_CHIPS, i, 0))
    return pl.pallas_call(
        body, name="add_halves",
        grid_spec=pltpu.PrefetchScalarGridSpec(
            num_scalar_prefetch=1, grid=(nblk,),
            in_specs=[gspec(0), gspec(1), gspec(2), gspec(3), ospec(0), ospec(1), ospec(2), ospec(3)],
            out_specs=[pl.BlockSpec((tr, wd), lambda i, pr: (i, 0)),
                       pl.BlockSpec((3, tr, wd), lambda i, pr: (0, i, 0))]),
        out_shape=[jax.ShapeDtypeStruct((half, wd), F32), jax.ShapeDtypeStruct((3, half, wd), BF16)],
        compiler_params=_cp(("parallel",), VMEM_LIMIT),
    )(place, g, g, g, g, other, other, other, other)


def _tile_rows(n, width):
    best = 16
    for t in range(16, max(16, (384 * 1024) // width) + 1, 16):
        if n % t == 0:
            best = t
    assert n % best == 0
    return best


def add_partials(pf, got, place):
    half, wd = pf.shape
    tr = _tile_rows(half, wd)

    def body(pref, pf_ref, got_ref, o_ref):
        o_ref[...] = ((pf_ref[...] + got_ref[0].astype(F32)) + got_ref[1].astype(F32)) + got_ref[2].astype(F32)

    return pl.pallas_call(
        body, name="add_partials",
        grid_spec=pltpu.PrefetchScalarGridSpec(
            num_scalar_prefetch=1, grid=(half // tr,),
            in_specs=[pl.BlockSpec((tr, wd), lambda i, pr: (i, 0)),
                      pl.BlockSpec((3, tr, wd), lambda i, pr: (0, i, 0))],
            out_specs=pl.BlockSpec((None, tr, wd), lambda i, pr: (pr[1], i, 0))),
        out_shape=jax.ShapeDtypeStruct((2, half, wd), F32),
        compiler_params=_cp(("parallel",), VMEM_LIMIT),
    )(place, pf, got)


class GradReduce:
    def __init__(self, place, names, tag):
        self.place, self.names, self.tag = place, names, tag
        self.nb = len(_width_groups(names))

    def _swap_plan(self, refs):
        x, y, c = _place()
        plan = []
        for g_ref, o_ref in zip(refs[:self.nb], refs[self.nb:]):
            half = o_ref.shape[1]
            plan.append((g_ref.at[:, pl.ds(pl.multiple_of((1 - c) * half, 16), half), :], o_ref, (x, y, 1 - c)))
        return plan

    def _exchange_plan(self, refs):
        x, y, c = _place()
        me = 2 * x + y
        return [(p_ref.at[k - 1], o_ref.at[k - 1], (((me + k) % N_CHIPS) // 2, ((me + k) % N_CHIPS) % 2, c))
                for p_ref, o_ref in zip(refs[:self.nb], refs[self.nb:]) for k in range(1, N_CHIPS)]

    def _join_plan(self, refs):
        x, y, c = _place()
        return [(r.at[c], r.at[c], (x, y, 1 - c)) for r in refs]

    def _join_wait_plan(self, refs):
        x, y, c = _place()
        return [(r.at[c], r.at[1 - c], (x, y, 1 - c)) for r in refs]

    def begin(self, grads):
        gs = grad_buffers(grads, self.names)
        lands = [_landing((N_CHIPS, g.shape[1] // 2, g.shape[2]), BF16) for g in gs]
        self.s1, self.r1, self.b1, tok = copies_start(self.tag + "_swap_start", gs + lands, self.nb, self._swap_plan)
        return tok

    def exchange(self, after):
        b1 = copies_wait(self.tag + "_swap_wait", self.s1, self.r1, self.b1, after, self._swap_plan)
        sums = [add_halves(g, other, self.place) for g, other in zip(b1[:self.nb], b1[self.nb:])]
        self.pfs = [pf for pf, _ in sums]
        pbs = [pb for _, pb in sums]
        lands = [_landing(pb.shape, BF16) for pb in pbs]
        self.s2, self.r2, self.b2, tok = copies_start(self.tag + "_exchange_start", pbs + lands, 3 * self.nb,
                                                      self._exchange_plan)
        return tok

    def join(self, after):
        b2 = copies_wait(self.tag + "_exchange_wait", self.s2, self.r2, self.b2, after, self._exchange_plan)
        boths = [add_partials(pf, got, self.place) for pf, got in zip(self.pfs, b2[self.nb:])]
        self.s3, self.r3, self.b3, tok = copies_start(self.tag + "_join_start", boths, self.nb, self._join_plan)
        return tok

    def finish(self, after):
        boths = copies_wait(self.tag + "_join_wait", self.s3, self.r3, self.b3, after, self._join_wait_plan)
        return split_buffers([b.reshape(-1, b.shape[2]) for b in boths], self.names)


SMALL = (("g_mix", D_MODEL), ("g_ffn", D_MODEL), ("g_ple", D_MODEL), ("g_final", D_MODEL),
         ("conv_w", CONV_K * CONV_CH), ("rel_bias", A_HEADS * N_REL), ("w_onorm", B_DIM),
         ("a_log", B_HEADS), ("dt_bias", B_HEADS), ("loss", 1))


def _pad128(v):
    v = v.reshape(-1)
    return jnp.pad(v, (0, -v.shape[0] % 128))


def pack_small(d, names, rows):
    flat = jnp.concatenate([_pad128(d[n]) for n in names]).reshape(-1, 128)
    return jnp.pad(flat, ((0, rows - flat.shape[0]), (0, 0)))


def unpack_small(flat, names_sizes):
    out, r0 = {}, 0
    v = flat.reshape(-1)
    for n, size in names_sizes:
        out[n] = v[r0:r0 + size]
        r0 += -(-size // 128) * 128
    return out


def kernel(x, p, g_mix, w_in, conv_w, a_log, dt_bias, rel_bias, w_onorm, w_branch_a, w_branch_b, w_out, g_ffn, w_gate_up, w_down, g_ple, w_ple_gate, w_ple_proj, g_final, loss_target, m_g_mix, m_w_in, m_conv_w, m_a_log, m_dt_bias, m_rel_bias, m_w_onorm, m_w_branch_a, m_w_branch_b, m_w_out, m_g_ffn, m_w_gate_up, m_w_down, m_g_ple, m_w_ple_gate, m_w_ple_proj, m_g_final, v_g_mix, v_w_in, v_conv_w, v_a_log, v_dt_bias, v_rel_bias, v_w_onorm, v_w_branch_a, v_w_branch_b, v_w_out, v_g_ffn, v_w_gate_up, v_w_down, v_g_ple, v_w_ple_gate, v_w_ple_proj, v_g_final):
    names = ["g_mix", "w_in", "conv_w", "a_log", "dt_bias", "rel_bias", "w_onorm", "w_branch_a", "w_branch_b",
             "w_out", "g_ffn", "w_gate_up", "w_down", "g_ple", "w_ple_gate", "w_ple_proj", "g_final"]
    w = dict(zip(names, [g_mix, w_in, conv_w, a_log, dt_bias, rel_bias, w_onorm, w_branch_a, w_branch_b, w_out,
                         g_ffn, w_gate_up, w_down, g_ple, w_ple_gate, w_ple_proj, g_final]))
    m = dict(zip(names, [m_g_mix, m_w_in, m_conv_w, m_a_log, m_dt_bias, m_rel_bias, m_w_onorm, m_w_branch_a,
                         m_w_branch_b, m_w_out, m_g_ffn, m_w_gate_up, m_w_down, m_g_ple, m_w_ple_gate,
                         m_w_ple_proj, m_g_final]))
    v = dict(zip(names, [v_g_mix, v_w_in, v_conv_w, v_a_log, v_dt_bias, v_rel_bias, v_w_onorm, v_w_branch_a,
                         v_w_branch_b, v_w_out, v_g_ffn, v_w_gate_up, v_w_down, v_g_ple, v_w_ple_gate,
                         v_w_ple_proj, v_g_final]))
    xi, yi, ci = _place()
    chip = 2 * xi + yi
    big_names = [n for n, _, _ in BIG]

    shards2d = {n: w[n].reshape(w[n].shape[-2:]) for n in big_names}
    shards_bf = {n: a.astype(BF16) for n, a in shards2d.items()}
    g4 = allgather_weights(shards_bf, FIRST_WEIGHTS)["w_in"]
    place = jnp.stack([chip, ci]).astype(jnp.int32)
    conv_sh = jnp.where(ci == 0, w["conv_w"].reshape(CONV_K, CONV_CH // N_CHIPS), 0.0)
    conv_slots = lax.dynamic_update_slice(jnp.zeros((N_CHIPS, CONV_K, CONV_CH // N_CHIPS), F32), conv_sh[None],
                                          (chip, 0, 0))
    conv_all = small_allreduce(conv_slots.reshape(-1, 128), "gather_conv_w")
    conv_full = jnp.transpose(conv_all.reshape(N_CHIPS, CONV_K, CONV_CH // N_CHIPS), (1, 0, 2)).reshape(CONV_K, CONV_CH)
    small = {n: w[n] for n in names if n not in big_names}
    small["conv_w"] = conv_full

    grad_x, grads, small_grads, reduced_early = local_step(
        x, p[0], loss_target, g4, small, LaterWeights(shards_bf), GradReduce(place, EARLY_GRADS, "grads"))

    late = GradReduce(place, LATE_GRADS, "late")
    tok = late.begin(grads)
    small_names = [n for n, _ in SMALL]
    small_grads["loss"] = small_grads["loss"] + tok
    red_flat = small_allreduce(pack_small(small_grads, small_names, 112), "allreduce_small")
    red = unpack_small(red_flat, SMALL)
    dep = jnp.full((8, 128), late.exchange(red_flat), F32)
    gshard = dict(reduced_early)
    loss = red["loss"][0]
    conv_g = lax.dynamic_slice(red["conv_w"].reshape(CONV_K, N_CHIPS, CONV_CH // N_CHIPS), (0, chip, 0),
                               (CONV_K, 1, CONV_CH // N_CHIPS))
    gsmall = {n: red[n].reshape(w[n].shape) for n in small_names if n not in ("loss", "conv_w")}
    gsmall["conv_w"] = conv_g.reshape(w["conv_w"].shape)

    grad, delta, new_m, new_v = {}, {}, {}, {}
    for n in list(EARLY_GRADS) + list(LATE_GRADS):
        if n in LATE_GRADS:
            late.join(v_)
            gshard.update(late.finish(v_))
        shp = w[n].shape
        d_, m_, v_ = adamw(shards2d[n], gshard[n], m[n].reshape(shp[-2:]), v[n].reshape(shp[-2:]), "adamw_" + n,
                           dep=dep if n in EARLY_GRADS else None)
        dep, v_ = lax.optimization_barrier((dep, v_))
        grad[n], delta[n], new_m[n], new_v[n] = gshard[n].reshape(shp), d_.reshape(shp), m_.reshape(shp), v_.reshape(shp)
    snames = [n for n in small_names if n != "loss"]
    ssizes = [(n, w[n].size) for n in snames]
    pk = lambda d: pack_small(d, snames, 64)
    d_, m_, v_ = adamw(pk(w), pk(gsmall), pk(m), pk(v), "adamw_small")
    ds, ms, vs = unpack_small(d_, ssizes), unpack_small(m_, ssizes), unpack_small(v_, ssizes)
    for n in snames:
        shp = w[n].shape
        grad[n], delta[n], new_m[n], new_v[n] = gsmall[n], ds[n].reshape(shp), ms[n].reshape(shp), vs[n].reshape(shp)

    return (loss, grad_x, *[grad[n] for n in names], *[delta[n] for n in names],
            *[new_m[n] for n in names], *[new_v[n] for n in names])
```

```python
import functools

import jax
import jax.numpy as jnp
from jax import lax
from jax.experimental import pallas as pl
from jax.experimental.pallas import tpu as pltpu

F32 = jnp.float32
BF16 = jnp.bfloat16
MESH = pl.DeviceIdType.MESH

D_MODEL = 1024
CHUNK = 64
PLE_DIM = 256
EPS = 1e-6
A_HEADS = 8
A_HEAD_DIM = 64
A_WIDTH = 512
A_LOOKBACK = 8
BAND = (A_LOOKBACK + 1) * CHUNK
TAIL = 3 * CHUNK
REL_CLIP = 128
N_REL = 2 * REL_CLIP + 1
B_HEADS = 4
B_DIM = 128
B_WIDTH = 512
CONV_K = 4
CONV_CH = 1536
D_FF = 2816
SPLIT_Z = 3584
D_IN = 5640
ADAM_LR, ADAM_B1, ADAM_B2, ADAM_EPS, ADAM_WD, ADAM_STEP = 0.001, 0.9, 0.999, 1e-08, 0.01, 10

P_GATES, P_QA, P_KA, P_VA, P_CONV, P_Z, P_BD, P_WIDTH = 0, 2048, 2560, 3072, 3584, 5120, 5632, 5760

VMEM_LIMIT = 56 * 1024 * 1024


def _cp(sem, vmem=None, **kw):
    return pltpu.CompilerParams(dimension_semantics=sem, vmem_limit_bytes=vmem, **kw)


def _tile(n, cap):
    best = None
    for t in range(128, cap + 1, 128):
        if n % t == 0:
            best = t
    assert best is not None, (n, cap)
    return best


def _nn(a, b, prec=None):
    return lax.dot_general(a, b, (((1,), (0,)), ((), ())), preferred_element_type=F32, precision=prec)


def _nt(a, b, prec=None):
    return lax.dot_general(a, b, (((1,), (1,)), ((), ())), preferred_element_type=F32, precision=prec)


def _tn(a, b, prec=None):
    return lax.dot_general(a, b, (((0,), (0,)), ((), ())), preferred_element_type=F32, precision=prec)


def _bnn(a, b, prec=None):
    return lax.dot_general(a, b, (((2,), (1,)), ((0,), (0,))), preferred_element_type=F32, precision=prec)


def _bnt(a, b, prec=None):
    return lax.dot_general(a, b, (((2,), (2,)), ((0,), (0,))), preferred_element_type=F32, precision=prec)


def _bf(a):
    return a.astype(BF16)


def _split(a):
    hi = a.astype(BF16)
    return hi, (a - hi.astype(F32)).astype(BF16)


def _split3(a):
    h1 = _bf(a)
    r1 = a - h1.astype(F32)
    h2 = _bf(r1)
    return h1, h2, _bf(r1 - h2.astype(F32))


def _bnn_exact(lhs_b, rhs):
    h1, h2, h3 = _split3(rhs)
    return _bnn(lhs_b, h1) + (_bnn(lhs_b, h2) + _bnn(lhs_b, h3))


def _bnn3(a, b):
    ah, al = a if isinstance(a, tuple) else _split(a)
    bh, bl = b if isinstance(b, tuple) else _split(b)
    return _bnn(ah, bh) + (_bnn(ah, bl) + _bnn(al, bh))


def _sigmoid(x):
    return 0.5 * jnp.tanh(0.5 * x) + 0.5


def _softplus(x):
    return jnp.maximum(x, 0.0) + jnp.log(1.0 + jnp.exp(-jnp.abs(x)))


def rms_matmul(x, g, w, name, tm=512, tn_cap=1024):
    t, d = x.shape
    n = w.shape[1]
    tm = min(tm, t)
    tn = _tile(n, tn_cap)

    nj = n // tn

    def body(x_ref, g_ref, w_ref, o_ref, h_ref, tail_ref):
        @pl.when(pl.program_id(1) == 0)
        def _():
            xv = x_ref[...]
            r = lax.rsqrt(jnp.mean(xv * xv, axis=-1, keepdims=True) + EPS)
            h_ref[...] = _bf(xv * r * g_ref[...])

        res = _nn(h_ref[...], w_ref[...])
        o_ref[...] = _bf(res)

        @pl.when(pl.program_id(1) == nj - 1)
        def _():
            tail_ref[...] = res[:, tn - 128:]

    return pl.pallas_call(
        body, name=name, grid=(t // tm, nj),
        in_specs=[pl.BlockSpec((tm, d), lambda i, j: (i, 0)),
                  pl.BlockSpec((1, d), lambda i, j: (0, 0)),
                  pl.BlockSpec((d, tn), lambda i, j: (0, j))],
        out_specs=[pl.BlockSpec((tm, tn), lambda i, j: (i, j)),
                   pl.BlockSpec((tm, d), lambda i, j: (i, 0)),
                   pl.BlockSpec((tm, 128), lambda i, j: (i, 0))],
        out_shape=[jax.ShapeDtypeStruct((t, n), BF16), jax.ShapeDtypeStruct((t, d), BF16),
                   jax.ShapeDtypeStruct((t, 128), F32)],
        compiler_params=_cp(("parallel", "arbitrary"), VMEM_LIMIT),
    )(x, g, w)


def matmul_tn(a, b, name, into=None, col0=0, width=None, tm=1024, tk_cap=1408, tn_cap=1408, tiles_major=False):
    m, k1 = a.shape
    n = b.shape[1]
    tm = min(tm, m)
    tk = _tile(k1, tk_cap)
    tn = _tile(n, tn_cap)
    while col0 % tn:
        tn = _tile(n, tn - 128)
    nk = m // tm
    c0 = col0 // tn

    def body(*refs):
        a_ref, b_ref, o_ref, acc = refs[0], refs[1], refs[-2], refs[-1]

        @pl.when(pl.program_id(2) == 0)
        def _():
            acc[...] = jnp.zeros_like(acc)

        acc[...] += _tn(_bf(a_ref[...]), _bf(b_ref[...]))

        @pl.when(pl.program_id(2) == nk - 1)
        def _():
            o_ref[...] = _bf(acc[...])

    in_specs = [pl.BlockSpec((tm, tk), lambda i, j, k: (k, i)),
                pl.BlockSpec((tm, tn), lambda i, j, k: (k, j))]
    args = [a, b]
    total = n if width is None else width
    aliases = {}
    if into is not None:
        in_specs.append(ANY)
        args.append(into)
        aliases = {2: 0}
    if tiles_major:
        out_spec = pl.BlockSpec((None, tk, tn), lambda i, j, k: (c0 + j, i, 0))
        out_shape = jax.ShapeDtypeStruct((total // tn, k1, tn), BF16)
    else:
        out_spec = pl.BlockSpec((tk, tn), lambda i, j, k: (i, c0 + j))
        out_shape = jax.ShapeDtypeStruct((k1, total), BF16)
    return pl.pallas_call(
        body, name=name, grid=(k1 // tk, n // tn, nk),
        in_specs=in_specs,
        out_specs=out_spec,
        out_shape=out_shape,
        scratch_shapes=[pltpu.VMEM((tk, tn), F32)],
        input_output_aliases=aliases,
        compiler_params=_cp(("parallel", "parallel", "arbitrary"), VMEM_LIMIT),
    )(*args)


def _tail_onehot(qi):
    r = lax.broadcasted_iota(jnp.int32, (384, TAIL), 0)
    kj = lax.broadcasted_iota(jnp.int32, (384, TAIL), 1)
    return (r == jnp.minimum(REL_CLIP + qi - kj, REL_CLIP) + REL_CLIP).astype(F32)


def bias_tail(rel_pad):
    def body(rb_ref, o_ref):
        parts = _split3(rb_ref[...])
        for qi in range(CHUNK):
            oh = _bf(_tail_onehot(qi))
            o_ref[qi] = _nn(parts[0], oh) + (_nn(parts[1], oh) + _nn(parts[2], oh))

    return pl.pallas_call(
        body, name="bias_tail",
        out_shape=jax.ShapeDtypeStruct((CHUNK, A_HEADS, TAIL), F32),
    )(rel_pad)


def bias_grad(db_t, db_far):
    def body(t_ref, f_ref, o_ref):
        acc = jnp.zeros((A_HEADS, 384), F32)
        for qi in range(CHUNK):
            oh = _bf(_tail_onehot(qi))
            parts = _split3(t_ref[qi])
            acc = acc + (_nt(parts[0], oh) + (_nt(parts[1], oh) + _nt(parts[2], oh)))
        far = jnp.sum(jnp.sum(f_ref[...], axis=2), axis=1, keepdims=True)
        lane = lax.broadcasted_iota(jnp.int32, (A_HEADS, 384), 1)
        o_ref[...] = acc + jnp.where(lane == 2 * REL_CLIP, far, 0.0)

    return pl.pallas_call(
        body, name="bias_grad",
        out_shape=jax.ShapeDtypeStruct((A_HEADS, 384), F32),
    )(db_t, db_far)


ATT_CB = 8


WIN = BAND + CHUNK


def _stack_heads(a, lane):
    return jnp.concatenate([jnp.where(lane < 64, a, 0.0), jnp.where(lane >= 64, a, 0.0)], axis=0)


def _fill_band_pads(k_ref, v_ref, kp, vp, s):
    z = jnp.zeros((A_LOOKBACK * CHUNK, 128), BF16)
    kp[pl.ds(0, A_LOOKBACK * CHUNK), :] = z
    vp[pl.ds(0, A_LOOKBACK * CHUNK), :] = z
    kp[pl.ds(A_LOOKBACK * CHUNK, s), :] = _bf(k_ref[...])
    vp[pl.ds(A_LOOKBACK * CHUNK, s), :] = _bf(v_ref[...])


def attn_fwd(proj, bias_band, b, s):
    t = b * s
    nc = s // CHUNK
    qb, kb_, vb_ = P_QA // 128, P_KA // 128, P_VA // 128

    nstep = nc // ATT_CB
    rows = ATT_CB * CHUNK

    def body(q_ref, k_ref, v_ref, b_ref, o_ref, lse_ref, kp, vp):
        n0 = pl.program_id(2) * ATT_CB

        @pl.when(n0 == 0)
        def _():
            _fill_band_pads(k_ref, v_ref, kp, vp, s)

        lane = lax.broadcasted_iota(jnp.int32, (2 * CHUNK, 128), 1)
        col = lax.broadcasted_iota(jnp.int32, (4 * CHUNK, WIN), 1)
        bias4 = b_ref[...]

        def two_pairs(i, carry):
            pps = (2 * i, 2 * i + 1)
            ns = [n0 + 2 * pp for pp in pps]
            r0s = [pl.multiple_of(pp * 2 * CHUNK, 2 * CHUNK) for pp in pps]
            starts = [pl.multiple_of(n * CHUNK, CHUNK) for n in ns]
            kbs = [kp[pl.ds(st_, WIN), :] for st_ in starts]
            vbs = [vp[pl.ds(st_, WIN), :] for st_ in starts]
            q4s = [_bf(_stack_heads(q_ref[pl.ds(r0, 2 * CHUNK), :] * (A_HEAD_DIM ** -0.5), lane)) for r0 in r0s]
            qks = [_nt(q4, kb) for q4, kb in zip(q4s, kbs)]
            scs = [jnp.where(col >= (A_LOOKBACK - n) * CHUNK, qk + bias4, -1e30) for n, qk in zip(ns, qks)]
            mxs = [jnp.max(sc, axis=1, keepdims=True) for sc in scs]
            ps = [jnp.exp(sc - mx) for sc, mx in zip(scs, mxs)]
            ls = [jnp.sum(p, axis=1, keepdims=True) for p in ps]
            o4s = [_nn(_bf(p), vb) / l for p, vb, l in zip(ps, vbs, ls)]
            for r0, o4, mx, l in zip(r0s, o4s, mxs, ls):
                lse4 = mx + jnp.log(l)
                o_ref[pl.ds(r0, 2 * CHUNK), :] = jnp.where(lane < 64, o4[:2 * CHUNK], o4[2 * CHUNK:])
                lse_ref[pl.ds(r0, 2 * CHUNK), :] = jnp.where(lane < 64, lse4[:2 * CHUNK], lse4[2 * CHUNK:])
            return carry

        lax.fori_loop(0, ATT_CB // 4, two_pairs, 0)

    return pl.pallas_call(
        body, name="attn_fwd", grid=(b, 4, nstep),
        in_specs=[pl.BlockSpec((rows, 128), lambda bb, m, n: (bb * nstep + n, qb + m)),
                  pl.BlockSpec((s, 128), lambda bb, m, n: (bb, kb_ + m)),
                  pl.BlockSpec((s, 128), lambda bb, m, n: (bb, vb_ + m)),
                  pl.BlockSpec((None, 4 * CHUNK, WIN), lambda bb, m, n: (m, 0, 0))],
        out_specs=[pl.BlockSpec((rows, 128), lambda bb, m, n: (bb * nstep + n, m)),
                   pl.BlockSpec((rows, 128), lambda bb, m, n: (bb * nstep + n, m))],
        out_shape=[jax.ShapeDtypeStruct((t, A_WIDTH), F32), jax.ShapeDtypeStruct((t, A_WIDTH), F32)],
        scratch_shapes=[pltpu.VMEM((s + A_LOOKBACK * CHUNK, 128), BF16),
                        pltpu.VMEM((s + A_LOOKBACK * CHUNK, 128), BF16)],
        compiler_params=_cp(("parallel", "parallel", "arbitrary"), VMEM_LIMIT),
    )(proj, proj, proj, bias_band)


def attn_bwd(proj, bias_band, y_a, lse, dy_a, b, s):
    t = b * s
    nc = s // CHUNK
    qb, kb_, vb_ = P_QA // 128, P_KA // 128, P_VA // 128
    pad = A_LOOKBACK * CHUNK
    nstep = nc // ATT_CB
    rows = ATT_CB * CHUNK

    def body(q_ref, k_ref, v_ref, b_ref, do_ref, o_ref, lse_ref,
             dq_ref, dk_ref, dv_ref, dbt_ref, dbf_ref, kp, vp, dkp, dvp):
        bb = pl.program_id(1)
        n0 = pl.program_id(2) * ATT_CB

        @pl.when(n0 == 0)
        def _():
            _fill_band_pads(k_ref, v_ref, kp, vp, s)
            dkp[...] = jnp.zeros_like(dkp)
            dvp[...] = jnp.zeros_like(dvp)

        @pl.when((n0 == 0) & (bb == 0))
        def _():
            dbt_ref[...] = jnp.zeros_like(dbt_ref)
            dbf_ref[...] = jnp.zeros_like(dbf_ref)

        lane = lax.broadcasted_iota(jnp.int32, (2 * CHUNK, 128), 1)
        col = lax.broadcasted_iota(jnp.int32, (4 * CHUNK, WIN), 1)
        bias4 = b_ref[...]

        def two_pairs(i, carry):
            pps = (2 * i, 2 * i + 1)
            two = range(2)
            ns = [n0 + 2 * pp for pp in pps]
            r0s = [pl.multiple_of(pp * 2 * CHUNK, 2 * CHUNK) for pp in pps]
            starts = [pl.multiple_of(n * CHUNK, CHUNK) for n in ns]
            kbs = [kp[pl.ds(st_, WIN), :] for st_ in starts]
            vbs = [vp[pl.ds(st_, WIN), :] for st_ in starts]
            q4bs = [_bf(_stack_heads(q_ref[pl.ds(r0, 2 * CHUNK), :] * (A_HEAD_DIM ** -0.5), lane)) for r0 in r0s]
            do4s = [_stack_heads(do_ref[pl.ds(r0, 2 * CHUNK), :], lane) for r0 in r0s]
            do4bs = [_bf(d) for d in do4s]
            os_ = [o_ref[pl.ds(r0, 2 * CHUNK), :] for r0 in r0s]
            lsevs = [lse_ref[pl.ds(r0, 2 * CHUNK), :] for r0 in r0s]
            lse4s = [jnp.concatenate([v_[:, 0:1], v_[:, 64:65]], axis=0) for v_ in lsevs]
            qks = [_nt(q4bs[j], kbs[j]) for j in two]
            dps = [_nt(do4bs[j], vbs[j]) for j in two]
            deltas = [jnp.sum(do4s[j] * jnp.concatenate([os_[j], os_[j]], axis=0), axis=1, keepdims=True) for j in two]
            ps = [jnp.exp(jnp.where(col >= (A_LOOKBACK - ns[j]) * CHUNK, qks[j] + bias4, -1e30) - lse4s[j])
                  for j in two]
            pbs = [_bf(p) for p in ps]
            dss = [ps[j] * (dps[j] - deltas[j]) for j in two]
            dsbs = [_bf(d) for d in dss]
            dv_ws = [_tn(pbs[j], do4bs[j]) for j in two]
            dq4s = [_nn(dsbs[j], kbs[j]) for j in two]
            dk_ws = [_tn(dsbs[j], q4bs[j]) for j in two]
            for j in two:
                dq_ref[pl.ds(r0s[j], 2 * CHUNK), :] = _bf(
                    jnp.where(lane < 64, dq4s[j][:2 * CHUNK], dq4s[j][2 * CHUNK:]) * (A_HEAD_DIM ** -0.5))
                dkp[pl.ds(starts[j], WIN), :] += dk_ws[j]
                dvp[pl.ds(starts[j], WIN), :] += dv_ws[j]
                dbt_ref[...] += dss[j][:, WIN - 256:]
                dbf_ref[...] += dss[j][:, 0:128] + dss[j][:, 128:256] + dss[j][:, 256:384]
            return carry

        lax.fori_loop(0, ATT_CB // 4, two_pairs, 0)

        @pl.when(n0 == nc - ATT_CB)
        def _():
            dk_ref[...] = _bf(dkp[pl.ds(pad, s), :])
            dv_ref[...] = _bf(dvp[pl.ds(pad, s), :])

    return pl.pallas_call(
        body, name="attn_bwd", grid=(4, b, nstep),
        in_specs=[pl.BlockSpec((rows, 128), lambda m, bb, n: (bb * nstep + n, qb + m)),
                  pl.BlockSpec((s, 128), lambda m, bb, n: (bb, kb_ + m)),
                  pl.BlockSpec((s, 128), lambda m, bb, n: (bb, vb_ + m)),
                  pl.BlockSpec((None, 4 * CHUNK, WIN), lambda m, bb, n: (m, 0, 0)),
                  pl.BlockSpec((rows, 128), lambda m, bb, n: (bb * nstep + n, m)),
                  pl.BlockSpec((rows, 128), lambda m, bb, n: (bb * nstep + n, m)),
                  pl.BlockSpec((rows, 128), lambda m, bb, n: (bb * nstep + n, m))],
        out_specs=[pl.BlockSpec((rows, 128), lambda m, bb, n: (bb * nstep + n, m)),
                   pl.BlockSpec((s, 128), lambda m, bb, n: (bb, m)),
                   pl.BlockSpec((s, 128), lambda m, bb, n: (bb, m)),
                   pl.BlockSpec((None, 4 * CHUNK, 256), lambda m, bb, n: (m, 0, 0)),
                   pl.BlockSpec((None, 4 * CHUNK, 128), lambda m, bb, n: (m, 0, 0))],
        out_shape=[jax.ShapeDtypeStruct((t, A_WIDTH), BF16)] * 3
        + [jax.ShapeDtypeStruct((4, 4 * CHUNK, 256), F32),
           jax.ShapeDtypeStruct((4, 4 * CHUNK, 128), F32)],
        scratch_shapes=[pltpu.VMEM((s + pad, 128), BF16), pltpu.VMEM((s + pad, 128), BF16),
                        pltpu.VMEM((s + pad, 128), F32), pltpu.VMEM((s + pad, 128), F32)],
        compiler_params=_cp(("parallel", "arbitrary", "arbitrary"), VMEM_LIMIT),
    )(proj, proj, proj, bias_band, dy_a, y_a, lse)


def _conv_taps(x, w, s):
    row = lax.broadcasted_iota(jnp.int32, x.shape, 0)
    shifted = [x] + [jnp.where(row >= i, pltpu.roll(x, i, 0), 0.0) for i in range(1, CONV_K)]
    acc = shifted[0] * w[CONV_K - 1:CONV_K, :]
    for i in range(1, CONV_K):
        acc = acc + shifted[i] * w[CONV_K - 1 - i:CONV_K - i, :]
    return acc, shifted


def conv_fwd(proj, conv_w8, b, s):
    cb = 512
    c0 = P_CONV // cb

    def body(x_ref, w_ref, o_ref):
        a, _ = _conv_taps(x_ref[...].astype(F32), w_ref[...], s)
        o_ref[...] = a * _sigmoid(a)

    return pl.pallas_call(
        body, name="conv_fwd", grid=(b, CONV_CH // cb),
        in_specs=[pl.BlockSpec((s, cb), lambda bb, j: (bb, c0 + j)),
                  pl.BlockSpec((8, cb), lambda bb, j: (0, j))],
        out_specs=pl.BlockSpec((s, cb), lambda bb, j: (bb, j)),
        out_shape=jax.ShapeDtypeStruct((b * s, CONV_CH), F32),
        compiler_params=_cp(("parallel", "parallel"), VMEM_LIMIT),
    )(proj, conv_w8)


def conv_bwd(proj, conv_w8, dc3, b, s):
    cb = 512
    c0 = P_CONV // cb

    def body(x_ref, w_ref, dc_ref, dx_ref, dw_ref):
        @pl.when(pl.program_id(1) == 0)
        def _():
            dw_ref[...] = jnp.zeros_like(dw_ref)

        w = w_ref[...]
        a, shifted = _conv_taps(x_ref[...].astype(F32), w, s)
        sg = _sigmoid(a)
        da = dc_ref[...] * (sg * (1.0 + a * (1.0 - sg)))
        row = lax.broadcasted_iota(jnp.int32, da.shape, 0)
        dx = da * w[CONV_K - 1:CONV_K, :]
        for i in range(1, CONV_K):
            dx = dx + jnp.where(row < s - i, pltpu.roll(da, s - i, 0), 0.0) * w[CONV_K - 1 - i:CONV_K - i, :]
        dx_ref[...] = _bf(dx)
        r8 =lax.broadcasted_iota(jnp.int32, (8, cb), 0)
        dw = jnp.zeros((8, cb), F32)
        for i in range(CONV_K):
            dw = dw + jnp.where(r8 == CONV_K - 1 - i, jnp.sum(da * shifted[i], axis=0, keepdims=True), 0.0)
        dw_ref[...] += dw

    return pl.pallas_call(
        body, name="conv_bwd", grid=(CONV_CH // cb, b),
        in_specs=[pl.BlockSpec((s, cb), lambda j, bb: (bb, c0 + j)),
                  pl.BlockSpec((8, cb), lambda j, bb: (0, j)),
                  pl.BlockSpec((None, s, cb), lambda j, bb: (j, bb, 0))],
        out_specs=[pl.BlockSpec((s, cb), lambda j, bb: (bb, j)),
                   pl.BlockSpec((8, cb), lambda j, bb: (0, j))],
        out_shape=[jax.ShapeDtypeStruct((b * s, CONV_CH), BF16), jax.ShapeDtypeStruct((8, CONV_CH), F32)],
        compiler_params=_cp(("parallel", "arbitrary"), VMEM_LIMIT),
    )(proj, conv_w8, dc3)


def _pick_lane(v, k):
    lane = lax.broadcasted_iota(jnp.int32, v.shape, 1)
    return jnp.sum(jnp.where(lane == k, v, 0.0), axis=1, keepdims=True)


def _chunk_masks(ncb):
    i = lax.broadcasted_iota(jnp.int32, (ncb, CHUNK, CHUNK), 1)
    j = lax.broadcasted_iota(jnp.int32, (ncb, CHUNK, CHUNK), 2)
    return i, j


def _col_of_row(rowvec, eye):
    return jnp.sum(jnp.where(eye, rowvec, 0.0), axis=2, keepdims=True)


def _dn_chunk_math(cq, ck, cv, bd, al_row, dtb_row, h, ncb, tm=None):
    r = ncb * CHUNK
    i, j = _chunk_masks(ncb)
    eye = i == j
    low = i >= j
    strict = i > j
    ones = jnp.ones((ncb, CHUNK, CHUNK), F32)

    braw = _pick_lane(bd, h)
    draw = _pick_lane(bd, B_HEADS + h)
    al = _pick_lane(al_row, h)
    dtb = _pick_lane(dtb_row, h)
    ea = jnp.exp(al)
    beta = _sigmoid(braw)
    sp_arg = draw + dtb
    g = -ea * _softplus(sp_arg)

    rq = lax.rsqrt(jnp.sum(cq * cq, axis=1, keepdims=True) + EPS)
    rk = lax.rsqrt(jnp.sum(ck * ck, axis=1, keepdims=True) + EPS)
    nq = cq * rq
    kn = ck * rk
    qn = nq * (B_DIM ** -0.5)

    def c3(a):
        return a.reshape(ncb, CHUNK, a.shape[-1])

    qn3, kn3, v3, beta3 = c3(qn), c3(kn), c3(cv), c3(beta)
    gb = jnp.broadcast_to(c3(g), (ncb, CHUNK, CHUNK))
    gc_b = _bnn_exact(low.astype(BF16), gb)
    gr_b = _bnn_exact(_bf(ones), jnp.where(eye, gc_b, 0.0))
    dm = jnp.where(low, jnp.exp(jnp.where(low, gc_b - gr_b, 0.0)), 0.0)
    gc = gc_b[:, :, 0:1]
    gl = gc_b[:, CHUNK - 1:CHUNK, 0:1]
    gam = jnp.exp(gc)
    egl = jnp.exp(gl)
    edec = jnp.exp(gl - gc)

    knb = _bf(kn3)
    kk = _bnt(knb, knb)
    kd = jnp.where(strict, kk * dm, 0.0)
    a = beta3 * kd
    sz = 1 if tm is None else CHUNK
    if tm is None:
        tm = eye.astype(F32)
    while sz < CHUNK:
        off = jnp.where(((i // (2 * sz)) == (j // (2 * sz))) & ((i // sz) != (j // sz)), a, 0.0)
        tmb = _bf(tm)
        tm = tm - _bnn(_bf(_bnn(tmb, _bf(off))), tmb)
        sz *= 2
    bv = beta3 * v3
    bk = (beta3 * gam) * kn3
    sol = _bnn3(_split(tm), jnp.concatenate([bv, bk], axis=2))
    u, wk = sol[:, :, :B_DIM], sol[:, :, B_DIM:]
    qk = _bnt(_bf(qn3), knb)
    p = jnp.where(low, qk * dm, 0.0)
    kdec = kn3 * edec
    qg = gam * qn3
    return dict(beta=beta3, g=c3(g), ea=ea, sp_arg=c3(sp_arg), rq=c3(rq), rk=c3(rk), nq=c3(nq),
                qn=qn3, kn=kn3, v=v3, gc=gc, gl=gl, gam=gam, egl=egl, edec=edec, dm=dm, kd=kd, a=a,
                tm=tm, u=u, wk=wk, qk=qk, p=p, kdec=kdec, qg=qg, eye=eye, low=low, strict=strict)


def dn_prep(c, proj, al_row, dtb_row, b, s, ncb=16):
    t = b * s
    r = ncb * CHUNK
    nblk = t // r
    bd_blk = 0

    def body(cq_ref, ck_ref, cv_ref, bd_ref, al_ref, dtb_ref, u_ref, wk_ref, qg_ref, kdec_ref, p_ref, egl_ref,
             tm_ref):
        h = pl.program_id(1)
        m = _dn_chunk_math(cq_ref[...], ck_ref[...], cv_ref[...], bd_ref[...].astype(F32), al_ref[...], dtb_ref[...], h, ncb)
        tm_ref[...] = m["tm"].reshape(r, CHUNK)
        u_ref[...] = m["u"].reshape(r, B_DIM)
        wk_ref[...] = _bf(m["wk"].reshape(r, B_DIM))
        qg_ref[...] = _bf(m["qg"].reshape(r, B_DIM))
        kdec_ref[...] = _bf(m["kdec"].reshape(r, B_DIM))
        p_ref[...] = m["p"].reshape(r, CHUNK)
        egl_ref[...] = jnp.broadcast_to(m["egl"], (ncb, 8, 128)).reshape(ncb * 8, 128)

    col = lambda k: pl.BlockSpec((r, 128), lambda i, h: (i, k * B_HEADS + h))
    out_col = pl.BlockSpec((r, 128), lambda i, h: (i, h))
    small = pl.BlockSpec((1, 128), lambda i, h: (0, 0))
    return pl.pallas_call(
        body, name="dn_prep", grid=(nblk, B_HEADS),
        in_specs=[col(0), col(1), col(2), pl.BlockSpec((r, 128), lambda i, h: (i, bd_blk)), small, small],
        out_specs=[out_col, out_col, out_col, out_col,
                   pl.BlockSpec((None, r, CHUNK), lambda i, h: (h, i, 0)),
                   pl.BlockSpec((None, ncb * 8, 128), lambda i, h: (h, i, 0)),
                   pl.BlockSpec((None, r, CHUNK), lambda i, h: (h, i, 0))],
        out_shape=[jax.ShapeDtypeStruct((t, B_WIDTH), F32)] + [jax.ShapeDtypeStruct((t, B_WIDTH), BF16)] * 3
        + [jax.ShapeDtypeStruct((B_HEADS, t, CHUNK), F32),
           jax.ShapeDtypeStruct((B_HEADS, t // 8, 128), F32),
           jax.ShapeDtypeStruct((B_HEADS, t, CHUNK), F32)],
        compiler_params=_cp(("parallel", "parallel"), VMEM_LIMIT),
    )(c, c, c, proj, al_row, dtb_row)


def dn_scan_fwd(u, wk, qg, kdec, p, egl, b, s):
    t = b * s
    nc = s // CHUNK

    def body(u_ref, wk_ref, qg_ref, kdec_ref, p_ref, egl_ref, o_ref, ss_ref, st):
        @pl.when(pl.program_id(0) == 0)
        def _():
            st[...] = jnp.zeros_like(st)

        chains = [(bb, h) for bb in range(b) for h in range(B_HEADS)]
        states = [st[bb * B_HEADS + h] for bb, h in chains]
        sls = [slice(h * B_DIM, (h + 1) * B_DIM) for _, h in chains]
        sbs = [_bf(sh) for sh in states]
        ws = [u_ref[bb, :, sl] - _nt(_bf(wk_ref[bb, :, sl]), sb) for (bb, _), sl, sb in zip(chains, sls, sbs)]
        qs = [_nt(_bf(qg_ref[bb, :, sl]), sb) for (bb, _), sl, sb in zip(chains, sls, sbs)]
        wbs = [_bf(w) for w in ws]
        outs = [q + _nn(_bf(p_ref[h, bb]), wb) for (bb, h), q, wb in zip(chains, qs, wbs)]
        new_states = [egl_ref[h, bb][0:1, :] * sh + _tn(wb, _bf(kdec_ref[bb, :, sl]))
                      for (bb, h), sl, sh, wb in zip(chains, sls, states, wbs)]
        for (bb, h), sh, o, ns in zip(chains, states, outs, new_states):
            ss_ref[bb, h] = sh
            o_ref[bb, :, h * B_DIM:(h + 1) * B_DIM] = o
            st[bb * B_HEADS + h] = ns

    r3 = lambda a: a.reshape(b, s, B_WIDTH)
    act = pl.BlockSpec((b, CHUNK, B_WIDTH), lambda n: (0, n, 0))
    o, states = pl.pallas_call(
        body, name="dn_scan_fwd", grid=(nc,),
        in_specs=[act, act, act, act,
                  pl.BlockSpec((B_HEADS, b, CHUNK, CHUNK), lambda n: (0, 0, n, 0)),
                  pl.BlockSpec((B_HEADS, b, 8, 128), lambda n: (0, 0, n, 0))],
        out_specs=[act, pl.BlockSpec((b, None, B_HEADS, B_DIM, B_DIM), lambda n: (0, n, 0, 0, 0))],
        out_shape=[jax.ShapeDtypeStruct((b, s, B_WIDTH), F32),
                   jax.ShapeDtypeStruct((b, nc, B_HEADS, B_DIM, B_DIM), F32)],
        scratch_shapes=[pltpu.VMEM((b * B_HEADS, B_DIM, B_DIM), F32)],
        compiler_params=_cp(("arbitrary",), VMEM_LIMIT),
    )(r3(u), r3(wk), r3(qg), r3(kdec), p.reshape(B_HEADS, b, s, CHUNK), egl.reshape(B_HEADS, b, s // 8, 128))
    return o.reshape(t, B_WIDTH), states


def dn_scan_bwd(u, wk, qg, kdec, p, egl, states, do, b, s):
    t = b * s
    nc = s // CHUNK

    def body(u_ref, wk_ref, qg_ref, kdec_ref, p_ref, egl_ref, ss_ref, do_ref,
             dw_ref, dwk_ref, dqg_ref, dkdec_ref, dp_ref, degl_ref, dst):
        @pl.when(pl.program_id(0) == 0)
        def _():
            dst[...] = jnp.zeros_like(dst)

        chains = [(bb, h) for bb in range(b) for h in range(B_HEADS)]
        dstates = [dst[bb * B_HEADS + h] for bb, h in chains]
        n8 = range(len(chains))
        sls = [slice(h * B_DIM, (h + 1) * B_DIM) for _, h in chains]
        shs = [ss_ref[bb, h] for bb, h in chains]
        sbs = [_bf(sh) for sh in shs]
        dsbs = [_bf(dsp) for dsp in dstates]
        wkbs = [_bf(wk_ref[bb, :, sl]) for (bb, _), sl in zip(chains, sls)]
        dobs = [_bf(do_ref[bb, :, sl]) for (bb, _), sl in zip(chains, sls)]
        t1 = [_nt(wkbs[i], sbs[i]) for i in n8]
        dwa = [_tn(_bf(p_ref[h, bb]), dobs[i]) for i, (bb, h) in enumerate(chains)]
        dwb_ = [_nt(_bf(kdec_ref[bb, :, sls[i]]), dsbs[i]) for i, (bb, _) in enumerate(chains)]
        dqgs = [_nn(dobs[i], sbs[i]) for i in n8]
        dsq = [_tn(dobs[i], _bf(qg_ref[bb, :, sls[i]])) for i, (bb, _) in enumerate(chains)]
        wbs = [_bf(u_ref[bb, :, sls[i]] - t1[i]) for i, (bb, _) in enumerate(chains)]
        dws = [dwa[i] + dwb_[i] for i in n8]
        dwbs = [_bf(dw) for dw in dws]
        dwks = [-_nn(dwbs[i], sbs[i]) for i in n8]
        dkdecs = [_nn(wbs[i], dsbs[i]) for i in n8]
        dpms = [_nt(dobs[i], wbs[i]) for i in n8]
        dsw = [_tn(dwbs[i], wkbs[i]) for i in n8]
        tots = [jnp.sum(jnp.sum(shs[i] * dstates[i], axis=1, keepdims=True), axis=0, keepdims=True) for i in n8]
        new_dss = [egl_ref[h, bb][0:1, :] * dstates[i] + dsq[i] - dsw[i] for i, (bb, h) in enumerate(chains)]
        results = [(dws[i], dqgs[i], dwks[i], dkdecs[i], dpms[i], tots[i], new_dss[i]) for i in n8]
        for (bb, h), (dw, dqg, dwk, dkdec, dpm, tot, new_ds) in zip(chains, results):
            sl = slice(h * B_DIM, (h + 1) * B_DIM)
            dw_ref[bb, :, sl] = dw
            dqg_ref[bb, :, sl] = dqg
            dwk_ref[bb, :, sl] = dwk
            dkdec_ref[bb, :, sl] = dkdec
            dp_ref[h, bb] = dpm
            degl_ref[h, bb] = jnp.broadcast_to(tot, (8, 128))
            dst[bb * B_HEADS + h] = new_ds

    r3 = lambda a: a.reshape(b, s, B_WIDTH)
    act = pl.BlockSpec((b, CHUNK, B_WIDTH), lambda n: (0, nc - 1 - n, 0))
    pspec = pl.BlockSpec((B_HEADS, b, CHUNK, CHUNK), lambda n: (0, 0, nc - 1 - n, 0))
    espec = pl.BlockSpec((B_HEADS, b, 8, 128), lambda n: (0, 0, nc - 1 - n, 0))
    outs = pl.pallas_call(
        body, name="dn_scan_bwd", grid=(nc,),
        in_specs=[act, act, act, act, pspec, espec,
                  pl.BlockSpec((b, None, B_HEADS, B_DIM, B_DIM), lambda n: (0, nc - 1 - n, 0, 0, 0)),
                  act],
        out_specs=[act, act, act, act, pspec, espec],
        out_shape=[jax.ShapeDtypeStruct((b, s, B_WIDTH), F32)] * 4
        + [jax.ShapeDtypeStruct((B_HEADS, b, s, CHUNK), F32),
           jax.ShapeDtypeStruct((B_HEADS, b, s // 8, 128), F32)],
        scratch_shapes=[pltpu.VMEM((b * B_HEADS, B_DIM, B_DIM), F32)],
        compiler_params=_cp(("arbitrary",), VMEM_LIMIT),
    )(r3(u), r3(wk), r3(qg), r3(kdec), p.reshape(B_HEADS, b, s, CHUNK), egl.reshape(B_HEADS, b, s // 8, 128),
      states, r3(do))
    return (*[a.reshape(t, B_WIDTH) for a in outs[:4]], outs[4].reshape(B_HEADS, t, CHUNK),
            outs[5].reshape(B_HEADS, t // 8, 128))


def dn_post_bwd(c, proj, al_row, dtb_row, tmat, dw, dwk, dqg, dkdec, dp, degl, b, s, ncb=16):
    t = b * s
    r = ncb * CHUNK
    nblk = t // r
    bd_blk = 0

    def body(cq_ref, ck_ref, cv_ref, bd_ref, al_ref, dtb_ref, tm_ref, dw_ref, dwk_ref, dqg_ref, dkdec_ref, dp_ref,
             degl_ref, dc_ref, dbd_ref, dal_ref, ddtb_ref):
        h = pl.program_id(1)

        @pl.when((pl.program_id(0) == 0) & (h == 0))
        def _():
            dal_ref[...] = jnp.zeros_like(dal_ref)
            ddtb_ref[...] = jnp.zeros_like(ddtb_ref)

        m = _dn_chunk_math(cq_ref[...], ck_ref[...], cv_ref[...], bd_ref[...].astype(F32), al_ref[...], dtb_ref[...], h, ncb,
                           tm=tm_ref[...].reshape(ncb, CHUNK, CHUNK))
        eye, low, strict = m["eye"], m["low"], m["strict"]
        eyef = eye.astype(F32)

        def c3(a):
            return a.reshape(ncb, CHUNK, a.shape[-1])

        du, dwkv, dqg, dkdec = c3(dw_ref[...]), c3(dwk_ref[...]), c3(dqg_ref[...]), c3(dkdec_ref[...])
        dpm = jnp.where(low, c3(dp_ref[...]), 0.0)
        degl = degl_ref[...].reshape(ncb, 8, 128)[:, 0:1, 0:1]
        beta, gam, kn, qn, v = m["beta"], m["gam"], m["kn"], m["qn"], m["v"]
        dm, kd, a, p = m["dm"], m["kd"], m["a"], m["p"]
        knb, qnb = _bf(kn), _bf(qn)

        eyeb = _bf(eyef)
        th, tl = _split(m["tm"])
        tts = (_bf(_bnt(eyeb, th)), _bf(_bnt(eyeb, tl)))
        xy = _bnn3(tts, jnp.concatenate([du, dwkv], axis=2))
        x, y = xy[:, :, :B_DIM], xy[:, :, B_DIM:]
        da = -jnp.where(strict, _bnt(_bf(x), _bf(m["u"])) + _bnt(_bf(y), _bf(m["wk"])), 0.0)
        dv = beta * x
        sy = jnp.sum(y * kn, axis=2, keepdims=True)
        dbeta = jnp.sum(x * v, axis=2, keepdims=True) + gam * sy + jnp.sum(da * kd, axis=2, keepdims=True)
        dgam = beta * sy + jnp.sum(dqg * qn, axis=2, keepdims=True)
        dkk = da * beta * dm
        dqk = dpm * dm
        dkkb, dqkb = _bf(dkk), _bf(dqk)
        dkn = ((beta * gam) * y + _bnn(dkkb, knb) + _bnn(_bf(_bnt(eyeb, dkkb)), knb)
               + _bnn(_bf(_bnt(eyeb, dqkb)), qnb) + dkdec * m["edec"])
        dqn = gam * dqg + _bnn(dqkb, knb)
        mm = da * a + dpm * p
        ek = jnp.sum(dkdec * m["kdec"], axis=2, keepdims=True)
        dgc = (jnp.sum(mm, axis=2, keepdims=True) - _col_of_row(jnp.sum(mm, axis=1, keepdims=True), eye)
               + dgam * gam - ek)
        dgl = jnp.sum(ek, axis=1, keepdims=True) + degl * m["egl"]
        i, _ = _chunk_masks(ncb)
        dgc = dgc + jnp.where(i[:, :, 0:1] == CHUNK - 1, dgl, 0.0)
        upper = (i <= _chunk_masks(ncb)[1]).astype(BF16)
        dg = _bnn_exact(upper, jnp.broadcast_to(dgc, (ncb, CHUNK, CHUNK)))[:, :, 0:1]

        nq = m["nq"]
        dnq = dqn * (B_DIM ** -0.5)
        dcq = m["rq"] * (dnq - nq * jnp.sum(nq * dnq, axis=2, keepdims=True))
        dck = m["rk"] * (dkn - kn * jnp.sum(kn * dkn, axis=2, keepdims=True))
        dc_ref[0] = dcq.reshape(r, B_DIM)
        dc_ref[1] = dck.reshape(r, B_DIM)
        dc_ref[2] = dv.reshape(r, B_DIM)

        dbraw = (dbeta * beta * (1.0 - beta)).reshape(r, 1)
        sgm = _sigmoid(m["sp_arg"])
        ddraw3 = dg * (-m["ea"]) * sgm
        ddraw = ddraw3.reshape(r, 1)
        lane = lax.broadcasted_iota(jnp.int32, (r, 128), 1)
        contrib = jnp.where(lane == h, dbraw, 0.0) + jnp.where(lane == B_HEADS + h, ddraw, 0.0)

        @pl.when(h == 0)
        def _():
            dbd_ref[...] = contrib

        @pl.when(h != 0)
        def _():
            dbd_ref[...] += contrib

        lane8 = lax.broadcasted_iota(jnp.int32, (8, 128), 1)
        tot_al = jnp.sum(jnp.sum(dg * m["g"], axis=1, keepdims=True), axis=0, keepdims=True).reshape(1, 1)
        tot_dtb = jnp.sum(jnp.sum(ddraw3, axis=1, keepdims=True), axis=0, keepdims=True).reshape(1, 1)
        dal_ref[...] += jnp.where(lane8 == h, tot_al, 0.0)
        ddtb_ref[...] += jnp.where(lane8 == h, tot_dtb, 0.0)

    col = lambda k: pl.BlockSpec((r, 128), lambda i, h: (i, k * B_HEADS + h))
    hcol = pl.BlockSpec((r, 128), lambda i, h: (i, h))
    small = pl.BlockSpec((1, 128), lambda i, h: (0, 0))
    acc = pl.BlockSpec((8, 128), lambda i, h: (0, 0))
    return pl.pallas_call(
        body, name="dn_post_bwd", grid=(nblk, B_HEADS),
        in_specs=[col(0), col(1), col(2), pl.BlockSpec((r, 128), lambda i, h: (i, bd_blk)), small, small,
                  pl.BlockSpec((None, r, CHUNK), lambda i, h: (h, i, 0)),
                  hcol, hcol, hcol, hcol,
                  pl.BlockSpec((None, r, CHUNK), lambda i, h: (h, i, 0)),
                  pl.BlockSpec((None, ncb * 8, 128), lambda i, h: (h, i, 0))],
        out_specs=[pl.BlockSpec((3, r, 128), lambda i, h: (0, i, h)),
                   pl.BlockSpec((r, 128), lambda i, h: (i, 0)), acc, acc],
        out_shape=[jax.ShapeDtypeStruct((3, t, B_WIDTH), F32), jax.ShapeDtypeStruct((t, 128), F32),
                   jax.ShapeDtypeStruct((8, 128), F32), jax.ShapeDtypeStruct((8, 128), F32)],
        compiler_params=_cp(("arbitrary", "arbitrary"), VMEM_LIMIT),
    )(c, c, c, proj, al_row, dtb_row, tmat, dw, dwk, dqg, dkdec, dp, degl)


def make_bias_band(rel_bias):
    tail = bias_tail(jnp.pad(rel_bias, ((0, 0), (0, 384 - N_REL))))
    far = jnp.broadcast_to(rel_bias[:, 2 * REL_CLIP][:, None, None], (A_HEADS, CHUNK, BAND - TAIL))
    band = jnp.concatenate([far, jnp.transpose(tail, (1, 0, 2))], axis=2)
    off = jnp.full((A_HEADS, CHUNK, CHUNK), -1e30, F32)
    both = jnp.stack([jnp.concatenate([band, off], axis=2), jnp.concatenate([off, band], axis=2)], axis=1)
    return both.reshape(4, 4 * CHUNK, WIN)


def bias_band_grad(dbt, dbf):
    t5 = dbt.reshape(A_HEADS, 2, CHUNK, 256)
    tail = t5[:, 0, :, :TAIL] + t5[:, 1, :, CHUNK:]
    far = dbf.reshape(A_HEADS, 2, CHUNK, 128).sum(axis=1) + jnp.pad(t5[:, 1, :, :CHUNK], ((0, 0), (0, 0), (0, CHUNK)))
    return bias_grad(jnp.transpose(tail, (1, 0, 2)), far)[:, :N_REL]


def _rms(x):
    r = lax.rsqrt(jnp.mean(x * x, axis=-1, keepdims=True) + EPS)
    return r, x * r


def _rms_bwd(dh, g, r, n):
    dn = dh * g
    return r * (dn - n * jnp.mean(dn * n, axis=-1, keepdims=True)), dh * n


def _gated_onorm(o, z, w_on):
    parts = []
    for h in range(B_HEADS):
        sl = slice(h * B_DIM, (h + 1) * B_DIM)
        r, n = _rms(o[:, sl])
        parts.append((r, n))
    r4 = [p[0] for p in parts]
    n4 = jnp.concatenate([p[1] for p in parts], axis=1)
    w4 = jnp.concatenate([w_on] * B_HEADS, axis=1)
    sz = _sigmoid(z)
    silu = z * sz
    return n4 * w4 * silu, r4, n4, w4, sz, silu


def mid_fwd(x, y_a, o_b, proj, w_on, wa, wb, w_out, tm=256):
    t = x.shape[0]
    tm = min(tm, t)

    def body(x_ref, ya_ref, ob_ref, z_ref, ga_ref, gb_ref, won_ref, wa_ref, wb_ref, wo_ref, x1_ref, mg_ref):
        yb = _gated_onorm(ob_ref[...], z_ref[...].astype(F32), won_ref[...])[0]
        ua = _nn(_bf(ya_ref[...]), wa_ref[...])
        ub = _nn(_bf(yb), wb_ref[...])
        merged = _sigmoid(ga_ref[...].astype(F32)) * ua + _sigmoid(gb_ref[...].astype(F32)) * ub
        mb = _bf(merged)
        mg_ref[...] = mb
        x1_ref[...] = x_ref[...] + _nn(mb, wo_ref[...])

    rowd = pl.BlockSpec((tm, D_MODEL), lambda i: (i, 0))
    row5 = pl.BlockSpec((tm, 512), lambda i: (i, 0))
    full = lambda a: pl.BlockSpec(a.shape, lambda i: (0,) * a.ndim)
    return pl.pallas_call(
        body, name="mid_fwd", grid=(t // tm,),
        in_specs=[rowd, row5, row5,
                  pl.BlockSpec((tm, 512), lambda i: (i, P_Z // 512)),
                  pl.BlockSpec((tm, D_MODEL), lambda i: (i, 0)),
                  pl.BlockSpec((tm, D_MODEL), lambda i: (i, 1)),
                  full(w_on), full(wa), full(wb), full(w_out)],
        out_specs=[rowd, rowd],
        out_shape=[jax.ShapeDtypeStruct((t, D_MODEL), F32), jax.ShapeDtypeStruct((t, D_MODEL), BF16)],
        compiler_params=_cp(("parallel",), VMEM_LIMIT),
    )(x, y_a, o_b, proj, proj, proj, w_on, wa, wb, w_out)


def mid_bwd(dx1, merged, y_a, o_b, proj, w_on, wa, wb, w_out, tm=256):
    t = dx1.shape[0]
    tm = min(tm, t)

    def body(dx1_ref, mg_ref, ya_ref, ob_ref, z_ref, ga_ref, gb_ref, won_ref, wa_ref, wb_ref, wo_ref,
             dya_ref, dob_ref, dz_ref, dg_ref, dwo_ref, dwa_ref, dwb_ref, dwon_ref):
        @pl.when(pl.program_id(0) == 0)
        def _():
            dwo_ref[...] = jnp.zeros_like(dwo_ref)
            dwa_ref[...] = jnp.zeros_like(dwa_ref)
            dwb_ref[...] = jnp.zeros_like(dwb_ref)
            dwon_ref[...] = jnp.zeros_like(dwon_ref)

        dx1b = _bf(dx1_ref[...])
        dmerged = _nt(dx1b, wo_ref[...])
        dwo_ref[...] += _tn(mg_ref[...], dx1b)
        o = ob_ref[...]
        z = z_ref[...].astype(F32)
        yb, r4, n4, w4, sz, silu = _gated_onorm(o, z, won_ref[...])
        yab, ybb = _bf(ya_ref[...]), _bf(yb)
        ua = _nn(yab, wa_ref[...])
        ub = _nn(ybb, wb_ref[...])
        sa, sb = _sigmoid(ga_ref[...].astype(F32)), _sigmoid(gb_ref[...].astype(F32))
        dua, dub = _bf(dmerged * sa), _bf(dmerged * sb)
        dg_ref[:, 0:D_MODEL] = _bf(dmerged * ua * sa * (1.0 - sa))
        dg_ref[:, D_MODEL:2 * D_MODEL] = _bf(dmerged * ub * sb * (1.0 - sb))
        dwa_ref[...] += _tn(yab, dua)
        dwb_ref[...] += _tn(ybb, dub)
        dya_ref[...] = _nt(dua, wa_ref[...])
        dyb = _nt(dub, wb_ref[...])
        dz_ref[...] = _bf(dyb * (n4 * w4) * (sz * (1.0 + z * (1.0 - sz))))
        dnw = dyb * silu
        dwon = jnp.zeros((1, B_DIM), F32)
        for h in range(B_HEADS):
            sl = slice(h * B_DIM, (h + 1) * B_DIM)
            dxh, dgh = _rms_bwd(dnw[:, sl], won_ref[...], r4[h], n4[:, sl])
            dob_ref[:, sl] = dxh
            dwon = dwon + jnp.sum(dgh, axis=0, keepdims=True)
        dwon_ref[...] += jnp.broadcast_to(dwon, (8, B_DIM))

    rowd = pl.BlockSpec((tm, D_MODEL), lambda i: (i, 0))
    row5 = pl.BlockSpec((tm, 512), lambda i: (i, 0))
    full = lambda a: pl.BlockSpec(a.shape, lambda i: (0,) * a.ndim)
    fixed = lambda shp: pl.BlockSpec(shp, lambda i: (0,) * len(shp))
    return pl.pallas_call(
        body, name="mid_bwd", grid=(t // tm,),
        in_specs=[rowd, rowd, row5, row5,
                  pl.BlockSpec((tm, 512), lambda i: (i, P_Z // 512)),
                  pl.BlockSpec((tm, D_MODEL), lambda i: (i, 0)),
                  pl.BlockSpec((tm, D_MODEL), lambda i: (i, 1)),
                  full(w_on), full(wa), full(wb), full(w_out)],
        out_specs=[row5, row5, row5, pl.BlockSpec((tm, 2 * D_MODEL), lambda i: (i, 0)),
                   fixed((D_MODEL, D_MODEL)), fixed((A_WIDTH, D_MODEL)), fixed((B_WIDTH, D_MODEL)),
                   fixed((8, B_DIM))],
        out_shape=[jax.ShapeDtypeStruct((t, 512), F32), jax.ShapeDtypeStruct((t, 512), F32),
                   jax.ShapeDtypeStruct((t, 512), BF16), jax.ShapeDtypeStruct((t, 2 * D_MODEL), BF16),
           jax.ShapeDtypeStruct((D_MODEL, D_MODEL), F32), jax.ShapeDtypeStruct((A_WIDTH, D_MODEL), F32),
           jax.ShapeDtypeStruct((B_WIDTH, D_MODEL), F32), jax.ShapeDtypeStruct((8, B_DIM), F32)],
        compiler_params=_cp(("arbitrary",), VMEM_LIMIT),
    )(dx1, merged, y_a, o_b, proj, proj, proj, w_on, wa, wb, w_out)


FFN_TF = 1408


def ffn_up(x1, g, w_gu, tm=512, tf=FFN_TF):
    t = x1.shape[0]
    tm = min(tm, t)
    nf = D_FF // tf

    def body(x_ref, g_ref, wg_ref, wu_ref, gate_ref, up_ref, act_ref, h_ref):
        @pl.when(pl.program_id(1) == 0)
        def _():
            r, n = _rms(x_ref[...])
            h_ref[...] = _bf(n * g_ref[...])

        hb = h_ref[...]
        gate = _nn(hb, wg_ref[...])
        up = _nn(hb, wu_ref[...])
        gate_ref[...] = _bf(gate)
        up_ref[...] = _bf(up)
        act_ref[...] = _bf(gate * _sigmoid(gate) * up)

    ff = pl.BlockSpec((tm, tf), lambda i, j: (i, j))
    return pl.pallas_call(
        body, name="ffn_up", grid=(t // tm, nf),
        in_specs=[pl.BlockSpec((tm, D_MODEL), lambda i, j: (i, 0)),
                  pl.BlockSpec((1, D_MODEL), lambda i, j: (0, 0)),
                  pl.BlockSpec((D_MODEL, tf), lambda i, j: (0, j)),
                  pl.BlockSpec((D_MODEL, tf), lambda i, j: (0, nf + j))],
        out_specs=[ff, ff, ff, pl.BlockSpec((tm, D_MODEL), lambda i, j: (i, 0))],
        out_shape=[jax.ShapeDtypeStruct((t, D_FF), BF16)] * 3 + [jax.ShapeDtypeStruct((t, D_MODEL), BF16)],
        compiler_params=_cp(("parallel", "arbitrary"), VMEM_LIMIT),
    )(x1, g, w_gu, w_gu)


def matmul_residual(a, w, res, name, tm=512, tk=FFN_TF):
    t, k = a.shape
    n = w.shape[1]
    tm = min(tm, t)

    def body(a_ref, w_ref, r_ref, o_ref):
        @pl.when(pl.program_id(1) == 0)
        def _():
            o_ref[...] = r_ref[...]

        o_ref[...] += _nn(a_ref[...], w_ref[...])

    return pl.pallas_call(
        body, name=name, grid=(t // tm, k // tk),
        in_specs=[pl.BlockSpec((tm, tk), lambda i, j: (i, j)),
                  pl.BlockSpec((tk, n), lambda i, j: (j, 0)),
                  pl.BlockSpec((tm, n), lambda i, j: (i, 0))],
        out_specs=pl.BlockSpec((tm, n), lambda i, j: (i, 0)),
        out_shape=jax.ShapeDtypeStruct((t, n), F32),
        compiler_params=_cp(("parallel", "arbitrary"), VMEM_LIMIT),
    )(a, w, res)


def ffn_act_bwd(dx2, gate, up, w_down, tm=512, tf=FFN_TF):
    t = dx2.shape[0]
    tm = min(tm, t)

    def body(dx2_ref, gate_ref, up_ref, wd_ref, dgate_ref, dup_ref, dx2b_ref):
        @pl.when(pl.program_id(1) == 0)
        def _():
            dx2b_ref[...] = _bf(dx2_ref[...])

        dact = _nt(dx2b_ref[...], wd_ref[...])
        gt, upv = gate_ref[...].astype(F32), up_ref[...].astype(F32)
        sg = _sigmoid(gt)
        t = dact * sg
        dgate_ref[...] = _bf(t * upv * (1.0 + gt * (1.0 - sg)))
        dup_ref[...] = _bf(t * gt)

    ff = pl.BlockSpec((tm, tf), lambda i, j: (i, j))
    return pl.pallas_call(
        body, name="ffn_act_bwd", grid=(t // tm, D_FF // tf),
        in_specs=[pl.BlockSpec((tm, D_MODEL), lambda i, j: (i, 0)), ff, ff,
                  pl.BlockSpec((tf, D_MODEL), lambda i, j: (j, 0))],
        out_specs=[ff, ff],
        out_shape=[jax.ShapeDtypeStruct((t, D_FF), BF16)] * 2,
        scratch_shapes=[pltpu.VMEM((tm, D_MODEL), BF16)],
        compiler_params=_cp(("parallel", "arbitrary"), VMEM_LIMIT),
    )(dx2, gate, up, w_down)


def tail_fwd_bwd(x2, p, target, g_ple, g_final, w_pg, w_pp, tm=256):
    t = x2.shape[0]
    tm = min(tm, t)

    def body(x_ref, p_ref, t_ref, gp_ref, gf_ref, wpg_ref, wpp_ref,
             dx_ref, dwpg_ref, dwpp_ref, dgp_ref, dgf_ref, loss_ref):
        @pl.when(pl.program_id(0) == 0)
        def _():
            dwpg_ref[...] = jnp.zeros_like(dwpg_ref)
            dwpp_ref[...] = jnp.zeros_like(dwpp_ref)
            dgp_ref[...] = jnp.zeros_like(dgp_ref)
            dgf_ref[...] = jnp.zeros_like(dgf_ref)
            loss_ref[...] = jnp.zeros_like(loss_ref)

        x2v = x_ref[...]
        gp, gf = gp_ref[...], gf_ref[...]
        r3, n3 = _rms(x2v)
        h3b = _bf(n3 * gp)
        pb = _bf(p_ref[...])
        pg = _sigmoid(_nn(h3b, wpg_ref[...]))
        pp = _nn(pb, wpp_ref[...])
        x3 = x2v + pg * pp
        r4, n4 = _rms(x3)
        err = n4 * gf - t_ref[...]
        part = 0.5 * jnp.sum(jnp.sum(err * err, axis=1, keepdims=True), axis=0, keepdims=True) / D_MODEL
        loss_ref[...] += jnp.broadcast_to(part, (8, 128))
        dy = err * (1.0 / D_MODEL)
        dx3, dgf = _rms_bwd(dy, gf, r4, n4)
        dgf_ref[...] += jnp.broadcast_to(jnp.sum(dgf, axis=0, keepdims=True), (8, D_MODEL))
        dzp = _bf(dx3 * pp * pg * (1.0 - pg))
        dpp = _bf(dx3 * pg)
        dwpg_ref[...] += _tn(h3b, dzp)
        dwpp_ref[...] += _tn(pb, dpp)
        dh3 = _nt(dzp, wpg_ref[...])
        dx, dgp = _rms_bwd(dh3, gp, r3, n3)
        dgp_ref[...] += jnp.broadcast_to(jnp.sum(dgp, axis=0, keepdims=True), (8, D_MODEL))
        dx_ref[...] = dx3 + dx

    rowd = pl.BlockSpec((tm, D_MODEL), lambda i: (i, 0))
    fixed = lambda shp: pl.BlockSpec(shp, lambda i: (0,) * len(shp))
    return pl.pallas_call(
        body, name="tail_fwd_bwd", grid=(t // tm,),
        in_specs=[rowd, pl.BlockSpec((tm, PLE_DIM), lambda i: (i, 0)), rowd,
                  fixed((1, D_MODEL)), fixed((1, D_MODEL)), fixed((D_MODEL, D_MODEL)), fixed((PLE_DIM, D_MODEL))],
        out_specs=[rowd, fixed((D_MODEL, D_MODEL)), fixed((PLE_DIM, D_MODEL)),
                   fixed((8, D_MODEL)), fixed((8, D_MODEL)), fixed((8, 128))],
        out_shape=[jax.ShapeDtypeStruct((t, D_MODEL), F32), jax.ShapeDtypeStruct((D_MODEL, D_MODEL), F32),
                   jax.ShapeDtypeStruct((PLE_DIM, D_MODEL), F32), jax.ShapeDtypeStruct((8, D_MODEL), F32),
                   jax.ShapeDtypeStruct((8, D_MODEL), F32), jax.ShapeDtypeStruct((8, 128), F32)],
        compiler_params=_cp(("arbitrary",), VMEM_LIMIT),
    )(x2, p, target, g_ple, g_final, w_pg, w_pp)


def in_proj_bwd(pieces, weights, x, dx1, g, name="in_proj_bwd", tm=256):
    t = x.shape[0]
    tm = min(tm, t)
    k = len(pieces)
    assert all(c0 % wd == 0 and w0 % wd == 0 for (_, c0, wd), (_, w0) in zip(pieces, weights))

    def body(*refs):
        p_refs, w_refs = refs[:k], refs[k:2 * k]
        x_ref, dx1_ref, g_ref, dx_ref, dg_ref = refs[2 * k:]

        @pl.when(pl.program_id(0) == 0)
        def _():
            dg_ref[...] = jnp.zeros_like(dg_ref)

        dh = _nt(_bf(p_refs[0][...]), w_refs[0][...])
        for pr, wr in zip(p_refs[1:], w_refs[1:]):
            dh = dh + _nt(_bf(pr[...]), wr[...])
        r, n = _rms(x_ref[...])
        dx, dgc = _rms_bwd(dh, g_ref[...], r, n)
        dx_ref[...] = dx1_ref[...] + dx
        dg_ref[...] += jnp.broadcast_to(jnp.sum(dgc, axis=0, keepdims=True), (8, D_MODEL))

    rowd = pl.BlockSpec((tm, D_MODEL), lambda i: (i, 0))
    return pl.pallas_call(
        body, name=name, grid=(t // tm,),
        in_specs=[pl.BlockSpec((tm, wd), functools.partial(lambda i, cb: (i, cb), cb=c0 // wd))
                  for _, c0, wd in pieces]
        + [pl.BlockSpec((w.shape[0], wd), functools.partial(lambda i, cb: (0, cb), cb=w0 // wd))
           for (w, w0), (_, _, wd) in zip(weights, pieces)]
        + [rowd, rowd, pl.BlockSpec((1, D_MODEL), lambda i: (0, 0))],
        out_specs=[rowd, pl.BlockSpec((8, D_MODEL), lambda i: (0, 0))],
        out_shape=[jax.ShapeDtypeStruct((t, D_MODEL), F32), jax.ShapeDtypeStruct((8, D_MODEL), F32)],
        compiler_params=_cp(("arbitrary",), VMEM_LIMIT),
    )(*[a for a, _, _ in pieces], *[w for w, _ in weights], x, dx1, g)


def adamw(w, g, m, v, name, rows_cap=256, dep=None):
    lead = w.shape[:-2]
    r, c = w.shape[-2:]
    tr = r
    for cand in range(8, min(r, rows_cap) + 1, 8):
        if r % cand == 0:
            tr = cand

    def body(w_ref, g_ref, m_ref, v_ref, *rest):
        d_ref, mo_ref, vo_ref = rest[-3:]
        gv = g_ref[...]
        mn = ADAM_B1 * m_ref[...] + (1.0 - ADAM_B1) * gv
        vn = ADAM_B2 * v_ref[...] + (1.0 - ADAM_B2) * (gv * gv)
        m_hat = mn / (1.0 - ADAM_B1 ** ADAM_STEP)
        v_hat = vn / (1.0 - ADAM_B2 ** ADAM_STEP)
        d_ref[...] = -ADAM_LR * (m_hat / (jnp.sqrt(v_hat) + ADAM_EPS) + ADAM_WD * w_ref[...])
        mo_ref[...] = mn
        vo_ref[...] = vn

    spec = pl.BlockSpec((None,) * len(lead) + (tr, c), lambda i: (0,) * len(lead) + (i, 0))
    extra = [] if dep is None else [dep]
    return pl.pallas_call(
        body, name=name, grid=(r // tr,),
        in_specs=[spec] * 4 + [pl.BlockSpec((8, 128), lambda i: (0, 0))] * len(extra), out_specs=[spec] * 3,
        out_shape=[jax.ShapeDtypeStruct(w.shape, F32)] * 3,
        compiler_params=_cp(("parallel",), VMEM_LIMIT),
    )(w, g.reshape(w.shape), m, v, *extra)


def _w_in_shards(dwp):
    cs = D_IN // N_CHIPS
    regions = ((0, SPLIT_Z, P_QA), (SPLIT_Z, SPLIT_Z + 8, P_BD - SPLIT_Z), (SPLIT_Z + 8, D_IN, -(SPLIT_Z + 8)))

    def original(lo, hi):
        parts = [dwp[:, max(lo, a) + off:min(hi, e) + off] for a, e, off in regions if max(lo, a) < min(hi, e)]
        return parts[0] if len(parts) == 1 else jnp.concatenate(parts, axis=1)

    return jnp.stack([original(s * cs, (s + 1) * cs) for s in range(N_CHIPS)])


class Standalone:
    def __init__(self, later_weights):
        self.later_weights = later_weights

    def begin(self, *a):
        return 0.0

    forward = exchange = join = begin

    def finish(self, after):
        return self.later_weights


def local_step(x3d, p3d, target3d, g4, small, later, early):
    b, s, _ = x3d.shape
    t = b * s
    x = x3d.reshape(t, D_MODEL)
    p = p3d.reshape(t, PLE_DIM)
    target = target3d.reshape(t, D_MODEL)
    cut = SPLIT_Z - 2 * (D_IN // N_CHIPS)
    w_inp = jnp.concatenate([g4[2][:, cut + 8:], g4[3], g4[0], g4[1], g4[2][:, :cut], g4[2][:, cut:cut + 8],
                             jnp.zeros((D_MODEL, 120), BF16)], axis=1)
    al_row = jnp.pad(small["a_log"].reshape(1, B_HEADS), ((0, 0), (0, 128 - B_HEADS)))
    dtb_row = jnp.pad(small["dt_bias"].reshape(1, B_HEADS), ((0, 0), (0, 128 - B_HEADS)))
    conv_w8 = jnp.pad(small["conv_w"].reshape(CONV_K, CONV_CH), ((0, 8 - CONV_K), (0, 0)))
    w_on = small["w_onorm"].reshape(1, B_DIM)
    g_mix, g_ffn = small["g_mix"].reshape(1, D_MODEL), small["g_ffn"].reshape(1, D_MODEL)
    g_ple, g_final = small["g_ple"].reshape(1, D_MODEL), small["g_final"].reshape(1, D_MODEL)
    bias_band = make_bias_band(small["rel_bias"].reshape(A_HEADS, N_REL))

    tok = later.begin()
    proj, h1, bd32 = rms_matmul(x, g_mix + tok, w_inp, "in_proj", tm=1024, tn_cap=1152)
    y_a, lse = attn_fwd(proj, bias_band, b, s)
    tok = later.forward(lse)
    c = conv_fwd(proj, conv_w8 + tok, b, s)
    u, wk, qg, kdec, pm, egl, tmat = dn_prep(c, bd32, al_row, dtb_row, b, s)
    o_b, states = dn_scan_fwd(u, wk, qg, kdec, pm, egl, b, s)
    wts = later.finish(o_b)
    x1, merged = mid_fwd(x, y_a, o_b, proj, w_on, wts["w_branch_a"], wts["w_branch_b"], wts["w_out"])
    gate, up, act, h2 = ffn_up(x1, g_ffn, wts["w_gate_up"])
    x2 = matmul_residual(act, wts["w_down"], x1, "ffn_down")

    dx2, dw_pg, dw_pp, dg_ple, dg_final, loss = tail_fwd_bwd(
        x2, p, target, g_ple, g_final, wts["w_ple_gate"], wts["w_ple_proj"])
    dgate, dup = ffn_act_bwd(dx2, gate, up, wts["w_down"])
    w_gu = wts["w_gate_up"]
    dx1, dg_ffn = in_proj_bwd([(dgate, 0, D_FF), (dup, 0, D_FF)], [(w_gu, 0), (w_gu, D_FF)], x1, dx2, g_ffn,
                              name="ffn_in_bwd")
    dw_down = matmul_tn(act, dx2, "dw_down")
    dw_gu = matmul_tn(h2, dgate, "dw_gate", width=2 * D_FF, tiles_major=True)
    dw_gu = matmul_tn(h2, dup, "dw_up", into=dw_gu, col0=D_FF, width=2 * D_FF, tiles_major=True)
    dy_a, do_b, dz, dgates, dw_out, dwa, dwb, dw_on = mid_bwd(
        dx1, merged, y_a, o_b, proj, w_on, wts["w_branch_a"], wts["w_branch_b"], wts["w_out"])
    tok = early.begin(dict(w_branch_a=dwa, w_branch_b=dwb, w_out=dw_out, w_gate_up=dw_gu, w_down=dw_down,
                           w_ple_gate=dw_pg, w_ple_proj=dw_pp))
    ddw, ddwk, ddqg, ddkdec, ddp, ddegl = dn_scan_bwd(u, wk, qg, kdec, pm, egl + tok, states, do_b, b, s)
    tok = early.exchange(ddegl)
    dc3, dbd, dal, ddtb = dn_post_bwd(c, bd32, al_row + tok, dtb_row, tmat, ddw, ddwk, ddqg, ddkdec, ddp, ddegl, b, s)
    dconv, dconv_w = conv_bwd(proj, conv_w8, dc3, b, s)
    dqa, dka, dva, dbt, dbf = attn_bwd(proj, bias_band, y_a, lse, dy_a, b, s)
    tok = early.join(dqa)
    d_rel = bias_band_grad(dbt, dbf)

    pieces = [dgates, dqa, dka, dva, dconv, dz, dbd]
    bounds = [0, 2048, 2560, 3072, 3584, 5120, 5632, 5760]
    windows = [(dgates, 0, 2048), (dqa, 0, 512), (dka, 0, 512), (dva, 0, 512), (dconv, 0, 512), (dconv, 512, 512),
               (dconv, 1024, 512), (dz, 0, 512), (dbd, 0, 128)]
    w_cols = [0, P_QA, P_KA, P_VA, P_CONV, P_CONV + 512, P_CONV + 1024, P_Z, P_BD]
    dx, dg_mix = in_proj_bwd(windows, [(w_inp, c0) for c0 in w_cols], x, dx1, g_mix + tok)
    dwp = None
    for k, pc in enumerate(pieces):
        dwp = matmul_tn(h1, pc, "dw_in_%d" % k, into=dwp, col0=bounds[k], width=P_WIDTH)
    reduced_early = early.finish(dwp)
    dw_in = _w_in_shards(dwp)

    grads = dict(w_in=dw_in, w_branch_a=dwa, w_branch_b=dwb, w_out=dw_out, w_gate_up=dw_gu, w_down=dw_down,
                 w_ple_gate=dw_pg, w_ple_proj=dw_pp)
    small_grads = dict(g_mix=dg_mix[0], g_ffn=dg_ffn[0], g_ple=dg_ple[0], g_final=dg_final[0],
                       conv_w=dconv_w[:CONV_K].reshape(-1), rel_bias=d_rel.reshape(-1), w_onorm=dw_on[0],
                       a_log=dal[0, :B_HEADS], dt_bias=ddtb[0, :B_HEADS], loss=loss[0, :1])
    return dx.reshape(b, s, D_MODEL), grads, small_grads, reduced_early


BIG = (("w_in", (D_MODEL, D_IN), 1), ("w_branch_a", (A_WIDTH, D_MODEL), 1), ("w_branch_b", (B_WIDTH, D_MODEL), 1),
       ("w_out", (D_MODEL, D_MODEL), 0), ("w_gate_up", (D_MODEL, 2 * D_FF), 1), ("w_down", (D_FF, D_MODEL), 0),
       ("w_ple_gate", (D_MODEL, D_MODEL), 0), ("w_ple_proj", (PLE_DIM, D_MODEL), 1))
N_CHIPS = 4
FIRST_WEIGHTS = ("w_in",)
LATER_WEIGHTS = ("w_branch_a", "w_branch_b", "w_out", "w_gate_up", "w_down", "w_ple_gate", "w_ple_proj")
LATE_GRADS = ("w_in",)
EARLY_GRADS = ("w_branch_a", "w_branch_b", "w_out", "w_gate_up", "w_down", "w_ple_gate", "w_ple_proj")


def _items(names):
    return [it for it in BIG if it[0] in names]


def _shard_shape(shape, axis):
    return (shape[0] // N_CHIPS, shape[1]) if axis == 0 else (shape[0], shape[1] // N_CHIPS)


def _width_groups(names):
    groups = {}
    for n, shape, axis in _items(names):
        rs, cs = _shard_shape(shape, axis)
        groups.setdefault(cs, []).append((n, rs))
    return sorted(groups.items())


def grad_buffers(grads, names):
    info = {n: (shape, axis) for n, shape, axis in _items(names)}
    bufs = []
    for cs, members in _width_groups(names):
        segs = []
        for n, rs in members:
            g = grads[n].astype(BF16)
            if g.ndim == 2:
                g = (g.reshape(N_CHIPS, rs, cs) if info[n][1] == 0
                     else jnp.transpose(g.reshape(rs, N_CHIPS, cs), (1, 0, 2)))
            segs.append(g)
        bufs.append(segs[0] if len(segs) == 1 else jnp.concatenate(segs, axis=1))
    return bufs


def split_buffers(reduced, names):
    out = {}
    for (cs, members), buf in zip(_width_groups(names), reduced):
        r0 = 0
        for n, rs in members:
            out[n] = buf[r0:r0 + rs]
            r0 += rs
    return out


def _place():
    return lax.axis_index("x"), lax.axis_index("y"), lax.axis_index("c")


ANY = pl.BlockSpec(memory_space=pl.ANY)


def _gathered_shape(item):
    n, shape, _ = item
    return (N_CHIPS,) + _shard_shape(shape, 1) if n == "w_in" else shape


def _gather_block(o_ref, item, cx, cy, hf):
    n, shape, axis = item
    rs, cs = _shard_shape(shape, axis)
    hr = rs // 2
    ci = 2 * cx + cy
    if n == "w_in":
        return o_ref.at[ci, pl.ds(pl.multiple_of(hf * hr, 16), hr), :]
    if axis == 0:
        return o_ref.at[pl.ds(pl.multiple_of(ci * rs + hf * hr, 16), hr), :]
    return o_ref.at[pl.ds(pl.multiple_of(hf * hr, 16), hr), pl.ds(pl.multiple_of(ci * cs, 128), cs)]


def _own_half(w_ref, item, c):
    hr = _shard_shape(item[1], item[2])[0] // 2
    return w_ref.at[pl.ds(pl.multiple_of(c * hr, 16), hr), :]


def _gather_slot(o_ref, item, cx, cy):
    n, shape, axis = item
    rs, cs = _shard_shape(shape, axis)
    ci = 2 * cx + cy
    if n == "w_in":
        return o_ref.at[ci]
    if axis == 0:
        return o_ref.at[pl.ds(pl.multiple_of(ci * rs, 16), rs), :]
    return o_ref.at[:, pl.ds(pl.multiple_of(ci * cs, 128), cs)]


def _other_chips(x, y):
    return [(1 - x, y), (x, 1 - y), (1 - x, 1 - y)]


def allgather_weights(shards, names):
    items = _items(names)
    nw = len(items)

    def body(*refs):
        w_refs, o_refs = refs[:nw], refs[nw:2 * nw]
        send_sems, recv_sems = refs[2 * nw:]
        x, y, c = _place()
        sibling = (x, y, 1 - c)
        chips = _other_chips(x, y)

        def copy(k, src, dst, to):
            return pltpu.make_async_remote_copy(src_ref=src, dst_ref=dst, send_sem=send_sems.at[k],
                                                recv_sem=recv_sems.at[k], device_id=to, device_id_type=MESH)

        def blk(i, cx, cy, hf):
            return _gather_block(o_refs[i], items[i], cx, cy, hf)

        def my_half(i):
            return _own_half(w_refs[i], items[i], c)

        def own(i):
            return _gather_slot(o_refs[i], items[i], x, y)

        first = [copy(7 * i + j, my_half(i), blk(i, x, y, c), (*chip_, c))
                 for i in range(nw) for j, chip_ in enumerate(chips)]
        first += [copy(7 * i + 6, w_refs[i], own(i), sibling) for i in range(nw)]
        for cp in first:
            cp.start()
        passed = []
        for i in range(nw):
            for j, chip_ in enumerate(chips):
                copy(7 * i + j, my_half(i), blk(i, *chip_, c), (*chip_, c)).wait_recv()
                fwd = copy(7 * i + 3 + j, blk(i, *chip_, c), blk(i, *chip_, c), sibling)
                fwd.start()
                passed.append(fwd)
        for i in range(nw):
            for j, chip_ in enumerate(chips):
                copy(7 * i + 3 + j, my_half(i), blk(i, *chip_, 1 - c), sibling).wait_recv()
            copy(7 * i + 6, w_refs[i], own(i), sibling).wait_recv()
        for cp in first + passed:
            cp.wait_send()

    outs = pl.pallas_call(
        body, name="allgather_weights",
        in_specs=[ANY] * nw, out_specs=[ANY] * nw,
        out_shape=[jax.ShapeDtypeStruct(_gathered_shape(it), BF16) for it in items],
        scratch_shapes=[pltpu.SemaphoreType.DMA((7 * nw,)), pltpu.SemaphoreType.DMA((7 * nw,))],
    )(*[shards[it[0]] for it in items])
    return {it[0]: o for it, o in zip(items, outs)}


HBM_SPEC = pl.BlockSpec(memory_space=pltpu.HBM)
SEM_SPEC = pl.BlockSpec(memory_space=pltpu.SEMAPHORE)
EFFECT = pltpu.SideEffectType.DATAFLOW_SIDE_EFFECTING


def _in_hbm(a):
    return pltpu.with_memory_space_constraint(a, pltpu.HBM)


def copies_start(name, bufs, ncopies, plan):
    nb = len(bufs)

    def body(*refs):
        in_refs, send_sems, recv_sems, token = refs[:nb], refs[nb], refs[nb + 1], refs[-1]
        for k, (src, dst, to) in enumerate(plan(in_refs)):
            pltpu.make_async_remote_copy(src_ref=src, dst_ref=dst, send_sem=send_sems.at[k],
                                         recv_sem=recv_sems.at[k], device_id=to, device_id_type=MESH).start()
        token[...] = jnp.zeros_like(token)

    outs = pl.pallas_call(
        body, name=name,
        in_specs=[HBM_SPEC] * nb,
        out_specs=(SEM_SPEC, SEM_SPEC, *[HBM_SPEC] * nb, pl.BlockSpec(memory_space=pltpu.VMEM)),
        out_shape=(pltpu.SemaphoreType.DMA((ncopies,)), pltpu.SemaphoreType.DMA((ncopies,)),
                   *[pltpu.HBM(b.shape, b.dtype) for b in bufs], jax.ShapeDtypeStruct((8, 128), F32)),
        input_output_aliases={i: 2 + i for i in range(nb)},
        compiler_params=pltpu.CompilerParams(has_side_effects=EFFECT),
    )(*[_in_hbm(b) for b in bufs])
    return outs[0], outs[1], list(outs[2:2 + nb]), outs[-1][0, 0]


def copies_wait(name, send_sems, recv_sems, bufs, after, plan):
    nb = len(bufs)

    def body(*refs):
        in_refs, s_sems, r_sems = refs[:nb], refs[nb], refs[nb + 1]
        for k, (src, dst, to) in enumerate(plan(in_refs)):
            cp = pltpu.make_async_remote_copy(src_ref=src, dst_ref=dst, send_sem=s_sems.at[k],
                                              recv_sem=r_sems.at[k], device_id=to, device_id_type=MESH)
            cp.wait_send()
            cp.wait_recv()

    return list(pl.pallas_call(
        body, name=name,
        in_specs=[HBM_SPEC] * nb + [SEM_SPEC, SEM_SPEC, ANY],
        out_specs=tuple([HBM_SPEC] * nb),
        out_shape=tuple(pltpu.HBM(b.shape, b.dtype) for b in bufs),
        input_output_aliases={i: i for i in range(nb)},
        compiler_params=pltpu.CompilerParams(has_side_effects=EFFECT),
    )(*bufs, send_sems, recv_sems, after))


def _landing(shape, dtype):
    return _in_hbm(lax.empty(shape, dtype))


class LaterWeights:
    def __init__(self, shards):
        self.items = _items(LATER_WEIGHTS)
        self.shards = shards
        self.nw = len(self.items)

    def _ici_plan(self, refs):
        x, y, c = _place()
        w_refs, o_refs = refs[:self.nw], refs[self.nw:]
        plan = [(_own_half(w_refs[i], it, c), _gather_block(o_refs[i], it, x, y, c), (*chip_, c))
                for i, it in enumerate(self.items) for chip_ in _other_chips(x, y)]
        return plan + [(w_refs[i], _gather_slot(o_refs[i], it, x, y), (x, y, 1 - c))
                       for i, it in enumerate(self.items)]

    def _d2d_plan(self, refs):
        x, y, c = _place()
        return [(_gather_block(refs[i], it, *chip_, c), _gather_block(refs[i], it, *chip_, c), (x, y, 1 - c))
                for i, it in enumerate(self.items) for chip_ in _other_chips(x, y)]

    def _d2d_wait_plan(self, refs):
        x, y, c = _place()
        return [(_gather_block(refs[i], it, *chip_, c), _gather_block(refs[i], it, *chip_, 1 - c), (x, y, 1 - c))
                for i, it in enumerate(self.items) for chip_ in _other_chips(x, y)]

    def _ici_wait_plan(self, refs):
        x, y, c = _place()
        w_refs, o_refs = refs[:self.nw], refs[self.nw:]
        plan = [(_own_half(w_refs[i], it, c), _gather_block(o_refs[i], it, *chip_, c), (*chip_, c))
                for i, it in enumerate(self.items) for chip_ in _other_chips(x, y)]
        return plan + [(w_refs[i], _gather_slot(o_refs[i], it, x, y), (x, y, 1 - c))
                       for i, it in enumerate(self.items)]

    def begin(self):
        srcs = [self.shards[it[0]] for it in self.items]
        lands = [_landing(_gathered_shape(it), BF16) for it in self.items]
        self.s1, self.r1, self.b1, tok = copies_start("gather_ici_start", srcs + lands, 4 * self.nw, self._ici_plan)
        return tok

    def forward(self, after):
        b1 = copies_wait("gather_ici_wait", self.s1, self.r1, self.b1, after, self._ici_wait_plan)
        self.s2, self.r2, self.b2, tok = copies_start("gather_d2d_start", b1[self.nw:], 3 * self.nw, self._d2d_plan)
        return tok

    def finish(self, after):
        outs = copies_wait("gather_d2d_wait", self.s2, self.r2, self.b2, after, self._d2d_wait_plan)
        return {it[0]: o for it, o in zip(self.items, outs)}


def small_allreduce(v, name):
    r = v.shape[0]

    def body(v_ref, o_ref, buf, send_sems, recv_sems):
        x, y, c = _place()
        me = 4 * x + 2 * y + c
        buf[me] = v_ref[...]
        flips = [(fx, fy, fc) for fx in (0, 1) for fy in (0, 1) for fc in (0, 1)][1:]
        peers = [((1 - x) if fx else x, (1 - y) if fy else y, (1 - c) if fc else c) for fx, fy, fc in flips]

        def copy(k, slot, to):
            return pltpu.make_async_remote_copy(src_ref=v_ref, dst_ref=buf.at[slot], send_sem=send_sems.at[k],
                                                recv_sem=recv_sems.at[k], device_id=to, device_id_type=MESH)

        sends = [copy(k, me, peer) for k, peer in enumerate(peers)]
        for cp in sends:
            cp.start()
        for k, (px, py, pc) in enumerate(peers):
            copy(k, 4 * px + 2 * py + pc, (px, py, pc)).wait_recv()
        for cp in sends:
            cp.wait_send()
        acc = buf[0]
        for d in range(1, 8):
            acc = acc + buf[d]
        o_ref[...] = acc

    return pl.pallas_call(
        body, name=name,
        in_specs=[pl.BlockSpec(memory_space=pltpu.VMEM)], out_specs=pl.BlockSpec(memory_space=pltpu.VMEM),
        out_shape=jax.ShapeDtypeStruct((r, 128), F32),
        scratch_shapes=[pltpu.VMEM((8, r, 128), F32), pltpu.SemaphoreType.DMA((7,)), pltpu.SemaphoreType.DMA((7,))],
    )(v)


def add_halves(g, other, place):
    half, wd = other.shape[1:]
    tr = _tile_rows(half, wd)
    nblk = half // tr

    def body(pref, g0, g1, g2, g3, o0, o1, o2, o3, pf_ref, pb_ref):
        f = lambda r: r[...].astype(F32)
        pf_ref[...] = f(g0) + f(o0)
        pb_ref[0] = _bf(f(g1) + f(o1))
        pb_ref[1] = _bf(f(g2) + f(o2))
        pb_ref[2] = _bf(f(g3) + f(o3))

    gspec = lambda k: pl.BlockSpec((None, tr, wd), lambda i, pr: ((pr[0] + k) % N_CHIPS, pr[1] * nblk + i, 0))
    ospec = lambda k: pl.BlockSpec((None, tr, wd), lambda i, pr: ((pr[0] + k) % N_CHIPS, i, 0))
    return pl.pallas_call(
        body, name="add_halves",
        grid_spec=pltpu.PrefetchScalarGridSpec(
            num_scalar_prefetch=1, grid=(nblk,),
            in_specs=[gspec(0), gspec(1), gspec(2), gspec(3), ospec(0), ospec(1), ospec(2), ospec(3)],
            out_specs=[pl.BlockSpec((tr, wd), lambda i, pr: (i, 0)),
                       pl.BlockSpec((3, tr, wd), lambda i, pr: (0, i, 0))]),
        out_shape=[jax.ShapeDtypeStruct((half, wd), F32), jax.ShapeDtypeStruct((3, half, wd), BF16)],
        compiler_params=_cp(("parallel",), VMEM_LIMIT),
    )(place, g, g, g, g, other, other, other, other)


def _tile_rows(n, width):
    best = 16
    for t in range(16, max(16, (384 * 1024) // width) + 1, 16):
        if n % t == 0:
            best = t
    assert n % best == 0
    return best


def add_partials(pf, got, place):
    half, wd = pf.shape
    tr = _tile_rows(half, wd)

    def body(pref, pf_ref, got_ref, o_ref):
        o_ref[...] = ((pf_ref[...] + got_ref[0].astype(F32)) + got_ref[1].astype(F32)) + got_ref[2].astype(F32)

    return pl.pallas_call(
        body, name="add_partials",
        grid_spec=pltpu.PrefetchScalarGridSpec(
            num_scalar_prefetch=1, grid=(half // tr,),
            in_specs=[pl.BlockSpec((tr, wd), lambda i, pr: (i, 0)),
                      pl.BlockSpec((3, tr, wd), lambda i, pr: (0, i, 0))],
            out_specs=pl.BlockSpec((None, tr, wd), lambda i, pr: (pr[1], i, 0))),
        out_shape=jax.ShapeDtypeStruct((2, half, wd), F32),
        compiler_params=_cp(("parallel",), VMEM_LIMIT),
    )(place, pf, got)


class GradReduce:
    def __init__(self, place, names, tag):
        self.place, self.names, self.tag = place, names, tag
        self.nb = len(_width_groups(names))

    def _swap_plan(self, refs):
        x, y, c = _place()
        plan = []
        for g_ref, o_ref in zip(refs[:self.nb], refs[self.nb:]):
            half = o_ref.shape[1]
            plan.append((g_ref.at[:, pl.ds(pl.multiple_of((1 - c) * half, 16), half), :], o_ref, (x, y, 1 - c)))
        return plan

    def _exchange_plan(self, refs):
        x, y, c = _place()
        me = 2 * x + y
        return [(p_ref.at[k - 1], o_ref.at[k - 1], (((me + k) % N_CHIPS) // 2, ((me + k) % N_CHIPS) % 2, c))
                for p_ref, o_ref in zip(refs[:self.nb], refs[self.nb:]) for k in range(1, N_CHIPS)]

    def _join_plan(self, refs):
        x, y, c = _place()
        return [(r.at[c], r.at[c], (x, y, 1 - c)) for r in refs]

    def _join_wait_plan(self, refs):
        x, y, c = _place()
        return [(r.at[c], r.at[1 - c], (x, y, 1 - c)) for r in refs]

    def begin(self, grads):
        gs = grad_buffers(grads, self.names)
        lands = [_landing((N_CHIPS, g.shape[1] // 2, g.shape[2]), BF16) for g in gs]
        self.s1, self.r1, self.b1, tok = copies_start(self.tag + "_swap_start", gs + lands, self.nb, self._swap_plan)
        return tok

    def exchange(self, after):
        b1 = copies_wait(self.tag + "_swap_wait", self.s1, self.r1, self.b1, after, self._swap_plan)
        sums = [add_halves(g, other, self.place) for g, other in zip(b1[:self.nb], b1[self.nb:])]
        self.pfs = [pf for pf, _ in sums]
        pbs = [pb for _, pb in sums]
        lands = [_landing(pb.shape, BF16) for pb in pbs]
        self.s2, self.r2, self.b2, tok = copies_start(self.tag + "_exchange_start", pbs + lands, 3 * self.nb,
                                                      self._exchange_plan)
        return tok

    def join(self, after):
        b2 = copies_wait(self.tag + "_exchange_wait", self.s2, self.r2, self.b2, after, self._exchange_plan)
        boths = [add_partials(pf, got, self.place) for pf, got in zip(self.pfs, b2[self.nb:])]
        self.s3, self.r3, self.b3, tok = copies_start(self.tag + "_join_start", boths, self.nb, self._join_plan)
        return tok

    def finish(self, after):
        boths = copies_wait(self.tag + "_join_wait", self.s3, self.r3, self.b3, after, self._join_wait_plan)
        return split_buffers([b.reshape(-1, b.shape[2]) for b in boths], self.names)


SMALL = (("g_mix", D_MODEL), ("g_ffn", D_MODEL), ("g_ple", D_MODEL), ("g_final", D_MODEL),
         ("conv_w", CONV_K * CONV_CH), ("rel_bias", A_HEADS * N_REL), ("w_onorm", B_DIM),
         ("a_log", B_HEADS), ("dt_bias", B_HEADS), ("loss", 1))


def _pad128(v):
    v = v.reshape(-1)
    return jnp.pad(v, (0, -v.shape[0] % 128))


def pack_small(d, names, rows):
    flat = jnp.concatenate([_pad128(d[n]) for n in names]).reshape(-1, 128)
    return jnp.pad(flat, ((0, rows - flat.shape[0]), (0, 0)))


def unpack_small(flat, names_sizes):
    out, r0 = {}, 0
    v = flat.reshape(-1)
    for n, size in names_sizes:
        out[n] = v[r0:r0 + size]
        r0 += -(-size // 128) * 128
    return out


def kernel(x, p, g_mix, w_in, conv_w, a_log, dt_bias, rel_bias, w_onorm, w_branch_a, w_branch_b, w_out, g_ffn, w_gate_up, w_down, g_ple, w_ple_gate, w_ple_proj, g_final, loss_target, m_g_mix, m_w_in, m_conv_w, m_a_log, m_dt_bias, m_rel_bias, m_w_onorm, m_w_branch_a, m_w_branch_b, m_w_out, m_g_ffn, m_w_gate_up, m_w_down, m_g_ple, m_w_ple_gate, m_w_ple_proj, m_g_final, v_g_mix, v_w_in, v_conv_w, v_a_log, v_dt_bias, v_rel_bias, v_w_onorm, v_w_branch_a, v_w_branch_b, v_w_out, v_g_ffn, v_w_gate_up, v_w_down, v_g_ple, v_w_ple_gate, v_w_ple_proj, v_g_final):
    names = ["g_mix", "w_in", "conv_w", "a_log", "dt_bias", "rel_bias", "w_onorm", "w_branch_a", "w_branch_b",
             "w_out", "g_ffn", "w_gate_up", "w_down", "g_ple", "w_ple_gate", "w_ple_proj", "g_final"]
    w = dict(zip(names, [g_mix, w_in, conv_w, a_log, dt_bias, rel_bias, w_onorm, w_branch_a, w_branch_b, w_out,
                         g_ffn, w_gate_up, w_down, g_ple, w_ple_gate, w_ple_proj, g_final]))
    m = dict(zip(names, [m_g_mix, m_w_in, m_conv_w, m_a_log, m_dt_bias, m_rel_bias, m_w_onorm, m_w_branch_a,
                         m_w_branch_b, m_w_out, m_g_ffn, m_w_gate_up, m_w_down, m_g_ple, m_w_ple_gate,
                         m_w_ple_proj, m_g_final]))
    v = dict(zip(names, [v_g_mix, v_w_in, v_conv_w, v_a_log, v_dt_bias, v_rel_bias, v_w_onorm, v_w_branch_a,
                         v_w_branch_b, v_w_out, v_g_ffn, v_w_gate_up, v_w_down, v_g_ple, v_w_ple_gate,
                         v_w_ple_proj, v_g_final]))
    xi, yi, ci = _place()
    chip = 2 * xi + yi
    big_names = [n for n, _, _ in BIG]

    shards2d = {n: w[n].reshape(w[n].shape[-2:]) for n in big_names}
    shards_bf = {n: a.astype(BF16) for n, a in shards2d.items()}
    g4 = allgather_weights(shards_bf, FIRST_WEIGHTS)["w_in"]
    place = jnp.stack([chip, ci]).astype(jnp.int32)
    conv_sh = jnp.where(ci == 0, w["conv_w"].reshape(CONV_K, CONV_CH // N_CHIPS), 0.0)
    conv_slots = lax.dynamic_update_slice(jnp.zeros((N_CHIPS, CONV_K, CONV_CH // N_CHIPS), F32), conv_sh[None],
                                          (chip, 0, 0))
    conv_all = small_allreduce(conv_slots.reshape(-1, 128), "gather_conv_w")
    conv_full = jnp.transpose(conv_all.reshape(N_CHIPS, CONV_K, CONV_CH // N_CHIPS), (1, 0, 2)).reshape(CONV_K, CONV_CH)
    small = {n: w[n] for n in names if n not in big_names}
    small["conv_w"] = conv_full

    grad_x, grads, small_grads, reduced_early = local_step(
        x, p[0], loss_target, g4, small, LaterWeights(shards_bf), GradReduce(place, EARLY_GRADS, "grads"))

    late = GradReduce(place, LATE_GRADS, "late")
    tok = late.begin(grads)
    small_names = [n for n, _ in SMALL]
    small_grads["loss"] = small_grads["loss"] + tok
    red_flat = small_allreduce(pack_small(small_grads, small_names, 112), "allreduce_small")
    red = unpack_small(red_flat, SMALL)
    dep = jnp.full((8, 128), late.exchange(red_flat), F32)
    gshard = dict(reduced_early)
    loss = red["loss"][0]
    conv_g = lax.dynamic_slice(red["conv_w"].reshape(CONV_K, N_CHIPS, CONV_CH // N_CHIPS), (0, chip, 0),
                               (CONV_K, 1, CONV_CH // N_CHIPS))
    gsmall = {n: red[n].reshape(w[n].shape) for n in small_names if n not in ("loss", "conv_w")}
    gsmall["conv_w"] = conv_g.reshape(w["conv_w"].shape)

    grad, delta, new_m, new_v = {}, {}, {}, {}
    for n in list(EARLY_GRADS) + list(LATE_GRADS):
        if n in LATE_GRADS:
            late.join(v_)
            gshard.update(late.finish(v_))
        shp = w[n].shape
        d_, m_, v_ = adamw(shards2d[n], gshard[n], m[n].reshape(shp[-2:]), v[n].reshape(shp[-2:]), "adamw_" + n,
                           dep=dep if n in EARLY_GRADS else None)
        dep, v_ = lax.optimization_barrier((dep, v_))
        grad[n], delta[n], new_m[n], new_v[n] = gshard[n].reshape(shp), d_.reshape(shp), m_.reshape(shp), v_.reshape(shp)
    snames = [n for n in small_names if n != "loss"]
    ssizes = [(n, w[n].size) for n in snames]
    pk = lambda d: pack_small(d, snames, 64)
    d_, m_, v_ = adamw(pk(w), pk(gsmall), pk(m), pk(v), "adamw_small")
    ds, ms, vs = unpack_small(d_, ssizes), unpack_small(m_, ssizes), unpack_small(v_, ssizes)
    for n in snames:
        shp = w[n].shape
        grad[n], delta[n], new_m[n], new_v[n] = gsmall[n], ds[n].reshape(shp), ms[n].reshape(shp), vs[n].reshape(shp)

    return (loss, grad_x, *[grad[n] for n in names], *[delta[n] for n in names],
            *[new_m[n] for n in names], *[new_v[n] for n in names])
```

```python
import functools

import jax
import jax.numpy as jnp
from jax import lax
from jax.experimental import pallas as pl
from jax.experimental.pallas import tpu as pltpu

F32 = jnp.float32
BF16 = jnp.bfloat16
MESH = pl.DeviceIdType.MESH

D_MODEL = 1024
CHUNK = 64
PLE_DIM = 256
EPS = 1e-6
A_HEADS = 8
A_HEAD_DIM = 64
A_WIDTH = 512
A_LOOKBACK = 8
BAND = (A_LOOKBACK + 1) * CHUNK
TAIL = 3 * CHUNK
REL_CLIP = 128
N_REL = 2 * REL_CLIP + 1
B_HEADS = 4
B_DIM = 128
B_WIDTH = 512
CONV_K = 4
CONV_CH = 1536
D_FF = 2816
SPLIT_Z = 3584
D_IN = 5640
ADAM_LR, ADAM_B1, ADAM_B2, ADAM_EPS, ADAM_WD, ADAM_STEP = 0.001, 0.9, 0.999, 1e-08, 0.01, 10

P_GATES, P_QA, P_KA, P_VA, P_CONV, P_Z, P_BD, P_WIDTH = 0, 2048, 2560, 3072, 3584, 5120, 5632, 5760

VMEM_LIMIT = 56 * 1024 * 1024


def _cp(sem, vmem=None, **kw):
    return pltpu.CompilerParams(dimension_semantics=sem, vmem_limit_bytes=vmem, **kw)


def _tile(n, cap):
    best = None
    for t in range(128, cap + 1, 128):
        if n % t == 0:
            best = t
    assert best is not None, (n, cap)
    return best


def _nn(a, b, prec=None):
    return lax.dot_general(a, b, (((1,), (0,)), ((), ())), preferred_element_type=F32, precision=prec)


def _nt(a, b, prec=None):
    return lax.dot_general(a, b, (((1,), (1,)), ((), ())), preferred_element_type=F32, precision=prec)


def _tn(a, b, prec=None):
    return lax.dot_general(a, b, (((0,), (0,)), ((), ())), preferred_element_type=F32, precision=prec)


def _bnn(a, b, prec=None):
    return lax.dot_general(a, b, (((2,), (1,)), ((0,), (0,))), preferred_element_type=F32, precision=prec)


def _bnt(a, b, prec=None):
    return lax.dot_general(a, b, (((2,), (2,)), ((0,), (0,))), preferred_element_type=F32, precision=prec)


def _bf(a):
    return a.astype(BF16)


def _split(a):
    hi = a.astype(BF16)
    return hi, (a - hi.astype(F32)).astype(BF16)


def _split3(a):
    h1 = _bf(a)
    r1 = a - h1.astype(F32)
    h2 = _bf(r1)
    return h1, h2, _bf(r1 - h2.astype(F32))


def _bnn_exact(lhs_b, rhs):
    h1, h2, h3 = _split3(rhs)
    return _bnn(lhs_b, h1) + (_bnn(lhs_b, h2) + _bnn(lhs_b, h3))


def _bnn3(a, b):
    ah, al = a if isinstance(a, tuple) else _split(a)
    bh, bl = b if isinstance(b, tuple) else _split(b)
    return _bnn(ah, bh) + (_bnn(ah, bl) + _bnn(al, bh))


def _sigmoid(x):
    return 0.5 * jnp.tanh(0.5 * x) + 0.5


def _softplus(x):
    return jnp.maximum(x, 0.0) + jnp.log(1.0 + jnp.exp(-jnp.abs(x)))


def rms_matmul(x, g, w, name, tm=512, tn_cap=1024):
    t, d = x.shape
    n = w.shape[1]
    tm = min(tm, t)
    tn = _tile(n, tn_cap)

    nj = n // tn

    def body(x_ref, g_ref, w_ref, o_ref, h_ref, tail_ref):
        @pl.when(pl.program_id(1) == 0)
        def _():
            xv = x_ref[...]
            r = lax.rsqrt(jnp.mean(xv * xv, axis=-1, keepdims=True) + EPS)
            h_ref[...] = _bf(xv * r * g_ref[...])

        res = _nn(h_ref[...], w_ref[...])
        o_ref[...] = _bf(res)

        @pl.when(pl.program_id(1) == nj - 1)
        def _():
            tail_ref[...] = res[:, tn - 128:]

    return pl.pallas_call(
        body, name=name, grid=(t // tm, nj),
        in_specs=[pl.BlockSpec((tm, d), lambda i, j: (i, 0)),
                  pl.BlockSpec((1, d), lambda i, j: (0, 0)),
                  pl.BlockSpec((d, tn), lambda i, j: (0, j))],
        out_specs=[pl.BlockSpec((tm, tn), lambda i, j: (i, j)),
                   pl.BlockSpec((tm, d), lambda i, j: (i, 0)),
                   pl.BlockSpec((tm, 128), lambda i, j: (i, 0))],
        out_shape=[jax.ShapeDtypeStruct((t, n), BF16), jax.ShapeDtypeStruct((t, d), BF16),
                   jax.ShapeDtypeStruct((t, 128), F32)],
        compiler_params=_cp(("parallel", "arbitrary"), VMEM_LIMIT),
    )(x, g, w)


def matmul_tn(a, b, name, into=None, col0=0, width=None, tm=1024, tk_cap=1408, tn_cap=1408, tiles_major=False):
    m, k1 = a.shape
    n = b.shape[1]
    tm = min(tm, m)
    tk = _tile(k1, tk_cap)
    tn = _tile(n, tn_cap)
    while col0 % tn:
        tn = _tile(n, tn - 128)
    nk = m // tm
    c0 = col0 // tn

    def body(*refs):
        a_ref, b_ref, o_ref, acc = refs[0], refs[1], refs[-2], refs[-1]

        @pl.when(pl.program_id(2) == 0)
        def _():
            acc[...] = jnp.zeros_like(acc)

        acc[...] += _tn(_bf(a_ref[...]), _bf(b_ref[...]))

        @pl.when(pl.program_id(2) == nk - 1)
        def _():
            o_ref[...] = _bf(acc[...])

    in_specs = [pl.BlockSpec((tm, tk), lambda i, j, k: (k, i)),
                pl.BlockSpec((tm, tn), lambda i, j, k: (k, j))]
    args = [a, b]
    total = n if width is None else width
    aliases = {}
    if into is not None:
        in_specs.append(ANY)
        args.append(into)
        aliases = {2: 0}
    if tiles_major:
        out_spec = pl.BlockSpec((None, tk, tn), lambda i, j, k: (c0 + j, i, 0))
        out_shape = jax.ShapeDtypeStruct((total // tn, k1, tn), BF16)
    else:
        out_spec = pl.BlockSpec((tk, tn), lambda i, j, k: (i, c0 + j))
        out_shape = jax.ShapeDtypeStruct((k1, total), BF16)
    return pl.pallas_call(
        body, name=name, grid=(k1 // tk, n // tn, nk),
        in_specs=in_specs,
        out_specs=out_spec,
        out_shape=out_shape,
        scratch_shapes=[pltpu.VMEM((tk, tn), F32)],
        input_output_aliases=aliases,
        compiler_params=_cp(("parallel", "parallel", "arbitrary"), VMEM_LIMIT),
    )(*args)


def _tail_onehot(qi):
    r = lax.broadcasted_iota(jnp.int32, (384, TAIL), 0)
    kj = lax.broadcasted_iota(jnp.int32, (384, TAIL), 1)
    return (r == jnp.minimum(REL_CLIP + qi - kj, REL_CLIP) + REL_CLIP).astype(F32)


def bias_tail(rel_pad):
    def body(rb_ref, o_ref):
        parts = _split3(rb_ref[...])
        for qi in range(CHUNK):
            oh = _bf(_tail_onehot(qi))
            o_ref[qi] = _nn(parts[0], oh) + (_nn(parts[1], oh) + _nn(parts[2], oh))

    return pl.pallas_call(
        body, name="bias_tail",
        out_shape=jax.ShapeDtypeStruct((CHUNK, A_HEADS, TAIL), F32),
    )(rel_pad)


def bias_grad(db_t, db_far):
    def body(t_ref, f_ref, o_ref):
        acc = jnp.zeros((A_HEADS, 384), F32)
        for qi in range(CHUNK):
            oh = _bf(_tail_onehot(qi))
            parts = _split3(t_ref[qi])
            acc = acc + (_nt(parts[0], oh) + (_nt(parts[1], oh) + _nt(parts[2], oh)))
        far = jnp.sum(jnp.sum(f_ref[...], axis=2), axis=1, keepdims=True)
        lane = lax.broadcasted_iota(jnp.int32, (A_HEADS, 384), 1)
        o_ref[...] = acc + jnp.where(lane == 2 * REL_CLIP, far, 0.0)

    return pl.pallas_call(
        body, name="bias_grad",
        out_shape=jax.ShapeDtypeStruct((A_HEADS, 384), F32),
    )(db_t, db_far)


ATT_CB = 8


WIN = BAND + CHUNK


def _stack_heads(a, lane):
    return jnp.concatenate([jnp.where(lane < 64, a, 0.0), jnp.where(lane >= 64, a, 0.0)], axis=0)


def _fill_band_pads(k_ref, v_ref, kp, vp, s):
    z = jnp.zeros((A_LOOKBACK * CHUNK, 128), BF16)
    kp[pl.ds(0, A_LOOKBACK * CHUNK), :] = z
    vp[pl.ds(0, A_LOOKBACK * CHUNK), :] = z
    kp[pl.ds(A_LOOKBACK * CHUNK, s), :] = _bf(k_ref[...])
    vp[pl.ds(A_LOOKBACK * CHUNK, s), :] = _bf(v_ref[...])


def attn_fwd(proj, bias_band, b, s):
    t = b * s
    nc = s // CHUNK
    qb, kb_, vb_ = P_QA // 128, P_KA // 128, P_VA // 128

    nstep = nc // ATT_CB
    rows = ATT_CB * CHUNK

    def body(q_ref, k_ref, v_ref, b_ref, o_ref, lse_ref, kp, vp):
        n0 = pl.program_id(2) * ATT_CB

        @pl.when(n0 == 0)
        def _():
            _fill_band_pads(k_ref, v_ref, kp, vp, s)

        lane = lax.broadcasted_iota(jnp.int32, (2 * CHUNK, 128), 1)
        col = lax.broadcasted_iota(jnp.int32, (4 * CHUNK, WIN), 1)
        bias4 = b_ref[...]

        def two_pairs(i, carry):
            pps = (2 * i, 2 * i + 1)
            ns = [n0 + 2 * pp for pp in pps]
            r0s = [pl.multiple_of(pp * 2 * CHUNK, 2 * CHUNK) for pp in pps]
            starts = [pl.multiple_of(n * CHUNK, CHUNK) for n in ns]
            kbs = [kp[pl.ds(st_, WIN), :] for st_ in starts]
            vbs = [vp[pl.ds(st_, WIN), :] for st_ in starts]
            q4s = [_bf(_stack_heads(q_ref[pl.ds(r0, 2 * CHUNK), :] * (A_HEAD_DIM ** -0.5), lane)) for r0 in r0s]
            qks = [_nt(q4, kb) for q4, kb in zip(q4s, kbs)]
            scs = [jnp.where(col >= (A_LOOKBACK - n) * CHUNK, qk + bias4, -1e30) for n, qk in zip(ns, qks)]
            mxs = [jnp.max(sc, axis=1, keepdims=True) for sc in scs]
            ps = [jnp.exp(sc - mx) for sc, mx in zip(scs, mxs)]
            ls = [jnp.sum(p, axis=1, keepdims=True) for p in ps]
            o4s = [_nn(_bf(p), vb) / l for p, vb, l in zip(ps, vbs, ls)]
            for r0, o4, mx, l in zip(r0s, o4s, mxs, ls):
                lse4 = mx + jnp.log(l)
                o_ref[pl.ds(r0, 2 * CHUNK), :] = jnp.where(lane < 64, o4[:2 * CHUNK], o4[2 * CHUNK:])
                lse_ref[pl.ds(r0, 2 * CHUNK), :] = jnp.where(lane < 64, lse4[:2 * CHUNK], lse4[2 * CHUNK:])
            return carry

        lax.fori_loop(0, ATT_CB // 4, two_pairs, 0)

    return pl.pallas_call(
        body, name="attn_fwd", grid=(b, 4, nstep),
        in_specs=[pl.BlockSpec((rows, 128), lambda bb, m, n: (bb * nstep + n, qb + m)),
                  pl.BlockSpec((s, 128), lambda bb, m, n: (bb, kb_ + m)),
                  pl.BlockSpec((s, 128), lambda bb, m, n: (bb, vb_ + m)),
                  pl.BlockSpec((None, 4 * CHUNK, WIN), lambda bb, m, n: (m, 0, 0))],
        out_specs=[pl.BlockSpec((rows, 128), lambda bb, m, n: (bb * nstep + n, m)),
                   pl.BlockSpec((rows, 128), lambda bb, m, n: (bb * nstep + n, m))],
        out_shape=[jax.ShapeDtypeStruct((t, A_WIDTH), F32), jax.ShapeDtypeStruct((t, A_WIDTH), F32)],
        scratch_shapes=[pltpu.VMEM((s + A_LOOKBACK * CHUNK, 128), BF16),
                        pltpu.VMEM((s + A_LOOKBACK * CHUNK, 128), BF16)],
        compiler_params=_cp(("parallel", "parallel", "arbitrary"), VMEM_LIMIT),
    )(proj, proj, proj, bias_band)


def attn_bwd(proj, bias_band, y_a, lse, dy_a, b, s):
    t = b * s
    nc = s // CHUNK
    qb, kb_, vb_ = P_QA // 128, P_KA // 128, P_VA // 128
    pad = A_LOOKBACK * CHUNK
    nstep = nc // ATT_CB
    rows = ATT_CB * CHUNK

    def body(q_ref, k_ref, v_ref, b_ref, do_ref, o_ref, lse_ref,
             dq_ref, dk_ref, dv_ref, dbt_ref, dbf_ref, kp, vp, dkp, dvp):
        bb = pl.program_id(1)
        n0 = pl.program_id(2) * ATT_CB

        @pl.when(n0 == 0)
        def _():
            _fill_band_pads(k_ref, v_ref, kp, vp, s)
            dkp[...] = jnp.zeros_like(dkp)
            dvp[...] = jnp.zeros_like(dvp)

        @pl.when((n0 == 0) & (bb == 0))
        def _():
            dbt_ref[...] = jnp.zeros_like(dbt_ref)
            dbf_ref[...] = jnp.zeros_like(dbf_ref)

        lane = lax.broadcasted_iota(jnp.int32, (2 * CHUNK, 128), 1)
        col = lax.broadcasted_iota(jnp.int32, (4 * CHUNK, WIN), 1)
        bias4 = b_ref[...]

        def two_pairs(i, carry):
            pps = (2 * i, 2 * i + 1)
            two = range(2)
            ns = [n0 + 2 * pp for pp in pps]
            r0s = [pl.multiple_of(pp * 2 * CHUNK, 2 * CHUNK) for pp in pps]
            starts = [pl.multiple_of(n * CHUNK, CHUNK) for n in ns]
            kbs = [kp[pl.ds(st_, WIN), :] for st_ in starts]
            vbs = [vp[pl.ds(st_, WIN), :] for st_ in starts]
            q4bs = [_bf(_stack_heads(q_ref[pl.ds(r0, 2 * CHUNK), :] * (A_HEAD_DIM ** -0.5), lane)) for r0 in r0s]
            do4s = [_stack_heads(do_ref[pl.ds(r0, 2 * CHUNK), :], lane) for r0 in r0s]
            do4bs = [_bf(d) for d in do4s]
            os_ = [o_ref[pl.ds(r0, 2 * CHUNK), :] for r0 in r0s]
            lsevs = [lse_ref[pl.ds(r0, 2 * CHUNK), :] for r0 in r0s]
            lse4s = [jnp.concatenate([v_[:, 0:1], v_[:, 64:65]], axis=0) for v_ in lsevs]
            qks = [_nt(q4bs[j], kbs[j]) for j in two]
            dps = [_nt(do4bs[j], vbs[j]) for j in two]
            deltas = [jnp.sum(do4s[j] * jnp.concatenate([os_[j], os_[j]], axis=0), axis=1, keepdims=True) for j in two]
            ps = [jnp.exp(jnp.where(col >= (A_LOOKBACK - ns[j]) * CHUNK, qks[j] + bias4, -1e30) - lse4s[j])
                  for j in two]
            pbs = [_bf(p) for p in ps]
            dss = [ps[j] * (dps[j] - deltas[j]) for j in two]
            dsbs = [_bf(d) for d in dss]
            dv_ws = [_tn(pbs[j], do4bs[j]) for j in two]
            dq4s = [_nn(dsbs[j], kbs[j]) for j in two]
            dk_ws = [_tn(dsbs[j], q4bs[j]) for j in two]
            for j in two:
                dq_ref[pl.ds(r0s[j], 2 * CHUNK), :] = _bf(
                    jnp.where(lane < 64, dq4s[j][:2 * CHUNK], dq4s[j][2 * CHUNK:]) * (A_HEAD_DIM ** -0.5))
                dkp[pl.ds(starts[j], WIN), :] += dk_ws[j]
                dvp[pl.ds(starts[j], WIN), :] += dv_ws[j]
                dbt_ref[...] += dss[j][:, WIN - 256:]
                dbf_ref[...] += dss[j][:, 0:128] + dss[j][:, 128:256] + dss[j][:, 256:384]
            return carry

        lax.fori_loop(0, ATT_CB // 4, two_pairs, 0)

        @pl.when(n0 == nc - ATT_CB)
        def _():
            dk_ref[...] = _bf(dkp[pl.ds(pad, s), :])
            dv_ref[...] = _bf(dvp[pl.ds(pad, s), :])

    return pl.pallas_call(
        body, name="attn_bwd", grid=(4, b, nstep),
        in_specs=[pl.BlockSpec((rows, 128), lambda m, bb, n: (bb * nstep + n, qb + m)),
                  pl.BlockSpec((s, 128), lambda m, bb, n: (bb, kb_ + m)),
                  pl.BlockSpec((s, 128), lambda m, bb, n: (bb, vb_ + m)),
                  pl.BlockSpec((None, 4 * CHUNK, WIN), lambda m, bb, n: (m, 0, 0)),
                  pl.BlockSpec((rows, 128), lambda m, bb, n: (bb * nstep + n, m)),
                  pl.BlockSpec((rows, 128), lambda m, bb, n: (bb * nstep + n, m)),
                  pl.BlockSpec((rows, 128), lambda m, bb, n: (bb * nstep + n, m))],
        out_specs=[pl.BlockSpec((rows, 128), lambda m, bb, n: (bb * nstep + n, m)),
                   pl.BlockSpec((s, 128), lambda m, bb, n: (bb, m)),
                   pl.BlockSpec((s, 128), lambda m, bb, n: (bb, m)),
                   pl.BlockSpec((None, 4 * CHUNK, 256), lambda m, bb, n: (m, 0, 0)),
                   pl.BlockSpec((None, 4 * CHUNK, 128), lambda m, bb, n: (m, 0, 0))],
        out_shape=[jax.ShapeDtypeStruct((t, A_WIDTH), BF16)] * 3
        + [jax.ShapeDtypeStruct((4, 4 * CHUNK, 256), F32),
           jax.ShapeDtypeStruct((4, 4 * CHUNK, 128), F32)],
        scratch_shapes=[pltpu.VMEM((s + pad, 128), BF16), pltpu.VMEM((s + pad, 128), BF16),
                        pltpu.VMEM((s + pad, 128), F32), pltpu.VMEM((s + pad, 128), F32)],
        compiler_params=_cp(("parallel", "arbitrary", "arbitrary"), VMEM_LIMIT),
    )(proj, proj, proj, bias_band, dy_a, y_a, lse)


def _conv_taps(x, w, s):
    row = lax.broadcasted_iota(jnp.int32, x.shape, 0)
    shifted = [x] + [jnp.where(row >= i, pltpu.roll(x, i, 0), 0.0) for i in range(1, CONV_K)]
    acc = shifted[0] * w[CONV_K - 1:CONV_K, :]
    for i in range(1, CONV_K):
        acc = acc + shifted[i] * w[CONV_K - 1 - i:CONV_K - i, :]
    return acc, shifted


def conv_fwd(proj, conv_w8, b, s):
    cb = 512
    c0 = P_CONV // cb

    def body(x_ref, w_ref, o_ref):
        a, _ = _conv_taps(x_ref[...].astype(F32), w_ref[...], s)
        o_ref[...] = a * _sigmoid(a)

    return pl.pallas_call(
        body, name="conv_fwd", grid=(b, CONV_CH // cb),
        in_specs=[pl.BlockSpec((s, cb), lambda bb, j: (bb, c0 + j)),
                  pl.BlockSpec((8, cb), lambda bb, j: (0, j))],
        out_specs=pl.BlockSpec((s, cb), lambda bb, j: (bb, j)),
        out_shape=jax.ShapeDtypeStruct((b * s, CONV_CH), F32),
        compiler_params=_cp(("parallel", "parallel"), VMEM_LIMIT),
    )(proj, conv_w8)


def conv_bwd(proj, conv_w8, dc3, b, s):
    cb = 512
    c0 = P_CONV // cb

    def body(x_ref, w_ref, dc_ref, dx_ref, dw_ref):
        @pl.when(pl.program_id(1) == 0)
        def _():
            dw_ref[...] = jnp.zeros_like(dw_ref)

        w = w_ref[...]
        a, shifted = _conv_taps(x_ref[...].astype(F32), w, s)
        sg = _sigmoid(a)
        da = dc_ref[...] * (sg * (1.0 + a * (1.0 - sg)))
        row = lax.broadcasted_iota(jnp.int32, da.shape, 0)
        dx = da * w[CONV_K - 1:CONV_K, :]
        for i in range(1, CONV_K):
            dx = dx + jnp.where(row < s - i, pltpu.roll(da, s - i, 0), 0.0) * w[CONV_K - 1 - i:CONV_K - i, :]
        dx_ref[...] = _bf(dx)
        r8 =lax.broadcasted_iota(jnp.int32, (8, cb), 0)
        dw = jnp.zeros((8, cb), F32)
        for i in range(CONV_K):
            dw = dw + jnp.where(r8 == CONV_K - 1 - i, jnp.sum(da * shifted[i], axis=0, keepdims=True), 0.0)
        dw_ref[...] += dw

    return pl.pallas_call(
        body, name="conv_bwd", grid=(CONV_CH // cb, b),
        in_specs=[pl.BlockSpec((s, cb), lambda j, bb: (bb, c0 + j)),
                  pl.BlockSpec((8, cb), lambda j, bb: (0, j)),
                  pl.BlockSpec((None, s, cb), lambda j, bb: (j, bb, 0))],
        out_specs=[pl.BlockSpec((s, cb), lambda j, bb: (bb, j)),
                   pl.BlockSpec((8, cb), lambda j, bb: (0, j))],
        out_shape=[jax.ShapeDtypeStruct((b * s, CONV_CH), BF16), jax.ShapeDtypeStruct((8, CONV_CH), F32)],
        compiler_params=_cp(("parallel", "arbitrary"), VMEM_LIMIT),
    )(proj, conv_w8, dc3)


def _pick_lane(v, k):
    lane = lax.broadcasted_iota(jnp.int32, v.shape, 1)
    return jnp.sum(jnp.where(lane == k, v, 0.0), axis=1, keepdims=True)


def _chunk_masks(ncb):
    i = lax.broadcasted_iota(jnp.int32, (ncb, CHUNK, CHUNK), 1)
    j = lax.broadcasted_iota(jnp.int32, (ncb, CHUNK, CHUNK), 2)
    return i, j


def _col_of_row(rowvec, eye):
    return jnp.sum(jnp.where(eye, rowvec, 0.0), axis=2, keepdims=True)


def _dn_chunk_math(cq, ck, cv, bd, al_row, dtb_row, h, ncb, tm=None):
    r = ncb * CHUNK
    i, j = _chunk_masks(ncb)
    eye = i == j
    low = i >= j
    strict = i > j
    ones = jnp.ones((ncb, CHUNK, CHUNK), F32)

    braw = _pick_lane(bd, h)
    draw = _pick_lane(bd, B_HEADS + h)
    al = _pick_lane(al_row, h)
    dtb = _pick_lane(dtb_row, h)
    ea = jnp.exp(al)
    beta = _sigmoid(braw)
    sp_arg = draw + dtb
    g = -ea * _softplus(sp_arg)

    rq = lax.rsqrt(jnp.sum(cq * cq, axis=1, keepdims=True) + EPS)
    rk = lax.rsqrt(jnp.sum(ck * ck, axis=1, keepdims=True) + EPS)
    nq = cq * rq
    kn = ck * rk
    qn = nq * (B_DIM ** -0.5)

    def c3(a):
        return a.reshape(ncb, CHUNK, a.shape[-1])

    qn3, kn3, v3, beta3 = c3(qn), c3(kn), c3(cv), c3(beta)
    gb = jnp.broadcast_to(c3(g), (ncb, CHUNK, CHUNK))
    gc_b = _bnn_exact(low.astype(BF16), gb)
    gr_b = _bnn_exact(_bf(ones), jnp.where(eye, gc_b, 0.0))
    dm = jnp.where(low, jnp.exp(jnp.where(low, gc_b - gr_b, 0.0)), 0.0)
    gc = gc_b[:, :, 0:1]
    gl = gc_b[:, CHUNK - 1:CHUNK, 0:1]
    gam = jnp.exp(gc)
    egl = jnp.exp(gl)
    edec = jnp.exp(gl - gc)

    knb = _bf(kn3)
    kk = _bnt(knb, knb)
    kd = jnp.where(strict, kk * dm, 0.0)
    a = beta3 * kd
    sz = 1 if tm is None else CHUNK
    if tm is None:
        tm = eye.astype(F32)
    while sz < CHUNK:
        off = jnp.where(((i // (2 * sz)) == (j // (2 * sz))) & ((i // sz) != (j // sz)), a, 0.0)
        tmb = _bf(tm)
        tm = tm - _bnn(_bf(_bnn(tmb, _bf(off))), tmb)
        sz *= 2
    bv = beta3 * v3
    bk = (beta3 * gam) * kn3
    sol = _bnn3(_split(tm), jnp.concatenate([bv, bk], axis=2))
    u, wk = sol[:, :, :B_DIM], sol[:, :, B_DIM:]
    qk = _bnt(_bf(qn3), knb)
    p = jnp.where(low, qk * dm, 0.0)
    kdec = kn3 * edec
    qg = gam * qn3
    return dict(beta=beta3, g=c3(g), ea=ea, sp_arg=c3(sp_arg), rq=c3(rq), rk=c3(rk), nq=c3(nq),
                qn=qn3, kn=kn3, v=v3, gc=gc, gl=gl, gam=gam, egl=egl, edec=edec, dm=dm, kd=kd, a=a,
                tm=tm, u=u, wk=wk, qk=qk, p=p, kdec=kdec, qg=qg, eye=eye, low=low, strict=strict)


def dn_prep(c, proj, al_row, dtb_row, b, s, ncb=16):
    t = b * s
    r = ncb * CHUNK
    nblk = t // r
    bd_blk = 0

    def body(cq_ref, ck_ref, cv_ref, bd_ref, al_ref, dtb_ref, u_ref, wk_ref, qg_ref, kdec_ref, p_ref, egl_ref,
             tm_ref):
        h = pl.program_id(1)
        m = _dn_chunk_math(cq_ref[...], ck_ref[...], cv_ref[...], bd_ref[...].astype(F32), al_ref[...], dtb_ref[...], h, ncb)
        tm_ref[...] = m["tm"].reshape(r, CHUNK)
        u_ref[...] = m["u"].reshape(r, B_DIM)
        wk_ref[...] = _bf(m["wk"].reshape(r, B_DIM))
        qg_ref[...] = _bf(m["qg"].reshape(r, B_DIM))
        kdec_ref[...] = _bf(m["kdec"].reshape(r, B_DIM))
        p_ref[...] = m["p"].reshape(r, CHUNK)
        egl_ref[...] = jnp.broadcast_to(m["egl"], (ncb, 8, 128)).reshape(ncb * 8, 128)

    col = lambda k: pl.BlockSpec((r, 128), lambda i, h: (i, k * B_HEADS + h))
    out_col = pl.BlockSpec((r, 128), lambda i, h: (i, h))
    small = pl.BlockSpec((1, 128), lambda i, h: (0, 0))
    return pl.pallas_call(
        body, name="dn_prep", grid=(nblk, B_HEADS),
        in_specs=[col(0), col(1), col(2), pl.BlockSpec((r, 128), lambda i, h: (i, bd_blk)), small, small],
        out_specs=[out_col, out_col, out_col, out_col,
                   pl.BlockSpec((None, r, CHUNK), lambda i, h: (h, i, 0)),
                   pl.BlockSpec((None, ncb * 8, 128), lambda i, h: (h, i, 0)),
                   pl.BlockSpec((None, r, CHUNK), lambda i, h: (h, i, 0))],
        out_shape=[jax.ShapeDtypeStruct((t, B_WIDTH), F32)] + [jax.ShapeDtypeStruct((t, B_WIDTH), BF16)] * 3
        + [jax.ShapeDtypeStruct((B_HEADS, t, CHUNK), F32),
           jax.ShapeDtypeStruct((B_HEADS, t // 8, 128), F32),
           jax.ShapeDtypeStruct((B_HEADS, t, CHUNK), F32)],
        compiler_params=_cp(("parallel", "parallel"), VMEM_LIMIT),
    )(c, c, c, proj, al_row, dtb_row)


SCAN_CB = 2


def dn_scan_fwd(u, wk, qg, kdec, p, egl, b, s):
    t = b * s
    nc = s // CHUNK

    def body(u_ref, wk_ref, qg_ref, kdec_ref, p_ref, egl_ref, o_ref, ss_ref, st):
        @pl.when(pl.program_id(0) == 0)
        def _():
            st[...] = jnp.zeros_like(st)

        chains = [(bb, h) for bb in range(b) for h in range(B_HEADS)]
        sls = [slice(h * B_DIM, (h + 1) * B_DIM) for _, h in chains]
        states = [st[bb * B_HEADS + h] for bb, h in chains]
        for cc in range(SCAN_CB):
            rs = slice(cc * CHUNK, (cc + 1) * CHUNK)
            sbs = [_bf(sh) for sh in states]
            ws = [u_ref[bb, rs, sl] - _nt(wk_ref[bb, rs, sl], sb) for (bb, _), sl, sb in zip(chains, sls, sbs)]
            qs = [_nt(qg_ref[bb, rs, sl], sb) for (bb, _), sl, sb in zip(chains, sls, sbs)]
            wbs = [_bf(w) for w in ws]
            outs = [q + _nn(_bf(p_ref[h, bb, rs, :]), wb) for (bb, h), q, wb in zip(chains, qs, wbs)]
            new_states = [egl_ref[h, bb, cc * 8:cc * 8 + 1, :] * sh + _tn(wb, kdec_ref[bb, rs, sl])
                          for (bb, h), sl, sh, wb in zip(chains, sls, states, wbs)]
            for (bb, h), sh, o in zip(chains, states, outs):
                ss_ref[bb, cc, h] = sh
                o_ref[bb, rs, h * B_DIM:(h + 1) * B_DIM] = o
            states = new_states
        for (bb, h), sh in zip(chains, states):
            st[bb * B_HEADS + h] = sh

    r3 = lambda a: a.reshape(b, s, B_WIDTH)
    rows = SCAN_CB * CHUNK
    act = pl.BlockSpec((b, rows, B_WIDTH), lambda n: (0, n, 0))
    o, states = pl.pallas_call(
        body, name="dn_scan_fwd", grid=(nc // SCAN_CB,),
        in_specs=[act, act, act, act,
                  pl.BlockSpec((B_HEADS, b, rows, CHUNK), lambda n: (0, 0, n, 0)),
                  pl.BlockSpec((B_HEADS, b, SCAN_CB * 8, 128), lambda n: (0, 0, n, 0))],
        out_specs=[act, pl.BlockSpec((b, SCAN_CB, B_HEADS, B_DIM, B_DIM), lambda n: (0, n, 0, 0, 0))],
        out_shape=[jax.ShapeDtypeStruct((b, s, B_WIDTH), F32),
                   jax.ShapeDtypeStruct((b, nc, B_HEADS, B_DIM, B_DIM), F32)],
        scratch_shapes=[pltpu.VMEM((b * B_HEADS, B_DIM, B_DIM), F32)],
        compiler_params=_cp(("arbitrary",), VMEM_LIMIT),
    )(r3(u), r3(wk), r3(qg), r3(kdec), p.reshape(B_HEADS, b, s, CHUNK), egl.reshape(B_HEADS, b, s // 8, 128))
    return o.reshape(t, B_WIDTH), states


def dn_scan_bwd(u, wk, qg, kdec, p, egl, states, do, b, s):
    t = b * s
    nc = s // CHUNK

    def body(u_ref, wk_ref, qg_ref, kdec_ref, p_ref, egl_ref, ss_ref, do_ref,
             dw_ref, dwk_ref, dqg_ref, dkdec_ref, dp_ref, degl_ref, dst):
        @pl.when(pl.program_id(0) == 0)
        def _():
            dst[...] = jnp.zeros_like(dst)

        chains = [(bb, h) for bb in range(b) for h in range(B_HEADS)]
        dstates = [dst[bb * B_HEADS + h] for bb, h in chains]
        n8 = range(len(chains))
        sls = [slice(h * B_DIM, (h + 1) * B_DIM) for _, h in chains]
        for cc in reversed(range(SCAN_CB)):
            rs = slice(cc * CHUNK, (cc + 1) * CHUNK)
            shs = [ss_ref[bb, cc, h] for bb, h in chains]
            sbs = [_bf(sh) for sh in shs]
            dsbs = [_bf(dsp) for dsp in dstates]
            wkbs = [wk_ref[bb, rs, sl] for (bb, _), sl in zip(chains, sls)]
            dobs = [_bf(do_ref[bb, rs, sl]) for (bb, _), sl in zip(chains, sls)]
            t1 = [_nt(wkbs[i], sbs[i]) for i in n8]
            dwa = [_tn(_bf(p_ref[h, bb, rs, :]), dobs[i]) for i, (bb, h) in enumerate(chains)]
            dwb_ = [_nt(kdec_ref[bb, rs, sls[i]], dsbs[i]) for i, (bb, _) in enumerate(chains)]
            dqgs = [_nn(dobs[i], sbs[i]) for i in n8]
            dsq = [_tn(dobs[i], qg_ref[bb, rs, sls[i]]) for i, (bb, _) in enumerate(chains)]
            wbs = [_bf(u_ref[bb, rs, sls[i]] - t1[i]) for i, (bb, _) in enumerate(chains)]
            dws = [dwa[i] + dwb_[i] for i in n8]
            dwbs = [_bf(dw) for dw in dws]
            dwks = [-_nn(dwbs[i], sbs[i]) for i in n8]
            dkdecs = [_nn(wbs[i], dsbs[i]) for i in n8]
            dpms = [_nt(dobs[i], wbs[i]) for i in n8]
            dsw = [_tn(dwbs[i], wkbs[i]) for i in n8]
            tots = [jnp.sum(jnp.sum(shs[i] * dstates[i], axis=1, keepdims=True), axis=0, keepdims=True) for i in n8]
            new_dss = [egl_ref[h, bb, cc * 8:cc * 8 + 1, :] * dstates[i] + dsq[i] - dsw[i]
                       for i, (bb, h) in enumerate(chains)]
            for i, (bb, h) in enumerate(chains):
                dw_ref[bb, rs, sls[i]] = dws[i]
                dqg_ref[bb, rs, sls[i]] = dqgs[i]
                dwk_ref[bb, rs, sls[i]] = dwks[i]
                dkdec_ref[bb, rs, sls[i]] = dkdecs[i]
                dp_ref[h, bb, rs, :] = dpms[i]
                degl_ref[h, bb, cc * 8:(cc + 1) * 8, :] = jnp.broadcast_to(tots[i], (8, 128))
            dstates = new_dss
        for (bb, h), dsp in zip(chains, dstates):
            dst[bb * B_HEADS + h] = dsp

    r3 = lambda a: a.reshape(b, s, B_WIDTH)
    rows = SCAN_CB * CHUNK
    last = nc // SCAN_CB - 1
    act = pl.BlockSpec((b, rows, B_WIDTH), lambda n: (0, last - n, 0))
    pspec = pl.BlockSpec((B_HEADS, b, rows, CHUNK), lambda n: (0, 0, last - n, 0))
    espec = pl.BlockSpec((B_HEADS, b, SCAN_CB * 8, 128), lambda n: (0, 0, last - n, 0))
    outs = pl.pallas_call(
        body, name="dn_scan_bwd", grid=(nc // SCAN_CB,),
        in_specs=[act, act, act, act, pspec, espec,
                  pl.BlockSpec((b, SCAN_CB, B_HEADS, B_DIM, B_DIM), lambda n: (0, last - n, 0, 0, 0)),
                  act],
        out_specs=[act, act, act, act, pspec, espec],
        out_shape=[jax.ShapeDtypeStruct((b, s, B_WIDTH), F32)] * 4
        + [jax.ShapeDtypeStruct((B_HEADS, b, s, CHUNK), F32),
           jax.ShapeDtypeStruct((B_HEADS, b, s // 8, 128), F32)],
        scratch_shapes=[pltpu.VMEM((b * B_HEADS, B_DIM, B_DIM), F32)],
        compiler_params=_cp(("arbitrary",), VMEM_LIMIT),
    )(r3(u), r3(wk), r3(qg), r3(kdec), p.reshape(B_HEADS, b, s, CHUNK), egl.reshape(B_HEADS, b, s // 8, 128),
      states, r3(do))
    return (*[a.reshape(t, B_WIDTH) for a in outs[:4]], outs[4].reshape(B_HEADS, t, CHUNK),
            outs[5].reshape(B_HEADS, t // 8, 128))


def dn_post_bwd(c, proj, al_row, dtb_row, tmat, dw, dwk, dqg, dkdec, dp, degl, b, s, ncb=16):
    t = b * s
    r = ncb * CHUNK
    nblk = t // r
    bd_blk = 0

    def body(cq_ref, ck_ref, cv_ref, bd_ref, al_ref, dtb_ref, tm_ref, dw_ref, dwk_ref, dqg_ref, dkdec_ref, dp_ref,
             degl_ref, dc_ref, dbd_ref, dal_ref, ddtb_ref):
        h = pl.program_id(1)

        @pl.when((pl.program_id(0) == 0) & (h == 0))
        def _():
            dal_ref[...] = jnp.zeros_like(dal_ref)
            ddtb_ref[...] = jnp.zeros_like(ddtb_ref)

        m = _dn_chunk_math(cq_ref[...], ck_ref[...], cv_ref[...], bd_ref[...].astype(F32), al_ref[...], dtb_ref[...], h, ncb,
                           tm=tm_ref[...].reshape(ncb, CHUNK, CHUNK))
        eye, low, strict = m["eye"], m["low"], m["strict"]
        eyef = eye.astype(F32)

        def c3(a):
            return a.reshape(ncb, CHUNK, a.shape[-1])

        du, dwkv, dqg, dkdec = c3(dw_ref[...]), c3(dwk_ref[...]), c3(dqg_ref[...]), c3(dkdec_ref[...])
        dpm = jnp.where(low, c3(dp_ref[...]), 0.0)
        degl = degl_ref[...].reshape(ncb, 8, 128)[:, 0:1, 0:1]
        beta, gam, kn, qn, v = m["beta"], m["gam"], m["kn"], m["qn"], m["v"]
        dm, kd, a, p = m["dm"], m["kd"], m["a"], m["p"]
        knb, qnb = _bf(kn), _bf(qn)

        eyeb = _bf(eyef)
        th, tl = _split(m["tm"])
        tts = (_bf(_bnt(eyeb, th)), _bf(_bnt(eyeb, tl)))
        xy = _bnn3(tts, jnp.concatenate([du, dwkv], axis=2))
        x, y = xy[:, :, :B_DIM], xy[:, :, B_DIM:]
        da = -jnp.where(strict, _bnt(_bf(x), _bf(m["u"])) + _bnt(_bf(y), _bf(m["wk"])), 0.0)
        dv = beta * x
        sy = jnp.sum(y * kn, axis=2, keepdims=True)
        dbeta = jnp.sum(x * v, axis=2, keepdims=True) + gam * sy + jnp.sum(da * kd, axis=2, keepdims=True)
        dgam = beta * sy + jnp.sum(dqg * qn, axis=2, keepdims=True)
        dkk = da * beta * dm
        dqk = dpm * dm
        dkkb, dqkb = _bf(dkk), _bf(dqk)
        dkn = ((beta * gam) * y + _bnn(dkkb, knb) + _bnn(_bf(_bnt(eyeb, dkkb)), knb)
               + _bnn(_bf(_bnt(eyeb, dqkb)), qnb) + dkdec * m["edec"])
        dqn = gam * dqg + _bnn(dqkb, knb)
        mm = da * a + dpm * p
        ek = jnp.sum(dkdec * m["kdec"], axis=2, keepdims=True)
        dgc = (jnp.sum(mm, axis=2, keepdims=True) - _col_of_row(jnp.sum(mm, axis=1, keepdims=True), eye)
               + dgam * gam - ek)
        dgl = jnp.sum(ek, axis=1, keepdims=True) + degl * m["egl"]
        i, _ = _chunk_masks(ncb)
        dgc = dgc + jnp.where(i[:, :, 0:1] == CHUNK - 1, dgl, 0.0)
        upper = (i <= _chunk_masks(ncb)[1]).astype(BF16)
        dg = _bnn_exact(upper, jnp.broadcast_to(dgc, (ncb, CHUNK, CHUNK)))[:, :, 0:1]

        nq = m["nq"]
        dnq = dqn * (B_DIM ** -0.5)
        dcq = m["rq"] * (dnq - nq * jnp.sum(nq * dnq, axis=2, keepdims=True))
        dck = m["rk"] * (dkn - kn * jnp.sum(kn * dkn, axis=2, keepdims=True))
        dc_ref[0] = dcq.reshape(r, B_DIM)
        dc_ref[1] = dck.reshape(r, B_DIM)
        dc_ref[2] = dv.reshape(r, B_DIM)

        dbraw = (dbeta * beta * (1.0 - beta)).reshape(r, 1)
        sgm = _sigmoid(m["sp_arg"])
        ddraw3 = dg * (-m["ea"]) * sgm
        ddraw = ddraw3.reshape(r, 1)
        lane = lax.broadcasted_iota(jnp.int32, (r, 128), 1)
        contrib = jnp.where(lane == h, dbraw, 0.0) + jnp.where(lane == B_HEADS + h, ddraw, 0.0)

        @pl.when(h == 0)
        def _():
            dbd_ref[...] = contrib

        @pl.when(h != 0)
        def _():
            dbd_ref[...] += contrib

        lane8 = lax.broadcasted_iota(jnp.int32, (8, 128), 1)
        tot_al = jnp.sum(jnp.sum(dg * m["g"], axis=1, keepdims=True), axis=0, keepdims=True).reshape(1, 1)
        tot_dtb = jnp.sum(jnp.sum(ddraw3, axis=1, keepdims=True), axis=0, keepdims=True).reshape(1, 1)
        dal_ref[...] += jnp.where(lane8 == h, tot_al, 0.0)
        ddtb_ref[...] += jnp.where(lane8 == h, tot_dtb, 0.0)

    col = lambda k: pl.BlockSpec((r, 128), lambda i, h: (i, k * B_HEADS + h))
    hcol = pl.BlockSpec((r, 128), lambda i, h: (i, h))
    small = pl.BlockSpec((1, 128), lambda i, h: (0, 0))
    acc = pl.BlockSpec((8, 128), lambda i, h: (0, 0))
    return pl.pallas_call(
        body, name="dn_post_bwd", grid=(nblk, B_HEADS),
        in_specs=[col(0), col(1), col(2), pl.BlockSpec((r, 128), lambda i, h: (i, bd_blk)), small, small,
                  pl.BlockSpec((None, r, CHUNK), lambda i, h: (h, i, 0)),
                  hcol, hcol, hcol, hcol,
                  pl.BlockSpec((None, r, CHUNK), lambda i, h: (h, i, 0)),
                  pl.BlockSpec((None, ncb * 8, 128), lambda i, h: (h, i, 0))],
        out_specs=[pl.BlockSpec((3, r, 128), lambda i, h: (0, i, h)),
                   pl.BlockSpec((r, 128), lambda i, h: (i, 0)), acc, acc],
        out_shape=[jax.ShapeDtypeStruct((3, t, B_WIDTH), F32), jax.ShapeDtypeStruct((t, 128), F32),
                   jax.ShapeDtypeStruct((8, 128), F32), jax.ShapeDtypeStruct((8, 128), F32)],
        compiler_params=_cp(("arbitrary", "arbitrary"), VMEM_LIMIT),
    )(c, c, c, proj, al_row, dtb_row, tmat, dw, dwk, dqg, dkdec, dp, degl)


def make_bias_band(rel_bias):
    tail = bias_tail(jnp.pad(rel_bias, ((0, 0), (0, 384 - N_REL))))
    far = jnp.broadcast_to(rel_bias[:, 2 * REL_CLIP][:, None, None], (A_HEADS, CHUNK, BAND - TAIL))
    band = jnp.concatenate([far, jnp.transpose(tail, (1, 0, 2))], axis=2)
    off = jnp.full((A_HEADS, CHUNK, CHUNK), -1e30, F32)
    both = jnp.stack([jnp.concatenate([band, off], axis=2), jnp.concatenate([off, band], axis=2)], axis=1)
    return both.reshape(4, 4 * CHUNK, WIN)


def bias_band_grad(dbt, dbf):
    t5 = dbt.reshape(A_HEADS, 2, CHUNK, 256)
    tail = t5[:, 0, :, :TAIL] + t5[:, 1, :, CHUNK:]
    far = dbf.reshape(A_HEADS, 2, CHUNK, 128).sum(axis=1) + jnp.pad(t5[:, 1, :, :CHUNK], ((0, 0), (0, 0), (0, CHUNK)))
    return bias_grad(jnp.transpose(tail, (1, 0, 2)), far)[:, :N_REL]


def _rms(x):
    r = lax.rsqrt(jnp.mean(x * x, axis=-1, keepdims=True) + EPS)
    return r, x * r


def _rms_bwd(dh, g, r, n):
    dn = dh * g
    return r * (dn - n * jnp.mean(dn * n, axis=-1, keepdims=True)), dh * n


def _gated_onorm(o, z, w_on):
    parts = []
    for h in range(B_HEADS):
        sl = slice(h * B_DIM, (h + 1) * B_DIM)
        r, n = _rms(o[:, sl])
        parts.append((r, n))
    r4 = [p[0] for p in parts]
    n4 = jnp.concatenate([p[1] for p in parts], axis=1)
    w4 = jnp.concatenate([w_on] * B_HEADS, axis=1)
    sz = _sigmoid(z)
    silu = z * sz
    return n4 * w4 * silu, r4, n4, w4, sz, silu


def mid_fwd(x, y_a, o_b, proj, w_on, wa, wb, w_out, tm=256):
    t = x.shape[0]
    tm = min(tm, t)

    def body(x_ref, ya_ref, ob_ref, z_ref, ga_ref, gb_ref, won_ref, wa_ref, wb_ref, wo_ref, x1_ref, mg_ref):
        yb = _gated_onorm(ob_ref[...], z_ref[...].astype(F32), won_ref[...])[0]
        ua = _nn(_bf(ya_ref[...]), wa_ref[...])
        ub = _nn(_bf(yb), wb_ref[...])
        merged = _sigmoid(ga_ref[...].astype(F32)) * ua + _sigmoid(gb_ref[...].astype(F32)) * ub
        mb = _bf(merged)
        mg_ref[...] = mb
        x1_ref[...] = x_ref[...] + _nn(mb, wo_ref[...])

    rowd = pl.BlockSpec((tm, D_MODEL), lambda i: (i, 0))
    row5 = pl.BlockSpec((tm, 512), lambda i: (i, 0))
    full = lambda a: pl.BlockSpec(a.shape, lambda i: (0,) * a.ndim)
    return pl.pallas_call(
        body, name="mid_fwd", grid=(t // tm,),
        in_specs=[rowd, row5, row5,
                  pl.BlockSpec((tm, 512), lambda i: (i, P_Z // 512)),
                  pl.BlockSpec((tm, D_MODEL), lambda i: (i, 0)),
                  pl.BlockSpec((tm, D_MODEL), lambda i: (i, 1)),
                  full(w_on), full(wa), full(wb), full(w_out)],
        out_specs=[rowd, rowd],
        out_shape=[jax.ShapeDtypeStruct((t, D_MODEL), F32), jax.ShapeDtypeStruct((t, D_MODEL), BF16)],
        compiler_params=_cp(("parallel",), VMEM_LIMIT),
    )(x, y_a, o_b, proj, proj, proj, w_on, wa, wb, w_out)


def mid_bwd(dx1, merged, y_a, o_b, proj, w_on, wa, wb, w_out, tm=256):
    t = dx1.shape[0]
    tm = min(tm, t)

    def body(dx1_ref, mg_ref, ya_ref, ob_ref, z_ref, ga_ref, gb_ref, won_ref, wa_ref, wb_ref, wo_ref,
             dya_ref, dob_ref, dz_ref, dg_ref, dwo_ref, dwa_ref, dwb_ref, dwon_ref):
        @pl.when(pl.program_id(0) == 0)
        def _():
            dwo_ref[...] = jnp.zeros_like(dwo_ref)
            dwa_ref[...] = jnp.zeros_like(dwa_ref)
            dwb_ref[...] = jnp.zeros_like(dwb_ref)
            dwon_ref[...] = jnp.zeros_like(dwon_ref)

        dx1b = _bf(dx1_ref[...])
        dmerged = _nt(dx1b, wo_ref[...])
        dwo_ref[...] += _tn(mg_ref[...], dx1b)
        o = ob_ref[...]
        z = z_ref[...].astype(F32)
        yb, r4, n4, w4, sz, silu = _gated_onorm(o, z, won_ref[...])
        yab, ybb = _bf(ya_ref[...]), _bf(yb)
        ua = _nn(yab, wa_ref[...])
        ub = _nn(ybb, wb_ref[...])
        sa, sb = _sigmoid(ga_ref[...].astype(F32)), _sigmoid(gb_ref[...].astype(F32))
        dua, dub = _bf(dmerged * sa), _bf(dmerged * sb)
        dg_ref[:, 0:D_MODEL] = _bf(dmerged * ua * sa * (1.0 - sa))
        dg_ref[:, D_MODEL:2 * D_MODEL] = _bf(dmerged * ub * sb * (1.0 - sb))
        dwa_ref[...] += _tn(yab, dua)
        dwb_ref[...] += _tn(ybb, dub)
        dya_ref[...] = _nt(dua, wa_ref[...])
        dyb = _nt(dub, wb_ref[...])
        dz_ref[...] = _bf(dyb * (n4 * w4) * (sz * (1.0 + z * (1.0 - sz))))
        dnw = dyb * silu
        dwon = jnp.zeros((1, B_DIM), F32)
        for h in range(B_HEADS):
            sl = slice(h * B_DIM, (h + 1) * B_DIM)
            dxh, dgh = _rms_bwd(dnw[:, sl], won_ref[...], r4[h], n4[:, sl])
            dob_ref[:, sl] = dxh
            dwon = dwon + jnp.sum(dgh, axis=0, keepdims=True)
        dwon_ref[...] += jnp.broadcast_to(dwon, (8, B_DIM))

    rowd = pl.BlockSpec((tm, D_MODEL), lambda i: (i, 0))
    row5 = pl.BlockSpec((tm, 512), lambda i: (i, 0))
    full = lambda a: pl.BlockSpec(a.shape, lambda i: (0,) * a.ndim)
    fixed = lambda shp: pl.BlockSpec(shp, lambda i: (0,) * len(shp))
    return pl.pallas_call(
        body, name="mid_bwd", grid=(t // tm,),
        in_specs=[rowd, rowd, row5, row5,
                  pl.BlockSpec((tm, 512), lambda i: (i, P_Z // 512)),
                  pl.BlockSpec((tm, D_MODEL), lambda i: (i, 0)),
                  pl.BlockSpec((tm, D_MODEL), lambda i: (i, 1)),
                  full(w_on), full(wa), full(wb), full(w_out)],
        out_specs=[row5, row5, row5, pl.BlockSpec((tm, 2 * D_MODEL), lambda i: (i, 0)),
                   fixed((D_MODEL, D_MODEL)), fixed((A_WIDTH, D_MODEL)), fixed((B_WIDTH, D_MODEL)),
                   fixed((8, B_DIM))],
        out_shape=[jax.ShapeDtypeStruct((t, 512), F32), jax.ShapeDtypeStruct((t, 512), F32),
                   jax.ShapeDtypeStruct((t, 512), BF16), jax.ShapeDtypeStruct((t, 2 * D_MODEL), BF16),
           jax.ShapeDtypeStruct((D_MODEL, D_MODEL), F32), jax.ShapeDtypeStruct((A_WIDTH, D_MODEL), F32),
           jax.ShapeDtypeStruct((B_WIDTH, D_MODEL), F32), jax.ShapeDtypeStruct((8, B_DIM), F32)],
        compiler_params=_cp(("arbitrary",), VMEM_LIMIT),
    )(dx1, merged, y_a, o_b, proj, proj, proj, w_on, wa, wb, w_out)


FFN_TF = 1408


def ffn_up(x1, g, w_gu, tm=512, tf=FFN_TF):
    t = x1.shape[0]
    tm = min(tm, t)
    nf = D_FF // tf

    def body(x_ref, g_ref, wg_ref, wu_ref, gate_ref, up_ref, act_ref, h_ref):
        @pl.when(pl.program_id(1) == 0)
        def _():
            r, n = _rms(x_ref[...])
            h_ref[...] = _bf(n * g_ref[...])

        hb = h_ref[...]
        gate = _nn(hb, wg_ref[...])
        up = _nn(hb, wu_ref[...])
        gate_ref[...] = _bf(gate)
        up_ref[...] = _bf(up)
        act_ref[...] = _bf(gate * _sigmoid(gate) * up)

    ff = pl.BlockSpec((tm, tf), lambda i, j: (i, j))
    return pl.pallas_call(
        body, name="ffn_up", grid=(t // tm, nf),
        in_specs=[pl.BlockSpec((tm, D_MODEL), lambda i, j: (i, 0)),
                  pl.BlockSpec((1, D_MODEL), lambda i, j: (0, 0)),
                  pl.BlockSpec((D_MODEL, tf), lambda i, j: (0, j)),
                  pl.BlockSpec((D_MODEL, tf), lambda i, j: (0, nf + j))],
        out_specs=[ff, ff, ff, pl.BlockSpec((tm, D_MODEL), lambda i, j: (i, 0))],
        out_shape=[jax.ShapeDtypeStruct((t, D_FF), BF16)] * 3 + [jax.ShapeDtypeStruct((t, D_MODEL), BF16)],
        compiler_params=_cp(("parallel", "arbitrary"), VMEM_LIMIT),
    )(x1, g, w_gu, w_gu)


def matmul_residual(a, w, res, name, tm=512, tk=FFN_TF):
    t, k = a.shape
    n = w.shape[1]
    tm = min(tm, t)

    def body(a_ref, w_ref, r_ref, o_ref):
        @pl.when(pl.program_id(1) == 0)
        def _():
            o_ref[...] = r_ref[...]

        o_ref[...] += _nn(a_ref[...], w_ref[...])

    return pl.pallas_call(
        body, name=name, grid=(t // tm, k // tk),
        in_specs=[pl.BlockSpec((tm, tk), lambda i, j: (i, j)),
                  pl.BlockSpec((tk, n), lambda i, j: (j, 0)),
                  pl.BlockSpec((tm, n), lambda i, j: (i, 0))],
        out_specs=pl.BlockSpec((tm, n), lambda i, j: (i, 0)),
        out_shape=jax.ShapeDtypeStruct((t, n), F32),
        compiler_params=_cp(("parallel", "arbitrary"), VMEM_LIMIT),
    )(a, w, res)


def ffn_act_bwd(dx2, gate, up, w_down, tm=512, tf=FFN_TF):
    t = dx2.shape[0]
    tm = min(tm, t)

    def body(dx2_ref, gate_ref, up_ref, wd_ref, dgate_ref, dup_ref, dx2b_ref):
        @pl.when(pl.program_id(1) == 0)
        def _():
            dx2b_ref[...] = _bf(dx2_ref[...])

        dact = _nt(dx2b_ref[...], wd_ref[...])
        gt, upv = gate_ref[...].astype(F32), up_ref[...].astype(F32)
        sg = _sigmoid(gt)
        t = dact * sg
        dgate_ref[...] = _bf(t * upv * (1.0 + gt * (1.0 - sg)))
        dup_ref[...] = _bf(t * gt)

    ff = pl.BlockSpec((tm, tf), lambda i, j: (i, j))
    return pl.pallas_call(
        body, name="ffn_act_bwd", grid=(t // tm, D_FF // tf),
        in_specs=[pl.BlockSpec((tm, D_MODEL), lambda i, j: (i, 0)), ff, ff,
                  pl.BlockSpec((tf, D_MODEL), lambda i, j: (j, 0))],
        out_specs=[ff, ff],
        out_shape=[jax.ShapeDtypeStruct((t, D_FF), BF16)] * 2,
        scratch_shapes=[pltpu.VMEM((tm, D_MODEL), BF16)],
        compiler_params=_cp(("parallel", "arbitrary"), VMEM_LIMIT),
    )(dx2, gate, up, w_down)


def tail_fwd_bwd(x2, p, target, g_ple, g_final, w_pg, w_pp, tm=256):
    t = x2.shape[0]
    tm = min(tm, t)

    def body(x_ref, p_ref, t_ref, gp_ref, gf_ref, wpg_ref, wpp_ref,
             dx_ref, dwpg_ref, dwpp_ref, dgp_ref, dgf_ref, loss_ref):
        @pl.when(pl.program_id(0) == 0)
        def _():
            dwpg_ref[...] = jnp.zeros_like(dwpg_ref)
            dwpp_ref[...] = jnp.zeros_like(dwpp_ref)
            dgp_ref[...] = jnp.zeros_like(dgp_ref)
            dgf_ref[...] = jnp.zeros_like(dgf_ref)
            loss_ref[...] = jnp.zeros_like(loss_ref)

        x2v = x_ref[...]
        gp, gf = gp_ref[...], gf_ref[...]
        r3, n3 = _rms(x2v)
        h3b = _bf(n3 * gp)
        pb = _bf(p_ref[...])
        pg = _sigmoid(_nn(h3b, wpg_ref[...]))
        pp = _nn(pb, wpp_ref[...])
        x3 = x2v + pg * pp
        r4, n4 = _rms(x3)
        err = n4 * gf - t_ref[...]
        part = 0.5 * jnp.sum(jnp.sum(err * err, axis=1, keepdims=True), axis=0, keepdims=True) / D_MODEL
        loss_ref[...] += jnp.broadcast_to(part, (8, 128))
        dy = err * (1.0 / D_MODEL)
        dx3, dgf = _rms_bwd(dy, gf, r4, n4)
        dgf_ref[...] += jnp.broadcast_to(jnp.sum(dgf, axis=0, keepdims=True), (8, D_MODEL))
        dzp = _bf(dx3 * pp * pg * (1.0 - pg))
        dpp = _bf(dx3 * pg)
        dwpg_ref[...] += _tn(h3b, dzp)
        dwpp_ref[...] += _tn(pb, dpp)
        dh3 = _nt(dzp, wpg_ref[...])
        dx, dgp = _rms_bwd(dh3, gp, r3, n3)
        dgp_ref[...] += jnp.broadcast_to(jnp.sum(dgp, axis=0, keepdims=True), (8, D_MODEL))
        dx_ref[...] = dx3 + dx

    rowd = pl.BlockSpec((tm, D_MODEL), lambda i: (i, 0))
    fixed = lambda shp: pl.BlockSpec(shp, lambda i: (0,) * len(shp))
    return pl.pallas_call(
        body, name="tail_fwd_bwd", grid=(t // tm,),
        in_specs=[rowd, pl.BlockSpec((tm, PLE_DIM), lambda i: (i, 0)), rowd,
                  fixed((1, D_MODEL)), fixed((1, D_MODEL)), fixed((D_MODEL, D_MODEL)), fixed((PLE_DIM, D_MODEL))],
        out_specs=[rowd, fixed((D_MODEL, D_MODEL)), fixed((PLE_DIM, D_MODEL)),
                   fixed((8, D_MODEL)), fixed((8, D_MODEL)), fixed((8, 128))],
        out_shape=[jax.ShapeDtypeStruct((t, D_MODEL), F32), jax.ShapeDtypeStruct((D_MODEL, D_MODEL), F32),
                   jax.ShapeDtypeStruct((PLE_DIM, D_MODEL), F32), jax.ShapeDtypeStruct((8, D_MODEL), F32),
                   jax.ShapeDtypeStruct((8, D_MODEL), F32), jax.ShapeDtypeStruct((8, 128), F32)],
        compiler_params=_cp(("arbitrary",), VMEM_LIMIT),
    )(x2, p, target, g_ple, g_final, w_pg, w_pp)


def in_proj_bwd(pieces, weights, x, dx1, g, name="in_proj_bwd", tm=256):
    t = x.shape[0]
    tm = min(tm, t)
    k = len(pieces)
    assert all(c0 % wd == 0 and w0 % wd == 0 for (_, c0, wd), (_, w0) in zip(pieces, weights))

    def body(*refs):
        p_refs, w_refs = refs[:k], refs[k:2 * k]
        x_ref, dx1_ref, g_ref, dx_ref, dg_ref = refs[2 * k:]

        @pl.when(pl.program_id(0) == 0)
        def _():
            dg_ref[...] = jnp.zeros_like(dg_ref)

        dh = _nt(_bf(p_refs[0][...]), w_refs[0][...])
        for pr, wr in zip(p_refs[1:], w_refs[1:]):
            dh = dh + _nt(_bf(pr[...]), wr[...])
        r, n = _rms(x_ref[...])
        dx, dgc = _rms_bwd(dh, g_ref[...], r, n)
        dx_ref[...] = dx1_ref[...] + dx
        dg_ref[...] += jnp.broadcast_to(jnp.sum(dgc, axis=0, keepdims=True), (8, D_MODEL))

    rowd = pl.BlockSpec((tm, D_MODEL), lambda i: (i, 0))
    return pl.pallas_call(
        body, name=name, grid=(t // tm,),
        in_specs=[pl.BlockSpec((tm, wd), functools.partial(lambda i, cb: (i, cb), cb=c0 // wd))
                  for _, c0, wd in pieces]
        + [pl.BlockSpec((w.shape[0], wd), functools.partial(lambda i, cb: (0, cb), cb=w0 // wd))
           for (w, w0), (_, _, wd) in zip(weights, pieces)]
        + [rowd, rowd, pl.BlockSpec((1, D_MODEL), lambda i: (0, 0))],
        out_specs=[rowd, pl.BlockSpec((8, D_MODEL), lambda i: (0, 0))],
        out_shape=[jax.ShapeDtypeStruct((t, D_MODEL), F32), jax.ShapeDtypeStruct((8, D_MODEL), F32)],
        compiler_params=_cp(("arbitrary",), VMEM_LIMIT),
    )(*[a for a, _, _ in pieces], *[w for w, _ in weights], x, dx1, g)


def adamw(w, g, m, v, name, rows_cap=256, dep=None):
    lead = w.shape[:-2]
    r, c = w.shape[-2:]
    tr = r
    for cand in range(8, min(r, rows_cap) + 1, 8):
        if r % cand == 0:
            tr = cand

    def body(w_ref, g_ref, m_ref, v_ref, *rest):
        d_ref, mo_ref, vo_ref = rest[-3:]
        gv = g_ref[...]
        mn = ADAM_B1 * m_ref[...] + (1.0 - ADAM_B1) * gv
        vn = ADAM_B2 * v_ref[...] + (1.0 - ADAM_B2) * (gv * gv)
        m_hat = mn / (1.0 - ADAM_B1 ** ADAM_STEP)
        v_hat = vn / (1.0 - ADAM_B2 ** ADAM_STEP)
        d_ref[...] = -ADAM_LR * (m_hat / (jnp.sqrt(v_hat) + ADAM_EPS) + ADAM_WD * w_ref[...])
        mo_ref[...] = mn
        vo_ref[...] = vn

    spec = pl.BlockSpec((None,) * len(lead) + (tr, c), lambda i: (0,) * len(lead) + (i, 0))
    extra = [] if dep is None else [dep]
    return pl.pallas_call(
        body, name=name, grid=(r // tr,),
        in_specs=[spec] * 4 + [pl.BlockSpec((8, 128), lambda i: (0, 0))] * len(extra), out_specs=[spec] * 3,
        out_shape=[jax.ShapeDtypeStruct(w.shape, F32)] * 3,
        compiler_params=_cp(("parallel",), VMEM_LIMIT),
    )(w, g.reshape(w.shape), m, v, *extra)


def _w_in_shards(dwp):
    cs = D_IN // N_CHIPS
    regions = ((0, SPLIT_Z, P_QA), (SPLIT_Z, SPLIT_Z + 8, P_BD - SPLIT_Z), (SPLIT_Z + 8, D_IN, -(SPLIT_Z + 8)))

    def original(lo, hi):
        parts = [dwp[:, max(lo, a) + off:min(hi, e) + off] for a, e, off in regions if max(lo, a) < min(hi, e)]
        return parts[0] if len(parts) == 1 else jnp.concatenate(parts, axis=1)

    return jnp.stack([original(s * cs, (s + 1) * cs) for s in range(N_CHIPS)])


class Standalone:
    def __init__(self, later_weights):
        self.later_weights = later_weights

    def begin(self, *a):
        return 0.0

    forward = exchange = join = begin

    def finish(self, after):
        return self.later_weights


def local_step(x3d, p3d, target3d, g4, small, later, early):
    b, s, _ = x3d.shape
    t = b * s
    x = x3d.reshape(t, D_MODEL)
    p = p3d.reshape(t, PLE_DIM)
    target = target3d.reshape(t, D_MODEL)
    cut = SPLIT_Z - 2 * (D_IN // N_CHIPS)
    w_inp = jnp.concatenate([g4[2][:, cut + 8:], g4[3], g4[0], g4[1], g4[2][:, :cut], g4[2][:, cut:cut + 8],
                             jnp.zeros((D_MODEL, 120), BF16)], axis=1)
    al_row = jnp.pad(small["a_log"].reshape(1, B_HEADS), ((0, 0), (0, 128 - B_HEADS)))
    dtb_row = jnp.pad(small["dt_bias"].reshape(1, B_HEADS), ((0, 0), (0, 128 - B_HEADS)))
    conv_w8 = jnp.pad(small["conv_w"].reshape(CONV_K, CONV_CH), ((0, 8 - CONV_K), (0, 0)))
    w_on = small["w_onorm"].reshape(1, B_DIM)
    g_mix, g_ffn = small["g_mix"].reshape(1, D_MODEL), small["g_ffn"].reshape(1, D_MODEL)
    g_ple, g_final = small["g_ple"].reshape(1, D_MODEL), small["g_final"].reshape(1, D_MODEL)
    bias_band = make_bias_band(small["rel_bias"].reshape(A_HEADS, N_REL))

    tok = later.begin()
    proj, h1, bd32 = rms_matmul(x, g_mix + tok, w_inp, "in_proj", tm=1024, tn_cap=1152)
    y_a, lse = attn_fwd(proj, bias_band, b, s)
    tok = later.forward(lse)
    c = conv_fwd(proj, conv_w8 + tok, b, s)
    u, wk, qg, kdec, pm, egl, tmat = dn_prep(c, bd32, al_row, dtb_row, b, s)
    o_b, states = dn_scan_fwd(u, wk, qg, kdec, pm, egl, b, s)
    wts = later.finish(o_b)
    x1, merged = mid_fwd(x, y_a, o_b, proj, w_on, wts["w_branch_a"], wts["w_branch_b"], wts["w_out"])
    gate, up, act, h2 = ffn_up(x1, g_ffn, wts["w_gate_up"])
    x2 = matmul_residual(act, wts["w_down"], x1, "ffn_down")

    dx2, dw_pg, dw_pp, dg_ple, dg_final, loss = tail_fwd_bwd(
        x2, p, target, g_ple, g_final, wts["w_ple_gate"], wts["w_ple_proj"])
    dgate, dup = ffn_act_bwd(dx2, gate, up, wts["w_down"])
    w_gu = wts["w_gate_up"]
    dx1, dg_ffn = in_proj_bwd([(dgate, 0, D_FF), (dup, 0, D_FF)], [(w_gu, 0), (w_gu, D_FF)], x1, dx2, g_ffn,
                              name="ffn_in_bwd")
    dw_down = matmul_tn(act, dx2, "dw_down")
    dw_gu = matmul_tn(h2, dgate, "dw_gate", width=2 * D_FF, tiles_major=True)
    dw_gu = matmul_tn(h2, dup, "dw_up", into=dw_gu, col0=D_FF, width=2 * D_FF, tiles_major=True)
    dy_a, do_b, dz, dgates, dw_out, dwa, dwb, dw_on = mid_bwd(
        dx1, merged, y_a, o_b, proj, w_on, wts["w_branch_a"], wts["w_branch_b"], wts["w_out"])
    tok = early.begin(dict(w_branch_a=dwa, w_branch_b=dwb, w_out=dw_out, w_gate_up=dw_gu, w_down=dw_down,
                           w_ple_gate=dw_pg, w_ple_proj=dw_pp))
    ddw, ddwk, ddqg, ddkdec, ddp, ddegl = dn_scan_bwd(u, wk, qg, kdec, pm, egl + tok, states, do_b, b, s)
    tok = early.exchange(ddegl)
    dc3, dbd, dal, ddtb = dn_post_bwd(c, bd32, al_row + tok, dtb_row, tmat, ddw, ddwk, ddqg, ddkdec, ddp, ddegl, b, s)
    dconv, dconv_w = conv_bwd(proj, conv_w8, dc3, b, s)
    dqa, dka, dva, dbt, dbf = attn_bwd(proj, bias_band, y_a, lse, dy_a, b, s)
    tok = early.join(dqa)
    d_rel = bias_band_grad(dbt, dbf)

    pieces = [dgates, dqa, dka, dva, dconv, dz, dbd]
    bounds = [0, 2048, 2560, 3072, 3584, 5120, 5632, 5760]
    windows = [(dgates, 0, 2048), (dqa, 0, 512), (dka, 0, 512), (dva, 0, 512), (dconv, 0, 512), (dconv, 512, 512),
               (dconv, 1024, 512), (dz, 0, 512), (dbd, 0, 128)]
    w_cols = [0, P_QA, P_KA, P_VA, P_CONV, P_CONV + 512, P_CONV + 1024, P_Z, P_BD]
    dx, dg_mix = in_proj_bwd(windows, [(w_inp, c0) for c0 in w_cols], x, dx1, g_mix + tok)
    dwp = None
    for k, pc in enumerate(pieces):
        dwp = matmul_tn(h1, pc, "dw_in_%d" % k, into=dwp, col0=bounds[k], width=P_WIDTH)
    reduced_early = early.finish(dwp)
    dw_in = _w_in_shards(dwp)

    grads = dict(w_in=dw_in, w_branch_a=dwa, w_branch_b=dwb, w_out=dw_out, w_gate_up=dw_gu, w_down=dw_down,
                 w_ple_gate=dw_pg, w_ple_proj=dw_pp)
    small_grads = dict(g_mix=dg_mix[0], g_ffn=dg_ffn[0], g_ple=dg_ple[0], g_final=dg_final[0],
                       conv_w=dconv_w[:CONV_K].reshape(-1), rel_bias=d_rel.reshape(-1), w_onorm=dw_on[0],
                       a_log=dal[0, :B_HEADS], dt_bias=ddtb[0, :B_HEADS], loss=loss[0, :1])
    return dx.reshape(b, s, D_MODEL), grads, small_grads, reduced_early


BIG = (("w_in", (D_MODEL, D_IN), 1), ("w_branch_a", (A_WIDTH, D_MODEL), 1), ("w_branch_b", (B_WIDTH, D_MODEL), 1),
       ("w_out", (D_MODEL, D_MODEL), 0), ("w_gate_up", (D_MODEL, 2 * D_FF), 1), ("w_down", (D_FF, D_MODEL), 0),
       ("w_ple_gate", (D_MODEL, D_MODEL), 0), ("w_ple_proj", (PLE_DIM, D_MODEL), 1))
N_CHIPS = 4
FIRST_WEIGHTS = ("w_in",)
LATER_WEIGHTS = ("w_branch_a", "w_branch_b", "w_out", "w_gate_up", "w_down", "w_ple_gate", "w_ple_proj")
LATE_GRADS = ("w_in",)
EARLY_GRADS = ("w_branch_a", "w_branch_b", "w_out", "w_gate_up", "w_down", "w_ple_gate", "w_ple_proj")


def _items(names):
    return [it for it in BIG if it[0] in names]


def _shard_shape(shape, axis):
    return (shape[0] // N_CHIPS, shape[1]) if axis == 0 else (shape[0], shape[1] // N_CHIPS)


def _width_groups(names):
    groups = {}
    for n, shape, axis in _items(names):
        rs, cs = _shard_shape(shape, axis)
        groups.setdefault(cs, []).append((n, rs))
    return sorted(groups.items())


def grad_buffers(grads, names):
    info = {n: (shape, axis) for n, shape, axis in _items(names)}
    bufs = []
    for cs, members in _width_groups(names):
        segs = []
        for n, rs in members:
            g = grads[n].astype(BF16)
            if g.ndim == 2:
                g = (g.reshape(N_CHIPS, rs, cs) if info[n][1] == 0
                     else jnp.transpose(g.reshape(rs, N_CHIPS, cs), (1, 0, 2)))
            segs.append(g)
        bufs.append(segs[0] if len(segs) == 1 else jnp.concatenate(segs, axis=1))
    return bufs


def split_buffers(reduced, names):
    out = {}
    for (cs, members), buf in zip(_width_groups(names), reduced):
        r0 = 0
        for n, rs in members:
            out[n] = buf[r0:r0 + rs]
            r0 += rs
    return out


def _place():
    return lax.axis_index("x"), lax.axis_index("y"), lax.axis_index("c")


ANY = pl.BlockSpec(memory_space=pl.ANY)


def _gathered_shape(item):
    n, shape, _ = item
    return (N_CHIPS,) + _shard_shape(shape, 1) if n == "w_in" else shape


def _gather_block(o_ref, item, cx, cy, hf):
    n, shape, axis = item
    rs, cs = _shard_shape(shape, axis)
    hr = rs // 2
    ci = 2 * cx + cy
    if n == "w_in":
        return o_ref.at[ci, pl.ds(pl.multiple_of(hf * hr, 16), hr), :]
    if axis == 0:
        return o_ref.at[pl.ds(pl.multiple_of(ci * rs + hf * hr, 16), hr), :]
    return o_ref.at[pl.ds(pl.multiple_of(hf * hr, 16), hr), pl.ds(pl.multiple_of(ci * cs, 128), cs)]


def _own_half(w_ref, item, c):
    hr = _shard_shape(item[1], item[2])[0] // 2
    return w_ref.at[pl.ds(pl.multiple_of(c * hr, 16), hr), :]


def _gather_slot(o_ref, item, cx, cy):
    n, shape, axis = item
    rs, cs = _shard_shape(shape, axis)
    ci = 2 * cx + cy
    if n == "w_in":
        return o_ref.at[ci]
    if axis == 0:
        return o_ref.at[pl.ds(pl.multiple_of(ci * rs, 16), rs), :]
    return o_ref.at[:, pl.ds(pl.multiple_of(ci * cs, 128), cs)]


def _other_chips(x, y):
    return [(1 - x, y), (x, 1 - y), (1 - x, 1 - y)]


def allgather_weights(shards, names):
    items = _items(names)
    nw = len(items)

    def body(*refs):
        w_refs, o_refs = refs[:nw], refs[nw:2 * nw]
        send_sems, recv_sems = refs[2 * nw:]
        x, y, c = _place()
        sibling = (x, y, 1 - c)
        chips = _other_chips(x, y)

        def copy(k, src, dst, to):
            return pltpu.make_async_remote_copy(src_ref=src, dst_ref=dst, send_sem=send_sems.at[k],
                                                recv_sem=recv_sems.at[k], device_id=to, device_id_type=MESH)

        def blk(i, cx, cy, hf):
            return _gather_block(o_refs[i], items[i], cx, cy, hf)

        def my_half(i):
            return _own_half(w_refs[i], items[i], c)

        def own(i):
            return _gather_slot(o_refs[i], items[i], x, y)

        first = [copy(7 * i + j, my_half(i), blk(i, x, y, c), (*chip_, c))
                 for i in range(nw) for j, chip_ in enumerate(chips)]
        first += [copy(7 * i + 6, w_refs[i], own(i), sibling) for i in range(nw)]
        for cp in first:
            cp.start()
        passed = []
        for i in range(nw):
            for j, chip_ in enumerate(chips):
                copy(7 * i + j, my_half(i), blk(i, *chip_, c), (*chip_, c)).wait_recv()
                fwd = copy(7 * i + 3 + j, blk(i, *chip_, c), blk(i, *chip_, c), sibling)
                fwd.start()
                passed.append(fwd)
        for i in range(nw):
            for j, chip_ in enumerate(chips):
                copy(7 * i + 3 + j, my_half(i), blk(i, *chip_, 1 - c), sibling).wait_recv()
            copy(7 * i + 6, w_refs[i], own(i), sibling).wait_recv()
        for cp in first + passed:
            cp.wait_send()

    outs = pl.pallas_call(
        body, name="allgather_weights",
        in_specs=[ANY] * nw, out_specs=[ANY] * nw,
        out_shape=[jax.ShapeDtypeStruct(_gathered_shape(it), BF16) for it in items],
        scratch_shapes=[pltpu.SemaphoreType.DMA((7 * nw,)), pltpu.SemaphoreType.DMA((7 * nw,))],
    )(*[shards[it[0]] for it in items])
    return {it[0]: o for it, o in zip(items, outs)}


HBM_SPEC = pl.BlockSpec(memory_space=pltpu.HBM)
SEM_SPEC = pl.BlockSpec(memory_space=pltpu.SEMAPHORE)
EFFECT = pltpu.SideEffectType.DATAFLOW_SIDE_EFFECTING


def _in_hbm(a):
    return pltpu.with_memory_space_constraint(a, pltpu.HBM)


def copies_start(name, bufs, ncopies, plan):
    nb = len(bufs)

    def body(*refs):
        in_refs, send_sems, recv_sems, token = refs[:nb], refs[nb], refs[nb + 1], refs[-1]
        for k, (src, dst, to) in enumerate(plan(in_refs)):
            pltpu.make_async_remote_copy(src_ref=src, dst_ref=dst, send_sem=send_sems.at[k],
                                         recv_sem=recv_sems.at[k], device_id=to, device_id_type=MESH).start()
        token[...] = jnp.zeros_like(token)

    outs = pl.pallas_call(
        body, name=name,
        in_specs=[HBM_SPEC] * nb,
        out_specs=(SEM_SPEC, SEM_SPEC, *[HBM_SPEC] * nb, pl.BlockSpec(memory_space=pltpu.VMEM)),
        out_shape=(pltpu.SemaphoreType.DMA((ncopies,)), pltpu.SemaphoreType.DMA((ncopies,)),
                   *[pltpu.HBM(b.shape, b.dtype) for b in bufs], jax.ShapeDtypeStruct((8, 128), F32)),
        input_output_aliases={i: 2 + i for i in range(nb)},
        compiler_params=pltpu.CompilerParams(has_side_effects=EFFECT),
    )(*[_in_hbm(b) for b in bufs])
    return outs[0], outs[1], list(outs[2:2 + nb]), outs[-1][0, 0]


def copies_wait(name, send_sems, recv_sems, bufs, after, plan):
    nb = len(bufs)

    def body(*refs):
        in_refs, s_sems, r_sems = refs[:nb], refs[nb], refs[nb + 1]
        for k, (src, dst, to) in enumerate(plan(in_refs)):
            cp = pltpu.make_async_remote_copy(src_ref=src, dst_ref=dst, send_sem=s_sems.at[k],
                                              recv_sem=r_sems.at[k], device_id=to, device_id_type=MESH)
            cp.wait_send()
            cp.wait_recv()

    return list(pl.pallas_call(
        body, name=name,
        in_specs=[HBM_SPEC] * nb + [SEM_SPEC, SEM_SPEC, ANY],
        out_specs=tuple([HBM_SPEC] * nb),
        out_shape=tuple(pltpu.HBM(b.shape, b.dtype) for b in bufs),
        input_output_aliases={i: i for i in range(nb)},
        compiler_params=pltpu.CompilerParams(has_side_effects=EFFECT),
    )(*bufs, send_sems, recv_sems, after))


def _landing(shape, dtype):
    return _in_hbm(lax.empty(shape, dtype))


class LaterWeights:
    def __init__(self, shards):
        self.items = _items(LATER_WEIGHTS)
        self.shards = shards
        self.nw = len(self.items)

    def _ici_plan(self, refs):
        x, y, c = _place()
        w_refs, o_refs = refs[:self.nw], refs[self.nw:]
        plan = [(_own_half(w_refs[i], it, c), _gather_block(o_refs[i], it, x, y, c), (*chip_, c))
                for i, it in enumerate(self.items) for chip_ in _other_chips(x, y)]
        return plan + [(w_refs[i], _gather_slot(o_refs[i], it, x, y), (x, y, 1 - c))
                       for i, it in enumerate(self.items)]

    def _d2d_plan(self, refs):
        x, y, c = _place()
        return [(_gather_block(refs[i], it, *chip_, c), _gather_block(refs[i], it, *chip_, c), (x, y, 1 - c))
                for i, it in enumerate(self.items) for chip_ in _other_chips(x, y)]

    def _d2d_wait_plan(self, refs):
        x, y, c = _place()
        return [(_gather_block(refs[i], it, *chip_, c), _gather_block(refs[i], it, *chip_, 1 - c), (x, y, 1 - c))
                for i, it in enumerate(self.items) for chip_ in _other_chips(x, y)]

    def _ici_wait_plan(self, refs):
        x, y, c = _place()
        w_refs, o_refs = refs[:self.nw], refs[self.nw:]
        plan = [(_own_half(w_refs[i], it, c), _gather_block(o_refs[i], it, *chip_, c), (*chip_, c))
                for i, it in enumerate(self.items) for chip_ in _other_chips(x, y)]
        return plan + [(w_refs[i], _gather_slot(o_refs[i], it, x, y), (x, y, 1 - c))
                       for i, it in enumerate(self.items)]

    def begin(self):
        srcs = [self.shards[it[0]] for it in self.items]
        lands = [_landing(_gathered_shape(it), BF16) for it in self.items]
        self.s1, self.r1, self.b1, tok = copies_start("gather_ici_start", srcs + lands, 4 * self.nw, self._ici_plan)
        return tok

    def forward(self, after):
        b1 = copies_wait("gather_ici_wait", self.s1, self.r1, self.b1, after, self._ici_wait_plan)
        self.s2, self.r2, self.b2, tok = copies_start("gather_d2d_start", b1[self.nw:], 3 * self.nw, self._d2d_plan)
        return tok

    def finish(self, after):
        outs = copies_wait("gather_d2d_wait", self.s2, self.r2, self.b2, after, self._d2d_wait_plan)
        return {it[0]: o for it, o in zip(self.items, outs)}


def small_allreduce(v, name):
    r = v.shape[0]

    def body(v_ref, o_ref, buf, send_sems, recv_sems):
        x, y, c = _place()
        me = 4 * x + 2 * y + c
        buf[me] = v_ref[...]
        flips = [(fx, fy, fc) for fx in (0, 1) for fy in (0, 1) for fc in (0, 1)][1:]
        peers = [((1 - x) if fx else x, (1 - y) if fy else y, (1 - c) if fc else c) for fx, fy, fc in flips]

        def copy(k, slot, to):
            return pltpu.make_async_remote_copy(src_ref=v_ref, dst_ref=buf.at[slot], send_sem=send_sems.at[k],
                                                recv_sem=recv_sems.at[k], device_id=to, device_id_type=MESH)

        sends = [copy(k, me, peer) for k, peer in enumerate(peers)]
        for cp in sends:
            cp.start()
        for k, (px, py, pc) in enumerate(peers):
            copy(k, 4 * px + 2 * py + pc, (px, py, pc)).wait_recv()
        for cp in sends:
            cp.wait_send()
        acc = buf[0]
        for d in range(1, 8):
            acc = acc + buf[d]
        o_ref[...] = acc

    return pl.pallas_call(
        body, name=name,
        in_specs=[pl.BlockSpec(memory_space=pltpu.VMEM)], out_specs=pl.BlockSpec(memory_space=pltpu.VMEM),
        out_shape=jax.ShapeDtypeStruct((r, 128), F32),
        scratch_shapes=[pltpu.VMEM((8, r, 128), F32), pltpu.SemaphoreType.DMA((7,)), pltpu.SemaphoreType.DMA((7,))],
    )(v)


def add_halves(g, other, place):
    half, wd = other.shape[1:]
    tr = _tile_rows(half, wd)
    nblk = half // tr

    def body(pref, g0, g1, g2, g3, o0, o1, o2, o3, pf_ref, pb_ref):
        f = lambda r: r[...].astype(F32)
        pf_ref[...] = f(g0) + f(o0)
        pb_ref[0] = _bf(f(g1) + f(o1))
        pb_ref[1] = _bf(f(g2) + f(o2))
        pb_ref[2] = _bf(f(g3) + f(o3))

    gspec = lambda k: pl.BlockSpec((None, tr, wd), lambda i, pr: ((pr[0] + k) % N_CHIPS, pr[1] * nblk + i, 0))
    ospec = lambda k: pl.BlockSpec((None, tr, wd), lambda i, pr: ((pr[0] + k) % N_CHIPS, i, 0))
    return pl.pallas_call(
        body, name="add_halves",
        grid_spec=pltpu.PrefetchScalarGridSpec(
            num_scalar_prefetch=1, grid=(nblk,),
            in_specs=[gspec(0), gspec(1), gspec(2), gspec(3), ospec(0), ospec(1), ospec(2), ospec(3)],
            out_specs=[pl.BlockSpec((tr, wd), lambda i, pr: (i, 0)),
                       pl.BlockSpec((3, tr, wd), lambda i, pr: (0, i, 0))]),
        out_shape=[jax.ShapeDtypeStruct((half, wd), F32), jax.ShapeDtypeStruct((3, half, wd), BF16)],
        compiler_params=_cp(("parallel",), VMEM_LIMIT),
    )(place, g, g, g, g, other, other, other, other)


def _tile_rows(n, width):
    best = 16
    for t in range(16, max(16, (384 * 1024) // width) + 1, 16):
        if n % t == 0:
            best = t
    assert n % best == 0
    return best


def add_partials(pf, got, place):
    half, wd = pf.shape
    tr = _tile_rows(half, wd)

    def body(pref, pf_ref, got_ref, o_ref):
        o_ref[...] = ((pf_ref[...] + got_ref[0].astype(F32)) + got_ref[1].astype(F32)) + got_ref[2].astype(F32)

    return pl.pallas_call(
        body, name="add_partials",
        grid_spec=pltpu.PrefetchScalarGridSpec(
            num_scalar_prefetch=1, grid=(half // tr,),
            in_specs=[pl.BlockSpec((tr, wd), lambda i, pr: (i, 0)),
                      pl.BlockSpec((3, tr, wd), lambda i, pr: (0, i, 0))],
            out_specs=pl.BlockSpec((None, tr, wd), lambda i, pr: (pr[1], i, 0))),
        out_shape=jax.ShapeDtypeStruct((2, half, wd), F32),
        compiler_params=_cp(("parallel",), VMEM_LIMIT),
    )(place, pf, got)


class GradReduce:
    def __init__(self, place, names, tag):
        self.place, self.names, self.tag = place, names, tag
        self.nb = len(_width_groups(names))

    def _swap_plan(self, refs):
        x, y, c = _place()
        plan = []
        for g_ref, o_ref in zip(refs[:self.nb], refs[self.nb:]):
            half = o_ref.shape[1]
            plan.append((g_ref.at[:, pl.ds(pl.multiple_of((1 - c) * half, 16), half), :], o_ref, (x, y, 1 - c)))
        return plan

    def _exchange_plan(self, refs):
        x, y, c = _place()
        me = 2 * x + y
        return [(p_ref.at[k - 1], o_ref.at[k - 1], (((me + k) % N_CHIPS) // 2, ((me + k) % N_CHIPS) % 2, c))
                for p_ref, o_ref in zip(refs[:self.nb], refs[self.nb:]) for k in range(1, N_CHIPS)]

    def _join_plan(self, refs):
        x, y, c = _place()
        return [(r.at[c], r.at[c], (x, y, 1 - c)) for r in refs]

    def _join_wait_plan(self, refs):
        x, y, c = _place()
        return [(r.at[c], r.at[1 - c], (x, y, 1 - c)) for r in refs]

    def begin(self, grads):
        gs = grad_buffers(grads, self.names)
        lands = [_landing((N_CHIPS, g.shape[1] // 2, g.shape[2]), BF16) for g in gs]
        self.s1, self.r1, self.b1, tok = copies_start(self.tag + "_swap_start", gs + lands, self.nb, self._swap_plan)
        return tok

    def exchange(self, after):
        b1 = copies_wait(self.tag + "_swap_wait", self.s1, self.r1, self.b1, after, self._swap_plan)
        sums = [add_halves(g, other, self.place) for g, other in zip(b1[:self.nb], b1[self.nb:])]
        self.pfs = [pf for pf, _ in sums]
        pbs = [pb for _, pb in sums]
        lands = [_landing(pb.shape, BF16) for pb in pbs]
        self.s2, self.r2, self.b2, tok = copies_start(self.tag + "_exchange_start", pbs + lands, 3 * self.nb,
                                                      self._exchange_plan)
        return tok

    def join(self, after):
        b2 = copies_wait(self.tag + "_exchange_wait", self.s2, self.r2, self.b2, after, self._exchange_plan)
        boths = [add_partials(pf, got, self.place) for pf, got in zip(self.pfs, b2[self.nb:])]
        self.s3, self.r3, self.b3, tok = copies_start(self.tag + "_join_start", boths, self.nb, self._join_plan)
        return tok

    def finish(self, after):
        boths = copies_wait(self.tag + "_join_wait", self.s3, self.r3, self.b3, after, self._join_wait_plan)
        return split_buffers([b.reshape(-1, b.shape[2]) for b in boths], self.names)


SMALL = (("g_mix", D_MODEL), ("g_ffn", D_MODEL), ("g_ple", D_MODEL), ("g_final", D_MODEL),
         ("conv_w", CONV_K * CONV_CH), ("rel_bias", A_HEADS * N_REL), ("w_onorm", B_DIM),
         ("a_log", B_HEADS), ("dt_bias", B_HEADS), ("loss", 1))


def _pad128(v):
    v = v.reshape(-1)
    return jnp.pad(v, (0, -v.shape[0] % 128))


def pack_small(d, names, rows):
    flat = jnp.concatenate([_pad128(d[n]) for n in names]).reshape(-1, 128)
    return jnp.pad(flat, ((0, rows - flat.shape[0]), (0, 0)))


def unpack_small(flat, names_sizes):
    out, r0 = {}, 0
    v = flat.reshape(-1)
    for n, size in names_sizes:
        out[n] = v[r0:r0 + size]
        r0 += -(-size // 128) * 128
    return out


def kernel(x, p, g_mix, w_in, conv_w, a_log, dt_bias, rel_bias, w_onorm, w_branch_a, w_branch_b, w_out, g_ffn, w_gate_up, w_down, g_ple, w_ple_gate, w_ple_proj, g_final, loss_target, m_g_mix, m_w_in, m_conv_w, m_a_log, m_dt_bias, m_rel_bias, m_w_onorm, m_w_branch_a, m_w_branch_b, m_w_out, m_g_ffn, m_w_gate_up, m_w_down, m_g_ple, m_w_ple_gate, m_w_ple_proj, m_g_final, v_g_mix, v_w_in, v_conv_w, v_a_log, v_dt_bias, v_rel_bias, v_w_onorm, v_w_branch_a, v_w_branch_b, v_w_out, v_g_ffn, v_w_gate_up, v_w_down, v_g_ple, v_w_ple_gate, v_w_ple_proj, v_g_final):
    names = ["g_mix", "w_in", "conv_w", "a_log", "dt_bias", "rel_bias", "w_onorm", "w_branch_a", "w_branch_b",
             "w_out", "g_ffn", "w_gate_up", "w_down", "g_ple", "w_ple_gate", "w_ple_proj", "g_final"]
    w = dict(zip(names, [g_mix, w_in, conv_w, a_log, dt_bias, rel_bias, w_onorm, w_branch_a, w_branch_b, w_out,
                         g_ffn, w_gate_up, w_down, g_ple, w_ple_gate, w_ple_proj, g_final]))
    m = dict(zip(names, [m_g_mix, m_w_in, m_conv_w, m_a_log, m_dt_bias, m_rel_bias, m_w_onorm, m_w_branch_a,
                         m_w_branch_b, m_w_out, m_g_ffn, m_w_gate_up, m_w_down, m_g_ple, m_w_ple_gate,
                         m_w_ple_proj, m_g_final]))
    v = dict(zip(names, [v_g_mix, v_w_in, v_conv_w, v_a_log, v_dt_bias, v_rel_bias, v_w_onorm, v_w_branch_a,
                         v_w_branch_b, v_w_out, v_g_ffn, v_w_gate_up, v_w_down, v_g_ple, v_w_ple_gate,
                         v_w_ple_proj, v_g_final]))
    xi, yi, ci = _place()
    chip = 2 * xi + yi
    big_names = [n for n, _, _ in BIG]

    shards2d = {n: w[n].reshape(w[n].shape[-2:]) for n in big_names}
    shards_bf = {n: a.astype(BF16) for n, a in shards2d.items()}
    g4 = allgather_weights(shards_bf, FIRST_WEIGHTS)["w_in"]
    place = jnp.stack([chip, ci]).astype(jnp.int32)
    conv_sh = jnp.where(ci == 0, w["conv_w"].reshape(CONV_K, CONV_CH // N_CHIPS), 0.0)
    conv_slots = lax.dynamic_update_slice(jnp.zeros((N_CHIPS, CONV_K, CONV_CH // N_CHIPS), F32), conv_sh[None],
                                          (chip, 0, 0))
    conv_all = small_allreduce(conv_slots.reshape(-1, 128), "gather_conv_w")
    conv_full = jnp.transpose(conv_all.reshape(N_CHIPS, CONV_K, CONV_CH // N_CHIPS), (1, 0, 2)).reshape(CONV_K, CONV_CH)
    small = {n: w[n] for n in names if n not in big_names}
    small["conv_w"] = conv_full

    grad_x, grads, small_grads, reduced_early = local_step(
        x, p[0], loss_target, g4, small, LaterWeights(shards_bf), GradReduce(place, EARLY_GRADS, "grads"))

    late = GradReduce(place, LATE_GRADS, "late")
    tok = late.begin(grads)
    small_names = [n for n, _ in SMALL]
    small_grads["loss"] = small_grads["loss"] + tok
    red_flat = small_allreduce(pack_small(small_grads, small_names, 112), "allreduce_small")
    red = unpack_small(red_flat, SMALL)
    dep = jnp.full((8, 128), late.exchange(red_flat), F32)
    gshard = dict(reduced_early)
    loss = red["loss"][0]
    conv_g = lax.dynamic_slice(red["conv_w"].reshape(CONV_K, N_CHIPS, CONV_CH // N_CHIPS), (0, chip, 0),
                               (CONV_K, 1, CONV_CH // N_CHIPS))
    gsmall = {n: red[n].reshape(w[n].shape) for n in small_names if n not in ("loss", "conv_w")}
    gsmall["conv_w"] = conv_g.reshape(w["conv_w"].shape)

    grad, delta, new_m, new_v = {}, {}, {}, {}
    for n in list(EARLY_GRADS) + list(LATE_GRADS):
        if n in LATE_GRADS:
            late.join(v_)
            gshard.update(late.finish(v_))
        shp = w[n].shape
        d_, m_, v_ = adamw(shards2d[n], gshard[n], m[n].reshape(shp[-2:]), v[n].reshape(shp[-2:]), "adamw_" + n,
                           dep=dep if n in EARLY_GRADS else None)
        dep, v_ = lax.optimization_barrier((dep, v_))
        grad[n], delta[n], new_m[n], new_v[n] = gshard[n].reshape(shp), d_.reshape(shp), m_.reshape(shp), v_.reshape(shp)
    snames = [n for n in small_names if n != "loss"]
    ssizes = [(n, w[n].size) for n in snames]
    pk = lambda d: pack_small(d, snames, 64)
    d_, m_, v_ = adamw(pk(w), pk(gsmall), pk(m), pk(v), "adamw_small")
    ds, ms, vs = unpack_small(d_, ssizes), unpack_small(m_, ssizes), unpack_small(v_, ssizes)
    for n in snames:
        shp = w[n].shape
        grad[n], delta[n], new_m[n], new_v[n] = gsmall[n], ds[n].reshape(shp), ms[n].reshape(shp), vs[n].reshape(shp)

    return (loss, grad_x, *[grad[n] for n in names], *[delta[n] for n in names],
            *[new_m[n] for n in names], *[new_v[n] for n in names])
```

```python
import functools

import jax
import jax.numpy as jnp
from jax import lax
from jax.experimental import pallas as pl
from jax.experimental.pallas import tpu as pltpu

F32 = jnp.float32
BF16 = jnp.bfloat16
MESH = pl.DeviceIdType.MESH

D_MODEL = 1024
CHUNK = 64
PLE_DIM = 256
EPS = 1e-6
A_HEADS = 8
A_HEAD_DIM = 64
A_WIDTH = 512
A_LOOKBACK = 8
BAND = (A_LOOKBACK + 1) * CHUNK
TAIL = 3 * CHUNK
REL_CLIP = 128
N_REL = 2 * REL_CLIP + 1
B_HEADS = 4
B_DIM = 128
B_WIDTH = 512
CONV_K = 4
CONV_CH = 1536
D_FF = 2816
SPLIT_Z = 3584
D_IN = 5640
ADAM_LR, ADAM_B1, ADAM_B2, ADAM_EPS, ADAM_WD, ADAM_STEP = 0.001, 0.9, 0.999, 1e-08, 0.01, 10

P_GATES, P_QA, P_KA, P_VA, P_CONV, P_Z, P_BD, P_WIDTH = 0, 2048, 2560, 3072, 3584, 5120, 5632, 5760

VMEM_LIMIT = 56 * 1024 * 1024


def _cp(sem, vmem=None, **kw):
    return pltpu.CompilerParams(dimension_semantics=sem, vmem_limit_bytes=vmem, **kw)


def _tile(n, cap):
    best = None
    for t in range(128, cap + 1, 128):
        if n % t == 0:
            best = t
    assert best is not None, (n, cap)
    return best


def _nn(a, b, prec=None):
    return lax.dot_general(a, b, (((1,), (0,)), ((), ())), preferred_element_type=F32, precision=prec)


def _nt(a, b, prec=None):
    return lax.dot_general(a, b, (((1,), (1,)), ((), ())), preferred_element_type=F32, precision=prec)


def _tn(a, b, prec=None):
    return lax.dot_general(a, b, (((0,), (0,)), ((), ())), preferred_element_type=F32, precision=prec)


def _bnn(a, b, prec=None):
    return lax.dot_general(a, b, (((2,), (1,)), ((0,), (0,))), preferred_element_type=F32, precision=prec)


def _bnt(a, b, prec=None):
    return lax.dot_general(a, b, (((2,), (2,)), ((0,), (0,))), preferred_element_type=F32, precision=prec)


def _bf(a):
    return a.astype(BF16)


def _split(a):
    hi = a.astype(BF16)
    return hi, (a - hi.astype(F32)).astype(BF16)


def _split3(a):
    h1 = _bf(a)
    r1 = a - h1.astype(F32)
    h2 = _bf(r1)
    return h1, h2, _bf(r1 - h2.astype(F32))


def _bnn_exact(lhs_b, rhs):
    h1, h2, h3 = _split3(rhs)
    return _bnn(lhs_b, h1) + (_bnn(lhs_b, h2) + _bnn(lhs_b, h3))


def _bnn3(a, b):
    ah, al = a if isinstance(a, tuple) else _split(a)
    bh, bl = b if isinstance(b, tuple) else _split(b)
    return _bnn(ah, bh) + (_bnn(ah, bl) + _bnn(al, bh))


def _sigmoid(x):
    return 0.5 * jnp.tanh(0.5 * x) + 0.5


def _softplus(x):
    return jnp.maximum(x, 0.0) + jnp.log(1.0 + jnp.exp(-jnp.abs(x)))


def rms_matmul(x, g, w, name, tm=512, tn_cap=1024):
    t, d = x.shape
    n = w.shape[1]
    tm = min(tm, t)
    tn = _tile(n, tn_cap)

    nj = n // tn

    def body(x_ref, g_ref, w_ref, o_ref, h_ref, tail_ref):
        @pl.when(pl.program_id(1) == 0)
        def _():
            xv = x_ref[...]
            r = lax.rsqrt(jnp.mean(xv * xv, axis=-1, keepdims=True) + EPS)
            h_ref[...] = _bf(xv * r * g_ref[...])

        res = _nn(h_ref[...], w_ref[...])
        o_ref[...] = _bf(res)

        @pl.when(pl.program_id(1) == nj - 1)
        def _():
            tail_ref[...] = res[:, tn - 128:]

    return pl.pallas_call(
        body, name=name, grid=(t // tm, nj),
        in_specs=[pl.BlockSpec((tm, d), lambda i, j: (i, 0)),
                  pl.BlockSpec((1, d), lambda i, j: (0, 0)),
                  pl.BlockSpec((d, tn), lambda i, j: (0, j))],
        out_specs=[pl.BlockSpec((tm, tn), lambda i, j: (i, j)),
                   pl.BlockSpec((tm, d), lambda i, j: (i, 0)),
                   pl.BlockSpec((tm, 128), lambda i, j: (i, 0))],
        out_shape=[jax.ShapeDtypeStruct((t, n), BF16), jax.ShapeDtypeStruct((t, d), BF16),
                   jax.ShapeDtypeStruct((t, 128), F32)],
        compiler_params=_cp(("parallel", "arbitrary"), VMEM_LIMIT),
    )(x, g, w)


def matmul_tn(a, b, name, into=None, col0=0, width=None, tm=1024, tk_cap=1408, tn_cap=1408, tiles_major=False):
    m, k1 = a.shape
    n = b.shape[1]
    tm = min(tm, m)
    tk = _tile(k1, tk_cap)
    tn = _tile(n, tn_cap)
    while col0 % tn:
        tn = _tile(n, tn - 128)
    nk = m // tm
    c0 = col0 // tn

    def body(*refs):
        a_ref, b_ref, o_ref, acc = refs[0], refs[1], refs[-2], refs[-1]

        @pl.when(pl.program_id(2) == 0)
        def _():
            acc[...] = jnp.zeros_like(acc)

        acc[...] += _tn(_bf(a_ref[...]), _bf(b_ref[...]))

        @pl.when(pl.program_id(2) == nk - 1)
        def _():
            o_ref[...] = _bf(acc[...])

    in_specs = [pl.BlockSpec((tm, tk), lambda i, j, k: (k, i)),
                pl.BlockSpec((tm, tn), lambda i, j, k: (k, j))]
    args = [a, b]
    total = n if width is None else width
    aliases = {}
    if into is not None:
        in_specs.append(ANY)
        args.append(into)
        aliases = {2: 0}
    if tiles_major:
        out_spec = pl.BlockSpec((None, tk, tn), lambda i, j, k: (c0 + j, i, 0))
        out_shape = jax.ShapeDtypeStruct((total // tn, k1, tn), BF16)
    else:
        out_spec = pl.BlockSpec((tk, tn), lambda i, j, k: (i, c0 + j))
        out_shape = jax.ShapeDtypeStruct((k1, total), BF16)
    return pl.pallas_call(
        body, name=name, grid=(k1 // tk, n // tn, nk),
        in_specs=in_specs,
        out_specs=out_spec,
        out_shape=out_shape,
        scratch_shapes=[pltpu.VMEM((tk, tn), F32)],
        input_output_aliases=aliases,
        compiler_params=_cp(("parallel", "parallel", "arbitrary"), VMEM_LIMIT),
    )(*args)


def _tail_onehot(qi):
    r = lax.broadcasted_iota(jnp.int32, (384, TAIL), 0)
    kj = lax.broadcasted_iota(jnp.int32, (384, TAIL), 1)
    return (r == jnp.minimum(REL_CLIP + qi - kj, REL_CLIP) + REL_CLIP).astype(F32)


def bias_tail(rel_pad):
    def body(rb_ref, o_ref):
        parts = _split3(rb_ref[...])
        for qi in range(CHUNK):
            oh = _bf(_tail_onehot(qi))
            o_ref[qi] = _nn(parts[0], oh) + (_nn(parts[1], oh) + _nn(parts[2], oh))

    return pl.pallas_call(
        body, name="bias_tail",
        out_shape=jax.ShapeDtypeStruct((CHUNK, A_HEADS, TAIL), F32),
    )(rel_pad)


def bias_grad(db_t, db_far):
    def body(t_ref, f_ref, o_ref):
        acc = jnp.zeros((A_HEADS, 384), F32)
        for qi in range(CHUNK):
            oh = _bf(_tail_onehot(qi))
            parts = _split3(t_ref[qi])
            acc = acc + (_nt(parts[0], oh) + (_nt(parts[1], oh) + _nt(parts[2], oh)))
        far = jnp.sum(jnp.sum(f_ref[...], axis=2), axis=1, keepdims=True)
        lane = lax.broadcasted_iota(jnp.int32, (A_HEADS, 384), 1)
        o_ref[...] = acc + jnp.where(lane == 2 * REL_CLIP, far, 0.0)

    return pl.pallas_call(
        body, name="bias_grad",
        out_shape=jax.ShapeDtypeStruct((A_HEADS, 384), F32),
    )(db_t, db_far)


ATT_CB = 8


WIN = BAND + CHUNK


def _stack_heads(a, lane):
    return jnp.concatenate([jnp.where(lane < 64, a, 0.0), jnp.where(lane >= 64, a, 0.0)], axis=0)


def _fill_band_pads(k_ref, v_ref, kp, vp, s):
    z = jnp.zeros((A_LOOKBACK * CHUNK, 128), BF16)
    kp[pl.ds(0, A_LOOKBACK * CHUNK), :] = z
    vp[pl.ds(0, A_LOOKBACK * CHUNK), :] = z
    kp[pl.ds(A_LOOKBACK * CHUNK, s), :] = _bf(k_ref[...])
    vp[pl.ds(A_LOOKBACK * CHUNK, s), :] = _bf(v_ref[...])


def attn_fwd(proj, bias_band, b, s):
    t = b * s
    nc = s // CHUNK
    qb, kb_, vb_ = P_QA // 128, P_KA // 128, P_VA // 128

    nstep = nc // ATT_CB
    rows = ATT_CB * CHUNK

    def body(q_ref, k_ref, v_ref, b_ref, o_ref, lse_ref, kp, vp):
        n0 = pl.program_id(2) * ATT_CB

        @pl.when(n0 == 0)
        def _():
            _fill_band_pads(k_ref, v_ref, kp, vp, s)

        lane = lax.broadcasted_iota(jnp.int32, (2 * CHUNK, 128), 1)
        col = lax.broadcasted_iota(jnp.int32, (4 * CHUNK, WIN), 1)
        bias4 = b_ref[...]

        def two_pairs(i, carry):
            pps = (2 * i, 2 * i + 1)
            ns = [n0 + 2 * pp for pp in pps]
            r0s = [pl.multiple_of(pp * 2 * CHUNK, 2 * CHUNK) for pp in pps]
            starts = [pl.multiple_of(n * CHUNK, CHUNK) for n in ns]
            kbs = [kp[pl.ds(st_, WIN), :] for st_ in starts]
            vbs = [vp[pl.ds(st_, WIN), :] for st_ in starts]
            q4s = [_bf(_stack_heads(q_ref[pl.ds(r0, 2 * CHUNK), :] * (A_HEAD_DIM ** -0.5), lane)) for r0 in r0s]
            qks = [_nt(q4, kb) for q4, kb in zip(q4s, kbs)]
            scs = [jnp.where(col >= (A_LOOKBACK - n) * CHUNK, qk + bias4, -1e30) for n, qk in zip(ns, qks)]
            mxs = [jnp.max(sc, axis=1, keepdims=True) for sc in scs]
            ps = [jnp.exp(sc - mx) for sc, mx in zip(scs, mxs)]
            ls = [jnp.sum(p, axis=1, keepdims=True) for p in ps]
            o4s = [_nn(_bf(p), vb) / l for p, vb, l in zip(ps, vbs, ls)]
            for r0, o4, mx, l in zip(r0s, o4s, mxs, ls):
                lse4 = mx + jnp.log(l)
                o_ref[pl.ds(r0, 2 * CHUNK), :] = jnp.where(lane < 64, o4[:2 * CHUNK], o4[2 * CHUNK:])
                lse_ref[pl.ds(r0, 2 * CHUNK), :] = jnp.where(lane < 64, lse4[:2 * CHUNK], lse4[2 * CHUNK:])
            return carry

        lax.fori_loop(0, ATT_CB // 4, two_pairs, 0)

    return pl.pallas_call(
        body, name="attn_fwd", grid=(b, 4, nstep),
        in_specs=[pl.BlockSpec((rows, 128), lambda bb, m, n: (bb * nstep + n, qb + m)),
                  pl.BlockSpec((s, 128), lambda bb, m, n: (bb, kb_ + m)),
                  pl.BlockSpec((s, 128), lambda bb, m, n: (bb, vb_ + m)),
                  pl.BlockSpec((None, 4 * CHUNK, WIN), lambda bb, m, n: (m, 0, 0))],
        out_specs=[pl.BlockSpec((rows, 128), lambda bb, m, n: (bb * nstep + n, m)),
                   pl.BlockSpec((rows, 128), lambda bb, m, n: (bb * nstep + n, m))],
        out_shape=[jax.ShapeDtypeStruct((t, A_WIDTH), F32), jax.ShapeDtypeStruct((t, A_WIDTH), F32)],
        scratch_shapes=[pltpu.VMEM((s + A_LOOKBACK * CHUNK, 128), BF16),
                        pltpu.VMEM((s + A_LOOKBACK * CHUNK, 128), BF16)],
        compiler_params=_cp(("parallel", "parallel", "arbitrary"), VMEM_LIMIT),
    )(proj, proj, proj, bias_band)


def attn_bwd(proj, bias_band, y_a, lse, dy_a, b, s):
    t = b * s
    nc = s // CHUNK
    qb, kb_, vb_ = P_QA // 128, P_KA // 128, P_VA // 128
    pad = A_LOOKBACK * CHUNK
    nstep = nc // ATT_CB
    rows = ATT_CB * CHUNK

    def body(q_ref, k_ref, v_ref, b_ref, do_ref, o_ref, lse_ref,
             dq_ref, dk_ref, dv_ref, dbt_ref, dbf_ref, kp, vp, dkp, dvp):
        bb = pl.program_id(1)
        n0 = pl.program_id(2) * ATT_CB

        @pl.when(n0 == 0)
        def _():
            _fill_band_pads(k_ref, v_ref, kp, vp, s)
            dkp[...] = jnp.zeros_like(dkp)
            dvp[...] = jnp.zeros_like(dvp)

        @pl.when((n0 == 0) & (bb == 0))
        def _():
            dbt_ref[...] = jnp.zeros_like(dbt_ref)
            dbf_ref[...] = jnp.zeros_like(dbf_ref)

        lane = lax.broadcasted_iota(jnp.int32, (2 * CHUNK, 128), 1)
        col = lax.broadcasted_iota(jnp.int32, (4 * CHUNK, WIN), 1)
        bias4 = b_ref[...]

        def two_pairs(i, carry):
            pps = (2 * i, 2 * i + 1)
            two = range(2)
            ns = [n0 + 2 * pp for pp in pps]
            r0s = [pl.multiple_of(pp * 2 * CHUNK, 2 * CHUNK) for pp in pps]
            starts = [pl.multiple_of(n * CHUNK, CHUNK) for n in ns]
            kbs = [kp[pl.ds(st_, WIN), :] for st_ in starts]
            vbs = [vp[pl.ds(st_, WIN), :] for st_ in starts]
            q4bs = [_bf(_stack_heads(q_ref[pl.ds(r0, 2 * CHUNK), :] * (A_HEAD_DIM ** -0.5), lane)) for r0 in r0s]
            do4s = [_stack_heads(do_ref[pl.ds(r0, 2 * CHUNK), :], lane) for r0 in r0s]
            do4bs = [_bf(d) for d in do4s]
            os_ = [o_ref[pl.ds(r0, 2 * CHUNK), :] for r0 in r0s]
            lsevs = [lse_ref[pl.ds(r0, 2 * CHUNK), :] for r0 in r0s]
            lse4s = [jnp.concatenate([v_[:, 0:1], v_[:, 64:65]], axis=0) for v_ in lsevs]
            qks = [_nt(q4bs[j], kbs[j]) for j in two]
            dps = [_nt(do4bs[j], vbs[j]) for j in two]
            deltas = [jnp.sum(do4s[j] * jnp.concatenate([os_[j], os_[j]], axis=0), axis=1, keepdims=True) for j in two]
            ps = [jnp.exp(jnp.where(col >= (A_LOOKBACK - ns[j]) * CHUNK, qks[j] + bias4, -1e30) - lse4s[j])
                  for j in two]
            pbs = [_bf(p) for p in ps]
            dss = [ps[j] * (dps[j] - deltas[j]) for j in two]
            dsbs = [_bf(d) for d in dss]
            dv_ws = [_tn(pbs[j], do4bs[j]) for j in two]
            dq4s = [_nn(dsbs[j], kbs[j]) for j in two]
            dk_ws = [_tn(dsbs[j], q4bs[j]) for j in two]
            for j in two:
                dq_ref[pl.ds(r0s[j], 2 * CHUNK), :] = _bf(
                    jnp.where(lane < 64, dq4s[j][:2 * CHUNK], dq4s[j][2 * CHUNK:]) * (A_HEAD_DIM ** -0.5))
                dkp[pl.ds(starts[j], WIN), :] += dk_ws[j]
                dvp[pl.ds(starts[j], WIN), :] += dv_ws[j]
                dbt_ref[...] += dss[j][:, WIN - 256:]
                dbf_ref[...] += dss[j][:, 0:128] + dss[j][:, 128:256] + dss[j][:, 256:384]
            return carry

        lax.fori_loop(0, ATT_CB // 4, two_pairs, 0)

        @pl.when(n0 == nc - ATT_CB)
        def _():
            dk_ref[...] = _bf(dkp[pl.ds(pad, s), :])
            dv_ref[...] = _bf(dvp[pl.ds(pad, s), :])

    return pl.pallas_call(
        body, name="attn_bwd", grid=(4, b, nstep),
        in_specs=[pl.BlockSpec((rows, 128), lambda m, bb, n: (bb * nstep + n, qb + m)),
                  pl.BlockSpec((s, 128), lambda m, bb, n: (bb, kb_ + m)),
                  pl.BlockSpec((s, 128), lambda m, bb, n: (bb, vb_ + m)),
                  pl.BlockSpec((None, 4 * CHUNK, WIN), lambda m, bb, n: (m, 0, 0)),
                  pl.BlockSpec((rows, 128), lambda m, bb, n: (bb * nstep + n, m)),
                  pl.BlockSpec((rows, 128), lambda m, bb, n: (bb * nstep + n, m)),
                  pl.BlockSpec((rows, 128), lambda m, bb, n: (bb * nstep + n, m))],
        out_specs=[pl.BlockSpec((rows, 128), lambda m, bb, n: (bb * nstep + n, m)),
                   pl.BlockSpec((s, 128), lambda m, bb, n: (bb, m)),
                   pl.BlockSpec((s, 128), lambda m, bb, n: (bb, m)),
                   pl.BlockSpec((None, 4 * CHUNK, 256), lambda m, bb, n: (m, 0, 0)),
                   pl.BlockSpec((None, 4 * CHUNK, 128), lambda m, bb, n: (m, 0, 0))],
        out_shape=[jax.ShapeDtypeStruct((t, A_WIDTH), BF16)] * 3
        + [jax.ShapeDtypeStruct((4, 4 * CHUNK, 256), F32),
           jax.ShapeDtypeStruct((4, 4 * CHUNK, 128), F32)],
        scratch_shapes=[pltpu.VMEM((s + pad, 128), BF16), pltpu.VMEM((s + pad, 128), BF16),
                        pltpu.VMEM((s + pad, 128), F32), pltpu.VMEM((s + pad, 128), F32)],
        compiler_params=_cp(("parallel", "arbitrary", "arbitrary"), VMEM_LIMIT),
    )(proj, proj, proj, bias_band, dy_a, y_a, lse)


def _conv_taps(x, w, s):
    row = lax.broadcasted_iota(jnp.int32, x.shape, 0)
    shifted = [x] + [jnp.where(row >= i, pltpu.roll(x, i, 0), 0.0) for i in range(1, CONV_K)]
    acc = shifted[0] * w[CONV_K - 1:CONV_K, :]
    for i in range(1, CONV_K):
        acc = acc + shifted[i] * w[CONV_K - 1 - i:CONV_K - i, :]
    return acc, shifted


def conv_fwd(proj, conv_w8, b, s):
    cb = 512
    c0 = P_CONV // cb

    def body(x_ref, w_ref, o_ref):
        a, _ = _conv_taps(x_ref[...].astype(F32), w_ref[...], s)
        o_ref[...] = a * _sigmoid(a)

    return pl.pallas_call(
        body, name="conv_fwd", grid=(b, CONV_CH // cb),
        in_specs=[pl.BlockSpec((s, cb), lambda bb, j: (bb, c0 + j)),
                  pl.BlockSpec((8, cb), lambda bb, j: (0, j))],
        out_specs=pl.BlockSpec((s, cb), lambda bb, j: (bb, j)),
        out_shape=jax.ShapeDtypeStruct((b * s, CONV_CH), F32),
        compiler_params=_cp(("parallel", "parallel"), VMEM_LIMIT),
    )(proj, conv_w8)


def conv_bwd(proj, conv_w8, dc3, b, s):
    cb = 512
    c0 = P_CONV // cb

    def body(x_ref, w_ref, dc_ref, dx_ref, dw_ref):
        @pl.when(pl.program_id(1) == 0)
        def _():
            dw_ref[...] = jnp.zeros_like(dw_ref)

        w = w_ref[...]
        a, shifted = _conv_taps(x_ref[...].astype(F32), w, s)
        sg = _sigmoid(a)
        da = dc_ref[...] * (sg * (1.0 + a * (1.0 - sg)))
        row = lax.broadcasted_iota(jnp.int32, da.shape, 0)
        dx = da * w[CONV_K - 1:CONV_K, :]
        for i in range(1, CONV_K):
            dx = dx + jnp.where(row < s - i, pltpu.roll(da, s - i, 0), 0.0) * w[CONV_K - 1 - i:CONV_K - i, :]
        dx_ref[...] = _bf(dx)
        r8 =lax.broadcasted_iota(jnp.int32, (8, cb), 0)
        dw = jnp.zeros((8, cb), F32)
        for i in range(CONV_K):
            dw = dw + jnp.where(r8 == CONV_K - 1 - i, jnp.sum(da * shifted[i], axis=0, keepdims=True), 0.0)
        dw_ref[...] += dw

    return pl.pallas_call(
        body, name="conv_bwd", grid=(CONV_CH // cb, b),
        in_specs=[pl.BlockSpec((s, cb), lambda j, bb: (bb, c0 + j)),
                  pl.BlockSpec((8, cb), lambda j, bb: (0, j)),
                  pl.BlockSpec((None, s, cb), lambda j, bb: (j, bb, 0))],
        out_specs=[pl.BlockSpec((s, cb), lambda j, bb: (bb, j)),
                   pl.BlockSpec((8, cb), lambda j, bb: (0, j))],
        out_shape=[jax.ShapeDtypeStruct((b * s, CONV_CH), BF16), jax.ShapeDtypeStruct((8, CONV_CH), F32)],
        compiler_params=_cp(("parallel", "arbitrary"), VMEM_LIMIT),
    )(proj, conv_w8, dc3)


def _pick_lane(v, k):
    lane = lax.broadcasted_iota(jnp.int32, v.shape, 1)
    return jnp.sum(jnp.where(lane == k, v, 0.0), axis=1, keepdims=True)


def _chunk_masks(ncb):
    i = lax.broadcasted_iota(jnp.int32, (ncb, CHUNK, CHUNK), 1)
    j = lax.broadcasted_iota(jnp.int32, (ncb, CHUNK, CHUNK), 2)
    return i, j


def _col_of_row(rowvec, eye):
    return jnp.sum(jnp.where(eye, rowvec, 0.0), axis=2, keepdims=True)


def _dn_chunk_math(cq, ck, cv, bd, al_row, dtb_row, h, ncb, tm=None):
    r = ncb * CHUNK
    i, j = _chunk_masks(ncb)
    eye = i == j
    low = i >= j
    strict = i > j
    ones = jnp.ones((ncb, CHUNK, CHUNK), F32)

    braw = _pick_lane(bd, h)
    draw = _pick_lane(bd, B_HEADS + h)
    al = _pick_lane(al_row, h)
    dtb = _pick_lane(dtb_row, h)
    ea = jnp.exp(al)
    beta = _sigmoid(braw)
    sp_arg = draw + dtb
    g = -ea * _softplus(sp_arg)

    rq = lax.rsqrt(jnp.sum(cq * cq, axis=1, keepdims=True) + EPS)
    rk = lax.rsqrt(jnp.sum(ck * ck, axis=1, keepdims=True) + EPS)
    nq = cq * rq
    kn = ck * rk
    qn = nq * (B_DIM ** -0.5)

    def c3(a):
        return a.reshape(ncb, CHUNK, a.shape[-1])

    qn3, kn3, v3, beta3 = c3(qn), c3(kn), c3(cv), c3(beta)
    gb = jnp.broadcast_to(c3(g), (ncb, CHUNK, CHUNK))
    gc_b = _bnn_exact(low.astype(BF16), gb)
    gr_b = _bnn_exact(_bf(ones), jnp.where(eye, gc_b, 0.0))
    dm = jnp.where(low, jnp.exp(jnp.where(low, gc_b - gr_b, 0.0)), 0.0)
    gc = gc_b[:, :, 0:1]
    gl = gc_b[:, CHUNK - 1:CHUNK, 0:1]
    gam = jnp.exp(gc)
    egl = jnp.exp(gl)
    edec = jnp.exp(gl - gc)

    knb = _bf(kn3)
    kk = _bnt(knb, knb)
    kd = jnp.where(strict, kk * dm, 0.0)
    a = beta3 * kd
    sz = 1 if tm is None else CHUNK
    if tm is None:
        tm = eye.astype(F32)
    while sz < CHUNK:
        off = jnp.where(((i // (2 * sz)) == (j // (2 * sz))) & ((i // sz) != (j // sz)), a, 0.0)
        tmb = _bf(tm)
        tm = tm - _bnn(_bf(_bnn(tmb, _bf(off))), tmb)
        sz *= 2
    bv = beta3 * v3
    bk = (beta3 * gam) * kn3
    sol = _bnn3(_split(tm), jnp.concatenate([bv, bk], axis=2))
    u, wk = sol[:, :, :B_DIM], sol[:, :, B_DIM:]
    qk = _bnt(_bf(qn3), knb)
    p = jnp.where(low, qk * dm, 0.0)
    kdec = kn3 * edec
    qg = gam * qn3
    return dict(beta=beta3, g=c3(g), ea=ea, sp_arg=c3(sp_arg), rq=c3(rq), rk=c3(rk), nq=c3(nq),
                qn=qn3, kn=kn3, v=v3, gc=gc, gl=gl, gam=gam, egl=egl, edec=edec, dm=dm, kd=kd, a=a,
                tm=tm, u=u, wk=wk, qk=qk, p=p, kdec=kdec, qg=qg, eye=eye, low=low, strict=strict)


def dn_prep(c, proj, al_row, dtb_row, b, s, ncb=32):
    t = b * s
    r = ncb * CHUNK
    nblk = t // r
    bd_blk = 0

    def body(cq_ref, ck_ref, cv_ref, bd_ref, al_ref, dtb_ref, u_ref, wk_ref, qg_ref, kdec_ref, p_ref, egl_ref,
             tm_ref):
        h = pl.program_id(1)
        m = _dn_chunk_math(cq_ref[...], ck_ref[...], cv_ref[...], bd_ref[...].astype(F32), al_ref[...], dtb_ref[...], h, ncb)
        tm_ref[...] = m["tm"].reshape(r, CHUNK)
        u_ref[...] = m["u"].reshape(r, B_DIM)
        wk_ref[...] = _bf(m["wk"].reshape(r, B_DIM))
        qg_ref[...] = _bf(m["qg"].reshape(r, B_DIM))
        kdec_ref[...] = _bf(m["kdec"].reshape(r, B_DIM))
        p_ref[...] = m["p"].reshape(r, CHUNK)
        egl_ref[...] = jnp.broadcast_to(m["egl"], (ncb, 8, 128)).reshape(ncb * 8, 128)

    col = lambda k: pl.BlockSpec((r, 128), lambda i, h: (i, k * B_HEADS + h))
    out_col = pl.BlockSpec((r, 128), lambda i, h: (i, h))
    small = pl.BlockSpec((1, 128), lambda i, h: (0, 0))
    return pl.pallas_call(
        body, name="dn_prep", grid=(nblk, B_HEADS),
        in_specs=[col(0), col(1), col(2), pl.BlockSpec((r, 128), lambda i, h: (i, bd_blk)), small, small],
        out_specs=[out_col, out_col, out_col, out_col,
                   pl.BlockSpec((None, r, CHUNK), lambda i, h: (h, i, 0)),
                   pl.BlockSpec((None, ncb * 8, 128), lambda i, h: (h, i, 0)),
                   pl.BlockSpec((None, r, CHUNK), lambda i, h: (h, i, 0))],
        out_shape=[jax.ShapeDtypeStruct((t, B_WIDTH), F32)] + [jax.ShapeDtypeStruct((t, B_WIDTH), BF16)] * 3
        + [jax.ShapeDtypeStruct((B_HEADS, t, CHUNK), F32),
           jax.ShapeDtypeStruct((B_HEADS, t // 8, 128), F32),
           jax.ShapeDtypeStruct((B_HEADS, t, CHUNK), F32)],
        compiler_params=_cp(("parallel", "parallel"), VMEM_LIMIT),
    )(c, c, c, proj, al_row, dtb_row)


SCAN_CB = 4


def dn_scan_fwd(u, wk, qg, kdec, p, egl, b, s):
    t = b * s
    nc = s // CHUNK

    def body(u_ref, wk_ref, qg_ref, kdec_ref, p_ref, egl_ref, o_ref, ss_ref, st):
        @pl.when(pl.program_id(0) == 0)
        def _():
            st[...] = jnp.zeros_like(st)

        chains = [(bb, h) for bb in range(b) for h in range(B_HEADS)]
        sls = [slice(h * B_DIM, (h + 1) * B_DIM) for _, h in chains]
        states = [st[bb * B_HEADS + h] for bb, h in chains]
        for cc in range(SCAN_CB):
            rs = slice(cc * CHUNK, (cc + 1) * CHUNK)
            sbs = [_bf(sh) for sh in states]
            ws = [u_ref[bb, rs, sl] - _nt(wk_ref[bb, rs, sl], sb) for (bb, _), sl, sb in zip(chains, sls, sbs)]
            qs = [_nt(qg_ref[bb, rs, sl], sb) for (bb, _), sl, sb in zip(chains, sls, sbs)]
            wbs = [_bf(w) for w in ws]
            outs = [q + _nn(_bf(p_ref[h, bb, rs, :]), wb) for (bb, h), q, wb in zip(chains, qs, wbs)]
            new_states = [egl_ref[h, bb, cc * 8:cc * 8 + 1, :] * sh + _tn(wb, kdec_ref[bb, rs, sl])
                          for (bb, h), sl, sh, wb in zip(chains, sls, states, wbs)]
            for (bb, h), sh, o in zip(chains, states, outs):
                ss_ref[bb, cc, h] = sh
                o_ref[bb, rs, h * B_DIM:(h + 1) * B_DIM] = o
            states = new_states
        for (bb, h), sh in zip(chains, states):
            st[bb * B_HEADS + h] = sh

    r3 = lambda a: a.reshape(b, s, B_WIDTH)
    rows = SCAN_CB * CHUNK
    act = pl.BlockSpec((b, rows, B_WIDTH), lambda n: (0, n, 0))
    o, states = pl.pallas_call(
        body, name="dn_scan_fwd", grid=(nc // SCAN_CB,),
        in_specs=[act, act, act, act,
                  pl.BlockSpec((B_HEADS, b, rows, CHUNK), lambda n: (0, 0, n, 0)),
                  pl.BlockSpec((B_HEADS, b, SCAN_CB * 8, 128), lambda n: (0, 0, n, 0))],
        out_specs=[act, pl.BlockSpec((b, SCAN_CB, B_HEADS, B_DIM, B_DIM), lambda n: (0, n, 0, 0, 0))],
        out_shape=[jax.ShapeDtypeStruct((b, s, B_WIDTH), F32),
                   jax.ShapeDtypeStruct((b, nc, B_HEADS, B_DIM, B_DIM), F32)],
        scratch_shapes=[pltpu.VMEM((b * B_HEADS, B_DIM, B_DIM), F32)],
        compiler_params=_cp(("arbitrary",), VMEM_LIMIT),
    )(r3(u), r3(wk), r3(qg), r3(kdec), p.reshape(B_HEADS, b, s, CHUNK), egl.reshape(B_HEADS, b, s // 8, 128))
    return o.reshape(t, B_WIDTH), states


def dn_scan_bwd(u, wk, qg, kdec, p, egl, states, do, b, s):
    t = b * s
    nc = s // CHUNK

    def body(u_ref, wk_ref, qg_ref, kdec_ref, p_ref, egl_ref, ss_ref, do_ref,
             dw_ref, dwk_ref, dqg_ref, dkdec_ref, dp_ref, degl_ref, dst):
        @pl.when(pl.program_id(0) == 0)
        def _():
            dst[...] = jnp.zeros_like(dst)

        chains = [(bb, h) for bb in range(b) for h in range(B_HEADS)]
        dstates = [dst[bb * B_HEADS + h] for bb, h in chains]
        n8 = range(len(chains))
        sls = [slice(h * B_DIM, (h + 1) * B_DIM) for _, h in chains]
        for cc in reversed(range(SCAN_CB)):
            rs = slice(cc * CHUNK, (cc + 1) * CHUNK)
            shs = [ss_ref[bb, cc, h] for bb, h in chains]
            sbs = [_bf(sh) for sh in shs]
            dsbs = [_bf(dsp) for dsp in dstates]
            wkbs = [wk_ref[bb, rs, sl] for (bb, _), sl in zip(chains, sls)]
            dobs = [_bf(do_ref[bb, rs, sl]) for (bb, _), sl in zip(chains, sls)]
            t1 = [_nt(wkbs[i], sbs[i]) for i in n8]
            dwa = [_tn(_bf(p_ref[h, bb, rs, :]), dobs[i]) for i, (bb, h) in enumerate(chains)]
            dwb_ = [_nt(kdec_ref[bb, rs, sls[i]], dsbs[i]) for i, (bb, _) in enumerate(chains)]
            dqgs = [_nn(dobs[i], sbs[i]) for i in n8]
            dsq = [_tn(dobs[i], qg_ref[bb, rs, sls[i]]) for i, (bb, _) in enumerate(chains)]
            wbs = [_bf(u_ref[bb, rs, sls[i]] - t1[i]) for i, (bb, _) in enumerate(chains)]
            dws = [dwa[i] + dwb_[i] for i in n8]
            dwbs = [_bf(dw) for dw in dws]
            dwks = [-_nn(dwbs[i], sbs[i]) for i in n8]
            dkdecs = [_nn(wbs[i], dsbs[i]) for i in n8]
            dpms = [_nt(dobs[i], wbs[i]) for i in n8]
            dsw = [_tn(dwbs[i], wkbs[i]) for i in n8]
            tots = [jnp.sum(jnp.sum(shs[i] * dstates[i], axis=1, keepdims=True), axis=0, keepdims=True) for i in n8]
            new_dss = [egl_ref[h, bb, cc * 8:cc * 8 + 1, :] * dstates[i] + dsq[i] - dsw[i]
                       for i, (bb, h) in enumerate(chains)]
            for i, (bb, h) in enumerate(chains):
                dw_ref[bb, rs, sls[i]] = dws[i]
                dqg_ref[bb, rs, sls[i]] = dqgs[i]
                dwk_ref[bb, rs, sls[i]] = dwks[i]
                dkdec_ref[bb, rs, sls[i]] = dkdecs[i]
                dp_ref[h, bb, rs, :] = dpms[i]
                degl_ref[h, bb, cc * 8:(cc + 1) * 8, :] = jnp.broadcast_to(tots[i], (8, 128))
            dstates = new_dss
        for (bb, h), dsp in zip(chains, dstates):
            dst[bb * B_HEADS + h] = dsp

    r3 = lambda a: a.reshape(b, s, B_WIDTH)
    rows = SCAN_CB * CHUNK
    last = nc // SCAN_CB - 1
    act = pl.BlockSpec((b, rows, B_WIDTH), lambda n: (0, last - n, 0))
    pspec = pl.BlockSpec((B_HEADS, b, rows, CHUNK), lambda n: (0, 0, last - n, 0))
    espec = pl.BlockSpec((B_HEADS, b, SCAN_CB * 8, 128), lambda n: (0, 0, last - n, 0))
    outs = pl.pallas_call(
        body, name="dn_scan_bwd", grid=(nc // SCAN_CB,),
        in_specs=[act, act, act, act, pspec, espec,
                  pl.BlockSpec((b, SCAN_CB, B_HEADS, B_DIM, B_DIM), lambda n: (0, last - n, 0, 0, 0)),
                  act],
        out_specs=[act, act, act, act, pspec, espec],
        out_shape=[jax.ShapeDtypeStruct((b, s, B_WIDTH), F32)] * 4
        + [jax.ShapeDtypeStruct((B_HEADS, b, s, CHUNK), F32),
           jax.ShapeDtypeStruct((B_HEADS, b, s // 8, 128), F32)],
        scratch_shapes=[pltpu.VMEM((b * B_HEADS, B_DIM, B_DIM), F32)],
        compiler_params=_cp(("arbitrary",), VMEM_LIMIT),
    )(r3(u), r3(wk), r3(qg), r3(kdec), p.reshape(B_HEADS, b, s, CHUNK), egl.reshape(B_HEADS, b, s // 8, 128),
      states, r3(do))
    return (*[a.reshape(t, B_WIDTH) for a in outs[:4]], outs[4].reshape(B_HEADS, t, CHUNK),
            outs[5].reshape(B_HEADS, t // 8, 128))


def dn_post_bwd(c, proj, al_row, dtb_row, tmat, dw, dwk, dqg, dkdec, dp, degl, b, s, ncb=16):
    t = b * s
    r = ncb * CHUNK
    nblk = t // r
    bd_blk = 0

    def body(cq_ref, ck_ref, cv_ref, bd_ref, al_ref, dtb_ref, tm_ref, dw_ref, dwk_ref, dqg_ref, dkdec_ref, dp_ref,
             degl_ref, dc_ref, dbd_ref, dal_ref, ddtb_ref):
        h = pl.program_id(1)

        @pl.when((pl.program_id(0) == 0) & (h == 0))
        def _():
            dal_ref[...] = jnp.zeros_like(dal_ref)
            ddtb_ref[...] = jnp.zeros_like(ddtb_ref)

        m = _dn_chunk_math(cq_ref[...], ck_ref[...], cv_ref[...], bd_ref[...].astype(F32), al_ref[...], dtb_ref[...], h, ncb,
                           tm=tm_ref[...].reshape(ncb, CHUNK, CHUNK))
        eye, low, strict = m["eye"], m["low"], m["strict"]
        eyef = eye.astype(F32)

        def c3(a):
            return a.reshape(ncb, CHUNK, a.shape[-1])

        du, dwkv, dqg, dkdec = c3(dw_ref[...]), c3(dwk_ref[...]), c3(dqg_ref[...]), c3(dkdec_ref[...])
        dpm = jnp.where(low, c3(dp_ref[...]), 0.0)
        degl = degl_ref[...].reshape(ncb, 8, 128)[:, 0:1, 0:1]
        beta, gam, kn, qn, v = m["beta"], m["gam"], m["kn"], m["qn"], m["v"]
        dm, kd, a, p = m["dm"], m["kd"], m["a"], m["p"]
        knb, qnb = _bf(kn), _bf(qn)

        eyeb = _bf(eyef)
        th, tl = _split(m["tm"])
        tts = (_bf(_bnt(eyeb, th)), _bf(_bnt(eyeb, tl)))
        xy = _bnn3(tts, jnp.concatenate([du, dwkv], axis=2))
        x, y = xy[:, :, :B_DIM], xy[:, :, B_DIM:]
        da = -jnp.where(strict, _bnt(_bf(x), _bf(m["u"])) + _bnt(_bf(y), _bf(m["wk"])), 0.0)
        dv = beta * x
        sy = jnp.sum(y * kn, axis=2, keepdims=True)
        dbeta = jnp.sum(x * v, axis=2, keepdims=True) + gam * sy + jnp.sum(da * kd, axis=2, keepdims=True)
        dgam = beta * sy + jnp.sum(dqg * qn, axis=2, keepdims=True)
        dkk = da * beta * dm
        dqk = dpm * dm
        dkkb, dqkb = _bf(dkk), _bf(dqk)
        dkn = ((beta * gam) * y + _bnn(dkkb, knb) + _bnn(_bf(_bnt(eyeb, dkkb)), knb)
               + _bnn(_bf(_bnt(eyeb, dqkb)), qnb) + dkdec * m["edec"])
        dqn = gam * dqg + _bnn(dqkb, knb)
        mm = da * a + dpm * p
        ek = jnp.sum(dkdec * m["kdec"], axis=2, keepdims=True)
        dgc = (jnp.sum(mm, axis=2, keepdims=True) - _col_of_row(jnp.sum(mm, axis=1, keepdims=True), eye)
               + dgam * gam - ek)
        dgl = jnp.sum(ek, axis=1, keepdims=True) + degl * m["egl"]
        i, _ = _chunk_masks(ncb)
        dgc = dgc + jnp.where(i[:, :, 0:1] == CHUNK - 1, dgl, 0.0)
        upper = (i <= _chunk_masks(ncb)[1]).astype(BF16)
        dg = _bnn_exact(upper, jnp.broadcast_to(dgc, (ncb, CHUNK, CHUNK)))[:, :, 0:1]

        nq = m["nq"]
        dnq = dqn * (B_DIM ** -0.5)
        dcq = m["rq"] * (dnq - nq * jnp.sum(nq * dnq, axis=2, keepdims=True))
        dck = m["rk"] * (dkn - kn * jnp.sum(kn * dkn, axis=2, keepdims=True))
        dc_ref[0] = dcq.reshape(r, B_DIM)
        dc_ref[1] = dck.reshape(r, B_DIM)
        dc_ref[2] = dv.reshape(r, B_DIM)

        dbraw = (dbeta * beta * (1.0 - beta)).reshape(r, 1)
        sgm = _sigmoid(m["sp_arg"])
        ddraw3 = dg * (-m["ea"]) * sgm
        ddraw = ddraw3.reshape(r, 1)
        lane = lax.broadcasted_iota(jnp.int32, (r, 128), 1)
        contrib = jnp.where(lane == h, dbraw, 0.0) + jnp.where(lane == B_HEADS + h, ddraw, 0.0)

        @pl.when(h == 0)
        def _():
            dbd_ref[...] = contrib

        @pl.when(h != 0)
        def _():
            dbd_ref[...] += contrib

        lane8 = lax.broadcasted_iota(jnp.int32, (8, 128), 1)
        tot_al = jnp.sum(jnp.sum(dg * m["g"], axis=1, keepdims=True), axis=0, keepdims=True).reshape(1, 1)
        tot_dtb = jnp.sum(jnp.sum(ddraw3, axis=1, keepdims=True), axis=0, keepdims=True).reshape(1, 1)
        dal_ref[...] += jnp.where(lane8 == h, tot_al, 0.0)
        ddtb_ref[...] += jnp.where(lane8 == h, tot_dtb, 0.0)

    col = lambda k: pl.BlockSpec((r, 128), lambda i, h: (i, k * B_HEADS + h))
    hcol = pl.BlockSpec((r, 128), lambda i, h: (i, h))
    small = pl.BlockSpec((1, 128), lambda i, h: (0, 0))
    acc = pl.BlockSpec((8, 128), lambda i, h: (0, 0))
    return pl.pallas_call(
        body, name="dn_post_bwd", grid=(nblk, B_HEADS),
        in_specs=[col(0), col(1), col(2), pl.BlockSpec((r, 128), lambda i, h: (i, bd_blk)), small, small,
                  pl.BlockSpec((None, r, CHUNK), lambda i, h: (h, i, 0)),
                  hcol, hcol, hcol, hcol,
                  pl.BlockSpec((None, r, CHUNK), lambda i, h: (h, i, 0)),
                  pl.BlockSpec((None, ncb * 8, 128), lambda i, h: (h, i, 0))],
        out_specs=[pl.BlockSpec((3, r, 128), lambda i, h: (0, i, h)),
                   pl.BlockSpec((r, 128), lambda i, h: (i, 0)), acc, acc],
        out_shape=[jax.ShapeDtypeStruct((3, t, B_WIDTH), F32), jax.ShapeDtypeStruct((t, 128), F32),
                   jax.ShapeDtypeStruct((8, 128), F32), jax.ShapeDtypeStruct((8, 128), F32)],
        compiler_params=_cp(("arbitrary", "arbitrary"), VMEM_LIMIT),
    )(c, c, c, proj, al_row, dtb_row, tmat, dw, dwk, dqg, dkdec, dp, degl)


def make_bias_band(rel_bias):
    tail = bias_tail(jnp.pad(rel_bias, ((0, 0), (0, 384 - N_REL))))
    far = jnp.broadcast_to(rel_bias[:, 2 * REL_CLIP][:, None, None], (A_HEADS, CHUNK, BAND - TAIL))
    band = jnp.concatenate([far, jnp.transpose(tail, (1, 0, 2))], axis=2)
    off = jnp.full((A_HEADS, CHUNK, CHUNK), -1e30, F32)
    both = jnp.stack([jnp.concatenate([band, off], axis=2), jnp.concatenate([off, band], axis=2)], axis=1)
    return both.reshape(4, 4 * CHUNK, WIN)


def bias_band_grad(dbt, dbf):
    t5 = dbt.reshape(A_HEADS, 2, CHUNK, 256)
    tail = t5[:, 0, :, :TAIL] + t5[:, 1, :, CHUNK:]
    far = dbf.reshape(A_HEADS, 2, CHUNK, 128).sum(axis=1) + jnp.pad(t5[:, 1, :, :CHUNK], ((0, 0), (0, 0), (0, CHUNK)))
    return bias_grad(jnp.transpose(tail, (1, 0, 2)), far)[:, :N_REL]


def _rms(x):
    r = lax.rsqrt(jnp.mean(x * x, axis=-1, keepdims=True) + EPS)
    return r, x * r


def _rms_bwd(dh, g, r, n):
    dn = dh * g
    return r * (dn - n * jnp.mean(dn * n, axis=-1, keepdims=True)), dh * n


def _gated_onorm(o, z, w_on):
    parts = []
    for h in range(B_HEADS):
        sl = slice(h * B_DIM, (h + 1) * B_DIM)
        r, n = _rms(o[:, sl])
        parts.append((r, n))
    r4 = [p[0] for p in parts]
    n4 = jnp.concatenate([p[1] for p in parts], axis=1)
    w4 = jnp.concatenate([w_on] * B_HEADS, axis=1)
    sz = _sigmoid(z)
    silu = z * sz
    return n4 * w4 * silu, r4, n4, w4, sz, silu


def mid_fwd(x, y_a, o_b, proj, w_on, wa, wb, w_out, tm=256):
    t = x.shape[0]
    tm = min(tm, t)

    def body(x_ref, ya_ref, ob_ref, z_ref, ga_ref, gb_ref, won_ref, wa_ref, wb_ref, wo_ref, x1_ref, mg_ref):
        yb = _gated_onorm(ob_ref[...], z_ref[...].astype(F32), won_ref[...])[0]
        ua = _nn(_bf(ya_ref[...]), wa_ref[...])
        ub = _nn(_bf(yb), wb_ref[...])
        merged = _sigmoid(ga_ref[...].astype(F32)) * ua + _sigmoid(gb_ref[...].astype(F32)) * ub
        mb = _bf(merged)
        mg_ref[...] = mb
        x1_ref[...] = x_ref[...] + _nn(mb, wo_ref[...])

    rowd = pl.BlockSpec((tm, D_MODEL), lambda i: (i, 0))
    row5 = pl.BlockSpec((tm, 512), lambda i: (i, 0))
    full = lambda a: pl.BlockSpec(a.shape, lambda i: (0,) * a.ndim)
    return pl.pallas_call(
        body, name="mid_fwd", grid=(t // tm,),
        in_specs=[rowd, row5, row5,
                  pl.BlockSpec((tm, 512), lambda i: (i, P_Z // 512)),
                  pl.BlockSpec((tm, D_MODEL), lambda i: (i, 0)),
                  pl.BlockSpec((tm, D_MODEL), lambda i: (i, 1)),
                  full(w_on), full(wa), full(wb), full(w_out)],
        out_specs=[rowd, rowd],
        out_shape=[jax.ShapeDtypeStruct((t, D_MODEL), F32), jax.ShapeDtypeStruct((t, D_MODEL), BF16)],
        compiler_params=_cp(("parallel",), VMEM_LIMIT),
    )(x, y_a, o_b, proj, proj, proj, w_on, wa, wb, w_out)


def mid_bwd(dx1, merged, y_a, o_b, proj, w_on, wa, wb, w_out, tm=256):
    t = dx1.shape[0]
    tm = min(tm, t)

    def body(dx1_ref, mg_ref, ya_ref, ob_ref, z_ref, ga_ref, gb_ref, won_ref, wa_ref, wb_ref, wo_ref,
             dya_ref, dob_ref, dz_ref, dg_ref, dwo_ref, dwa_ref, dwb_ref, dwon_ref):
        @pl.when(pl.program_id(0) == 0)
        def _():
            dwo_ref[...] = jnp.zeros_like(dwo_ref)
            dwa_ref[...] = jnp.zeros_like(dwa_ref)
            dwb_ref[...] = jnp.zeros_like(dwb_ref)
            dwon_ref[...] = jnp.zeros_like(dwon_ref)

        dx1b = _bf(dx1_ref[...])
        dmerged = _nt(dx1b, wo_ref[...])
        dwo_ref[...] += _tn(mg_ref[...], dx1b)
        o = ob_ref[...]
        z = z_ref[...].astype(F32)
        yb, r4, n4, w4, sz, silu = _gated_onorm(o, z, won_ref[...])
        yab, ybb = _bf(ya_ref[...]), _bf(yb)
        ua = _nn(yab, wa_ref[...])
        ub = _nn(ybb, wb_ref[...])
        sa, sb = _sigmoid(ga_ref[...].astype(F32)), _sigmoid(gb_ref[...].astype(F32))
        dua, dub = _bf(dmerged * sa), _bf(dmerged * sb)
        dg_ref[:, 0:D_MODEL] = _bf(dmerged * ua * sa * (1.0 - sa))
        dg_ref[:, D_MODEL:2 * D_MODEL] = _bf(dmerged * ub * sb * (1.0 - sb))
        dwa_ref[...] += _tn(yab, dua)
        dwb_ref[...] += _tn(ybb, dub)
        dya_ref[...] = _nt(dua, wa_ref[...])
        dyb = _nt(dub, wb_ref[...])
        dz_ref[...] = _bf(dyb * (n4 * w4) * (sz * (1.0 + z * (1.0 - sz))))
        dnw = dyb * silu
        dwon = jnp.zeros((1, B_DIM), F32)
        for h in range(B_HEADS):
            sl = slice(h * B_DIM, (h + 1) * B_DIM)
            dxh, dgh = _rms_bwd(dnw[:, sl], won_ref[...], r4[h], n4[:, sl])
            dob_ref[:, sl] = dxh
            dwon = dwon + jnp.sum(dgh, axis=0, keepdims=True)
        dwon_ref[...] += jnp.broadcast_to(dwon, (8, B_DIM))

    rowd = pl.BlockSpec((tm, D_MODEL), lambda i: (i, 0))
    row5 = pl.BlockSpec((tm, 512), lambda i: (i, 0))
    full = lambda a: pl.BlockSpec(a.shape, lambda i: (0,) * a.ndim)
    fixed = lambda shp: pl.BlockSpec(shp, lambda i: (0,) * len(shp))
    return pl.pallas_call(
        body, name="mid_bwd", grid=(t // tm,),
        in_specs=[rowd, rowd, row5, row5,
                  pl.BlockSpec((tm, 512), lambda i: (i, P_Z // 512)),
                  pl.BlockSpec((tm, D_MODEL), lambda i: (i, 0)),
                  pl.BlockSpec((tm, D_MODEL), lambda i: (i, 1)),
                  full(w_on), full(wa), full(wb), full(w_out)],
        out_specs=[row5, row5, row5, pl.BlockSpec((tm, 2 * D_MODEL), lambda i: (i, 0)),
                   fixed((D_MODEL, D_MODEL)), fixed((A_WIDTH, D_MODEL)), fixed((B_WIDTH, D_MODEL)),
                   fixed((8, B_DIM))],
        out_shape=[jax.ShapeDtypeStruct((t, 512), F32), jax.ShapeDtypeStruct((t, 512), F32),
                   jax.ShapeDtypeStruct((t, 512), BF16), jax.ShapeDtypeStruct((t, 2 * D_MODEL), BF16),
           jax.ShapeDtypeStruct((D_MODEL, D_MODEL), F32), jax.ShapeDtypeStruct((A_WIDTH, D_MODEL), F32),
           jax.ShapeDtypeStruct((B_WIDTH, D_MODEL), F32), jax.ShapeDtypeStruct((8, B_DIM), F32)],
        compiler_params=_cp(("arbitrary",), VMEM_LIMIT),
    )(dx1, merged, y_a, o_b, proj, proj, proj, w_on, wa, wb, w_out)


FFN_TF = 1408


def ffn_up(x1, g, w_gu, tm=512, tf=FFN_TF):
    t = x1.shape[0]
    tm = min(tm, t)
    nf = D_FF // tf

    def body(x_ref, g_ref, wg_ref, wu_ref, gate_ref, up_ref, act_ref, h_ref):
        @pl.when(pl.program_id(1) == 0)
        def _():
            r, n = _rms(x_ref[...])
            h_ref[...] = _bf(n * g_ref[...])

        hb = h_ref[...]
        gate = _nn(hb, wg_ref[...])
        up = _nn(hb, wu_ref[...])
        gate_ref[...] = _bf(gate)
        up_ref[...] = _bf(up)
        act_ref[...] = _bf(gate * _sigmoid(gate) * up)

    ff = pl.BlockSpec((tm, tf), lambda i, j: (i, j))
    return pl.pallas_call(
        body, name="ffn_up", grid=(t // tm, nf),
        in_specs=[pl.BlockSpec((tm, D_MODEL), lambda i, j: (i, 0)),
                  pl.BlockSpec((1, D_MODEL), lambda i, j: (0, 0)),
                  pl.BlockSpec((D_MODEL, tf), lambda i, j: (0, j)),
                  pl.BlockSpec((D_MODEL, tf), lambda i, j: (0, nf + j))],
        out_specs=[ff, ff, ff, pl.BlockSpec((tm, D_MODEL), lambda i, j: (i, 0))],
        out_shape=[jax.ShapeDtypeStruct((t, D_FF), BF16)] * 3 + [jax.ShapeDtypeStruct((t, D_MODEL), BF16)],
        compiler_params=_cp(("parallel", "arbitrary"), VMEM_LIMIT),
    )(x1, g, w_gu, w_gu)


def matmul_residual(a, w, res, name, tm=512, tk=FFN_TF):
    t, k = a.shape
    n = w.shape[1]
    tm = min(tm, t)

    def body(a_ref, w_ref, r_ref, o_ref):
        @pl.when(pl.program_id(1) == 0)
        def _():
            o_ref[...] = r_ref[...]

        o_ref[...] += _nn(a_ref[...], w_ref[...])

    return pl.pallas_call(
        body, name=name, grid=(t // tm, k // tk),
        in_specs=[pl.BlockSpec((tm, tk), lambda i, j: (i, j)),
                  pl.BlockSpec((tk, n), lambda i, j: (j, 0)),
                  pl.BlockSpec((tm, n), lambda i, j: (i, 0))],
        out_specs=pl.BlockSpec((tm, n), lambda i, j: (i, 0)),
        out_shape=jax.ShapeDtypeStruct((t, n), F32),
        compiler_params=_cp(("parallel", "arbitrary"), VMEM_LIMIT),
    )(a, w, res)


def ffn_act_bwd(dx2, gate, up, w_down, tm=512, tf=FFN_TF):
    t = dx2.shape[0]
    tm = min(tm, t)

    def body(dx2_ref, gate_ref, up_ref, wd_ref, dgate_ref, dup_ref, dx2b_ref):
        @pl.when(pl.program_id(1) == 0)
        def _():
            dx2b_ref[...] = _bf(dx2_ref[...])

        dact = _nt(dx2b_ref[...], wd_ref[...])
        gt, upv = gate_ref[...].astype(F32), up_ref[...].astype(F32)
        sg = _sigmoid(gt)
        t = dact * sg
        dgate_ref[...] = _bf(t * upv * (1.0 + gt * (1.0 - sg)))
        dup_ref[...] = _bf(t * gt)

    ff = pl.BlockSpec((tm, tf), lambda i, j: (i, j))
    return pl.pallas_call(
        body, name="ffn_act_bwd", grid=(t // tm, D_FF // tf),
        in_specs=[pl.BlockSpec((tm, D_MODEL), lambda i, j: (i, 0)), ff, ff,
                  pl.BlockSpec((tf, D_MODEL), lambda i, j: (j, 0))],
        out_specs=[ff, ff],
        out_shape=[jax.ShapeDtypeStruct((t, D_FF), BF16)] * 2,
        scratch_shapes=[pltpu.VMEM((tm, D_MODEL), BF16)],
        compiler_params=_cp(("parallel", "arbitrary"), VMEM_LIMIT),
    )(dx2, gate, up, w_down)


def tail_fwd_bwd(x2, p, target, g_ple, g_final, w_pg, w_pp, tm=256):
    t = x2.shape[0]
    tm = min(tm, t)

    def body(x_ref, p_ref, t_ref, gp_ref, gf_ref, wpg_ref, wpp_ref,
             dx_ref, dwpg_ref, dwpp_ref, dgp_ref, dgf_ref, loss_ref):
        @pl.when(pl.program_id(0) == 0)
        def _():
            dwpg_ref[...] = jnp.zeros_like(dwpg_ref)
            dwpp_ref[...] = jnp.zeros_like(dwpp_ref)
            dgp_ref[...] = jnp.zeros_like(dgp_ref)
            dgf_ref[...] = jnp.zeros_like(dgf_ref)
            loss_ref[...] = jnp.zeros_like(loss_ref)

        x2v = x_ref[...]
        gp, gf = gp_ref[...], gf_ref[...]
        r3, n3 = _rms(x2v)
        h3b = _bf(n3 * gp)
        pb = _bf(p_ref[...])
        pg = _sigmoid(_nn(h3b, wpg_ref[...]))
        pp = _nn(pb, wpp_ref[...])
        x3 = x2v + pg * pp
        r4, n4 = _rms(x3)
        err = n4 * gf - t_ref[...]
        part = 0.5 * jnp.sum(jnp.sum(err * err, axis=1, keepdims=True), axis=0, keepdims=True) / D_MODEL
        loss_ref[...] += jnp.broadcast_to(part, (8, 128))
        dy = err * (1.0 / D_MODEL)
        dx3, dgf = _rms_bwd(dy, gf, r4, n4)
        dgf_ref[...] += jnp.broadcast_to(jnp.sum(dgf, axis=0, keepdims=True), (8, D_MODEL))
        dzp = _bf(dx3 * pp * pg * (1.0 - pg))
        dpp = _bf(dx3 * pg)
        dwpg_ref[...] += _tn(h3b, dzp)
        dwpp_ref[...] += _tn(pb, dpp)
        dh3 = _nt(dzp, wpg_ref[...])
        dx, dgp = _rms_bwd(dh3, gp, r3, n3)
        dgp_ref[...] += jnp.broadcast_to(jnp.sum(dgp, axis=0, keepdims=True), (8, D_MODEL))
        dx_ref[...] = dx3 + dx

    rowd = pl.BlockSpec((tm, D_MODEL), lambda i: (i, 0))
    fixed = lambda shp: pl.BlockSpec(shp, lambda i: (0,) * len(shp))
    return pl.pallas_call(
        body, name="tail_fwd_bwd", grid=(t // tm,),
        in_specs=[rowd, pl.BlockSpec((tm, PLE_DIM), lambda i: (i, 0)), rowd,
                  fixed((1, D_MODEL)), fixed((1, D_MODEL)), fixed((D_MODEL, D_MODEL)), fixed((PLE_DIM, D_MODEL))],
        out_specs=[rowd, fixed((D_MODEL, D_MODEL)), fixed((PLE_DIM, D_MODEL)),
                   fixed((8, D_MODEL)), fixed((8, D_MODEL)), fixed((8, 128))],
        out_shape=[jax.ShapeDtypeStruct((t, D_MODEL), F32), jax.ShapeDtypeStruct((D_MODEL, D_MODEL), F32),
                   jax.ShapeDtypeStruct((PLE_DIM, D_MODEL), F32), jax.ShapeDtypeStruct((8, D_MODEL), F32),
                   jax.ShapeDtypeStruct((8, D_MODEL), F32), jax.ShapeDtypeStruct((8, 128), F32)],
        compiler_params=_cp(("arbitrary",), VMEM_LIMIT),
    )(x2, p, target, g_ple, g_final, w_pg, w_pp)


def in_proj_bwd(pieces, weights, x, dx1, g, name="in_proj_bwd", tm=256):
    t = x.shape[0]
    tm = min(tm, t)
    k = len(pieces)
    assert all(c0 % wd == 0 and w0 % wd == 0 for (_, c0, wd), (_, w0) in zip(pieces, weights))

    def body(*refs):
        p_refs, w_refs = refs[:k], refs[k:2 * k]
        x_ref, dx1_ref, g_ref, dx_ref, dg_ref = refs[2 * k:]

        @pl.when(pl.program_id(0) == 0)
        def _():
            dg_ref[...] = jnp.zeros_like(dg_ref)

        dh = _nt(_bf(p_refs[0][...]), w_refs[0][...])
        for pr, wr in zip(p_refs[1:], w_refs[1:]):
            dh = dh + _nt(_bf(pr[...]), wr[...])
        r, n = _rms(x_ref[...])
        dx, dgc = _rms_bwd(dh, g_ref[...], r, n)
        dx_ref[...] = dx1_ref[...] + dx
        dg_ref[...] += jnp.broadcast_to(jnp.sum(dgc, axis=0, keepdims=True), (8, D_MODEL))

    rowd = pl.BlockSpec((tm, D_MODEL), lambda i: (i, 0))
    return pl.pallas_call(
        body, name=name, grid=(t // tm,),
        in_specs=[pl.BlockSpec((tm, wd), functools.partial(lambda i, cb: (i, cb), cb=c0 // wd))
                  for _, c0, wd in pieces]
        + [pl.BlockSpec((w.shape[0], wd), functools.partial(lambda i, cb: (0, cb), cb=w0 // wd))
           for (w, w0), (_, _, wd) in zip(weights, pieces)]
        + [rowd, rowd, pl.BlockSpec((1, D_MODEL), lambda i: (0, 0))],
        out_specs=[rowd, pl.BlockSpec((8, D_MODEL), lambda i: (0, 0))],
        out_shape=[jax.ShapeDtypeStruct((t, D_MODEL), F32), jax.ShapeDtypeStruct((8, D_MODEL), F32)],
        compiler_params=_cp(("arbitrary",), VMEM_LIMIT),
    )(*[a for a, _, _ in pieces], *[w for w, _ in weights], x, dx1, g)


def adamw(w, g, m, v, name, rows_cap=256, dep=None):
    lead = w.shape[:-2]
    r, c = w.shape[-2:]
    tr = r
    for cand in range(8, min(r, rows_cap) + 1, 8):
        if r % cand == 0:
            tr = cand

    def body(w_ref, g_ref, m_ref, v_ref, *rest):
        d_ref, mo_ref, vo_ref = rest[-3:]
        gv = g_ref[...]
        mn = ADAM_B1 * m_ref[...] + (1.0 - ADAM_B1) * gv
        vn = ADAM_B2 * v_ref[...] + (1.0 - ADAM_B2) * (gv * gv)
        m_hat = mn / (1.0 - ADAM_B1 ** ADAM_STEP)
        v_hat = vn / (1.0 - ADAM_B2 ** ADAM_STEP)
        d_ref[...] = -ADAM_LR * (m_hat / (jnp.sqrt(v_hat) + ADAM_EPS) + ADAM_WD * w_ref[...])
        mo_ref[...] = mn
        vo_ref[...] = vn

    spec = pl.BlockSpec((None,) * len(lead) + (tr, c), lambda i: (0,) * len(lead) + (i, 0))
    extra = [] if dep is None else [dep]
    return pl.pallas_call(
        body, name=name, grid=(r // tr,),
        in_specs=[spec] * 4 + [pl.BlockSpec((8, 128), lambda i: (0, 0))] * len(extra), out_specs=[spec] * 3,
        out_shape=[jax.ShapeDtypeStruct(w.shape, F32)] * 3,
        compiler_params=_cp(("parallel",), VMEM_LIMIT),
    )(w, g.reshape(w.shape), m, v, *extra)


def _w_in_shards(dwp):
    cs = D_IN // N_CHIPS
    regions = ((0, SPLIT_Z, P_QA), (SPLIT_Z, SPLIT_Z + 8, P_BD - SPLIT_Z), (SPLIT_Z + 8, D_IN, -(SPLIT_Z + 8)))

    def original(lo, hi):
        parts = [dwp[:, max(lo, a) + off:min(hi, e) + off] for a, e, off in regions if max(lo, a) < min(hi, e)]
        return parts[0] if len(parts) == 1 else jnp.concatenate(parts, axis=1)

    return jnp.stack([original(s * cs, (s + 1) * cs) for s in range(N_CHIPS)])


class Standalone:
    def __init__(self, later_weights):
        self.later_weights = later_weights

    def begin(self, *a):
        return 0.0

    forward = exchange = join = begin

    def finish(self, after):
        return self.later_weights


def local_step(x3d, p3d, target3d, g4, small, later, early):
    b, s, _ = x3d.shape
    t = b * s
    x = x3d.reshape(t, D_MODEL)
    p = p3d.reshape(t, PLE_DIM)
    target = target3d.reshape(t, D_MODEL)
    cut = SPLIT_Z - 2 * (D_IN // N_CHIPS)
    w_inp = jnp.concatenate([g4[2][:, cut + 8:], g4[3], g4[0], g4[1], g4[2][:, :cut], g4[2][:, cut:cut + 8],
                             jnp.zeros((D_MODEL, 120), BF16)], axis=1)
    al_row = jnp.pad(small["a_log"].reshape(1, B_HEADS), ((0, 0), (0, 128 - B_HEADS)))
    dtb_row = jnp.pad(small["dt_bias"].reshape(1, B_HEADS), ((0, 0), (0, 128 - B_HEADS)))
    conv_w8 = jnp.pad(small["conv_w"].reshape(CONV_K, CONV_CH), ((0, 8 - CONV_K), (0, 0)))
    w_on = small["w_onorm"].reshape(1, B_DIM)
    g_mix, g_ffn = small["g_mix"].reshape(1, D_MODEL), small["g_ffn"].reshape(1, D_MODEL)
    g_ple, g_final = small["g_ple"].reshape(1, D_MODEL), small["g_final"].reshape(1, D_MODEL)
    bias_band = make_bias_band(small["rel_bias"].reshape(A_HEADS, N_REL))

    tok = later.begin()
    proj, h1, bd32 = rms_matmul(x, g_mix + tok, w_inp, "in_proj", tm=1024, tn_cap=1152)
    y_a, lse = attn_fwd(proj, bias_band, b, s)
    tok = later.forward(lse)
    c = conv_fwd(proj, conv_w8 + tok, b, s)
    u, wk, qg, kdec, pm, egl, tmat = dn_prep(c, bd32, al_row, dtb_row, b, s)
    o_b, states = dn_scan_fwd(u, wk, qg, kdec, pm, egl, b, s)
    wts = later.finish(o_b)
    x1, merged = mid_fwd(x, y_a, o_b, proj, w_on, wts["w_branch_a"], wts["w_branch_b"], wts["w_out"])
    gate, up, act, h2 = ffn_up(x1, g_ffn, wts["w_gate_up"])
    x2 = matmul_residual(act, wts["w_down"], x1, "ffn_down")

    dx2, dw_pg, dw_pp, dg_ple, dg_final, loss = tail_fwd_bwd(
        x2, p, target, g_ple, g_final, wts["w_ple_gate"], wts["w_ple_proj"])
    dgate, dup = ffn_act_bwd(dx2, gate, up, wts["w_down"])
    w_gu = wts["w_gate_up"]
    dx1, dg_ffn = in_proj_bwd([(dgate, 0, D_FF), (dup, 0, D_FF)], [(w_gu, 0), (w_gu, D_FF)], x1, dx2, g_ffn,
                              name="ffn_in_bwd")
    dw_down = matmul_tn(act, dx2, "dw_down")
    dw_gu = matmul_tn(h2, dgate, "dw_gate", width=2 * D_FF, tiles_major=True)
    dw_gu = matmul_tn(h2, dup, "dw_up", into=dw_gu, col0=D_FF, width=2 * D_FF, tiles_major=True)
    dy_a, do_b, dz, dgates, dw_out, dwa, dwb, dw_on = mid_bwd(
        dx1, merged, y_a, o_b, proj, w_on, wts["w_branch_a"], wts["w_branch_b"], wts["w_out"])
    tok = early.begin(dict(w_branch_a=dwa, w_branch_b=dwb, w_out=dw_out, w_gate_up=dw_gu, w_down=dw_down,
                           w_ple_gate=dw_pg, w_ple_proj=dw_pp))
    ddw, ddwk, ddqg, ddkdec, ddp, ddegl = dn_scan_bwd(u, wk, qg, kdec, pm, egl + tok, states, do_b, b, s)
    tok = early.exchange(ddegl)
    dc3, dbd, dal, ddtb = dn_post_bwd(c, bd32, al_row + tok, dtb_row, tmat, ddw, ddwk, ddqg, ddkdec, ddp, ddegl, b, s)
    dconv, dconv_w = conv_bwd(proj, conv_w8, dc3, b, s)
    dqa, dka, dva, dbt, dbf = attn_bwd(proj, bias_band, y_a, lse, dy_a, b, s)
    tok = early.join(dqa)
    d_rel = bias_band_grad(dbt, dbf)

    pieces = [dgates, dqa, dka, dva, dconv, dz, dbd]
    bounds = [0, 2048, 2560, 3072, 3584, 5120, 5632, 5760]
    windows = [(dgates, 0, 2048), (dqa, 0, 512), (dka, 0, 512), (dva, 0, 512), (dconv, 0, 512), (dconv, 512, 512),
               (dconv, 1024, 512), (dz, 0, 512), (dbd, 0, 128)]
    w_cols = [0, P_QA, P_KA, P_VA, P_CONV, P_CONV + 512, P_CONV + 1024, P_Z, P_BD]
    dx, dg_mix = in_proj_bwd(windows, [(w_inp, c0) for c0 in w_cols], x, dx1, g_mix + tok)
    dwp = None
    for k, pc in enumerate(pieces):
        dwp = matmul_tn(h1, pc, "dw_in_%d" % k, into=dwp, col0=bounds[k], width=P_WIDTH)
    reduced_early = early.finish(dwp)
    dw_in = _w_in_shards(dwp)

    grads = dict(w_in=dw_in, w_branch_a=dwa, w_branch_b=dwb, w_out=dw_out, w_gate_up=dw_gu, w_down=dw_down,
                 w_ple_gate=dw_pg, w_ple_proj=dw_pp)
    small_grads = dict(g_mix=dg_mix[0], g_ffn=dg_ffn[0], g_ple=dg_ple[0], g_final=dg_final[0],
                       conv_w=dconv_w[:CONV_K].reshape(-1), rel_bias=d_rel.reshape(-1), w_onorm=dw_on[0],
                       a_log=dal[0, :B_HEADS], dt_bias=ddtb[0, :B_HEADS], loss=loss[0, :1])
    return dx.reshape(b, s, D_MODEL), grads, small_grads, reduced_early


BIG = (("w_in", (D_MODEL, D_IN), 1), ("w_branch_a", (A_WIDTH, D_MODEL), 1), ("w_branch_b", (B_WIDTH, D_MODEL), 1),
       ("w_out", (D_MODEL, D_MODEL), 0), ("w_gate_up", (D_MODEL, 2 * D_FF), 1), ("w_down", (D_FF, D_MODEL), 0),
       ("w_ple_gate", (D_MODEL, D_MODEL), 0), ("w_ple_proj", (PLE_DIM, D_MODEL), 1))
N_CHIPS = 4
FIRST_WEIGHTS = ("w_in",)
LATER_WEIGHTS = ("w_branch_a", "w_branch_b", "w_out", "w_gate_up", "w_down", "w_ple_gate", "w_ple_proj")
LATE_GRADS = ("w_in",)
EARLY_GRADS = ("w_branch_a", "w_branch_b", "w_out", "w_gate_up", "w_down", "w_ple_gate", "w_ple_proj")


def _items(names):
    return [it for it in BIG if it[0] in names]


def _shard_shape(shape, axis):
    return (shape[0] // N_CHIPS, shape[1]) if axis == 0 else (shape[0], shape[1] // N_CHIPS)


def _width_groups(names):
    groups = {}
    for n, shape, axis in _items(names):
        rs, cs = _shard_shape(shape, axis)
        groups.setdefault(cs, []).append((n, rs))
    return sorted(groups.items())


def grad_buffers(grads, names):
    info = {n: (shape, axis) for n, shape, axis in _items(names)}
    bufs = []
    for cs, members in _width_groups(names):
        segs = []
        for n, rs in members:
            g = grads[n].astype(BF16)
            if g.ndim == 2:
                g = (g.reshape(N_CHIPS, rs, cs) if info[n][1] == 0
                     else jnp.transpose(g.reshape(rs, N_CHIPS, cs), (1, 0, 2)))
            segs.append(g)
        bufs.append(segs[0] if len(segs) == 1 else jnp.concatenate(segs, axis=1))
    return bufs


def split_buffers(reduced, names):
    out = {}
    for (cs, members), buf in zip(_width_groups(names), reduced):
        r0 = 0
        for n, rs in members:
            out[n] = buf[r0:r0 + rs]
            r0 += rs
    return out


def _place():
    return lax.axis_index("x"), lax.axis_index("y"), lax.axis_index("c")


ANY = pl.BlockSpec(memory_space=pl.ANY)


def _gathered_shape(item):
    n, shape, _ = item
    return (N_CHIPS,) + _shard_shape(shape, 1) if n == "w_in" else shape


def _gather_block(o_ref, item, cx, cy, hf):
    n, shape, axis = item
    rs, cs = _shard_shape(shape, axis)
    hr = rs // 2
    ci = 2 * cx + cy
    if n == "w_in":
        return o_ref.at[ci, pl.ds(pl.multiple_of(hf * hr, 16), hr), :]
    if axis == 0:
        return o_ref.at[pl.ds(pl.multiple_of(ci * rs + hf * hr, 16), hr), :]
    return o_ref.at[pl.ds(pl.multiple_of(hf * hr, 16), hr), pl.ds(pl.multiple_of(ci * cs, 128), cs)]


def _own_half(w_ref, item, c):
    hr = _shard_shape(item[1], item[2])[0] // 2
    return w_ref.at[pl.ds(pl.multiple_of(c * hr, 16), hr), :]


def _gather_slot(o_ref, item, cx, cy):
    n, shape, axis = item
    rs, cs = _shard_shape(shape, axis)
    ci = 2 * cx + cy
    if n == "w_in":
        return o_ref.at[ci]
    if axis == 0:
        return o_ref.at[pl.ds(pl.multiple_of(ci * rs, 16), rs), :]
    return o_ref.at[:, pl.ds(pl.multiple_of(ci * cs, 128), cs)]


def _other_chips(x, y):
    return [(1 - x, y), (x, 1 - y), (1 - x, 1 - y)]


def allgather_weights(shards, names):
    items = _items(names)
    nw = len(items)

    def body(*refs):
        w_refs, o_refs = refs[:nw], refs[nw:2 * nw]
        send_sems, recv_sems = refs[2 * nw:]
        x, y, c = _place()
        sibling = (x, y, 1 - c)
        chips = _other_chips(x, y)

        def copy(k, src, dst, to):
            return pltpu.make_async_remote_copy(src_ref=src, dst_ref=dst, send_sem=send_sems.at[k],
                                                recv_sem=recv_sems.at[k], device_id=to, device_id_type=MESH)

        def blk(i, cx, cy, hf):
            return _gather_block(o_refs[i], items[i], cx, cy, hf)

        def my_half(i):
            return _own_half(w_refs[i], items[i], c)

        def own(i):
            return _gather_slot(o_refs[i], items[i], x, y)

        first = [copy(7 * i + j, my_half(i), blk(i, x, y, c), (*chip_, c))
                 for i in range(nw) for j, chip_ in enumerate(chips)]
        first += [copy(7 * i + 6, w_refs[i], own(i), sibling) for i in range(nw)]
        for cp in first:
            cp.start()
        passed = []
        for i in range(nw):
            for j, chip_ in enumerate(chips):
                copy(7 * i + j, my_half(i), blk(i, *chip_, c), (*chip_, c)).wait_recv()
                fwd = copy(7 * i + 3 + j, blk(i, *chip_, c), blk(i, *chip_, c), sibling)
                fwd.start()
                passed.append(fwd)
        for i in range(nw):
            for j, chip_ in enumerate(chips):
                copy(7 * i + 3 + j, my_half(i), blk(i, *chip_, 1 - c), sibling).wait_recv()
            copy(7 * i + 6, w_refs[i], own(i), sibling).wait_recv()
        for cp in first + passed:
            cp.wait_send()

    outs = pl.pallas_call(
        body, name="allgather_weights",
        in_specs=[ANY] * nw, out_specs=[ANY] * nw,
        out_shape=[jax.ShapeDtypeStruct(_gathered_shape(it), BF16) for it in items],
        scratch_shapes=[pltpu.SemaphoreType.DMA((7 * nw,)), pltpu.SemaphoreType.DMA((7 * nw,))],
    )(*[shards[it[0]] for it in items])
    return {it[0]: o for it, o in zip(items, outs)}


HBM_SPEC = pl.BlockSpec(memory_space=pltpu.HBM)
SEM_SPEC = pl.BlockSpec(memory_space=pltpu.SEMAPHORE)
EFFECT = pltpu.SideEffectType.DATAFLOW_SIDE_EFFECTING


def _in_hbm(a):
    return pltpu.with_memory_space_constraint(a, pltpu.HBM)


def copies_start(name, bufs, ncopies, plan):
    nb = len(bufs)

    def body(*refs):
        in_refs, send_sems, recv_sems, token = refs[:nb], refs[nb], refs[nb + 1], refs[-1]
        for k, (src, dst, to) in enumerate(plan(in_refs)):
            pltpu.make_async_remote_copy(src_ref=src, dst_ref=dst, send_sem=send_sems.at[k],
                                         recv_sem=recv_sems.at[k], device_id=to, device_id_type=MESH).start()
        token[...] = jnp.zeros_like(token)

    outs = pl.pallas_call(
        body, name=name,
        in_specs=[HBM_SPEC] * nb,
        out_specs=(SEM_SPEC, SEM_SPEC, *[HBM_SPEC] * nb, pl.BlockSpec(memory_space=pltpu.VMEM)),
        out_shape=(pltpu.SemaphoreType.DMA((ncopies,)), pltpu.SemaphoreType.DMA((ncopies,)),
                   *[pltpu.HBM(b.shape, b.dtype) for b in bufs], jax.ShapeDtypeStruct((8, 128), F32)),
        input_output_aliases={i: 2 + i for i in range(nb)},
        compiler_params=pltpu.CompilerParams(has_side_effects=EFFECT),
    )(*[_in_hbm(b) for b in bufs])
    return outs[0], outs[1], list(outs[2:2 + nb]), outs[-1][0, 0]


def copies_wait(name, send_sems, recv_sems, bufs, after, plan):
    nb = len(bufs)

    def body(*refs):
        in_refs, s_sems, r_sems = refs[:nb], refs[nb], refs[nb + 1]
        for k, (src, dst, to) in enumerate(plan(in_refs)):
            cp = pltpu.make_async_remote_copy(src_ref=src, dst_ref=dst, send_sem=s_sems.at[k],
                                              recv_sem=r_sems.at[k], device_id=to, device_id_type=MESH)
            cp.wait_send()
            cp.wait_recv()

    return list(pl.pallas_call(
        body, name=name,
        in_specs=[HBM_SPEC] * nb + [SEM_SPEC, SEM_SPEC, ANY],
        out_specs=tuple([HBM_SPEC] * nb),
        out_shape=tuple(pltpu.HBM(b.shape, b.dtype) for b in bufs),
        input_output_aliases={i: i for i in range(nb)},
        compiler_params=pltpu.CompilerParams(has_side_effects=EFFECT),
    )(*bufs, send_sems, recv_sems, after))


def _landing(shape, dtype):
    return _in_hbm(lax.empty(shape, dtype))


class LaterWeights:
    def __init__(self, shards):
        self.items = _items(LATER_WEIGHTS)
        self.shards = shards
        self.nw = len(self.items)

    def _ici_plan(self, refs):
        x, y, c = _place()
        w_refs, o_refs = refs[:self.nw], refs[self.nw:]
        plan = [(_own_half(w_refs[i], it, c), _gather_block(o_refs[i], it, x, y, c), (*chip_, c))
                for i, it in enumerate(self.items) for chip_ in _other_chips(x, y)]
        return plan + [(w_refs[i], _gather_slot(o_refs[i], it, x, y), (x, y, 1 - c))
                       for i, it in enumerate(self.items)]

    def _d2d_plan(self, refs):
        x, y, c = _place()
        return [(_gather_block(refs[i], it, *chip_, c), _gather_block(refs[i], it, *chip_, c), (x, y, 1 - c))
                for i, it in enumerate(self.items) for chip_ in _other_chips(x, y)]

    def _d2d_wait_plan(self, refs):
        x, y, c = _place()
        return [(_gather_block(refs[i], it, *chip_, c), _gather_block(refs[i], it, *chip_, 1 - c), (x, y, 1 - c))
                for i, it in enumerate(self.items) for chip_ in _other_chips(x, y)]

    def _ici_wait_plan(self, refs):
        x, y, c = _place()
        w_refs, o_refs = refs[:self.nw], refs[self.nw:]
        plan = [(_own_half(w_refs[i], it, c), _gather_block(o_refs[i], it, *chip_, c), (*chip_, c))
                for i, it in enumerate(self.items) for chip_ in _other_chips(x, y)]
        return plan + [(w_refs[i], _gather_slot(o_refs[i], it, x, y), (x, y, 1 - c))
                       for i, it in enumerate(self.items)]

    def begin(self):
        srcs = [self.shards[it[0]] for it in self.items]
        lands = [_landing(_gathered_shape(it), BF16) for it in self.items]
        self.s1, self.r1, self.b1, tok = copies_start("gather_ici_start", srcs + lands, 4 * self.nw, self._ici_plan)
        return tok

    def forward(self, after):
        b1 = copies_wait("gather_ici_wait", self.s1, self.r1, self.b1, after, self._ici_wait_plan)
        self.s2, self.r2, self.b2, tok = copies_start("gather_d2d_start", b1[self.nw:], 3 * self.nw, self._d2d_plan)
        return tok

    def finish(self, after):
        outs = copies_wait("gather_d2d_wait", self.s2, self.r2, self.b2, after, self._d2d_wait_plan)
        return {it[0]: o for it, o in zip(self.items, outs)}


def small_allreduce(v, name):
    r = v.shape[0]

    def body(v_ref, o_ref, buf, send_sems, recv_sems):
        x, y, c = _place()
        me = 4 * x + 2 * y + c
        buf[me] = v_ref[...]
        flips = [(fx, fy, fc) for fx in (0, 1) for fy in (0, 1) for fc in (0, 1)][1:]
        peers = [((1 - x) if fx else x, (1 - y) if fy else y, (1 - c) if fc else c) for fx, fy, fc in flips]

        def copy(k, slot, to):
            return pltpu.make_async_remote_copy(src_ref=v_ref, dst_ref=buf.at[slot], send_sem=send_sems.at[k],
                                                recv_sem=recv_sems.at[k], device_id=to, device_id_type=MESH)

        sends = [copy(k, me, peer) for k, peer in enumerate(peers)]
        for cp in sends:
            cp.start()
        for k, (px, py, pc) in enumerate(peers):
            copy(k, 4 * px + 2 * py + pc, (px, py, pc)).wait_recv()
        for cp in sends:
            cp.wait_send()
        acc = buf[0]
        for d in range(1, 8):
            acc = acc + buf[d]
        o_ref[...] = acc

    return pl.pallas_call(
        body, name=name,
        in_specs=[pl.BlockSpec(memory_space=pltpu.VMEM)], out_specs=pl.BlockSpec(memory_space=pltpu.VMEM),
        out_shape=jax.ShapeDtypeStruct((r, 128), F32),
        scratch_shapes=[pltpu.VMEM((8, r, 128), F32), pltpu.SemaphoreType.DMA((7,)), pltpu.SemaphoreType.DMA((7,))],
    )(v)


def add_halves(g, other, place):
    half, wd = other.shape[1:]
    tr = _tile_rows(half, wd)
    nblk = half // tr

    def body(pref, g0, g1, g2, g3, o0, o1, o2, o3, pf_ref, pb_ref):
        f = lambda r: r[...].astype(F32)
        pf_ref[...] = f(g0) + f(o0)
        pb_ref[0] = _bf(f(g1) + f(o1))
        pb_ref[1] = _bf(f(g2) + f(o2))
        pb_ref[2] = _bf(f(g3) + f(o3))

    gspec = lambda k: pl.BlockSpec((None, tr, wd), lambda i, pr: ((pr[0] + k) % N_CHIPS, pr[1] * nblk + i, 0))
    ospec = lambda k: pl.BlockSpec((None, tr, wd), lambda i, pr: ((pr[0] + k) % N_CHIPS, i, 0))
    return pl.pallas_call(
        body, name="add_halves",
        grid_spec=pltpu.PrefetchScalarGridSpec(
            num_scalar_prefetch=1, grid=(nblk,),
            in_specs=[gspec(0), gspec(1), gspec(2), gspec(3), ospec(0), ospec(1), ospec(2), ospec(3)],
            out_specs=[pl.BlockSpec((tr, wd), lambda i, pr: (i, 0)),
                       pl.BlockSpec((3, tr, wd), lambda i, pr: (0, i, 0))]),
        out_shape=[jax.ShapeDtypeStruct((half, wd), F32), jax.ShapeDtypeStruct((3, half, wd), BF16)],
        compiler_params=_cp(("parallel",), VMEM_LIMIT),
    )(place, g, g, g, g, other, other, other, other)


def _tile_rows(n, width):
    best = 16
    for t in range(16, max(16, (384 * 1024) // width) + 1, 16):
        if n % t == 0:
            best = t
    assert n % best == 0
    return best


def add_partials(pf, got, place):
    half, wd = pf.shape
    tr = _tile_rows(half, wd)

    def body(pref, pf_ref, got_ref, o_ref):
        o_ref[...] = ((pf_ref[...] + got_ref[0].astype(F32)) + got_ref[1].astype(F32)) + got_ref[2].astype(F32)

    return pl.pallas_call(
        body, name="add_partials",
        grid_spec=pltpu.PrefetchScalarGridSpec(
            num_scalar_prefetch=1, grid=(half // tr,),
            in_specs=[pl.BlockSpec((tr, wd), lambda i, pr: (i, 0)),
                      pl.BlockSpec((3, tr, wd), lambda i, pr: (0, i, 0))],
            out_specs=pl.BlockSpec((None, tr, wd), lambda i, pr: (pr[1], i, 0))),
        out_shape=jax.ShapeDtypeStruct((2, half, wd), F32),
        compiler_params=_cp(("parallel",), VMEM_LIMIT),
    )(place, pf, got)


class GradReduce:
    def __init__(self, place, names, tag):
        self.place, self.names, self.tag = place, names, tag
        self.nb = len(_width_groups(names))

    def _swap_plan(self, refs):
        x, y, c = _place()
        plan = []
        for g_ref, o_ref in zip(refs[:self.nb], refs[self.nb:]):
            half = o_ref.shape[1]
            plan.append((g_ref.at[:, pl.ds(pl.multiple_of((1 - c) * half, 16), half), :], o_ref, (x, y, 1 - c)))
        return plan

    def _exchange_plan(self, refs):
        x, y, c = _place()
        me = 2 * x + y
        return [(p_ref.at[k - 1], o_ref.at[k - 1], (((me + k) % N_CHIPS) // 2, ((me + k) % N_CHIPS) % 2, c))
                for p_ref, o_ref in zip(refs[:self.nb], refs[self.nb:]) for k in range(1, N_CHIPS)]

    def _join_plan(self, refs):
        x, y, c = _place()
        return [(r.at[c], r.at[c], (x, y, 1 - c)) for r in refs]

    def _join_wait_plan(self, refs):
        x, y, c = _place()
        return [(r.at[c], r.at[1 - c], (x, y, 1 - c)) for r in refs]

    def begin(self, grads):
        gs = grad_buffers(grads, self.names)
        lands = [_landing((N_CHIPS, g.shape[1] // 2, g.shape[2]), BF16) for g in gs]
        self.s1, self.r1, self.b1, tok = copies_start(self.tag + "_swap_start", gs + lands, self.nb, self._swap_plan)
        return tok

    def exchange(self, after):
        b1 = copies_wait(self.tag + "_swap_wait", self.s1, self.r1, self.b1, after, self._swap_plan)
        sums = [add_halves(g, other, self.place) for g, other in zip(b1[:self.nb], b1[self.nb:])]
        self.pfs = [pf for pf, _ in sums]
        pbs = [pb for _, pb in sums]
        lands = [_landing(pb.shape, BF16) for pb in pbs]
        self.s2, self.r2, self.b2, tok = copies_start(self.tag + "_exchange_start", pbs + lands, 3 * self.nb,
                                                      self._exchange_plan)
        return tok

    def join(self, after):
        b2 = copies_wait(self.tag + "_exchange_wait", self.s2, self.r2, self.b2, after, self._exchange_plan)
        boths = [add_partials(pf, got, self.place) for pf, got in zip(self.pfs, b2[self.nb:])]
        self.s3, self.r3, self.b3, tok = copies_start(self.tag + "_join_start", boths, self.nb, self._join_plan)
        return tok

    def finish(self, after):
        boths = copies_wait(self.tag + "_join_wait", self.s3, self.r3, self.b3, after, self._join_wait_plan)
        return split_buffers([b.reshape(-1, b.shape[2]) for b in boths], self.names)


SMALL = (("g_mix", D_MODEL), ("g_ffn", D_MODEL), ("g_ple", D_MODEL), ("g_final", D_MODEL),
         ("conv_w", CONV_K * CONV_CH), ("rel_bias", A_HEADS * N_REL), ("w_onorm", B_DIM),
         ("a_log", B_HEADS), ("dt_bias", B_HEADS), ("loss", 1))


def _pad128(v):
    v = v.reshape(-1)
    return jnp.pad(v, (0, -v.shape[0] % 128))


def pack_small(d, names, rows):
    flat = jnp.concatenate([_pad128(d[n]) for n in names]).reshape(-1, 128)
    return jnp.pad(flat, ((0, rows - flat.shape[0]), (0, 0)))


def unpack_small(flat, names_sizes):
    out, r0 = {}, 0
    v = flat.reshape(-1)
    for n, size in names_sizes:
        out[n] = v[r0:r0 + size]
        r0 += -(-size // 128) * 128
    return out


def kernel(x, p, g_mix, w_in, conv_w, a_log, dt_bias, rel_bias, w_onorm, w_branch_a, w_branch_b, w_out, g_ffn, w_gate_up, w_down, g_ple, w_ple_gate, w_ple_proj, g_final, loss_target, m_g_mix, m_w_in, m_conv_w, m_a_log, m_dt_bias, m_rel_bias, m_w_onorm, m_w_branch_a, m_w_branch_b, m_w_out, m_g_ffn, m_w_gate_up, m_w_down, m_g_ple, m_w_ple_gate, m_w_ple_proj, m_g_final, v_g_mix, v_w_in, v_conv_w, v_a_log, v_dt_bias, v_rel_bias, v_w_onorm, v_w_branch_a, v_w_branch_b, v_w_out, v_g_ffn, v_w_gate_up, v_w_down, v_g_ple, v_w_ple_gate, v_w_ple_proj, v_g_final):
    names = ["g_mix", "w_in", "conv_w", "a_log", "dt_bias", "rel_bias", "w_onorm", "w_branch_a", "w_branch_b",
             "w_out", "g_ffn", "w_gate_up", "w_down", "g_ple", "w_ple_gate", "w_ple_proj", "g_final"]
    w = dict(zip(names, [g_mix, w_in, conv_w, a_log, dt_bias, rel_bias, w_onorm, w_branch_a, w_branch_b, w_out,
                         g_ffn, w_gate_up, w_down, g_ple, w_ple_gate, w_ple_proj, g_final]))
    m = dict(zip(names, [m_g_mix, m_w_in, m_conv_w, m_a_log, m_dt_bias, m_rel_bias, m_w_onorm, m_w_branch_a,
                         m_w_branch_b, m_w_out, m_g_ffn, m_w_gate_up, m_w_down, m_g_ple, m_w_ple_gate,
                         m_w_ple_proj, m_g_final]))
    v = dict(zip(names, [v_g_mix, v_w_in, v_conv_w, v_a_log, v_dt_bias, v_rel_bias, v_w_onorm, v_w_branch_a,
                         v_w_branch_b, v_w_out, v_g_ffn, v_w_gate_up, v_w_down, v_g_ple, v_w_ple_gate,
                         v_w_ple_proj, v_g_final]))
    xi, yi, ci = _place()
    chip = 2 * xi + yi
    big_names = [n for n, _, _ in BIG]

    shards2d = {n: w[n].reshape(w[n].shape[-2:]) for n in big_names}
    shards_bf = {n: a.astype(BF16) for n, a in shards2d.items()}
    g4 = allgather_weights(shards_bf, FIRST_WEIGHTS)["w_in"]
    place = jnp.stack([chip, ci]).astype(jnp.int32)
    conv_sh = jnp.where(ci == 0, w["conv_w"].reshape(CONV_K, CONV_CH // N_CHIPS), 0.0)
    conv_slots = lax.dynamic_update_slice(jnp.zeros((N_CHIPS, CONV_K, CONV_CH // N_CHIPS), F32), conv_sh[None],
                                          (chip, 0, 0))
    conv_all = small_allreduce(conv_slots.reshape(-1, 128), "gather_conv_w")
    conv_full = jnp.transpose(conv_all.reshape(N_CHIPS, CONV_K, CONV_CH // N_CHIPS), (1, 0, 2)).reshape(CONV_K, CONV_CH)
    small = {n: w[n] for n in names if n not in big_names}
    small["conv_w"] = conv_full

    grad_x, grads, small_grads, reduced_early = local_step(
        x, p[0], loss_target, g4, small, LaterWeights(shards_bf), GradReduce(place, EARLY_GRADS, "grads"))

    late = GradReduce(place, LATE_GRADS, "late")
    tok = late.begin(grads)
    small_names = [n for n, _ in SMALL]
    small_grads["loss"] = small_grads["loss"] + tok
    red_flat = small_allreduce(pack_small(small_grads, small_names, 112), "allreduce_small")
    red = unpack_small(red_flat, SMALL)
    dep = jnp.full((8, 128), late.exchange(red_flat), F32)
    gshard = dict(reduced_early)
    loss = red["loss"][0]
    conv_g = lax.dynamic_slice(red["conv_w"].reshape(CONV_K, N_CHIPS, CONV_CH // N_CHIPS), (0, chip, 0),
                               (CONV_K, 1, CONV_CH // N_CHIPS))
    gsmall = {n: red[n].reshape(w[n].shape) for n in small_names if n not in ("loss", "conv_w")}
    gsmall["conv_w"] = conv_g.reshape(w["conv_w"].shape)

    grad, delta, new_m, new_v = {}, {}, {}, {}
    for n in list(EARLY_GRADS) + list(LATE_GRADS):
        if n in LATE_GRADS:
            late.join(v_)
            gshard.update(late.finish(v_))
        shp = w[n].shape
        d_, m_, v_ = adamw(shards2d[n], gshard[n], m[n].reshape(shp[-2:]), v[n].reshape(shp[-2:]), "adamw_" + n,
                           dep=dep if n in EARLY_GRADS else None)
        dep, v_ = lax.optimization_barrier((dep, v_))
        grad[n], delta[n], new_m[n], new_v[n] = gshard[n].reshape(shp), d_.reshape(shp), m_.reshape(shp), v_.reshape(shp)
    snames = [n for n in small_names if n != "loss"]
    ssizes = [(n, w[n].size) for n in snames]
    pk = lambda d: pack_small(d, snames, 64)
    d_, m_, v_ = adamw(pk(w), pk(gsmall), pk(m), pk(v), "adamw_small")
    ds, ms, vs = unpack_small(d_, ssizes), unpack_small(m_, ssizes), unpack_small(v_, ssizes)
    for n in snames:
        shp = w[n].shape
        grad[n], delta[n], new_m[n], new_v[n] = gsmall[n], ds[n].reshape(shp), ms[n].reshape(shp), vs[n].reshape(shp)

    return (loss, grad_x, *[grad[n] for n in names], *[delta[n] for n in names],
            *[new_m[n] for n in names], *[new_v[n] for n in names])
```

```python
import functools

import jax
import jax.numpy as jnp
from jax import lax
from jax.experimental import pallas as pl
from jax.experimental.pallas import tpu as pltpu

F32 = jnp.float32
BF16 = jnp.bfloat16
MESH = pl.DeviceIdType.MESH

D_MODEL = 1024
CHUNK = 64
PLE_DIM = 256
EPS = 1e-6
A_HEADS = 8
A_HEAD_DIM = 64
A_WIDTH = 512
A_LOOKBACK = 8
BAND = (A_LOOKBACK + 1) * CHUNK
TAIL = 3 * CHUNK
REL_CLIP = 128
N_REL = 2 * REL_CLIP + 1
B_HEADS = 4
B_DIM = 128
B_WIDTH = 512
CONV_K = 4
CONV_CH = 1536
D_FF = 2816
SPLIT_Z = 3584
D_IN = 5640
ADAM_LR, ADAM_B1, ADAM_B2, ADAM_EPS, ADAM_WD, ADAM_STEP = 0.001, 0.9, 0.999, 1e-08, 0.01, 10

P_GATES, P_QA, P_KA, P_VA, P_CONV, P_Z, P_BD, P_WIDTH = 0, 2048, 2560, 3072, 3584, 5120, 5632, 5760

VMEM_LIMIT = 56 * 1024 * 1024


def _cp(sem, vmem=None, **kw):
    return pltpu.CompilerParams(dimension_semantics=sem, vmem_limit_bytes=vmem, **kw)


def _tile(n, cap):
    best = None
    for t in range(128, cap + 1, 128):
        if n % t == 0:
            best = t
    assert best is not None, (n, cap)
    return best


def _nn(a, b, prec=None):
    return lax.dot_general(a, b, (((1,), (0,)), ((), ())), preferred_element_type=F32, precision=prec)


def _nt(a, b, prec=None):
    return lax.dot_general(a, b, (((1,), (1,)), ((), ())), preferred_element_type=F32, precision=prec)


def _tn(a, b, prec=None):
    return lax.dot_general(a, b, (((0,), (0,)), ((), ())), preferred_element_type=F32, precision=prec)


def _bnn(a, b, prec=None):
    return lax.dot_general(a, b, (((2,), (1,)), ((0,), (0,))), preferred_element_type=F32, precision=prec)


def _bnt(a, b, prec=None):
    return lax.dot_general(a, b, (((2,), (2,)), ((0,), (0,))), preferred_element_type=F32, precision=prec)


def _bf(a):
    return a.astype(BF16)


def _split(a):
    hi = a.astype(BF16)
    return hi, (a - hi.astype(F32)).astype(BF16)


def _split3(a):
    h1 = _bf(a)
    r1 = a - h1.astype(F32)
    h2 = _bf(r1)
    return h1, h2, _bf(r1 - h2.astype(F32))


def _bnn_exact(lhs_b, rhs):
    h1, h2, h3 = _split3(rhs)
    return _bnn(lhs_b, h1) + (_bnn(lhs_b, h2) + _bnn(lhs_b, h3))


def _bnn3(a, b):
    ah, al = a if isinstance(a, tuple) else _split(a)
    bh, bl = b if isinstance(b, tuple) else _split(b)
    return _bnn(ah, bh) + (_bnn(ah, bl) + _bnn(al, bh))


def _sigmoid(x):
    return 0.5 * jnp.tanh(0.5 * x) + 0.5


def _softplus(x):
    return jnp.maximum(x, 0.0) + jnp.log(1.0 + jnp.exp(-jnp.abs(x)))


def rms_matmul(x, g, w, name, tm=512, tn_cap=1024):
    t, d = x.shape
    n = w.shape[1]
    tm = min(tm, t)
    tn = _tile(n, tn_cap)

    nj = n // tn

    def body(x_ref, g_ref, w_ref, o_ref, h_ref, tail_ref):
        @pl.when(pl.program_id(1) == 0)
        def _():
            xv = x_ref[...]
            r = lax.rsqrt(jnp.mean(xv * xv, axis=-1, keepdims=True) + EPS)
            h_ref[...] = _bf(xv * r * g_ref[...])

        res = _nn(h_ref[...], w_ref[...])
        o_ref[...] = _bf(res)

        @pl.when(pl.program_id(1) == nj - 1)
        def _():
            tail_ref[...] = res[:, tn - 128:]

    return pl.pallas_call(
        body, name=name, grid=(t // tm, nj),
        in_specs=[pl.BlockSpec((tm, d), lambda i, j: (i, 0)),
                  pl.BlockSpec((1, d), lambda i, j: (0, 0)),
                  pl.BlockSpec((d, tn), lambda i, j: (0, j))],
        out_specs=[pl.BlockSpec((tm, tn), lambda i, j: (i, j)),
                   pl.BlockSpec((tm, d), lambda i, j: (i, 0)),
                   pl.BlockSpec((tm, 128), lambda i, j: (i, 0))],
        out_shape=[jax.ShapeDtypeStruct((t, n), BF16), jax.ShapeDtypeStruct((t, d), BF16),
                   jax.ShapeDtypeStruct((t, 128), F32)],
        compiler_params=_cp(("parallel", "arbitrary"), VMEM_LIMIT),
    )(x, g, w)


def matmul_tn(a, b, name, into=None, col0=0, width=None, tm=1024, tk_cap=1408, tn_cap=1408, tiles_major=False):
    m, k1 = a.shape
    n = b.shape[1]
    tm = min(tm, m)
    tk = _tile(k1, tk_cap)
    tn = _tile(n, tn_cap)
    while col0 % tn:
        tn = _tile(n, tn - 128)
    nk = m // tm
    c0 = col0 // tn

    def body(*refs):
        a_ref, b_ref, o_ref, acc = refs[0], refs[1], refs[-2], refs[-1]

        @pl.when(pl.program_id(2) == 0)
        def _():
            acc[...] = jnp.zeros_like(acc)

        acc[...] += _tn(_bf(a_ref[...]), _bf(b_ref[...]))

        @pl.when(pl.program_id(2) == nk - 1)
        def _():
            o_ref[...] = _bf(acc[...])

    in_specs = [pl.BlockSpec((tm, tk), lambda i, j, k: (k, i)),
                pl.BlockSpec((tm, tn), lambda i, j, k: (k, j))]
    args = [a, b]
    total = n if width is None else width
    aliases = {}
    if into is not None:
        in_specs.append(ANY)
        args.append(into)
        aliases = {2: 0}
    if tiles_major:
        out_spec = pl.BlockSpec((None, tk, tn), lambda i, j, k: (c0 + j, i, 0))
        out_shape = jax.ShapeDtypeStruct((total // tn, k1, tn), BF16)
    else:
        out_spec = pl.BlockSpec((tk, tn), lambda i, j, k: (i, c0 + j))
        out_shape = jax.ShapeDtypeStruct((k1, total), BF16)
    return pl.pallas_call(
        body, name=name, grid=(k1 // tk, n // tn, nk),
        in_specs=in_specs,
        out_specs=out_spec,
        out_shape=out_shape,
        scratch_shapes=[pltpu.VMEM((tk, tn), F32)],
        input_output_aliases=aliases,
        compiler_params=_cp(("parallel", "parallel", "arbitrary"), VMEM_LIMIT),
    )(*args)


def _tail_onehot(qi):
    r = lax.broadcasted_iota(jnp.int32, (384, TAIL), 0)
    kj = lax.broadcasted_iota(jnp.int32, (384, TAIL), 1)
    return (r == jnp.minimum(REL_CLIP + qi - kj, REL_CLIP) + REL_CLIP).astype(F32)


def bias_tail(rel_pad):
    def body(rb_ref, o_ref):
        parts = _split3(rb_ref[...])
        for qi in range(CHUNK):
            oh = _bf(_tail_onehot(qi))
            o_ref[qi] = _nn(parts[0], oh) + (_nn(parts[1], oh) + _nn(parts[2], oh))

    return pl.pallas_call(
        body, name="bias_tail",
        out_shape=jax.ShapeDtypeStruct((CHUNK, A_HEADS, TAIL), F32),
    )(rel_pad)


def bias_grad(db_t, db_far):
    def body(t_ref, f_ref, o_ref):
        acc = jnp.zeros((A_HEADS, 384), F32)
        for qi in range(CHUNK):
            oh = _bf(_tail_onehot(qi))
            parts = _split3(t_ref[qi])
            acc = acc + (_nt(parts[0], oh) + (_nt(parts[1], oh) + _nt(parts[2], oh)))
        far = jnp.sum(jnp.sum(f_ref[...], axis=2), axis=1, keepdims=True)
        lane = lax.broadcasted_iota(jnp.int32, (A_HEADS, 384), 1)
        o_ref[...] = acc + jnp.where(lane == 2 * REL_CLIP, far, 0.0)

    return pl.pallas_call(
        body, name="bias_grad",
        out_shape=jax.ShapeDtypeStruct((A_HEADS, 384), F32),
    )(db_t, db_far)


ATT_CB = 8


WIN = BAND + CHUNK


def _stack_heads(a, lane):
    return jnp.concatenate([jnp.where(lane < 64, a, 0.0), jnp.where(lane >= 64, a, 0.0)], axis=0)


def _fill_band_pads(k_ref, v_ref, kp, vp, s):
    z = jnp.zeros((A_LOOKBACK * CHUNK, 128), BF16)
    kp[pl.ds(0, A_LOOKBACK * CHUNK), :] = z
    vp[pl.ds(0, A_LOOKBACK * CHUNK), :] = z
    kp[pl.ds(A_LOOKBACK * CHUNK, s), :] = _bf(k_ref[...])
    vp[pl.ds(A_LOOKBACK * CHUNK, s), :] = _bf(v_ref[...])


def attn_fwd(proj, bias_band, b, s):
    t = b * s
    nc = s // CHUNK
    qb, kb_, vb_ = P_QA // 128, P_KA // 128, P_VA // 128

    nstep = nc // ATT_CB
    rows = ATT_CB * CHUNK

    def body(q_ref, k_ref, v_ref, b_ref, o_ref, lse_ref, kp, vp):
        n0 = pl.program_id(2) * ATT_CB

        @pl.when(n0 == 0)
        def _():
            _fill_band_pads(k_ref, v_ref, kp, vp, s)

        lane = lax.broadcasted_iota(jnp.int32, (2 * CHUNK, 128), 1)
        col = lax.broadcasted_iota(jnp.int32, (4 * CHUNK, WIN), 1)
        bias4 = b_ref[...]

        def two_pairs(i, carry):
            pps = (2 * i, 2 * i + 1)
            ns = [n0 + 2 * pp for pp in pps]
            r0s = [pl.multiple_of(pp * 2 * CHUNK, 2 * CHUNK) for pp in pps]
            starts = [pl.multiple_of(n * CHUNK, CHUNK) for n in ns]
            kbs = [kp[pl.ds(st_, WIN), :] for st_ in starts]
            vbs = [vp[pl.ds(st_, WIN), :] for st_ in starts]
            q4s = [_bf(_stack_heads(q_ref[pl.ds(r0, 2 * CHUNK), :] * (A_HEAD_DIM ** -0.5), lane)) for r0 in r0s]
            qks = [_nt(q4, kb) for q4, kb in zip(q4s, kbs)]
            scs = [jnp.where(col >= (A_LOOKBACK - n) * CHUNK, qk + bias4, -1e30) for n, qk in zip(ns, qks)]
            mxs = [jnp.max(sc, axis=1, keepdims=True) for sc in scs]
            ps = [jnp.exp(sc - mx) for sc, mx in zip(scs, mxs)]
            ls = [jnp.sum(p, axis=1, keepdims=True) for p in ps]
            o4s = [_nn(_bf(p), vb) / l for p, vb, l in zip(ps, vbs, ls)]
            for r0, o4, mx, l in zip(r0s, o4s, mxs, ls):
                lse4 = mx + jnp.log(l)
                o_ref[pl.ds(r0, 2 * CHUNK), :] = jnp.where(lane < 64, o4[:2 * CHUNK], o4[2 * CHUNK:])
                lse_ref[pl.ds(r0, 2 * CHUNK), :] = jnp.where(lane < 64, lse4[:2 * CHUNK], lse4[2 * CHUNK:])
            return carry

        lax.fori_loop(0, ATT_CB // 4, two_pairs, 0)

    return pl.pallas_call(
        body, name="attn_fwd", grid=(b, 4, nstep),
        in_specs=[pl.BlockSpec((rows, 128), lambda bb, m, n: (bb * nstep + n, qb + m)),
                  pl.BlockSpec((s, 128), lambda bb, m, n: (bb, kb_ + m)),
                  pl.BlockSpec((s, 128), lambda bb, m, n: (bb, vb_ + m)),
                  pl.BlockSpec((None, 4 * CHUNK, WIN), lambda bb, m, n: (m, 0, 0))],
        out_specs=[pl.BlockSpec((rows, 128), lambda bb, m, n: (bb * nstep + n, m)),
                   pl.BlockSpec((rows, 128), lambda bb, m, n: (bb * nstep + n, m))],
        out_shape=[jax.ShapeDtypeStruct((t, A_WIDTH), F32), jax.ShapeDtypeStruct((t, A_WIDTH), F32)],
        scratch_shapes=[pltpu.VMEM((s + A_LOOKBACK * CHUNK, 128), BF16),
                        pltpu.VMEM((s + A_LOOKBACK * CHUNK, 128), BF16)],
        compiler_params=_cp(("parallel", "parallel", "arbitrary"), VMEM_LIMIT),
    )(proj, proj, proj, bias_band)


def attn_bwd(proj, bias_band, y_a, lse, dy_a, b, s):
    t = b * s
    nc = s // CHUNK
    qb, kb_, vb_ = P_QA // 128, P_KA // 128, P_VA // 128
    pad = A_LOOKBACK * CHUNK
    nstep = nc // ATT_CB
    rows = ATT_CB * CHUNK

    def body(q_ref, k_ref, v_ref, b_ref, do_ref, o_ref, lse_ref,
             dq_ref, dk_ref, dv_ref, dbt_ref, dbf_ref, kp, vp, dkp, dvp):
        bb = pl.program_id(1)
        n0 = pl.program_id(2) * ATT_CB

        @pl.when(n0 == 0)
        def _():
            _fill_band_pads(k_ref, v_ref, kp, vp, s)
            dkp[...] = jnp.zeros_like(dkp)
            dvp[...] = jnp.zeros_like(dvp)

        @pl.when((n0 == 0) & (bb == 0))
        def _():
            dbt_ref[...] = jnp.zeros_like(dbt_ref)
            dbf_ref[...] = jnp.zeros_like(dbf_ref)

        lane = lax.broadcasted_iota(jnp.int32, (2 * CHUNK, 128), 1)
        col = lax.broadcasted_iota(jnp.int32, (4 * CHUNK, WIN), 1)
        bias4 = b_ref[...]

        def two_pairs(i, carry):
            pps = (2 * i, 2 * i + 1)
            two = range(2)
            ns = [n0 + 2 * pp for pp in pps]
            r0s = [pl.multiple_of(pp * 2 * CHUNK, 2 * CHUNK) for pp in pps]
            starts = [pl.multiple_of(n * CHUNK, CHUNK) for n in ns]
            kbs = [kp[pl.ds(st_, WIN), :] for st_ in starts]
            vbs = [vp[pl.ds(st_, WIN), :] for st_ in starts]
            q4bs = [_bf(_stack_heads(q_ref[pl.ds(r0, 2 * CHUNK), :] * (A_HEAD_DIM ** -0.5), lane)) for r0 in r0s]
            do4s = [_stack_heads(do_ref[pl.ds(r0, 2 * CHUNK), :], lane) for r0 in r0s]
            do4bs = [_bf(d) for d in do4s]
            os_ = [o_ref[pl.ds(r0, 2 * CHUNK), :] for r0 in r0s]
            lsevs = [lse_ref[pl.ds(r0, 2 * CHUNK), :] for r0 in r0s]
            lse4s = [jnp.concatenate([v_[:, 0:1], v_[:, 64:65]], axis=0) for v_ in lsevs]
            qks = [_nt(q4bs[j], kbs[j]) for j in two]
            dps = [_nt(do4bs[j], vbs[j]) for j in two]
            deltas = [jnp.sum(do4s[j] * jnp.concatenate([os_[j], os_[j]], axis=0), axis=1, keepdims=True) for j in two]
            ps = [jnp.exp(jnp.where(col >= (A_LOOKBACK - ns[j]) * CHUNK, qks[j] + bias4, -1e30) - lse4s[j])
                  for j in two]
            pbs = [_bf(p) for p in ps]
            dss = [ps[j] * (dps[j] - deltas[j]) for j in two]
            dsbs = [_bf(d) for d in dss]
            dv_ws = [_tn(pbs[j], do4bs[j]) for j in two]
            dq4s = [_nn(dsbs[j], kbs[j]) for j in two]
            dk_ws = [_tn(dsbs[j], q4bs[j]) for j in two]
            for j in two:
                dq_ref[pl.ds(r0s[j], 2 * CHUNK), :] = _bf(
                    jnp.where(lane < 64, dq4s[j][:2 * CHUNK], dq4s[j][2 * CHUNK:]) * (A_HEAD_DIM ** -0.5))
                dkp[pl.ds(starts[j], WIN), :] += dk_ws[j]
                dvp[pl.ds(starts[j], WIN), :] += dv_ws[j]
                dbt_ref[...] += dss[j][:, WIN - 256:]
                dbf_ref[...] += dss[j][:, 0:128] + dss[j][:, 128:256] + dss[j][:, 256:384]
            return carry

        lax.fori_loop(0, ATT_CB // 4, two_pairs, 0)

        @pl.when(n0 == nc - ATT_CB)
        def _():
            dk_ref[...] = _bf(dkp[pl.ds(pad, s), :])
            dv_ref[...] = _bf(dvp[pl.ds(pad, s), :])

    return pl.pallas_call(
        body, name="attn_bwd", grid=(4, b, nstep),
        in_specs=[pl.BlockSpec((rows, 128), lambda m, bb, n: (bb * nstep + n, qb + m)),
                  pl.BlockSpec((s, 128), lambda m, bb, n: (bb, kb_ + m)),
                  pl.BlockSpec((s, 128), lambda m, bb, n: (bb, vb_ + m)),
                  pl.BlockSpec((None, 4 * CHUNK, WIN), lambda m, bb, n: (m, 0, 0)),
                  pl.BlockSpec((rows, 128), lambda m, bb, n: (bb * nstep + n, m)),
                  pl.BlockSpec((rows, 128), lambda m, bb, n: (bb * nstep + n, m)),
                  pl.BlockSpec((rows, 128), lambda m, bb, n: (bb * nstep + n, m))],
        out_specs=[pl.BlockSpec((rows, 128), lambda m, bb, n: (bb * nstep + n, m)),
                   pl.BlockSpec((s, 128), lambda m, bb, n: (bb, m)),
                   pl.BlockSpec((s, 128), lambda m, bb, n: (bb, m)),
                   pl.BlockSpec((None, 4 * CHUNK, 256), lambda m, bb, n: (m, 0, 0)),
                   pl.BlockSpec((None, 4 * CHUNK, 128), lambda m, bb, n: (m, 0, 0))],
        out_shape=[jax.ShapeDtypeStruct((t, A_WIDTH), BF16)] * 3
        + [jax.ShapeDtypeStruct((4, 4 * CHUNK, 256), F32),
           jax.ShapeDtypeStruct((4, 4 * CHUNK, 128), F32)],
        scratch_shapes=[pltpu.VMEM((s + pad, 128), BF16), pltpu.VMEM((s + pad, 128), BF16),
                        pltpu.VMEM((s + pad, 128), F32), pltpu.VMEM((s + pad, 128), F32)],
        compiler_params=_cp(("parallel", "arbitrary", "arbitrary"), VMEM_LIMIT),
    )(proj, proj, proj, bias_band, dy_a, y_a, lse)


def _conv_taps(x, w, s):
    row = lax.broadcasted_iota(jnp.int32, x.shape, 0)
    shifted = [x] + [jnp.where(row >= i, pltpu.roll(x, i, 0), 0.0) for i in range(1, CONV_K)]
    acc = shifted[0] * w[CONV_K - 1:CONV_K, :]
    for i in range(1, CONV_K):
        acc = acc + shifted[i] * w[CONV_K - 1 - i:CONV_K - i, :]
    return acc, shifted


def conv_fwd(proj, conv_w8, b, s):
    cb = 512
    c0 = P_CONV // cb

    def body(x_ref, w_ref, o_ref):
        a, _ = _conv_taps(x_ref[...].astype(F32), w_ref[...], s)
        o_ref[...] = a * _sigmoid(a)

    return pl.pallas_call(
        body, name="conv_fwd", grid=(b, CONV_CH // cb),
        in_specs=[pl.BlockSpec((s, cb), lambda bb, j: (bb, c0 + j)),
                  pl.BlockSpec((8, cb), lambda bb, j: (0, j))],
        out_specs=pl.BlockSpec((s, cb), lambda bb, j: (bb, j)),
        out_shape=jax.ShapeDtypeStruct((b * s, CONV_CH), F32),
        compiler_params=_cp(("parallel", "parallel"), VMEM_LIMIT),
    )(proj, conv_w8)


def conv_bwd(proj, conv_w8, dc3, b, s):
    cb = 512
    c0 = P_CONV // cb

    def body(x_ref, w_ref, dc_ref, dx_ref, dw_ref):
        @pl.when(pl.program_id(1) == 0)
        def _():
            dw_ref[...] = jnp.zeros_like(dw_ref)

        w = w_ref[...]
        a, shifted = _conv_taps(x_ref[...].astype(F32), w, s)
        sg = _sigmoid(a)
        da = dc_ref[...] * (sg * (1.0 + a * (1.0 - sg)))
        row = lax.broadcasted_iota(jnp.int32, da.shape, 0)
        dx = da * w[CONV_K - 1:CONV_K, :]
        for i in range(1, CONV_K):
            dx = dx + jnp.where(row < s - i, pltpu.roll(da, s - i, 0), 0.0) * w[CONV_K - 1 - i:CONV_K - i, :]
        dx_ref[...] = _bf(dx)
        r8 =lax.broadcasted_iota(jnp.int32, (8, cb), 0)
        dw = jnp.zeros((8, cb), F32)
        for i in range(CONV_K):
            dw = dw + jnp.where(r8 == CONV_K - 1 - i, jnp.sum(da * shifted[i], axis=0, keepdims=True), 0.0)
        dw_ref[...] += dw

    return pl.pallas_call(
        body, name="conv_bwd", grid=(CONV_CH // cb, b),
        in_specs=[pl.BlockSpec((s, cb), lambda j, bb: (bb, c0 + j)),
                  pl.BlockSpec((8, cb), lambda j, bb: (0, j)),
                  pl.BlockSpec((None, s, cb), lambda j, bb: (j, bb, 0))],
        out_specs=[pl.BlockSpec((s, cb), lambda j, bb: (bb, j)),
                   pl.BlockSpec((8, cb), lambda j, bb: (0, j))],
        out_shape=[jax.ShapeDtypeStruct((b * s, CONV_CH), BF16), jax.ShapeDtypeStruct((8, CONV_CH), F32)],
        compiler_params=_cp(("parallel", "arbitrary"), VMEM_LIMIT),
    )(proj, conv_w8, dc3)


def _pick_lane(v, k):
    lane = lax.broadcasted_iota(jnp.int32, v.shape, 1)
    return jnp.sum(jnp.where(lane == k, v, 0.0), axis=1, keepdims=True)


def _chunk_masks(ncb):
    i = lax.broadcasted_iota(jnp.int32, (ncb, CHUNK, CHUNK), 1)
    j = lax.broadcasted_iota(jnp.int32, (ncb, CHUNK, CHUNK), 2)
    return i, j


def _col_of_row(rowvec, eye):
    return jnp.sum(jnp.where(eye, rowvec, 0.0), axis=2, keepdims=True)


def _dn_chunk_math(cq, ck, cv, bd, al_row, dtb_row, h, ncb, tm=None):
    r = ncb * CHUNK
    i, j = _chunk_masks(ncb)
    eye = i == j
    low = i >= j
    strict = i > j
    ones = jnp.ones((ncb, CHUNK, CHUNK), F32)

    braw = _pick_lane(bd, h)
    draw = _pick_lane(bd, B_HEADS + h)
    al = _pick_lane(al_row, h)
    dtb = _pick_lane(dtb_row, h)
    ea = jnp.exp(al)
    beta = _sigmoid(braw)
    sp_arg = draw + dtb
    g = -ea * _softplus(sp_arg)

    rq = lax.rsqrt(jnp.sum(cq * cq, axis=1, keepdims=True) + EPS)
    rk = lax.rsqrt(jnp.sum(ck * ck, axis=1, keepdims=True) + EPS)
    nq = cq * rq
    kn = ck * rk
    qn = nq * (B_DIM ** -0.5)

    def c3(a):
        return a.reshape(ncb, CHUNK, a.shape[-1])

    qn3, kn3, v3, beta3 = c3(qn), c3(kn), c3(cv), c3(beta)
    gb = jnp.broadcast_to(c3(g), (ncb, CHUNK, CHUNK))
    gc_b = _bnn_exact(low.astype(BF16), gb)
    gr_b = _bnn_exact(_bf(ones), jnp.where(eye, gc_b, 0.0))
    dm = jnp.where(low, jnp.exp(jnp.where(low, gc_b - gr_b, 0.0)), 0.0)
    gc = gc_b[:, :, 0:1]
    gl = gc_b[:, CHUNK - 1:CHUNK, 0:1]
    gam = jnp.exp(gc)
    egl = jnp.exp(gl)
    edec = jnp.exp(gl - gc)

    knb = _bf(kn3)
    kk = _bnt(knb, knb)
    kd = jnp.where(strict, kk * dm, 0.0)
    a = beta3 * kd
    sz = 1 if tm is None else CHUNK
    if tm is None:
        tm = eye.astype(F32)
    while sz < CHUNK:
        off = jnp.where(((i // (2 * sz)) == (j // (2 * sz))) & ((i // sz) != (j // sz)), a, 0.0)
        tmb = _bf(tm)
        tm = tm - _bnn(_bf(_bnn(tmb, _bf(off))), tmb)
        sz *= 2
    bv = beta3 * v3
    bk = (beta3 * gam) * kn3
    sol = _bnn3(_split(tm), jnp.concatenate([bv, bk], axis=2))
    u, wk = sol[:, :, :B_DIM], sol[:, :, B_DIM:]
    qk = _bnt(_bf(qn3), knb)
    p = jnp.where(low, qk * dm, 0.0)
    kdec = kn3 * edec
    qg = gam * qn3
    return dict(beta=beta3, g=c3(g), ea=ea, sp_arg=c3(sp_arg), rq=c3(rq), rk=c3(rk), nq=c3(nq),
                qn=qn3, kn=kn3, v=v3, gc=gc, gl=gl, gam=gam, egl=egl, edec=edec, dm=dm, kd=kd, a=a,
                tm=tm, u=u, wk=wk, qk=qk, p=p, kdec=kdec, qg=qg, eye=eye, low=low, strict=strict)


def dn_prep(c, proj, al_row, dtb_row, b, s, ncb=32):
    t = b * s
    r = ncb * CHUNK
    nblk = t // r
    bd_blk = 0

    def body(cq_ref, ck_ref, cv_ref, bd_ref, al_ref, dtb_ref, u_ref, wk_ref, qg_ref, kdec_ref, p_ref, egl_ref,
             tm_ref):
        h = pl.program_id(1)
        m = _dn_chunk_math(cq_ref[...], ck_ref[...], cv_ref[...], bd_ref[...].astype(F32), al_ref[...], dtb_ref[...], h, ncb)
        tm_ref[...] = m["tm"].reshape(r, CHUNK)
        u_ref[...] = m["u"].reshape(r, B_DIM)
        wk_ref[...] = _bf(m["wk"].reshape(r, B_DIM))
        qg_ref[...] = _bf(m["qg"].reshape(r, B_DIM))
        kdec_ref[...] = _bf(m["kdec"].reshape(r, B_DIM))
        p_ref[...] = m["p"].reshape(r, CHUNK)
        egl_ref[...] = jnp.broadcast_to(m["egl"], (ncb, 8, 128)).reshape(ncb * 8, 128)

    col = lambda k: pl.BlockSpec((r, 128), lambda i, h: (i, k * B_HEADS + h))
    out_col = pl.BlockSpec((r, 128), lambda i, h: (i, h))
    small = pl.BlockSpec((1, 128), lambda i, h: (0, 0))
    return pl.pallas_call(
        body, name="dn_prep", grid=(nblk, B_HEADS),
        in_specs=[col(0), col(1), col(2), pl.BlockSpec((r, 128), lambda i, h: (i, bd_blk)), small, small],
        out_specs=[out_col, out_col, out_col, out_col,
                   pl.BlockSpec((None, r, CHUNK), lambda i, h: (h, i, 0)),
                   pl.BlockSpec((None, ncb * 8, 128), lambda i, h: (h, i, 0)),
                   pl.BlockSpec((None, r, CHUNK), lambda i, h: (h, i, 0))],
        out_shape=[jax.ShapeDtypeStruct((t, B_WIDTH), F32)] + [jax.ShapeDtypeStruct((t, B_WIDTH), BF16)] * 3
        + [jax.ShapeDtypeStruct((B_HEADS, t, CHUNK), F32),
           jax.ShapeDtypeStruct((B_HEADS, t // 8, 128), F32),
           jax.ShapeDtypeStruct((B_HEADS, t, CHUNK), F32)],
        compiler_params=_cp(("parallel", "parallel"), VMEM_LIMIT),
    )(c, c, c, proj, al_row, dtb_row)


SCAN_CB = 4


def dn_scan_fwd(u, wk, qg, kdec, p, egl, b, s):
    t = b * s
    nc = s // CHUNK

    def body(u_ref, wk_ref, qg_ref, kdec_ref, p_ref, egl_ref, o_ref, ss_ref, st):
        @pl.when(pl.program_id(0) == 0)
        def _():
            st[...] = jnp.zeros_like(st)

        chains = [(bb, h) for bb in range(b) for h in range(B_HEADS)]
        sls = [slice(h * B_DIM, (h + 1) * B_DIM) for _, h in chains]
        states = [st[bb * B_HEADS + h] for bb, h in chains]
        for cc in range(SCAN_CB):
            rs = slice(cc * CHUNK, (cc + 1) * CHUNK)
            sbs = [_bf(sh) for sh in states]
            ws = [u_ref[bb, rs, sl] - _nt(wk_ref[bb, rs, sl], sb) for (bb, _), sl, sb in zip(chains, sls, sbs)]
            qs = [_nt(qg_ref[bb, rs, sl], sb) for (bb, _), sl, sb in zip(chains, sls, sbs)]
            wbs = [_bf(w) for w in ws]
            outs = [q + _nn(_bf(p_ref[h, bb, rs, :]), wb) for (bb, h), q, wb in zip(chains, qs, wbs)]
            new_states = [egl_ref[h, bb, cc * 8:cc * 8 + 1, :] * sh + _tn(wb, kdec_ref[bb, rs, sl])
                          for (bb, h), sl, sh, wb in zip(chains, sls, states, wbs)]
            for (bb, h), sh, o in zip(chains, states, outs):
                ss_ref[bb, cc, h] = sh
                o_ref[bb, rs, h * B_DIM:(h + 1) * B_DIM] = o
            states = new_states
        for (bb, h), sh in zip(chains, states):
            st[bb * B_HEADS + h] = sh

    r3 = lambda a: a.reshape(b, s, B_WIDTH)
    rows = SCAN_CB * CHUNK
    act = pl.BlockSpec((b, rows, B_WIDTH), lambda n: (0, n, 0))
    o, states = pl.pallas_call(
        body, name="dn_scan_fwd", grid=(nc // SCAN_CB,),
        in_specs=[act, act, act, act,
                  pl.BlockSpec((B_HEADS, b, rows, CHUNK), lambda n: (0, 0, n, 0)),
                  pl.BlockSpec((B_HEADS, b, SCAN_CB * 8, 128), lambda n: (0, 0, n, 0))],
        out_specs=[act, pl.BlockSpec((b, SCAN_CB, B_HEADS, B_DIM, B_DIM), lambda n: (0, n, 0, 0, 0))],
        out_shape=[jax.ShapeDtypeStruct((b, s, B_WIDTH), F32),
                   jax.ShapeDtypeStruct((b, nc, B_HEADS, B_DIM, B_DIM), F32)],
        scratch_shapes=[pltpu.VMEM((b * B_HEADS, B_DIM, B_DIM), F32)],
        compiler_params=_cp(("arbitrary",), VMEM_LIMIT),
    )(r3(u), r3(wk), r3(qg), r3(kdec), p.reshape(B_HEADS, b, s, CHUNK), egl.reshape(B_HEADS, b, s // 8, 128))
    return o.reshape(t, B_WIDTH), states


def dn_scan_bwd(u, wk, qg, kdec, p, egl, states, do, b, s):
    t = b * s
    nc = s // CHUNK

    def body(u_ref, wk_ref, qg_ref, kdec_ref, p_ref, egl_ref, ss_ref, do_ref,
             dw_ref, dwk_ref, dqg_ref, dkdec_ref, dp_ref, degl_ref, dst):
        @pl.when(pl.program_id(0) == 0)
        def _():
            dst[...] = jnp.zeros_like(dst)

        chains = [(bb, h) for bb in range(b) for h in range(B_HEADS)]
        dstates = [dst[bb * B_HEADS + h] for bb, h in chains]
        n8 = range(len(chains))
        sls = [slice(h * B_DIM, (h + 1) * B_DIM) for _, h in chains]
        for cc in reversed(range(SCAN_CB)):
            rs = slice(cc * CHUNK, (cc + 1) * CHUNK)
            shs = [ss_ref[bb, cc, h] for bb, h in chains]
            sbs = [_bf(sh) for sh in shs]
            dsbs = [_bf(dsp) for dsp in dstates]
            wkbs = [wk_ref[bb, rs, sl] for (bb, _), sl in zip(chains, sls)]
            dobs = [_bf(do_ref[bb, rs, sl]) for (bb, _), sl in zip(chains, sls)]
            t1 = [_nt(wkbs[i], sbs[i]) for i in n8]
            dwa = [_tn(_bf(p_ref[h, bb, rs, :]), dobs[i]) for i, (bb, h) in enumerate(chains)]
            dwb_ = [_nt(kdec_ref[bb, rs, sls[i]], dsbs[i]) for i, (bb, _) in enumerate(chains)]
            dqgs = [_nn(dobs[i], sbs[i]) for i in n8]
            dsq = [_tn(dobs[i], qg_ref[bb, rs, sls[i]]) for i, (bb, _) in enumerate(chains)]
            wbs = [_bf(u_ref[bb, rs, sls[i]] - t1[i]) for i, (bb, _) in enumerate(chains)]
            dws = [dwa[i] + dwb_[i] for i in n8]
            dwbs = [_bf(dw) for dw in dws]
            dwks = [-_nn(dwbs[i], sbs[i]) for i in n8]
            dkdecs = [_nn(wbs[i], dsbs[i]) for i in n8]
            dpms = [_nt(dobs[i], wbs[i]) for i in n8]
            dsw = [_tn(dwbs[i], wkbs[i]) for i in n8]
            tots = [jnp.sum(jnp.sum(shs[i] * dstates[i], axis=1, keepdims=True), axis=0, keepdims=True) for i in n8]
            new_dss = [egl_ref[h, bb, cc * 8:cc * 8 + 1, :] * dstates[i] + dsq[i] - dsw[i]
                       for i, (bb, h) in enumerate(chains)]
            for i, (bb, h) in enumerate(chains):
                dw_ref[bb, rs, sls[i]] = dws[i]
                dqg_ref[bb, rs, sls[i]] = dqgs[i]
                dwk_ref[bb, rs, sls[i]] = dwks[i]
                dkdec_ref[bb, rs, sls[i]] = dkdecs[i]
                dp_ref[h, bb, rs, :] = dpms[i]
                degl_ref[h, bb, cc * 8:(cc + 1) * 8, :] = jnp.broadcast_to(tots[i], (8, 128))
            dstates = new_dss
        for (bb, h), dsp in zip(chains, dstates):
            dst[bb * B_HEADS + h] = dsp

    r3 = lambda a: a.reshape(b, s, B_WIDTH)
    rows = SCAN_CB * CHUNK
    last = nc // SCAN_CB - 1
    act = pl.BlockSpec((b, rows, B_WIDTH), lambda n: (0, last - n, 0))
    pspec = pl.BlockSpec((B_HEADS, b, rows, CHUNK), lambda n: (0, 0, last - n, 0))
    espec = pl.BlockSpec((B_HEADS, b, SCAN_CB * 8, 128), lambda n: (0, 0, last - n, 0))
    outs = pl.pallas_call(
        body, name="dn_scan_bwd", grid=(nc // SCAN_CB,),
        in_specs=[act, act, act, act, pspec, espec,
                  pl.BlockSpec((b, SCAN_CB, B_HEADS, B_DIM, B_DIM), lambda n: (0, last - n, 0, 0, 0)),
                  act],
        out_specs=[act, act, act, act, pspec, espec],
        out_shape=[jax.ShapeDtypeStruct((b, s, B_WIDTH), F32)] * 4
        + [jax.ShapeDtypeStruct((B_HEADS, b, s, CHUNK), F32),
           jax.ShapeDtypeStruct((B_HEADS, b, s // 8, 128), F32)],
        scratch_shapes=[pltpu.VMEM((b * B_HEADS, B_DIM, B_DIM), F32)],
        compiler_params=_cp(("arbitrary",), VMEM_LIMIT),
    )(r3(u), r3(wk), r3(qg), r3(kdec), p.reshape(B_HEADS, b, s, CHUNK), egl.reshape(B_HEADS, b, s // 8, 128),
      states, r3(do))
    return (*[a.reshape(t, B_WIDTH) for a in outs[:4]], outs[4].reshape(B_HEADS, t, CHUNK),
            outs[5].reshape(B_HEADS, t // 8, 128))


def dn_post_bwd(c, proj, al_row, dtb_row, tmat, dw, dwk, dqg, dkdec, dp, degl, b, s, ncb=16):
    t = b * s
    r = ncb * CHUNK
    nblk = t // r
    bd_blk = 0

    def body(cq_ref, ck_ref, cv_ref, bd_ref, al_ref, dtb_ref, tm_ref, dw_ref, dwk_ref, dqg_ref, dkdec_ref, dp_ref,
             degl_ref, dc_ref, dbd_ref, dal_ref, ddtb_ref):
        h = pl.program_id(1)

        @pl.when((pl.program_id(0) == 0) & (h == 0))
        def _():
            dal_ref[...] = jnp.zeros_like(dal_ref)
            ddtb_ref[...] = jnp.zeros_like(ddtb_ref)

        m = _dn_chunk_math(cq_ref[...], ck_ref[...], cv_ref[...], bd_ref[...].astype(F32), al_ref[...], dtb_ref[...], h, ncb,
                           tm=tm_ref[...].reshape(ncb, CHUNK, CHUNK))
        eye, low, strict = m["eye"], m["low"], m["strict"]
        eyef = eye.astype(F32)

        def c3(a):
            return a.reshape(ncb, CHUNK, a.shape[-1])

        du, dwkv, dqg, dkdec = c3(dw_ref[...]), c3(dwk_ref[...]), c3(dqg_ref[...]), c3(dkdec_ref[...])
        dpm = jnp.where(low, c3(dp_ref[...]), 0.0)
        degl = degl_ref[...].reshape(ncb, 8, 128)[:, 0:1, 0:1]
        beta, gam, kn, qn, v = m["beta"], m["gam"], m["kn"], m["qn"], m["v"]
        dm, kd, a, p = m["dm"], m["kd"], m["a"], m["p"]
        knb, qnb = _bf(kn), _bf(qn)

        eyeb = _bf(eyef)
        th, tl = _split(m["tm"])
        tts = (_bf(_bnt(eyeb, th)), _bf(_bnt(eyeb, tl)))
        xy = _bnn3(tts, jnp.concatenate([du, dwkv], axis=2))
        x, y = xy[:, :, :B_DIM], xy[:, :, B_DIM:]
        da = -jnp.where(strict, _bnt(_bf(x), _bf(m["u"])) + _bnt(_bf(y), _bf(m["wk"])), 0.0)
        dv = beta * x
        sy = jnp.sum(y * kn, axis=2, keepdims=True)
        dbeta = jnp.sum(x * v, axis=2, keepdims=True) + gam * sy + jnp.sum(da * kd, axis=2, keepdims=True)
        dgam = beta * sy + jnp.sum(dqg * qn, axis=2, keepdims=True)
        dkk = da * beta * dm
        dqk = dpm * dm
        dkkb, dqkb = _bf(dkk), _bf(dqk)
        dkn = ((beta * gam) * y + _bnn(dkkb, knb) + _bnn(_bf(_bnt(eyeb, dkkb)), knb)
               + _bnn(_bf(_bnt(eyeb, dqkb)), qnb) + dkdec * m["edec"])
        dqn = gam * dqg + _bnn(dqkb, knb)
        mm = da * a + dpm * p
        ek = jnp.sum(dkdec * m["kdec"], axis=2, keepdims=True)
        dgc = (jnp.sum(mm, axis=2, keepdims=True) - _col_of_row(jnp.sum(mm, axis=1, keepdims=True), eye)
               + dgam * gam - ek)
        dgl = jnp.sum(ek, axis=1, keepdims=True) + degl * m["egl"]
        i, _ = _chunk_masks(ncb)
        dgc = dgc + jnp.where(i[:, :, 0:1] == CHUNK - 1, dgl, 0.0)
        upper = (i <= _chunk_masks(ncb)[1]).astype(BF16)
        dg = _bnn_exact(upper, jnp.broadcast_to(dgc, (ncb, CHUNK, CHUNK)))[:, :, 0:1]

        nq = m["nq"]
        dnq = dqn * (B_DIM ** -0.5)
        dcq = m["rq"] * (dnq - nq * jnp.sum(nq * dnq, axis=2, keepdims=True))
        dck = m["rk"] * (dkn - kn * jnp.sum(kn * dkn, axis=2, keepdims=True))
        dc_ref[0] = dcq.reshape(r, B_DIM)
        dc_ref[1] = dck.reshape(r, B_DIM)
        dc_ref[2] = dv.reshape(r, B_DIM)

        dbraw = (dbeta * beta * (1.0 - beta)).reshape(r, 1)
        sgm = _sigmoid(m["sp_arg"])
        ddraw3 = dg * (-m["ea"]) * sgm
        ddraw = ddraw3.reshape(r, 1)
        lane = lax.broadcasted_iota(jnp.int32, (r, 128), 1)
        contrib = jnp.where(lane == h, dbraw, 0.0) + jnp.where(lane == B_HEADS + h, ddraw, 0.0)

        @pl.when(h == 0)
        def _():
            dbd_ref[...] = contrib

        @pl.when(h != 0)
        def _():
            dbd_ref[...] += contrib

        lane8 = lax.broadcasted_iota(jnp.int32, (8, 128), 1)
        tot_al = jnp.sum(jnp.sum(dg * m["g"], axis=1, keepdims=True), axis=0, keepdims=True).reshape(1, 1)
        tot_dtb = jnp.sum(jnp.sum(ddraw3, axis=1, keepdims=True), axis=0, keepdims=True).reshape(1, 1)
        dal_ref[...] += jnp.where(lane8 == h, tot_al, 0.0)
        ddtb_ref[...] += jnp.where(lane8 == h, tot_dtb, 0.0)

    col = lambda k: pl.BlockSpec((r, 128), lambda i, h: (i, k * B_HEADS + h))
    hcol = pl.BlockSpec((r, 128), lambda i, h: (i, h))
    small = pl.BlockSpec((1, 128), lambda i, h: (0, 0))
    acc = pl.BlockSpec((8, 128), lambda i, h: (0, 0))
    return pl.pallas_call(
        body, name="dn_post_bwd", grid=(nblk, B_HEADS),
        in_specs=[col(0), col(1), col(2), pl.BlockSpec((r, 128), lambda i, h: (i, bd_blk)), small, small,
                  pl.BlockSpec((None, r, CHUNK), lambda i, h: (h, i, 0)),
                  hcol, hcol, hcol, hcol,
                  pl.BlockSpec((None, r, CHUNK), lambda i, h: (h, i, 0)),
                  pl.BlockSpec((None, ncb * 8, 128), lambda i, h: (h, i, 0))],
        out_specs=[pl.BlockSpec((3, r, 128), lambda i, h: (0, i, h)),
                   pl.BlockSpec((r, 128), lambda i, h: (i, 0)), acc, acc],
        out_shape=[jax.ShapeDtypeStruct((3, t, B_WIDTH), F32), jax.ShapeDtypeStruct((t, 128), F32),
                   jax.ShapeDtypeStruct((8, 128), F32), jax.ShapeDtypeStruct((8, 128), F32)],
        compiler_params=_cp(("arbitrary", "arbitrary"), VMEM_LIMIT),
    )(c, c, c, proj, al_row, dtb_row, tmat, dw, dwk, dqg, dkdec, dp, degl)


def make_bias_band(rel_bias):
    tail = bias_tail(jnp.pad(rel_bias, ((0, 0), (0, 384 - N_REL))))
    far = jnp.broadcast_to(rel_bias[:, 2 * REL_CLIP][:, None, None], (A_HEADS, CHUNK, BAND - TAIL))
    band = jnp.concatenate([far, jnp.transpose(tail, (1, 0, 2))], axis=2)
    off = jnp.full((A_HEADS, CHUNK, CHUNK), -1e30, F32)
    both = jnp.stack([jnp.concatenate([band, off], axis=2), jnp.concatenate([off, band], axis=2)], axis=1)
    return both.reshape(4, 4 * CHUNK, WIN)


def bias_band_grad(dbt, dbf):
    t5 = dbt.reshape(A_HEADS, 2, CHUNK, 256)
    tail = t5[:, 0, :, :TAIL] + t5[:, 1, :, CHUNK:]
    far = dbf.reshape(A_HEADS, 2, CHUNK, 128).sum(axis=1) + jnp.pad(t5[:, 1, :, :CHUNK], ((0, 0), (0, 0), (0, CHUNK)))
    return bias_grad(jnp.transpose(tail, (1, 0, 2)), far)[:, :N_REL]


def _rms(x):
    r = lax.rsqrt(jnp.mean(x * x, axis=-1, keepdims=True) + EPS)
    return r, x * r


def _rms_bwd(dh, g, r, n):
    dn = dh * g
    return r * (dn - n * jnp.mean(dn * n, axis=-1, keepdims=True)), dh * n


def _gated_onorm(o, z, w_on):
    parts = []
    for h in range(B_HEADS):
        sl = slice(h * B_DIM, (h + 1) * B_DIM)
        r, n = _rms(o[:, sl])
        parts.append((r, n))
    r4 = [p[0] for p in parts]
    n4 = jnp.concatenate([p[1] for p in parts], axis=1)
    w4 = jnp.concatenate([w_on] * B_HEADS, axis=1)
    sz = _sigmoid(z)
    silu = z * sz
    return n4 * w4 * silu, r4, n4, w4, sz, silu


def mid_fwd(x, y_a, o_b, proj, w_on, wa, wb, w_out, tm=256):
    t = x.shape[0]
    tm = min(tm, t)

    def body(x_ref, ya_ref, ob_ref, z_ref, ga_ref, gb_ref, won_ref, wa_ref, wb_ref, wo_ref, x1_ref, mg_ref):
        yb = _gated_onorm(ob_ref[...], z_ref[...].astype(F32), won_ref[...])[0]
        ua = _nn(_bf(ya_ref[...]), wa_ref[...])
        ub = _nn(_bf(yb), wb_ref[...])
        merged = _sigmoid(ga_ref[...].astype(F32)) * ua + _sigmoid(gb_ref[...].astype(F32)) * ub
        mb = _bf(merged)
        mg_ref[...] = mb
        x1_ref[...] = x_ref[...] + _nn(mb, wo_ref[...])

    rowd = pl.BlockSpec((tm, D_MODEL), lambda i: (i, 0))
    row5 = pl.BlockSpec((tm, 512), lambda i: (i, 0))
    full = lambda a: pl.BlockSpec(a.shape, lambda i: (0,) * a.ndim)
    return pl.pallas_call(
        body, name="mid_fwd", grid=(t // tm,),
        in_specs=[rowd, row5, row5,
                  pl.BlockSpec((tm, 512), lambda i: (i, P_Z // 512)),
                  pl.BlockSpec((tm, D_MODEL), lambda i: (i, 0)),
                  pl.BlockSpec((tm, D_MODEL), lambda i: (i, 1)),
                  full(w_on), full(wa), full(wb), full(w_out)],
        out_specs=[rowd, rowd],
        out_shape=[jax.ShapeDtypeStruct((t, D_MODEL), F32), jax.ShapeDtypeStruct((t, D_MODEL), BF16)],
        compiler_params=_cp(("parallel",), VMEM_LIMIT),
    )(x, y_a, o_b, proj, proj, proj, w_on, wa, wb, w_out)


def mid_bwd(dx1, merged, y_a, o_b, proj, w_on, wa, wb, w_out, tm=256):
    t = dx1.shape[0]
    tm = min(tm, t)

    def body(dx1_ref, mg_ref, ya_ref, ob_ref, z_ref, ga_ref, gb_ref, won_ref, wa_ref, wb_ref, wo_ref,
             dya_ref, dob_ref, dz_ref, dg_ref, dwo_ref, dwa_ref, dwb_ref, dwon_ref):
        @pl.when(pl.program_id(0) == 0)
        def _():
            dwo_ref[...] = jnp.zeros_like(dwo_ref)
            dwa_ref[...] = jnp.zeros_like(dwa_ref)
            dwb_ref[...] = jnp.zeros_like(dwb_ref)
            dwon_ref[...] = jnp.zeros_like(dwon_ref)

        dx1b = _bf(dx1_ref[...])
        dmerged = _nt(dx1b, wo_ref[...])
        dwo_ref[...] += _tn(mg_ref[...], dx1b)
        o = ob_ref[...]
        z = z_ref[...].astype(F32)
        yb, r4, n4, w4, sz, silu = _gated_onorm(o, z, won_ref[...])
        yab, ybb = _bf(ya_ref[...]), _bf(yb)
        ua = _nn(yab, wa_ref[...])
        ub = _nn(ybb, wb_ref[...])
        sa, sb = _sigmoid(ga_ref[...].astype(F32)), _sigmoid(gb_ref[...].astype(F32))
        dua, dub = _bf(dmerged * sa), _bf(dmerged * sb)
        dg_ref[:, 0:D_MODEL] = _bf(dmerged * ua * sa * (1.0 - sa))
        dg_ref[:, D_MODEL:2 * D_MODEL] = _bf(dmerged * ub * sb * (1.0 - sb))
        dwa_ref[...] += _tn(yab, dua)
        dwb_ref[...] += _tn(ybb, dub)
        dya_ref[...] = _nt(dua, wa_ref[...])
        dyb = _nt(dub, wb_ref[...])
        dz_ref[...] = _bf(dyb * (n4 * w4) * (sz * (1.0 + z * (1.0 - sz))))
        dnw = dyb * silu
        dwon = jnp.zeros((1, B_DIM), F32)
        for h in range(B_HEADS):
            sl = slice(h * B_DIM, (h + 1) * B_DIM)
            dxh, dgh = _rms_bwd(dnw[:, sl], won_ref[...], r4[h], n4[:, sl])
            dob_ref[:, sl] = dxh
            dwon = dwon + jnp.sum(dgh, axis=0, keepdims=True)
        dwon_ref[...] += jnp.broadcast_to(dwon, (8, B_DIM))

    rowd = pl.BlockSpec((tm, D_MODEL), lambda i: (i, 0))
    row5 = pl.BlockSpec((tm, 512), lambda i: (i, 0))
    full = lambda a: pl.BlockSpec(a.shape, lambda i: (0,) * a.ndim)
    fixed = lambda shp: pl.BlockSpec(shp, lambda i: (0,) * len(shp))
    return pl.pallas_call(
        body, name="mid_bwd", grid=(t // tm,),
        in_specs=[rowd, rowd, row5, row5,
                  pl.BlockSpec((tm, 512), lambda i: (i, P_Z // 512)),
                  pl.BlockSpec((tm, D_MODEL), lambda i: (i, 0)),
                  pl.BlockSpec((tm, D_MODEL), lambda i: (i, 1)),
                  full(w_on), full(wa), full(wb), full(w_out)],
        out_specs=[row5, row5, row5, pl.BlockSpec((tm, 2 * D_MODEL), lambda i: (i, 0)),
                   fixed((D_MODEL, D_MODEL)), fixed((A_WIDTH, D_MODEL)), fixed((B_WIDTH, D_MODEL)),
                   fixed((8, B_DIM))],
        out_shape=[jax.ShapeDtypeStruct((t, 512), F32), jax.ShapeDtypeStruct((t, 512), F32),
                   jax.ShapeDtypeStruct((t, 512), BF16), jax.ShapeDtypeStruct((t, 2 * D_MODEL), BF16),
           jax.ShapeDtypeStruct((D_MODEL, D_MODEL), F32), jax.ShapeDtypeStruct((A_WIDTH, D_MODEL), F32),
           jax.ShapeDtypeStruct((B_WIDTH, D_MODEL), F32), jax.ShapeDtypeStruct((8, B_DIM), F32)],
        compiler_params=_cp(("arbitrary",), VMEM_LIMIT),
    )(dx1, merged, y_a, o_b, proj, proj, proj, w_on, wa, wb, w_out)


FFN_TF = 1408


def ffn_up(x1, g, w_gu, tm=512, tf=FFN_TF):
    t = x1.shape[0]
    tm = min(tm, t)
    nf = D_FF // tf

    def body(x_ref, g_ref, wg_ref, wu_ref, gate_ref, up_ref, act_ref, h_ref):
        @pl.when(pl.program_id(1) == 0)
        def _():
            r, n = _rms(x_ref[...])
            h_ref[...] = _bf(n * g_ref[...])

        hb = h_ref[...]
        gate = _nn(hb, wg_ref[...])
        up = _nn(hb, wu_ref[...])
        gate_ref[...] = _bf(gate)
        up_ref[...] = _bf(up)
        act_ref[...] = _bf(gate * _sigmoid(gate) * up)

    ff = pl.BlockSpec((tm, tf), lambda i, j: (i, j))
    return pl.pallas_call(
        body, name="ffn_up", grid=(t // tm, nf),
        in_specs=[pl.BlockSpec((tm, D_MODEL), lambda i, j: (i, 0)),
                  pl.BlockSpec((1, D_MODEL), lambda i, j: (0, 0)),
                  pl.BlockSpec((D_MODEL, tf), lambda i, j: (0, j)),
                  pl.BlockSpec((D_MODEL, tf), lambda i, j: (0, nf + j))],
        out_specs=[ff, ff, ff, pl.BlockSpec((tm, D_MODEL), lambda i, j: (i, 0))],
        out_shape=[jax.ShapeDtypeStruct((t, D_FF), BF16)] * 3 + [jax.ShapeDtypeStruct((t, D_MODEL), BF16)],
        compiler_params=_cp(("parallel", "arbitrary"), VMEM_LIMIT),
    )(x1, g, w_gu, w_gu)


def matmul_residual(a, w, res, name, tm=512, tk=FFN_TF):
    t, k = a.shape
    n = w.shape[1]
    tm = min(tm, t)

    def body(a_ref, w_ref, r_ref, o_ref):
        @pl.when(pl.program_id(1) == 0)
        def _():
            o_ref[...] = r_ref[...]

        o_ref[...] += _nn(a_ref[...], w_ref[...])

    return pl.pallas_call(
        body, name=name, grid=(t // tm, k // tk),
        in_specs=[pl.BlockSpec((tm, tk), lambda i, j: (i, j)),
                  pl.BlockSpec((tk, n), lambda i, j: (j, 0)),
                  pl.BlockSpec((tm, n), lambda i, j: (i, 0))],
        out_specs=pl.BlockSpec((tm, n), lambda i, j: (i, 0)),
        out_shape=jax.ShapeDtypeStruct((t, n), F32),
        compiler_params=_cp(("parallel", "arbitrary"), VMEM_LIMIT),
    )(a, w, res)


def ffn_act_bwd(dx2, gate, up, w_down, tm=512, tf=FFN_TF):
    t = dx2.shape[0]
    tm = min(tm, t)

    def body(dx2_ref, gate_ref, up_ref, wd_ref, dgate_ref, dup_ref, dx2b_ref):
        @pl.when(pl.program_id(1) == 0)
        def _():
            dx2b_ref[...] = _bf(dx2_ref[...])

        dact = _nt(dx2b_ref[...], wd_ref[...])
        gt, upv = gate_ref[...].astype(F32), up_ref[...].astype(F32)
        sg = _sigmoid(gt)
        t = dact * sg
        dgate_ref[...] = _bf(t * upv * (1.0 + gt * (1.0 - sg)))
        dup_ref[...] = _bf(t * gt)

    ff = pl.BlockSpec((tm, tf), lambda i, j: (i, j))
    return pl.pallas_call(
        body, name="ffn_act_bwd", grid=(t // tm, D_FF // tf),
        in_specs=[pl.BlockSpec((tm, D_MODEL), lambda i, j: (i, 0)), ff, ff,
                  pl.BlockSpec((tf, D_MODEL), lambda i, j: (j, 0))],
        out_specs=[ff, ff],
        out_shape=[jax.ShapeDtypeStruct((t, D_FF), BF16)] * 2,
        scratch_shapes=[pltpu.VMEM((tm, D_MODEL), BF16)],
        compiler_params=_cp(("parallel", "arbitrary"), VMEM_LIMIT),
    )(dx2, gate, up, w_down)


def tail_fwd_bwd(x2, p, target, g_ple, g_final, w_pg, w_pp, tm=512):
    t = x2.shape[0]
    tm = min(tm, t)

    def body(x_ref, p_ref, t_ref, gp_ref, gf_ref, wpg_ref, wpp_ref,
             dx_ref, dwpg_ref, dwpp_ref, dgp_ref, dgf_ref, loss_ref):
        @pl.when(pl.program_id(0) == 0)
        def _():
            dwpg_ref[...] = jnp.zeros_like(dwpg_ref)
            dwpp_ref[...] = jnp.zeros_like(dwpp_ref)
            dgp_ref[...] = jnp.zeros_like(dgp_ref)
            dgf_ref[...] = jnp.zeros_like(dgf_ref)
            loss_ref[...] = jnp.zeros_like(loss_ref)

        x2v = x_ref[...]
        gp, gf = gp_ref[...], gf_ref[...]
        r3, n3 = _rms(x2v)
        h3b = _bf(n3 * gp)
        pb = _bf(p_ref[...])
        pg = _sigmoid(_nn(h3b, wpg_ref[...]))
        pp = _nn(pb, wpp_ref[...])
        x3 = x2v + pg * pp
        r4, n4 = _rms(x3)
        err = n4 * gf - t_ref[...]
        part = 0.5 * jnp.sum(jnp.sum(err * err, axis=1, keepdims=True), axis=0, keepdims=True) / D_MODEL
        loss_ref[...] += jnp.broadcast_to(part, (8, 128))
        dy = err * (1.0 / D_MODEL)
        dx3, dgf = _rms_bwd(dy, gf, r4, n4)
        dgf_ref[...] += jnp.broadcast_to(jnp.sum(dgf, axis=0, keepdims=True), (8, D_MODEL))
        dzp = _bf(dx3 * pp * pg * (1.0 - pg))
        dpp = _bf(dx3 * pg)
        dwpg_ref[...] += _tn(h3b, dzp)
        dwpp_ref[...] += _tn(pb, dpp)
        dh3 = _nt(dzp, wpg_ref[...])
        dx, dgp = _rms_bwd(dh3, gp, r3, n3)
        dgp_ref[...] += jnp.broadcast_to(jnp.sum(dgp, axis=0, keepdims=True), (8, D_MODEL))
        dx_ref[...] = dx3 + dx

    rowd = pl.BlockSpec((tm, D_MODEL), lambda i: (i, 0))
    fixed = lambda shp: pl.BlockSpec(shp, lambda i: (0,) * len(shp))
    return pl.pallas_call(
        body, name="tail_fwd_bwd", grid=(t // tm,),
        in_specs=[rowd, pl.BlockSpec((tm, PLE_DIM), lambda i: (i, 0)), rowd,
                  fixed((1, D_MODEL)), fixed((1, D_MODEL)), fixed((D_MODEL, D_MODEL)), fixed((PLE_DIM, D_MODEL))],
        out_specs=[rowd, fixed((D_MODEL, D_MODEL)), fixed((PLE_DIM, D_MODEL)),
                   fixed((8, D_MODEL)), fixed((8, D_MODEL)), fixed((8, 128))],
        out_shape=[jax.ShapeDtypeStruct((t, D_MODEL), F32), jax.ShapeDtypeStruct((D_MODEL, D_MODEL), F32),
                   jax.ShapeDtypeStruct((PLE_DIM, D_MODEL), F32), jax.ShapeDtypeStruct((8, D_MODEL), F32),
                   jax.ShapeDtypeStruct((8, D_MODEL), F32), jax.ShapeDtypeStruct((8, 128), F32)],
        compiler_params=_cp(("arbitrary",), VMEM_LIMIT),
    )(x2, p, target, g_ple, g_final, w_pg, w_pp)


def in_proj_bwd(pieces, weights, x, dx1, g, name="in_proj_bwd", tm=256):
    t = x.shape[0]
    tm = min(tm, t)
    k = len(pieces)
    assert all(c0 % wd == 0 and w0 % wd == 0 for (_, c0, wd), (_, w0) in zip(pieces, weights))

    def body(*refs):
        p_refs, w_refs = refs[:k], refs[k:2 * k]
        x_ref, dx1_ref, g_ref, dx_ref, dg_ref = refs[2 * k:]

        @pl.when(pl.program_id(0) == 0)
        def _():
            dg_ref[...] = jnp.zeros_like(dg_ref)

        dh = _nt(_bf(p_refs[0][...]), w_refs[0][...])
        for pr, wr in zip(p_refs[1:], w_refs[1:]):
            dh = dh + _nt(_bf(pr[...]), wr[...])
        r, n = _rms(x_ref[...])
        dx, dgc = _rms_bwd(dh, g_ref[...], r, n)
        dx_ref[...] = dx1_ref[...] + dx
        dg_ref[...] += jnp.broadcast_to(jnp.sum(dgc, axis=0, keepdims=True), (8, D_MODEL))

    rowd = pl.BlockSpec((tm, D_MODEL), lambda i: (i, 0))
    return pl.pallas_call(
        body, name=name, grid=(t // tm,),
        in_specs=[pl.BlockSpec((tm, wd), functools.partial(lambda i, cb: (i, cb), cb=c0 // wd))
                  for _, c0, wd in pieces]
        + [pl.BlockSpec((w.shape[0], wd), functools.partial(lambda i, cb: (0, cb), cb=w0 // wd))
           for (w, w0), (_, _, wd) in zip(weights, pieces)]
        + [rowd, rowd, pl.BlockSpec((1, D_MODEL), lambda i: (0, 0))],
        out_specs=[rowd, pl.BlockSpec((8, D_MODEL), lambda i: (0, 0))],
        out_shape=[jax.ShapeDtypeStruct((t, D_MODEL), F32), jax.ShapeDtypeStruct((8, D_MODEL), F32)],
        compiler_params=_cp(("arbitrary",), VMEM_LIMIT),
    )(*[a for a, _, _ in pieces], *[w for w, _ in weights], x, dx1, g)


def adamw(w, g, m, v, name, rows_cap=256, dep=None):
    lead = w.shape[:-2]
    r, c = w.shape[-2:]
    tr = r
    for cand in range(8, min(r, rows_cap) + 1, 8):
        if r % cand == 0:
            tr = cand

    def body(w_ref, g_ref, m_ref, v_ref, *rest):
        d_ref, mo_ref, vo_ref = rest[-3:]
        gv = g_ref[...]
        mn = ADAM_B1 * m_ref[...] + (1.0 - ADAM_B1) * gv
        vn = ADAM_B2 * v_ref[...] + (1.0 - ADAM_B2) * (gv * gv)
        m_hat = mn / (1.0 - ADAM_B1 ** ADAM_STEP)
        v_hat = vn / (1.0 - ADAM_B2 ** ADAM_STEP)
        d_ref[...] = -ADAM_LR * (m_hat / (jnp.sqrt(v_hat) + ADAM_EPS) + ADAM_WD * w_ref[...])
        mo_ref[...] = mn
        vo_ref[...] = vn

    spec = pl.BlockSpec((None,) * len(lead) + (tr, c), lambda i: (0,) * len(lead) + (i, 0))
    extra = [] if dep is None else [dep]
    return pl.pallas_call(
        body, name=name, grid=(r // tr,),
        in_specs=[spec] * 4 + [pl.BlockSpec((8, 128), lambda i: (0, 0))] * len(extra), out_specs=[spec] * 3,
        out_shape=[jax.ShapeDtypeStruct(w.shape, F32)] * 3,
        compiler_params=_cp(("parallel",), VMEM_LIMIT),
    )(w, g.reshape(w.shape), m, v, *extra)


def _w_in_shards(dwp):
    cs = D_IN // N_CHIPS
    regions = ((0, SPLIT_Z, P_QA), (SPLIT_Z, SPLIT_Z + 8, P_BD - SPLIT_Z), (SPLIT_Z + 8, D_IN, -(SPLIT_Z + 8)))

    def original(lo, hi):
        parts = [dwp[:, max(lo, a) + off:min(hi, e) + off] for a, e, off in regions if max(lo, a) < min(hi, e)]
        return parts[0] if len(parts) == 1 else jnp.concatenate(parts, axis=1)

    return jnp.stack([original(s * cs, (s + 1) * cs) for s in range(N_CHIPS)])


class Standalone:
    def __init__(self, later_weights):
        self.later_weights = later_weights

    def begin(self, *a):
        return 0.0

    forward = exchange = join = begin

    def finish(self, after):
        return self.later_weights


def local_step(x3d, p3d, target3d, g4, small, later, early):
    b, s, _ = x3d.shape
    t = b * s
    x = x3d.reshape(t, D_MODEL)
    p = p3d.reshape(t, PLE_DIM)
    target = target3d.reshape(t, D_MODEL)
    cut = SPLIT_Z - 2 * (D_IN // N_CHIPS)
    w_inp = jnp.concatenate([g4[2][:, cut + 8:], g4[3], g4[0], g4[1], g4[2][:, :cut], g4[2][:, cut:cut + 8],
                             jnp.zeros((D_MODEL, 120), BF16)], axis=1)
    al_row = jnp.pad(small["a_log"].reshape(1, B_HEADS), ((0, 0), (0, 128 - B_HEADS)))
    dtb_row = jnp.pad(small["dt_bias"].reshape(1, B_HEADS), ((0, 0), (0, 128 - B_HEADS)))
    conv_w8 = jnp.pad(small["conv_w"].reshape(CONV_K, CONV_CH), ((0, 8 - CONV_K), (0, 0)))
    w_on = small["w_onorm"].reshape(1, B_DIM)
    g_mix, g_ffn = small["g_mix"].reshape(1, D_MODEL), small["g_ffn"].reshape(1, D_MODEL)
    g_ple, g_final = small["g_ple"].reshape(1, D_MODEL), small["g_final"].reshape(1, D_MODEL)
    bias_band = make_bias_band(small["rel_bias"].reshape(A_HEADS, N_REL))

    tok = later.begin()
    proj, h1, bd32 = rms_matmul(x, g_mix + tok, w_inp, "in_proj", tm=1024, tn_cap=1152)
    y_a, lse = attn_fwd(proj, bias_band, b, s)
    tok = later.forward(lse)
    c = conv_fwd(proj, conv_w8 + tok, b, s)
    u, wk, qg, kdec, pm, egl, tmat = dn_prep(c, bd32, al_row, dtb_row, b, s)
    o_b, states = dn_scan_fwd(u, wk, qg, kdec, pm, egl, b, s)
    wts = later.finish(o_b)
    x1, merged = mid_fwd(x, y_a, o_b, proj, w_on, wts["w_branch_a"], wts["w_branch_b"], wts["w_out"])
    gate, up, act, h2 = ffn_up(x1, g_ffn, wts["w_gate_up"])
    x2 = matmul_residual(act, wts["w_down"], x1, "ffn_down")

    dx2, dw_pg, dw_pp, dg_ple, dg_final, loss = tail_fwd_bwd(
        x2, p, target, g_ple, g_final, wts["w_ple_gate"], wts["w_ple_proj"])
    dgate, dup = ffn_act_bwd(dx2, gate, up, wts["w_down"])
    w_gu = wts["w_gate_up"]
    dx1, dg_ffn = in_proj_bwd([(dgate, 0, D_FF), (dup, 0, D_FF)], [(w_gu, 0), (w_gu, D_FF)], x1, dx2, g_ffn,
                              name="ffn_in_bwd")
    dw_down = matmul_tn(act, dx2, "dw_down")
    dw_gu = matmul_tn(h2, dgate, "dw_gate", width=2 * D_FF, tiles_major=True)
    dw_gu = matmul_tn(h2, dup, "dw_up", into=dw_gu, col0=D_FF, width=2 * D_FF, tiles_major=True)
    dy_a, do_b, dz, dgates, dw_out, dwa, dwb, dw_on = mid_bwd(
        dx1, merged, y_a, o_b, proj, w_on, wts["w_branch_a"], wts["w_branch_b"], wts["w_out"])
    tok = early.begin(dict(w_branch_a=dwa, w_branch_b=dwb, w_out=dw_out, w_gate_up=dw_gu, w_down=dw_down,
                           w_ple_gate=dw_pg, w_ple_proj=dw_pp))
    ddw, ddwk, ddqg, ddkdec, ddp, ddegl = dn_scan_bwd(u, wk, qg, kdec, pm, egl + tok, states, do_b, b, s)
    tok = early.exchange(ddegl)
    dc3, dbd, dal, ddtb = dn_post_bwd(c, bd32, al_row + tok, dtb_row, tmat, ddw, ddwk, ddqg, ddkdec, ddp, ddegl, b, s)
    dconv, dconv_w = conv_bwd(proj, conv_w8, dc3, b, s)
    dqa, dka, dva, dbt, dbf = attn_bwd(proj, bias_band, y_a, lse, dy_a, b, s)
    tok = early.join(dqa)
    d_rel = bias_band_grad(dbt, dbf)

    pieces = [dgates, dqa, dka, dva, dconv, dz, dbd]
    bounds = [0, 2048, 2560, 3072, 3584, 5120, 5632, 5760]
    windows = [(dgates, 0, 2048), (dqa, 0, 512), (dka, 0, 512), (dva, 0, 512), (dconv, 0, 512), (dconv, 512, 512),
               (dconv, 1024, 512), (dz, 0, 512), (dbd, 0, 128)]
    w_cols = [0, P_QA, P_KA, P_VA, P_CONV, P_CONV + 512, P_CONV + 1024, P_Z, P_BD]
    dx, dg_mix = in_proj_bwd(windows, [(w_inp, c0) for c0 in w_cols], x, dx1, g_mix + tok)
    dwp = None
    for k, pc in enumerate(pieces):
        dwp = matmul_tn(h1, pc, "dw_in_%d" % k, into=dwp, col0=bounds[k], width=P_WIDTH)
    reduced_early = early.finish(dwp)
    dw_in = _w_in_shards(dwp)

    grads = dict(w_in=dw_in, w_branch_a=dwa, w_branch_b=dwb, w_out=dw_out, w_gate_up=dw_gu, w_down=dw_down,
                 w_ple_gate=dw_pg, w_ple_proj=dw_pp)
    small_grads = dict(g_mix=dg_mix[0], g_ffn=dg_ffn[0], g_ple=dg_ple[0], g_final=dg_final[0],
                       conv_w=dconv_w[:CONV_K].reshape(-1), rel_bias=d_rel.reshape(-1), w_onorm=dw_on[0],
                       a_log=dal[0, :B_HEADS], dt_bias=ddtb[0, :B_HEADS], loss=loss[0, :1])
    return dx.reshape(b, s, D_MODEL), grads, small_grads, reduced_early


BIG = (("w_in", (D_MODEL, D_IN), 1), ("w_branch_a", (A_WIDTH, D_MODEL), 1), ("w_branch_b", (B_WIDTH, D_MODEL), 1),
       ("w_out", (D_MODEL, D_MODEL), 0), ("w_gate_up", (D_MODEL, 2 * D_FF), 1), ("w_down", (D_FF, D_MODEL), 0),
       ("w_ple_gate", (D_MODEL, D_MODEL), 0), ("w_ple_proj", (PLE_DIM, D_MODEL), 1))
N_CHIPS = 4
FIRST_WEIGHTS = ("w_in",)
LATER_WEIGHTS = ("w_branch_a", "w_branch_b", "w_out", "w_gate_up", "w_down", "w_ple_gate", "w_ple_proj")
LATE_GRADS = ("w_in",)
EARLY_GRADS = ("w_branch_a", "w_branch_b", "w_out", "w_gate_up", "w_down", "w_ple_gate", "w_ple_proj")


def _items(names):
    return [it for it in BIG if it[0] in names]


def _shard_shape(shape, axis):
    return (shape[0] // N_CHIPS, shape[1]) if axis == 0 else (shape[0], shape[1] // N_CHIPS)


def _width_groups(names):
    groups = {}
    for n, shape, axis in _items(names):
        rs, cs = _shard_shape(shape, axis)
        groups.setdefault(cs, []).append((n, rs))
    return sorted(groups.items())


def grad_buffers(grads, names):
    info = {n: (shape, axis) for n, shape, axis in _items(names)}
    bufs = []
    for cs, members in _width_groups(names):
        segs = []
        for n, rs in members:
            g = grads[n].astype(BF16)
            if g.ndim == 2:
                g = (g.reshape(N_CHIPS, rs, cs) if info[n][1] == 0
                     else jnp.transpose(g.reshape(rs, N_CHIPS, cs), (1, 0, 2)))
            segs.append(g)
        bufs.append(segs[0] if len(segs) == 1 else jnp.concatenate(segs, axis=1))
    return bufs


def split_buffers(reduced, names):
    out = {}
    for (cs, members), buf in zip(_width_groups(names), reduced):
        r0 = 0
        for n, rs in members:
            out[n] = buf[r0:r0 + rs]
            r0 += rs
    return out


def _place():
    return lax.axis_index("x"), lax.axis_index("y"), lax.axis_index("c")


ANY = pl.BlockSpec(memory_space=pl.ANY)


def _gathered_shape(item):
    n, shape, _ = item
    return (N_CHIPS,) + _shard_shape(shape, 1) if n == "w_in" else shape


def _gather_block(o_ref, item, cx, cy, hf):
    n, shape, axis = item
    rs, cs = _shard_shape(shape, axis)
    hr = rs // 2
    ci = 2 * cx + cy
    if n == "w_in":
        return o_ref.at[ci, pl.ds(pl.multiple_of(hf * hr, 16), hr), :]
    if axis == 0:
        return o_ref.at[pl.ds(pl.multiple_of(ci * rs + hf * hr, 16), hr), :]
    return o_ref.at[pl.ds(pl.multiple_of(hf * hr, 16), hr), pl.ds(pl.multiple_of(ci * cs, 128), cs)]


def _own_half(w_ref, item, c):
    hr = _shard_shape(item[1], item[2])[0] // 2
    return w_ref.at[pl.ds(pl.multiple_of(c * hr, 16), hr), :]


def _gather_slot(o_ref, item, cx, cy):
    n, shape, axis = item
    rs, cs = _shard_shape(shape, axis)
    ci = 2 * cx + cy
    if n == "w_in":
        return o_ref.at[ci]
    if axis == 0:
        return o_ref.at[pl.ds(pl.multiple_of(ci * rs, 16), rs), :]
    return o_ref.at[:, pl.ds(pl.multiple_of(ci * cs, 128), cs)]


def _other_chips(x, y):
    return [(1 - x, y), (x, 1 - y), (1 - x, 1 - y)]


def allgather_weights(shards, names):
    items = _items(names)
    nw = len(items)

    def body(*refs):
        w_refs, o_refs = refs[:nw], refs[nw:2 * nw]
        send_sems, recv_sems = refs[2 * nw:]
        x, y, c = _place()
        sibling = (x, y, 1 - c)
        chips = _other_chips(x, y)

        def copy(k, src, dst, to):
            return pltpu.make_async_remote_copy(src_ref=src, dst_ref=dst, send_sem=send_sems.at[k],
                                                recv_sem=recv_sems.at[k], device_id=to, device_id_type=MESH)

        def blk(i, cx, cy, hf):
            return _gather_block(o_refs[i], items[i], cx, cy, hf)

        def my_half(i):
            return _own_half(w_refs[i], items[i], c)

        def own(i):
            return _gather_slot(o_refs[i], items[i], x, y)

        first = [copy(7 * i + j, my_half(i), blk(i, x, y, c), (*chip_, c))
                 for i in range(nw) for j, chip_ in enumerate(chips)]
        first += [copy(7 * i + 6, w_refs[i], own(i), sibling) for i in range(nw)]
        for cp in first:
            cp.start()
        passed = []
        for i in range(nw):
            for j, chip_ in enumerate(chips):
                copy(7 * i + j, my_half(i), blk(i, *chip_, c), (*chip_, c)).wait_recv()
                fwd = copy(7 * i + 3 + j, blk(i, *chip_, c), blk(i, *chip_, c), sibling)
                fwd.start()
                passed.append(fwd)
        for i in range(nw):
            for j, chip_ in enumerate(chips):
                copy(7 * i + 3 + j, my_half(i), blk(i, *chip_, 1 - c), sibling).wait_recv()
            copy(7 * i + 6, w_refs[i], own(i), sibling).wait_recv()
        for cp in first + passed:
            cp.wait_send()

    outs = pl.pallas_call(
        body, name="allgather_weights",
        in_specs=[ANY] * nw, out_specs=[ANY] * nw,
        out_shape=[jax.ShapeDtypeStruct(_gathered_shape(it), BF16) for it in items],
        scratch_shapes=[pltpu.SemaphoreType.DMA((7 * nw,)), pltpu.SemaphoreType.DMA((7 * nw,))],
    )(*[shards[it[0]] for it in items])
    return {it[0]: o for it, o in zip(items, outs)}


HBM_SPEC = pl.BlockSpec(memory_space=pltpu.HBM)
SEM_SPEC = pl.BlockSpec(memory_space=pltpu.SEMAPHORE)
EFFECT = pltpu.SideEffectType.DATAFLOW_SIDE_EFFECTING


def _in_hbm(a):
    return pltpu.with_memory_space_constraint(a, pltpu.HBM)


def copies_start(name, bufs, ncopies, plan):
    nb = len(bufs)

    def body(*refs):
        in_refs, send_sems, recv_sems, token = refs[:nb], refs[nb], refs[nb + 1], refs[-1]
        for k, (src, dst, to) in enumerate(plan(in_refs)):
            pltpu.make_async_remote_copy(src_ref=src, dst_ref=dst, send_sem=send_sems.at[k],
                                         recv_sem=recv_sems.at[k], device_id=to, device_id_type=MESH).start()
        token[...] = jnp.zeros_like(token)

    outs = pl.pallas_call(
        body, name=name,
        in_specs=[HBM_SPEC] * nb,
        out_specs=(SEM_SPEC, SEM_SPEC, *[HBM_SPEC] * nb, pl.BlockSpec(memory_space=pltpu.VMEM)),
        out_shape=(pltpu.SemaphoreType.DMA((ncopies,)), pltpu.SemaphoreType.DMA((ncopies,)),
                   *[pltpu.HBM(b.shape, b.dtype) for b in bufs], jax.ShapeDtypeStruct((8, 128), F32)),
        input_output_aliases={i: 2 + i for i in range(nb)},
        compiler_params=pltpu.CompilerParams(has_side_effects=EFFECT),
    )(*[_in_hbm(b) for b in bufs])
    return outs[0], outs[1], list(outs[2:2 + nb]), outs[-1][0, 0]


def copies_wait(name, send_sems, recv_sems, bufs, after, plan):
    nb = len(bufs)

    def body(*refs):
        in_refs, s_sems, r_sems = refs[:nb], refs[nb], refs[nb + 1]
        for k, (src, dst, to) in enumerate(plan(in_refs)):
            cp = pltpu.make_async_remote_copy(src_ref=src, dst_ref=dst, send_sem=s_sems.at[k],
                                              recv_sem=r_sems.at[k], device_id=to, device_id_type=MESH)
            cp.wait_send()
            cp.wait_recv()

    return list(pl.pallas_call(
        body, name=name,
        in_specs=[HBM_SPEC] * nb + [SEM_SPEC, SEM_SPEC, ANY],
        out_specs=tuple([HBM_SPEC] * nb),
        out_shape=tuple(pltpu.HBM(b.shape, b.dtype) for b in bufs),
        input_output_aliases={i: i for i in range(nb)},
        compiler_params=pltpu.CompilerParams(has_side_effects=EFFECT),
    )(*bufs, send_sems, recv_sems, after))


def _landing(shape, dtype):
    return _in_hbm(lax.empty(shape, dtype))


class LaterWeights:
    def __init__(self, shards):
        self.items = _items(LATER_WEIGHTS)
        self.shards = shards
        self.nw = len(self.items)

    def _ici_plan(self, refs):
        x, y, c = _place()
        w_refs, o_refs = refs[:self.nw], refs[self.nw:]
        plan = [(_own_half(w_refs[i], it, c), _gather_block(o_refs[i], it, x, y, c), (*chip_, c))
                for i, it in enumerate(self.items) for chip_ in _other_chips(x, y)]
        return plan + [(w_refs[i], _gather_slot(o_refs[i], it, x, y), (x, y, 1 - c))
                       for i, it in enumerate(self.items)]

    def _d2d_plan(self, refs):
        x, y, c = _place()
        return [(_gather_block(refs[i], it, *chip_, c), _gather_block(refs[i], it, *chip_, c), (x, y, 1 - c))
                for i, it in enumerate(self.items) for chip_ in _other_chips(x, y)]

    def _d2d_wait_plan(self, refs):
        x, y, c = _place()
        return [(_gather_block(refs[i], it, *chip_, c), _gather_block(refs[i], it, *chip_, 1 - c), (x, y, 1 - c))
                for i, it in enumerate(self.items) for chip_ in _other_chips(x, y)]

    def _ici_wait_plan(self, refs):
        x, y, c = _place()
        w_refs, o_refs = refs[:self.nw], refs[self.nw:]
        plan = [(_own_half(w_refs[i], it, c), _gather_block(o_refs[i], it, *chip_, c), (*chip_, c))
                for i, it in enumerate(self.items) for chip_ in _other_chips(x, y)]
        return plan + [(w_refs[i], _gather_slot(o_refs[i], it, x, y), (x, y, 1 - c))
                       for i, it in enumerate(self.items)]

    def begin(self):
        srcs = [self.shards[it[0]] for it in self.items]
        lands = [_landing(_gathered_shape(it), BF16) for it in self.items]
        self.s1, self.r1, self.b1, tok = copies_start("gather_ici_start", srcs + lands, 4 * self.nw, self._ici_plan)
        return tok

    def forward(self, after):
        b1 = copies_wait("gather_ici_wait", self.s1, self.r1, self.b1, after, self._ici_wait_plan)
        self.s2, self.r2, self.b2, tok = copies_start("gather_d2d_start", b1[self.nw:], 3 * self.nw, self._d2d_plan)
        return tok

    def finish(self, after):
        outs = copies_wait("gather_d2d_wait", self.s2, self.r2, self.b2, after, self._d2d_wait_plan)
        return {it[0]: o for it, o in zip(self.items, outs)}


def small_allreduce(v, name):
    r = v.shape[0]

    def body(v_ref, o_ref, buf, send_sems, recv_sems):
        x, y, c = _place()
        me = 4 * x + 2 * y + c
        buf[me] = v_ref[...]
        flips = [(fx, fy, fc) for fx in (0, 1) for fy in (0, 1) for fc in (0, 1)][1:]
        peers = [((1 - x) if fx else x, (1 - y) if fy else y, (1 - c) if fc else c) for fx, fy, fc in flips]

        def copy(k, slot, to):
            return pltpu.make_async_remote_copy(src_ref=v_ref, dst_ref=buf.at[slot], send_sem=send_sems.at[k],
                                                recv_sem=recv_sems.at[k], device_id=to, device_id_type=MESH)

        sends = [copy(k, me, peer) for k, peer in enumerate(peers)]
        for cp in sends:
            cp.start()
        for k, (px, py, pc) in enumerate(peers):
            copy(k, 4 * px + 2 * py + pc, (px, py, pc)).wait_recv()
        for cp in sends:
            cp.wait_send()
        acc = buf[0]
        for d in range(1, 8):
            acc = acc + buf[d]
        o_ref[...] = acc

    return pl.pallas_call(
        body, name=name,
        in_specs=[pl.BlockSpec(memory_space=pltpu.VMEM)], out_specs=pl.BlockSpec(memory_space=pltpu.VMEM),
        out_shape=jax.ShapeDtypeStruct((r, 128), F32),
        scratch_shapes=[pltpu.VMEM((8, r, 128), F32), pltpu.SemaphoreType.DMA((7,)), pltpu.SemaphoreType.DMA((7,))],
    )(v)


def add_halves(g, other, place):
    half, wd = other.shape[1:]
    tr = _tile_rows(half, wd)
    nblk = half // tr

    def body(pref, g0, g1, g2, g3, o0, o1, o2, o3, pf_ref, pb_ref):
        f = lambda r: r[...].astype(F32)
        pf_ref[...] = f(g0) + f(o0)
        pb_ref[0] = _bf(f(g1) + f(o1))
        pb_ref[1] = _bf(f(g2) + f(o2))
        pb_ref[2] = _bf(f(g3) + f(o3))

    gspec = lambda k: pl.BlockSpec((None, tr, wd), lambda i, pr: ((pr[0] + k) % N_CHIPS, pr[1] * nblk + i, 0))
    ospec = lambda k: pl.BlockSpec((None, tr, wd), lambda i, pr: ((pr[0] + k) % N_CHIPS, i, 0))
    return pl.pallas_call(
        body, name="add_halves",
        grid_spec=pltpu.PrefetchScalarGridSpec(
            num_scalar_prefetch=1, grid=(nblk,),
            in_specs=[gspec(0), gspec(1), gspec(2), gspec(3), ospec(0), ospec(1), ospec(2), ospec(3)],
            out_specs=[pl.BlockSpec((tr, wd), lambda i, pr: (i, 0)),
                       pl.BlockSpec((3, tr, wd), lambda i, pr: (0, i, 0))]),
        out_shape=[jax.ShapeDtypeStruct((half, wd), F32), jax.ShapeDtypeStruct((3, half, wd), BF16)],
        compiler_params=_cp(("parallel",), VMEM_LIMIT),
    )(place, g, g, g, g, other, other, other, other)


def _tile_rows(n, width):
    best = 16
    for t in range(16, max(16, (384 * 1024) // width) + 1, 16):
        if n % t == 0:
            best = t
    assert n % best == 0
    return best


def add_partials(pf, got, place):
    half, wd = pf.shape
    tr = _tile_rows(half, wd)

    def body(pref, pf_ref, got_ref, o_ref):
        o_ref[...] = ((pf_ref[...] + got_ref[0].astype(F32)) + got_ref[1].astype(F32)) + got_ref[2].astype(F32)

    return pl.pallas_call(
        body, name="add_partials",
        grid_spec=pltpu.PrefetchScalarGridSpec(
            num_scalar_prefetch=1, grid=(half // tr,),
            in_specs=[pl.BlockSpec((tr, wd), lambda i, pr: (i, 0)),
                      pl.BlockSpec((3, tr, wd), lambda i, pr: (0, i, 0))],
            out_specs=pl.BlockSpec((None, tr, wd), lambda i, pr: (pr[1], i, 0))),
        out_shape=jax.ShapeDtypeStruct((2, half, wd), F32),
        compiler_params=_cp(("parallel",), VMEM_LIMIT),
    )(place, pf, got)


class GradReduce:
    def __init__(self, place, names, tag):
        self.place, self.names, self.tag = place, names, tag
        self.nb = len(_width_groups(names))

    def _swap_plan(self, refs):
        x, y, c = _place()
        plan = []
        for g_ref, o_ref in zip(refs[:self.nb], refs[self.nb:]):
            half = o_ref.shape[1]
            plan.append((g_ref.at[:, pl.ds(pl.multiple_of((1 - c) * half, 16), half), :], o_ref, (x, y, 1 - c)))
        return plan

    def _exchange_plan(self, refs):
        x, y, c = _place()
        me = 2 * x + y
        return [(p_ref.at[k - 1], o_ref.at[k - 1], (((me + k) % N_CHIPS) // 2, ((me + k) % N_CHIPS) % 2, c))
                for p_ref, o_ref in zip(refs[:self.nb], refs[self.nb:]) for k in range(1, N_CHIPS)]

    def _join_plan(self, refs):
        x, y, c = _place()
        return [(r.at[c], r.at[c], (x, y, 1 - c)) for r in refs]

    def _join_wait_plan(self, refs):
        x, y, c = _place()
        return [(r.at[c], r.at[1 - c], (x, y, 1 - c)) for r in refs]

    def begin(self, grads):
        gs = grad_buffers(grads, self.names)
        lands = [_landing((N_CHIPS, g.shape[1] // 2, g.shape[2]), BF16) for g in gs]
        self.s1, self.r1, self.b1, tok = copies_start(self.tag + "_swap_start", gs + lands, self.nb, self._swap_plan)
        return tok

    def exchange(self, after):
        b1 = copies_wait(self.tag + "_swap_wait", self.s1, self.r1, self.b1, after, self._swap_plan)
        sums = [add_halves(g, other, self.place) for g, other in zip(b1[:self.nb], b1[self.nb:])]
        self.pfs = [pf for pf, _ in sums]
        pbs = [pb for _, pb in sums]
        lands = [_landing(pb.shape, BF16) for pb in pbs]
        self.s2, self.r2, self.b2, tok = copies_start(self.tag + "_exchange_start", pbs + lands, 3 * self.nb,
                                                      self._exchange_plan)
        return tok

    def join(self, after):
        b2 = copies_wait(self.tag + "_exchange_wait", self.s2, self.r2, self.b2, after, self._exchange_plan)
        boths = [add_partials(pf, got, self.place) for pf, got in zip(self.pfs, b2[self.nb:])]
        self.s3, self.r3, self.b3, tok = copies_start(self.tag + "_join_start", boths, self.nb, self._join_plan)
        return tok

    def finish(self, after):
        boths = copies_wait(self.tag + "_join_wait", self.s3, self.r3, self.b3, after, self._join_wait_plan)
        return split_buffers([b.reshape(-1, b.shape[2]) for b in boths], self.names)


SMALL = (("g_mix", D_MODEL), ("g_ffn", D_MODEL), ("g_ple", D_MODEL), ("g_final", D_MODEL),
         ("conv_w", CONV_K * CONV_CH), ("rel_bias", A_HEADS * N_REL), ("w_onorm", B_DIM),
         ("a_log", B_HEADS), ("dt_bias", B_HEADS), ("loss", 1))


def _pad128(v):
    v = v.reshape(-1)
    return jnp.pad(v, (0, -v.shape[0] % 128))


def pack_small(d, names, rows):
    flat = jnp.concatenate([_pad128(d[n]) for n in names]).reshape(-1, 128)
    return jnp.pad(flat, ((0, rows - flat.shape[0]), (0, 0)))


def unpack_small(flat, names_sizes):
    out, r0 = {}, 0
    v = flat.reshape(-1)
    for n, size in names_sizes:
        out[n] = v[r0:r0 + size]
        r0 += -(-size // 128) * 128
    return out


def kernel(x, p, g_mix, w_in, conv_w, a_log, dt_bias, rel_bias, w_onorm, w_branch_a, w_branch_b, w_out, g_ffn, w_gate_up, w_down, g_ple, w_ple_gate, w_ple_proj, g_final, loss_target, m_g_mix, m_w_in, m_conv_w, m_a_log, m_dt_bias, m_rel_bias, m_w_onorm, m_w_branch_a, m_w_branch_b, m_w_out, m_g_ffn, m_w_gate_up, m_w_down, m_g_ple, m_w_ple_gate, m_w_ple_proj, m_g_final, v_g_mix, v_w_in, v_conv_w, v_a_log, v_dt_bias, v_rel_bias, v_w_onorm, v_w_branch_a, v_w_branch_b, v_w_out, v_g_ffn, v_w_gate_up, v_w_down, v_g_ple, v_w_ple_gate, v_w_ple_proj, v_g_final):
    names = ["g_mix", "w_in", "conv_w", "a_log", "dt_bias", "rel_bias", "w_onorm", "w_branch_a", "w_branch_b",
             "w_out", "g_ffn", "w_gate_up", "w_down", "g_ple", "w_ple_gate", "w_ple_proj", "g_final"]
    w = dict(zip(names, [g_mix, w_in, conv_w, a_log, dt_bias, rel_bias, w_onorm, w_branch_a, w_branch_b, w_out,
                         g_ffn, w_gate_up, w_down, g_ple, w_ple_gate, w_ple_proj, g_final]))
    m = dict(zip(names, [m_g_mix, m_w_in, m_conv_w, m_a_log, m_dt_bias, m_rel_bias, m_w_onorm, m_w_branch_a,
                         m_w_branch_b, m_w_out, m_g_ffn, m_w_gate_up, m_w_down, m_g_ple, m_w_ple_gate,
                         m_w_ple_proj, m_g_final]))
    v = dict(zip(names, [v_g_mix, v_w_in, v_conv_w, v_a_log, v_dt_bias, v_rel_bias, v_w_onorm, v_w_branch_a,
                         v_w_branch_b, v_w_out, v_g_ffn, v_w_gate_up, v_w_down, v_g_ple, v_w_ple_gate,
                         v_w_ple_proj, v_g_final]))
    xi, yi, ci = _place()
    chip = 2 * xi + yi
    big_names = [n for n, _, _ in BIG]

    shards2d = {n: w[n].reshape(w[n].shape[-2:]) for n in big_names}
    shards_bf = {n: a.astype(BF16) for n, a in shards2d.items()}
    g4 = allgather_weights(shards_bf, FIRST_WEIGHTS)["w_in"]
    place = jnp.stack([chip, ci]).astype(jnp.int32)
    conv_sh = jnp.where(ci == 0, w["conv_w"].reshape(CONV_K, CONV_CH // N_CHIPS), 0.0)
    conv_slots = lax.dynamic_update_slice(jnp.zeros((N_CHIPS, CONV_K, CONV_CH // N_CHIPS), F32), conv_sh[None],
                                          (chip, 0, 0))
    conv_all = small_allreduce(conv_slots.reshape(-1, 128), "gather_conv_w")
    conv_full = jnp.transpose(conv_all.reshape(N_CHIPS, CONV_K, CONV_CH // N_CHIPS), (1, 0, 2)).reshape(CONV_K, CONV_CH)
    small = {n: w[n] for n in names if n not in big_names}
    small["conv_w"] = conv_full

    grad_x, grads, small_grads, reduced_early = local_step(
        x, p[0], loss_target, g4, small, LaterWeights(shards_bf), GradReduce(place, EARLY_GRADS, "grads"))

    late = GradReduce(place, LATE_GRADS, "late")
    tok = late.begin(grads)
    small_names = [n for n, _ in SMALL]
    small_grads["loss"] = small_grads["loss"] + tok
    red_flat = small_allreduce(pack_small(small_grads, small_names, 112), "allreduce_small")
    red = unpack_small(red_flat, SMALL)
    dep = jnp.full((8, 128), late.exchange(red_flat), F32)
    gshard = dict(reduced_early)
    loss = red["loss"][0]
    conv_g = lax.dynamic_slice(red["conv_w"].reshape(CONV_K, N_CHIPS, CONV_CH // N_CHIPS), (0, chip, 0),
                               (CONV_K, 1, CONV_CH // N_CHIPS))
    gsmall = {n: red[n].reshape(w[n].shape) for n in small_names if n not in ("loss", "conv_w")}
    gsmall["conv_w"] = conv_g.reshape(w["conv_w"].shape)

    grad, delta, new_m, new_v = {}, {}, {}, {}
    for n in list(EARLY_GRADS) + list(LATE_GRADS):
        if n in LATE_GRADS:
            late.join(v_)
            gshard.update(late.finish(v_))
        shp = w[n].shape
        d_, m_, v_ = adamw(shards2d[n], gshard[n], m[n].reshape(shp[-2:]), v[n].reshape(shp[-2:]), "adamw_" + n,
                           dep=dep if n in EARLY_GRADS else None)
        dep, v_ = lax.optimization_barrier((dep, v_))
        grad[n], delta[n], new_m[n], new_v[n] = gshard[n].reshape(shp), d_.reshape(shp), m_.reshape(shp), v_.reshape(shp)
    snames = [n for n in small_names if n != "loss"]
    ssizes = [(n, w[n].size) for n in snames]
    pk = lambda d: pack_small(d, snames, 64)
    d_, m_, v_ = adamw(pk(w), pk(gsmall), pk(m), pk(v), "adamw_small")
    ds, ms, vs = unpack_small(d_, ssizes), unpack_small(m_, ssizes), unpack_small(v_, ssizes)
    for n in snames:
        shp = w[n].shape
        grad[n], delta[n], new_m[n], new_v[n] = gsmall[n], ds[n].reshape(shp), ms[n].reshape(shp), vs[n].reshape(shp)

    return (loss, grad_x, *[grad[n] for n in names], *[delta[n] for n in names],
            *[new_m[n] for n in names], *[new_v[n] for n in names])
```

```python
import functools

import jax
import jax.numpy as jnp
from jax import lax
from jax.experimental import pallas as pl
from jax.experimental.pallas import tpu as pltpu

F32 = jnp.float32
BF16 = jnp.bfloat16
MESH = pl.DeviceIdType.MESH

D_MODEL = 1024
CHUNK = 64
PLE_DIM = 256
EPS = 1e-6
A_HEADS = 8
A_HEAD_DIM = 64
A_WIDTH = 512
A_LOOKBACK = 8
BAND = (A_LOOKBACK + 1) * CHUNK
TAIL = 3 * CHUNK
REL_CLIP = 128
N_REL = 2 * REL_CLIP + 1
B_HEADS = 4
B_DIM = 128
B_WIDTH = 512
CONV_K = 4
CONV_CH = 1536
D_FF = 2816
SPLIT_Z = 3584
D_IN = 5640
ADAM_LR, ADAM_B1, ADAM_B2, ADAM_EPS, ADAM_WD, ADAM_STEP = 0.001, 0.9, 0.999, 1e-08, 0.01, 10

P_GATES, P_QA, P_KA, P_VA, P_CONV, P_Z, P_BD, P_WIDTH = 0, 2048, 2560, 3072, 3584, 5120, 5632, 5760

VMEM_LIMIT = 56 * 1024 * 1024


def _cp(sem, vmem=None, **kw):
    return pltpu.CompilerParams(dimension_semantics=sem, vmem_limit_bytes=vmem, **kw)


def _tile(n, cap):
    best = None
    for t in range(128, cap + 1, 128):
        if n % t == 0:
            best = t
    assert best is not None, (n, cap)
    return best


def _nn(a, b, prec=None):
    return lax.dot_general(a, b, (((1,), (0,)), ((), ())), preferred_element_type=F32, precision=prec)


def _nt(a, b, prec=None):
    return lax.dot_general(a, b, (((1,), (1,)), ((), ())), preferred_element_type=F32, precision=prec)


def _tn(a, b, prec=None):
    return lax.dot_general(a, b, (((0,), (0,)), ((), ())), preferred_element_type=F32, precision=prec)


def _bnn(a, b, prec=None):
    return lax.dot_general(a, b, (((2,), (1,)), ((0,), (0,))), preferred_element_type=F32, precision=prec)


def _bnt(a, b, prec=None):
    return lax.dot_general(a, b, (((2,), (2,)), ((0,), (0,))), preferred_element_type=F32, precision=prec)


def _bf(a):
    return a.astype(BF16)


def _split(a):
    hi = a.astype(BF16)
    return hi, (a - hi.astype(F32)).astype(BF16)


def _split3(a):
    h1 = _bf(a)
    r1 = a - h1.astype(F32)
    h2 = _bf(r1)
    return h1, h2, _bf(r1 - h2.astype(F32))


def _bnn_exact(lhs_b, rhs):
    h1, h2, h3 = _split3(rhs)
    return _bnn(lhs_b, h1) + (_bnn(lhs_b, h2) + _bnn(lhs_b, h3))


def _bnn3(a, b):
    ah, al = a if isinstance(a, tuple) else _split(a)
    bh, bl = b if isinstance(b, tuple) else _split(b)
    return _bnn(ah, bh) + (_bnn(ah, bl) + _bnn(al, bh))


def _sigmoid(x):
    return 0.5 * jnp.tanh(0.5 * x) + 0.5


def _softplus(x):
    return jnp.maximum(x, 0.0) + jnp.log(1.0 + jnp.exp(-jnp.abs(x)))


def rms_matmul(x, g, w, name, tm=512, tn_cap=1024):
    t, d = x.shape
    n = w.shape[1]
    tm = min(tm, t)
    tn = _tile(n, tn_cap)

    nj = n // tn

    def body(x_ref, g_ref, w_ref, o_ref, h_ref, tail_ref):
        @pl.when(pl.program_id(1) == 0)
        def _():
            xv = x_ref[...]
            r = lax.rsqrt(jnp.mean(xv * xv, axis=-1, keepdims=True) + EPS)
            h_ref[...] = _bf(xv * r * g_ref[...])

        res = _nn(h_ref[...], w_ref[...])
        o_ref[...] = _bf(res)

        @pl.when(pl.program_id(1) == nj - 1)
        def _():
            tail_ref[...] = res[:, tn - 128:]

    return pl.pallas_call(
        body, name=name, grid=(t // tm, nj),
        in_specs=[pl.BlockSpec((tm, d), lambda i, j: (i, 0)),
                  pl.BlockSpec((1, d), lambda i, j: (0, 0)),
                  pl.BlockSpec((d, tn), lambda i, j: (0, j))],
        out_specs=[pl.BlockSpec((tm, tn), lambda i, j: (i, j)),
                   pl.BlockSpec((tm, d), lambda i, j: (i, 0)),
                   pl.BlockSpec((tm, 128), lambda i, j: (i, 0))],
        out_shape=[jax.ShapeDtypeStruct((t, n), BF16), jax.ShapeDtypeStruct((t, d), BF16),
                   jax.ShapeDtypeStruct((t, 128), F32)],
        compiler_params=_cp(("parallel", "arbitrary"), VMEM_LIMIT),
    )(x, g, w)


def matmul_tn(a, b, name, into=None, col0=0, width=None, tm=1024, tk_cap=1408, tn_cap=1408, tiles_major=False):
    m, k1 = a.shape
    n = b.shape[1]
    tm = min(tm, m)
    tk = _tile(k1, tk_cap)
    tn = _tile(n, tn_cap)
    while col0 % tn:
        tn = _tile(n, tn - 128)
    nk = m // tm
    c0 = col0 // tn

    def body(*refs):
        a_ref, b_ref, o_ref, acc = refs[0], refs[1], refs[-2], refs[-1]

        @pl.when(pl.program_id(2) == 0)
        def _():
            acc[...] = jnp.zeros_like(acc)

        acc[...] += _tn(_bf(a_ref[...]), _bf(b_ref[...]))

        @pl.when(pl.program_id(2) == nk - 1)
        def _():
            o_ref[...] = _bf(acc[...])

    in_specs = [pl.BlockSpec((tm, tk), lambda i, j, k: (k, i)),
                pl.BlockSpec((tm, tn), lambda i, j, k: (k, j))]
    args = [a, b]
    total = n if width is None else width
    aliases = {}
    if into is not None:
        in_specs.append(ANY)
        args.append(into)
        aliases = {2: 0}
    if tiles_major:
        out_spec = pl.BlockSpec((None, tk, tn), lambda i, j, k: (c0 + j, i, 0))
        out_shape = jax.ShapeDtypeStruct((total // tn, k1, tn), BF16)
    else:
        out_spec = pl.BlockSpec((tk, tn), lambda i, j, k: (i, c0 + j))
        out_shape = jax.ShapeDtypeStruct((k1, total), BF16)
    return pl.pallas_call(
        body, name=name, grid=(k1 // tk, n // tn, nk),
        in_specs=in_specs,
        out_specs=out_spec,
        out_shape=out_shape,
        scratch_shapes=[pltpu.VMEM((tk, tn), F32)],
        input_output_aliases=aliases,
        compiler_params=_cp(("parallel", "parallel", "arbitrary"), VMEM_LIMIT),
    )(*args)


def _tail_onehot(qi):
    r = lax.broadcasted_iota(jnp.int32, (384, TAIL), 0)
    kj = lax.broadcasted_iota(jnp.int32, (384, TAIL), 1)
    return (r == jnp.minimum(REL_CLIP + qi - kj, REL_CLIP) + REL_CLIP).astype(F32)


def bias_tail(rel_pad):
    def body(rb_ref, o_ref):
        parts = _split3(rb_ref[...])
        for qi in range(CHUNK):
            oh = _bf(_tail_onehot(qi))
            o_ref[qi] = _nn(parts[0], oh) + (_nn(parts[1], oh) + _nn(parts[2], oh))

    return pl.pallas_call(
        body, name="bias_tail",
        out_shape=jax.ShapeDtypeStruct((CHUNK, A_HEADS, TAIL), F32),
    )(rel_pad)


def bias_grad(db_t, db_far):
    def body(t_ref, f_ref, o_ref):
        acc = jnp.zeros((A_HEADS, 384), F32)
        for qi in range(CHUNK):
            oh = _bf(_tail_onehot(qi))
            parts = _split3(t_ref[qi])
            acc = acc + (_nt(parts[0], oh) + (_nt(parts[1], oh) + _nt(parts[2], oh)))
        far = jnp.sum(jnp.sum(f_ref[...], axis=2), axis=1, keepdims=True)
        lane = lax.broadcasted_iota(jnp.int32, (A_HEADS, 384), 1)
        o_ref[...] = acc + jnp.where(lane == 2 * REL_CLIP, far, 0.0)

    return pl.pallas_call(
        body, name="bias_grad",
        out_shape=jax.ShapeDtypeStruct((A_HEADS, 384), F32),
    )(db_t, db_far)


ATT_CB = 8


WIN = BAND + CHUNK


def _stack_heads(a, lane):
    return jnp.concatenate([jnp.where(lane < 64, a, 0.0), jnp.where(lane >= 64, a, 0.0)], axis=0)


def _fill_band_pads(k_ref, v_ref, kp, vp, s):
    z = jnp.zeros((A_LOOKBACK * CHUNK, 128), BF16)
    kp[pl.ds(0, A_LOOKBACK * CHUNK), :] = z
    vp[pl.ds(0, A_LOOKBACK * CHUNK), :] = z
    kp[pl.ds(A_LOOKBACK * CHUNK, s), :] = _bf(k_ref[...])
    vp[pl.ds(A_LOOKBACK * CHUNK, s), :] = _bf(v_ref[...])


def attn_fwd(proj, bias_band, b, s):
    t = b * s
    nc = s // CHUNK
    qb, kb_, vb_ = P_QA // 128, P_KA // 128, P_VA // 128

    nstep = nc // ATT_CB
    rows = ATT_CB * CHUNK

    def body(q_ref, k_ref, v_ref, b_ref, o_ref, lse_ref, kp, vp):
        n0 = pl.program_id(2) * ATT_CB

        @pl.when(n0 == 0)
        def _():
            _fill_band_pads(k_ref, v_ref, kp, vp, s)

        lane = lax.broadcasted_iota(jnp.int32, (2 * CHUNK, 128), 1)
        col = lax.broadcasted_iota(jnp.int32, (4 * CHUNK, WIN), 1)
        bias4 = b_ref[...]

        def two_pairs(i, carry):
            pps = (2 * i, 2 * i + 1)
            ns = [n0 + 2 * pp for pp in pps]
            r0s = [pl.multiple_of(pp * 2 * CHUNK, 2 * CHUNK) for pp in pps]
            starts = [pl.multiple_of(n * CHUNK, CHUNK) for n in ns]
            kbs = [kp[pl.ds(st_, WIN), :] for st_ in starts]
            vbs = [vp[pl.ds(st_, WIN), :] for st_ in starts]
            q4s = [_bf(_stack_heads(q_ref[pl.ds(r0, 2 * CHUNK), :] * (A_HEAD_DIM ** -0.5), lane)) for r0 in r0s]
            qks = [_nt(q4, kb) for q4, kb in zip(q4s, kbs)]
            scs = [jnp.where(col >= (A_LOOKBACK - n) * CHUNK, qk + bias4, -1e30) for n, qk in zip(ns, qks)]
            mxs = [jnp.max(sc, axis=1, keepdims=True) for sc in scs]
            ps = [jnp.exp(sc - mx) for sc, mx in zip(scs, mxs)]
            ls = [jnp.sum(p, axis=1, keepdims=True) for p in ps]
            o4s = [_nn(_bf(p), vb) / l for p, vb, l in zip(ps, vbs, ls)]
            for r0, o4, mx, l in zip(r0s, o4s, mxs, ls):
                lse4 = mx + jnp.log(l)
                o_ref[pl.ds(r0, 2 * CHUNK), :] = jnp.where(lane < 64, o4[:2 * CHUNK], o4[2 * CHUNK:])
                lse_ref[pl.ds(r0, 2 * CHUNK), :] = jnp.where(lane < 64, lse4[:2 * CHUNK], lse4[2 * CHUNK:])
            return carry

        lax.fori_loop(0, ATT_CB // 4, two_pairs, 0)

    return pl.pallas_call(
        body, name="attn_fwd", grid=(b, 4, nstep),
        in_specs=[pl.BlockSpec((rows, 128), lambda bb, m, n: (bb * nstep + n, qb + m)),
                  pl.BlockSpec((s, 128), lambda bb, m, n: (bb, kb_ + m)),
                  pl.BlockSpec((s, 128), lambda bb, m, n: (bb, vb_ + m)),
                  pl.BlockSpec((None, 4 * CHUNK, WIN), lambda bb, m, n: (m, 0, 0))],
        out_specs=[pl.BlockSpec((rows, 128), lambda bb, m, n: (bb * nstep + n, m)),
                   pl.BlockSpec((rows, 128), lambda bb, m, n: (bb * nstep + n, m))],
        out_shape=[jax.ShapeDtypeStruct((t, A_WIDTH), F32), jax.ShapeDtypeStruct((t, A_WIDTH), F32)],
        scratch_shapes=[pltpu.VMEM((s + A_LOOKBACK * CHUNK, 128), BF16),
                        pltpu.VMEM((s + A_LOOKBACK * CHUNK, 128), BF16)],
        compiler_params=_cp(("parallel", "parallel", "arbitrary"), VMEM_LIMIT),
    )(proj, proj, proj, bias_band)


def attn_bwd(proj, bias_band, y_a, lse, dy_a, b, s):
    t = b * s
    nc = s // CHUNK
    qb, kb_, vb_ = P_QA // 128, P_KA // 128, P_VA // 128
    pad = A_LOOKBACK * CHUNK
    nstep = nc // ATT_CB
    rows = ATT_CB * CHUNK

    def body(q_ref, k_ref, v_ref, b_ref, do_ref, o_ref, lse_ref,
             dq_ref, dk_ref, dv_ref, dbt_ref, dbf_ref, kp, vp, dkp, dvp):
        bb = pl.program_id(1)
        n0 = pl.program_id(2) * ATT_CB

        @pl.when(n0 == 0)
        def _():
            _fill_band_pads(k_ref, v_ref, kp, vp, s)
            dkp[...] = jnp.zeros_like(dkp)
            dvp[...] = jnp.zeros_like(dvp)

        @pl.when((n0 == 0) & (bb == 0))
        def _():
            dbt_ref[...] = jnp.zeros_like(dbt_ref)
            dbf_ref[...] = jnp.zeros_like(dbf_ref)

        lane = lax.broadcasted_iota(jnp.int32, (2 * CHUNK, 128), 1)
        col = lax.broadcasted_iota(jnp.int32, (4 * CHUNK, WIN), 1)
        bias4 = b_ref[...]

        def two_pairs(i, carry):
            pps = (2 * i, 2 * i + 1)
            two = range(2)
            ns = [n0 + 2 * pp for pp in pps]
            r0s = [pl.multiple_of(pp * 2 * CHUNK, 2 * CHUNK) for pp in pps]
            starts = [pl.multiple_of(n * CHUNK, CHUNK) for n in ns]
            kbs = [kp[pl.ds(st_, WIN), :] for st_ in starts]
            vbs = [vp[pl.ds(st_, WIN), :] for st_ in starts]
            q4bs = [_bf(_stack_heads(q_ref[pl.ds(r0, 2 * CHUNK), :] * (A_HEAD_DIM ** -0.5), lane)) for r0 in r0s]
            do4s = [_stack_heads(do_ref[pl.ds(r0, 2 * CHUNK), :], lane) for r0 in r0s]
            do4bs = [_bf(d) for d in do4s]
            os_ = [o_ref[pl.ds(r0, 2 * CHUNK), :] for r0 in r0s]
            lsevs = [lse_ref[pl.ds(r0, 2 * CHUNK), :] for r0 in r0s]
            lse4s = [jnp.concatenate([v_[:, 0:1], v_[:, 64:65]], axis=0) for v_ in lsevs]
            qks = [_nt(q4bs[j], kbs[j]) for j in two]
            dps = [_nt(do4bs[j], vbs[j]) for j in two]
            deltas = [jnp.sum(do4s[j] * jnp.concatenate([os_[j], os_[j]], axis=0), axis=1, keepdims=True) for j in two]
            ps = [jnp.exp(jnp.where(col >= (A_LOOKBACK - ns[j]) * CHUNK, qks[j] + bias4, -1e30) - lse4s[j])
                  for j in two]
            pbs = [_bf(p) for p in ps]
            dss = [ps[j] * (dps[j] - deltas[j]) for j in two]
            dsbs = [_bf(d) for d in dss]
            dv_ws = [_tn(pbs[j], do4bs[j]) for j in two]
            dq4s = [_nn(dsbs[j], kbs[j]) for j in two]
            dk_ws = [_tn(dsbs[j], q4bs[j]) for j in two]
            for j in two:
                dq_ref[pl.ds(r0s[j], 2 * CHUNK), :] = _bf(
                    jnp.where(lane < 64, dq4s[j][:2 * CHUNK], dq4s[j][2 * CHUNK:]) * (A_HEAD_DIM ** -0.5))
                dkp[pl.ds(starts[j], WIN), :] += dk_ws[j]
                dvp[pl.ds(starts[j], WIN), :] += dv_ws[j]
                dbt_ref[...] += dss[j][:, WIN - 256:]
                dbf_ref[...] += dss[j][:, 0:128] + dss[j][:, 128:256] + dss[j][:, 256:384]
            return carry

        lax.fori_loop(0, ATT_CB // 4, two_pairs, 0)

        @pl.when(n0 == nc - ATT_CB)
        def _():
            dk_ref[...] = _bf(dkp[pl.ds(pad, s), :])
            dv_ref[...] = _bf(dvp[pl.ds(pad, s), :])

    return pl.pallas_call(
        body, name="attn_bwd", grid=(4, b, nstep),
        in_specs=[pl.BlockSpec((rows, 128), lambda m, bb, n: (bb * nstep + n, qb + m)),
                  pl.BlockSpec((s, 128), lambda m, bb, n: (bb, kb_ + m)),
                  pl.BlockSpec((s, 128), lambda m, bb, n: (bb, vb_ + m)),
                  pl.BlockSpec((None, 4 * CHUNK, WIN), lambda m, bb, n: (m, 0, 0)),
                  pl.BlockSpec((rows, 128), lambda m, bb, n: (bb * nstep + n, m)),
                  pl.BlockSpec((rows, 128), lambda m, bb, n: (bb * nstep + n, m)),
                  pl.BlockSpec((rows, 128), lambda m, bb, n: (bb * nstep + n, m))],
        out_specs=[pl.BlockSpec((rows, 128), lambda m, bb, n: (bb * nstep + n, m)),
                   pl.BlockSpec((s, 128), lambda m, bb, n: (bb, m)),
                   pl.BlockSpec((s, 128), lambda m, bb, n: (bb, m)),
                   pl.BlockSpec((None, 4 * CHUNK, 256), lambda m, bb, n: (m, 0, 0)),
                   pl.BlockSpec((None, 4 * CHUNK, 128), lambda m, bb, n: (m, 0, 0))],
        out_shape=[jax.ShapeDtypeStruct((t, A_WIDTH), BF16)] * 3
        + [jax.ShapeDtypeStruct((4, 4 * CHUNK, 256), F32),
           jax.ShapeDtypeStruct((4, 4 * CHUNK, 128), F32)],
        scratch_shapes=[pltpu.VMEM((s + pad, 128), BF16), pltpu.VMEM((s + pad, 128), BF16),
                        pltpu.VMEM((s + pad, 128), F32), pltpu.VMEM((s + pad, 128), F32)],
        compiler_params=_cp(("parallel", "arbitrary", "arbitrary"), VMEM_LIMIT),
    )(proj, proj, proj, bias_band, dy_a, y_a, lse)


def _conv_taps(x, w, s):
    row = lax.broadcasted_iota(jnp.int32, x.shape, 0)
    shifted = [x] + [jnp.where(row >= i, pltpu.roll(x, i, 0), 0.0) for i in range(1, CONV_K)]
    acc = shifted[0] * w[CONV_K - 1:CONV_K, :]
    for i in range(1, CONV_K):
        acc = acc + shifted[i] * w[CONV_K - 1 - i:CONV_K - i, :]
    return acc, shifted


def conv_fwd(proj, conv_w8, b, s):
    cb = 512
    c0 = P_CONV // cb

    def body(x_ref, w_ref, o_ref):
        a, _ = _conv_taps(x_ref[...].astype(F32), w_ref[...], s)
        o_ref[...] = a * _sigmoid(a)

    return pl.pallas_call(
        body, name="conv_fwd", grid=(b, CONV_CH // cb),
        in_specs=[pl.BlockSpec((s, cb), lambda bb, j: (bb, c0 + j)),
                  pl.BlockSpec((8, cb), lambda bb, j: (0, j))],
        out_specs=pl.BlockSpec((s, cb), lambda bb, j: (bb, j)),
        out_shape=jax.ShapeDtypeStruct((b * s, CONV_CH), F32),
        compiler_params=_cp(("parallel", "parallel"), VMEM_LIMIT),
    )(proj, conv_w8)


def conv_bwd(proj, conv_w8, dc3, b, s):
    cb = 512
    c0 = P_CONV // cb

    def body(x_ref, w_ref, dc_ref, dx_ref, dw_ref):
        @pl.when(pl.program_id(1) == 0)
        def _():
            dw_ref[...] = jnp.zeros_like(dw_ref)

        w = w_ref[...]
        a, shifted = _conv_taps(x_ref[...].astype(F32), w, s)
        sg = _sigmoid(a)
        da = dc_ref[...] * (sg * (1.0 + a * (1.0 - sg)))
        row = lax.broadcasted_iota(jnp.int32, da.shape, 0)
        dx = da * w[CONV_K - 1:CONV_K, :]
        for i in range(1, CONV_K):
            dx = dx + jnp.where(row < s - i, pltpu.roll(da, s - i, 0), 0.0) * w[CONV_K - 1 - i:CONV_K - i, :]
        dx_ref[...] = _bf(dx)
        r8 =lax.broadcasted_iota(jnp.int32, (8, cb), 0)
        dw = jnp.zeros((8, cb), F32)
        for i in range(CONV_K):
            dw = dw + jnp.where(r8 == CONV_K - 1 - i, jnp.sum(da * shifted[i], axis=0, keepdims=True), 0.0)
        dw_ref[...] += dw

    return pl.pallas_call(
        body, name="conv_bwd", grid=(CONV_CH // cb, b),
        in_specs=[pl.BlockSpec((s, cb), lambda j, bb: (bb, c0 + j)),
                  pl.BlockSpec((8, cb), lambda j, bb: (0, j)),
                  pl.BlockSpec((None, s, cb), lambda j, bb: (j, bb, 0))],
        out_specs=[pl.BlockSpec((s, cb), lambda j, bb: (bb, j)),
                   pl.BlockSpec((8, cb), lambda j, bb: (0, j))],
        out_shape=[jax.ShapeDtypeStruct((b * s, CONV_CH), BF16), jax.ShapeDtypeStruct((8, CONV_CH), F32)],
        compiler_params=_cp(("parallel", "arbitrary"), VMEM_LIMIT),
    )(proj, conv_w8, dc3)


def _pick_lane(v, k):
    lane = lax.broadcasted_iota(jnp.int32, v.shape, 1)
    return jnp.sum(jnp.where(lane == k, v, 0.0), axis=1, keepdims=True)


def _chunk_masks(ncb):
    i = lax.broadcasted_iota(jnp.int32, (ncb, CHUNK, CHUNK), 1)
    j = lax.broadcasted_iota(jnp.int32, (ncb, CHUNK, CHUNK), 2)
    return i, j


def _col_of_row(rowvec, eye):
    return jnp.sum(jnp.where(eye, rowvec, 0.0), axis=2, keepdims=True)


def _dn_chunk_math(cq, ck, cv, bd, al_row, dtb_row, h, ncb, tm=None):
    r = ncb * CHUNK
    i, j = _chunk_masks(ncb)
    eye = i == j
    low = i >= j
    strict = i > j
    ones = jnp.ones((ncb, CHUNK, CHUNK), F32)

    braw = _pick_lane(bd, h)
    draw = _pick_lane(bd, B_HEADS + h)
    al = _pick_lane(al_row, h)
    dtb = _pick_lane(dtb_row, h)
    ea = jnp.exp(al)
    beta = _sigmoid(braw)
    sp_arg = draw + dtb
    g = -ea * _softplus(sp_arg)

    rq = lax.rsqrt(jnp.sum(cq * cq, axis=1, keepdims=True) + EPS)
    rk = lax.rsqrt(jnp.sum(ck * ck, axis=1, keepdims=True) + EPS)
    nq = cq * rq
    kn = ck * rk
    qn = nq * (B_DIM ** -0.5)

    def c3(a):
        return a.reshape(ncb, CHUNK, a.shape[-1])

    qn3, kn3, v3, beta3 = c3(qn), c3(kn), c3(cv), c3(beta)
    gb = jnp.broadcast_to(c3(g), (ncb, CHUNK, CHUNK))
    gc_b = _bnn_exact(low.astype(BF16), gb)
    gr_b = _bnn_exact(_bf(ones), jnp.where(eye, gc_b, 0.0))
    dm = jnp.where(low, jnp.exp(jnp.where(low, gc_b - gr_b, 0.0)), 0.0)
    gc = gc_b[:, :, 0:1]
    gl = gc_b[:, CHUNK - 1:CHUNK, 0:1]
    gam = jnp.exp(gc)
    egl = jnp.exp(gl)
    edec = jnp.exp(gl - gc)

    knb = _bf(kn3)
    kk = _bnt(knb, knb)
    kd = jnp.where(strict, kk * dm, 0.0)
    a = beta3 * kd
    sz = 1 if tm is None else CHUNK
    if tm is None:
        tm = eye.astype(F32)
    while sz < CHUNK:
        off = jnp.where(((i // (2 * sz)) == (j // (2 * sz))) & ((i // sz) != (j // sz)), a, 0.0)
        tmb = _bf(tm)
        tm = tm - _bnn(_bf(_bnn(tmb, _bf(off))), tmb)
        sz *= 2
    bv = beta3 * v3
    bk = (beta3 * gam) * kn3
    sol = _bnn3(_split(tm), jnp.concatenate([bv, bk], axis=2))
    u, wk = sol[:, :, :B_DIM], sol[:, :, B_DIM:]
    qk = _bnt(_bf(qn3), knb)
    p = jnp.where(low, qk * dm, 0.0)
    kdec = kn3 * edec
    qg = gam * qn3
    return dict(beta=beta3, g=c3(g), ea=ea, sp_arg=c3(sp_arg), rq=c3(rq), rk=c3(rk), nq=c3(nq),
                qn=qn3, kn=kn3, v=v3, gc=gc, gl=gl, gam=gam, egl=egl, edec=edec, dm=dm, kd=kd, a=a,
                tm=tm, u=u, wk=wk, qk=qk, p=p, kdec=kdec, qg=qg, eye=eye, low=low, strict=strict)


def dn_prep(c, proj, al_row, dtb_row, b, s, ncb=32):
    t = b * s
    r = ncb * CHUNK
    nblk = t // r
    bd_blk = 0

    def body(cq_ref, ck_ref, cv_ref, bd_ref, al_ref, dtb_ref, u_ref, wk_ref, qg_ref, kdec_ref, p_ref, egl_ref,
             tm_ref):
        h = pl.program_id(1)
        m = _dn_chunk_math(cq_ref[...], ck_ref[...], cv_ref[...], bd_ref[...].astype(F32), al_ref[...], dtb_ref[...], h, ncb)
        tm_ref[...] = m["tm"].reshape(r, CHUNK)
        u_ref[...] = m["u"].reshape(r, B_DIM)
        wk_ref[...] = _bf(m["wk"].reshape(r, B_DIM))
        qg_ref[...] = _bf(m["qg"].reshape(r, B_DIM))
        kdec_ref[...] = _bf(m["kdec"].reshape(r, B_DIM))
        p_ref[...] = m["p"].reshape(r, CHUNK)
        egl_ref[...] = jnp.broadcast_to(m["egl"], (ncb, 8, 128)).reshape(ncb * 8, 128)

    col = lambda k: pl.BlockSpec((r, 128), lambda i, h: (i, k * B_HEADS + h))
    out_col = pl.BlockSpec((r, 128), lambda i, h: (i, h))
    small = pl.BlockSpec((1, 128), lambda i, h: (0, 0))
    return pl.pallas_call(
        body, name="dn_prep", grid=(nblk, B_HEADS),
        in_specs=[col(0), col(1), col(2), pl.BlockSpec((r, 128), lambda i, h: (i, bd_blk)), small, small],
        out_specs=[out_col, out_col, out_col, out_col,
                   pl.BlockSpec((None, r, CHUNK), lambda i, h: (h, i, 0)),
                   pl.BlockSpec((None, ncb * 8, 128), lambda i, h: (h, i, 0)),
                   pl.BlockSpec((None, r, CHUNK), lambda i, h: (h, i, 0))],
        out_shape=[jax.ShapeDtypeStruct((t, B_WIDTH), F32)] + [jax.ShapeDtypeStruct((t, B_WIDTH), BF16)] * 3
        + [jax.ShapeDtypeStruct((B_HEADS, t, CHUNK), F32),
           jax.ShapeDtypeStruct((B_HEADS, t // 8, 128), F32),
           jax.ShapeDtypeStruct((B_HEADS, t, CHUNK), F32)],
        compiler_params=_cp(("parallel", "parallel"), VMEM_LIMIT),
    )(c, c, c, proj, al_row, dtb_row)


SCAN_CB = 4


def dn_scan_fwd(u, wk, qg, kdec, p, egl, b, s):
    t = b * s
    nc = s // CHUNK

    def body(u_ref, wk_ref, qg_ref, kdec_ref, p_ref, egl_ref, o_ref, ss_ref, st):
        @pl.when(pl.program_id(0) == 0)
        def _():
            st[...] = jnp.zeros_like(st)

        chains = [(bb, h) for bb in range(b) for h in range(B_HEADS)]
        sls = [slice(h * B_DIM, (h + 1) * B_DIM) for _, h in chains]
        states = [st[bb * B_HEADS + h] for bb, h in chains]
        for cc in range(SCAN_CB):
            rs = slice(cc * CHUNK, (cc + 1) * CHUNK)
            sbs = [_bf(sh) for sh in states]
            ws = [u_ref[bb, rs, sl] - _nt(wk_ref[bb, rs, sl], sb) for (bb, _), sl, sb in zip(chains, sls, sbs)]
            qs = [_nt(qg_ref[bb, rs, sl], sb) for (bb, _), sl, sb in zip(chains, sls, sbs)]
            wbs = [_bf(w) for w in ws]
            outs = [q + _nn(_bf(p_ref[h, bb, rs, :]), wb) for (bb, h), q, wb in zip(chains, qs, wbs)]
            new_states = [egl_ref[h, bb, cc * 8:cc * 8 + 1, :] * sh + _tn(wb, kdec_ref[bb, rs, sl])
                          for (bb, h), sl, sh, wb in zip(chains, sls, states, wbs)]
            for (bb, h), sh, o in zip(chains, states, outs):
                ss_ref[bb, cc, h] = sh
                o_ref[bb, rs, h * B_DIM:(h + 1) * B_DIM] = o
            states = new_states
        for (bb, h), sh in zip(chains, states):
            st[bb * B_HEADS + h] = sh

    r3 = lambda a: a.reshape(b, s, B_WIDTH)
    rows = SCAN_CB * CHUNK
    act = pl.BlockSpec((b, rows, B_WIDTH), lambda n: (0, n, 0))
    o, states = pl.pallas_call(
        body, name="dn_scan_fwd", grid=(nc // SCAN_CB,),
        in_specs=[act, act, act, act,
                  pl.BlockSpec((B_HEADS, b, rows, CHUNK), lambda n: (0, 0, n, 0)),
                  pl.BlockSpec((B_HEADS, b, SCAN_CB * 8, 128), lambda n: (0, 0, n, 0))],
        out_specs=[act, pl.BlockSpec((b, SCAN_CB, B_HEADS, B_DIM, B_DIM), lambda n: (0, n, 0, 0, 0))],
        out_shape=[jax.ShapeDtypeStruct((b, s, B_WIDTH), F32),
                   jax.ShapeDtypeStruct((b, nc, B_HEADS, B_DIM, B_DIM), F32)],
        scratch_shapes=[pltpu.VMEM((b * B_HEADS, B_DIM, B_DIM), F32)],
        compiler_params=_cp(("arbitrary",), VMEM_LIMIT),
    )(r3(u), r3(wk), r3(qg), r3(kdec), p.reshape(B_HEADS, b, s, CHUNK), egl.reshape(B_HEADS, b, s // 8, 128))
    return o.reshape(t, B_WIDTH), states


def dn_scan_bwd(u, wk, qg, kdec, p, egl, states, do, b, s):
    t = b * s
    nc = s // CHUNK

    def body(u_ref, wk_ref, qg_ref, kdec_ref, p_ref, egl_ref, ss_ref, do_ref,
             dw_ref, dwk_ref, dqg_ref, dkdec_ref, dp_ref, degl_ref, dst):
        @pl.when(pl.program_id(0) == 0)
        def _():
            dst[...] = jnp.zeros_like(dst)

        chains = [(bb, h) for bb in range(b) for h in range(B_HEADS)]
        dstates = [dst[bb * B_HEADS + h] for bb, h in chains]
        n8 = range(len(chains))
        sls = [slice(h * B_DIM, (h + 1) * B_DIM) for _, h in chains]
        for cc in reversed(range(SCAN_CB)):
            rs = slice(cc * CHUNK, (cc + 1) * CHUNK)
            shs = [ss_ref[bb, cc, h] for bb, h in chains]
            sbs = [_bf(sh) for sh in shs]
            dsbs = [_bf(dsp) for dsp in dstates]
            wkbs = [wk_ref[bb, rs, sl] for (bb, _), sl in zip(chains, sls)]
            dobs = [_bf(do_ref[bb, rs, sl]) for (bb, _), sl in zip(chains, sls)]
            t1 = [_nt(wkbs[i], sbs[i]) for i in n8]
            dwa = [_tn(_bf(p_ref[h, bb, rs, :]), dobs[i]) for i, (bb, h) in enumerate(chains)]
            dwb_ = [_nt(kdec_ref[bb, rs, sls[i]], dsbs[i]) for i, (bb, _) in enumerate(chains)]
            dqgs = [_nn(dobs[i], sbs[i]) for i in n8]
            dsq = [_tn(dobs[i], qg_ref[bb, rs, sls[i]]) for i, (bb, _) in enumerate(chains)]
            wbs = [_bf(u_ref[bb, rs, sls[i]] - t1[i]) for i, (bb, _) in enumerate(chains)]
            dws = [dwa[i] + dwb_[i] for i in n8]
            dwbs = [_bf(dw) for dw in dws]
            dwks = [-_nn(dwbs[i], sbs[i]) for i in n8]
            dkdecs = [_nn(wbs[i], dsbs[i]) for i in n8]
            dpms = [_nt(dobs[i], wbs[i]) for i in n8]
            dsw = [_tn(dwbs[i], wkbs[i]) for i in n8]
            tots = [jnp.sum(jnp.sum(shs[i] * dstates[i], axis=1, keepdims=True), axis=0, keepdims=True) for i in n8]
            new_dss = [egl_ref[h, bb, cc * 8:cc * 8 + 1, :] * dstates[i] + dsq[i] - dsw[i]
                       for i, (bb, h) in enumerate(chains)]
            for i, (bb, h) in enumerate(chains):
                dw_ref[bb, rs, sls[i]] = dws[i]
                dqg_ref[bb, rs, sls[i]] = dqgs[i]
                dwk_ref[bb, rs, sls[i]] = dwks[i]
                dkdec_ref[bb, rs, sls[i]] = dkdecs[i]
                dp_ref[h, bb, rs, :] = dpms[i]
                degl_ref[h, bb, cc * 8:(cc + 1) * 8, :] = jnp.broadcast_to(tots[i], (8, 128))
            dstates = new_dss
        for (bb, h), dsp in zip(chains, dstates):
            dst[bb * B_HEADS + h] = dsp

    r3 = lambda a: a.reshape(b, s, B_WIDTH)
    rows = SCAN_CB * CHUNK
    last = nc // SCAN_CB - 1
    act = pl.BlockSpec((b, rows, B_WIDTH), lambda n: (0, last - n, 0))
    pspec = pl.BlockSpec((B_HEADS, b, rows, CHUNK), lambda n: (0, 0, last - n, 0))
    espec = pl.BlockSpec((B_HEADS, b, SCAN_CB * 8, 128), lambda n: (0, 0, last - n, 0))
    outs = pl.pallas_call(
        body, name="dn_scan_bwd", grid=(nc // SCAN_CB,),
        in_specs=[act, act, act, act, pspec, espec,
                  pl.BlockSpec((b, SCAN_CB, B_HEADS, B_DIM, B_DIM), lambda n: (0, last - n, 0, 0, 0)),
                  act],
        out_specs=[act, act, act, act, pspec, espec],
        out_shape=[jax.ShapeDtypeStruct((b, s, B_WIDTH), F32)] * 4
        + [jax.ShapeDtypeStruct((B_HEADS, b, s, CHUNK), F32),
           jax.ShapeDtypeStruct((B_HEADS, b, s // 8, 128), F32)],
        scratch_shapes=[pltpu.VMEM((b * B_HEADS, B_DIM, B_DIM), F32)],
        compiler_params=_cp(("arbitrary",), VMEM_LIMIT),
    )(r3(u), r3(wk), r3(qg), r3(kdec), p.reshape(B_HEADS, b, s, CHUNK), egl.reshape(B_HEADS, b, s // 8, 128),
      states, r3(do))
    return (*[a.reshape(t, B_WIDTH) for a in outs[:4]], outs[4].reshape(B_HEADS, t, CHUNK),
            outs[5].reshape(B_HEADS, t // 8, 128))


def dn_post_bwd(c, proj, al_row, dtb_row, tmat, dw, dwk, dqg, dkdec, dp, degl, b, s, ncb=16):
    t = b * s
    r = ncb * CHUNK
    nblk = t // r
    bd_blk = 0

    def body(cq_ref, ck_ref, cv_ref, bd_ref, al_ref, dtb_ref, tm_ref, dw_ref, dwk_ref, dqg_ref, dkdec_ref, dp_ref,
             degl_ref, dc_ref, dbd_ref, dal_ref, ddtb_ref):
        h = pl.program_id(1)

        @pl.when((pl.program_id(0) == 0) & (h == 0))
        def _():
            dal_ref[...] = jnp.zeros_like(dal_ref)
            ddtb_ref[...] = jnp.zeros_like(ddtb_ref)

        m = _dn_chunk_math(cq_ref[...], ck_ref[...], cv_ref[...], bd_ref[...].astype(F32), al_ref[...], dtb_ref[...], h, ncb,
                           tm=tm_ref[...].reshape(ncb, CHUNK, CHUNK))
        eye, low, strict = m["eye"], m["low"], m["strict"]
        eyef = eye.astype(F32)

        def c3(a):
            return a.reshape(ncb, CHUNK, a.shape[-1])

        du, dwkv, dqg, dkdec = c3(dw_ref[...]), c3(dwk_ref[...]), c3(dqg_ref[...]), c3(dkdec_ref[...])
        dpm = jnp.where(low, c3(dp_ref[...]), 0.0)
        degl = degl_ref[...].reshape(ncb, 8, 128)[:, 0:1, 0:1]
        beta, gam, kn, qn, v = m["beta"], m["gam"], m["kn"], m["qn"], m["v"]
        dm, kd, a, p = m["dm"], m["kd"], m["a"], m["p"]
        knb, qnb = _bf(kn), _bf(qn)

        eyeb = _bf(eyef)
        th, tl = _split(m["tm"])
        tts = (_bf(_bnt(eyeb, th)), _bf(_bnt(eyeb, tl)))
        xy = _bnn3(tts, jnp.concatenate([du, dwkv], axis=2))
        x, y = xy[:, :, :B_DIM], xy[:, :, B_DIM:]
        da = -jnp.where(strict, _bnt(_bf(x), _bf(m["u"])) + _bnt(_bf(y), _bf(m["wk"])), 0.0)
        dv = beta * x
        sy = jnp.sum(y * kn, axis=2, keepdims=True)
        dbeta = jnp.sum(x * v, axis=2, keepdims=True) + gam * sy + jnp.sum(da * kd, axis=2, keepdims=True)
        dgam = beta * sy + jnp.sum(dqg * qn, axis=2, keepdims=True)
        dkk = da * beta * dm
        dqk = dpm * dm
        dkkb, dqkb = _bf(dkk), _bf(dqk)
        dkn = ((beta * gam) * y + _bnn(dkkb, knb) + _bnn(_bf(_bnt(eyeb, dkkb)), knb)
               + _bnn(_bf(_bnt(eyeb, dqkb)), qnb) + dkdec * m["edec"])
        dqn = gam * dqg + _bnn(dqkb, knb)
        mm = da * a + dpm * p
        ek = jnp.sum(dkdec * m["kdec"], axis=2, keepdims=True)
        dgc = (jnp.sum(mm, axis=2, keepdims=True) - _col_of_row(jnp.sum(mm, axis=1, keepdims=True), eye)
               + dgam * gam - ek)
        dgl = jnp.sum(ek, axis=1, keepdims=True) + degl * m["egl"]
        i, _ = _chunk_masks(ncb)
        dgc = dgc + jnp.where(i[:, :, 0:1] == CHUNK - 1, dgl, 0.0)
        upper = (i <= _chunk_masks(ncb)[1]).astype(BF16)
        dg = _bnn_exact(upper, jnp.broadcast_to(dgc, (ncb, CHUNK, CHUNK)))[:, :, 0:1]

        nq = m["nq"]
        dnq = dqn * (B_DIM ** -0.5)
        dcq = m["rq"] * (dnq - nq * jnp.sum(nq * dnq, axis=2, keepdims=True))
        dck = m["rk"] * (dkn - kn * jnp.sum(kn * dkn, axis=2, keepdims=True))
        dc_ref[0] = dcq.reshape(r, B_DIM)
        dc_ref[1] = dck.reshape(r, B_DIM)
        dc_ref[2] = dv.reshape(r, B_DIM)

        dbraw = (dbeta * beta * (1.0 - beta)).reshape(r, 1)
        sgm = _sigmoid(m["sp_arg"])
        ddraw3 = dg * (-m["ea"]) * sgm
        ddraw = ddraw3.reshape(r, 1)
        lane = lax.broadcasted_iota(jnp.int32, (r, 128), 1)
        contrib = jnp.where(lane == h, dbraw, 0.0) + jnp.where(lane == B_HEADS + h, ddraw, 0.0)

        @pl.when(h == 0)
        def _():
            dbd_ref[...] = contrib

        @pl.when(h != 0)
        def _():
            dbd_ref[...] += contrib

        lane8 = lax.broadcasted_iota(jnp.int32, (8, 128), 1)
        tot_al = jnp.sum(jnp.sum(dg * m["g"], axis=1, keepdims=True), axis=0, keepdims=True).reshape(1, 1)
        tot_dtb = jnp.sum(jnp.sum(ddraw3, axis=1, keepdims=True), axis=0, keepdims=True).reshape(1, 1)
        dal_ref[...] += jnp.where(lane8 == h, tot_al, 0.0)
        ddtb_ref[...] += jnp.where(lane8 == h, tot_dtb, 0.0)

    col = lambda k: pl.BlockSpec((r, 128), lambda i, h: (i, k * B_HEADS + h))
    hcol = pl.BlockSpec((r, 128), lambda i, h: (i, h))
    small = pl.BlockSpec((1, 128), lambda i, h: (0, 0))
    acc = pl.BlockSpec((8, 128), lambda i, h: (0, 0))
    return pl.pallas_call(
        body, name="dn_post_bwd", grid=(nblk, B_HEADS),
        in_specs=[col(0), col(1), col(2), pl.BlockSpec((r, 128), lambda i, h: (i, bd_blk)), small, small,
                  pl.BlockSpec((None, r, CHUNK), lambda i, h: (h, i, 0)),
                  hcol, hcol, hcol, hcol,
                  pl.BlockSpec((None, r, CHUNK), lambda i, h: (h, i, 0)),
                  pl.BlockSpec((None, ncb * 8, 128), lambda i, h: (h, i, 0))],
        out_specs=[pl.BlockSpec((3, r, 128), lambda i, h: (0, i, h)),
                   pl.BlockSpec((r, 128), lambda i, h: (i, 0)), acc, acc],
        out_shape=[jax.ShapeDtypeStruct((3, t, B_WIDTH), F32), jax.ShapeDtypeStruct((t, 128), F32),
                   jax.ShapeDtypeStruct((8, 128), F32), jax.ShapeDtypeStruct((8, 128), F32)],
        compiler_params=_cp(("arbitrary", "arbitrary"), VMEM_LIMIT),
    )(c, c, c, proj, al_row, dtb_row, tmat, dw, dwk, dqg, dkdec, dp, degl)


def make_bias_band(rel_bias):
    tail = bias_tail(jnp.pad(rel_bias, ((0, 0), (0, 384 - N_REL))))
    far = jnp.broadcast_to(rel_bias[:, 2 * REL_CLIP][:, None, None], (A_HEADS, CHUNK, BAND - TAIL))
    band = jnp.concatenate([far, jnp.transpose(tail, (1, 0, 2))], axis=2)
    off = jnp.full((A_HEADS, CHUNK, CHUNK), -1e30, F32)
    both = jnp.stack([jnp.concatenate([band, off], axis=2), jnp.concatenate([off, band], axis=2)], axis=1)
    return both.reshape(4, 4 * CHUNK, WIN)


def bias_band_grad(dbt, dbf):
    t5 = dbt.reshape(A_HEADS, 2, CHUNK, 256)
    tail = t5[:, 0, :, :TAIL] + t5[:, 1, :, CHUNK:]
    far = dbf.reshape(A_HEADS, 2, CHUNK, 128).sum(axis=1) + jnp.pad(t5[:, 1, :, :CHUNK], ((0, 0), (0, 0), (0, CHUNK)))
    return bias_grad(jnp.transpose(tail, (1, 0, 2)), far)[:, :N_REL]


def _rms(x):
    r = lax.rsqrt(jnp.mean(x * x, axis=-1, keepdims=True) + EPS)
    return r, x * r


def _rms_bwd(dh, g, r, n):
    dn = dh * g
    return r * (dn - n * jnp.mean(dn * n, axis=-1, keepdims=True)), dh * n


def _gated_onorm(o, z, w_on):
    parts = []
    for h in range(B_HEADS):
        sl = slice(h * B_DIM, (h + 1) * B_DIM)
        r, n = _rms(o[:, sl])
        parts.append((r, n))
    r4 = [p[0] for p in parts]
    n4 = jnp.concatenate([p[1] for p in parts], axis=1)
    w4 = jnp.concatenate([w_on] * B_HEADS, axis=1)
    sz = _sigmoid(z)
    silu = z * sz
    return n4 * w4 * silu, r4, n4, w4, sz, silu


def mid_fwd(x, y_a, o_b, proj, w_on, wa, wb, w_out, tm=256):
    t = x.shape[0]
    tm = min(tm, t)

    def body(x_ref, ya_ref, ob_ref, z_ref, ga_ref, gb_ref, won_ref, wa_ref, wb_ref, wo_ref, x1_ref, mg_ref):
        yb = _gated_onorm(ob_ref[...], z_ref[...].astype(F32), won_ref[...])[0]
        ua = _nn(_bf(ya_ref[...]), wa_ref[...])
        ub = _nn(_bf(yb), wb_ref[...])
        merged = _sigmoid(ga_ref[...].astype(F32)) * ua + _sigmoid(gb_ref[...].astype(F32)) * ub
        mb = _bf(merged)
        mg_ref[...] = mb
        x1_ref[...] = x_ref[...] + _nn(mb, wo_ref[...])

    rowd = pl.BlockSpec((tm, D_MODEL), lambda i: (i, 0))
    row5 = pl.BlockSpec((tm, 512), lambda i: (i, 0))
    full = lambda a: pl.BlockSpec(a.shape, lambda i: (0,) * a.ndim)
    return pl.pallas_call(
        body, name="mid_fwd", grid=(t // tm,),
        in_specs=[rowd, row5, row5,
                  pl.BlockSpec((tm, 512), lambda i: (i, P_Z // 512)),
                  pl.BlockSpec((tm, D_MODEL), lambda i: (i, 0)),
                  pl.BlockSpec((tm, D_MODEL), lambda i: (i, 1)),
                  full(w_on), full(wa), full(wb), full(w_out)],
        out_specs=[rowd, rowd],
        out_shape=[jax.ShapeDtypeStruct((t, D_MODEL), F32), jax.ShapeDtypeStruct((t, D_MODEL), BF16)],
        compiler_params=_cp(("parallel",), VMEM_LIMIT),
    )(x, y_a, o_b, proj, proj, proj, w_on, wa, wb, w_out)


def mid_bwd(dx1, merged, y_a, o_b, proj, w_on, wa, wb, w_out, tm=256):
    t = dx1.shape[0]
    tm = min(tm, t)

    def body(dx1_ref, mg_ref, ya_ref, ob_ref, z_ref, ga_ref, gb_ref, won_ref, wa_ref, wb_ref, wo_ref,
             dya_ref, dob_ref, dz_ref, dg_ref, dwo_ref, dwa_ref, dwb_ref, dwon_ref):
        @pl.when(pl.program_id(0) == 0)
        def _():
            dwo_ref[...] = jnp.zeros_like(dwo_ref)
            dwa_ref[...] = jnp.zeros_like(dwa_ref)
            dwb_ref[...] = jnp.zeros_like(dwb_ref)
            dwon_ref[...] = jnp.zeros_like(dwon_ref)

        dx1b = _bf(dx1_ref[...])
        dmerged = _nt(dx1b, wo_ref[...])
        dwo_ref[...] += _tn(mg_ref[...], dx1b)
        o = ob_ref[...]
        z = z_ref[...].astype(F32)
        yb, r4, n4, w4, sz, silu = _gated_onorm(o, z, won_ref[...])
        yab, ybb = _bf(ya_ref[...]), _bf(yb)
        ua = _nn(yab, wa_ref[...])
        ub = _nn(ybb, wb_ref[...])
        sa, sb = _sigmoid(ga_ref[...].astype(F32)), _sigmoid(gb_ref[...].astype(F32))
        dua, dub = _bf(dmerged * sa), _bf(dmerged * sb)
        dg_ref[:, 0:D_MODEL] = _bf(dmerged * ua * sa * (1.0 - sa))
        dg_ref[:, D_MODEL:2 * D_MODEL] = _bf(dmerged * ub * sb * (1.0 - sb))
        dwa_ref[...] += _tn(yab, dua)
        dwb_ref[...] += _tn(ybb, dub)
        dya_ref[...] = _nt(dua, wa_ref[...])
        dyb = _nt(dub, wb_ref[...])
        dz_ref[...] = _bf(dyb * (n4 * w4) * (sz * (1.0 + z * (1.0 - sz))))
        dnw = dyb * silu
        dwon = jnp.zeros((1, B_DIM), F32)
        for h in range(B_HEADS):
            sl = slice(h * B_DIM, (h + 1) * B_DIM)
            dxh, dgh = _rms_bwd(dnw[:, sl], won_ref[...], r4[h], n4[:, sl])
            dob_ref[:, sl] = dxh
            dwon = dwon + jnp.sum(dgh, axis=0, keepdims=True)
        dwon_ref[...] += jnp.broadcast_to(dwon, (8, B_DIM))

    rowd = pl.BlockSpec((tm, D_MODEL), lambda i: (i, 0))
    row5 = pl.BlockSpec((tm, 512), lambda i: (i, 0))
    full = lambda a: pl.BlockSpec(a.shape, lambda i: (0,) * a.ndim)
    fixed = lambda shp: pl.BlockSpec(shp, lambda i: (0,) * len(shp))
    return pl.pallas_call(
        body, name="mid_bwd", grid=(t // tm,),
        in_specs=[rowd, rowd, row5, row5,
                  pl.BlockSpec((tm, 512), lambda i: (i, P_Z // 512)),
                  pl.BlockSpec((tm, D_MODEL), lambda i: (i, 0)),
                  pl.BlockSpec((tm, D_MODEL), lambda i: (i, 1)),
                  full(w_on), full(wa), full(wb), full(w_out)],
        out_specs=[row5, row5, row5, pl.BlockSpec((tm, 2 * D_MODEL), lambda i: (i, 0)),
                   fixed((D_MODEL, D_MODEL)), fixed((A_WIDTH, D_MODEL)), fixed((B_WIDTH, D_MODEL)),
                   fixed((8, B_DIM))],
        out_shape=[jax.ShapeDtypeStruct((t, 512), F32), jax.ShapeDtypeStruct((t, 512), F32),
                   jax.ShapeDtypeStruct((t, 512), BF16), jax.ShapeDtypeStruct((t, 2 * D_MODEL), BF16),
           jax.ShapeDtypeStruct((D_MODEL, D_MODEL), F32), jax.ShapeDtypeStruct((A_WIDTH, D_MODEL), F32),
           jax.ShapeDtypeStruct((B_WIDTH, D_MODEL), F32), jax.ShapeDtypeStruct((8, B_DIM), F32)],
        compiler_params=_cp(("arbitrary",), VMEM_LIMIT),
    )(dx1, merged, y_a, o_b, proj, proj, proj, w_on, wa, wb, w_out)


FFN_TF = 1408


def ffn_up(x1, g, w_gu, tm=512, tf=FFN_TF):
    t = x1.shape[0]
    tm = min(tm, t)
    nf = D_FF // tf

    def body(x_ref, g_ref, wg_ref, wu_ref, gate_ref, up_ref, act_ref, h_ref):
        @pl.when(pl.program_id(1) == 0)
        def _():
            r, n = _rms(x_ref[...])
            h_ref[...] = _bf(n * g_ref[...])

        hb = h_ref[...]
        gate = _nn(hb, wg_ref[...])
        up = _nn(hb, wu_ref[...])
        gate_ref[...] = _bf(gate)
        up_ref[...] = _bf(up)
        act_ref[...] = _bf(gate * _sigmoid(gate) * up)

    ff = pl.BlockSpec((tm, tf), lambda i, j: (i, j))
    return pl.pallas_call(
        body, name="ffn_up", grid=(t // tm, nf),
        in_specs=[pl.BlockSpec((tm, D_MODEL), lambda i, j: (i, 0)),
                  pl.BlockSpec((1, D_MODEL), lambda i, j: (0, 0)),
                  pl.BlockSpec((D_MODEL, tf), lambda i, j: (0, j)),
                  pl.BlockSpec((D_MODEL, tf), lambda i, j: (0, nf + j))],
        out_specs=[ff, ff, ff, pl.BlockSpec((tm, D_MODEL), lambda i, j: (i, 0))],
        out_shape=[jax.ShapeDtypeStruct((t, D_FF), BF16)] * 3 + [jax.ShapeDtypeStruct((t, D_MODEL), BF16)],
        compiler_params=_cp(("parallel", "arbitrary"), VMEM_LIMIT),
    )(x1, g, w_gu, w_gu)


def matmul_residual(a, w, res, name, tm=512, tk=FFN_TF):
    t, k = a.shape
    n = w.shape[1]
    tm = min(tm, t)

    def body(a_ref, w_ref, r_ref, o_ref):
        @pl.when(pl.program_id(1) == 0)
        def _():
            o_ref[...] = r_ref[...]

        o_ref[...] += _nn(a_ref[...], w_ref[...])

    return pl.pallas_call(
        body, name=name, grid=(t // tm, k // tk),
        in_specs=[pl.BlockSpec((tm, tk), lambda i, j: (i, j)),
                  pl.BlockSpec((tk, n), lambda i, j: (j, 0)),
                  pl.BlockSpec((tm, n), lambda i, j: (i, 0))],
        out_specs=pl.BlockSpec((tm, n), lambda i, j: (i, 0)),
        out_shape=jax.ShapeDtypeStruct((t, n), F32),
        compiler_params=_cp(("parallel", "arbitrary"), VMEM_LIMIT),
    )(a, w, res)


def ffn_act_bwd(dx2, gate, up, w_down, tm=512, tf=FFN_TF):
    t = dx2.shape[0]
    tm = min(tm, t)

    def body(dx2_ref, gate_ref, up_ref, wd_ref, dgate_ref, dup_ref, dx2b_ref):
        @pl.when(pl.program_id(1) == 0)
        def _():
            dx2b_ref[...] = _bf(dx2_ref[...])

        dact = _nt(dx2b_ref[...], wd_ref[...])
        gt, upv = gate_ref[...].astype(F32), up_ref[...].astype(F32)
        sg = _sigmoid(gt)
        t = dact * sg
        dgate_ref[...] = _bf(t * upv * (1.0 + gt * (1.0 - sg)))
        dup_ref[...] = _bf(t * gt)

    ff = pl.BlockSpec((tm, tf), lambda i, j: (i, j))
    return pl.pallas_call(
        body, name="ffn_act_bwd", grid=(t // tm, D_FF // tf),
        in_specs=[pl.BlockSpec((tm, D_MODEL), lambda i, j: (i, 0)), ff, ff,
                  pl.BlockSpec((tf, D_MODEL), lambda i, j: (j, 0))],
        out_specs=[ff, ff],
        out_shape=[jax.ShapeDtypeStruct((t, D_FF), BF16)] * 2,
        scratch_shapes=[pltpu.VMEM((tm, D_MODEL), BF16)],
        compiler_params=_cp(("parallel", "arbitrary"), VMEM_LIMIT),
    )(dx2, gate, up, w_down)


def tail_fwd_bwd(x2, p, target, g_ple, g_final, w_pg, w_pp, tm=512):
    t = x2.shape[0]
    tm = min(tm, t)

    def body(x_ref, p_ref, t_ref, gp_ref, gf_ref, wpg_ref, wpp_ref,
             dx_ref, dwpg_ref, dwpp_ref, dgp_ref, dgf_ref, loss_ref):
        @pl.when(pl.program_id(0) == 0)
        def _():
            dwpg_ref[...] = jnp.zeros_like(dwpg_ref)
            dwpp_ref[...] = jnp.zeros_like(dwpp_ref)
            dgp_ref[...] = jnp.zeros_like(dgp_ref)
            dgf_ref[...] = jnp.zeros_like(dgf_ref)
            loss_ref[...] = jnp.zeros_like(loss_ref)

        x2v = x_ref[...]
        gp, gf = gp_ref[...], gf_ref[...]
        r3, n3 = _rms(x2v)
        h3b = _bf(n3 * gp)
        pb = _bf(p_ref[...])
        pg = _sigmoid(_nn(h3b, wpg_ref[...]))
        pp = _nn(pb, wpp_ref[...])
        x3 = x2v + pg * pp
        r4, n4 = _rms(x3)
        err = n4 * gf - t_ref[...]
        part = 0.5 * jnp.sum(jnp.sum(err * err, axis=1, keepdims=True), axis=0, keepdims=True) / D_MODEL
        loss_ref[...] += jnp.broadcast_to(part, (8, 128))
        dy = err * (1.0 / D_MODEL)
        dx3, dgf = _rms_bwd(dy, gf, r4, n4)
        dgf_ref[...] += jnp.broadcast_to(jnp.sum(dgf, axis=0, keepdims=True), (8, D_MODEL))
        dzp = _bf(dx3 * pp * pg * (1.0 - pg))
        dpp = _bf(dx3 * pg)
        dwpg_ref[...] += _tn(h3b, dzp)
        dwpp_ref[...] += _tn(pb, dpp)
        dh3 = _nt(dzp, wpg_ref[...])
        dx, dgp = _rms_bwd(dh3, gp, r3, n3)
        dgp_ref[...] += jnp.broadcast_to(jnp.sum(dgp, axis=0, keepdims=True), (8, D_MODEL))
        dx_ref[...] = dx3 + dx

    rowd = pl.BlockSpec((tm, D_MODEL), lambda i: (i, 0))
    fixed = lambda shp: pl.BlockSpec(shp, lambda i: (0,) * len(shp))
    return pl.pallas_call(
        body, name="tail_fwd_bwd", grid=(t // tm,),
        in_specs=[rowd, pl.BlockSpec((tm, PLE_DIM), lambda i: (i, 0)), rowd,
                  fixed((1, D_MODEL)), fixed((1, D_MODEL)), fixed((D_MODEL, D_MODEL)), fixed((PLE_DIM, D_MODEL))],
        out_specs=[rowd, fixed((D_MODEL, D_MODEL)), fixed((PLE_DIM, D_MODEL)),
                   fixed((8, D_MODEL)), fixed((8, D_MODEL)), fixed((8, 128))],
        out_shape=[jax.ShapeDtypeStruct((t, D_MODEL), F32), jax.ShapeDtypeStruct((D_MODEL, D_MODEL), F32),
                   jax.ShapeDtypeStruct((PLE_DIM, D_MODEL), F32), jax.ShapeDtypeStruct((8, D_MODEL), F32),
                   jax.ShapeDtypeStruct((8, D_MODEL), F32), jax.ShapeDtypeStruct((8, 128), F32)],
        compiler_params=_cp(("arbitrary",), VMEM_LIMIT),
    )(x2, p, target, g_ple, g_final, w_pg, w_pp)


def in_proj_bwd(pieces, weights, x, dx1, g, name="in_proj_bwd", tm=256):
    t = x.shape[0]
    tm = min(tm, t)
    k = len(pieces)
    assert all(c0 % wd == 0 and w0 % wd == 0 for (_, c0, wd), (_, w0) in zip(pieces, weights))

    def body(*refs):
        p_refs, w_refs = refs[:k], refs[k:2 * k]
        x_ref, dx1_ref, g_ref, dx_ref, dg_ref = refs[2 * k:]

        @pl.when(pl.program_id(0) == 0)
        def _():
            dg_ref[...] = jnp.zeros_like(dg_ref)

        dh = _nt(_bf(p_refs[0][...]), w_refs[0][...])
        for pr, wr in zip(p_refs[1:], w_refs[1:]):
            dh = dh + _nt(_bf(pr[...]), wr[...])
        r, n = _rms(x_ref[...])
        dx, dgc = _rms_bwd(dh, g_ref[...], r, n)
        dx_ref[...] = dx1_ref[...] + dx
        dg_ref[...] += jnp.broadcast_to(jnp.sum(dgc, axis=0, keepdims=True), (8, D_MODEL))

    rowd = pl.BlockSpec((tm, D_MODEL), lambda i: (i, 0))
    return pl.pallas_call(
        body, name=name, grid=(t // tm,),
        in_specs=[pl.BlockSpec((tm, wd), functools.partial(lambda i, cb: (i, cb), cb=c0 // wd))
                  for _, c0, wd in pieces]
        + [pl.BlockSpec((w.shape[0], wd), functools.partial(lambda i, cb: (0, cb), cb=w0 // wd))
           for (w, w0), (_, _, wd) in zip(weights, pieces)]
        + [rowd, rowd, pl.BlockSpec((1, D_MODEL), lambda i: (0, 0))],
        out_specs=[rowd, pl.BlockSpec((8, D_MODEL), lambda i: (0, 0))],
        out_shape=[jax.ShapeDtypeStruct((t, D_MODEL), F32), jax.ShapeDtypeStruct((8, D_MODEL), F32)],
        compiler_params=_cp(("arbitrary",), VMEM_LIMIT),
    )(*[a for a, _, _ in pieces], *[w for w, _ in weights], x, dx1, g)


def adamw(w, g, m, v, name, rows_cap=256, dep=None):
    lead = w.shape[:-2]
    r, c = w.shape[-2:]
    tr = r
    for cand in range(8, min(r, rows_cap) + 1, 8):
        if r % cand == 0:
            tr = cand

    def body(w_ref, g_ref, m_ref, v_ref, *rest):
        d_ref, mo_ref, vo_ref = rest[-3:]
        gv = g_ref[...]
        mn = ADAM_B1 * m_ref[...] + (1.0 - ADAM_B1) * gv
        vn = ADAM_B2 * v_ref[...] + (1.0 - ADAM_B2) * (gv * gv)
        m_hat = mn / (1.0 - ADAM_B1 ** ADAM_STEP)
        v_hat = vn / (1.0 - ADAM_B2 ** ADAM_STEP)
        d_ref[...] = -ADAM_LR * (m_hat / (jnp.sqrt(v_hat) + ADAM_EPS) + ADAM_WD * w_ref[...])
        mo_ref[...] = mn
        vo_ref[...] = vn

    spec = pl.BlockSpec((None,) * len(lead) + (tr, c), lambda i: (0,) * len(lead) + (i, 0))
    extra = [] if dep is None else [dep]
    return pl.pallas_call(
        body, name=name, grid=(r // tr,),
        in_specs=[spec] * 4 + [pl.BlockSpec((8, 128), lambda i: (0, 0))] * len(extra), out_specs=[spec] * 3,
        out_shape=[jax.ShapeDtypeStruct(w.shape, F32)] * 3,
        compiler_params=_cp(("parallel",), VMEM_LIMIT),
    )(w, g.reshape(w.shape), m, v, *extra)


def _w_in_shards(dwp):
    cs = D_IN // N_CHIPS
    regions = ((0, SPLIT_Z, P_QA), (SPLIT_Z, SPLIT_Z + 8, P_BD - SPLIT_Z), (SPLIT_Z + 8, D_IN, -(SPLIT_Z + 8)))

    def original(lo, hi):
        parts = [dwp[:, max(lo, a) + off:min(hi, e) + off] for a, e, off in regions if max(lo, a) < min(hi, e)]
        return parts[0] if len(parts) == 1 else jnp.concatenate(parts, axis=1)

    return jnp.stack([original(s * cs, (s + 1) * cs) for s in range(N_CHIPS)])


class Standalone:
    def __init__(self, later_weights):
        self.later_weights = later_weights

    def begin(self, *a):
        return 0.0

    forward = exchange = join = begin

    def finish(self, after):
        return self.later_weights


def local_step(x3d, p3d, target3d, g4, small, later, early):
    b, s, _ = x3d.shape
    t = b * s
    x = x3d.reshape(t, D_MODEL)
    p = p3d.reshape(t, PLE_DIM)
    target = target3d.reshape(t, D_MODEL)
    cut = SPLIT_Z - 2 * (D_IN // N_CHIPS)
    w_inp = jnp.concatenate([g4[2][:, cut + 8:], g4[3], g4[0], g4[1], g4[2][:, :cut], g4[2][:, cut:cut + 8],
                             jnp.zeros((D_MODEL, 120), BF16)], axis=1)
    al_row = jnp.pad(small["a_log"].reshape(1, B_HEADS), ((0, 0), (0, 128 - B_HEADS)))
    dtb_row = jnp.pad(small["dt_bias"].reshape(1, B_HEADS), ((0, 0), (0, 128 - B_HEADS)))
    conv_w8 = jnp.pad(small["conv_w"].reshape(CONV_K, CONV_CH), ((0, 8 - CONV_K), (0, 0)))
    w_on = small["w_onorm"].reshape(1, B_DIM)
    g_mix, g_ffn = small["g_mix"].reshape(1, D_MODEL), small["g_ffn"].reshape(1, D_MODEL)
    g_ple, g_final = small["g_ple"].reshape(1, D_MODEL), small["g_final"].reshape(1, D_MODEL)
    bias_band = make_bias_band(small["rel_bias"].reshape(A_HEADS, N_REL))

    tok = later.begin()
    proj, h1, bd32 = rms_matmul(x, g_mix + tok, w_inp, "in_proj", tm=1024, tn_cap=1152)
    y_a, lse = attn_fwd(proj, bias_band, b, s)
    tok = later.forward(lse)
    c = conv_fwd(proj, conv_w8 + tok, b, s)
    u, wk, qg, kdec, pm, egl, tmat = dn_prep(c, bd32, al_row, dtb_row, b, s)
    o_b, states = dn_scan_fwd(u, wk, qg, kdec, pm, egl, b, s)
    wts = later.finish(o_b)
    x1, merged = mid_fwd(x, y_a, o_b, proj, w_on, wts["w_branch_a"], wts["w_branch_b"], wts["w_out"])
    gate, up, act, h2 = ffn_up(x1, g_ffn, wts["w_gate_up"])
    x2 = matmul_residual(act, wts["w_down"], x1, "ffn_down")

    dx2, dw_pg, dw_pp, dg_ple, dg_final, loss = tail_fwd_bwd(
        x2, p, target, g_ple, g_final, wts["w_ple_gate"], wts["w_ple_proj"])
    dgate, dup = ffn_act_bwd(dx2, gate, up, wts["w_down"])
    w_gu = wts["w_gate_up"]
    dx1, dg_ffn = in_proj_bwd([(dgate, 0, D_FF), (dup, 0, D_FF)], [(w_gu, 0), (w_gu, D_FF)], x1, dx2, g_ffn,
                              name="ffn_in_bwd")
    dw_down = matmul_tn(act, dx2, "dw_down")
    dw_gu = matmul_tn(h2, dgate, "dw_gate", width=2 * D_FF, tiles_major=True)
    dw_gu = matmul_tn(h2, dup, "dw_up", into=dw_gu, col0=D_FF, width=2 * D_FF, tiles_major=True)
    dy_a, do_b, dz, dgates, dw_out, dwa, dwb, dw_on = mid_bwd(
        dx1, merged, y_a, o_b, proj, w_on, wts["w_branch_a"], wts["w_branch_b"], wts["w_out"])
    tok = early.begin(dict(w_branch_a=dwa, w_branch_b=dwb, w_out=dw_out, w_gate_up=dw_gu, w_down=dw_down,
                           w_ple_gate=dw_pg, w_ple_proj=dw_pp))
    ddw, ddwk, ddqg, ddkdec, ddp, ddegl = dn_scan_bwd(u, wk, qg, kdec, pm, egl + tok, states, do_b, b, s)
    tok = early.exchange(ddegl)
    dc3, dbd, dal, ddtb = dn_post_bwd(c, bd32, al_row + tok, dtb_row, tmat, ddw, ddwk, ddqg, ddkdec, ddp, ddegl, b, s)
    dconv, dconv_w = conv_bwd(proj, conv_w8, dc3, b, s)
    dqa, dka, dva, dbt, dbf = attn_bwd(proj, bias_band, y_a, lse, dy_a, b, s)
    tok = early.join(dqa)

    pieces = [dgates, dqa, dka, dva, dconv, dz, dbd]
    bounds = [0, 2048, 2560, 3072, 3584, 5120, 5632, 5760]
    windows = [(dgates, 0, 2048), (dqa, 0, 512), (dka, 0, 512), (dva, 0, 512), (dconv, 0, 512), (dconv, 512, 512),
               (dconv, 1024, 512), (dz, 0, 512), (dbd, 0, 128)]
    w_cols = [0, P_QA, P_KA, P_VA, P_CONV, P_CONV + 512, P_CONV + 1024, P_Z, P_BD]
    dx, dg_mix = in_proj_bwd(windows, [(w_inp, c0) for c0 in w_cols], x, dx1, g_mix + tok)
    dwp = None
    for k, pc in enumerate(pieces):
        dwp = matmul_tn(h1, pc, "dw_in_%d" % k, into=dwp, col0=bounds[k], width=P_WIDTH)
    reduced_early = early.finish(dwp)
    dw_in = _w_in_shards(dwp)

    grads = dict(w_in=dw_in, w_branch_a=dwa, w_branch_b=dwb, w_out=dw_out, w_gate_up=dw_gu, w_down=dw_down,
                 w_ple_gate=dw_pg, w_ple_proj=dw_pp)
    small_grads = dict(g_mix=dg_mix[0], g_ffn=dg_ffn[0], g_ple=dg_ple[0], g_final=dg_final[0],
                       conv_w=dconv_w[:CONV_K].reshape(-1), rel_bias_parts=(dbt, dbf), w_onorm=dw_on[0],
                       a_log=dal[0, :B_HEADS], dt_bias=ddtb[0, :B_HEADS], loss=loss[0, :1])
    return dx.reshape(b, s, D_MODEL), grads, small_grads, reduced_early


BIG = (("w_in", (D_MODEL, D_IN), 1), ("w_branch_a", (A_WIDTH, D_MODEL), 1), ("w_branch_b", (B_WIDTH, D_MODEL), 1),
       ("w_out", (D_MODEL, D_MODEL), 0), ("w_gate_up", (D_MODEL, 2 * D_FF), 1), ("w_down", (D_FF, D_MODEL), 0),
       ("w_ple_gate", (D_MODEL, D_MODEL), 0), ("w_ple_proj", (PLE_DIM, D_MODEL), 1))
N_CHIPS = 4
FIRST_WEIGHTS = ("w_in",)
LATER_WEIGHTS = ("w_branch_a", "w_branch_b", "w_out", "w_gate_up", "w_down", "w_ple_gate", "w_ple_proj")
LATE_GRADS = ("w_in",)
EARLY_GRADS = ("w_branch_a", "w_branch_b", "w_out", "w_gate_up", "w_down", "w_ple_gate", "w_ple_proj")


def _items(names):
    return [it for it in BIG if it[0] in names]


def _shard_shape(shape, axis):
    return (shape[0] // N_CHIPS, shape[1]) if axis == 0 else (shape[0], shape[1] // N_CHIPS)


def _width_groups(names):
    groups = {}
    for n, shape, axis in _items(names):
        rs, cs = _shard_shape(shape, axis)
        groups.setdefault(cs, []).append((n, rs))
    return sorted(groups.items())


def grad_buffers(grads, names):
    info = {n: (shape, axis) for n, shape, axis in _items(names)}
    bufs = []
    for cs, members in _width_groups(names):
        segs = []
        for n, rs in members:
            g = grads[n].astype(BF16)
            if g.ndim == 2:
                g = (g.reshape(N_CHIPS, rs, cs) if info[n][1] == 0
                     else jnp.transpose(g.reshape(rs, N_CHIPS, cs), (1, 0, 2)))
            segs.append(g)
        bufs.append(segs[0] if len(segs) == 1 else jnp.concatenate(segs, axis=1))
    return bufs


def split_buffers(reduced, names):
    out = {}
    for (cs, members), buf in zip(_width_groups(names), reduced):
        r0 = 0
        for n, rs in members:
            out[n] = buf[r0:r0 + rs]
            r0 += rs
    return out


def _place():
    return lax.axis_index("x"), lax.axis_index("y"), lax.axis_index("c")


ANY = pl.BlockSpec(memory_space=pl.ANY)


def _gathered_shape(item):
    n, shape, _ = item
    return (N_CHIPS,) + _shard_shape(shape, 1) if n == "w_in" else shape


def _gather_block(o_ref, item, cx, cy, hf):
    n, shape, axis = item
    rs, cs = _shard_shape(shape, axis)
    hr = rs // 2
    ci = 2 * cx + cy
    if n == "w_in":
        return o_ref.at[ci, pl.ds(pl.multiple_of(hf * hr, 16), hr), :]
    if axis == 0:
        return o_ref.at[pl.ds(pl.multiple_of(ci * rs + hf * hr, 16), hr), :]
    return o_ref.at[pl.ds(pl.multiple_of(hf * hr, 16), hr), pl.ds(pl.multiple_of(ci * cs, 128), cs)]


def _own_half(w_ref, item, c):
    hr = _shard_shape(item[1], item[2])[0] // 2
    return w_ref.at[pl.ds(pl.multiple_of(c * hr, 16), hr), :]


def _gather_slot(o_ref, item, cx, cy):
    n, shape, axis = item
    rs, cs = _shard_shape(shape, axis)
    ci = 2 * cx + cy
    if n == "w_in":
        return o_ref.at[ci]
    if axis == 0:
        return o_ref.at[pl.ds(pl.multiple_of(ci * rs, 16), rs), :]
    return o_ref.at[:, pl.ds(pl.multiple_of(ci * cs, 128), cs)]


def _other_chips(x, y):
    return [(1 - x, y), (x, 1 - y), (1 - x, 1 - y)]


def allgather_weights(shards, names):
    items = _items(names)
    nw = len(items)

    def body(*refs):
        w_refs, o_refs = refs[:nw], refs[nw:2 * nw]
        send_sems, recv_sems = refs[2 * nw:]
        x, y, c = _place()
        sibling = (x, y, 1 - c)
        chips = _other_chips(x, y)

        def copy(k, src, dst, to):
            return pltpu.make_async_remote_copy(src_ref=src, dst_ref=dst, send_sem=send_sems.at[k],
                                                recv_sem=recv_sems.at[k], device_id=to, device_id_type=MESH)

        def blk(i, cx, cy, hf):
            return _gather_block(o_refs[i], items[i], cx, cy, hf)

        def my_half(i):
            return _own_half(w_refs[i], items[i], c)

        def own(i):
            return _gather_slot(o_refs[i], items[i], x, y)

        first = [copy(7 * i + j, my_half(i), blk(i, x, y, c), (*chip_, c))
                 for i in range(nw) for j, chip_ in enumerate(chips)]
        first += [copy(7 * i + 6, w_refs[i], own(i), sibling) for i in range(nw)]
        for cp in first:
            cp.start()
        passed = []
        for i in range(nw):
            for j, chip_ in enumerate(chips):
                copy(7 * i + j, my_half(i), blk(i, *chip_, c), (*chip_, c)).wait_recv()
                fwd = copy(7 * i + 3 + j, blk(i, *chip_, c), blk(i, *chip_, c), sibling)
                fwd.start()
                passed.append(fwd)
        for i in range(nw):
            for j, chip_ in enumerate(chips):
                copy(7 * i + 3 + j, my_half(i), blk(i, *chip_, 1 - c), sibling).wait_recv()
            copy(7 * i + 6, w_refs[i], own(i), sibling).wait_recv()
        for cp in first + passed:
            cp.wait_send()

    outs = pl.pallas_call(
        body, name="allgather_weights",
        in_specs=[ANY] * nw, out_specs=[ANY] * nw,
        out_shape=[jax.ShapeDtypeStruct(_gathered_shape(it), BF16) for it in items],
        scratch_shapes=[pltpu.SemaphoreType.DMA((7 * nw,)), pltpu.SemaphoreType.DMA((7 * nw,))],
    )(*[shards[it[0]] for it in items])
    return {it[0]: o for it, o in zip(items, outs)}


HBM_SPEC = pl.BlockSpec(memory_space=pltpu.HBM)
SEM_SPEC = pl.BlockSpec(memory_space=pltpu.SEMAPHORE)
EFFECT = pltpu.SideEffectType.DATAFLOW_SIDE_EFFECTING


def _in_hbm(a):
    return pltpu.with_memory_space_constraint(a, pltpu.HBM)


def copies_start(name, bufs, ncopies, plan):
    nb = len(bufs)

    def body(*refs):
        in_refs, send_sems, recv_sems, token = refs[:nb], refs[nb], refs[nb + 1], refs[-1]
        for k, (src, dst, to) in enumerate(plan(in_refs)):
            pltpu.make_async_remote_copy(src_ref=src, dst_ref=dst, send_sem=send_sems.at[k],
                                         recv_sem=recv_sems.at[k], device_id=to, device_id_type=MESH).start()
        token[...] = jnp.zeros_like(token)

    outs = pl.pallas_call(
        body, name=name,
        in_specs=[HBM_SPEC] * nb,
        out_specs=(SEM_SPEC, SEM_SPEC, *[HBM_SPEC] * nb, pl.BlockSpec(memory_space=pltpu.VMEM)),
        out_shape=(pltpu.SemaphoreType.DMA((ncopies,)), pltpu.SemaphoreType.DMA((ncopies,)),
                   *[pltpu.HBM(b.shape, b.dtype) for b in bufs], jax.ShapeDtypeStruct((8, 128), F32)),
        input_output_aliases={i: 2 + i for i in range(nb)},
        compiler_params=pltpu.CompilerParams(has_side_effects=EFFECT),
    )(*[_in_hbm(b) for b in bufs])
    return outs[0], outs[1], list(outs[2:2 + nb]), outs[-1][0, 0]


def copies_wait(name, send_sems, recv_sems, bufs, after, plan):
    nb = len(bufs)

    def body(*refs):
        in_refs, s_sems, r_sems = refs[:nb], refs[nb], refs[nb + 1]
        for k, (src, dst, to) in enumerate(plan(in_refs)):
            cp = pltpu.make_async_remote_copy(src_ref=src, dst_ref=dst, send_sem=s_sems.at[k],
                                              recv_sem=r_sems.at[k], device_id=to, device_id_type=MESH)
            cp.wait_send()
            cp.wait_recv()

    return list(pl.pallas_call(
        body, name=name,
        in_specs=[HBM_SPEC] * nb + [SEM_SPEC, SEM_SPEC, ANY],
        out_specs=tuple([HBM_SPEC] * nb),
        out_shape=tuple(pltpu.HBM(b.shape, b.dtype) for b in bufs),
        input_output_aliases={i: i for i in range(nb)},
        compiler_params=pltpu.CompilerParams(has_side_effects=EFFECT),
    )(*bufs, send_sems, recv_sems, after))


def _landing(shape, dtype):
    return _in_hbm(lax.empty(shape, dtype))


class LaterWeights:
    def __init__(self, shards):
        self.items = _items(LATER_WEIGHTS)
        self.shards = shards
        self.nw = len(self.items)

    def _ici_plan(self, refs):
        x, y, c = _place()
        w_refs, o_refs = refs[:self.nw], refs[self.nw:]
        plan = [(_own_half(w_refs[i], it, c), _gather_block(o_refs[i], it, x, y, c), (*chip_, c))
                for i, it in enumerate(self.items) for chip_ in _other_chips(x, y)]
        return plan + [(w_refs[i], _gather_slot(o_refs[i], it, x, y), (x, y, 1 - c))
                       for i, it in enumerate(self.items)]

    def _d2d_plan(self, refs):
        x, y, c = _place()
        return [(_gather_block(refs[i], it, *chip_, c), _gather_block(refs[i], it, *chip_, c), (x, y, 1 - c))
                for i, it in enumerate(self.items) for chip_ in _other_chips(x, y)]

    def _d2d_wait_plan(self, refs):
        x, y, c = _place()
        return [(_gather_block(refs[i], it, *chip_, c), _gather_block(refs[i], it, *chip_, 1 - c), (x, y, 1 - c))
                for i, it in enumerate(self.items) for chip_ in _other_chips(x, y)]

    def _ici_wait_plan(self, refs):
        x, y, c = _place()
        w_refs, o_refs = refs[:self.nw], refs[self.nw:]
        plan = [(_own_half(w_refs[i], it, c), _gather_block(o_refs[i], it, *chip_, c), (*chip_, c))
                for i, it in enumerate(self.items) for chip_ in _other_chips(x, y)]
        return plan + [(w_refs[i], _gather_slot(o_refs[i], it, x, y), (x, y, 1 - c))
                       for i, it in enumerate(self.items)]

    def begin(self):
        srcs = [self.shards[it[0]] for it in self.items]
        lands = [_landing(_gathered_shape(it), BF16) for it in self.items]
        self.s1, self.r1, self.b1, tok = copies_start("gather_ici_start", srcs + lands, 4 * self.nw, self._ici_plan)
        return tok

    def forward(self, after):
        b1 = copies_wait("gather_ici_wait", self.s1, self.r1, self.b1, after, self._ici_wait_plan)
        self.s2, self.r2, self.b2, tok = copies_start("gather_d2d_start", b1[self.nw:], 3 * self.nw, self._d2d_plan)
        return tok

    def finish(self, after):
        outs = copies_wait("gather_d2d_wait", self.s2, self.r2, self.b2, after, self._d2d_wait_plan)
        return {it[0]: o for it, o in zip(self.items, outs)}


def small_allreduce(v, name):
    r = v.shape[0]

    def body(v_ref, o_ref, buf, send_sems, recv_sems):
        x, y, c = _place()
        me = 4 * x + 2 * y + c
        buf[me] = v_ref[...]
        flips = [(fx, fy, fc) for fx in (0, 1) for fy in (0, 1) for fc in (0, 1)][1:]
        peers = [((1 - x) if fx else x, (1 - y) if fy else y, (1 - c) if fc else c) for fx, fy, fc in flips]

        def copy(k, slot, to):
            return pltpu.make_async_remote_copy(src_ref=v_ref, dst_ref=buf.at[slot], send_sem=send_sems.at[k],
                                                recv_sem=recv_sems.at[k], device_id=to, device_id_type=MESH)

        sends = [copy(k, me, peer) for k, peer in enumerate(peers)]
        for cp in sends:
            cp.start()
        for k, (px, py, pc) in enumerate(peers):
            copy(k, 4 * px + 2 * py + pc, (px, py, pc)).wait_recv()
        for cp in sends:
            cp.wait_send()
        acc = buf[0]
        for d in range(1, 8):
            acc = acc + buf[d]
        o_ref[...] = acc

    return pl.pallas_call(
        body, name=name,
        in_specs=[pl.BlockSpec(memory_space=pltpu.VMEM)], out_specs=pl.BlockSpec(memory_space=pltpu.VMEM),
        out_shape=jax.ShapeDtypeStruct((r, 128), F32),
        scratch_shapes=[pltpu.VMEM((8, r, 128), F32), pltpu.SemaphoreType.DMA((7,)), pltpu.SemaphoreType.DMA((7,))],
    )(v)


def add_halves(g, other, place):
    half, wd = other.shape[1:]
    tr = _tile_rows(half, wd)
    nblk = half // tr

    def body(pref, g0, g1, g2, g3, o0, o1, o2, o3, pf_ref, pb_ref):
        f = lambda r: r[...].astype(F32)
        pf_ref[...] = f(g0) + f(o0)
        pb_ref[0] = _bf(f(g1) + f(o1))
        pb_ref[1] = _bf(f(g2) + f(o2))
        pb_ref[2] = _bf(f(g3) + f(o3))

    gspec = lambda k: pl.BlockSpec((None, tr, wd), lambda i, pr: ((pr[0] + k) % N_CHIPS, pr[1] * nblk + i, 0))
    ospec = lambda k: pl.BlockSpec((None, tr, wd), lambda i, pr: ((pr[0] + k) % N_CHIPS, i, 0))
    return pl.pallas_call(
        body, name="add_halves",
        grid_spec=pltpu.PrefetchScalarGridSpec(
            num_scalar_prefetch=1, grid=(nblk,),
            in_specs=[gspec(0), gspec(1), gspec(2), gspec(3), ospec(0), ospec(1), ospec(2), ospec(3)],
            out_specs=[pl.BlockSpec((tr, wd), lambda i, pr: (i, 0)),
                       pl.BlockSpec((3, tr, wd), lambda i, pr: (0, i, 0))]),
        out_shape=[jax.ShapeDtypeStruct((half, wd), F32), jax.ShapeDtypeStruct((3, half, wd), BF16)],
        compiler_params=_cp(("parallel",), VMEM_LIMIT),
    )(place, g, g, g, g, other, other, other, other)


def _tile_rows(n, width):
    best = 16
    for t in range(16, max(16, (384 * 1024) // width) + 1, 16):
        if n % t == 0:
            best = t
    assert n % best == 0
    return best


def add_partials(pf, got, place):
    half, wd = pf.shape
    tr = _tile_rows(half, wd)

    def body(pref, pf_ref, got_ref, o_ref):
        o_ref[...] = ((pf_ref[...] + got_ref[0].astype(F32)) + got_ref[1].astype(F32)) + got_ref[2].astype(F32)

    return pl.pallas_call(
        body, name="add_partials",
        grid_spec=pltpu.PrefetchScalarGridSpec(
            num_scalar_prefetch=1, grid=(half // tr,),
            in_specs=[pl.BlockSpec((tr, wd), lambda i, pr: (i, 0)),
                      pl.BlockSpec((3, tr, wd), lambda i, pr: (0, i, 0))],
            out_specs=pl.BlockSpec((None, tr, wd), lambda i, pr: (pr[1], i, 0))),
        out_shape=jax.ShapeDtypeStruct((2, half, wd), F32),
        compiler_params=_cp(("parallel",), VMEM_LIMIT),
    )(place, pf, got)


class GradReduce:
    def __init__(self, place, names, tag):
        self.place, self.names, self.tag = place, names, tag
        self.nb = len(_width_groups(names))

    def _swap_plan(self, refs):
        x, y, c = _place()
        plan = []
        for g_ref, o_ref in zip(refs[:self.nb], refs[self.nb:]):
            half = o_ref.shape[1]
            plan.append((g_ref.at[:, pl.ds(pl.multiple_of((1 - c) * half, 16), half), :], o_ref, (x, y, 1 - c)))
        return plan

    def _exchange_plan(self, refs):
        x, y, c = _place()
        me = 2 * x + y
        return [(p_ref.at[k - 1], o_ref.at[k - 1], (((me + k) % N_CHIPS) // 2, ((me + k) % N_CHIPS) % 2, c))
                for p_ref, o_ref in zip(refs[:self.nb], refs[self.nb:]) for k in range(1, N_CHIPS)]

    def _join_plan(self, refs):
        x, y, c = _place()
        return [(r.at[c], r.at[c], (x, y, 1 - c)) for r in refs]

    def _join_wait_plan(self, refs):
        x, y, c = _place()
        return [(r.at[c], r.at[1 - c], (x, y, 1 - c)) for r in refs]

    def begin(self, grads):
        gs = grad_buffers(grads, self.names)
        lands = [_landing((N_CHIPS, g.shape[1] // 2, g.shape[2]), BF16) for g in gs]
        self.s1, self.r1, self.b1, tok = copies_start(self.tag + "_swap_start", gs + lands, self.nb, self._swap_plan)
        return tok

    def exchange(self, after):
        b1 = copies_wait(self.tag + "_swap_wait", self.s1, self.r1, self.b1, after, self._swap_plan)
        sums = [add_halves(g, other, self.place) for g, other in zip(b1[:self.nb], b1[self.nb:])]
        self.pfs = [pf for pf, _ in sums]
        pbs = [pb for _, pb in sums]
        lands = [_landing(pb.shape, BF16) for pb in pbs]
        self.s2, self.r2, self.b2, tok = copies_start(self.tag + "_exchange_start", pbs + lands, 3 * self.nb,
                                                      self._exchange_plan)
        return tok

    def join(self, after):
        b2 = copies_wait(self.tag + "_exchange_wait", self.s2, self.r2, self.b2, after, self._exchange_plan)
        boths = [add_partials(pf, got, self.place) for pf, got in zip(self.pfs, b2[self.nb:])]
        self.s3, self.r3, self.b3, tok = copies_start(self.tag + "_join_start", boths, self.nb, self._join_plan)
        return tok

    def finish(self, after):
        boths = copies_wait(self.tag + "_join_wait", self.s3, self.r3, self.b3, after, self._join_wait_plan)
        return split_buffers([b.reshape(-1, b.shape[2]) for b in boths], self.names)


SMALL = (("g_mix", D_MODEL), ("g_ffn", D_MODEL), ("g_ple", D_MODEL), ("g_final", D_MODEL),
         ("conv_w", CONV_K * CONV_CH), ("rel_bias", A_HEADS * N_REL), ("w_onorm", B_DIM),
         ("a_log", B_HEADS), ("dt_bias", B_HEADS), ("loss", 1))


def _pad128(v):
    v = v.reshape(-1)
    return jnp.pad(v, (0, -v.shape[0] % 128))


def pack_small(d, names, rows):
    flat = jnp.concatenate([_pad128(d[n]) for n in names]).reshape(-1, 128)
    return jnp.pad(flat, ((0, rows - flat.shape[0]), (0, 0)))


def unpack_small(flat, names_sizes):
    out, r0 = {}, 0
    v = flat.reshape(-1)
    for n, size in names_sizes:
        out[n] = v[r0:r0 + size]
        r0 += -(-size // 128) * 128
    return out


def kernel(x, p, g_mix, w_in, conv_w, a_log, dt_bias, rel_bias, w_onorm, w_branch_a, w_branch_b, w_out, g_ffn, w_gate_up, w_down, g_ple, w_ple_gate, w_ple_proj, g_final, loss_target, m_g_mix, m_w_in, m_conv_w, m_a_log, m_dt_bias, m_rel_bias, m_w_onorm, m_w_branch_a, m_w_branch_b, m_w_out, m_g_ffn, m_w_gate_up, m_w_down, m_g_ple, m_w_ple_gate, m_w_ple_proj, m_g_final, v_g_mix, v_w_in, v_conv_w, v_a_log, v_dt_bias, v_rel_bias, v_w_onorm, v_w_branch_a, v_w_branch_b, v_w_out, v_g_ffn, v_w_gate_up, v_w_down, v_g_ple, v_w_ple_gate, v_w_ple_proj, v_g_final):
    names = ["g_mix", "w_in", "conv_w", "a_log", "dt_bias", "rel_bias", "w_onorm", "w_branch_a", "w_branch_b",
             "w_out", "g_ffn", "w_gate_up", "w_down", "g_ple", "w_ple_gate", "w_ple_proj", "g_final"]
    w = dict(zip(names, [g_mix, w_in, conv_w, a_log, dt_bias, rel_bias, w_onorm, w_branch_a, w_branch_b, w_out,
                         g_ffn, w_gate_up, w_down, g_ple, w_ple_gate, w_ple_proj, g_final]))
    m = dict(zip(names, [m_g_mix, m_w_in, m_conv_w, m_a_log, m_dt_bias, m_rel_bias, m_w_onorm, m_w_branch_a,
                         m_w_branch_b, m_w_out, m_g_ffn, m_w_gate_up, m_w_down, m_g_ple, m_w_ple_gate,
                         m_w_ple_proj, m_g_final]))
    v = dict(zip(names, [v_g_mix, v_w_in, v_conv_w, v_a_log, v_dt_bias, v_rel_bias, v_w_onorm, v_w_branch_a,
                         v_w_branch_b, v_w_out, v_g_ffn, v_w_gate_up, v_w_down, v_g_ple, v_w_ple_gate,
                         v_w_ple_proj, v_g_final]))
    xi, yi, ci = _place()
    chip = 2 * xi + yi
    big_names = [n for n, _, _ in BIG]

    shards2d = {n: w[n].reshape(w[n].shape[-2:]) for n in big_names}
    shards_bf = {n: a.astype(BF16) for n, a in shards2d.items()}
    g4 = allgather_weights(shards_bf, FIRST_WEIGHTS)["w_in"]
    place = jnp.stack([chip, ci]).astype(jnp.int32)
    conv_sh = jnp.where(ci == 0, w["conv_w"].reshape(CONV_K, CONV_CH // N_CHIPS), 0.0)
    conv_slots = lax.dynamic_update_slice(jnp.zeros((N_CHIPS, CONV_K, CONV_CH // N_CHIPS), F32), conv_sh[None],
                                          (chip, 0, 0))
    conv_all = small_allreduce(conv_slots.reshape(-1, 128), "gather_conv_w")
    conv_full = jnp.transpose(conv_all.reshape(N_CHIPS, CONV_K, CONV_CH // N_CHIPS), (1, 0, 2)).reshape(CONV_K, CONV_CH)
    small = {n: w[n] for n in names if n not in big_names}
    small["conv_w"] = conv_full

    grad_x, grads, small_grads, reduced_early = local_step(
        x, p[0], loss_target, g4, small, LaterWeights(shards_bf), GradReduce(place, EARLY_GRADS, "grads"))

    late = GradReduce(place, LATE_GRADS, "late")
    tok = late.begin(grads)
    small_names = [n for n, _ in SMALL]
    dbt, dbf = small_grads.pop("rel_bias_parts")
    small_grads["rel_bias"] = bias_band_grad(dbt + tok, dbf).reshape(-1)
    tok = late.exchange(small_grads["rel_bias"])
    small_grads["loss"] = small_grads["loss"] + tok
    red_flat = small_allreduce(pack_small(small_grads, small_names, 112), "allreduce_small")
    red = unpack_small(red_flat, SMALL)
    dep, red_flat = lax.optimization_barrier((jnp.full((8, 128), tok, F32), red_flat))
    gshard = dict(reduced_early)
    loss = red["loss"][0]
    conv_g = lax.dynamic_slice(red["conv_w"].reshape(CONV_K, N_CHIPS, CONV_CH // N_CHIPS), (0, chip, 0),
                               (CONV_K, 1, CONV_CH // N_CHIPS))
    gsmall = {n: red[n].reshape(w[n].shape) for n in small_names if n not in ("loss", "conv_w")}
    gsmall["conv_w"] = conv_g.reshape(w["conv_w"].shape)

    grad, delta, new_m, new_v = {}, {}, {}, {}
    for n in list(EARLY_GRADS) + list(LATE_GRADS):
        if n in LATE_GRADS:
            late.join(v_)
            gshard.update(late.finish(v_))
        shp = w[n].shape
        d_, m_, v_ = adamw(shards2d[n], gshard[n], m[n].reshape(shp[-2:]), v[n].reshape(shp[-2:]), "adamw_" + n,
                           dep=dep if n in EARLY_GRADS else None)
        dep, v_ = lax.optimization_barrier((dep, v_))
        grad[n], delta[n], new_m[n], new_v[n] = gshard[n].reshape(shp), d_.reshape(shp), m_.reshape(shp), v_.reshape(shp)
    snames = [n for n in small_names if n != "loss"]
    ssizes = [(n, w[n].size) for n in snames]
    pk = lambda d: pack_small(d, snames, 64)
    d_, m_, v_ = adamw(pk(w), pk(gsmall), pk(m), pk(v), "adamw_small")
    ds, ms, vs = unpack_small(d_, ssizes), unpack_small(m_, ssizes), unpack_small(v_, ssizes)
    for n in snames:
        shp = w[n].shape
        grad[n], delta[n], new_m[n], new_v[n] = gsmall[n], ds[n].reshape(shp), ms[n].reshape(shp), vs[n].reshape(shp)

    return (loss, grad_x, *[grad[n] for n in names], *[delta[n] for n in names],
            *[new_m[n] for n in names], *[new_v[n] for n in names])
```

```python
import functools

import jax
import jax.numpy as jnp
from jax import lax
from jax.experimental import pallas as pl
from jax.experimental.pallas import tpu as pltpu

F32 = jnp.float32
BF16 = jnp.bfloat16
MESH = pl.DeviceIdType.MESH

D_MODEL = 1024
CHUNK = 64
PLE_DIM = 256
EPS = 1e-6
A_HEADS = 8
A_HEAD_DIM = 64
A_WIDTH = 512
A_LOOKBACK = 8
BAND = (A_LOOKBACK + 1) * CHUNK
TAIL = 3 * CHUNK
REL_CLIP = 128
N_REL = 2 * REL_CLIP + 1
B_HEADS = 4
B_DIM = 128
B_WIDTH = 512
CONV_K = 4
CONV_CH = 1536
D_FF = 2816
SPLIT_Z = 3584
D_IN = 5640
ADAM_LR, ADAM_B1, ADAM_B2, ADAM_EPS, ADAM_WD, ADAM_STEP = 0.001, 0.9, 0.999, 1e-08, 0.01, 10

P_GATES, P_QA, P_KA, P_VA, P_CONV, P_Z, P_BD, P_WIDTH = 0, 2048, 2560, 3072, 3584, 5120, 5632, 5760

VMEM_LIMIT = 56 * 1024 * 1024


def _cp(sem, vmem=None, **kw):
    return pltpu.CompilerParams(dimension_semantics=sem, vmem_limit_bytes=vmem, **kw)


def _tile(n, cap):
    best = None
    for t in range(128, cap + 1, 128):
        if n % t == 0:
            best = t
    assert best is not None, (n, cap)
    return best


def _nn(a, b, prec=None):
    return lax.dot_general(a, b, (((1,), (0,)), ((), ())), preferred_element_type=F32, precision=prec)


def _nt(a, b, prec=None):
    return lax.dot_general(a, b, (((1,), (1,)), ((), ())), preferred_element_type=F32, precision=prec)


def _tn(a, b, prec=None):
    return lax.dot_general(a, b, (((0,), (0,)), ((), ())), preferred_element_type=F32, precision=prec)


def _bnn(a, b, prec=None):
    return lax.dot_general(a, b, (((2,), (1,)), ((0,), (0,))), preferred_element_type=F32, precision=prec)


def _bnt(a, b, prec=None):
    return lax.dot_general(a, b, (((2,), (2,)), ((0,), (0,))), preferred_element_type=F32, precision=prec)


def _bf(a):
    return a.astype(BF16)


def _split(a):
    hi = a.astype(BF16)
    return hi, (a - hi.astype(F32)).astype(BF16)


def _split3(a):
    h1 = _bf(a)
    r1 = a - h1.astype(F32)
    h2 = _bf(r1)
    return h1, h2, _bf(r1 - h2.astype(F32))


def _bnn_exact(lhs_b, rhs):
    h1, h2, h3 = _split3(rhs)
    return _bnn(lhs_b, h1) + (_bnn(lhs_b, h2) + _bnn(lhs_b, h3))


def _bnn3(a, b):
    ah, al = a if isinstance(a, tuple) else _split(a)
    bh, bl = b if isinstance(b, tuple) else _split(b)
    return _bnn(ah, bh) + (_bnn(ah, bl) + _bnn(al, bh))


def _sigmoid(x):
    return 0.5 * jnp.tanh(0.5 * x) + 0.5


def _softplus(x):
    return jnp.maximum(x, 0.0) + jnp.log(1.0 + jnp.exp(-jnp.abs(x)))


def rms_matmul(x, g, w, name, tm=512, tn_cap=1024):
    t, d = x.shape
    n = w.shape[1]
    tm = min(tm, t)
    tn = _tile(n, tn_cap)

    nj = n // tn

    def body(x_ref, g_ref, w_ref, o_ref, h_ref, tail_ref):
        @pl.when(pl.program_id(1) == 0)
        def _():
            xv = x_ref[...]
            r = lax.rsqrt(jnp.mean(xv * xv, axis=-1, keepdims=True) + EPS)
            h_ref[...] = _bf(xv * r * g_ref[...])

        res = _nn(h_ref[...], w_ref[...])
        o_ref[...] = _bf(res)

        @pl.when(pl.program_id(1) == nj - 1)
        def _():
            tail_ref[...] = res[:, tn - 128:]

    return pl.pallas_call(
        body, name=name, grid=(t // tm, nj),
        in_specs=[pl.BlockSpec((tm, d), lambda i, j: (i, 0)),
                  pl.BlockSpec((1, d), lambda i, j: (0, 0)),
                  pl.BlockSpec((d, tn), lambda i, j: (0, j))],
        out_specs=[pl.BlockSpec((tm, tn), lambda i, j: (i, j)),
                   pl.BlockSpec((tm, d), lambda i, j: (i, 0)),
                   pl.BlockSpec((tm, 128), lambda i, j: (i, 0))],
        out_shape=[jax.ShapeDtypeStruct((t, n), BF16), jax.ShapeDtypeStruct((t, d), BF16),
                   jax.ShapeDtypeStruct((t, 128), F32)],
        compiler_params=_cp(("parallel", "arbitrary"), VMEM_LIMIT),
    )(x, g, w)


def matmul_tn(a, b, name, into=None, col0=0, width=None, tm=1024, tk_cap=1408, tn_cap=1408, tiles_major=False):
    m, k1 = a.shape
    n = b.shape[1]
    tm = min(tm, m)
    tk = _tile(k1, tk_cap)
    tn = _tile(n, tn_cap)
    while col0 % tn:
        tn = _tile(n, tn - 128)
    nk = m // tm
    c0 = col0 // tn

    def body(*refs):
        a_ref, b_ref, o_ref, acc = refs[0], refs[1], refs[-2], refs[-1]

        @pl.when(pl.program_id(2) == 0)
        def _():
            acc[...] = jnp.zeros_like(acc)

        acc[...] += _tn(_bf(a_ref[...]), _bf(b_ref[...]))

        @pl.when(pl.program_id(2) == nk - 1)
        def _():
            o_ref[...] = _bf(acc[...])

    in_specs = [pl.BlockSpec((tm, tk), lambda i, j, k: (k, i)),
                pl.BlockSpec((tm, tn), lambda i, j, k: (k, j))]
    args = [a, b]
    total = n if width is None else width
    aliases = {}
    if into is not None:
        in_specs.append(ANY)
        args.append(into)
        aliases = {2: 0}
    if tiles_major:
        out_spec = pl.BlockSpec((None, tk, tn), lambda i, j, k: (c0 + j, i, 0))
        out_shape = jax.ShapeDtypeStruct((total // tn, k1, tn), BF16)
    else:
        out_spec = pl.BlockSpec((tk, tn), lambda i, j, k: (i, c0 + j))
        out_shape = jax.ShapeDtypeStruct((k1, total), BF16)
    return pl.pallas_call(
        body, name=name, grid=(k1 // tk, n // tn, nk),
        in_specs=in_specs,
        out_specs=out_spec,
        out_shape=out_shape,
        scratch_shapes=[pltpu.VMEM((tk, tn), F32)],
        input_output_aliases=aliases,
        compiler_params=_cp(("parallel", "parallel", "arbitrary"), VMEM_LIMIT),
    )(*args)


def _tail_onehot(qi):
    r = lax.broadcasted_iota(jnp.int32, (384, TAIL), 0)
    kj = lax.broadcasted_iota(jnp.int32, (384, TAIL), 1)
    return (r == jnp.minimum(REL_CLIP + qi - kj, REL_CLIP) + REL_CLIP).astype(F32)


def bias_tail(rel_pad):
    def body(rb_ref, o_ref):
        parts = _split3(rb_ref[...])
        for qi in range(CHUNK):
            oh = _bf(_tail_onehot(qi))
            o_ref[qi] = _nn(parts[0], oh) + (_nn(parts[1], oh) + _nn(parts[2], oh))

    return pl.pallas_call(
        body, name="bias_tail",
        out_shape=jax.ShapeDtypeStruct((CHUNK, A_HEADS, TAIL), F32),
    )(rel_pad)


def bias_grad(db_t, db_far):
    def body(t_ref, f_ref, o_ref):
        acc = jnp.zeros((A_HEADS, 384), F32)
        for qi in range(CHUNK):
            oh = _bf(_tail_onehot(qi))
            parts = _split3(t_ref[qi])
            acc = acc + (_nt(parts[0], oh) + (_nt(parts[1], oh) + _nt(parts[2], oh)))
        far = jnp.sum(jnp.sum(f_ref[...], axis=2), axis=1, keepdims=True)
        lane = lax.broadcasted_iota(jnp.int32, (A_HEADS, 384), 1)
        o_ref[...] = acc + jnp.where(lane == 2 * REL_CLIP, far, 0.0)

    return pl.pallas_call(
        body, name="bias_grad",
        out_shape=jax.ShapeDtypeStruct((A_HEADS, 384), F32),
    )(db_t, db_far)


ATT_CB = 8


WIN = BAND + CHUNK


def _stack_heads(a, lane):
    return jnp.concatenate([jnp.where(lane < 64, a, 0.0), jnp.where(lane >= 64, a, 0.0)], axis=0)


def _fill_band_pads(k_ref, v_ref, kp, vp, s):
    z = jnp.zeros((A_LOOKBACK * CHUNK, 128), BF16)
    kp[pl.ds(0, A_LOOKBACK * CHUNK), :] = z
    vp[pl.ds(0, A_LOOKBACK * CHUNK), :] = z
    kp[pl.ds(A_LOOKBACK * CHUNK, s), :] = _bf(k_ref[...])
    vp[pl.ds(A_LOOKBACK * CHUNK, s), :] = _bf(v_ref[...])


def attn_fwd(proj, bias_band, b, s):
    t = b * s
    nc = s // CHUNK
    qb, kb_, vb_ = P_QA // 128, P_KA // 128, P_VA // 128

    nstep = nc // ATT_CB
    rows = ATT_CB * CHUNK

    def body(q_ref, k_ref, v_ref, b_ref, o_ref, lse_ref, kp, vp):
        n0 = pl.program_id(2) * ATT_CB

        @pl.when(n0 == 0)
        def _():
            _fill_band_pads(k_ref, v_ref, kp, vp, s)

        lane = lax.broadcasted_iota(jnp.int32, (2 * CHUNK, 128), 1)
        col = lax.broadcasted_iota(jnp.int32, (4 * CHUNK, WIN), 1)
        bias4 = b_ref[...]

        def two_pairs(i, carry):
            pps = (2 * i, 2 * i + 1)
            ns = [n0 + 2 * pp for pp in pps]
            r0s = [pl.multiple_of(pp * 2 * CHUNK, 2 * CHUNK) for pp in pps]
            starts = [pl.multiple_of(n * CHUNK, CHUNK) for n in ns]
            kbs = [kp[pl.ds(st_, WIN), :] for st_ in starts]
            vbs = [vp[pl.ds(st_, WIN), :] for st_ in starts]
            q4s = [_bf(_stack_heads(q_ref[pl.ds(r0, 2 * CHUNK), :] * (A_HEAD_DIM ** -0.5), lane)) for r0 in r0s]
            qks = [_nt(q4, kb) for q4, kb in zip(q4s, kbs)]
            scs = [jnp.where(col >= (A_LOOKBACK - n) * CHUNK, qk + bias4, -1e30) for n, qk in zip(ns, qks)]
            mxs = [jnp.max(sc, axis=1, keepdims=True) for sc in scs]
            ps = [jnp.exp(sc - mx) for sc, mx in zip(scs, mxs)]
            ls = [jnp.sum(p, axis=1, keepdims=True) for p in ps]
            o4s = [_nn(_bf(p), vb) / l for p, vb, l in zip(ps, vbs, ls)]
            for r0, o4, mx, l in zip(r0s, o4s, mxs, ls):
                lse4 = mx + jnp.log(l)
                o_ref[pl.ds(r0, 2 * CHUNK), :] = jnp.where(lane < 64, o4[:2 * CHUNK], o4[2 * CHUNK:])
                lse_ref[pl.ds(r0, 2 * CHUNK), :] = jnp.where(lane < 64, lse4[:2 * CHUNK], lse4[2 * CHUNK:])
            return carry

        lax.fori_loop(0, ATT_CB // 4, two_pairs, 0)

    return pl.pallas_call(
        body, name="attn_fwd", grid=(b, 4, nstep),
        in_specs=[pl.BlockSpec((rows, 128), lambda bb, m, n: (bb * nstep + n, qb + m)),
                  pl.BlockSpec((s, 128), lambda bb, m, n: (bb, kb_ + m)),
                  pl.BlockSpec((s, 128), lambda bb, m, n: (bb, vb_ + m)),
                  pl.BlockSpec((None, 4 * CHUNK, WIN), lambda bb, m, n: (m, 0, 0))],
        out_specs=[pl.BlockSpec((rows, 128), lambda bb, m, n: (bb * nstep + n, m)),
                   pl.BlockSpec((rows, 128), lambda bb, m, n: (bb * nstep + n, m))],
        out_shape=[jax.ShapeDtypeStruct((t, A_WIDTH), F32), jax.ShapeDtypeStruct((t, A_WIDTH), F32)],
        scratch_shapes=[pltpu.VMEM((s + A_LOOKBACK * CHUNK, 128), BF16),
                        pltpu.VMEM((s + A_LOOKBACK * CHUNK, 128), BF16)],
        compiler_params=_cp(("parallel", "parallel", "arbitrary"), VMEM_LIMIT),
    )(proj, proj, proj, bias_band)


def attn_bwd(proj, bias_band, y_a, lse, dy_a, b, s):
    t = b * s
    nc = s // CHUNK
    qb, kb_, vb_ = P_QA // 128, P_KA // 128, P_VA // 128
    pad = A_LOOKBACK * CHUNK
    nstep = nc // ATT_CB
    rows = ATT_CB * CHUNK

    def body(q_ref, k_ref, v_ref, b_ref, do_ref, o_ref, lse_ref,
             dq_ref, dk_ref, dv_ref, dbt_ref, dbf_ref, kp, vp, dkp, dvp):
        bb = pl.program_id(1)
        n0 = pl.program_id(2) * ATT_CB

        @pl.when(n0 == 0)
        def _():
            _fill_band_pads(k_ref, v_ref, kp, vp, s)
            dkp[...] = jnp.zeros_like(dkp)
            dvp[...] = jnp.zeros_like(dvp)

        @pl.when((n0 == 0) & (bb == 0))
        def _():
            dbt_ref[...] = jnp.zeros_like(dbt_ref)
            dbf_ref[...] = jnp.zeros_like(dbf_ref)

        lane = lax.broadcasted_iota(jnp.int32, (2 * CHUNK, 128), 1)
        col = lax.broadcasted_iota(jnp.int32, (4 * CHUNK, WIN), 1)
        bias4 = b_ref[...]

        def two_pairs(i, carry):
            pps = (2 * i, 2 * i + 1)
            two = range(2)
            ns = [n0 + 2 * pp for pp in pps]
            r0s = [pl.multiple_of(pp * 2 * CHUNK, 2 * CHUNK) for pp in pps]
            starts = [pl.multiple_of(n * CHUNK, CHUNK) for n in ns]
            kbs = [kp[pl.ds(st_, WIN), :] for st_ in starts]
            vbs = [vp[pl.ds(st_, WIN), :] for st_ in starts]
            q4bs = [_bf(_stack_heads(q_ref[pl.ds(r0, 2 * CHUNK), :] * (A_HEAD_DIM ** -0.5), lane)) for r0 in r0s]
            do4s = [_stack_heads(do_ref[pl.ds(r0, 2 * CHUNK), :], lane) for r0 in r0s]
            do4bs = [_bf(d) for d in do4s]
            os_ = [o_ref[pl.ds(r0, 2 * CHUNK), :] for r0 in r0s]
            lsevs = [lse_ref[pl.ds(r0, 2 * CHUNK), :] for r0 in r0s]
            lse4s = [jnp.concatenate([v_[:, 0:1], v_[:, 64:65]], axis=0) for v_ in lsevs]
            qks = [_nt(q4bs[j], kbs[j]) for j in two]
            dps = [_nt(do4bs[j], vbs[j]) for j in two]
            deltas = [jnp.sum(do4s[j] * jnp.concatenate([os_[j], os_[j]], axis=0), axis=1, keepdims=True) for j in two]
            ps = [jnp.exp(jnp.where(col >= (A_LOOKBACK - ns[j]) * CHUNK, qks[j] + bias4, -1e30) - lse4s[j])
                  for j in two]
            pbs = [_bf(p) for p in ps]
            dss = [ps[j] * (dps[j] - deltas[j]) for j in two]
            dsbs = [_bf(d) for d in dss]
            dv_ws = [_tn(pbs[j], do4bs[j]) for j in two]
            dq4s = [_nn(dsbs[j], kbs[j]) for j in two]
            dk_ws = [_tn(dsbs[j], q4bs[j]) for j in two]
            for j in two:
                dq_ref[pl.ds(r0s[j], 2 * CHUNK), :] = _bf(
                    jnp.where(lane < 64, dq4s[j][:2 * CHUNK], dq4s[j][2 * CHUNK:]) * (A_HEAD_DIM ** -0.5))
                dkp[pl.ds(starts[j], WIN), :] += dk_ws[j]
                dvp[pl.ds(starts[j], WIN), :] += dv_ws[j]
                dbt_ref[...] += dss[j][:, WIN - 256:]
                dbf_ref[...] += dss[j][:, 0:128] + dss[j][:, 128:256] + dss[j][:, 256:384]
            return carry

        lax.fori_loop(0, ATT_CB // 4, two_pairs, 0)

        @pl.when(n0 == nc - ATT_CB)
        def _():
            dk_ref[...] = _bf(dkp[pl.ds(pad, s), :])
            dv_ref[...] = _bf(dvp[pl.ds(pad, s), :])

    return pl.pallas_call(
        body, name="attn_bwd", grid=(4, b, nstep),
        in_specs=[pl.BlockSpec((rows, 128), lambda m, bb, n: (bb * nstep + n, qb + m)),
                  pl.BlockSpec((s, 128), lambda m, bb, n: (bb, kb_ + m)),
                  pl.BlockSpec((s, 128), lambda m, bb, n: (bb, vb_ + m)),
                  pl.BlockSpec((None, 4 * CHUNK, WIN), lambda m, bb, n: (m, 0, 0)),
                  pl.BlockSpec((rows, 128), lambda m, bb, n: (bb * nstep + n, m)),
                  pl.BlockSpec((rows, 128), lambda m, bb, n: (bb * nstep + n, m)),
                  pl.BlockSpec((rows, 128), lambda m, bb, n: (bb * nstep + n, m))],
        out_specs=[pl.BlockSpec((rows, 128), lambda m, bb, n: (bb * nstep + n, m)),
                   pl.BlockSpec((s, 128), lambda m, bb, n: (bb, m)),
                   pl.BlockSpec((s, 128), lambda m, bb, n: (bb, m)),
                   pl.BlockSpec((None, 4 * CHUNK, 256), lambda m, bb, n: (m, 0, 0)),
                   pl.BlockSpec((None, 4 * CHUNK, 128), lambda m, bb, n: (m, 0, 0))],
        out_shape=[jax.ShapeDtypeStruct((t, A_WIDTH), BF16)] * 3
        + [jax.ShapeDtypeStruct((4, 4 * CHUNK, 256), F32),
           jax.ShapeDtypeStruct((4, 4 * CHUNK, 128), F32)],
        scratch_shapes=[pltpu.VMEM((s + pad, 128), BF16), pltpu.VMEM((s + pad, 128), BF16),
                        pltpu.VMEM((s + pad, 128), F32), pltpu.VMEM((s + pad, 128), F32)],
        compiler_params=_cp(("parallel", "arbitrary", "arbitrary"), VMEM_LIMIT),
    )(proj, proj, proj, bias_band, dy_a, y_a, lse)


def _conv_taps(x, w, s):
    row = lax.broadcasted_iota(jnp.int32, x.shape, 0)
    shifted = [x] + [jnp.where(row >= i, pltpu.roll(x, i, 0), 0.0) for i in range(1, CONV_K)]
    acc = shifted[0] * w[CONV_K - 1:CONV_K, :]
    for i in range(1, CONV_K):
        acc = acc + shifted[i] * w[CONV_K - 1 - i:CONV_K - i, :]
    return acc, shifted


def conv_fwd(proj, conv_w8, b, s):
    cb = 512
    c0 = P_CONV // cb

    def body(x_ref, w_ref, o_ref):
        a, _ = _conv_taps(x_ref[...].astype(F32), w_ref[...], s)
        o_ref[...] = a * _sigmoid(a)

    return pl.pallas_call(
        body, name="conv_fwd", grid=(b, CONV_CH // cb),
        in_specs=[pl.BlockSpec((s, cb), lambda bb, j: (bb, c0 + j)),
                  pl.BlockSpec((8, cb), lambda bb, j: (0, j))],
        out_specs=pl.BlockSpec((s, cb), lambda bb, j: (bb, j)),
        out_shape=jax.ShapeDtypeStruct((b * s, CONV_CH), F32),
        compiler_params=_cp(("parallel", "parallel"), VMEM_LIMIT),
    )(proj, conv_w8)


def conv_bwd(proj, conv_w8, dc3, b, s):
    cb = 512
    c0 = P_CONV // cb

    def body(x_ref, w_ref, dc_ref, dx_ref, dw_ref):
        @pl.when(pl.program_id(1) == 0)
        def _():
            dw_ref[...] = jnp.zeros_like(dw_ref)

        w = w_ref[...]
        a, shifted = _conv_taps(x_ref[...].astype(F32), w, s)
        sg = _sigmoid(a)
        da = dc_ref[...] * (sg * (1.0 + a * (1.0 - sg)))
        row = lax.broadcasted_iota(jnp.int32, da.shape, 0)
        dx = da * w[CONV_K - 1:CONV_K, :]
        for i in range(1, CONV_K):
            dx = dx + jnp.where(row < s - i, pltpu.roll(da, s - i, 0), 0.0) * w[CONV_K - 1 - i:CONV_K - i, :]
        dx_ref[...] = _bf(dx)
        r8 =lax.broadcasted_iota(jnp.int32, (8, cb), 0)
        dw = jnp.zeros((8, cb), F32)
        for i in range(CONV_K):
            dw = dw + jnp.where(r8 == CONV_K - 1 - i, jnp.sum(da * shifted[i], axis=0, keepdims=True), 0.0)
        dw_ref[...] += dw

    return pl.pallas_call(
        body, name="conv_bwd", grid=(CONV_CH // cb, b),
        in_specs=[pl.BlockSpec((s, cb), lambda j, bb: (bb, c0 + j)),
                  pl.BlockSpec((8, cb), lambda j, bb: (0, j)),
                  pl.BlockSpec((None, s, cb), lambda j, bb: (j, bb, 0))],
        out_specs=[pl.BlockSpec((s, cb), lambda j, bb: (bb, j)),
                   pl.BlockSpec((8, cb), lambda j, bb: (0, j))],
        out_shape=[jax.ShapeDtypeStruct((b * s, CONV_CH), BF16), jax.ShapeDtypeStruct((8, CONV_CH), F32)],
        compiler_params=_cp(("parallel", "arbitrary"), VMEM_LIMIT),
    )(proj, conv_w8, dc3)


def _pick_lane(v, k):
    lane = lax.broadcasted_iota(jnp.int32, v.shape, 1)
    return jnp.sum(jnp.where(lane == k, v, 0.0), axis=1, keepdims=True)


def _chunk_masks(ncb):
    i = lax.broadcasted_iota(jnp.int32, (ncb, CHUNK, CHUNK), 1)
    j = lax.broadcasted_iota(jnp.int32, (ncb, CHUNK, CHUNK), 2)
    return i, j


def _col_of_row(rowvec, eye):
    return jnp.sum(jnp.where(eye, rowvec, 0.0), axis=2, keepdims=True)


def _dn_chunk_math(cq, ck, cv, bd, al_row, dtb_row, h, ncb, tm=None):
    r = ncb * CHUNK
    i, j = _chunk_masks(ncb)
    eye = i == j
    low = i >= j
    strict = i > j
    ones = jnp.ones((ncb, CHUNK, CHUNK), F32)

    braw = _pick_lane(bd, h)
    draw = _pick_lane(bd, B_HEADS + h)
    al = _pick_lane(al_row, h)
    dtb = _pick_lane(dtb_row, h)
    ea = jnp.exp(al)
    beta = _sigmoid(braw)
    sp_arg = draw + dtb
    g = -ea * _softplus(sp_arg)

    rq = lax.rsqrt(jnp.sum(cq * cq, axis=1, keepdims=True) + EPS)
    rk = lax.rsqrt(jnp.sum(ck * ck, axis=1, keepdims=True) + EPS)
    nq = cq * rq
    kn = ck * rk
    qn = nq * (B_DIM ** -0.5)

    def c3(a):
        return a.reshape(ncb, CHUNK, a.shape[-1])

    qn3, kn3, v3, beta3 = c3(qn), c3(kn), c3(cv), c3(beta)
    gb = jnp.broadcast_to(c3(g), (ncb, CHUNK, CHUNK))
    gc_b = _bnn_exact(low.astype(BF16), gb)
    gr_b = _bnn_exact(_bf(ones), jnp.where(eye, gc_b, 0.0))
    dm = jnp.where(low, jnp.exp(jnp.where(low, gc_b - gr_b, 0.0)), 0.0)
    gc = gc_b[:, :, 0:1]
    gl = gc_b[:, CHUNK - 1:CHUNK, 0:1]
    gam = jnp.exp(gc)
    egl = jnp.exp(gl)
    edec = jnp.exp(gl - gc)

    knb = _bf(kn3)
    kk = _bnt(knb, knb)
    kd = jnp.where(strict, kk * dm, 0.0)
    a = beta3 * kd
    sz = 1 if tm is None else CHUNK
    if tm is None:
        tm = eye.astype(F32)
    while sz < CHUNK:
        off = jnp.where(((i // (2 * sz)) == (j // (2 * sz))) & ((i // sz) != (j // sz)), a, 0.0)
        tmb = _bf(tm)
        tm = tm - _bnn(_bf(_bnn(tmb, _bf(off))), tmb)
        sz *= 2
    bv = beta3 * v3
    bk = (beta3 * gam) * kn3
    sol = _bnn3(_split(tm), jnp.concatenate([bv, bk], axis=2))
    u, wk = sol[:, :, :B_DIM], sol[:, :, B_DIM:]
    qk = _bnt(_bf(qn3), knb)
    p = jnp.where(low, qk * dm, 0.0)
    kdec = kn3 * edec
    qg = gam * qn3
    return dict(beta=beta3, g=c3(g), ea=ea, sp_arg=c3(sp_arg), rq=c3(rq), rk=c3(rk), nq=c3(nq),
                qn=qn3, kn=kn3, v=v3, gc=gc, gl=gl, gam=gam, egl=egl, edec=edec, dm=dm, kd=kd, a=a,
                tm=tm, u=u, wk=wk, qk=qk, p=p, kdec=kdec, qg=qg, eye=eye, low=low, strict=strict)


def dn_prep(c, proj, al_row, dtb_row, b, s, ncb=32):
    t = b * s
    r = ncb * CHUNK
    nblk = t // r
    bd_blk = 0

    def body(cq_ref, ck_ref, cv_ref, bd_ref, al_ref, dtb_ref, u_ref, wk_ref, qg_ref, kdec_ref, p_ref, egl_ref,
             tm_ref):
        h = pl.program_id(1)
        m = _dn_chunk_math(cq_ref[...], ck_ref[...], cv_ref[...], bd_ref[...].astype(F32), al_ref[...], dtb_ref[...], h, ncb)
        tm_ref[...] = m["tm"].reshape(r, CHUNK)
        u_ref[...] = m["u"].reshape(r, B_DIM)
        wk_ref[...] = _bf(m["wk"].reshape(r, B_DIM))
        qg_ref[...] = _bf(m["qg"].reshape(r, B_DIM))
        kdec_ref[...] = _bf(m["kdec"].reshape(r, B_DIM))
        p_ref[...] = m["p"].reshape(r, CHUNK)
        egl_ref[...] = jnp.broadcast_to(m["egl"], (ncb, 8, 128)).reshape(ncb * 8, 128)

    col = lambda k: pl.BlockSpec((r, 128), lambda i, h: (i, k * B_HEADS + h))
    out_col = pl.BlockSpec((r, 128), lambda i, h: (i, h))
    small = pl.BlockSpec((1, 128), lambda i, h: (0, 0))
    return pl.pallas_call(
        body, name="dn_prep", grid=(nblk, B_HEADS),
        in_specs=[col(0), col(1), col(2), pl.BlockSpec((r, 128), lambda i, h: (i, bd_blk)), small, small],
        out_specs=[out_col, out_col, out_col, out_col,
                   pl.BlockSpec((None, r, CHUNK), lambda i, h: (h, i, 0)),
                   pl.BlockSpec((None, ncb * 8, 128), lambda i, h: (h, i, 0)),
                   pl.BlockSpec((None, r, CHUNK), lambda i, h: (h, i, 0))],
        out_shape=[jax.ShapeDtypeStruct((t, B_WIDTH), F32)] + [jax.ShapeDtypeStruct((t, B_WIDTH), BF16)] * 3
        + [jax.ShapeDtypeStruct((B_HEADS, t, CHUNK), F32),
           jax.ShapeDtypeStruct((B_HEADS, t // 8, 128), F32),
           jax.ShapeDtypeStruct((B_HEADS, t, CHUNK), F32)],
        compiler_params=_cp(("parallel", "parallel"), VMEM_LIMIT),
    )(c, c, c, proj, al_row, dtb_row)


SCAN_CB = 8


def dn_scan_fwd(u, wk, qg, kdec, p, egl, b, s):
    t = b * s
    nc = s // CHUNK

    def body(u_ref, wk_ref, qg_ref, kdec_ref, p_ref, egl_ref, o_ref, ss_ref, st):
        @pl.when(pl.program_id(0) == 0)
        def _():
            st[...] = jnp.zeros_like(st)

        chains = [(bb, h) for bb in range(b) for h in range(B_HEADS)]
        sls = [slice(h * B_DIM, (h + 1) * B_DIM) for _, h in chains]
        states = [st[bb * B_HEADS + h] for bb, h in chains]
        for cc in range(SCAN_CB):
            rs = slice(cc * CHUNK, (cc + 1) * CHUNK)
            sbs = [_bf(sh) for sh in states]
            ws = [u_ref[bb, rs, sl] - _nt(wk_ref[bb, rs, sl], sb) for (bb, _), sl, sb in zip(chains, sls, sbs)]
            qs = [_nt(qg_ref[bb, rs, sl], sb) for (bb, _), sl, sb in zip(chains, sls, sbs)]
            wbs = [_bf(w) for w in ws]
            outs = [q + _nn(_bf(p_ref[h, bb, rs, :]), wb) for (bb, h), q, wb in zip(chains, qs, wbs)]
            new_states = [egl_ref[h, bb, cc * 8:cc * 8 + 1, :] * sh + _tn(wb, kdec_ref[bb, rs, sl])
                          for (bb, h), sl, sh, wb in zip(chains, sls, states, wbs)]
            for (bb, h), sh, o in zip(chains, states, outs):
                ss_ref[bb, cc, h] = sh
                o_ref[bb, rs, h * B_DIM:(h + 1) * B_DIM] = o
            states = new_states
        for (bb, h), sh in zip(chains, states):
            st[bb * B_HEADS + h] = sh

    r3 = lambda a: a.reshape(b, s, B_WIDTH)
    rows = SCAN_CB * CHUNK
    act = pl.BlockSpec((b, rows, B_WIDTH), lambda n: (0, n, 0))
    o, states = pl.pallas_call(
        body, name="dn_scan_fwd", grid=(nc // SCAN_CB,),
        in_specs=[act, act, act, act,
                  pl.BlockSpec((B_HEADS, b, rows, CHUNK), lambda n: (0, 0, n, 0)),
                  pl.BlockSpec((B_HEADS, b, SCAN_CB * 8, 128), lambda n: (0, 0, n, 0))],
        out_specs=[act, pl.BlockSpec((b, SCAN_CB, B_HEADS, B_DIM, B_DIM), lambda n: (0, n, 0, 0, 0))],
        out_shape=[jax.ShapeDtypeStruct((b, s, B_WIDTH), F32),
                   jax.ShapeDtypeStruct((b, nc, B_HEADS, B_DIM, B_DIM), F32)],
        scratch_shapes=[pltpu.VMEM((b * B_HEADS, B_DIM, B_DIM), F32)],
        compiler_params=_cp(("arbitrary",), VMEM_LIMIT),
    )(r3(u), r3(wk), r3(qg), r3(kdec), p.reshape(B_HEADS, b, s, CHUNK), egl.reshape(B_HEADS, b, s // 8, 128))
    return o.reshape(t, B_WIDTH), states


def dn_scan_bwd(u, wk, qg, kdec, p, egl, states, do, b, s):
    t = b * s
    nc = s // CHUNK

    def body(u_ref, wk_ref, qg_ref, kdec_ref, p_ref, egl_ref, ss_ref, do_ref,
             dw_ref, dwk_ref, dqg_ref, dkdec_ref, dp_ref, degl_ref, dst):
        @pl.when(pl.program_id(0) == 0)
        def _():
            dst[...] = jnp.zeros_like(dst)

        chains = [(bb, h) for bb in range(b) for h in range(B_HEADS)]
        dstates = [dst[bb * B_HEADS + h] for bb, h in chains]
        n8 = range(len(chains))
        sls = [slice(h * B_DIM, (h + 1) * B_DIM) for _, h in chains]
        for cc in reversed(range(SCAN_CB)):
            rs = slice(cc * CHUNK, (cc + 1) * CHUNK)
            shs = [ss_ref[bb, cc, h] for bb, h in chains]
            sbs = [_bf(sh) for sh in shs]
            dsbs = [_bf(dsp) for dsp in dstates]
            wkbs = [wk_ref[bb, rs, sl] for (bb, _), sl in zip(chains, sls)]
            dobs = [_bf(do_ref[bb, rs, sl]) for (bb, _), sl in zip(chains, sls)]
            t1 = [_nt(wkbs[i], sbs[i]) for i in n8]
            dwa = [_tn(_bf(p_ref[h, bb, rs, :]), dobs[i]) for i, (bb, h) in enumerate(chains)]
            dwb_ = [_nt(kdec_ref[bb, rs, sls[i]], dsbs[i]) for i, (bb, _) in enumerate(chains)]
            dqgs = [_nn(dobs[i], sbs[i]) for i in n8]
            dsq = [_tn(dobs[i], qg_ref[bb, rs, sls[i]]) for i, (bb, _) in enumerate(chains)]
            wbs = [_bf(u_ref[bb, rs, sls[i]] - t1[i]) for i, (bb, _) in enumerate(chains)]
            dws = [dwa[i] + dwb_[i] for i in n8]
            dwbs = [_bf(dw) for dw in dws]
            dwks = [-_nn(dwbs[i], sbs[i]) for i in n8]
            dkdecs = [_nn(wbs[i], dsbs[i]) for i in n8]
            dpms = [_nt(dobs[i], wbs[i]) for i in n8]
            dsw = [_tn(dwbs[i], wkbs[i]) for i in n8]
            tots = [jnp.sum(jnp.sum(shs[i] * dstates[i], axis=1, keepdims=True), axis=0, keepdims=True) for i in n8]
            new_dss = [egl_ref[h, bb, cc * 8:cc * 8 + 1, :] * dstates[i] + dsq[i] - dsw[i]
                       for i, (bb, h) in enumerate(chains)]
            for i, (bb, h) in enumerate(chains):
                dw_ref[bb, rs, sls[i]] = dws[i]
                dqg_ref[bb, rs, sls[i]] = dqgs[i]
                dwk_ref[bb, rs, sls[i]] = dwks[i]
                dkdec_ref[bb, rs, sls[i]] = dkdecs[i]
                dp_ref[h, bb, rs, :] = dpms[i]
                degl_ref[h, bb, cc * 8:(cc + 1) * 8, :] = jnp.broadcast_to(tots[i], (8, 128))
            dstates = new_dss
        for (bb, h), dsp in zip(chains, dstates):
            dst[bb * B_HEADS + h] = dsp

    r3 = lambda a: a.reshape(b, s, B_WIDTH)
    rows = SCAN_CB * CHUNK
    last = nc // SCAN_CB - 1
    act = pl.BlockSpec((b, rows, B_WIDTH), lambda n: (0, last - n, 0))
    pspec = pl.BlockSpec((B_HEADS, b, rows, CHUNK), lambda n: (0, 0, last - n, 0))
    espec = pl.BlockSpec((B_HEADS, b, SCAN_CB * 8, 128), lambda n: (0, 0, last - n, 0))
    outs = pl.pallas_call(
        body, name="dn_scan_bwd", grid=(nc // SCAN_CB,),
        in_specs=[act, act, act, act, pspec, espec,
                  pl.BlockSpec((b, SCAN_CB, B_HEADS, B_DIM, B_DIM), lambda n: (0, last - n, 0, 0, 0)),
                  act],
        out_specs=[act, act, act, act, pspec, espec],
        out_shape=[jax.ShapeDtypeStruct((b, s, B_WIDTH), F32)] * 4
        + [jax.ShapeDtypeStruct((B_HEADS, b, s, CHUNK), F32),
           jax.ShapeDtypeStruct((B_HEADS, b, s // 8, 128), F32)],
        scratch_shapes=[pltpu.VMEM((b * B_HEADS, B_DIM, B_DIM), F32)],
        compiler_params=_cp(("arbitrary",), VMEM_LIMIT),
    )(r3(u), r3(wk), r3(qg), r3(kdec), p.reshape(B_HEADS, b, s, CHUNK), egl.reshape(B_HEADS, b, s // 8, 128),
      states, r3(do))
    return (*[a.reshape(t, B_WIDTH) for a in outs[:4]], outs[4].reshape(B_HEADS, t, CHUNK),
            outs[5].reshape(B_HEADS, t // 8, 128))


def dn_post_bwd(c, proj, al_row, dtb_row, tmat, dw, dwk, dqg, dkdec, dp, degl, b, s, ncb=16):
    t = b * s
    r = ncb * CHUNK
    nblk = t // r
    bd_blk = 0

    def body(cq_ref, ck_ref, cv_ref, bd_ref, al_ref, dtb_ref, tm_ref, dw_ref, dwk_ref, dqg_ref, dkdec_ref, dp_ref,
             degl_ref, dc_ref, dbd_ref, dal_ref, ddtb_ref):
        h = pl.program_id(1)

        @pl.when((pl.program_id(0) == 0) & (h == 0))
        def _():
            dal_ref[...] = jnp.zeros_like(dal_ref)
            ddtb_ref[...] = jnp.zeros_like(ddtb_ref)

        m = _dn_chunk_math(cq_ref[...], ck_ref[...], cv_ref[...], bd_ref[...].astype(F32), al_ref[...], dtb_ref[...], h, ncb,
                           tm=tm_ref[...].reshape(ncb, CHUNK, CHUNK))
        eye, low, strict = m["eye"], m["low"], m["strict"]
        eyef = eye.astype(F32)

        def c3(a):
            return a.reshape(ncb, CHUNK, a.shape[-1])

        du, dwkv, dqg, dkdec = c3(dw_ref[...]), c3(dwk_ref[...]), c3(dqg_ref[...]), c3(dkdec_ref[...])
        dpm = jnp.where(low, c3(dp_ref[...]), 0.0)
        degl = degl_ref[...].reshape(ncb, 8, 128)[:, 0:1, 0:1]
        beta, gam, kn, qn, v = m["beta"], m["gam"], m["kn"], m["qn"], m["v"]
        dm, kd, a, p = m["dm"], m["kd"], m["a"], m["p"]
        knb, qnb = _bf(kn), _bf(qn)

        eyeb = _bf(eyef)
        th, tl = _split(m["tm"])
        tts = (_bf(_bnt(eyeb, th)), _bf(_bnt(eyeb, tl)))
        xy = _bnn3(tts, jnp.concatenate([du, dwkv], axis=2))
        x, y = xy[:, :, :B_DIM], xy[:, :, B_DIM:]
        da = -jnp.where(strict, _bnt(_bf(x), _bf(m["u"])) + _bnt(_bf(y), _bf(m["wk"])), 0.0)
        dv = beta * x
        sy = jnp.sum(y * kn, axis=2, keepdims=True)
        dbeta = jnp.sum(x * v, axis=2, keepdims=True) + gam * sy + jnp.sum(da * kd, axis=2, keepdims=True)
        dgam = beta * sy + jnp.sum(dqg * qn, axis=2, keepdims=True)
        dkk = da * beta * dm
        dqk = dpm * dm
        dkkb, dqkb = _bf(dkk), _bf(dqk)
        dkn = ((beta * gam) * y + _bnn(dkkb, knb) + _bnn(_bf(_bnt(eyeb, dkkb)), knb)
               + _bnn(_bf(_bnt(eyeb, dqkb)), qnb) + dkdec * m["edec"])
        dqn = gam * dqg + _bnn(dqkb, knb)
        mm = da * a + dpm * p
        ek = jnp.sum(dkdec * m["kdec"], axis=2, keepdims=True)
        dgc = (jnp.sum(mm, axis=2, keepdims=True) - _col_of_row(jnp.sum(mm, axis=1, keepdims=True), eye)
               + dgam * gam - ek)
        dgl = jnp.sum(ek, axis=1, keepdims=True) + degl * m["egl"]
        i, _ = _chunk_masks(ncb)
        dgc = dgc + jnp.where(i[:, :, 0:1] == CHUNK - 1, dgl, 0.0)
        upper = (i <= _chunk_masks(ncb)[1]).astype(BF16)
        dg = _bnn_exact(upper, jnp.broadcast_to(dgc, (ncb, CHUNK, CHUNK)))[:, :, 0:1]

        nq = m["nq"]
        dnq = dqn * (B_DIM ** -0.5)
        dcq = m["rq"] * (dnq - nq * jnp.sum(nq * dnq, axis=2, keepdims=True))
        dck = m["rk"] * (dkn - kn * jnp.sum(kn * dkn, axis=2, keepdims=True))
        dc_ref[0] = dcq.reshape(r, B_DIM)
        dc_ref[1] = dck.reshape(r, B_DIM)
        dc_ref[2] = dv.reshape(r, B_DIM)

        dbraw = (dbeta * beta * (1.0 - beta)).reshape(r, 1)
        sgm = _sigmoid(m["sp_arg"])
        ddraw3 = dg * (-m["ea"]) * sgm
        ddraw = ddraw3.reshape(r, 1)
        lane = lax.broadcasted_iota(jnp.int32, (r, 128), 1)
        contrib = jnp.where(lane == h, dbraw, 0.0) + jnp.where(lane == B_HEADS + h, ddraw, 0.0)

        @pl.when(h == 0)
        def _():
            dbd_ref[...] = contrib

        @pl.when(h != 0)
        def _():
            dbd_ref[...] += contrib

        lane8 = lax.broadcasted_iota(jnp.int32, (8, 128), 1)
        tot_al = jnp.sum(jnp.sum(dg * m["g"], axis=1, keepdims=True), axis=0, keepdims=True).reshape(1, 1)
        tot_dtb = jnp.sum(jnp.sum(ddraw3, axis=1, keepdims=True), axis=0, keepdims=True).reshape(1, 1)
        dal_ref[...] += jnp.where(lane8 == h, tot_al, 0.0)
        ddtb_ref[...] += jnp.where(lane8 == h, tot_dtb, 0.0)

    col = lambda k: pl.BlockSpec((r, 128), lambda i, h: (i, k * B_HEADS + h))
    hcol = pl.BlockSpec((r, 128), lambda i, h: (i, h))
    small = pl.BlockSpec((1, 128), lambda i, h: (0, 0))
    acc = pl.BlockSpec((8, 128), lambda i, h: (0, 0))
    return pl.pallas_call(
        body, name="dn_post_bwd", grid=(nblk, B_HEADS),
        in_specs=[col(0), col(1), col(2), pl.BlockSpec((r, 128), lambda i, h: (i, bd_blk)), small, small,
                  pl.BlockSpec((None, r, CHUNK), lambda i, h: (h, i, 0)),
                  hcol, hcol, hcol, hcol,
                  pl.BlockSpec((None, r, CHUNK), lambda i, h: (h, i, 0)),
                  pl.BlockSpec((None, ncb * 8, 128), lambda i, h: (h, i, 0))],
        out_specs=[pl.BlockSpec((3, r, 128), lambda i, h: (0, i, h)),
                   pl.BlockSpec((r, 128), lambda i, h: (i, 0)), acc, acc],
        out_shape=[jax.ShapeDtypeStruct((3, t, B_WIDTH), F32), jax.ShapeDtypeStruct((t, 128), F32),
                   jax.ShapeDtypeStruct((8, 128), F32), jax.ShapeDtypeStruct((8, 128), F32)],
        compiler_params=_cp(("arbitrary", "arbitrary"), VMEM_LIMIT),
    )(c, c, c, proj, al_row, dtb_row, tmat, dw, dwk, dqg, dkdec, dp, degl)


def make_bias_band(rel_bias):
    tail = bias_tail(jnp.pad(rel_bias, ((0, 0), (0, 384 - N_REL))))
    far = jnp.broadcast_to(rel_bias[:, 2 * REL_CLIP][:, None, None], (A_HEADS, CHUNK, BAND - TAIL))
    band = jnp.concatenate([far, jnp.transpose(tail, (1, 0, 2))], axis=2)
    off = jnp.full((A_HEADS, CHUNK, CHUNK), -1e30, F32)
    both = jnp.stack([jnp.concatenate([band, off], axis=2), jnp.concatenate([off, band], axis=2)], axis=1)
    return both.reshape(4, 4 * CHUNK, WIN)


def bias_band_grad(dbt, dbf):
    t5 = dbt.reshape(A_HEADS, 2, CHUNK, 256)
    tail = t5[:, 0, :, :TAIL] + t5[:, 1, :, CHUNK:]
    far = dbf.reshape(A_HEADS, 2, CHUNK, 128).sum(axis=1) + jnp.pad(t5[:, 1, :, :CHUNK], ((0, 0), (0, 0), (0, CHUNK)))
    return bias_grad(jnp.transpose(tail, (1, 0, 2)), far)[:, :N_REL]


def _rms(x):
    r = lax.rsqrt(jnp.mean(x * x, axis=-1, keepdims=True) + EPS)
    return r, x * r


def _rms_bwd(dh, g, r, n):
    dn = dh * g
    return r * (dn - n * jnp.mean(dn * n, axis=-1, keepdims=True)), dh * n


def _gated_onorm(o, z, w_on):
    parts = []
    for h in range(B_HEADS):
        sl = slice(h * B_DIM, (h + 1) * B_DIM)
        r, n = _rms(o[:, sl])
        parts.append((r, n))
    r4 = [p[0] for p in parts]
    n4 = jnp.concatenate([p[1] for p in parts], axis=1)
    w4 = jnp.concatenate([w_on] * B_HEADS, axis=1)
    sz = _sigmoid(z)
    silu = z * sz
    return n4 * w4 * silu, r4, n4, w4, sz, silu


def mid_fwd(x, y_a, o_b, proj, w_on, wa, wb, w_out, tm=512):
    t = x.shape[0]
    tm = min(tm, t)

    def body(x_ref, ya_ref, ob_ref, z_ref, ga_ref, gb_ref, won_ref, wa_ref, wb_ref, wo_ref, x1_ref, mg_ref):
        yb = _gated_onorm(ob_ref[...], z_ref[...].astype(F32), won_ref[...])[0]
        ua = _nn(_bf(ya_ref[...]), wa_ref[...])
        ub = _nn(_bf(yb), wb_ref[...])
        merged = _sigmoid(ga_ref[...].astype(F32)) * ua + _sigmoid(gb_ref[...].astype(F32)) * ub
        mb = _bf(merged)
        mg_ref[...] = mb
        x1_ref[...] = x_ref[...] + _nn(mb, wo_ref[...])

    rowd = pl.BlockSpec((tm, D_MODEL), lambda i: (i, 0))
    row5 = pl.BlockSpec((tm, 512), lambda i: (i, 0))
    full = lambda a: pl.BlockSpec(a.shape, lambda i: (0,) * a.ndim)
    return pl.pallas_call(
        body, name="mid_fwd", grid=(t // tm,),
        in_specs=[rowd, row5, row5,
                  pl.BlockSpec((tm, 512), lambda i: (i, P_Z // 512)),
                  pl.BlockSpec((tm, D_MODEL), lambda i: (i, 0)),
                  pl.BlockSpec((tm, D_MODEL), lambda i: (i, 1)),
                  full(w_on), full(wa), full(wb), full(w_out)],
        out_specs=[rowd, rowd],
        out_shape=[jax.ShapeDtypeStruct((t, D_MODEL), F32), jax.ShapeDtypeStruct((t, D_MODEL), BF16)],
        compiler_params=_cp(("parallel",), VMEM_LIMIT),
    )(x, y_a, o_b, proj, proj, proj, w_on, wa, wb, w_out)


def mid_bwd(dx1, merged, y_a, o_b, proj, w_on, wa, wb, w_out, tm=256):
    t = dx1.shape[0]
    tm = min(tm, t)

    def body(dx1_ref, mg_ref, ya_ref, ob_ref, z_ref, ga_ref, gb_ref, won_ref, wa_ref, wb_ref, wo_ref,
             dya_ref, dob_ref, dz_ref, dg_ref, dwo_ref, dwa_ref, dwb_ref, dwon_ref):
        @pl.when(pl.program_id(0) == 0)
        def _():
            dwo_ref[...] = jnp.zeros_like(dwo_ref)
            dwa_ref[...] = jnp.zeros_like(dwa_ref)
            dwb_ref[...] = jnp.zeros_like(dwb_ref)
            dwon_ref[...] = jnp.zeros_like(dwon_ref)

        dx1b = _bf(dx1_ref[...])
        dmerged = _nt(dx1b, wo_ref[...])
        dwo_ref[...] += _tn(mg_ref[...], dx1b)
        o = ob_ref[...]
        z = z_ref[...].astype(F32)
        yb, r4, n4, w4, sz, silu = _gated_onorm(o, z, won_ref[...])
        yab, ybb = _bf(ya_ref[...]), _bf(yb)
        ua = _nn(yab, wa_ref[...])
        ub = _nn(ybb, wb_ref[...])
        sa, sb = _sigmoid(ga_ref[...].astype(F32)), _sigmoid(gb_ref[...].astype(F32))
        dua, dub = _bf(dmerged * sa), _bf(dmerged * sb)
        dg_ref[:, 0:D_MODEL] = _bf(dmerged * ua * sa * (1.0 - sa))
        dg_ref[:, D_MODEL:2 * D_MODEL] = _bf(dmerged * ub * sb * (1.0 - sb))
        dwa_ref[...] += _tn(yab, dua)
        dwb_ref[...] += _tn(ybb, dub)
        dya_ref[...] = _nt(dua, wa_ref[...])
        dyb = _nt(dub, wb_ref[...])
        dz_ref[...] = _bf(dyb * (n4 * w4) * (sz * (1.0 + z * (1.0 - sz))))
        dnw = dyb * silu
        dwon = jnp.zeros((1, B_DIM), F32)
        for h in range(B_HEADS):
            sl = slice(h * B_DIM, (h + 1) * B_DIM)
            dxh, dgh = _rms_bwd(dnw[:, sl], won_ref[...], r4[h], n4[:, sl])
            dob_ref[:, sl] = dxh
            dwon = dwon + jnp.sum(dgh, axis=0, keepdims=True)
        dwon_ref[...] += jnp.broadcast_to(dwon, (8, B_DIM))

    rowd = pl.BlockSpec((tm, D_MODEL), lambda i: (i, 0))
    row5 = pl.BlockSpec((tm, 512), lambda i: (i, 0))
    full = lambda a: pl.BlockSpec(a.shape, lambda i: (0,) * a.ndim)
    fixed = lambda shp: pl.BlockSpec(shp, lambda i: (0,) * len(shp))
    return pl.pallas_call(
        body, name="mid_bwd", grid=(t // tm,),
        in_specs=[rowd, rowd, row5, row5,
                  pl.BlockSpec((tm, 512), lambda i: (i, P_Z // 512)),
                  pl.BlockSpec((tm, D_MODEL), lambda i: (i, 0)),
                  pl.BlockSpec((tm, D_MODEL), lambda i: (i, 1)),
                  full(w_on), full(wa), full(wb), full(w_out)],
        out_specs=[row5, row5, row5, pl.BlockSpec((tm, 2 * D_MODEL), lambda i: (i, 0)),
                   fixed((D_MODEL, D_MODEL)), fixed((A_WIDTH, D_MODEL)), fixed((B_WIDTH, D_MODEL)),
                   fixed((8, B_DIM))],
        out_shape=[jax.ShapeDtypeStruct((t, 512), F32), jax.ShapeDtypeStruct((t, 512), F32),
                   jax.ShapeDtypeStruct((t, 512), BF16), jax.ShapeDtypeStruct((t, 2 * D_MODEL), BF16),
           jax.ShapeDtypeStruct((D_MODEL, D_MODEL), F32), jax.ShapeDtypeStruct((A_WIDTH, D_MODEL), F32),
           jax.ShapeDtypeStruct((B_WIDTH, D_MODEL), F32), jax.ShapeDtypeStruct((8, B_DIM), F32)],
        compiler_params=_cp(("arbitrary",), VMEM_LIMIT),
    )(dx1, merged, y_a, o_b, proj, proj, proj, w_on, wa, wb, w_out)


FFN_TF = 1408


def ffn_up(x1, g, w_gu, tm=512, tf=FFN_TF):
    t = x1.shape[0]
    tm = min(tm, t)
    nf = D_FF // tf

    def body(x_ref, g_ref, wg_ref, wu_ref, gate_ref, up_ref, act_ref, h_ref):
        @pl.when(pl.program_id(1) == 0)
        def _():
            r, n = _rms(x_ref[...])
            h_ref[...] = _bf(n * g_ref[...])

        hb = h_ref[...]
        gate = _nn(hb, wg_ref[...])
        up = _nn(hb, wu_ref[...])
        gate_ref[...] = _bf(gate)
        up_ref[...] = _bf(up)
        act_ref[...] = _bf(gate * _sigmoid(gate) * up)

    ff = pl.BlockSpec((tm, tf), lambda i, j: (i, j))
    return pl.pallas_call(
        body, name="ffn_up", grid=(t // tm, nf),
        in_specs=[pl.BlockSpec((tm, D_MODEL), lambda i, j: (i, 0)),
                  pl.BlockSpec((1, D_MODEL), lambda i, j: (0, 0)),
                  pl.BlockSpec((D_MODEL, tf), lambda i, j: (0, j)),
                  pl.BlockSpec((D_MODEL, tf), lambda i, j: (0, nf + j))],
        out_specs=[ff, ff, ff, pl.BlockSpec((tm, D_MODEL), lambda i, j: (i, 0))],
        out_shape=[jax.ShapeDtypeStruct((t, D_FF), BF16)] * 3 + [jax.ShapeDtypeStruct((t, D_MODEL), BF16)],
        compiler_params=_cp(("parallel", "arbitrary"), VMEM_LIMIT),
    )(x1, g, w_gu, w_gu)


def matmul_residual(a, w, res, name, tm=512, tk=FFN_TF):
    t, k = a.shape
    n = w.shape[1]
    tm = min(tm, t)

    def body(a_ref, w_ref, r_ref, o_ref):
        @pl.when(pl.program_id(1) == 0)
        def _():
            o_ref[...] = r_ref[...]

        o_ref[...] += _nn(a_ref[...], w_ref[...])

    return pl.pallas_call(
        body, name=name, grid=(t // tm, k // tk),
        in_specs=[pl.BlockSpec((tm, tk), lambda i, j: (i, j)),
                  pl.BlockSpec((tk, n), lambda i, j: (j, 0)),
                  pl.BlockSpec((tm, n), lambda i, j: (i, 0))],
        out_specs=pl.BlockSpec((tm, n), lambda i, j: (i, 0)),
        out_shape=jax.ShapeDtypeStruct((t, n), F32),
        compiler_params=_cp(("parallel", "arbitrary"), VMEM_LIMIT),
    )(a, w, res)


def ffn_act_bwd(dx2, gate, up, w_down, tm=512, tf=FFN_TF):
    t = dx2.shape[0]
    tm = min(tm, t)

    def body(dx2_ref, gate_ref, up_ref, wd_ref, dgate_ref, dup_ref, dx2b_ref):
        @pl.when(pl.program_id(1) == 0)
        def _():
            dx2b_ref[...] = _bf(dx2_ref[...])

        dact = _nt(dx2b_ref[...], wd_ref[...])
        gt, upv = gate_ref[...].astype(F32), up_ref[...].astype(F32)
        sg = _sigmoid(gt)
        t = dact * sg
        dgate_ref[...] = _bf(t * upv * (1.0 + gt * (1.0 - sg)))
        dup_ref[...] = _bf(t * gt)

    ff = pl.BlockSpec((tm, tf), lambda i, j: (i, j))
    return pl.pallas_call(
        body, name="ffn_act_bwd", grid=(t // tm, D_FF // tf),
        in_specs=[pl.BlockSpec((tm, D_MODEL), lambda i, j: (i, 0)), ff, ff,
                  pl.BlockSpec((tf, D_MODEL), lambda i, j: (j, 0))],
        out_specs=[ff, ff],
        out_shape=[jax.ShapeDtypeStruct((t, D_FF), BF16)] * 2,
        scratch_shapes=[pltpu.VMEM((tm, D_MODEL), BF16)],
        compiler_params=_cp(("parallel", "arbitrary"), VMEM_LIMIT),
    )(dx2, gate, up, w_down)


def tail_fwd_bwd(x2, p, target, g_ple, g_final, w_pg, w_pp, tm=512):
    t = x2.shape[0]
    tm = min(tm, t)

    def body(x_ref, p_ref, t_ref, gp_ref, gf_ref, wpg_ref, wpp_ref,
             dx_ref, dwpg_ref, dwpp_ref, dgp_ref, dgf_ref, loss_ref):
        @pl.when(pl.program_id(0) == 0)
        def _():
            dwpg_ref[...] = jnp.zeros_like(dwpg_ref)
            dwpp_ref[...] = jnp.zeros_like(dwpp_ref)
            dgp_ref[...] = jnp.zeros_like(dgp_ref)
            dgf_ref[...] = jnp.zeros_like(dgf_ref)
            loss_ref[...] = jnp.zeros_like(loss_ref)

        x2v = x_ref[...]
        gp, gf = gp_ref[...], gf_ref[...]
        r3, n3 = _rms(x2v)
        h3b = _bf(n3 * gp)
        pb = _bf(p_ref[...])
        pg = _sigmoid(_nn(h3b, wpg_ref[...]))
        pp = _nn(pb, wpp_ref[...])
        x3 = x2v + pg * pp
        r4, n4 = _rms(x3)
        err = n4 * gf - t_ref[...]
        part = 0.5 * jnp.sum(jnp.sum(err * err, axis=1, keepdims=True), axis=0, keepdims=True) / D_MODEL
        loss_ref[...] += jnp.broadcast_to(part, (8, 128))
        dy = err * (1.0 / D_MODEL)
        dx3, dgf = _rms_bwd(dy, gf, r4, n4)
        dgf_ref[...] += jnp.broadcast_to(jnp.sum(dgf, axis=0, keepdims=True), (8, D_MODEL))
        dzp = _bf(dx3 * pp * pg * (1.0 - pg))
        dpp = _bf(dx3 * pg)
        dwpg_ref[...] += _tn(h3b, dzp)
        dwpp_ref[...] += _tn(pb, dpp)
        dh3 = _nt(dzp, wpg_ref[...])
        dx, dgp = _rms_bwd(dh3, gp, r3, n3)
        dgp_ref[...] += jnp.broadcast_to(jnp.sum(dgp, axis=0, keepdims=True), (8, D_MODEL))
        dx_ref[...] = dx3 + dx

    rowd = pl.BlockSpec((tm, D_MODEL), lambda i: (i, 0))
    fixed = lambda shp: pl.BlockSpec(shp, lambda i: (0,) * len(shp))
    return pl.pallas_call(
        body, name="tail_fwd_bwd", grid=(t // tm,),
        in_specs=[rowd, pl.BlockSpec((tm, PLE_DIM), lambda i: (i, 0)), rowd,
                  fixed((1, D_MODEL)), fixed((1, D_MODEL)), fixed((D_MODEL, D_MODEL)), fixed((PLE_DIM, D_MODEL))],
        out_specs=[rowd, fixed((D_MODEL, D_MODEL)), fixed((PLE_DIM, D_MODEL)),
                   fixed((8, D_MODEL)), fixed((8, D_MODEL)), fixed((8, 128))],
        out_shape=[jax.ShapeDtypeStruct((t, D_MODEL), F32), jax.ShapeDtypeStruct((D_MODEL, D_MODEL), F32),
                   jax.ShapeDtypeStruct((PLE_DIM, D_MODEL), F32), jax.ShapeDtypeStruct((8, D_MODEL), F32),
                   jax.ShapeDtypeStruct((8, D_MODEL), F32), jax.ShapeDtypeStruct((8, 128), F32)],
        compiler_params=_cp(("arbitrary",), VMEM_LIMIT),
    )(x2, p, target, g_ple, g_final, w_pg, w_pp)


def in_proj_bwd(pieces, weights, x, dx1, g, name="in_proj_bwd", tm=256):
    t = x.shape[0]
    tm = min(tm, t)
    k = len(pieces)
    assert all(c0 % wd == 0 and w0 % wd == 0 for (_, c0, wd), (_, w0) in zip(pieces, weights))

    def body(*refs):
        p_refs, w_refs = refs[:k], refs[k:2 * k]
        x_ref, dx1_ref, g_ref, dx_ref, dg_ref = refs[2 * k:]

        @pl.when(pl.program_id(0) == 0)
        def _():
            dg_ref[...] = jnp.zeros_like(dg_ref)

        dh = _nt(_bf(p_refs[0][...]), w_refs[0][...])
        for pr, wr in zip(p_refs[1:], w_refs[1:]):
            dh = dh + _nt(_bf(pr[...]), wr[...])
        r, n = _rms(x_ref[...])
        dx, dgc = _rms_bwd(dh, g_ref[...], r, n)
        dx_ref[...] = dx1_ref[...] + dx
        dg_ref[...] += jnp.broadcast_to(jnp.sum(dgc, axis=0, keepdims=True), (8, D_MODEL))

    rowd = pl.BlockSpec((tm, D_MODEL), lambda i: (i, 0))
    return pl.pallas_call(
        body, name=name, grid=(t // tm,),
        in_specs=[pl.BlockSpec((tm, wd), functools.partial(lambda i, cb: (i, cb), cb=c0 // wd))
                  for _, c0, wd in pieces]
        + [pl.BlockSpec((w.shape[0], wd), functools.partial(lambda i, cb: (0, cb), cb=w0 // wd))
           for (w, w0), (_, _, wd) in zip(weights, pieces)]
        + [rowd, rowd, pl.BlockSpec((1, D_MODEL), lambda i: (0, 0))],
        out_specs=[rowd, pl.BlockSpec((8, D_MODEL), lambda i: (0, 0))],
        out_shape=[jax.ShapeDtypeStruct((t, D_MODEL), F32), jax.ShapeDtypeStruct((8, D_MODEL), F32)],
        compiler_params=_cp(("arbitrary",), VMEM_LIMIT),
    )(*[a for a, _, _ in pieces], *[w for w, _ in weights], x, dx1, g)


def adamw(w, g, m, v, name, rows_cap=256, dep=None):
    lead = w.shape[:-2]
    r, c = w.shape[-2:]
    tr = r
    for cand in range(8, min(r, rows_cap) + 1, 8):
        if r % cand == 0:
            tr = cand

    def body(w_ref, g_ref, m_ref, v_ref, *rest):
        d_ref, mo_ref, vo_ref = rest[-3:]
        gv = g_ref[...]
        mn = ADAM_B1 * m_ref[...] + (1.0 - ADAM_B1) * gv
        vn = ADAM_B2 * v_ref[...] + (1.0 - ADAM_B2) * (gv * gv)
        m_hat = mn / (1.0 - ADAM_B1 ** ADAM_STEP)
        v_hat = vn / (1.0 - ADAM_B2 ** ADAM_STEP)
        d_ref[...] = -ADAM_LR * (m_hat / (jnp.sqrt(v_hat) + ADAM_EPS) + ADAM_WD * w_ref[...])
        mo_ref[...] = mn
        vo_ref[...] = vn

    spec = pl.BlockSpec((None,) * len(lead) + (tr, c), lambda i: (0,) * len(lead) + (i, 0))
    extra = [] if dep is None else [dep]
    return pl.pallas_call(
        body, name=name, grid=(r // tr,),
        in_specs=[spec] * 4 + [pl.BlockSpec((8, 128), lambda i: (0, 0))] * len(extra), out_specs=[spec] * 3,
        out_shape=[jax.ShapeDtypeStruct(w.shape, F32)] * 3,
        compiler_params=_cp(("parallel",), VMEM_LIMIT),
    )(w, g.reshape(w.shape), m, v, *extra)


def _w_in_shards(dwp):
    cs = D_IN // N_CHIPS
    regions = ((0, SPLIT_Z, P_QA), (SPLIT_Z, SPLIT_Z + 8, P_BD - SPLIT_Z), (SPLIT_Z + 8, D_IN, -(SPLIT_Z + 8)))

    def original(lo, hi):
        parts = [dwp[:, max(lo, a) + off:min(hi, e) + off] for a, e, off in regions if max(lo, a) < min(hi, e)]
        return parts[0] if len(parts) == 1 else jnp.concatenate(parts, axis=1)

    return jnp.stack([original(s * cs, (s + 1) * cs) for s in range(N_CHIPS)])


class Standalone:
    def __init__(self, later_weights):
        self.later_weights = later_weights

    def begin(self, *a):
        return 0.0

    forward = exchange = join = begin

    def finish(self, after):
        return self.later_weights


def local_step(x3d, p3d, target3d, g4, small, later, early):
    b, s, _ = x3d.shape
    t = b * s
    x = x3d.reshape(t, D_MODEL)
    p = p3d.reshape(t, PLE_DIM)
    target = target3d.reshape(t, D_MODEL)
    cut = SPLIT_Z - 2 * (D_IN // N_CHIPS)
    w_inp = jnp.concatenate([g4[2][:, cut + 8:], g4[3], g4[0], g4[1], g4[2][:, :cut], g4[2][:, cut:cut + 8],
                             jnp.zeros((D_MODEL, 120), BF16)], axis=1)
    al_row = jnp.pad(small["a_log"].reshape(1, B_HEADS), ((0, 0), (0, 128 - B_HEADS)))
    dtb_row = jnp.pad(small["dt_bias"].reshape(1, B_HEADS), ((0, 0), (0, 128 - B_HEADS)))
    conv_w8 = jnp.pad(small["conv_w"].reshape(CONV_K, CONV_CH), ((0, 8 - CONV_K), (0, 0)))
    w_on = small["w_onorm"].reshape(1, B_DIM)
    g_mix, g_ffn = small["g_mix"].reshape(1, D_MODEL), small["g_ffn"].reshape(1, D_MODEL)
    g_ple, g_final = small["g_ple"].reshape(1, D_MODEL), small["g_final"].reshape(1, D_MODEL)
    bias_band = make_bias_band(small["rel_bias"].reshape(A_HEADS, N_REL))

    tok = later.begin()
    proj, h1, bd32 = rms_matmul(x, g_mix + tok, w_inp, "in_proj", tm=1024, tn_cap=1152)
    y_a, lse = attn_fwd(proj, bias_band, b, s)
    tok = later.forward(lse)
    c = conv_fwd(proj, conv_w8 + tok, b, s)
    u, wk, qg, kdec, pm, egl, tmat = dn_prep(c, bd32, al_row, dtb_row, b, s)
    o_b, states = dn_scan_fwd(u, wk, qg, kdec, pm, egl, b, s)
    wts = later.finish(o_b)
    x1, merged = mid_fwd(x, y_a, o_b, proj, w_on, wts["w_branch_a"], wts["w_branch_b"], wts["w_out"])
    gate, up, act, h2 = ffn_up(x1, g_ffn, wts["w_gate_up"])
    x2 = matmul_residual(act, wts["w_down"], x1, "ffn_down")

    dx2, dw_pg, dw_pp, dg_ple, dg_final, loss = tail_fwd_bwd(
        x2, p, target, g_ple, g_final, wts["w_ple_gate"], wts["w_ple_proj"])
    dgate, dup = ffn_act_bwd(dx2, gate, up, wts["w_down"])
    w_gu = wts["w_gate_up"]
    dx1, dg_ffn = in_proj_bwd([(dgate, 0, D_FF), (dup, 0, D_FF)], [(w_gu, 0), (w_gu, D_FF)], x1, dx2, g_ffn,
                              name="ffn_in_bwd")
    dw_down = matmul_tn(act, dx2, "dw_down")
    dw_gu = matmul_tn(h2, dgate, "dw_gate", width=2 * D_FF, tiles_major=True)
    dw_gu = matmul_tn(h2, dup, "dw_up", into=dw_gu, col0=D_FF, width=2 * D_FF, tiles_major=True)
    dy_a, do_b, dz, dgates, dw_out, dwa, dwb, dw_on = mid_bwd(
        dx1, merged, y_a, o_b, proj, w_on, wts["w_branch_a"], wts["w_branch_b"], wts["w_out"])
    tok = early.begin(dict(w_branch_a=dwa, w_branch_b=dwb, w_out=dw_out, w_gate_up=dw_gu, w_down=dw_down,
                           w_ple_gate=dw_pg, w_ple_proj=dw_pp))
    ddw, ddwk, ddqg, ddkdec, ddp, ddegl = dn_scan_bwd(u, wk, qg, kdec, pm, egl + tok, states, do_b, b, s)
    tok = early.exchange(ddegl)
    dc3, dbd, dal, ddtb = dn_post_bwd(c, bd32, al_row + tok, dtb_row, tmat, ddw, ddwk, ddqg, ddkdec, ddp, ddegl, b, s)
    dconv, dconv_w = conv_bwd(proj, conv_w8, dc3, b, s)
    dqa, dka, dva, dbt, dbf = attn_bwd(proj, bias_band, y_a, lse, dy_a, b, s)
    tok = early.join(dqa)

    pieces = [dgates, dqa, dka, dva, dconv, dz, dbd]
    bounds = [0, 2048, 2560, 3072, 3584, 5120, 5632, 5760]
    windows = [(dgates, 0, 2048), (dqa, 0, 512), (dka, 0, 512), (dva, 0, 512), (dconv, 0, 512), (dconv, 512, 512),
               (dconv, 1024, 512), (dz, 0, 512), (dbd, 0, 128)]
    w_cols = [0, P_QA, P_KA, P_VA, P_CONV, P_CONV + 512, P_CONV + 1024, P_Z, P_BD]
    dx, dg_mix = in_proj_bwd(windows, [(w_inp, c0) for c0 in w_cols], x, dx1, g_mix + tok)
    dwp = None
    for k, pc in enumerate(pieces):
        dwp = matmul_tn(h1, pc, "dw_in_%d" % k, into=dwp, col0=bounds[k], width=P_WIDTH)
    reduced_early = early.finish(dwp)
    dw_in = _w_in_shards(dwp)

    grads = dict(w_in=dw_in, w_branch_a=dwa, w_branch_b=dwb, w_out=dw_out, w_gate_up=dw_gu, w_down=dw_down,
                 w_ple_gate=dw_pg, w_ple_proj=dw_pp)
    small_grads = dict(g_mix=dg_mix[0], g_ffn=dg_ffn[0], g_ple=dg_ple[0], g_final=dg_final[0],
                       conv_w=dconv_w[:CONV_K].reshape(-1), rel_bias_parts=(dbt, dbf), w_onorm=dw_on[0],
                       a_log=dal[0, :B_HEADS], dt_bias=ddtb[0, :B_HEADS], loss=loss[0, :1])
    return dx.reshape(b, s, D_MODEL), grads, small_grads, reduced_early


BIG = (("w_in", (D_MODEL, D_IN), 1), ("w_branch_a", (A_WIDTH, D_MODEL), 1), ("w_branch_b", (B_WIDTH, D_MODEL), 1),
       ("w_out", (D_MODEL, D_MODEL), 0), ("w_gate_up", (D_MODEL, 2 * D_FF), 1), ("w_down", (D_FF, D_MODEL), 0),
       ("w_ple_gate", (D_MODEL, D_MODEL), 0), ("w_ple_proj", (PLE_DIM, D_MODEL), 1))
N_CHIPS = 4
FIRST_WEIGHTS = ("w_in",)
LATER_WEIGHTS = ("w_branch_a", "w_branch_b", "w_out", "w_gate_up", "w_down", "w_ple_gate", "w_ple_proj")
LATE_GRADS = ("w_in",)
EARLY_GRADS = ("w_branch_a", "w_branch_b", "w_out", "w_gate_up", "w_down", "w_ple_gate", "w_ple_proj")


def _items(names):
    return [it for it in BIG if it[0] in names]


def _shard_shape(shape, axis):
    return (shape[0] // N_CHIPS, shape[1]) if axis == 0 else (shape[0], shape[1] // N_CHIPS)


def _width_groups(names):
    groups = {}
    for n, shape, axis in _items(names):
        rs, cs = _shard_shape(shape, axis)
        groups.setdefault(cs, []).append((n, rs))
    return sorted(groups.items())


def grad_buffers(grads, names):
    info = {n: (shape, axis) for n, shape, axis in _items(names)}
    bufs = []
    for cs, members in _width_groups(names):
        segs = []
        for n, rs in members:
            g = grads[n].astype(BF16)
            if g.ndim == 2:
                g = (g.reshape(N_CHIPS, rs, cs) if info[n][1] == 0
                     else jnp.transpose(g.reshape(rs, N_CHIPS, cs), (1, 0, 2)))
            segs.append(g)
        bufs.append(segs[0] if len(segs) == 1 else jnp.concatenate(segs, axis=1))
    return bufs


def split_buffers(reduced, names):
    out = {}
    for (cs, members), buf in zip(_width_groups(names), reduced):
        r0 = 0
        for n, rs in members:
            out[n] = buf[r0:r0 + rs]
            r0 += rs
    return out


def _place():
    return lax.axis_index("x"), lax.axis_index("y"), lax.axis_index("c")


ANY = pl.BlockSpec(memory_space=pl.ANY)


def _gathered_shape(item):
    n, shape, _ = item
    return (N_CHIPS,) + _shard_shape(shape, 1) if n == "w_in" else shape


def _gather_block(o_ref, item, cx, cy, hf):
    n, shape, axis = item
    rs, cs = _shard_shape(shape, axis)
    hr = rs // 2
    ci = 2 * cx + cy
    if n == "w_in":
        return o_ref.at[ci, pl.ds(pl.multiple_of(hf * hr, 16), hr), :]
    if axis == 0:
        return o_ref.at[pl.ds(pl.multiple_of(ci * rs + hf * hr, 16), hr), :]
    return o_ref.at[pl.ds(pl.multiple_of(hf * hr, 16), hr), pl.ds(pl.multiple_of(ci * cs, 128), cs)]


def _own_half(w_ref, item, c):
    hr = _shard_shape(item[1], item[2])[0] // 2
    return w_ref.at[pl.ds(pl.multiple_of(c * hr, 16), hr), :]


def _gather_slot(o_ref, item, cx, cy):
    n, shape, axis = item
    rs, cs = _shard_shape(shape, axis)
    ci = 2 * cx + cy
    if n == "w_in":
        return o_ref.at[ci]
    if axis == 0:
        return o_ref.at[pl.ds(pl.multiple_of(ci * rs, 16), rs), :]
    return o_ref.at[:, pl.ds(pl.multiple_of(ci * cs, 128), cs)]


def _other_chips(x, y):
    return [(1 - x, y), (x, 1 - y), (1 - x, 1 - y)]


def allgather_weights(shards, names):
    items = _items(names)
    nw = len(items)

    def body(*refs):
        w_refs, o_refs = refs[:nw], refs[nw:2 * nw]
        send_sems, recv_sems = refs[2 * nw:]
        x, y, c = _place()
        sibling = (x, y, 1 - c)
        chips = _other_chips(x, y)

        def copy(k, src, dst, to):
            return pltpu.make_async_remote_copy(src_ref=src, dst_ref=dst, send_sem=send_sems.at[k],
                                                recv_sem=recv_sems.at[k], device_id=to, device_id_type=MESH)

        def blk(i, cx, cy, hf):
            return _gather_block(o_refs[i], items[i], cx, cy, hf)

        def my_half(i):
            return _own_half(w_refs[i], items[i], c)

        def own(i):
            return _gather_slot(o_refs[i], items[i], x, y)

        first = [copy(7 * i + j, my_half(i), blk(i, x, y, c), (*chip_, c))
                 for i in range(nw) for j, chip_ in enumerate(chips)]
        first += [copy(7 * i + 6, w_refs[i], own(i), sibling) for i in range(nw)]
        for cp in first:
            cp.start()
        passed = []
        for i in range(nw):
            for j, chip_ in enumerate(chips):
                copy(7 * i + j, my_half(i), blk(i, *chip_, c), (*chip_, c)).wait_recv()
                fwd = copy(7 * i + 3 + j, blk(i, *chip_, c), blk(i, *chip_, c), sibling)
                fwd.start()
                passed.append(fwd)
        for i in range(nw):
            for j, chip_ in enumerate(chips):
                copy(7 * i + 3 + j, my_half(i), blk(i, *chip_, 1 - c), sibling).wait_recv()
            copy(7 * i + 6, w_refs[i], own(i), sibling).wait_recv()
        for cp in first + passed:
            cp.wait_send()

    outs = pl.pallas_call(
        body, name="allgather_weights",
        in_specs=[ANY] * nw, out_specs=[ANY] * nw,
        out_shape=[jax.ShapeDtypeStruct(_gathered_shape(it), BF16) for it in items],
        scratch_shapes=[pltpu.SemaphoreType.DMA((7 * nw,)), pltpu.SemaphoreType.DMA((7 * nw,))],
    )(*[shards[it[0]] for it in items])
    return {it[0]: o for it, o in zip(items, outs)}


HBM_SPEC = pl.BlockSpec(memory_space=pltpu.HBM)
SEM_SPEC = pl.BlockSpec(memory_space=pltpu.SEMAPHORE)
EFFECT = pltpu.SideEffectType.DATAFLOW_SIDE_EFFECTING


def _in_hbm(a):
    return pltpu.with_memory_space_constraint(a, pltpu.HBM)


def copies_start(name, bufs, ncopies, plan):
    nb = len(bufs)

    def body(*refs):
        in_refs, send_sems, recv_sems, token = refs[:nb], refs[nb], refs[nb + 1], refs[-1]
        for k, (src, dst, to) in enumerate(plan(in_refs)):
            pltpu.make_async_remote_copy(src_ref=src, dst_ref=dst, send_sem=send_sems.at[k],
                                         recv_sem=recv_sems.at[k], device_id=to, device_id_type=MESH).start()
        token[...] = jnp.zeros_like(token)

    outs = pl.pallas_call(
        body, name=name,
        in_specs=[HBM_SPEC] * nb,
        out_specs=(SEM_SPEC, SEM_SPEC, *[HBM_SPEC] * nb, pl.BlockSpec(memory_space=pltpu.VMEM)),
        out_shape=(pltpu.SemaphoreType.DMA((ncopies,)), pltpu.SemaphoreType.DMA((ncopies,)),
                   *[pltpu.HBM(b.shape, b.dtype) for b in bufs], jax.ShapeDtypeStruct((8, 128), F32)),
        input_output_aliases={i: 2 + i for i in range(nb)},
        compiler_params=pltpu.CompilerParams(has_side_effects=EFFECT),
    )(*[_in_hbm(b) for b in bufs])
    return outs[0], outs[1], list(outs[2:2 + nb]), outs[-1][0, 0]


def copies_wait(name, send_sems, recv_sems, bufs, after, plan):
    nb = len(bufs)

    def body(*refs):
        in_refs, s_sems, r_sems = refs[:nb], refs[nb], refs[nb + 1]
        for k, (src, dst, to) in enumerate(plan(in_refs)):
            cp = pltpu.make_async_remote_copy(src_ref=src, dst_ref=dst, send_sem=s_sems.at[k],
                                              recv_sem=r_sems.at[k], device_id=to, device_id_type=MESH)
            cp.wait_send()
            cp.wait_recv()

    return list(pl.pallas_call(
        body, name=name,
        in_specs=[HBM_SPEC] * nb + [SEM_SPEC, SEM_SPEC, ANY],
        out_specs=tuple([HBM_SPEC] * nb),
        out_shape=tuple(pltpu.HBM(b.shape, b.dtype) for b in bufs),
        input_output_aliases={i: i for i in range(nb)},
        compiler_params=pltpu.CompilerParams(has_side_effects=EFFECT),
    )(*bufs, send_sems, recv_sems, after))


def _landing(shape, dtype):
    return _in_hbm(lax.empty(shape, dtype))


class LaterWeights:
    def __init__(self, shards):
        self.items = _items(LATER_WEIGHTS)
        self.shards = shards
        self.nw = len(self.items)

    def _ici_plan(self, refs):
        x, y, c = _place()
        w_refs, o_refs = refs[:self.nw], refs[self.nw:]
        plan = [(_own_half(w_refs[i], it, c), _gather_block(o_refs[i], it, x, y, c), (*chip_, c))
                for i, it in enumerate(self.items) for chip_ in _other_chips(x, y)]
        return plan + [(w_refs[i], _gather_slot(o_refs[i], it, x, y), (x, y, 1 - c))
                       for i, it in enumerate(self.items)]

    def _d2d_plan(self, refs):
        x, y, c = _place()
        return [(_gather_block(refs[i], it, *chip_, c), _gather_block(refs[i], it, *chip_, c), (x, y, 1 - c))
                for i, it in enumerate(self.items) for chip_ in _other_chips(x, y)]

    def _d2d_wait_plan(self, refs):
        x, y, c = _place()
        return [(_gather_block(refs[i], it, *chip_, c), _gather_block(refs[i], it, *chip_, 1 - c), (x, y, 1 - c))
                for i, it in enumerate(self.items) for chip_ in _other_chips(x, y)]

    def _ici_wait_plan(self, refs):
        x, y, c = _place()
        w_refs, o_refs = refs[:self.nw], refs[self.nw:]
        plan = [(_own_half(w_refs[i], it, c), _gather_block(o_refs[i], it, *chip_, c), (*chip_, c))
                for i, it in enumerate(self.items) for chip_ in _other_chips(x, y)]
        return plan + [(w_refs[i], _gather_slot(o_refs[i], it, x, y), (x, y, 1 - c))
                       for i, it in enumerate(self.items)]

    def begin(self):
        srcs = [self.shards[it[0]] for it in self.items]
        lands = [_landing(_gathered_shape(it), BF16) for it in self.items]
        self.s1, self.r1, self.b1, tok = copies_start("gather_ici_start", srcs + lands, 4 * self.nw, self._ici_plan)
        return tok

    def forward(self, after):
        b1 = copies_wait("gather_ici_wait", self.s1, self.r1, self.b1, after, self._ici_wait_plan)
        self.s2, self.r2, self.b2, tok = copies_start("gather_d2d_start", b1[self.nw:], 3 * self.nw, self._d2d_plan)
        return tok

    def finish(self, after):
        outs = copies_wait("gather_d2d_wait", self.s2, self.r2, self.b2, after, self._d2d_wait_plan)
        return {it[0]: o for it, o in zip(self.items, outs)}


def small_allreduce(v, name):
    r = v.shape[0]

    def body(v_ref, o_ref, buf, send_sems, recv_sems):
        x, y, c = _place()
        me = 4 * x + 2 * y + c
        buf[me] = v_ref[...]
        flips = [(fx, fy, fc) for fx in (0, 1) for fy in (0, 1) for fc in (0, 1)][1:]
        peers = [((1 - x) if fx else x, (1 - y) if fy else y, (1 - c) if fc else c) for fx, fy, fc in flips]

        def copy(k, slot, to):
            return pltpu.make_async_remote_copy(src_ref=v_ref, dst_ref=buf.at[slot], send_sem=send_sems.at[k],
                                                recv_sem=recv_sems.at[k], device_id=to, device_id_type=MESH)

        sends = [copy(k, me, peer) for k, peer in enumerate(peers)]
        for cp in sends:
            cp.start()
        for k, (px, py, pc) in enumerate(peers):
            copy(k, 4 * px + 2 * py + pc, (px, py, pc)).wait_recv()
        for cp in sends:
            cp.wait_send()
        acc = buf[0]
        for d in range(1, 8):
            acc = acc + buf[d]
        o_ref[...] = acc

    return pl.pallas_call(
        body, name=name,
        in_specs=[pl.BlockSpec(memory_space=pltpu.VMEM)], out_specs=pl.BlockSpec(memory_space=pltpu.VMEM),
        out_shape=jax.ShapeDtypeStruct((r, 128), F32),
        scratch_shapes=[pltpu.VMEM((8, r, 128), F32), pltpu.SemaphoreType.DMA((7,)), pltpu.SemaphoreType.DMA((7,))],
    )(v)


def add_halves(g, other, place):
    half, wd = other.shape[1:]
    tr = _tile_rows(half, wd)
    nblk = half // tr

    def body(pref, g0, g1, g2, g3, o0, o1, o2, o3, pf_ref, pb_ref):
        f = lambda r: r[...].astype(F32)
        pf_ref[...] = f(g0) + f(o0)
        pb_ref[0] = _bf(f(g1) + f(o1))
        pb_ref[1] = _bf(f(g2) + f(o2))
        pb_ref[2] = _bf(f(g3) + f(o3))

    gspec = lambda k: pl.BlockSpec((None, tr, wd), lambda i, pr: ((pr[0] + k) % N_CHIPS, pr[1] * nblk + i, 0))
    ospec = lambda k: pl.BlockSpec((None, tr, wd), lambda i, pr: ((pr[0] + k) % N_CHIPS, i, 0))
    return pl.pallas_call(
        body, name="add_halves",
        grid_spec=pltpu.PrefetchScalarGridSpec(
            num_scalar_prefetch=1, grid=(nblk,),
            in_specs=[gspec(0), gspec(1), gspec(2), gspec(3), ospec(0), ospec(1), ospec(2), ospec(3)],
            out_specs=[pl.BlockSpec((tr, wd), lambda i, pr: (i, 0)),
                       pl.BlockSpec((3, tr, wd), lambda i, pr: (0, i, 0))]),
        out_shape=[jax.ShapeDtypeStruct((half, wd), F32), jax.ShapeDtypeStruct((3, half, wd), BF16)],
        compiler_params=_cp(("parallel",), VMEM_LIMIT),
    )(place, g, g, g, g, other, other, other, other)


def _tile_rows(n, width):
    best = 16
    for t in range(16, max(16, (384 * 1024) // width) + 1, 16):
        if n % t == 0:
            best = t
    assert n % best == 0
    return best


def add_partials(pf, got, place):
    half, wd = pf.shape
    tr = _tile_rows(half, wd)

    def body(pref, pf_ref, got_ref, o_ref):
        o_ref[...] = ((pf_ref[...] + got_ref[0].astype(F32)) + got_ref[1].astype(F32)) + got_ref[2].astype(F32)

    return pl.pallas_call(
        body, name="add_partials",
        grid_spec=pltpu.PrefetchScalarGridSpec(
            num_scalar_prefetch=1, grid=(half // tr,),
            in_specs=[pl.BlockSpec((tr, wd), lambda i, pr: (i, 0)),
                      pl.BlockSpec((3, tr, wd), lambda i, pr: (0, i, 0))],
            out_specs=pl.BlockSpec((None, tr, wd), lambda i, pr: (pr[1], i, 0))),
        out_shape=jax.ShapeDtypeStruct((2, half, wd), F32),
        compiler_params=_cp(("parallel",), VMEM_LIMIT),
    )(place, pf, got)


class GradReduce:
    def __init__(self, place, names, tag):
        self.place, self.names, self.tag = place, names, tag
        self.nb = len(_width_groups(names))

    def _swap_plan(self, refs):
        x, y, c = _place()
        plan = []
        for g_ref, o_ref in zip(refs[:self.nb], refs[self.nb:]):
            half = o_ref.shape[1]
            plan.append((g_ref.at[:, pl.ds(pl.multiple_of((1 - c) * half, 16), half), :], o_ref, (x, y, 1 - c)))
        return plan

    def _exchange_plan(self, refs):
        x, y, c = _place()
        me = 2 * x + y
        return [(p_ref.at[k - 1], o_ref.at[k - 1], (((me + k) % N_CHIPS) // 2, ((me + k) % N_CHIPS) % 2, c))
                for p_ref, o_ref in zip(refs[:self.nb], refs[self.nb:]) for k in range(1, N_CHIPS)]

    def _join_plan(self, refs):
        x, y, c = _place()
        return [(r.at[c], r.at[c], (x, y, 1 - c)) for r in refs]

    def _join_wait_plan(self, refs):
        x, y, c = _place()
        return [(r.at[c], r.at[1 - c], (x, y, 1 - c)) for r in refs]

    def begin(self, grads):
        gs = grad_buffers(grads, self.names)
        lands = [_landing((N_CHIPS, g.shape[1] // 2, g.shape[2]), BF16) for g in gs]
        self.s1, self.r1, self.b1, tok = copies_start(self.tag + "_swap_start", gs + lands, self.nb, self._swap_plan)
        return tok

    def exchange(self, after):
        b1 = copies_wait(self.tag + "_swap_wait", self.s1, self.r1, self.b1, after, self._swap_plan)
        sums = [add_halves(g, other, self.place) for g, other in zip(b1[:self.nb], b1[self.nb:])]
        self.pfs = [pf for pf, _ in sums]
        pbs = [pb for _, pb in sums]
        lands = [_landing(pb.shape, BF16) for pb in pbs]
        self.s2, self.r2, self.b2, tok = copies_start(self.tag + "_exchange_start", pbs + lands, 3 * self.nb,
                                                      self._exchange_plan)
        return tok

    def join(self, after):
        b2 = copies_wait(self.tag + "_exchange_wait", self.s2, self.r2, self.b2, after, self._exchange_plan)
        boths = [add_partials(pf, got, self.place) for pf, got in zip(self.pfs, b2[self.nb:])]
        self.s3, self.r3, self.b3, tok = copies_start(self.tag + "_join_start", boths, self.nb, self._join_plan)
        return tok

    def finish(self, after):
        boths = copies_wait(self.tag + "_join_wait", self.s3, self.r3, self.b3, after, self._join_wait_plan)
        return split_buffers([b.reshape(-1, b.shape[2]) for b in boths], self.names)


SMALL = (("g_mix", D_MODEL), ("g_ffn", D_MODEL), ("g_ple", D_MODEL), ("g_final", D_MODEL),
         ("conv_w", CONV_K * CONV_CH), ("rel_bias", A_HEADS * N_REL), ("w_onorm", B_DIM),
         ("a_log", B_HEADS), ("dt_bias", B_HEADS), ("loss", 1))


def _pad128(v):
    v = v.reshape(-1)
    return jnp.pad(v, (0, -v.shape[0] % 128))


def pack_small(d, names, rows):
    flat = jnp.concatenate([_pad128(d[n]) for n in names]).reshape(-1, 128)
    return jnp.pad(flat, ((0, rows - flat.shape[0]), (0, 0)))


def unpack_small(flat, names_sizes):
    out, r0 = {}, 0
    v = flat.reshape(-1)
    for n, size in names_sizes:
        out[n] = v[r0:r0 + size]
        r0 += -(-size // 128) * 128
    return out


def kernel(x, p, g_mix, w_in, conv_w, a_log, dt_bias, rel_bias, w_onorm, w_branch_a, w_branch_b, w_out, g_ffn, w_gate_up, w_down, g_ple, w_ple_gate, w_ple_proj, g_final, loss_target, m_g_mix, m_w_in, m_conv_w, m_a_log, m_dt_bias, m_rel_bias, m_w_onorm, m_w_branch_a, m_w_branch_b, m_w_out, m_g_ffn, m_w_gate_up, m_w_down, m_g_ple, m_w_ple_gate, m_w_ple_proj, m_g_final, v_g_mix, v_w_in, v_conv_w, v_a_log, v_dt_bias, v_rel_bias, v_w_onorm, v_w_branch_a, v_w_branch_b, v_w_out, v_g_ffn, v_w_gate_up, v_w_down, v_g_ple, v_w_ple_gate, v_w_ple_proj, v_g_final):
    names = ["g_mix", "w_in", "conv_w", "a_log", "dt_bias", "rel_bias", "w_onorm", "w_branch_a", "w_branch_b",
             "w_out", "g_ffn", "w_gate_up", "w_down", "g_ple", "w_ple_gate", "w_ple_proj", "g_final"]
    w = dict(zip(names, [g_mix, w_in, conv_w, a_log, dt_bias, rel_bias, w_onorm, w_branch_a, w_branch_b, w_out,
                         g_ffn, w_gate_up, w_down, g_ple, w_ple_gate, w_ple_proj, g_final]))
    m = dict(zip(names, [m_g_mix, m_w_in, m_conv_w, m_a_log, m_dt_bias, m_rel_bias, m_w_onorm, m_w_branch_a,
                         m_w_branch_b, m_w_out, m_g_ffn, m_w_gate_up, m_w_down, m_g_ple, m_w_ple_gate,
                         m_w_ple_proj, m_g_final]))
    v = dict(zip(names, [v_g_mix, v_w_in, v_conv_w, v_a_log, v_dt_bias, v_rel_bias, v_w_onorm, v_w_branch_a,
                         v_w_branch_b, v_w_out, v_g_ffn, v_w_gate_up, v_w_down, v_g_ple, v_w_ple_gate,
                         v_w_ple_proj, v_g_final]))
    xi, yi, ci = _place()
    chip = 2 * xi + yi
    big_names = [n for n, _, _ in BIG]

    shards2d = {n: w[n].reshape(w[n].shape[-2:]) for n in big_names}
    shards_bf = {n: a.astype(BF16) for n, a in shards2d.items()}
    g4 = allgather_weights(shards_bf, FIRST_WEIGHTS)["w_in"]
    place = jnp.stack([chip, ci]).astype(jnp.int32)
    conv_sh = jnp.where(ci == 0, w["conv_w"].reshape(CONV_K, CONV_CH // N_CHIPS), 0.0)
    conv_slots = lax.dynamic_update_slice(jnp.zeros((N_CHIPS, CONV_K, CONV_CH // N_CHIPS), F32), conv_sh[None],
                                          (chip, 0, 0))
    conv_all = small_allreduce(conv_slots.reshape(-1, 128), "gather_conv_w")
    conv_full = jnp.transpose(conv_all.reshape(N_CHIPS, CONV_K, CONV_CH // N_CHIPS), (1, 0, 2)).reshape(CONV_K, CONV_CH)
    small = {n: w[n] for n in names if n not in big_names}
    small["conv_w"] = conv_full

    grad_x, grads, small_grads, reduced_early = local_step(
        x, p[0], loss_target, g4, small, LaterWeights(shards_bf), GradReduce(place, EARLY_GRADS, "grads"))

    late = GradReduce(place, LATE_GRADS, "late")
    tok = late.begin(grads)
    small_names = [n for n, _ in SMALL]
    dbt, dbf = small_grads.pop("rel_bias_parts")
    small_grads["rel_bias"] = bias_band_grad(dbt + tok, dbf).reshape(-1)
    tok = late.exchange(small_grads["rel_bias"])
    small_grads["loss"] = small_grads["loss"] + tok
    red_flat = small_allreduce(pack_small(small_grads, small_names, 112), "allreduce_small")
    red = unpack_small(red_flat, SMALL)
    dep, red_flat = lax.optimization_barrier((jnp.full((8, 128), tok, F32), red_flat))
    gshard = dict(reduced_early)
    loss = red["loss"][0]
    conv_g = lax.dynamic_slice(red["conv_w"].reshape(CONV_K, N_CHIPS, CONV_CH // N_CHIPS), (0, chip, 0),
                               (CONV_K, 1, CONV_CH // N_CHIPS))
    gsmall = {n: red[n].reshape(w[n].shape) for n in small_names if n not in ("loss", "conv_w")}
    gsmall["conv_w"] = conv_g.reshape(w["conv_w"].shape)

    grad, delta, new_m, new_v = {}, {}, {}, {}
    for n in list(EARLY_GRADS) + list(LATE_GRADS):
        if n in LATE_GRADS:
            late.join(v_)
            gshard.update(late.finish(v_))
        shp = w[n].shape
        d_, m_, v_ = adamw(shards2d[n], gshard[n], m[n].reshape(shp[-2:]), v[n].reshape(shp[-2:]), "adamw_" + n,
                           dep=dep if n in EARLY_GRADS else None)
        dep, v_ = lax.optimization_barrier((dep, v_))
        grad[n], delta[n], new_m[n], new_v[n] = gshard[n].reshape(shp), d_.reshape(shp), m_.reshape(shp), v_.reshape(shp)
    snames = [n for n in small_names if n != "loss"]
    ssizes = [(n, w[n].size) for n in snames]
    pk = lambda d: pack_small(d, snames, 64)
    d_, m_, v_ = adamw(pk(w), pk(gsmall), pk(m), pk(v), "adamw_small")
    ds, ms, vs = unpack_small(d_, ssizes), unpack_small(m_, ssizes), unpack_small(v_, ssizes)
    for n in snames:
        shp = w[n].shape
        grad[n], delta[n], new_m[n], new_v[n] = gsmall[n], ds[n].reshape(shp), ms[n].reshape(shp), vs[n].reshape(shp)

    return (loss, grad_x, *[grad[n] for n in names], *[delta[n] for n in names],
            *[new_m[n] for n in names], *[new_v[n] for n in names])
```

```python
import functools

import jax
import jax.numpy as jnp
from jax import lax
from jax.experimental import pallas as pl
from jax.experimental.pallas import tpu as pltpu

F32 = jnp.float32
BF16 = jnp.bfloat16
MESH = pl.DeviceIdType.MESH

D_MODEL = 1024
CHUNK = 64
PLE_DIM = 256
EPS = 1e-6
A_HEADS = 8
A_HEAD_DIM = 64
A_WIDTH = 512
A_LOOKBACK = 8
BAND = (A_LOOKBACK + 1) * CHUNK
TAIL = 3 * CHUNK
REL_CLIP = 128
N_REL = 2 * REL_CLIP + 1
B_HEADS = 4
B_DIM = 128
B_WIDTH = 512
CONV_K = 4
CONV_CH = 1536
D_FF = 2816
SPLIT_Z = 3584
D_IN = 5640
ADAM_LR, ADAM_B1, ADAM_B2, ADAM_EPS, ADAM_WD, ADAM_STEP = 0.001, 0.9, 0.999, 1e-08, 0.01, 10

P_GATES, P_QA, P_KA, P_VA, P_CONV, P_Z, P_BD, P_WIDTH = 0, 2048, 2560, 3072, 3584, 5120, 5632, 5760

VMEM_LIMIT = 56 * 1024 * 1024


def _cp(sem, vmem=None, **kw):
    return pltpu.CompilerParams(dimension_semantics=sem, vmem_limit_bytes=vmem, **kw)


def _tile(n, cap):
    best = None
    for t in range(128, cap + 1, 128):
        if n % t == 0:
            best = t
    assert best is not None, (n, cap)
    return best


def _nn(a, b, prec=None):
    return lax.dot_general(a, b, (((1,), (0,)), ((), ())), preferred_element_type=F32, precision=prec)


def _nt(a, b, prec=None):
    return lax.dot_general(a, b, (((1,), (1,)), ((), ())), preferred_element_type=F32, precision=prec)


def _tn(a, b, prec=None):
    return lax.dot_general(a, b, (((0,), (0,)), ((), ())), preferred_element_type=F32, precision=prec)


def _bnn(a, b, prec=None):
    return lax.dot_general(a, b, (((2,), (1,)), ((0,), (0,))), preferred_element_type=F32, precision=prec)


def _bnt(a, b, prec=None):
    return lax.dot_general(a, b, (((2,), (2,)), ((0,), (0,))), preferred_element_type=F32, precision=prec)


def _bf(a):
    return a.astype(BF16)


def _split(a):
    hi = a.astype(BF16)
    return hi, (a - hi.astype(F32)).astype(BF16)


def _split3(a):
    h1 = _bf(a)
    r1 = a - h1.astype(F32)
    h2 = _bf(r1)
    return h1, h2, _bf(r1 - h2.astype(F32))


def _bnn_exact(lhs_b, rhs):
    h1, h2, h3 = _split3(rhs)
    return _bnn(lhs_b, h1) + (_bnn(lhs_b, h2) + _bnn(lhs_b, h3))


def _bnn3(a, b):
    ah, al = a if isinstance(a, tuple) else _split(a)
    bh, bl = b if isinstance(b, tuple) else _split(b)
    return _bnn(ah, bh) + (_bnn(ah, bl) + _bnn(al, bh))


def _sigmoid(x):
    return 0.5 * jnp.tanh(0.5 * x) + 0.5


def _softplus(x):
    return jnp.maximum(x, 0.0) + jnp.log(1.0 + jnp.exp(-jnp.abs(x)))


def rms_matmul(x, g, w, name, tm=512, tn_cap=1024):
    t, d = x.shape
    n = w.shape[1]
    tm = min(tm, t)
    tn = _tile(n, tn_cap)

    nj = n // tn

    def body(x_ref, g_ref, w_ref, o_ref, h_ref, tail_ref):
        @pl.when(pl.program_id(1) == 0)
        def _():
            xv = x_ref[...]
            r = lax.rsqrt(jnp.mean(xv * xv, axis=-1, keepdims=True) + EPS)
            h_ref[...] = _bf(xv * r * g_ref[...])

        res = _nn(h_ref[...], w_ref[...])
        o_ref[...] = _bf(res)

        @pl.when(pl.program_id(1) == nj - 1)
        def _():
            tail_ref[...] = res[:, tn - 128:]

    return pl.pallas_call(
        body, name=name, grid=(t // tm, nj),
        in_specs=[pl.BlockSpec((tm, d), lambda i, j: (i, 0)),
                  pl.BlockSpec((1, d), lambda i, j: (0, 0)),
                  pl.BlockSpec((d, tn), lambda i, j: (0, j))],
        out_specs=[pl.BlockSpec((tm, tn), lambda i, j: (i, j)),
                   pl.BlockSpec((tm, d), lambda i, j: (i, 0)),
                   pl.BlockSpec((tm, 128), lambda i, j: (i, 0))],
        out_shape=[jax.ShapeDtypeStruct((t, n), BF16), jax.ShapeDtypeStruct((t, d), BF16),
                   jax.ShapeDtypeStruct((t, 128), F32)],
        compiler_params=_cp(("parallel", "arbitrary"), VMEM_LIMIT),
    )(x, g, w)


def matmul_tn(a, b, name, into=None, col0=0, width=None, tm=1024, tk_cap=1408, tn_cap=1408, tiles_major=False):
    m, k1 = a.shape
    n = b.shape[1]
    tm = min(tm, m)
    tk = _tile(k1, tk_cap)
    tn = _tile(n, tn_cap)
    while col0 % tn:
        tn = _tile(n, tn - 128)
    nk = m // tm
    c0 = col0 // tn

    def body(*refs):
        a_ref, b_ref, o_ref, acc = refs[0], refs[1], refs[-2], refs[-1]

        @pl.when(pl.program_id(2) == 0)
        def _():
            acc[...] = jnp.zeros_like(acc)

        acc[...] += _tn(_bf(a_ref[...]), _bf(b_ref[...]))

        @pl.when(pl.program_id(2) == nk - 1)
        def _():
            o_ref[...] = _bf(acc[...])

    in_specs = [pl.BlockSpec((tm, tk), lambda i, j, k: (k, i)),
                pl.BlockSpec((tm, tn), lambda i, j, k: (k, j))]
    args = [a, b]
    total = n if width is None else width
    aliases = {}
    if into is not None:
        in_specs.append(ANY)
        args.append(into)
        aliases = {2: 0}
    if tiles_major:
        out_spec = pl.BlockSpec((None, tk, tn), lambda i, j, k: (c0 + j, i, 0))
        out_shape = jax.ShapeDtypeStruct((total // tn, k1, tn), BF16)
    else:
        out_spec = pl.BlockSpec((tk, tn), lambda i, j, k: (i, c0 + j))
        out_shape = jax.ShapeDtypeStruct((k1, total), BF16)
    return pl.pallas_call(
        body, name=name, grid=(k1 // tk, n // tn, nk),
        in_specs=in_specs,
        out_specs=out_spec,
        out_shape=out_shape,
        scratch_shapes=[pltpu.VMEM((tk, tn), F32)],
        input_output_aliases=aliases,
        compiler_params=_cp(("parallel", "parallel", "arbitrary"), VMEM_LIMIT),
    )(*args)


def _tail_onehot(qi):
    r = lax.broadcasted_iota(jnp.int32, (384, TAIL), 0)
    kj = lax.broadcasted_iota(jnp.int32, (384, TAIL), 1)
    return (r == jnp.minimum(REL_CLIP + qi - kj, REL_CLIP) + REL_CLIP).astype(F32)


def bias_tail(rel_pad):
    def body(rb_ref, o_ref):
        parts = _split3(rb_ref[...])
        for qi in range(CHUNK):
            oh = _bf(_tail_onehot(qi))
            o_ref[qi] = _nn(parts[0], oh) + (_nn(parts[1], oh) + _nn(parts[2], oh))

    return pl.pallas_call(
        body, name="bias_tail",
        out_shape=jax.ShapeDtypeStruct((CHUNK, A_HEADS, TAIL), F32),
    )(rel_pad)


def bias_grad(db_t, db_far):
    def body(t_ref, f_ref, o_ref):
        acc = jnp.zeros((A_HEADS, 384), F32)
        for qi in range(CHUNK):
            oh = _bf(_tail_onehot(qi))
            parts = _split3(t_ref[qi])
            acc = acc + (_nt(parts[0], oh) + (_nt(parts[1], oh) + _nt(parts[2], oh)))
        far = jnp.sum(jnp.sum(f_ref[...], axis=2), axis=1, keepdims=True)
        lane = lax.broadcasted_iota(jnp.int32, (A_HEADS, 384), 1)
        o_ref[...] = acc + jnp.where(lane == 2 * REL_CLIP, far, 0.0)

    return pl.pallas_call(
        body, name="bias_grad",
        out_shape=jax.ShapeDtypeStruct((A_HEADS, 384), F32),
    )(db_t, db_far)


ATT_CB = 16


WIN = BAND + CHUNK


def _stack_heads(a, lane):
    return jnp.concatenate([jnp.where(lane < 64, a, 0.0), jnp.where(lane >= 64, a, 0.0)], axis=0)


def _fill_band_pads(k_ref, v_ref, kp, vp, s):
    z = jnp.zeros((A_LOOKBACK * CHUNK, 128), BF16)
    kp[pl.ds(0, A_LOOKBACK * CHUNK), :] = z
    vp[pl.ds(0, A_LOOKBACK * CHUNK), :] = z
    kp[pl.ds(A_LOOKBACK * CHUNK, s), :] = _bf(k_ref[...])
    vp[pl.ds(A_LOOKBACK * CHUNK, s), :] = _bf(v_ref[...])


def attn_fwd(proj, bias_band, b, s):
    t = b * s
    nc = s // CHUNK
    qb, kb_, vb_ = P_QA // 128, P_KA // 128, P_VA // 128

    nstep = nc // ATT_CB
    rows = ATT_CB * CHUNK

    def body(q_ref, k_ref, v_ref, b_ref, o_ref, lse_ref, kp, vp):
        n0 = pl.program_id(2) * ATT_CB

        @pl.when(n0 == 0)
        def _():
            _fill_band_pads(k_ref, v_ref, kp, vp, s)

        lane = lax.broadcasted_iota(jnp.int32, (2 * CHUNK, 128), 1)
        col = lax.broadcasted_iota(jnp.int32, (4 * CHUNK, WIN), 1)
        bias4 = b_ref[...]

        def two_pairs(i, carry):
            pps = (2 * i, 2 * i + 1)
            ns = [n0 + 2 * pp for pp in pps]
            r0s = [pl.multiple_of(pp * 2 * CHUNK, 2 * CHUNK) for pp in pps]
            starts = [pl.multiple_of(n * CHUNK, CHUNK) for n in ns]
            kbs = [kp[pl.ds(st_, WIN), :] for st_ in starts]
            vbs = [vp[pl.ds(st_, WIN), :] for st_ in starts]
            q4s = [_bf(_stack_heads(q_ref[pl.ds(r0, 2 * CHUNK), :] * (A_HEAD_DIM ** -0.5), lane)) for r0 in r0s]
            qks = [_nt(q4, kb) for q4, kb in zip(q4s, kbs)]
            scs = [jnp.where(col >= (A_LOOKBACK - n) * CHUNK, qk + bias4, -1e30) for n, qk in zip(ns, qks)]
            mxs = [jnp.max(sc, axis=1, keepdims=True) for sc in scs]
            ps = [jnp.exp(sc - mx) for sc, mx in zip(scs, mxs)]
            ls = [jnp.sum(p, axis=1, keepdims=True) for p in ps]
            o4s = [_nn(_bf(p), vb) / l for p, vb, l in zip(ps, vbs, ls)]
            for r0, o4, mx, l in zip(r0s, o4s, mxs, ls):
                lse4 = mx + jnp.log(l)
                o_ref[pl.ds(r0, 2 * CHUNK), :] = jnp.where(lane < 64, o4[:2 * CHUNK], o4[2 * CHUNK:])
                lse_ref[pl.ds(r0, 2 * CHUNK), :] = jnp.where(lane < 64, lse4[:2 * CHUNK], lse4[2 * CHUNK:])
            return carry

        lax.fori_loop(0, ATT_CB // 4, two_pairs, 0)

    return pl.pallas_call(
        body, name="attn_fwd", grid=(b, 4, nstep),
        in_specs=[pl.BlockSpec((rows, 128), lambda bb, m, n: (bb * nstep + n, qb + m)),
                  pl.BlockSpec((s, 128), lambda bb, m, n: (bb, kb_ + m)),
                  pl.BlockSpec((s, 128), lambda bb, m, n: (bb, vb_ + m)),
                  pl.BlockSpec((None, 4 * CHUNK, WIN), lambda bb, m, n: (m, 0, 0))],
        out_specs=[pl.BlockSpec((rows, 128), lambda bb, m, n: (bb * nstep + n, m)),
                   pl.BlockSpec((rows, 128), lambda bb, m, n: (bb * nstep + n, m))],
        out_shape=[jax.ShapeDtypeStruct((t, A_WIDTH), F32), jax.ShapeDtypeStruct((t, A_WIDTH), F32)],
        scratch_shapes=[pltpu.VMEM((s + A_LOOKBACK * CHUNK, 128), BF16),
                        pltpu.VMEM((s + A_LOOKBACK * CHUNK, 128), BF16)],
        compiler_params=_cp(("parallel", "parallel", "arbitrary"), VMEM_LIMIT),
    )(proj, proj, proj, bias_band)


def attn_bwd(proj, bias_band, y_a, lse, dy_a, b, s):
    t = b * s
    nc = s // CHUNK
    qb, kb_, vb_ = P_QA // 128, P_KA // 128, P_VA // 128
    pad = A_LOOKBACK * CHUNK
    nstep = nc // ATT_CB
    rows = ATT_CB * CHUNK

    def body(q_ref, k_ref, v_ref, b_ref, do_ref, o_ref, lse_ref,
             dq_ref, dk_ref, dv_ref, dbt_ref, dbf_ref, kp, vp, dkp, dvp):
        bb = pl.program_id(1)
        n0 = pl.program_id(2) * ATT_CB

        @pl.when(n0 == 0)
        def _():
            _fill_band_pads(k_ref, v_ref, kp, vp, s)
            dkp[...] = jnp.zeros_like(dkp)
            dvp[...] = jnp.zeros_like(dvp)

        @pl.when((n0 == 0) & (bb == 0))
        def _():
            dbt_ref[...] = jnp.zeros_like(dbt_ref)
            dbf_ref[...] = jnp.zeros_like(dbf_ref)

        lane = lax.broadcasted_iota(jnp.int32, (2 * CHUNK, 128), 1)
        col = lax.broadcasted_iota(jnp.int32, (4 * CHUNK, WIN), 1)
        bias4 = b_ref[...]

        def two_pairs(i, carry):
            pps = (2 * i, 2 * i + 1)
            two = range(2)
            ns = [n0 + 2 * pp for pp in pps]
            r0s = [pl.multiple_of(pp * 2 * CHUNK, 2 * CHUNK) for pp in pps]
            starts = [pl.multiple_of(n * CHUNK, CHUNK) for n in ns]
            kbs = [kp[pl.ds(st_, WIN), :] for st_ in starts]
            vbs = [vp[pl.ds(st_, WIN), :] for st_ in starts]
            q4bs = [_bf(_stack_heads(q_ref[pl.ds(r0, 2 * CHUNK), :] * (A_HEAD_DIM ** -0.5), lane)) for r0 in r0s]
            do4s = [_stack_heads(do_ref[pl.ds(r0, 2 * CHUNK), :], lane) for r0 in r0s]
            do4bs = [_bf(d) for d in do4s]
            os_ = [o_ref[pl.ds(r0, 2 * CHUNK), :] for r0 in r0s]
            lsevs = [lse_ref[pl.ds(r0, 2 * CHUNK), :] for r0 in r0s]
            lse4s = [jnp.concatenate([v_[:, 0:1], v_[:, 64:65]], axis=0) for v_ in lsevs]
            qks = [_nt(q4bs[j], kbs[j]) for j in two]
            dps = [_nt(do4bs[j], vbs[j]) for j in two]
            deltas = [jnp.sum(do4s[j] * jnp.concatenate([os_[j], os_[j]], axis=0), axis=1, keepdims=True) for j in two]
            ps = [jnp.exp(jnp.where(col >= (A_LOOKBACK - ns[j]) * CHUNK, qks[j] + bias4, -1e30) - lse4s[j])
                  for j in two]
            pbs = [_bf(p) for p in ps]
            dss = [ps[j] * (dps[j] - deltas[j]) for j in two]
            dsbs = [_bf(d) for d in dss]
            dv_ws = [_tn(pbs[j], do4bs[j]) for j in two]
            dq4s = [_nn(dsbs[j], kbs[j]) for j in two]
            dk_ws = [_tn(dsbs[j], q4bs[j]) for j in two]
            for j in two:
                dq_ref[pl.ds(r0s[j], 2 * CHUNK), :] = _bf(
                    jnp.where(lane < 64, dq4s[j][:2 * CHUNK], dq4s[j][2 * CHUNK:]) * (A_HEAD_DIM ** -0.5))
                dkp[pl.ds(starts[j], WIN), :] += dk_ws[j]
                dvp[pl.ds(starts[j], WIN), :] += dv_ws[j]
                dbt_ref[...] += dss[j][:, WIN - 256:]
                dbf_ref[...] += dss[j][:, 0:128] + dss[j][:, 128:256] + dss[j][:, 256:384]
            return carry

        lax.fori_loop(0, ATT_CB // 4, two_pairs, 0)

        @pl.when(n0 == nc - ATT_CB)
        def _():
            dk_ref[...] = _bf(dkp[pl.ds(pad, s), :])
            dv_ref[...] = _bf(dvp[pl.ds(pad, s), :])

    return pl.pallas_call(
        body, name="attn_bwd", grid=(4, b, nstep),
        in_specs=[pl.BlockSpec((rows, 128), lambda m, bb, n: (bb * nstep + n, qb + m)),
                  pl.BlockSpec((s, 128), lambda m, bb, n: (bb, kb_ + m)),
                  pl.BlockSpec((s, 128), lambda m, bb, n: (bb, vb_ + m)),
                  pl.BlockSpec((None, 4 * CHUNK, WIN), lambda m, bb, n: (m, 0, 0)),
                  pl.BlockSpec((rows, 128), lambda m, bb, n: (bb * nstep + n, m)),
                  pl.BlockSpec((rows, 128), lambda m, bb, n: (bb * nstep + n, m)),
                  pl.BlockSpec((rows, 128), lambda m, bb, n: (bb * nstep + n, m))],
        out_specs=[pl.BlockSpec((rows, 128), lambda m, bb, n: (bb * nstep + n, m)),
                   pl.BlockSpec((s, 128), lambda m, bb, n: (bb, m)),
                   pl.BlockSpec((s, 128), lambda m, bb, n: (bb, m)),
                   pl.BlockSpec((None, 4 * CHUNK, 256), lambda m, bb, n: (m, 0, 0)),
                   pl.BlockSpec((None, 4 * CHUNK, 128), lambda m, bb, n: (m, 0, 0))],
        out_shape=[jax.ShapeDtypeStruct((t, A_WIDTH), BF16)] * 3
        + [jax.ShapeDtypeStruct((4, 4 * CHUNK, 256), F32),
           jax.ShapeDtypeStruct((4, 4 * CHUNK, 128), F32)],
        scratch_shapes=[pltpu.VMEM((s + pad, 128), BF16), pltpu.VMEM((s + pad, 128), BF16),
                        pltpu.VMEM((s + pad, 128), F32), pltpu.VMEM((s + pad, 128), F32)],
        compiler_params=_cp(("parallel", "arbitrary", "arbitrary"), VMEM_LIMIT),
    )(proj, proj, proj, bias_band, dy_a, y_a, lse)


def _conv_taps(x, w, s):
    row = lax.broadcasted_iota(jnp.int32, x.shape, 0)
    shifted = [x] + [jnp.where(row >= i, pltpu.roll(x, i, 0), 0.0) for i in range(1, CONV_K)]
    acc = shifted[0] * w[CONV_K - 1:CONV_K, :]
    for i in range(1, CONV_K):
        acc = acc + shifted[i] * w[CONV_K - 1 - i:CONV_K - i, :]
    return acc, shifted


def conv_fwd(proj, conv_w8, b, s):
    cb = 512
    c0 = P_CONV // cb

    def body(x_ref, w_ref, o_ref):
        a, _ = _conv_taps(x_ref[...].astype(F32), w_ref[...], s)
        o_ref[...] = a * _sigmoid(a)

    return pl.pallas_call(
        body, name="conv_fwd", grid=(b, CONV_CH // cb),
        in_specs=[pl.BlockSpec((s, cb), lambda bb, j: (bb, c0 + j)),
                  pl.BlockSpec((8, cb), lambda bb, j: (0, j))],
        out_specs=pl.BlockSpec((s, cb), lambda bb, j: (bb, j)),
        out_shape=jax.ShapeDtypeStruct((b * s, CONV_CH), F32),
        compiler_params=_cp(("parallel", "parallel"), VMEM_LIMIT),
    )(proj, conv_w8)


def conv_bwd(proj, conv_w8, dc3, b, s):
    cb = 512
    c0 = P_CONV // cb

    def body(x_ref, w_ref, dc_ref, dx_ref, dw_ref):
        @pl.when(pl.program_id(1) == 0)
        def _():
            dw_ref[...] = jnp.zeros_like(dw_ref)

        w = w_ref[...]
        a, shifted = _conv_taps(x_ref[...].astype(F32), w, s)
        sg = _sigmoid(a)
        da = dc_ref[...] * (sg * (1.0 + a * (1.0 - sg)))
        row = lax.broadcasted_iota(jnp.int32, da.shape, 0)
        dx = da * w[CONV_K - 1:CONV_K, :]
        for i in range(1, CONV_K):
            dx = dx + jnp.where(row < s - i, pltpu.roll(da, s - i, 0), 0.0) * w[CONV_K - 1 - i:CONV_K - i, :]
        dx_ref[...] = _bf(dx)
        r8 =lax.broadcasted_iota(jnp.int32, (8, cb), 0)
        dw = jnp.zeros((8, cb), F32)
        for i in range(CONV_K):
            dw = dw + jnp.where(r8 == CONV_K - 1 - i, jnp.sum(da * shifted[i], axis=0, keepdims=True), 0.0)
        dw_ref[...] += dw

    return pl.pallas_call(
        body, name="conv_bwd", grid=(CONV_CH // cb, b),
        in_specs=[pl.BlockSpec((s, cb), lambda j, bb: (bb, c0 + j)),
                  pl.BlockSpec((8, cb), lambda j, bb: (0, j)),
                  pl.BlockSpec((None, s, cb), lambda j, bb: (j, bb, 0))],
        out_specs=[pl.BlockSpec((s, cb), lambda j, bb: (bb, j)),
                   pl.BlockSpec((8, cb), lambda j, bb: (0, j))],
        out_shape=[jax.ShapeDtypeStruct((b * s, CONV_CH), BF16), jax.ShapeDtypeStruct((8, CONV_CH), F32)],
        compiler_params=_cp(("parallel", "arbitrary"), VMEM_LIMIT),
    )(proj, conv_w8, dc3)


def _pick_lane(v, k):
    lane = lax.broadcasted_iota(jnp.int32, v.shape, 1)
    return jnp.sum(jnp.where(lane == k, v, 0.0), axis=1, keepdims=True)


def _chunk_masks(ncb):
    i = lax.broadcasted_iota(jnp.int32, (ncb, CHUNK, CHUNK), 1)
    j = lax.broadcasted_iota(jnp.int32, (ncb, CHUNK, CHUNK), 2)
    return i, j


def _col_of_row(rowvec, eye):
    return jnp.sum(jnp.where(eye, rowvec, 0.0), axis=2, keepdims=True)


def _dn_chunk_math(cq, ck, cv, bd, al_row, dtb_row, h, ncb, tm=None):
    r = ncb * CHUNK
    i, j = _chunk_masks(ncb)
    eye = i == j
    low = i >= j
    strict = i > j
    ones = jnp.ones((ncb, CHUNK, CHUNK), F32)

    braw = _pick_lane(bd, h)
    draw = _pick_lane(bd, B_HEADS + h)
    al = _pick_lane(al_row, h)
    dtb = _pick_lane(dtb_row, h)
    ea = jnp.exp(al)
    beta = _sigmoid(braw)
    sp_arg = draw + dtb
    g = -ea * _softplus(sp_arg)

    rq = lax.rsqrt(jnp.sum(cq * cq, axis=1, keepdims=True) + EPS)
    rk = lax.rsqrt(jnp.sum(ck * ck, axis=1, keepdims=True) + EPS)
    nq = cq * rq
    kn = ck * rk
    qn = nq * (B_DIM ** -0.5)

    def c3(a):
        return a.reshape(ncb, CHUNK, a.shape[-1])

    qn3, kn3, v3, beta3 = c3(qn), c3(kn), c3(cv), c3(beta)
    gb = jnp.broadcast_to(c3(g), (ncb, CHUNK, CHUNK))
    gc_b = _bnn_exact(low.astype(BF16), gb)
    gr_b = _bnn_exact(_bf(ones), jnp.where(eye, gc_b, 0.0))
    dm = jnp.where(low, jnp.exp(jnp.where(low, gc_b - gr_b, 0.0)), 0.0)
    gc = gc_b[:, :, 0:1]
    gl = gc_b[:, CHUNK - 1:CHUNK, 0:1]
    gam = jnp.exp(gc)
    egl = jnp.exp(gl)
    edec = jnp.exp(gl - gc)

    knb = _bf(kn3)
    kk = _bnt(knb, knb)
    kd = jnp.where(strict, kk * dm, 0.0)
    a = beta3 * kd
    sz = 1 if tm is None else CHUNK
    if tm is None:
        tm = eye.astype(F32)
    while sz < CHUNK:
        off = jnp.where(((i // (2 * sz)) == (j // (2 * sz))) & ((i // sz) != (j // sz)), a, 0.0)
        tmb = _bf(tm)
        tm = tm - _bnn(_bf(_bnn(tmb, _bf(off))), tmb)
        sz *= 2
    bv = beta3 * v3
    bk = (beta3 * gam) * kn3
    sol = _bnn3(_split(tm), jnp.concatenate([bv, bk], axis=2))
    u, wk = sol[:, :, :B_DIM], sol[:, :, B_DIM:]
    qk = _bnt(_bf(qn3), knb)
    p = jnp.where(low, qk * dm, 0.0)
    kdec = kn3 * edec
    qg = gam * qn3
    return dict(beta=beta3, g=c3(g), ea=ea, sp_arg=c3(sp_arg), rq=c3(rq), rk=c3(rk), nq=c3(nq),
                qn=qn3, kn=kn3, v=v3, gc=gc, gl=gl, gam=gam, egl=egl, edec=edec, dm=dm, kd=kd, a=a,
                tm=tm, u=u, wk=wk, qk=qk, p=p, kdec=kdec, qg=qg, eye=eye, low=low, strict=strict)


def dn_prep(c, proj, al_row, dtb_row, b, s, ncb=32):
    t = b * s
    r = ncb * CHUNK
    nblk = t // r
    bd_blk = 0

    def body(cq_ref, ck_ref, cv_ref, bd_ref, al_ref, dtb_ref, u_ref, wk_ref, qg_ref, kdec_ref, p_ref, egl_ref,
             tm_ref):
        h = pl.program_id(1)
        m = _dn_chunk_math(cq_ref[...], ck_ref[...], cv_ref[...], bd_ref[...].astype(F32), al_ref[...], dtb_ref[...], h, ncb)
        tm_ref[...] = m["tm"].reshape(r, CHUNK)
        u_ref[...] = m["u"].reshape(r, B_DIM)
        wk_ref[...] = _bf(m["wk"].reshape(r, B_DIM))
        qg_ref[...] = _bf(m["qg"].reshape(r, B_DIM))
        kdec_ref[...] = _bf(m["kdec"].reshape(r, B_DIM))
        p_ref[...] = m["p"].reshape(r, CHUNK)
        egl_ref[...] = jnp.broadcast_to(m["egl"], (ncb, 8, 128)).reshape(ncb * 8, 128)

    col = lambda k: pl.BlockSpec((r, 128), lambda i, h: (i, k * B_HEADS + h))
    out_col = pl.BlockSpec((r, 128), lambda i, h: (i, h))
    small = pl.BlockSpec((1, 128), lambda i, h: (0, 0))
    return pl.pallas_call(
        body, name="dn_prep", grid=(nblk, B_HEADS),
        in_specs=[col(0), col(1), col(2), pl.BlockSpec((r, 128), lambda i, h: (i, bd_blk)), small, small],
        out_specs=[out_col, out_col, out_col, out_col,
                   pl.BlockSpec((None, r, CHUNK), lambda i, h: (h, i, 0)),
                   pl.BlockSpec((None, ncb * 8, 128), lambda i, h: (h, i, 0)),
                   pl.BlockSpec((None, r, CHUNK), lambda i, h: (h, i, 0))],
        out_shape=[jax.ShapeDtypeStruct((t, B_WIDTH), F32)] + [jax.ShapeDtypeStruct((t, B_WIDTH), BF16)] * 3
        + [jax.ShapeDtypeStruct((B_HEADS, t, CHUNK), F32),
           jax.ShapeDtypeStruct((B_HEADS, t // 8, 128), F32),
           jax.ShapeDtypeStruct((B_HEADS, t, CHUNK), F32)],
        compiler_params=_cp(("parallel", "parallel"), VMEM_LIMIT),
    )(c, c, c, proj, al_row, dtb_row)


SCAN_CB = 8


def dn_scan_fwd(u, wk, qg, kdec, p, egl, b, s):
    t = b * s
    nc = s // CHUNK

    def body(u_ref, wk_ref, qg_ref, kdec_ref, p_ref, egl_ref, o_ref, ss_ref, st):
        @pl.when(pl.program_id(0) == 0)
        def _():
            st[...] = jnp.zeros_like(st)

        chains = [(bb, h) for bb in range(b) for h in range(B_HEADS)]
        sls = [slice(h * B_DIM, (h + 1) * B_DIM) for _, h in chains]
        states = [st[bb * B_HEADS + h] for bb, h in chains]
        for cc in range(SCAN_CB):
            rs = slice(cc * CHUNK, (cc + 1) * CHUNK)
            sbs = [_bf(sh) for sh in states]
            ws = [u_ref[bb, rs, sl] - _nt(wk_ref[bb, rs, sl], sb) for (bb, _), sl, sb in zip(chains, sls, sbs)]
            qs = [_nt(qg_ref[bb, rs, sl], sb) for (bb, _), sl, sb in zip(chains, sls, sbs)]
            wbs = [_bf(w) for w in ws]
            outs = [q + _nn(_bf(p_ref[h, bb, rs, :]), wb) for (bb, h), q, wb in zip(chains, qs, wbs)]
            new_states = [egl_ref[h, bb, cc * 8:cc * 8 + 1, :] * sh + _tn(wb, kdec_ref[bb, rs, sl])
                          for (bb, h), sl, sh, wb in zip(chains, sls, states, wbs)]
            for (bb, h), sh, o in zip(chains, states, outs):
                ss_ref[bb, cc, h] = sh
                o_ref[bb, rs, h * B_DIM:(h + 1) * B_DIM] = o
            states = new_states
        for (bb, h), sh in zip(chains, states):
            st[bb * B_HEADS + h] = sh

    r3 = lambda a: a.reshape(b, s, B_WIDTH)
    rows = SCAN_CB * CHUNK
    act = pl.BlockSpec((b, rows, B_WIDTH), lambda n: (0, n, 0))
    o, states = pl.pallas_call(
        body, name="dn_scan_fwd", grid=(nc // SCAN_CB,),
        in_specs=[act, act, act, act,
                  pl.BlockSpec((B_HEADS, b, rows, CHUNK), lambda n: (0, 0, n, 0)),
                  pl.BlockSpec((B_HEADS, b, SCAN_CB * 8, 128), lambda n: (0, 0, n, 0))],
        out_specs=[act, pl.BlockSpec((b, SCAN_CB, B_HEADS, B_DIM, B_DIM), lambda n: (0, n, 0, 0, 0))],
        out_shape=[jax.ShapeDtypeStruct((b, s, B_WIDTH), F32),
                   jax.ShapeDtypeStruct((b, nc, B_HEADS, B_DIM, B_DIM), F32)],
        scratch_shapes=[pltpu.VMEM((b * B_HEADS, B_DIM, B_DIM), F32)],
        compiler_params=_cp(("arbitrary",), VMEM_LIMIT),
    )(r3(u), r3(wk), r3(qg), r3(kdec), p.reshape(B_HEADS, b, s, CHUNK), egl.reshape(B_HEADS, b, s // 8, 128))
    return o.reshape(t, B_WIDTH), states


def dn_scan_bwd(u, wk, qg, kdec, p, egl, states, do, b, s):
    t = b * s
    nc = s // CHUNK

    def body(u_ref, wk_ref, qg_ref, kdec_ref, p_ref, egl_ref, ss_ref, do_ref,
             dw_ref, dwk_ref, dqg_ref, dkdec_ref, dp_ref, degl_ref, dst):
        @pl.when(pl.program_id(0) == 0)
        def _():
            dst[...] = jnp.zeros_like(dst)

        chains = [(bb, h) for bb in range(b) for h in range(B_HEADS)]
        dstates = [dst[bb * B_HEADS + h] for bb, h in chains]
        n8 = range(len(chains))
        sls = [slice(h * B_DIM, (h + 1) * B_DIM) for _, h in chains]
        for cc in reversed(range(SCAN_CB)):
            rs = slice(cc * CHUNK, (cc + 1) * CHUNK)
            shs = [ss_ref[bb, cc, h] for bb, h in chains]
            sbs = [_bf(sh) for sh in shs]
            dsbs = [_bf(dsp) for dsp in dstates]
            wkbs = [wk_ref[bb, rs, sl] for (bb, _), sl in zip(chains, sls)]
            dobs = [_bf(do_ref[bb, rs, sl]) for (bb, _), sl in zip(chains, sls)]
            t1 = [_nt(wkbs[i], sbs[i]) for i in n8]
            dwa = [_tn(_bf(p_ref[h, bb, rs, :]), dobs[i]) for i, (bb, h) in enumerate(chains)]
            dwb_ = [_nt(kdec_ref[bb, rs, sls[i]], dsbs[i]) for i, (bb, _) in enumerate(chains)]
            dqgs = [_nn(dobs[i], sbs[i]) for i in n8]
            dsq = [_tn(dobs[i], qg_ref[bb, rs, sls[i]]) for i, (bb, _) in enumerate(chains)]
            wbs = [_bf(u_ref[bb, rs, sls[i]] - t1[i]) for i, (bb, _) in enumerate(chains)]
            dws = [dwa[i] + dwb_[i] for i in n8]
            dwbs = [_bf(dw) for dw in dws]
            dwks = [-_nn(dwbs[i], sbs[i]) for i in n8]
            dkdecs = [_nn(wbs[i], dsbs[i]) for i in n8]
            dpms = [_nt(dobs[i], wbs[i]) for i in n8]
            dsw = [_tn(dwbs[i], wkbs[i]) for i in n8]
            tots = [jnp.sum(jnp.sum(shs[i] * dstates[i], axis=1, keepdims=True), axis=0, keepdims=True) for i in n8]
            new_dss = [egl_ref[h, bb, cc * 8:cc * 8 + 1, :] * dstates[i] + dsq[i] - dsw[i]
                       for i, (bb, h) in enumerate(chains)]
            for i, (bb, h) in enumerate(chains):
                dw_ref[bb, rs, sls[i]] = dws[i]
                dqg_ref[bb, rs, sls[i]] = dqgs[i]
                dwk_ref[bb, rs, sls[i]] = dwks[i]
                dkdec_ref[bb, rs, sls[i]] = dkdecs[i]
                dp_ref[h, bb, rs, :] = dpms[i]
                degl_ref[h, bb, cc * 8:(cc + 1) * 8, :] = jnp.broadcast_to(tots[i], (8, 128))
            dstates = new_dss
        for (bb, h), dsp in zip(chains, dstates):
            dst[bb * B_HEADS + h] = dsp

    r3 = lambda a: a.reshape(b, s, B_WIDTH)
    rows = SCAN_CB * CHUNK
    last = nc // SCAN_CB - 1
    act = pl.BlockSpec((b, rows, B_WIDTH), lambda n: (0, last - n, 0))
    pspec = pl.BlockSpec((B_HEADS, b, rows, CHUNK), lambda n: (0, 0, last - n, 0))
    espec = pl.BlockSpec((B_HEADS, b, SCAN_CB * 8, 128), lambda n: (0, 0, last - n, 0))
    outs = pl.pallas_call(
        body, name="dn_scan_bwd", grid=(nc // SCAN_CB,),
        in_specs=[act, act, act, act, pspec, espec,
                  pl.BlockSpec((b, SCAN_CB, B_HEADS, B_DIM, B_DIM), lambda n: (0, last - n, 0, 0, 0)),
                  act],
        out_specs=[act, act, act, act, pspec, espec],
        out_shape=[jax.ShapeDtypeStruct((b, s, B_WIDTH), F32)] * 4
        + [jax.ShapeDtypeStruct((B_HEADS, b, s, CHUNK), F32),
           jax.ShapeDtypeStruct((B_HEADS, b, s // 8, 128), F32)],
        scratch_shapes=[pltpu.VMEM((b * B_HEADS, B_DIM, B_DIM), F32)],
        compiler_params=_cp(("arbitrary",), VMEM_LIMIT),
    )(r3(u), r3(wk), r3(qg), r3(kdec), p.reshape(B_HEADS, b, s, CHUNK), egl.reshape(B_HEADS, b, s // 8, 128),
      states, r3(do))
    return (*[a.reshape(t, B_WIDTH) for a in outs[:4]], outs[4].reshape(B_HEADS, t, CHUNK),
            outs[5].reshape(B_HEADS, t // 8, 128))


def dn_post_bwd(c, proj, al_row, dtb_row, tmat, dw, dwk, dqg, dkdec, dp, degl, b, s, ncb=16):
    t = b * s
    r = ncb * CHUNK
    nblk = t // r
    bd_blk = 0

    def body(cq_ref, ck_ref, cv_ref, bd_ref, al_ref, dtb_ref, tm_ref, dw_ref, dwk_ref, dqg_ref, dkdec_ref, dp_ref,
             degl_ref, dc_ref, dbd_ref, dal_ref, ddtb_ref):
        h = pl.program_id(1)

        @pl.when((pl.program_id(0) == 0) & (h == 0))
        def _():
            dal_ref[...] = jnp.zeros_like(dal_ref)
            ddtb_ref[...] = jnp.zeros_like(ddtb_ref)

        m = _dn_chunk_math(cq_ref[...], ck_ref[...], cv_ref[...], bd_ref[...].astype(F32), al_ref[...], dtb_ref[...], h, ncb,
                           tm=tm_ref[...].reshape(ncb, CHUNK, CHUNK))
        eye, low, strict = m["eye"], m["low"], m["strict"]
        eyef = eye.astype(F32)

        def c3(a):
            return a.reshape(ncb, CHUNK, a.shape[-1])

        du, dwkv, dqg, dkdec = c3(dw_ref[...]), c3(dwk_ref[...]), c3(dqg_ref[...]), c3(dkdec_ref[...])
        dpm = jnp.where(low, c3(dp_ref[...]), 0.0)
        degl = degl_ref[...].reshape(ncb, 8, 128)[:, 0:1, 0:1]
        beta, gam, kn, qn, v = m["beta"], m["gam"], m["kn"], m["qn"], m["v"]
        dm, kd, a, p = m["dm"], m["kd"], m["a"], m["p"]
        knb, qnb = _bf(kn), _bf(qn)

        eyeb = _bf(eyef)
        th, tl = _split(m["tm"])
        tts = (_bf(_bnt(eyeb, th)), _bf(_bnt(eyeb, tl)))
        xy = _bnn3(tts, jnp.concatenate([du, dwkv], axis=2))
        x, y = xy[:, :, :B_DIM], xy[:, :, B_DIM:]
        da = -jnp.where(strict, _bnt(_bf(x), _bf(m["u"])) + _bnt(_bf(y), _bf(m["wk"])), 0.0)
        dv = beta * x
        sy = jnp.sum(y * kn, axis=2, keepdims=True)
        dbeta = jnp.sum(x * v, axis=2, keepdims=True) + gam * sy + jnp.sum(da * kd, axis=2, keepdims=True)
        dgam = beta * sy + jnp.sum(dqg * qn, axis=2, keepdims=True)
        dkk = da * beta * dm
        dqk = dpm * dm
        dkkb, dqkb = _bf(dkk), _bf(dqk)
        dkn = ((beta * gam) * y + _bnn(dkkb, knb) + _bnn(_bf(_bnt(eyeb, dkkb)), knb)
               + _bnn(_bf(_bnt(eyeb, dqkb)), qnb) + dkdec * m["edec"])
        dqn = gam * dqg + _bnn(dqkb, knb)
        mm = da * a + dpm * p
        ek = jnp.sum(dkdec * m["kdec"], axis=2, keepdims=True)
        dgc = (jnp.sum(mm, axis=2, keepdims=True) - _col_of_row(jnp.sum(mm, axis=1, keepdims=True), eye)
               + dgam * gam - ek)
        dgl = jnp.sum(ek, axis=1, keepdims=True) + degl * m["egl"]
        i, _ = _chunk_masks(ncb)
        dgc = dgc + jnp.where(i[:, :, 0:1] == CHUNK - 1, dgl, 0.0)
        upper = (i <= _chunk_masks(ncb)[1]).astype(BF16)
        dg = _bnn_exact(upper, jnp.broadcast_to(dgc, (ncb, CHUNK, CHUNK)))[:, :, 0:1]

        nq = m["nq"]
        dnq = dqn * (B_DIM ** -0.5)
        dcq = m["rq"] * (dnq - nq * jnp.sum(nq * dnq, axis=2, keepdims=True))
        dck = m["rk"] * (dkn - kn * jnp.sum(kn * dkn, axis=2, keepdims=True))
        dc_ref[0] = dcq.reshape(r, B_DIM)
        dc_ref[1] = dck.reshape(r, B_DIM)
        dc_ref[2] = dv.reshape(r, B_DIM)

        dbraw = (dbeta * beta * (1.0 - beta)).reshape(r, 1)
        sgm = _sigmoid(m["sp_arg"])
        ddraw3 = dg * (-m["ea"]) * sgm
        ddraw = ddraw3.reshape(r, 1)
        lane = lax.broadcasted_iota(jnp.int32, (r, 128), 1)
        contrib = jnp.where(lane == h, dbraw, 0.0) + jnp.where(lane == B_HEADS + h, ddraw, 0.0)

        @pl.when(h == 0)
        def _():
            dbd_ref[...] = contrib

        @pl.when(h != 0)
        def _():
            dbd_ref[...] += contrib

        lane8 = lax.broadcasted_iota(jnp.int32, (8, 128), 1)
        tot_al = jnp.sum(jnp.sum(dg * m["g"], axis=1, keepdims=True), axis=0, keepdims=True).reshape(1, 1)
        tot_dtb = jnp.sum(jnp.sum(ddraw3, axis=1, keepdims=True), axis=0, keepdims=True).reshape(1, 1)
        dal_ref[...] += jnp.where(lane8 == h, tot_al, 0.0)
        ddtb_ref[...] += jnp.where(lane8 == h, tot_dtb, 0.0)

    col = lambda k: pl.BlockSpec((r, 128), lambda i, h: (i, k * B_HEADS + h))
    hcol = pl.BlockSpec((r, 128), lambda i, h: (i, h))
    small = pl.BlockSpec((1, 128), lambda i, h: (0, 0))
    acc = pl.BlockSpec((8, 128), lambda i, h: (0, 0))
    return pl.pallas_call(
        body, name="dn_post_bwd", grid=(nblk, B_HEADS),
        in_specs=[col(0), col(1), col(2), pl.BlockSpec((r, 128), lambda i, h: (i, bd_blk)), small, small,
                  pl.BlockSpec((None, r, CHUNK), lambda i, h: (h, i, 0)),
                  hcol, hcol, hcol, hcol,
                  pl.BlockSpec((None, r, CHUNK), lambda i, h: (h, i, 0)),
                  pl.BlockSpec((None, ncb * 8, 128), lambda i, h: (h, i, 0))],
        out_specs=[pl.BlockSpec((3, r, 128), lambda i, h: (0, i, h)),
                   pl.BlockSpec((r, 128), lambda i, h: (i, 0)), acc, acc],
        out_shape=[jax.ShapeDtypeStruct((3, t, B_WIDTH), F32), jax.ShapeDtypeStruct((t, 128), F32),
                   jax.ShapeDtypeStruct((8, 128), F32), jax.ShapeDtypeStruct((8, 128), F32)],
        compiler_params=_cp(("arbitrary", "arbitrary"), VMEM_LIMIT),
    )(c, c, c, proj, al_row, dtb_row, tmat, dw, dwk, dqg, dkdec, dp, degl)


def make_bias_band(rel_bias):
    tail = bias_tail(jnp.pad(rel_bias, ((0, 0), (0, 384 - N_REL))))
    far = jnp.broadcast_to(rel_bias[:, 2 * REL_CLIP][:, None, None], (A_HEADS, CHUNK, BAND - TAIL))
    band = jnp.concatenate([far, jnp.transpose(tail, (1, 0, 2))], axis=2)
    off = jnp.full((A_HEADS, CHUNK, CHUNK), -1e30, F32)
    both = jnp.stack([jnp.concatenate([band, off], axis=2), jnp.concatenate([off, band], axis=2)], axis=1)
    return both.reshape(4, 4 * CHUNK, WIN)


def bias_band_grad(dbt, dbf):
    t5 = dbt.reshape(A_HEADS, 2, CHUNK, 256)
    tail = t5[:, 0, :, :TAIL] + t5[:, 1, :, CHUNK:]
    far = dbf.reshape(A_HEADS, 2, CHUNK, 128).sum(axis=1) + jnp.pad(t5[:, 1, :, :CHUNK], ((0, 0), (0, 0), (0, CHUNK)))
    return bias_grad(jnp.transpose(tail, (1, 0, 2)), far)[:, :N_REL]


def _rms(x):
    r = lax.rsqrt(jnp.mean(x * x, axis=-1, keepdims=True) + EPS)
    return r, x * r


def _rms_bwd(dh, g, r, n):
    dn = dh * g
    return r * (dn - n * jnp.mean(dn * n, axis=-1, keepdims=True)), dh * n


def _gated_onorm(o, z, w_on):
    parts = []
    for h in range(B_HEADS):
        sl = slice(h * B_DIM, (h + 1) * B_DIM)
        r, n = _rms(o[:, sl])
        parts.append((r, n))
    r4 = [p[0] for p in parts]
    n4 = jnp.concatenate([p[1] for p in parts], axis=1)
    w4 = jnp.concatenate([w_on] * B_HEADS, axis=1)
    sz = _sigmoid(z)
    silu = z * sz
    return n4 * w4 * silu, r4, n4, w4, sz, silu


def mid_fwd(x, y_a, o_b, proj, w_on, wa, wb, w_out, tm=512):
    t = x.shape[0]
    tm = min(tm, t)

    def body(x_ref, ya_ref, ob_ref, z_ref, ga_ref, gb_ref, won_ref, wa_ref, wb_ref, wo_ref, x1_ref, mg_ref):
        yb = _gated_onorm(ob_ref[...], z_ref[...].astype(F32), won_ref[...])[0]
        ua = _nn(_bf(ya_ref[...]), wa_ref[...])
        ub = _nn(_bf(yb), wb_ref[...])
        merged = _sigmoid(ga_ref[...].astype(F32)) * ua + _sigmoid(gb_ref[...].astype(F32)) * ub
        mb = _bf(merged)
        mg_ref[...] = mb
        x1_ref[...] = x_ref[...] + _nn(mb, wo_ref[...])

    rowd = pl.BlockSpec((tm, D_MODEL), lambda i: (i, 0))
    row5 = pl.BlockSpec((tm, 512), lambda i: (i, 0))
    full = lambda a: pl.BlockSpec(a.shape, lambda i: (0,) * a.ndim)
    return pl.pallas_call(
        body, name="mid_fwd", grid=(t // tm,),
        in_specs=[rowd, row5, row5,
                  pl.BlockSpec((tm, 512), lambda i: (i, P_Z // 512)),
                  pl.BlockSpec((tm, D_MODEL), lambda i: (i, 0)),
                  pl.BlockSpec((tm, D_MODEL), lambda i: (i, 1)),
                  full(w_on), full(wa), full(wb), full(w_out)],
        out_specs=[rowd, rowd],
        out_shape=[jax.ShapeDtypeStruct((t, D_MODEL), F32), jax.ShapeDtypeStruct((t, D_MODEL), BF16)],
        compiler_params=_cp(("parallel",), VMEM_LIMIT),
    )(x, y_a, o_b, proj, proj, proj, w_on, wa, wb, w_out)


def mid_bwd(dx1, merged, y_a, o_b, proj, w_on, wa, wb, w_out, tm=256):
    t = dx1.shape[0]
    tm = min(tm, t)

    def body(dx1_ref, mg_ref, ya_ref, ob_ref, z_ref, ga_ref, gb_ref, won_ref, wa_ref, wb_ref, wo_ref,
             dya_ref, dob_ref, dz_ref, dg_ref, dwo_ref, dwa_ref, dwb_ref, dwon_ref):
        @pl.when(pl.program_id(0) == 0)
        def _():
            dwo_ref[...] = jnp.zeros_like(dwo_ref)
            dwa_ref[...] = jnp.zeros_like(dwa_ref)
            dwb_ref[...] = jnp.zeros_like(dwb_ref)
            dwon_ref[...] = jnp.zeros_like(dwon_ref)

        dx1b = _bf(dx1_ref[...])
        dmerged = _nt(dx1b, wo_ref[...])
        dwo_ref[...] += _tn(mg_ref[...], dx1b)
        o = ob_ref[...]
        z = z_ref[...].astype(F32)
        yb, r4, n4, w4, sz, silu = _gated_onorm(o, z, won_ref[...])
        yab, ybb = _bf(ya_ref[...]), _bf(yb)
        ua = _nn(yab, wa_ref[...])
        ub = _nn(ybb, wb_ref[...])
        sa, sb = _sigmoid(ga_ref[...].astype(F32)), _sigmoid(gb_ref[...].astype(F32))
        dua, dub = _bf(dmerged * sa), _bf(dmerged * sb)
        dg_ref[:, 0:D_MODEL] = _bf(dmerged * ua * sa * (1.0 - sa))
        dg_ref[:, D_MODEL:2 * D_MODEL] = _bf(dmerged * ub * sb * (1.0 - sb))
        dwa_ref[...] += _tn(yab, dua)
        dwb_ref[...] += _tn(ybb, dub)
        dya_ref[...] = _nt(dua, wa_ref[...])
        dyb = _nt(dub, wb_ref[...])
        dz_ref[...] = _bf(dyb * (n4 * w4) * (sz * (1.0 + z * (1.0 - sz))))
        dnw = dyb * silu
        dwon = jnp.zeros((1, B_DIM), F32)
        for h in range(B_HEADS):
            sl = slice(h * B_DIM, (h + 1) * B_DIM)
            dxh, dgh = _rms_bwd(dnw[:, sl], won_ref[...], r4[h], n4[:, sl])
            dob_ref[:, sl] = dxh
            dwon = dwon + jnp.sum(dgh, axis=0, keepdims=True)
        dwon_ref[...] += jnp.broadcast_to(dwon, (8, B_DIM))

    rowd = pl.BlockSpec((tm, D_MODEL), lambda i: (i, 0))
    row5 = pl.BlockSpec((tm, 512), lambda i: (i, 0))
    full = lambda a: pl.BlockSpec(a.shape, lambda i: (0,) * a.ndim)
    fixed = lambda shp: pl.BlockSpec(shp, lambda i: (0,) * len(shp))
    return pl.pallas_call(
        body, name="mid_bwd", grid=(t // tm,),
        in_specs=[rowd, rowd, row5, row5,
                  pl.BlockSpec((tm, 512), lambda i: (i, P_Z // 512)),
                  pl.BlockSpec((tm, D_MODEL), lambda i: (i, 0)),
                  pl.BlockSpec((tm, D_MODEL), lambda i: (i, 1)),
                  full(w_on), full(wa), full(wb), full(w_out)],
        out_specs=[row5, row5, row5, pl.BlockSpec((tm, 2 * D_MODEL), lambda i: (i, 0)),
                   fixed((D_MODEL, D_MODEL)), fixed((A_WIDTH, D_MODEL)), fixed((B_WIDTH, D_MODEL)),
                   fixed((8, B_DIM))],
        out_shape=[jax.ShapeDtypeStruct((t, 512), F32), jax.ShapeDtypeStruct((t, 512), F32),
                   jax.ShapeDtypeStruct((t, 512), BF16), jax.ShapeDtypeStruct((t, 2 * D_MODEL), BF16),
           jax.ShapeDtypeStruct((D_MODEL, D_MODEL), F32), jax.ShapeDtypeStruct((A_WIDTH, D_MODEL), F32),
           jax.ShapeDtypeStruct((B_WIDTH, D_MODEL), F32), jax.ShapeDtypeStruct((8, B_DIM), F32)],
        compiler_params=_cp(("arbitrary",), VMEM_LIMIT),
    )(dx1, merged, y_a, o_b, proj, proj, proj, w_on, wa, wb, w_out)


FFN_TF = 1408


def ffn_up(x1, g, w_gu, tm=512, tf=FFN_TF):
    t = x1.shape[0]
    tm = min(tm, t)
    nf = D_FF // tf

    def body(x_ref, g_ref, wg_ref, wu_ref, gate_ref, up_ref, act_ref, h_ref):
        @pl.when(pl.program_id(1) == 0)
        def _():
            r, n = _rms(x_ref[...])
            h_ref[...] = _bf(n * g_ref[...])

        hb = h_ref[...]
        gate = _nn(hb, wg_ref[...])
        up = _nn(hb, wu_ref[...])
        gate_ref[...] = _bf(gate)
        up_ref[...] = _bf(up)
        act_ref[...] = _bf(gate * _sigmoid(gate) * up)

    ff = pl.BlockSpec((tm, tf), lambda i, j: (i, j))
    return pl.pallas_call(
        body, name="ffn_up", grid=(t // tm, nf),
        in_specs=[pl.BlockSpec((tm, D_MODEL), lambda i, j: (i, 0)),
                  pl.BlockSpec((1, D_MODEL), lambda i, j: (0, 0)),
                  pl.BlockSpec((D_MODEL, tf), lambda i, j: (0, j)),
                  pl.BlockSpec((D_MODEL, tf), lambda i, j: (0, nf + j))],
        out_specs=[ff, ff, ff, pl.BlockSpec((tm, D_MODEL), lambda i, j: (i, 0))],
        out_shape=[jax.ShapeDtypeStruct((t, D_FF), BF16)] * 3 + [jax.ShapeDtypeStruct((t, D_MODEL), BF16)],
        compiler_params=_cp(("parallel", "arbitrary"), VMEM_LIMIT),
    )(x1, g, w_gu, w_gu)


def matmul_residual(a, w, res, name, tm=512, tk=FFN_TF):
    t, k = a.shape
    n = w.shape[1]
    tm = min(tm, t)

    def body(a_ref, w_ref, r_ref, o_ref):
        @pl.when(pl.program_id(1) == 0)
        def _():
            o_ref[...] = r_ref[...]

        o_ref[...] += _nn(a_ref[...], w_ref[...])

    return pl.pallas_call(
        body, name=name, grid=(t // tm, k // tk),
        in_specs=[pl.BlockSpec((tm, tk), lambda i, j: (i, j)),
                  pl.BlockSpec((tk, n), lambda i, j: (j, 0)),
                  pl.BlockSpec((tm, n), lambda i, j: (i, 0))],
        out_specs=pl.BlockSpec((tm, n), lambda i, j: (i, 0)),
        out_shape=jax.ShapeDtypeStruct((t, n), F32),
        compiler_params=_cp(("parallel", "arbitrary"), VMEM_LIMIT),
    )(a, w, res)


def ffn_act_bwd(dx2, gate, up, w_down, tm=512, tf=FFN_TF):
    t = dx2.shape[0]
    tm = min(tm, t)

    def body(dx2_ref, gate_ref, up_ref, wd_ref, dgate_ref, dup_ref, dx2b_ref):
        @pl.when(pl.program_id(1) == 0)
        def _():
            dx2b_ref[...] = _bf(dx2_ref[...])

        dact = _nt(dx2b_ref[...], wd_ref[...])
        gt, upv = gate_ref[...].astype(F32), up_ref[...].astype(F32)
        sg = _sigmoid(gt)
        t = dact * sg
        dgate_ref[...] = _bf(t * upv * (1.0 + gt * (1.0 - sg)))
        dup_ref[...] = _bf(t * gt)

    ff = pl.BlockSpec((tm, tf), lambda i, j: (i, j))
    return pl.pallas_call(
        body, name="ffn_act_bwd", grid=(t // tm, D_FF // tf),
        in_specs=[pl.BlockSpec((tm, D_MODEL), lambda i, j: (i, 0)), ff, ff,
                  pl.BlockSpec((tf, D_MODEL), lambda i, j: (j, 0))],
        out_specs=[ff, ff],
        out_shape=[jax.ShapeDtypeStruct((t, D_FF), BF16)] * 2,
        scratch_shapes=[pltpu.VMEM((tm, D_MODEL), BF16)],
        compiler_params=_cp(("parallel", "arbitrary"), VMEM_LIMIT),
    )(dx2, gate, up, w_down)


def tail_fwd_bwd(x2, p, target, g_ple, g_final, w_pg, w_pp, tm=512):
    t = x2.shape[0]
    tm = min(tm, t)

    def body(x_ref, p_ref, t_ref, gp_ref, gf_ref, wpg_ref, wpp_ref,
             dx_ref, dwpg_ref, dwpp_ref, dgp_ref, dgf_ref, loss_ref):
        @pl.when(pl.program_id(0) == 0)
        def _():
            dwpg_ref[...] = jnp.zeros_like(dwpg_ref)
            dwpp_ref[...] = jnp.zeros_like(dwpp_ref)
            dgp_ref[...] = jnp.zeros_like(dgp_ref)
            dgf_ref[...] = jnp.zeros_like(dgf_ref)
            loss_ref[...] = jnp.zeros_like(loss_ref)

        x2v = x_ref[...]
        gp, gf = gp_ref[...], gf_ref[...]
        r3, n3 = _rms(x2v)
        h3b = _bf(n3 * gp)
        pb = _bf(p_ref[...])
        pg = _sigmoid(_nn(h3b, wpg_ref[...]))
        pp = _nn(pb, wpp_ref[...])
        x3 = x2v + pg * pp
        r4, n4 = _rms(x3)
        err = n4 * gf - t_ref[...]
        part = 0.5 * jnp.sum(jnp.sum(err * err, axis=1, keepdims=True), axis=0, keepdims=True) / D_MODEL
        loss_ref[...] += jnp.broadcast_to(part, (8, 128))
        dy = err * (1.0 / D_MODEL)
        dx3, dgf = _rms_bwd(dy, gf, r4, n4)
        dgf_ref[...] += jnp.broadcast_to(jnp.sum(dgf, axis=0, keepdims=True), (8, D_MODEL))
        dzp = _bf(dx3 * pp * pg * (1.0 - pg))
        dpp = _bf(dx3 * pg)
        dwpg_ref[...] += _tn(h3b, dzp)
        dwpp_ref[...] += _tn(pb, dpp)
        dh3 = _nt(dzp, wpg_ref[...])
        dx, dgp = _rms_bwd(dh3, gp, r3, n3)
        dgp_ref[...] += jnp.broadcast_to(jnp.sum(dgp, axis=0, keepdims=True), (8, D_MODEL))
        dx_ref[...] = dx3 + dx

    rowd = pl.BlockSpec((tm, D_MODEL), lambda i: (i, 0))
    fixed = lambda shp: pl.BlockSpec(shp, lambda i: (0,) * len(shp))
    return pl.pallas_call(
        body, name="tail_fwd_bwd", grid=(t // tm,),
        in_specs=[rowd, pl.BlockSpec((tm, PLE_DIM), lambda i: (i, 0)), rowd,
                  fixed((1, D_MODEL)), fixed((1, D_MODEL)), fixed((D_MODEL, D_MODEL)), fixed((PLE_DIM, D_MODEL))],
        out_specs=[rowd, fixed((D_MODEL, D_MODEL)), fixed((PLE_DIM, D_MODEL)),
                   fixed((8, D_MODEL)), fixed((8, D_MODEL)), fixed((8, 128))],
        out_shape=[jax.ShapeDtypeStruct((t, D_MODEL), F32), jax.ShapeDtypeStruct((D_MODEL, D_MODEL), F32),
                   jax.ShapeDtypeStruct((PLE_DIM, D_MODEL), F32), jax.ShapeDtypeStruct((8, D_MODEL), F32),
                   jax.ShapeDtypeStruct((8, D_MODEL), F32), jax.ShapeDtypeStruct((8, 128), F32)],
        compiler_params=_cp(("arbitrary",), VMEM_LIMIT),
    )(x2, p, target, g_ple, g_final, w_pg, w_pp)


def in_proj_bwd(pieces, weights, x, dx1, g, name="in_proj_bwd", tm=256):
    t = x.shape[0]
    tm = min(tm, t)
    k = len(pieces)
    assert all(c0 % wd == 0 and w0 % wd == 0 for (_, c0, wd), (_, w0) in zip(pieces, weights))

    def body(*refs):
        p_refs, w_refs = refs[:k], refs[k:2 * k]
        x_ref, dx1_ref, g_ref, dx_ref, dg_ref = refs[2 * k:]

        @pl.when(pl.program_id(0) == 0)
        def _():
            dg_ref[...] = jnp.zeros_like(dg_ref)

        dh = _nt(_bf(p_refs[0][...]), w_refs[0][...])
        for pr, wr in zip(p_refs[1:], w_refs[1:]):
            dh = dh + _nt(_bf(pr[...]), wr[...])
        r, n = _rms(x_ref[...])
        dx, dgc = _rms_bwd(dh, g_ref[...], r, n)
        dx_ref[...] = dx1_ref[...] + dx
        dg_ref[...] += jnp.broadcast_to(jnp.sum(dgc, axis=0, keepdims=True), (8, D_MODEL))

    rowd = pl.BlockSpec((tm, D_MODEL), lambda i: (i, 0))
    return pl.pallas_call(
        body, name=name, grid=(t // tm,),
        in_specs=[pl.BlockSpec((tm, wd), functools.partial(lambda i, cb: (i, cb), cb=c0 // wd))
                  for _, c0, wd in pieces]
        + [pl.BlockSpec((w.shape[0], wd), functools.partial(lambda i, cb: (0, cb), cb=w0 // wd))
           for (w, w0), (_, _, wd) in zip(weights, pieces)]
        + [rowd, rowd, pl.BlockSpec((1, D_MODEL), lambda i: (0, 0))],
        out_specs=[rowd, pl.BlockSpec((8, D_MODEL), lambda i: (0, 0))],
        out_shape=[jax.ShapeDtypeStruct((t, D_MODEL), F32), jax.ShapeDtypeStruct((8, D_MODEL), F32)],
        compiler_params=_cp(("arbitrary",), VMEM_LIMIT),
    )(*[a for a, _, _ in pieces], *[w for w, _ in weights], x, dx1, g)


def adamw(w, g, m, v, name, rows_cap=256, dep=None):
    lead = w.shape[:-2]
    r, c = w.shape[-2:]
    tr = r
    for cand in range(8, min(r, rows_cap) + 1, 8):
        if r % cand == 0:
            tr = cand

    def body(w_ref, g_ref, m_ref, v_ref, *rest):
        d_ref, mo_ref, vo_ref = rest[-3:]
        gv = g_ref[...]
        mn = ADAM_B1 * m_ref[...] + (1.0 - ADAM_B1) * gv
        vn = ADAM_B2 * v_ref[...] + (1.0 - ADAM_B2) * (gv * gv)
        m_hat = mn / (1.0 - ADAM_B1 ** ADAM_STEP)
        v_hat = vn / (1.0 - ADAM_B2 ** ADAM_STEP)
        d_ref[...] = -ADAM_LR * (m_hat / (jnp.sqrt(v_hat) + ADAM_EPS) + ADAM_WD * w_ref[...])
        mo_ref[...] = mn
        vo_ref[...] = vn

    spec = pl.BlockSpec((None,) * len(lead) + (tr, c), lambda i: (0,) * len(lead) + (i, 0))
    extra = [] if dep is None else [dep]
    return pl.pallas_call(
        body, name=name, grid=(r // tr,),
        in_specs=[spec] * 4 + [pl.BlockSpec((8, 128), lambda i: (0, 0))] * len(extra), out_specs=[spec] * 3,
        out_shape=[jax.ShapeDtypeStruct(w.shape, F32)] * 3,
        compiler_params=_cp(("parallel",), VMEM_LIMIT),
    )(w, g.reshape(w.shape), m, v, *extra)


def _w_in_shards(dwp):
    cs = D_IN // N_CHIPS
    regions = ((0, SPLIT_Z, P_QA), (SPLIT_Z, SPLIT_Z + 8, P_BD - SPLIT_Z), (SPLIT_Z + 8, D_IN, -(SPLIT_Z + 8)))

    def original(lo, hi):
        parts = [dwp[:, max(lo, a) + off:min(hi, e) + off] for a, e, off in regions if max(lo, a) < min(hi, e)]
        return parts[0] if len(parts) == 1 else jnp.concatenate(parts, axis=1)

    return jnp.stack([original(s * cs, (s + 1) * cs) for s in range(N_CHIPS)])


class Standalone:
    def __init__(self, later_weights):
        self.later_weights = later_weights

    def begin(self, *a):
        return 0.0

    forward = exchange = join = begin

    def finish(self, after):
        return self.later_weights


def local_step(x3d, p3d, target3d, g4, small, later, early):
    b, s, _ = x3d.shape
    t = b * s
    x = x3d.reshape(t, D_MODEL)
    p = p3d.reshape(t, PLE_DIM)
    target = target3d.reshape(t, D_MODEL)
    cut = SPLIT_Z - 2 * (D_IN // N_CHIPS)
    w_inp = jnp.concatenate([g4[2][:, cut + 8:], g4[3], g4[0], g4[1], g4[2][:, :cut], g4[2][:, cut:cut + 8],
                             jnp.zeros((D_MODEL, 120), BF16)], axis=1)
    al_row = jnp.pad(small["a_log"].reshape(1, B_HEADS), ((0, 0), (0, 128 - B_HEADS)))
    dtb_row = jnp.pad(small["dt_bias"].reshape(1, B_HEADS), ((0, 0), (0, 128 - B_HEADS)))
    conv_w8 = jnp.pad(small["conv_w"].reshape(CONV_K, CONV_CH), ((0, 8 - CONV_K), (0, 0)))
    w_on = small["w_onorm"].reshape(1, B_DIM)
    g_mix, g_ffn = small["g_mix"].reshape(1, D_MODEL), small["g_ffn"].reshape(1, D_MODEL)
    g_ple, g_final = small["g_ple"].reshape(1, D_MODEL), small["g_final"].reshape(1, D_MODEL)
    bias_band = make_bias_band(small["rel_bias"].reshape(A_HEADS, N_REL))

    tok = later.begin()
    proj, h1, bd32 = rms_matmul(x, g_mix + tok, w_inp, "in_proj", tm=1024, tn_cap=1152)
    y_a, lse = attn_fwd(proj, bias_band, b, s)
    tok = later.forward(lse)
    c = conv_fwd(proj, conv_w8 + tok, b, s)
    u, wk, qg, kdec, pm, egl, tmat = dn_prep(c, bd32, al_row, dtb_row, b, s)
    o_b, states = dn_scan_fwd(u, wk, qg, kdec, pm, egl, b, s)
    wts = later.finish(o_b)
    x1, merged = mid_fwd(x, y_a, o_b, proj, w_on, wts["w_branch_a"], wts["w_branch_b"], wts["w_out"])
    gate, up, act, h2 = ffn_up(x1, g_ffn, wts["w_gate_up"])
    x2 = matmul_residual(act, wts["w_down"], x1, "ffn_down")

    dx2, dw_pg, dw_pp, dg_ple, dg_final, loss = tail_fwd_bwd(
        x2, p, target, g_ple, g_final, wts["w_ple_gate"], wts["w_ple_proj"])
    dgate, dup = ffn_act_bwd(dx2, gate, up, wts["w_down"])
    w_gu = wts["w_gate_up"]
    dx1, dg_ffn = in_proj_bwd([(dgate, 0, D_FF), (dup, 0, D_FF)], [(w_gu, 0), (w_gu, D_FF)], x1, dx2, g_ffn,
                              name="ffn_in_bwd")
    dw_down = matmul_tn(act, dx2, "dw_down")
    dw_gu = matmul_tn(h2, dgate, "dw_gate", width=2 * D_FF, tiles_major=True)
    dw_gu = matmul_tn(h2, dup, "dw_up", into=dw_gu, col0=D_FF, width=2 * D_FF, tiles_major=True)
    dy_a, do_b, dz, dgates, dw_out, dwa, dwb, dw_on = mid_bwd(
        dx1, merged, y_a, o_b, proj, w_on, wts["w_branch_a"], wts["w_branch_b"], wts["w_out"])
    tok = early.begin(dict(w_branch_a=dwa, w_branch_b=dwb, w_out=dw_out, w_gate_up=dw_gu, w_down=dw_down,
                           w_ple_gate=dw_pg, w_ple_proj=dw_pp))
    ddw, ddwk, ddqg, ddkdec, ddp, ddegl = dn_scan_bwd(u, wk, qg, kdec, pm, egl + tok, states, do_b, b, s)
    tok = early.exchange(ddegl)
    dc3, dbd, dal, ddtb = dn_post_bwd(c, bd32, al_row + tok, dtb_row, tmat, ddw, ddwk, ddqg, ddkdec, ddp, ddegl, b, s)
    dconv, dconv_w = conv_bwd(proj, conv_w8, dc3, b, s)
    dqa, dka, dva, dbt, dbf = attn_bwd(proj, bias_band, y_a, lse, dy_a, b, s)
    tok = early.join(dqa)

    pieces = [dgates, dqa, dka, dva, dconv, dz, dbd]
    bounds = [0, 2048, 2560, 3072, 3584, 5120, 5632, 5760]
    windows = [(dgates, 0, 2048), (dqa, 0, 512), (dka, 0, 512), (dva, 0, 512), (dconv, 0, 512), (dconv, 512, 512),
               (dconv, 1024, 512), (dz, 0, 512), (dbd, 0, 128)]
    w_cols = [0, P_QA, P_KA, P_VA, P_CONV, P_CONV + 512, P_CONV + 1024, P_Z, P_BD]
    dx, dg_mix = in_proj_bwd(windows, [(w_inp, c0) for c0 in w_cols], x, dx1, g_mix + tok)
    dwp = None
    for k, pc in enumerate(pieces):
        dwp = matmul_tn(h1, pc, "dw_in_%d" % k, into=dwp, col0=bounds[k], width=P_WIDTH)
    reduced_early = early.finish(dwp)
    dw_in = _w_in_shards(dwp)

    grads = dict(w_in=dw_in, w_branch_a=dwa, w_branch_b=dwb, w_out=dw_out, w_gate_up=dw_gu, w_down=dw_down,
                 w_ple_gate=dw_pg, w_ple_proj=dw_pp)
    small_grads = dict(g_mix=dg_mix[0], g_ffn=dg_ffn[0], g_ple=dg_ple[0], g_final=dg_final[0],
                       conv_w=dconv_w[:CONV_K].reshape(-1), rel_bias_parts=(dbt, dbf), w_onorm=dw_on[0],
                       a_log=dal[0, :B_HEADS], dt_bias=ddtb[0, :B_HEADS], loss=loss[0, :1])
    return dx.reshape(b, s, D_MODEL), grads, small_grads, reduced_early


BIG = (("w_in", (D_MODEL, D_IN), 1), ("w_branch_a", (A_WIDTH, D_MODEL), 1), ("w_branch_b", (B_WIDTH, D_MODEL), 1),
       ("w_out", (D_MODEL, D_MODEL), 0), ("w_gate_up", (D_MODEL, 2 * D_FF), 1), ("w_down", (D_FF, D_MODEL), 0),
       ("w_ple_gate", (D_MODEL, D_MODEL), 0), ("w_ple_proj", (PLE_DIM, D_MODEL), 1))
N_CHIPS = 4
FIRST_WEIGHTS = ("w_in",)
LATER_WEIGHTS = ("w_branch_a", "w_branch_b", "w_out", "w_gate_up", "w_down", "w_ple_gate", "w_ple_proj")
LATE_GRADS = ("w_in",)
EARLY_GRADS = ("w_branch_a", "w_branch_b", "w_out", "w_gate_up", "w_down", "w_ple_gate", "w_ple_proj")


def _items(names):
    return [it for it in BIG if it[0] in names]


def _shard_shape(shape, axis):
    return (shape[0] // N_CHIPS, shape[1]) if axis == 0 else (shape[0], shape[1] // N_CHIPS)


def _width_groups(names):
    groups = {}
    for n, shape, axis in _items(names):
        rs, cs = _shard_shape(shape, axis)
        groups.setdefault(cs, []).append((n, rs))
    return sorted(groups.items())


def grad_buffers(grads, names):
    info = {n: (shape, axis) for n, shape, axis in _items(names)}
    bufs = []
    for cs, members in _width_groups(names):
        segs = []
        for n, rs in members:
            g = grads[n].astype(BF16)
            if g.ndim == 2:
                g = (g.reshape(N_CHIPS, rs, cs) if info[n][1] == 0
                     else jnp.transpose(g.reshape(rs, N_CHIPS, cs), (1, 0, 2)))
            segs.append(g)
        bufs.append(segs[0] if len(segs) == 1 else jnp.concatenate(segs, axis=1))
    return bufs


def split_buffers(reduced, names):
    out = {}
    for (cs, members), buf in zip(_width_groups(names), reduced):
        r0 = 0
        for n, rs in members:
            out[n] = buf[r0:r0 + rs]
            r0 += rs
    return out


def _place():
    return lax.axis_index("x"), lax.axis_index("y"), lax.axis_index("c")


ANY = pl.BlockSpec(memory_space=pl.ANY)


def _gathered_shape(item):
    n, shape, _ = item
    return (N_CHIPS,) + _shard_shape(shape, 1) if n == "w_in" else shape


def _gather_block(o_ref, item, cx, cy, hf):
    n, shape, axis = item
    rs, cs = _shard_shape(shape, axis)
    hr = rs // 2
    ci = 2 * cx + cy
    if n == "w_in":
        return o_ref.at[ci, pl.ds(pl.multiple_of(hf * hr, 16), hr), :]
    if axis == 0:
        return o_ref.at[pl.ds(pl.multiple_of(ci * rs + hf * hr, 16), hr), :]
    return o_ref.at[pl.ds(pl.multiple_of(hf * hr, 16), hr), pl.ds(pl.multiple_of(ci * cs, 128), cs)]


def _own_half(w_ref, item, c):
    hr = _shard_shape(item[1], item[2])[0] // 2
    return w_ref.at[pl.ds(pl.multiple_of(c * hr, 16), hr), :]


def _gather_slot(o_ref, item, cx, cy):
    n, shape, axis = item
    rs, cs = _shard_shape(shape, axis)
    ci = 2 * cx + cy
    if n == "w_in":
        return o_ref.at[ci]
    if axis == 0:
        return o_ref.at[pl.ds(pl.multiple_of(ci * rs, 16), rs), :]
    return o_ref.at[:, pl.ds(pl.multiple_of(ci * cs, 128), cs)]


def _other_chips(x, y):
    return [(1 - x, y), (x, 1 - y), (1 - x, 1 - y)]


def allgather_weights(shards, names):
    items = _items(names)
    nw = len(items)

    def body(*refs):
        w_refs, o_refs = refs[:nw], refs[nw:2 * nw]
        send_sems, recv_sems = refs[2 * nw:]
        x, y, c = _place()
        sibling = (x, y, 1 - c)
        chips = _other_chips(x, y)

        def copy(k, src, dst, to):
            return pltpu.make_async_remote_copy(src_ref=src, dst_ref=dst, send_sem=send_sems.at[k],
                                                recv_sem=recv_sems.at[k], device_id=to, device_id_type=MESH)

        def blk(i, cx, cy, hf):
            return _gather_block(o_refs[i], items[i], cx, cy, hf)

        def my_half(i):
            return _own_half(w_refs[i], items[i], c)

        def own(i):
            return _gather_slot(o_refs[i], items[i], x, y)

        first = [copy(7 * i + j, my_half(i), blk(i, x, y, c), (*chip_, c))
                 for i in range(nw) for j, chip_ in enumerate(chips)]
        first += [copy(7 * i + 6, w_refs[i], own(i), sibling) for i in range(nw)]
        for cp in first:
            cp.start()
        passed = []
        for i in range(nw):
            for j, chip_ in enumerate(chips):
                copy(7 * i + j, my_half(i), blk(i, *chip_, c), (*chip_, c)).wait_recv()
                fwd = copy(7 * i + 3 + j, blk(i, *chip_, c), blk(i, *chip_, c), sibling)
                fwd.start()
                passed.append(fwd)
        for i in range(nw):
            for j, chip_ in enumerate(chips):
                copy(7 * i + 3 + j, my_half(i), blk(i, *chip_, 1 - c), sibling).wait_recv()
            copy(7 * i + 6, w_refs[i], own(i), sibling).wait_recv()
        for cp in first + passed:
            cp.wait_send()

    outs = pl.pallas_call(
        body, name="allgather_weights",
        in_specs=[ANY] * nw, out_specs=[ANY] * nw,
        out_shape=[jax.ShapeDtypeStruct(_gathered_shape(it), BF16) for it in items],
        scratch_shapes=[pltpu.SemaphoreType.DMA((7 * nw,)), pltpu.SemaphoreType.DMA((7 * nw,))],
    )(*[shards[it[0]] for it in items])
    return {it[0]: o for it, o in zip(items, outs)}


HBM_SPEC = pl.BlockSpec(memory_space=pltpu.HBM)
SEM_SPEC = pl.BlockSpec(memory_space=pltpu.SEMAPHORE)
EFFECT = pltpu.SideEffectType.DATAFLOW_SIDE_EFFECTING


def _in_hbm(a):
    return pltpu.with_memory_space_constraint(a, pltpu.HBM)


def copies_start(name, bufs, ncopies, plan):
    nb = len(bufs)

    def body(*refs):
        in_refs, send_sems, recv_sems, token = refs[:nb], refs[nb], refs[nb + 1], refs[-1]
        for k, (src, dst, to) in enumerate(plan(in_refs)):
            pltpu.make_async_remote_copy(src_ref=src, dst_ref=dst, send_sem=send_sems.at[k],
                                         recv_sem=recv_sems.at[k], device_id=to, device_id_type=MESH).start()
        token[...] = jnp.zeros_like(token)

    outs = pl.pallas_call(
        body, name=name,
        in_specs=[HBM_SPEC] * nb,
        out_specs=(SEM_SPEC, SEM_SPEC, *[HBM_SPEC] * nb, pl.BlockSpec(memory_space=pltpu.VMEM)),
        out_shape=(pltpu.SemaphoreType.DMA((ncopies,)), pltpu.SemaphoreType.DMA((ncopies,)),
                   *[pltpu.HBM(b.shape, b.dtype) for b in bufs], jax.ShapeDtypeStruct((8, 128), F32)),
        input_output_aliases={i: 2 + i for i in range(nb)},
        compiler_params=pltpu.CompilerParams(has_side_effects=EFFECT),
    )(*[_in_hbm(b) for b in bufs])
    return outs[0], outs[1], list(outs[2:2 + nb]), outs[-1][0, 0]


def copies_wait(name, send_sems, recv_sems, bufs, after, plan):
    nb = len(bufs)

    def body(*refs):
        in_refs, s_sems, r_sems = refs[:nb], refs[nb], refs[nb + 1]
        for k, (src, dst, to) in enumerate(plan(in_refs)):
            cp = pltpu.make_async_remote_copy(src_ref=src, dst_ref=dst, send_sem=s_sems.at[k],
                                              recv_sem=r_sems.at[k], device_id=to, device_id_type=MESH)
            cp.wait_send()
            cp.wait_recv()

    return list(pl.pallas_call(
        body, name=name,
        in_specs=[HBM_SPEC] * nb + [SEM_SPEC, SEM_SPEC, ANY],
        out_specs=tuple([HBM_SPEC] * nb),
        out_shape=tuple(pltpu.HBM(b.shape, b.dtype) for b in bufs),
        input_output_aliases={i: i for i in range(nb)},
        compiler_params=pltpu.CompilerParams(has_side_effects=EFFECT),
    )(*bufs, send_sems, recv_sems, after))


def _landing(shape, dtype):
    return _in_hbm(lax.empty(shape, dtype))


class LaterWeights:
    def __init__(self, shards):
        self.items = _items(LATER_WEIGHTS)
        self.shards = shards
        self.nw = len(self.items)

    def _ici_plan(self, refs):
        x, y, c = _place()
        w_refs, o_refs = refs[:self.nw], refs[self.nw:]
        plan = [(_own_half(w_refs[i], it, c), _gather_block(o_refs[i], it, x, y, c), (*chip_, c))
                for i, it in enumerate(self.items) for chip_ in _other_chips(x, y)]
        return plan + [(w_refs[i], _gather_slot(o_refs[i], it, x, y), (x, y, 1 - c))
                       for i, it in enumerate(self.items)]

    def _d2d_plan(self, refs):
        x, y, c = _place()
        return [(_gather_block(refs[i], it, *chip_, c), _gather_block(refs[i], it, *chip_, c), (x, y, 1 - c))
                for i, it in enumerate(self.items) for chip_ in _other_chips(x, y)]

    def _d2d_wait_plan(self, refs):
        x, y, c = _place()
        return [(_gather_block(refs[i], it, *chip_, c), _gather_block(refs[i], it, *chip_, 1 - c), (x, y, 1 - c))
                for i, it in enumerate(self.items) for chip_ in _other_chips(x, y)]

    def _ici_wait_plan(self, refs):
        x, y, c = _place()
        w_refs, o_refs = refs[:self.nw], refs[self.nw:]
        plan = [(_own_half(w_refs[i], it, c), _gather_block(o_refs[i], it, *chip_, c), (*chip_, c))
                for i, it in enumerate(self.items) for chip_ in _other_chips(x, y)]
        return plan + [(w_refs[i], _gather_slot(o_refs[i], it, x, y), (x, y, 1 - c))
                       for i, it in enumerate(self.items)]

    def begin(self):
        srcs = [self.shards[it[0]] for it in self.items]
        lands = [_landing(_gathered_shape(it), BF16) for it in self.items]
        self.s1, self.r1, self.b1, tok = copies_start("gather_ici_start", srcs + lands, 4 * self.nw, self._ici_plan)
        return tok

    def forward(self, after):
        b1 = copies_wait("gather_ici_wait", self.s1, self.r1, self.b1, after, self._ici_wait_plan)
        self.s2, self.r2, self.b2, tok = copies_start("gather_d2d_start", b1[self.nw:], 3 * self.nw, self._d2d_plan)
        return tok

    def finish(self, after):
        outs = copies_wait("gather_d2d_wait", self.s2, self.r2, self.b2, after, self._d2d_wait_plan)
        return {it[0]: o for it, o in zip(self.items, outs)}


def small_allreduce(v, name):
    r = v.shape[0]

    def body(v_ref, o_ref, buf, send_sems, recv_sems):
        x, y, c = _place()
        me = 4 * x + 2 * y + c
        buf[me] = v_ref[...]
        flips = [(fx, fy, fc) for fx in (0, 1) for fy in (0, 1) for fc in (0, 1)][1:]
        peers = [((1 - x) if fx else x, (1 - y) if fy else y, (1 - c) if fc else c) for fx, fy, fc in flips]

        def copy(k, slot, to):
            return pltpu.make_async_remote_copy(src_ref=v_ref, dst_ref=buf.at[slot], send_sem=send_sems.at[k],
                                                recv_sem=recv_sems.at[k], device_id=to, device_id_type=MESH)

        sends = [copy(k, me, peer) for k, peer in enumerate(peers)]
        for cp in sends:
            cp.start()
        for k, (px, py, pc) in enumerate(peers):
            copy(k, 4 * px + 2 * py + pc, (px, py, pc)).wait_recv()
        for cp in sends:
            cp.wait_send()
        acc = buf[0]
        for d in range(1, 8):
            acc = acc + buf[d]
        o_ref[...] = acc

    return pl.pallas_call(
        body, name=name,
        in_specs=[pl.BlockSpec(memory_space=pltpu.VMEM)], out_specs=pl.BlockSpec(memory_space=pltpu.VMEM),
        out_shape=jax.ShapeDtypeStruct((r, 128), F32),
        scratch_shapes=[pltpu.VMEM((8, r, 128), F32), pltpu.SemaphoreType.DMA((7,)), pltpu.SemaphoreType.DMA((7,))],
    )(v)


def add_halves(g, other, place):
    half, wd = other.shape[1:]
    tr = _tile_rows(half, wd)
    nblk = half // tr

    def body(pref, g0, g1, g2, g3, o0, o1, o2, o3, pf_ref, pb_ref):
        f = lambda r: r[...].astype(F32)
        pf_ref[...] = f(g0) + f(o0)
        pb_ref[0] = _bf(f(g1) + f(o1))
        pb_ref[1] = _bf(f(g2) + f(o2))
        pb_ref[2] = _bf(f(g3) + f(o3))

    gspec = lambda k: pl.BlockSpec((None, tr, wd), lambda i, pr: ((pr[0] + k) % N_CHIPS, pr[1] * nblk + i, 0))
    ospec = lambda k: pl.BlockSpec((None, tr, wd), lambda i, pr: ((pr[0] + k) % N_CHIPS, i, 0))
    return pl.pallas_call(
        body, name="add_halves",
        grid_spec=pltpu.PrefetchScalarGridSpec(
            num_scalar_prefetch=1, grid=(nblk,),
            in_specs=[gspec(0), gspec(1), gspec(2), gspec(3), ospec(0), ospec(1), ospec(2), ospec(3)],
            out_specs=[pl.BlockSpec((tr, wd), lambda i, pr: (i, 0)),
                       pl.BlockSpec((3, tr, wd), lambda i, pr: (0, i, 0))]),
        out_shape=[jax.ShapeDtypeStruct((half, wd), F32), jax.ShapeDtypeStruct((3, half, wd), BF16)],
        compiler_params=_cp(("parallel",), VMEM_LIMIT),
    )(place, g, g, g, g, other, other, other, other)


def _tile_rows(n, width):
    best = 16
    for t in range(16, max(16, (384 * 1024) // width) + 1, 16):
        if n % t == 0:
            best = t
    assert n % best == 0
    return best


def add_partials(pf, got, place):
    half, wd = pf.shape
    tr = _tile_rows(half, wd)

    def body(pref, pf_ref, got_ref, o_ref):
        o_ref[...] = ((pf_ref[...] + got_ref[0].astype(F32)) + got_ref[1].astype(F32)) + got_ref[2].astype(F32)

    return pl.pallas_call(
        body, name="add_partials",
        grid_spec=pltpu.PrefetchScalarGridSpec(
            num_scalar_prefetch=1, grid=(half // tr,),
            in_specs=[pl.BlockSpec((tr, wd), lambda i, pr: (i, 0)),
                      pl.BlockSpec((3, tr, wd), lambda i, pr: (0, i, 0))],
            out_specs=pl.BlockSpec((None, tr, wd), lambda i, pr: (pr[1], i, 0))),
        out_shape=jax.ShapeDtypeStruct((2, half, wd), F32),
        compiler_params=_cp(("parallel",), VMEM_LIMIT),
    )(place, pf, got)


class GradReduce:
    def __init__(self, place, names, tag):
        self.place, self.names, self.tag = place, names, tag
        self.nb = len(_width_groups(names))

    def _swap_plan(self, refs):
        x, y, c = _place()
        plan = []
        for g_ref, o_ref in zip(refs[:self.nb], refs[self.nb:]):
            half = o_ref.shape[1]
            plan.append((g_ref.at[:, pl.ds(pl.multiple_of((1 - c) * half, 16), half), :], o_ref, (x, y, 1 - c)))
        return plan

    def _exchange_plan(self, refs):
        x, y, c = _place()
        me = 2 * x + y
        return [(p_ref.at[k - 1], o_ref.at[k - 1], (((me + k) % N_CHIPS) // 2, ((me + k) % N_CHIPS) % 2, c))
                for p_ref, o_ref in zip(refs[:self.nb], refs[self.nb:]) for k in range(1, N_CHIPS)]

    def _join_plan(self, refs):
        x, y, c = _place()
        return [(r.at[c], r.at[c], (x, y, 1 - c)) for r in refs]

    def _join_wait_plan(self, refs):
        x, y, c = _place()
        return [(r.at[c], r.at[1 - c], (x, y, 1 - c)) for r in refs]

    def begin(self, grads):
        gs = grad_buffers(grads, self.names)
        lands = [_landing((N_CHIPS, g.shape[1] // 2, g.shape[2]), BF16) for g in gs]
        self.s1, self.r1, self.b1, tok = copies_start(self.tag + "_swap_start", gs + lands, self.nb, self._swap_plan)
        return tok

    def exchange(self, after):
        b1 = copies_wait(self.tag + "_swap_wait", self.s1, self.r1, self.b1, after, self._swap_plan)
        sums = [add_halves(g, other, self.place) for g, other in zip(b1[:self.nb], b1[self.nb:])]
        self.pfs = [pf for pf, _ in sums]
        pbs = [pb for _, pb in sums]
        lands = [_landing(pb.shape, BF16) for pb in pbs]
        self.s2, self.r2, self.b2, tok = copies_start(self.tag + "_exchange_start", pbs + lands, 3 * self.nb,
                                                      self._exchange_plan)
        return tok

    def join(self, after):
        b2 = copies_wait(self.tag + "_exchange_wait", self.s2, self.r2, self.b2, after, self._exchange_plan)
        boths = [add_partials(pf, got, self.place) for pf, got in zip(self.pfs, b2[self.nb:])]
        self.s3, self.r3, self.b3, tok = copies_start(self.tag + "_join_start", boths, self.nb, self._join_plan)
        return tok

    def finish(self, after):
        boths = copies_wait(self.tag + "_join_wait", self.s3, self.r3, self.b3, after, self._join_wait_plan)
        return split_buffers([b.reshape(-1, b.shape[2]) for b in boths], self.names)


SMALL = (("g_mix", D_MODEL), ("g_ffn", D_MODEL), ("g_ple", D_MODEL), ("g_final", D_MODEL),
         ("conv_w", CONV_K * CONV_CH), ("rel_bias", A_HEADS * N_REL), ("w_onorm", B_DIM),
         ("a_log", B_HEADS), ("dt_bias", B_HEADS), ("loss", 1))


def _pad128(v):
    v = v.reshape(-1)
    return jnp.pad(v, (0, -v.shape[0] % 128))


def pack_small(d, names, rows):
    flat = jnp.concatenate([_pad128(d[n]) for n in names]).reshape(-1, 128)
    return jnp.pad(flat, ((0, rows - flat.shape[0]), (0, 0)))


def unpack_small(flat, names_sizes):
    out, r0 = {}, 0
    v = flat.reshape(-1)
    for n, size in names_sizes:
        out[n] = v[r0:r0 + size]
        r0 += -(-size // 128) * 128
    return out


def kernel(x, p, g_mix, w_in, conv_w, a_log, dt_bias, rel_bias, w_onorm, w_branch_a, w_branch_b, w_out, g_ffn, w_gate_up, w_down, g_ple, w_ple_gate, w_ple_proj, g_final, loss_target, m_g_mix, m_w_in, m_conv_w, m_a_log, m_dt_bias, m_rel_bias, m_w_onorm, m_w_branch_a, m_w_branch_b, m_w_out, m_g_ffn, m_w_gate_up, m_w_down, m_g_ple, m_w_ple_gate, m_w_ple_proj, m_g_final, v_g_mix, v_w_in, v_conv_w, v_a_log, v_dt_bias, v_rel_bias, v_w_onorm, v_w_branch_a, v_w_branch_b, v_w_out, v_g_ffn, v_w_gate_up, v_w_down, v_g_ple, v_w_ple_gate, v_w_ple_proj, v_g_final):
    names = ["g_mix", "w_in", "conv_w", "a_log", "dt_bias", "rel_bias", "w_onorm", "w_branch_a", "w_branch_b",
             "w_out", "g_ffn", "w_gate_up", "w_down", "g_ple", "w_ple_gate", "w_ple_proj", "g_final"]
    w = dict(zip(names, [g_mix, w_in, conv_w, a_log, dt_bias, rel_bias, w_onorm, w_branch_a, w_branch_b, w_out,
                         g_ffn, w_gate_up, w_down, g_ple, w_ple_gate, w_ple_proj, g_final]))
    m = dict(zip(names, [m_g_mix, m_w_in, m_conv_w, m_a_log, m_dt_bias, m_rel_bias, m_w_onorm, m_w_branch_a,
                         m_w_branch_b, m_w_out, m_g_ffn, m_w_gate_up, m_w_down, m_g_ple, m_w_ple_gate,
                         m_w_ple_proj, m_g_final]))
    v = dict(zip(names, [v_g_mix, v_w_in, v_conv_w, v_a_log, v_dt_bias, v_rel_bias, v_w_onorm, v_w_branch_a,
                         v_w_branch_b, v_w_out, v_g_ffn, v_w_gate_up, v_w_down, v_g_ple, v_w_ple_gate,
                         v_w_ple_proj, v_g_final]))
    xi, yi, ci = _place()
    chip = 2 * xi + yi
    big_names = [n for n, _, _ in BIG]

    shards2d = {n: w[n].reshape(w[n].shape[-2:]) for n in big_names}
    shards_bf = {n: a.astype(BF16) for n, a in shards2d.items()}
    g4 = allgather_weights(shards_bf, FIRST_WEIGHTS)["w_in"]
    place = jnp.stack([chip, ci]).astype(jnp.int32)
    conv_sh = jnp.where(ci == 0, w["conv_w"].reshape(CONV_K, CONV_CH // N_CHIPS), 0.0)
    conv_slots = lax.dynamic_update_slice(jnp.zeros((N_CHIPS, CONV_K, CONV_CH // N_CHIPS), F32), conv_sh[None],
                                          (chip, 0, 0))
    conv_all = small_allreduce(conv_slots.reshape(-1, 128), "gather_conv_w")
    conv_full = jnp.transpose(conv_all.reshape(N_CHIPS, CONV_K, CONV_CH // N_CHIPS), (1, 0, 2)).reshape(CONV_K, CONV_CH)
    small = {n: w[n] for n in names if n not in big_names}
    small["conv_w"] = conv_full

    grad_x, grads, small_grads, reduced_early = local_step(
        x, p[0], loss_target, g4, small, LaterWeights(shards_bf), GradReduce(place, EARLY_GRADS, "grads"))

    late = GradReduce(place, LATE_GRADS, "late")
    tok = late.begin(grads)
    small_names = [n for n, _ in SMALL]
    dbt, dbf = small_grads.pop("rel_bias_parts")
    small_grads["rel_bias"] = bias_band_grad(dbt + tok, dbf).reshape(-1)
    tok = late.exchange(small_grads["rel_bias"])
    small_grads["loss"] = small_grads["loss"] + tok
    red_flat = small_allreduce(pack_small(small_grads, small_names, 112), "allreduce_small")
    red = unpack_small(red_flat, SMALL)
    dep, red_flat = lax.optimization_barrier((jnp.full((8, 128), tok, F32), red_flat))
    gshard = dict(reduced_early)
    loss = red["loss"][0]
    conv_g = lax.dynamic_slice(red["conv_w"].reshape(CONV_K, N_CHIPS, CONV_CH // N_CHIPS), (0, chip, 0),
                               (CONV_K, 1, CONV_CH // N_CHIPS))
    gsmall = {n: red[n].reshape(w[n].shape) for n in small_names if n not in ("loss", "conv_w")}
    gsmall["conv_w"] = conv_g.reshape(w["conv_w"].shape)

    grad, delta, new_m, new_v = {}, {}, {}, {}
    for n in list(EARLY_GRADS) + list(LATE_GRADS):
        if n in LATE_GRADS:
            late.join(v_)
            gshard.update(late.finish(v_))
        shp = w[n].shape
        d_, m_, v_ = adamw(shards2d[n], gshard[n], m[n].reshape(shp[-2:]), v[n].reshape(shp[-2:]), "adamw_" + n,
                           dep=dep if n in EARLY_GRADS else None)
        dep, v_ = lax.optimization_barrier((dep, v_))
        grad[n], delta[n], new_m[n], new_v[n] = gshard[n].reshape(shp), d_.reshape(shp), m_.reshape(shp), v_.reshape(shp)
    snames = [n for n in small_names if n != "loss"]
    ssizes = [(n, w[n].size) for n in snames]
    pk = lambda d: pack_small(d, snames, 64)
    d_, m_, v_ = adamw(pk(w), pk(gsmall), pk(m), pk(v), "adamw_small")
    ds, ms, vs = unpack_small(d_, ssizes), unpack_small(m_, ssizes), unpack_small(v_, ssizes)
    for n in snames:
        shp = w[n].shape
        grad[n], delta[n], new_m[n], new_v[n] = gsmall[n], ds[n].reshape(shp), ms[n].reshape(shp), vs[n].reshape(shp)

    return (loss, grad_x, *[grad[n] for n in names], *[delta[n] for n in names],
            *[new_m[n] for n in names], *[new_v[n] for n in names])
```
